```python
import jax, jax.numpy as jnp
from jax import lax
import numpy as np

D_MODEL = 1024
BATCH = 16
SEQ = 2048
DEPTH = 1

GDN_HEADS = 8
GDN_DK = 128
GDN_DV = 128
GDN_CONV = 4
GDN_CHUNK = 64
FOX_HEADS = 8
FOX_DH = 128
FOX_BLOCK = 128
D_FF = 4 * D_MODEL
EPS = 1e-6

GDN_QK_W = GDN_HEADS * GDN_DK
GDN_V_W = GDN_HEADS * GDN_DV
FOX_W = FOX_HEADS * FOX_DH
IN_SPLITS = (GDN_QK_W, GDN_QK_W, GDN_V_W, GDN_V_W, GDN_HEADS, GDN_HEADS,
             FOX_W, FOX_W, FOX_W, FOX_HEADS, D_MODEL, D_MODEL)
N_IN = sum(IN_SPLITS)

kernel_name = "hybrid_gdn_fox_gated_merge_block"


def rmsnorm(x, g):
    xf = x.astype(jnp.float32)
    y = xf * lax.rsqrt(jnp.mean(xf * xf, axis=-1, keepdims=True) + EPS)
    return (y * g.astype(jnp.float32)).astype(x.dtype)


def l2norm(x):
    xf = x.astype(jnp.float32)
    return xf * lax.rsqrt(jnp.sum(xf * xf, axis=-1, keepdims=True) + EPS)


def to_heads(x, n_heads):
    b, t, w = x.shape
    return x.reshape(b, t, n_heads, w // n_heads).transpose(0, 2, 1, 3)


def causal_depthwise_conv(x, w):
    k_width = w.shape[0]
    t = x.shape[1]
    xp = jnp.pad(x, ((0, 0), (k_width - 1, 0), (0, 0)))
    y = xp[:, 0:t] * w[0]
    for i in range(1, k_width):
        y = y + xp[:, i:i + t] * w[i]
    return y


def gated_delta_rule_chunked(q, k, v, g, beta):
    b, h, t, dk = q.shape
    dv = v.shape[-1]
    c = GDN_CHUNK
    n = t // c
    q = q.reshape(b, h, n, c, dk)
    k = k.reshape(b, h, n, c, dk)
    v = v.reshape(b, h, n, c, dv)
    g = g.reshape(b, h, n, c)
    beta = beta.reshape(b, h, n, c)

    G = jnp.cumsum(g, axis=-1)
    diff = G[..., :, None] - G[..., None, :]
    tri_incl = jnp.tril(jnp.ones((c, c), dtype=bool))
    tri_strict = jnp.tril(jnp.ones((c, c), dtype=bool), -1)
    decay = jnp.where(tri_incl, jnp.exp(jnp.where(tri_incl, diff, 0.0)), 0.0)

    kk = jnp.einsum('bhncd,bhnsd->bhncs', k, k)
    a_mat = jnp.where(tri_strict, beta[..., None] * kk * decay, 0.0)
    eye = jnp.eye(c, dtype=jnp.float32)
    rhs = jnp.concatenate([v * beta[..., None], k * (beta * jnp.exp(G))[..., None]], axis=-1)
    a_full = jnp.broadcast_to(a_mat + eye, (b, h, n, c, c))
    sol = lax.linalg.triangular_solve(a_full, rhs, left_side=True, lower=True,
                                      unit_diagonal=True)
    value, k_cum = sol[..., :dv], sol[..., dv:]

    attn_intra = jnp.einsum('bhncd,bhnsd->bhncs', q, k) * decay
    q_dec = q * jnp.exp(G)[..., None]
    g_last = G[..., -1:]
    k_dec = k * jnp.exp(g_last - G)[..., None]
    chunk_decay = jnp.exp(g_last[..., 0])

    def step(s, inp):
        value_c, kcum_c, attn_c, qdec_c, kdec_c, dec_c = inp
        v_new = value_c - jnp.einsum('bhcd,bhde->bhce', kcum_c, s)
        o = (jnp.einsum('bhcd,bhde->bhce', qdec_c, s)
             + jnp.einsum('bhcs,bhse->bhce', attn_c, v_new))
        s = s * dec_c[..., None, None] + jnp.einsum('bhcd,bhce->bhde', kdec_c, v_new)
        return s, o

    xs = tuple(jnp.moveaxis(a, 2, 0) for a in (value, k_cum, attn_intra, q_dec, k_dec, chunk_decay))
    s0 = jnp.zeros((b, h, dk, dv), jnp.float32)
    _, o = lax.scan(step, s0, xs)
    return jnp.moveaxis(o, 0, 2).reshape(b, h, t, dv)


def forgetting_attention(q, k, v, log_f):
    t = q.shape[2]
    cum = jnp.cumsum(log_f, axis=-1)
    scale = FOX_DH ** -0.5
    outs = []
    for i in range(t // FOX_BLOCK):
        q0 = i * FOX_BLOCK
        q1 = q0 + FOX_BLOCK
        s = (jnp.einsum('bhqd,bhkd->bhqk', q[:, :, q0:q1], k[:, :, :q1]) * scale
             + cum[:, :, q0:q1, None] - cum[:, :, None, :q1])
        mask = (q0 + jnp.arange(FOX_BLOCK))[:, None] >= jnp.arange(q1)[None, :]
        p = jax.nn.softmax(jnp.where(mask, s, -jnp.inf), axis=-1)
        outs.append(jnp.einsum('bhqk,bhkd->bhqd', p, v[:, :, :q1]))
    return jnp.concatenate(outs, axis=2)


def _fwd_setup_inputs(seed: int = 0) -> dict:
    key = jax.random.key(seed)
    ks = jax.random.split(key, 20)
    f32 = jnp.float32

    def nrm(k, shape, fan_in):
        return jax.random.normal(k, shape, f32) * (fan_in ** -0.5)

    def gain(k, shape):
        return 1.0 + 0.02 * jax.random.normal(k, shape, f32)

    dt = jnp.exp(jax.random.uniform(ks[5], (DEPTH, GDN_HEADS), f32,
                                    minval=float(np.log(1e-3)), maxval=float(np.log(1e-1))))
    return {
        "x": jax.random.normal(ks[0], (BATCH, SEQ, D_MODEL), f32),
        "norm_mix_g": gain(ks[1], (DEPTH, D_MODEL)),
        "w_in": nrm(ks[2], (DEPTH, D_MODEL, N_IN), D_MODEL),
        "gdn_conv_w": nrm(ks[3], (DEPTH, GDN_CONV, 2 * GDN_QK_W + GDN_V_W), GDN_CONV),
        "gdn_a_log": jnp.log(jax.random.uniform(ks[4], (DEPTH, GDN_HEADS), f32, minval=1.0, maxval=16.0)),
        "gdn_dt_bias": dt + jnp.log(-jnp.expm1(-dt)),
        "gdn_norm_g": gain(ks[6], (DEPTH, GDN_DV)),
        "fox_q_norm_g": gain(ks[7], (DEPTH, FOX_DH)),
        "fox_k_norm_g": gain(ks[8], (DEPTH, FOX_DH)),
        "fox_f_bias": jax.random.uniform(ks[9], (DEPTH, FOX_HEADS), f32, minval=2.0, maxval=6.0),
        "w_proj_gdn": nrm(ks[10], (DEPTH, GDN_V_W, D_MODEL), GDN_V_W),
        "w_proj_fox": nrm(ks[11], (DEPTH, FOX_W, D_MODEL), FOX_W),
        "w_out": nrm(ks[12], (DEPTH, D_MODEL, D_MODEL), D_MODEL),
        "norm_mlp_g": gain(ks[13], (DEPTH, D_MODEL)),
        "w_up": nrm(ks[14], (DEPTH, D_MODEL, D_FF), D_MODEL),
        "w_down": nrm(ks[15], (DEPTH, D_FF, D_MODEL), D_FF),
    }


def _fwd_reference(x, norm_mix_g, w_in, gdn_conv_w, gdn_a_log, gdn_dt_bias, gdn_norm_g,
              fox_q_norm_g, fox_k_norm_g, fox_f_bias, w_proj_gdn, w_proj_fox, w_out,
              norm_mlp_g, w_up, w_down):
    f32 = jnp.float32
    b, t, _ = x.shape
    split_idx = [int(i) for i in np.cumsum(IN_SPLITS)[:-1]]
    for l in range(DEPTH):
        u = rmsnorm(x, norm_mix_g[l])
        proj = u @ w_in[l]
        (gq, gk, gv, gz, ga, gb, fq, fk, fv, ff, gate_a, gate_b) = jnp.split(proj, split_idx, axis=-1)

        qkv = jax.nn.silu(causal_depthwise_conv(jnp.concatenate([gq, gk, gv], axis=-1), gdn_conv_w[l]))
        cq, ck, cv = jnp.split(qkv, [GDN_QK_W, 2 * GDN_QK_W], axis=-1)
        qh = l2norm(to_heads(cq, GDN_HEADS)) * (GDN_DK ** -0.5)
        kh = l2norm(to_heads(ck, GDN_HEADS))
        vh = to_heads(cv, GDN_HEADS).astype(f32)
        beta = jax.nn.sigmoid(gb.astype(f32)).transpose(0, 2, 1)
        g = (-jnp.exp(gdn_a_log[l].astype(f32))
             * jax.nn.softplus(ga.astype(f32) + gdn_dt_bias[l].astype(f32))).transpose(0, 2, 1)
        o_a = gated_delta_rule_chunked(qh, kh, vh, g, beta).transpose(0, 2, 1, 3)
        z = gz.reshape(b, t, GDN_HEADS, GDN_DV).astype(f32)
        o_a = rmsnorm(o_a, gdn_norm_g[l]) * jax.nn.silu(z)
        y_a = o_a.reshape(b, t, GDN_V_W).astype(x.dtype) @ w_proj_gdn[l]

        fqh = rmsnorm(to_heads(fq, FOX_HEADS), fox_q_norm_g[l]).astype(f32)
        fkh = rmsnorm(to_heads(fk, FOX_HEADS), fox_k_norm_g[l]).astype(f32)
        fvh = to_heads(fv, FOX_HEADS).astype(f32)
        log_f = jax.nn.log_sigmoid(ff.astype(f32) + fox_f_bias[l].astype(f32)).transpose(0, 2, 1)
        o_b = forgetting_attention(fqh, fkh, fvh, log_f).transpose(0, 2, 1, 3)
        y_b = o_b.reshape(b, t, FOX_W).astype(x.dtype) @ w_proj_fox[l]

        merged = jax.nn.sigmoid(gate_a) * y_a + jax.nn.sigmoid(gate_b) * y_b
        h = x + merged @ w_out[l]

        hn = rmsnorm(h, norm_mlp_g[l])
        x = h + jnp.square(jax.nn.relu(hn @ w_up[l])) @ w_down[l]
    return x


import jax as _jax
import jax.numpy as _jnp

TWIN_FORMAT = 'train_step'
FWD_PARAMS = ['x', 'norm_mix_g', 'w_in', 'gdn_conv_w', 'gdn_a_log', 'gdn_dt_bias', 'gdn_norm_g', 'fox_q_norm_g', 'fox_k_norm_g', 'fox_f_bias', 'w_proj_gdn', 'w_proj_fox', 'w_out', 'norm_mlp_g', 'w_up', 'w_down']
TWIN_WEIGHTS = ['norm_mix_g', 'w_in', 'gdn_conv_w', 'gdn_a_log', 'gdn_dt_bias', 'gdn_norm_g', 'fox_q_norm_g', 'fox_k_norm_g', 'fox_f_bias', 'w_proj_gdn', 'w_proj_fox', 'w_out', 'norm_mlp_g', 'w_up', 'w_down']
TWIN_DIFF_INPUT = 'x'
TWIN_INPUTS = ['x', 'norm_mix_g', 'w_in', 'gdn_conv_w', 'gdn_a_log', 'gdn_dt_bias', 'gdn_norm_g', 'fox_q_norm_g', 'fox_k_norm_g', 'fox_f_bias', 'w_proj_gdn', 'w_proj_fox', 'w_out', 'norm_mlp_g', 'w_up', 'w_down', 'loss_target', 'm_norm_mix_g', 'm_w_in', 'm_gdn_conv_w', 'm_gdn_a_log', 'm_gdn_dt_bias', 'm_gdn_norm_g', 'm_fox_q_norm_g', 'm_fox_k_norm_g', 'm_fox_f_bias', 'm_w_proj_gdn', 'm_w_proj_fox', 'm_w_out', 'm_norm_mlp_g', 'm_w_up', 'm_w_down', 'v_norm_mix_g', 'v_w_in', 'v_gdn_conv_w', 'v_gdn_a_log', 'v_gdn_dt_bias', 'v_gdn_norm_g', 'v_fox_q_norm_g', 'v_fox_k_norm_g', 'v_fox_f_bias', 'v_w_proj_gdn', 'v_w_proj_fox', 'v_w_out', 'v_norm_mlp_g', 'v_w_up', 'v_w_down']
TWIN_OUTPUTS = ['loss', 'grad_x', 'grad_norm_mix_g', 'grad_w_in', 'grad_gdn_conv_w', 'grad_gdn_a_log', 'grad_gdn_dt_bias', 'grad_gdn_norm_g', 'grad_fox_q_norm_g', 'grad_fox_k_norm_g', 'grad_fox_f_bias', 'grad_w_proj_gdn', 'grad_w_proj_fox', 'grad_w_out', 'grad_norm_mlp_g', 'grad_w_up', 'grad_w_down', 'delta_norm_mix_g', 'delta_w_in', 'delta_gdn_conv_w', 'delta_gdn_a_log', 'delta_gdn_dt_bias', 'delta_gdn_norm_g', 'delta_fox_q_norm_g', 'delta_fox_k_norm_g', 'delta_fox_f_bias', 'delta_w_proj_gdn', 'delta_w_proj_fox', 'delta_w_out', 'delta_norm_mlp_g', 'delta_w_up', 'delta_w_down', 'new_m_norm_mix_g', 'new_m_w_in', 'new_m_gdn_conv_w', 'new_m_gdn_a_log', 'new_m_gdn_dt_bias', 'new_m_gdn_norm_g', 'new_m_fox_q_norm_g', 'new_m_fox_k_norm_g', 'new_m_fox_f_bias', 'new_m_w_proj_gdn', 'new_m_w_proj_fox', 'new_m_w_out', 'new_m_norm_mlp_g', 'new_m_w_up', 'new_m_w_down', 'new_v_norm_mix_g', 'new_v_w_in', 'new_v_gdn_conv_w', 'new_v_gdn_a_log', 'new_v_gdn_dt_bias', 'new_v_gdn_norm_g', 'new_v_fox_q_norm_g', 'new_v_fox_k_norm_g', 'new_v_fox_f_bias', 'new_v_w_proj_gdn', 'new_v_w_proj_fox', 'new_v_w_out', 'new_v_norm_mlp_g', 'new_v_w_up', 'new_v_w_down']
TWIN_LEAF_KINDS = {'loss': 'loss', 'grad_x': 'grad_x', 'grad_norm_mix_g': 'grad_w', 'grad_w_in': 'grad_w', 'grad_gdn_conv_w': 'grad_w', 'grad_gdn_a_log': 'grad_w', 'grad_gdn_dt_bias': 'grad_w', 'grad_gdn_norm_g': 'grad_w', 'grad_fox_q_norm_g': 'grad_w', 'grad_fox_k_norm_g': 'grad_w', 'grad_fox_f_bias': 'grad_w', 'grad_w_proj_gdn': 'grad_w', 'grad_w_proj_fox': 'grad_w', 'grad_w_out': 'grad_w', 'grad_norm_mlp_g': 'grad_w', 'grad_w_up': 'grad_w', 'grad_w_down': 'grad_w', 'delta_norm_mix_g': 'delta_w', 'delta_w_in': 'delta_w', 'delta_gdn_conv_w': 'delta_w', 'delta_gdn_a_log': 'delta_w', 'delta_gdn_dt_bias': 'delta_w', 'delta_gdn_norm_g': 'delta_w', 'delta_fox_q_norm_g': 'delta_w', 'delta_fox_k_norm_g': 'delta_w', 'delta_fox_f_bias': 'delta_w', 'delta_w_proj_gdn': 'delta_w', 'delta_w_proj_fox': 'delta_w', 'delta_w_out': 'delta_w', 'delta_norm_mlp_g': 'delta_w', 'delta_w_up': 'delta_w', 'delta_w_down': 'delta_w', 'new_m_norm_mix_g': 'new_m', 'new_m_w_in': 'new_m', 'new_m_gdn_conv_w': 'new_m', 'new_m_gdn_a_log': 'new_m', 'new_m_gdn_dt_bias': 'new_m', 'new_m_gdn_norm_g': 'new_m', 'new_m_fox_q_norm_g': 'new_m', 'new_m_fox_k_norm_g': 'new_m', 'new_m_fox_f_bias': 'new_m', 'new_m_w_proj_gdn': 'new_m', 'new_m_w_proj_fox': 'new_m', 'new_m_w_out': 'new_m', 'new_m_norm_mlp_g': 'new_m', 'new_m_w_up': 'new_m', 'new_m_w_down': 'new_m', 'new_v_norm_mix_g': 'new_v', 'new_v_w_in': 'new_v', 'new_v_gdn_conv_w': 'new_v', 'new_v_gdn_a_log': 'new_v', 'new_v_gdn_dt_bias': 'new_v', 'new_v_gdn_norm_g': 'new_v', 'new_v_fox_q_norm_g': 'new_v', 'new_v_fox_k_norm_g': 'new_v', 'new_v_fox_f_bias': 'new_v', 'new_v_w_proj_gdn': 'new_v', 'new_v_w_proj_fox': 'new_v', 'new_v_w_out': 'new_v', 'new_v_norm_mlp_g': 'new_v', 'new_v_w_up': 'new_v', 'new_v_w_down': 'new_v'}


def _forward(args):
    return _fwd_reference(*[args[k] for k in FWD_PARAMS])


def _output_shape():
    out = _jax.eval_shape(lambda: _forward(_fwd_setup_inputs(0)))
    return out.shape, out.dtype

N_MICROBATCH = 1
ADAM_LR = 0.001
ADAM_B1 = 0.9
ADAM_B2 = 0.999
ADAM_EPS = 1e-08
ADAM_WD = 0.01
ADAM_STEP = 10
PER_EXAMPLE_BATCH_AXIS = {'x': 0, 'loss_target': 0}
SHARED_INPUTS = []
_WEIGHT_DTYPES = {'norm_mix_g': _jnp.float32, 'w_in': _jnp.float32, 'gdn_conv_w': _jnp.float32, 'gdn_a_log': _jnp.float32, 'gdn_dt_bias': _jnp.float32, 'gdn_norm_g': _jnp.float32, 'fox_q_norm_g': _jnp.float32, 'fox_k_norm_g': _jnp.float32, 'fox_f_bias': _jnp.float32, 'w_proj_gdn': _jnp.float32, 'w_proj_fox': _jnp.float32, 'w_out': _jnp.float32, 'norm_mlp_g': _jnp.float32, 'w_up': _jnp.float32, 'w_down': _jnp.float32}
MOMENT_SCALE = {'norm_mix_g': 4.108477e+00, 'w_in': 1.632037e-01, 'gdn_conv_w': 6.211632e-01, 'gdn_a_log': 1.249477e+01, 'gdn_dt_bias': 1.196472e+01, 'gdn_norm_g': 2.568979e+01, 'fox_q_norm_g': 1.758401e+00, 'fox_k_norm_g': 1.769527e+00, 'fox_f_bias': 2.021467e+01, 'w_proj_gdn': 1.673711e+00, 'w_proj_fox': 1.471509e-01, 'w_out': 1.544018e+00, 'norm_mlp_g': 9.607146e+01, 'w_up': 9.048936e-01, 'w_down': 8.026486e+00}


def _to_microbatches(a, axis):
    t = _jnp.moveaxis(a, axis, 0)
    t = t.reshape((N_MICROBATCH, t.shape[0] // N_MICROBATCH) + t.shape[1:])
    return _jnp.moveaxis(t, 1, axis + 1)


def setup_inputs(seed: int = 0) -> dict:
    inp = _fwd_setup_inputs(seed)
    key = _jax.random.fold_in(_jax.random.key(seed), 7919)
    shape, _ = _output_shape()
    out = dict(inp)
    out["loss_target"] = _jax.random.normal(_jax.random.fold_in(key, 0), shape, _jnp.float32)
    for i, name in enumerate(TWIN_WEIGHTS):
        w = inp[name].astype(_jnp.float32)
        if MOMENT_SCALE is None:
            s = _jnp.sqrt(_jnp.mean(_jnp.square(w)) + 1e-30)
        else:
            s = MOMENT_SCALE[name]
        km, kv = _jax.random.split(_jax.random.fold_in(key, i + 1))
        out[name] = w
        out["m_" + name] = s * _jax.random.normal(km, w.shape, _jnp.float32)
        out["v_" + name] = (s * s) * _jax.random.uniform(kv, w.shape, _jnp.float32, 0.5, 1.5)
    if N_MICROBATCH > 1:
        for name, axis in PER_EXAMPLE_BATCH_AXIS.items():
            out[name] = _to_microbatches(out[name], axis)
    return {'x': out['x'], 'norm_mix_g': out['norm_mix_g'], 'w_in': out['w_in'], 'gdn_conv_w': out['gdn_conv_w'], 'gdn_a_log': out['gdn_a_log'], 'gdn_dt_bias': out['gdn_dt_bias'], 'gdn_norm_g': out['gdn_norm_g'], 'fox_q_norm_g': out['fox_q_norm_g'], 'fox_k_norm_g': out['fox_k_norm_g'], 'fox_f_bias': out['fox_f_bias'], 'w_proj_gdn': out['w_proj_gdn'], 'w_proj_fox': out['w_proj_fox'], 'w_out': out['w_out'], 'norm_mlp_g': out['norm_mlp_g'], 'w_up': out['w_up'], 'w_down': out['w_down'], 'loss_target': out['loss_target'], 'm_norm_mix_g': out['m_norm_mix_g'], 'm_w_in': out['m_w_in'], 'm_gdn_conv_w': out['m_gdn_conv_w'], 'm_gdn_a_log': out['m_gdn_a_log'], 'm_gdn_dt_bias': out['m_gdn_dt_bias'], 'm_gdn_norm_g': out['m_gdn_norm_g'], 'm_fox_q_norm_g': out['m_fox_q_norm_g'], 'm_fox_k_norm_g': out['m_fox_k_norm_g'], 'm_fox_f_bias': out['m_fox_f_bias'], 'm_w_proj_gdn': out['m_w_proj_gdn'], 'm_w_proj_fox': out['m_w_proj_fox'], 'm_w_out': out['m_w_out'], 'm_norm_mlp_g': out['m_norm_mlp_g'], 'm_w_up': out['m_w_up'], 'm_w_down': out['m_w_down'], 'v_norm_mix_g': out['v_norm_mix_g'], 'v_w_in': out['v_w_in'], 'v_gdn_conv_w': out['v_gdn_conv_w'], 'v_gdn_a_log': out['v_gdn_a_log'], 'v_gdn_dt_bias': out['v_gdn_dt_bias'], 'v_gdn_norm_g': out['v_gdn_norm_g'], 'v_fox_q_norm_g': out['v_fox_q_norm_g'], 'v_fox_k_norm_g': out['v_fox_k_norm_g'], 'v_fox_f_bias': out['v_fox_f_bias'], 'v_w_proj_gdn': out['v_w_proj_gdn'], 'v_w_proj_fox': out['v_w_proj_fox'], 'v_w_out': out['v_w_out'], 'v_norm_mlp_g': out['v_norm_mlp_g'], 'v_w_up': out['v_w_up'], 'v_w_down': out['v_w_down']}


def _loss(weights, diff, rest, loss_target):
    with _jax.named_scope("forward"):
        args = {**rest, TWIN_DIFF_INPUT: diff, **{k: w.astype(_WEIGHT_DTYPES[k]) for k, w in weights.items()}}
        y = _forward(args)
    with _jax.named_scope("loss_head"):
        err = _jnp.square(y.astype(_jnp.float32) - loss_target)
        return 0.5 * _jnp.sum(_jnp.mean(err, axis=-1)) if err.ndim else 0.5 * err


def _adamw(w, g, m, v):
    m = ADAM_B1 * m + (1.0 - ADAM_B1) * g
    v = ADAM_B2 * v + (1.0 - ADAM_B2) * _jnp.square(g)
    m_hat = m / (1.0 - ADAM_B1 ** ADAM_STEP)
    v_hat = v / (1.0 - ADAM_B2 ** ADAM_STEP)
    delta = -ADAM_LR * (m_hat / (_jnp.sqrt(v_hat) + ADAM_EPS) + ADAM_WD * w)
    return delta, m, v


def reference(x, norm_mix_g, w_in, gdn_conv_w, gdn_a_log, gdn_dt_bias, gdn_norm_g, fox_q_norm_g, fox_k_norm_g, fox_f_bias, w_proj_gdn, w_proj_fox, w_out, norm_mlp_g, w_up, w_down, loss_target, m_norm_mix_g, m_w_in, m_gdn_conv_w, m_gdn_a_log, m_gdn_dt_bias, m_gdn_norm_g, m_fox_q_norm_g, m_fox_k_norm_g, m_fox_f_bias, m_w_proj_gdn, m_w_proj_fox, m_w_out, m_norm_mlp_g, m_w_up, m_w_down, v_norm_mix_g, v_w_in, v_gdn_conv_w, v_gdn_a_log, v_gdn_dt_bias, v_gdn_norm_g, v_fox_q_norm_g, v_fox_k_norm_g, v_fox_f_bias, v_w_proj_gdn, v_w_proj_fox, v_w_out, v_norm_mlp_g, v_w_up, v_w_down):
    given = dict(x=x, norm_mix_g=norm_mix_g, w_in=w_in, gdn_conv_w=gdn_conv_w, gdn_a_log=gdn_a_log, gdn_dt_bias=gdn_dt_bias, gdn_norm_g=gdn_norm_g, fox_q_norm_g=fox_q_norm_g, fox_k_norm_g=fox_k_norm_g, fox_f_bias=fox_f_bias, w_proj_gdn=w_proj_gdn, w_proj_fox=w_proj_fox, w_out=w_out, norm_mlp_g=norm_mlp_g, w_up=w_up, w_down=w_down, loss_target=loss_target, m_norm_mix_g=m_norm_mix_g, m_w_in=m_w_in, m_gdn_conv_w=m_gdn_conv_w, m_gdn_a_log=m_gdn_a_log, m_gdn_dt_bias=m_gdn_dt_bias, m_gdn_norm_g=m_gdn_norm_g, m_fox_q_norm_g=m_fox_q_norm_g, m_fox_k_norm_g=m_fox_k_norm_g, m_fox_f_bias=m_fox_f_bias, m_w_proj_gdn=m_w_proj_gdn, m_w_proj_fox=m_w_proj_fox, m_w_out=m_w_out, m_norm_mlp_g=m_norm_mlp_g, m_w_up=m_w_up, m_w_down=m_w_down, v_norm_mix_g=v_norm_mix_g, v_w_in=v_w_in, v_gdn_conv_w=v_gdn_conv_w, v_gdn_a_log=v_gdn_a_log, v_gdn_dt_bias=v_gdn_dt_bias, v_gdn_norm_g=v_gdn_norm_g, v_fox_q_norm_g=v_fox_q_norm_g, v_fox_k_norm_g=v_fox_k_norm_g, v_fox_f_bias=v_fox_f_bias, v_w_proj_gdn=v_w_proj_gdn, v_w_proj_fox=v_w_proj_fox, v_w_out=v_w_out, v_norm_mlp_g=v_norm_mlp_g, v_w_up=v_w_up, v_w_down=v_w_down)
    weights = {n: given[n] for n in TWIN_WEIGHTS}
    shared = {n: given[n] for n in SHARED_INPUTS}
    per_example = {n: given[n] for n in ['x']}
    grad_fn = _jax.value_and_grad(_loss, argnums=(0, 1))

    def one_microbatch(ex, loss_target):
        ex = dict(ex)
        diff = ex.pop(TWIN_DIFF_INPUT)
        return grad_fn(weights, diff, {**shared, **ex}, loss_target)

    if N_MICROBATCH == 1:
        loss, (grad_w, grad_x) = one_microbatch(per_example, given["loss_target"])
    else:
        def body(carry, xs):
            loss_sum, grad_sum = carry
            l_k, (gw_k, gx_k) = one_microbatch(xs[0], xs[1])
            with _jax.named_scope("update"):
                return (loss_sum + l_k, _jax.tree.map(_jnp.add, grad_sum, gw_k)), gx_k

        init = (_jnp.zeros((), _jnp.float32), _jax.tree.map(_jnp.zeros_like, weights))
        (loss, grad_w), grad_x = _jax.lax.scan(body, init, (per_example, given["loss_target"]))
    with _jax.named_scope("update"):
        delta_w, new_m, new_v = {}, {}, {}
        for n in TWIN_WEIGHTS:
            delta_w[n], new_m[n], new_v[n] = _adamw(weights[n], grad_w[n], given["m_" + n], given["v_" + n])
    return (loss, grad_x, *[grad_w[n] for n in TWIN_WEIGHTS], *[delta_w[n] for n in TWIN_WEIGHTS],
            *[new_m[n] for n in TWIN_WEIGHTS], *[new_v[n] for n in TWIN_WEIGHTS])
```

```python
import functools

import jax
import jax.numpy as jnp
from jax import lax
from jax.experimental import pallas as pl
from jax.experimental.pallas import tpu as pltpu

F32 = jnp.float32
BF16 = jnp.bfloat16
LANES = 128
NH = 8
EPS = 1e-6
GDN_CHUNK = 64
GDN_ROWS = 256
GDN_BASE = 16
ROW_TILE = 256
ATT_TILE = 512
NEG = -1e30
VMEM_LIMIT_BYTES = 48 * 1024 * 1024
HI = lax.Precision.HIGHEST
LO = lax.Precision.DEFAULT
MESH = pl.DeviceIdType.MESH
ANY = pl.BlockSpec(memory_space=pl.ANY)

ADAM_LR, ADAM_B1, ADAM_B2, ADAM_EPS, ADAM_WD, ADAM_STEP = 0.001, 0.9, 0.999, 1e-08, 0.01, 10


def _params(n_grid):
    return pltpu.CompilerParams(dimension_semantics=("arbitrary",) * n_grid,
                                vmem_limit_bytes=VMEM_LIMIT_BYTES)


def _dot(a, b, dims, precision=None):
    dn = {"nn": (((1,), (0,)), ((), ())), "nt": (((1,), (1,)), ((), ())), "tn": (((0,), (0,)), ((), ()))}[dims]
    return lax.dot_general(a, b, dn, precision=precision, preferred_element_type=F32)


def _iota(shape, dim):
    return lax.broadcasted_iota(jnp.int32, shape, dim)


def matmul(name, a, b, dims, out_dtype, add=None, tm=512, tn=512, tk=512):
    if dims == "nn":
        (m, k), (_, n) = a.shape, b.shape
    elif dims == "nt":
        (m, k), (n, _) = a.shape, b.shape
    else:
        (k, m), (_, n) = a.shape, b.shape
    tm, tn, tk = min(tm, m), min(tn, n), min(tk, k)
    assert m % tm == 0 and n % tn == 0 and k % tk == 0, (name, m, n, k)
    nk = k // tk
    a_spec = pl.BlockSpec((tk, tm), lambda i, j, kk: (kk, i)) if dims == "tn" else pl.BlockSpec((tm, tk), lambda i, j, kk: (i, kk))
    b_spec = pl.BlockSpec((tn, tk), lambda i, j, kk: (j, kk)) if dims == "nt" else pl.BlockSpec((tk, tn), lambda i, j, kk: (kk, j))
    o_spec = pl.BlockSpec((tm, tn), lambda i, j, kk: (i, j))
    has_add = add is not None

    def body(*refs):
        a_ref, b_ref = refs[0], refs[1]
        add_ref = refs[2] if has_add else None
        o_ref, acc_ref = refs[-2], refs[-1]
        kk = pl.program_id(2)

        @pl.when(kk == 0)
        def _():
            acc_ref[...] = jnp.zeros_like(acc_ref)

        acc_ref[...] += _dot(a_ref[...], b_ref[...], dims)

        @pl.when(kk == nk - 1)
        def _():
            r = acc_ref[...]
            if has_add:
                r = r + add_ref[...]
            o_ref[...] = r.astype(o_ref.dtype)

    ins = [a, b] + ([add] if has_add else [])
    in_specs = [a_spec, b_spec] + ([o_spec] if has_add else [])
    return pl.pallas_call(
        body, name=name, grid=(m // tm, n // tn, nk), in_specs=in_specs, out_specs=o_spec,
        out_shape=jax.ShapeDtypeStruct((m, n), out_dtype),
        scratch_shapes=[pltpu.VMEM((tm, tn), F32)], compiler_params=_params(3),
    )(*ins)


def _ew_spec(kind, off, width, tb, order, shape=None):
    def ih(g0, g1):
        return (g0, g1) if order == "ih" else (g1, g0)

    if kind == "row":
        return pl.BlockSpec((tb, width), lambda g0, g1: (ih(g0, g1)[0], off))
    if kind == "rowh":
        return pl.BlockSpec((tb, width), lambda g0, g1: (ih(g0, g1)[0], ih(g0, g1)[1] + off))
    if kind == "par":
        return pl.BlockSpec(shape, lambda g0, g1: (0, 0))
    if kind == "parh":
        return pl.BlockSpec((shape[0], width), lambda g0, g1: (0, ih(g0, g1)[1] + off))
    raise ValueError(kind)


def _ew_grid(rows, tb, nh, order):
    return (rows // tb, nh) if order == "ih" else (nh, rows // tb)


def ew_fwd(name, f, ins, outs, rows, nh=1, tb=ROW_TILE, order="ih"):
    n_in = len(ins)

    def body(*refs):
        h = pl.program_id(1) if order == "ih" else pl.program_id(0)
        vals = [r[...].astype(F32) for r in refs[:n_in]]
        res = f(h, *vals)
        for r, v in zip(refs[n_in:], res):
            r[...] = v.astype(r.dtype)

    in_specs = [_ew_spec(kd, off, w, tb, order, a.shape) for (a, kd, off, w) in ins]
    out_specs = [_ew_spec(kd, 0, w, tb, order) for (_, kd, w, _) in outs]
    out_shape = [jax.ShapeDtypeStruct((rows, tw), dt) for (tw, _, _, dt) in outs]
    return pl.pallas_call(
        body, name=name, grid=_ew_grid(rows, tb, nh, order), in_specs=in_specs, out_specs=out_specs,
        out_shape=out_shape, compiler_params=_params(2),
    )(*[a for (a, _, _, _) in ins])


def ew_bwd(name, f, ins, cts, extras, emit, outs, rows, nh=1, tb=ROW_TILE, order="ih"):
    n_in = len(ins)
    flat_cts = [d for group in cts for d in group]
    n_ct, n_ex = len(flat_cts), len(extras)

    def body(*refs):
        g0, g1 = pl.program_id(0), pl.program_id(1)
        h = g1 if order == "ih" else g0
        vals = [r[...].astype(F32) for r in refs[:n_in]]
        ct_refs = refs[n_in:n_in + n_ct]
        ct_vals, pos = [], 0
        for group in cts:
            v = ct_refs[pos][...].astype(F32)
            for r in ct_refs[pos + 1:pos + len(group)]:
                v = v + r[...].astype(F32)
            pos += len(group)
            ct_vals.append(v)
        ex_vals = [r[...].astype(F32) for r in refs[n_in + n_ct:n_in + n_ct + n_ex]]
        _, vjp = jax.vjp(lambda *a: f(h, *a), *vals)
        grads = vjp(tuple(ct_vals))
        res = emit(grads, ex_vals)
        for r, v, (_, _, _, _, acc) in zip(refs[n_in + n_ct + n_ex:], res, outs):
            if acc is None:
                r[...] = v.astype(r.dtype)
            else:
                first = (g1 == 0) if acc == "inner" else jnp.logical_and(g0 == 0, g1 == 0)

                @pl.when(first)
                def _(r=r, v=v):
                    r[...] = v.astype(r.dtype)

                @pl.when(jnp.logical_not(first))
                def _(r=r, v=v):
                    r[...] += v.astype(r.dtype)

    operands = list(ins) + flat_cts + list(extras)
    in_specs = [_ew_spec(kd, off, w, tb, order, a.shape) for (a, kd, off, w) in operands]
    out_specs = [_ew_spec(kd, 0, w, tb, order, shp) for (shp, kd, w, _, _) in outs]
    out_shape = [jax.ShapeDtypeStruct(shp, dt) for (shp, _, _, dt, _) in outs]
    return pl.pallas_call(
        body, name=name, grid=_ew_grid(rows, tb, nh, order), in_specs=in_specs, out_specs=out_specs,
        out_shape=out_shape, compiler_params=_params(2),
    )(*[a for (a, _, _, _) in operands])


def f_rms(h, x, g):
    r = lax.rsqrt(jnp.mean(x * x, axis=-1, keepdims=True) + EPS)
    return (x * r * g,)


def _softplus(z):
    return jnp.maximum(z, 0.0) + jnp.log1p(jnp.exp(-jnp.abs(z)))


def f_small(h, sp, p1, p2):
    lane = _iota(sp.shape, 1)
    z = sp + p1
    g = -jnp.exp(p2) * _softplus(z)
    beta = jax.nn.sigmoid(z)
    logf = -_softplus(-z)
    return (jnp.where(lane < NH, g, jnp.where(lane < 2 * NH, beta, jnp.where(lane < 3 * NH, logf, 0.0))),)


def _pick(x, lane_id):
    lane = _iota(x.shape, 1)
    col = jnp.sum(jnp.where(lane == lane_id, x, 0.0), axis=1, keepdims=True)
    return jnp.broadcast_to(col, x.shape)


def f_bcast(h, so, cs):
    return _pick(so, h), _pick(so, h + NH), _pick(cs, h + 2 * NH)


def _shift_down(s):
    def down(x):
        return jnp.where(_iota(x.shape, 0) >= s, pltpu.roll(x, s, 0), 0.0)

    def up(g):
        n = g.shape[0]
        return jnp.where(_iota(g.shape, 0) < n - s, pltpu.roll(g, n - s, 0), 0.0)

    @jax.custom_vjp
    def shift(x):
        return down(x)

    shift.defvjp(lambda x: (down(x), None), lambda _, g: (up(g),))
    return shift


def _silu(x):
    return x * jax.nn.sigmoid(x)


def make_f_conv(mode):
    sh1, sh2, sh3 = _shift_down(1), _shift_down(2), _shift_down(3)

    def f(h, x, w):
        sub = _iota(w.shape, 0)

        def tap(i):
            return jnp.sum(jnp.where(sub == i, w, 0.0), axis=0, keepdims=True)

        y = sh3(x) * tap(0)
        y = y + sh2(x) * tap(1)
        y = y + sh1(x) * tap(2)
        y = y + x * tap(3)
        s = _silu(y)
        if mode == "v":
            return (s,)
        n = s * lax.rsqrt(jnp.sum(s * s, axis=-1, keepdims=True) + EPS)
        if mode == "q":
            n = n * (LANES ** -0.5)
        return (n,)

    return f


def f_post(h, o, z, g):
    r = lax.rsqrt(jnp.mean(o * o, axis=-1, keepdims=True) + EPS)
    return (o * r * g * _silu(z),)


def f_merge(h, ga, gb, ya, yb):
    return (jax.nn.sigmoid(ga) * ya + jax.nn.sigmoid(gb) * yb,)


def f_relu2(h, a):
    m = jnp.maximum(a, 0.0)
    return (m * m,)


def f_delta(h, do, o):
    return (jnp.broadcast_to(jnp.sum(do * o, axis=1, keepdims=True), o.shape),)


def cumsum_time(name, x, nseq, seq, reverse):
    nb = seq // LANES

    def body(x_ref, o_ref):
        r, c = _iota((LANES, LANES), 0), _iota((LANES, LANES), 1)
        tri = jnp.where((r <= c) if reverse else (r >= c), 1.0, 0.0).astype(F32)
        carry = jnp.zeros((1, LANES), F32)
        for b in (range(nb - 1, -1, -1) if reverse else range(nb)):
            blk = x_ref[b * LANES:(b + 1) * LANES, :]
            o_ref[b * LANES:(b + 1) * LANES, :] = _dot(tri, blk, "nn", HI) + carry
            carry = carry + jnp.sum(blk, axis=0, keepdims=True)

    spec = pl.BlockSpec((seq, LANES), lambda s: (s, 0))
    return pl.pallas_call(body, name=name, grid=(nseq,), in_specs=[spec], out_specs=spec,
                          out_shape=jax.ShapeDtypeStruct(x.shape, F32), compiler_params=_params(1))(x)


def transpose_time(name, x, nseq, seq):
    def body(x_ref, o_ref):
        o_ref[...] = x_ref[...].T

    return pl.pallas_call(
        body, name=name, grid=(nseq,), in_specs=[pl.BlockSpec((seq, LANES), lambda s: (s, 0))],
        out_specs=pl.BlockSpec((LANES, seq), lambda s: (s, 0)),
        out_shape=jax.ShapeDtypeStruct((nseq * LANES, seq), F32), compiler_params=_params(1))(x)


def _gdn_masks():
    n = GDN_ROWS
    r, c = _iota((n, n), 0), _iota((n, n), 1)
    shift = GDN_CHUNK.bit_length() - 1
    same = lax.shift_right_logical(r, shift) == lax.shift_right_logical(c, shift)
    return r, c, same


def _gdn_decay(gb):
    r, c, same = _gdn_masks()
    seg_tril = jnp.where(jnp.logical_and(same, r >= c), 1.0, 0.0).astype(F32)
    g_cum = _dot(seg_tril, gb, "nn", HI)
    lane0 = _iota(g_cum.shape, 1) == 0
    g_col = jnp.sum(jnp.where(lane0, g_cum, 0.0), axis=1, keepdims=True)
    g_row = jnp.sum(jnp.where(r == c, jnp.broadcast_to(g_col, (GDN_ROWS, GDN_ROWS)), 0.0), axis=0, keepdims=True)
    return g_cum, g_col - g_row


def gdn_f1(q, k, gb, bb):
    r, c, same = _gdn_masks()
    strict = jnp.logical_and(same, r > c)
    _, diff = _gdn_decay(gb)
    lane0 = _iota(bb.shape, 1) == 0
    beta_col = jnp.sum(jnp.where(lane0, bb, 0.0), axis=1, keepdims=True)
    kk = _dot(k, k, "nt", LO)
    return jnp.where(strict, beta_col * kk * jnp.exp(jnp.where(strict, diff, 0.0)), 0.0)


def gdn_f2(t_inv, q, k, v, gb, bb):
    r, c, same = _gdn_masks()
    incl = jnp.logical_and(same, r >= c)
    g_cum, diff = _gdn_decay(gb)
    decay = jnp.where(incl, jnp.exp(jnp.where(incl, diff, 0.0)), 0.0)
    e_g = jnp.exp(g_cum)
    value = _dot(t_inv, v * bb, "nn", HI)
    k_cum = _dot(t_inv, k * bb * e_g, "nn", HI)
    attn = _dot(q, k, "nt", LO) * decay
    g_last = _dot(jnp.where(same, 1.0, 0.0).astype(F32), gb, "nn", HI)
    return value, k_cum, attn, q * e_g, k * jnp.exp(g_last - g_cum)


def tri_inverse(a):
    n = GDN_ROWS
    r, c = _iota((n, n), 0), _iota((n, n), 1)
    shift = GDN_BASE.bit_length() - 1
    blk = lax.shift_right_logical(r, shift) == lax.shift_right_logical(c, shift)
    eye = jnp.where(r == c, 1.0, 0.0).astype(F32)
    d = jnp.where(blk, a, 0.0)
    lo = a - d
    p = -d
    t_d = eye + p
    steps = shift - 1
    for _ in range(steps):
        p = _dot(p, p, "nn", HI)
        t_d = t_d + _dot(t_d, p, "nn", HI)
    assert GDN_CHUNK // GDN_BASE == 4
    nmat = _dot(t_d, lo, "nn", HI)
    n2 = _dot(nmat, nmat, "nn", HI)
    t_n = (eye - nmat) + _dot(eye - nmat, n2, "nn", HI)
    return _dot(t_n, t_d, "nn", HI)


def gdn_a_fwd(q, k, v, gb, bb, rows):
    blk = pl.BlockSpec((GDN_ROWS, LANES), lambda i, h: (i, h))
    sq = pl.BlockSpec((GDN_ROWS, GDN_ROWS), lambda i, h: (i, h))

    def body(q_ref, k_ref, v_ref, gb_ref, bb_ref, val_ref, kc_ref, at_ref, qd_ref, kd_ref, t_ref):
        qv, kv, vv, gv, bv = q_ref[...], k_ref[...], v_ref[...], gb_ref[...], bb_ref[...]
        t_inv = tri_inverse(gdn_f1(qv, kv, gv, bv))
        value, k_cum, attn, q_dec, k_dec = gdn_f2(t_inv, qv, kv, vv, gv, bv)
        val_ref[...], kc_ref[...], at_ref[...], qd_ref[...], kd_ref[...], t_ref[...] = value, k_cum, attn, q_dec, k_dec, t_inv

    wide = jax.ShapeDtypeStruct((rows, NH * LANES), F32)
    square = jax.ShapeDtypeStruct((rows, NH * GDN_ROWS), F32)
    return pl.pallas_call(
        body, name="gdn_a_fwd", grid=(rows // GDN_ROWS, NH), in_specs=[blk] * 5,
        out_specs=[blk, blk, sq, blk, blk, sq], out_shape=[wide, wide, square, wide, wide, square],
        compiler_params=_params(2))(q, k, v, gb, bb)


def gdn_a_bwd(q, k, v, gb, bb, t_inv, dval, dkc, dat, dqd, dkd, dgb_b, rows):
    blk = pl.BlockSpec((GDN_ROWS, LANES), lambda i, h: (i, h))
    sq = pl.BlockSpec((GDN_ROWS, GDN_ROWS), lambda i, h: (i, h))

    def body(q_ref, k_ref, v_ref, gb_ref, bb_ref, t_ref, dval_ref, dkc_ref, dat_ref, dqd_ref, dkd_ref, dgbb_ref,
             dq_ref, dk_ref, dv_ref, dgb_ref, dbb_ref):
        qv, kv, vv, gv, bv, tv = q_ref[...], k_ref[...], v_ref[...], gb_ref[...], bb_ref[...], t_ref[...]
        _, vjp1 = jax.vjp(gdn_f1, qv, kv, gv, bv)
        _, vjp2 = jax.vjp(gdn_f2, tv, qv, kv, vv, gv, bv)
        dt, dq2, dk2, dv2, dgb2, dbb2 = vjp2((dval_ref[...], dkc_ref[...], dat_ref[...], dqd_ref[...], dkd_ref[...]))
        da = -_dot(tv, _dot(dt, tv, "nt", HI), "tn", HI)
        dq1, dk1, dgb1, dbb1 = vjp1(da)
        dq_ref[...] = dq1 + dq2
        dk_ref[...] = dk1 + dk2
        dv_ref[...] = dv2
        dgb_ref[...] = dgb1 + dgb2 + dgbb_ref[...]
        dbb_ref[...] = dbb1 + dbb2

    wide = jax.ShapeDtypeStruct((rows, NH * LANES), F32)
    return pl.pallas_call(
        body, name="gdn_a_bwd", grid=(rows // GDN_ROWS, NH),
        in_specs=[blk] * 5 + [sq, blk, blk, sq, blk, blk, blk], out_specs=[blk] * 5, out_shape=[wide] * 5,
        compiler_params=_params(2))(q, k, v, gb, bb, t_inv, dval, dkc, dat, dqd, dkd, dgb_b)


N_CH = GDN_ROWS // GDN_CHUNK


def gdn_fb(*args):
    val, kc, at, qd, kd, gb = (args[i * N_CH:(i + 1) * N_CH] for i in range(6))
    s = args[6 * N_CH]
    outs = []
    zero = jnp.zeros((GDN_CHUNK, LANES), F32)
    for c in range(N_CH):
        v_new = val[c] - _dot(kc[c], s, "nn", LO)
        v_pad = jnp.concatenate([zero] * c + [v_new] + [zero] * (N_CH - 1 - c), axis=0)
        outs.append(_dot(qd[c], s, "nn", LO) + _dot(at[c], v_pad, "nn", LO))
        dec = jnp.exp(jnp.sum(gb[c], axis=0, keepdims=True))
        s = s * dec + _dot(kd[c], v_new, "tn", LO)
    return (*outs, s)


def _gdn_pieces(refs):
    return [r[c * GDN_CHUNK:(c + 1) * GDN_CHUNK, :] for r in refs for c in range(N_CH)]


def gdn_b_fwd(val, kc, at, qd, kd, gb, nseq, seq):
    nb = seq // GDN_ROWS
    rows = nseq * seq
    blk = pl.BlockSpec((GDN_ROWS, LANES), lambda s, h, j: (s * nb + j, h))
    sq = pl.BlockSpec((GDN_ROWS, GDN_ROWS), lambda s, h, j: (s * nb + j, h))
    snap = pl.BlockSpec((LANES, LANES), lambda s, h, j: ((s * NH + h) * nb + j, 0))

    def body(val_ref, kc_ref, at_ref, qd_ref, kd_ref, gb_ref, o_ref, snap_ref, s_ref):
        @pl.when(pl.program_id(2) == 0)
        def _():
            s_ref[...] = jnp.zeros_like(s_ref)

        s_in = s_ref[...]
        snap_ref[...] = s_in
        res = gdn_fb(*_gdn_pieces([val_ref, kc_ref, at_ref, qd_ref, kd_ref, gb_ref]), s_in)
        for c in range(N_CH):
            o_ref[c * GDN_CHUNK:(c + 1) * GDN_CHUNK, :] = res[c]
        s_ref[...] = res[N_CH]

    return pl.pallas_call(
        body, name="gdn_b_fwd", grid=(nseq, NH, nb), in_specs=[blk, blk, sq, blk, blk, blk], out_specs=[blk, snap],
        out_shape=[jax.ShapeDtypeStruct((rows, NH * LANES), F32), jax.ShapeDtypeStruct((nseq * NH * nb * LANES, LANES), F32)],
        scratch_shapes=[pltpu.VMEM((LANES, LANES), F32)], compiler_params=_params(3))(val, kc, at, qd, kd, gb)


def gdn_b_bwd(val, kc, at, qd, kd, gb, snaps, do, nseq, seq):
    nb = seq // GDN_ROWS
    rows = nseq * seq
    blk = pl.BlockSpec((GDN_ROWS, LANES), lambda s, h, j: (s * nb + nb - 1 - j, h))
    sq = pl.BlockSpec((GDN_ROWS, GDN_ROWS), lambda s, h, j: (s * nb + nb - 1 - j, h))
    snap = pl.BlockSpec((LANES, LANES), lambda s, h, j: ((s * NH + h) * nb + nb - 1 - j, 0))

    def body(val_ref, kc_ref, at_ref, qd_ref, kd_ref, gb_ref, snap_ref, do_ref,
             dval_ref, dkc_ref, dat_ref, dqd_ref, dkd_ref, dgb_ref, ds_ref):
        @pl.when(pl.program_id(2) == 0)
        def _():
            ds_ref[...] = jnp.zeros_like(ds_ref)

        _, vjp = jax.vjp(gdn_fb, *_gdn_pieces([val_ref, kc_ref, at_ref, qd_ref, kd_ref, gb_ref]), snap_ref[...])
        grads = vjp((*_gdn_pieces([do_ref]), ds_ref[...]))
        for i, r in enumerate([dval_ref, dkc_ref, dat_ref, dqd_ref, dkd_ref, dgb_ref]):
            for c in range(N_CH):
                r[c * GDN_CHUNK:(c + 1) * GDN_CHUNK, :] = grads[i * N_CH + c]
        ds_ref[...] = grads[6 * N_CH]

    wide = jax.ShapeDtypeStruct((rows, NH * LANES), F32)
    square = jax.ShapeDtypeStruct((rows, NH * GDN_ROWS), F32)
    return pl.pallas_call(
        body, name="gdn_b_bwd", grid=(nseq, NH, nb), in_specs=[blk, blk, sq, blk, blk, blk, snap, blk],
        out_specs=[blk, blk, sq, blk, blk, blk], out_shape=[wide, wide, square, wide, wide, wide],
        scratch_shapes=[pltpu.VMEM((LANES, LANES), F32)], compiler_params=_params(3))(val, kc, at, qd, kd, gb, snaps, do)


FOX_Q, FOX_K, FOX_V = 4 * NH, 5 * NH, 6 * NH
FOX_SCALE = LANES ** -0.5


def _fox_scores(q, k, cq, ck, row0, col0):
    s = _dot(q, k, "nt") * FOX_SCALE + cq - ck
    rows = row0 + _iota(s.shape, 0)
    cols = col0 + _iota(s.shape, 1)
    return s, rows >= cols


def _head_row(ct_ref, h, off, width):
    blk = ct_ref[:, pl.ds(off, width)]
    return jnp.sum(jnp.where(_iota(blk.shape, 0) == h, blk, 0.0), axis=0, keepdims=True)


def _col(x):
    return jnp.max(x, axis=1, keepdims=True)


def fox_fwd(qn, kn, proj, cb, ct, nseq, seq):
    tq = tk = min(ATT_TILE, seq)
    nq = seq // tq
    rows = nseq * seq
    qblk = pl.BlockSpec((tq, LANES), lambda s, h, i: (s * nq + i, h))
    full = pl.BlockSpec((seq, LANES), lambda s, h, i: (s, h))
    vfull = pl.BlockSpec((seq, LANES), lambda s, h, i: (s, h + FOX_V))
    ctb = pl.BlockSpec((NH, seq), lambda s, h, i: (s * (LANES // NH) + 2, 0))

    def body(q_ref, k_ref, v_ref, cb_ref, ct_ref, o_ref, o16_ref, lse_ref):
        h, i = pl.program_id(1), pl.program_id(2)
        q = q_ref[...]
        cq = _col(cb_ref[...])

        def step(j, carry):
            m, l, acc = carry
            off = pl.multiple_of(j * tk, tk)
            s, mask = _fox_scores(q, k_ref[pl.ds(off, tk), :], cq, _head_row(ct_ref, h, off, tk), i * tq, j * tk)
            s = jnp.where(mask, s, NEG)
            m_new = jnp.maximum(m, jnp.max(s, axis=1, keepdims=True))
            p = jnp.exp(s - m_new)
            alpha = jnp.exp(m - m_new)
            l = alpha * l + jnp.sum(p, axis=1, keepdims=True)
            acc = alpha * acc + _dot(p.astype(BF16), v_ref[pl.ds(off, tk), :].astype(BF16), "nn")
            return m_new, l, acc

        init = (jnp.full((tq, 1), NEG, F32), jnp.zeros((tq, 1), F32), jnp.zeros((tq, LANES), F32))
        m, l, acc = lax.fori_loop(0, i + 1, step, init)
        o = acc / l
        o_ref[...] = o
        o16_ref[...] = o.astype(BF16)
        lse_ref[...] = jnp.broadcast_to(m + jnp.log(l), (tq, LANES))

    wide = (rows, NH * LANES)
    return pl.pallas_call(
        body, name="fox_fwd", grid=(nseq, NH, nq), in_specs=[qblk, full, vfull, qblk, ctb], out_specs=[qblk] * 3,
        out_shape=[jax.ShapeDtypeStruct(wide, F32), jax.ShapeDtypeStruct(wide, BF16), jax.ShapeDtypeStruct(wide, F32)],
        compiler_params=_params(3))(qn, kn, proj, cb, ct)


def fox_dq(qn, kn, proj, cb, ct, do, lse, delta, nseq, seq):
    tq = tk = min(ATT_TILE, seq)
    nq = seq // tq
    rows = nseq * seq
    qblk = pl.BlockSpec((tq, LANES), lambda s, h, i: (s * nq + i, h))
    full = pl.BlockSpec((seq, LANES), lambda s, h, i: (s, h))
    vfull = pl.BlockSpec((seq, LANES), lambda s, h, i: (s, h + FOX_V))
    ctb = pl.BlockSpec((NH, seq), lambda s, h, i: (s * (LANES // NH) + 2, 0))

    def body(q_ref, k_ref, v_ref, cb_ref, ct_ref, do_ref, lse_ref, dl_ref, dq_ref, dc_ref):
        h, i = pl.program_id(1), pl.program_id(2)
        q = q_ref[...]
        cq, lse, delta = _col(cb_ref[...]), _col(lse_ref[...]), _col(dl_ref[...])
        do16 = do_ref[...].astype(BF16)

        def step(j, carry):
            dq, dc = carry
            off = pl.multiple_of(j * tk, tk)
            k = k_ref[pl.ds(off, tk), :]
            s, mask = _fox_scores(q, k, cq, _head_row(ct_ref, h, off, tk), i * tq, j * tk)
            p = jnp.where(mask, jnp.exp(s - lse), 0.0)
            dp = _dot(do16, v_ref[pl.ds(off, tk), :].astype(BF16), "nt")
            ds = p * (dp - delta)
            return dq + _dot(ds.astype(BF16), k, "nn"), dc + jnp.sum(ds, axis=1, keepdims=True)

        dq, dc = lax.fori_loop(0, i + 1, step, (jnp.zeros((tq, LANES), F32), jnp.zeros((tq, 1), F32)))
        dq_ref[...] = dq * FOX_SCALE
        dc_ref[...] = jnp.where(_iota((tq, LANES), 1) == 0, dc, 0.0)

    wide = jax.ShapeDtypeStruct((rows, NH * LANES), F32)
    return pl.pallas_call(
        body, name="fox_dq", grid=(nseq, NH, nq), in_specs=[qblk, full, vfull, qblk, ctb, qblk, qblk, qblk],
        out_specs=[qblk, qblk], out_shape=[wide, wide], compiler_params=_params(3))(qn, kn, proj, cb, ct, do, lse, delta)


def fox_dkv(qn, kn, proj, cb, ct, do, lse, delta, nseq, seq):
    tq = tk = min(ATT_TILE, seq)
    nq = seq // tq
    rows = nseq * seq
    kblk = pl.BlockSpec((tk, LANES), lambda s, h, j: (s * nq + j, h))
    vblk = pl.BlockSpec((tk, LANES), lambda s, h, j: (s * nq + j, h + FOX_V))
    full = pl.BlockSpec((seq, LANES), lambda s, h, j: (s, h))
    ctb = pl.BlockSpec((NH, seq), lambda s, h, j: (s * (LANES // NH) + 2, 0))

    def body(q_ref, k_ref, v_ref, cb_ref, ct_ref, do_ref, lse_ref, dl_ref, dk_ref, dv_ref, dc_ref):
        h, j = pl.program_id(1), pl.program_id(2)
        k = k_ref[...]
        v16 = v_ref[...].astype(BF16)
        ck = _head_row(ct_ref, h, pl.multiple_of(j * tk, tk), tk)
        e0 = jnp.where(_iota((tq, LANES), 1) == 0, 1.0, 0.0).astype(BF16)

        def step(i, carry):
            dk, dv, dc = carry
            off = pl.multiple_of(i * tq, tq)
            q = q_ref[pl.ds(off, tq), :]
            do16 = do_ref[pl.ds(off, tq), :].astype(BF16)
            cq, lse, delta = (_col(r[pl.ds(off, tq), :]) for r in (cb_ref, lse_ref, dl_ref))
            s, mask = _fox_scores(q, k, cq, ck, i * tq, j * tk)
            p = jnp.where(mask, jnp.exp(s - lse), 0.0)
            dv = dv + _dot(p.astype(BF16), do16, "tn")
            ds = (p * (_dot(do16, v16, "nt") - delta)).astype(BF16)
            return dk + _dot(ds, q, "tn"), dv, dc + _dot(ds, e0, "tn")

        zero = jnp.zeros((tk, LANES), F32)
        dk, dv, dc = lax.fori_loop(j, nq, step, (zero, zero, zero))
        dk_ref[...] = dk * FOX_SCALE
        dv_ref[...] = dv.astype(BF16)
        dc_ref[...] = -dc

    wide = (rows, NH * LANES)
    return pl.pallas_call(
        body, name="fox_dkv", grid=(nseq, NH, nq), in_specs=[full, kblk, vblk, full, ctb, full, full, full],
        out_specs=[kblk, kblk, kblk],
        out_shape=[jax.ShapeDtypeStruct(wide, F32), jax.ShapeDtypeStruct(wide, BF16), jax.ShapeDtypeStruct(wide, F32)],
        compiler_params=_params(3))(qn, kn, proj, cb, ct, do, lse, delta)


def loss_head(out, tgt, rows, width):
    tb = ROW_TILE
    blk = pl.BlockSpec((tb, width), lambda i: (i, 0))
    accb = pl.BlockSpec((8, LANES), lambda i: (0, 0))

    def body(o_ref, t_ref, d32_ref, d16_ref, acc_ref):
        d = o_ref[...] - t_ref[...]
        row_loss = 0.5 * jnp.mean(d * d, axis=1, keepdims=True)
        g = d * (1.0 / width)
        d32_ref[...] = g
        d16_ref[...] = g.astype(BF16)
        part = jnp.where(_iota((tb, LANES), 1) == 0, row_loss, 0.0).reshape(tb // 8, 8, LANES).sum(axis=0)

        @pl.when(pl.program_id(0) == 0)
        def _():
            acc_ref[...] = part

        @pl.when(pl.program_id(0) != 0)
        def _():
            acc_ref[...] += part

    return pl.pallas_call(
        body, name="loss_head", grid=(rows // tb,), in_specs=[blk, blk], out_specs=[blk, blk, accb],
        out_shape=[jax.ShapeDtypeStruct((rows, width), F32), jax.ShapeDtypeStruct((rows, width), BF16),
                   jax.ShapeDtypeStruct((8, LANES), F32)], compiler_params=_params(1))(out, tgt)


def adamw(name, w, g, m, v):
    rows, cols = w.shape
    tb = min(rows, 128)
    assert rows % tb == 0
    blk = pl.BlockSpec((tb, cols), lambda i: (i, 0))

    def body(w_ref, g_ref, m_ref, v_ref, d_ref, mo_ref, vo_ref):
        gv = g_ref[...]
        m_new = ADAM_B1 * m_ref[...] + (1.0 - ADAM_B1) * gv
        v_new = ADAM_B2 * v_ref[...] + (1.0 - ADAM_B2) * (gv * gv)
        m_hat = m_new / (1.0 - ADAM_B1 ** ADAM_STEP)
        v_hat = v_new / (1.0 - ADAM_B2 ** ADAM_STEP)
        d_ref[...] = -ADAM_LR * (m_hat / (jnp.sqrt(v_hat) + ADAM_EPS) + ADAM_WD * w_ref[...])
        mo_ref[...] = m_new
        vo_ref[...] = v_new

    shp = jax.ShapeDtypeStruct(w.shape, F32)
    return pl.pallas_call(body, name=name, grid=(rows // tb,), in_specs=[blk] * 4, out_specs=[blk] * 3,
                          out_shape=[shp] * 3, compiler_params=_params(1))(w, g, m, v)


def add_n(name, xs):
    outs = []
    for idx, x in enumerate(xs):
        n, rows, cols = x.shape
        tb = min(rows, 128)
        assert rows % tb == 0

        def body(x_ref, o_ref, n=n):
            acc = x_ref[0]
            for t in range(1, n):
                acc = acc + x_ref[t]
            o_ref[...] = acc

        outs.append(pl.pallas_call(
            body, name=f"{name}_{idx}", grid=(rows // tb,), in_specs=[pl.BlockSpec((n, tb, cols), lambda i: (0, i, 0))],
            out_specs=pl.BlockSpec((tb, cols), lambda i: (i, 0)), out_shape=jax.ShapeDtypeStruct((rows, cols), F32),
            compiler_params=_params(1))(x))
    return outs


def add_pair(name, gs, rs, c):
    outs = []
    for idx, (g, r) in enumerate(zip(gs, rs)):
        nb, half, cols = r.shape
        tb = min(half, 128)
        steps = half // tb

        def body(c_ref, g_ref, r_ref, o_ref):
            o_ref[...] = g_ref[...] + r_ref[...]

        grid_spec = pltpu.PrefetchScalarGridSpec(
            num_scalar_prefetch=1, grid=(nb, steps),
            in_specs=[pl.BlockSpec((None, tb, cols), lambda b, i, c_ref: (b, c_ref[0] * steps + i, 0)),
                      pl.BlockSpec((None, tb, cols), lambda b, i, c_ref: (b, i, 0))],
            out_specs=pl.BlockSpec((None, tb, cols), lambda b, i, c_ref: (b, i, 0)))
        outs.append(pl.pallas_call(
            body, name=f"{name}_{idx}", grid_spec=grid_spec, out_shape=jax.ShapeDtypeStruct(r.shape, F32),
            compiler_params=_params(2))(c, g, r))
    return outs


def _place():
    x, y, c = lax.axis_index("x"), lax.axis_index("y"), lax.axis_index("c")
    return x, y, c, [(1 - x, y), (x, 1 - y), (1 - x, 1 - y)]


def _remote(src, dst, send_sem, recv_sem, dev):
    return pltpu.make_async_remote_copy(src_ref=src, dst_ref=dst, send_sem=send_sem, recv_sem=recv_sem,
                                        device_id=dev, device_id_type=MESH)


def gather_weights(shards):
    n = len(shards)

    def body(*refs):
        ins, outs = refs[:n], refs[n:2 * n]
        ici_s, ici_r, d2d_s, d2d_r, loc = refs[2 * n:]
        x, y, c, chips = _place()
        me = 2 * x + y
        local = [pltpu.make_async_copy(ins[w], outs[w].at[me], loc.at[w]) for w in range(n)]
        for cp in local:
            cp.start()
        sends, passes = [], []
        for w in range(n):
            half = ins[w].shape[0] // 2
            for j, (ox, oy) in enumerate(chips):
                mine = pl.ds(c * half, half)
                cp = _remote(ins[w].at[mine, :], outs[w].at[me, mine, :], ici_s.at[3 * w + j], ici_r.at[3 * w + j], (ox, oy, c))
                cp.start()
                sends.append(cp)
        for w in range(n):
            half = ins[w].shape[0] // 2
            for j, (ox, oy) in enumerate(chips):
                landed = outs[w].at[2 * ox + oy, pl.ds(c * half, half), :]
                _remote(landed, landed, ici_s.at[3 * w + j], ici_r.at[3 * w + j], (ox, oy, c)).wait_recv()
                cp = _remote(landed, landed, d2d_s.at[3 * w + j], d2d_r.at[3 * w + j], (x, y, 1 - c))
                cp.start()
                passes.append(cp)
        for w in range(n):
            half = ins[w].shape[0] // 2
            for j, (ox, oy) in enumerate(chips):
                other = outs[w].at[2 * ox + oy, pl.ds((1 - c) * half, half), :]
                _remote(other, other, d2d_s.at[3 * w + j], d2d_r.at[3 * w + j], (x, y, 1 - c)).wait_recv()
        for cp in sends + passes:
            cp.wait_send()
        for cp in local:
            cp.wait()

    return pl.pallas_call(
        body, name="gather_weights", in_specs=[ANY] * n, out_specs=[ANY] * n,
        out_shape=[jax.ShapeDtypeStruct((4,) + s.shape, s.dtype) for s in shards],
        scratch_shapes=[pltpu.SemaphoreType.DMA((3 * n,))] * 4 + [pltpu.SemaphoreType.DMA((n,))],
    )(*shards)


def pair_swap(grads):
    n = len(grads)

    def body(*refs):
        ins, outs = refs[:n], refs[n:2 * n]
        send, recv = refs[2 * n:]
        x, y, c, _ = _place()
        cps = []
        for w in range(n):
            half = ins[w].shape[1] // 2
            cp = _remote(ins[w].at[:, pl.ds((1 - c) * half, half), :], outs[w], send.at[w], recv.at[w], (x, y, 1 - c))
            cp.start()
            cps.append(cp)
        for cp in cps:
            cp.wait_recv()
        for cp in cps:
            cp.wait_send()

    return pl.pallas_call(
        body, name="pair_swap", in_specs=[ANY] * n, out_specs=[ANY] * n,
        out_shape=[jax.ShapeDtypeStruct((4, g.shape[1] // 2, g.shape[2]), g.dtype) for g in grads],
        scratch_shapes=[pltpu.SemaphoreType.DMA((n,))] * 2,
    )(*grads)


def chip_exchange(parts):
    n = len(parts)

    def body(*refs):
        ins, outs = refs[:n], refs[n:2 * n]
        send, recv, loc = refs[2 * n:]
        x, y, c, chips = _place()
        me = 2 * x + y
        local = [pltpu.make_async_copy(ins[w].at[me], outs[w].at[me], loc.at[w]) for w in range(n)]
        for cp in local:
            cp.start()
        cps = []
        for w in range(n):
            for j, (ox, oy) in enumerate(chips):
                cp = _remote(ins[w].at[2 * ox + oy], outs[w].at[me], send.at[3 * w + j], recv.at[3 * w + j], (ox, oy, c))
                cp.start()
                cps.append(cp)
        for w in range(n):
            for j, (ox, oy) in enumerate(chips):
                slot = outs[w].at[2 * ox + oy]
                _remote(slot, slot, send.at[3 * w + j], recv.at[3 * w + j], (ox, oy, c)).wait_recv()
        for cp in cps:
            cp.wait_send()
        for cp in local:
            cp.wait()

    return pl.pallas_call(
        body, name="chip_exchange", in_specs=[ANY] * n, out_specs=[ANY] * n,
        out_shape=[jax.ShapeDtypeStruct(p.shape, p.dtype) for p in parts],
        scratch_shapes=[pltpu.SemaphoreType.DMA((3 * n,))] * 2 + [pltpu.SemaphoreType.DMA((n,))],
    )(*parts)


def pair_gather(halves):
    n = len(halves)

    def body(*refs):
        ins, outs = refs[:n], refs[n:2 * n]
        send, recv, loc = refs[2 * n:]
        x, y, c, _ = _place()
        local, cps = [], []
        for w in range(n):
            half = ins[w].shape[0]
            mine = outs[w].at[pl.ds(c * half, half), :]
            cp = pltpu.make_async_copy(ins[w], mine, loc.at[w])
            cp.start()
            local.append(cp)
            cp = _remote(ins[w], mine, send.at[w], recv.at[w], (x, y, 1 - c))
            cp.start()
            cps.append(cp)
        for w in range(n):
            half = ins[w].shape[0]
            other = outs[w].at[pl.ds((1 - c) * half, half), :]
            _remote(other, other, send.at[w], recv.at[w], (x, y, 1 - c)).wait_recv()
        for cp in cps:
            cp.wait_send()
        for cp in local:
            cp.wait()

    return pl.pallas_call(
        body, name="pair_gather", in_specs=[ANY] * n, out_specs=[ANY] * n,
        out_shape=[jax.ShapeDtypeStruct((2 * h.shape[0], h.shape[1]), h.dtype) for h in halves],
        scratch_shapes=[pltpu.SemaphoreType.DMA((n,))] * 2 + [pltpu.SemaphoreType.DMA((n,))],
    )(*halves)


def all_reduce_small(name, vec):
    rows = vec.shape[0]

    def body(v_ref, o_ref, buf, send, recv):
        x, y, c, _ = _place()
        me = 4 * x + 2 * y + c
        buf[me] = v_ref[...]
        cps = []
        for k in range(1, 8):
            kx, ky, kc = (k >> 2) & 1, (k >> 1) & 1, k & 1
            peer = (x if kx == 0 else 1 - x, y if ky == 0 else 1 - y, c if kc == 0 else 1 - c)
            cp = _remote(v_ref, buf.at[me], send.at[k - 1], recv.at[k - 1], peer)
            cp.start()
            cps.append(cp)
        for k in range(1, 8):
            kx, ky, kc = (k >> 2) & 1, (k >> 1) & 1, k & 1
            px, py, pc = (x if kx == 0 else 1 - x, y if ky == 0 else 1 - y, c if kc == 0 else 1 - c)
            slot = buf.at[4 * px + 2 * py + pc]
            _remote(slot, slot, send.at[k - 1], recv.at[k - 1], (px, py, pc)).wait_recv()
        for cp in cps:
            cp.wait_send()
        acc = buf[0]
        for d in range(1, 8):
            acc = acc + buf[d]
        o_ref[...] = acc

    vm = pl.BlockSpec(memory_space=pltpu.VMEM)
    return pl.pallas_call(
        body, name=name, in_specs=[vm], out_specs=vm, out_shape=jax.ShapeDtypeStruct(vec.shape, F32),
        scratch_shapes=[pltpu.VMEM((8, rows, LANES), F32), pltpu.SemaphoreType.DMA((7,)), pltpu.SemaphoreType.DMA((7,))],
    )(vec)


def local_step(x2, tgt2, g1, g2, gdn_ng, qn_g, kn_g, p1, p2, conv_w, w_main, w_small, p_a, p_b, w_o, w_u, w_d, nseq, seq):
    rows, dm = x2.shape
    wide = NH * LANES
    row = lambda a, off=0, w=None: (a, "row", off, a.shape[1] if w is None else w)
    rowh = lambda a, off=0, w=LANES: (a, "rowh", off, w)
    par = lambda a: (a, "par", 0, a.shape[1])
    parh = lambda a, off=0: (a, "parh", off, LANES)
    o_row = lambda w, dt: (w, "row", w, dt)
    o_rowh = lambda dt, tw=wide, w=LANES: (tw, "rowh", w, dt)

    u, = ew_fwd("rms1", f_rms, [row(x2), par(g1)], [o_row(dm, BF16)], rows)
    proj = matmul("mm_in", u, w_main, "nn", F32, tn=1024)
    sp = matmul("mm_in_small", u, w_small, "nn", F32)
    so, = ew_fwd("small", f_small, [row(sp), par(p1), par(p2)], [o_row(LANES, F32)], rows)
    cs = cumsum_time("cumsum", so, nseq, seq, False)
    gb, bb, cb = ew_fwd("bcast", f_bcast, [row(so), row(cs)], [o_rowh(F32)] * 3, rows, NH)
    ct = transpose_time("c_time_major", cs, nseq, seq)
    conv = {}
    for mode, off in (("q", 0), ("k", NH), ("v", 2 * NH)):
        conv[mode], = ew_fwd(f"conv_{mode}", make_f_conv(mode), [rowh(proj, off), parh(conv_w, off)], [o_rowh(F32)],
                             rows, NH, seq, "hi")
    val, kcum, attn, qdec, kdec, t_inv = gdn_a_fwd(conv["q"], conv["k"], conv["v"], gb, bb, rows)
    o_a, snaps = gdn_b_fwd(val, kcum, attn, qdec, kdec, gb, nseq, seq)
    ya_in, = ew_fwd("gdn_post", f_post, [rowh(o_a), rowh(proj, 3 * NH), par(gdn_ng)], [o_rowh(BF16)], rows, NH)
    fqn, = ew_fwd("fox_qn", f_rms, [rowh(proj, FOX_Q), par(qn_g)], [o_rowh(BF16)], rows, NH)
    fkn, = ew_fwd("fox_kn", f_rms, [rowh(proj, FOX_K), par(kn_g)], [o_rowh(BF16)], rows, NH)
    o_b, o_b16, lse = fox_fwd(fqn, fkn, proj, cb, ct, nseq, seq)
    y_a = matmul("mm_pa", ya_in, p_a, "nn", F32, tn=1024)
    y_b = matmul("mm_pb", o_b16, p_b, "nn", F32, tn=1024)
    gates = [row(proj, 7, dm), row(proj, 8, dm)]
    merged, = ew_fwd("merge", f_merge, gates + [row(y_a), row(y_b)], [o_row(dm, BF16)], rows)
    hres = matmul("mm_out", merged, w_o, "nn", F32, add=x2, tn=1024)
    hn, = ew_fwd("rms2", f_rms, [row(hres), par(g2)], [o_row(dm, BF16)], rows)
    dff = w_u.shape[1]
    act = matmul("mm_up", hn, w_u, "nn", F32, tn=1024)
    nff = dff // dm
    relu2, = ew_fwd("relu2", f_relu2, [rowh(act, 0, dm)], [o_rowh(BF16, dff, dm)], rows, nff)
    out = matmul("mm_down", relu2, w_d, "nn", F32, add=hres, tn=1024)
    dout, dout16, loss_acc = loss_head(out, tgt2, rows, dm)

    g_first = lambda dt=F32: (lambda g, e: [g[0]])
    d_relu2 = matmul("mm_d_relu2", dout16, w_d, "nt", F32, tn=1024)
    dw_d = matmul("mm_dw_down", relu2, dout16, "tn", F32, tn=1024)
    d_act, = ew_bwd("relu2_b", f_relu2, [rowh(act, 0, dm)], [(rowh(d_relu2, 0, dm),)], [], lambda g, e: [g[0]],
                    [((rows, dff), "rowh", dm, BF16, None)], rows, nff)
    dw_u = matmul("mm_dw_up", hn, d_act, "tn", F32, tn=1024)
    d_hn = matmul("mm_d_hn", d_act, w_u, "nt", F32, tn=1024)
    dh, dh16, dg2 = ew_bwd("rms2_b", f_rms, [row(hres), par(g2)], [(row(d_hn),)], [row(dout)],
                           lambda g, e: [g[0] + e[0], g[0] + e[0], g[1]],
                           [((rows, dm), "row", dm, F32, None), ((rows, dm), "row", dm, BF16, None), ((1, dm), "par", dm, F32, "all")], rows)
    d_merged = matmul("mm_d_merged", dh16, w_o, "nt", F32, tn=1024)
    dw_o = matmul("mm_dw_out", merged, dh16, "tn", F32, tn=1024)
    seg16 = ((rows, dm), "row", dm, BF16, None)
    d_ga16, d_gb16, d_ya16, d_yb16 = ew_bwd("merge_b", f_merge, gates + [row(y_a), row(y_b)], [(row(d_merged),)], [],
                                            lambda g, e: list(g), [seg16] * 4, rows)
    dp_a = matmul("mm_dp_a", ya_in, d_ya16, "tn", F32, tn=1024)
    d_ya_in = matmul("mm_d_ya_in", d_ya16, p_a, "nt", F32, tn=1024)
    dp_b = matmul("mm_dp_b", o_b16, d_yb16, "tn", F32, tn=1024)
    d_ob = matmul("mm_d_ob", d_yb16, p_b, "nt", F32, tn=1024)
    h32 = ((rows, wide), "rowh", LANES, F32, None)
    h16 = ((rows, wide), "rowh", LANES, BF16, None)
    gain = ((1, LANES), "par", LANES, F32, "all")
    d_oa, d_z16, d_gdn_ng = ew_bwd("gdn_post_b", f_post, [rowh(o_a), rowh(proj, 3 * NH), par(gdn_ng)], [(rowh(d_ya_in),)], [],
                                   lambda g, e: list(g), [h32, h16, gain], rows, NH)
    dval, dkc, dat, dqd, dkd, dgb_b = gdn_b_bwd(val, kcum, attn, qdec, kdec, gb, snaps, d_oa, nseq, seq)
    d_cq, d_ck, d_cv, d_gb, d_bb = gdn_a_bwd(conv["q"], conv["k"], conv["v"], gb, bb, t_inv, dval, dkc, dat, dqd, dkd, dgb_b, rows)
    d_pre, d_conv = {}, {}
    tap = ((4, wide), "parh", LANES, F32, "inner")
    for mode, off, ctg in (("q", 0, d_cq), ("k", NH, d_ck), ("v", 2 * NH, d_cv)):
        d_pre[mode], d_conv[mode] = ew_bwd(f"conv_{mode}_b", make_f_conv(mode), [rowh(proj, off), parh(conv_w, off)],
                                           [(rowh(ctg),)], [], lambda g, e: list(g), [h16, tap], rows, NH, seq, "hi")
    delta, = ew_fwd("fox_delta", f_delta, [rowh(d_ob), rowh(o_b)], [o_rowh(F32)], rows, NH)
    d_fqn, d_cq_b = fox_dq(fqn, fkn, proj, cb, ct, d_ob, lse, delta, nseq, seq)
    d_fkn, d_fv16, d_ck_b = fox_dkv(fqn, fkn, proj, cb, ct, d_ob, lse, delta, nseq, seq)
    d_fq16, d_qn_g = ew_bwd("fox_qn_b", f_rms, [rowh(proj, FOX_Q), par(qn_g)], [(rowh(d_fqn),)], [], lambda g, e: list(g),
                            [h16, gain], rows, NH)
    d_fk16, d_kn_g = ew_bwd("fox_kn_b", f_rms, [rowh(proj, FOX_K), par(kn_g)], [(rowh(d_fkn),)], [], lambda g, e: list(g),
                            [h16, gain], rows, NH)
    narrow = ((rows, LANES), "row", LANES, F32, "inner")
    d_so, d_cs = ew_bwd("bcast_b", f_bcast, [row(so), row(cs)], [(rowh(d_gb),), (rowh(d_bb),), (rowh(d_cq_b), rowh(d_ck_b))], [],
                        lambda g, e: list(g), [narrow, narrow], rows, NH)
    d_logf = cumsum_time("cumsum_b", d_cs, nseq, seq, True)
    vec = ((1, LANES), "par", LANES, F32, "all")
    d_sp16, d_p1, d_p2 = ew_bwd("small_b", f_small, [row(sp), par(p1), par(p2)], [(row(d_so), row(d_logf))], [],
                                lambda g, e: list(g), [((rows, LANES), "row", LANES, BF16, None), vec, vec], rows)
    d_proj16 = jnp.concatenate([d_pre["q"], d_pre["k"], d_pre["v"], d_z16, d_fq16, d_fk16, d_fv16, d_ga16, d_gb16], axis=1)
    dw_main = matmul("mm_dw_main", u, d_proj16, "tn", F32, tn=1024)
    dw_small = matmul("mm_dw_small", u, d_sp16, "tn", F32)
    d_u = matmul("mm_d_u_small", d_sp16, w_small, "nt", F32, tn=1024)
    d_u = matmul("mm_d_u", d_proj16, w_main, "nt", F32, add=d_u, tn=1024)
    dx, dg1 = ew_bwd("rms1_b", f_rms, [row(x2), par(g1)], [(row(d_u),)], [row(dh)], lambda g, e: [g[0] + e[0], g[1]],
                     [((rows, dm), "row", dm, F32, None), ((1, dm), "par", dm, F32, "all")], rows)
    d_conv_w = jnp.concatenate([d_conv["q"], d_conv["k"], d_conv["v"]], axis=1)
    return dict(loss_acc=loss_acc, dx=dx, g1=dg1, g2=dg2, gdn_ng=d_gdn_ng, qn=d_qn_g, kn=d_kn_g, p1=d_p1, p2=d_p2,
                conv=d_conv_w, w_main=dw_main, w_small=dw_small, p_a=dp_a, p_b=dp_b, w_o=dw_o, w_u=dw_u, w_d=dw_d)


_W = NH * LANES
_A0, _A1 = 4 * _W, 4 * _W + 2 * NH
_B0, _B1 = _A1 + 3 * _W, _A1 + 3 * _W + NH
N_IN = _B1 + 2 * _W


def _split_w_in(full):
    main = jnp.concatenate([full[:, :_A0], full[:, _A1:_B0], full[:, _B1:]], axis=1)
    small = jnp.concatenate([full[:, _A0:_A1], full[:, _B0:_B1], jnp.zeros((full.shape[0], LANES - 3 * NH), full.dtype)], axis=1)
    return main, small


def _join_w_in(main, small):
    return jnp.concatenate([main[:, :_A0], small[:, :2 * NH], main[:, _A0:_A0 + 3 * _W], small[:, 2 * NH:3 * NH],
                            main[:, _A0 + 3 * _W:]], axis=1)


def _lanes(v, at=0):
    return jnp.pad(v.reshape(1, -1), ((0, 0), (at, LANES - at - v.size)))


def kernel(x, norm_mix_g, w_in, gdn_conv_w, gdn_a_log, gdn_dt_bias, gdn_norm_g, fox_q_norm_g, fox_k_norm_g, fox_f_bias, w_proj_gdn, w_proj_fox, w_out, norm_mlp_g, w_up, w_down, loss_target, m_norm_mix_g, m_w_in, m_gdn_conv_w, m_gdn_a_log, m_gdn_dt_bias, m_gdn_norm_g, m_fox_q_norm_g, m_fox_k_norm_g, m_fox_f_bias, m_w_proj_gdn, m_w_proj_fox, m_w_out, m_norm_mlp_g, m_w_up, m_w_down, v_norm_mix_g, v_w_in, v_gdn_conv_w, v_gdn_a_log, v_gdn_dt_bias, v_gdn_norm_g, v_fox_q_norm_g, v_fox_k_norm_g, v_fox_f_bias, v_w_proj_gdn, v_w_proj_fox, v_w_out, v_norm_mlp_g, v_w_up, v_w_down):
    nseq, seq, dm = x.shape
    rows = nseq * seq
    xi, yi, ci = lax.axis_index("x"), lax.axis_index("y"), lax.axis_index("c")
    chip = 2 * xi + yi
    conv_cols = gdn_conv_w.shape[2]

    big = [w_in[0], w_proj_gdn[0], w_proj_fox[0], w_out[0], w_up[0], w_down[0]]
    g_in, g_pa, g_pb, g_wo, g_wu, g_wd = gather_weights([w.astype(BF16) for w in big])
    w_main, w_small = _split_w_in(g_in.transpose(1, 0, 2).reshape(dm, -1))
    p_a, p_b, w_o = (g.reshape(-1, dm) for g in (g_pa, g_pb, g_wo))
    w_u = g_wu.transpose(1, 0, 2).reshape(dm, -1)
    w_d = g_wd.reshape(-1, dm)
    conv_slot = jnp.zeros((4, 4, conv_cols), F32).at[:, chip].set(jnp.where(ci == 0, gdn_conv_w[0], 0.0))
    conv_full = all_reduce_small("gather_conv", conv_slot.reshape(-1, LANES)).reshape(4, 4 * conv_cols)
    p1 = _lanes(gdn_dt_bias[0]) + _lanes(fox_f_bias[0], 2 * NH)
    p2 = _lanes(gdn_a_log[0])

    g = local_step(x.reshape(rows, dm), loss_target.reshape(rows, dm), norm_mix_g, norm_mlp_g, gdn_norm_g, fox_q_norm_g,
                   fox_k_norm_g, p1, p2, conv_full, w_main, w_small, p_a, p_b, w_o, w_u, w_d, nseq, seq)

    small_parts = [g["loss_acc"], g["g1"].reshape(8, LANES), g["g2"].reshape(8, LANES), g["gdn_ng"], g["qn"], g["kn"], g["p1"], g["p2"],
                   g["conv"].reshape(-1, LANES)]
    tiled = [jnp.pad(p, ((0, -p.shape[0] % 8), (0, 0))) for p in small_parts]
    red = all_reduce_small("reduce_small", jnp.concatenate(tiled, axis=0))
    pos, red_parts = 0, []
    for p, t in zip(small_parts, tiled):
        red_parts.append(red[pos:pos + p.shape[0]])
        pos += t.shape[0]
    r_loss, r_g1, r_g2, r_gdn_ng, r_qn, r_kn, r_p1, r_p2, r_conv = red_parts
    loss = jnp.sum(r_loss)
    g_conv = lax.dynamic_slice_in_dim(r_conv.reshape(4, 4, conv_cols), chip, 1, axis=1).reshape(4, conv_cols)
    small_grads = [r_g1.reshape(1, dm), r_p2[:, :NH], r_p1[:, :NH], r_gdn_ng, r_qn, r_kn, r_p1[:, 2 * NH:3 * NH], r_g2.reshape(1, dm)]
    small_w = [norm_mix_g, gdn_a_log, gdn_dt_bias, gdn_norm_g, fox_q_norm_g, fox_k_norm_g, fox_f_bias, norm_mlp_g]
    small_m = [m_norm_mix_g, m_gdn_a_log, m_gdn_dt_bias, m_gdn_norm_g, m_fox_q_norm_g, m_fox_k_norm_g, m_fox_f_bias, m_norm_mlp_g]
    small_v = [v_norm_mix_g, v_gdn_a_log, v_gdn_dt_bias, v_gdn_norm_g, v_fox_q_norm_g, v_fox_k_norm_g, v_fox_f_bias, v_norm_mlp_g]

    def pack(parts):
        flat = jnp.concatenate([jnp.pad(p.reshape(-1), (0, -p.size % LANES)) for p in parts])
        return jnp.pad(flat, (0, -flat.size % (8 * LANES))).reshape(-1, LANES)

    packed = adamw("adamw_small", pack(small_w + [gdn_conv_w[0]]), pack(small_grads + [g_conv]),
                   pack(small_m + [m_gdn_conv_w[0]]), pack(small_v + [v_gdn_conv_w[0]]))

    def unpack(flat2d):
        flat, pos, res = flat2d.reshape(-1), 0, []
        for p in small_w + [gdn_conv_w[0]]:
            res.append(flat[pos:pos + p.size].reshape(p.shape))
            pos += p.size + (-p.size % LANES)
        return res

    s_delta, s_m, s_v = (unpack(a) for a in packed)

    dw_in = _join_w_in(g["w_main"], g["w_small"])
    blocks = [dw_in.reshape(dm, 4, -1).transpose(1, 0, 2), g["p_a"].reshape(4, -1, dm), g["p_b"].reshape(4, -1, dm),
              g["w_o"].reshape(4, -1, dm), g["w_u"].reshape(dm, 4, -1).transpose(1, 0, 2), g["w_d"].reshape(4, -1, dm)]
    swapped = pair_swap(blocks)
    chip_part = add_pair("add_pair", blocks, swapped, ci.reshape(1).astype(jnp.int32))
    slots = chip_exchange(chip_part)
    halves = add_n("add_chips", slots)
    red_in, red_pa, red_pb, red_wo, red_wu, red_wd = pair_gather(halves)
    big_g = [red_in, red_pa, red_pb, red_wo, red_wu, red_wd]
    big_m = [m_w_in[0], m_w_proj_gdn[0], m_w_proj_fox[0], m_w_out[0], m_w_up[0], m_w_down[0]]
    big_v = [v_w_in[0], v_w_proj_gdn[0], v_w_proj_fox[0], v_w_out[0], v_w_up[0], v_w_down[0]]
    names = ["w_in", "w_proj_gdn", "w_proj_fox", "w_out", "w_up", "w_down"]
    big_res = {nm: adamw(f"adamw_{nm}", w, gr, m, v) for nm, w, gr, m, v in zip(names, big, big_g, big_m, big_v)}
    big_grad = dict(zip(names, big_g))

    order = ["norm_mix_g", "w_in", "gdn_conv_w", "gdn_a_log", "gdn_dt_bias", "gdn_norm_g", "fox_q_norm_g", "fox_k_norm_g",
             "fox_f_bias", "w_proj_gdn", "w_proj_fox", "w_out", "norm_mlp_g", "w_up", "w_down"]
    small_names = ["norm_mix_g", "gdn_a_log", "gdn_dt_bias", "gdn_norm_g", "fox_q_norm_g", "fox_k_norm_g", "fox_f_bias", "norm_mlp_g",
                   "gdn_conv_w"]
    small_idx = {nm: i for i, nm in enumerate(small_names)}
    shapes = dict(zip(order, (a.shape for a in (norm_mix_g, w_in, gdn_conv_w, gdn_a_log, gdn_dt_bias, gdn_norm_g, fox_q_norm_g,
                                                 fox_k_norm_g, fox_f_bias, w_proj_gdn, w_proj_fox, w_out, norm_mlp_g, w_up, w_down))))
    grads_out, delta_out, m_out, v_out = [], [], [], []
    for nm in order:
        if nm in big_res:
            d, mm, vv = big_res[nm]
            gr = big_grad[nm]
        else:
            i = small_idx[nm]
            gr = (small_grads + [g_conv])[i]
            d, mm, vv = s_delta[i], s_m[i], s_v[i]
        for lst, val in ((grads_out, gr), (delta_out, d), (m_out, mm), (v_out, vv)):
            lst.append(val.reshape(shapes[nm]))
    return (loss, g["dx"].reshape(x.shape), *grads_out, *delta_out, *m_out, *v_out)
```

```python
import functools

import jax
import jax.numpy as jnp
from jax import lax
from jax.experimental import pallas as pl
from jax.experimental.pallas import tpu as pltpu

F32 = jnp.float32
BF16 = jnp.bfloat16
LANES = 128
NH = 8
EPS = 1e-6
GDN_CHUNK = 64
GDN_ROWS = 256
GDN_BASE = 16
ROW_TILE = 512
CONV_HEADS = 2
ATT_TILE = 512
NEG = -1e30
VMEM_LIMIT_BYTES = 48 * 1024 * 1024
HI = lax.Precision.HIGHEST
LO = lax.Precision.DEFAULT
MESH = pl.DeviceIdType.MESH
ANY = pl.BlockSpec(memory_space=pl.ANY)

ADAM_LR, ADAM_B1, ADAM_B2, ADAM_EPS, ADAM_WD, ADAM_STEP = 0.001, 0.9, 0.999, 1e-08, 0.01, 10


def _params(n_grid):
    return pltpu.CompilerParams(dimension_semantics=("arbitrary",) * n_grid,
                                vmem_limit_bytes=VMEM_LIMIT_BYTES)


def _dot(a, b, dims, precision=None):
    dn = {"nn": (((1,), (0,)), ((), ())), "nt": (((1,), (1,)), ((), ())), "tn": (((0,), (0,)), ((), ()))}[dims]
    return lax.dot_general(a, b, dn, precision=precision, preferred_element_type=F32)


def _iota(shape, dim):
    return lax.broadcasted_iota(jnp.int32, shape, dim)


def _split(x, parts):
    out = []
    for _ in range(parts - 1):
        hi = x.astype(BF16)
        out.append(hi)
        x = x - hi.astype(F32)
    return out + [x.astype(BF16)]


def _dot_split(a, b, dims, a_exact=False):
    if a_exact:
        a16 = a.astype(BF16)
        b1, b2, b3 = _split(b, 3)
        return _dot(a16, b1, dims) + (_dot(a16, b2, dims) + _dot(a16, b3, dims))
    (ah, al), (bh, bl) = _split(a, 2), _split(b, 2)
    return _dot(ah, bh, dims) + (_dot(ah, bl, dims) + _dot(al, bh, dims))


@jax.custom_vjp
def mm_split(a, b):
    return _dot_split(a, b, "nn")


mm_split.defvjp(lambda a, b: (_dot_split(a, b, "nn"), (a, b)),
                lambda res, g: (_dot_split(g, res[1], "nt"), _dot_split(res[0], g, "tn")))


@jax.custom_vjp
def mm_mask(mask, b):
    return _dot_split(mask, b, "nn", True)


mm_mask.defvjp(lambda mask, b: (_dot_split(mask, b, "nn", True), mask),
               lambda mask, g: (jnp.zeros_like(mask), _dot_split(mask, g, "tn", True)))


def matmul(name, a, b, dims, out_dtype, add=None, tm=1024, tn=1024, tk=512):
    if dims == "nn":
        (m, k), (_, n) = a.shape, b.shape
    elif dims == "nt":
        (m, k), (n, _) = a.shape, b.shape
    else:
        (k, m), (_, n) = a.shape, b.shape
    if k <= 1024:
        tk = k
    tm, tn, tk = min(tm, m), min(tn, n), min(tk, k)
    assert m % tm == 0 and n % tn == 0 and k % tk == 0, (name, m, n, k)
    nk = k // tk
    a_spec = pl.BlockSpec((tk, tm), lambda i, j, kk: (kk, i)) if dims == "tn" else pl.BlockSpec((tm, tk), lambda i, j, kk: (i, kk))
    b_spec = pl.BlockSpec((tn, tk), lambda i, j, kk: (j, kk)) if dims == "nt" else pl.BlockSpec((tk, tn), lambda i, j, kk: (kk, j))
    o_spec = pl.BlockSpec((tm, tn), lambda i, j, kk: (i, j))
    has_add = add is not None

    def body(*refs):
        a_ref, b_ref = refs[0], refs[1]
        add_ref = refs[2] if has_add else None
        o_ref = refs[3] if has_add else refs[2]

        def finish(r):
            if has_add:
                r = r + add_ref[...]
            o_ref[...] = r.astype(o_ref.dtype)

        if nk == 1:
            finish(_dot(a_ref[...], b_ref[...], dims))
            return
        acc_ref = refs[-1]
        kk = pl.program_id(2)

        @pl.when(kk == 0)
        def _():
            acc_ref[...] = jnp.zeros_like(acc_ref)

        acc_ref[...] += _dot(a_ref[...], b_ref[...], dims)

        @pl.when(kk == nk - 1)
        def _():
            finish(acc_ref[...])

    ins = [a, b] + ([add] if has_add else [])
    in_specs = [a_spec, b_spec] + ([o_spec] if has_add else [])
    return pl.pallas_call(
        body, name=name, grid=(m // tm, n // tn, nk), in_specs=in_specs, out_specs=o_spec,
        out_shape=jax.ShapeDtypeStruct((m, n), out_dtype),
        scratch_shapes=[pltpu.VMEM((tm, tn), F32)] if nk > 1 else [], compiler_params=_params(3),
    )(*ins)


def _ew_spec(kind, off, width, tb, hp, order, shape=None):
    def ih(g0, g1):
        return (g0, g1) if order == "ih" else (g1, g0)

    assert off % hp == 0 or kind in ("row", "par")
    if kind == "row":
        return pl.BlockSpec((tb, width), lambda g0, g1: (ih(g0, g1)[0], off))
    if kind == "rowh":
        return pl.BlockSpec((tb, hp * width), lambda g0, g1: (ih(g0, g1)[0], ih(g0, g1)[1] + off // hp))
    if kind == "par":
        return pl.BlockSpec(shape, lambda g0, g1: (0, 0))
    if kind == "parh":
        return pl.BlockSpec((shape[0], hp * width), lambda g0, g1: (0, ih(g0, g1)[1] + off // hp))
    raise ValueError(kind)


def _ew_grid(rows, tb, nh, hp, order):
    assert nh % hp == 0 and rows % tb == 0
    return (rows // tb, nh // hp) if order == "ih" else (nh // hp, rows // tb)


def _ew_load(ref, kind, width, hh):
    if kind in ("row", "par"):
        return ref[...].astype(F32)
    return ref[:, hh * width:(hh + 1) * width].astype(F32)


def ew_fwd(name, f, ins, outs, rows, nh=1, tb=ROW_TILE, order="ih", hp=None):
    hp = nh if hp is None else hp
    n_in = len(ins)

    def body(*refs):
        hb = pl.program_id(1) if order == "ih" else pl.program_id(0)
        for hh in range(hp):
            h = hh if hp == nh else hb * hp + hh
            vals = [_ew_load(r, kd, w, hh) for r, (_, kd, _, w) in zip(refs[:n_in], ins)]
            res = f(h, *vals)
            for r, v, (_, kd, w, _) in zip(refs[n_in:], res, outs):
                if kd == "row":
                    assert hp == 1
                    r[...] = v.astype(r.dtype)
                else:
                    r[:, hh * w:(hh + 1) * w] = v.astype(r.dtype)

    in_specs = [_ew_spec(kd, off, w, tb, hp, order, a.shape) for (a, kd, off, w) in ins]
    out_specs = [_ew_spec(kd, 0, w, tb, hp, order) for (_, kd, w, _) in outs]
    out_shape = [jax.ShapeDtypeStruct((rows, tw), dt) for (tw, _, _, dt) in outs]
    return pl.pallas_call(
        body, name=name, grid=_ew_grid(rows, tb, nh, hp, order), in_specs=in_specs, out_specs=out_specs,
        out_shape=out_shape, compiler_params=_params(2),
    )(*[a for (a, _, _, _) in ins])


def ew_bwd(name, f, ins, cts, extras, emit, outs, rows, nh=1, tb=ROW_TILE, order="ih", hp=None):
    hp = nh if hp is None else hp
    n_in = len(ins)
    flat_cts = [d for group in cts for d in group]
    n_ct, n_ex = len(flat_cts), len(extras)

    def body(*refs):
        g0, g1 = pl.program_id(0), pl.program_id(1)
        hb = g1 if order == "ih" else g0
        out_refs = refs[n_in + n_ct + n_ex:]
        shared = [None] * len(outs)

        def store(r, v, first, sl=None):
            def put(val, add):
                if sl is None:
                    r[...] = (r[...] + val if add else val).astype(r.dtype)
                else:
                    r[:, sl] = (r[:, sl] + val if add else val).astype(r.dtype)

            if first is None:
                put(v, False)
            else:
                pl.when(first)(lambda: put(v, False))
                pl.when(jnp.logical_not(first))(lambda: put(v, True))

        for hh in range(hp):
            h = hh if hp == nh else hb * hp + hh
            vals = [_ew_load(r, kd, w, hh) for r, (_, kd, _, w) in zip(refs[:n_in], ins)]
            ct_refs = list(zip(refs[n_in:n_in + n_ct], flat_cts))
            ct_vals, pos = [], 0
            for group in cts:
                v = None
                for r, (_, kd, _, w) in ct_refs[pos:pos + len(group)]:
                    t = _ew_load(r, kd, w, hh)
                    v = t if v is None else v + t
                pos += len(group)
                ct_vals.append(v)
            ex_vals = [_ew_load(r, kd, w, hh) for r, (_, kd, _, w) in zip(refs[n_in + n_ct:n_in + n_ct + n_ex], extras)]
            _, vjp = jax.vjp(lambda *a: f(h, *a), *vals)
            res = emit(vjp(tuple(ct_vals)), ex_vals)
            for idx, (r, v, (_, kd, w, _, acc)) in enumerate(zip(out_refs, res, outs)):
                if kd in ("row", "par"):
                    shared[idx] = v if shared[idx] is None else shared[idx] + v
                else:
                    store(r, v, (g1 == 0) if acc == "inner" else None, slice(hh * w, (hh + 1) * w))
        for idx, (r, (_, kd, _, _, acc)) in enumerate(zip(out_refs, outs)):
            if kd in ("row", "par"):
                assert acc == "all" or hp == nh
                store(r, shared[idx], jnp.logical_and(g0 == 0, g1 == 0) if acc == "all" else None)

    operands = list(ins) + flat_cts + list(extras)
    in_specs = [_ew_spec(kd, off, w, tb, hp, order, a.shape) for (a, kd, off, w) in operands]
    out_specs = [_ew_spec(kd, 0, w, tb, hp, order, shp) for (shp, kd, w, _, _) in outs]
    out_shape = [jax.ShapeDtypeStruct(shp, dt) for (shp, _, _, dt, _) in outs]
    return pl.pallas_call(
        body, name=name, grid=_ew_grid(rows, tb, nh, hp, order), in_specs=in_specs, out_specs=out_specs,
        out_shape=out_shape, compiler_params=_params(2),
    )(*[a for (a, _, _, _) in operands])


def f_rms(h, x, g):
    r = lax.rsqrt(jnp.mean(x * x, axis=-1, keepdims=True) + EPS)
    return (x * r * g,)


def _softplus(z):
    return jnp.maximum(z, 0.0) + jnp.log1p(jnp.exp(-jnp.abs(z)))


def f_small(h, sp, p1, p2):
    lane = _iota(sp.shape, 1)
    z = sp + p1
    g = -jnp.exp(p2) * _softplus(z)
    beta = jax.nn.sigmoid(z)
    logf = -_softplus(-z)
    return (jnp.where(lane < NH, g, jnp.where(lane < 2 * NH, beta, jnp.where(lane < 3 * NH, logf, 0.0))),)


def _pick(x, lane_id):
    lane = _iota(x.shape, 1)
    col = jnp.sum(jnp.where(lane == lane_id, x, 0.0), axis=1, keepdims=True)
    return jnp.broadcast_to(col, x.shape)


def f_bcast(h, so, cs):
    return _pick(so, h), _pick(so, h + NH), _pick(cs, h + 2 * NH)


def _shift_down(s):
    def down(x):
        return jnp.where(_iota(x.shape, 0) >= s, pltpu.roll(x, s, 0), 0.0)

    def up(g):
        n = g.shape[0]
        return jnp.where(_iota(g.shape, 0) < n - s, pltpu.roll(g, n - s, 0), 0.0)

    @jax.custom_vjp
    def shift(x):
        return down(x)

    shift.defvjp(lambda x: (down(x), None), lambda _, g: (up(g),))
    return shift


def _silu(x):
    return x * jax.nn.sigmoid(x)


def make_f_conv(mode):
    sh1, sh2, sh3 = _shift_down(1), _shift_down(2), _shift_down(3)

    def f(h, x, w):
        sub = _iota(w.shape, 0)

        def tap(i):
            return jnp.sum(jnp.where(sub == i, w, 0.0), axis=0, keepdims=True)

        y = sh3(x) * tap(0)
        y = y + sh2(x) * tap(1)
        y = y + sh1(x) * tap(2)
        y = y + x * tap(3)
        s = _silu(y)
        if mode == "v":
            return (s,)
        n = s * lax.rsqrt(jnp.sum(s * s, axis=-1, keepdims=True) + EPS)
        if mode == "q":
            n = n * (LANES ** -0.5)
        return (n,)

    return f


def f_post(h, o, z, g):
    r = lax.rsqrt(jnp.mean(o * o, axis=-1, keepdims=True) + EPS)
    return (o * r * g * _silu(z),)


def f_merge(h, ga, gb, ya, yb):
    return (jax.nn.sigmoid(ga) * ya + jax.nn.sigmoid(gb) * yb,)


def f_relu2(h, a):
    m = jnp.maximum(a, 0.0)
    return (m * m,)


def f_delta(h, do, o):
    return (jnp.broadcast_to(jnp.sum(do * o, axis=1, keepdims=True), o.shape),)


def cumsum_time(name, x, nseq, seq, reverse):
    nb = seq // LANES

    def body(x_ref, o_ref):
        r, c = _iota((LANES, LANES), 0), _iota((LANES, LANES), 1)
        tri = jnp.where((r <= c) if reverse else (r >= c), 1.0, 0.0).astype(F32)
        carry = jnp.zeros((1, LANES), F32)
        for b in (range(nb - 1, -1, -1) if reverse else range(nb)):
            blk = x_ref[b * LANES:(b + 1) * LANES, :]
            o_ref[b * LANES:(b + 1) * LANES, :] = _dot_split(tri, blk, "nn", True) + carry
            carry = carry + jnp.sum(blk, axis=0, keepdims=True)

    spec = pl.BlockSpec((seq, LANES), lambda s: (s, 0))
    return pl.pallas_call(body, name=name, grid=(nseq,), in_specs=[spec], out_specs=spec,
                          out_shape=jax.ShapeDtypeStruct(x.shape, F32), compiler_params=_params(1))(x)


def transpose_time(name, x, nseq, seq):
    def body(x_ref, o_ref):
        o_ref[...] = x_ref[...].T

    return pl.pallas_call(
        body, name=name, grid=(nseq,), in_specs=[pl.BlockSpec((seq, LANES), lambda s: (s, 0))],
        out_specs=pl.BlockSpec((LANES, seq), lambda s: (s, 0)),
        out_shape=jax.ShapeDtypeStruct((nseq * LANES, seq), F32), compiler_params=_params(1))(x)


def _gdn_masks():
    n = GDN_ROWS
    r, c = _iota((n, n), 0), _iota((n, n), 1)
    shift = GDN_CHUNK.bit_length() - 1
    same = lax.shift_right_logical(r, shift) == lax.shift_right_logical(c, shift)
    return r, c, same


def _gdn_decay(gb):
    r, c, same = _gdn_masks()
    seg_tril = jnp.where(jnp.logical_and(same, r >= c), 1.0, 0.0).astype(F32)
    g_cum = mm_mask(seg_tril, gb)
    lane0 = _iota(g_cum.shape, 1) == 0
    g_col = jnp.sum(jnp.where(lane0, g_cum, 0.0), axis=1, keepdims=True)
    g_row = jnp.sum(jnp.where(r == c, jnp.broadcast_to(g_col, (GDN_ROWS, GDN_ROWS)), 0.0), axis=0, keepdims=True)
    return g_cum, g_col - g_row


def gdn_f1(q, k, gb, bb):
    r, c, same = _gdn_masks()
    strict = jnp.logical_and(same, r > c)
    _, diff = _gdn_decay(gb)
    lane0 = _iota(bb.shape, 1) == 0
    beta_col = jnp.sum(jnp.where(lane0, bb, 0.0), axis=1, keepdims=True)
    kk = _dot(k, k, "nt", LO)
    return jnp.where(strict, beta_col * kk * jnp.exp(jnp.where(strict, diff, 0.0)), 0.0)


def gdn_f2(t_inv, q, k, v, gb, bb):
    r, c, same = _gdn_masks()
    incl = jnp.logical_and(same, r >= c)
    g_cum, diff = _gdn_decay(gb)
    decay = jnp.where(incl, jnp.exp(jnp.where(incl, diff, 0.0)), 0.0)
    e_g = jnp.exp(g_cum)
    value = mm_split(t_inv, v * bb)
    k_cum = mm_split(t_inv, k * bb * e_g)
    attn = _dot(q, k, "nt", LO) * decay
    g_last = mm_mask(jnp.where(same, 1.0, 0.0).astype(F32), gb)
    return value, k_cum, attn, q * e_g, k * jnp.exp(g_last - g_cum)


def tri_inverse(a):
    n = GDN_ROWS
    r, c = _iota((n, n), 0), _iota((n, n), 1)
    shift = GDN_BASE.bit_length() - 1
    blk = lax.shift_right_logical(r, shift) == lax.shift_right_logical(c, shift)
    eye = jnp.where(r == c, 1.0, 0.0).astype(F32)
    d = jnp.where(blk, a, 0.0)
    lo = a - d
    p = -d
    t_d = eye + p
    steps = shift - 1
    for _ in range(steps):
        p = _dot_split(p, p, "nn")
        t_d = t_d + _dot_split(t_d, p, "nn")
    assert GDN_CHUNK // GDN_BASE == 4
    nmat = _dot_split(t_d, lo, "nn")
    n2 = _dot_split(nmat, nmat, "nn")
    t_n = (eye - nmat) + _dot_split(eye - nmat, n2, "nn")
    return _dot_split(t_n, t_d, "nn")


def gdn_a_fwd(q, k, v, gb, bb, rows):
    blk = pl.BlockSpec((GDN_ROWS, LANES), lambda i, h: (i, h))
    sq = pl.BlockSpec((GDN_ROWS, GDN_ROWS), lambda i, h: (i, h))

    def body(q_ref, k_ref, v_ref, gb_ref, bb_ref, val_ref, kc_ref, at_ref, qd_ref, kd_ref, t_ref):
        qv, kv, vv, gv, bv = q_ref[...], k_ref[...], v_ref[...], gb_ref[...], bb_ref[...]
        t_inv = tri_inverse(gdn_f1(qv, kv, gv, bv))
        value, k_cum, attn, q_dec, k_dec = gdn_f2(t_inv, qv, kv, vv, gv, bv)
        val_ref[...], kc_ref[...], at_ref[...], qd_ref[...], kd_ref[...], t_ref[...] = value, k_cum, attn, q_dec, k_dec, t_inv

    wide = jax.ShapeDtypeStruct((rows, NH * LANES), F32)
    square = jax.ShapeDtypeStruct((rows, NH * GDN_ROWS), F32)
    return pl.pallas_call(
        body, name="gdn_a_fwd", grid=(rows // GDN_ROWS, NH), in_specs=[blk] * 5,
        out_specs=[blk, blk, sq, blk, blk, sq], out_shape=[wide, wide, square, wide, wide, square],
        compiler_params=_params(2))(q, k, v, gb, bb)


def gdn_a_bwd(q, k, v, gb, bb, t_inv, dval, dkc, dat, dqd, dkd, dgb_b, rows):
    blk = pl.BlockSpec((GDN_ROWS, LANES), lambda i, h: (i, h))
    sq = pl.BlockSpec((GDN_ROWS, GDN_ROWS), lambda i, h: (i, h))

    def body(q_ref, k_ref, v_ref, gb_ref, bb_ref, t_ref, dval_ref, dkc_ref, dat_ref, dqd_ref, dkd_ref, dgbb_ref,
             dq_ref, dk_ref, dv_ref, dgb_ref, dbb_ref):
        qv, kv, vv, gv, bv, tv = q_ref[...], k_ref[...], v_ref[...], gb_ref[...], bb_ref[...], t_ref[...]
        _, vjp1 = jax.vjp(gdn_f1, qv, kv, gv, bv)
        _, vjp2 = jax.vjp(gdn_f2, tv, qv, kv, vv, gv, bv)
        dt, dq2, dk2, dv2, dgb2, dbb2 = vjp2((dval_ref[...], dkc_ref[...], dat_ref[...], dqd_ref[...], dkd_ref[...]))
        da = -_dot_split(tv, _dot_split(dt, tv, "nt"), "tn")
        dq1, dk1, dgb1, dbb1 = vjp1(da)
        dq_ref[...] = dq1 + dq2
        dk_ref[...] = dk1 + dk2
        dv_ref[...] = dv2
        dgb_ref[...] = dgb1 + dgb2 + dgbb_ref[...]
        dbb_ref[...] = dbb1 + dbb2

    wide = jax.ShapeDtypeStruct((rows, NH * LANES), F32)
    return pl.pallas_call(
        body, name="gdn_a_bwd", grid=(rows // GDN_ROWS, NH),
        in_specs=[blk] * 5 + [sq, blk, blk, sq, blk, blk, blk], out_specs=[blk] * 5, out_shape=[wide] * 5,
        compiler_params=_params(2))(q, k, v, gb, bb, t_inv, dval, dkc, dat, dqd, dkd, dgb_b)


N_CH = GDN_ROWS // GDN_CHUNK


def gdn_fb(*args):
    val, kc, at, qd, kd, gb = (args[i * N_CH:(i + 1) * N_CH] for i in range(6))
    s = args[6 * N_CH]
    outs = []
    zero = jnp.zeros((GDN_CHUNK, LANES), F32)
    for c in range(N_CH):
        v_new = val[c] - _dot(kc[c], s, "nn", LO)
        v_pad = jnp.concatenate([zero] * c + [v_new] + [zero] * (N_CH - 1 - c), axis=0)
        outs.append(_dot(qd[c], s, "nn", LO) + _dot(at[c], v_pad, "nn", LO))
        dec = jnp.exp(jnp.sum(gb[c], axis=0, keepdims=True))
        s = s * dec + _dot(kd[c], v_new, "tn", LO)
    return (*outs, s)


def _gdn_pieces(refs):
    return [r[c * GDN_CHUNK:(c + 1) * GDN_CHUNK, :] for r in refs for c in range(N_CH)]


def gdn_b_fwd(val, kc, at, qd, kd, gb, nseq, seq):
    nb = seq // GDN_ROWS
    rows = nseq * seq
    blk = pl.BlockSpec((GDN_ROWS, LANES), lambda s, h, j: (s * nb + j, h))
    sq = pl.BlockSpec((GDN_ROWS, GDN_ROWS), lambda s, h, j: (s * nb + j, h))
    snap = pl.BlockSpec((LANES, LANES), lambda s, h, j: ((s * NH + h) * nb + j, 0))

    def body(val_ref, kc_ref, at_ref, qd_ref, kd_ref, gb_ref, o_ref, snap_ref, s_ref):
        @pl.when(pl.program_id(2) == 0)
        def _():
            s_ref[...] = jnp.zeros_like(s_ref)

        s_in = s_ref[...]
        snap_ref[...] = s_in
        res = gdn_fb(*_gdn_pieces([val_ref, kc_ref, at_ref, qd_ref, kd_ref, gb_ref]), s_in)
        for c in range(N_CH):
            o_ref[c * GDN_CHUNK:(c + 1) * GDN_CHUNK, :] = res[c]
        s_ref[...] = res[N_CH]

    return pl.pallas_call(
        body, name="gdn_b_fwd", grid=(nseq, NH, nb), in_specs=[blk, blk, sq, blk, blk, blk], out_specs=[blk, snap],
        out_shape=[jax.ShapeDtypeStruct((rows, NH * LANES), F32), jax.ShapeDtypeStruct((nseq * NH * nb * LANES, LANES), F32)],
        scratch_shapes=[pltpu.VMEM((LANES, LANES), F32)], compiler_params=_params(3))(val, kc, at, qd, kd, gb)


def gdn_b_bwd(val, kc, at, qd, kd, gb, snaps, do, nseq, seq):
    nb = seq // GDN_ROWS
    rows = nseq * seq
    blk = pl.BlockSpec((GDN_ROWS, LANES), lambda s, h, j: (s * nb + nb - 1 - j, h))
    sq = pl.BlockSpec((GDN_ROWS, GDN_ROWS), lambda s, h, j: (s * nb + nb - 1 - j, h))
    snap = pl.BlockSpec((LANES, LANES), lambda s, h, j: ((s * NH + h) * nb + nb - 1 - j, 0))

    def body(val_ref, kc_ref, at_ref, qd_ref, kd_ref, gb_ref, snap_ref, do_ref,
             dval_ref, dkc_ref, dat_ref, dqd_ref, dkd_ref, dgb_ref, ds_ref):
        @pl.when(pl.program_id(2) == 0)
        def _():
            ds_ref[...] = jnp.zeros_like(ds_ref)

        _, vjp = jax.vjp(gdn_fb, *_gdn_pieces([val_ref, kc_ref, at_ref, qd_ref, kd_ref, gb_ref]), snap_ref[...])
        grads = vjp((*_gdn_pieces([do_ref]), ds_ref[...]))
        for i, r in enumerate([dval_ref, dkc_ref, dat_ref, dqd_ref, dkd_ref, dgb_ref]):
            for c in range(N_CH):
                r[c * GDN_CHUNK:(c + 1) * GDN_CHUNK, :] = grads[i * N_CH + c]
        ds_ref[...] = grads[6 * N_CH]

    wide = jax.ShapeDtypeStruct((rows, NH * LANES), F32)
    square = jax.ShapeDtypeStruct((rows, NH * GDN_ROWS), F32)
    return pl.pallas_call(
        body, name="gdn_b_bwd", grid=(nseq, NH, nb), in_specs=[blk, blk, sq, blk, blk, blk, snap, blk],
        out_specs=[blk, blk, sq, blk, blk, blk], out_shape=[wide, wide, square, wide, wide, wide],
        scratch_shapes=[pltpu.VMEM((LANES, LANES), F32)], compiler_params=_params(3))(val, kc, at, qd, kd, gb, snaps, do)


FOX_Q, FOX_K, FOX_V = 4 * NH, 5 * NH, 6 * NH
FOX_SCALE = LANES ** -0.5


def _fox_scores(q, k, cq, ck, row0, col0):
    s = _dot(q, k, "nt") * FOX_SCALE + cq - ck
    rows = row0 + _iota(s.shape, 0)
    cols = col0 + _iota(s.shape, 1)
    return s, rows >= cols


def _head_row(ct_ref, h, off, width):
    blk = ct_ref[:, pl.ds(off, width)]
    return jnp.sum(jnp.where(_iota(blk.shape, 0) == h, blk, 0.0), axis=0, keepdims=True)


def _col(x):
    return jnp.max(x, axis=1, keepdims=True)


def fox_fwd(qn, kn, proj, cb, ct, nseq, seq):
    tq = tk = min(ATT_TILE, seq)
    nq = seq // tq
    rows = nseq * seq
    qblk = pl.BlockSpec((tq, LANES), lambda s, h, i: (s * nq + i, h))
    full = pl.BlockSpec((seq, LANES), lambda s, h, i: (s, h))
    vfull = pl.BlockSpec((seq, LANES), lambda s, h, i: (s, h + FOX_V))
    ctb = pl.BlockSpec((NH, seq), lambda s, h, i: (s * (LANES // NH) + 2, 0))

    def body(q_ref, k_ref, v_ref, cb_ref, ct_ref, o_ref, o16_ref, lse_ref):
        h, i = pl.program_id(1), pl.program_id(2)
        q = q_ref[...]
        cq = _col(cb_ref[...])

        def step(j, carry):
            m, l, acc = carry
            off = pl.multiple_of(j * tk, tk)
            s, mask = _fox_scores(q, k_ref[pl.ds(off, tk), :], cq, _head_row(ct_ref, h, off, tk), i * tq, j * tk)
            s = jnp.where(mask, s, NEG)
            m_new = jnp.maximum(m, jnp.max(s, axis=1, keepdims=True))
            p = jnp.exp(s - m_new)
            alpha = jnp.exp(m - m_new)
            l = alpha * l + jnp.sum(p, axis=1, keepdims=True)
            acc = alpha * acc + _dot(p.astype(BF16), v_ref[pl.ds(off, tk), :].astype(BF16), "nn")
            return m_new, l, acc

        init = (jnp.full((tq, 1), NEG, F32), jnp.zeros((tq, 1), F32), jnp.zeros((tq, LANES), F32))
        m, l, acc = lax.fori_loop(0, i + 1, step, init)
        o = acc / l
        o_ref[...] = o
        o16_ref[...] = o.astype(BF16)
        lse_ref[...] = jnp.broadcast_to(m + jnp.log(l), (tq, LANES))

    wide = (rows, NH * LANES)
    return pl.pallas_call(
        body, name="fox_fwd", grid=(nseq, NH, nq), in_specs=[qblk, full, vfull, qblk, ctb], out_specs=[qblk] * 3,
        out_shape=[jax.ShapeDtypeStruct(wide, F32), jax.ShapeDtypeStruct(wide, BF16), jax.ShapeDtypeStruct(wide, F32)],
        compiler_params=_params(3))(qn, kn, proj, cb, ct)


def fox_dq(qn, kn, proj, cb, ct, do, lse, delta, nseq, seq):
    tq = tk = min(ATT_TILE, seq)
    nq = seq // tq
    rows = nseq * seq
    qblk = pl.BlockSpec((tq, LANES), lambda s, h, i: (s * nq + i, h))
    full = pl.BlockSpec((seq, LANES), lambda s, h, i: (s, h))
    vfull = pl.BlockSpec((seq, LANES), lambda s, h, i: (s, h + FOX_V))
    ctb = pl.BlockSpec((NH, seq), lambda s, h, i: (s * (LANES // NH) + 2, 0))

    def body(q_ref, k_ref, v_ref, cb_ref, ct_ref, do_ref, lse_ref, dl_ref, dq_ref, dc_ref):
        h, i = pl.program_id(1), pl.program_id(2)
        q = q_ref[...]
        cq, lse, delta = _col(cb_ref[...]), _col(lse_ref[...]), _col(dl_ref[...])
        do16 = do_ref[...].astype(BF16)

        def step(j, carry):
            dq, dc = carry
            off = pl.multiple_of(j * tk, tk)
            k = k_ref[pl.ds(off, tk), :]
            s, mask = _fox_scores(q, k, cq, _head_row(ct_ref, h, off, tk), i * tq, j * tk)
            p = jnp.where(mask, jnp.exp(s - lse), 0.0)
            dp = _dot(do16, v_ref[pl.ds(off, tk), :].astype(BF16), "nt")
            ds = p * (dp - delta)
            return dq + _dot(ds.astype(BF16), k, "nn"), dc + jnp.sum(ds, axis=1, keepdims=True)

        dq, dc = lax.fori_loop(0, i + 1, step, (jnp.zeros((tq, LANES), F32), jnp.zeros((tq, 1), F32)))
        dq_ref[...] = dq * FOX_SCALE
        dc_ref[...] = jnp.where(_iota((tq, LANES), 1) == 0, dc, 0.0)

    wide = jax.ShapeDtypeStruct((rows, NH * LANES), F32)
    return pl.pallas_call(
        body, name="fox_dq", grid=(nseq, NH, nq), in_specs=[qblk, full, vfull, qblk, ctb, qblk, qblk, qblk],
        out_specs=[qblk, qblk], out_shape=[wide, wide], compiler_params=_params(3))(qn, kn, proj, cb, ct, do, lse, delta)


def fox_dkv(qn, kn, proj, cb, ct, do, lse, delta, nseq, seq):
    tq = tk = min(ATT_TILE, seq)
    nq = seq // tq
    rows = nseq * seq
    kblk = pl.BlockSpec((tk, LANES), lambda s, h, j: (s * nq + j, h))
    vblk = pl.BlockSpec((tk, LANES), lambda s, h, j: (s * nq + j, h + FOX_V))
    full = pl.BlockSpec((seq, LANES), lambda s, h, j: (s, h))
    ctb = pl.BlockSpec((NH, seq), lambda s, h, j: (s * (LANES // NH) + 2, 0))

    def body(q_ref, k_ref, v_ref, cb_ref, ct_ref, do_ref, lse_ref, dl_ref, dk_ref, dv_ref, dc_ref):
        h, j = pl.program_id(1), pl.program_id(2)
        k = k_ref[...]
        v16 = v_ref[...].astype(BF16)
        ck = _head_row(ct_ref, h, pl.multiple_of(j * tk, tk), tk)
        e0 = jnp.where(_iota((tq, LANES), 1) == 0, 1.0, 0.0).astype(BF16)

        def step(i, carry):
            dk, dv, dc = carry
            off = pl.multiple_of(i * tq, tq)
            q = q_ref[pl.ds(off, tq), :]
            do16 = do_ref[pl.ds(off, tq), :].astype(BF16)
            cq, lse, delta = (_col(r[pl.ds(off, tq), :]) for r in (cb_ref, lse_ref, dl_ref))
            s, mask = _fox_scores(q, k, cq, ck, i * tq, j * tk)
            p = jnp.where(mask, jnp.exp(s - lse), 0.0)
            dv = dv + _dot(p.astype(BF16), do16, "tn")
            ds32 = p * (_dot(do16, v16, "nt") - delta)
            ds = ds32.astype(BF16)
            ds_lo = (ds32 - ds.astype(F32)).astype(BF16)
            return dk + _dot(ds, q, "tn"), dv, dc + (_dot(ds, e0, "tn") + _dot(ds_lo, e0, "tn"))

        zero = jnp.zeros((tk, LANES), F32)
        dk, dv, dc = lax.fori_loop(j, nq, step, (zero, zero, zero))
        dk_ref[...] = dk * FOX_SCALE
        dv_ref[...] = dv.astype(BF16)
        dc_ref[...] = -dc

    wide = (rows, NH * LANES)
    return pl.pallas_call(
        body, name="fox_dkv", grid=(nseq, NH, nq), in_specs=[full, kblk, vblk, full, ctb, full, full, full],
        out_specs=[kblk, kblk, kblk],
        out_shape=[jax.ShapeDtypeStruct(wide, F32), jax.ShapeDtypeStruct(wide, BF16), jax.ShapeDtypeStruct(wide, F32)],
        compiler_params=_params(3))(qn, kn, proj, cb, ct, do, lse, delta)


def loss_head(out, tgt, rows, width):
    tb = ROW_TILE
    blk = pl.BlockSpec((tb, width), lambda i: (i, 0))
    accb = pl.BlockSpec((8, LANES), lambda i: (0, 0))

    def body(o_ref, t_ref, d32_ref, d16_ref, acc_ref):
        d = o_ref[...] - t_ref[...]
        row_loss = 0.5 * jnp.mean(d * d, axis=1, keepdims=True)
        g = d * (1.0 / width)
        d32_ref[...] = g
        d16_ref[...] = g.astype(BF16)
        part = jnp.where(_iota((tb, LANES), 1) == 0, row_loss, 0.0).reshape(tb // 8, 8, LANES).sum(axis=0)

        @pl.when(pl.program_id(0) == 0)
        def _():
            acc_ref[...] = part

        @pl.when(pl.program_id(0) != 0)
        def _():
            acc_ref[...] += part

    return pl.pallas_call(
        body, name="loss_head", grid=(rows // tb,), in_specs=[blk, blk], out_specs=[blk, blk, accb],
        out_shape=[jax.ShapeDtypeStruct((rows, width), F32), jax.ShapeDtypeStruct((rows, width), BF16),
                   jax.ShapeDtypeStruct((8, LANES), F32)], compiler_params=_params(1))(out, tgt)


def adamw(name, w, g, m, v):
    rows, cols = w.shape
    tb = min(rows, 128)
    assert rows % tb == 0
    blk = pl.BlockSpec((tb, cols), lambda i: (i, 0))

    def body(w_ref, g_ref, m_ref, v_ref, d_ref, mo_ref, vo_ref):
        gv = g_ref[...]
        m_new = ADAM_B1 * m_ref[...] + (1.0 - ADAM_B1) * gv
        v_new = ADAM_B2 * v_ref[...] + (1.0 - ADAM_B2) * (gv * gv)
        m_hat = m_new / (1.0 - ADAM_B1 ** ADAM_STEP)
        v_hat = v_new / (1.0 - ADAM_B2 ** ADAM_STEP)
        d_ref[...] = -ADAM_LR * (m_hat / (jnp.sqrt(v_hat) + ADAM_EPS) + ADAM_WD * w_ref[...])
        mo_ref[...] = m_new
        vo_ref[...] = v_new

    shp = jax.ShapeDtypeStruct(w.shape, F32)
    return pl.pallas_call(body, name=name, grid=(rows // tb,), in_specs=[blk] * 4, out_specs=[blk] * 3,
                          out_shape=[shp] * 3, compiler_params=_params(1))(w, g, m, v)


def add_n(name, xs):
    outs = []
    for idx, x in enumerate(xs):
        n, rows, cols = x.shape
        tb = min(rows, 128)
        assert rows % tb == 0

        def body(x_ref, o_ref, n=n):
            acc = x_ref[0].astype(F32)
            for t in range(1, n):
                acc = acc + x_ref[t].astype(F32)
            o_ref[...] = acc

        outs.append(pl.pallas_call(
            body, name=f"{name}_{idx}", grid=(rows // tb,), in_specs=[pl.BlockSpec((n, tb, cols), lambda i: (0, i, 0))],
            out_specs=pl.BlockSpec((tb, cols), lambda i: (i, 0)), out_shape=jax.ShapeDtypeStruct((rows, cols), F32),
            compiler_params=_params(1))(x))
    return outs


def add_pair(name, gs, rs, c):
    outs = []
    for idx, (g, r) in enumerate(zip(gs, rs)):
        nb, half, cols = r.shape
        tb = min(half, 128)
        steps = half // tb

        def body(c_ref, g_ref, r_ref, o_ref):
            o_ref[...] = (g_ref[...] + r_ref[...]).astype(BF16)

        grid_spec = pltpu.PrefetchScalarGridSpec(
            num_scalar_prefetch=1, grid=(nb, steps),
            in_specs=[pl.BlockSpec((None, tb, cols), lambda b, i, c_ref: (b, c_ref[0] * steps + i, 0)),
                      pl.BlockSpec((None, tb, cols), lambda b, i, c_ref: (b, i, 0))],
            out_specs=pl.BlockSpec((None, tb, cols), lambda b, i, c_ref: (b, i, 0)))
        outs.append(pl.pallas_call(
            body, name=f"{name}_{idx}", grid_spec=grid_spec, out_shape=jax.ShapeDtypeStruct(r.shape, BF16),
            compiler_params=_params(2))(c, g, r))
    return outs


def _place():
    x, y, c = lax.axis_index("x"), lax.axis_index("y"), lax.axis_index("c")
    return x, y, c, [(1 - x, y), (x, 1 - y), (1 - x, 1 - y)]


def _remote(src, dst, send_sem, recv_sem, dev):
    return pltpu.make_async_remote_copy(src_ref=src, dst_ref=dst, send_sem=send_sem, recv_sem=recv_sem,
                                        device_id=dev, device_id_type=MESH)


def gather_weights(shards):
    n = len(shards)

    def body(*refs):
        ins, outs = refs[:n], refs[n:2 * n]
        ici_s, ici_r, d2d_s, d2d_r, loc = refs[2 * n:]
        x, y, c, chips = _place()
        me = 2 * x + y
        local = [pltpu.make_async_copy(ins[w], outs[w].at[me], loc.at[w]) for w in range(n)]
        for cp in local:
            cp.start()
        sends, passes = [], []
        for w in range(n):
            half = ins[w].shape[0] // 2
            for j, (ox, oy) in enumerate(chips):
                mine = pl.ds(c * half, half)
                cp = _remote(ins[w].at[mine, :], outs[w].at[me, mine, :], ici_s.at[3 * w + j], ici_r.at[3 * w + j], (ox, oy, c))
                cp.start()
                sends.append(cp)
        for w in range(n):
            half = ins[w].shape[0] // 2
            for j, (ox, oy) in enumerate(chips):
                landed = outs[w].at[2 * ox + oy, pl.ds(c * half, half), :]
                _remote(landed, landed, ici_s.at[3 * w + j], ici_r.at[3 * w + j], (ox, oy, c)).wait_recv()
                cp = _remote(landed, landed, d2d_s.at[3 * w + j], d2d_r.at[3 * w + j], (x, y, 1 - c))
                cp.start()
                passes.append(cp)
        for w in range(n):
            half = ins[w].shape[0] // 2
            for j, (ox, oy) in enumerate(chips):
                other = outs[w].at[2 * ox + oy, pl.ds((1 - c) * half, half), :]
                _remote(other, other, d2d_s.at[3 * w + j], d2d_r.at[3 * w + j], (x, y, 1 - c)).wait_recv()
        for cp in sends + passes:
            cp.wait_send()
        for cp in local:
            cp.wait()

    return pl.pallas_call(
        body, name="gather_weights", in_specs=[ANY] * n, out_specs=[ANY] * n,
        out_shape=[jax.ShapeDtypeStruct((4,) + s.shape, s.dtype) for s in shards],
        scratch_shapes=[pltpu.SemaphoreType.DMA((3 * n,))] * 4 + [pltpu.SemaphoreType.DMA((n,))],
    )(*shards)


def pair_swap(grads):
    n = len(grads)

    def body(*refs):
        ins, outs = refs[:n], refs[n:2 * n]
        send, recv = refs[2 * n:]
        x, y, c, _ = _place()
        cps = []
        for w in range(n):
            half = ins[w].shape[1] // 2
            cp = _remote(ins[w].at[:, pl.ds((1 - c) * half, half), :], outs[w], send.at[w], recv.at[w], (x, y, 1 - c))
            cp.start()
            cps.append(cp)
        for cp in cps:
            cp.wait_recv()
        for cp in cps:
            cp.wait_send()

    return pl.pallas_call(
        body, name="pair_swap", in_specs=[ANY] * n, out_specs=[ANY] * n,
        out_shape=[jax.ShapeDtypeStruct((4, g.shape[1] // 2, g.shape[2]), g.dtype) for g in grads],
        scratch_shapes=[pltpu.SemaphoreType.DMA((n,))] * 2,
    )(*grads)


def chip_exchange(parts):
    n = len(parts)

    def body(*refs):
        ins, outs = refs[:n], refs[n:2 * n]
        send, recv, loc = refs[2 * n:]
        x, y, c, chips = _place()
        me = 2 * x + y
        local = [pltpu.make_async_copy(ins[w].at[me], outs[w].at[me], loc.at[w]) for w in range(n)]
        for cp in local:
            cp.start()
        cps = []
        for w in range(n):
            for j, (ox, oy) in enumerate(chips):
                cp = _remote(ins[w].at[2 * ox + oy], outs[w].at[me], send.at[3 * w + j], recv.at[3 * w + j], (ox, oy, c))
                cp.start()
                cps.append(cp)
        for w in range(n):
            for j, (ox, oy) in enumerate(chips):
                slot = outs[w].at[2 * ox + oy]
                _remote(slot, slot, send.at[3 * w + j], recv.at[3 * w + j], (ox, oy, c)).wait_recv()
        for cp in cps:
            cp.wait_send()
        for cp in local:
            cp.wait()

    return pl.pallas_call(
        body, name="chip_exchange", in_specs=[ANY] * n, out_specs=[ANY] * n,
        out_shape=[jax.ShapeDtypeStruct(p.shape, p.dtype) for p in parts],
        scratch_shapes=[pltpu.SemaphoreType.DMA((3 * n,))] * 2 + [pltpu.SemaphoreType.DMA((n,))],
    )(*parts)


def pair_gather(halves):
    n = len(halves)

    def body(*refs):
        ins, outs = refs[:n], refs[n:2 * n]
        send, recv, loc = refs[2 * n:]
        x, y, c, _ = _place()
        local, cps = [], []
        for w in range(n):
            half = ins[w].shape[0]
            mine = outs[w].at[pl.ds(c * half, half), :]
            cp = pltpu.make_async_copy(ins[w], mine, loc.at[w])
            cp.start()
            local.append(cp)
            cp = _remote(ins[w], mine, send.at[w], recv.at[w], (x, y, 1 - c))
            cp.start()
            cps.append(cp)
        for w in range(n):
            half = ins[w].shape[0]
            other = outs[w].at[pl.ds((1 - c) * half, half), :]
            _remote(other, other, send.at[w], recv.at[w], (x, y, 1 - c)).wait_recv()
        for cp in cps:
            cp.wait_send()
        for cp in local:
            cp.wait()

    return pl.pallas_call(
        body, name="pair_gather", in_specs=[ANY] * n, out_specs=[ANY] * n,
        out_shape=[jax.ShapeDtypeStruct((2 * h.shape[0], h.shape[1]), h.dtype) for h in halves],
        scratch_shapes=[pltpu.SemaphoreType.DMA((n,))] * 2 + [pltpu.SemaphoreType.DMA((n,))],
    )(*halves)


def all_reduce_small(name, vec):
    rows = vec.shape[0]

    def body(v_ref, o_ref, buf, send, recv):
        x, y, c, _ = _place()
        me = 4 * x + 2 * y + c
        buf[me] = v_ref[...]
        cps = []
        for k in range(1, 8):
            kx, ky, kc = (k >> 2) & 1, (k >> 1) & 1, k & 1
            peer = (x if kx == 0 else 1 - x, y if ky == 0 else 1 - y, c if kc == 0 else 1 - c)
            cp = _remote(v_ref, buf.at[me], send.at[k - 1], recv.at[k - 1], peer)
            cp.start()
            cps.append(cp)
        for k in range(1, 8):
            kx, ky, kc = (k >> 2) & 1, (k >> 1) & 1, k & 1
            px, py, pc = (x if kx == 0 else 1 - x, y if ky == 0 else 1 - y, c if kc == 0 else 1 - c)
            slot = buf.at[4 * px + 2 * py + pc]
            _remote(slot, slot, send.at[k - 1], recv.at[k - 1], (px, py, pc)).wait_recv()
        for cp in cps:
            cp.wait_send()
        acc = buf[0]
        for d in range(1, 8):
            acc = acc + buf[d]
        o_ref[...] = acc

    vm = pl.BlockSpec(memory_space=pltpu.VMEM)
    return pl.pallas_call(
        body, name=name, in_specs=[vm], out_specs=vm, out_shape=jax.ShapeDtypeStruct(vec.shape, F32),
        scratch_shapes=[pltpu.VMEM((8, rows, LANES), F32), pltpu.SemaphoreType.DMA((7,)), pltpu.SemaphoreType.DMA((7,))],
    )(vec)


def local_step(x2, tgt2, g1, g2, gdn_ng, qn_g, kn_g, p1, p2, conv_w, w_main, w_small, p_a, p_b, w_o, w_u, w_d, nseq, seq):
    rows, dm = x2.shape
    wide = NH * LANES
    row = lambda a, off=0, w=None: (a, "row", off, a.shape[1] if w is None else w)
    rowh = lambda a, off=0, w=LANES: (a, "rowh", off, w)
    par = lambda a: (a, "par", 0, a.shape[1])
    parh = lambda a, off=0: (a, "parh", off, LANES)
    o_row = lambda w, dt: (w, "row", w, dt)
    o_rowh = lambda dt, tw=wide, w=LANES: (tw, "rowh", w, dt)

    u, = ew_fwd("rms1", f_rms, [row(x2), par(g1)], [o_row(dm, BF16)], rows)
    proj = matmul("mm_in", u, w_main, "nn", F32, tn=1024)
    sp = matmul("mm_in_small", u, w_small, "nn", F32)
    so, = ew_fwd("small", f_small, [row(sp), par(p1), par(p2)], [o_row(LANES, F32)], rows)
    cs = cumsum_time("cumsum", so, nseq, seq, False)
    gb, bb, cb = ew_fwd("bcast", f_bcast, [row(so), row(cs)], [o_rowh(F32)] * 3, rows, NH)
    ct = transpose_time("c_time_major", cs, nseq, seq)
    conv = {}
    for mode, off in (("q", 0), ("k", NH), ("v", 2 * NH)):
        conv[mode], = ew_fwd(f"conv_{mode}", make_f_conv(mode), [rowh(proj, off), parh(conv_w, off)], [o_rowh(F32)],
                             rows, NH, seq, "hi", CONV_HEADS)
    val, kcum, attn, qdec, kdec, t_inv = gdn_a_fwd(conv["q"], conv["k"], conv["v"], gb, bb, rows)
    o_a, snaps = gdn_b_fwd(val, kcum, attn, qdec, kdec, gb, nseq, seq)
    ya_in, = ew_fwd("gdn_post", f_post, [rowh(o_a), rowh(proj, 3 * NH), par(gdn_ng)], [o_rowh(BF16)], rows, NH)
    fqn, = ew_fwd("fox_qn", f_rms, [rowh(proj, FOX_Q), par(qn_g)], [o_rowh(BF16)], rows, NH)
    fkn, = ew_fwd("fox_kn", f_rms, [rowh(proj, FOX_K), par(kn_g)], [o_rowh(BF16)], rows, NH)
    o_b, o_b16, lse = fox_fwd(fqn, fkn, proj, cb, ct, nseq, seq)
    y_a = matmul("mm_pa", ya_in, p_a, "nn", F32, tn=1024)
    y_b = matmul("mm_pb", o_b16, p_b, "nn", F32, tn=1024)
    gates = [row(proj, 7, dm), row(proj, 8, dm)]
    merged, = ew_fwd("merge", f_merge, gates + [row(y_a), row(y_b)], [o_row(dm, BF16)], rows)
    hres = matmul("mm_out", merged, w_o, "nn", F32, add=x2, tn=1024)
    hn, = ew_fwd("rms2", f_rms, [row(hres), par(g2)], [o_row(dm, BF16)], rows)
    dff = w_u.shape[1]
    act = matmul("mm_up", hn, w_u, "nn", F32, tn=1024)
    nff = dff // dm
    relu2, = ew_fwd("relu2", f_relu2, [rowh(act, 0, dm)], [o_rowh(BF16, dff, dm)], rows, nff, ROW_TILE // 2)
    out = matmul("mm_down", relu2, w_d, "nn", F32, add=hres, tn=1024)
    dout, dout16, loss_acc = loss_head(out, tgt2, rows, dm)

    g_first = lambda dt=F32: (lambda g, e: [g[0]])
    d_relu2 = matmul("mm_d_relu2", dout16, w_d, "nt", F32, tn=1024)
    dw_d = matmul("mm_dw_down", relu2, dout16, "tn", F32, tn=1024)
    d_act, = ew_bwd("relu2_b", f_relu2, [rowh(act, 0, dm)], [(rowh(d_relu2, 0, dm),)], [], lambda g, e: [g[0]],
                    [((rows, dff), "rowh", dm, BF16, None)], rows, nff, ROW_TILE // 2)
    dw_u = matmul("mm_dw_up", hn, d_act, "tn", F32, tn=1024)
    d_hn = matmul("mm_d_hn", d_act, w_u, "nt", F32, tn=1024)
    dh, dh16, dg2 = ew_bwd("rms2_b", f_rms, [row(hres), par(g2)], [(row(d_hn),)], [row(dout)],
                           lambda g, e: [g[0] + e[0], g[0] + e[0], g[1]],
                           [((rows, dm), "row", dm, F32, None), ((rows, dm), "row", dm, BF16, None), ((1, dm), "par", dm, F32, "all")], rows)
    d_merged = matmul("mm_d_merged", dh16, w_o, "nt", F32, tn=1024)
    dw_o = matmul("mm_dw_out", merged, dh16, "tn", F32, tn=1024)
    seg16 = ((rows, dm), "row", dm, BF16, None)
    d_ga16, d_gb16, d_ya16, d_yb16 = ew_bwd("merge_b", f_merge, gates + [row(y_a), row(y_b)], [(row(d_merged),)], [],
                                            lambda g, e: list(g), [seg16] * 4, rows)
    dp_a = matmul("mm_dp_a", ya_in, d_ya16, "tn", F32, tn=1024)
    d_ya_in = matmul("mm_d_ya_in", d_ya16, p_a, "nt", F32, tn=1024)
    dp_b = matmul("mm_dp_b", o_b16, d_yb16, "tn", F32, tn=1024)
    d_ob = matmul("mm_d_ob", d_yb16, p_b, "nt", F32, tn=1024)
    h32 = ((rows, wide), "rowh", LANES, F32, None)
    h16 = ((rows, wide), "rowh", LANES, BF16, None)
    gain = ((1, LANES), "par", LANES, F32, "all")
    d_oa, d_z16, d_gdn_ng = ew_bwd("gdn_post_b", f_post, [rowh(o_a), rowh(proj, 3 * NH), par(gdn_ng)], [(rowh(d_ya_in),)], [],
                                   lambda g, e: list(g), [h32, h16, gain], rows, NH)
    dval, dkc, dat, dqd, dkd, dgb_b = gdn_b_bwd(val, kcum, attn, qdec, kdec, gb, snaps, d_oa, nseq, seq)
    d_cq, d_ck, d_cv, d_gb, d_bb = gdn_a_bwd(conv["q"], conv["k"], conv["v"], gb, bb, t_inv, dval, dkc, dat, dqd, dkd, dgb_b, rows)
    d_pre, d_conv = {}, {}
    tap = ((4, wide), "parh", LANES, F32, "inner")
    for mode, off, ctg in (("q", 0, d_cq), ("k", NH, d_ck), ("v", 2 * NH, d_cv)):
        d_pre[mode], d_conv[mode] = ew_bwd(f"conv_{mode}_b", make_f_conv(mode), [rowh(proj, off), parh(conv_w, off)],
                                           [(rowh(ctg),)], [], lambda g, e: list(g), [h16, tap], rows, NH, seq, "hi", CONV_HEADS)
    delta, = ew_fwd("fox_delta", f_delta, [rowh(d_ob), rowh(o_b)], [o_rowh(F32)], rows, NH)
    d_fqn, d_cq_b = fox_dq(fqn, fkn, proj, cb, ct, d_ob, lse, delta, nseq, seq)
    d_fkn, d_fv16, d_ck_b = fox_dkv(fqn, fkn, proj, cb, ct, d_ob, lse, delta, nseq, seq)
    d_fq16, d_qn_g = ew_bwd("fox_qn_b", f_rms, [rowh(proj, FOX_Q), par(qn_g)], [(rowh(d_fqn),)], [], lambda g, e: list(g),
                            [h16, gain], rows, NH)
    d_fk16, d_kn_g = ew_bwd("fox_kn_b", f_rms, [rowh(proj, FOX_K), par(kn_g)], [(rowh(d_fkn),)], [], lambda g, e: list(g),
                            [h16, gain], rows, NH)
    narrow = ((rows, LANES), "row", LANES, F32, None)
    d_so, d_cs = ew_bwd("bcast_b", f_bcast, [row(so), row(cs)], [(rowh(d_gb),), (rowh(d_bb),), (rowh(d_cq_b), rowh(d_ck_b))], [],
                        lambda g, e: list(g), [narrow, narrow], rows, NH)
    d_logf = cumsum_time("cumsum_b", d_cs, nseq, seq, True)
    vec = ((1, LANES), "par", LANES, F32, "all")
    d_sp16, d_p1, d_p2 = ew_bwd("small_b", f_small, [row(sp), par(p1), par(p2)], [(row(d_so), row(d_logf))], [],
                                lambda g, e: list(g), [((rows, LANES), "row", LANES, BF16, None), vec, vec], rows)
    d_proj16 = jnp.concatenate([d_pre["q"], d_pre["k"], d_pre["v"], d_z16, d_fq16, d_fk16, d_fv16, d_ga16, d_gb16], axis=1)
    dw_main = matmul("mm_dw_main", u, d_proj16, "tn", F32, tn=1024)
    dw_small = matmul("mm_dw_small", u, d_sp16, "tn", F32)
    d_u = matmul("mm_d_u_small", d_sp16, w_small, "nt", F32, tn=1024)
    d_u = matmul("mm_d_u", d_proj16, w_main, "nt", F32, add=d_u, tn=1024)
    dx, dg1 = ew_bwd("rms1_b", f_rms, [row(x2), par(g1)], [(row(d_u),)], [row(dh)], lambda g, e: [g[0] + e[0], g[1]],
                     [((rows, dm), "row", dm, F32, None), ((1, dm), "par", dm, F32, "all")], rows)
    d_conv_w = jnp.concatenate([d_conv["q"], d_conv["k"], d_conv["v"]], axis=1)
    return dict(loss_acc=loss_acc, dx=dx, g1=dg1, g2=dg2, gdn_ng=d_gdn_ng, qn=d_qn_g, kn=d_kn_g, p1=d_p1, p2=d_p2,
                conv=d_conv_w, w_main=dw_main, w_small=dw_small, p_a=dp_a, p_b=dp_b, w_o=dw_o, w_u=dw_u, w_d=dw_d)


_W = NH * LANES
_A0, _A1 = 4 * _W, 4 * _W + 2 * NH
_B0, _B1 = _A1 + 3 * _W, _A1 + 3 * _W + NH
N_IN = _B1 + 2 * _W


def _split_w_in(full):
    main = jnp.concatenate([full[:, :_A0], full[:, _A1:_B0], full[:, _B1:]], axis=1)
    small = jnp.concatenate([full[:, _A0:_A1], full[:, _B0:_B1], jnp.zeros((full.shape[0], LANES - 3 * NH), full.dtype)], axis=1)
    return main, small


def _join_w_in(main, small):
    return jnp.concatenate([main[:, :_A0], small[:, :2 * NH], main[:, _A0:_A0 + 3 * _W], small[:, 2 * NH:3 * NH],
                            main[:, _A0 + 3 * _W:]], axis=1)


def _lanes(v, at=0):
    return jnp.pad(v.reshape(1, -1), ((0, 0), (at, LANES - at - v.size)))


def kernel(x, norm_mix_g, w_in, gdn_conv_w, gdn_a_log, gdn_dt_bias, gdn_norm_g, fox_q_norm_g, fox_k_norm_g, fox_f_bias, w_proj_gdn, w_proj_fox, w_out, norm_mlp_g, w_up, w_down, loss_target, m_norm_mix_g, m_w_in, m_gdn_conv_w, m_gdn_a_log, m_gdn_dt_bias, m_gdn_norm_g, m_fox_q_norm_g, m_fox_k_norm_g, m_fox_f_bias, m_w_proj_gdn, m_w_proj_fox, m_w_out, m_norm_mlp_g, m_w_up, m_w_down, v_norm_mix_g, v_w_in, v_gdn_conv_w, v_gdn_a_log, v_gdn_dt_bias, v_gdn_norm_g, v_fox_q_norm_g, v_fox_k_norm_g, v_fox_f_bias, v_w_proj_gdn, v_w_proj_fox, v_w_out, v_norm_mlp_g, v_w_up, v_w_down):
    nseq, seq, dm = x.shape
    rows = nseq * seq
    xi, yi, ci = lax.axis_index("x"), lax.axis_index("y"), lax.axis_index("c")
    chip = 2 * xi + yi
    conv_cols = gdn_conv_w.shape[2]

    big = [w_in[0], w_proj_gdn[0], w_proj_fox[0], w_out[0], w_up[0], w_down[0]]
    g_in, g_pa, g_pb, g_wo, g_wu, g_wd = gather_weights([w.astype(BF16) for w in big])
    w_main, w_small = _split_w_in(g_in.transpose(1, 0, 2).reshape(dm, -1))
    p_a, p_b, w_o = (g.reshape(-1, dm) for g in (g_pa, g_pb, g_wo))
    w_u = g_wu.transpose(1, 0, 2).reshape(dm, -1)
    w_d = g_wd.reshape(-1, dm)
    conv_slot = jnp.zeros((4, 4, conv_cols), F32).at[:, chip].set(jnp.where(ci == 0, gdn_conv_w[0], 0.0))
    conv_full = all_reduce_small("gather_conv", conv_slot.reshape(-1, LANES)).reshape(4, 4 * conv_cols)
    p1 = _lanes(gdn_dt_bias[0]) + _lanes(fox_f_bias[0], 2 * NH)
    p2 = _lanes(gdn_a_log[0])

    g = local_step(x.reshape(rows, dm), loss_target.reshape(rows, dm), norm_mix_g, norm_mlp_g, gdn_norm_g, fox_q_norm_g,
                   fox_k_norm_g, p1, p2, conv_full, w_main, w_small, p_a, p_b, w_o, w_u, w_d, nseq, seq)

    small_parts = [g["loss_acc"], g["g1"].reshape(8, LANES), g["g2"].reshape(8, LANES), g["gdn_ng"], g["qn"], g["kn"], g["p1"], g["p2"],
                   g["conv"].reshape(-1, LANES)]
    tiled = [jnp.pad(p, ((0, -p.shape[0] % 8), (0, 0))) for p in small_parts]
    red = all_reduce_small("reduce_small", jnp.concatenate(tiled, axis=0))
    pos, red_parts = 0, []
    for p, t in zip(small_parts, tiled):
        red_parts.append(red[pos:pos + p.shape[0]])
        pos += t.shape[0]
    r_loss, r_g1, r_g2, r_gdn_ng, r_qn, r_kn, r_p1, r_p2, r_conv = red_parts
    loss = jnp.sum(r_loss)
    g_conv = lax.dynamic_slice_in_dim(r_conv.reshape(4, 4, conv_cols), chip, 1, axis=1).reshape(4, conv_cols)
    small_grads = [r_g1.reshape(1, dm), r_p2[:, :NH], r_p1[:, :NH], r_gdn_ng, r_qn, r_kn, r_p1[:, 2 * NH:3 * NH], r_g2.reshape(1, dm)]
    small_w = [norm_mix_g, gdn_a_log, gdn_dt_bias, gdn_norm_g, fox_q_norm_g, fox_k_norm_g, fox_f_bias, norm_mlp_g]
    small_m = [m_norm_mix_g, m_gdn_a_log, m_gdn_dt_bias, m_gdn_norm_g, m_fox_q_norm_g, m_fox_k_norm_g, m_fox_f_bias, m_norm_mlp_g]
    small_v = [v_norm_mix_g, v_gdn_a_log, v_gdn_dt_bias, v_gdn_norm_g, v_fox_q_norm_g, v_fox_k_norm_g, v_fox_f_bias, v_norm_mlp_g]

    def pack(parts):
        flat = jnp.concatenate([jnp.pad(p.reshape(-1), (0, -p.size % LANES)) for p in parts])
        return jnp.pad(flat, (0, -flat.size % (8 * LANES))).reshape(-1, LANES)

    packed = adamw("adamw_small", pack(small_w + [gdn_conv_w[0]]), pack(small_grads + [g_conv]),
                   pack(small_m + [m_gdn_conv_w[0]]), pack(small_v + [v_gdn_conv_w[0]]))

    def unpack(flat2d):
        flat, pos, res = flat2d.reshape(-1), 0, []
        for p in small_w + [gdn_conv_w[0]]:
            res.append(flat[pos:pos + p.size].reshape(p.shape))
            pos += p.size + (-p.size % LANES)
        return res

    s_delta, s_m, s_v = (unpack(a) for a in packed)

    dw_in = _join_w_in(g["w_main"], g["w_small"])
    blocks = [dw_in.reshape(dm, 4, -1).transpose(1, 0, 2), g["p_a"].reshape(4, -1, dm), g["p_b"].reshape(4, -1, dm),
              g["w_o"].reshape(4, -1, dm), g["w_u"].reshape(dm, 4, -1).transpose(1, 0, 2), g["w_d"].reshape(4, -1, dm)]
    swapped = pair_swap(blocks)
    chip_part = add_pair("add_pair", blocks, swapped, ci.reshape(1).astype(jnp.int32))
    slots = chip_exchange(chip_part)
    halves = add_n("add_chips", slots)
    red_in, red_pa, red_pb, red_wo, red_wu, red_wd = pair_gather(halves)
    big_g = [red_in, red_pa, red_pb, red_wo, red_wu, red_wd]
    big_m = [m_w_in[0], m_w_proj_gdn[0], m_w_proj_fox[0], m_w_out[0], m_w_up[0], m_w_down[0]]
    big_v = [v_w_in[0], v_w_proj_gdn[0], v_w_proj_fox[0], v_w_out[0], v_w_up[0], v_w_down[0]]
    names = ["w_in", "w_proj_gdn", "w_proj_fox", "w_out", "w_up", "w_down"]
    big_res = {nm: adamw(f"adamw_{nm}", w, gr, m, v) for nm, w, gr, m, v in zip(names, big, big_g, big_m, big_v)}
    big_grad = dict(zip(names, big_g))

    order = ["norm_mix_g", "w_in", "gdn_conv_w", "gdn_a_log", "gdn_dt_bias", "gdn_norm_g", "fox_q_norm_g", "fox_k_norm_g",
             "fox_f_bias", "w_proj_gdn", "w_proj_fox", "w_out", "norm_mlp_g", "w_up", "w_down"]
    small_names = ["norm_mix_g", "gdn_a_log", "gdn_dt_bias", "gdn_norm_g", "fox_q_norm_g", "fox_k_norm_g", "fox_f_bias", "norm_mlp_g",
                   "gdn_conv_w"]
    small_idx = {nm: i for i, nm in enumerate(small_names)}
    shapes = dict(zip(order, (a.shape for a in (norm_mix_g, w_in, gdn_conv_w, gdn_a_log, gdn_dt_bias, gdn_norm_g, fox_q_norm_g,
                                                 fox_k_norm_g, fox_f_bias, w_proj_gdn, w_proj_fox, w_out, norm_mlp_g, w_up, w_down))))
    grads_out, delta_out, m_out, v_out = [], [], [], []
    for nm in order:
        if nm in big_res:
            d, mm, vv = big_res[nm]
            gr = big_grad[nm]
        else:
            i = small_idx[nm]
            gr = (small_grads + [g_conv])[i]
            d, mm, vv = s_delta[i], s_m[i], s_v[i]
        for lst, val in ((grads_out, gr), (delta_out, d), (m_out, mm), (v_out, vv)):
            lst.append(val.reshape(shapes[nm]))
    return (loss, g["dx"].reshape(x.shape), *grads_out, *delta_out, *m_out, *v_out)
```

```python
import functools

import jax
import jax.numpy as jnp
from jax import lax
from jax.experimental import pallas as pl
from jax.experimental.pallas import tpu as pltpu

F32 = jnp.float32
BF16 = jnp.bfloat16
LANES = 128
NH = 8
EPS = 1e-6
GDN_CHUNK = 64
GDN_ROWS = 256
GDN_BASE = 16
ROW_TILE = 512
CONV_HEADS = 2
ATT_TILE = 512
NEG = -1e30
VMEM_LIMIT_BYTES = 48 * 1024 * 1024
HI = lax.Precision.HIGHEST
LO = lax.Precision.DEFAULT
MESH = pl.DeviceIdType.MESH
ANY = pl.BlockSpec(memory_space=pl.ANY)

ADAM_LR, ADAM_B1, ADAM_B2, ADAM_EPS, ADAM_WD, ADAM_STEP = 0.001, 0.9, 0.999, 1e-08, 0.01, 10


def _params(n_grid):
    return pltpu.CompilerParams(dimension_semantics=("arbitrary",) * n_grid,
                                vmem_limit_bytes=VMEM_LIMIT_BYTES)


def _dot(a, b, dims, precision=None):
    dn = {"nn": (((1,), (0,)), ((), ())), "nt": (((1,), (1,)), ((), ())), "tn": (((0,), (0,)), ((), ()))}[dims]
    return lax.dot_general(a, b, dn, precision=precision, preferred_element_type=F32)


def _iota(shape, dim):
    return lax.broadcasted_iota(jnp.int32, shape, dim)


def _split(x, parts):
    out = []
    for _ in range(parts - 1):
        hi = x.astype(BF16)
        out.append(hi)
        x = x - hi.astype(F32)
    return out + [x.astype(BF16)]


def _dot_split(a, b, dims, a_exact=False):
    if a_exact:
        a16 = a.astype(BF16)
        b1, b2, b3 = _split(b, 3)
        return _dot(a16, b1, dims) + (_dot(a16, b2, dims) + _dot(a16, b3, dims))
    (ah, al), (bh, bl) = _split(a, 2), _split(b, 2)
    return _dot(ah, bh, dims) + (_dot(ah, bl, dims) + _dot(al, bh, dims))


@jax.custom_vjp
def mm_split(a, b):
    return _dot_split(a, b, "nn")


mm_split.defvjp(lambda a, b: (_dot_split(a, b, "nn"), (a, b)),
                lambda res, g: (_dot_split(g, res[1], "nt"), _dot_split(res[0], g, "tn")))


@jax.custom_vjp
def mm_mask(mask, b):
    return _dot_split(mask, b, "nn", True)


mm_mask.defvjp(lambda mask, b: (_dot_split(mask, b, "nn", True), mask),
               lambda mask, g: (jnp.zeros_like(mask), _dot_split(mask, g, "tn", True)))


def matmul(name, a, b, dims, out_dtype, add=None, tm=1024, tn=1024, tk=512):
    if dims == "nn":
        (m, k), (_, n) = a.shape, b.shape
    elif dims == "nt":
        (m, k), (n, _) = a.shape, b.shape
    else:
        (k, m), (_, n) = a.shape, b.shape
    if k <= 1024:
        tk = k
    tm, tn, tk = min(tm, m), min(tn, n), min(tk, k)
    assert m % tm == 0 and n % tn == 0 and k % tk == 0, (name, m, n, k)
    nk = k // tk
    a_spec = pl.BlockSpec((tk, tm), lambda i, j, kk: (kk, i)) if dims == "tn" else pl.BlockSpec((tm, tk), lambda i, j, kk: (i, kk))
    b_spec = pl.BlockSpec((tn, tk), lambda i, j, kk: (j, kk)) if dims == "nt" else pl.BlockSpec((tk, tn), lambda i, j, kk: (kk, j))
    o_spec = pl.BlockSpec((tm, tn), lambda i, j, kk: (i, j))
    has_add = add is not None

    def body(*refs):
        a_ref, b_ref = refs[0], refs[1]
        add_ref = refs[2] if has_add else None
        o_ref = refs[3] if has_add else refs[2]

        def finish(r):
            if has_add:
                r = r + add_ref[...]
            o_ref[...] = r.astype(o_ref.dtype)

        if nk == 1:
            finish(_dot(a_ref[...], b_ref[...], dims))
            return
        acc_ref = refs[-1]
        kk = pl.program_id(2)

        @pl.when(kk == 0)
        def _():
            acc_ref[...] = jnp.zeros_like(acc_ref)

        acc_ref[...] += _dot(a_ref[...], b_ref[...], dims)

        @pl.when(kk == nk - 1)
        def _():
            finish(acc_ref[...])

    ins = [a, b] + ([add] if has_add else [])
    in_specs = [a_spec, b_spec] + ([o_spec] if has_add else [])
    return pl.pallas_call(
        body, name=name, grid=(m // tm, n // tn, nk), in_specs=in_specs, out_specs=o_spec,
        out_shape=jax.ShapeDtypeStruct((m, n), out_dtype),
        scratch_shapes=[pltpu.VMEM((tm, tn), F32)] if nk > 1 else [], compiler_params=_params(3),
    )(*ins)


def _ew_spec(kind, off, width, tb, hp, order, shape=None):
    def ih(g0, g1):
        return (g0, g1) if order == "ih" else (g1, g0)

    assert off % hp == 0 or kind in ("row", "par")
    if kind == "row":
        return pl.BlockSpec((tb, width), lambda g0, g1: (ih(g0, g1)[0], off))
    if kind == "rowh":
        return pl.BlockSpec((tb, hp * width), lambda g0, g1: (ih(g0, g1)[0], ih(g0, g1)[1] + off // hp))
    if kind == "par":
        return pl.BlockSpec(shape, lambda g0, g1: (0, 0))
    if kind == "parh":
        return pl.BlockSpec((shape[0], hp * width), lambda g0, g1: (0, ih(g0, g1)[1] + off // hp))
    raise ValueError(kind)


def _ew_grid(rows, tb, nh, hp, order):
    assert nh % hp == 0 and rows % tb == 0
    return (rows // tb, nh // hp) if order == "ih" else (nh // hp, rows // tb)


def _ew_load(ref, kind, width, hh):
    if kind in ("row", "par"):
        return ref[...].astype(F32)
    return ref[:, hh * width:(hh + 1) * width].astype(F32)


def ew_fwd(name, f, ins, outs, rows, nh=1, tb=ROW_TILE, order="ih", hp=None):
    hp = nh if hp is None else hp
    n_in = len(ins)

    def body(*refs):
        hb = pl.program_id(1) if order == "ih" else pl.program_id(0)
        for hh in range(hp):
            h = hh if hp == nh else hb * hp + hh
            vals = [_ew_load(r, kd, w, hh) for r, (_, kd, _, w) in zip(refs[:n_in], ins)]
            res = f(h, *vals)
            for r, v, (_, kd, w, _) in zip(refs[n_in:], res, outs):
                if kd == "row":
                    assert hp == 1
                    r[...] = v.astype(r.dtype)
                else:
                    r[:, hh * w:(hh + 1) * w] = v.astype(r.dtype)

    in_specs = [_ew_spec(kd, off, w, tb, hp, order, a.shape) for (a, kd, off, w) in ins]
    out_specs = [_ew_spec(kd, 0, w, tb, hp, order) for (_, kd, w, _) in outs]
    out_shape = [jax.ShapeDtypeStruct((rows, tw), dt) for (tw, _, _, dt) in outs]
    return pl.pallas_call(
        body, name=name, grid=_ew_grid(rows, tb, nh, hp, order), in_specs=in_specs, out_specs=out_specs,
        out_shape=out_shape, compiler_params=_params(2),
    )(*[a for (a, _, _, _) in ins])


def ew_bwd(name, f, ins, cts, extras, emit, outs, rows, nh=1, tb=ROW_TILE, order="ih", hp=None):
    hp = nh if hp is None else hp
    n_in = len(ins)
    flat_cts = [d for group in cts for d in group]
    n_ct, n_ex = len(flat_cts), len(extras)

    def body(*refs):
        g0, g1 = pl.program_id(0), pl.program_id(1)
        hb = g1 if order == "ih" else g0
        out_refs = refs[n_in + n_ct + n_ex:]
        shared = [None] * len(outs)

        def store(r, v, first, sl=None):
            def put(val, add):
                if sl is None:
                    r[...] = (r[...] + val if add else val).astype(r.dtype)
                else:
                    r[:, sl] = (r[:, sl] + val if add else val).astype(r.dtype)

            if first is None:
                put(v, False)
            else:
                pl.when(first)(lambda: put(v, False))
                pl.when(jnp.logical_not(first))(lambda: put(v, True))

        for hh in range(hp):
            h = hh if hp == nh else hb * hp + hh
            vals = [_ew_load(r, kd, w, hh) for r, (_, kd, _, w) in zip(refs[:n_in], ins)]
            ct_refs = list(zip(refs[n_in:n_in + n_ct], flat_cts))
            ct_vals, pos = [], 0
            for group in cts:
                v = None
                for r, (_, kd, _, w) in ct_refs[pos:pos + len(group)]:
                    t = _ew_load(r, kd, w, hh)
                    v = t if v is None else v + t
                pos += len(group)
                ct_vals.append(v)
            ex_vals = [_ew_load(r, kd, w, hh) for r, (_, kd, _, w) in zip(refs[n_in + n_ct:n_in + n_ct + n_ex], extras)]
            _, vjp = jax.vjp(lambda *a: f(h, *a), *vals)
            res = emit(vjp(tuple(ct_vals)), ex_vals)
            for idx, (r, v, (_, kd, w, _, acc)) in enumerate(zip(out_refs, res, outs)):
                if kd in ("row", "par"):
                    shared[idx] = v if shared[idx] is None else shared[idx] + v
                else:
                    store(r, v, (g1 == 0) if acc == "inner" else None, slice(hh * w, (hh + 1) * w))
        for idx, (r, (_, kd, _, _, acc)) in enumerate(zip(out_refs, outs)):
            if kd in ("row", "par"):
                assert acc == "all" or hp == nh
                store(r, shared[idx], jnp.logical_and(g0 == 0, g1 == 0) if acc == "all" else None)

    operands = list(ins) + flat_cts + list(extras)
    in_specs = [_ew_spec(kd, off, w, tb, hp, order, a.shape) for (a, kd, off, w) in operands]
    out_specs = [_ew_spec(kd, 0, w, tb, hp, order, shp) for (shp, kd, w, _, _) in outs]
    out_shape = [jax.ShapeDtypeStruct(shp, dt) for (shp, _, _, dt, _) in outs]
    return pl.pallas_call(
        body, name=name, grid=_ew_grid(rows, tb, nh, hp, order), in_specs=in_specs, out_specs=out_specs,
        out_shape=out_shape, compiler_params=_params(2),
    )(*[a for (a, _, _, _) in operands])


def f_rms(h, x, g):
    r = lax.rsqrt(jnp.mean(x * x, axis=-1, keepdims=True) + EPS)
    return (x * r * g,)


def _softplus(z):
    return jnp.maximum(z, 0.0) + jnp.log1p(jnp.exp(-jnp.abs(z)))


def f_small(h, sp, p1, p2):
    lane = _iota(sp.shape, 1)
    z = sp + p1
    g = -jnp.exp(p2) * _softplus(z)
    beta = jax.nn.sigmoid(z)
    logf = -_softplus(-z)
    return (jnp.where(lane < NH, g, jnp.where(lane < 2 * NH, beta, jnp.where(lane < 3 * NH, logf, 0.0))),)


def _pick(x, lane_id):
    lane = _iota(x.shape, 1)
    col = jnp.sum(jnp.where(lane == lane_id, x, 0.0), axis=1, keepdims=True)
    return jnp.broadcast_to(col, x.shape)


def f_bcast(h, so, cs):
    return _pick(so, h), _pick(so, h + NH), _pick(cs, h + 2 * NH)


def _shift_down(s):
    def down(x):
        return jnp.where(_iota(x.shape, 0) >= s, pltpu.roll(x, s, 0), 0.0)

    def up(g):
        n = g.shape[0]
        return jnp.where(_iota(g.shape, 0) < n - s, pltpu.roll(g, n - s, 0), 0.0)

    @jax.custom_vjp
    def shift(x):
        return down(x)

    shift.defvjp(lambda x: (down(x), None), lambda _, g: (up(g),))
    return shift


def _silu(x):
    return x * jax.nn.sigmoid(x)


def make_f_conv(mode):
    sh1, sh2, sh3 = _shift_down(1), _shift_down(2), _shift_down(3)

    def f(h, x, w):
        sub = _iota(w.shape, 0)

        def tap(i):
            return jnp.sum(jnp.where(sub == i, w, 0.0), axis=0, keepdims=True)

        y = sh3(x) * tap(0)
        y = y + sh2(x) * tap(1)
        y = y + sh1(x) * tap(2)
        y = y + x * tap(3)
        s = _silu(y)
        if mode == "v":
            return (s,)
        n = s * lax.rsqrt(jnp.sum(s * s, axis=-1, keepdims=True) + EPS)
        if mode == "q":
            n = n * (LANES ** -0.5)
        return (n,)

    return f


def f_post(h, o, z, g):
    r = lax.rsqrt(jnp.mean(o * o, axis=-1, keepdims=True) + EPS)
    return (o * r * g * _silu(z),)


def f_merge(h, ga, gb, ya, yb):
    return (jax.nn.sigmoid(ga) * ya + jax.nn.sigmoid(gb) * yb,)


def f_relu2(h, a):
    m = jnp.maximum(a, 0.0)
    return (m * m,)


def f_delta(h, do, o):
    return (jnp.broadcast_to(jnp.sum(do * o, axis=1, keepdims=True), o.shape),)


def cumsum_time(name, x, nseq, seq, reverse):
    nb = seq // LANES

    def body(x_ref, o_ref):
        r, c = _iota((LANES, LANES), 0), _iota((LANES, LANES), 1)
        tri = jnp.where((r <= c) if reverse else (r >= c), 1.0, 0.0).astype(F32)
        carry = jnp.zeros((1, LANES), F32)
        for b in (range(nb - 1, -1, -1) if reverse else range(nb)):
            blk = x_ref[b * LANES:(b + 1) * LANES, :]
            o_ref[b * LANES:(b + 1) * LANES, :] = _dot_split(tri, blk, "nn", True) + carry
            carry = carry + jnp.sum(blk, axis=0, keepdims=True)

    spec = pl.BlockSpec((seq, LANES), lambda s: (s, 0))
    return pl.pallas_call(body, name=name, grid=(nseq,), in_specs=[spec], out_specs=spec,
                          out_shape=jax.ShapeDtypeStruct(x.shape, F32), compiler_params=_params(1))(x)


def transpose_time(name, x, nseq, seq):
    def body(x_ref, o_ref):
        o_ref[...] = x_ref[...].T

    return pl.pallas_call(
        body, name=name, grid=(nseq,), in_specs=[pl.BlockSpec((seq, LANES), lambda s: (s, 0))],
        out_specs=pl.BlockSpec((LANES, seq), lambda s: (s, 0)),
        out_shape=jax.ShapeDtypeStruct((nseq * LANES, seq), F32), compiler_params=_params(1))(x)


def _gdn_masks():
    n = GDN_ROWS
    r, c = _iota((n, n), 0), _iota((n, n), 1)
    shift = GDN_CHUNK.bit_length() - 1
    same = lax.shift_right_logical(r, shift) == lax.shift_right_logical(c, shift)
    return r, c, same


def _gdn_decay(gb):
    r, c, same = _gdn_masks()
    seg_tril = jnp.where(jnp.logical_and(same, r >= c), 1.0, 0.0).astype(F32)
    g_cum = mm_mask(seg_tril, gb)
    lane0 = _iota(g_cum.shape, 1) == 0
    g_col = jnp.sum(jnp.where(lane0, g_cum, 0.0), axis=1, keepdims=True)
    g_row = jnp.sum(jnp.where(r == c, jnp.broadcast_to(g_col, (GDN_ROWS, GDN_ROWS)), 0.0), axis=0, keepdims=True)
    return g_cum, g_col - g_row


def gdn_f1(q, k, gb, bb):
    r, c, same = _gdn_masks()
    strict = jnp.logical_and(same, r > c)
    _, diff = _gdn_decay(gb)
    lane0 = _iota(bb.shape, 1) == 0
    beta_col = jnp.sum(jnp.where(lane0, bb, 0.0), axis=1, keepdims=True)
    kk = _dot(k, k, "nt", LO)
    return jnp.where(strict, beta_col * kk * jnp.exp(jnp.where(strict, diff, 0.0)), 0.0)


def gdn_f2(t_inv, q, k, v, gb, bb):
    r, c, same = _gdn_masks()
    incl = jnp.logical_and(same, r >= c)
    g_cum, diff = _gdn_decay(gb)
    decay = jnp.where(incl, jnp.exp(jnp.where(incl, diff, 0.0)), 0.0)
    e_g = jnp.exp(g_cum)
    value = mm_split(t_inv, v * bb)
    k_cum = mm_split(t_inv, k * bb * e_g)
    attn = _dot(q, k, "nt", LO) * decay
    g_last = mm_mask(jnp.where(same, 1.0, 0.0).astype(F32), gb)
    return value, k_cum, attn, q * e_g, k * jnp.exp(g_last - g_cum)


def tri_inverse(a):
    n = GDN_ROWS
    r, c = _iota((n, n), 0), _iota((n, n), 1)
    shift = GDN_BASE.bit_length() - 1
    blk = lax.shift_right_logical(r, shift) == lax.shift_right_logical(c, shift)
    eye = jnp.where(r == c, 1.0, 0.0).astype(F32)
    d = jnp.where(blk, a, 0.0)
    lo = a - d
    p = -d
    t_d = eye + p
    steps = shift - 1
    for _ in range(steps):
        p = _dot_split(p, p, "nn")
        t_d = t_d + _dot_split(t_d, p, "nn")
    assert GDN_CHUNK // GDN_BASE == 4
    nmat = _dot_split(t_d, lo, "nn")
    n2 = _dot_split(nmat, nmat, "nn")
    t_n = (eye - nmat) + _dot_split(eye - nmat, n2, "nn")
    return _dot_split(t_n, t_d, "nn")


def gdn_a_fwd(q, k, v, gb, bb, rows):
    blk = pl.BlockSpec((GDN_ROWS, LANES), lambda i, h: (i, h))
    sq = pl.BlockSpec((GDN_ROWS, GDN_ROWS), lambda i, h: (i, h))

    def body(q_ref, k_ref, v_ref, gb_ref, bb_ref, val_ref, kc_ref, at_ref, qd_ref, kd_ref, t_ref):
        qv, kv, vv, gv, bv = q_ref[...], k_ref[...], v_ref[...], gb_ref[...], bb_ref[...]
        t_inv = tri_inverse(gdn_f1(qv, kv, gv, bv))
        value, k_cum, attn, q_dec, k_dec = gdn_f2(t_inv, qv, kv, vv, gv, bv)
        val_ref[...], kc_ref[...], at_ref[...], qd_ref[...], kd_ref[...], t_ref[...] = value, k_cum, attn, q_dec, k_dec, t_inv

    wide = jax.ShapeDtypeStruct((rows, NH * LANES), F32)
    square = jax.ShapeDtypeStruct((rows, NH * GDN_ROWS), F32)
    return pl.pallas_call(
        body, name="gdn_a_fwd", grid=(rows // GDN_ROWS, NH), in_specs=[blk] * 5,
        out_specs=[blk, blk, sq, blk, blk, sq], out_shape=[wide, wide, square, wide, wide, square],
        compiler_params=_params(2))(q, k, v, gb, bb)


def gdn_a_bwd(q, k, v, gb, bb, t_inv, dval, dkc, dat, dqd, dkd, dgb_b, rows):
    blk = pl.BlockSpec((GDN_ROWS, LANES), lambda i, h: (i, h))
    sq = pl.BlockSpec((GDN_ROWS, GDN_ROWS), lambda i, h: (i, h))

    def body(q_ref, k_ref, v_ref, gb_ref, bb_ref, t_ref, dval_ref, dkc_ref, dat_ref, dqd_ref, dkd_ref, dgbb_ref,
             dq_ref, dk_ref, dv_ref, dgb_ref, dbb_ref):
        qv, kv, vv, gv, bv, tv = q_ref[...], k_ref[...], v_ref[...], gb_ref[...], bb_ref[...], t_ref[...]
        _, vjp1 = jax.vjp(gdn_f1, qv, kv, gv, bv)
        _, vjp2 = jax.vjp(gdn_f2, tv, qv, kv, vv, gv, bv)
        dt, dq2, dk2, dv2, dgb2, dbb2 = vjp2((dval_ref[...], dkc_ref[...], dat_ref[...], dqd_ref[...], dkd_ref[...]))
        da = -_dot_split(tv, _dot_split(dt, tv, "nt"), "tn")
        dq1, dk1, dgb1, dbb1 = vjp1(da)
        dq_ref[...] = dq1 + dq2
        dk_ref[...] = dk1 + dk2
        dv_ref[...] = dv2
        dgb_ref[...] = dgb1 + dgb2 + dgbb_ref[...]
        dbb_ref[...] = dbb1 + dbb2

    wide = jax.ShapeDtypeStruct((rows, NH * LANES), F32)
    return pl.pallas_call(
        body, name="gdn_a_bwd", grid=(rows // GDN_ROWS, NH),
        in_specs=[blk] * 5 + [sq, blk, blk, sq, blk, blk, blk], out_specs=[blk] * 5, out_shape=[wide] * 5,
        compiler_params=_params(2))(q, k, v, gb, bb, t_inv, dval, dkc, dat, dqd, dkd, dgb_b)


N_CH = GDN_ROWS // GDN_CHUNK


def gdn_fb(*args):
    val, kc, at, qd, kd, gb = (args[i * N_CH:(i + 1) * N_CH] for i in range(6))
    s = args[6 * N_CH]
    outs = []
    zero = jnp.zeros((GDN_CHUNK, LANES), F32)
    for c in range(N_CH):
        v_new = val[c] - _dot(kc[c], s, "nn", LO)
        v_pad = jnp.concatenate([zero] * c + [v_new] + [zero] * (N_CH - 1 - c), axis=0)
        outs.append(_dot(qd[c], s, "nn", LO) + _dot(at[c], v_pad, "nn", LO))
        dec = jnp.exp(jnp.sum(gb[c], axis=0, keepdims=True))
        s = s * dec + _dot(kd[c], v_new, "tn", LO)
    return (*outs, s)


GDN_HP = 4


def _gdn_piece(ref, hh, c):
    width = ref.shape[1] // GDN_HP
    return ref.at[c * GDN_CHUNK:(c + 1) * GDN_CHUNK, hh * width:(hh + 1) * width]


def _gdn_pieces(refs, hh):
    return [_gdn_piece(r, hh, c)[...] for r in refs for c in range(N_CH)]


def _gdn_b_specs(nb, rev):
    def blk_row(s, j):
        return s * nb + (nb - 1 - j if rev else j)

    blk = pl.BlockSpec((GDN_ROWS, GDN_HP * LANES), lambda s, hb, j: (blk_row(s, j), hb))
    sq = pl.BlockSpec((GDN_ROWS, GDN_HP * GDN_ROWS), lambda s, hb, j: (blk_row(s, j), hb))
    snap = pl.BlockSpec((GDN_HP * LANES, LANES), lambda s, hb, j: (blk_row(s, j) * (NH // GDN_HP) + hb, 0))
    return blk, sq, snap


def gdn_b_fwd(val, kc, at, qd, kd, gb, nseq, seq):
    nb = seq // GDN_ROWS
    rows = nseq * seq
    blk, sq, snap = _gdn_b_specs(nb, False)

    def body(val_ref, kc_ref, at_ref, qd_ref, kd_ref, gb_ref, o_ref, snap_ref, s_ref):
        @pl.when(pl.program_id(2) == 0)
        def _():
            s_ref[...] = jnp.zeros_like(s_ref)

        for hh in range(GDN_HP):
            s_in = s_ref[hh]
            snap_ref[hh * LANES:(hh + 1) * LANES, :] = s_in
            res = gdn_fb(*_gdn_pieces([val_ref, kc_ref, at_ref, qd_ref, kd_ref, gb_ref], hh), s_in)
            for c in range(N_CH):
                _gdn_piece(o_ref, hh, c)[...] = res[c]
            s_ref[hh] = res[N_CH]

    return pl.pallas_call(
        body, name="gdn_b_fwd", grid=(nseq, NH // GDN_HP, nb), in_specs=[blk, blk, sq, blk, blk, blk], out_specs=[blk, snap],
        out_shape=[jax.ShapeDtypeStruct((rows, NH * LANES), F32), jax.ShapeDtypeStruct((nseq * nb * NH * LANES, LANES), F32)],
        scratch_shapes=[pltpu.VMEM((GDN_HP, LANES, LANES), F32)], compiler_params=_params(3))(val, kc, at, qd, kd, gb)


def gdn_b_bwd(val, kc, at, qd, kd, gb, snaps, do, nseq, seq):
    nb = seq // GDN_ROWS
    rows = nseq * seq
    blk, sq, snap = _gdn_b_specs(nb, True)

    def body(val_ref, kc_ref, at_ref, qd_ref, kd_ref, gb_ref, snap_ref, do_ref,
             dval_ref, dkc_ref, dat_ref, dqd_ref, dkd_ref, dgb_ref, ds_ref):
        @pl.when(pl.program_id(2) == 0)
        def _():
            ds_ref[...] = jnp.zeros_like(ds_ref)

        for hh in range(GDN_HP):
            _, vjp = jax.vjp(gdn_fb, *_gdn_pieces([val_ref, kc_ref, at_ref, qd_ref, kd_ref, gb_ref], hh),
                             snap_ref[hh * LANES:(hh + 1) * LANES, :])
            grads = vjp((*_gdn_pieces([do_ref], hh), ds_ref[hh]))
            for i, r in enumerate([dval_ref, dkc_ref, dat_ref, dqd_ref, dkd_ref, dgb_ref]):
                for c in range(N_CH):
                    _gdn_piece(r, hh, c)[...] = grads[i * N_CH + c]
            ds_ref[hh] = grads[6 * N_CH]

    wide = jax.ShapeDtypeStruct((rows, NH * LANES), F32)
    square = jax.ShapeDtypeStruct((rows, NH * GDN_ROWS), F32)
    return pl.pallas_call(
        body, name="gdn_b_bwd", grid=(nseq, NH // GDN_HP, nb), in_specs=[blk, blk, sq, blk, blk, blk, snap, blk],
        out_specs=[blk, blk, sq, blk, blk, blk], out_shape=[wide, wide, square, wide, wide, wide],
        scratch_shapes=[pltpu.VMEM((GDN_HP, LANES, LANES), F32)], compiler_params=_params(3))(val, kc, at, qd, kd, gb, snaps, do)


FOX_Q, FOX_K, FOX_V = 4 * NH, 5 * NH, 6 * NH
FOX_SCALE = LANES ** -0.5


def _head_row(ct_ref, h, off, width):
    blk = ct_ref[:, pl.ds(off, width)]
    return jnp.sum(jnp.where(_iota(blk.shape, 0) == h, blk, 0.0), axis=0, keepdims=True)


def _col(x):
    return jnp.max(x, axis=1, keepdims=True)


def _row(x):
    return jnp.max(x.T, axis=0, keepdims=True)


def _causal(shape, q_dim):
    return _iota(shape, q_dim) >= _iota(shape, 1 - q_dim)


def fox_fwd(qn, kn, proj, ct, nseq, seq):
    tq = tk = min(ATT_TILE, seq)
    nq = seq // tq
    rows = nseq * seq
    qblk = pl.BlockSpec((tq, LANES), lambda s, h, i: (s * nq + i, h))
    full = pl.BlockSpec((seq, LANES), lambda s, h, i: (s, h))
    vfull = pl.BlockSpec((seq, LANES), lambda s, h, i: (s, h + FOX_V))
    ctb = pl.BlockSpec((NH, seq), lambda s, h, i: (s * (LANES // NH) + 2, 0))

    def body(q_ref, k_ref, v_ref, ct_ref, o_ref, o16_ref, lse_ref):
        h, i = pl.program_id(1), pl.program_id(2)
        q = q_ref[...]

        def step(j, carry, diag):
            m, l, acc = carry
            off = pl.multiple_of(j * tk, tk)
            s = _dot(q, k_ref[pl.ds(off, tk), :], "nt") * FOX_SCALE - _head_row(ct_ref, h, off, tk)
            if diag:
                s = jnp.where(_causal(s.shape, 0), s, NEG)
            m_new = jnp.maximum(m, jnp.max(s, axis=1, keepdims=True))
            p = jnp.exp(s - m_new)
            alpha = jnp.exp(m - m_new)
            l = alpha * l + jnp.sum(p, axis=1, keepdims=True)
            acc = alpha * acc + _dot(p.astype(BF16), v_ref[pl.ds(off, tk), :].astype(BF16), "nn")
            return m_new, l, acc

        init = (jnp.full((tq, 1), NEG, F32), jnp.zeros((tq, 1), F32), jnp.zeros((tq, LANES), F32))
        carry = lax.fori_loop(0, i, lambda j, c: step(j, c, False), init)
        m, l, acc = step(i, carry, True)
        o = acc / l
        o_ref[...] = o
        o16_ref[...] = o.astype(BF16)
        lse_ref[...] = jnp.broadcast_to(m + jnp.log(l), (tq, LANES))

    wide = (rows, NH * LANES)
    return pl.pallas_call(
        body, name="fox_fwd", grid=(nseq, NH, nq), in_specs=[qblk, full, vfull, ctb], out_specs=[qblk] * 3,
        out_shape=[jax.ShapeDtypeStruct(wide, F32), jax.ShapeDtypeStruct(wide, BF16), jax.ShapeDtypeStruct(wide, F32)],
        compiler_params=_params(3))(qn, kn, proj, ct)


def fox_dq(qn, kn, proj, ct, do, lse, delta, nseq, seq):
    tq = tk = min(ATT_TILE, seq)
    nq = seq // tq
    rows = nseq * seq
    qblk = pl.BlockSpec((tq, LANES), lambda s, h, i: (s * nq + i, h))
    full = pl.BlockSpec((seq, LANES), lambda s, h, i: (s, h))
    vfull = pl.BlockSpec((seq, LANES), lambda s, h, i: (s, h + FOX_V))
    ctb = pl.BlockSpec((NH, seq), lambda s, h, i: (s * (LANES // NH) + 2, 0))

    def body(q_ref, k_ref, v_ref, ct_ref, do_ref, lse_ref, dl_ref, dq_ref, dc_ref):
        h, i = pl.program_id(1), pl.program_id(2)
        q = q_ref[...]
        lse, delta = _col(lse_ref[...]), _col(dl_ref[...])
        do16 = do_ref[...].astype(BF16)

        def step(j, carry, diag):
            dq, dc = carry
            off = pl.multiple_of(j * tk, tk)
            k = k_ref[pl.ds(off, tk), :]
            p = jnp.exp(_dot(q, k, "nt") * FOX_SCALE - _head_row(ct_ref, h, off, tk) - lse)
            if diag:
                p = jnp.where(_causal(p.shape, 0), p, 0.0)
            dp = _dot(do16, v_ref[pl.ds(off, tk), :].astype(BF16), "nt")
            ds = p * (dp - delta)
            return dq + _dot(ds.astype(BF16), k, "nn"), dc + jnp.sum(ds, axis=1, keepdims=True)

        init = (jnp.zeros((tq, LANES), F32), jnp.zeros((tq, 1), F32))
        dq, dc = step(i, lax.fori_loop(0, i, lambda j, c: step(j, c, False), init), True)
        dq_ref[...] = dq * FOX_SCALE
        dc_ref[...] = jnp.where(_iota((tq, LANES), 1) == 0, dc, 0.0)

    wide = jax.ShapeDtypeStruct((rows, NH * LANES), F32)
    return pl.pallas_call(
        body, name="fox_dq", grid=(nseq, NH, nq), in_specs=[qblk, full, vfull, ctb, qblk, qblk, qblk],
        out_specs=[qblk, qblk], out_shape=[wide, wide], compiler_params=_params(3))(qn, kn, proj, ct, do, lse, delta)


def fox_dkv(qn, kn, proj, cb, do, lse, delta, nseq, seq):
    tq = tk = min(ATT_TILE, seq)
    nq = seq // tq
    rows = nseq * seq
    kblk = pl.BlockSpec((tk, LANES), lambda s, h, j: (s * nq + j, h))
    vblk = pl.BlockSpec((tk, LANES), lambda s, h, j: (s * nq + j, h + FOX_V))
    full = pl.BlockSpec((seq, LANES), lambda s, h, j: (s, h))

    def body(q_ref, k_ref, v_ref, cb_ref, do_ref, lse_ref, dl_ref, dk_ref, dv_ref, dc_ref):
        j = pl.program_id(2)
        k = k_ref[...]
        v16 = v_ref[...].astype(BF16)
        ck = _col(cb_ref[...])

        def step(i, carry, diag):
            dk, dv, dc = carry
            off = pl.multiple_of(i * tq, tq)
            q = q_ref[pl.ds(off, tq), :]
            do16 = do_ref[pl.ds(off, tq), :].astype(BF16)
            lse, delta = (_row(r[pl.ds(off, tq), :]) for r in (lse_ref, dl_ref))
            p = jnp.exp(_dot(k, q, "nt") * FOX_SCALE - ck - lse)
            if diag:
                p = jnp.where(_causal(p.shape, 1), p, 0.0)
            dv = dv + _dot(p.astype(BF16), do16, "nn")
            ds = p * (_dot(v16, do16, "nt") - delta)
            return dk + _dot(ds.astype(BF16), q, "nn"), dv, dc + jnp.sum(ds, axis=1, keepdims=True)

        zero = jnp.zeros((tk, LANES), F32)
        carry = step(j, (zero, zero, jnp.zeros((tk, 1), F32)), True)
        dk, dv, dc = lax.fori_loop(j + 1, nq, lambda i, c: step(i, c, False), carry)
        dk_ref[...] = dk * FOX_SCALE
        dv_ref[...] = dv.astype(BF16)
        dc_ref[...] = jnp.where(_iota((tk, LANES), 1) == 0, -dc, 0.0)

    wide = (rows, NH * LANES)
    return pl.pallas_call(
        body, name="fox_dkv", grid=(nseq, NH, nq), in_specs=[full, kblk, vblk, kblk, full, full, full],
        out_specs=[kblk, kblk, kblk],
        out_shape=[jax.ShapeDtypeStruct(wide, F32), jax.ShapeDtypeStruct(wide, BF16), jax.ShapeDtypeStruct(wide, F32)],
        compiler_params=_params(3))(qn, kn, proj, cb, do, lse, delta)


def loss_head(out, tgt, rows, width):
    tb = ROW_TILE
    blk = pl.BlockSpec((tb, width), lambda i: (i, 0))
    accb = pl.BlockSpec((8, LANES), lambda i: (0, 0))

    def body(o_ref, t_ref, d32_ref, d16_ref, acc_ref):
        d = o_ref[...] - t_ref[...]
        row_loss = 0.5 * jnp.mean(d * d, axis=1, keepdims=True)
        g = d * (1.0 / width)
        d32_ref[...] = g
        d16_ref[...] = g.astype(BF16)
        part = jnp.where(_iota((tb, LANES), 1) == 0, row_loss, 0.0).reshape(tb // 8, 8, LANES).sum(axis=0)

        @pl.when(pl.program_id(0) == 0)
        def _():
            acc_ref[...] = part

        @pl.when(pl.program_id(0) != 0)
        def _():
            acc_ref[...] += part

    return pl.pallas_call(
        body, name="loss_head", grid=(rows // tb,), in_specs=[blk, blk], out_specs=[blk, blk, accb],
        out_shape=[jax.ShapeDtypeStruct((rows, width), F32), jax.ShapeDtypeStruct((rows, width), BF16),
                   jax.ShapeDtypeStruct((8, LANES), F32)], compiler_params=_params(1))(out, tgt)


def _adamw_update(w, g, m, v):
    m_new = ADAM_B1 * m + (1.0 - ADAM_B1) * g
    v_new = ADAM_B2 * v + (1.0 - ADAM_B2) * (g * g)
    m_hat = m_new / (1.0 - ADAM_B1 ** ADAM_STEP)
    v_hat = v_new / (1.0 - ADAM_B2 ** ADAM_STEP)
    return -ADAM_LR * (m_hat / (jnp.sqrt(v_hat) + ADAM_EPS) + ADAM_WD * w), m_new, v_new


def adamw(name, w, g, m, v):
    rows, cols = w.shape
    tb = min(rows, 128)
    assert rows % tb == 0
    blk = pl.BlockSpec((tb, cols), lambda i: (i, 0))

    def body(w_ref, g_ref, m_ref, v_ref, d_ref, mo_ref, vo_ref):
        d_ref[...], mo_ref[...], vo_ref[...] = _adamw_update(w_ref[...], g_ref[...], m_ref[...], v_ref[...])

    shp = jax.ShapeDtypeStruct(w.shape, F32)
    return pl.pallas_call(body, name=name, grid=(rows // tb,), in_specs=[blk] * 4, out_specs=[blk] * 3,
                          out_shape=[shp] * 3, compiler_params=_params(1))(w, g, m, v)


def adamw_halves(name, w, mine, other, m, v, c):
    rows, cols = w.shape
    half = rows // 2
    tb = min(half, 128)
    steps = half // tb
    assert half % tb == 0

    def body(c_ref, w_ref, mine_ref, other_ref, m_ref, v_ref, g_ref, d_ref, mo_ref, vo_ref):
        g = jnp.where(pl.program_id(0) // steps == c_ref[0], mine_ref[...], other_ref[...])
        g_ref[...] = g
        d_ref[...], mo_ref[...], vo_ref[...] = _adamw_update(w_ref[...], g, m_ref[...], v_ref[...])

    blk = pl.BlockSpec((tb, cols), lambda i, c_ref: (i, 0))
    hblk = pl.BlockSpec((tb, cols), lambda i, c_ref: (i % steps, 0))
    grid_spec = pltpu.PrefetchScalarGridSpec(num_scalar_prefetch=1, grid=(rows // tb,),
                                             in_specs=[blk, hblk, hblk, blk, blk], out_specs=[blk] * 4)
    shp = jax.ShapeDtypeStruct(w.shape, F32)
    return pl.pallas_call(body, name=name, grid_spec=grid_spec, out_shape=[shp] * 4,
                          compiler_params=_params(1))(c, w, mine, other, m, v)


def add_chips(name, slots, parts, chip):
    outs = []
    for idx, (x, own) in enumerate(zip(slots, parts)):
        n, rows, cols = x.shape
        tb = min(rows, 128)
        assert rows % tb == 0

        def body(me_ref, *refs, n=n):
            o_ref = refs[n + 1]
            acc = None
            for t in range(n):
                term = jnp.where(me_ref[0] == t, refs[n][...], refs[t][...]).astype(F32)
                acc = term if acc is None else acc + term
            o_ref[...] = acc

        def filled(t, n=n):
            return lambda i, me_ref: (jnp.where(me_ref[0] == t, (t + 1) % n, t), i, 0)

        grid_spec = pltpu.PrefetchScalarGridSpec(
            num_scalar_prefetch=1, grid=(rows // tb,),
            in_specs=[pl.BlockSpec((None, tb, cols), filled(t)) for t in range(n)]
            + [pl.BlockSpec((None, tb, cols), lambda i, me_ref: (me_ref[0], i, 0))],
            out_specs=pl.BlockSpec((tb, cols), lambda i, me_ref: (i, 0)))
        outs.append(pl.pallas_call(
            body, name=f"{name}_{idx}", grid_spec=grid_spec, out_shape=jax.ShapeDtypeStruct((rows, cols), F32),
            compiler_params=_params(1))(chip, *([x] * n), own))
    return outs


def add_pair(name, gs, rs, c):
    outs = []
    for idx, (g, r) in enumerate(zip(gs, rs)):
        nb, half, cols = r.shape
        tb = min(half, 128)
        steps = half // tb

        def body(c_ref, g_ref, r_ref, o_ref):
            o_ref[...] = (g_ref[...] + r_ref[...]).astype(BF16)

        grid_spec = pltpu.PrefetchScalarGridSpec(
            num_scalar_prefetch=1, grid=(nb, steps),
            in_specs=[pl.BlockSpec((None, tb, cols), lambda b, i, c_ref: (b, c_ref[0] * steps + i, 0)),
                      pl.BlockSpec((None, tb, cols), lambda b, i, c_ref: (b, i, 0))],
            out_specs=pl.BlockSpec((None, tb, cols), lambda b, i, c_ref: (b, i, 0)))
        outs.append(pl.pallas_call(
            body, name=f"{name}_{idx}", grid_spec=grid_spec, out_shape=jax.ShapeDtypeStruct(r.shape, BF16),
            compiler_params=_params(2))(c, g, r))
    return outs


def _place():
    x, y, c = lax.axis_index("x"), lax.axis_index("y"), lax.axis_index("c")
    return x, y, c, [(1 - x, y), (x, 1 - y), (1 - x, 1 - y)]


def _remote(src, dst, send_sem, recv_sem, dev):
    return pltpu.make_async_remote_copy(src_ref=src, dst_ref=dst, send_sem=send_sem, recv_sem=recv_sem,
                                        device_id=dev, device_id_type=MESH)


def gather_weights(shards):
    n = len(shards)

    def body(*refs):
        ins, outs = refs[:n], refs[n:2 * n]
        ici_s, ici_r, d2d_s, d2d_r = refs[2 * n:]
        x, y, c, chips = _place()
        me = 2 * x + y
        sends, passes = [], []
        for w in range(n):
            half = ins[w].shape[0] // 2
            for j, (ox, oy) in enumerate(chips):
                mine = pl.ds(c * half, half)
                cp = _remote(ins[w].at[mine, :], outs[w].at[me, mine, :], ici_s.at[3 * w + j], ici_r.at[3 * w + j], (ox, oy, c))
                cp.start()
                sends.append(cp)
        for w in range(n):
            half = ins[w].shape[0] // 2
            for j, (ox, oy) in enumerate(chips):
                landed = outs[w].at[2 * ox + oy, pl.ds(c * half, half), :]
                _remote(landed, landed, ici_s.at[3 * w + j], ici_r.at[3 * w + j], (ox, oy, c)).wait_recv()
                cp = _remote(landed, landed, d2d_s.at[3 * w + j], d2d_r.at[3 * w + j], (x, y, 1 - c))
                cp.start()
                passes.append(cp)
        for w in range(n):
            half = ins[w].shape[0] // 2
            for j, (ox, oy) in enumerate(chips):
                other = outs[w].at[2 * ox + oy, pl.ds((1 - c) * half, half), :]
                _remote(other, other, d2d_s.at[3 * w + j], d2d_r.at[3 * w + j], (x, y, 1 - c)).wait_recv()
        for cp in sends + passes:
            cp.wait_send()

    return pl.pallas_call(
        body, name="gather_weights", in_specs=[ANY] * n, out_specs=[ANY] * n,
        out_shape=[jax.ShapeDtypeStruct((4,) + s.shape, s.dtype) for s in shards],
        scratch_shapes=[pltpu.SemaphoreType.DMA((3 * n,))] * 4,
    )(*shards)


def pair_swap(grads):
    n = len(grads)

    def body(*refs):
        ins, outs = refs[:n], refs[n:2 * n]
        send, recv = refs[2 * n:]
        x, y, c, _ = _place()
        cps = []
        for w in range(n):
            half = ins[w].shape[1] // 2
            cp = _remote(ins[w].at[:, pl.ds((1 - c) * half, half), :], outs[w], send.at[w], recv.at[w], (x, y, 1 - c))
            cp.start()
            cps.append(cp)
        for cp in cps:
            cp.wait_recv()
        for cp in cps:
            cp.wait_send()

    return pl.pallas_call(
        body, name="pair_swap", in_specs=[ANY] * n, out_specs=[ANY] * n,
        out_shape=[jax.ShapeDtypeStruct((4, g.shape[1] // 2, g.shape[2]), g.dtype) for g in grads],
        scratch_shapes=[pltpu.SemaphoreType.DMA((n,))] * 2,
    )(*grads)


def chip_exchange(parts):
    n = len(parts)

    def body(*refs):
        ins, outs = refs[:n], refs[n:2 * n]
        send, recv = refs[2 * n:]
        x, y, c, chips = _place()
        me = 2 * x + y
        cps = []
        for w in range(n):
            for j, (ox, oy) in enumerate(chips):
                cp = _remote(ins[w].at[2 * ox + oy], outs[w].at[me], send.at[3 * w + j], recv.at[3 * w + j], (ox, oy, c))
                cp.start()
                cps.append(cp)
        for w in range(n):
            for j, (ox, oy) in enumerate(chips):
                slot = outs[w].at[2 * ox + oy]
                _remote(slot, slot, send.at[3 * w + j], recv.at[3 * w + j], (ox, oy, c)).wait_recv()
        for cp in cps:
            cp.wait_send()

    return pl.pallas_call(
        body, name="chip_exchange", in_specs=[ANY] * n, out_specs=[ANY] * n,
        out_shape=[jax.ShapeDtypeStruct(p.shape, p.dtype) for p in parts],
        scratch_shapes=[pltpu.SemaphoreType.DMA((3 * n,))] * 2,
    )(*parts)


def pair_send(halves):
    n = len(halves)

    def body(*refs):
        ins, outs = refs[:n], refs[n:2 * n]
        send, recv = refs[2 * n:]
        x, y, c, _ = _place()
        cps = [_remote(ins[w], outs[w], send.at[w], recv.at[w], (x, y, 1 - c)) for w in range(n)]
        for cp in cps:
            cp.start()
        for cp in cps:
            cp.wait_recv()
        for cp in cps:
            cp.wait_send()

    return pl.pallas_call(
        body, name="pair_send", in_specs=[ANY] * n, out_specs=[ANY] * n,
        out_shape=[jax.ShapeDtypeStruct(h.shape, h.dtype) for h in halves],
        scratch_shapes=[pltpu.SemaphoreType.DMA((n,))] * 2,
    )(*halves)


def all_reduce_small(name, vec):
    rows = vec.shape[0]

    def body(v_ref, o_ref, buf, send, recv):
        x, y, c, _ = _place()
        me = 4 * x + 2 * y + c
        buf[me] = v_ref[...]
        cps = []
        for k in range(1, 8):
            kx, ky, kc = (k >> 2) & 1, (k >> 1) & 1, k & 1
            peer = (x if kx == 0 else 1 - x, y if ky == 0 else 1 - y, c if kc == 0 else 1 - c)
            cp = _remote(v_ref, buf.at[me], send.at[k - 1], recv.at[k - 1], peer)
            cp.start()
            cps.append(cp)
        for k in range(1, 8):
            kx, ky, kc = (k >> 2) & 1, (k >> 1) & 1, k & 1
            px, py, pc = (x if kx == 0 else 1 - x, y if ky == 0 else 1 - y, c if kc == 0 else 1 - c)
            slot = buf.at[4 * px + 2 * py + pc]
            _remote(slot, slot, send.at[k - 1], recv.at[k - 1], (px, py, pc)).wait_recv()
        for cp in cps:
            cp.wait_send()
        acc = buf[0]
        for d in range(1, 8):
            acc = acc + buf[d]
        o_ref[...] = acc

    vm = pl.BlockSpec(memory_space=pltpu.VMEM)
    return pl.pallas_call(
        body, name=name, in_specs=[vm], out_specs=vm, out_shape=jax.ShapeDtypeStruct(vec.shape, F32),
        scratch_shapes=[pltpu.VMEM((8, rows, LANES), F32), pltpu.SemaphoreType.DMA((7,)), pltpu.SemaphoreType.DMA((7,))],
    )(vec)


def local_step(x2, tgt2, g1, g2, gdn_ng, qn_g, kn_g, p1, p2, conv_w, w_main, w_small, p_a, p_b, w_o, w_u, w_d, nseq, seq):
    rows, dm = x2.shape
    wide = NH * LANES
    row = lambda a, off=0, w=None: (a, "row", off, a.shape[1] if w is None else w)
    rowh = lambda a, off=0, w=LANES: (a, "rowh", off, w)
    par = lambda a: (a, "par", 0, a.shape[1])
    parh = lambda a, off=0: (a, "parh", off, LANES)
    o_row = lambda w, dt: (w, "row", w, dt)
    o_rowh = lambda dt, tw=wide, w=LANES: (tw, "rowh", w, dt)

    u, = ew_fwd("rms1", f_rms, [row(x2), par(g1)], [o_row(dm, BF16)], rows)
    proj = matmul("mm_in", u, w_main, "nn", F32, tn=1024)
    sp = matmul("mm_in_small", u, w_small, "nn", F32)
    so, = ew_fwd("small", f_small, [row(sp), par(p1), par(p2)], [o_row(LANES, F32)], rows)
    cs = cumsum_time("cumsum", so, nseq, seq, False)
    gb, bb, cb = ew_fwd("bcast", f_bcast, [row(so), row(cs)], [o_rowh(F32)] * 3, rows, NH)
    ct = transpose_time("c_time_major", cs, nseq, seq)
    conv = {}
    for mode, off in (("q", 0), ("k", NH), ("v", 2 * NH)):
        conv[mode], = ew_fwd(f"conv_{mode}", make_f_conv(mode), [rowh(proj, off), parh(conv_w, off)], [o_rowh(F32)],
                             rows, NH, seq, "hi", CONV_HEADS)
    val, kcum, attn, qdec, kdec, t_inv = gdn_a_fwd(conv["q"], conv["k"], conv["v"], gb, bb, rows)
    o_a, snaps = gdn_b_fwd(val, kcum, attn, qdec, kdec, gb, nseq, seq)
    ya_in, = ew_fwd("gdn_post", f_post, [rowh(o_a), rowh(proj, 3 * NH), par(gdn_ng)], [o_rowh(BF16)], rows, NH)
    fqn, = ew_fwd("fox_qn", f_rms, [rowh(proj, FOX_Q), par(qn_g)], [o_rowh(BF16)], rows, NH)
    fkn, = ew_fwd("fox_kn", f_rms, [rowh(proj, FOX_K), par(kn_g)], [o_rowh(BF16)], rows, NH)
    o_b, o_b16, lse = fox_fwd(fqn, fkn, proj, ct, nseq, seq)
    y_a = matmul("mm_pa", ya_in, p_a, "nn", F32, tn=1024)
    y_b = matmul("mm_pb", o_b16, p_b, "nn", F32, tn=1024)
    gates = [row(proj, 7, dm), row(proj, 8, dm)]
    merged, = ew_fwd("merge", f_merge, gates + [row(y_a), row(y_b)], [o_row(dm, BF16)], rows)
    hres = matmul("mm_out", merged, w_o, "nn", F32, add=x2, tn=1024)
    hn, = ew_fwd("rms2", f_rms, [row(hres), par(g2)], [o_row(dm, BF16)], rows)
    dff = w_u.shape[1]
    act = matmul("mm_up", hn, w_u, "nn", F32, tn=1024)
    nff = dff // dm
    relu2, = ew_fwd("relu2", f_relu2, [rowh(act, 0, dm)], [o_rowh(BF16, dff, dm)], rows, nff, ROW_TILE // 2)
    out = matmul("mm_down", relu2, w_d, "nn", F32, add=hres, tn=1024)
    dout, dout16, loss_acc = loss_head(out, tgt2, rows, dm)

    g_first = lambda dt=F32: (lambda g, e: [g[0]])
    d_relu2 = matmul("mm_d_relu2", dout16, w_d, "nt", F32, tn=1024)
    dw_d = matmul("mm_dw_down", relu2, dout16, "tn", F32, tn=1024)
    d_act, = ew_bwd("relu2_b", f_relu2, [rowh(act, 0, dm)], [(rowh(d_relu2, 0, dm),)], [], lambda g, e: [g[0]],
                    [((rows, dff), "rowh", dm, BF16, None)], rows, nff, ROW_TILE // 2)
    dw_u = matmul("mm_dw_up", hn, d_act, "tn", F32, tn=1024)
    d_hn = matmul("mm_d_hn", d_act, w_u, "nt", F32, tn=1024)
    dh, dh16, dg2 = ew_bwd("rms2_b", f_rms, [row(hres), par(g2)], [(row(d_hn),)], [row(dout)],
                           lambda g, e: [g[0] + e[0], g[0] + e[0], g[1]],
                           [((rows, dm), "row", dm, F32, None), ((rows, dm), "row", dm, BF16, None), ((1, dm), "par", dm, F32, "all")], rows)
    d_merged = matmul("mm_d_merged", dh16, w_o, "nt", F32, tn=1024)
    dw_o = matmul("mm_dw_out", merged, dh16, "tn", F32, tn=1024)
    seg16 = ((rows, dm), "row", dm, BF16, None)
    d_ga16, d_gb16, d_ya16, d_yb16 = ew_bwd("merge_b", f_merge, gates + [row(y_a), row(y_b)], [(row(d_merged),)], [],
                                            lambda g, e: list(g), [seg16] * 4, rows)
    dp_a = matmul("mm_dp_a", ya_in, d_ya16, "tn", F32, tn=1024)
    d_ya_in = matmul("mm_d_ya_in", d_ya16, p_a, "nt", F32, tn=1024)
    dp_b = matmul("mm_dp_b", o_b16, d_yb16, "tn", F32, tn=1024)
    d_ob = matmul("mm_d_ob", d_yb16, p_b, "nt", F32, tn=1024)
    h32 = ((rows, wide), "rowh", LANES, F32, None)
    h16 = ((rows, wide), "rowh", LANES, BF16, None)
    gain = ((1, LANES), "par", LANES, F32, "all")
    d_oa, d_z16, d_gdn_ng = ew_bwd("gdn_post_b", f_post, [rowh(o_a), rowh(proj, 3 * NH), par(gdn_ng)], [(rowh(d_ya_in),)], [],
                                   lambda g, e: list(g), [h32, h16, gain], rows, NH)
    dval, dkc, dat, dqd, dkd, dgb_b = gdn_b_bwd(val, kcum, attn, qdec, kdec, gb, snaps, d_oa, nseq, seq)
    d_cq, d_ck, d_cv, d_gb, d_bb = gdn_a_bwd(conv["q"], conv["k"], conv["v"], gb, bb, t_inv, dval, dkc, dat, dqd, dkd, dgb_b, rows)
    d_pre, d_conv = {}, {}
    tap = ((4, wide), "parh", LANES, F32, "inner")
    for mode, off, ctg in (("q", 0, d_cq), ("k", NH, d_ck), ("v", 2 * NH, d_cv)):
        d_pre[mode], d_conv[mode] = ew_bwd(f"conv_{mode}_b", make_f_conv(mode), [rowh(proj, off), parh(conv_w, off)],
                                           [(rowh(ctg),)], [], lambda g, e: list(g), [h16, tap], rows, NH, seq, "hi", CONV_HEADS)
    delta, = ew_fwd("fox_delta", f_delta, [rowh(d_ob), rowh(o_b)], [o_rowh(F32)], rows, NH)
    d_fqn, d_cq_b = fox_dq(fqn, fkn, proj, ct, d_ob, lse, delta, nseq, seq)
    d_fkn, d_fv16, d_ck_b = fox_dkv(fqn, fkn, proj, cb, d_ob, lse, delta, nseq, seq)
    d_fq16, d_qn_g = ew_bwd("fox_qn_b", f_rms, [rowh(proj, FOX_Q), par(qn_g)], [(rowh(d_fqn),)], [], lambda g, e: list(g),
                            [h16, gain], rows, NH)
    d_fk16, d_kn_g = ew_bwd("fox_kn_b", f_rms, [rowh(proj, FOX_K), par(kn_g)], [(rowh(d_fkn),)], [], lambda g, e: list(g),
                            [h16, gain], rows, NH)
    narrow = ((rows, LANES), "row", LANES, F32, None)
    d_so, d_cs = ew_bwd("bcast_b", f_bcast, [row(so), row(cs)], [(rowh(d_gb),), (rowh(d_bb),), (rowh(d_cq_b), rowh(d_ck_b))], [],
                        lambda g, e: list(g), [narrow, narrow], rows, NH)
    d_logf = cumsum_time("cumsum_b", d_cs, nseq, seq, True)
    vec = ((1, LANES), "par", LANES, F32, "all")
    d_sp16, d_p1, d_p2 = ew_bwd("small_b", f_small, [row(sp), par(p1), par(p2)], [(row(d_so), row(d_logf))], [],
                                lambda g, e: list(g), [((rows, LANES), "row", LANES, BF16, None), vec, vec], rows)
    d_proj16 = jnp.concatenate([d_pre["q"], d_pre["k"], d_pre["v"], d_z16, d_fq16, d_fk16, d_fv16, d_ga16, d_gb16], axis=1)
    dw_main = matmul("mm_dw_main", u, d_proj16, "tn", F32, tn=1024)
    dw_small = matmul("mm_dw_small", u, d_sp16, "tn", F32)
    d_u = matmul("mm_d_u_small", d_sp16, w_small, "nt", F32, tn=1024)
    d_u = matmul("mm_d_u", d_proj16, w_main, "nt", F32, add=d_u, tn=1024)
    dx, dg1 = ew_bwd("rms1_b", f_rms, [row(x2), par(g1)], [(row(d_u),)], [row(dh)], lambda g, e: [g[0] + e[0], g[1]],
                     [((rows, dm), "row", dm, F32, None), ((1, dm), "par", dm, F32, "all")], rows)
    d_conv_w = jnp.concatenate([d_conv["q"], d_conv["k"], d_conv["v"]], axis=1)
    return dict(loss_acc=loss_acc, dx=dx, g1=dg1, g2=dg2, gdn_ng=d_gdn_ng, qn=d_qn_g, kn=d_kn_g, p1=d_p1, p2=d_p2,
                conv=d_conv_w, w_main=dw_main, w_small=dw_small, p_a=dp_a, p_b=dp_b, w_o=dw_o, w_u=dw_u, w_d=dw_d)


_W = NH * LANES
_A0, _A1 = 4 * _W, 4 * _W + 2 * NH
_B0, _B1 = _A1 + 3 * _W, _A1 + 3 * _W + NH
N_IN = _B1 + 2 * _W


def _split_w_in(full):
    main = jnp.concatenate([full[:, :_A0], full[:, _A1:_B0], full[:, _B1:]], axis=1)
    small = jnp.concatenate([full[:, _A0:_A1], full[:, _B0:_B1], jnp.zeros((full.shape[0], LANES - 3 * NH), full.dtype)], axis=1)
    return main, small


def _join_w_in(main, small):
    return jnp.concatenate([main[:, :_A0], small[:, :2 * NH], main[:, _A0:_A0 + 3 * _W], small[:, 2 * NH:3 * NH],
                            main[:, _A0 + 3 * _W:]], axis=1)


def _lanes(v, at=0):
    return jnp.pad(v.reshape(1, -1), ((0, 0), (at, LANES - at - v.size)))


def kernel(x, norm_mix_g, w_in, gdn_conv_w, gdn_a_log, gdn_dt_bias, gdn_norm_g, fox_q_norm_g, fox_k_norm_g, fox_f_bias, w_proj_gdn, w_proj_fox, w_out, norm_mlp_g, w_up, w_down, loss_target, m_norm_mix_g, m_w_in, m_gdn_conv_w, m_gdn_a_log, m_gdn_dt_bias, m_gdn_norm_g, m_fox_q_norm_g, m_fox_k_norm_g, m_fox_f_bias, m_w_proj_gdn, m_w_proj_fox, m_w_out, m_norm_mlp_g, m_w_up, m_w_down, v_norm_mix_g, v_w_in, v_gdn_conv_w, v_gdn_a_log, v_gdn_dt_bias, v_gdn_norm_g, v_fox_q_norm_g, v_fox_k_norm_g, v_fox_f_bias, v_w_proj_gdn, v_w_proj_fox, v_w_out, v_norm_mlp_g, v_w_up, v_w_down):
    nseq, seq, dm = x.shape
    rows = nseq * seq
    xi, yi, ci = lax.axis_index("x"), lax.axis_index("y"), lax.axis_index("c")
    chip = 2 * xi + yi
    conv_cols = gdn_conv_w.shape[2]

    big = [w_in[0], w_proj_gdn[0], w_proj_fox[0], w_out[0], w_up[0], w_down[0]]
    big16 = [w.astype(BF16) for w in big]
    g_in, g_pa, g_pb, g_wo, g_wu, g_wd = (lax.dynamic_update_index_in_dim(got, own, chip, 0)
                                          for got, own in zip(gather_weights(big16), big16))
    w_main, w_small = _split_w_in(g_in.transpose(1, 0, 2).reshape(dm, -1))
    p_a, p_b, w_o = (g.reshape(-1, dm) for g in (g_pa, g_pb, g_wo))
    w_u = g_wu.transpose(1, 0, 2).reshape(dm, -1)
    w_d = g_wd.reshape(-1, dm)
    conv_slot = jnp.zeros((4, 4, conv_cols), F32).at[:, chip].set(jnp.where(ci == 0, gdn_conv_w[0], 0.0))
    conv_full = all_reduce_small("gather_conv", conv_slot.reshape(-1, LANES)).reshape(4, 4 * conv_cols)
    p1 = _lanes(gdn_dt_bias[0]) + _lanes(fox_f_bias[0], 2 * NH)
    p2 = _lanes(gdn_a_log[0])

    g = local_step(x.reshape(rows, dm), loss_target.reshape(rows, dm), norm_mix_g, norm_mlp_g, gdn_norm_g, fox_q_norm_g,
                   fox_k_norm_g, p1, p2, conv_full, w_main, w_small, p_a, p_b, w_o, w_u, w_d, nseq, seq)

    small_parts = [g["loss_acc"], g["g1"].reshape(8, LANES), g["g2"].reshape(8, LANES), g["gdn_ng"], g["qn"], g["kn"], g["p1"], g["p2"],
                   g["conv"].reshape(-1, LANES)]
    tiled = [jnp.pad(p, ((0, -p.shape[0] % 8), (0, 0))) for p in small_parts]
    red = all_reduce_small("reduce_small", jnp.concatenate(tiled, axis=0))
    pos, red_parts = 0, []
    for p, t in zip(small_parts, tiled):
        red_parts.append(red[pos:pos + p.shape[0]])
        pos += t.shape[0]
    r_loss, r_g1, r_g2, r_gdn_ng, r_qn, r_kn, r_p1, r_p2, r_conv = red_parts
    loss = jnp.sum(r_loss)
    g_conv = lax.dynamic_slice_in_dim(r_conv.reshape(4, 4, conv_cols), chip, 1, axis=1).reshape(4, conv_cols)
    small_grads = [r_g1.reshape(1, dm), r_p2[:, :NH], r_p1[:, :NH], r_gdn_ng, r_qn, r_kn, r_p1[:, 2 * NH:3 * NH], r_g2.reshape(1, dm)]
    small_w = [norm_mix_g, gdn_a_log, gdn_dt_bias, gdn_norm_g, fox_q_norm_g, fox_k_norm_g, fox_f_bias, norm_mlp_g]
    small_m = [m_norm_mix_g, m_gdn_a_log, m_gdn_dt_bias, m_gdn_norm_g, m_fox_q_norm_g, m_fox_k_norm_g, m_fox_f_bias, m_norm_mlp_g]
    small_v = [v_norm_mix_g, v_gdn_a_log, v_gdn_dt_bias, v_gdn_norm_g, v_fox_q_norm_g, v_fox_k_norm_g, v_fox_f_bias, v_norm_mlp_g]

    def pack(parts):
        flat = jnp.concatenate([jnp.pad(p.reshape(-1), (0, -p.size % LANES)) for p in parts])
        return jnp.pad(flat, (0, -flat.size % (8 * LANES))).reshape(-1, LANES)

    packed = adamw("adamw_small", pack(small_w + [gdn_conv_w[0]]), pack(small_grads + [g_conv]),
                   pack(small_m + [m_gdn_conv_w[0]]), pack(small_v + [v_gdn_conv_w[0]]))

    def unpack(flat2d):
        flat, pos, res = flat2d.reshape(-1), 0, []
        for p in small_w + [gdn_conv_w[0]]:
            res.append(flat[pos:pos + p.size].reshape(p.shape))
            pos += p.size + (-p.size % LANES)
        return res

    s_delta, s_m, s_v = (unpack(a) for a in packed)

    dw_in = _join_w_in(g["w_main"], g["w_small"])
    blocks = [dw_in.reshape(dm, 4, -1).transpose(1, 0, 2), g["p_a"].reshape(4, -1, dm), g["p_b"].reshape(4, -1, dm),
              g["w_o"].reshape(4, -1, dm), g["w_u"].reshape(dm, 4, -1).transpose(1, 0, 2), g["w_d"].reshape(4, -1, dm)]
    core = ci.reshape(1).astype(jnp.int32)
    swapped = pair_swap(blocks)
    chip_part = add_pair("add_pair", blocks, swapped, core)
    slots = chip_exchange(chip_part)
    halves = add_chips("add_chips", slots, chip_part, chip.reshape(1).astype(jnp.int32))
    others = pair_send(halves)
    big_m = [m_w_in[0], m_w_proj_gdn[0], m_w_proj_fox[0], m_w_out[0], m_w_up[0], m_w_down[0]]
    big_v = [v_w_in[0], v_w_proj_gdn[0], v_w_proj_fox[0], v_w_out[0], v_w_up[0], v_w_down[0]]
    names = ["w_in", "w_proj_gdn", "w_proj_fox", "w_out", "w_up", "w_down"]
    big_res, big_grad = {}, {}
    for nm, w, mine, other, m, v in zip(names, big, halves, others, big_m, big_v):
        big_grad[nm], *big_res[nm] = adamw_halves(f"adamw_{nm}", w, mine, other, m, v, core)

    order = ["norm_mix_g", "w_in", "gdn_conv_w", "gdn_a_log", "gdn_dt_bias", "gdn_norm_g", "fox_q_norm_g", "fox_k_norm_g",
             "fox_f_bias", "w_proj_gdn", "w_proj_fox", "w_out", "norm_mlp_g", "w_up", "w_down"]
    small_names = ["norm_mix_g", "gdn_a_log", "gdn_dt_bias", "gdn_norm_g", "fox_q_norm_g", "fox_k_norm_g", "fox_f_bias", "norm_mlp_g",
                   "gdn_conv_w"]
    small_idx = {nm: i for i, nm in enumerate(small_names)}
    shapes = dict(zip(order, (a.shape for a in (norm_mix_g, w_in, gdn_conv_w, gdn_a_log, gdn_dt_bias, gdn_norm_g, fox_q_norm_g,
                                                 fox_k_norm_g, fox_f_bias, w_proj_gdn, w_proj_fox, w_out, norm_mlp_g, w_up, w_down))))
    grads_out, delta_out, m_out, v_out = [], [], [], []
    for nm in order:
        if nm in big_res:
            d, mm, vv = big_res[nm]
            gr = big_grad[nm]
        else:
            i = small_idx[nm]
            gr = (small_grads + [g_conv])[i]
            d, mm, vv = s_delta[i], s_m[i], s_v[i]
        for lst, val in ((grads_out, gr), (delta_out, d), (m_out, mm), (v_out, vv)):
            lst.append(val.reshape(shapes[nm]))
    return (loss, g["dx"].reshape(x.shape), *grads_out, *delta_out, *m_out, *v_out)
```

```python
import functools

import jax
import jax.numpy as jnp
from jax import lax
from jax.experimental import pallas as pl
from jax.experimental.pallas import tpu as pltpu

F32 = jnp.float32
BF16 = jnp.bfloat16
LANES = 128
NH = 8
EPS = 1e-6
GDN_CHUNK = 64
GDN_ROWS = 256
GDN_BASE = 16
ROW_TILE = 512
CONV_HEADS = 2
ATT_TILE = 512
NEG = -1e30
VMEM_LIMIT_BYTES = 48 * 1024 * 1024
HI = lax.Precision.HIGHEST
LO = lax.Precision.DEFAULT
MESH = pl.DeviceIdType.MESH
ANY = pl.BlockSpec(memory_space=pl.ANY)

ADAM_LR, ADAM_B1, ADAM_B2, ADAM_EPS, ADAM_WD, ADAM_STEP = 0.001, 0.9, 0.999, 1e-08, 0.01, 10


def _params(n_grid):
    return pltpu.CompilerParams(dimension_semantics=("arbitrary",) * n_grid,
                                vmem_limit_bytes=VMEM_LIMIT_BYTES)


def _dot(a, b, dims, precision=None):
    dn = {"nn": (((1,), (0,)), ((), ())), "nt": (((1,), (1,)), ((), ())), "tn": (((0,), (0,)), ((), ()))}[dims]
    return lax.dot_general(a, b, dn, precision=precision, preferred_element_type=F32)


def _iota(shape, dim):
    return lax.broadcasted_iota(jnp.int32, shape, dim)


def _split(x, parts):
    out = []
    for _ in range(parts - 1):
        hi = x.astype(BF16)
        out.append(hi)
        x = x - hi.astype(F32)
    return out + [x.astype(BF16)]


def _dot_split(a, b, dims, a_exact=False):
    if a_exact:
        a16 = a.astype(BF16)
        b1, b2, b3 = _split(b, 3)
        return _dot(a16, b1, dims) + (_dot(a16, b2, dims) + _dot(a16, b3, dims))
    (ah, al), (bh, bl) = _split(a, 2), _split(b, 2)
    return _dot(ah, bh, dims) + (_dot(ah, bl, dims) + _dot(al, bh, dims))


@jax.custom_vjp
def mm_split(a, b):
    return _dot_split(a, b, "nn")


mm_split.defvjp(lambda a, b: (_dot_split(a, b, "nn"), (a, b)),
                lambda res, g: (_dot_split(g, res[1], "nt"), _dot_split(res[0], g, "tn")))


@jax.custom_vjp
def mm_mask(mask, b):
    return _dot_split(mask, b, "nn", True)


mm_mask.defvjp(lambda mask, b: (_dot_split(mask, b, "nn", True), mask),
               lambda mask, g: (jnp.zeros_like(mask), _dot_split(mask, g, "tn", True)))


def matmul(name, a, b, dims, out_dtype, add=None, tm=1024, tn=1024, tk=512, col_blocks=None):
    if col_blocks and dims != "tn":
        nb, b_rows, bw = b.shape
        b_shape = (b_rows, nb * bw)
    else:
        b_shape = b.shape
    if dims == "nn":
        (m, k), (_, n) = a.shape, b_shape
    elif dims == "nt":
        (m, k), (n, _) = a.shape, b_shape
    else:
        (k, m), (_, n) = a.shape, b_shape
    if k <= 1024:
        tk = k
    tm, tn, tk = min(tm, m), min(tn, n), min(tk, k)
    assert m % tm == 0 and n % tn == 0 and k % tk == 0, (name, m, n, k)
    nk = k // tk
    a_spec = pl.BlockSpec((tk, tm), lambda i, j, kk: (kk, i)) if dims == "tn" else pl.BlockSpec((tm, tk), lambda i, j, kk: (i, kk))
    b_spec = pl.BlockSpec((tn, tk), lambda i, j, kk: (j, kk)) if dims == "nt" else pl.BlockSpec((tk, tn), lambda i, j, kk: (kk, j))
    o_spec = pl.BlockSpec((tm, tn), lambda i, j, kk: (i, j))
    out_shape = (m, n)
    if col_blocks and dims == "nn":
        per = bw // tn
        assert bw % tn == 0
        b_spec = pl.BlockSpec((None, tk, tn), lambda i, j, kk: (j // per, kk, j % per))
    elif col_blocks and dims == "nt":
        per = bw // tk
        assert bw % tk == 0
        b_spec = pl.BlockSpec((None, tn, tk), lambda i, j, kk: (kk // per, j, kk % per))
    elif col_blocks:
        bw = n // col_blocks
        per = bw // tn
        assert bw % tn == 0 and add is None
        o_spec = pl.BlockSpec((None, tm, tn), lambda i, j, kk: (j // per, i, j % per))
        out_shape = (col_blocks, m, bw)
    has_add = add is not None

    def body(*refs):
        a_ref, b_ref = refs[0], refs[1]
        add_ref = refs[2] if has_add else None
        o_ref = refs[3] if has_add else refs[2]

        def finish(r):
            if has_add:
                r = r + add_ref[...]
            o_ref[...] = r.astype(o_ref.dtype)

        if nk == 1:
            finish(_dot(a_ref[...], b_ref[...], dims))
            return
        acc_ref = refs[-1]
        kk = pl.program_id(2)

        @pl.when(kk == 0)
        def _():
            acc_ref[...] = jnp.zeros_like(acc_ref)

        acc_ref[...] += _dot(a_ref[...], b_ref[...], dims)

        @pl.when(kk == nk - 1)
        def _():
            finish(acc_ref[...])

    ins = [a, b] + ([add] if has_add else [])
    in_specs = [a_spec, b_spec] + ([o_spec] if has_add else [])
    return pl.pallas_call(
        body, name=name, grid=(m // tm, n // tn, nk), in_specs=in_specs, out_specs=o_spec,
        out_shape=jax.ShapeDtypeStruct(out_shape, out_dtype),
        scratch_shapes=[pltpu.VMEM((tm, tn), F32)] if nk > 1 else [], compiler_params=_params(3),
    )(*ins)


def _ew_spec(kind, off, width, tb, hp, order, shape=None):
    def ih(g0, g1):
        return (g0, g1) if order == "ih" else (g1, g0)

    assert off % hp == 0 or kind in ("row", "par")
    if kind == "row":
        return pl.BlockSpec((tb, width), lambda g0, g1: (ih(g0, g1)[0], off))
    if kind == "rowh":
        return pl.BlockSpec((tb, hp * width), lambda g0, g1: (ih(g0, g1)[0], ih(g0, g1)[1] + off // hp))
    if kind == "par":
        return pl.BlockSpec(shape, lambda g0, g1: (0, 0))
    if kind == "parh":
        return pl.BlockSpec((shape[0], hp * width), lambda g0, g1: (0, ih(g0, g1)[1] + off // hp))
    raise ValueError(kind)


def _ew_grid(rows, tb, nh, hp, order):
    assert nh % hp == 0 and rows % tb == 0
    return (rows // tb, nh // hp) if order == "ih" else (nh // hp, rows // tb)


def _ew_load(ref, kind, width, hh):
    if kind in ("row", "par"):
        return ref[...].astype(F32)
    return ref[:, hh * width:(hh + 1) * width].astype(F32)


def ew_fwd(name, f, ins, outs, rows, nh=1, tb=ROW_TILE, order="ih", hp=None):
    hp = nh if hp is None else hp
    n_in = len(ins)

    def body(*refs):
        hb = pl.program_id(1) if order == "ih" else pl.program_id(0)
        for hh in range(hp):
            h = hh if hp == nh else hb * hp + hh
            vals = [_ew_load(r, kd, w, hh) for r, (_, kd, _, w) in zip(refs[:n_in], ins)]
            res = f(h, *vals)
            for r, v, (_, kd, w, _) in zip(refs[n_in:], res, outs):
                if kd == "row":
                    assert hp == 1
                    r[...] = v.astype(r.dtype)
                else:
                    r[:, hh * w:(hh + 1) * w] = v.astype(r.dtype)

    in_specs = [_ew_spec(kd, off, w, tb, hp, order, a.shape) for (a, kd, off, w) in ins]
    out_specs = [_ew_spec(kd, 0, w, tb, hp, order) for (_, kd, w, _) in outs]
    out_shape = [jax.ShapeDtypeStruct((rows, tw), dt) for (tw, _, _, dt) in outs]
    return pl.pallas_call(
        body, name=name, grid=_ew_grid(rows, tb, nh, hp, order), in_specs=in_specs, out_specs=out_specs,
        out_shape=out_shape, compiler_params=_params(2),
    )(*[a for (a, _, _, _) in ins])


def ew_bwd(name, f, ins, cts, extras, emit, outs, rows, nh=1, tb=ROW_TILE, order="ih", hp=None):
    hp = nh if hp is None else hp
    n_in = len(ins)
    flat_cts = [d for group in cts for d in group]
    n_ct, n_ex = len(flat_cts), len(extras)

    def body(*refs):
        g0, g1 = pl.program_id(0), pl.program_id(1)
        hb = g1 if order == "ih" else g0
        out_refs = refs[n_in + n_ct + n_ex:]
        shared = [None] * len(outs)

        def store(r, v, first, sl=None):
            def put(val, add):
                if sl is None:
                    r[...] = (r[...] + val if add else val).astype(r.dtype)
                else:
                    r[:, sl] = (r[:, sl] + val if add else val).astype(r.dtype)

            if first is None:
                put(v, False)
            else:
                pl.when(first)(lambda: put(v, False))
                pl.when(jnp.logical_not(first))(lambda: put(v, True))

        for hh in range(hp):
            h = hh if hp == nh else hb * hp + hh
            vals = [_ew_load(r, kd, w, hh) for r, (_, kd, _, w) in zip(refs[:n_in], ins)]
            ct_refs = list(zip(refs[n_in:n_in + n_ct], flat_cts))
            ct_vals, pos = [], 0
            for group in cts:
                v = None
                for r, (_, kd, _, w) in ct_refs[pos:pos + len(group)]:
                    t = _ew_load(r, kd, w, hh)
                    v = t if v is None else v + t
                pos += len(group)
                ct_vals.append(v)
            ex_vals = [_ew_load(r, kd, w, hh) for r, (_, kd, _, w) in zip(refs[n_in + n_ct:n_in + n_ct + n_ex], extras)]
            _, vjp = jax.vjp(lambda *a: f(h, *a), *vals)
            res = emit(vjp(tuple(ct_vals)), ex_vals)
            for idx, (r, v, (_, kd, w, _, acc)) in enumerate(zip(out_refs, res, outs)):
                if kd in ("row", "par"):
                    shared[idx] = v if shared[idx] is None else shared[idx] + v
                else:
                    store(r, v, (g1 == 0) if acc == "inner" else None, slice(hh * w, (hh + 1) * w))
        for idx, (r, (_, kd, _, _, acc)) in enumerate(zip(out_refs, outs)):
            if kd in ("row", "par"):
                assert acc == "all" or hp == nh
                store(r, shared[idx], jnp.logical_and(g0 == 0, g1 == 0) if acc == "all" else None)

    operands = list(ins) + flat_cts + list(extras)
    in_specs = [_ew_spec(kd, off, w, tb, hp, order, a.shape) for (a, kd, off, w) in operands]
    out_specs = [_ew_spec(kd, 0, w, tb, hp, order, shp) for (shp, kd, w, _, _) in outs]
    out_shape = [jax.ShapeDtypeStruct(shp, dt) for (shp, _, _, dt, _) in outs]
    return pl.pallas_call(
        body, name=name, grid=_ew_grid(rows, tb, nh, hp, order), in_specs=in_specs, out_specs=out_specs,
        out_shape=out_shape, compiler_params=_params(2),
    )(*[a for (a, _, _, _) in operands])


def f_rms(h, x, g):
    r = lax.rsqrt(jnp.mean(x * x, axis=-1, keepdims=True) + EPS)
    return (x * r * g,)


def _softplus(z):
    return jnp.maximum(z, 0.0) + jnp.log1p(jnp.exp(-jnp.abs(z)))


def f_small(h, sp, p1, p2):
    lane = _iota(sp.shape, 1)
    z = sp + p1
    g = -jnp.exp(p2) * _softplus(z)
    beta = jax.nn.sigmoid(z)
    logf = -_softplus(-z)
    return (jnp.where(lane < NH, g, jnp.where(lane < 2 * NH, beta, jnp.where(lane < 3 * NH, logf, 0.0))),)


def _pick(x, lane_id):
    lane = _iota(x.shape, 1)
    col = jnp.sum(jnp.where(lane == lane_id, x, 0.0), axis=1, keepdims=True)
    return jnp.broadcast_to(col, x.shape)


def f_bcast(h, so, cs):
    return _pick(so, h), _pick(so, h + NH), _pick(cs, h + 2 * NH)


def _shift_down(s):
    def down(x):
        return jnp.where(_iota(x.shape, 0) >= s, pltpu.roll(x, s, 0), 0.0)

    def up(g):
        n = g.shape[0]
        return jnp.where(_iota(g.shape, 0) < n - s, pltpu.roll(g, n - s, 0), 0.0)

    @jax.custom_vjp
    def shift(x):
        return down(x)

    shift.defvjp(lambda x: (down(x), None), lambda _, g: (up(g),))
    return shift


def _silu(x):
    return x * jax.nn.sigmoid(x)


def make_f_conv(mode):
    sh1, sh2, sh3 = _shift_down(1), _shift_down(2), _shift_down(3)

    def f(h, x, w):
        sub = _iota(w.shape, 0)

        def tap(i):
            return jnp.sum(jnp.where(sub == i, w, 0.0), axis=0, keepdims=True)

        y = sh3(x) * tap(0)
        y = y + sh2(x) * tap(1)
        y = y + sh1(x) * tap(2)
        y = y + x * tap(3)
        s = _silu(y)
        if mode == "v":
            return (s,)
        n = s * lax.rsqrt(jnp.sum(s * s, axis=-1, keepdims=True) + EPS)
        if mode == "q":
            n = n * (LANES ** -0.5)
        return (n,)

    return f


def f_post(h, o, z, g):
    r = lax.rsqrt(jnp.mean(o * o, axis=-1, keepdims=True) + EPS)
    return (o * r * g * _silu(z),)


def f_merge(h, ga, gb, ya, yb):
    return (jax.nn.sigmoid(ga) * ya + jax.nn.sigmoid(gb) * yb,)


def f_relu2(h, a):
    m = jnp.maximum(a, 0.0)
    return (m * m,)


def f_delta(h, do, o):
    return (jnp.broadcast_to(jnp.sum(do * o, axis=1, keepdims=True), o.shape),)


def cumsum_time(name, x, nseq, seq, reverse):
    nb = seq // LANES

    def body(x_ref, o_ref):
        r, c = _iota((LANES, LANES), 0), _iota((LANES, LANES), 1)
        tri = jnp.where((r <= c) if reverse else (r >= c), 1.0, 0.0).astype(F32)
        carry = jnp.zeros((1, LANES), F32)
        for b in (range(nb - 1, -1, -1) if reverse else range(nb)):
            blk = x_ref[b * LANES:(b + 1) * LANES, :]
            o_ref[b * LANES:(b + 1) * LANES, :] = _dot_split(tri, blk, "nn", True) + carry
            carry = carry + jnp.sum(blk, axis=0, keepdims=True)

    spec = pl.BlockSpec((seq, LANES), lambda s: (s, 0))
    return pl.pallas_call(body, name=name, grid=(nseq,), in_specs=[spec], out_specs=spec,
                          out_shape=jax.ShapeDtypeStruct(x.shape, F32), compiler_params=_params(1))(x)


def transpose_time(name, x, nseq, seq):
    def body(x_ref, o_ref):
        o_ref[...] = x_ref[...].T

    return pl.pallas_call(
        body, name=name, grid=(nseq,), in_specs=[pl.BlockSpec((seq, LANES), lambda s: (s, 0))],
        out_specs=pl.BlockSpec((LANES, seq), lambda s: (s, 0)),
        out_shape=jax.ShapeDtypeStruct((nseq * LANES, seq), F32), compiler_params=_params(1))(x)


def _gdn_masks():
    n = GDN_ROWS
    r, c = _iota((n, n), 0), _iota((n, n), 1)
    shift = GDN_CHUNK.bit_length() - 1
    same = lax.shift_right_logical(r, shift) == lax.shift_right_logical(c, shift)
    return r, c, same


def _gdn_decay(gb):
    r, c, same = _gdn_masks()
    seg_tril = jnp.where(jnp.logical_and(same, r >= c), 1.0, 0.0).astype(F32)
    g_cum = mm_mask(seg_tril, gb)
    lane0 = _iota(g_cum.shape, 1) == 0
    g_col = jnp.sum(jnp.where(lane0, g_cum, 0.0), axis=1, keepdims=True)
    g_row = jnp.sum(jnp.where(r == c, jnp.broadcast_to(g_col, (GDN_ROWS, GDN_ROWS)), 0.0), axis=0, keepdims=True)
    return g_cum, g_col - g_row


def gdn_f1(q, k, gb, bb):
    r, c, same = _gdn_masks()
    strict = jnp.logical_and(same, r > c)
    _, diff = _gdn_decay(gb)
    lane0 = _iota(bb.shape, 1) == 0
    beta_col = jnp.sum(jnp.where(lane0, bb, 0.0), axis=1, keepdims=True)
    kk = _dot(k, k, "nt", LO)
    return jnp.where(strict, beta_col * kk * jnp.exp(jnp.where(strict, diff, 0.0)), 0.0)


def gdn_f2(t_inv, q, k, v, gb, bb):
    r, c, same = _gdn_masks()
    incl = jnp.logical_and(same, r >= c)
    g_cum, diff = _gdn_decay(gb)
    decay = jnp.where(incl, jnp.exp(jnp.where(incl, diff, 0.0)), 0.0)
    e_g = jnp.exp(g_cum)
    value = mm_split(t_inv, v * bb)
    k_cum = mm_split(t_inv, k * bb * e_g)
    attn = _dot(q, k, "nt", LO) * decay
    g_last = mm_mask(jnp.where(same, 1.0, 0.0).astype(F32), gb)
    return value, k_cum, attn, q * e_g, k * jnp.exp(g_last - g_cum)


def tri_inverse(a):
    n = GDN_ROWS
    r, c = _iota((n, n), 0), _iota((n, n), 1)
    shift = GDN_BASE.bit_length() - 1
    blk = lax.shift_right_logical(r, shift) == lax.shift_right_logical(c, shift)
    eye = jnp.where(r == c, 1.0, 0.0).astype(F32)
    d = jnp.where(blk, a, 0.0)
    lo = a - d
    p = -d
    t_d = eye + p
    steps = shift - 1
    for _ in range(steps):
        p = _dot_split(p, p, "nn")
        t_d = t_d + _dot_split(t_d, p, "nn")
    assert GDN_CHUNK // GDN_BASE == 4
    nmat = _dot_split(t_d, lo, "nn")
    n2 = _dot_split(nmat, nmat, "nn")
    t_n = (eye - nmat) + _dot_split(eye - nmat, n2, "nn")
    return _dot_split(t_n, t_d, "nn")


def gdn_a_fwd(q, k, v, gb, bb, rows):
    blk = pl.BlockSpec((GDN_ROWS, LANES), lambda i, h: (i, h))
    sq = pl.BlockSpec((GDN_ROWS, GDN_ROWS), lambda i, h: (i, h))

    def body(q_ref, k_ref, v_ref, gb_ref, bb_ref, val_ref, kc_ref, at_ref, qd_ref, kd_ref, t_ref):
        qv, kv, vv, gv, bv = q_ref[...], k_ref[...], v_ref[...], gb_ref[...], bb_ref[...]
        t_inv = tri_inverse(gdn_f1(qv, kv, gv, bv))
        value, k_cum, attn, q_dec, k_dec = gdn_f2(t_inv, qv, kv, vv, gv, bv)
        val_ref[...], kc_ref[...], at_ref[...], qd_ref[...], kd_ref[...], t_ref[...] = value, k_cum, attn, q_dec, k_dec, t_inv

    wide = jax.ShapeDtypeStruct((rows, NH * LANES), F32)
    square = jax.ShapeDtypeStruct((rows, NH * GDN_ROWS), F32)
    return pl.pallas_call(
        body, name="gdn_a_fwd", grid=(rows // GDN_ROWS, NH), in_specs=[blk] * 5,
        out_specs=[blk, blk, sq, blk, blk, sq], out_shape=[wide, wide, square, wide, wide, square],
        compiler_params=_params(2))(q, k, v, gb, bb)


def gdn_a_bwd(q, k, v, gb, bb, t_inv, dval, dkc, dat, dqd, dkd, dgb_b, rows):
    blk = pl.BlockSpec((GDN_ROWS, LANES), lambda i, h: (i, h))
    sq = pl.BlockSpec((GDN_ROWS, GDN_ROWS), lambda i, h: (i, h))

    def body(q_ref, k_ref, v_ref, gb_ref, bb_ref, t_ref, dval_ref, dkc_ref, dat_ref, dqd_ref, dkd_ref, dgbb_ref,
             dq_ref, dk_ref, dv_ref, dgb_ref, dbb_ref):
        qv, kv, vv, gv, bv, tv = q_ref[...], k_ref[...], v_ref[...], gb_ref[...], bb_ref[...], t_ref[...]
        _, vjp1 = jax.vjp(gdn_f1, qv, kv, gv, bv)
        _, vjp2 = jax.vjp(gdn_f2, tv, qv, kv, vv, gv, bv)
        dt, dq2, dk2, dv2, dgb2, dbb2 = vjp2((dval_ref[...], dkc_ref[...], dat_ref[...], dqd_ref[...], dkd_ref[...]))
        da = -_dot_split(tv, _dot_split(dt, tv, "nt"), "tn")
        dq1, dk1, dgb1, dbb1 = vjp1(da)
        dq_ref[...] = dq1 + dq2
        dk_ref[...] = dk1 + dk2
        dv_ref[...] = dv2
        dgb_ref[...] = dgb1 + dgb2 + dgbb_ref[...]
        dbb_ref[...] = dbb1 + dbb2

    wide = jax.ShapeDtypeStruct((rows, NH * LANES), F32)
    return pl.pallas_call(
        body, name="gdn_a_bwd", grid=(rows // GDN_ROWS, NH),
        in_specs=[blk] * 5 + [sq, blk, blk, sq, blk, blk, blk], out_specs=[blk] * 5, out_shape=[wide] * 5,
        compiler_params=_params(2))(q, k, v, gb, bb, t_inv, dval, dkc, dat, dqd, dkd, dgb_b)


N_CH = GDN_ROWS // GDN_CHUNK


GDN_HP = 8


def gdn_fb(*args):
    per_head = 6 * N_CH
    states = list(args[GDN_HP * per_head:])
    outs = [[None] * N_CH for _ in range(GDN_HP)]
    zero = jnp.zeros((GDN_CHUNK, LANES), F32)
    for c in range(N_CH):
        for hh in range(GDN_HP):
            val, kc, at, qd, kd, gb = (args[hh * per_head + i * N_CH + c] for i in range(6))
            s = states[hh]
            v_new = val - _dot(kc, s, "nn", LO)
            v_pad = jnp.concatenate([zero] * c + [v_new] + [zero] * (N_CH - 1 - c), axis=0)
            outs[hh][c] = _dot(qd, s, "nn", LO) + _dot(at, v_pad, "nn", LO)
            dec = jnp.exp(jnp.sum(gb, axis=0, keepdims=True))
            states[hh] = s * dec + _dot(kd, v_new, "tn", LO)
    return (*[o for head in outs for o in head], *states)


def _gdn_piece(ref, hh, c):
    width = ref.shape[1] // GDN_HP
    return ref.at[c * GDN_CHUNK:(c + 1) * GDN_CHUNK, hh * width:(hh + 1) * width]


def _gdn_pieces(refs, hh):
    return [_gdn_piece(r, hh, c)[...] for r in refs for c in range(N_CH)]


def _gdn_b_specs(nb, rev):
    def blk_row(s, j):
        return s * nb + (nb - 1 - j if rev else j)

    blk = pl.BlockSpec((GDN_ROWS, GDN_HP * LANES), lambda s, hb, j: (blk_row(s, j), hb))
    sq = pl.BlockSpec((GDN_ROWS, GDN_HP * GDN_ROWS), lambda s, hb, j: (blk_row(s, j), hb))
    snap = pl.BlockSpec((GDN_HP * LANES, LANES), lambda s, hb, j: (blk_row(s, j) * (NH // GDN_HP) + hb, 0))
    return blk, sq, snap


def gdn_b_fwd(val, kc, at, qd, kd, gb, nseq, seq):
    nb = seq // GDN_ROWS
    rows = nseq * seq
    blk, sq, snap = _gdn_b_specs(nb, False)

    def body(val_ref, kc_ref, at_ref, qd_ref, kd_ref, gb_ref, o_ref, snap_ref, s_ref):
        @pl.when(pl.program_id(2) == 0)
        def _():
            s_ref[...] = jnp.zeros_like(s_ref)

        states = [s_ref[hh] for hh in range(GDN_HP)]
        for hh in range(GDN_HP):
            snap_ref[hh * LANES:(hh + 1) * LANES, :] = states[hh]
        pieces = [p for hh in range(GDN_HP) for p in _gdn_pieces([val_ref, kc_ref, at_ref, qd_ref, kd_ref, gb_ref], hh)]
        res = gdn_fb(*pieces, *states)
        for hh in range(GDN_HP):
            for c in range(N_CH):
                _gdn_piece(o_ref, hh, c)[...] = res[hh * N_CH + c]
            s_ref[hh] = res[GDN_HP * N_CH + hh]

    return pl.pallas_call(
        body, name="gdn_b_fwd", grid=(nseq, NH // GDN_HP, nb), in_specs=[blk, blk, sq, blk, blk, blk], out_specs=[blk, snap],
        out_shape=[jax.ShapeDtypeStruct((rows, NH * LANES), F32), jax.ShapeDtypeStruct((nseq * nb * NH * LANES, LANES), F32)],
        scratch_shapes=[pltpu.VMEM((GDN_HP, LANES, LANES), F32)], compiler_params=_params(3))(val, kc, at, qd, kd, gb)


def gdn_b_bwd(val, kc, at, qd, kd, gb, snaps, do, nseq, seq):
    nb = seq // GDN_ROWS
    rows = nseq * seq
    blk, sq, snap = _gdn_b_specs(nb, True)

    def body(val_ref, kc_ref, at_ref, qd_ref, kd_ref, gb_ref, snap_ref, do_ref,
             dval_ref, dkc_ref, dat_ref, dqd_ref, dkd_ref, dgb_ref, ds_ref):
        @pl.when(pl.program_id(2) == 0)
        def _():
            ds_ref[...] = jnp.zeros_like(ds_ref)

        pieces = [p for hh in range(GDN_HP) for p in _gdn_pieces([val_ref, kc_ref, at_ref, qd_ref, kd_ref, gb_ref], hh)]
        states = [snap_ref[hh * LANES:(hh + 1) * LANES, :] for hh in range(GDN_HP)]
        _, vjp = jax.vjp(gdn_fb, *pieces, *states)
        cts = [p for hh in range(GDN_HP) for p in _gdn_pieces([do_ref], hh)] + [ds_ref[hh] for hh in range(GDN_HP)]
        grads = vjp(tuple(cts))
        for hh in range(GDN_HP):
            for i, r in enumerate([dval_ref, dkc_ref, dat_ref, dqd_ref, dkd_ref, dgb_ref]):
                for c in range(N_CH):
                    _gdn_piece(r, hh, c)[...] = grads[hh * 6 * N_CH + i * N_CH + c]
            ds_ref[hh] = grads[GDN_HP * 6 * N_CH + hh]

    wide = jax.ShapeDtypeStruct((rows, NH * LANES), F32)
    square = jax.ShapeDtypeStruct((rows, NH * GDN_ROWS), F32)
    return pl.pallas_call(
        body, name="gdn_b_bwd", grid=(nseq, NH // GDN_HP, nb), in_specs=[blk, blk, sq, blk, blk, blk, snap, blk],
        out_specs=[blk, blk, sq, blk, blk, blk], out_shape=[wide, wide, square, wide, wide, wide],
        scratch_shapes=[pltpu.VMEM((GDN_HP, LANES, LANES), F32)], compiler_params=_params(3))(val, kc, at, qd, kd, gb, snaps, do)


FOX_Q, FOX_K, FOX_V = 4 * NH, 5 * NH, 6 * NH
FOX_SCALE = LANES ** -0.5


def _head_row(ct_ref, h, off, width):
    blk = ct_ref[:, pl.ds(off, width)]
    return jnp.sum(jnp.where(_iota(blk.shape, 0) == h, blk, 0.0), axis=0, keepdims=True)


def _col(x):
    return jnp.max(x, axis=1, keepdims=True)


def _row(x):
    return jnp.max(x.T, axis=0, keepdims=True)


def _causal(shape, q_dim):
    return _iota(shape, q_dim) >= _iota(shape, 1 - q_dim)


def fox_fwd(qn, kn, proj, ct, nseq, seq):
    tq = tk = min(ATT_TILE, seq)
    nq = seq // tq
    rows = nseq * seq
    qblk = pl.BlockSpec((tq, LANES), lambda s, h, i: (s * nq + i, h))
    full = pl.BlockSpec((seq, LANES), lambda s, h, i: (s, h))
    vfull = pl.BlockSpec((seq, LANES), lambda s, h, i: (s, h + FOX_V))
    ctb = pl.BlockSpec((NH, seq), lambda s, h, i: (s * (LANES // NH) + 2, 0))

    def body(q_ref, k_ref, v_ref, ct_ref, o_ref, o16_ref, lse_ref):
        h, i = pl.program_id(1), pl.program_id(2)
        q = q_ref[...]

        def step(j, carry, diag):
            m, l, acc = carry
            off = pl.multiple_of(j * tk, tk)
            s = _dot(q, k_ref[pl.ds(off, tk), :], "nt") * FOX_SCALE - _head_row(ct_ref, h, off, tk)
            if diag:
                s = jnp.where(_causal(s.shape, 0), s, NEG)
            m_new = jnp.maximum(m, jnp.max(s, axis=1, keepdims=True))
            p = jnp.exp(s - m_new)
            alpha = jnp.exp(m - m_new)
            l = alpha * l + jnp.sum(p, axis=1, keepdims=True)
            acc = alpha * acc + _dot(p.astype(BF16), v_ref[pl.ds(off, tk), :].astype(BF16), "nn")
            return m_new, l, acc

        init = (jnp.full((tq, 1), NEG, F32), jnp.zeros((tq, 1), F32), jnp.zeros((tq, LANES), F32))
        carry = lax.fori_loop(0, i, lambda j, c: step(j, c, False), init)
        m, l, acc = step(i, carry, True)
        o = acc / l
        o_ref[...] = o
        o16_ref[...] = o.astype(BF16)
        lse_ref[...] = jnp.broadcast_to(m + jnp.log(l), (tq, LANES))

    wide = (rows, NH * LANES)
    return pl.pallas_call(
        body, name="fox_fwd", grid=(nseq, NH, nq), in_specs=[qblk, full, vfull, ctb], out_specs=[qblk] * 3,
        out_shape=[jax.ShapeDtypeStruct(wide, F32), jax.ShapeDtypeStruct(wide, BF16), jax.ShapeDtypeStruct(wide, F32)],
        compiler_params=_params(3))(qn, kn, proj, ct)


def fox_dq(qn, kn, proj, ct, do, lse, delta, nseq, seq):
    tq = tk = min(ATT_TILE, seq)
    nq = seq // tq
    rows = nseq * seq
    qblk = pl.BlockSpec((tq, LANES), lambda s, h, i: (s * nq + i, h))
    full = pl.BlockSpec((seq, LANES), lambda s, h, i: (s, h))
    vfull = pl.BlockSpec((seq, LANES), lambda s, h, i: (s, h + FOX_V))
    ctb = pl.BlockSpec((NH, seq), lambda s, h, i: (s * (LANES // NH) + 2, 0))

    def body(q_ref, k_ref, v_ref, ct_ref, do_ref, lse_ref, dl_ref, dq_ref, dc_ref):
        h, i = pl.program_id(1), pl.program_id(2)
        q = q_ref[...]
        lse, delta = _col(lse_ref[...]), _col(dl_ref[...])
        do16 = do_ref[...].astype(BF16)

        def step(j, carry, diag):
            dq, dc = carry
            off = pl.multiple_of(j * tk, tk)
            k = k_ref[pl.ds(off, tk), :]
            p = jnp.exp(_dot(q, k, "nt") * FOX_SCALE - _head_row(ct_ref, h, off, tk) - lse)
            if diag:
                p = jnp.where(_causal(p.shape, 0), p, 0.0)
            dp = _dot(do16, v_ref[pl.ds(off, tk), :].astype(BF16), "nt")
            ds = p * (dp - delta)
            return dq + _dot(ds.astype(BF16), k, "nn"), dc + jnp.sum(ds, axis=1, keepdims=True)

        init = (jnp.zeros((tq, LANES), F32), jnp.zeros((tq, 1), F32))
        dq, dc = step(i, lax.fori_loop(0, i, lambda j, c: step(j, c, False), init), True)
        dq_ref[...] = dq * FOX_SCALE
        dc_ref[...] = jnp.where(_iota((tq, LANES), 1) == 0, dc, 0.0)

    wide = jax.ShapeDtypeStruct((rows, NH * LANES), F32)
    return pl.pallas_call(
        body, name="fox_dq", grid=(nseq, NH, nq), in_specs=[qblk, full, vfull, ctb, qblk, qblk, qblk],
        out_specs=[qblk, qblk], out_shape=[wide, wide], compiler_params=_params(3))(qn, kn, proj, ct, do, lse, delta)


def fox_dkv(qn, kn, proj, cb, do, lse, delta, nseq, seq):
    tq = tk = min(ATT_TILE, seq)
    nq = seq // tq
    rows = nseq * seq
    kblk = pl.BlockSpec((tk, LANES), lambda s, h, j: (s * nq + j, h))
    vblk = pl.BlockSpec((tk, LANES), lambda s, h, j: (s * nq + j, h + FOX_V))
    full = pl.BlockSpec((seq, LANES), lambda s, h, j: (s, h))

    def body(q_ref, k_ref, v_ref, cb_ref, do_ref, lse_ref, dl_ref, dk_ref, dv_ref, dc_ref):
        j = pl.program_id(2)
        k = k_ref[...]
        v16 = v_ref[...].astype(BF16)
        ck = _col(cb_ref[...])

        def step(i, carry, diag):
            dk, dv, dc = carry
            off = pl.multiple_of(i * tq, tq)
            q = q_ref[pl.ds(off, tq), :]
            do16 = do_ref[pl.ds(off, tq), :].astype(BF16)
            lse, delta = (_row(r[pl.ds(off, tq), :]) for r in (lse_ref, dl_ref))
            p = jnp.exp(_dot(k, q, "nt") * FOX_SCALE - ck - lse)
            if diag:
                p = jnp.where(_causal(p.shape, 1), p, 0.0)
            dv = dv + _dot(p.astype(BF16), do16, "nn")
            ds = p * (_dot(v16, do16, "nt") - delta)
            return dk + _dot(ds.astype(BF16), q, "nn"), dv, dc + jnp.sum(ds, axis=1, keepdims=True)

        zero = jnp.zeros((tk, LANES), F32)
        carry = step(j, (zero, zero, jnp.zeros((tk, 1), F32)), True)
        dk, dv, dc = lax.fori_loop(j + 1, nq, lambda i, c: step(i, c, False), carry)
        dk_ref[...] = dk * FOX_SCALE
        dv_ref[...] = dv.astype(BF16)
        dc_ref[...] = jnp.where(_iota((tk, LANES), 1) == 0, -dc, 0.0)

    wide = (rows, NH * LANES)
    return pl.pallas_call(
        body, name="fox_dkv", grid=(nseq, NH, nq), in_specs=[full, kblk, vblk, kblk, full, full, full],
        out_specs=[kblk, kblk, kblk],
        out_shape=[jax.ShapeDtypeStruct(wide, F32), jax.ShapeDtypeStruct(wide, BF16), jax.ShapeDtypeStruct(wide, F32)],
        compiler_params=_params(3))(qn, kn, proj, cb, do, lse, delta)


def loss_head(out, tgt, rows, width):
    tb = ROW_TILE
    blk = pl.BlockSpec((tb, width), lambda i: (i, 0))
    accb = pl.BlockSpec((8, LANES), lambda i: (0, 0))

    def body(o_ref, t_ref, d32_ref, d16_ref, acc_ref):
        d = o_ref[...] - t_ref[...]
        row_loss = 0.5 * jnp.mean(d * d, axis=1, keepdims=True)
        g = d * (1.0 / width)
        d32_ref[...] = g
        d16_ref[...] = g.astype(BF16)
        part = jnp.where(_iota((tb, LANES), 1) == 0, row_loss, 0.0).reshape(tb // 8, 8, LANES).sum(axis=0)

        @pl.when(pl.program_id(0) == 0)
        def _():
            acc_ref[...] = part

        @pl.when(pl.program_id(0) != 0)
        def _():
            acc_ref[...] += part

    return pl.pallas_call(
        body, name="loss_head", grid=(rows // tb,), in_specs=[blk, blk], out_specs=[blk, blk, accb],
        out_shape=[jax.ShapeDtypeStruct((rows, width), F32), jax.ShapeDtypeStruct((rows, width), BF16),
                   jax.ShapeDtypeStruct((8, LANES), F32)], compiler_params=_params(1))(out, tgt)


def _adamw_update(w, g, m, v):
    m_new = ADAM_B1 * m + (1.0 - ADAM_B1) * g
    v_new = ADAM_B2 * v + (1.0 - ADAM_B2) * (g * g)
    m_hat = m_new / (1.0 - ADAM_B1 ** ADAM_STEP)
    v_hat = v_new / (1.0 - ADAM_B2 ** ADAM_STEP)
    return -ADAM_LR * (m_hat / (jnp.sqrt(v_hat) + ADAM_EPS) + ADAM_WD * w), m_new, v_new


def adamw(name, w, g, m, v):
    rows, cols = w.shape
    tb = min(rows, 128)
    assert rows % tb == 0
    blk = pl.BlockSpec((tb, cols), lambda i: (i, 0))

    def body(w_ref, g_ref, m_ref, v_ref, d_ref, mo_ref, vo_ref):
        d_ref[...], mo_ref[...], vo_ref[...] = _adamw_update(w_ref[...], g_ref[...], m_ref[...], v_ref[...])

    shp = jax.ShapeDtypeStruct(w.shape, F32)
    return pl.pallas_call(body, name=name, grid=(rows // tb,), in_specs=[blk] * 4, out_specs=[blk] * 3,
                          out_shape=[shp] * 3, compiler_params=_params(1))(w, g, m, v)


SPLIT_TILE = 128


def _tiled(shape2d, ax, n_lead, index):
    blk = (SPLIT_TILE, shape2d[1]) if ax == 0 else (shape2d[0], SPLIT_TILE)

    def index_map(*args):
        *lead, t = index(*args)
        return (*lead, t, 0) if ax == 0 else (*lead, 0, t)

    return pl.BlockSpec((None,) * n_lead + blk, index_map)


def adamw_halves(name, w, mine, other, m, v, c, ax):
    steps = w.shape[ax] // 2 // SPLIT_TILE
    assert w.shape[ax] == 2 * steps * SPLIT_TILE

    def body(c_ref, w_ref, mine_ref, other_ref, m_ref, v_ref, g_ref, d_ref, mo_ref, vo_ref):
        g = jnp.where(pl.program_id(0) // steps == c_ref[0], mine_ref[...], other_ref[...])
        g_ref[...] = g
        d_ref[...], mo_ref[...], vo_ref[...] = _adamw_update(w_ref[...], g, m_ref[...], v_ref[...])

    blk = _tiled(w.shape, ax, 0, lambda i, c_ref: (i,))
    hblk = _tiled(mine.shape, ax, 0, lambda i, c_ref: (i % steps,))
    grid_spec = pltpu.PrefetchScalarGridSpec(num_scalar_prefetch=1, grid=(2 * steps,),
                                             in_specs=[blk, hblk, hblk, blk, blk], out_specs=[blk] * 4)
    shp = jax.ShapeDtypeStruct(w.shape, F32)
    return pl.pallas_call(body, name=name, grid_spec=grid_spec, out_shape=[shp] * 4,
                          compiler_params=_params(1))(c, w, mine, other, m, v)


def add_chips(name, slots, parts, chip, axes):
    outs = []
    for idx, (x, own, ax) in enumerate(zip(slots, parts, axes)):
        n, shape2d = x.shape[0], x.shape[1:]
        steps = shape2d[ax] // SPLIT_TILE
        assert shape2d[ax] == steps * SPLIT_TILE

        def body(me_ref, *refs, n=n):
            o_ref = refs[n + 1]
            acc = None
            for t in range(n):
                term = jnp.where(me_ref[0] == t, refs[n][...], refs[t][...]).astype(F32)
                acc = term if acc is None else acc + term
            o_ref[...] = acc

        def filled(t, n=n):
            return lambda i, me_ref: (jnp.where(me_ref[0] == t, (t + 1) % n, t), i)

        grid_spec = pltpu.PrefetchScalarGridSpec(
            num_scalar_prefetch=1, grid=(steps,),
            in_specs=[_tiled(shape2d, ax, 1, filled(t)) for t in range(n)]
            + [_tiled(shape2d, ax, 1, lambda i, me_ref: (me_ref[0], i))],
            out_specs=_tiled(shape2d, ax, 0, lambda i, me_ref: (i,)))
        outs.append(pl.pallas_call(
            body, name=f"{name}_{idx}", grid_spec=grid_spec, out_shape=jax.ShapeDtypeStruct(shape2d, F32),
            compiler_params=_params(1))(chip, *([x] * n), own))
    return outs


def add_pair(name, gs, rs, c, axes):
    outs = []
    for idx, (g, r, ax) in enumerate(zip(gs, rs, axes)):
        nb = r.shape[0]
        steps = r.shape[1 + ax] // SPLIT_TILE
        assert r.shape[1 + ax] == steps * SPLIT_TILE

        def body(c_ref, g_ref, r_ref, o_ref):
            o_ref[...] = (g_ref[...] + r_ref[...]).astype(BF16)

        grid_spec = pltpu.PrefetchScalarGridSpec(
            num_scalar_prefetch=1, grid=(nb, steps),
            in_specs=[_tiled(g.shape[1:], ax, 1, lambda b, i, c_ref: (b, c_ref[0] * steps + i)),
                      _tiled(r.shape[1:], ax, 1, lambda b, i, c_ref: (b, i))],
            out_specs=_tiled(r.shape[1:], ax, 1, lambda b, i, c_ref: (b, i)))
        outs.append(pl.pallas_call(
            body, name=f"{name}_{idx}", grid_spec=grid_spec, out_shape=jax.ShapeDtypeStruct(r.shape, BF16),
            compiler_params=_params(2))(c, g, r))
    return outs


def _place():
    x, y, c = lax.axis_index("x"), lax.axis_index("y"), lax.axis_index("c")
    return x, y, c, [(1 - x, y), (x, 1 - y), (1 - x, 1 - y)]


def _remote(src, dst, send_sem, recv_sem, dev):
    return pltpu.make_async_remote_copy(src_ref=src, dst_ref=dst, send_sem=send_sem, recv_sem=recv_sem,
                                        device_id=dev, device_id_type=MESH)


def _half(ref, lead, ax, which):
    size = ref.shape[len(lead) + ax] // 2
    part = pl.ds(which * size, size)
    return ref.at[(*lead, part, slice(None)) if ax == 0 else (*lead, slice(None), part)]


def gather_weights(shards, axes):
    n = len(shards)

    def body(*refs):
        ins, outs = refs[:n], refs[n:2 * n]
        ici_s, ici_r, d2d_s, d2d_r = refs[2 * n:]
        x, y, c, chips = _place()
        me = 2 * x + y
        sends, passes = [], []
        for w in range(n):
            for j, (ox, oy) in enumerate(chips):
                cp = _remote(_half(ins[w], (), axes[w], c), _half(outs[w], (me,), axes[w], c),
                             ici_s.at[3 * w + j], ici_r.at[3 * w + j], (ox, oy, c))
                cp.start()
                sends.append(cp)
        for w in range(n):
            for j, (ox, oy) in enumerate(chips):
                landed = _half(outs[w], (2 * ox + oy,), axes[w], c)
                _remote(landed, landed, ici_s.at[3 * w + j], ici_r.at[3 * w + j], (ox, oy, c)).wait_recv()
                cp = _remote(landed, landed, d2d_s.at[3 * w + j], d2d_r.at[3 * w + j], (x, y, 1 - c))
                cp.start()
                passes.append(cp)
        for w in range(n):
            for j, (ox, oy) in enumerate(chips):
                other = _half(outs[w], (2 * ox + oy,), axes[w], 1 - c)
                _remote(other, other, d2d_s.at[3 * w + j], d2d_r.at[3 * w + j], (x, y, 1 - c)).wait_recv()
        for cp in sends + passes:
            cp.wait_send()

    return pl.pallas_call(
        body, name="gather_weights", in_specs=[ANY] * n, out_specs=[ANY] * n,
        out_shape=[jax.ShapeDtypeStruct((4,) + s.shape, s.dtype) for s in shards],
        scratch_shapes=[pltpu.SemaphoreType.DMA((3 * n,))] * 4,
    )(*shards)


def pair_swap(grads, axes):
    n = len(grads)

    def body(*refs):
        ins, outs = refs[:n], refs[n:2 * n]
        send, recv = refs[2 * n:]
        x, y, c, _ = _place()
        cps = []
        for w in range(n):
            cp = _remote(_half(ins[w], (slice(None),), axes[w], 1 - c), outs[w], send.at[w], recv.at[w], (x, y, 1 - c))
            cp.start()
            cps.append(cp)
        for cp in cps:
            cp.wait_recv()
        for cp in cps:
            cp.wait_send()

    def halved(g, ax):
        return tuple(d // 2 if i == 1 + ax else d for i, d in enumerate(g.shape))

    return pl.pallas_call(
        body, name="pair_swap", in_specs=[ANY] * n, out_specs=[ANY] * n,
        out_shape=[jax.ShapeDtypeStruct(halved(g, ax), g.dtype) for g, ax in zip(grads, axes)],
        scratch_shapes=[pltpu.SemaphoreType.DMA((n,))] * 2,
    )(*grads)


def chip_exchange(parts):
    n = len(parts)

    def body(*refs):
        ins, outs = refs[:n], refs[n:2 * n]
        send, recv = refs[2 * n:]
        x, y, c, chips = _place()
        me = 2 * x + y
        cps = []
        for w in range(n):
            for j, (ox, oy) in enumerate(chips):
                cp = _remote(ins[w].at[2 * ox + oy], outs[w].at[me], send.at[3 * w + j], recv.at[3 * w + j], (ox, oy, c))
                cp.start()
                cps.append(cp)
        for w in range(n):
            for j, (ox, oy) in enumerate(chips):
                slot = outs[w].at[2 * ox + oy]
                _remote(slot, slot, send.at[3 * w + j], recv.at[3 * w + j], (ox, oy, c)).wait_recv()
        for cp in cps:
            cp.wait_send()

    return pl.pallas_call(
        body, name="chip_exchange", in_specs=[ANY] * n, out_specs=[ANY] * n,
        out_shape=[jax.ShapeDtypeStruct(p.shape, p.dtype) for p in parts],
        scratch_shapes=[pltpu.SemaphoreType.DMA((3 * n,))] * 2,
    )(*parts)


def pair_send(halves):
    n = len(halves)

    def body(*refs):
        ins, outs = refs[:n], refs[n:2 * n]
        send, recv = refs[2 * n:]
        x, y, c, _ = _place()
        cps = [_remote(ins[w], outs[w], send.at[w], recv.at[w], (x, y, 1 - c)) for w in range(n)]
        for cp in cps:
            cp.start()
        for cp in cps:
            cp.wait_recv()
        for cp in cps:
            cp.wait_send()

    return pl.pallas_call(
        body, name="pair_send", in_specs=[ANY] * n, out_specs=[ANY] * n,
        out_shape=[jax.ShapeDtypeStruct(h.shape, h.dtype) for h in halves],
        scratch_shapes=[pltpu.SemaphoreType.DMA((n,))] * 2,
    )(*halves)


def all_reduce_small(name, vec):
    rows = vec.shape[0]

    def body(v_ref, o_ref, buf, send, recv):
        x, y, c, _ = _place()
        me = 4 * x + 2 * y + c
        buf[me] = v_ref[...]
        cps = []
        for k in range(1, 8):
            kx, ky, kc = (k >> 2) & 1, (k >> 1) & 1, k & 1
            peer = (x if kx == 0 else 1 - x, y if ky == 0 else 1 - y, c if kc == 0 else 1 - c)
            cp = _remote(v_ref, buf.at[me], send.at[k - 1], recv.at[k - 1], peer)
            cp.start()
            cps.append(cp)
        for k in range(1, 8):
            kx, ky, kc = (k >> 2) & 1, (k >> 1) & 1, k & 1
            px, py, pc = (x if kx == 0 else 1 - x, y if ky == 0 else 1 - y, c if kc == 0 else 1 - c)
            slot = buf.at[4 * px + 2 * py + pc]
            _remote(slot, slot, send.at[k - 1], recv.at[k - 1], (px, py, pc)).wait_recv()
        for cp in cps:
            cp.wait_send()
        acc = buf[0]
        for d in range(1, 8):
            acc = acc + buf[d]
        o_ref[...] = acc

    vm = pl.BlockSpec(memory_space=pltpu.VMEM)
    return pl.pallas_call(
        body, name=name, in_specs=[vm], out_specs=vm, out_shape=jax.ShapeDtypeStruct(vec.shape, F32),
        scratch_shapes=[pltpu.VMEM((8, rows, LANES), F32), pltpu.SemaphoreType.DMA((7,)), pltpu.SemaphoreType.DMA((7,))],
    )(vec)


def local_step(x2, tgt2, g1, g2, gdn_ng, qn_g, kn_g, p1, p2, conv_w, wt_main, wt_small, p_a, p_b, w_o, w_u, w_d, nseq, seq):
    rows, dm = x2.shape
    wide = NH * LANES
    row = lambda a, off=0, w=None: (a, "row", off, a.shape[1] if w is None else w)
    rowh = lambda a, off=0, w=LANES: (a, "rowh", off, w)
    par = lambda a: (a, "par", 0, a.shape[1])
    parh = lambda a, off=0: (a, "parh", off, LANES)
    o_row = lambda w, dt: (w, "row", w, dt)
    o_rowh = lambda dt, tw=wide, w=LANES: (tw, "rowh", w, dt)

    u, = ew_fwd("rms1", f_rms, [row(x2), par(g1)], [o_row(dm, BF16)], rows)
    proj = matmul("mm_in", u, wt_main, "nt", F32)
    sp = matmul("mm_in_small", u, wt_small, "nt", F32)
    so, = ew_fwd("small", f_small, [row(sp), par(p1), par(p2)], [o_row(LANES, F32)], rows)
    cs = cumsum_time("cumsum", so, nseq, seq, False)
    gb, bb, cb = ew_fwd("bcast", f_bcast, [row(so), row(cs)], [o_rowh(F32)] * 3, rows, NH)
    ct = transpose_time("c_time_major", cs, nseq, seq)
    conv = {}
    for mode, off in (("q", 0), ("k", NH), ("v", 2 * NH)):
        conv[mode], = ew_fwd(f"conv_{mode}", make_f_conv(mode), [rowh(proj, off), parh(conv_w, off)], [o_rowh(F32)],
                             rows, NH, seq, "hi", CONV_HEADS)
    val, kcum, attn, qdec, kdec, t_inv = gdn_a_fwd(conv["q"], conv["k"], conv["v"], gb, bb, rows)
    o_a, snaps = gdn_b_fwd(val, kcum, attn, qdec, kdec, gb, nseq, seq)
    ya_in, = ew_fwd("gdn_post", f_post, [rowh(o_a), rowh(proj, 3 * NH), par(gdn_ng)], [o_rowh(BF16)], rows, NH)
    fqn, = ew_fwd("fox_qn", f_rms, [rowh(proj, FOX_Q), par(qn_g)], [o_rowh(BF16)], rows, NH)
    fkn, = ew_fwd("fox_kn", f_rms, [rowh(proj, FOX_K), par(kn_g)], [o_rowh(BF16)], rows, NH)
    o_b, o_b16, lse = fox_fwd(fqn, fkn, proj, ct, nseq, seq)
    y_a = matmul("mm_pa", ya_in, p_a, "nn", F32, tn=1024)
    y_b = matmul("mm_pb", o_b16, p_b, "nn", F32, tn=1024)
    gates = [row(proj, 7, dm), row(proj, 8, dm)]
    merged, = ew_fwd("merge", f_merge, gates + [row(y_a), row(y_b)], [o_row(dm, BF16)], rows)
    hres = matmul("mm_out", merged, w_o, "nn", F32, add=x2, tn=1024)
    hn, = ew_fwd("rms2", f_rms, [row(hres), par(g2)], [o_row(dm, BF16)], rows)
    up_blocks = w_u.shape[0]
    dff = up_blocks * w_u.shape[2]
    act = matmul("mm_up", hn, w_u, "nn", F32, col_blocks=up_blocks)
    nff = dff // dm
    relu2, = ew_fwd("relu2", f_relu2, [rowh(act, 0, dm)], [o_rowh(BF16, dff, dm)], rows, nff, ROW_TILE // 2)
    out = matmul("mm_down", relu2, w_d, "nn", F32, add=hres, tn=1024)
    dout, dout16, loss_acc = loss_head(out, tgt2, rows, dm)

    g_first = lambda dt=F32: (lambda g, e: [g[0]])
    d_relu2 = matmul("mm_d_relu2", dout16, w_d, "nt", F32, tn=1024)
    dw_d = matmul("mm_dw_down", relu2, dout16, "tn", F32, tn=1024)
    d_act, = ew_bwd("relu2_b", f_relu2, [rowh(act, 0, dm)], [(rowh(d_relu2, 0, dm),)], [], lambda g, e: [g[0]],
                    [((rows, dff), "rowh", dm, BF16, None)], rows, nff, ROW_TILE // 2)
    dw_u = matmul("mm_dw_up", hn, d_act, "tn", F32, col_blocks=up_blocks)
    d_hn = matmul("mm_d_hn", d_act, w_u, "nt", F32, col_blocks=up_blocks)
    dh, dh16, dg2 = ew_bwd("rms2_b", f_rms, [row(hres), par(g2)], [(row(d_hn),)], [row(dout)],
                           lambda g, e: [g[0] + e[0], g[0] + e[0], g[1]],
                           [((rows, dm), "row", dm, F32, None), ((rows, dm), "row", dm, BF16, None), ((1, dm), "par", dm, F32, "all")], rows)
    d_merged = matmul("mm_d_merged", dh16, w_o, "nt", F32, tn=1024)
    dw_o = matmul("mm_dw_out", merged, dh16, "tn", F32, tn=1024)
    seg16 = ((rows, dm), "row", dm, BF16, None)
    d_ga16, d_gb16, d_ya16, d_yb16 = ew_bwd("merge_b", f_merge, gates + [row(y_a), row(y_b)], [(row(d_merged),)], [],
                                            lambda g, e: list(g), [seg16] * 4, rows)
    dp_a = matmul("mm_dp_a", ya_in, d_ya16, "tn", F32, tn=1024)
    d_ya_in = matmul("mm_d_ya_in", d_ya16, p_a, "nt", F32, tn=1024)
    dp_b = matmul("mm_dp_b", o_b16, d_yb16, "tn", F32, tn=1024)
    d_ob = matmul("mm_d_ob", d_yb16, p_b, "nt", F32, tn=1024)
    h32 = ((rows, wide), "rowh", LANES, F32, None)
    h16 = ((rows, wide), "rowh", LANES, BF16, None)
    gain = ((1, LANES), "par", LANES, F32, "all")
    d_oa, d_z16, d_gdn_ng = ew_bwd("gdn_post_b", f_post, [rowh(o_a), rowh(proj, 3 * NH), par(gdn_ng)], [(rowh(d_ya_in),)], [],
                                   lambda g, e: list(g), [h32, h16, gain], rows, NH)
    dval, dkc, dat, dqd, dkd, dgb_b = gdn_b_bwd(val, kcum, attn, qdec, kdec, gb, snaps, d_oa, nseq, seq)
    d_cq, d_ck, d_cv, d_gb, d_bb = gdn_a_bwd(conv["q"], conv["k"], conv["v"], gb, bb, t_inv, dval, dkc, dat, dqd, dkd, dgb_b, rows)
    d_pre, d_conv = {}, {}
    tap = ((4, wide), "parh", LANES, F32, "inner")
    for mode, off, ctg in (("q", 0, d_cq), ("k", NH, d_ck), ("v", 2 * NH, d_cv)):
        d_pre[mode], d_conv[mode] = ew_bwd(f"conv_{mode}_b", make_f_conv(mode), [rowh(proj, off), parh(conv_w, off)],
                                           [(rowh(ctg),)], [], lambda g, e: list(g), [h16, tap], rows, NH, seq, "hi", CONV_HEADS)
    delta, = ew_fwd("fox_delta", f_delta, [rowh(d_ob), rowh(o_b)], [o_rowh(F32)], rows, NH)
    d_fqn, d_cq_b = fox_dq(fqn, fkn, proj, ct, d_ob, lse, delta, nseq, seq)
    d_fkn, d_fv16, d_ck_b = fox_dkv(fqn, fkn, proj, cb, d_ob, lse, delta, nseq, seq)
    d_fq16, d_qn_g = ew_bwd("fox_qn_b", f_rms, [rowh(proj, FOX_Q), par(qn_g)], [(rowh(d_fqn),)], [], lambda g, e: list(g),
                            [h16, gain], rows, NH)
    d_fk16, d_kn_g = ew_bwd("fox_kn_b", f_rms, [rowh(proj, FOX_K), par(kn_g)], [(rowh(d_fkn),)], [], lambda g, e: list(g),
                            [h16, gain], rows, NH)
    narrow = ((rows, LANES), "row", LANES, F32, None)
    d_so, d_cs = ew_bwd("bcast_b", f_bcast, [row(so), row(cs)], [(rowh(d_gb),), (rowh(d_bb),), (rowh(d_cq_b), rowh(d_ck_b))], [],
                        lambda g, e: list(g), [narrow, narrow], rows, NH)
    d_logf = cumsum_time("cumsum_b", d_cs, nseq, seq, True)
    vec = ((1, LANES), "par", LANES, F32, "all")
    d_sp16, d_p1, d_p2 = ew_bwd("small_b", f_small, [row(sp), par(p1), par(p2)], [(row(d_so), row(d_logf))], [],
                                lambda g, e: list(g), [((rows, LANES), "row", LANES, BF16, None), vec, vec], rows)
    d_proj16 = jnp.concatenate([d_pre["q"], d_pre["k"], d_pre["v"], d_z16, d_fq16, d_fk16, d_fv16, d_ga16, d_gb16], axis=1)
    dw_main = matmul("mm_dw_main", d_proj16, u, "tn", F32)
    dw_small = matmul("mm_dw_small", d_sp16, u, "tn", F32)
    d_u = matmul("mm_d_u_small", d_sp16, wt_small, "nn", F32)
    d_u = matmul("mm_d_u", d_proj16, wt_main, "nn", F32, add=d_u)
    dx, dg1 = ew_bwd("rms1_b", f_rms, [row(x2), par(g1)], [(row(d_u),)], [row(dh)], lambda g, e: [g[0] + e[0], g[1]],
                     [((rows, dm), "row", dm, F32, None), ((1, dm), "par", dm, F32, "all")], rows)
    d_conv_w = jnp.concatenate([d_conv["q"], d_conv["k"], d_conv["v"]], axis=1)
    return dict(loss_acc=loss_acc, dx=dx, g1=dg1, g2=dg2, gdn_ng=d_gdn_ng, qn=d_qn_g, kn=d_kn_g, p1=d_p1, p2=d_p2,
                conv=d_conv_w, w_main=dw_main, w_small=dw_small, p_a=dp_a, p_b=dp_b, w_o=dw_o, w_u=dw_u, w_d=dw_d)


_W = NH * LANES
_A0, _A1 = 4 * _W, 4 * _W + 2 * NH
_B0, _B1 = _A1 + 3 * _W, _A1 + 3 * _W + NH
N_IN = _B1 + 2 * _W


def _split_w_in(full_t):
    main = jnp.concatenate([full_t[:_A0], full_t[_A1:_B0], full_t[_B1:]], axis=0)
    small = jnp.concatenate([full_t[_A0:_A1], full_t[_B0:_B1], jnp.zeros((LANES - 3 * NH, full_t.shape[1]), full_t.dtype)], axis=0)
    return main, small


def _join_w_in(main, small):
    return jnp.concatenate([main[:_A0], small[:2 * NH], main[_A0:_A0 + 3 * _W], small[2 * NH:3 * NH], main[_A0 + 3 * _W:]], axis=0)


def _lanes(v, at=0):
    return jnp.pad(v.reshape(1, -1), ((0, 0), (at, LANES - at - v.size)))


def kernel(x, norm_mix_g, w_in, gdn_conv_w, gdn_a_log, gdn_dt_bias, gdn_norm_g, fox_q_norm_g, fox_k_norm_g, fox_f_bias, w_proj_gdn, w_proj_fox, w_out, norm_mlp_g, w_up, w_down, loss_target, m_norm_mix_g, m_w_in, m_gdn_conv_w, m_gdn_a_log, m_gdn_dt_bias, m_gdn_norm_g, m_fox_q_norm_g, m_fox_k_norm_g, m_fox_f_bias, m_w_proj_gdn, m_w_proj_fox, m_w_out, m_norm_mlp_g, m_w_up, m_w_down, v_norm_mix_g, v_w_in, v_gdn_conv_w, v_gdn_a_log, v_gdn_dt_bias, v_gdn_norm_g, v_fox_q_norm_g, v_fox_k_norm_g, v_fox_f_bias, v_w_proj_gdn, v_w_proj_fox, v_w_out, v_norm_mlp_g, v_w_up, v_w_down):
    nseq, seq, dm = x.shape
    rows = nseq * seq
    xi, yi, ci = lax.axis_index("x"), lax.axis_index("y"), lax.axis_index("c")
    chip = 2 * xi + yi
    conv_cols = gdn_conv_w.shape[2]

    tr = lambda a: jnp.swapaxes(a[0], 0, 1)
    big = [tr(w_in), w_proj_gdn[0], w_proj_fox[0], w_out[0], w_up[0], w_down[0]]
    axes = [1, 0, 0, 0, 0, 0]
    big16 = [w.astype(BF16) for w in big]
    g_in, g_pa, g_pb, g_wo, w_u, g_wd = (lax.dynamic_update_index_in_dim(got, own, chip, 0)
                                         for got, own in zip(gather_weights(big16, axes), big16))
    wt_main, wt_small = _split_w_in(g_in.reshape(-1, dm))
    p_a, p_b, w_o, w_d = (g.reshape(-1, dm) for g in (g_pa, g_pb, g_wo, g_wd))
    conv_slot = jnp.zeros((4, 4, conv_cols), F32).at[:, chip].set(jnp.where(ci == 0, gdn_conv_w[0], 0.0))
    conv_full = all_reduce_small("gather_conv", conv_slot.reshape(-1, LANES)).reshape(4, 4 * conv_cols)
    p1 = _lanes(gdn_dt_bias[0]) + _lanes(fox_f_bias[0], 2 * NH)
    p2 = _lanes(gdn_a_log[0])

    g = local_step(x.reshape(rows, dm), loss_target.reshape(rows, dm), norm_mix_g, norm_mlp_g, gdn_norm_g, fox_q_norm_g,
                   fox_k_norm_g, p1, p2, conv_full, wt_main, wt_small, p_a, p_b, w_o, w_u, w_d, nseq, seq)

    small_parts = [g["loss_acc"], g["g1"].reshape(8, LANES), g["g2"].reshape(8, LANES), g["gdn_ng"], g["qn"], g["kn"], g["p1"], g["p2"],
                   g["conv"].reshape(-1, LANES)]
    tiled = [jnp.pad(p, ((0, -p.shape[0] % 8), (0, 0))) for p in small_parts]
    red = all_reduce_small("reduce_small", jnp.concatenate(tiled, axis=0))
    pos, red_parts = 0, []
    for p, t in zip(small_parts, tiled):
        red_parts.append(red[pos:pos + p.shape[0]])
        pos += t.shape[0]
    r_loss, r_g1, r_g2, r_gdn_ng, r_qn, r_kn, r_p1, r_p2, r_conv = red_parts
    loss = jnp.sum(r_loss)
    g_conv = lax.dynamic_slice_in_dim(r_conv.reshape(4, 4, conv_cols), chip, 1, axis=1).reshape(4, conv_cols)
    small_grads = [r_g1.reshape(1, dm), r_p2[:, :NH], r_p1[:, :NH], r_gdn_ng, r_qn, r_kn, r_p1[:, 2 * NH:3 * NH], r_g2.reshape(1, dm)]
    small_w = [norm_mix_g, gdn_a_log, gdn_dt_bias, gdn_norm_g, fox_q_norm_g, fox_k_norm_g, fox_f_bias, norm_mlp_g]
    small_m = [m_norm_mix_g, m_gdn_a_log, m_gdn_dt_bias, m_gdn_norm_g, m_fox_q_norm_g, m_fox_k_norm_g, m_fox_f_bias, m_norm_mlp_g]
    small_v = [v_norm_mix_g, v_gdn_a_log, v_gdn_dt_bias, v_gdn_norm_g, v_fox_q_norm_g, v_fox_k_norm_g, v_fox_f_bias, v_norm_mlp_g]

    def pack(parts):
        flat = jnp.concatenate([jnp.pad(p.reshape(-1), (0, -p.size % LANES)) for p in parts])
        return jnp.pad(flat, (0, -flat.size % (8 * LANES))).reshape(-1, LANES)

    packed = adamw("adamw_small", pack(small_w + [gdn_conv_w[0]]), pack(small_grads + [g_conv]),
                   pack(small_m + [m_gdn_conv_w[0]]), pack(small_v + [v_gdn_conv_w[0]]))

    def unpack(flat2d):
        flat, pos, res = flat2d.reshape(-1), 0, []
        for p in small_w + [gdn_conv_w[0]]:
            res.append(flat[pos:pos + p.size].reshape(p.shape))
            pos += p.size + (-p.size % LANES)
        return res

    s_delta, s_m, s_v = (unpack(a) for a in packed)

    blocks = [_join_w_in(g["w_main"], g["w_small"]).reshape(4, -1, dm), g["p_a"].reshape(4, -1, dm), g["p_b"].reshape(4, -1, dm),
              g["w_o"].reshape(4, -1, dm), g["w_u"], g["w_d"].reshape(4, -1, dm)]
    core = ci.reshape(1).astype(jnp.int32)
    swapped = pair_swap(blocks, axes)
    chip_part = add_pair("add_pair", blocks, swapped, core, axes)
    slots = chip_exchange(chip_part)
    halves = add_chips("add_chips", slots, chip_part, chip.reshape(1).astype(jnp.int32), axes)
    others = pair_send(halves)
    big_m = [tr(m_w_in), m_w_proj_gdn[0], m_w_proj_fox[0], m_w_out[0], m_w_up[0], m_w_down[0]]
    big_v = [tr(v_w_in), v_w_proj_gdn[0], v_w_proj_fox[0], v_w_out[0], v_w_up[0], v_w_down[0]]
    names = ["w_in", "w_proj_gdn", "w_proj_fox", "w_out", "w_up", "w_down"]
    big_res, big_grad = {}, {}
    for nm, w, mine, other, m, v, ax in zip(names, big, halves, others, big_m, big_v, axes):
        res = adamw_halves(f"adamw_{nm}", w, mine, other, m, v, core, ax)
        if nm == "w_in":
            res = [jnp.swapaxes(r, 0, 1) for r in res]
        big_grad[nm], *big_res[nm] = res

    order = ["norm_mix_g", "w_in", "gdn_conv_w", "gdn_a_log", "gdn_dt_bias", "gdn_norm_g", "fox_q_norm_g", "fox_k_norm_g",
             "fox_f_bias", "w_proj_gdn", "w_proj_fox", "w_out", "norm_mlp_g", "w_up", "w_down"]
    small_names = ["norm_mix_g", "gdn_a_log", "gdn_dt_bias", "gdn_norm_g", "fox_q_norm_g", "fox_k_norm_g", "fox_f_bias", "norm_mlp_g",
                   "gdn_conv_w"]
    small_idx = {nm: i for i, nm in enumerate(small_names)}
    shapes = dict(zip(order, (a.shape for a in (norm_mix_g, w_in, gdn_conv_w, gdn_a_log, gdn_dt_bias, gdn_norm_g, fox_q_norm_g,
                                                 fox_k_norm_g, fox_f_bias, w_proj_gdn, w_proj_fox, w_out, norm_mlp_g, w_up, w_down))))
    grads_out, delta_out, m_out, v_out = [], [], [], []
    for nm in order:
        if nm in big_res:
            d, mm, vv = big_res[nm]
            gr = big_grad[nm]
        else:
            i = small_idx[nm]
            gr = (small_grads + [g_conv])[i]
            d, mm, vv = s_delta[i], s_m[i], s_v[i]
        for lst, val in ((grads_out, gr), (delta_out, d), (m_out, mm), (v_out, vv)):
            lst.append(val.reshape(shapes[nm]))
    return (loss, g["dx"].reshape(x.shape), *grads_out, *delta_out, *m_out, *v_out)
```

```python
import functools

import jax
import jax.numpy as jnp
from jax import lax
from jax.experimental import pallas as pl
from jax.experimental.pallas import tpu as pltpu

F32 = jnp.float32
BF16 = jnp.bfloat16
LANES = 128
NH = 8
EPS = 1e-6
GDN_CHUNK = 64
GDN_ROWS = 256
GDN_BASE = 16
ROW_TILE = 512
CONV_HEADS = 2
ATT_TILE = 512
NEG = -1e30
VMEM_LIMIT_BYTES = 48 * 1024 * 1024
HI = lax.Precision.HIGHEST
LO = lax.Precision.DEFAULT
MESH = pl.DeviceIdType.MESH
ANY = pl.BlockSpec(memory_space=pl.ANY)

ADAM_LR, ADAM_B1, ADAM_B2, ADAM_EPS, ADAM_WD, ADAM_STEP = 0.001, 0.9, 0.999, 1e-08, 0.01, 10


def _params(n_grid):
    return pltpu.CompilerParams(dimension_semantics=("arbitrary",) * n_grid,
                                vmem_limit_bytes=VMEM_LIMIT_BYTES)


def _dot(a, b, dims, precision=None):
    dn = {"nn": (((1,), (0,)), ((), ())), "nt": (((1,), (1,)), ((), ())), "tn": (((0,), (0,)), ((), ()))}[dims]
    return lax.dot_general(a, b, dn, precision=precision, preferred_element_type=F32)


def _iota(shape, dim):
    return lax.broadcasted_iota(jnp.int32, shape, dim)


def _split(x, parts):
    out = []
    for _ in range(parts - 1):
        hi = x.astype(BF16)
        out.append(hi)
        x = x - hi.astype(F32)
    return out + [x.astype(BF16)]


def _dot_mask(mask, b, dims):
    m16 = mask.astype(BF16)
    b1, b2, b3 = _split(b, 3)
    return _dot(m16, b1, dims) + (_dot(m16, b2, dims) + _dot(m16, b3, dims))


@jax.custom_vjp
def mm_mask(mask, b):
    return _dot_mask(mask, b, "nn")


mm_mask.defvjp(lambda mask, b: (_dot_mask(mask, b, "nn"), mask),
               lambda mask, g: (jnp.zeros_like(mask), _dot_mask(mask, g, "tn")))


def matmul(name, a, b, dims, out_dtype, add=None, tm=1024, tn=1024, tk=512, col_blocks=None,
           extras=(), epilogue=None, out_dtypes=None):
    if col_blocks and dims != "tn":
        nb, b_rows, bw = b.shape
        b_shape = (b_rows, nb * bw)
    else:
        b_shape = b.shape
    if dims == "nn":
        (m, k), (_, n) = a.shape, b_shape
    elif dims == "nt":
        (m, k), (n, _) = a.shape, b_shape
    else:
        (k, m), (_, n) = a.shape, b_shape
    if k <= 1024:
        tk = k
    tm, tn, tk = min(tm, m), min(tn, n), min(tk, k)
    assert m % tm == 0 and n % tn == 0 and k % tk == 0, (name, m, n, k)
    nk = k // tk
    a_spec = pl.BlockSpec((tk, tm), lambda i, j, kk: (kk, i)) if dims == "tn" else pl.BlockSpec((tm, tk), lambda i, j, kk: (i, kk))
    b_spec = pl.BlockSpec((tn, tk), lambda i, j, kk: (j, kk)) if dims == "nt" else pl.BlockSpec((tk, tn), lambda i, j, kk: (kk, j))
    o_spec = pl.BlockSpec((tm, tn), lambda i, j, kk: (i, j))
    out_shape = (m, n)
    if col_blocks and dims == "nn":
        per = bw // tn
        assert bw % tn == 0
        b_spec = pl.BlockSpec((None, tk, tn), lambda i, j, kk: (j // per, kk, j % per))
    elif col_blocks and dims == "nt":
        per = bw // tk
        assert bw % tk == 0
        b_spec = pl.BlockSpec((None, tn, tk), lambda i, j, kk: (kk // per, j, kk % per))
    elif col_blocks:
        bw = n // col_blocks
        per = bw // tn
        assert bw % tn == 0 and add is None
        o_spec = pl.BlockSpec((None, tm, tn), lambda i, j, kk: (j // per, i, j % per))
        out_shape = (col_blocks, m, bw)
    extras = list(extras) + ([add] if add is not None else [])
    if add is not None:
        assert epilogue is None
        epilogue = lambda r, *e: [r + e[-1]]
    out_dtypes = [out_dtype] if epilogue is None or out_dtypes is None else list(out_dtypes)
    n_ex, n_out = len(extras), len(out_dtypes)

    def body(*refs):
        a_ref, b_ref = refs[0], refs[1]
        ex_refs, o_refs = refs[2:2 + n_ex], refs[2 + n_ex:2 + n_ex + n_out]

        def finish(r):
            res = [r] if epilogue is None else epilogue(r, *[e[...] for e in ex_refs])
            for o_ref, v in zip(o_refs, res):
                o_ref[...] = v.astype(o_ref.dtype)

        if nk == 1:
            finish(_dot(a_ref[...], b_ref[...], dims))
            return
        acc_ref = refs[-1]
        kk = pl.program_id(2)

        @pl.when(kk == 0)
        def _():
            acc_ref[...] = jnp.zeros_like(acc_ref)

        acc_ref[...] += _dot(a_ref[...], b_ref[...], dims)

        @pl.when(kk == nk - 1)
        def _():
            finish(acc_ref[...])

    res = pl.pallas_call(
        body, name=name, grid=(m // tm, n // tn, nk), in_specs=[a_spec, b_spec] + [o_spec] * n_ex, out_specs=[o_spec] * n_out,
        out_shape=[jax.ShapeDtypeStruct(out_shape, dt) for dt in out_dtypes],
        scratch_shapes=[pltpu.VMEM((tm, tn), F32)] if nk > 1 else [], compiler_params=_params(3),
    )(a, b, *extras)
    return res[0] if n_out == 1 else res


def _ew_spec(kind, off, width, tb, hp, order, shape=None):
    def ih(g0, g1):
        return (g0, g1) if order == "ih" else (g1, g0)

    assert off % hp == 0 or kind in ("row", "par")
    if kind == "row":
        return pl.BlockSpec((tb, width), lambda g0, g1: (ih(g0, g1)[0], off))
    if kind == "rowh":
        return pl.BlockSpec((tb, hp * width), lambda g0, g1: (ih(g0, g1)[0], ih(g0, g1)[1] + off // hp))
    if kind == "par":
        return pl.BlockSpec(shape, lambda g0, g1: (0, 0))
    if kind == "parh":
        return pl.BlockSpec((shape[0], hp * width), lambda g0, g1: (0, ih(g0, g1)[1] + off // hp))
    raise ValueError(kind)


def _ew_grid(rows, tb, nh, hp, order):
    assert nh % hp == 0 and rows % tb == 0
    return (rows // tb, nh // hp) if order == "ih" else (nh // hp, rows // tb)


def _ew_load(ref, kind, width, hh):
    if kind in ("row", "par"):
        return ref[...].astype(F32)
    return ref[:, hh * width:(hh + 1) * width].astype(F32)


def ew_fwd(name, f, ins, outs, rows, nh=1, tb=ROW_TILE, order="ih", hp=None):
    hp = nh if hp is None else hp
    n_in = len(ins)

    def body(*refs):
        hb = pl.program_id(1) if order == "ih" else pl.program_id(0)
        for hh in range(hp):
            h = hh if hp == nh else hb * hp + hh
            vals = [_ew_load(r, kd, w, hh) for r, (_, kd, _, w) in zip(refs[:n_in], ins)]
            res = f(h, *vals)
            for r, v, (_, kd, w, _) in zip(refs[n_in:], res, outs):
                if kd == "row":
                    assert hp == 1
                    r[...] = v.astype(r.dtype)
                else:
                    r[:, hh * w:(hh + 1) * w] = v.astype(r.dtype)

    in_specs = [_ew_spec(kd, off, w, tb, hp, order, a.shape) for (a, kd, off, w) in ins]
    out_specs = [_ew_spec(kd, 0, w, tb, hp, order) for (_, kd, w, _) in outs]
    out_shape = [jax.ShapeDtypeStruct((rows, tw), dt) for (tw, _, _, dt) in outs]
    return pl.pallas_call(
        body, name=name, grid=_ew_grid(rows, tb, nh, hp, order), in_specs=in_specs, out_specs=out_specs,
        out_shape=out_shape, compiler_params=_params(2),
    )(*[a for (a, _, _, _) in ins])


def ew_bwd(name, f, ins, cts, extras, emit, outs, rows, nh=1, tb=ROW_TILE, order="ih", hp=None):
    hp = nh if hp is None else hp
    n_in = len(ins)
    flat_cts = [d for group in cts for d in group]
    n_ct, n_ex = len(flat_cts), len(extras)

    def body(*refs):
        g0, g1 = pl.program_id(0), pl.program_id(1)
        hb = g1 if order == "ih" else g0
        out_refs = refs[n_in + n_ct + n_ex:]
        shared = [None] * len(outs)

        def store(r, v, first, sl=None):
            def put(val, add):
                if sl is None:
                    r[...] = (r[...] + val if add else val).astype(r.dtype)
                else:
                    r[:, sl] = (r[:, sl] + val if add else val).astype(r.dtype)

            if first is None:
                put(v, False)
            else:
                pl.when(first)(lambda: put(v, False))
                pl.when(jnp.logical_not(first))(lambda: put(v, True))

        for hh in range(hp):
            h = hh if hp == nh else hb * hp + hh
            vals = [_ew_load(r, kd, w, hh) for r, (_, kd, _, w) in zip(refs[:n_in], ins)]
            ct_refs = list(zip(refs[n_in:n_in + n_ct], flat_cts))
            ct_vals, pos = [], 0
            for group in cts:
                v = None
                for r, (_, kd, _, w) in ct_refs[pos:pos + len(group)]:
                    t = _ew_load(r, kd, w, hh)
                    v = t if v is None else v + t
                pos += len(group)
                ct_vals.append(v)
            ex_vals = [_ew_load(r, kd, w, hh) for r, (_, kd, _, w) in zip(refs[n_in + n_ct:n_in + n_ct + n_ex], extras)]
            _, vjp = jax.vjp(lambda *a: f(h, *a), *vals)
            res = emit(vjp(tuple(ct_vals)), ex_vals)
            for idx, (r, v, (_, kd, w, _, acc)) in enumerate(zip(out_refs, res, outs)):
                if kd in ("row", "par"):
                    shared[idx] = v if shared[idx] is None else shared[idx] + v
                else:
                    store(r, v, (g1 == 0) if acc == "inner" else None, slice(hh * w, (hh + 1) * w))
        for idx, (r, (_, kd, _, _, acc)) in enumerate(zip(out_refs, outs)):
            if kd in ("row", "par"):
                assert acc == "all" or hp == nh
                store(r, shared[idx], jnp.logical_and(g0 == 0, g1 == 0) if acc == "all" else None)

    operands = list(ins) + flat_cts + list(extras)
    in_specs = [_ew_spec(kd, off, w, tb, hp, order, a.shape) for (a, kd, off, w) in operands]
    out_specs = [_ew_spec(kd, 0, w, tb, hp, order, shp) for (shp, kd, w, _, _) in outs]
    out_shape = [jax.ShapeDtypeStruct(shp, dt) for (shp, _, _, dt, _) in outs]
    return pl.pallas_call(
        body, name=name, grid=_ew_grid(rows, tb, nh, hp, order), in_specs=in_specs, out_specs=out_specs,
        out_shape=out_shape, compiler_params=_params(2),
    )(*[a for (a, _, _, _) in operands])


def f_rms(h, x, g):
    r = lax.rsqrt(jnp.mean(x * x, axis=-1, keepdims=True) + EPS)
    return (x * r * g,)


def _softplus(z):
    return jnp.maximum(z, 0.0) + jnp.log1p(jnp.exp(-jnp.abs(z)))


def f_small(h, sp, p1, p2):
    lane = _iota(sp.shape, 1)
    z = sp + p1
    g = -jnp.exp(p2) * _softplus(z)
    beta = jax.nn.sigmoid(z)
    logf = -_softplus(-z)
    return (jnp.where(lane < NH, g, jnp.where(lane < 2 * NH, beta, jnp.where(lane < 3 * NH, logf, 0.0))),)


def _pick(x, lane_id):
    lane = _iota(x.shape, 1)
    col = jnp.sum(jnp.where(lane == lane_id, x, 0.0), axis=1, keepdims=True)
    return jnp.broadcast_to(col, x.shape)


def f_bcast(h, so, cs):
    return _pick(so, h), _pick(so, h + NH), _pick(cs, h + 2 * NH)


def _shift_down(s):
    def down(x):
        return jnp.where(_iota(x.shape, 0) >= s, pltpu.roll(x, s, 0), 0.0)

    def up(g):
        n = g.shape[0]
        return jnp.where(_iota(g.shape, 0) < n - s, pltpu.roll(g, n - s, 0), 0.0)

    @jax.custom_vjp
    def shift(x):
        return down(x)

    shift.defvjp(lambda x: (down(x), None), lambda _, g: (up(g),))
    return shift


def _silu(x):
    return x * jax.nn.sigmoid(x)


def make_f_conv(mode):
    sh1, sh2, sh3 = _shift_down(1), _shift_down(2), _shift_down(3)

    def f(h, x, w):
        sub = _iota(w.shape, 0)

        def tap(i):
            return jnp.sum(jnp.where(sub == i, w, 0.0), axis=0, keepdims=True)

        y = sh3(x) * tap(0)
        y = y + sh2(x) * tap(1)
        y = y + sh1(x) * tap(2)
        y = y + x * tap(3)
        s = _silu(y)
        if mode == "v":
            return (s,)
        n = s * lax.rsqrt(jnp.sum(s * s, axis=-1, keepdims=True) + EPS)
        if mode == "q":
            n = n * (LANES ** -0.5)
        return (n,)

    return f


def f_post(h, o, z, g):
    r = lax.rsqrt(jnp.mean(o * o, axis=-1, keepdims=True) + EPS)
    return (o * r * g * _silu(z),)


def f_merge(h, ga, gb, ya, yb):
    return (jax.nn.sigmoid(ga) * ya + jax.nn.sigmoid(gb) * yb,)


def f_delta(h, do, o):
    return (jnp.broadcast_to(jnp.sum(do * o, axis=1, keepdims=True), o.shape),)


def cumsum_time(name, x, nseq, seq, reverse):
    nb = seq // LANES

    def body(x_ref, o_ref):
        r, c = _iota((LANES, LANES), 0), _iota((LANES, LANES), 1)
        tri = jnp.where((r <= c) if reverse else (r >= c), 1.0, 0.0).astype(F32)
        carry = jnp.zeros((1, LANES), F32)
        for b in (range(nb - 1, -1, -1) if reverse else range(nb)):
            blk = x_ref[b * LANES:(b + 1) * LANES, :]
            o_ref[b * LANES:(b + 1) * LANES, :] = _dot_mask(tri, blk, "nn") + carry
            carry = carry + jnp.sum(blk, axis=0, keepdims=True)

    spec = pl.BlockSpec((seq, LANES), lambda s: (s, 0))
    return pl.pallas_call(body, name=name, grid=(nseq,), in_specs=[spec], out_specs=spec,
                          out_shape=jax.ShapeDtypeStruct(x.shape, F32), compiler_params=_params(1))(x)


def transpose_time(name, x, nseq, seq):
    def body(x_ref, o_ref):
        o_ref[...] = x_ref[...].T

    return pl.pallas_call(
        body, name=name, grid=(nseq,), in_specs=[pl.BlockSpec((seq, LANES), lambda s: (s, 0))],
        out_specs=pl.BlockSpec((LANES, seq), lambda s: (s, 0)),
        out_shape=jax.ShapeDtypeStruct((nseq * LANES, seq), F32), compiler_params=_params(1))(x)


def _gdn_masks():
    n = GDN_ROWS
    r, c = _iota((n, n), 0), _iota((n, n), 1)
    shift = GDN_CHUNK.bit_length() - 1
    same = lax.shift_right_logical(r, shift) == lax.shift_right_logical(c, shift)
    return r, c, same


def _gdn_decay(gb):
    r, c, same = _gdn_masks()
    seg_tril = jnp.where(jnp.logical_and(same, r >= c), 1.0, 0.0).astype(F32)
    g_cum = mm_mask(seg_tril, gb)
    lane0 = _iota(g_cum.shape, 1) == 0
    g_col = jnp.sum(jnp.where(lane0, g_cum, 0.0), axis=1, keepdims=True)
    g_row = jnp.sum(jnp.where(r == c, jnp.broadcast_to(g_col, (GDN_ROWS, GDN_ROWS)), 0.0), axis=0, keepdims=True)
    return g_cum, g_col - g_row


def gdn_f1(q, k, gb, bb):
    r, c, same = _gdn_masks()
    strict = jnp.logical_and(same, r > c)
    _, diff = _gdn_decay(gb)
    lane0 = _iota(bb.shape, 1) == 0
    beta_col = jnp.sum(jnp.where(lane0, bb, 0.0), axis=1, keepdims=True)
    kk = _dot(k, k, "nt", LO)
    return jnp.where(strict, beta_col * kk * jnp.exp(jnp.where(strict, diff, 0.0)), 0.0)


def gdn_f2(t_corr, q, k, v, gb, bb):
    r, c, same = _gdn_masks()
    incl = jnp.logical_and(same, r >= c)
    g_cum, diff = _gdn_decay(gb)
    decay = jnp.where(incl, jnp.exp(jnp.where(incl, diff, 0.0)), 0.0)
    e_g = jnp.exp(g_cum)
    v_beta, k_beta = v * bb, k * bb * e_g
    value = v_beta + _dot(t_corr, v_beta, "nn", LO)
    k_cum = k_beta + _dot(t_corr, k_beta, "nn", LO)
    attn = _dot(q, k, "nt", LO) * decay
    g_last = mm_mask(jnp.where(same, 1.0, 0.0).astype(F32), gb)
    return value, k_cum, attn, q * e_g, k * jnp.exp(g_last - g_cum)


def tri_inverse(a):
    n = GDN_ROWS
    r, c = _iota((n, n), 0), _iota((n, n), 1)
    shift = GDN_BASE.bit_length() - 1
    blk = lax.shift_right_logical(r, shift) == lax.shift_right_logical(c, shift)
    d = jnp.where(blk, a, 0.0)
    lo = a - d
    p = -d
    c_d = p
    for _ in range(shift - 1):
        p = _dot(p, p, "nn", LO)
        c_d = c_d + p + _dot(c_d, p, "nn", LO)
    assert GDN_CHUNK // GDN_BASE == 4
    nmat = lo + _dot(c_d, lo, "nn", LO)
    n2 = _dot(nmat, nmat, "nn", LO)
    c_n = (n2 - nmat) - _dot(nmat, n2, "nn", LO)
    return c_n + c_d + _dot(c_n, c_d, "nn", LO)


def gdn_a_fwd(q, k, v, gb, bb, rows):
    blk = pl.BlockSpec((GDN_ROWS, LANES), lambda i, h: (i, h))
    sq = pl.BlockSpec((GDN_ROWS, GDN_ROWS), lambda i, h: (i, h))

    def body(q_ref, k_ref, v_ref, gb_ref, bb_ref, val_ref, kc_ref, at_ref, qd_ref, kd_ref, t_ref):
        qv, kv, vv, gv, bv = q_ref[...], k_ref[...], v_ref[...], gb_ref[...], bb_ref[...]
        t_inv = tri_inverse(gdn_f1(qv, kv, gv, bv))
        value, k_cum, attn, q_dec, k_dec = gdn_f2(t_inv, qv, kv, vv, gv, bv)
        val_ref[...], kc_ref[...], at_ref[...], qd_ref[...], kd_ref[...], t_ref[...] = value, k_cum, attn, q_dec, k_dec, t_inv

    wide = jax.ShapeDtypeStruct((rows, NH * LANES), F32)
    square = jax.ShapeDtypeStruct((rows, NH * GDN_ROWS), F32)
    return pl.pallas_call(
        body, name="gdn_a_fwd", grid=(rows // GDN_ROWS, NH), in_specs=[blk] * 5,
        out_specs=[blk, blk, sq, blk, blk, sq], out_shape=[wide, wide, square, wide, wide, square],
        compiler_params=_params(2))(q, k, v, gb, bb)


def gdn_a_bwd(q, k, v, gb, bb, t_inv, dval, dkc, dat, dqd, dkd, dgb_b, rows):
    blk = pl.BlockSpec((GDN_ROWS, LANES), lambda i, h: (i, h))
    sq = pl.BlockSpec((GDN_ROWS, GDN_ROWS), lambda i, h: (i, h))

    def body(q_ref, k_ref, v_ref, gb_ref, bb_ref, t_ref, dval_ref, dkc_ref, dat_ref, dqd_ref, dkd_ref, dgbb_ref,
             dq_ref, dk_ref, dv_ref, dgb_ref, dbb_ref):
        qv, kv, vv, gv, bv, tv = q_ref[...], k_ref[...], v_ref[...], gb_ref[...], bb_ref[...], t_ref[...]
        _, vjp1 = jax.vjp(gdn_f1, qv, kv, gv, bv)
        _, vjp2 = jax.vjp(gdn_f2, tv, qv, kv, vv, gv, bv)
        dt, dq2, dk2, dv2, dgb2, dbb2 = vjp2((dval_ref[...], dkc_ref[...], dat_ref[...], dqd_ref[...], dkd_ref[...]))
        left = dt + _dot(tv, dt, "tn", LO)
        da = -(left + _dot(left, tv, "nt", LO))
        dq1, dk1, dgb1, dbb1 = vjp1(da)
        dq_ref[...] = dq1 + dq2
        dk_ref[...] = dk1 + dk2
        dv_ref[...] = dv2
        dgb_ref[...] = dgb1 + dgb2 + dgbb_ref[...]
        dbb_ref[...] = dbb1 + dbb2

    wide = jax.ShapeDtypeStruct((rows, NH * LANES), F32)
    return pl.pallas_call(
        body, name="gdn_a_bwd", grid=(rows // GDN_ROWS, NH),
        in_specs=[blk] * 5 + [sq, blk, blk, sq, blk, blk, blk], out_specs=[blk] * 5, out_shape=[wide] * 5,
        compiler_params=_params(2))(q, k, v, gb, bb, t_inv, dval, dkc, dat, dqd, dkd, dgb_b)


N_CH = GDN_ROWS // GDN_CHUNK


GDN_HP = 8


def gdn_fb(*args):
    per_head = 6 * N_CH
    states = list(args[GDN_HP * per_head:])
    outs = [[None] * N_CH for _ in range(GDN_HP)]
    zero = jnp.zeros((GDN_CHUNK, LANES), F32)
    for c in range(N_CH):
        for hh in range(GDN_HP):
            val, kc, at, qd, kd, gb = (args[hh * per_head + i * N_CH + c] for i in range(6))
            s = states[hh]
            v_new = val - _dot(kc, s, "nn", LO)
            v_pad = jnp.concatenate([zero] * c + [v_new] + [zero] * (N_CH - 1 - c), axis=0)
            outs[hh][c] = _dot(qd, s, "nn", LO) + _dot(at, v_pad, "nn", LO)
            dec = jnp.exp(jnp.sum(gb, axis=0, keepdims=True))
            states[hh] = s * dec + _dot(kd, v_new, "tn", LO)
    return (*[o for head in outs for o in head], *states)


def _gdn_piece(ref, hh, c):
    width = ref.shape[1] // GDN_HP
    return ref.at[c * GDN_CHUNK:(c + 1) * GDN_CHUNK, hh * width:(hh + 1) * width]


def _gdn_pieces(refs, hh):
    return [_gdn_piece(r, hh, c)[...] for r in refs for c in range(N_CH)]


def _gdn_b_specs(nb, rev):
    def blk_row(s, j):
        return s * nb + (nb - 1 - j if rev else j)

    blk = pl.BlockSpec((GDN_ROWS, GDN_HP * LANES), lambda s, hb, j: (blk_row(s, j), hb))
    sq = pl.BlockSpec((GDN_ROWS, GDN_HP * GDN_ROWS), lambda s, hb, j: (blk_row(s, j), hb))
    snap = pl.BlockSpec((GDN_HP * LANES, LANES), lambda s, hb, j: (blk_row(s, j) * (NH // GDN_HP) + hb, 0))
    return blk, sq, snap


def gdn_b_fwd(val, kc, at, qd, kd, gb, nseq, seq):
    nb = seq // GDN_ROWS
    rows = nseq * seq
    blk, sq, snap = _gdn_b_specs(nb, False)

    def body(val_ref, kc_ref, at_ref, qd_ref, kd_ref, gb_ref, o_ref, snap_ref, s_ref):
        @pl.when(pl.program_id(2) == 0)
        def _():
            s_ref[...] = jnp.zeros_like(s_ref)

        states = [s_ref[hh] for hh in range(GDN_HP)]
        for hh in range(GDN_HP):
            snap_ref[hh * LANES:(hh + 1) * LANES, :] = states[hh]
        pieces = [p for hh in range(GDN_HP) for p in _gdn_pieces([val_ref, kc_ref, at_ref, qd_ref, kd_ref, gb_ref], hh)]
        res = gdn_fb(*pieces, *states)
        for hh in range(GDN_HP):
            for c in range(N_CH):
                _gdn_piece(o_ref, hh, c)[...] = res[hh * N_CH + c]
            s_ref[hh] = res[GDN_HP * N_CH + hh]

    return pl.pallas_call(
        body, name="gdn_b_fwd", grid=(nseq, NH // GDN_HP, nb), in_specs=[blk, blk, sq, blk, blk, blk], out_specs=[blk, snap],
        out_shape=[jax.ShapeDtypeStruct((rows, NH * LANES), F32), jax.ShapeDtypeStruct((nseq * nb * NH * LANES, LANES), F32)],
        scratch_shapes=[pltpu.VMEM((GDN_HP, LANES, LANES), F32)], compiler_params=_params(3))(val, kc, at, qd, kd, gb)


def gdn_b_bwd(val, kc, at, qd, kd, gb, snaps, do, nseq, seq):
    nb = seq // GDN_ROWS
    rows = nseq * seq
    blk, sq, snap = _gdn_b_specs(nb, True)

    def body(val_ref, kc_ref, at_ref, qd_ref, kd_ref, gb_ref, snap_ref, do_ref,
             dval_ref, dkc_ref, dat_ref, dqd_ref, dkd_ref, dgb_ref, ds_ref):
        @pl.when(pl.program_id(2) == 0)
        def _():
            ds_ref[...] = jnp.zeros_like(ds_ref)

        pieces = [p for hh in range(GDN_HP) for p in _gdn_pieces([val_ref, kc_ref, at_ref, qd_ref, kd_ref, gb_ref], hh)]
        states = [snap_ref[hh * LANES:(hh + 1) * LANES, :] for hh in range(GDN_HP)]
        _, vjp = jax.vjp(gdn_fb, *pieces, *states)
        cts = [p for hh in range(GDN_HP) for p in _gdn_pieces([do_ref], hh)] + [ds_ref[hh] for hh in range(GDN_HP)]
        grads = vjp(tuple(cts))
        for hh in range(GDN_HP):
            for i, r in enumerate([dval_ref, dkc_ref, dat_ref, dqd_ref, dkd_ref, dgb_ref]):
                for c in range(N_CH):
                    _gdn_piece(r, hh, c)[...] = grads[hh * 6 * N_CH + i * N_CH + c]
            ds_ref[hh] = grads[GDN_HP * 6 * N_CH + hh]

    wide = jax.ShapeDtypeStruct((rows, NH * LANES), F32)
    square = jax.ShapeDtypeStruct((rows, NH * GDN_ROWS), F32)
    return pl.pallas_call(
        body, name="gdn_b_bwd", grid=(nseq, NH // GDN_HP, nb), in_specs=[blk, blk, sq, blk, blk, blk, snap, blk],
        out_specs=[blk, blk, sq, blk, blk, blk], out_shape=[wide, wide, square, wide, wide, wide],
        scratch_shapes=[pltpu.VMEM((GDN_HP, LANES, LANES), F32)], compiler_params=_params(3))(val, kc, at, qd, kd, gb, snaps, do)


FOX_Q, FOX_K, FOX_V = 4 * NH, 5 * NH, 6 * NH
FOX_SCALE = LANES ** -0.5


def _head_row(ct_ref, h, off, width):
    blk = ct_ref[:, pl.ds(off, width)]
    return jnp.sum(jnp.where(_iota(blk.shape, 0) == h, blk, 0.0), axis=0, keepdims=True)


def _col(x):
    return jnp.max(x, axis=1, keepdims=True)


def _row(x):
    return jnp.max(x.T, axis=0, keepdims=True)


def _causal(shape, q_dim):
    return _iota(shape, q_dim) >= _iota(shape, 1 - q_dim)


def fox_fwd(qn, kn, proj, ct, nseq, seq):
    tq = tk = min(ATT_TILE, seq)
    nq = seq // tq
    rows = nseq * seq
    qblk = pl.BlockSpec((tq, LANES), lambda s, h, i: (s * nq + i, h))
    full = pl.BlockSpec((seq, LANES), lambda s, h, i: (s, h))
    vfull = pl.BlockSpec((seq, LANES), lambda s, h, i: (s, h + FOX_V))
    ctb = pl.BlockSpec((NH, seq), lambda s, h, i: (s * (LANES // NH) + 2, 0))

    def body(q_ref, k_ref, v_ref, ct_ref, o_ref, o16_ref, lse_ref):
        h, i = pl.program_id(1), pl.program_id(2)
        q = q_ref[...]

        def step(j, carry, diag):
            m, l, acc = carry
            off = pl.multiple_of(j * tk, tk)
            s = _dot(q, k_ref[pl.ds(off, tk), :], "nt") * FOX_SCALE - _head_row(ct_ref, h, off, tk)
            if diag:
                s = jnp.where(_causal(s.shape, 0), s, NEG)
            m_new = jnp.maximum(m, jnp.max(s, axis=1, keepdims=True))
            p = jnp.exp(s - m_new)
            alpha = jnp.exp(m - m_new)
            l = alpha * l + jnp.sum(p, axis=1, keepdims=True)
            acc = alpha * acc + _dot(p.astype(BF16), v_ref[pl.ds(off, tk), :].astype(BF16), "nn")
            return m_new, l, acc

        init = (jnp.full((tq, 1), NEG, F32), jnp.zeros((tq, 1), F32), jnp.zeros((tq, LANES), F32))
        carry = lax.fori_loop(0, i, lambda j, c: step(j, c, False), init)
        m, l, acc = step(i, carry, True)
        o = acc / l
        o_ref[...] = o
        o16_ref[...] = o.astype(BF16)
        lse_ref[...] = jnp.broadcast_to(m + jnp.log(l), (tq, LANES))

    wide = (rows, NH * LANES)
    return pl.pallas_call(
        body, name="fox_fwd", grid=(nseq, NH, nq), in_specs=[qblk, full, vfull, ctb], out_specs=[qblk] * 3,
        out_shape=[jax.ShapeDtypeStruct(wide, F32), jax.ShapeDtypeStruct(wide, BF16), jax.ShapeDtypeStruct(wide, F32)],
        compiler_params=_params(3))(qn, kn, proj, ct)


def fox_dq(qn, kn, proj, ct, do, lse, delta, nseq, seq):
    tq = tk = min(ATT_TILE, seq)
    nq = seq // tq
    rows = nseq * seq
    qblk = pl.BlockSpec((tq, LANES), lambda s, h, i: (s * nq + i, h))
    full = pl.BlockSpec((seq, LANES), lambda s, h, i: (s, h))
    vfull = pl.BlockSpec((seq, LANES), lambda s, h, i: (s, h + FOX_V))
    ctb = pl.BlockSpec((NH, seq), lambda s, h, i: (s * (LANES // NH) + 2, 0))

    def body(q_ref, k_ref, v_ref, ct_ref, do_ref, lse_ref, dl_ref, dq_ref, dc_ref):
        h, i = pl.program_id(1), pl.program_id(2)
        q = q_ref[...]
        lse, delta = _col(lse_ref[...]), _col(dl_ref[...])
        do16 = do_ref[...].astype(BF16)

        def step(j, carry, diag):
            dq, dc = carry
            off = pl.multiple_of(j * tk, tk)
            k = k_ref[pl.ds(off, tk), :]
            p = jnp.exp(_dot(q, k, "nt") * FOX_SCALE - _head_row(ct_ref, h, off, tk) - lse)
            if diag:
                p = jnp.where(_causal(p.shape, 0), p, 0.0)
            dp = _dot(do16, v_ref[pl.ds(off, tk), :].astype(BF16), "nt")
            ds = p * (dp - delta)
            return dq + _dot(ds.astype(BF16), k, "nn"), dc + jnp.sum(ds, axis=1, keepdims=True)

        init = (jnp.zeros((tq, LANES), F32), jnp.zeros((tq, 1), F32))
        dq, dc = step(i, lax.fori_loop(0, i, lambda j, c: step(j, c, False), init), True)
        dq_ref[...] = dq * FOX_SCALE
        dc_ref[...] = jnp.where(_iota((tq, LANES), 1) == 0, dc, 0.0)

    wide = jax.ShapeDtypeStruct((rows, NH * LANES), F32)
    return pl.pallas_call(
        body, name="fox_dq", grid=(nseq, NH, nq), in_specs=[qblk, full, vfull, ctb, qblk, qblk, qblk],
        out_specs=[qblk, qblk], out_shape=[wide, wide], compiler_params=_params(3))(qn, kn, proj, ct, do, lse, delta)


def fox_dkv(qn, kn, proj, cb, do, lse, delta, nseq, seq):
    tq = tk = min(ATT_TILE, seq)
    nq = seq // tq
    rows = nseq * seq
    kblk = pl.BlockSpec((tk, LANES), lambda s, h, j: (s * nq + j, h))
    vblk = pl.BlockSpec((tk, LANES), lambda s, h, j: (s * nq + j, h + FOX_V))
    full = pl.BlockSpec((seq, LANES), lambda s, h, j: (s, h))

    def body(q_ref, k_ref, v_ref, cb_ref, do_ref, lse_ref, dl_ref, dk_ref, dv_ref, dc_ref):
        j = pl.program_id(2)
        k = k_ref[...]
        v16 = v_ref[...].astype(BF16)
        ck = _col(cb_ref[...])

        def step(i, carry, diag):
            dk, dv, dc = carry
            off = pl.multiple_of(i * tq, tq)
            q = q_ref[pl.ds(off, tq), :]
            do16 = do_ref[pl.ds(off, tq), :].astype(BF16)
            lse, delta = (_row(r[pl.ds(off, tq), :]) for r in (lse_ref, dl_ref))
            p = jnp.exp(_dot(k, q, "nt") * FOX_SCALE - ck - lse)
            if diag:
                p = jnp.where(_causal(p.shape, 1), p, 0.0)
            dv = dv + _dot(p.astype(BF16), do16, "nn")
            ds = p * (_dot(v16, do16, "nt") - delta)
            return dk + _dot(ds.astype(BF16), q, "nn"), dv, dc + jnp.sum(ds, axis=1, keepdims=True)

        zero = jnp.zeros((tk, LANES), F32)
        carry = step(j, (zero, zero, jnp.zeros((tk, 1), F32)), True)
        dk, dv, dc = lax.fori_loop(j + 1, nq, lambda i, c: step(i, c, False), carry)
        dk_ref[...] = dk * FOX_SCALE
        dv_ref[...] = dv.astype(BF16)
        dc_ref[...] = jnp.where(_iota((tk, LANES), 1) == 0, -dc, 0.0)

    wide = (rows, NH * LANES)
    return pl.pallas_call(
        body, name="fox_dkv", grid=(nseq, NH, nq), in_specs=[full, kblk, vblk, kblk, full, full, full],
        out_specs=[kblk, kblk, kblk],
        out_shape=[jax.ShapeDtypeStruct(wide, F32), jax.ShapeDtypeStruct(wide, BF16), jax.ShapeDtypeStruct(wide, F32)],
        compiler_params=_params(3))(qn, kn, proj, cb, do, lse, delta)


def loss_head(out, tgt, rows, width):
    tb = ROW_TILE
    blk = pl.BlockSpec((tb, width), lambda i: (i, 0))
    accb = pl.BlockSpec((8, LANES), lambda i: (0, 0))

    def body(o_ref, t_ref, d32_ref, d16_ref, acc_ref):
        d = o_ref[...] - t_ref[...]
        row_loss = 0.5 * jnp.mean(d * d, axis=1, keepdims=True)
        g = d * (1.0 / width)
        d32_ref[...] = g
        d16_ref[...] = g.astype(BF16)
        part = jnp.where(_iota((tb, LANES), 1) == 0, row_loss, 0.0).reshape(tb // 8, 8, LANES).sum(axis=0)

        @pl.when(pl.program_id(0) == 0)
        def _():
            acc_ref[...] = part

        @pl.when(pl.program_id(0) != 0)
        def _():
            acc_ref[...] += part

    return pl.pallas_call(
        body, name="loss_head", grid=(rows // tb,), in_specs=[blk, blk], out_specs=[blk, blk, accb],
        out_shape=[jax.ShapeDtypeStruct((rows, width), F32), jax.ShapeDtypeStruct((rows, width), BF16),
                   jax.ShapeDtypeStruct((8, LANES), F32)], compiler_params=_params(1))(out, tgt)


def _adamw_update(w, g, m, v):
    m_new = ADAM_B1 * m + (1.0 - ADAM_B1) * g
    v_new = ADAM_B2 * v + (1.0 - ADAM_B2) * (g * g)
    m_hat = m_new / (1.0 - ADAM_B1 ** ADAM_STEP)
    v_hat = v_new / (1.0 - ADAM_B2 ** ADAM_STEP)
    return -ADAM_LR * (m_hat / (jnp.sqrt(v_hat) + ADAM_EPS) + ADAM_WD * w), m_new, v_new


def adamw(name, w, g, m, v):
    rows, cols = w.shape
    tb = min(rows, 128)
    assert rows % tb == 0
    blk = pl.BlockSpec((tb, cols), lambda i: (i, 0))

    def body(w_ref, g_ref, m_ref, v_ref, d_ref, mo_ref, vo_ref):
        d_ref[...], mo_ref[...], vo_ref[...] = _adamw_update(w_ref[...], g_ref[...], m_ref[...], v_ref[...])

    shp = jax.ShapeDtypeStruct(w.shape, F32)
    return pl.pallas_call(body, name=name, grid=(rows // tb,), in_specs=[blk] * 4, out_specs=[blk] * 3,
                          out_shape=[shp] * 3, compiler_params=_params(1))(w, g, m, v)


SPLIT_TILE = 128


def _tiled(shape2d, ax, n_lead, index):
    blk = (SPLIT_TILE, shape2d[1]) if ax == 0 else (shape2d[0], SPLIT_TILE)

    def index_map(*args):
        *lead, t = index(*args)
        return (*lead, t, 0) if ax == 0 else (*lead, 0, t)

    return pl.BlockSpec((None,) * n_lead + blk, index_map)


def adamw_halves(name, w, mine, other, m, v, c, ax):
    steps = w.shape[ax] // 2 // SPLIT_TILE
    assert w.shape[ax] == 2 * steps * SPLIT_TILE

    def body(c_ref, w_ref, mine_ref, other_ref, m_ref, v_ref, g_ref, d_ref, mo_ref, vo_ref):
        g = jnp.where(pl.program_id(0) // steps == c_ref[0], mine_ref[...], other_ref[...])
        g_ref[...] = g
        d_ref[...], mo_ref[...], vo_ref[...] = _adamw_update(w_ref[...], g, m_ref[...], v_ref[...])

    blk = _tiled(w.shape, ax, 0, lambda i, c_ref: (i,))
    hblk = _tiled(mine.shape, ax, 0, lambda i, c_ref: (i % steps,))
    grid_spec = pltpu.PrefetchScalarGridSpec(num_scalar_prefetch=1, grid=(2 * steps,),
                                             in_specs=[blk, hblk, hblk, blk, blk], out_specs=[blk] * 4)
    shp = jax.ShapeDtypeStruct(w.shape, F32)
    return pl.pallas_call(body, name=name, grid_spec=grid_spec, out_shape=[shp] * 4,
                          compiler_params=_params(1))(c, w, mine, other, m, v)


def add_chips(name, slots, parts, chip, axes):
    outs = []
    for idx, (x, own, ax) in enumerate(zip(slots, parts, axes)):
        n, shape2d = x.shape[0], x.shape[1:]
        steps = shape2d[ax] // SPLIT_TILE
        assert shape2d[ax] == steps * SPLIT_TILE

        def body(me_ref, *refs, n=n):
            o_ref = refs[n + 1]
            acc = None
            for t in range(n):
                term = jnp.where(me_ref[0] == t, refs[n][...], refs[t][...]).astype(F32)
                acc = term if acc is None else acc + term
            o_ref[...] = acc

        def filled(t, n=n):
            return lambda i, me_ref: (jnp.where(me_ref[0] == t, (t + 1) % n, t), i)

        grid_spec = pltpu.PrefetchScalarGridSpec(
            num_scalar_prefetch=1, grid=(steps,),
            in_specs=[_tiled(shape2d, ax, 1, filled(t)) for t in range(n)]
            + [_tiled(shape2d, ax, 1, lambda i, me_ref: (me_ref[0], i))],
            out_specs=_tiled(shape2d, ax, 0, lambda i, me_ref: (i,)))
        outs.append(pl.pallas_call(
            body, name=f"{name}_{idx}", grid_spec=grid_spec, out_shape=jax.ShapeDtypeStruct(shape2d, F32),
            compiler_params=_params(1))(chip, *([x] * n), own))
    return outs


def add_pair(name, gs, rs, c, axes):
    outs = []
    for idx, (g, r, ax) in enumerate(zip(gs, rs, axes)):
        nb = r.shape[0]
        steps = r.shape[1 + ax] // SPLIT_TILE
        assert r.shape[1 + ax] == steps * SPLIT_TILE

        def body(c_ref, g_ref, r_ref, o_ref):
            o_ref[...] = (g_ref[...] + r_ref[...]).astype(BF16)

        grid_spec = pltpu.PrefetchScalarGridSpec(
            num_scalar_prefetch=1, grid=(nb, steps),
            in_specs=[_tiled(g.shape[1:], ax, 1, lambda b, i, c_ref: (b, c_ref[0] * steps + i)),
                      _tiled(r.shape[1:], ax, 1, lambda b, i, c_ref: (b, i))],
            out_specs=_tiled(r.shape[1:], ax, 1, lambda b, i, c_ref: (b, i)))
        outs.append(pl.pallas_call(
            body, name=f"{name}_{idx}", grid_spec=grid_spec, out_shape=jax.ShapeDtypeStruct(r.shape, BF16),
            compiler_params=_params(2))(c, g, r))
    return outs


def _place():
    x, y, c = lax.axis_index("x"), lax.axis_index("y"), lax.axis_index("c")
    return x, y, c, [(1 - x, y), (x, 1 - y), (1 - x, 1 - y)]


def _remote(src, dst, send_sem, recv_sem, dev):
    return pltpu.make_async_remote_copy(src_ref=src, dst_ref=dst, send_sem=send_sem, recv_sem=recv_sem,
                                        device_id=dev, device_id_type=MESH)


def _half(ref, lead, ax, which):
    size = ref.shape[len(lead) + ax] // 2
    part = pl.ds(which * size, size)
    return ref.at[(*lead, part, slice(None)) if ax == 0 else (*lead, slice(None), part)]


def gather_weights(shards, axes):
    n = len(shards)

    def body(*refs):
        ins, outs = refs[:n], refs[n:2 * n]
        ici_s, ici_r, d2d_s, d2d_r = refs[2 * n:]
        x, y, c, chips = _place()
        me = 2 * x + y
        sends, passes = [], []
        for w in range(n):
            for j, (ox, oy) in enumerate(chips):
                cp = _remote(_half(ins[w], (), axes[w], c), _half(outs[w], (me,), axes[w], c),
                             ici_s.at[3 * w + j], ici_r.at[3 * w + j], (ox, oy, c))
                cp.start()
                sends.append(cp)
        for w in range(n):
            for j, (ox, oy) in enumerate(chips):
                landed = _half(outs[w], (2 * ox + oy,), axes[w], c)
                _remote(landed, landed, ici_s.at[3 * w + j], ici_r.at[3 * w + j], (ox, oy, c)).wait_recv()
                cp = _remote(landed, landed, d2d_s.at[3 * w + j], d2d_r.at[3 * w + j], (x, y, 1 - c))
                cp.start()
                passes.append(cp)
        for w in range(n):
            for j, (ox, oy) in enumerate(chips):
                other = _half(outs[w], (2 * ox + oy,), axes[w], 1 - c)
                _remote(other, other, d2d_s.at[3 * w + j], d2d_r.at[3 * w + j], (x, y, 1 - c)).wait_recv()
        for cp in sends + passes:
            cp.wait_send()

    return pl.pallas_call(
        body, name="gather_weights", in_specs=[ANY] * n, out_specs=[ANY] * n,
        out_shape=[jax.ShapeDtypeStruct((4,) + s.shape, s.dtype) for s in shards],
        scratch_shapes=[pltpu.SemaphoreType.DMA((3 * n,))] * 4,
    )(*shards)


def pair_swap(grads, axes):
    n = len(grads)

    def body(*refs):
        ins, outs = refs[:n], refs[n:2 * n]
        send, recv = refs[2 * n:]
        x, y, c, _ = _place()
        cps = []
        for w in range(n):
            cp = _remote(_half(ins[w], (slice(None),), axes[w], 1 - c), outs[w], send.at[w], recv.at[w], (x, y, 1 - c))
            cp.start()
            cps.append(cp)
        for cp in cps:
            cp.wait_recv()
        for cp in cps:
            cp.wait_send()

    def halved(g, ax):
        return tuple(d // 2 if i == 1 + ax else d for i, d in enumerate(g.shape))

    return pl.pallas_call(
        body, name="pair_swap", in_specs=[ANY] * n, out_specs=[ANY] * n,
        out_shape=[jax.ShapeDtypeStruct(halved(g, ax), g.dtype) for g, ax in zip(grads, axes)],
        scratch_shapes=[pltpu.SemaphoreType.DMA((n,))] * 2,
    )(*grads)


def chip_exchange(parts):
    n = len(parts)

    def body(*refs):
        ins, outs = refs[:n], refs[n:2 * n]
        send, recv = refs[2 * n:]
        x, y, c, chips = _place()
        me = 2 * x + y
        cps = []
        for w in range(n):
            for j, (ox, oy) in enumerate(chips):
                cp = _remote(ins[w].at[2 * ox + oy], outs[w].at[me], send.at[3 * w + j], recv.at[3 * w + j], (ox, oy, c))
                cp.start()
                cps.append(cp)
        for w in range(n):
            for j, (ox, oy) in enumerate(chips):
                slot = outs[w].at[2 * ox + oy]
                _remote(slot, slot, send.at[3 * w + j], recv.at[3 * w + j], (ox, oy, c)).wait_recv()
        for cp in cps:
            cp.wait_send()

    return pl.pallas_call(
        body, name="chip_exchange", in_specs=[ANY] * n, out_specs=[ANY] * n,
        out_shape=[jax.ShapeDtypeStruct(p.shape, p.dtype) for p in parts],
        scratch_shapes=[pltpu.SemaphoreType.DMA((3 * n,))] * 2,
    )(*parts)


def pair_send(halves):
    n = len(halves)

    def body(*refs):
        ins, outs = refs[:n], refs[n:2 * n]
        send, recv = refs[2 * n:]
        x, y, c, _ = _place()
        cps = [_remote(ins[w], outs[w], send.at[w], recv.at[w], (x, y, 1 - c)) for w in range(n)]
        for cp in cps:
            cp.start()
        for cp in cps:
            cp.wait_recv()
        for cp in cps:
            cp.wait_send()

    return pl.pallas_call(
        body, name="pair_send", in_specs=[ANY] * n, out_specs=[ANY] * n,
        out_shape=[jax.ShapeDtypeStruct(h.shape, h.dtype) for h in halves],
        scratch_shapes=[pltpu.SemaphoreType.DMA((n,))] * 2,
    )(*halves)


def all_reduce_small(name, vec):
    rows = vec.shape[0]

    def body(v_ref, o_ref, buf, send, recv):
        x, y, c, _ = _place()
        me = 4 * x + 2 * y + c
        buf[me] = v_ref[...]
        cps = []
        for k in range(1, 8):
            kx, ky, kc = (k >> 2) & 1, (k >> 1) & 1, k & 1
            peer = (x if kx == 0 else 1 - x, y if ky == 0 else 1 - y, c if kc == 0 else 1 - c)
            cp = _remote(v_ref, buf.at[me], send.at[k - 1], recv.at[k - 1], peer)
            cp.start()
            cps.append(cp)
        for k in range(1, 8):
            kx, ky, kc = (k >> 2) & 1, (k >> 1) & 1, k & 1
            px, py, pc = (x if kx == 0 else 1 - x, y if ky == 0 else 1 - y, c if kc == 0 else 1 - c)
            slot = buf.at[4 * px + 2 * py + pc]
            _remote(slot, slot, send.at[k - 1], recv.at[k - 1], (px, py, pc)).wait_recv()
        for cp in cps:
            cp.wait_send()
        acc = buf[0]
        for d in range(1, 8):
            acc = acc + buf[d]
        o_ref[...] = acc

    vm = pl.BlockSpec(memory_space=pltpu.VMEM)
    return pl.pallas_call(
        body, name=name, in_specs=[vm], out_specs=vm, out_shape=jax.ShapeDtypeStruct(vec.shape, F32),
        scratch_shapes=[pltpu.VMEM((8, rows, LANES), F32), pltpu.SemaphoreType.DMA((7,)), pltpu.SemaphoreType.DMA((7,))],
    )(vec)


def local_step(x2, tgt2, g1, g2, gdn_ng, qn_g, kn_g, p1, p2, conv_w, wt_main, wt_small, p_a, p_b, w_o, w_u, w_d, nseq, seq):
    rows, dm = x2.shape
    wide = NH * LANES
    row = lambda a, off=0, w=None: (a, "row", off, a.shape[1] if w is None else w)
    rowh = lambda a, off=0, w=LANES: (a, "rowh", off, w)
    par = lambda a: (a, "par", 0, a.shape[1])
    parh = lambda a, off=0: (a, "parh", off, LANES)
    o_row = lambda w, dt: (w, "row", w, dt)
    o_rowh = lambda dt, tw=wide, w=LANES: (tw, "rowh", w, dt)

    u, = ew_fwd("rms1", f_rms, [row(x2), par(g1)], [o_row(dm, BF16)], rows)
    proj = matmul("mm_in", u, wt_main, "nt", F32)
    sp = matmul("mm_in_small", u, wt_small, "nt", F32)
    so, = ew_fwd("small", f_small, [row(sp), par(p1), par(p2)], [o_row(LANES, F32)], rows)
    cs = cumsum_time("cumsum", so, nseq, seq, False)
    gb, bb, cb = ew_fwd("bcast", f_bcast, [row(so), row(cs)], [o_rowh(F32)] * 3, rows, NH)
    ct = transpose_time("c_time_major", cs, nseq, seq)
    conv = {}
    for mode, off in (("q", 0), ("k", NH), ("v", 2 * NH)):
        conv[mode], = ew_fwd(f"conv_{mode}", make_f_conv(mode), [rowh(proj, off), parh(conv_w, off)], [o_rowh(F32)],
                             rows, NH, seq, "hi", CONV_HEADS)
    val, kcum, attn, qdec, kdec, t_inv = gdn_a_fwd(conv["q"], conv["k"], conv["v"], gb, bb, rows)
    o_a, snaps = gdn_b_fwd(val, kcum, attn, qdec, kdec, gb, nseq, seq)
    ya_in, = ew_fwd("gdn_post", f_post, [rowh(o_a), rowh(proj, 3 * NH), par(gdn_ng)], [o_rowh(BF16)], rows, NH)
    fqn, = ew_fwd("fox_qn", f_rms, [rowh(proj, FOX_Q), par(qn_g)], [o_rowh(BF16)], rows, NH)
    fkn, = ew_fwd("fox_kn", f_rms, [rowh(proj, FOX_K), par(kn_g)], [o_rowh(BF16)], rows, NH)
    o_b, o_b16, lse = fox_fwd(fqn, fkn, proj, ct, nseq, seq)
    y_a = matmul("mm_pa", ya_in, p_a, "nn", F32, tn=1024)
    y_b = matmul("mm_pb", o_b16, p_b, "nn", F32, tn=1024)
    gates = [row(proj, 7, dm), row(proj, 8, dm)]
    merged, = ew_fwd("merge", f_merge, gates + [row(y_a), row(y_b)], [o_row(dm, BF16)], rows)
    hres = matmul("mm_out", merged, w_o, "nn", F32, add=x2, tn=1024)
    hn, = ew_fwd("rms2", f_rms, [row(hres), par(g2)], [o_row(dm, BF16)], rows)
    up_blocks = w_u.shape[0]
    act, relu2 = matmul("mm_up", hn, w_u, "nn", F32, col_blocks=up_blocks, out_dtypes=[F32, BF16],
                        epilogue=lambda r: [r, jnp.maximum(r, 0.0) * jnp.maximum(r, 0.0)])
    out = matmul("mm_down", relu2, w_d, "nn", F32, add=hres, tn=1024)
    dout, dout16, loss_acc = loss_head(out, tgt2, rows, dm)

    d_act = matmul("mm_d_act", dout16, w_d, "nt", BF16, extras=[act], epilogue=lambda r, a: [2.0 * jnp.maximum(a, 0.0) * r])
    dw_d = matmul("mm_dw_down", relu2, dout16, "tn", F32, tn=1024)
    dw_u = matmul("mm_dw_up", hn, d_act, "tn", F32, col_blocks=up_blocks)
    d_hn = matmul("mm_d_hn", d_act, w_u, "nt", F32, col_blocks=up_blocks)
    dh, dh16, dg2 = ew_bwd("rms2_b", f_rms, [row(hres), par(g2)], [(row(d_hn),)], [row(dout)],
                           lambda g, e: [g[0] + e[0], g[0] + e[0], g[1]],
                           [((rows, dm), "row", dm, F32, None), ((rows, dm), "row", dm, BF16, None), ((1, dm), "par", dm, F32, "all")], rows)
    d_merged = matmul("mm_d_merged", dh16, w_o, "nt", F32, tn=1024)
    dw_o = matmul("mm_dw_out", merged, dh16, "tn", F32, tn=1024)
    seg16 = ((rows, dm), "row", dm, BF16, None)
    d_ga16, d_gb16, d_ya16, d_yb16 = ew_bwd("merge_b", f_merge, gates + [row(y_a), row(y_b)], [(row(d_merged),)], [],
                                            lambda g, e: list(g), [seg16] * 4, rows)
    dp_a = matmul("mm_dp_a", ya_in, d_ya16, "tn", F32, tn=1024)
    d_ya_in = matmul("mm_d_ya_in", d_ya16, p_a, "nt", F32, tn=1024)
    dp_b = matmul("mm_dp_b", o_b16, d_yb16, "tn", F32, tn=1024)
    d_ob = matmul("mm_d_ob", d_yb16, p_b, "nt", F32, tn=1024)
    h32 = ((rows, wide), "rowh", LANES, F32, None)
    h16 = ((rows, wide), "rowh", LANES, BF16, None)
    gain = ((1, LANES), "par", LANES, F32, "all")
    d_oa, d_z16, d_gdn_ng = ew_bwd("gdn_post_b", f_post, [rowh(o_a), rowh(proj, 3 * NH), par(gdn_ng)], [(rowh(d_ya_in),)], [],
                                   lambda g, e: list(g), [h32, h16, gain], rows, NH)
    dval, dkc, dat, dqd, dkd, dgb_b = gdn_b_bwd(val, kcum, attn, qdec, kdec, gb, snaps, d_oa, nseq, seq)
    d_cq, d_ck, d_cv, d_gb, d_bb = gdn_a_bwd(conv["q"], conv["k"], conv["v"], gb, bb, t_inv, dval, dkc, dat, dqd, dkd, dgb_b, rows)
    d_pre, d_conv = {}, {}
    tap = ((4, wide), "parh", LANES, F32, "inner")
    for mode, off, ctg in (("q", 0, d_cq), ("k", NH, d_ck), ("v", 2 * NH, d_cv)):
        d_pre[mode], d_conv[mode] = ew_bwd(f"conv_{mode}_b", make_f_conv(mode), [rowh(proj, off), parh(conv_w, off)],
                                           [(rowh(ctg),)], [], lambda g, e: list(g), [h16, tap], rows, NH, seq, "hi", CONV_HEADS)
    delta, = ew_fwd("fox_delta", f_delta, [rowh(d_ob), rowh(o_b)], [o_rowh(F32)], rows, NH)
    d_fqn, d_cq_b = fox_dq(fqn, fkn, proj, ct, d_ob, lse, delta, nseq, seq)
    d_fkn, d_fv16, d_ck_b = fox_dkv(fqn, fkn, proj, cb, d_ob, lse, delta, nseq, seq)
    d_fq16, d_qn_g = ew_bwd("fox_qn_b", f_rms, [rowh(proj, FOX_Q), par(qn_g)], [(rowh(d_fqn),)], [], lambda g, e: list(g),
                            [h16, gain], rows, NH)
    d_fk16, d_kn_g = ew_bwd("fox_kn_b", f_rms, [rowh(proj, FOX_K), par(kn_g)], [(rowh(d_fkn),)], [], lambda g, e: list(g),
                            [h16, gain], rows, NH)
    narrow = ((rows, LANES), "row", LANES, F32, None)
    d_so, d_cs = ew_bwd("bcast_b", f_bcast, [row(so), row(cs)], [(rowh(d_gb),), (rowh(d_bb),), (rowh(d_cq_b), rowh(d_ck_b))], [],
                        lambda g, e: list(g), [narrow, narrow], rows, NH)
    d_logf = cumsum_time("cumsum_b", d_cs, nseq, seq, True)
    vec = ((1, LANES), "par", LANES, F32, "all")
    d_sp16, d_p1, d_p2 = ew_bwd("small_b", f_small, [row(sp), par(p1), par(p2)], [(row(d_so), row(d_logf))], [],
                                lambda g, e: list(g), [((rows, LANES), "row", LANES, BF16, None), vec, vec], rows)
    d_proj16 = jnp.concatenate([d_pre["q"], d_pre["k"], d_pre["v"], d_z16, d_fq16, d_fk16, d_fv16, d_ga16, d_gb16], axis=1)
    dw_main = matmul("mm_dw_main", d_proj16, u, "tn", F32)
    dw_small = matmul("mm_dw_small", d_sp16, u, "tn", F32)
    d_u = matmul("mm_d_u_small", d_sp16, wt_small, "nn", F32)
    d_u = matmul("mm_d_u", d_proj16, wt_main, "nn", F32, add=d_u)
    dx, dg1 = ew_bwd("rms1_b", f_rms, [row(x2), par(g1)], [(row(d_u),)], [row(dh)], lambda g, e: [g[0] + e[0], g[1]],
                     [((rows, dm), "row", dm, F32, None), ((1, dm), "par", dm, F32, "all")], rows)
    d_conv_w = jnp.concatenate([d_conv["q"], d_conv["k"], d_conv["v"]], axis=1)
    return dict(loss_acc=loss_acc, dx=dx, g1=dg1, g2=dg2, gdn_ng=d_gdn_ng, qn=d_qn_g, kn=d_kn_g, p1=d_p1, p2=d_p2,
                conv=d_conv_w, w_main=dw_main, w_small=dw_small, p_a=dp_a, p_b=dp_b, w_o=dw_o, w_u=dw_u, w_d=dw_d)


_W = NH * LANES
_A0, _A1 = 4 * _W, 4 * _W + 2 * NH
_B0, _B1 = _A1 + 3 * _W, _A1 + 3 * _W + NH
N_IN = _B1 + 2 * _W


def _split_w_in(full_t):
    main = jnp.concatenate([full_t[:_A0], full_t[_A1:_B0], full_t[_B1:]], axis=0)
    small = jnp.concatenate([full_t[_A0:_A1], full_t[_B0:_B1], jnp.zeros((LANES - 3 * NH, full_t.shape[1]), full_t.dtype)], axis=0)
    return main, small


def _join_w_in(main, small):
    return jnp.concatenate([main[:_A0], small[:2 * NH], main[_A0:_A0 + 3 * _W], small[2 * NH:3 * NH], main[_A0 + 3 * _W:]], axis=0)


def _lanes(v, at=0):
    return jnp.pad(v.reshape(1, -1), ((0, 0), (at, LANES - at - v.size)))


def kernel(x, norm_mix_g, w_in, gdn_conv_w, gdn_a_log, gdn_dt_bias, gdn_norm_g, fox_q_norm_g, fox_k_norm_g, fox_f_bias, w_proj_gdn, w_proj_fox, w_out, norm_mlp_g, w_up, w_down, loss_target, m_norm_mix_g, m_w_in, m_gdn_conv_w, m_gdn_a_log, m_gdn_dt_bias, m_gdn_norm_g, m_fox_q_norm_g, m_fox_k_norm_g, m_fox_f_bias, m_w_proj_gdn, m_w_proj_fox, m_w_out, m_norm_mlp_g, m_w_up, m_w_down, v_norm_mix_g, v_w_in, v_gdn_conv_w, v_gdn_a_log, v_gdn_dt_bias, v_gdn_norm_g, v_fox_q_norm_g, v_fox_k_norm_g, v_fox_f_bias, v_w_proj_gdn, v_w_proj_fox, v_w_out, v_norm_mlp_g, v_w_up, v_w_down):
    nseq, seq, dm = x.shape
    rows = nseq * seq
    xi, yi, ci = lax.axis_index("x"), lax.axis_index("y"), lax.axis_index("c")
    chip = 2 * xi + yi
    conv_cols = gdn_conv_w.shape[2]

    tr = lambda a: jnp.swapaxes(a[0], 0, 1)
    big = [tr(w_in), w_proj_gdn[0], w_proj_fox[0], w_out[0], w_up[0], w_down[0]]
    axes = [1, 0, 0, 0, 0, 0]
    big16 = [w.astype(BF16) for w in big]
    g_in, g_pa, g_pb, g_wo, w_u, g_wd = (lax.dynamic_update_index_in_dim(got, own, chip, 0)
                                         for got, own in zip(gather_weights(big16, axes), big16))
    wt_main, wt_small = _split_w_in(g_in.reshape(-1, dm))
    p_a, p_b, w_o, w_d = (g.reshape(-1, dm) for g in (g_pa, g_pb, g_wo, g_wd))
    conv_slot = jnp.zeros((4, 4, conv_cols), F32).at[:, chip].set(jnp.where(ci == 0, gdn_conv_w[0], 0.0))
    conv_full = all_reduce_small("gather_conv", conv_slot.reshape(-1, LANES)).reshape(4, 4 * conv_cols)
    p1 = _lanes(gdn_dt_bias[0]) + _lanes(fox_f_bias[0], 2 * NH)
    p2 = _lanes(gdn_a_log[0])

    g = local_step(x.reshape(rows, dm), loss_target.reshape(rows, dm), norm_mix_g, norm_mlp_g, gdn_norm_g, fox_q_norm_g,
                   fox_k_norm_g, p1, p2, conv_full, wt_main, wt_small, p_a, p_b, w_o, w_u, w_d, nseq, seq)

    small_parts = [g["loss_acc"], g["g1"].reshape(8, LANES), g["g2"].reshape(8, LANES), g["gdn_ng"], g["qn"], g["kn"], g["p1"], g["p2"],
                   g["conv"].reshape(-1, LANES)]
    tiled = [jnp.pad(p, ((0, -p.shape[0] % 8), (0, 0))) for p in small_parts]
    red = all_reduce_small("reduce_small", jnp.concatenate(tiled, axis=0))
    pos, red_parts = 0, []
    for p, t in zip(small_parts, tiled):
        red_parts.append(red[pos:pos + p.shape[0]])
        pos += t.shape[0]
    r_loss, r_g1, r_g2, r_gdn_ng, r_qn, r_kn, r_p1, r_p2, r_conv = red_parts
    loss = jnp.sum(r_loss)
    g_conv = lax.dynamic_slice_in_dim(r_conv.reshape(4, 4, conv_cols), chip, 1, axis=1).reshape(4, conv_cols)
    small_grads = [r_g1.reshape(1, dm), r_p2[:, :NH], r_p1[:, :NH], r_gdn_ng, r_qn, r_kn, r_p1[:, 2 * NH:3 * NH], r_g2.reshape(1, dm)]
    small_w = [norm_mix_g, gdn_a_log, gdn_dt_bias, gdn_norm_g, fox_q_norm_g, fox_k_norm_g, fox_f_bias, norm_mlp_g]
    small_m = [m_norm_mix_g, m_gdn_a_log, m_gdn_dt_bias, m_gdn_norm_g, m_fox_q_norm_g, m_fox_k_norm_g, m_fox_f_bias, m_norm_mlp_g]
    small_v = [v_norm_mix_g, v_gdn_a_log, v_gdn_dt_bias, v_gdn_norm_g, v_fox_q_norm_g, v_fox_k_norm_g, v_fox_f_bias, v_norm_mlp_g]

    def pack(parts):
        flat = jnp.concatenate([jnp.pad(p.reshape(-1), (0, -p.size % LANES)) for p in parts])
        return jnp.pad(flat, (0, -flat.size % (8 * LANES))).reshape(-1, LANES)

    packed = adamw("adamw_small", pack(small_w + [gdn_conv_w[0]]), pack(small_grads + [g_conv]),
                   pack(small_m + [m_gdn_conv_w[0]]), pack(small_v + [v_gdn_conv_w[0]]))

    def unpack(flat2d):
        flat, pos, res = flat2d.reshape(-1), 0, []
        for p in small_w + [gdn_conv_w[0]]:
            res.append(flat[pos:pos + p.size].reshape(p.shape))
            pos += p.size + (-p.size % LANES)
        return res

    s_delta, s_m, s_v = (unpack(a) for a in packed)

    blocks = [_join_w_in(g["w_main"], g["w_small"]).reshape(4, -1, dm), g["p_a"].reshape(4, -1, dm), g["p_b"].reshape(4, -1, dm),
              g["w_o"].reshape(4, -1, dm), g["w_u"], g["w_d"].reshape(4, -1, dm)]
    core = ci.reshape(1).astype(jnp.int32)
    swapped = pair_swap(blocks, axes)
    chip_part = add_pair("add_pair", blocks, swapped, core, axes)
    slots = chip_exchange(chip_part)
    halves = add_chips("add_chips", slots, chip_part, chip.reshape(1).astype(jnp.int32), axes)
    others = pair_send(halves)
    big_m = [tr(m_w_in), m_w_proj_gdn[0], m_w_proj_fox[0], m_w_out[0], m_w_up[0], m_w_down[0]]
    big_v = [tr(v_w_in), v_w_proj_gdn[0], v_w_proj_fox[0], v_w_out[0], v_w_up[0], v_w_down[0]]
    names = ["w_in", "w_proj_gdn", "w_proj_fox", "w_out", "w_up", "w_down"]
    big_res, big_grad = {}, {}
    for nm, w, mine, other, m, v, ax in zip(names, big, halves, others, big_m, big_v, axes):
        res = adamw_halves(f"adamw_{nm}", w, mine, other, m, v, core, ax)
        if nm == "w_in":
            res = [jnp.swapaxes(r, 0, 1) for r in res]
        big_grad[nm], *big_res[nm] = res

    order = ["norm_mix_g", "w_in", "gdn_conv_w", "gdn_a_log", "gdn_dt_bias", "gdn_norm_g", "fox_q_norm_g", "fox_k_norm_g",
             "fox_f_bias", "w_proj_gdn", "w_proj_fox", "w_out", "norm_mlp_g", "w_up", "w_down"]
    small_names = ["norm_mix_g", "gdn_a_log", "gdn_dt_bias", "gdn_norm_g", "fox_q_norm_g", "fox_k_norm_g", "fox_f_bias", "norm_mlp_g",
                   "gdn_conv_w"]
    small_idx = {nm: i for i, nm in enumerate(small_names)}
    shapes = dict(zip(order, (a.shape for a in (norm_mix_g, w_in, gdn_conv_w, gdn_a_log, gdn_dt_bias, gdn_norm_g, fox_q_norm_g,
                                                 fox_k_norm_g, fox_f_bias, w_proj_gdn, w_proj_fox, w_out, norm_mlp_g, w_up, w_down))))
    grads_out, delta_out, m_out, v_out = [], [], [], []
    for nm in order:
        if nm in big_res:
            d, mm, vv = big_res[nm]
            gr = big_grad[nm]
        else:
            i = small_idx[nm]
            gr = (small_grads + [g_conv])[i]
            d, mm, vv = s_delta[i], s_m[i], s_v[i]
        for lst, val in ((grads_out, gr), (delta_out, d), (m_out, mm), (v_out, vv)):
            lst.append(val.reshape(shapes[nm]))
    return (loss, g["dx"].reshape(x.shape), *grads_out, *delta_out, *m_out, *v_out)
```

```python
import functools

import jax
import jax.numpy as jnp
from jax import lax
from jax.experimental import pallas as pl
from jax.experimental.pallas import tpu as pltpu

F32 = jnp.float32
BF16 = jnp.bfloat16
LANES = 128
NH = 8
EPS = 1e-6
GDN_CHUNK = 64
GDN_ROWS = 256
GDN_BASE = 16
ROW_TILE = 512
CONV_HEADS = 2
ATT_TILE = 512
NEG = -1e30
VMEM_LIMIT_BYTES = 48 * 1024 * 1024
HI = lax.Precision.HIGHEST
LO = lax.Precision.DEFAULT
MESH = pl.DeviceIdType.MESH
ANY = pl.BlockSpec(memory_space=pl.ANY)

ADAM_LR, ADAM_B1, ADAM_B2, ADAM_EPS, ADAM_WD, ADAM_STEP = 0.001, 0.9, 0.999, 1e-08, 0.01, 10


def _params(n_grid):
    return pltpu.CompilerParams(dimension_semantics=("arbitrary",) * n_grid,
                                vmem_limit_bytes=VMEM_LIMIT_BYTES)


def _dot(a, b, dims, precision=None):
    dn = {"nn": (((1,), (0,)), ((), ())), "nt": (((1,), (1,)), ((), ())), "tn": (((0,), (0,)), ((), ()))}[dims]
    return lax.dot_general(a, b, dn, precision=precision, preferred_element_type=F32)


def _iota(shape, dim):
    return lax.broadcasted_iota(jnp.int32, shape, dim)


def _split(x, parts):
    out = []
    for _ in range(parts - 1):
        hi = x.astype(BF16)
        out.append(hi)
        x = x - hi.astype(F32)
    return out + [x.astype(BF16)]


def _dot_mask(mask, b, dims):
    m16 = mask.astype(BF16)
    b1, b2, b3 = _split(b, 3)
    return _dot(m16, b1, dims) + (_dot(m16, b2, dims) + _dot(m16, b3, dims))


@jax.custom_vjp
def mm_mask(mask, b):
    return _dot_mask(mask, b, "nn")


mm_mask.defvjp(lambda mask, b: (_dot_mask(mask, b, "nn"), mask),
               lambda mask, g: (jnp.zeros_like(mask), _dot_mask(mask, g, "tn")))


def matmul(name, a, b, dims, out_dtype, add=None, tm=1024, tn=1024, tk=512, col_blocks=None,
           extras=(), epilogue=None, out_dtypes=None):
    if col_blocks and dims != "tn":
        nb, b_rows, bw = b.shape
        b_shape = (b_rows, nb * bw)
    else:
        b_shape = b.shape
    if dims == "nn":
        (m, k), (_, n) = a.shape, b_shape
    elif dims == "nt":
        (m, k), (n, _) = a.shape, b_shape
    else:
        (k, m), (_, n) = a.shape, b_shape
    if k <= 1024:
        tk = k
    tm, tn, tk = min(tm, m), min(tn, n), min(tk, k)
    assert m % tm == 0 and n % tn == 0 and k % tk == 0, (name, m, n, k)
    nk = k // tk
    a_spec = pl.BlockSpec((tk, tm), lambda i, j, kk: (kk, i)) if dims == "tn" else pl.BlockSpec((tm, tk), lambda i, j, kk: (i, kk))
    b_spec = pl.BlockSpec((tn, tk), lambda i, j, kk: (j, kk)) if dims == "nt" else pl.BlockSpec((tk, tn), lambda i, j, kk: (kk, j))
    o_spec = pl.BlockSpec((tm, tn), lambda i, j, kk: (i, j))
    out_shape = (m, n)
    if col_blocks and dims == "nn":
        per = bw // tn
        assert bw % tn == 0
        b_spec = pl.BlockSpec((None, tk, tn), lambda i, j, kk: (j // per, kk, j % per))
    elif col_blocks and dims == "nt":
        per = bw // tk
        assert bw % tk == 0
        b_spec = pl.BlockSpec((None, tn, tk), lambda i, j, kk: (kk // per, j, kk % per))
    elif col_blocks:
        bw = n // col_blocks
        per = bw // tn
        assert bw % tn == 0 and add is None
        o_spec = pl.BlockSpec((None, tm, tn), lambda i, j, kk: (j // per, i, j % per))
        out_shape = (col_blocks, m, bw)
    extras = list(extras) + ([add] if add is not None else [])
    if add is not None:
        assert epilogue is None
        epilogue = lambda r, *e: [r + e[-1]]
    out_dtypes = [out_dtype] if epilogue is None or out_dtypes is None else list(out_dtypes)
    n_ex, n_out = len(extras), len(out_dtypes)

    def body(*refs):
        a_ref, b_ref = refs[0], refs[1]
        ex_refs, o_refs = refs[2:2 + n_ex], refs[2 + n_ex:2 + n_ex + n_out]

        def finish(r):
            res = [r] if epilogue is None else epilogue(r, *[e[...] for e in ex_refs])
            for o_ref, v in zip(o_refs, res):
                o_ref[...] = v.astype(o_ref.dtype)

        if nk == 1:
            finish(_dot(a_ref[...], b_ref[...], dims))
            return
        acc_ref = refs[-1]
        kk = pl.program_id(2)

        @pl.when(kk == 0)
        def _():
            acc_ref[...] = jnp.zeros_like(acc_ref)

        acc_ref[...] += _dot(a_ref[...], b_ref[...], dims)

        @pl.when(kk == nk - 1)
        def _():
            finish(acc_ref[...])

    res = pl.pallas_call(
        body, name=name, grid=(m // tm, n // tn, nk), in_specs=[a_spec, b_spec] + [o_spec] * n_ex, out_specs=[o_spec] * n_out,
        out_shape=[jax.ShapeDtypeStruct(out_shape, dt) for dt in out_dtypes],
        scratch_shapes=[pltpu.VMEM((tm, tn), F32)] if nk > 1 else [], compiler_params=_params(3),
    )(a, b, *extras)
    return res[0] if n_out == 1 else res


def _ew_spec(kind, off, width, tb, hp, order, shape=None):
    def ih(g0, g1):
        return (g0, g1) if order == "ih" else (g1, g0)

    assert off % hp == 0 or kind in ("row", "par")
    if kind == "row":
        return pl.BlockSpec((tb, width), lambda g0, g1: (ih(g0, g1)[0], off))
    if kind == "rowh":
        return pl.BlockSpec((tb, hp * width), lambda g0, g1: (ih(g0, g1)[0], ih(g0, g1)[1] + off // hp))
    if kind == "par":
        return pl.BlockSpec(shape, lambda g0, g1: (0, 0))
    if kind == "parh":
        return pl.BlockSpec((shape[0], hp * width), lambda g0, g1: (0, ih(g0, g1)[1] + off // hp))
    raise ValueError(kind)


def _ew_grid(rows, tb, nh, hp, order):
    assert nh % hp == 0 and rows % tb == 0
    return (rows // tb, nh // hp) if order == "ih" else (nh // hp, rows // tb)


def _ew_load(ref, kind, width, hh):
    if kind in ("row", "par"):
        return ref[...].astype(F32)
    return ref[:, hh * width:(hh + 1) * width].astype(F32)


def ew_fwd(name, f, ins, outs, rows, nh=1, tb=ROW_TILE, order="ih", hp=None):
    hp = nh if hp is None else hp
    n_in = len(ins)

    def body(*refs):
        hb = pl.program_id(1) if order == "ih" else pl.program_id(0)
        for hh in range(hp):
            h = hh if hp == nh else hb * hp + hh
            vals = [_ew_load(r, kd, w, hh) for r, (_, kd, _, w) in zip(refs[:n_in], ins)]
            res = f(h, *vals)
            for r, v, (_, kd, w, _) in zip(refs[n_in:], res, outs):
                if kd == "row":
                    assert hp == 1
                    r[...] = v.astype(r.dtype)
                else:
                    r[:, hh * w:(hh + 1) * w] = v.astype(r.dtype)

    in_specs = [_ew_spec(kd, off, w, tb, hp, order, a.shape) for (a, kd, off, w) in ins]
    out_specs = [_ew_spec(kd, 0, w, tb, hp, order) for (_, kd, w, _) in outs]
    out_shape = [jax.ShapeDtypeStruct((rows, tw), dt) for (tw, _, _, dt) in outs]
    return pl.pallas_call(
        body, name=name, grid=_ew_grid(rows, tb, nh, hp, order), in_specs=in_specs, out_specs=out_specs,
        out_shape=out_shape, compiler_params=_params(2),
    )(*[a for (a, _, _, _) in ins])


def ew_bwd(name, f, ins, cts, extras, emit, outs, rows, nh=1, tb=ROW_TILE, order="ih", hp=None):
    hp = nh if hp is None else hp
    n_in = len(ins)
    flat_cts = [d for group in cts for d in group]
    n_ct, n_ex = len(flat_cts), len(extras)

    def body(*refs):
        g0, g1 = pl.program_id(0), pl.program_id(1)
        hb = g1 if order == "ih" else g0
        out_refs = refs[n_in + n_ct + n_ex:]
        shared = [None] * len(outs)

        def store(r, v, first, sl=None):
            def put(val, add):
                if sl is None:
                    r[...] = (r[...] + val if add else val).astype(r.dtype)
                else:
                    r[:, sl] = (r[:, sl] + val if add else val).astype(r.dtype)

            if first is None:
                put(v, False)
            else:
                pl.when(first)(lambda: put(v, False))
                pl.when(jnp.logical_not(first))(lambda: put(v, True))

        for hh in range(hp):
            h = hh if hp == nh else hb * hp + hh
            vals = [_ew_load(r, kd, w, hh) for r, (_, kd, _, w) in zip(refs[:n_in], ins)]
            ct_refs = list(zip(refs[n_in:n_in + n_ct], flat_cts))
            ct_vals, pos = [], 0
            for group in cts:
                v = None
                for r, (_, kd, _, w) in ct_refs[pos:pos + len(group)]:
                    t = _ew_load(r, kd, w, hh)
                    v = t if v is None else v + t
                pos += len(group)
                ct_vals.append(v)
            ex_vals = [_ew_load(r, kd, w, hh) for r, (_, kd, _, w) in zip(refs[n_in + n_ct:n_in + n_ct + n_ex], extras)]
            _, vjp = jax.vjp(lambda *a: f(h, *a), *vals)
            res = emit(vjp(tuple(ct_vals)), ex_vals)
            for idx, (r, v, (_, kd, w, _, acc)) in enumerate(zip(out_refs, res, outs)):
                if kd in ("row", "par"):
                    shared[idx] = v if shared[idx] is None else shared[idx] + v
                else:
                    store(r, v, (g1 == 0) if acc == "inner" else None, slice(hh * w, (hh + 1) * w))
        for idx, (r, (_, kd, _, _, acc)) in enumerate(zip(out_refs, outs)):
            if kd in ("row", "par"):
                assert acc == "all" or hp == nh
                store(r, shared[idx], jnp.logical_and(g0 == 0, g1 == 0) if acc == "all" else None)

    operands = list(ins) + flat_cts + list(extras)
    in_specs = [_ew_spec(kd, off, w, tb, hp, order, a.shape) for (a, kd, off, w) in operands]
    out_specs = [_ew_spec(kd, 0, w, tb, hp, order, shp) for (shp, kd, w, _, _) in outs]
    out_shape = [jax.ShapeDtypeStruct(shp, dt) for (shp, _, _, dt, _) in outs]
    return pl.pallas_call(
        body, name=name, grid=_ew_grid(rows, tb, nh, hp, order), in_specs=in_specs, out_specs=out_specs,
        out_shape=out_shape, compiler_params=_params(2),
    )(*[a for (a, _, _, _) in operands])


def f_rms(h, x, g):
    r = lax.rsqrt(jnp.mean(x * x, axis=-1, keepdims=True) + EPS)
    return (x * r * g,)


def _softplus(z):
    return jnp.maximum(z, 0.0) + jnp.log1p(jnp.exp(-jnp.abs(z)))


def f_small(h, sp, p1, p2):
    lane = _iota(sp.shape, 1)
    z = sp + p1
    g = -jnp.exp(p2) * _softplus(z)
    beta = jax.nn.sigmoid(z)
    logf = -_softplus(-z)
    return (jnp.where(lane < NH, g, jnp.where(lane < 2 * NH, beta, jnp.where(lane < 3 * NH, logf, 0.0))),)


def _pick(x, lane_id):
    lane = _iota(x.shape, 1)
    col = jnp.sum(jnp.where(lane == lane_id, x, 0.0), axis=1, keepdims=True)
    return jnp.broadcast_to(col, x.shape)


def f_bcast(h, so, cs):
    return _pick(so, h), _pick(so, h + NH), _pick(cs, h + 2 * NH)


def _shift_down(s):
    def down(x):
        return jnp.where(_iota(x.shape, 0) >= s, pltpu.roll(x, s, 0), 0.0)

    def up(g):
        n = g.shape[0]
        return jnp.where(_iota(g.shape, 0) < n - s, pltpu.roll(g, n - s, 0), 0.0)

    @jax.custom_vjp
    def shift(x):
        return down(x)

    shift.defvjp(lambda x: (down(x), None), lambda _, g: (up(g),))
    return shift


def _silu(x):
    return x * jax.nn.sigmoid(x)


def make_f_conv(mode):
    sh1, sh2, sh3 = _shift_down(1), _shift_down(2), _shift_down(3)

    def f(h, x, w):
        sub = _iota(w.shape, 0)

        def tap(i):
            return jnp.sum(jnp.where(sub == i, w, 0.0), axis=0, keepdims=True)

        y = sh3(x) * tap(0)
        y = y + sh2(x) * tap(1)
        y = y + sh1(x) * tap(2)
        y = y + x * tap(3)
        s = _silu(y)
        if mode == "v":
            return (s,)
        n = s * lax.rsqrt(jnp.sum(s * s, axis=-1, keepdims=True) + EPS)
        if mode == "q":
            n = n * (LANES ** -0.5)
        return (n,)

    return f


def f_post(h, o, z, g):
    r = lax.rsqrt(jnp.mean(o * o, axis=-1, keepdims=True) + EPS)
    return (o * r * g * _silu(z),)


def f_merge(h, ga, gb, ya, yb):
    return (jax.nn.sigmoid(ga) * ya + jax.nn.sigmoid(gb) * yb,)


def f_delta(h, do, o):
    return (jnp.broadcast_to(jnp.sum(do * o, axis=1, keepdims=True), o.shape),)


def cumsum_time(name, x, nseq, seq, reverse):
    nb = seq // LANES

    def body(x_ref, o_ref):
        r, c = _iota((LANES, LANES), 0), _iota((LANES, LANES), 1)
        tri = jnp.where((r <= c) if reverse else (r >= c), 1.0, 0.0).astype(F32)
        carry = jnp.zeros((1, LANES), F32)
        for b in (range(nb - 1, -1, -1) if reverse else range(nb)):
            blk = x_ref[b * LANES:(b + 1) * LANES, :]
            o_ref[b * LANES:(b + 1) * LANES, :] = _dot_mask(tri, blk, "nn") + carry
            carry = carry + jnp.sum(blk, axis=0, keepdims=True)

    spec = pl.BlockSpec((seq, LANES), lambda s: (s, 0))
    return pl.pallas_call(body, name=name, grid=(nseq,), in_specs=[spec], out_specs=spec,
                          out_shape=jax.ShapeDtypeStruct(x.shape, F32), compiler_params=_params(1))(x)


def transpose_time(name, x, nseq, seq):
    def body(x_ref, o_ref):
        o_ref[...] = x_ref[...].T

    return pl.pallas_call(
        body, name=name, grid=(nseq,), in_specs=[pl.BlockSpec((seq, LANES), lambda s: (s, 0))],
        out_specs=pl.BlockSpec((LANES, seq), lambda s: (s, 0)),
        out_shape=jax.ShapeDtypeStruct((nseq * LANES, seq), F32), compiler_params=_params(1))(x)


def _gdn_masks():
    n = GDN_ROWS
    r, c = _iota((n, n), 0), _iota((n, n), 1)
    shift = GDN_CHUNK.bit_length() - 1
    same = lax.shift_right_logical(r, shift) == lax.shift_right_logical(c, shift)
    return r, c, same


def _gdn_decay(gb):
    r, c, same = _gdn_masks()
    seg_tril = jnp.where(jnp.logical_and(same, r >= c), 1.0, 0.0).astype(F32)
    g_cum = mm_mask(seg_tril, gb)
    lane0 = _iota(g_cum.shape, 1) == 0
    g_col = jnp.sum(jnp.where(lane0, g_cum, 0.0), axis=1, keepdims=True)
    g_row = jnp.sum(jnp.where(r == c, jnp.broadcast_to(g_col, (GDN_ROWS, GDN_ROWS)), 0.0), axis=0, keepdims=True)
    return g_cum, g_col - g_row


def gdn_f1(q, k, gb, bb):
    r, c, same = _gdn_masks()
    strict = jnp.logical_and(same, r > c)
    _, diff = _gdn_decay(gb)
    lane0 = _iota(bb.shape, 1) == 0
    beta_col = jnp.sum(jnp.where(lane0, bb, 0.0), axis=1, keepdims=True)
    kk = _dot(k, k, "nt", LO)
    return jnp.where(strict, beta_col * kk * jnp.exp(jnp.where(strict, diff, 0.0)), 0.0)


def gdn_f2(t_corr, q, k, v, gb, bb):
    r, c, same = _gdn_masks()
    incl = jnp.logical_and(same, r >= c)
    g_cum, diff = _gdn_decay(gb)
    decay = jnp.where(incl, jnp.exp(jnp.where(incl, diff, 0.0)), 0.0)
    e_g = jnp.exp(g_cum)
    v_beta, k_beta = v * bb, k * bb * e_g
    value = v_beta + _dot(t_corr, v_beta, "nn", LO)
    k_cum = k_beta + _dot(t_corr, k_beta, "nn", LO)
    attn = _dot(q, k, "nt", LO) * decay
    g_last = mm_mask(jnp.where(same, 1.0, 0.0).astype(F32), gb)
    return value, k_cum, attn, q * e_g, k * jnp.exp(g_last - g_cum)


def tri_inverse(a):
    n = GDN_ROWS
    r, c = _iota((n, n), 0), _iota((n, n), 1)
    shift = GDN_BASE.bit_length() - 1
    blk = lax.shift_right_logical(r, shift) == lax.shift_right_logical(c, shift)
    d = jnp.where(blk, a, 0.0)
    lo = a - d
    p = -d
    c_d = p
    for _ in range(shift - 1):
        p = _dot(p, p, "nn", LO)
        c_d = c_d + p + _dot(c_d, p, "nn", LO)
    assert GDN_CHUNK // GDN_BASE == 4
    nmat = lo + _dot(c_d, lo, "nn", LO)
    n2 = _dot(nmat, nmat, "nn", LO)
    c_n = (n2 - nmat) - _dot(nmat, n2, "nn", LO)
    return c_n + c_d + _dot(c_n, c_d, "nn", LO)


def gdn_a_fwd(q, k, v, gb, bb, rows):
    blk = pl.BlockSpec((GDN_ROWS, LANES), lambda i, h: (i, h))
    sq = pl.BlockSpec((GDN_ROWS, GDN_ROWS), lambda i, h: (i, h))

    def body(q_ref, k_ref, v_ref, gb_ref, bb_ref, val_ref, kc_ref, at_ref, qd_ref, kd_ref, t_ref):
        qv, kv, vv, gv, bv = q_ref[...], k_ref[...], v_ref[...], gb_ref[...], bb_ref[...]
        t_inv = tri_inverse(gdn_f1(qv, kv, gv, bv))
        value, k_cum, attn, q_dec, k_dec = gdn_f2(t_inv, qv, kv, vv, gv, bv)
        val_ref[...], kc_ref[...], at_ref[...], qd_ref[...], kd_ref[...], t_ref[...] = value, k_cum, attn, q_dec, k_dec, t_inv

    wide = jax.ShapeDtypeStruct((rows, NH * LANES), F32)
    square = jax.ShapeDtypeStruct((rows, NH * GDN_ROWS), F32)
    return pl.pallas_call(
        body, name="gdn_a_fwd", grid=(rows // GDN_ROWS, NH), in_specs=[blk] * 5,
        out_specs=[blk, blk, sq, blk, blk, sq], out_shape=[wide, wide, square, wide, wide, square],
        compiler_params=_params(2))(q, k, v, gb, bb)


def gdn_a_bwd(q, k, v, gb, bb, t_inv, dval, dkc, dat, dqd, dkd, dgb_b, rows):
    blk = pl.BlockSpec((GDN_ROWS, LANES), lambda i, h: (i, h))
    sq = pl.BlockSpec((GDN_ROWS, GDN_ROWS), lambda i, h: (i, h))

    def body(q_ref, k_ref, v_ref, gb_ref, bb_ref, t_ref, dval_ref, dkc_ref, dat_ref, dqd_ref, dkd_ref, dgbb_ref,
             dq_ref, dk_ref, dv_ref, dgb_ref, dbb_ref):
        qv, kv, vv, gv, bv, tv = q_ref[...], k_ref[...], v_ref[...], gb_ref[...], bb_ref[...], t_ref[...]
        _, vjp1 = jax.vjp(gdn_f1, qv, kv, gv, bv)
        _, vjp2 = jax.vjp(gdn_f2, tv, qv, kv, vv, gv, bv)
        dt, dq2, dk2, dv2, dgb2, dbb2 = vjp2((dval_ref[...], dkc_ref[...], dat_ref[...], dqd_ref[...], dkd_ref[...]))
        left = dt + _dot(tv, dt, "tn", LO)
        da = -(left + _dot(left, tv, "nt", LO))
        dq1, dk1, dgb1, dbb1 = vjp1(da)
        dq_ref[...] = dq1 + dq2
        dk_ref[...] = dk1 + dk2
        dv_ref[...] = dv2
        dgb_ref[...] = dgb1 + dgb2 + dgbb_ref[...]
        dbb_ref[...] = dbb1 + dbb2

    wide = jax.ShapeDtypeStruct((rows, NH * LANES), F32)
    return pl.pallas_call(
        body, name="gdn_a_bwd", grid=(rows // GDN_ROWS, NH),
        in_specs=[blk] * 5 + [sq, blk, blk, sq, blk, blk, blk], out_specs=[blk] * 5, out_shape=[wide] * 5,
        compiler_params=_params(2))(q, k, v, gb, bb, t_inv, dval, dkc, dat, dqd, dkd, dgb_b)


N_CH = GDN_ROWS // GDN_CHUNK


GDN_HP = 8


def gdn_fb(*args):
    per_head = 6 * N_CH
    states = list(args[GDN_HP * per_head:])
    outs = [[None] * N_CH for _ in range(GDN_HP)]
    zero = jnp.zeros((GDN_CHUNK, LANES), F32)
    for c in range(N_CH):
        for hh in range(GDN_HP):
            val, kc, at, qd, kd, gb = (args[hh * per_head + i * N_CH + c] for i in range(6))
            s = states[hh]
            v_new = val - _dot(kc, s, "nn", LO)
            v_pad = jnp.concatenate([zero] * c + [v_new] + [zero] * (N_CH - 1 - c), axis=0)
            outs[hh][c] = _dot(qd, s, "nn", LO) + _dot(at, v_pad, "nn", LO)
            dec = jnp.exp(jnp.sum(gb, axis=0, keepdims=True))
            states[hh] = s * dec + _dot(kd, v_new, "tn", LO)
    return (*[o for head in outs for o in head], *states)


def _gdn_piece(ref, hh, c):
    width = ref.shape[1] // GDN_HP
    return ref.at[c * GDN_CHUNK:(c + 1) * GDN_CHUNK, hh * width:(hh + 1) * width]


def _gdn_pieces(refs, hh):
    return [_gdn_piece(r, hh, c)[...] for r in refs for c in range(N_CH)]


def _gdn_b_specs(nb, rev):
    def blk_row(s, j):
        return s * nb + (nb - 1 - j if rev else j)

    blk = pl.BlockSpec((GDN_ROWS, GDN_HP * LANES), lambda s, hb, j: (blk_row(s, j), hb))
    sq = pl.BlockSpec((GDN_ROWS, GDN_HP * GDN_ROWS), lambda s, hb, j: (blk_row(s, j), hb))
    snap = pl.BlockSpec((GDN_HP * LANES, LANES), lambda s, hb, j: (blk_row(s, j) * (NH // GDN_HP) + hb, 0))
    return blk, sq, snap


def gdn_b_fwd(val, kc, at, qd, kd, gb, nseq, seq):
    nb = seq // GDN_ROWS
    rows = nseq * seq
    blk, sq, snap = _gdn_b_specs(nb, False)

    def body(val_ref, kc_ref, at_ref, qd_ref, kd_ref, gb_ref, o_ref, snap_ref, s_ref):
        @pl.when(pl.program_id(2) == 0)
        def _():
            s_ref[...] = jnp.zeros_like(s_ref)

        states = [s_ref[hh] for hh in range(GDN_HP)]
        for hh in range(GDN_HP):
            snap_ref[hh * LANES:(hh + 1) * LANES, :] = states[hh]
        pieces = [p for hh in range(GDN_HP) for p in _gdn_pieces([val_ref, kc_ref, at_ref, qd_ref, kd_ref, gb_ref], hh)]
        res = gdn_fb(*pieces, *states)
        for hh in range(GDN_HP):
            for c in range(N_CH):
                _gdn_piece(o_ref, hh, c)[...] = res[hh * N_CH + c]
            s_ref[hh] = res[GDN_HP * N_CH + hh]

    return pl.pallas_call(
        body, name="gdn_b_fwd", grid=(nseq, NH // GDN_HP, nb), in_specs=[blk, blk, sq, blk, blk, blk], out_specs=[blk, snap],
        out_shape=[jax.ShapeDtypeStruct((rows, NH * LANES), F32), jax.ShapeDtypeStruct((nseq * nb * NH * LANES, LANES), F32)],
        scratch_shapes=[pltpu.VMEM((GDN_HP, LANES, LANES), F32)], compiler_params=_params(3))(val, kc, at, qd, kd, gb)


def gdn_b_bwd(val, kc, at, qd, kd, gb, snaps, do, nseq, seq):
    nb = seq // GDN_ROWS
    rows = nseq * seq
    blk, sq, snap = _gdn_b_specs(nb, True)

    def body(val_ref, kc_ref, at_ref, qd_ref, kd_ref, gb_ref, snap_ref, do_ref,
             dval_ref, dkc_ref, dat_ref, dqd_ref, dkd_ref, dgb_ref, ds_ref):
        @pl.when(pl.program_id(2) == 0)
        def _():
            ds_ref[...] = jnp.zeros_like(ds_ref)

        pieces = [p for hh in range(GDN_HP) for p in _gdn_pieces([val_ref, kc_ref, at_ref, qd_ref, kd_ref, gb_ref], hh)]
        states = [snap_ref[hh * LANES:(hh + 1) * LANES, :] for hh in range(GDN_HP)]
        _, vjp = jax.vjp(gdn_fb, *pieces, *states)
        cts = [p for hh in range(GDN_HP) for p in _gdn_pieces([do_ref], hh)] + [ds_ref[hh] for hh in range(GDN_HP)]
        grads = vjp(tuple(cts))
        for hh in range(GDN_HP):
            for i, r in enumerate([dval_ref, dkc_ref, dat_ref, dqd_ref, dkd_ref, dgb_ref]):
                for c in range(N_CH):
                    _gdn_piece(r, hh, c)[...] = grads[hh * 6 * N_CH + i * N_CH + c]
            ds_ref[hh] = grads[GDN_HP * 6 * N_CH + hh]

    wide = jax.ShapeDtypeStruct((rows, NH * LANES), F32)
    square = jax.ShapeDtypeStruct((rows, NH * GDN_ROWS), F32)
    return pl.pallas_call(
        body, name="gdn_b_bwd", grid=(nseq, NH // GDN_HP, nb), in_specs=[blk, blk, sq, blk, blk, blk, snap, blk],
        out_specs=[blk, blk, sq, blk, blk, blk], out_shape=[wide, wide, square, wide, wide, wide],
        scratch_shapes=[pltpu.VMEM((GDN_HP, LANES, LANES), F32)], compiler_params=_params(3))(val, kc, at, qd, kd, gb, snaps, do)


FOX_Q, FOX_K, FOX_V = 4 * NH, 5 * NH, 6 * NH
FOX_SCALE = LANES ** -0.5


def _head_row(ct_ref, h, off, width):
    blk = ct_ref[:, pl.ds(off, width)]
    return jnp.sum(jnp.where(_iota(blk.shape, 0) == h, blk, 0.0), axis=0, keepdims=True)


def _col(x):
    return jnp.max(x, axis=1, keepdims=True)


def _row(x):
    return jnp.max(x.T, axis=0, keepdims=True)


def _causal(shape, q_dim):
    return _iota(shape, q_dim) >= _iota(shape, 1 - q_dim)


def fox_fwd(qn, kn, proj, ct, nseq, seq):
    tq = tk = min(ATT_TILE, seq)
    nq = seq // tq
    rows = nseq * seq
    qblk = pl.BlockSpec((tq, LANES), lambda s, h, i: (s * nq + i, h))
    full = pl.BlockSpec((seq, LANES), lambda s, h, i: (s, h))
    vfull = pl.BlockSpec((seq, LANES), lambda s, h, i: (s, h + FOX_V))
    ctb = pl.BlockSpec((NH, seq), lambda s, h, i: (s * (LANES // NH) + 2, 0))

    def body(q_ref, k_ref, v_ref, ct_ref, o_ref, o16_ref, lse_ref):
        h, i = pl.program_id(1), pl.program_id(2)
        q = q_ref[...]

        def step(j, carry, diag):
            m, l, acc = carry
            off = pl.multiple_of(j * tk, tk)
            s = _dot(q, k_ref[pl.ds(off, tk), :], "nt") * FOX_SCALE - _head_row(ct_ref, h, off, tk)
            if diag:
                s = jnp.where(_causal(s.shape, 0), s, NEG)
            m_new = jnp.maximum(m, jnp.max(s, axis=1, keepdims=True))
            p = jnp.exp(s - m_new)
            alpha = jnp.exp(m - m_new)
            l = alpha * l + jnp.sum(p, axis=1, keepdims=True)
            acc = alpha * acc + _dot(p.astype(BF16), v_ref[pl.ds(off, tk), :].astype(BF16), "nn")
            return m_new, l, acc

        init = (jnp.full((tq, 1), NEG, F32), jnp.zeros((tq, 1), F32), jnp.zeros((tq, LANES), F32))
        carry = lax.fori_loop(0, i, lambda j, c: step(j, c, False), init)
        m, l, acc = step(i, carry, True)
        o = acc / l
        o_ref[...] = o
        o16_ref[...] = o.astype(BF16)
        lse_ref[...] = jnp.broadcast_to(m + jnp.log(l), (tq, LANES))

    wide = (rows, NH * LANES)
    return pl.pallas_call(
        body, name="fox_fwd", grid=(nseq, NH, nq), in_specs=[qblk, full, vfull, ctb], out_specs=[qblk] * 3,
        out_shape=[jax.ShapeDtypeStruct(wide, F32), jax.ShapeDtypeStruct(wide, BF16), jax.ShapeDtypeStruct(wide, F32)],
        compiler_params=_params(3))(qn, kn, proj, ct)


def fox_dq(qn, kn, proj, ct, do, lse, delta, nseq, seq):
    tq = tk = min(ATT_TILE, seq)
    nq = seq // tq
    rows = nseq * seq
    qblk = pl.BlockSpec((tq, LANES), lambda s, h, i: (s * nq + i, h))
    full = pl.BlockSpec((seq, LANES), lambda s, h, i: (s, h))
    vfull = pl.BlockSpec((seq, LANES), lambda s, h, i: (s, h + FOX_V))
    ctb = pl.BlockSpec((NH, seq), lambda s, h, i: (s * (LANES // NH) + 2, 0))

    def body(q_ref, k_ref, v_ref, ct_ref, do_ref, lse_ref, dl_ref, dq_ref, dc_ref):
        h, i = pl.program_id(1), pl.program_id(2)
        q = q_ref[...]
        lse, delta = _col(lse_ref[...]), _col(dl_ref[...])
        do16 = do_ref[...].astype(BF16)

        def step(j, carry, diag):
            dq, dc = carry
            off = pl.multiple_of(j * tk, tk)
            k = k_ref[pl.ds(off, tk), :]
            p = jnp.exp(_dot(q, k, "nt") * FOX_SCALE - _head_row(ct_ref, h, off, tk) - lse)
            if diag:
                p = jnp.where(_causal(p.shape, 0), p, 0.0)
            dp = _dot(do16, v_ref[pl.ds(off, tk), :].astype(BF16), "nt")
            ds = p * (dp - delta)
            return dq + _dot(ds.astype(BF16), k, "nn"), dc + jnp.sum(ds, axis=1, keepdims=True)

        init = (jnp.zeros((tq, LANES), F32), jnp.zeros((tq, 1), F32))
        dq, dc = step(i, lax.fori_loop(0, i, lambda j, c: step(j, c, False), init), True)
        dq_ref[...] = dq * FOX_SCALE
        dc_ref[...] = jnp.where(_iota((tq, LANES), 1) == 0, dc, 0.0)

    wide = jax.ShapeDtypeStruct((rows, NH * LANES), F32)
    return pl.pallas_call(
        body, name="fox_dq", grid=(nseq, NH, nq), in_specs=[qblk, full, vfull, ctb, qblk, qblk, qblk],
        out_specs=[qblk, qblk], out_shape=[wide, wide], compiler_params=_params(3))(qn, kn, proj, ct, do, lse, delta)


def fox_dkv(qn, kn, proj, cb, do, lse, delta, nseq, seq):
    tq = tk = min(ATT_TILE, seq)
    nq = seq // tq
    rows = nseq * seq
    kblk = pl.BlockSpec((tk, LANES), lambda s, h, j: (s * nq + j, h))
    vblk = pl.BlockSpec((tk, LANES), lambda s, h, j: (s * nq + j, h + FOX_V))
    full = pl.BlockSpec((seq, LANES), lambda s, h, j: (s, h))

    def body(q_ref, k_ref, v_ref, cb_ref, do_ref, lse_ref, dl_ref, dk_ref, dv_ref, dc_ref):
        j = pl.program_id(2)
        k = k_ref[...]
        v16 = v_ref[...].astype(BF16)
        ck = _col(cb_ref[...])

        def step(i, carry, diag):
            dk, dv, dc = carry
            off = pl.multiple_of(i * tq, tq)
            q = q_ref[pl.ds(off, tq), :]
            do16 = do_ref[pl.ds(off, tq), :].astype(BF16)
            lse, delta = (_row(r[pl.ds(off, tq), :]) for r in (lse_ref, dl_ref))
            p = jnp.exp(_dot(k, q, "nt") * FOX_SCALE - ck - lse)
            if diag:
                p = jnp.where(_causal(p.shape, 1), p, 0.0)
            dv = dv + _dot(p.astype(BF16), do16, "nn")
            ds = p * (_dot(v16, do16, "nt") - delta)
            return dk + _dot(ds.astype(BF16), q, "nn"), dv, dc + jnp.sum(ds, axis=1, keepdims=True)

        zero = jnp.zeros((tk, LANES), F32)
        carry = step(j, (zero, zero, jnp.zeros((tk, 1), F32)), True)
        dk, dv, dc = lax.fori_loop(j + 1, nq, lambda i, c: step(i, c, False), carry)
        dk_ref[...] = dk * FOX_SCALE
        dv_ref[...] = dv.astype(BF16)
        dc_ref[...] = jnp.where(_iota((tk, LANES), 1) == 0, -dc, 0.0)

    wide = (rows, NH * LANES)
    return pl.pallas_call(
        body, name="fox_dkv", grid=(nseq, NH, nq), in_specs=[full, kblk, vblk, kblk, full, full, full],
        out_specs=[kblk, kblk, kblk],
        out_shape=[jax.ShapeDtypeStruct(wide, F32), jax.ShapeDtypeStruct(wide, BF16), jax.ShapeDtypeStruct(wide, F32)],
        compiler_params=_params(3))(qn, kn, proj, cb, do, lse, delta)


def loss_head(out, tgt, rows, width):
    tb = ROW_TILE
    blk = pl.BlockSpec((tb, width), lambda i: (i, 0))
    accb = pl.BlockSpec((8, LANES), lambda i: (0, 0))

    def body(o_ref, t_ref, d32_ref, d16_ref, acc_ref):
        d = o_ref[...] - t_ref[...]
        row_loss = 0.5 * jnp.mean(d * d, axis=1, keepdims=True)
        g = d * (1.0 / width)
        d32_ref[...] = g
        d16_ref[...] = g.astype(BF16)
        part = jnp.where(_iota((tb, LANES), 1) == 0, row_loss, 0.0).reshape(tb // 8, 8, LANES).sum(axis=0)

        @pl.when(pl.program_id(0) == 0)
        def _():
            acc_ref[...] = part

        @pl.when(pl.program_id(0) != 0)
        def _():
            acc_ref[...] += part

    return pl.pallas_call(
        body, name="loss_head", grid=(rows // tb,), in_specs=[blk, blk], out_specs=[blk, blk, accb],
        out_shape=[jax.ShapeDtypeStruct((rows, width), F32), jax.ShapeDtypeStruct((rows, width), BF16),
                   jax.ShapeDtypeStruct((8, LANES), F32)], compiler_params=_params(1))(out, tgt)


def _adamw_update(w, g, m, v):
    m_new = ADAM_B1 * m + (1.0 - ADAM_B1) * g
    v_new = ADAM_B2 * v + (1.0 - ADAM_B2) * (g * g)
    m_hat = m_new / (1.0 - ADAM_B1 ** ADAM_STEP)
    v_hat = v_new / (1.0 - ADAM_B2 ** ADAM_STEP)
    return -ADAM_LR * (m_hat / (jnp.sqrt(v_hat) + ADAM_EPS) + ADAM_WD * w), m_new, v_new


def adamw(name, w, g, m, v):
    rows, cols = w.shape
    tb = min(rows, 128)
    assert rows % tb == 0
    blk = pl.BlockSpec((tb, cols), lambda i: (i, 0))

    def body(w_ref, g_ref, m_ref, v_ref, d_ref, mo_ref, vo_ref):
        d_ref[...], mo_ref[...], vo_ref[...] = _adamw_update(w_ref[...], g_ref[...], m_ref[...], v_ref[...])

    shp = jax.ShapeDtypeStruct(w.shape, F32)
    return pl.pallas_call(body, name=name, grid=(rows // tb,), in_specs=[blk] * 4, out_specs=[blk] * 3,
                          out_shape=[shp] * 3, compiler_params=_params(1))(w, g, m, v)


SPLIT_TILE = 128


def _tiled(shape2d, ax, n_lead, index):
    blk = (SPLIT_TILE, shape2d[1]) if ax == 0 else (shape2d[0], SPLIT_TILE)

    def index_map(*args):
        *lead, t = index(*args)
        return (*lead, t, 0) if ax == 0 else (*lead, 0, t)

    return pl.BlockSpec((None,) * n_lead + blk, index_map)


def adamw_halves(name, w, mine, other, m, v, c, ax):
    steps = w.shape[ax] // 2 // SPLIT_TILE
    assert w.shape[ax] == 2 * steps * SPLIT_TILE

    def body(c_ref, w_ref, mine_ref, other_ref, m_ref, v_ref, g_ref, d_ref, mo_ref, vo_ref):
        g = jnp.where(pl.program_id(0) // steps == c_ref[0], mine_ref[...], other_ref[...])
        g_ref[...] = g
        d_ref[...], mo_ref[...], vo_ref[...] = _adamw_update(w_ref[...], g, m_ref[...], v_ref[...])

    blk = _tiled(w.shape, ax, 0, lambda i, c_ref: (i,))
    hblk = _tiled(mine.shape, ax, 0, lambda i, c_ref: (i % steps,))
    grid_spec = pltpu.PrefetchScalarGridSpec(num_scalar_prefetch=1, grid=(2 * steps,),
                                             in_specs=[blk, hblk, hblk, blk, blk], out_specs=[blk] * 4)
    shp = jax.ShapeDtypeStruct(w.shape, F32)
    return pl.pallas_call(body, name=name, grid_spec=grid_spec, out_shape=[shp] * 4,
                          compiler_params=_params(1))(c, w, mine, other, m, v)


def add_chips(name, slots, parts, chip, axes):
    outs = []
    for idx, (x, own, ax) in enumerate(zip(slots, parts, axes)):
        n, shape2d = x.shape[0], x.shape[1:]
        steps = shape2d[ax] // SPLIT_TILE
        assert shape2d[ax] == steps * SPLIT_TILE

        def body(me_ref, *refs, n=n):
            o_ref = refs[n + 1]
            acc = None
            for t in range(n):
                term = jnp.where(me_ref[0] == t, refs[n][...], refs[t][...]).astype(F32)
                acc = term if acc is None else acc + term
            o_ref[...] = acc

        def filled(t, n=n):
            return lambda i, me_ref: (jnp.where(me_ref[0] == t, (t + 1) % n, t), i)

        grid_spec = pltpu.PrefetchScalarGridSpec(
            num_scalar_prefetch=1, grid=(steps,),
            in_specs=[_tiled(shape2d, ax, 1, filled(t)) for t in range(n)]
            + [_tiled(shape2d, ax, 1, lambda i, me_ref: (me_ref[0], i))],
            out_specs=_tiled(shape2d, ax, 0, lambda i, me_ref: (i,)))
        outs.append(pl.pallas_call(
            body, name=f"{name}_{idx}", grid_spec=grid_spec, out_shape=jax.ShapeDtypeStruct(shape2d, F32),
            compiler_params=_params(1))(chip, *([x] * n), own))
    return outs


def add_pair(name, gs, rs, c, axes):
    outs = []
    for idx, (g, r, ax) in enumerate(zip(gs, rs, axes)):
        nb = r.shape[0]
        steps = r.shape[1 + ax] // SPLIT_TILE
        assert r.shape[1 + ax] == steps * SPLIT_TILE

        def body(c_ref, g_ref, r_ref, o_ref):
            o_ref[...] = (g_ref[...] + r_ref[...]).astype(BF16)

        grid_spec = pltpu.PrefetchScalarGridSpec(
            num_scalar_prefetch=1, grid=(nb, steps),
            in_specs=[_tiled(g.shape[1:], ax, 1, lambda b, i, c_ref: (b, c_ref[0] * steps + i)),
                      _tiled(r.shape[1:], ax, 1, lambda b, i, c_ref: (b, i))],
            out_specs=_tiled(r.shape[1:], ax, 1, lambda b, i, c_ref: (b, i)))
        outs.append(pl.pallas_call(
            body, name=f"{name}_{idx}", grid_spec=grid_spec, out_shape=jax.ShapeDtypeStruct(r.shape, BF16),
            compiler_params=_params(2))(c, g, r))
    return outs


def _place():
    x, y, c = lax.axis_index("x"), lax.axis_index("y"), lax.axis_index("c")
    return x, y, c, [(1 - x, y), (x, 1 - y), (1 - x, 1 - y)]


def _remote(src, dst, send_sem, recv_sem, dev):
    return pltpu.make_async_remote_copy(src_ref=src, dst_ref=dst, send_sem=send_sem, recv_sem=recv_sem,
                                        device_id=dev, device_id_type=MESH)


def _half(ref, lead, ax, which):
    size = ref.shape[len(lead) + ax] // 2
    part = pl.ds(which * size, size)
    return ref.at[(*lead, part, slice(None)) if ax == 0 else (*lead, slice(None), part)]


def gather_weights(shards, axes):
    n = len(shards)

    def body(*refs):
        ins, outs = refs[:n], refs[n:2 * n]
        ici_s, ici_r, d2d_s, d2d_r = refs[2 * n:]
        x, y, c, chips = _place()
        me = 2 * x + y
        sends, passes = [], []
        for w in range(n):
            for j, (ox, oy) in enumerate(chips):
                cp = _remote(_half(ins[w], (), axes[w], c), _half(outs[w], (me,), axes[w], c),
                             ici_s.at[3 * w + j], ici_r.at[3 * w + j], (ox, oy, c))
                cp.start()
                sends.append(cp)
        for w in range(n):
            for j, (ox, oy) in enumerate(chips):
                landed = _half(outs[w], (2 * ox + oy,), axes[w], c)
                _remote(landed, landed, ici_s.at[3 * w + j], ici_r.at[3 * w + j], (ox, oy, c)).wait_recv()
                cp = _remote(landed, landed, d2d_s.at[3 * w + j], d2d_r.at[3 * w + j], (x, y, 1 - c))
                cp.start()
                passes.append(cp)
        for w in range(n):
            for j, (ox, oy) in enumerate(chips):
                other = _half(outs[w], (2 * ox + oy,), axes[w], 1 - c)
                _remote(other, other, d2d_s.at[3 * w + j], d2d_r.at[3 * w + j], (x, y, 1 - c)).wait_recv()
        for cp in sends + passes:
            cp.wait_send()

    return pl.pallas_call(
        body, name="gather_weights", in_specs=[ANY] * n, out_specs=[ANY] * n,
        out_shape=[jax.ShapeDtypeStruct((4,) + s.shape, s.dtype) for s in shards],
        scratch_shapes=[pltpu.SemaphoreType.DMA((3 * n,))] * 4,
    )(*shards)


HBM = pl.BlockSpec(memory_space=pltpu.HBM)
SEM = pl.BlockSpec(memory_space=pltpu.SEMAPHORE)
DATAFLOW = pltpu.SideEffectType.DATAFLOW_SIDE_EFFECTING


def _hbm(a):
    return pltpu.with_memory_space_constraint(a, pltpu.HBM)


def gather_start(shards, after):
    n = len(shards)

    def body(*refs):
        ins, lands = refs[:n], refs[n:2 * n]
        send, recv, token = refs[2 * n + 1], refs[2 * n + 2], refs[-1]
        x, y, c, chips = _place()
        me = 2 * x + y
        for w in range(n):
            for j, (ox, oy) in enumerate(chips):
                for tc in range(2):
                    _remote(_half(ins[w], (), 0, c), _half(lands[w], (me,), 0, c),
                            send.at[2 * (3 * w + j) + tc], recv.at[2 * (3 * w + j) + c], (ox, oy, tc)).start()
        token[...] = jnp.zeros_like(token)

    zones = [_hbm(lax.empty((4,) + s.shape, s.dtype)) for s in shards]
    res = pl.pallas_call(
        body, name="gather_start", in_specs=[HBM] * (2 * n) + [ANY],
        out_specs=[SEM, SEM] + [HBM] * (2 * n) + [pl.BlockSpec(memory_space=pltpu.VMEM)],
        out_shape=[pltpu.SemaphoreType.DMA((6 * n,)), pltpu.SemaphoreType.DMA((6 * n,))]
        + [pltpu.HBM(s.shape, s.dtype) for s in shards] + [pltpu.HBM(z.shape, z.dtype) for z in zones]
        + [jax.ShapeDtypeStruct((8, LANES), F32)],
        input_output_aliases={i: 2 + i for i in range(2 * n)},
        compiler_params=pltpu.CompilerParams(has_side_effects=DATAFLOW),
    )(*[_hbm(s) for s in shards], *zones, after)
    return res[0], res[1], res[2:2 + n], res[2 + n:2 + 2 * n], res[-1]


def gather_wait(send, recv, shards, zones, after):
    n = len(shards)

    def body(*refs):
        ins, lands = refs[:n], refs[n:2 * n]
        send_sems, recv_sems = refs[2 * n], refs[2 * n + 1]
        x, y, c, chips = _place()
        for w in range(n):
            for j, (ox, oy) in enumerate(chips):
                for k in range(2):
                    mine = _half(ins[w], (), 0, c)
                    _remote(mine, mine, send_sems.at[2 * (3 * w + j) + k], recv_sems.at[2 * (3 * w + j) + k], (ox, oy, k)).wait_send()
                    landed = _half(lands[w], (2 * ox + oy,), 0, k)
                    _remote(landed, landed, send_sems.at[2 * (3 * w + j) + k], recv_sems.at[2 * (3 * w + j) + k], (ox, oy, k)).wait_recv()

    res = pl.pallas_call(
        body, name="gather_wait", in_specs=[HBM] * (2 * n) + [SEM, SEM, ANY], out_specs=[HBM] * (2 * n),
        out_shape=[pltpu.HBM(s.shape, s.dtype) for s in shards] + [pltpu.HBM(z.shape, z.dtype) for z in zones],
        input_output_aliases={i: i for i in range(2 * n)},
        compiler_params=pltpu.CompilerParams(has_side_effects=DATAFLOW),
    )(*shards, *zones, send, recv, after)
    return res[n:]


def pair_swap(grads, axes):
    n = len(grads)

    def body(*refs):
        ins, outs = refs[:n], refs[n:2 * n]
        send, recv = refs[2 * n:]
        x, y, c, _ = _place()
        cps = []
        for w in range(n):
            cp = _remote(_half(ins[w], (slice(None),), axes[w], 1 - c), outs[w], send.at[w], recv.at[w], (x, y, 1 - c))
            cp.start()
            cps.append(cp)
        for cp in cps:
            cp.wait_recv()
        for cp in cps:
            cp.wait_send()

    def halved(g, ax):
        return tuple(d // 2 if i == 1 + ax else d for i, d in enumerate(g.shape))

    return pl.pallas_call(
        body, name="pair_swap", in_specs=[ANY] * n, out_specs=[ANY] * n,
        out_shape=[jax.ShapeDtypeStruct(halved(g, ax), g.dtype) for g, ax in zip(grads, axes)],
        scratch_shapes=[pltpu.SemaphoreType.DMA((n,))] * 2,
    )(*grads)


def chip_exchange(parts):
    n = len(parts)

    def body(*refs):
        ins, outs = refs[:n], refs[n:2 * n]
        send, recv = refs[2 * n:]
        x, y, c, chips = _place()
        me = 2 * x + y
        cps = []
        for w in range(n):
            for j, (ox, oy) in enumerate(chips):
                cp = _remote(ins[w].at[2 * ox + oy], outs[w].at[me], send.at[3 * w + j], recv.at[3 * w + j], (ox, oy, c))
                cp.start()
                cps.append(cp)
        for w in range(n):
            for j, (ox, oy) in enumerate(chips):
                slot = outs[w].at[2 * ox + oy]
                _remote(slot, slot, send.at[3 * w + j], recv.at[3 * w + j], (ox, oy, c)).wait_recv()
        for cp in cps:
            cp.wait_send()

    return pl.pallas_call(
        body, name="chip_exchange", in_specs=[ANY] * n, out_specs=[ANY] * n,
        out_shape=[jax.ShapeDtypeStruct(p.shape, p.dtype) for p in parts],
        scratch_shapes=[pltpu.SemaphoreType.DMA((3 * n,))] * 2,
    )(*parts)


def pair_send(halves):
    n = len(halves)

    def body(*refs):
        ins, outs = refs[:n], refs[n:2 * n]
        send, recv = refs[2 * n:]
        x, y, c, _ = _place()
        cps = [_remote(ins[w], outs[w], send.at[w], recv.at[w], (x, y, 1 - c)) for w in range(n)]
        for cp in cps:
            cp.start()
        for cp in cps:
            cp.wait_recv()
        for cp in cps:
            cp.wait_send()

    return pl.pallas_call(
        body, name="pair_send", in_specs=[ANY] * n, out_specs=[ANY] * n,
        out_shape=[jax.ShapeDtypeStruct(h.shape, h.dtype) for h in halves],
        scratch_shapes=[pltpu.SemaphoreType.DMA((n,))] * 2,
    )(*halves)


def all_reduce_small(name, vec):
    rows = vec.shape[0]

    def body(v_ref, o_ref, buf, send, recv):
        x, y, c, _ = _place()
        me = 4 * x + 2 * y + c
        buf[me] = v_ref[...]
        cps = []
        for k in range(1, 8):
            kx, ky, kc = (k >> 2) & 1, (k >> 1) & 1, k & 1
            peer = (x if kx == 0 else 1 - x, y if ky == 0 else 1 - y, c if kc == 0 else 1 - c)
            cp = _remote(v_ref, buf.at[me], send.at[k - 1], recv.at[k - 1], peer)
            cp.start()
            cps.append(cp)
        for k in range(1, 8):
            kx, ky, kc = (k >> 2) & 1, (k >> 1) & 1, k & 1
            px, py, pc = (x if kx == 0 else 1 - x, y if ky == 0 else 1 - y, c if kc == 0 else 1 - c)
            slot = buf.at[4 * px + 2 * py + pc]
            _remote(slot, slot, send.at[k - 1], recv.at[k - 1], (px, py, pc)).wait_recv()
        for cp in cps:
            cp.wait_send()
        acc = buf[0]
        for d in range(1, 8):
            acc = acc + buf[d]
        o_ref[...] = acc

    vm = pl.BlockSpec(memory_space=pltpu.VMEM)
    return pl.pallas_call(
        body, name=name, in_specs=[vm], out_specs=vm, out_shape=jax.ShapeDtypeStruct(vec.shape, F32),
        scratch_shapes=[pltpu.VMEM((8, rows, LANES), F32), pltpu.SemaphoreType.DMA((7,)), pltpu.SemaphoreType.DMA((7,))],
    )(vec)


def local_step(x2, tgt2, g1, g2, gdn_ng, qn_g, kn_g, p1, p2, conv_w, wt_main, wt_small, late_weights, nseq, seq):
    rows, dm = x2.shape
    wide = NH * LANES
    row = lambda a, off=0, w=None: (a, "row", off, a.shape[1] if w is None else w)
    rowh = lambda a, off=0, w=LANES: (a, "rowh", off, w)
    par = lambda a: (a, "par", 0, a.shape[1])
    parh = lambda a, off=0: (a, "parh", off, LANES)
    o_row = lambda w, dt: (w, "row", w, dt)
    o_rowh = lambda dt, tw=wide, w=LANES: (tw, "rowh", w, dt)

    u, = ew_fwd("rms1", f_rms, [row(x2), par(g1)], [o_row(dm, BF16)], rows)
    proj = matmul("mm_in", u, wt_main, "nt", F32)
    sp = matmul("mm_in_small", u, wt_small, "nt", F32)
    so, = ew_fwd("small", f_small, [row(sp), par(p1), par(p2)], [o_row(LANES, F32)], rows)
    cs = cumsum_time("cumsum", so, nseq, seq, False)
    gb, bb, cb = ew_fwd("bcast", f_bcast, [row(so), row(cs)], [o_rowh(F32)] * 3, rows, NH)
    ct = transpose_time("c_time_major", cs, nseq, seq)
    conv = {}
    for mode, off in (("q", 0), ("k", NH), ("v", 2 * NH)):
        conv[mode], = ew_fwd(f"conv_{mode}", make_f_conv(mode), [rowh(proj, off), parh(conv_w, off)], [o_rowh(F32)],
                             rows, NH, seq, "hi", CONV_HEADS)
    val, kcum, attn, qdec, kdec, t_inv = gdn_a_fwd(conv["q"], conv["k"], conv["v"], gb, bb, rows)
    o_a, snaps = gdn_b_fwd(val, kcum, attn, qdec, kdec, gb, nseq, seq)
    ya_in, = ew_fwd("gdn_post", f_post, [rowh(o_a), rowh(proj, 3 * NH), par(gdn_ng)], [o_rowh(BF16)], rows, NH)
    fqn, = ew_fwd("fox_qn", f_rms, [rowh(proj, FOX_Q), par(qn_g)], [o_rowh(BF16)], rows, NH)
    fkn, = ew_fwd("fox_kn", f_rms, [rowh(proj, FOX_K), par(kn_g)], [o_rowh(BF16)], rows, NH)
    o_b, o_b16, lse = fox_fwd(fqn, fkn, proj, ct, nseq, seq)
    p_a, p_b, w_o, w_u, w_d = late_weights(o_a)
    y_a = matmul("mm_pa", ya_in, p_a, "nn", F32, tn=1024)
    y_b = matmul("mm_pb", o_b16, p_b, "nn", F32, tn=1024)
    gates = [row(proj, 7, dm), row(proj, 8, dm)]
    merged, = ew_fwd("merge", f_merge, gates + [row(y_a), row(y_b)], [o_row(dm, BF16)], rows)
    hres = matmul("mm_out", merged, w_o, "nn", F32, add=x2, tn=1024)
    hn, = ew_fwd("rms2", f_rms, [row(hres), par(g2)], [o_row(dm, BF16)], rows)
    up_blocks = w_u.shape[0]
    act, relu2 = matmul("mm_up", hn, w_u, "nn", F32, col_blocks=up_blocks, out_dtypes=[F32, BF16],
                        epilogue=lambda r: [r, jnp.maximum(r, 0.0) * jnp.maximum(r, 0.0)])
    out = matmul("mm_down", relu2, w_d, "nn", F32, add=hres, tn=1024)
    dout, dout16, loss_acc = loss_head(out, tgt2, rows, dm)

    d_act = matmul("mm_d_act", dout16, w_d, "nt", BF16, extras=[act], epilogue=lambda r, a: [2.0 * jnp.maximum(a, 0.0) * r])
    dw_d = matmul("mm_dw_down", relu2, dout16, "tn", F32, tn=1024)
    dw_u = matmul("mm_dw_up", hn, d_act, "tn", F32, col_blocks=up_blocks)
    d_hn = matmul("mm_d_hn", d_act, w_u, "nt", F32, col_blocks=up_blocks)
    dh, dh16, dg2 = ew_bwd("rms2_b", f_rms, [row(hres), par(g2)], [(row(d_hn),)], [row(dout)],
                           lambda g, e: [g[0] + e[0], g[0] + e[0], g[1]],
                           [((rows, dm), "row", dm, F32, None), ((rows, dm), "row", dm, BF16, None), ((1, dm), "par", dm, F32, "all")], rows)
    d_merged = matmul("mm_d_merged", dh16, w_o, "nt", F32, tn=1024)
    dw_o = matmul("mm_dw_out", merged, dh16, "tn", F32, tn=1024)
    seg16 = ((rows, dm), "row", dm, BF16, None)
    d_ga16, d_gb16, d_ya16, d_yb16 = ew_bwd("merge_b", f_merge, gates + [row(y_a), row(y_b)], [(row(d_merged),)], [],
                                            lambda g, e: list(g), [seg16] * 4, rows)
    dp_a = matmul("mm_dp_a", ya_in, d_ya16, "tn", F32, tn=1024)
    d_ya_in = matmul("mm_d_ya_in", d_ya16, p_a, "nt", F32, tn=1024)
    dp_b = matmul("mm_dp_b", o_b16, d_yb16, "tn", F32, tn=1024)
    d_ob = matmul("mm_d_ob", d_yb16, p_b, "nt", F32, tn=1024)
    h32 = ((rows, wide), "rowh", LANES, F32, None)
    h16 = ((rows, wide), "rowh", LANES, BF16, None)
    gain = ((1, LANES), "par", LANES, F32, "all")
    d_oa, d_z16, d_gdn_ng = ew_bwd("gdn_post_b", f_post, [rowh(o_a), rowh(proj, 3 * NH), par(gdn_ng)], [(rowh(d_ya_in),)], [],
                                   lambda g, e: list(g), [h32, h16, gain], rows, NH)
    dval, dkc, dat, dqd, dkd, dgb_b = gdn_b_bwd(val, kcum, attn, qdec, kdec, gb, snaps, d_oa, nseq, seq)
    d_cq, d_ck, d_cv, d_gb, d_bb = gdn_a_bwd(conv["q"], conv["k"], conv["v"], gb, bb, t_inv, dval, dkc, dat, dqd, dkd, dgb_b, rows)
    d_pre, d_conv = {}, {}
    tap = ((4, wide), "parh", LANES, F32, "inner")
    for mode, off, ctg in (("q", 0, d_cq), ("k", NH, d_ck), ("v", 2 * NH, d_cv)):
        d_pre[mode], d_conv[mode] = ew_bwd(f"conv_{mode}_b", make_f_conv(mode), [rowh(proj, off), parh(conv_w, off)],
                                           [(rowh(ctg),)], [], lambda g, e: list(g), [h16, tap], rows, NH, seq, "hi", CONV_HEADS)
    delta, = ew_fwd("fox_delta", f_delta, [rowh(d_ob), rowh(o_b)], [o_rowh(F32)], rows, NH)
    d_fqn, d_cq_b = fox_dq(fqn, fkn, proj, ct, d_ob, lse, delta, nseq, seq)
    d_fkn, d_fv16, d_ck_b = fox_dkv(fqn, fkn, proj, cb, d_ob, lse, delta, nseq, seq)
    d_fq16, d_qn_g = ew_bwd("fox_qn_b", f_rms, [rowh(proj, FOX_Q), par(qn_g)], [(rowh(d_fqn),)], [], lambda g, e: list(g),
                            [h16, gain], rows, NH)
    d_fk16, d_kn_g = ew_bwd("fox_kn_b", f_rms, [rowh(proj, FOX_K), par(kn_g)], [(rowh(d_fkn),)], [], lambda g, e: list(g),
                            [h16, gain], rows, NH)
    narrow = ((rows, LANES), "row", LANES, F32, None)
    d_so, d_cs = ew_bwd("bcast_b", f_bcast, [row(so), row(cs)], [(rowh(d_gb),), (rowh(d_bb),), (rowh(d_cq_b), rowh(d_ck_b))], [],
                        lambda g, e: list(g), [narrow, narrow], rows, NH)
    d_logf = cumsum_time("cumsum_b", d_cs, nseq, seq, True)
    vec = ((1, LANES), "par", LANES, F32, "all")
    d_sp16, d_p1, d_p2 = ew_bwd("small_b", f_small, [row(sp), par(p1), par(p2)], [(row(d_so), row(d_logf))], [],
                                lambda g, e: list(g), [((rows, LANES), "row", LANES, BF16, None), vec, vec], rows)
    d_proj16 = jnp.concatenate([d_pre["q"], d_pre["k"], d_pre["v"], d_z16, d_fq16, d_fk16, d_fv16, d_ga16, d_gb16], axis=1)
    dw_main = matmul("mm_dw_main", d_proj16, u, "tn", F32)
    dw_small = matmul("mm_dw_small", d_sp16, u, "tn", F32)
    d_u = matmul("mm_d_u_small", d_sp16, wt_small, "nn", F32)
    d_u = matmul("mm_d_u", d_proj16, wt_main, "nn", F32, add=d_u)
    dx, dg1 = ew_bwd("rms1_b", f_rms, [row(x2), par(g1)], [(row(d_u),)], [row(dh)], lambda g, e: [g[0] + e[0], g[1]],
                     [((rows, dm), "row", dm, F32, None), ((1, dm), "par", dm, F32, "all")], rows)
    d_conv_w = jnp.concatenate([d_conv["q"], d_conv["k"], d_conv["v"]], axis=1)
    return dict(loss_acc=loss_acc, dx=dx, g1=dg1, g2=dg2, gdn_ng=d_gdn_ng, qn=d_qn_g, kn=d_kn_g, p1=d_p1, p2=d_p2,
                conv=d_conv_w, w_main=dw_main, w_small=dw_small, p_a=dp_a, p_b=dp_b, w_o=dw_o, w_u=dw_u, w_d=dw_d)


_W = NH * LANES
_A0, _A1 = 4 * _W, 4 * _W + 2 * NH
_B0, _B1 = _A1 + 3 * _W, _A1 + 3 * _W + NH
N_IN = _B1 + 2 * _W


def _split_w_in(full_t):
    main = jnp.concatenate([full_t[:_A0], full_t[_A1:_B0], full_t[_B1:]], axis=0)
    small = jnp.concatenate([full_t[_A0:_A1], full_t[_B0:_B1], jnp.zeros((LANES - 3 * NH, full_t.shape[1]), full_t.dtype)], axis=0)
    return main, small


def _join_w_in(main, small):
    return jnp.concatenate([main[:_A0], small[:2 * NH], main[_A0:_A0 + 3 * _W], small[2 * NH:3 * NH], main[_A0 + 3 * _W:]], axis=0)


def _lanes(v, at=0):
    return jnp.pad(v.reshape(1, -1), ((0, 0), (at, LANES - at - v.size)))


def kernel(x, norm_mix_g, w_in, gdn_conv_w, gdn_a_log, gdn_dt_bias, gdn_norm_g, fox_q_norm_g, fox_k_norm_g, fox_f_bias, w_proj_gdn, w_proj_fox, w_out, norm_mlp_g, w_up, w_down, loss_target, m_norm_mix_g, m_w_in, m_gdn_conv_w, m_gdn_a_log, m_gdn_dt_bias, m_gdn_norm_g, m_fox_q_norm_g, m_fox_k_norm_g, m_fox_f_bias, m_w_proj_gdn, m_w_proj_fox, m_w_out, m_norm_mlp_g, m_w_up, m_w_down, v_norm_mix_g, v_w_in, v_gdn_conv_w, v_gdn_a_log, v_gdn_dt_bias, v_gdn_norm_g, v_fox_q_norm_g, v_fox_k_norm_g, v_fox_f_bias, v_w_proj_gdn, v_w_proj_fox, v_w_out, v_norm_mlp_g, v_w_up, v_w_down):
    nseq, seq, dm = x.shape
    rows = nseq * seq
    xi, yi, ci = lax.axis_index("x"), lax.axis_index("y"), lax.axis_index("c")
    chip = 2 * xi + yi
    conv_cols = gdn_conv_w.shape[2]

    tr = lambda a: jnp.swapaxes(a[0], 0, 1)
    big = [tr(w_in), w_proj_gdn[0], w_proj_fox[0], w_out[0], w_up[0], w_down[0]]
    axes = [1, 0, 0, 0, 0, 0]
    big16 = [w.astype(BF16) for w in big]
    fill = lambda got, own: lax.dynamic_update_index_in_dim(got, own, chip, 0)
    got_in, = gather_weights(big16[:1], axes[:1])
    wt_main, wt_small = _split_w_in(fill(got_in, big16[0]).reshape(-1, dm))
    send, recv, sent, zones, token = gather_start(big16[1:], got_in)

    def late_weights(after):
        g_pa, g_pb, g_wo, w_u, g_wd = (fill(got, own) for got, own in zip(gather_wait(send, recv, sent, zones, after), big16[1:]))
        return (*(g.reshape(-1, dm) for g in (g_pa, g_pb, g_wo)), w_u, g_wd.reshape(-1, dm))
    conv_slot = jnp.zeros((4, 4, conv_cols), F32).at[:, chip].set(jnp.where(ci == 0, gdn_conv_w[0], 0.0))
    conv_full = all_reduce_small("gather_conv", conv_slot.reshape(-1, LANES)).reshape(4, 4 * conv_cols)
    p1 = _lanes(gdn_dt_bias[0]) + _lanes(fox_f_bias[0], 2 * NH)
    p2 = _lanes(gdn_a_log[0])

    g = local_step(x.reshape(rows, dm), loss_target.reshape(rows, dm), norm_mix_g + token[0, 0], norm_mlp_g, gdn_norm_g,
                   fox_q_norm_g, fox_k_norm_g, p1, p2, conv_full, wt_main, wt_small, late_weights, nseq, seq)

    small_parts = [g["loss_acc"], g["g1"].reshape(8, LANES), g["g2"].reshape(8, LANES), g["gdn_ng"], g["qn"], g["kn"], g["p1"], g["p2"],
                   g["conv"].reshape(-1, LANES)]
    tiled = [jnp.pad(p, ((0, -p.shape[0] % 8), (0, 0))) for p in small_parts]
    red = all_reduce_small("reduce_small", jnp.concatenate(tiled, axis=0))
    pos, red_parts = 0, []
    for p, t in zip(small_parts, tiled):
        red_parts.append(red[pos:pos + p.shape[0]])
        pos += t.shape[0]
    r_loss, r_g1, r_g2, r_gdn_ng, r_qn, r_kn, r_p1, r_p2, r_conv = red_parts
    loss = jnp.sum(r_loss)
    g_conv = lax.dynamic_slice_in_dim(r_conv.reshape(4, 4, conv_cols), chip, 1, axis=1).reshape(4, conv_cols)
    small_grads = [r_g1.reshape(1, dm), r_p2[:, :NH], r_p1[:, :NH], r_gdn_ng, r_qn, r_kn, r_p1[:, 2 * NH:3 * NH], r_g2.reshape(1, dm)]
    small_w = [norm_mix_g, gdn_a_log, gdn_dt_bias, gdn_norm_g, fox_q_norm_g, fox_k_norm_g, fox_f_bias, norm_mlp_g]
    small_m = [m_norm_mix_g, m_gdn_a_log, m_gdn_dt_bias, m_gdn_norm_g, m_fox_q_norm_g, m_fox_k_norm_g, m_fox_f_bias, m_norm_mlp_g]
    small_v = [v_norm_mix_g, v_gdn_a_log, v_gdn_dt_bias, v_gdn_norm_g, v_fox_q_norm_g, v_fox_k_norm_g, v_fox_f_bias, v_norm_mlp_g]

    def pack(parts):
        flat = jnp.concatenate([jnp.pad(p.reshape(-1), (0, -p.size % LANES)) for p in parts])
        return jnp.pad(flat, (0, -flat.size % (8 * LANES))).reshape(-1, LANES)

    packed = adamw("adamw_small", pack(small_w + [gdn_conv_w[0]]), pack(small_grads + [g_conv]),
                   pack(small_m + [m_gdn_conv_w[0]]), pack(small_v + [v_gdn_conv_w[0]]))

    def unpack(flat2d):
        flat, pos, res = flat2d.reshape(-1), 0, []
        for p in small_w + [gdn_conv_w[0]]:
            res.append(flat[pos:pos + p.size].reshape(p.shape))
            pos += p.size + (-p.size % LANES)
        return res

    s_delta, s_m, s_v = (unpack(a) for a in packed)

    blocks = [_join_w_in(g["w_main"], g["w_small"]).reshape(4, -1, dm), g["p_a"].reshape(4, -1, dm), g["p_b"].reshape(4, -1, dm),
              g["w_o"].reshape(4, -1, dm), g["w_u"], g["w_d"].reshape(4, -1, dm)]
    core = ci.reshape(1).astype(jnp.int32)
    swapped = pair_swap(blocks, axes)
    chip_part = add_pair("add_pair", blocks, swapped, core, axes)
    slots = chip_exchange(chip_part)
    halves = add_chips("add_chips", slots, chip_part, chip.reshape(1).astype(jnp.int32), axes)
    others = pair_send(halves)
    big_m = [tr(m_w_in), m_w_proj_gdn[0], m_w_proj_fox[0], m_w_out[0], m_w_up[0], m_w_down[0]]
    big_v = [tr(v_w_in), v_w_proj_gdn[0], v_w_proj_fox[0], v_w_out[0], v_w_up[0], v_w_down[0]]
    names = ["w_in", "w_proj_gdn", "w_proj_fox", "w_out", "w_up", "w_down"]
    big_res, big_grad = {}, {}
    for nm, w, mine, other, m, v, ax in zip(names, big, halves, others, big_m, big_v, axes):
        res = adamw_halves(f"adamw_{nm}", w, mine, other, m, v, core, ax)
        if nm == "w_in":
            res = [jnp.swapaxes(r, 0, 1) for r in res]
        big_grad[nm], *big_res[nm] = res

    order = ["norm_mix_g", "w_in", "gdn_conv_w", "gdn_a_log", "gdn_dt_bias", "gdn_norm_g", "fox_q_norm_g", "fox_k_norm_g",
             "fox_f_bias", "w_proj_gdn", "w_proj_fox", "w_out", "norm_mlp_g", "w_up", "w_down"]
    small_names = ["norm_mix_g", "gdn_a_log", "gdn_dt_bias", "gdn_norm_g", "fox_q_norm_g", "fox_k_norm_g", "fox_f_bias", "norm_mlp_g",
                   "gdn_conv_w"]
    small_idx = {nm: i for i, nm in enumerate(small_names)}
    shapes = dict(zip(order, (a.shape for a in (norm_mix_g, w_in, gdn_conv_w, gdn_a_log, gdn_dt_bias, gdn_norm_g, fox_q_norm_g,
                                                 fox_k_norm_g, fox_f_bias, w_proj_gdn, w_proj_fox, w_out, norm_mlp_g, w_up, w_down))))
    grads_out, delta_out, m_out, v_out = [], [], [], []
    for nm in order:
        if nm in big_res:
            d, mm, vv = big_res[nm]
            gr = big_grad[nm]
        else:
            i = small_idx[nm]
            gr = (small_grads + [g_conv])[i]
            d, mm, vv = s_delta[i], s_m[i], s_v[i]
        for lst, val in ((grads_out, gr), (delta_out, d), (m_out, mm), (v_out, vv)):
            lst.append(val.reshape(shapes[nm]))
    return (loss, g["dx"].reshape(x.shape), *grads_out, *delta_out, *m_out, *v_out)
```

```python
import functools

import jax
import jax.numpy as jnp
from jax import lax
from jax.experimental import pallas as pl
from jax.experimental.pallas import tpu as pltpu

F32 = jnp.float32
BF16 = jnp.bfloat16
LANES = 128
NH = 8
EPS = 1e-6
GDN_CHUNK = 64
GDN_ROWS = 256
GDN_BASE = 16
ROW_TILE = 512
CONV_HEADS = 2
ATT_TILE = 512
NEG = -1e30
VMEM_LIMIT_BYTES = 48 * 1024 * 1024
HI = lax.Precision.HIGHEST
LO = lax.Precision.DEFAULT
MESH = pl.DeviceIdType.MESH
ANY = pl.BlockSpec(memory_space=pl.ANY)

ADAM_LR, ADAM_B1, ADAM_B2, ADAM_EPS, ADAM_WD, ADAM_STEP = 0.001, 0.9, 0.999, 1e-08, 0.01, 10


def _params(n_grid):
    return pltpu.CompilerParams(dimension_semantics=("arbitrary",) * n_grid,
                                vmem_limit_bytes=VMEM_LIMIT_BYTES)


def _dot(a, b, dims, precision=None):
    dn = {"nn": (((1,), (0,)), ((), ())), "nt": (((1,), (1,)), ((), ())), "tn": (((0,), (0,)), ((), ()))}[dims]
    return lax.dot_general(a, b, dn, precision=precision, preferred_element_type=F32)


def _iota(shape, dim):
    return lax.broadcasted_iota(jnp.int32, shape, dim)


def _split(x, parts):
    out = []
    for _ in range(parts - 1):
        hi = x.astype(BF16)
        out.append(hi)
        x = x - hi.astype(F32)
    return out + [x.astype(BF16)]


def _dot_mask(mask, b, dims):
    m16 = mask.astype(BF16)
    b1, b2, b3 = _split(b, 3)
    return _dot(m16, b1, dims) + (_dot(m16, b2, dims) + _dot(m16, b3, dims))


@jax.custom_vjp
def mm_mask(mask, b):
    return _dot_mask(mask, b, "nn")


mm_mask.defvjp(lambda mask, b: (_dot_mask(mask, b, "nn"), mask),
               lambda mask, g: (jnp.zeros_like(mask), _dot_mask(mask, g, "tn")))


def matmul(name, a, b, dims, out_dtype, add=None, tm=1024, tn=1024, tk=512, col_blocks=None,
           extras=(), epilogue=None, out_dtypes=None):
    if col_blocks and dims != "tn":
        nb, b_rows, bw = b.shape
        b_shape = (b_rows, nb * bw)
    else:
        b_shape = b.shape
    if dims == "nn":
        (m, k), (_, n) = a.shape, b_shape
    elif dims == "nt":
        (m, k), (n, _) = a.shape, b_shape
    else:
        (k, m), (_, n) = a.shape, b_shape
    if k <= 1024:
        tk = k
    tm, tn, tk = min(tm, m), min(tn, n), min(tk, k)
    assert m % tm == 0 and n % tn == 0 and k % tk == 0, (name, m, n, k)
    nk = k // tk
    a_spec = pl.BlockSpec((tk, tm), lambda i, j, kk: (kk, i)) if dims == "tn" else pl.BlockSpec((tm, tk), lambda i, j, kk: (i, kk))
    b_spec = pl.BlockSpec((tn, tk), lambda i, j, kk: (j, kk)) if dims == "nt" else pl.BlockSpec((tk, tn), lambda i, j, kk: (kk, j))
    o_spec = pl.BlockSpec((tm, tn), lambda i, j, kk: (i, j))
    out_shape = (m, n)
    if col_blocks and dims == "nn":
        per = bw // tn
        assert bw % tn == 0
        b_spec = pl.BlockSpec((None, tk, tn), lambda i, j, kk: (j // per, kk, j % per))
    elif col_blocks and dims == "nt":
        per = bw // tk
        assert bw % tk == 0
        b_spec = pl.BlockSpec((None, tn, tk), lambda i, j, kk: (kk // per, j, kk % per))
    elif col_blocks:
        bw = n // col_blocks
        per = bw // tn
        assert bw % tn == 0 and add is None
        o_spec = pl.BlockSpec((None, tm, tn), lambda i, j, kk: (j // per, i, j % per))
        out_shape = (col_blocks, m, bw)
    extras = list(extras) + ([add] if add is not None else [])
    if add is not None:
        assert epilogue is None
        epilogue = lambda r, *e: [r + e[-1]]
    out_dtypes = [out_dtype] if epilogue is None or out_dtypes is None else list(out_dtypes)
    n_ex, n_out = len(extras), len(out_dtypes)

    def body(*refs):
        a_ref, b_ref = refs[0], refs[1]
        ex_refs, o_refs = refs[2:2 + n_ex], refs[2 + n_ex:2 + n_ex + n_out]

        def finish(r):
            res = [r] if epilogue is None else epilogue(r, *[e[...] for e in ex_refs])
            for o_ref, v in zip(o_refs, res):
                o_ref[...] = v.astype(o_ref.dtype)

        if nk == 1:
            finish(_dot(a_ref[...], b_ref[...], dims))
            return
        acc_ref = refs[-1]
        kk = pl.program_id(2)

        @pl.when(kk == 0)
        def _():
            acc_ref[...] = jnp.zeros_like(acc_ref)

        acc_ref[...] += _dot(a_ref[...], b_ref[...], dims)

        @pl.when(kk == nk - 1)
        def _():
            finish(acc_ref[...])

    res = pl.pallas_call(
        body, name=name, grid=(m // tm, n // tn, nk), in_specs=[a_spec, b_spec] + [o_spec] * n_ex, out_specs=[o_spec] * n_out,
        out_shape=[jax.ShapeDtypeStruct(out_shape, dt) for dt in out_dtypes],
        scratch_shapes=[pltpu.VMEM((tm, tn), F32)] if nk > 1 else [], compiler_params=_params(3),
    )(a, b, *extras)
    return res[0] if n_out == 1 else res


def _ew_spec(kind, off, width, tb, hp, order, shape=None):
    def ih(g0, g1):
        return (g0, g1) if order == "ih" else (g1, g0)

    assert off % hp == 0 or kind in ("row", "par")
    if kind == "row":
        return pl.BlockSpec((tb, width), lambda g0, g1: (ih(g0, g1)[0], off))
    if kind == "rowh":
        return pl.BlockSpec((tb, hp * width), lambda g0, g1: (ih(g0, g1)[0], ih(g0, g1)[1] + off // hp))
    if kind == "par":
        return pl.BlockSpec(shape, lambda g0, g1: (0, 0))
    if kind == "parh":
        return pl.BlockSpec((shape[0], hp * width), lambda g0, g1: (0, ih(g0, g1)[1] + off // hp))
    raise ValueError(kind)


def _ew_grid(rows, tb, nh, hp, order):
    assert nh % hp == 0 and rows % tb == 0
    return (rows // tb, nh // hp) if order == "ih" else (nh // hp, rows // tb)


def _ew_load(ref, kind, width, hh):
    if kind in ("row", "par"):
        return ref[...].astype(F32)
    return ref[:, hh * width:(hh + 1) * width].astype(F32)


def ew_fwd(name, f, ins, outs, rows, nh=1, tb=ROW_TILE, order="ih", hp=None):
    hp = nh if hp is None else hp
    n_in = len(ins)

    def body(*refs):
        hb = pl.program_id(1) if order == "ih" else pl.program_id(0)
        for hh in range(hp):
            h = hh if hp == nh else hb * hp + hh
            vals = [_ew_load(r, kd, w, hh) for r, (_, kd, _, w) in zip(refs[:n_in], ins)]
            res = f(h, *vals)
            for r, v, (_, kd, w, _) in zip(refs[n_in:], res, outs):
                if kd == "row":
                    assert hp == 1
                    r[...] = v.astype(r.dtype)
                else:
                    r[:, hh * w:(hh + 1) * w] = v.astype(r.dtype)

    in_specs = [_ew_spec(kd, off, w, tb, hp, order, a.shape) for (a, kd, off, w) in ins]
    out_specs = [_ew_spec(kd, 0, w, tb, hp, order) for (_, kd, w, _) in outs]
    out_shape = [jax.ShapeDtypeStruct((rows, tw), dt) for (tw, _, _, dt) in outs]
    return pl.pallas_call(
        body, name=name, grid=_ew_grid(rows, tb, nh, hp, order), in_specs=in_specs, out_specs=out_specs,
        out_shape=out_shape, compiler_params=_params(2),
    )(*[a for (a, _, _, _) in ins])


def ew_bwd(name, f, ins, cts, extras, emit, outs, rows, nh=1, tb=ROW_TILE, order="ih", hp=None):
    hp = nh if hp is None else hp
    n_in = len(ins)
    flat_cts = [d for group in cts for d in group]
    n_ct, n_ex = len(flat_cts), len(extras)

    def body(*refs):
        g0, g1 = pl.program_id(0), pl.program_id(1)
        hb = g1 if order == "ih" else g0
        out_refs = refs[n_in + n_ct + n_ex:]
        shared = [None] * len(outs)

        def store(r, v, first, sl=None):
            def put(val, add):
                if sl is None:
                    r[...] = (r[...] + val if add else val).astype(r.dtype)
                else:
                    r[:, sl] = (r[:, sl] + val if add else val).astype(r.dtype)

            if first is None:
                put(v, False)
            else:
                pl.when(first)(lambda: put(v, False))
                pl.when(jnp.logical_not(first))(lambda: put(v, True))

        for hh in range(hp):
            h = hh if hp == nh else hb * hp + hh
            vals = [_ew_load(r, kd, w, hh) for r, (_, kd, _, w) in zip(refs[:n_in], ins)]
            ct_refs = list(zip(refs[n_in:n_in + n_ct], flat_cts))
            ct_vals, pos = [], 0
            for group in cts:
                v = None
                for r, (_, kd, _, w) in ct_refs[pos:pos + len(group)]:
                    t = _ew_load(r, kd, w, hh)
                    v = t if v is None else v + t
                pos += len(group)
                ct_vals.append(v)
            ex_vals = [_ew_load(r, kd, w, hh) for r, (_, kd, _, w) in zip(refs[n_in + n_ct:n_in + n_ct + n_ex], extras)]
            _, vjp = jax.vjp(lambda *a: f(h, *a), *vals)
            res = emit(vjp(tuple(ct_vals)), ex_vals)
            for idx, (r, v, (_, kd, w, _, acc)) in enumerate(zip(out_refs, res, outs)):
                if kd in ("row", "par"):
                    shared[idx] = v if shared[idx] is None else shared[idx] + v
                else:
                    store(r, v, (g1 == 0) if acc == "inner" else None, slice(hh * w, (hh + 1) * w))
        for idx, (r, (_, kd, _, _, acc)) in enumerate(zip(out_refs, outs)):
            if kd in ("row", "par"):
                assert acc == "all" or hp == nh
                store(r, shared[idx], jnp.logical_and(g0 == 0, g1 == 0) if acc == "all" else None)

    operands = list(ins) + flat_cts + list(extras)
    in_specs = [_ew_spec(kd, off, w, tb, hp, order, a.shape) for (a, kd, off, w) in operands]
    out_specs = [_ew_spec(kd, 0, w, tb, hp, order, shp) for (shp, kd, w, _, _) in outs]
    out_shape = [jax.ShapeDtypeStruct(shp, dt) for (shp, _, _, dt, _) in outs]
    return pl.pallas_call(
        body, name=name, grid=_ew_grid(rows, tb, nh, hp, order), in_specs=in_specs, out_specs=out_specs,
        out_shape=out_shape, compiler_params=_params(2),
    )(*[a for (a, _, _, _) in operands])


def f_rms(h, x, g):
    r = lax.rsqrt(jnp.mean(x * x, axis=-1, keepdims=True) + EPS)
    return (x * r * g,)


def _softplus(z):
    return jnp.maximum(z, 0.0) + jnp.log1p(jnp.exp(-jnp.abs(z)))


def f_small(h, sp, p1, p2):
    lane = _iota(sp.shape, 1)
    z = sp + p1
    g = -jnp.exp(p2) * _softplus(z)
    beta = jax.nn.sigmoid(z)
    logf = -_softplus(-z)
    return (jnp.where(lane < NH, g, jnp.where(lane < 2 * NH, beta, jnp.where(lane < 3 * NH, logf, 0.0))),)


def _pick(x, lane_id):
    lane = _iota(x.shape, 1)
    col = jnp.sum(jnp.where(lane == lane_id, x, 0.0), axis=1, keepdims=True)
    return jnp.broadcast_to(col, x.shape)


def f_bcast(h, so, cs):
    return _pick(so, h), _pick(so, h + NH), _pick(cs, h + 2 * NH)


def _shift_down(s):
    def down(x):
        return jnp.where(_iota(x.shape, 0) >= s, pltpu.roll(x, s, 0), 0.0)

    def up(g):
        n = g.shape[0]
        return jnp.where(_iota(g.shape, 0) < n - s, pltpu.roll(g, n - s, 0), 0.0)

    @jax.custom_vjp
    def shift(x):
        return down(x)

    shift.defvjp(lambda x: (down(x), None), lambda _, g: (up(g),))
    return shift


def _silu(x):
    return x * jax.nn.sigmoid(x)


def make_f_conv(mode):
    sh1, sh2, sh3 = _shift_down(1), _shift_down(2), _shift_down(3)

    def f(h, x, w):
        sub = _iota(w.shape, 0)

        def tap(i):
            return jnp.sum(jnp.where(sub == i, w, 0.0), axis=0, keepdims=True)

        y = sh3(x) * tap(0)
        y = y + sh2(x) * tap(1)
        y = y + sh1(x) * tap(2)
        y = y + x * tap(3)
        s = _silu(y)
        if mode == "v":
            return (s,)
        n = s * lax.rsqrt(jnp.sum(s * s, axis=-1, keepdims=True) + EPS)
        if mode == "q":
            n = n * (LANES ** -0.5)
        return (n,)

    return f


def f_post(h, o, z, g):
    r = lax.rsqrt(jnp.mean(o * o, axis=-1, keepdims=True) + EPS)
    return (o * r * g * _silu(z),)


def f_merge(h, ga, gb, ya, yb):
    return (jax.nn.sigmoid(ga) * ya + jax.nn.sigmoid(gb) * yb,)


def f_delta(h, do, o):
    return (jnp.broadcast_to(jnp.sum(do * o, axis=1, keepdims=True), o.shape),)


def cumsum_time(name, x, nseq, seq, reverse):
    nb = seq // LANES

    def body(x_ref, o_ref):
        r, c = _iota((LANES, LANES), 0), _iota((LANES, LANES), 1)
        tri = jnp.where((r <= c) if reverse else (r >= c), 1.0, 0.0).astype(F32)
        carry = jnp.zeros((1, LANES), F32)
        for b in (range(nb - 1, -1, -1) if reverse else range(nb)):
            blk = x_ref[b * LANES:(b + 1) * LANES, :]
            o_ref[b * LANES:(b + 1) * LANES, :] = _dot_mask(tri, blk, "nn") + carry
            carry = carry + jnp.sum(blk, axis=0, keepdims=True)

    spec = pl.BlockSpec((seq, LANES), lambda s: (s, 0))
    return pl.pallas_call(body, name=name, grid=(nseq,), in_specs=[spec], out_specs=spec,
                          out_shape=jax.ShapeDtypeStruct(x.shape, F32), compiler_params=_params(1))(x)


def transpose_time(name, x, nseq, seq):
    def body(x_ref, o_ref):
        o_ref[...] = x_ref[...].T

    return pl.pallas_call(
        body, name=name, grid=(nseq,), in_specs=[pl.BlockSpec((seq, LANES), lambda s: (s, 0))],
        out_specs=pl.BlockSpec((LANES, seq), lambda s: (s, 0)),
        out_shape=jax.ShapeDtypeStruct((nseq * LANES, seq), F32), compiler_params=_params(1))(x)


def _gdn_masks():
    n = GDN_ROWS
    r, c = _iota((n, n), 0), _iota((n, n), 1)
    shift = GDN_CHUNK.bit_length() - 1
    same = lax.shift_right_logical(r, shift) == lax.shift_right_logical(c, shift)
    return r, c, same


def _gdn_decay(gb):
    r, c, same = _gdn_masks()
    seg_tril = jnp.where(jnp.logical_and(same, r >= c), 1.0, 0.0).astype(F32)
    g_cum = mm_mask(seg_tril, gb)
    lane0 = _iota(g_cum.shape, 1) == 0
    g_col = jnp.sum(jnp.where(lane0, g_cum, 0.0), axis=1, keepdims=True)
    g_row = jnp.sum(jnp.where(r == c, jnp.broadcast_to(g_col, (GDN_ROWS, GDN_ROWS)), 0.0), axis=0, keepdims=True)
    return g_cum, g_col - g_row


def gdn_f1(q, k, gb, bb):
    r, c, same = _gdn_masks()
    strict = jnp.logical_and(same, r > c)
    _, diff = _gdn_decay(gb)
    lane0 = _iota(bb.shape, 1) == 0
    beta_col = jnp.sum(jnp.where(lane0, bb, 0.0), axis=1, keepdims=True)
    kk = _dot(k, k, "nt", LO)
    return jnp.where(strict, beta_col * kk * jnp.exp(jnp.where(strict, diff, 0.0)), 0.0)


def gdn_f2(t_corr, q, k, v, gb, bb):
    r, c, same = _gdn_masks()
    incl = jnp.logical_and(same, r >= c)
    g_cum, diff = _gdn_decay(gb)
    decay = jnp.where(incl, jnp.exp(jnp.where(incl, diff, 0.0)), 0.0)
    e_g = jnp.exp(g_cum)
    v_beta, k_beta = v * bb, k * bb * e_g
    value = v_beta + _dot(t_corr, v_beta, "nn", LO)
    k_cum = k_beta + _dot(t_corr, k_beta, "nn", LO)
    attn = _dot(q, k, "nt", LO) * decay
    g_last = mm_mask(jnp.where(same, 1.0, 0.0).astype(F32), gb)
    return value, k_cum, attn, q * e_g, k * jnp.exp(g_last - g_cum)


def tri_inverse(a):
    n = GDN_ROWS
    r, c = _iota((n, n), 0), _iota((n, n), 1)
    shift = GDN_BASE.bit_length() - 1
    blk = lax.shift_right_logical(r, shift) == lax.shift_right_logical(c, shift)
    d = jnp.where(blk, a, 0.0)
    lo = a - d
    p = -d
    c_d = p
    for _ in range(shift - 1):
        p = _dot(p, p, "nn", LO)
        c_d = c_d + p + _dot(c_d, p, "nn", LO)
    assert GDN_CHUNK // GDN_BASE == 4
    nmat = lo + _dot(c_d, lo, "nn", LO)
    n2 = _dot(nmat, nmat, "nn", LO)
    c_n = (n2 - nmat) - _dot(nmat, n2, "nn", LO)
    return c_n + c_d + _dot(c_n, c_d, "nn", LO)


def gdn_a_fwd(q, k, v, gb, bb, rows):
    blk = pl.BlockSpec((GDN_ROWS, LANES), lambda i, h: (i, h))
    sq = pl.BlockSpec((GDN_ROWS, GDN_ROWS), lambda i, h: (i, h))

    def body(q_ref, k_ref, v_ref, gb_ref, bb_ref, val_ref, kc_ref, at_ref, qd_ref, kd_ref, t_ref):
        qv, kv, vv, gv, bv = q_ref[...], k_ref[...], v_ref[...], gb_ref[...], bb_ref[...]
        t_inv = tri_inverse(gdn_f1(qv, kv, gv, bv))
        value, k_cum, attn, q_dec, k_dec = gdn_f2(t_inv, qv, kv, vv, gv, bv)
        val_ref[...], kc_ref[...], at_ref[...], qd_ref[...], kd_ref[...], t_ref[...] = value, k_cum, attn, q_dec, k_dec, t_inv

    wide = jax.ShapeDtypeStruct((rows, NH * LANES), F32)
    square = jax.ShapeDtypeStruct((rows, NH * GDN_ROWS), F32)
    return pl.pallas_call(
        body, name="gdn_a_fwd", grid=(rows // GDN_ROWS, NH), in_specs=[blk] * 5,
        out_specs=[blk, blk, sq, blk, blk, sq], out_shape=[wide, wide, square, wide, wide, square],
        compiler_params=_params(2))(q, k, v, gb, bb)


def gdn_a_bwd(q, k, v, gb, bb, t_inv, dval, dkc, dat, dqd, dkd, dgb_b, rows):
    blk = pl.BlockSpec((GDN_ROWS, LANES), lambda i, h: (i, h))
    sq = pl.BlockSpec((GDN_ROWS, GDN_ROWS), lambda i, h: (i, h))

    def body(q_ref, k_ref, v_ref, gb_ref, bb_ref, t_ref, dval_ref, dkc_ref, dat_ref, dqd_ref, dkd_ref, dgbb_ref,
             dq_ref, dk_ref, dv_ref, dgb_ref, dbb_ref):
        qv, kv, vv, gv, bv, tv = q_ref[...], k_ref[...], v_ref[...], gb_ref[...], bb_ref[...], t_ref[...]
        _, vjp1 = jax.vjp(gdn_f1, qv, kv, gv, bv)
        _, vjp2 = jax.vjp(gdn_f2, tv, qv, kv, vv, gv, bv)
        dt, dq2, dk2, dv2, dgb2, dbb2 = vjp2((dval_ref[...], dkc_ref[...], dat_ref[...], dqd_ref[...], dkd_ref[...]))
        left = dt + _dot(tv, dt, "tn", LO)
        da = -(left + _dot(left, tv, "nt", LO))
        dq1, dk1, dgb1, dbb1 = vjp1(da)
        dq_ref[...] = dq1 + dq2
        dk_ref[...] = dk1 + dk2
        dv_ref[...] = dv2
        dgb_ref[...] = dgb1 + dgb2 + dgbb_ref[...]
        dbb_ref[...] = dbb1 + dbb2

    wide = jax.ShapeDtypeStruct((rows, NH * LANES), F32)
    return pl.pallas_call(
        body, name="gdn_a_bwd", grid=(rows // GDN_ROWS, NH),
        in_specs=[blk] * 5 + [sq, blk, blk, sq, blk, blk, blk], out_specs=[blk] * 5, out_shape=[wide] * 5,
        compiler_params=_params(2))(q, k, v, gb, bb, t_inv, dval, dkc, dat, dqd, dkd, dgb_b)


N_CH = GDN_ROWS // GDN_CHUNK


GDN_HP = 8


def gdn_fb(*args):
    per_head = 6 * N_CH
    states = list(args[GDN_HP * per_head:])
    outs = [[None] * N_CH for _ in range(GDN_HP)]
    zero = jnp.zeros((GDN_CHUNK, LANES), F32)
    for c in range(N_CH):
        for hh in range(GDN_HP):
            val, kc, at, qd, kd, gb = (args[hh * per_head + i * N_CH + c] for i in range(6))
            s = states[hh]
            v_new = val - _dot(kc, s, "nn", LO)
            v_pad = jnp.concatenate([zero] * c + [v_new] + [zero] * (N_CH - 1 - c), axis=0)
            outs[hh][c] = _dot(qd, s, "nn", LO) + _dot(at, v_pad, "nn", LO)
            dec = jnp.exp(jnp.sum(gb, axis=0, keepdims=True))
            states[hh] = s * dec + _dot(kd, v_new, "tn", LO)
    return (*[o for head in outs for o in head], *states)


def _gdn_piece(ref, hh, c):
    width = ref.shape[1] // GDN_HP
    return ref.at[c * GDN_CHUNK:(c + 1) * GDN_CHUNK, hh * width:(hh + 1) * width]


def _gdn_pieces(refs, hh):
    return [_gdn_piece(r, hh, c)[...] for r in refs for c in range(N_CH)]


def _gdn_b_specs(nb, rev):
    def blk_row(s, j):
        return s * nb + (nb - 1 - j if rev else j)

    blk = pl.BlockSpec((GDN_ROWS, GDN_HP * LANES), lambda s, hb, j: (blk_row(s, j), hb))
    sq = pl.BlockSpec((GDN_ROWS, GDN_HP * GDN_ROWS), lambda s, hb, j: (blk_row(s, j), hb))
    snap = pl.BlockSpec((GDN_HP * LANES, LANES), lambda s, hb, j: (blk_row(s, j) * (NH // GDN_HP) + hb, 0))
    return blk, sq, snap


def gdn_b_fwd(val, kc, at, qd, kd, gb, nseq, seq):
    nb = seq // GDN_ROWS
    rows = nseq * seq
    blk, sq, snap = _gdn_b_specs(nb, False)

    def body(val_ref, kc_ref, at_ref, qd_ref, kd_ref, gb_ref, o_ref, snap_ref, s_ref):
        @pl.when(pl.program_id(2) == 0)
        def _():
            s_ref[...] = jnp.zeros_like(s_ref)

        states = [s_ref[hh] for hh in range(GDN_HP)]
        for hh in range(GDN_HP):
            snap_ref[hh * LANES:(hh + 1) * LANES, :] = states[hh]
        pieces = [p for hh in range(GDN_HP) for p in _gdn_pieces([val_ref, kc_ref, at_ref, qd_ref, kd_ref, gb_ref], hh)]
        res = gdn_fb(*pieces, *states)
        for hh in range(GDN_HP):
            for c in range(N_CH):
                _gdn_piece(o_ref, hh, c)[...] = res[hh * N_CH + c]
            s_ref[hh] = res[GDN_HP * N_CH + hh]

    return pl.pallas_call(
        body, name="gdn_b_fwd", grid=(nseq, NH // GDN_HP, nb), in_specs=[blk, blk, sq, blk, blk, blk], out_specs=[blk, snap],
        out_shape=[jax.ShapeDtypeStruct((rows, NH * LANES), F32), jax.ShapeDtypeStruct((nseq * nb * NH * LANES, LANES), F32)],
        scratch_shapes=[pltpu.VMEM((GDN_HP, LANES, LANES), F32)], compiler_params=_params(3))(val, kc, at, qd, kd, gb)


def gdn_b_bwd(val, kc, at, qd, kd, gb, snaps, do, nseq, seq):
    nb = seq // GDN_ROWS
    rows = nseq * seq
    blk, sq, snap = _gdn_b_specs(nb, True)

    def body(val_ref, kc_ref, at_ref, qd_ref, kd_ref, gb_ref, snap_ref, do_ref,
             dval_ref, dkc_ref, dat_ref, dqd_ref, dkd_ref, dgb_ref, ds_ref):
        @pl.when(pl.program_id(2) == 0)
        def _():
            ds_ref[...] = jnp.zeros_like(ds_ref)

        pieces = [p for hh in range(GDN_HP) for p in _gdn_pieces([val_ref, kc_ref, at_ref, qd_ref, kd_ref, gb_ref], hh)]
        states = [snap_ref[hh * LANES:(hh + 1) * LANES, :] for hh in range(GDN_HP)]
        _, vjp = jax.vjp(gdn_fb, *pieces, *states)
        cts = [p for hh in range(GDN_HP) for p in _gdn_pieces([do_ref], hh)] + [ds_ref[hh] for hh in range(GDN_HP)]
        grads = vjp(tuple(cts))
        for hh in range(GDN_HP):
            for i, r in enumerate([dval_ref, dkc_ref, dat_ref, dqd_ref, dkd_ref, dgb_ref]):
                for c in range(N_CH):
                    _gdn_piece(r, hh, c)[...] = grads[hh * 6 * N_CH + i * N_CH + c]
            ds_ref[hh] = grads[GDN_HP * 6 * N_CH + hh]

    wide = jax.ShapeDtypeStruct((rows, NH * LANES), F32)
    square = jax.ShapeDtypeStruct((rows, NH * GDN_ROWS), F32)
    return pl.pallas_call(
        body, name="gdn_b_bwd", grid=(nseq, NH // GDN_HP, nb), in_specs=[blk, blk, sq, blk, blk, blk, snap, blk],
        out_specs=[blk, blk, sq, blk, blk, blk], out_shape=[wide, wide, square, wide, wide, wide],
        scratch_shapes=[pltpu.VMEM((GDN_HP, LANES, LANES), F32)], compiler_params=_params(3))(val, kc, at, qd, kd, gb, snaps, do)


FOX_Q, FOX_K, FOX_V = 4 * NH, 5 * NH, 6 * NH
FOX_SCALE = LANES ** -0.5


def _head_row(ct_ref, h, off, width):
    blk = ct_ref[:, pl.ds(off, width)]
    return jnp.sum(jnp.where(_iota(blk.shape, 0) == h, blk, 0.0), axis=0, keepdims=True)


def _col(x):
    return jnp.max(x, axis=1, keepdims=True)


def _row(x):
    return jnp.max(x.T, axis=0, keepdims=True)


def _causal(shape, q_dim):
    return _iota(shape, q_dim) >= _iota(shape, 1 - q_dim)


def fox_fwd(qn, kn, proj, ct, nseq, seq):
    tq = tk = min(ATT_TILE, seq)
    nq = seq // tq
    rows = nseq * seq
    qblk = pl.BlockSpec((tq, LANES), lambda s, h, i: (s * nq + i, h))
    full = pl.BlockSpec((seq, LANES), lambda s, h, i: (s, h))
    vfull = pl.BlockSpec((seq, LANES), lambda s, h, i: (s, h + FOX_V))
    ctb = pl.BlockSpec((NH, seq), lambda s, h, i: (s * (LANES // NH) + 2, 0))

    def body(q_ref, k_ref, v_ref, ct_ref, o_ref, o16_ref, lse_ref):
        h, i = pl.program_id(1), pl.program_id(2)
        q = q_ref[...]

        def step(j, carry, diag):
            m, l, acc = carry
            off = pl.multiple_of(j * tk, tk)
            s = _dot(q, k_ref[pl.ds(off, tk), :], "nt") * FOX_SCALE - _head_row(ct_ref, h, off, tk)
            if diag:
                s = jnp.where(_causal(s.shape, 0), s, NEG)
            m_new = jnp.maximum(m, jnp.max(s, axis=1, keepdims=True))
            p = jnp.exp(s - m_new)
            alpha = jnp.exp(m - m_new)
            l = alpha * l + jnp.sum(p, axis=1, keepdims=True)
            acc = alpha * acc + _dot(p.astype(BF16), v_ref[pl.ds(off, tk), :].astype(BF16), "nn")
            return m_new, l, acc

        init = (jnp.full((tq, 1), NEG, F32), jnp.zeros((tq, 1), F32), jnp.zeros((tq, LANES), F32))
        carry = lax.fori_loop(0, i, lambda j, c: step(j, c, False), init)
        m, l, acc = step(i, carry, True)
        o = acc / l
        o_ref[...] = o
        o16_ref[...] = o.astype(BF16)
        lse_ref[...] = jnp.broadcast_to(m + jnp.log(l), (tq, LANES))

    wide = (rows, NH * LANES)
    return pl.pallas_call(
        body, name="fox_fwd", grid=(nseq, NH, nq), in_specs=[qblk, full, vfull, ctb], out_specs=[qblk] * 3,
        out_shape=[jax.ShapeDtypeStruct(wide, F32), jax.ShapeDtypeStruct(wide, BF16), jax.ShapeDtypeStruct(wide, F32)],
        compiler_params=_params(3))(qn, kn, proj, ct)


def fox_dq(qn, kn, proj, ct, do, lse, delta, nseq, seq):
    tq = tk = min(ATT_TILE, seq)
    nq = seq // tq
    rows = nseq * seq
    qblk = pl.BlockSpec((tq, LANES), lambda s, h, i: (s * nq + i, h))
    full = pl.BlockSpec((seq, LANES), lambda s, h, i: (s, h))
    vfull = pl.BlockSpec((seq, LANES), lambda s, h, i: (s, h + FOX_V))
    ctb = pl.BlockSpec((NH, seq), lambda s, h, i: (s * (LANES // NH) + 2, 0))

    def body(q_ref, k_ref, v_ref, ct_ref, do_ref, lse_ref, dl_ref, dq_ref, dc_ref):
        h, i = pl.program_id(1), pl.program_id(2)
        q = q_ref[...]
        lse, delta = _col(lse_ref[...]), _col(dl_ref[...])
        do16 = do_ref[...].astype(BF16)

        def step(j, carry, diag):
            dq, dc = carry
            off = pl.multiple_of(j * tk, tk)
            k = k_ref[pl.ds(off, tk), :]
            p = jnp.exp(_dot(q, k, "nt") * FOX_SCALE - _head_row(ct_ref, h, off, tk) - lse)
            if diag:
                p = jnp.where(_causal(p.shape, 0), p, 0.0)
            dp = _dot(do16, v_ref[pl.ds(off, tk), :].astype(BF16), "nt")
            ds = p * (dp - delta)
            return dq + _dot(ds.astype(BF16), k, "nn"), dc + jnp.sum(ds, axis=1, keepdims=True)

        init = (jnp.zeros((tq, LANES), F32), jnp.zeros((tq, 1), F32))
        dq, dc = step(i, lax.fori_loop(0, i, lambda j, c: step(j, c, False), init), True)
        dq_ref[...] = dq * FOX_SCALE
        dc_ref[...] = jnp.where(_iota((tq, LANES), 1) == 0, dc, 0.0)

    wide = jax.ShapeDtypeStruct((rows, NH * LANES), F32)
    return pl.pallas_call(
        body, name="fox_dq", grid=(nseq, NH, nq), in_specs=[qblk, full, vfull, ctb, qblk, qblk, qblk],
        out_specs=[qblk, qblk], out_shape=[wide, wide], compiler_params=_params(3))(qn, kn, proj, ct, do, lse, delta)


def fox_dkv(qn, kn, proj, cb, do, lse, delta, nseq, seq):
    tq = tk = min(ATT_TILE, seq)
    nq = seq // tq
    rows = nseq * seq
    kblk = pl.BlockSpec((tk, LANES), lambda s, h, j: (s * nq + j, h))
    vblk = pl.BlockSpec((tk, LANES), lambda s, h, j: (s * nq + j, h + FOX_V))
    full = pl.BlockSpec((seq, LANES), lambda s, h, j: (s, h))

    def body(q_ref, k_ref, v_ref, cb_ref, do_ref, lse_ref, dl_ref, dk_ref, dv_ref, dc_ref):
        j = pl.program_id(2)
        k = k_ref[...]
        v16 = v_ref[...].astype(BF16)
        ck = _col(cb_ref[...])

        def step(i, carry, diag):
            dk, dv, dc = carry
            off = pl.multiple_of(i * tq, tq)
            q = q_ref[pl.ds(off, tq), :]
            do16 = do_ref[pl.ds(off, tq), :].astype(BF16)
            lse, delta = (_row(r[pl.ds(off, tq), :]) for r in (lse_ref, dl_ref))
            p = jnp.exp(_dot(k, q, "nt") * FOX_SCALE - ck - lse)
            if diag:
                p = jnp.where(_causal(p.shape, 1), p, 0.0)
            dv = dv + _dot(p.astype(BF16), do16, "nn")
            ds = p * (_dot(v16, do16, "nt") - delta)
            return dk + _dot(ds.astype(BF16), q, "nn"), dv, dc + jnp.sum(ds, axis=1, keepdims=True)

        zero = jnp.zeros((tk, LANES), F32)
        carry = step(j, (zero, zero, jnp.zeros((tk, 1), F32)), True)
        dk, dv, dc = lax.fori_loop(j + 1, nq, lambda i, c: step(i, c, False), carry)
        dk_ref[...] = dk * FOX_SCALE
        dv_ref[...] = dv.astype(BF16)
        dc_ref[...] = jnp.where(_iota((tk, LANES), 1) == 0, -dc, 0.0)

    wide = (rows, NH * LANES)
    return pl.pallas_call(
        body, name="fox_dkv", grid=(nseq, NH, nq), in_specs=[full, kblk, vblk, kblk, full, full, full],
        out_specs=[kblk, kblk, kblk],
        out_shape=[jax.ShapeDtypeStruct(wide, F32), jax.ShapeDtypeStruct(wide, BF16), jax.ShapeDtypeStruct(wide, F32)],
        compiler_params=_params(3))(qn, kn, proj, cb, do, lse, delta)


def loss_head(out, tgt, rows, width):
    tb = ROW_TILE
    blk = pl.BlockSpec((tb, width), lambda i: (i, 0))
    accb = pl.BlockSpec((8, LANES), lambda i: (0, 0))

    def body(o_ref, t_ref, d32_ref, d16_ref, acc_ref):
        d = o_ref[...] - t_ref[...]
        row_loss = 0.5 * jnp.mean(d * d, axis=1, keepdims=True)
        g = d * (1.0 / width)
        d32_ref[...] = g
        d16_ref[...] = g.astype(BF16)
        part = jnp.where(_iota((tb, LANES), 1) == 0, row_loss, 0.0).reshape(tb // 8, 8, LANES).sum(axis=0)

        @pl.when(pl.program_id(0) == 0)
        def _():
            acc_ref[...] = part

        @pl.when(pl.program_id(0) != 0)
        def _():
            acc_ref[...] += part

    return pl.pallas_call(
        body, name="loss_head", grid=(rows // tb,), in_specs=[blk, blk], out_specs=[blk, blk, accb],
        out_shape=[jax.ShapeDtypeStruct((rows, width), F32), jax.ShapeDtypeStruct((rows, width), BF16),
                   jax.ShapeDtypeStruct((8, LANES), F32)], compiler_params=_params(1))(out, tgt)


def _adamw_update(w, g, m, v):
    m_new = ADAM_B1 * m + (1.0 - ADAM_B1) * g
    v_new = ADAM_B2 * v + (1.0 - ADAM_B2) * (g * g)
    m_hat = m_new / (1.0 - ADAM_B1 ** ADAM_STEP)
    v_hat = v_new / (1.0 - ADAM_B2 ** ADAM_STEP)
    return -ADAM_LR * (m_hat / (jnp.sqrt(v_hat) + ADAM_EPS) + ADAM_WD * w), m_new, v_new


def adamw(name, w, g, m, v):
    rows, cols = w.shape
    tb = min(rows, 128)
    assert rows % tb == 0
    blk = pl.BlockSpec((tb, cols), lambda i: (i, 0))

    def body(w_ref, g_ref, m_ref, v_ref, d_ref, mo_ref, vo_ref):
        d_ref[...], mo_ref[...], vo_ref[...] = _adamw_update(w_ref[...], g_ref[...], m_ref[...], v_ref[...])

    shp = jax.ShapeDtypeStruct(w.shape, F32)
    return pl.pallas_call(body, name=name, grid=(rows // tb,), in_specs=[blk] * 4, out_specs=[blk] * 3,
                          out_shape=[shp] * 3, compiler_params=_params(1))(w, g, m, v)


SPLIT_TILE = 128


def _tiled(shape2d, ax, n_lead, index):
    blk = (SPLIT_TILE, shape2d[1]) if ax == 0 else (shape2d[0], SPLIT_TILE)

    def index_map(*args):
        *lead, t = index(*args)
        return (*lead, t, 0) if ax == 0 else (*lead, 0, t)

    return pl.BlockSpec((None,) * n_lead + blk, index_map)


def adamw_halves(name, w, mine, other, m, v, c, ax):
    steps = w.shape[ax] // 2 // SPLIT_TILE
    assert w.shape[ax] == 2 * steps * SPLIT_TILE

    def body(c_ref, w_ref, mine_ref, other_ref, m_ref, v_ref, g_ref, d_ref, mo_ref, vo_ref):
        g = jnp.where(pl.program_id(0) // steps == c_ref[0], mine_ref[...], other_ref[...])
        g_ref[...] = g
        d_ref[...], mo_ref[...], vo_ref[...] = _adamw_update(w_ref[...], g, m_ref[...], v_ref[...])

    blk = _tiled(w.shape, ax, 0, lambda i, c_ref: (i,))
    hblk = _tiled(mine.shape, ax, 0, lambda i, c_ref: (i % steps,))
    grid_spec = pltpu.PrefetchScalarGridSpec(num_scalar_prefetch=1, grid=(2 * steps,),
                                             in_specs=[blk, hblk, hblk, blk, blk], out_specs=[blk] * 4)
    shp = jax.ShapeDtypeStruct(w.shape, F32)
    return pl.pallas_call(body, name=name, grid_spec=grid_spec, out_shape=[shp] * 4,
                          compiler_params=_params(1))(c, w, mine, other, m, v)


def add_chips(name, slots, parts, chip, axes):
    outs = []
    for idx, (x, own, ax) in enumerate(zip(slots, parts, axes)):
        n, shape2d = x.shape[0], x.shape[1:]
        steps = shape2d[ax] // SPLIT_TILE
        assert shape2d[ax] == steps * SPLIT_TILE

        def body(me_ref, *refs, n=n):
            o_ref = refs[n + 1]
            acc = None
            for t in range(n):
                term = jnp.where(me_ref[0] == t, refs[n][...], refs[t][...]).astype(F32)
                acc = term if acc is None else acc + term
            o_ref[...] = acc

        def filled(t, n=n):
            return lambda i, me_ref: (jnp.where(me_ref[0] == t, (t + 1) % n, t), i)

        grid_spec = pltpu.PrefetchScalarGridSpec(
            num_scalar_prefetch=1, grid=(steps,),
            in_specs=[_tiled(shape2d, ax, 1, filled(t)) for t in range(n)]
            + [_tiled(shape2d, ax, 1, lambda i, me_ref: (me_ref[0], i))],
            out_specs=_tiled(shape2d, ax, 0, lambda i, me_ref: (i,)))
        outs.append(pl.pallas_call(
            body, name=f"{name}_{idx}", grid_spec=grid_spec, out_shape=jax.ShapeDtypeStruct(shape2d, F32),
            compiler_params=_params(1))(chip, *([x] * n), own))
    return outs


def add_pair(name, gs, rs, c, axes):
    outs = []
    for idx, (g, r, ax) in enumerate(zip(gs, rs, axes)):
        nb = r.shape[0]
        steps = r.shape[1 + ax] // SPLIT_TILE
        assert r.shape[1 + ax] == steps * SPLIT_TILE

        def body(c_ref, g_ref, r_ref, o_ref):
            o_ref[...] = (g_ref[...] + r_ref[...]).astype(BF16)

        grid_spec = pltpu.PrefetchScalarGridSpec(
            num_scalar_prefetch=1, grid=(nb, steps),
            in_specs=[_tiled(g.shape[1:], ax, 1, lambda b, i, c_ref: (b, c_ref[0] * steps + i)),
                      _tiled(r.shape[1:], ax, 1, lambda b, i, c_ref: (b, i))],
            out_specs=_tiled(r.shape[1:], ax, 1, lambda b, i, c_ref: (b, i)))
        outs.append(pl.pallas_call(
            body, name=f"{name}_{idx}", grid_spec=grid_spec, out_shape=jax.ShapeDtypeStruct(r.shape, BF16),
            compiler_params=_params(2))(c, g, r))
    return outs


def _place():
    x, y, c = lax.axis_index("x"), lax.axis_index("y"), lax.axis_index("c")
    return x, y, c, [(1 - x, y), (x, 1 - y), (1 - x, 1 - y)]


def _remote(src, dst, send_sem, recv_sem, dev):
    return pltpu.make_async_remote_copy(src_ref=src, dst_ref=dst, send_sem=send_sem, recv_sem=recv_sem,
                                        device_id=dev, device_id_type=MESH)


def _half(ref, lead, ax, which):
    size = ref.shape[len(lead) + ax] // 2
    part = pl.ds(which * size, size)
    return ref.at[(*lead, part, slice(None)) if ax == 0 else (*lead, slice(None), part)]


def gather_weights(shards, axes):
    n = len(shards)

    def body(*refs):
        ins, outs = refs[:n], refs[n:2 * n]
        ici_s, ici_r, d2d_s, d2d_r = refs[2 * n:]
        x, y, c, chips = _place()
        me = 2 * x + y
        sends, passes = [], []
        for w in range(n):
            for j, (ox, oy) in enumerate(chips):
                cp = _remote(_half(ins[w], (), axes[w], c), _half(outs[w], (me,), axes[w], c),
                             ici_s.at[3 * w + j], ici_r.at[3 * w + j], (ox, oy, c))
                cp.start()
                sends.append(cp)
        for w in range(n):
            for j, (ox, oy) in enumerate(chips):
                landed = _half(outs[w], (2 * ox + oy,), axes[w], c)
                _remote(landed, landed, ici_s.at[3 * w + j], ici_r.at[3 * w + j], (ox, oy, c)).wait_recv()
                cp = _remote(landed, landed, d2d_s.at[3 * w + j], d2d_r.at[3 * w + j], (x, y, 1 - c))
                cp.start()
                passes.append(cp)
        for w in range(n):
            for j, (ox, oy) in enumerate(chips):
                other = _half(outs[w], (2 * ox + oy,), axes[w], 1 - c)
                _remote(other, other, d2d_s.at[3 * w + j], d2d_r.at[3 * w + j], (x, y, 1 - c)).wait_recv()
        for cp in sends + passes:
            cp.wait_send()

    return pl.pallas_call(
        body, name="gather_weights", in_specs=[ANY] * n, out_specs=[ANY] * n,
        out_shape=[jax.ShapeDtypeStruct((4,) + s.shape, s.dtype) for s in shards],
        scratch_shapes=[pltpu.SemaphoreType.DMA((3 * n,))] * 4,
    )(*shards)


HBM = pl.BlockSpec(memory_space=pltpu.HBM)
SEM = pl.BlockSpec(memory_space=pltpu.SEMAPHORE)
DATAFLOW = pltpu.SideEffectType.DATAFLOW_SIDE_EFFECTING


def _hbm(a):
    return pltpu.with_memory_space_constraint(a, pltpu.HBM)


class SplitExchange:
    def __init__(self, name, srcs, zone_shapes, n_sems, plan):
        self.name, self.n, self.n_sems, self.plan = name, len(srcs), n_sems, plan
        self.srcs = [_hbm(s) for s in srcs]
        self.zones = [_hbm(lax.empty(shape, s.dtype)) for shape, s in zip(zone_shapes, srcs)]

    def start(self, after):
        n, n_after = self.n, len(after)

        def body(*refs):
            ins, lands = refs[:n], refs[n:2 * n]
            send, recv, token = refs[2 * n + n_after], refs[2 * n + n_after + 1], refs[-1]
            for src, dst, si, ri, dev in self.plan(ins, lands)[0]:
                _remote(src, dst, send.at[si], recv.at[ri], dev).start()
            token[...] = jnp.zeros_like(token)

        res = pl.pallas_call(
            body, name=f"{self.name}_start", in_specs=[HBM] * (2 * n) + [ANY] * n_after,
            out_specs=[SEM, SEM] + [HBM] * (2 * n) + [pl.BlockSpec(memory_space=pltpu.VMEM)],
            out_shape=[pltpu.SemaphoreType.DMA((self.n_sems,)), pltpu.SemaphoreType.DMA((self.n_sems,))]
            + [pltpu.HBM(a.shape, a.dtype) for a in self.srcs + self.zones] + [jax.ShapeDtypeStruct((8, LANES), F32)],
            input_output_aliases={i: 2 + i for i in range(2 * n)},
            compiler_params=pltpu.CompilerParams(has_side_effects=DATAFLOW),
        )(*self.srcs, *self.zones, *after)
        self.sems, self.srcs, self.zones = res[:2], list(res[2:2 + n]), list(res[2 + n:2 + 2 * n])
        return res[-1][0, 0]

    def wait(self, after):
        n = self.n

        def body(*refs):
            ins, lands = refs[:n], refs[n:2 * n]
            send, recv = refs[2 * n], refs[2 * n + 1]
            sends, arrivals = self.plan(ins, lands)
            for src, _, si, _, dev in sends:
                _remote(src, src, send.at[si], recv.at[si], dev).wait_send()
            for landed, ri in arrivals:
                _remote(landed, landed, send.at[ri], recv.at[ri], _place()[:3]).wait_recv()

        res = pl.pallas_call(
            body, name=f"{self.name}_wait", in_specs=[HBM] * (2 * n) + [SEM, SEM, ANY], out_specs=[HBM] * (2 * n),
            out_shape=[pltpu.HBM(a.shape, a.dtype) for a in self.srcs + self.zones],
            input_output_aliases={i: i for i in range(2 * n)},
            compiler_params=pltpu.CompilerParams(has_side_effects=DATAFLOW),
        )(*self.srcs, *self.zones, *self.sems, after)
        self.srcs = list(res[:n])
        return list(res[n:])


def split_gather(shards):
    def plan(ins, lands):
        x, y, c, chips = _place()
        sends, arrivals = [], []
        for w in range(len(ins)):
            for j, (ox, oy) in enumerate(chips):
                for k in range(2):
                    base = 2 * (3 * w + j)
                    sends.append((_half(ins[w], (), 0, c), _half(lands[w], (2 * x + y,), 0, c), base + k, base + c, (ox, oy, k)))
                    arrivals.append((_half(lands[w], (2 * ox + oy,), 0, k), base + k))
        return sends, arrivals

    return SplitExchange("gather", shards, [(4,) + s.shape for s in shards], 6 * len(shards), plan)


def split_pair_swap(grads, axes):
    def plan(ins, lands):
        x, y, c, _ = _place()
        sends = [(_half(ins[w], (slice(None),), axes[w], 1 - c), lands[w], w, w, (x, y, 1 - c)) for w in range(len(ins))]
        return sends, [(lands[w], w) for w in range(len(ins))]

    halved = [tuple(d // 2 if i == 1 + ax else d for i, d in enumerate(g.shape)) for g, ax in zip(grads, axes)]
    return SplitExchange("pair_swap", grads, halved, len(grads), plan)


def split_chip_exchange(parts):
    def plan(ins, lands):
        x, y, c, chips = _place()
        sends, arrivals = [], []
        for w in range(len(ins)):
            for j, (ox, oy) in enumerate(chips):
                sends.append((ins[w].at[2 * ox + oy], lands[w].at[2 * x + y], 3 * w + j, 3 * w + j, (ox, oy, c)))
                arrivals.append((lands[w].at[2 * ox + oy], 3 * w + j))
        return sends, arrivals

    return SplitExchange("chip_exchange", parts, [p.shape for p in parts], 3 * len(parts), plan)


def split_pair_send(halves):
    def plan(ins, lands):
        x, y, c, _ = _place()
        return ([(ins[w], lands[w], w, w, (x, y, 1 - c)) for w in range(len(ins))],
                [(lands[w], w) for w in range(len(ins))])

    return SplitExchange("pair_send", halves, [h.shape for h in halves], len(halves), plan)


def pair_swap(grads, axes):
    n = len(grads)

    def body(*refs):
        ins, outs = refs[:n], refs[n:2 * n]
        send, recv = refs[2 * n:]
        x, y, c, _ = _place()
        cps = []
        for w in range(n):
            cp = _remote(_half(ins[w], (slice(None),), axes[w], 1 - c), outs[w], send.at[w], recv.at[w], (x, y, 1 - c))
            cp.start()
            cps.append(cp)
        for cp in cps:
            cp.wait_recv()
        for cp in cps:
            cp.wait_send()

    def halved(g, ax):
        return tuple(d // 2 if i == 1 + ax else d for i, d in enumerate(g.shape))

    return pl.pallas_call(
        body, name="pair_swap", in_specs=[ANY] * n, out_specs=[ANY] * n,
        out_shape=[jax.ShapeDtypeStruct(halved(g, ax), g.dtype) for g, ax in zip(grads, axes)],
        scratch_shapes=[pltpu.SemaphoreType.DMA((n,))] * 2,
    )(*grads)


def chip_exchange(parts):
    n = len(parts)

    def body(*refs):
        ins, outs = refs[:n], refs[n:2 * n]
        send, recv = refs[2 * n:]
        x, y, c, chips = _place()
        me = 2 * x + y
        cps = []
        for w in range(n):
            for j, (ox, oy) in enumerate(chips):
                cp = _remote(ins[w].at[2 * ox + oy], outs[w].at[me], send.at[3 * w + j], recv.at[3 * w + j], (ox, oy, c))
                cp.start()
                cps.append(cp)
        for w in range(n):
            for j, (ox, oy) in enumerate(chips):
                slot = outs[w].at[2 * ox + oy]
                _remote(slot, slot, send.at[3 * w + j], recv.at[3 * w + j], (ox, oy, c)).wait_recv()
        for cp in cps:
            cp.wait_send()

    return pl.pallas_call(
        body, name="chip_exchange", in_specs=[ANY] * n, out_specs=[ANY] * n,
        out_shape=[jax.ShapeDtypeStruct(p.shape, p.dtype) for p in parts],
        scratch_shapes=[pltpu.SemaphoreType.DMA((3 * n,))] * 2,
    )(*parts)


def pair_send(halves):
    n = len(halves)

    def body(*refs):
        ins, outs = refs[:n], refs[n:2 * n]
        send, recv = refs[2 * n:]
        x, y, c, _ = _place()
        cps = [_remote(ins[w], outs[w], send.at[w], recv.at[w], (x, y, 1 - c)) for w in range(n)]
        for cp in cps:
            cp.start()
        for cp in cps:
            cp.wait_recv()
        for cp in cps:
            cp.wait_send()

    return pl.pallas_call(
        body, name="pair_send", in_specs=[ANY] * n, out_specs=[ANY] * n,
        out_shape=[jax.ShapeDtypeStruct(h.shape, h.dtype) for h in halves],
        scratch_shapes=[pltpu.SemaphoreType.DMA((n,))] * 2,
    )(*halves)


def all_reduce_small(name, vec):
    rows = vec.shape[0]

    def body(v_ref, o_ref, buf, send, recv):
        x, y, c, _ = _place()
        me = 4 * x + 2 * y + c
        buf[me] = v_ref[...]
        cps = []
        for k in range(1, 8):
            kx, ky, kc = (k >> 2) & 1, (k >> 1) & 1, k & 1
            peer = (x if kx == 0 else 1 - x, y if ky == 0 else 1 - y, c if kc == 0 else 1 - c)
            cp = _remote(v_ref, buf.at[me], send.at[k - 1], recv.at[k - 1], peer)
            cp.start()
            cps.append(cp)
        for k in range(1, 8):
            kx, ky, kc = (k >> 2) & 1, (k >> 1) & 1, k & 1
            px, py, pc = (x if kx == 0 else 1 - x, y if ky == 0 else 1 - y, c if kc == 0 else 1 - c)
            slot = buf.at[4 * px + 2 * py + pc]
            _remote(slot, slot, send.at[k - 1], recv.at[k - 1], (px, py, pc)).wait_recv()
        for cp in cps:
            cp.wait_send()
        acc = buf[0]
        for d in range(1, 8):
            acc = acc + buf[d]
        o_ref[...] = acc

    vm = pl.BlockSpec(memory_space=pltpu.VMEM)
    return pl.pallas_call(
        body, name=name, in_specs=[vm], out_specs=vm, out_shape=jax.ShapeDtypeStruct(vec.shape, F32),
        scratch_shapes=[pltpu.VMEM((8, rows, LANES), F32), pltpu.SemaphoreType.DMA((7,)), pltpu.SemaphoreType.DMA((7,))],
    )(vec)


class NoExchange:
    def __init__(self, late):
        self.late = late

    def late_weights(self, after):
        return self.late

    def reduce_start(self, grads):
        return 0.0

    def reduce_exchange(self, after):
        return 0.0

    def reduce_finish(self, after):
        return 0.0


def local_step(x2, tgt2, g1, g2, gdn_ng, qn_g, kn_g, p1, p2, conv_w, wt_main, wt_small, hooks, nseq, seq):
    rows, dm = x2.shape
    wide = NH * LANES
    row = lambda a, off=0, w=None: (a, "row", off, a.shape[1] if w is None else w)
    rowh = lambda a, off=0, w=LANES: (a, "rowh", off, w)
    par = lambda a: (a, "par", 0, a.shape[1])
    parh = lambda a, off=0: (a, "parh", off, LANES)
    o_row = lambda w, dt: (w, "row", w, dt)
    o_rowh = lambda dt, tw=wide, w=LANES: (tw, "rowh", w, dt)

    u, = ew_fwd("rms1", f_rms, [row(x2), par(g1)], [o_row(dm, BF16)], rows)
    proj = matmul("mm_in", u, wt_main, "nt", F32)
    sp = matmul("mm_in_small", u, wt_small, "nt", F32)
    so, = ew_fwd("small", f_small, [row(sp), par(p1), par(p2)], [o_row(LANES, F32)], rows)
    cs = cumsum_time("cumsum", so, nseq, seq, False)
    gb, bb, cb = ew_fwd("bcast", f_bcast, [row(so), row(cs)], [o_rowh(F32)] * 3, rows, NH)
    ct = transpose_time("c_time_major", cs, nseq, seq)
    conv = {}
    for mode, off in (("q", 0), ("k", NH), ("v", 2 * NH)):
        conv[mode], = ew_fwd(f"conv_{mode}", make_f_conv(mode), [rowh(proj, off), parh(conv_w, off)], [o_rowh(F32)],
                             rows, NH, seq, "hi", CONV_HEADS)
    val, kcum, attn, qdec, kdec, t_inv = gdn_a_fwd(conv["q"], conv["k"], conv["v"], gb, bb, rows)
    o_a, snaps = gdn_b_fwd(val, kcum, attn, qdec, kdec, gb, nseq, seq)
    ya_in, = ew_fwd("gdn_post", f_post, [rowh(o_a), rowh(proj, 3 * NH), par(gdn_ng)], [o_rowh(BF16)], rows, NH)
    fqn, = ew_fwd("fox_qn", f_rms, [rowh(proj, FOX_Q), par(qn_g)], [o_rowh(BF16)], rows, NH)
    fkn, = ew_fwd("fox_kn", f_rms, [rowh(proj, FOX_K), par(kn_g)], [o_rowh(BF16)], rows, NH)
    o_b, o_b16, lse = fox_fwd(fqn, fkn, proj, ct, nseq, seq)
    p_a, p_b, w_o, w_u, w_d = hooks.late_weights(o_a)
    y_a = matmul("mm_pa", ya_in, p_a, "nn", F32, tn=1024)
    y_b = matmul("mm_pb", o_b16, p_b, "nn", F32, tn=1024)
    gates = [row(proj, 7, dm), row(proj, 8, dm)]
    merged, = ew_fwd("merge", f_merge, gates + [row(y_a), row(y_b)], [o_row(dm, BF16)], rows)
    hres = matmul("mm_out", merged, w_o, "nn", F32, add=x2, tn=1024)
    hn, = ew_fwd("rms2", f_rms, [row(hres), par(g2)], [o_row(dm, BF16)], rows)
    up_blocks = w_u.shape[0]
    act, relu2 = matmul("mm_up", hn, w_u, "nn", F32, col_blocks=up_blocks, out_dtypes=[F32, BF16],
                        epilogue=lambda r: [r, jnp.maximum(r, 0.0) * jnp.maximum(r, 0.0)])
    out = matmul("mm_down", relu2, w_d, "nn", F32, add=hres, tn=1024)
    dout, dout16, loss_acc = loss_head(out, tgt2, rows, dm)

    d_act = matmul("mm_d_act", dout16, w_d, "nt", BF16, extras=[act], epilogue=lambda r, a: [2.0 * jnp.maximum(a, 0.0) * r])
    dw_d = matmul("mm_dw_down", relu2, dout16, "tn", F32, tn=1024)
    dw_u = matmul("mm_dw_up", hn, d_act, "tn", F32, col_blocks=up_blocks)
    d_hn = matmul("mm_d_hn", d_act, w_u, "nt", F32, col_blocks=up_blocks)
    dh, dh16, dg2 = ew_bwd("rms2_b", f_rms, [row(hres), par(g2)], [(row(d_hn),)], [row(dout)],
                           lambda g, e: [g[0] + e[0], g[0] + e[0], g[1]],
                           [((rows, dm), "row", dm, F32, None), ((rows, dm), "row", dm, BF16, None), ((1, dm), "par", dm, F32, "all")], rows)
    d_merged = matmul("mm_d_merged", dh16, w_o, "nt", F32, tn=1024)
    dw_o = matmul("mm_dw_out", merged, dh16, "tn", F32, tn=1024)
    seg16 = ((rows, dm), "row", dm, BF16, None)
    d_ga16, d_gb16, d_ya16, d_yb16 = ew_bwd("merge_b", f_merge, gates + [row(y_a), row(y_b)], [(row(d_merged),)], [],
                                            lambda g, e: list(g), [seg16] * 4, rows)
    dp_a = matmul("mm_dp_a", ya_in, d_ya16, "tn", F32, tn=1024)
    d_ya_in = matmul("mm_d_ya_in", d_ya16, p_a, "nt", F32, tn=1024)
    dp_b = matmul("mm_dp_b", o_b16, d_yb16, "tn", F32, tn=1024)
    d_ob = matmul("mm_d_ob", d_yb16, p_b, "nt", F32, tn=1024)
    token = hooks.reduce_start(dict(p_a=dp_a, p_b=dp_b, w_o=dw_o, w_u=dw_u, w_d=dw_d))
    gdn_ng_t = gdn_ng + token
    h32 = ((rows, wide), "rowh", LANES, F32, None)
    h16 = ((rows, wide), "rowh", LANES, BF16, None)
    gain = ((1, LANES), "par", LANES, F32, "all")
    d_oa, d_z16, d_gdn_ng = ew_bwd("gdn_post_b", f_post, [rowh(o_a), rowh(proj, 3 * NH), par(gdn_ng_t)], [(rowh(d_ya_in),)], [],
                                   lambda g, e: list(g), [h32, h16, gain], rows, NH)
    dval, dkc, dat, dqd, dkd, dgb_b = gdn_b_bwd(val, kcum, attn, qdec, kdec, gb, snaps, d_oa, nseq, seq)
    d_cq, d_ck, d_cv, d_gb, d_bb = gdn_a_bwd(conv["q"], conv["k"], conv["v"], gb, bb, t_inv, dval, dkc, dat, dqd, dkd, dgb_b, rows)
    conv_w_t = conv_w + hooks.reduce_exchange(d_cq)
    d_pre, d_conv = {}, {}
    tap = ((4, wide), "parh", LANES, F32, "inner")
    for mode, off, ctg in (("q", 0, d_cq), ("k", NH, d_ck), ("v", 2 * NH, d_cv)):
        d_pre[mode], d_conv[mode] = ew_bwd(f"conv_{mode}_b", make_f_conv(mode), [rowh(proj, off), parh(conv_w_t, off)],
                                           [(rowh(ctg),)], [], lambda g, e: list(g), [h16, tap], rows, NH, seq, "hi", CONV_HEADS)
    delta, = ew_fwd("fox_delta", f_delta, [rowh(d_ob), rowh(o_b)], [o_rowh(F32)], rows, NH)
    d_fqn, d_cq_b = fox_dq(fqn, fkn, proj, ct, d_ob, lse, delta, nseq, seq)
    d_fkn, d_fv16, d_ck_b = fox_dkv(fqn, fkn, proj, cb, d_ob, lse, delta, nseq, seq)
    qn_g_t = qn_g + hooks.reduce_finish(d_fkn)
    d_fq16, d_qn_g = ew_bwd("fox_qn_b", f_rms, [rowh(proj, FOX_Q), par(qn_g_t)], [(rowh(d_fqn),)], [], lambda g, e: list(g),
                            [h16, gain], rows, NH)
    d_fk16, d_kn_g = ew_bwd("fox_kn_b", f_rms, [rowh(proj, FOX_K), par(kn_g)], [(rowh(d_fkn),)], [], lambda g, e: list(g),
                            [h16, gain], rows, NH)
    narrow = ((rows, LANES), "row", LANES, F32, None)
    d_so, d_cs = ew_bwd("bcast_b", f_bcast, [row(so), row(cs)], [(rowh(d_gb),), (rowh(d_bb),), (rowh(d_cq_b), rowh(d_ck_b))], [],
                        lambda g, e: list(g), [narrow, narrow], rows, NH)
    d_logf = cumsum_time("cumsum_b", d_cs, nseq, seq, True)
    vec = ((1, LANES), "par", LANES, F32, "all")
    d_sp16, d_p1, d_p2 = ew_bwd("small_b", f_small, [row(sp), par(p1), par(p2)], [(row(d_so), row(d_logf))], [],
                                lambda g, e: list(g), [((rows, LANES), "row", LANES, BF16, None), vec, vec], rows)
    d_proj16 = jnp.concatenate([d_pre["q"], d_pre["k"], d_pre["v"], d_z16, d_fq16, d_fk16, d_fv16, d_ga16, d_gb16], axis=1)
    dw_main = matmul("mm_dw_main", d_proj16, u, "tn", F32)
    dw_small = matmul("mm_dw_small", d_sp16, u, "tn", F32)
    d_u = matmul("mm_d_u_small", d_sp16, wt_small, "nn", F32)
    d_u = matmul("mm_d_u", d_proj16, wt_main, "nn", F32, add=d_u)
    dx, dg1 = ew_bwd("rms1_b", f_rms, [row(x2), par(g1)], [(row(d_u),)], [row(dh)], lambda g, e: [g[0] + e[0], g[1]],
                     [((rows, dm), "row", dm, F32, None), ((1, dm), "par", dm, F32, "all")], rows)
    d_conv_w = jnp.concatenate([d_conv["q"], d_conv["k"], d_conv["v"]], axis=1)
    return dict(loss_acc=loss_acc, dx=dx, g1=dg1, g2=dg2, gdn_ng=d_gdn_ng, qn=d_qn_g, kn=d_kn_g, p1=d_p1, p2=d_p2,
                conv=d_conv_w, w_main=dw_main, w_small=dw_small, p_a=dp_a, p_b=dp_b, w_o=dw_o, w_u=dw_u, w_d=dw_d)


_W = NH * LANES
_A0, _A1 = 4 * _W, 4 * _W + 2 * NH
_B0, _B1 = _A1 + 3 * _W, _A1 + 3 * _W + NH
N_IN = _B1 + 2 * _W


def _split_w_in(full_t):
    main = jnp.concatenate([full_t[:_A0], full_t[_A1:_B0], full_t[_B1:]], axis=0)
    small = jnp.concatenate([full_t[_A0:_A1], full_t[_B0:_B1], jnp.zeros((LANES - 3 * NH, full_t.shape[1]), full_t.dtype)], axis=0)
    return main, small


def _join_w_in(main, small):
    return jnp.concatenate([main[:_A0], small[:2 * NH], main[_A0:_A0 + 3 * _W], small[2 * NH:3 * NH], main[_A0 + 3 * _W:]], axis=0)


def _lanes(v, at=0):
    return jnp.pad(v.reshape(1, -1), ((0, 0), (at, LANES - at - v.size)))


def kernel(x, norm_mix_g, w_in, gdn_conv_w, gdn_a_log, gdn_dt_bias, gdn_norm_g, fox_q_norm_g, fox_k_norm_g, fox_f_bias, w_proj_gdn, w_proj_fox, w_out, norm_mlp_g, w_up, w_down, loss_target, m_norm_mix_g, m_w_in, m_gdn_conv_w, m_gdn_a_log, m_gdn_dt_bias, m_gdn_norm_g, m_fox_q_norm_g, m_fox_k_norm_g, m_fox_f_bias, m_w_proj_gdn, m_w_proj_fox, m_w_out, m_norm_mlp_g, m_w_up, m_w_down, v_norm_mix_g, v_w_in, v_gdn_conv_w, v_gdn_a_log, v_gdn_dt_bias, v_gdn_norm_g, v_fox_q_norm_g, v_fox_k_norm_g, v_fox_f_bias, v_w_proj_gdn, v_w_proj_fox, v_w_out, v_norm_mlp_g, v_w_up, v_w_down):
    nseq, seq, dm = x.shape
    rows = nseq * seq
    xi, yi, ci = lax.axis_index("x"), lax.axis_index("y"), lax.axis_index("c")
    chip = 2 * xi + yi
    conv_cols = gdn_conv_w.shape[2]

    tr = lambda a: jnp.swapaxes(a[0], 0, 1)
    big = [tr(w_in), w_proj_gdn[0], w_proj_fox[0], w_out[0], w_up[0], w_down[0]]
    axes = [1, 0, 0, 0, 0, 0]
    big16 = [w.astype(BF16) for w in big]
    fill = lambda got, own: lax.dynamic_update_index_in_dim(got, own, chip, 0)
    conv_slot = jnp.zeros((4, 4, conv_cols), F32).at[:, chip].set(jnp.where(ci == 0, gdn_conv_w[0], 0.0))
    conv_full = all_reduce_small("gather_conv", conv_slot.reshape(-1, LANES)).reshape(4, 4 * conv_cols)
    got_in, = gather_weights(big16[:1], axes[:1])
    wt_main, wt_small = _split_w_in(fill(got_in, big16[0]).reshape(-1, dm))
    core, chip_no = ci.reshape(1).astype(jnp.int32), chip.reshape(1).astype(jnp.int32)
    gather = split_gather(big16[1:])
    token = gather.start([got_in, conv_full])

    class Hooks:
        def late_weights(self, after):
            got = gather.wait(after)
            g_pa, g_pb, g_wo, w_u, g_wd = (fill(g, own) for g, own in zip(got, gather.srcs))
            return (*(g.reshape(-1, dm) for g in (g_pa, g_pb, g_wo)), w_u, g_wd.reshape(-1, dm))

        def reduce_start(self, grads):
            blocks = [grads["p_a"].reshape(4, -1, dm), grads["p_b"].reshape(4, -1, dm), grads["w_o"].reshape(4, -1, dm),
                      grads["w_u"], grads["w_d"].reshape(4, -1, dm)]
            self.swap = split_pair_swap(blocks, axes[1:])
            return self.swap.start([])

        def reduce_exchange(self, after):
            swapped = self.swap.wait(after)
            self.exchange = split_chip_exchange(add_pair("add_pair_late", self.swap.srcs, swapped, core, axes[1:]))
            return self.exchange.start([])

        def reduce_finish(self, after):
            slots = self.exchange.wait(after)
            self.send = split_pair_send(add_chips("add_chips_late", slots, self.exchange.srcs, chip_no, axes[1:]))
            return self.send.start([])

    hooks = Hooks()
    p1 = _lanes(gdn_dt_bias[0]) + _lanes(fox_f_bias[0], 2 * NH)
    p2 = _lanes(gdn_a_log[0])

    g = local_step(x.reshape(rows, dm), loss_target.reshape(rows, dm), norm_mix_g + token, norm_mlp_g, gdn_norm_g,
                   fox_q_norm_g, fox_k_norm_g, p1, p2, conv_full, wt_main, wt_small, hooks, nseq, seq)

    small_parts = [g["loss_acc"], g["g1"].reshape(8, LANES), g["g2"].reshape(8, LANES), g["gdn_ng"], g["qn"], g["kn"], g["p1"], g["p2"],
                   g["conv"].reshape(-1, LANES)]
    tiled = [jnp.pad(p, ((0, -p.shape[0] % 8), (0, 0))) for p in small_parts]
    red = all_reduce_small("reduce_small", jnp.concatenate(tiled, axis=0))
    pos, red_parts = 0, []
    for p, t in zip(small_parts, tiled):
        red_parts.append(red[pos:pos + p.shape[0]])
        pos += t.shape[0]
    r_loss, r_g1, r_g2, r_gdn_ng, r_qn, r_kn, r_p1, r_p2, r_conv = red_parts
    loss = jnp.sum(r_loss)
    g_conv = lax.dynamic_slice_in_dim(r_conv.reshape(4, 4, conv_cols), chip, 1, axis=1).reshape(4, conv_cols)
    small_grads = [r_g1.reshape(1, dm), r_p2[:, :NH], r_p1[:, :NH], r_gdn_ng, r_qn, r_kn, r_p1[:, 2 * NH:3 * NH], r_g2.reshape(1, dm)]
    small_w = [norm_mix_g, gdn_a_log, gdn_dt_bias, gdn_norm_g, fox_q_norm_g, fox_k_norm_g, fox_f_bias, norm_mlp_g]
    small_m = [m_norm_mix_g, m_gdn_a_log, m_gdn_dt_bias, m_gdn_norm_g, m_fox_q_norm_g, m_fox_k_norm_g, m_fox_f_bias, m_norm_mlp_g]
    small_v = [v_norm_mix_g, v_gdn_a_log, v_gdn_dt_bias, v_gdn_norm_g, v_fox_q_norm_g, v_fox_k_norm_g, v_fox_f_bias, v_norm_mlp_g]

    def pack(parts):
        flat = jnp.concatenate([jnp.pad(p.reshape(-1), (0, -p.size % LANES)) for p in parts])
        return jnp.pad(flat, (0, -flat.size % (8 * LANES))).reshape(-1, LANES)

    packed = adamw("adamw_small", pack(small_w + [gdn_conv_w[0]]), pack(small_grads + [g_conv]),
                   pack(small_m + [m_gdn_conv_w[0]]), pack(small_v + [v_gdn_conv_w[0]]))

    def unpack(flat2d):
        flat, pos, res = flat2d.reshape(-1), 0, []
        for p in small_w + [gdn_conv_w[0]]:
            res.append(flat[pos:pos + p.size].reshape(p.shape))
            pos += p.size + (-p.size % LANES)
        return res

    s_delta, s_m, s_v = (unpack(a) for a in packed)

    blocks = [_join_w_in(g["w_main"], g["w_small"]).reshape(4, -1, dm)]
    swapped = pair_swap(blocks, axes[:1])
    chip_part = add_pair("add_pair", blocks, swapped, core, axes[:1])
    slots = chip_exchange(chip_part)
    halves = add_chips("add_chips", slots, chip_part, chip_no, axes[:1])
    others = list(pair_send(halves)) + hooks.send.wait(g["dx"])
    halves = list(halves) + hooks.send.srcs
    big_m = [tr(m_w_in), m_w_proj_gdn[0], m_w_proj_fox[0], m_w_out[0], m_w_up[0], m_w_down[0]]
    big_v = [tr(v_w_in), v_w_proj_gdn[0], v_w_proj_fox[0], v_w_out[0], v_w_up[0], v_w_down[0]]
    names = ["w_in", "w_proj_gdn", "w_proj_fox", "w_out", "w_up", "w_down"]
    big_res, big_grad = {}, {}
    for nm, w, mine, other, m, v, ax in zip(names, big, halves, others, big_m, big_v, axes):
        res = adamw_halves(f"adamw_{nm}", w, mine, other, m, v, core, ax)
        if nm == "w_in":
            res = [jnp.swapaxes(r, 0, 1) for r in res]
        big_grad[nm], *big_res[nm] = res

    order = ["norm_mix_g", "w_in", "gdn_conv_w", "gdn_a_log", "gdn_dt_bias", "gdn_norm_g", "fox_q_norm_g", "fox_k_norm_g",
             "fox_f_bias", "w_proj_gdn", "w_proj_fox", "w_out", "norm_mlp_g", "w_up", "w_down"]
    small_names = ["norm_mix_g", "gdn_a_log", "gdn_dt_bias", "gdn_norm_g", "fox_q_norm_g", "fox_k_norm_g", "fox_f_bias", "norm_mlp_g",
                   "gdn_conv_w"]
    small_idx = {nm: i for i, nm in enumerate(small_names)}
    shapes = dict(zip(order, (a.shape for a in (norm_mix_g, w_in, gdn_conv_w, gdn_a_log, gdn_dt_bias, gdn_norm_g, fox_q_norm_g,
                                                 fox_k_norm_g, fox_f_bias, w_proj_gdn, w_proj_fox, w_out, norm_mlp_g, w_up, w_down))))
    grads_out, delta_out, m_out, v_out = [], [], [], []
    for nm in order:
        if nm in big_res:
            d, mm, vv = big_res[nm]
            gr = big_grad[nm]
        else:
            i = small_idx[nm]
            gr = (small_grads + [g_conv])[i]
            d, mm, vv = s_delta[i], s_m[i], s_v[i]
        for lst, val in ((grads_out, gr), (delta_out, d), (m_out, mm), (v_out, vv)):
            lst.append(val.reshape(shapes[nm]))
    return (loss, g["dx"].reshape(x.shape), *grads_out, *delta_out, *m_out, *v_out)
```

```python
import functools

import jax
import jax.numpy as jnp
from jax import lax
from jax.experimental import pallas as pl
from jax.experimental.pallas import tpu as pltpu

F32 = jnp.float32
BF16 = jnp.bfloat16
LANES = 128
NH = 8
EPS = 1e-6
GDN_CHUNK = 64
GDN_ROWS = 256
GDN_BASE = 16
ROW_TILE = 512
CONV_HEADS = 2
ATT_TILE = 512
NEG = -1e30
VMEM_LIMIT_BYTES = 48 * 1024 * 1024
HI = lax.Precision.HIGHEST
LO = lax.Precision.DEFAULT
MESH = pl.DeviceIdType.MESH
ANY = pl.BlockSpec(memory_space=pl.ANY)

ADAM_LR, ADAM_B1, ADAM_B2, ADAM_EPS, ADAM_WD, ADAM_STEP = 0.001, 0.9, 0.999, 1e-08, 0.01, 10


def _params(n_grid):
    return pltpu.CompilerParams(dimension_semantics=("arbitrary",) * n_grid,
                                vmem_limit_bytes=VMEM_LIMIT_BYTES)


def _dot(a, b, dims, precision=None):
    dn = {"nn": (((1,), (0,)), ((), ())), "nt": (((1,), (1,)), ((), ())), "tn": (((0,), (0,)), ((), ()))}[dims]
    return lax.dot_general(a, b, dn, precision=precision, preferred_element_type=F32)


def _iota(shape, dim):
    return lax.broadcasted_iota(jnp.int32, shape, dim)


def _split(x, parts):
    out = []
    for _ in range(parts - 1):
        hi = x.astype(BF16)
        out.append(hi)
        x = x - hi.astype(F32)
    return out + [x.astype(BF16)]


def _dot_mask(mask, b, dims):
    m16 = mask.astype(BF16)
    b1, b2, b3 = _split(b, 3)
    return _dot(m16, b1, dims) + (_dot(m16, b2, dims) + _dot(m16, b3, dims))


@jax.custom_vjp
def mm_mask(mask, b):
    return _dot_mask(mask, b, "nn")


mm_mask.defvjp(lambda mask, b: (_dot_mask(mask, b, "nn"), mask),
               lambda mask, g: (jnp.zeros_like(mask), _dot_mask(mask, g, "tn")))


def matmul(name, a, b, dims, out_dtype, add=None, tm=1024, tn=1024, tk=512, col_blocks=None,
           extras=(), epilogue=None, out_dtypes=None):
    if col_blocks and dims != "tn":
        nb, b_rows, bw = b.shape
        b_shape = (b_rows, nb * bw)
    else:
        b_shape = b.shape
    if dims == "nn":
        (m, k), (_, n) = a.shape, b_shape
    elif dims == "nt":
        (m, k), (n, _) = a.shape, b_shape
    else:
        (k, m), (_, n) = a.shape, b_shape
    if k <= 1024:
        tk = k
    tm, tn, tk = min(tm, m), min(tn, n), min(tk, k)
    assert m % tm == 0 and n % tn == 0 and k % tk == 0, (name, m, n, k)
    nk = k // tk
    a_spec = pl.BlockSpec((tk, tm), lambda i, j, kk: (kk, i)) if dims == "tn" else pl.BlockSpec((tm, tk), lambda i, j, kk: (i, kk))
    b_spec = pl.BlockSpec((tn, tk), lambda i, j, kk: (j, kk)) if dims == "nt" else pl.BlockSpec((tk, tn), lambda i, j, kk: (kk, j))
    o_spec = pl.BlockSpec((tm, tn), lambda i, j, kk: (i, j))
    out_shape = (m, n)
    if col_blocks and dims == "nn":
        per = bw // tn
        assert bw % tn == 0
        b_spec = pl.BlockSpec((None, tk, tn), lambda i, j, kk: (j // per, kk, j % per))
    elif col_blocks and dims == "nt":
        per = bw // tk
        assert bw % tk == 0
        b_spec = pl.BlockSpec((None, tn, tk), lambda i, j, kk: (kk // per, j, kk % per))
    elif col_blocks:
        bw = n // col_blocks
        per = bw // tn
        assert bw % tn == 0 and add is None
        o_spec = pl.BlockSpec((None, tm, tn), lambda i, j, kk: (j // per, i, j % per))
        out_shape = (col_blocks, m, bw)
    extras = list(extras) + ([add] if add is not None else [])
    if add is not None:
        assert epilogue is None
        epilogue = lambda r, *e: [r + e[-1]]
    out_dtypes = [out_dtype] if epilogue is None or out_dtypes is None else list(out_dtypes)
    n_ex, n_out = len(extras), len(out_dtypes)

    def body(*refs):
        a_ref, b_ref = refs[0], refs[1]
        ex_refs, o_refs = refs[2:2 + n_ex], refs[2 + n_ex:2 + n_ex + n_out]

        def finish(r):
            res = [r] if epilogue is None else epilogue(r, *[e[...] for e in ex_refs])
            for o_ref, v in zip(o_refs, res):
                o_ref[...] = v.astype(o_ref.dtype)

        if nk == 1:
            finish(_dot(a_ref[...], b_ref[...], dims))
            return
        acc_ref = refs[-1]
        kk = pl.program_id(2)

        @pl.when(kk == 0)
        def _():
            acc_ref[...] = jnp.zeros_like(acc_ref)

        acc_ref[...] += _dot(a_ref[...], b_ref[...], dims)

        @pl.when(kk == nk - 1)
        def _():
            finish(acc_ref[...])

    res = pl.pallas_call(
        body, name=name, grid=(m // tm, n // tn, nk), in_specs=[a_spec, b_spec] + [o_spec] * n_ex, out_specs=[o_spec] * n_out,
        out_shape=[jax.ShapeDtypeStruct(out_shape, dt) for dt in out_dtypes],
        scratch_shapes=[pltpu.VMEM((tm, tn), F32)] if nk > 1 else [], compiler_params=_params(3),
    )(a, b, *extras)
    return res[0] if n_out == 1 else res


def _ew_spec(kind, off, width, tb, hp, order, shape=None):
    def ih(g0, g1):
        return (g0, g1) if order == "ih" else (g1, g0)

    assert off % hp == 0 or kind in ("row", "par")
    if kind == "row":
        return pl.BlockSpec((tb, width), lambda g0, g1: (ih(g0, g1)[0], off))
    if kind == "rowh":
        return pl.BlockSpec((tb, hp * width), lambda g0, g1: (ih(g0, g1)[0], ih(g0, g1)[1] + off // hp))
    if kind == "par":
        return pl.BlockSpec(shape, lambda g0, g1: (0, 0))
    if kind == "parh":
        return pl.BlockSpec((shape[0], hp * width), lambda g0, g1: (0, ih(g0, g1)[1] + off // hp))
    raise ValueError(kind)


def _ew_grid(rows, tb, nh, hp, order):
    assert nh % hp == 0 and rows % tb == 0
    return (rows // tb, nh // hp) if order == "ih" else (nh // hp, rows // tb)


def _ew_load(ref, kind, width, hh):
    if kind in ("row", "par"):
        return ref[...].astype(F32)
    return ref[:, hh * width:(hh + 1) * width].astype(F32)


def ew_fwd(name, f, ins, outs, rows, nh=1, tb=ROW_TILE, order="ih", hp=None):
    hp = nh if hp is None else hp
    n_in = len(ins)

    def body(*refs):
        hb = pl.program_id(1) if order == "ih" else pl.program_id(0)
        for hh in range(hp):
            h = hh if hp == nh else hb * hp + hh
            vals = [_ew_load(r, kd, w, hh) for r, (_, kd, _, w) in zip(refs[:n_in], ins)]
            res = f(h, *vals)
            for r, v, (_, kd, w, _) in zip(refs[n_in:], res, outs):
                if kd == "row":
                    assert hp == 1
                    r[...] = v.astype(r.dtype)
                else:
                    r[:, hh * w:(hh + 1) * w] = v.astype(r.dtype)

    in_specs = [_ew_spec(kd, off, w, tb, hp, order, a.shape) for (a, kd, off, w) in ins]
    out_specs = [_ew_spec(kd, 0, w, tb, hp, order) for (_, kd, w, _) in outs]
    out_shape = [jax.ShapeDtypeStruct((rows, tw), dt) for (tw, _, _, dt) in outs]
    return pl.pallas_call(
        body, name=name, grid=_ew_grid(rows, tb, nh, hp, order), in_specs=in_specs, out_specs=out_specs,
        out_shape=out_shape, compiler_params=_params(2),
    )(*[a for (a, _, _, _) in ins])


def ew_bwd(name, f, ins, cts, extras, emit, outs, rows, nh=1, tb=ROW_TILE, order="ih", hp=None):
    hp = nh if hp is None else hp
    n_in = len(ins)
    flat_cts = [d for group in cts for d in group]
    n_ct, n_ex = len(flat_cts), len(extras)

    def body(*refs):
        g0, g1 = pl.program_id(0), pl.program_id(1)
        hb = g1 if order == "ih" else g0
        out_refs = refs[n_in + n_ct + n_ex:]
        shared = [None] * len(outs)

        def store(r, v, first, sl=None):
            def put(val, add):
                if sl is None:
                    r[...] = (r[...] + val if add else val).astype(r.dtype)
                else:
                    r[:, sl] = (r[:, sl] + val if add else val).astype(r.dtype)

            if first is None:
                put(v, False)
            else:
                pl.when(first)(lambda: put(v, False))
                pl.when(jnp.logical_not(first))(lambda: put(v, True))

        for hh in range(hp):
            h = hh if hp == nh else hb * hp + hh
            vals = [_ew_load(r, kd, w, hh) for r, (_, kd, _, w) in zip(refs[:n_in], ins)]
            ct_refs = list(zip(refs[n_in:n_in + n_ct], flat_cts))
            ct_vals, pos = [], 0
            for group in cts:
                v = None
                for r, (_, kd, _, w) in ct_refs[pos:pos + len(group)]:
                    t = _ew_load(r, kd, w, hh)
                    v = t if v is None else v + t
                pos += len(group)
                ct_vals.append(v)
            ex_vals = [_ew_load(r, kd, w, hh) for r, (_, kd, _, w) in zip(refs[n_in + n_ct:n_in + n_ct + n_ex], extras)]
            _, vjp = jax.vjp(lambda *a: f(h, *a), *vals)
            res = emit(vjp(tuple(ct_vals)), ex_vals)
            for idx, (r, v, (_, kd, w, _, acc)) in enumerate(zip(out_refs, res, outs)):
                if kd in ("row", "par"):
                    shared[idx] = v if shared[idx] is None else shared[idx] + v
                else:
                    store(r, v, (g1 == 0) if acc == "inner" else None, slice(hh * w, (hh + 1) * w))
        for idx, (r, (_, kd, _, _, acc)) in enumerate(zip(out_refs, outs)):
            if kd in ("row", "par"):
                assert acc == "all" or hp == nh
                store(r, shared[idx], jnp.logical_and(g0 == 0, g1 == 0) if acc == "all" else None)

    operands = list(ins) + flat_cts + list(extras)
    in_specs = [_ew_spec(kd, off, w, tb, hp, order, a.shape) for (a, kd, off, w) in operands]
    out_specs = [_ew_spec(kd, 0, w, tb, hp, order, shp) for (shp, kd, w, _, _) in outs]
    out_shape = [jax.ShapeDtypeStruct(shp, dt) for (shp, _, _, dt, _) in outs]
    return pl.pallas_call(
        body, name=name, grid=_ew_grid(rows, tb, nh, hp, order), in_specs=in_specs, out_specs=out_specs,
        out_shape=out_shape, compiler_params=_params(2),
    )(*[a for (a, _, _, _) in operands])


def f_rms(h, x, g):
    r = lax.rsqrt(jnp.mean(x * x, axis=-1, keepdims=True) + EPS)
    return (x * r * g,)


def _softplus(z):
    return jnp.maximum(z, 0.0) + jnp.log1p(jnp.exp(-jnp.abs(z)))


def f_small(h, sp, p1, p2):
    lane = _iota(sp.shape, 1)
    z = sp + p1
    g = -jnp.exp(p2) * _softplus(z)
    beta = jax.nn.sigmoid(z)
    logf = -_softplus(-z)
    return (jnp.where(lane < NH, g, jnp.where(lane < 2 * NH, beta, jnp.where(lane < 3 * NH, logf, 0.0))),)


def _pick(x, lane_id):
    lane = _iota(x.shape, 1)
    col = jnp.sum(jnp.where(lane == lane_id, x, 0.0), axis=1, keepdims=True)
    return jnp.broadcast_to(col, x.shape)


def f_bcast(h, so, cs):
    return _pick(so, h), _pick(so, h + NH), _pick(cs, h + 2 * NH)


def _shift_down(s):
    def down(x):
        return jnp.where(_iota(x.shape, 0) >= s, pltpu.roll(x, s, 0), 0.0)

    def up(g):
        n = g.shape[0]
        return jnp.where(_iota(g.shape, 0) < n - s, pltpu.roll(g, n - s, 0), 0.0)

    @jax.custom_vjp
    def shift(x):
        return down(x)

    shift.defvjp(lambda x: (down(x), None), lambda _, g: (up(g),))
    return shift


def _silu(x):
    return x * jax.nn.sigmoid(x)


def make_f_conv(mode):
    sh1, sh2, sh3 = _shift_down(1), _shift_down(2), _shift_down(3)

    def f(h, x, w):
        sub = _iota(w.shape, 0)

        def tap(i):
            return jnp.sum(jnp.where(sub == i, w, 0.0), axis=0, keepdims=True)

        y = sh3(x) * tap(0)
        y = y + sh2(x) * tap(1)
        y = y + sh1(x) * tap(2)
        y = y + x * tap(3)
        s = _silu(y)
        if mode == "v":
            return (s,)
        n = s * lax.rsqrt(jnp.sum(s * s, axis=-1, keepdims=True) + EPS)
        if mode == "q":
            n = n * (LANES ** -0.5)
        return (n,)

    return f


def f_post(h, o, z, g):
    r = lax.rsqrt(jnp.mean(o * o, axis=-1, keepdims=True) + EPS)
    return (o * r * g * _silu(z),)


def f_merge(h, ga, gb, ya, yb):
    return (jax.nn.sigmoid(ga) * ya + jax.nn.sigmoid(gb) * yb,)


def f_delta(h, do, o):
    return (jnp.broadcast_to(jnp.sum(do * o, axis=1, keepdims=True), o.shape),)


def cumsum_time(name, x, nseq, seq, reverse):
    nb = seq // LANES

    def body(x_ref, o_ref):
        r, c = _iota((LANES, LANES), 0), _iota((LANES, LANES), 1)
        tri = jnp.where((r <= c) if reverse else (r >= c), 1.0, 0.0).astype(F32)
        carry = jnp.zeros((1, LANES), F32)
        for b in (range(nb - 1, -1, -1) if reverse else range(nb)):
            blk = x_ref[b * LANES:(b + 1) * LANES, :]
            o_ref[b * LANES:(b + 1) * LANES, :] = _dot_mask(tri, blk, "nn") + carry
            carry = carry + jnp.sum(blk, axis=0, keepdims=True)

    spec = pl.BlockSpec((seq, LANES), lambda s: (s, 0))
    return pl.pallas_call(body, name=name, grid=(nseq,), in_specs=[spec], out_specs=spec,
                          out_shape=jax.ShapeDtypeStruct(x.shape, F32), compiler_params=_params(1))(x)


def transpose_time(name, x, nseq, seq):
    def body(x_ref, o_ref):
        o_ref[...] = x_ref[...].T

    return pl.pallas_call(
        body, name=name, grid=(nseq,), in_specs=[pl.BlockSpec((seq, LANES), lambda s: (s, 0))],
        out_specs=pl.BlockSpec((LANES, seq), lambda s: (s, 0)),
        out_shape=jax.ShapeDtypeStruct((nseq * LANES, seq), F32), compiler_params=_params(1))(x)


def _gdn_masks():
    n = GDN_ROWS
    r, c = _iota((n, n), 0), _iota((n, n), 1)
    shift = GDN_CHUNK.bit_length() - 1
    same = lax.shift_right_logical(r, shift) == lax.shift_right_logical(c, shift)
    return r, c, same


def _gdn_decay(gb):
    r, c, same = _gdn_masks()
    seg_tril = jnp.where(jnp.logical_and(same, r >= c), 1.0, 0.0).astype(F32)
    g_cum = mm_mask(seg_tril, gb)
    lane0 = _iota(g_cum.shape, 1) == 0
    g_col = jnp.sum(jnp.where(lane0, g_cum, 0.0), axis=1, keepdims=True)
    g_row = jnp.sum(jnp.where(r == c, jnp.broadcast_to(g_col, (GDN_ROWS, GDN_ROWS)), 0.0), axis=0, keepdims=True)
    return g_cum, g_col - g_row


def gdn_f1(q, k, gb, bb):
    r, c, same = _gdn_masks()
    strict = jnp.logical_and(same, r > c)
    _, diff = _gdn_decay(gb)
    lane0 = _iota(bb.shape, 1) == 0
    beta_col = jnp.sum(jnp.where(lane0, bb, 0.0), axis=1, keepdims=True)
    kk = _dot(k, k, "nt", LO)
    return jnp.where(strict, beta_col * kk * jnp.exp(jnp.where(strict, diff, 0.0)), 0.0)


def gdn_f2(t_corr, q, k, v, gb, bb):
    r, c, same = _gdn_masks()
    incl = jnp.logical_and(same, r >= c)
    g_cum, diff = _gdn_decay(gb)
    decay = jnp.where(incl, jnp.exp(jnp.where(incl, diff, 0.0)), 0.0)
    e_g = jnp.exp(g_cum)
    v_beta, k_beta = v * bb, k * bb * e_g
    value = v_beta + _dot(t_corr, v_beta, "nn", LO)
    k_cum = k_beta + _dot(t_corr, k_beta, "nn", LO)
    attn = _dot(q, k, "nt", LO) * decay
    g_last = mm_mask(jnp.where(same, 1.0, 0.0).astype(F32), gb)
    return value, k_cum, attn, q * e_g, k * jnp.exp(g_last - g_cum)


def tri_inverse(a):
    n = GDN_ROWS
    r, c = _iota((n, n), 0), _iota((n, n), 1)
    shift = GDN_BASE.bit_length() - 1
    blk = lax.shift_right_logical(r, shift) == lax.shift_right_logical(c, shift)
    d = jnp.where(blk, a, 0.0)
    lo = a - d
    p = -d
    c_d = p
    for _ in range(shift - 1):
        p = _dot(p, p, "nn", LO)
        c_d = c_d + p + _dot(c_d, p, "nn", LO)
    assert GDN_CHUNK // GDN_BASE == 4
    nmat = lo + _dot(c_d, lo, "nn", LO)
    n2 = _dot(nmat, nmat, "nn", LO)
    c_n = (n2 - nmat) - _dot(nmat, n2, "nn", LO)
    return c_n + c_d + _dot(c_n, c_d, "nn", LO)


def gdn_a_fwd(q, k, v, gb, bb, rows):
    blk = pl.BlockSpec((GDN_ROWS, LANES), lambda i, h: (i, h))
    sq = pl.BlockSpec((GDN_ROWS, GDN_ROWS), lambda i, h: (i, h))

    def body(q_ref, k_ref, v_ref, gb_ref, bb_ref, val_ref, kc_ref, at_ref, qd_ref, kd_ref, t_ref):
        qv, kv, vv, gv, bv = q_ref[...], k_ref[...], v_ref[...], gb_ref[...], bb_ref[...]
        t_inv = tri_inverse(gdn_f1(qv, kv, gv, bv))
        value, k_cum, attn, q_dec, k_dec = gdn_f2(t_inv, qv, kv, vv, gv, bv)
        val_ref[...], kc_ref[...], at_ref[...], qd_ref[...], kd_ref[...], t_ref[...] = value, k_cum, attn, q_dec, k_dec, t_inv

    wide = jax.ShapeDtypeStruct((rows, NH * LANES), F32)
    square = jax.ShapeDtypeStruct((rows, NH * GDN_ROWS), F32)
    return pl.pallas_call(
        body, name="gdn_a_fwd", grid=(rows // GDN_ROWS, NH), in_specs=[blk] * 5,
        out_specs=[blk, blk, sq, blk, blk, sq], out_shape=[wide, wide, square, wide, wide, square],
        compiler_params=_params(2))(q, k, v, gb, bb)


def gdn_a_bwd(q, k, v, gb, bb, t_inv, dval, dkc, dat, dqd, dkd, dgb_b, rows):
    blk = pl.BlockSpec((GDN_ROWS, LANES), lambda i, h: (i, h))
    sq = pl.BlockSpec((GDN_ROWS, GDN_ROWS), lambda i, h: (i, h))

    def body(q_ref, k_ref, v_ref, gb_ref, bb_ref, t_ref, dval_ref, dkc_ref, dat_ref, dqd_ref, dkd_ref, dgbb_ref,
             dq_ref, dk_ref, dv_ref, dgb_ref, dbb_ref):
        qv, kv, vv, gv, bv, tv = q_ref[...], k_ref[...], v_ref[...], gb_ref[...], bb_ref[...], t_ref[...]
        _, vjp1 = jax.vjp(gdn_f1, qv, kv, gv, bv)
        _, vjp2 = jax.vjp(gdn_f2, tv, qv, kv, vv, gv, bv)
        dt, dq2, dk2, dv2, dgb2, dbb2 = vjp2((dval_ref[...], dkc_ref[...], dat_ref[...], dqd_ref[...], dkd_ref[...]))
        left = dt + _dot(tv, dt, "tn", LO)
        da = -(left + _dot(left, tv, "nt", LO))
        dq1, dk1, dgb1, dbb1 = vjp1(da)
        dq_ref[...] = dq1 + dq2
        dk_ref[...] = dk1 + dk2
        dv_ref[...] = dv2
        dgb_ref[...] = dgb1 + dgb2 + dgbb_ref[...]
        dbb_ref[...] = dbb1 + dbb2

    wide = jax.ShapeDtypeStruct((rows, NH * LANES), F32)
    return pl.pallas_call(
        body, name="gdn_a_bwd", grid=(rows // GDN_ROWS, NH),
        in_specs=[blk] * 5 + [sq, blk, blk, sq, blk, blk, blk], out_specs=[blk] * 5, out_shape=[wide] * 5,
        compiler_params=_params(2))(q, k, v, gb, bb, t_inv, dval, dkc, dat, dqd, dkd, dgb_b)


N_CH = GDN_ROWS // GDN_CHUNK


GDN_HP = 8


def gdn_fb(*args):
    per_head = 6 * N_CH
    states = list(args[GDN_HP * per_head:])
    outs = [[None] * N_CH for _ in range(GDN_HP)]
    zero = jnp.zeros((GDN_CHUNK, LANES), F32)
    for c in range(N_CH):
        for hh in range(GDN_HP):
            val, kc, at, qd, kd, gb = (args[hh * per_head + i * N_CH + c] for i in range(6))
            s = states[hh]
            v_new = val - _dot(kc, s, "nn", LO)
            v_pad = jnp.concatenate([zero] * c + [v_new] + [zero] * (N_CH - 1 - c), axis=0)
            outs[hh][c] = _dot(qd, s, "nn", LO) + _dot(at, v_pad, "nn", LO)
            dec = jnp.exp(jnp.sum(gb, axis=0, keepdims=True))
            states[hh] = s * dec + _dot(kd, v_new, "tn", LO)
    return (*[o for head in outs for o in head], *states)


def _gdn_piece(ref, hh, c):
    width = ref.shape[1] // GDN_HP
    return ref.at[c * GDN_CHUNK:(c + 1) * GDN_CHUNK, hh * width:(hh + 1) * width]


def _gdn_pieces(refs, hh):
    return [_gdn_piece(r, hh, c)[...] for r in refs for c in range(N_CH)]


def _gdn_b_specs(nb, rev):
    def blk_row(s, j):
        return s * nb + (nb - 1 - j if rev else j)

    blk = pl.BlockSpec((GDN_ROWS, GDN_HP * LANES), lambda s, hb, j: (blk_row(s, j), hb))
    sq = pl.BlockSpec((GDN_ROWS, GDN_HP * GDN_ROWS), lambda s, hb, j: (blk_row(s, j), hb))
    snap = pl.BlockSpec((GDN_HP * LANES, LANES), lambda s, hb, j: (blk_row(s, j) * (NH // GDN_HP) + hb, 0))
    return blk, sq, snap


def gdn_b_fwd(val, kc, at, qd, kd, gb, nseq, seq):
    nb = seq // GDN_ROWS
    rows = nseq * seq
    blk, sq, snap = _gdn_b_specs(nb, False)

    def body(val_ref, kc_ref, at_ref, qd_ref, kd_ref, gb_ref, o_ref, snap_ref, s_ref):
        @pl.when(pl.program_id(2) == 0)
        def _():
            s_ref[...] = jnp.zeros_like(s_ref)

        states = [s_ref[hh] for hh in range(GDN_HP)]
        for hh in range(GDN_HP):
            snap_ref[hh * LANES:(hh + 1) * LANES, :] = states[hh]
        pieces = [p for hh in range(GDN_HP) for p in _gdn_pieces([val_ref, kc_ref, at_ref, qd_ref, kd_ref, gb_ref], hh)]
        res = gdn_fb(*pieces, *states)
        for hh in range(GDN_HP):
            for c in range(N_CH):
                _gdn_piece(o_ref, hh, c)[...] = res[hh * N_CH + c]
            s_ref[hh] = res[GDN_HP * N_CH + hh]

    return pl.pallas_call(
        body, name="gdn_b_fwd", grid=(nseq, NH // GDN_HP, nb), in_specs=[blk, blk, sq, blk, blk, blk], out_specs=[blk, snap],
        out_shape=[jax.ShapeDtypeStruct((rows, NH * LANES), F32), jax.ShapeDtypeStruct((nseq * nb * NH * LANES, LANES), F32)],
        scratch_shapes=[pltpu.VMEM((GDN_HP, LANES, LANES), F32)], compiler_params=_params(3))(val, kc, at, qd, kd, gb)


def gdn_b_bwd(val, kc, at, qd, kd, gb, snaps, do, nseq, seq):
    nb = seq // GDN_ROWS
    rows = nseq * seq
    blk, sq, snap = _gdn_b_specs(nb, True)

    def body(val_ref, kc_ref, at_ref, qd_ref, kd_ref, gb_ref, snap_ref, do_ref,
             dval_ref, dkc_ref, dat_ref, dqd_ref, dkd_ref, dgb_ref, ds_ref):
        @pl.when(pl.program_id(2) == 0)
        def _():
            ds_ref[...] = jnp.zeros_like(ds_ref)

        pieces = [p for hh in range(GDN_HP) for p in _gdn_pieces([val_ref, kc_ref, at_ref, qd_ref, kd_ref, gb_ref], hh)]
        states = [snap_ref[hh * LANES:(hh + 1) * LANES, :] for hh in range(GDN_HP)]
        _, vjp = jax.vjp(gdn_fb, *pieces, *states)
        cts = [p for hh in range(GDN_HP) for p in _gdn_pieces([do_ref], hh)] + [ds_ref[hh] for hh in range(GDN_HP)]
        grads = vjp(tuple(cts))
        for hh in range(GDN_HP):
            for i, r in enumerate([dval_ref, dkc_ref, dat_ref, dqd_ref, dkd_ref, dgb_ref]):
                for c in range(N_CH):
                    _gdn_piece(r, hh, c)[...] = grads[hh * 6 * N_CH + i * N_CH + c]
            ds_ref[hh] = grads[GDN_HP * 6 * N_CH + hh]

    wide = jax.ShapeDtypeStruct((rows, NH * LANES), F32)
    square = jax.ShapeDtypeStruct((rows, NH * GDN_ROWS), F32)
    return pl.pallas_call(
        body, name="gdn_b_bwd", grid=(nseq, NH // GDN_HP, nb), in_specs=[blk, blk, sq, blk, blk, blk, snap, blk],
        out_specs=[blk, blk, sq, blk, blk, blk], out_shape=[wide, wide, square, wide, wide, wide],
        scratch_shapes=[pltpu.VMEM((GDN_HP, LANES, LANES), F32)], compiler_params=_params(3))(val, kc, at, qd, kd, gb, snaps, do)


FOX_Q, FOX_K, FOX_V = 4 * NH, 5 * NH, 6 * NH
FOX_SCALE = LANES ** -0.5


def _head_row(ct_ref, h, off, width):
    blk = ct_ref[:, pl.ds(off, width)]
    return jnp.sum(jnp.where(_iota(blk.shape, 0) == h, blk, 0.0), axis=0, keepdims=True)


def _col(x):
    return jnp.max(x, axis=1, keepdims=True)


def _row(x):
    return jnp.max(x.T, axis=0, keepdims=True)


def _causal(shape, q_dim):
    return _iota(shape, q_dim) >= _iota(shape, 1 - q_dim)


def fox_fwd(qn, kn, proj, ct, nseq, seq):
    tq = tk = min(ATT_TILE, seq)
    nq = seq // tq
    rows = nseq * seq
    qblk = pl.BlockSpec((tq, LANES), lambda s, h, i: (s * nq + i, h))
    full = pl.BlockSpec((seq, LANES), lambda s, h, i: (s, h))
    vfull = pl.BlockSpec((seq, LANES), lambda s, h, i: (s, h + FOX_V))
    ctb = pl.BlockSpec((NH, seq), lambda s, h, i: (s * (LANES // NH) + 2, 0))

    def body(q_ref, k_ref, v_ref, ct_ref, o_ref, o16_ref, lse_ref):
        h, i = pl.program_id(1), pl.program_id(2)
        q = q_ref[...]

        def step(j, carry, diag):
            m, l, acc = carry
            off = pl.multiple_of(j * tk, tk)
            s = _dot(q, k_ref[pl.ds(off, tk), :], "nt") * FOX_SCALE - _head_row(ct_ref, h, off, tk)
            if diag:
                s = jnp.where(_causal(s.shape, 0), s, NEG)
            m_new = jnp.maximum(m, jnp.max(s, axis=1, keepdims=True))
            p = jnp.exp(s - m_new)
            alpha = jnp.exp(m - m_new)
            l = alpha * l + jnp.sum(p, axis=1, keepdims=True)
            acc = alpha * acc + _dot(p.astype(BF16), v_ref[pl.ds(off, tk), :].astype(BF16), "nn")
            return m_new, l, acc

        init = (jnp.full((tq, 1), NEG, F32), jnp.zeros((tq, 1), F32), jnp.zeros((tq, LANES), F32))
        carry = lax.fori_loop(0, i, lambda j, c: step(j, c, False), init)
        m, l, acc = step(i, carry, True)
        o = acc / l
        o_ref[...] = o
        o16_ref[...] = o.astype(BF16)
        lse_ref[...] = jnp.broadcast_to(m + jnp.log(l), (tq, LANES))

    wide = (rows, NH * LANES)
    return pl.pallas_call(
        body, name="fox_fwd", grid=(nseq, NH, nq), in_specs=[qblk, full, vfull, ctb], out_specs=[qblk] * 3,
        out_shape=[jax.ShapeDtypeStruct(wide, F32), jax.ShapeDtypeStruct(wide, BF16), jax.ShapeDtypeStruct(wide, F32)],
        compiler_params=_params(3))(qn, kn, proj, ct)


def fox_dq(qn, kn, proj, ct, do, lse, delta, nseq, seq):
    tq = tk = min(ATT_TILE, seq)
    nq = seq // tq
    rows = nseq * seq
    qblk = pl.BlockSpec((tq, LANES), lambda s, h, i: (s * nq + i, h))
    full = pl.BlockSpec((seq, LANES), lambda s, h, i: (s, h))
    vfull = pl.BlockSpec((seq, LANES), lambda s, h, i: (s, h + FOX_V))
    ctb = pl.BlockSpec((NH, seq), lambda s, h, i: (s * (LANES // NH) + 2, 0))

    def body(q_ref, k_ref, v_ref, ct_ref, do_ref, lse_ref, dl_ref, dq_ref, dc_ref):
        h, i = pl.program_id(1), pl.program_id(2)
        q = q_ref[...]
        lse, delta = _col(lse_ref[...]), _col(dl_ref[...])
        do16 = do_ref[...].astype(BF16)

        def step(j, carry, diag):
            dq, dc = carry
            off = pl.multiple_of(j * tk, tk)
            k = k_ref[pl.ds(off, tk), :]
            p = jnp.exp(_dot(q, k, "nt") * FOX_SCALE - _head_row(ct_ref, h, off, tk) - lse)
            if diag:
                p = jnp.where(_causal(p.shape, 0), p, 0.0)
            dp = _dot(do16, v_ref[pl.ds(off, tk), :].astype(BF16), "nt")
            ds = p * (dp - delta)
            return dq + _dot(ds.astype(BF16), k, "nn"), dc + jnp.sum(ds, axis=1, keepdims=True)

        init = (jnp.zeros((tq, LANES), F32), jnp.zeros((tq, 1), F32))
        dq, dc = step(i, lax.fori_loop(0, i, lambda j, c: step(j, c, False), init), True)
        dq_ref[...] = dq * FOX_SCALE
        dc_ref[...] = jnp.where(_iota((tq, LANES), 1) == 0, dc, 0.0)

    wide = jax.ShapeDtypeStruct((rows, NH * LANES), F32)
    return pl.pallas_call(
        body, name="fox_dq", grid=(nseq, NH, nq), in_specs=[qblk, full, vfull, ctb, qblk, qblk, qblk],
        out_specs=[qblk, qblk], out_shape=[wide, wide], compiler_params=_params(3))(qn, kn, proj, ct, do, lse, delta)


def fox_dkv(qn, kn, proj, cb, do, lse, delta, nseq, seq):
    tq = tk = min(ATT_TILE, seq)
    nq = seq // tq
    rows = nseq * seq
    kblk = pl.BlockSpec((tk, LANES), lambda s, h, j: (s * nq + j, h))
    vblk = pl.BlockSpec((tk, LANES), lambda s, h, j: (s * nq + j, h + FOX_V))
    full = pl.BlockSpec((seq, LANES), lambda s, h, j: (s, h))

    def body(q_ref, k_ref, v_ref, cb_ref, do_ref, lse_ref, dl_ref, dk_ref, dv_ref, dc_ref):
        j = pl.program_id(2)
        k = k_ref[...]
        v16 = v_ref[...].astype(BF16)
        ck = _col(cb_ref[...])

        def step(i, carry, diag):
            dk, dv, dc = carry
            off = pl.multiple_of(i * tq, tq)
            q = q_ref[pl.ds(off, tq), :]
            do16 = do_ref[pl.ds(off, tq), :].astype(BF16)
            lse, delta = (_row(r[pl.ds(off, tq), :]) for r in (lse_ref, dl_ref))
            p = jnp.exp(_dot(k, q, "nt") * FOX_SCALE - ck - lse)
            if diag:
                p = jnp.where(_causal(p.shape, 1), p, 0.0)
            dv = dv + _dot(p.astype(BF16), do16, "nn")
            ds = p * (_dot(v16, do16, "nt") - delta)
            return dk + _dot(ds.astype(BF16), q, "nn"), dv, dc + jnp.sum(ds, axis=1, keepdims=True)

        zero = jnp.zeros((tk, LANES), F32)
        carry = step(j, (zero, zero, jnp.zeros((tk, 1), F32)), True)
        dk, dv, dc = lax.fori_loop(j + 1, nq, lambda i, c: step(i, c, False), carry)
        dk_ref[...] = dk * FOX_SCALE
        dv_ref[...] = dv.astype(BF16)
        dc_ref[...] = jnp.where(_iota((tk, LANES), 1) == 0, -dc, 0.0)

    wide = (rows, NH * LANES)
    return pl.pallas_call(
        body, name="fox_dkv", grid=(nseq, NH, nq), in_specs=[full, kblk, vblk, kblk, full, full, full],
        out_specs=[kblk, kblk, kblk],
        out_shape=[jax.ShapeDtypeStruct(wide, F32), jax.ShapeDtypeStruct(wide, BF16), jax.ShapeDtypeStruct(wide, F32)],
        compiler_params=_params(3))(qn, kn, proj, cb, do, lse, delta)


def loss_head(out, tgt, rows, width):
    tb = ROW_TILE
    blk = pl.BlockSpec((tb, width), lambda i: (i, 0))
    accb = pl.BlockSpec((8, LANES), lambda i: (0, 0))

    def body(o_ref, t_ref, d32_ref, d16_ref, acc_ref):
        d = o_ref[...] - t_ref[...]
        row_loss = 0.5 * jnp.mean(d * d, axis=1, keepdims=True)
        g = d * (1.0 / width)
        d32_ref[...] = g
        d16_ref[...] = g.astype(BF16)
        part = jnp.where(_iota((tb, LANES), 1) == 0, row_loss, 0.0).reshape(tb // 8, 8, LANES).sum(axis=0)

        @pl.when(pl.program_id(0) == 0)
        def _():
            acc_ref[...] = part

        @pl.when(pl.program_id(0) != 0)
        def _():
            acc_ref[...] += part

    return pl.pallas_call(
        body, name="loss_head", grid=(rows // tb,), in_specs=[blk, blk], out_specs=[blk, blk, accb],
        out_shape=[jax.ShapeDtypeStruct((rows, width), F32), jax.ShapeDtypeStruct((rows, width), BF16),
                   jax.ShapeDtypeStruct((8, LANES), F32)], compiler_params=_params(1))(out, tgt)


def _adamw_update(w, g, m, v):
    m_new = ADAM_B1 * m + (1.0 - ADAM_B1) * g
    v_new = ADAM_B2 * v + (1.0 - ADAM_B2) * (g * g)
    m_hat = m_new / (1.0 - ADAM_B1 ** ADAM_STEP)
    v_hat = v_new / (1.0 - ADAM_B2 ** ADAM_STEP)
    return -ADAM_LR * (m_hat / (jnp.sqrt(v_hat) + ADAM_EPS) + ADAM_WD * w), m_new, v_new


def adamw(name, w, g, m, v):
    rows, cols = w.shape
    tb = min(rows, 128)
    assert rows % tb == 0
    blk = pl.BlockSpec((tb, cols), lambda i: (i, 0))

    def body(w_ref, g_ref, m_ref, v_ref, d_ref, mo_ref, vo_ref):
        d_ref[...], mo_ref[...], vo_ref[...] = _adamw_update(w_ref[...], g_ref[...], m_ref[...], v_ref[...])

    shp = jax.ShapeDtypeStruct(w.shape, F32)
    return pl.pallas_call(body, name=name, grid=(rows // tb,), in_specs=[blk] * 4, out_specs=[blk] * 3,
                          out_shape=[shp] * 3, compiler_params=_params(1))(w, g, m, v)


SPLIT_TILE = 128


def _tiled(shape2d, ax, n_lead, index):
    blk = (SPLIT_TILE, shape2d[1]) if ax == 0 else (shape2d[0], SPLIT_TILE)

    def index_map(*args):
        *lead, t = index(*args)
        return (*lead, t, 0) if ax == 0 else (*lead, 0, t)

    return pl.BlockSpec((None,) * n_lead + blk, index_map)


def adamw_halves(name, w, mine, other, m, v, c, ax):
    steps = w.shape[ax] // 2 // SPLIT_TILE
    assert w.shape[ax] == 2 * steps * SPLIT_TILE

    def body(c_ref, w_ref, mine_ref, other_ref, m_ref, v_ref, g_ref, d_ref, mo_ref, vo_ref):
        g = jnp.where(pl.program_id(0) // steps == c_ref[0], mine_ref[...], other_ref[...])
        g_ref[...] = g
        d_ref[...], mo_ref[...], vo_ref[...] = _adamw_update(w_ref[...], g, m_ref[...], v_ref[...])

    blk = _tiled(w.shape, ax, 0, lambda i, c_ref: (i,))
    hblk = _tiled(mine.shape, ax, 0, lambda i, c_ref: (i % steps,))
    grid_spec = pltpu.PrefetchScalarGridSpec(num_scalar_prefetch=1, grid=(2 * steps,),
                                             in_specs=[blk, hblk, hblk, blk, blk], out_specs=[blk] * 4)
    shp = jax.ShapeDtypeStruct(w.shape, F32)
    return pl.pallas_call(body, name=name, grid_spec=grid_spec, out_shape=[shp] * 4,
                          compiler_params=_params(1))(c, w, mine, other, m, v)


def add_chips(name, slots, parts, chip, axes):
    outs = []
    for idx, (x, own, ax) in enumerate(zip(slots, parts, axes)):
        n, shape2d = x.shape[0], x.shape[1:]
        steps = shape2d[ax] // SPLIT_TILE
        assert shape2d[ax] == steps * SPLIT_TILE

        def body(me_ref, *refs, n=n):
            o_ref = refs[n + 1]
            acc = None
            for t in range(n):
                term = jnp.where(me_ref[0] == t, refs[n][...], refs[t][...]).astype(F32)
                acc = term if acc is None else acc + term
            o_ref[...] = acc

        def filled(t, n=n):
            return lambda i, me_ref: (jnp.where(me_ref[0] == t, (t + 1) % n, t), i)

        grid_spec = pltpu.PrefetchScalarGridSpec(
            num_scalar_prefetch=1, grid=(steps,),
            in_specs=[_tiled(shape2d, ax, 1, filled(t)) for t in range(n)]
            + [_tiled(shape2d, ax, 1, lambda i, me_ref: (me_ref[0], i))],
            out_specs=_tiled(shape2d, ax, 0, lambda i, me_ref: (i,)))
        outs.append(pl.pallas_call(
            body, name=f"{name}_{idx}", grid_spec=grid_spec, out_shape=jax.ShapeDtypeStruct(shape2d, F32),
            compiler_params=_params(1))(chip, *([x] * n), own))
    return outs


def add_pair(name, gs, rs, c, axes):
    outs = []
    for idx, (g, r, ax) in enumerate(zip(gs, rs, axes)):
        nb = r.shape[0]
        steps = r.shape[1 + ax] // SPLIT_TILE
        assert r.shape[1 + ax] == steps * SPLIT_TILE

        def body(c_ref, g_ref, r_ref, o_ref):
            o_ref[...] = (g_ref[...] + r_ref[...]).astype(BF16)

        grid_spec = pltpu.PrefetchScalarGridSpec(
            num_scalar_prefetch=1, grid=(nb, steps),
            in_specs=[_tiled(g.shape[1:], ax, 1, lambda b, i, c_ref: (b, c_ref[0] * steps + i)),
                      _tiled(r.shape[1:], ax, 1, lambda b, i, c_ref: (b, i))],
            out_specs=_tiled(r.shape[1:], ax, 1, lambda b, i, c_ref: (b, i)))
        outs.append(pl.pallas_call(
            body, name=f"{name}_{idx}", grid_spec=grid_spec, out_shape=jax.ShapeDtypeStruct(r.shape, BF16),
            compiler_params=_params(2))(c, g, r))
    return outs


def _place():
    x, y, c = lax.axis_index("x"), lax.axis_index("y"), lax.axis_index("c")
    return x, y, c, [(1 - x, y), (x, 1 - y), (1 - x, 1 - y)]


def _remote(src, dst, send_sem, recv_sem, dev):
    return pltpu.make_async_remote_copy(src_ref=src, dst_ref=dst, send_sem=send_sem, recv_sem=recv_sem,
                                        device_id=dev, device_id_type=MESH)


def _half(ref, lead, ax, which):
    size = ref.shape[len(lead) + ax] // 2
    part = pl.ds(which * size, size)
    return ref.at[(*lead, part, slice(None)) if ax == 0 else (*lead, slice(None), part)]


def gather_weights(shards, axes):
    n = len(shards)

    def body(*refs):
        ins, outs = refs[:n], refs[n:2 * n]
        ici_s, ici_r, d2d_s, d2d_r = refs[2 * n:]
        x, y, c, chips = _place()
        me = 2 * x + y
        sends, passes = [], []
        for w in range(n):
            for j, (ox, oy) in enumerate(chips):
                cp = _remote(_half(ins[w], (), axes[w], c), _half(outs[w], (me,), axes[w], c),
                             ici_s.at[3 * w + j], ici_r.at[3 * w + j], (ox, oy, c))
                cp.start()
                sends.append(cp)
        for w in range(n):
            for j, (ox, oy) in enumerate(chips):
                landed = _half(outs[w], (2 * ox + oy,), axes[w], c)
                _remote(landed, landed, ici_s.at[3 * w + j], ici_r.at[3 * w + j], (ox, oy, c)).wait_recv()
                cp = _remote(landed, landed, d2d_s.at[3 * w + j], d2d_r.at[3 * w + j], (x, y, 1 - c))
                cp.start()
                passes.append(cp)
        for w in range(n):
            for j, (ox, oy) in enumerate(chips):
                other = _half(outs[w], (2 * ox + oy,), axes[w], 1 - c)
                _remote(other, other, d2d_s.at[3 * w + j], d2d_r.at[3 * w + j], (x, y, 1 - c)).wait_recv()
        for cp in sends + passes:
            cp.wait_send()

    return pl.pallas_call(
        body, name="gather_weights", in_specs=[ANY] * n, out_specs=[ANY] * n,
        out_shape=[jax.ShapeDtypeStruct((4,) + s.shape, s.dtype) for s in shards],
        scratch_shapes=[pltpu.SemaphoreType.DMA((3 * n,))] * 4,
    )(*shards)


HBM = pl.BlockSpec(memory_space=pltpu.HBM)
SEM = pl.BlockSpec(memory_space=pltpu.SEMAPHORE)
DATAFLOW = pltpu.SideEffectType.DATAFLOW_SIDE_EFFECTING


def _hbm(a):
    return pltpu.with_memory_space_constraint(a, pltpu.HBM)


class SplitExchange:
    def __init__(self, name, srcs, zone_shapes, n_sems, plan):
        self.name, self.n, self.n_sems, self.plan = name, len(srcs), n_sems, plan
        self.srcs = [_hbm(s) for s in srcs]
        self.zones = [_hbm(lax.empty(shape, s.dtype)) for shape, s in zip(zone_shapes, srcs)]

    def start(self, after):
        n, n_after = self.n, len(after)

        def body(*refs):
            ins, lands = refs[:n], refs[n:2 * n]
            send, recv, token = refs[2 * n + n_after], refs[2 * n + n_after + 1], refs[-1]
            for src, dst, si, ri, dev in self.plan(ins, lands)[0]:
                _remote(src, dst, send.at[si], recv.at[ri], dev).start()
            token[...] = jnp.zeros_like(token)

        res = pl.pallas_call(
            body, name=f"{self.name}_start", in_specs=[HBM] * (2 * n) + [ANY] * n_after,
            out_specs=[SEM, SEM] + [HBM] * (2 * n) + [pl.BlockSpec(memory_space=pltpu.VMEM)],
            out_shape=[pltpu.SemaphoreType.DMA((self.n_sems,)), pltpu.SemaphoreType.DMA((self.n_sems,))]
            + [pltpu.HBM(a.shape, a.dtype) for a in self.srcs + self.zones] + [jax.ShapeDtypeStruct((8, LANES), F32)],
            input_output_aliases={i: 2 + i for i in range(2 * n)},
            compiler_params=pltpu.CompilerParams(has_side_effects=DATAFLOW),
        )(*self.srcs, *self.zones, *after)
        self.sems, self.srcs, self.zones = res[:2], list(res[2:2 + n]), list(res[2 + n:2 + 2 * n])
        return res[-1][0, 0]

    def wait(self, after):
        n = self.n

        def body(*refs):
            ins, lands = refs[:n], refs[n:2 * n]
            send, recv = refs[2 * n], refs[2 * n + 1]
            sends, arrivals = self.plan(ins, lands)
            for src, _, si, _, dev in sends:
                _remote(src, src, send.at[si], recv.at[si], dev).wait_send()
            for landed, ri in arrivals:
                _remote(landed, landed, send.at[ri], recv.at[ri], _place()[:3]).wait_recv()

        res = pl.pallas_call(
            body, name=f"{self.name}_wait", in_specs=[HBM] * (2 * n) + [SEM, SEM, ANY], out_specs=[HBM] * (2 * n),
            out_shape=[pltpu.HBM(a.shape, a.dtype) for a in self.srcs + self.zones],
            input_output_aliases={i: i for i in range(2 * n)},
            compiler_params=pltpu.CompilerParams(has_side_effects=DATAFLOW),
        )(*self.srcs, *self.zones, *self.sems, after)
        self.srcs = list(res[:n])
        return list(res[n:])


def split_gather(shards):
    def plan(ins, lands):
        x, y, c, chips = _place()
        sends, arrivals = [], []
        for w in range(len(ins)):
            for j, (ox, oy) in enumerate(chips):
                for k in range(2):
                    base = 2 * (3 * w + j)
                    sends.append((_half(ins[w], (), 0, c), _half(lands[w], (2 * x + y,), 0, c), base + k, base + c, (ox, oy, k)))
                    arrivals.append((_half(lands[w], (2 * ox + oy,), 0, k), base + k))
        return sends, arrivals

    return SplitExchange("gather", shards, [(4,) + s.shape for s in shards], 6 * len(shards), plan)


def split_pair_swap(name, grads, axes):
    def plan(ins, lands):
        x, y, c, _ = _place()
        sends = [(_half(ins[w], (slice(None),), axes[w], 1 - c), lands[w], w, w, (x, y, 1 - c)) for w in range(len(ins))]
        return sends, [(lands[w], w) for w in range(len(ins))]

    halved = [tuple(d // 2 if i == 1 + ax else d for i, d in enumerate(g.shape)) for g, ax in zip(grads, axes)]
    return SplitExchange(name, grads, halved, len(grads), plan)


def split_chip_exchange(name, parts):
    def plan(ins, lands):
        x, y, c, chips = _place()
        sends, arrivals = [], []
        for w in range(len(ins)):
            for j, (ox, oy) in enumerate(chips):
                sends.append((ins[w].at[2 * ox + oy], lands[w].at[2 * x + y], 3 * w + j, 3 * w + j, (ox, oy, c)))
                arrivals.append((lands[w].at[2 * ox + oy], 3 * w + j))
        return sends, arrivals

    return SplitExchange(name, parts, [p.shape for p in parts], 3 * len(parts), plan)


def split_pair_send(halves):
    def plan(ins, lands):
        x, y, c, _ = _place()
        return ([(ins[w], lands[w], w, w, (x, y, 1 - c)) for w in range(len(ins))],
                [(lands[w], w) for w in range(len(ins))])

    return SplitExchange("pair_send", halves, [h.shape for h in halves], len(halves), plan)


def pair_send(halves):
    n = len(halves)

    def body(*refs):
        ins, outs = refs[:n], refs[n:2 * n]
        send, recv = refs[2 * n:]
        x, y, c, _ = _place()
        cps = [_remote(ins[w], outs[w], send.at[w], recv.at[w], (x, y, 1 - c)) for w in range(n)]
        for cp in cps:
            cp.start()
        for cp in cps:
            cp.wait_recv()
        for cp in cps:
            cp.wait_send()

    return pl.pallas_call(
        body, name="pair_send", in_specs=[ANY] * n, out_specs=[ANY] * n,
        out_shape=[jax.ShapeDtypeStruct(h.shape, h.dtype) for h in halves],
        scratch_shapes=[pltpu.SemaphoreType.DMA((n,))] * 2,
    )(*halves)


def all_reduce_small(name, vec, after=()):
    rows = vec.shape[0]

    def body(v_ref, *refs):
        o_ref, buf, send, recv = refs[len(after):]
        x, y, c, _ = _place()
        me = 4 * x + 2 * y + c
        buf[me] = v_ref[...]
        cps = []
        for k in range(1, 8):
            kx, ky, kc = (k >> 2) & 1, (k >> 1) & 1, k & 1
            peer = (x if kx == 0 else 1 - x, y if ky == 0 else 1 - y, c if kc == 0 else 1 - c)
            cp = _remote(v_ref, buf.at[me], send.at[k - 1], recv.at[k - 1], peer)
            cp.start()
            cps.append(cp)
        for k in range(1, 8):
            kx, ky, kc = (k >> 2) & 1, (k >> 1) & 1, k & 1
            px, py, pc = (x if kx == 0 else 1 - x, y if ky == 0 else 1 - y, c if kc == 0 else 1 - c)
            slot = buf.at[4 * px + 2 * py + pc]
            _remote(slot, slot, send.at[k - 1], recv.at[k - 1], (px, py, pc)).wait_recv()
        for cp in cps:
            cp.wait_send()
        acc = buf[0]
        for d in range(1, 8):
            acc = acc + buf[d]
        o_ref[...] = acc

    vm = pl.BlockSpec(memory_space=pltpu.VMEM)
    return pl.pallas_call(
        body, name=name, in_specs=[vm] + [ANY] * len(after), out_specs=vm, out_shape=jax.ShapeDtypeStruct(vec.shape, F32),
        scratch_shapes=[pltpu.VMEM((8, rows, LANES), F32), pltpu.SemaphoreType.DMA((7,)), pltpu.SemaphoreType.DMA((7,))],
    )(vec, *after)


class NoExchange:
    def __init__(self, late):
        self.late = late

    def late_weights(self, after):
        return self.late

    def reduce_start(self, grads):
        return 0.0

    def reduce_exchange(self, after):
        return 0.0

    def reduce_finish(self, after):
        return 0.0

    def input_grad_start(self, dw_main, dw_small):
        return jnp.zeros((), F32)

    def input_grad_exchange(self, after):
        return 0.0


def local_step(x2, tgt2, g1, g2, gdn_ng, qn_g, kn_g, p1, p2, conv_w, wt_main, wt_small, hooks, nseq, seq):
    rows, dm = x2.shape
    wide = NH * LANES
    row = lambda a, off=0, w=None: (a, "row", off, a.shape[1] if w is None else w)
    rowh = lambda a, off=0, w=LANES: (a, "rowh", off, w)
    par = lambda a: (a, "par", 0, a.shape[1])
    parh = lambda a, off=0: (a, "parh", off, LANES)
    o_row = lambda w, dt: (w, "row", w, dt)
    o_rowh = lambda dt, tw=wide, w=LANES: (tw, "rowh", w, dt)

    u, = ew_fwd("rms1", f_rms, [row(x2), par(g1)], [o_row(dm, BF16)], rows)
    proj = matmul("mm_in", u, wt_main, "nt", F32)
    sp = matmul("mm_in_small", u, wt_small, "nt", F32)
    so, = ew_fwd("small", f_small, [row(sp), par(p1), par(p2)], [o_row(LANES, F32)], rows)
    cs = cumsum_time("cumsum", so, nseq, seq, False)
    gb, bb, cb = ew_fwd("bcast", f_bcast, [row(so), row(cs)], [o_rowh(F32)] * 3, rows, NH)
    ct = transpose_time("c_time_major", cs, nseq, seq)
    conv = {}
    for mode, off in (("q", 0), ("k", NH), ("v", 2 * NH)):
        conv[mode], = ew_fwd(f"conv_{mode}", make_f_conv(mode), [rowh(proj, off), parh(conv_w, off)], [o_rowh(F32)],
                             rows, NH, seq, "hi", CONV_HEADS)
    val, kcum, attn, qdec, kdec, t_inv = gdn_a_fwd(conv["q"], conv["k"], conv["v"], gb, bb, rows)
    o_a, snaps = gdn_b_fwd(val, kcum, attn, qdec, kdec, gb, nseq, seq)
    ya_in, = ew_fwd("gdn_post", f_post, [rowh(o_a), rowh(proj, 3 * NH), par(gdn_ng)], [o_rowh(BF16)], rows, NH)
    fqn, = ew_fwd("fox_qn", f_rms, [rowh(proj, FOX_Q), par(qn_g)], [o_rowh(BF16)], rows, NH)
    fkn, = ew_fwd("fox_kn", f_rms, [rowh(proj, FOX_K), par(kn_g)], [o_rowh(BF16)], rows, NH)
    o_b, o_b16, lse = fox_fwd(fqn, fkn, proj, ct, nseq, seq)
    p_a, p_b, w_o, w_u, w_d = hooks.late_weights(o_a)
    y_a = matmul("mm_pa", ya_in, p_a, "nn", F32, tn=1024)
    y_b = matmul("mm_pb", o_b16, p_b, "nn", F32, tn=1024)
    gates = [row(proj, 7, dm), row(proj, 8, dm)]
    merged, = ew_fwd("merge", f_merge, gates + [row(y_a), row(y_b)], [o_row(dm, BF16)], rows)
    hres = matmul("mm_out", merged, w_o, "nn", F32, add=x2, tn=1024)
    hn, = ew_fwd("rms2", f_rms, [row(hres), par(g2)], [o_row(dm, BF16)], rows)
    up_blocks = w_u.shape[0]
    act, relu2 = matmul("mm_up", hn, w_u, "nn", F32, col_blocks=up_blocks, out_dtypes=[F32, BF16],
                        epilogue=lambda r: [r, jnp.maximum(r, 0.0) * jnp.maximum(r, 0.0)])
    out = matmul("mm_down", relu2, w_d, "nn", F32, add=hres, tn=1024)
    dout, dout16, loss_acc = loss_head(out, tgt2, rows, dm)

    d_act = matmul("mm_d_act", dout16, w_d, "nt", BF16, extras=[act], epilogue=lambda r, a: [2.0 * jnp.maximum(a, 0.0) * r])
    dw_d = matmul("mm_dw_down", relu2, dout16, "tn", F32, tn=1024)
    dw_u = matmul("mm_dw_up", hn, d_act, "tn", F32, col_blocks=up_blocks)
    d_hn = matmul("mm_d_hn", d_act, w_u, "nt", F32, col_blocks=up_blocks)
    dh, dh16, dg2 = ew_bwd("rms2_b", f_rms, [row(hres), par(g2)], [(row(d_hn),)], [row(dout)],
                           lambda g, e: [g[0] + e[0], g[0] + e[0], g[1]],
                           [((rows, dm), "row", dm, F32, None), ((rows, dm), "row", dm, BF16, None), ((1, dm), "par", dm, F32, "all")], rows)
    d_merged = matmul("mm_d_merged", dh16, w_o, "nt", F32, tn=1024)
    dw_o = matmul("mm_dw_out", merged, dh16, "tn", F32, tn=1024)
    seg16 = ((rows, dm), "row", dm, BF16, None)
    d_ga16, d_gb16, d_ya16, d_yb16 = ew_bwd("merge_b", f_merge, gates + [row(y_a), row(y_b)], [(row(d_merged),)], [],
                                            lambda g, e: list(g), [seg16] * 4, rows)
    dp_a = matmul("mm_dp_a", ya_in, d_ya16, "tn", F32, tn=1024)
    d_ya_in = matmul("mm_d_ya_in", d_ya16, p_a, "nt", F32, tn=1024)
    dp_b = matmul("mm_dp_b", o_b16, d_yb16, "tn", F32, tn=1024)
    d_ob = matmul("mm_d_ob", d_yb16, p_b, "nt", F32, tn=1024)
    token = hooks.reduce_start(dict(p_a=dp_a, p_b=dp_b, w_o=dw_o, w_u=dw_u, w_d=dw_d))
    gdn_ng_t = gdn_ng + token
    h32 = ((rows, wide), "rowh", LANES, F32, None)
    h16 = ((rows, wide), "rowh", LANES, BF16, None)
    gain = ((1, LANES), "par", LANES, F32, "all")
    d_oa, d_z16, d_gdn_ng = ew_bwd("gdn_post_b", f_post, [rowh(o_a), rowh(proj, 3 * NH), par(gdn_ng_t)], [(rowh(d_ya_in),)], [],
                                   lambda g, e: list(g), [h32, h16, gain], rows, NH)
    dval, dkc, dat, dqd, dkd, dgb_b = gdn_b_bwd(val, kcum, attn, qdec, kdec, gb, snaps, d_oa, nseq, seq)
    d_cq, d_ck, d_cv, d_gb, d_bb = gdn_a_bwd(conv["q"], conv["k"], conv["v"], gb, bb, t_inv, dval, dkc, dat, dqd, dkd, dgb_b, rows)
    conv_w_t = conv_w + hooks.reduce_exchange(d_cq)
    d_pre, d_conv = {}, {}
    tap = ((4, wide), "parh", LANES, F32, "inner")
    for mode, off, ctg in (("q", 0, d_cq), ("k", NH, d_ck), ("v", 2 * NH, d_cv)):
        d_pre[mode], d_conv[mode] = ew_bwd(f"conv_{mode}_b", make_f_conv(mode), [rowh(proj, off), parh(conv_w_t, off)],
                                           [(rowh(ctg),)], [], lambda g, e: list(g), [h16, tap], rows, NH, seq, "hi", CONV_HEADS)
    delta, = ew_fwd("fox_delta", f_delta, [rowh(d_ob), rowh(o_b)], [o_rowh(F32)], rows, NH)
    d_fqn, d_cq_b = fox_dq(fqn, fkn, proj, ct, d_ob, lse, delta, nseq, seq)
    d_fkn, d_fv16, d_ck_b = fox_dkv(fqn, fkn, proj, cb, d_ob, lse, delta, nseq, seq)
    qn_g_t = qn_g + hooks.reduce_finish(d_fkn)
    d_fq16, d_qn_g = ew_bwd("fox_qn_b", f_rms, [rowh(proj, FOX_Q), par(qn_g_t)], [(rowh(d_fqn),)], [], lambda g, e: list(g),
                            [h16, gain], rows, NH)
    d_fk16, d_kn_g = ew_bwd("fox_kn_b", f_rms, [rowh(proj, FOX_K), par(kn_g)], [(rowh(d_fkn),)], [], lambda g, e: list(g),
                            [h16, gain], rows, NH)
    narrow = ((rows, LANES), "row", LANES, F32, None)
    d_so, d_cs = ew_bwd("bcast_b", f_bcast, [row(so), row(cs)], [(rowh(d_gb),), (rowh(d_bb),), (rowh(d_cq_b), rowh(d_ck_b))], [],
                        lambda g, e: list(g), [narrow, narrow], rows, NH)
    d_logf = cumsum_time("cumsum_b", d_cs, nseq, seq, True)
    vec = ((1, LANES), "par", LANES, F32, "all")
    d_sp16, d_p1, d_p2 = ew_bwd("small_b", f_small, [row(sp), par(p1), par(p2)], [(row(d_so), row(d_logf))], [],
                                lambda g, e: list(g), [((rows, LANES), "row", LANES, BF16, None), vec, vec], rows)
    d_proj16 = jnp.concatenate([d_pre["q"], d_pre["k"], d_pre["v"], d_z16, d_fq16, d_fk16, d_fv16, d_ga16, d_gb16], axis=1)
    dw_main = matmul("mm_dw_main", d_proj16, u, "tn", F32)
    dw_small = matmul("mm_dw_small", d_sp16, u, "tn", F32)
    wt_small_t = wt_small + hooks.input_grad_start(dw_main, dw_small).astype(BF16)
    d_u = matmul("mm_d_u_small", d_sp16, wt_small_t, "nn", F32)
    d_u = matmul("mm_d_u", d_proj16, wt_main, "nn", F32, add=d_u)
    g1 = g1 + hooks.input_grad_exchange(d_u)
    dx, dg1 = ew_bwd("rms1_b", f_rms, [row(x2), par(g1)], [(row(d_u),)], [row(dh)], lambda g, e: [g[0] + e[0], g[1]],
                     [((rows, dm), "row", dm, F32, None), ((1, dm), "par", dm, F32, "all")], rows)
    d_conv_w = jnp.concatenate([d_conv["q"], d_conv["k"], d_conv["v"]], axis=1)
    return dict(loss_acc=loss_acc, dx=dx, g1=dg1, g2=dg2, gdn_ng=d_gdn_ng, qn=d_qn_g, kn=d_kn_g, p1=d_p1, p2=d_p2,
                conv=d_conv_w, w_main=dw_main, w_small=dw_small, p_a=dp_a, p_b=dp_b, w_o=dw_o, w_u=dw_u, w_d=dw_d)


_W = NH * LANES
_A0, _A1 = 4 * _W, 4 * _W + 2 * NH
_B0, _B1 = _A1 + 3 * _W, _A1 + 3 * _W + NH
N_IN = _B1 + 2 * _W


def _split_w_in(full_t):
    main = jnp.concatenate([full_t[:_A0], full_t[_A1:_B0], full_t[_B1:]], axis=0)
    small = jnp.concatenate([full_t[_A0:_A1], full_t[_B0:_B1], jnp.zeros((LANES - 3 * NH, full_t.shape[1]), full_t.dtype)], axis=0)
    return main, small


def _join_w_in(main, small):
    return jnp.concatenate([main[:_A0], small[:2 * NH], main[_A0:_A0 + 3 * _W], small[2 * NH:3 * NH], main[_A0 + 3 * _W:]], axis=0)


def _lanes(v, at=0):
    return jnp.pad(v.reshape(1, -1), ((0, 0), (at, LANES - at - v.size)))


def kernel(x, norm_mix_g, w_in, gdn_conv_w, gdn_a_log, gdn_dt_bias, gdn_norm_g, fox_q_norm_g, fox_k_norm_g, fox_f_bias, w_proj_gdn, w_proj_fox, w_out, norm_mlp_g, w_up, w_down, loss_target, m_norm_mix_g, m_w_in, m_gdn_conv_w, m_gdn_a_log, m_gdn_dt_bias, m_gdn_norm_g, m_fox_q_norm_g, m_fox_k_norm_g, m_fox_f_bias, m_w_proj_gdn, m_w_proj_fox, m_w_out, m_norm_mlp_g, m_w_up, m_w_down, v_norm_mix_g, v_w_in, v_gdn_conv_w, v_gdn_a_log, v_gdn_dt_bias, v_gdn_norm_g, v_fox_q_norm_g, v_fox_k_norm_g, v_fox_f_bias, v_w_proj_gdn, v_w_proj_fox, v_w_out, v_norm_mlp_g, v_w_up, v_w_down):
    nseq, seq, dm = x.shape
    rows = nseq * seq
    xi, yi, ci = lax.axis_index("x"), lax.axis_index("y"), lax.axis_index("c")
    chip = 2 * xi + yi
    conv_cols = gdn_conv_w.shape[2]

    tr = lambda a: jnp.swapaxes(a[0], 0, 1)
    big = [tr(w_in), w_proj_gdn[0], w_proj_fox[0], w_out[0], w_up[0], w_down[0]]
    axes = [1, 0, 0, 0, 0, 0]
    big16 = [w.astype(BF16) for w in big]
    fill = lambda got, own: lax.dynamic_update_index_in_dim(got, own, chip, 0)
    conv_slot = jnp.zeros((4, 4, conv_cols), F32).at[:, chip].set(jnp.where(ci == 0, gdn_conv_w[0], 0.0))
    conv_full = all_reduce_small("gather_conv", conv_slot.reshape(-1, LANES)).reshape(4, 4 * conv_cols)
    got_in, = gather_weights(big16[:1], axes[:1])
    wt_main, wt_small = _split_w_in(fill(got_in, big16[0]).reshape(-1, dm))
    core, chip_no = ci.reshape(1).astype(jnp.int32), chip.reshape(1).astype(jnp.int32)
    gather = split_gather(big16[1:])
    token = gather.start([got_in, conv_full])

    class Hooks:
        def late_weights(self, after):
            got = gather.wait(after)
            g_pa, g_pb, g_wo, w_u, g_wd = (fill(g, own) for g, own in zip(got, gather.srcs))
            return (*(g.reshape(-1, dm) for g in (g_pa, g_pb, g_wo)), w_u, g_wd.reshape(-1, dm))

        def reduce_start(self, grads):
            blocks = [grads["p_a"].reshape(4, -1, dm), grads["p_b"].reshape(4, -1, dm), grads["w_o"].reshape(4, -1, dm),
                      grads["w_u"], grads["w_d"].reshape(4, -1, dm)]
            self.swap = split_pair_swap("pair_swap_late", blocks, axes[1:])
            return self.swap.start([])

        def reduce_exchange(self, after):
            swapped = self.swap.wait(after)
            self.exchange = split_chip_exchange("chip_exchange_late", add_pair("add_pair_late", self.swap.srcs, swapped, core, axes[1:]))
            return self.exchange.start([])

        def reduce_finish(self, after):
            slots = self.exchange.wait(after)
            self.send = split_pair_send(add_chips("add_chips_late", slots, self.exchange.srcs, chip_no, axes[1:]))
            return self.send.start([])

        def input_grad_start(self, dw_main, dw_small):
            self.in_swap = split_pair_swap("pair_swap_in", [_join_w_in(dw_main, dw_small).reshape(4, -1, dm)], axes[:1])
            return self.in_swap.start([])

        def input_grad_exchange(self, after):
            swapped = self.in_swap.wait(after)
            self.in_exchange = split_chip_exchange("chip_exchange_in", add_pair("add_pair_in", self.in_swap.srcs, swapped, core, axes[:1]))
            return self.in_exchange.start([])

    hooks = Hooks()
    p1 = _lanes(gdn_dt_bias[0]) + _lanes(fox_f_bias[0], 2 * NH)
    p2 = _lanes(gdn_a_log[0])

    g = local_step(x.reshape(rows, dm), loss_target.reshape(rows, dm), norm_mix_g + token, norm_mlp_g, gdn_norm_g,
                   fox_q_norm_g, fox_k_norm_g, p1, p2, conv_full, wt_main, wt_small, hooks, nseq, seq)

    others = hooks.send.wait(g["dx"])
    big_m = [tr(m_w_in), m_w_proj_gdn[0], m_w_proj_fox[0], m_w_out[0], m_w_up[0], m_w_down[0]]
    big_v = [tr(v_w_in), v_w_proj_gdn[0], v_w_proj_fox[0], v_w_out[0], v_w_up[0], v_w_down[0]]
    names = ["w_in", "w_proj_gdn", "w_proj_fox", "w_out", "w_up", "w_down"]
    big_res, big_grad = {}, {}
    for i in range(1, len(names)):
        big_grad[names[i]], *big_res[names[i]] = adamw_halves(f"adamw_{names[i]}", big[i], hooks.send.srcs[i - 1], others[i - 1],
                                                              big_m[i], big_v[i], core, axes[i])
    slots = hooks.in_exchange.wait(big_res[names[-1]][0])
    mine = add_chips("add_chips_in", slots, hooks.in_exchange.srcs, chip_no, axes[:1])
    res = adamw_halves("adamw_w_in", big[0], mine[0], pair_send(mine)[0], big_m[0], big_v[0], core, axes[0])
    big_grad["w_in"], *big_res["w_in"] = [jnp.swapaxes(r, 0, 1) for r in res]

    small_parts = [g["loss_acc"], g["g1"].reshape(8, LANES), g["g2"].reshape(8, LANES), g["gdn_ng"], g["qn"], g["kn"], g["p1"], g["p2"],
                   g["conv"].reshape(-1, LANES)]
    tiled = [jnp.pad(p, ((0, -p.shape[0] % 8), (0, 0))) for p in small_parts]
    red = all_reduce_small("reduce_small", jnp.concatenate(tiled, axis=0), slots)
    pos, red_parts = 0, []
    for p, t in zip(small_parts, tiled):
        red_parts.append(red[pos:pos + p.shape[0]])
        pos += t.shape[0]
    r_loss, r_g1, r_g2, r_gdn_ng, r_qn, r_kn, r_p1, r_p2, r_conv = red_parts
    loss = jnp.sum(r_loss)
    g_conv = lax.dynamic_slice_in_dim(r_conv.reshape(4, 4, conv_cols), chip, 1, axis=1).reshape(4, conv_cols)
    small_grads = [r_g1.reshape(1, dm), r_p2[:, :NH], r_p1[:, :NH], r_gdn_ng, r_qn, r_kn, r_p1[:, 2 * NH:3 * NH], r_g2.reshape(1, dm)]
    small_w = [norm_mix_g, gdn_a_log, gdn_dt_bias, gdn_norm_g, fox_q_norm_g, fox_k_norm_g, fox_f_bias, norm_mlp_g]
    small_m = [m_norm_mix_g, m_gdn_a_log, m_gdn_dt_bias, m_gdn_norm_g, m_fox_q_norm_g, m_fox_k_norm_g, m_fox_f_bias, m_norm_mlp_g]
    small_v = [v_norm_mix_g, v_gdn_a_log, v_gdn_dt_bias, v_gdn_norm_g, v_fox_q_norm_g, v_fox_k_norm_g, v_fox_f_bias, v_norm_mlp_g]

    def pack(parts):
        flat = jnp.concatenate([jnp.pad(p.reshape(-1), (0, -p.size % LANES)) for p in parts])
        return jnp.pad(flat, (0, -flat.size % (8 * LANES))).reshape(-1, LANES)

    packed = adamw("adamw_small", pack(small_w + [gdn_conv_w[0]]), pack(small_grads + [g_conv]),
                   pack(small_m + [m_gdn_conv_w[0]]), pack(small_v + [v_gdn_conv_w[0]]))

    def unpack(flat2d):
        flat, pos, res = flat2d.reshape(-1), 0, []
        for p in small_w + [gdn_conv_w[0]]:
            res.append(flat[pos:pos + p.size].reshape(p.shape))
            pos += p.size + (-p.size % LANES)
        return res

    s_delta, s_m, s_v = (unpack(a) for a in packed)

    order = ["norm_mix_g", "w_in", "gdn_conv_w", "gdn_a_log", "gdn_dt_bias", "gdn_norm_g", "fox_q_norm_g", "fox_k_norm_g",
             "fox_f_bias", "w_proj_gdn", "w_proj_fox", "w_out", "norm_mlp_g", "w_up", "w_down"]
    small_names = ["norm_mix_g", "gdn_a_log", "gdn_dt_bias", "gdn_norm_g", "fox_q_norm_g", "fox_k_norm_g", "fox_f_bias", "norm_mlp_g",
                   "gdn_conv_w"]
    small_idx = {nm: i for i, nm in enumerate(small_names)}
    shapes = dict(zip(order, (a.shape for a in (norm_mix_g, w_in, gdn_conv_w, gdn_a_log, gdn_dt_bias, gdn_norm_g, fox_q_norm_g,
                                                 fox_k_norm_g, fox_f_bias, w_proj_gdn, w_proj_fox, w_out, norm_mlp_g, w_up, w_down))))
    grads_out, delta_out, m_out, v_out = [], [], [], []
    for nm in order:
        if nm in big_res:
            d, mm, vv = big_res[nm]
            gr = big_grad[nm]
        else:
            i = small_idx[nm]
            gr = (small_grads + [g_conv])[i]
            d, mm, vv = s_delta[i], s_m[i], s_v[i]
        for lst, val in ((grads_out, gr), (delta_out, d), (m_out, mm), (v_out, vv)):
            lst.append(val.reshape(shapes[nm]))
    return (loss, g["dx"].reshape(x.shape), *grads_out, *delta_out, *m_out, *v_out)
```

```python
import functools

import jax
import jax.numpy as jnp
from jax import lax
from jax.experimental import pallas as pl
from jax.experimental.pallas import tpu as pltpu

F32 = jnp.float32
BF16 = jnp.bfloat16
LANES = 128
NH = 8
EPS = 1e-6
GDN_CHUNK = 64
GDN_ROWS = 256
GDN_BASE = 16
ROW_TILE = 512
CONV_HEADS = 2
ATT_TILE = 512
NEG = -1e30
VMEM_LIMIT_BYTES = 48 * 1024 * 1024
HI = lax.Precision.HIGHEST
LO = lax.Precision.DEFAULT
MESH = pl.DeviceIdType.MESH
ANY = pl.BlockSpec(memory_space=pl.ANY)

ADAM_LR, ADAM_B1, ADAM_B2, ADAM_EPS, ADAM_WD, ADAM_STEP = 0.001, 0.9, 0.999, 1e-08, 0.01, 10


def _params(n_grid):
    return pltpu.CompilerParams(dimension_semantics=("arbitrary",) * n_grid,
                                vmem_limit_bytes=VMEM_LIMIT_BYTES)


def _dot(a, b, dims, precision=None):
    dn = {"nn": (((1,), (0,)), ((), ())), "nt": (((1,), (1,)), ((), ())), "tn": (((0,), (0,)), ((), ()))}[dims]
    return lax.dot_general(a, b, dn, precision=precision, preferred_element_type=F32)


def _iota(shape, dim):
    return lax.broadcasted_iota(jnp.int32, shape, dim)


def _split(x, parts):
    out = []
    for _ in range(parts - 1):
        hi = x.astype(BF16)
        out.append(hi)
        x = x - hi.astype(F32)
    return out + [x.astype(BF16)]


def _dot_mask(mask, b, dims):
    m16 = mask.astype(BF16)
    b1, b2, b3 = _split(b, 3)
    return _dot(m16, b1, dims) + (_dot(m16, b2, dims) + _dot(m16, b3, dims))


@jax.custom_vjp
def mm_mask(mask, b):
    return _dot_mask(mask, b, "nn")


mm_mask.defvjp(lambda mask, b: (_dot_mask(mask, b, "nn"), mask),
               lambda mask, g: (jnp.zeros_like(mask), _dot_mask(mask, g, "tn")))


def matmul(name, a, b, dims, out_dtype, add=None, tm=1024, tn=1024, tk=512, col_blocks=None,
           extras=(), epilogue=None, out_dtypes=None):
    if col_blocks and dims != "tn":
        nb, b_rows, bw = b.shape
        b_shape = (b_rows, nb * bw)
    else:
        b_shape = b.shape
    if dims == "nn":
        (m, k), (_, n) = a.shape, b_shape
    elif dims == "nt":
        (m, k), (n, _) = a.shape, b_shape
    else:
        (k, m), (_, n) = a.shape, b_shape
    if k <= 1024:
        tk = k
    tm, tn, tk = min(tm, m), min(tn, n), min(tk, k)
    assert m % tm == 0 and n % tn == 0 and k % tk == 0, (name, m, n, k)
    nk = k // tk
    a_spec = pl.BlockSpec((tk, tm), lambda i, j, kk: (kk, i)) if dims == "tn" else pl.BlockSpec((tm, tk), lambda i, j, kk: (i, kk))
    b_spec = pl.BlockSpec((tn, tk), lambda i, j, kk: (j, kk)) if dims == "nt" else pl.BlockSpec((tk, tn), lambda i, j, kk: (kk, j))
    o_spec = pl.BlockSpec((tm, tn), lambda i, j, kk: (i, j))
    out_shape = (m, n)
    if col_blocks and dims == "nn":
        per = bw // tn
        assert bw % tn == 0
        b_spec = pl.BlockSpec((None, tk, tn), lambda i, j, kk: (j // per, kk, j % per))
    elif col_blocks and dims == "nt":
        per = bw // tk
        assert bw % tk == 0
        b_spec = pl.BlockSpec((None, tn, tk), lambda i, j, kk: (kk // per, j, kk % per))
    elif col_blocks:
        bw = n // col_blocks
        per = bw // tn
        assert bw % tn == 0 and add is None
        o_spec = pl.BlockSpec((None, tm, tn), lambda i, j, kk: (j // per, i, j % per))
        out_shape = (col_blocks, m, bw)
    extras = list(extras) + ([add] if add is not None else [])
    if add is not None:
        assert epilogue is None
        epilogue = lambda r, *e: [r + e[-1]]
    out_dtypes = [out_dtype] if epilogue is None or out_dtypes is None else list(out_dtypes)
    n_ex, n_out = len(extras), len(out_dtypes)

    def body(*refs):
        a_ref, b_ref = refs[0], refs[1]
        ex_refs, o_refs = refs[2:2 + n_ex], refs[2 + n_ex:2 + n_ex + n_out]

        def finish(r):
            res = [r] if epilogue is None else epilogue(r, *[e[...] for e in ex_refs])
            for o_ref, v in zip(o_refs, res):
                o_ref[...] = v.astype(o_ref.dtype)

        if nk == 1:
            finish(_dot(a_ref[...], b_ref[...], dims))
            return
        acc_ref = refs[-1]
        kk = pl.program_id(2)

        @pl.when(kk == 0)
        def _():
            acc_ref[...] = jnp.zeros_like(acc_ref)

        acc_ref[...] += _dot(a_ref[...], b_ref[...], dims)

        @pl.when(kk == nk - 1)
        def _():
            finish(acc_ref[...])

    res = pl.pallas_call(
        body, name=name, grid=(m // tm, n // tn, nk), in_specs=[a_spec, b_spec] + [o_spec] * n_ex, out_specs=[o_spec] * n_out,
        out_shape=[jax.ShapeDtypeStruct(out_shape, dt) for dt in out_dtypes],
        scratch_shapes=[pltpu.VMEM((tm, tn), F32)] if nk > 1 else [], compiler_params=_params(3),
    )(a, b, *extras)
    return res[0] if n_out == 1 else res


def _ew_spec(kind, off, width, tb, hp, order, shape=None):
    def ih(g0, g1):
        return (g0, g1) if order == "ih" else (g1, g0)

    assert off % hp == 0 or kind in ("row", "par")
    if kind == "row":
        return pl.BlockSpec((tb, width), lambda g0, g1: (ih(g0, g1)[0], off))
    if kind == "rowh":
        return pl.BlockSpec((tb, hp * width), lambda g0, g1: (ih(g0, g1)[0], ih(g0, g1)[1] + off // hp))
    if kind == "par":
        return pl.BlockSpec(shape, lambda g0, g1: (0, 0))
    if kind == "parh":
        return pl.BlockSpec((shape[0], hp * width), lambda g0, g1: (0, ih(g0, g1)[1] + off // hp))
    raise ValueError(kind)


def _ew_grid(rows, tb, nh, hp, order):
    assert nh % hp == 0 and rows % tb == 0
    return (rows // tb, nh // hp) if order == "ih" else (nh // hp, rows // tb)


def _ew_load(ref, kind, width, hh):
    if kind in ("row", "par"):
        return ref[...].astype(F32)
    return ref[:, hh * width:(hh + 1) * width].astype(F32)


def ew_fwd(name, f, ins, outs, rows, nh=1, tb=ROW_TILE, order="ih", hp=None, after=()):
    hp = nh if hp is None else hp
    n_in = len(ins)

    def body(*refs):
        hb = pl.program_id(1) if order == "ih" else pl.program_id(0)
        for hh in range(hp):
            h = hh if hp == nh else hb * hp + hh
            vals = [_ew_load(r, kd, w, hh) for r, (_, kd, _, w) in zip(refs[:n_in], ins)]
            res = f(h, *vals)
            for r, v, (_, kd, w, _) in zip(refs[n_in + len(after):], res, outs):
                if kd == "row":
                    assert hp == 1
                    r[...] = v.astype(r.dtype)
                else:
                    r[:, hh * w:(hh + 1) * w] = v.astype(r.dtype)

    in_specs = [_ew_spec(kd, off, w, tb, hp, order, a.shape) for (a, kd, off, w) in ins]
    out_specs = [_ew_spec(kd, 0, w, tb, hp, order) for (_, kd, w, _) in outs]
    out_shape = [jax.ShapeDtypeStruct((rows, tw), dt) for (tw, _, _, dt) in outs]
    return pl.pallas_call(
        body, name=name, grid=_ew_grid(rows, tb, nh, hp, order), in_specs=in_specs + [ANY] * len(after), out_specs=out_specs,
        out_shape=out_shape, compiler_params=_params(2),
    )(*[a for (a, _, _, _) in ins], *after)


def ew_bwd(name, f, ins, cts, extras, emit, outs, rows, nh=1, tb=ROW_TILE, order="ih", hp=None):
    hp = nh if hp is None else hp
    n_in = len(ins)
    flat_cts = [d for group in cts for d in group]
    n_ct, n_ex = len(flat_cts), len(extras)

    def body(*refs):
        g0, g1 = pl.program_id(0), pl.program_id(1)
        hb = g1 if order == "ih" else g0
        out_refs = refs[n_in + n_ct + n_ex:]
        shared = [None] * len(outs)

        def store(r, v, first, sl=None):
            def put(val, add):
                if sl is None:
                    r[...] = (r[...] + val if add else val).astype(r.dtype)
                else:
                    r[:, sl] = (r[:, sl] + val if add else val).astype(r.dtype)

            if first is None:
                put(v, False)
            else:
                pl.when(first)(lambda: put(v, False))
                pl.when(jnp.logical_not(first))(lambda: put(v, True))

        for hh in range(hp):
            h = hh if hp == nh else hb * hp + hh
            vals = [_ew_load(r, kd, w, hh) for r, (_, kd, _, w) in zip(refs[:n_in], ins)]
            ct_refs = list(zip(refs[n_in:n_in + n_ct], flat_cts))
            ct_vals, pos = [], 0
            for group in cts:
                v = None
                for r, (_, kd, _, w) in ct_refs[pos:pos + len(group)]:
                    t = _ew_load(r, kd, w, hh)
                    v = t if v is None else v + t
                pos += len(group)
                ct_vals.append(v)
            ex_vals = [_ew_load(r, kd, w, hh) for r, (_, kd, _, w) in zip(refs[n_in + n_ct:n_in + n_ct + n_ex], extras)]
            _, vjp = jax.vjp(lambda *a: f(h, *a), *vals)
            res = emit(vjp(tuple(ct_vals)), ex_vals)
            for idx, (r, v, (_, kd, w, _, acc)) in enumerate(zip(out_refs, res, outs)):
                if kd in ("row", "par"):
                    shared[idx] = v if shared[idx] is None else shared[idx] + v
                else:
                    store(r, v, (g1 == 0) if acc == "inner" else None, slice(hh * w, (hh + 1) * w))
        for idx, (r, (_, kd, _, _, acc)) in enumerate(zip(out_refs, outs)):
            if kd in ("row", "par"):
                assert acc == "all" or hp == nh
                store(r, shared[idx], jnp.logical_and(g0 == 0, g1 == 0) if acc == "all" else None)

    operands = list(ins) + flat_cts + list(extras)
    in_specs = [_ew_spec(kd, off, w, tb, hp, order, a.shape) for (a, kd, off, w) in operands]
    out_specs = [_ew_spec(kd, 0, w, tb, hp, order, shp) for (shp, kd, w, _, _) in outs]
    out_shape = [jax.ShapeDtypeStruct(shp, dt) for (shp, _, _, dt, _) in outs]
    return pl.pallas_call(
        body, name=name, grid=_ew_grid(rows, tb, nh, hp, order), in_specs=in_specs, out_specs=out_specs,
        out_shape=out_shape, compiler_params=_params(2),
    )(*[a for (a, _, _, _) in operands])


def f_rms(h, x, g):
    r = lax.rsqrt(jnp.mean(x * x, axis=-1, keepdims=True) + EPS)
    return (x * r * g,)


def _softplus(z):
    return jnp.maximum(z, 0.0) + jnp.log1p(jnp.exp(-jnp.abs(z)))


def f_small(h, sp, p1, p2):
    lane = _iota(sp.shape, 1)
    z = sp + p1
    g = -jnp.exp(p2) * _softplus(z)
    beta = jax.nn.sigmoid(z)
    logf = -_softplus(-z)
    return (jnp.where(lane < NH, g, jnp.where(lane < 2 * NH, beta, jnp.where(lane < 3 * NH, logf, 0.0))),)


def _pick(x, lane_id):
    lane = _iota(x.shape, 1)
    col = jnp.sum(jnp.where(lane == lane_id, x, 0.0), axis=1, keepdims=True)
    return jnp.broadcast_to(col, x.shape)


def f_bcast(h, so, cs):
    return _pick(so, h), _pick(so, h + NH), _pick(cs, h + 2 * NH)


def _shift_down(s):
    def down(x):
        return jnp.where(_iota(x.shape, 0) >= s, pltpu.roll(x, s, 0), 0.0)

    def up(g):
        n = g.shape[0]
        return jnp.where(_iota(g.shape, 0) < n - s, pltpu.roll(g, n - s, 0), 0.0)

    @jax.custom_vjp
    def shift(x):
        return down(x)

    shift.defvjp(lambda x: (down(x), None), lambda _, g: (up(g),))
    return shift


def _silu(x):
    return x * jax.nn.sigmoid(x)


def make_f_conv(mode):
    sh1, sh2, sh3 = _shift_down(1), _shift_down(2), _shift_down(3)

    def f(h, x, w):
        sub = _iota(w.shape, 0)

        def tap(i):
            return jnp.sum(jnp.where(sub == i, w, 0.0), axis=0, keepdims=True)

        y = sh3(x) * tap(0)
        y = y + sh2(x) * tap(1)
        y = y + sh1(x) * tap(2)
        y = y + x * tap(3)
        s = _silu(y)
        if mode == "v":
            return (s,)
        n = s * lax.rsqrt(jnp.sum(s * s, axis=-1, keepdims=True) + EPS)
        if mode == "q":
            n = n * (LANES ** -0.5)
        return (n,)

    return f


def f_post(h, o, z, g):
    r = lax.rsqrt(jnp.mean(o * o, axis=-1, keepdims=True) + EPS)
    return (o * r * g * _silu(z),)


def f_merge(h, ga, gb, ya, yb):
    return (jax.nn.sigmoid(ga) * ya + jax.nn.sigmoid(gb) * yb,)


def f_delta(h, do, o):
    return (jnp.broadcast_to(jnp.sum(do * o, axis=1, keepdims=True), o.shape),)


def cumsum_time(name, x, nseq, seq, reverse):
    nb = seq // LANES

    def body(x_ref, o_ref):
        r, c = _iota((LANES, LANES), 0), _iota((LANES, LANES), 1)
        tri = jnp.where((r <= c) if reverse else (r >= c), 1.0, 0.0).astype(F32)
        carry = jnp.zeros((1, LANES), F32)
        for b in (range(nb - 1, -1, -1) if reverse else range(nb)):
            blk = x_ref[b * LANES:(b + 1) * LANES, :]
            o_ref[b * LANES:(b + 1) * LANES, :] = _dot_mask(tri, blk, "nn") + carry
            carry = carry + jnp.sum(blk, axis=0, keepdims=True)

    spec = pl.BlockSpec((seq, LANES), lambda s: (s, 0))
    return pl.pallas_call(body, name=name, grid=(nseq,), in_specs=[spec], out_specs=spec,
                          out_shape=jax.ShapeDtypeStruct(x.shape, F32), compiler_params=_params(1))(x)


def transpose_time(name, x, nseq, seq):
    def body(x_ref, o_ref):
        o_ref[...] = x_ref[...].T

    return pl.pallas_call(
        body, name=name, grid=(nseq,), in_specs=[pl.BlockSpec((seq, LANES), lambda s: (s, 0))],
        out_specs=pl.BlockSpec((LANES, seq), lambda s: (s, 0)),
        out_shape=jax.ShapeDtypeStruct((nseq * LANES, seq), F32), compiler_params=_params(1))(x)


def _gdn_masks():
    n = GDN_ROWS
    r, c = _iota((n, n), 0), _iota((n, n), 1)
    shift = GDN_CHUNK.bit_length() - 1
    same = lax.shift_right_logical(r, shift) == lax.shift_right_logical(c, shift)
    return r, c, same


def _gdn_decay(gb):
    r, c, same = _gdn_masks()
    seg_tril = jnp.where(jnp.logical_and(same, r >= c), 1.0, 0.0).astype(F32)
    g_cum = mm_mask(seg_tril, gb)
    lane0 = _iota(g_cum.shape, 1) == 0
    g_col = jnp.sum(jnp.where(lane0, g_cum, 0.0), axis=1, keepdims=True)
    g_row = jnp.sum(jnp.where(r == c, jnp.broadcast_to(g_col, (GDN_ROWS, GDN_ROWS)), 0.0), axis=0, keepdims=True)
    return g_cum, g_col - g_row


def gdn_f1(q, k, gb, bb):
    r, c, same = _gdn_masks()
    strict = jnp.logical_and(same, r > c)
    _, diff = _gdn_decay(gb)
    lane0 = _iota(bb.shape, 1) == 0
    beta_col = jnp.sum(jnp.where(lane0, bb, 0.0), axis=1, keepdims=True)
    kk = _dot(k, k, "nt", LO)
    return jnp.where(strict, beta_col * kk * jnp.exp(jnp.where(strict, diff, 0.0)), 0.0)


def gdn_f2(t_corr, q, k, v, gb, bb):
    r, c, same = _gdn_masks()
    incl = jnp.logical_and(same, r >= c)
    g_cum, diff = _gdn_decay(gb)
    decay = jnp.where(incl, jnp.exp(jnp.where(incl, diff, 0.0)), 0.0)
    e_g = jnp.exp(g_cum)
    v_beta, k_beta = v * bb, k * bb * e_g
    value = v_beta + _dot(t_corr, v_beta, "nn", LO)
    k_cum = k_beta + _dot(t_corr, k_beta, "nn", LO)
    attn = _dot(q, k, "nt", LO) * decay
    g_last = mm_mask(jnp.where(same, 1.0, 0.0).astype(F32), gb)
    return value, k_cum, attn, q * e_g, k * jnp.exp(g_last - g_cum)


def tri_inverse(a):
    n = GDN_ROWS
    r, c = _iota((n, n), 0), _iota((n, n), 1)
    shift = GDN_BASE.bit_length() - 1
    blk = lax.shift_right_logical(r, shift) == lax.shift_right_logical(c, shift)
    d = jnp.where(blk, a, 0.0)
    lo = a - d
    p = -d
    c_d = p
    for _ in range(shift - 1):
        p = _dot(p, p, "nn", LO)
        c_d = c_d + p + _dot(c_d, p, "nn", LO)
    assert GDN_CHUNK // GDN_BASE == 4
    nmat = lo + _dot(c_d, lo, "nn", LO)
    n2 = _dot(nmat, nmat, "nn", LO)
    c_n = (n2 - nmat) - _dot(nmat, n2, "nn", LO)
    return c_n + c_d + _dot(c_n, c_d, "nn", LO)


def gdn_a_fwd(q, k, v, gb, bb, rows):
    blk = pl.BlockSpec((GDN_ROWS, LANES), lambda i, h: (i, h))
    sq = pl.BlockSpec((GDN_ROWS, GDN_ROWS), lambda i, h: (i, h))

    def body(q_ref, k_ref, v_ref, gb_ref, bb_ref, val_ref, kc_ref, at_ref, qd_ref, kd_ref, t_ref):
        qv, kv, vv, gv, bv = q_ref[...], k_ref[...], v_ref[...], gb_ref[...], bb_ref[...]
        t_inv = tri_inverse(gdn_f1(qv, kv, gv, bv))
        value, k_cum, attn, q_dec, k_dec = gdn_f2(t_inv, qv, kv, vv, gv, bv)
        val_ref[...], kc_ref[...], at_ref[...], qd_ref[...], kd_ref[...], t_ref[...] = value, k_cum, attn, q_dec, k_dec, t_inv

    wide = jax.ShapeDtypeStruct((rows, NH * LANES), F32)
    square = jax.ShapeDtypeStruct((rows, NH * GDN_ROWS), F32)
    return pl.pallas_call(
        body, name="gdn_a_fwd", grid=(rows // GDN_ROWS, NH), in_specs=[blk] * 5,
        out_specs=[blk, blk, sq, blk, blk, sq], out_shape=[wide, wide, square, wide, wide, square],
        compiler_params=_params(2))(q, k, v, gb, bb)


def gdn_a_bwd(q, k, v, gb, bb, t_inv, dval, dkc, dat, dqd, dkd, dgb_b, rows):
    blk = pl.BlockSpec((GDN_ROWS, LANES), lambda i, h: (i, h))
    sq = pl.BlockSpec((GDN_ROWS, GDN_ROWS), lambda i, h: (i, h))

    def body(q_ref, k_ref, v_ref, gb_ref, bb_ref, t_ref, dval_ref, dkc_ref, dat_ref, dqd_ref, dkd_ref, dgbb_ref,
             dq_ref, dk_ref, dv_ref, dgb_ref, dbb_ref):
        qv, kv, vv, gv, bv, tv = q_ref[...], k_ref[...], v_ref[...], gb_ref[...], bb_ref[...], t_ref[...]
        _, vjp1 = jax.vjp(gdn_f1, qv, kv, gv, bv)
        _, vjp2 = jax.vjp(gdn_f2, tv, qv, kv, vv, gv, bv)
        dt, dq2, dk2, dv2, dgb2, dbb2 = vjp2((dval_ref[...], dkc_ref[...], dat_ref[...], dqd_ref[...], dkd_ref[...]))
        left = dt + _dot(tv, dt, "tn", LO)
        da = -(left + _dot(left, tv, "nt", LO))
        dq1, dk1, dgb1, dbb1 = vjp1(da)
        dq_ref[...] = dq1 + dq2
        dk_ref[...] = dk1 + dk2
        dv_ref[...] = dv2
        dgb_ref[...] = dgb1 + dgb2 + dgbb_ref[...]
        dbb_ref[...] = dbb1 + dbb2

    wide = jax.ShapeDtypeStruct((rows, NH * LANES), F32)
    return pl.pallas_call(
        body, name="gdn_a_bwd", grid=(rows // GDN_ROWS, NH),
        in_specs=[blk] * 5 + [sq, blk, blk, sq, blk, blk, blk], out_specs=[blk] * 5, out_shape=[wide] * 5,
        compiler_params=_params(2))(q, k, v, gb, bb, t_inv, dval, dkc, dat, dqd, dkd, dgb_b)


N_CH = GDN_ROWS // GDN_CHUNK


GDN_HP = 8


def gdn_fb(*args):
    per_head = 6 * N_CH
    states = list(args[GDN_HP * per_head:])
    outs = [[None] * N_CH for _ in range(GDN_HP)]
    zero = jnp.zeros((GDN_CHUNK, LANES), F32)
    for c in range(N_CH):
        for hh in range(GDN_HP):
            val, kc, at, qd, kd, gb = (args[hh * per_head + i * N_CH + c] for i in range(6))
            s = states[hh]
            v_new = val - _dot(kc, s, "nn", LO)
            v_pad = jnp.concatenate([zero] * c + [v_new] + [zero] * (N_CH - 1 - c), axis=0)
            outs[hh][c] = _dot(qd, s, "nn", LO) + _dot(at, v_pad, "nn", LO)
            dec = jnp.exp(jnp.sum(gb, axis=0, keepdims=True))
            states[hh] = s * dec + _dot(kd, v_new, "tn", LO)
    return (*[o for head in outs for o in head], *states)


def _gdn_piece(ref, hh, c):
    width = ref.shape[1] // GDN_HP
    return ref.at[c * GDN_CHUNK:(c + 1) * GDN_CHUNK, hh * width:(hh + 1) * width]


def _gdn_pieces(refs, hh):
    return [_gdn_piece(r, hh, c)[...] for r in refs for c in range(N_CH)]


def _gdn_b_specs(nb, rev):
    def blk_row(s, j):
        return s * nb + (nb - 1 - j if rev else j)

    blk = pl.BlockSpec((GDN_ROWS, GDN_HP * LANES), lambda s, hb, j: (blk_row(s, j), hb))
    sq = pl.BlockSpec((GDN_ROWS, GDN_HP * GDN_ROWS), lambda s, hb, j: (blk_row(s, j), hb))
    snap = pl.BlockSpec((GDN_HP * LANES, LANES), lambda s, hb, j: (blk_row(s, j) * (NH // GDN_HP) + hb, 0))
    return blk, sq, snap


def gdn_b_fwd(val, kc, at, qd, kd, gb, nseq, seq):
    nb = seq // GDN_ROWS
    rows = nseq * seq
    blk, sq, snap = _gdn_b_specs(nb, False)

    def body(val_ref, kc_ref, at_ref, qd_ref, kd_ref, gb_ref, o_ref, snap_ref, s_ref):
        @pl.when(pl.program_id(2) == 0)
        def _():
            s_ref[...] = jnp.zeros_like(s_ref)

        states = [s_ref[hh] for hh in range(GDN_HP)]
        for hh in range(GDN_HP):
            snap_ref[hh * LANES:(hh + 1) * LANES, :] = states[hh]
        pieces = [p for hh in range(GDN_HP) for p in _gdn_pieces([val_ref, kc_ref, at_ref, qd_ref, kd_ref, gb_ref], hh)]
        res = gdn_fb(*pieces, *states)
        for hh in range(GDN_HP):
            for c in range(N_CH):
                _gdn_piece(o_ref, hh, c)[...] = res[hh * N_CH + c]
            s_ref[hh] = res[GDN_HP * N_CH + hh]

    return pl.pallas_call(
        body, name="gdn_b_fwd", grid=(nseq, NH // GDN_HP, nb), in_specs=[blk, blk, sq, blk, blk, blk], out_specs=[blk, snap],
        out_shape=[jax.ShapeDtypeStruct((rows, NH * LANES), F32), jax.ShapeDtypeStruct((nseq * nb * NH * LANES, LANES), F32)],
        scratch_shapes=[pltpu.VMEM((GDN_HP, LANES, LANES), F32)], compiler_params=_params(3))(val, kc, at, qd, kd, gb)


def gdn_b_bwd(val, kc, at, qd, kd, gb, snaps, do, nseq, seq):
    nb = seq // GDN_ROWS
    rows = nseq * seq
    blk, sq, snap = _gdn_b_specs(nb, True)

    def body(val_ref, kc_ref, at_ref, qd_ref, kd_ref, gb_ref, snap_ref, do_ref,
             dval_ref, dkc_ref, dat_ref, dqd_ref, dkd_ref, dgb_ref, ds_ref):
        @pl.when(pl.program_id(2) == 0)
        def _():
            ds_ref[...] = jnp.zeros_like(ds_ref)

        pieces = [p for hh in range(GDN_HP) for p in _gdn_pieces([val_ref, kc_ref, at_ref, qd_ref, kd_ref, gb_ref], hh)]
        states = [snap_ref[hh * LANES:(hh + 1) * LANES, :] for hh in range(GDN_HP)]
        _, vjp = jax.vjp(gdn_fb, *pieces, *states)
        cts = [p for hh in range(GDN_HP) for p in _gdn_pieces([do_ref], hh)] + [ds_ref[hh] for hh in range(GDN_HP)]
        grads = vjp(tuple(cts))
        for hh in range(GDN_HP):
            for i, r in enumerate([dval_ref, dkc_ref, dat_ref, dqd_ref, dkd_ref, dgb_ref]):
                for c in range(N_CH):
                    _gdn_piece(r, hh, c)[...] = grads[hh * 6 * N_CH + i * N_CH + c]
            ds_ref[hh] = grads[GDN_HP * 6 * N_CH + hh]

    wide = jax.ShapeDtypeStruct((rows, NH * LANES), F32)
    square = jax.ShapeDtypeStruct((rows, NH * GDN_ROWS), F32)
    return pl.pallas_call(
        body, name="gdn_b_bwd", grid=(nseq, NH // GDN_HP, nb), in_specs=[blk, blk, sq, blk, blk, blk, snap, blk],
        out_specs=[blk, blk, sq, blk, blk, blk], out_shape=[wide, wide, square, wide, wide, wide],
        scratch_shapes=[pltpu.VMEM((GDN_HP, LANES, LANES), F32)], compiler_params=_params(3))(val, kc, at, qd, kd, gb, snaps, do)


FOX_Q, FOX_K, FOX_V = 4 * NH, 5 * NH, 6 * NH
FOX_SCALE = LANES ** -0.5


def _head_row(ct_ref, h, off, width):
    blk = ct_ref[:, pl.ds(off, width)]
    return jnp.sum(jnp.where(_iota(blk.shape, 0) == h, blk, 0.0), axis=0, keepdims=True)


def _col(x):
    return jnp.max(x, axis=1, keepdims=True)


def _row(x):
    return jnp.max(x.T, axis=0, keepdims=True)


def _causal(shape, q_dim):
    return _iota(shape, q_dim) >= _iota(shape, 1 - q_dim)


def fox_fwd(qn, kn, proj, ct, nseq, seq):
    tq = tk = min(ATT_TILE, seq)
    nq = seq // tq
    rows = nseq * seq
    qblk = pl.BlockSpec((tq, LANES), lambda s, h, i: (s * nq + i, h))
    full = pl.BlockSpec((seq, LANES), lambda s, h, i: (s, h))
    vfull = pl.BlockSpec((seq, LANES), lambda s, h, i: (s, h + FOX_V))
    ctb = pl.BlockSpec((NH, seq), lambda s, h, i: (s * (LANES // NH) + 2, 0))

    def body(q_ref, k_ref, v_ref, ct_ref, o_ref, o16_ref, lse_ref):
        h, i = pl.program_id(1), pl.program_id(2)
        q = q_ref[...]

        def step(j, carry, diag):
            m, l, acc = carry
            off = pl.multiple_of(j * tk, tk)
            s = _dot(q, k_ref[pl.ds(off, tk), :], "nt") * FOX_SCALE - _head_row(ct_ref, h, off, tk)
            if diag:
                s = jnp.where(_causal(s.shape, 0), s, NEG)
            m_new = jnp.maximum(m, jnp.max(s, axis=1, keepdims=True))
            p = jnp.exp(s - m_new)
            alpha = jnp.exp(m - m_new)
            l = alpha * l + jnp.sum(p, axis=1, keepdims=True)
            acc = alpha * acc + _dot(p.astype(BF16), v_ref[pl.ds(off, tk), :].astype(BF16), "nn")
            return m_new, l, acc

        init = (jnp.full((tq, 1), NEG, F32), jnp.zeros((tq, 1), F32), jnp.zeros((tq, LANES), F32))
        carry = lax.fori_loop(0, i, lambda j, c: step(j, c, False), init)
        m, l, acc = step(i, carry, True)
        o = acc / l
        o_ref[...] = o
        o16_ref[...] = o.astype(BF16)
        lse_ref[...] = jnp.broadcast_to(m + jnp.log(l), (tq, LANES))

    wide = (rows, NH * LANES)
    return pl.pallas_call(
        body, name="fox_fwd", grid=(nseq, NH, nq), in_specs=[qblk, full, vfull, ctb], out_specs=[qblk] * 3,
        out_shape=[jax.ShapeDtypeStruct(wide, F32), jax.ShapeDtypeStruct(wide, BF16), jax.ShapeDtypeStruct(wide, F32)],
        compiler_params=_params(3))(qn, kn, proj, ct)


def fox_dq(qn, kn, proj, ct, do, lse, delta, nseq, seq):
    tq = tk = min(ATT_TILE, seq)
    nq = seq // tq
    rows = nseq * seq
    qblk = pl.BlockSpec((tq, LANES), lambda s, h, i: (s * nq + i, h))
    full = pl.BlockSpec((seq, LANES), lambda s, h, i: (s, h))
    vfull = pl.BlockSpec((seq, LANES), lambda s, h, i: (s, h + FOX_V))
    ctb = pl.BlockSpec((NH, seq), lambda s, h, i: (s * (LANES // NH) + 2, 0))

    def body(q_ref, k_ref, v_ref, ct_ref, do_ref, lse_ref, dl_ref, dq_ref, dc_ref):
        h, i = pl.program_id(1), pl.program_id(2)
        q = q_ref[...]
        lse, delta = _col(lse_ref[...]), _col(dl_ref[...])
        do16 = do_ref[...].astype(BF16)

        def step(j, carry, diag):
            dq, dc = carry
            off = pl.multiple_of(j * tk, tk)
            k = k_ref[pl.ds(off, tk), :]
            p = jnp.exp(_dot(q, k, "nt") * FOX_SCALE - _head_row(ct_ref, h, off, tk) - lse)
            if diag:
                p = jnp.where(_causal(p.shape, 0), p, 0.0)
            dp = _dot(do16, v_ref[pl.ds(off, tk), :].astype(BF16), "nt")
            ds = p * (dp - delta)
            return dq + _dot(ds.astype(BF16), k, "nn"), dc + jnp.sum(ds, axis=1, keepdims=True)

        init = (jnp.zeros((tq, LANES), F32), jnp.zeros((tq, 1), F32))
        dq, dc = step(i, lax.fori_loop(0, i, lambda j, c: step(j, c, False), init), True)
        dq_ref[...] = dq * FOX_SCALE
        dc_ref[...] = jnp.where(_iota((tq, LANES), 1) == 0, dc, 0.0)

    wide = jax.ShapeDtypeStruct((rows, NH * LANES), F32)
    return pl.pallas_call(
        body, name="fox_dq", grid=(nseq, NH, nq), in_specs=[qblk, full, vfull, ctb, qblk, qblk, qblk],
        out_specs=[qblk, qblk], out_shape=[wide, wide], compiler_params=_params(3))(qn, kn, proj, ct, do, lse, delta)


def fox_dkv(qn, kn, proj, cb, do, lse, delta, nseq, seq):
    tq = tk = min(ATT_TILE, seq)
    nq = seq // tq
    rows = nseq * seq
    kblk = pl.BlockSpec((tk, LANES), lambda s, h, j: (s * nq + j, h))
    vblk = pl.BlockSpec((tk, LANES), lambda s, h, j: (s * nq + j, h + FOX_V))
    full = pl.BlockSpec((seq, LANES), lambda s, h, j: (s, h))

    def body(q_ref, k_ref, v_ref, cb_ref, do_ref, lse_ref, dl_ref, dk_ref, dv_ref, dc_ref):
        j = pl.program_id(2)
        k = k_ref[...]
        v16 = v_ref[...].astype(BF16)
        ck = _col(cb_ref[...])

        def step(i, carry, diag):
            dk, dv, dc = carry
            off = pl.multiple_of(i * tq, tq)
            q = q_ref[pl.ds(off, tq), :]
            do16 = do_ref[pl.ds(off, tq), :].astype(BF16)
            lse, delta = (_row(r[pl.ds(off, tq), :]) for r in (lse_ref, dl_ref))
            p = jnp.exp(_dot(k, q, "nt") * FOX_SCALE - ck - lse)
            if diag:
                p = jnp.where(_causal(p.shape, 1), p, 0.0)
            dv = dv + _dot(p.astype(BF16), do16, "nn")
            ds = p * (_dot(v16, do16, "nt") - delta)
            return dk + _dot(ds.astype(BF16), q, "nn"), dv, dc + jnp.sum(ds, axis=1, keepdims=True)

        zero = jnp.zeros((tk, LANES), F32)
        carry = step(j, (zero, zero, jnp.zeros((tk, 1), F32)), True)
        dk, dv, dc = lax.fori_loop(j + 1, nq, lambda i, c: step(i, c, False), carry)
        dk_ref[...] = dk * FOX_SCALE
        dv_ref[...] = dv.astype(BF16)
        dc_ref[...] = jnp.where(_iota((tk, LANES), 1) == 0, -dc, 0.0)

    wide = (rows, NH * LANES)
    return pl.pallas_call(
        body, name="fox_dkv", grid=(nseq, NH, nq), in_specs=[full, kblk, vblk, kblk, full, full, full],
        out_specs=[kblk, kblk, kblk],
        out_shape=[jax.ShapeDtypeStruct(wide, F32), jax.ShapeDtypeStruct(wide, BF16), jax.ShapeDtypeStruct(wide, F32)],
        compiler_params=_params(3))(qn, kn, proj, cb, do, lse, delta)


def loss_head(out, tgt, rows, width):
    tb = ROW_TILE
    blk = pl.BlockSpec((tb, width), lambda i: (i, 0))
    accb = pl.BlockSpec((8, LANES), lambda i: (0, 0))

    def body(o_ref, t_ref, d32_ref, d16_ref, acc_ref):
        d = o_ref[...] - t_ref[...]
        row_loss = 0.5 * jnp.mean(d * d, axis=1, keepdims=True)
        g = d * (1.0 / width)
        d32_ref[...] = g
        d16_ref[...] = g.astype(BF16)
        part = jnp.where(_iota((tb, LANES), 1) == 0, row_loss, 0.0).reshape(tb // 8, 8, LANES).sum(axis=0)

        @pl.when(pl.program_id(0) == 0)
        def _():
            acc_ref[...] = part

        @pl.when(pl.program_id(0) != 0)
        def _():
            acc_ref[...] += part

    return pl.pallas_call(
        body, name="loss_head", grid=(rows // tb,), in_specs=[blk, blk], out_specs=[blk, blk, accb],
        out_shape=[jax.ShapeDtypeStruct((rows, width), F32), jax.ShapeDtypeStruct((rows, width), BF16),
                   jax.ShapeDtypeStruct((8, LANES), F32)], compiler_params=_params(1))(out, tgt)


def _adamw_update(w, g, m, v):
    m_new = ADAM_B1 * m + (1.0 - ADAM_B1) * g
    v_new = ADAM_B2 * v + (1.0 - ADAM_B2) * (g * g)
    m_hat = m_new / (1.0 - ADAM_B1 ** ADAM_STEP)
    v_hat = v_new / (1.0 - ADAM_B2 ** ADAM_STEP)
    return -ADAM_LR * (m_hat / (jnp.sqrt(v_hat) + ADAM_EPS) + ADAM_WD * w), m_new, v_new


def adamw(name, w, g, m, v):
    rows, cols = w.shape
    tb = min(rows, 128)
    assert rows % tb == 0
    blk = pl.BlockSpec((tb, cols), lambda i: (i, 0))

    def body(w_ref, g_ref, m_ref, v_ref, d_ref, mo_ref, vo_ref):
        d_ref[...], mo_ref[...], vo_ref[...] = _adamw_update(w_ref[...], g_ref[...], m_ref[...], v_ref[...])

    shp = jax.ShapeDtypeStruct(w.shape, F32)
    return pl.pallas_call(body, name=name, grid=(rows // tb,), in_specs=[blk] * 4, out_specs=[blk] * 3,
                          out_shape=[shp] * 3, compiler_params=_params(1))(w, g, m, v)


SPLIT_TILE = 128


def _tiled(shape2d, ax, n_lead, index):
    blk = (SPLIT_TILE, shape2d[1]) if ax == 0 else (shape2d[0], SPLIT_TILE)

    def index_map(*args):
        *lead, t = index(*args)
        return (*lead, t, 0) if ax == 0 else (*lead, 0, t)

    return pl.BlockSpec((None,) * n_lead + blk, index_map)


def adamw_halves(name, w, mine, other, m, v, c, ax):
    steps = w.shape[ax] // 2 // SPLIT_TILE
    assert w.shape[ax] == 2 * steps * SPLIT_TILE

    def body(c_ref, w_ref, mine_ref, other_ref, m_ref, v_ref, g_ref, d_ref, mo_ref, vo_ref):
        g = jnp.where(pl.program_id(0) // steps == c_ref[0], mine_ref[...], other_ref[...])
        g_ref[...] = g
        d_ref[...], mo_ref[...], vo_ref[...] = _adamw_update(w_ref[...], g, m_ref[...], v_ref[...])

    blk = _tiled(w.shape, ax, 0, lambda i, c_ref: (i,))
    hblk = _tiled(mine.shape, ax, 0, lambda i, c_ref: (i % steps,))
    grid_spec = pltpu.PrefetchScalarGridSpec(num_scalar_prefetch=1, grid=(2 * steps,),
                                             in_specs=[blk, hblk, hblk, blk, blk], out_specs=[blk] * 4)
    shp = jax.ShapeDtypeStruct(w.shape, F32)
    return pl.pallas_call(body, name=name, grid_spec=grid_spec, out_shape=[shp] * 4,
                          compiler_params=_params(1))(c, w, mine, other, m, v)


def add_chips(name, slots, parts, chip, axes):
    outs = []
    for idx, (x, own, ax) in enumerate(zip(slots, parts, axes)):
        n, shape2d = x.shape[0], x.shape[1:]
        steps = shape2d[ax] // SPLIT_TILE
        assert shape2d[ax] == steps * SPLIT_TILE

        def body(me_ref, *refs, n=n):
            o_ref = refs[n + 1]
            acc = None
            for t in range(n):
                term = jnp.where(me_ref[0] == t, refs[n][...], refs[t][...]).astype(F32)
                acc = term if acc is None else acc + term
            o_ref[...] = acc

        def filled(t, n=n):
            return lambda i, me_ref: (jnp.where(me_ref[0] == t, (t + 1) % n, t), i)

        grid_spec = pltpu.PrefetchScalarGridSpec(
            num_scalar_prefetch=1, grid=(steps,),
            in_specs=[_tiled(shape2d, ax, 1, filled(t)) for t in range(n)]
            + [_tiled(shape2d, ax, 1, lambda i, me_ref: (me_ref[0], i))],
            out_specs=_tiled(shape2d, ax, 0, lambda i, me_ref: (i,)))
        outs.append(pl.pallas_call(
            body, name=f"{name}_{idx}", grid_spec=grid_spec, out_shape=jax.ShapeDtypeStruct(shape2d, F32),
            compiler_params=_params(1))(chip, *([x] * n), own))
    return outs


def add_pair(name, gs, rs, c, axes):
    outs = []
    for idx, (g, r, ax) in enumerate(zip(gs, rs, axes)):
        nb = r.shape[0]
        steps = r.shape[1 + ax] // SPLIT_TILE
        assert r.shape[1 + ax] == steps * SPLIT_TILE

        def body(c_ref, g_ref, r_ref, o_ref):
            o_ref[...] = (g_ref[...] + r_ref[...]).astype(BF16)

        grid_spec = pltpu.PrefetchScalarGridSpec(
            num_scalar_prefetch=1, grid=(nb, steps),
            in_specs=[_tiled(g.shape[1:], ax, 1, lambda b, i, c_ref: (b, c_ref[0] * steps + i)),
                      _tiled(r.shape[1:], ax, 1, lambda b, i, c_ref: (b, i))],
            out_specs=_tiled(r.shape[1:], ax, 1, lambda b, i, c_ref: (b, i)))
        outs.append(pl.pallas_call(
            body, name=f"{name}_{idx}", grid_spec=grid_spec, out_shape=jax.ShapeDtypeStruct(r.shape, BF16),
            compiler_params=_params(2))(c, g, r))
    return outs


def _place():
    x, y, c = lax.axis_index("x"), lax.axis_index("y"), lax.axis_index("c")
    return x, y, c, [(1 - x, y), (x, 1 - y), (1 - x, 1 - y)]


def _remote(src, dst, send_sem, recv_sem, dev):
    return pltpu.make_async_remote_copy(src_ref=src, dst_ref=dst, send_sem=send_sem, recv_sem=recv_sem,
                                        device_id=dev, device_id_type=MESH)


def _half(ref, lead, ax, which):
    size = ref.shape[len(lead) + ax] // 2
    part = pl.ds(which * size, size)
    return ref.at[(*lead, part, slice(None)) if ax == 0 else (*lead, slice(None), part)]


def gather_weights(shards, axes):
    n = len(shards)

    def body(*refs):
        ins, outs = refs[:n], refs[n:2 * n]
        ici_s, ici_r, d2d_s, d2d_r = refs[2 * n:]
        x, y, c, chips = _place()
        me = 2 * x + y
        sends, passes = [], []
        for w in range(n):
            for j, (ox, oy) in enumerate(chips):
                cp = _remote(_half(ins[w], (), axes[w], c), _half(outs[w], (me,), axes[w], c),
                             ici_s.at[3 * w + j], ici_r.at[3 * w + j], (ox, oy, c))
                cp.start()
                sends.append(cp)
        for w in range(n):
            for j, (ox, oy) in enumerate(chips):
                landed = _half(outs[w], (2 * ox + oy,), axes[w], c)
                _remote(landed, landed, ici_s.at[3 * w + j], ici_r.at[3 * w + j], (ox, oy, c)).wait_recv()
                cp = _remote(landed, landed, d2d_s.at[3 * w + j], d2d_r.at[3 * w + j], (x, y, 1 - c))
                cp.start()
                passes.append(cp)
        for w in range(n):
            for j, (ox, oy) in enumerate(chips):
                other = _half(outs[w], (2 * ox + oy,), axes[w], 1 - c)
                _remote(other, other, d2d_s.at[3 * w + j], d2d_r.at[3 * w + j], (x, y, 1 - c)).wait_recv()
        for cp in sends + passes:
            cp.wait_send()

    return pl.pallas_call(
        body, name="gather_weights", in_specs=[ANY] * n, out_specs=[ANY] * n,
        out_shape=[jax.ShapeDtypeStruct((4,) + s.shape, s.dtype) for s in shards],
        scratch_shapes=[pltpu.SemaphoreType.DMA((3 * n,))] * 4,
    )(*shards)


HBM = pl.BlockSpec(memory_space=pltpu.HBM)
SEM = pl.BlockSpec(memory_space=pltpu.SEMAPHORE)
DATAFLOW = pltpu.SideEffectType.DATAFLOW_SIDE_EFFECTING


def _hbm(a):
    return pltpu.with_memory_space_constraint(a, pltpu.HBM)


class SplitExchange:
    def __init__(self, name, srcs, zone_shapes, n_sems, plan):
        self.name, self.n, self.n_sems, self.plan = name, len(srcs), n_sems, plan
        self.srcs = [_hbm(s) for s in srcs]
        self.zones = [_hbm(lax.empty(shape, s.dtype)) for shape, s in zip(zone_shapes, srcs)]

    def start(self, after):
        n, n_after = self.n, len(after)

        def body(*refs):
            ins, lands = refs[:n], refs[n:2 * n]
            send, recv, token = refs[2 * n + n_after], refs[2 * n + n_after + 1], refs[-1]
            for src, dst, si, ri, dev in self.plan(ins, lands)[0]:
                _remote(src, dst, send.at[si], recv.at[ri], dev).start()
            token[...] = jnp.zeros_like(token)

        res = pl.pallas_call(
            body, name=f"{self.name}_start", in_specs=[HBM] * (2 * n) + [ANY] * n_after,
            out_specs=[SEM, SEM] + [HBM] * (2 * n) + [pl.BlockSpec(memory_space=pltpu.VMEM)],
            out_shape=[pltpu.SemaphoreType.DMA((self.n_sems,)), pltpu.SemaphoreType.DMA((self.n_sems,))]
            + [pltpu.HBM(a.shape, a.dtype) for a in self.srcs + self.zones] + [jax.ShapeDtypeStruct((8, LANES), F32)],
            input_output_aliases={i: 2 + i for i in range(2 * n)},
            compiler_params=pltpu.CompilerParams(has_side_effects=DATAFLOW),
        )(*self.srcs, *self.zones, *after)
        self.sems, self.srcs, self.zones = res[:2], list(res[2:2 + n]), list(res[2 + n:2 + 2 * n])
        return res[-1]

    def wait(self, after):
        n = self.n

        def body(*refs):
            ins, lands = refs[:n], refs[n:2 * n]
            send, recv = refs[2 * n], refs[2 * n + 1]
            sends, arrivals = self.plan(ins, lands)
            for src, _, si, _, dev in sends:
                _remote(src, src, send.at[si], recv.at[si], dev).wait_send()
            for landed, ri in arrivals:
                _remote(landed, landed, send.at[ri], recv.at[ri], _place()[:3]).wait_recv()

        res = pl.pallas_call(
            body, name=f"{self.name}_wait", in_specs=[HBM] * (2 * n) + [SEM, SEM, ANY], out_specs=[HBM] * (2 * n),
            out_shape=[pltpu.HBM(a.shape, a.dtype) for a in self.srcs + self.zones],
            input_output_aliases={i: i for i in range(2 * n)},
            compiler_params=pltpu.CompilerParams(has_side_effects=DATAFLOW),
        )(*self.srcs, *self.zones, *self.sems, after)
        self.srcs = list(res[:n])
        return list(res[n:])


def split_gather(shards):
    def plan(ins, lands):
        x, y, c, chips = _place()
        sends, arrivals = [], []
        for w in range(len(ins)):
            for j, (ox, oy) in enumerate(chips):
                for k in range(2):
                    base = 2 * (3 * w + j)
                    sends.append((_half(ins[w], (), 0, c), _half(lands[w], (2 * x + y,), 0, c), base + k, base + c, (ox, oy, k)))
                    arrivals.append((_half(lands[w], (2 * ox + oy,), 0, k), base + k))
        return sends, arrivals

    return SplitExchange("gather", shards, [(4,) + s.shape for s in shards], 6 * len(shards), plan)


def split_pair_swap(name, grads, axes):
    def plan(ins, lands):
        x, y, c, _ = _place()
        sends = [(_half(ins[w], (slice(None),), axes[w], 1 - c), lands[w], w, w, (x, y, 1 - c)) for w in range(len(ins))]
        return sends, [(lands[w], w) for w in range(len(ins))]

    halved = [tuple(d // 2 if i == 1 + ax else d for i, d in enumerate(g.shape)) for g, ax in zip(grads, axes)]
    return SplitExchange(name, grads, halved, len(grads), plan)


def split_chip_exchange(name, parts):
    def plan(ins, lands):
        x, y, c, chips = _place()
        sends, arrivals = [], []
        for w in range(len(ins)):
            for j, (ox, oy) in enumerate(chips):
                sends.append((ins[w].at[2 * ox + oy], lands[w].at[2 * x + y], 3 * w + j, 3 * w + j, (ox, oy, c)))
                arrivals.append((lands[w].at[2 * ox + oy], 3 * w + j))
        return sends, arrivals

    return SplitExchange(name, parts, [p.shape for p in parts], 3 * len(parts), plan)


def split_pair_send(halves):
    def plan(ins, lands):
        x, y, c, _ = _place()
        return ([(ins[w], lands[w], w, w, (x, y, 1 - c)) for w in range(len(ins))],
                [(lands[w], w) for w in range(len(ins))])

    return SplitExchange("pair_send", halves, [h.shape for h in halves], len(halves), plan)


def pair_send(halves):
    n = len(halves)

    def body(*refs):
        ins, outs = refs[:n], refs[n:2 * n]
        send, recv = refs[2 * n:]
        x, y, c, _ = _place()
        cps = [_remote(ins[w], outs[w], send.at[w], recv.at[w], (x, y, 1 - c)) for w in range(n)]
        for cp in cps:
            cp.start()
        for cp in cps:
            cp.wait_recv()
        for cp in cps:
            cp.wait_send()

    return pl.pallas_call(
        body, name="pair_send", in_specs=[ANY] * n, out_specs=[ANY] * n,
        out_shape=[jax.ShapeDtypeStruct(h.shape, h.dtype) for h in halves],
        scratch_shapes=[pltpu.SemaphoreType.DMA((n,))] * 2,
    )(*halves)


def all_reduce_small(name, vec, after=()):
    rows = vec.shape[0]

    def body(v_ref, *refs):
        o_ref, buf, send, recv = refs[len(after):]
        x, y, c, _ = _place()
        me = 4 * x + 2 * y + c
        buf[me] = v_ref[...]
        cps = []
        for k in range(1, 8):
            kx, ky, kc = (k >> 2) & 1, (k >> 1) & 1, k & 1
            peer = (x if kx == 0 else 1 - x, y if ky == 0 else 1 - y, c if kc == 0 else 1 - c)
            cp = _remote(v_ref, buf.at[me], send.at[k - 1], recv.at[k - 1], peer)
            cp.start()
            cps.append(cp)
        for k in range(1, 8):
            kx, ky, kc = (k >> 2) & 1, (k >> 1) & 1, k & 1
            px, py, pc = (x if kx == 0 else 1 - x, y if ky == 0 else 1 - y, c if kc == 0 else 1 - c)
            slot = buf.at[4 * px + 2 * py + pc]
            _remote(slot, slot, send.at[k - 1], recv.at[k - 1], (px, py, pc)).wait_recv()
        for cp in cps:
            cp.wait_send()
        acc = buf[0]
        for d in range(1, 8):
            acc = acc + buf[d]
        o_ref[...] = acc

    vm = pl.BlockSpec(memory_space=pltpu.VMEM)
    return pl.pallas_call(
        body, name=name, in_specs=[vm] + [ANY] * len(after), out_specs=vm, out_shape=jax.ShapeDtypeStruct(vec.shape, F32),
        scratch_shapes=[pltpu.VMEM((8, rows, LANES), F32), pltpu.SemaphoreType.DMA((7,)), pltpu.SemaphoreType.DMA((7,))],
    )(vec, *after)


class NoExchange:
    def __init__(self, late):
        self.late = late

    def late_weights(self, after):
        return self.late

    def reduce_start(self, grads):
        return jnp.zeros((8, LANES), F32)

    def reduce_exchange(self, after):
        return jnp.zeros((8, LANES), F32)

    def reduce_finish(self, after):
        return jnp.zeros((8, LANES), F32)

    def input_grad_start(self, dw_main, dw_small):
        return jnp.zeros((8, LANES), F32)

    def input_grad_exchange(self, after):
        return jnp.zeros((8, LANES), F32)


def local_step(x2, tgt2, g1, g2, gdn_ng, qn_g, kn_g, p1, p2, conv_w, wt_main, wt_small, hooks, nseq, seq):
    rows, dm = x2.shape
    wide = NH * LANES
    row = lambda a, off=0, w=None: (a, "row", off, a.shape[1] if w is None else w)
    rowh = lambda a, off=0, w=LANES: (a, "rowh", off, w)
    par = lambda a: (a, "par", 0, a.shape[1])
    parh = lambda a, off=0: (a, "parh", off, LANES)
    o_row = lambda w, dt: (w, "row", w, dt)
    o_rowh = lambda dt, tw=wide, w=LANES: (tw, "rowh", w, dt)

    u, = ew_fwd("rms1", f_rms, [row(x2), par(g1)], [o_row(dm, BF16)], rows)
    proj = matmul("mm_in", u, wt_main, "nt", F32)
    sp = matmul("mm_in_small", u, wt_small, "nt", F32)
    so, = ew_fwd("small", f_small, [row(sp), par(p1), par(p2)], [o_row(LANES, F32)], rows)
    cs = cumsum_time("cumsum", so, nseq, seq, False)
    gb, bb, cb = ew_fwd("bcast", f_bcast, [row(so), row(cs)], [o_rowh(F32)] * 3, rows, NH)
    ct = transpose_time("c_time_major", cs, nseq, seq)
    conv = {}
    for mode, off in (("q", 0), ("k", NH), ("v", 2 * NH)):
        conv[mode], = ew_fwd(f"conv_{mode}", make_f_conv(mode), [rowh(proj, off), parh(conv_w, off)], [o_rowh(F32)],
                             rows, NH, seq, "hi", CONV_HEADS)
    val, kcum, attn, qdec, kdec, t_inv = gdn_a_fwd(conv["q"], conv["k"], conv["v"], gb, bb, rows)
    o_a, snaps = gdn_b_fwd(val, kcum, attn, qdec, kdec, gb, nseq, seq)
    ya_in, = ew_fwd("gdn_post", f_post, [rowh(o_a), rowh(proj, 3 * NH), par(gdn_ng)], [o_rowh(BF16)], rows, NH)
    fqn, = ew_fwd("fox_qn", f_rms, [rowh(proj, FOX_Q), par(qn_g)], [o_rowh(BF16)], rows, NH)
    fkn, = ew_fwd("fox_kn", f_rms, [rowh(proj, FOX_K), par(kn_g)], [o_rowh(BF16)], rows, NH)
    o_b, o_b16, lse = fox_fwd(fqn, fkn, proj, ct, nseq, seq)
    p_a, p_b, w_o, w_u, w_d = hooks.late_weights(o_a)
    y_a = matmul("mm_pa", ya_in, p_a, "nn", F32, tn=1024)
    y_b = matmul("mm_pb", o_b16, p_b, "nn", F32, tn=1024)
    gates = [row(proj, 7, dm), row(proj, 8, dm)]
    merged, = ew_fwd("merge", f_merge, gates + [row(y_a), row(y_b)], [o_row(dm, BF16)], rows)
    hres = matmul("mm_out", merged, w_o, "nn", F32, add=x2, tn=1024)
    hn, = ew_fwd("rms2", f_rms, [row(hres), par(g2)], [o_row(dm, BF16)], rows)
    up_blocks = w_u.shape[0]
    act, relu2 = matmul("mm_up", hn, w_u, "nn", F32, col_blocks=up_blocks, out_dtypes=[F32, BF16],
                        epilogue=lambda r: [r, jnp.maximum(r, 0.0) * jnp.maximum(r, 0.0)])
    out = matmul("mm_down", relu2, w_d, "nn", F32, add=hres, tn=1024)
    dout, dout16, loss_acc = loss_head(out, tgt2, rows, dm)

    d_act = matmul("mm_d_act", dout16, w_d, "nt", BF16, extras=[act], epilogue=lambda r, a: [2.0 * jnp.maximum(a, 0.0) * r])
    dw_d = matmul("mm_dw_down", relu2, dout16, "tn", F32, tn=1024)
    dw_u = matmul("mm_dw_up", hn, d_act, "tn", F32, col_blocks=up_blocks)
    d_hn = matmul("mm_d_hn", d_act, w_u, "nt", F32, col_blocks=up_blocks)
    dh, dh16, dg2 = ew_bwd("rms2_b", f_rms, [row(hres), par(g2)], [(row(d_hn),)], [row(dout)],
                           lambda g, e: [g[0] + e[0], g[0] + e[0], g[1]],
                           [((rows, dm), "row", dm, F32, None), ((rows, dm), "row", dm, BF16, None), ((1, dm), "par", dm, F32, "all")], rows)
    d_merged = matmul("mm_d_merged", dh16, w_o, "nt", F32, tn=1024)
    dw_o = matmul("mm_dw_out", merged, dh16, "tn", F32, tn=1024)
    seg16 = ((rows, dm), "row", dm, BF16, None)
    d_ga16, d_gb16, d_ya16, d_yb16 = ew_bwd("merge_b", f_merge, gates + [row(y_a), row(y_b)], [(row(d_merged),)], [],
                                            lambda g, e: list(g), [seg16] * 4, rows)
    dp_a = matmul("mm_dp_a", ya_in, d_ya16, "tn", F32, tn=1024)
    d_ya_in = matmul("mm_d_ya_in", d_ya16, p_a, "nt", F32, tn=1024)
    dp_b = matmul("mm_dp_b", o_b16, d_yb16, "tn", F32, tn=1024)
    d_ob = matmul("mm_d_ob", d_yb16, p_b, "nt", F32, tn=1024)
    token = hooks.reduce_start(dict(p_a=dp_a, p_b=dp_b, w_o=dw_o, w_u=dw_u, w_d=dw_d))
    gdn_ng_t = gdn_ng + token[0, 0]
    h32 = ((rows, wide), "rowh", LANES, F32, None)
    h16 = ((rows, wide), "rowh", LANES, BF16, None)
    gain = ((1, LANES), "par", LANES, F32, "all")
    d_oa, d_z16, d_gdn_ng = ew_bwd("gdn_post_b", f_post, [rowh(o_a), rowh(proj, 3 * NH), par(gdn_ng_t)], [(rowh(d_ya_in),)], [],
                                   lambda g, e: list(g), [h32, h16, gain], rows, NH)
    dval, dkc, dat, dqd, dkd, dgb_b = gdn_b_bwd(val, kcum, attn, qdec, kdec, gb, snaps, d_oa, nseq, seq)
    d_cq, d_ck, d_cv, d_gb, d_bb = gdn_a_bwd(conv["q"], conv["k"], conv["v"], gb, bb, t_inv, dval, dkc, dat, dqd, dkd, dgb_b, rows)
    token = hooks.reduce_exchange(d_cq)
    conv_w_t = conv_w + token[0, 0]
    d_pre, d_conv = {}, {}
    tap = ((4, wide), "parh", LANES, F32, "inner")
    for mode, off, ctg in (("q", 0, d_cq), ("k", NH, d_ck), ("v", 2 * NH, d_cv)):
        d_pre[mode], d_conv[mode] = ew_bwd(f"conv_{mode}_b", make_f_conv(mode), [rowh(proj, off), parh(conv_w_t, off)],
                                           [(rowh(ctg),)], [], lambda g, e: list(g), [h16, tap], rows, NH, seq, "hi", CONV_HEADS)
    delta, = ew_fwd("fox_delta", f_delta, [rowh(d_ob), rowh(o_b)], [o_rowh(F32)], rows, NH, after=[token])
    d_fqn, d_cq_b = fox_dq(fqn, fkn, proj, ct, d_ob, lse, delta, nseq, seq)
    d_fkn, d_fv16, d_ck_b = fox_dkv(fqn, fkn, proj, cb, d_ob, lse, delta, nseq, seq)
    token = hooks.reduce_finish(d_fkn)
    qn_g_t, kn_g_t = qn_g + token[0, 0], kn_g + token[0, 0]
    d_fq16, d_qn_g = ew_bwd("fox_qn_b", f_rms, [rowh(proj, FOX_Q), par(qn_g_t)], [(rowh(d_fqn),)], [], lambda g, e: list(g),
                            [h16, gain], rows, NH)
    d_fk16, d_kn_g = ew_bwd("fox_kn_b", f_rms, [rowh(proj, FOX_K), par(kn_g_t)], [(rowh(d_fkn),)], [], lambda g, e: list(g),
                            [h16, gain], rows, NH)
    narrow = ((rows, LANES), "row", LANES, F32, None)
    d_so, d_cs = ew_bwd("bcast_b", f_bcast, [row(so), row(cs)], [(rowh(d_gb),), (rowh(d_bb),), (rowh(d_cq_b), rowh(d_ck_b))], [],
                        lambda g, e: list(g), [narrow, narrow], rows, NH)
    d_logf = cumsum_time("cumsum_b", d_cs, nseq, seq, True)
    vec = ((1, LANES), "par", LANES, F32, "all")
    d_sp16, d_p1, d_p2 = ew_bwd("small_b", f_small, [row(sp), par(p1), par(p2)], [(row(d_so), row(d_logf))], [],
                                lambda g, e: list(g), [((rows, LANES), "row", LANES, BF16, None), vec, vec], rows)
    d_proj16 = jnp.concatenate([d_pre["q"], d_pre["k"], d_pre["v"], d_z16, d_fq16, d_fk16, d_fv16, d_ga16, d_gb16], axis=1)
    dw_main = matmul("mm_dw_main", d_proj16, u, "tn", F32)
    dw_small = matmul("mm_dw_small", d_sp16, u, "tn", F32)
    wt_small_t = wt_small + hooks.input_grad_start(dw_main, dw_small)[0, 0].astype(BF16)
    d_u = matmul("mm_d_u_small", d_sp16, wt_small_t, "nn", F32)
    d_u = matmul("mm_d_u", d_proj16, wt_main, "nn", F32, add=d_u)
    g1 = g1 + hooks.input_grad_exchange(d_u)[0, 0]
    dx, dg1 = ew_bwd("rms1_b", f_rms, [row(x2), par(g1)], [(row(d_u),)], [row(dh)], lambda g, e: [g[0] + e[0], g[1]],
                     [((rows, dm), "row", dm, F32, None), ((1, dm), "par", dm, F32, "all")], rows)
    d_conv_w = jnp.concatenate([d_conv["q"], d_conv["k"], d_conv["v"]], axis=1)
    return dict(loss_acc=loss_acc, dx=dx, g1=dg1, g2=dg2, gdn_ng=d_gdn_ng, qn=d_qn_g, kn=d_kn_g, p1=d_p1, p2=d_p2,
                conv=d_conv_w, w_main=dw_main, w_small=dw_small, p_a=dp_a, p_b=dp_b, w_o=dw_o, w_u=dw_u, w_d=dw_d)


_W = NH * LANES
_A0, _A1 = 4 * _W, 4 * _W + 2 * NH
_B0, _B1 = _A1 + 3 * _W, _A1 + 3 * _W + NH
N_IN = _B1 + 2 * _W


def _split_w_in(full_t):
    main = jnp.concatenate([full_t[:_A0], full_t[_A1:_B0], full_t[_B1:]], axis=0)
    small = jnp.concatenate([full_t[_A0:_A1], full_t[_B0:_B1], jnp.zeros((LANES - 3 * NH, full_t.shape[1]), full_t.dtype)], axis=0)
    return main, small


def _join_w_in(main, small):
    return jnp.concatenate([main[:_A0], small[:2 * NH], main[_A0:_A0 + 3 * _W], small[2 * NH:3 * NH], main[_A0 + 3 * _W:]], axis=0)


def _lanes(v, at=0):
    return jnp.pad(v.reshape(1, -1), ((0, 0), (at, LANES - at - v.size)))


def kernel(x, norm_mix_g, w_in, gdn_conv_w, gdn_a_log, gdn_dt_bias, gdn_norm_g, fox_q_norm_g, fox_k_norm_g, fox_f_bias, w_proj_gdn, w_proj_fox, w_out, norm_mlp_g, w_up, w_down, loss_target, m_norm_mix_g, m_w_in, m_gdn_conv_w, m_gdn_a_log, m_gdn_dt_bias, m_gdn_norm_g, m_fox_q_norm_g, m_fox_k_norm_g, m_fox_f_bias, m_w_proj_gdn, m_w_proj_fox, m_w_out, m_norm_mlp_g, m_w_up, m_w_down, v_norm_mix_g, v_w_in, v_gdn_conv_w, v_gdn_a_log, v_gdn_dt_bias, v_gdn_norm_g, v_fox_q_norm_g, v_fox_k_norm_g, v_fox_f_bias, v_w_proj_gdn, v_w_proj_fox, v_w_out, v_norm_mlp_g, v_w_up, v_w_down):
    nseq, seq, dm = x.shape
    rows = nseq * seq
    xi, yi, ci = lax.axis_index("x"), lax.axis_index("y"), lax.axis_index("c")
    chip = 2 * xi + yi
    conv_cols = gdn_conv_w.shape[2]

    tr = lambda a: jnp.swapaxes(a[0], 0, 1)
    big = [tr(w_in), w_proj_gdn[0], w_proj_fox[0], w_out[0], w_up[0], w_down[0]]
    axes = [1, 0, 0, 0, 0, 0]
    big16 = [w.astype(BF16) for w in big]
    fill = lambda got, own: lax.dynamic_update_index_in_dim(got, own, chip, 0)
    conv_slot = jnp.zeros((4, 4, conv_cols), F32).at[:, chip].set(jnp.where(ci == 0, gdn_conv_w[0], 0.0))
    conv_full = all_reduce_small("gather_conv", conv_slot.reshape(-1, LANES)).reshape(4, 4 * conv_cols)
    got_in, = gather_weights(big16[:1], axes[:1])
    wt_main, wt_small = _split_w_in(fill(got_in, big16[0]).reshape(-1, dm))
    core, chip_no = ci.reshape(1).astype(jnp.int32), chip.reshape(1).astype(jnp.int32)
    gather = split_gather(big16[1:])
    token = gather.start([got_in, conv_full])

    class Hooks:
        def late_weights(self, after):
            got = gather.wait(after)
            g_pa, g_pb, g_wo, w_u, g_wd = (fill(g, own) for g, own in zip(got, gather.srcs))
            return (*(g.reshape(-1, dm) for g in (g_pa, g_pb, g_wo)), w_u, g_wd.reshape(-1, dm))

        def reduce_start(self, grads):
            blocks = [grads["p_a"].reshape(4, -1, dm), grads["p_b"].reshape(4, -1, dm), grads["w_o"].reshape(4, -1, dm),
                      grads["w_u"], grads["w_d"].reshape(4, -1, dm)]
            self.swap = split_pair_swap("pair_swap_late", blocks, axes[1:])
            return self.swap.start([])

        def reduce_exchange(self, after):
            swapped = self.swap.wait(after)
            self.exchange = split_chip_exchange("chip_exchange_late", add_pair("add_pair_late", self.swap.srcs, swapped, core, axes[1:]))
            return self.exchange.start([])

        def reduce_finish(self, after):
            slots = self.exchange.wait(after)
            self.send = split_pair_send(add_chips("add_chips_late", slots, self.exchange.srcs, chip_no, axes[1:]))
            return self.send.start([])

        def input_grad_start(self, dw_main, dw_small):
            self.in_swap = split_pair_swap("pair_swap_in", [_join_w_in(dw_main, dw_small).reshape(4, -1, dm)], axes[:1])
            return self.in_swap.start([])

        def input_grad_exchange(self, after):
            swapped = self.in_swap.wait(after)
            self.in_exchange = split_chip_exchange("chip_exchange_in", add_pair("add_pair_in", self.in_swap.srcs, swapped, core, axes[:1]))
            return self.in_exchange.start([])

    hooks = Hooks()
    p1 = _lanes(gdn_dt_bias[0]) + _lanes(fox_f_bias[0], 2 * NH)
    p2 = _lanes(gdn_a_log[0])

    g = local_step(x.reshape(rows, dm), loss_target.reshape(rows, dm), norm_mix_g + token[0, 0], norm_mlp_g, gdn_norm_g,
                   fox_q_norm_g, fox_k_norm_g, p1, p2, conv_full, wt_main, wt_small, hooks, nseq, seq)

    others = hooks.send.wait(g["dx"])
    big_m = [tr(m_w_in), m_w_proj_gdn[0], m_w_proj_fox[0], m_w_out[0], m_w_up[0], m_w_down[0]]
    big_v = [tr(v_w_in), v_w_proj_gdn[0], v_w_proj_fox[0], v_w_out[0], v_w_up[0], v_w_down[0]]
    names = ["w_in", "w_proj_gdn", "w_proj_fox", "w_out", "w_up", "w_down"]
    big_res, big_grad = {}, {}
    for i in range(1, len(names)):
        big_grad[names[i]], *big_res[names[i]] = adamw_halves(f"adamw_{names[i]}", big[i], hooks.send.srcs[i - 1], others[i - 1],
                                                              big_m[i], big_v[i], core, axes[i])
    slots = hooks.in_exchange.wait(big_res[names[-1]][0])
    mine = add_chips("add_chips_in", slots, hooks.in_exchange.srcs, chip_no, axes[:1])
    res = adamw_halves("adamw_w_in", big[0], mine[0], pair_send(mine)[0], big_m[0], big_v[0], core, axes[0])
    big_grad["w_in"], *big_res["w_in"] = [jnp.swapaxes(r, 0, 1) for r in res]

    small_parts = [g["loss_acc"], g["g1"].reshape(8, LANES), g["g2"].reshape(8, LANES), g["gdn_ng"], g["qn"], g["kn"], g["p1"], g["p2"],
                   g["conv"].reshape(-1, LANES)]
    tiled = [jnp.pad(p, ((0, -p.shape[0] % 8), (0, 0))) for p in small_parts]
    red = all_reduce_small("reduce_small", jnp.concatenate(tiled, axis=0), slots)
    pos, red_parts = 0, []
    for p, t in zip(small_parts, tiled):
        red_parts.append(red[pos:pos + p.shape[0]])
        pos += t.shape[0]
    r_loss, r_g1, r_g2, r_gdn_ng, r_qn, r_kn, r_p1, r_p2, r_conv = red_parts
    loss = jnp.sum(r_loss)
    g_conv = lax.dynamic_slice_in_dim(r_conv.reshape(4, 4, conv_cols), chip, 1, axis=1).reshape(4, conv_cols)
    small_grads = [r_g1.reshape(1, dm), r_p2[:, :NH], r_p1[:, :NH], r_gdn_ng, r_qn, r_kn, r_p1[:, 2 * NH:3 * NH], r_g2.reshape(1, dm)]
    small_w = [norm_mix_g, gdn_a_log, gdn_dt_bias, gdn_norm_g, fox_q_norm_g, fox_k_norm_g, fox_f_bias, norm_mlp_g]
    small_m = [m_norm_mix_g, m_gdn_a_log, m_gdn_dt_bias, m_gdn_norm_g, m_fox_q_norm_g, m_fox_k_norm_g, m_fox_f_bias, m_norm_mlp_g]
    small_v = [v_norm_mix_g, v_gdn_a_log, v_gdn_dt_bias, v_gdn_norm_g, v_fox_q_norm_g, v_fox_k_norm_g, v_fox_f_bias, v_norm_mlp_g]

    def pack(parts):
        flat = jnp.concatenate([jnp.pad(p.reshape(-1), (0, -p.size % LANES)) for p in parts])
        return jnp.pad(flat, (0, -flat.size % (8 * LANES))).reshape(-1, LANES)

    packed = adamw("adamw_small", pack(small_w + [gdn_conv_w[0]]), pack(small_grads + [g_conv]),
                   pack(small_m + [m_gdn_conv_w[0]]), pack(small_v + [v_gdn_conv_w[0]]))

    def unpack(flat2d):
        flat, pos, res = flat2d.reshape(-1), 0, []
        for p in small_w + [gdn_conv_w[0]]:
            res.append(flat[pos:pos + p.size].reshape(p.shape))
            pos += p.size + (-p.size % LANES)
        return res

    s_delta, s_m, s_v = (unpack(a) for a in packed)

    order = ["norm_mix_g", "w_in", "gdn_conv_w", "gdn_a_log", "gdn_dt_bias", "gdn_norm_g", "fox_q_norm_g", "fox_k_norm_g",
             "fox_f_bias", "w_proj_gdn", "w_proj_fox", "w_out", "norm_mlp_g", "w_up", "w_down"]
    small_names = ["norm_mix_g", "gdn_a_log", "gdn_dt_bias", "gdn_norm_g", "fox_q_norm_g", "fox_k_norm_g", "fox_f_bias", "norm_mlp_g",
                   "gdn_conv_w"]
    small_idx = {nm: i for i, nm in enumerate(small_names)}
    shapes = dict(zip(order, (a.shape for a in (norm_mix_g, w_in, gdn_conv_w, gdn_a_log, gdn_dt_bias, gdn_norm_g, fox_q_norm_g,
                                                 fox_k_norm_g, fox_f_bias, w_proj_gdn, w_proj_fox, w_out, norm_mlp_g, w_up, w_down))))
    grads_out, delta_out, m_out, v_out = [], [], [], []
    for nm in order:
        if nm in big_res:
            d, mm, vv = big_res[nm]
            gr = big_grad[nm]
        else:
            i = small_idx[nm]
            gr = (small_grads + [g_conv])[i]
            d, mm, vv = s_delta[i], s_m[i], s_v[i]
        for lst, val in ((grads_out, gr), (delta_out, d), (m_out, mm), (v_out, vv)):
            lst.append(val.reshape(shapes[nm]))
    return (loss, g["dx"].reshape(x.shape), *grads_out, *delta_out, *m_out, *v_out)
```

```python
import functools

import jax
import jax.numpy as jnp
from jax import lax
from jax.experimental import pallas as pl
from jax.experimental.pallas import tpu as pltpu

F32 = jnp.float32
BF16 = jnp.bfloat16
LANES = 128
NH = 8
EPS = 1e-6
GDN_CHUNK = 64
GDN_ROWS = 256
GDN_BASE = 16
ROW_TILE = 512
CONV_HEADS = 2
ATT_TILE = 512
NEG = -1e30
VMEM_LIMIT_BYTES = 48 * 1024 * 1024
HI = lax.Precision.HIGHEST
LO = lax.Precision.DEFAULT
MESH = pl.DeviceIdType.MESH
ANY = pl.BlockSpec(memory_space=pl.ANY)

ADAM_LR, ADAM_B1, ADAM_B2, ADAM_EPS, ADAM_WD, ADAM_STEP = 0.001, 0.9, 0.999, 1e-08, 0.01, 10


def _params(n_grid):
    return pltpu.CompilerParams(dimension_semantics=("arbitrary",) * n_grid,
                                vmem_limit_bytes=VMEM_LIMIT_BYTES)


def _dot(a, b, dims, precision=None):
    dn = {"nn": (((1,), (0,)), ((), ())), "nt": (((1,), (1,)), ((), ())), "tn": (((0,), (0,)), ((), ()))}[dims]
    return lax.dot_general(a, b, dn, precision=precision, preferred_element_type=F32)


def _iota(shape, dim):
    return lax.broadcasted_iota(jnp.int32, shape, dim)


def _split(x, parts):
    out = []
    for _ in range(parts - 1):
        hi = x.astype(BF16)
        out.append(hi)
        x = x - hi.astype(F32)
    return out + [x.astype(BF16)]


def _dot_mask(mask, b, dims):
    m16 = mask.astype(BF16)
    b1, b2, b3 = _split(b, 3)
    return _dot(m16, b1, dims) + (_dot(m16, b2, dims) + _dot(m16, b3, dims))


@jax.custom_vjp
def mm_mask(mask, b):
    return _dot_mask(mask, b, "nn")


mm_mask.defvjp(lambda mask, b: (_dot_mask(mask, b, "nn"), mask),
               lambda mask, g: (jnp.zeros_like(mask), _dot_mask(mask, g, "tn")))


def matmul(name, a, b, dims, out_dtype, add=None, tm=1024, tn=1024, tk=512, col_blocks=None,
           extras=(), epilogue=None, out_dtypes=None, k_part=None, after=()):
    if col_blocks and dims != "tn":
        nb, b_rows, bw = b.shape
        b_shape = (b_rows, nb * bw)
    else:
        b_shape = b.shape
    if dims == "nn":
        (m, k), (_, n) = a.shape, b_shape
    elif dims == "nt":
        (m, k), (n, _) = a.shape, b_shape
    else:
        (k, m), (_, n) = a.shape, b_shape
    if k <= 1024:
        tk = k
    tm, tn, tk = min(tm, m), min(tn, n), min(tk, k)
    assert m % tm == 0 and n % tn == 0 and k % tk == 0, (name, m, n, k)
    k0, nk = (0, k // tk) if k_part is None else (k_part[0] * (k // tk // k_part[1]), k // tk // k_part[1])
    assert k_part is None or (dims == "nn" and not col_blocks and (k // tk) % k_part[1] == 0)
    a_spec = pl.BlockSpec((tk, tm), lambda i, j, kk: (kk, i)) if dims == "tn" else pl.BlockSpec((tm, tk), lambda i, j, kk: (i, kk + k0))
    b_spec = pl.BlockSpec((tn, tk), lambda i, j, kk: (j, kk)) if dims == "nt" else pl.BlockSpec((tk, tn), lambda i, j, kk: (kk + k0, j))
    o_spec = pl.BlockSpec((tm, tn), lambda i, j, kk: (i, j))
    out_shape = (m, n)
    if col_blocks and dims == "nn":
        per = bw // tn
        assert bw % tn == 0
        b_spec = pl.BlockSpec((None, tk, tn), lambda i, j, kk: (j // per, kk, j % per))
    elif col_blocks and dims == "nt":
        per = bw // tk
        assert bw % tk == 0
        b_spec = pl.BlockSpec((None, tn, tk), lambda i, j, kk: (kk // per, j, kk % per))
    elif col_blocks:
        bw = n // col_blocks
        per = bw // tn
        assert bw % tn == 0 and add is None
        o_spec = pl.BlockSpec((None, tm, tn), lambda i, j, kk: (j // per, i, j % per))
        out_shape = (col_blocks, m, bw)
    extras = list(extras) + ([add] if add is not None else [])
    if add is not None:
        assert epilogue is None
        epilogue = lambda r, *e: [r + e[-1]]
    out_dtypes = [out_dtype] if epilogue is None or out_dtypes is None else list(out_dtypes)
    n_ex, n_out = len(extras), len(out_dtypes)

    def body(*refs):
        a_ref, b_ref = refs[0], refs[1]
        ex_refs, o_refs = refs[2:2 + n_ex], refs[2 + n_ex + len(after):2 + n_ex + len(after) + n_out]

        def finish(r):
            res = [r] if epilogue is None else epilogue(r, *[e[...] for e in ex_refs])
            for o_ref, v in zip(o_refs, res):
                o_ref[...] = v.astype(o_ref.dtype)

        if nk == 1:
            finish(_dot(a_ref[...], b_ref[...], dims))
            return
        acc_ref = refs[-1]
        kk = pl.program_id(2)

        @pl.when(kk == 0)
        def _():
            acc_ref[...] = jnp.zeros_like(acc_ref)

        acc_ref[...] += _dot(a_ref[...], b_ref[...], dims)

        @pl.when(kk == nk - 1)
        def _():
            finish(acc_ref[...])

    res = pl.pallas_call(
        body, name=name, grid=(m // tm, n // tn, nk), in_specs=[a_spec, b_spec] + [o_spec] * n_ex + [ANY] * len(after),
        out_specs=[o_spec] * n_out, out_shape=[jax.ShapeDtypeStruct(out_shape, dt) for dt in out_dtypes],
        scratch_shapes=[pltpu.VMEM((tm, tn), F32)] if nk > 1 else [], compiler_params=_params(3),
    )(a, b, *extras, *after)
    return res[0] if n_out == 1 else res


def _ew_spec(kind, off, width, tb, hp, order, shape=None):
    def ih(g0, g1):
        return (g0, g1) if order == "ih" else (g1, g0)

    assert off % hp == 0 or kind in ("row", "par")
    if kind == "row":
        return pl.BlockSpec((tb, width), lambda g0, g1: (ih(g0, g1)[0], off))
    if kind == "rowh":
        return pl.BlockSpec((tb, hp * width), lambda g0, g1: (ih(g0, g1)[0], ih(g0, g1)[1] + off // hp))
    if kind == "par":
        return pl.BlockSpec(shape, lambda g0, g1: (0, 0))
    if kind == "parh":
        return pl.BlockSpec((shape[0], hp * width), lambda g0, g1: (0, ih(g0, g1)[1] + off // hp))
    raise ValueError(kind)


def _ew_grid(rows, tb, nh, hp, order):
    assert nh % hp == 0 and rows % tb == 0
    return (rows // tb, nh // hp) if order == "ih" else (nh // hp, rows // tb)


def _ew_load(ref, kind, width, hh):
    if kind in ("row", "par"):
        return ref[...].astype(F32)
    return ref[:, hh * width:(hh + 1) * width].astype(F32)


def ew_fwd(name, f, ins, outs, rows, nh=1, tb=ROW_TILE, order="ih", hp=None, after=()):
    hp = nh if hp is None else hp
    n_in = len(ins)

    def body(*refs):
        hb = pl.program_id(1) if order == "ih" else pl.program_id(0)
        for hh in range(hp):
            h = hh if hp == nh else hb * hp + hh
            vals = [_ew_load(r, kd, w, hh) for r, (_, kd, _, w) in zip(refs[:n_in], ins)]
            res = f(h, *vals)
            for r, v, (_, kd, w, _) in zip(refs[n_in + len(after):], res, outs):
                if kd == "row":
                    assert hp == 1
                    r[...] = v.astype(r.dtype)
                else:
                    r[:, hh * w:(hh + 1) * w] = v.astype(r.dtype)

    in_specs = [_ew_spec(kd, off, w, tb, hp, order, a.shape) for (a, kd, off, w) in ins]
    out_specs = [_ew_spec(kd, 0, w, tb, hp, order) for (_, kd, w, _) in outs]
    out_shape = [jax.ShapeDtypeStruct((rows, tw), dt) for (tw, _, _, dt) in outs]
    return pl.pallas_call(
        body, name=name, grid=_ew_grid(rows, tb, nh, hp, order), in_specs=in_specs + [ANY] * len(after), out_specs=out_specs,
        out_shape=out_shape, compiler_params=_params(2),
    )(*[a for (a, _, _, _) in ins], *after)


def ew_bwd(name, f, ins, cts, extras, emit, outs, rows, nh=1, tb=ROW_TILE, order="ih", hp=None):
    hp = nh if hp is None else hp
    n_in = len(ins)
    flat_cts = [d for group in cts for d in group]
    n_ct, n_ex = len(flat_cts), len(extras)

    def body(*refs):
        g0, g1 = pl.program_id(0), pl.program_id(1)
        hb = g1 if order == "ih" else g0
        out_refs = refs[n_in + n_ct + n_ex:]
        shared = [None] * len(outs)

        def store(r, v, first, sl=None):
            def put(val, add):
                if sl is None:
                    r[...] = (r[...] + val if add else val).astype(r.dtype)
                else:
                    r[:, sl] = (r[:, sl] + val if add else val).astype(r.dtype)

            if first is None:
                put(v, False)
            else:
                pl.when(first)(lambda: put(v, False))
                pl.when(jnp.logical_not(first))(lambda: put(v, True))

        for hh in range(hp):
            h = hh if hp == nh else hb * hp + hh
            vals = [_ew_load(r, kd, w, hh) for r, (_, kd, _, w) in zip(refs[:n_in], ins)]
            ct_refs = list(zip(refs[n_in:n_in + n_ct], flat_cts))
            ct_vals, pos = [], 0
            for group in cts:
                v = None
                for r, (_, kd, _, w) in ct_refs[pos:pos + len(group)]:
                    t = _ew_load(r, kd, w, hh)
                    v = t if v is None else v + t
                pos += len(group)
                ct_vals.append(v)
            ex_vals = [_ew_load(r, kd, w, hh) for r, (_, kd, _, w) in zip(refs[n_in + n_ct:n_in + n_ct + n_ex], extras)]
            _, vjp = jax.vjp(lambda *a: f(h, *a), *vals)
            res = emit(vjp(tuple(ct_vals)), ex_vals)
            for idx, (r, v, (_, kd, w, _, acc)) in enumerate(zip(out_refs, res, outs)):
                if kd in ("row", "par"):
                    shared[idx] = v if shared[idx] is None else shared[idx] + v
                else:
                    store(r, v, (g1 == 0) if acc == "inner" else None, slice(hh * w, (hh + 1) * w))
        for idx, (r, (_, kd, _, _, acc)) in enumerate(zip(out_refs, outs)):
            if kd in ("row", "par"):
                assert acc == "all" or hp == nh
                store(r, shared[idx], jnp.logical_and(g0 == 0, g1 == 0) if acc == "all" else None)

    operands = list(ins) + flat_cts + list(extras)
    in_specs = [_ew_spec(kd, off, w, tb, hp, order, a.shape) for (a, kd, off, w) in operands]
    out_specs = [_ew_spec(kd, 0, w, tb, hp, order, shp) for (shp, kd, w, _, _) in outs]
    out_shape = [jax.ShapeDtypeStruct(shp, dt) for (shp, _, _, dt, _) in outs]
    return pl.pallas_call(
        body, name=name, grid=_ew_grid(rows, tb, nh, hp, order), in_specs=in_specs, out_specs=out_specs,
        out_shape=out_shape, compiler_params=_params(2),
    )(*[a for (a, _, _, _) in operands])


def f_rms(h, x, g):
    r = lax.rsqrt(jnp.mean(x * x, axis=-1, keepdims=True) + EPS)
    return (x * r * g,)


def _softplus(z):
    return jnp.maximum(z, 0.0) + jnp.log1p(jnp.exp(-jnp.abs(z)))


def f_small(h, sp, p1, p2):
    lane = _iota(sp.shape, 1)
    z = sp + p1
    g = -jnp.exp(p2) * _softplus(z)
    beta = jax.nn.sigmoid(z)
    logf = -_softplus(-z)
    return (jnp.where(lane < NH, g, jnp.where(lane < 2 * NH, beta, jnp.where(lane < 3 * NH, logf, 0.0))),)


def _pick(x, lane_id):
    lane = _iota(x.shape, 1)
    col = jnp.sum(jnp.where(lane == lane_id, x, 0.0), axis=1, keepdims=True)
    return jnp.broadcast_to(col, x.shape)


def f_bcast(h, so, cs):
    return _pick(so, h), _pick(so, h + NH), _pick(cs, h + 2 * NH)


def _shift_down(s):
    def down(x):
        return jnp.where(_iota(x.shape, 0) >= s, pltpu.roll(x, s, 0), 0.0)

    def up(g):
        n = g.shape[0]
        return jnp.where(_iota(g.shape, 0) < n - s, pltpu.roll(g, n - s, 0), 0.0)

    @jax.custom_vjp
    def shift(x):
        return down(x)

    shift.defvjp(lambda x: (down(x), None), lambda _, g: (up(g),))
    return shift


def _silu(x):
    return x * jax.nn.sigmoid(x)


def make_f_conv(mode):
    sh1, sh2, sh3 = _shift_down(1), _shift_down(2), _shift_down(3)

    def f(h, x, w):
        sub = _iota(w.shape, 0)

        def tap(i):
            return jnp.sum(jnp.where(sub == i, w, 0.0), axis=0, keepdims=True)

        y = sh3(x) * tap(0)
        y = y + sh2(x) * tap(1)
        y = y + sh1(x) * tap(2)
        y = y + x * tap(3)
        s = _silu(y)
        if mode == "v":
            return (s,)
        n = s * lax.rsqrt(jnp.sum(s * s, axis=-1, keepdims=True) + EPS)
        if mode == "q":
            n = n * (LANES ** -0.5)
        return (n,)

    return f


def f_post(h, o, z, g):
    r = lax.rsqrt(jnp.mean(o * o, axis=-1, keepdims=True) + EPS)
    return (o * r * g * _silu(z),)


def f_merge(h, ga, gb, ya, yb):
    return (jax.nn.sigmoid(ga) * ya + jax.nn.sigmoid(gb) * yb,)


def f_delta(h, do, o):
    return (jnp.broadcast_to(jnp.sum(do * o, axis=1, keepdims=True), o.shape),)


def cumsum_time(name, x, nseq, seq, reverse):
    nb = seq // LANES

    def body(x_ref, o_ref):
        r, c = _iota((LANES, LANES), 0), _iota((LANES, LANES), 1)
        tri = jnp.where((r <= c) if reverse else (r >= c), 1.0, 0.0).astype(F32)
        carry = jnp.zeros((1, LANES), F32)
        for b in (range(nb - 1, -1, -1) if reverse else range(nb)):
            blk = x_ref[b * LANES:(b + 1) * LANES, :]
            o_ref[b * LANES:(b + 1) * LANES, :] = _dot_mask(tri, blk, "nn") + carry
            carry = carry + jnp.sum(blk, axis=0, keepdims=True)

    spec = pl.BlockSpec((seq, LANES), lambda s: (s, 0))
    return pl.pallas_call(body, name=name, grid=(nseq,), in_specs=[spec], out_specs=spec,
                          out_shape=jax.ShapeDtypeStruct(x.shape, F32), compiler_params=_params(1))(x)


def transpose_time(name, x, nseq, seq):
    def body(x_ref, o_ref):
        o_ref[...] = x_ref[...].T

    return pl.pallas_call(
        body, name=name, grid=(nseq,), in_specs=[pl.BlockSpec((seq, LANES), lambda s: (s, 0))],
        out_specs=pl.BlockSpec((LANES, seq), lambda s: (s, 0)),
        out_shape=jax.ShapeDtypeStruct((nseq * LANES, seq), F32), compiler_params=_params(1))(x)


def _gdn_masks():
    n = GDN_ROWS
    r, c = _iota((n, n), 0), _iota((n, n), 1)
    shift = GDN_CHUNK.bit_length() - 1
    same = lax.shift_right_logical(r, shift) == lax.shift_right_logical(c, shift)
    return r, c, same


def _gdn_decay(gb):
    r, c, same = _gdn_masks()
    seg_tril = jnp.where(jnp.logical_and(same, r >= c), 1.0, 0.0).astype(F32)
    g_cum = mm_mask(seg_tril, gb)
    lane0 = _iota(g_cum.shape, 1) == 0
    g_col = jnp.sum(jnp.where(lane0, g_cum, 0.0), axis=1, keepdims=True)
    g_row = jnp.sum(jnp.where(r == c, jnp.broadcast_to(g_col, (GDN_ROWS, GDN_ROWS)), 0.0), axis=0, keepdims=True)
    return g_cum, g_col - g_row


def gdn_f1(q, k, gb, bb):
    r, c, same = _gdn_masks()
    strict = jnp.logical_and(same, r > c)
    _, diff = _gdn_decay(gb)
    lane0 = _iota(bb.shape, 1) == 0
    beta_col = jnp.sum(jnp.where(lane0, bb, 0.0), axis=1, keepdims=True)
    kk = _dot(k, k, "nt", LO)
    return jnp.where(strict, beta_col * kk * jnp.exp(jnp.where(strict, diff, 0.0)), 0.0)


def gdn_f2(t_corr, q, k, v, gb, bb):
    r, c, same = _gdn_masks()
    incl = jnp.logical_and(same, r >= c)
    g_cum, diff = _gdn_decay(gb)
    decay = jnp.where(incl, jnp.exp(jnp.where(incl, diff, 0.0)), 0.0)
    e_g = jnp.exp(g_cum)
    v_beta, k_beta = v * bb, k * bb * e_g
    value = v_beta + _dot(t_corr, v_beta, "nn", LO)
    k_cum = k_beta + _dot(t_corr, k_beta, "nn", LO)
    attn = _dot(q, k, "nt", LO) * decay
    g_last = mm_mask(jnp.where(same, 1.0, 0.0).astype(F32), gb)
    return value, k_cum, attn, q * e_g, k * jnp.exp(g_last - g_cum)


def tri_inverse(a):
    n = GDN_ROWS
    r, c = _iota((n, n), 0), _iota((n, n), 1)
    shift = GDN_BASE.bit_length() - 1
    blk = lax.shift_right_logical(r, shift) == lax.shift_right_logical(c, shift)
    d = jnp.where(blk, a, 0.0)
    lo = a - d
    p = -d
    c_d = p
    for _ in range(shift - 1):
        p = _dot(p, p, "nn", LO)
        c_d = c_d + p + _dot(c_d, p, "nn", LO)
    assert GDN_CHUNK // GDN_BASE == 4
    nmat = lo + _dot(c_d, lo, "nn", LO)
    n2 = _dot(nmat, nmat, "nn", LO)
    c_n = (n2 - nmat) - _dot(nmat, n2, "nn", LO)
    return c_n + c_d + _dot(c_n, c_d, "nn", LO)


def gdn_a_fwd(q, k, v, gb, bb, rows):
    blk = pl.BlockSpec((GDN_ROWS, LANES), lambda i, h: (i, h))
    sq = pl.BlockSpec((GDN_ROWS, GDN_ROWS), lambda i, h: (i, h))

    def body(q_ref, k_ref, v_ref, gb_ref, bb_ref, val_ref, kc_ref, at_ref, qd_ref, kd_ref, t_ref):
        qv, kv, vv, gv, bv = q_ref[...], k_ref[...], v_ref[...], gb_ref[...], bb_ref[...]
        t_inv = tri_inverse(gdn_f1(qv, kv, gv, bv))
        value, k_cum, attn, q_dec, k_dec = gdn_f2(t_inv, qv, kv, vv, gv, bv)
        val_ref[...], kc_ref[...], at_ref[...], qd_ref[...], kd_ref[...], t_ref[...] = value, k_cum, attn, q_dec, k_dec, t_inv

    wide = jax.ShapeDtypeStruct((rows, NH * LANES), F32)
    square = jax.ShapeDtypeStruct((rows, NH * GDN_ROWS), F32)
    return pl.pallas_call(
        body, name="gdn_a_fwd", grid=(rows // GDN_ROWS, NH), in_specs=[blk] * 5,
        out_specs=[blk, blk, sq, blk, blk, sq], out_shape=[wide, wide, square, wide, wide, square],
        compiler_params=_params(2))(q, k, v, gb, bb)


def gdn_a_bwd(q, k, v, gb, bb, t_inv, dval, dkc, dat, dqd, dkd, dgb_b, rows):
    blk = pl.BlockSpec((GDN_ROWS, LANES), lambda i, h: (i, h))
    sq = pl.BlockSpec((GDN_ROWS, GDN_ROWS), lambda i, h: (i, h))

    def body(q_ref, k_ref, v_ref, gb_ref, bb_ref, t_ref, dval_ref, dkc_ref, dat_ref, dqd_ref, dkd_ref, dgbb_ref,
             dq_ref, dk_ref, dv_ref, dgb_ref, dbb_ref):
        qv, kv, vv, gv, bv, tv = q_ref[...], k_ref[...], v_ref[...], gb_ref[...], bb_ref[...], t_ref[...]
        _, vjp1 = jax.vjp(gdn_f1, qv, kv, gv, bv)
        _, vjp2 = jax.vjp(gdn_f2, tv, qv, kv, vv, gv, bv)
        dt, dq2, dk2, dv2, dgb2, dbb2 = vjp2((dval_ref[...], dkc_ref[...], dat_ref[...], dqd_ref[...], dkd_ref[...]))
        left = dt + _dot(tv, dt, "tn", LO)
        da = -(left + _dot(left, tv, "nt", LO))
        dq1, dk1, dgb1, dbb1 = vjp1(da)
        dq_ref[...] = dq1 + dq2
        dk_ref[...] = dk1 + dk2
        dv_ref[...] = dv2
        dgb_ref[...] = dgb1 + dgb2 + dgbb_ref[...]
        dbb_ref[...] = dbb1 + dbb2

    wide = jax.ShapeDtypeStruct((rows, NH * LANES), F32)
    return pl.pallas_call(
        body, name="gdn_a_bwd", grid=(rows // GDN_ROWS, NH),
        in_specs=[blk] * 5 + [sq, blk, blk, sq, blk, blk, blk], out_specs=[blk] * 5, out_shape=[wide] * 5,
        compiler_params=_params(2))(q, k, v, gb, bb, t_inv, dval, dkc, dat, dqd, dkd, dgb_b)


N_CH = GDN_ROWS // GDN_CHUNK


GDN_HP = 8


def gdn_fb(*args):
    per_head = 6 * N_CH
    states = list(args[GDN_HP * per_head:])
    outs = [[None] * N_CH for _ in range(GDN_HP)]
    zero = jnp.zeros((GDN_CHUNK, LANES), F32)
    for c in range(N_CH):
        for hh in range(GDN_HP):
            val, kc, at, qd, kd, gb = (args[hh * per_head + i * N_CH + c] for i in range(6))
            s = states[hh]
            v_new = val - _dot(kc, s, "nn", LO)
            v_pad = jnp.concatenate([zero] * c + [v_new] + [zero] * (N_CH - 1 - c), axis=0)
            outs[hh][c] = _dot(qd, s, "nn", LO) + _dot(at, v_pad, "nn", LO)
            dec = jnp.exp(jnp.sum(gb, axis=0, keepdims=True))
            states[hh] = s * dec + _dot(kd, v_new, "tn", LO)
    return (*[o for head in outs for o in head], *states)


def _gdn_piece(ref, hh, c):
    width = ref.shape[1] // GDN_HP
    return ref.at[c * GDN_CHUNK:(c + 1) * GDN_CHUNK, hh * width:(hh + 1) * width]


def _gdn_pieces(refs, hh):
    return [_gdn_piece(r, hh, c)[...] for r in refs for c in range(N_CH)]


def _gdn_b_specs(nb, rev):
    def blk_row(s, j):
        return s * nb + (nb - 1 - j if rev else j)

    blk = pl.BlockSpec((GDN_ROWS, GDN_HP * LANES), lambda s, hb, j: (blk_row(s, j), hb))
    sq = pl.BlockSpec((GDN_ROWS, GDN_HP * GDN_ROWS), lambda s, hb, j: (blk_row(s, j), hb))
    snap = pl.BlockSpec((GDN_HP * LANES, LANES), lambda s, hb, j: (blk_row(s, j) * (NH // GDN_HP) + hb, 0))
    return blk, sq, snap


def gdn_b_fwd(val, kc, at, qd, kd, gb, nseq, seq):
    nb = seq // GDN_ROWS
    rows = nseq * seq
    blk, sq, snap = _gdn_b_specs(nb, False)

    def body(val_ref, kc_ref, at_ref, qd_ref, kd_ref, gb_ref, o_ref, snap_ref, s_ref):
        @pl.when(pl.program_id(2) == 0)
        def _():
            s_ref[...] = jnp.zeros_like(s_ref)

        states = [s_ref[hh] for hh in range(GDN_HP)]
        for hh in range(GDN_HP):
            snap_ref[hh * LANES:(hh + 1) * LANES, :] = states[hh]
        pieces = [p for hh in range(GDN_HP) for p in _gdn_pieces([val_ref, kc_ref, at_ref, qd_ref, kd_ref, gb_ref], hh)]
        res = gdn_fb(*pieces, *states)
        for hh in range(GDN_HP):
            for c in range(N_CH):
                _gdn_piece(o_ref, hh, c)[...] = res[hh * N_CH + c]
            s_ref[hh] = res[GDN_HP * N_CH + hh]

    return pl.pallas_call(
        body, name="gdn_b_fwd", grid=(nseq, NH // GDN_HP, nb), in_specs=[blk, blk, sq, blk, blk, blk], out_specs=[blk, snap],
        out_shape=[jax.ShapeDtypeStruct((rows, NH * LANES), F32), jax.ShapeDtypeStruct((nseq * nb * NH * LANES, LANES), F32)],
        scratch_shapes=[pltpu.VMEM((GDN_HP, LANES, LANES), F32)], compiler_params=_params(3))(val, kc, at, qd, kd, gb)


def gdn_b_bwd(val, kc, at, qd, kd, gb, snaps, do, nseq, seq):
    nb = seq // GDN_ROWS
    rows = nseq * seq
    blk, sq, snap = _gdn_b_specs(nb, True)

    def body(val_ref, kc_ref, at_ref, qd_ref, kd_ref, gb_ref, snap_ref, do_ref,
             dval_ref, dkc_ref, dat_ref, dqd_ref, dkd_ref, dgb_ref, ds_ref):
        @pl.when(pl.program_id(2) == 0)
        def _():
            ds_ref[...] = jnp.zeros_like(ds_ref)

        pieces = [p for hh in range(GDN_HP) for p in _gdn_pieces([val_ref, kc_ref, at_ref, qd_ref, kd_ref, gb_ref], hh)]
        states = [snap_ref[hh * LANES:(hh + 1) * LANES, :] for hh in range(GDN_HP)]
        _, vjp = jax.vjp(gdn_fb, *pieces, *states)
        cts = [p for hh in range(GDN_HP) for p in _gdn_pieces([do_ref], hh)] + [ds_ref[hh] for hh in range(GDN_HP)]
        grads = vjp(tuple(cts))
        for hh in range(GDN_HP):
            for i, r in enumerate([dval_ref, dkc_ref, dat_ref, dqd_ref, dkd_ref, dgb_ref]):
                for c in range(N_CH):
                    _gdn_piece(r, hh, c)[...] = grads[hh * 6 * N_CH + i * N_CH + c]
            ds_ref[hh] = grads[GDN_HP * 6 * N_CH + hh]

    wide = jax.ShapeDtypeStruct((rows, NH * LANES), F32)
    square = jax.ShapeDtypeStruct((rows, NH * GDN_ROWS), F32)
    return pl.pallas_call(
        body, name="gdn_b_bwd", grid=(nseq, NH // GDN_HP, nb), in_specs=[blk, blk, sq, blk, blk, blk, snap, blk],
        out_specs=[blk, blk, sq, blk, blk, blk], out_shape=[wide, wide, square, wide, wide, wide],
        scratch_shapes=[pltpu.VMEM((GDN_HP, LANES, LANES), F32)], compiler_params=_params(3))(val, kc, at, qd, kd, gb, snaps, do)


FOX_Q, FOX_K, FOX_V = 4 * NH, 5 * NH, 6 * NH
FOX_SCALE = LANES ** -0.5


def _head_row(ct_ref, h, off, width):
    blk = ct_ref[:, pl.ds(off, width)]
    return jnp.sum(jnp.where(_iota(blk.shape, 0) == h, blk, 0.0), axis=0, keepdims=True)


def _col(x):
    return jnp.max(x, axis=1, keepdims=True)


def _row(x):
    return jnp.max(x.T, axis=0, keepdims=True)


def _causal(shape, q_dim):
    return _iota(shape, q_dim) >= _iota(shape, 1 - q_dim)


def fox_fwd(qn, kn, proj, ct, nseq, seq):
    tq = tk = min(ATT_TILE, seq)
    nq = seq // tq
    rows = nseq * seq
    qblk = pl.BlockSpec((tq, LANES), lambda s, h, i: (s * nq + i, h))
    full = pl.BlockSpec((seq, LANES), lambda s, h, i: (s, h))
    vfull = pl.BlockSpec((seq, LANES), lambda s, h, i: (s, h + FOX_V))
    ctb = pl.BlockSpec((NH, seq), lambda s, h, i: (s * (LANES // NH) + 2, 0))

    def body(q_ref, k_ref, v_ref, ct_ref, o_ref, o16_ref, lse_ref):
        h, i = pl.program_id(1), pl.program_id(2)
        q = q_ref[...]

        def step(j, carry, diag):
            m, l, acc = carry
            off = pl.multiple_of(j * tk, tk)
            s = _dot(q, k_ref[pl.ds(off, tk), :], "nt") * FOX_SCALE - _head_row(ct_ref, h, off, tk)
            if diag:
                s = jnp.where(_causal(s.shape, 0), s, NEG)
            m_new = jnp.maximum(m, jnp.max(s, axis=1, keepdims=True))
            p = jnp.exp(s - m_new)
            alpha = jnp.exp(m - m_new)
            l = alpha * l + jnp.sum(p, axis=1, keepdims=True)
            acc = alpha * acc + _dot(p.astype(BF16), v_ref[pl.ds(off, tk), :].astype(BF16), "nn")
            return m_new, l, acc

        init = (jnp.full((tq, 1), NEG, F32), jnp.zeros((tq, 1), F32), jnp.zeros((tq, LANES), F32))
        carry = lax.fori_loop(0, i, lambda j, c: step(j, c, False), init)
        m, l, acc = step(i, carry, True)
        o = acc / l
        o_ref[...] = o
        o16_ref[...] = o.astype(BF16)
        lse_ref[...] = jnp.broadcast_to(m + jnp.log(l), (tq, LANES))

    wide = (rows, NH * LANES)
    return pl.pallas_call(
        body, name="fox_fwd", grid=(nseq, NH, nq), in_specs=[qblk, full, vfull, ctb], out_specs=[qblk] * 3,
        out_shape=[jax.ShapeDtypeStruct(wide, F32), jax.ShapeDtypeStruct(wide, BF16), jax.ShapeDtypeStruct(wide, F32)],
        compiler_params=_params(3))(qn, kn, proj, ct)


def fox_dq(qn, kn, proj, ct, do, lse, delta, nseq, seq):
    tq = tk = min(ATT_TILE, seq)
    nq = seq // tq
    rows = nseq * seq
    qblk = pl.BlockSpec((tq, LANES), lambda s, h, i: (s * nq + i, h))
    full = pl.BlockSpec((seq, LANES), lambda s, h, i: (s, h))
    vfull = pl.BlockSpec((seq, LANES), lambda s, h, i: (s, h + FOX_V))
    ctb = pl.BlockSpec((NH, seq), lambda s, h, i: (s * (LANES // NH) + 2, 0))

    def body(q_ref, k_ref, v_ref, ct_ref, do_ref, lse_ref, dl_ref, dq_ref, dc_ref):
        h, i = pl.program_id(1), pl.program_id(2)
        q = q_ref[...]
        lse, delta = _col(lse_ref[...]), _col(dl_ref[...])
        do16 = do_ref[...].astype(BF16)

        def step(j, carry, diag):
            dq, dc = carry
            off = pl.multiple_of(j * tk, tk)
            k = k_ref[pl.ds(off, tk), :]
            p = jnp.exp(_dot(q, k, "nt") * FOX_SCALE - _head_row(ct_ref, h, off, tk) - lse)
            if diag:
                p = jnp.where(_causal(p.shape, 0), p, 0.0)
            dp = _dot(do16, v_ref[pl.ds(off, tk), :].astype(BF16), "nt")
            ds = p * (dp - delta)
            return dq + _dot(ds.astype(BF16), k, "nn"), dc + jnp.sum(ds, axis=1, keepdims=True)

        init = (jnp.zeros((tq, LANES), F32), jnp.zeros((tq, 1), F32))
        dq, dc = step(i, lax.fori_loop(0, i, lambda j, c: step(j, c, False), init), True)
        dq_ref[...] = dq * FOX_SCALE
        dc_ref[...] = jnp.where(_iota((tq, LANES), 1) == 0, dc, 0.0)

    wide = jax.ShapeDtypeStruct((rows, NH * LANES), F32)
    return pl.pallas_call(
        body, name="fox_dq", grid=(nseq, NH, nq), in_specs=[qblk, full, vfull, ctb, qblk, qblk, qblk],
        out_specs=[qblk, qblk], out_shape=[wide, wide], compiler_params=_params(3))(qn, kn, proj, ct, do, lse, delta)


def fox_dkv(qn, kn, proj, cb, do, lse, delta, nseq, seq):
    tq = tk = min(ATT_TILE, seq)
    nq = seq // tq
    rows = nseq * seq
    kblk = pl.BlockSpec((tk, LANES), lambda s, h, j: (s * nq + j, h))
    vblk = pl.BlockSpec((tk, LANES), lambda s, h, j: (s * nq + j, h + FOX_V))
    full = pl.BlockSpec((seq, LANES), lambda s, h, j: (s, h))

    def body(q_ref, k_ref, v_ref, cb_ref, do_ref, lse_ref, dl_ref, dk_ref, dv_ref, dc_ref):
        j = pl.program_id(2)
        k = k_ref[...]
        v16 = v_ref[...].astype(BF16)
        ck = _col(cb_ref[...])

        def step(i, carry, diag):
            dk, dv, dc = carry
            off = pl.multiple_of(i * tq, tq)
            q = q_ref[pl.ds(off, tq), :]
            do16 = do_ref[pl.ds(off, tq), :].astype(BF16)
            lse, delta = (_row(r[pl.ds(off, tq), :]) for r in (lse_ref, dl_ref))
            p = jnp.exp(_dot(k, q, "nt") * FOX_SCALE - ck - lse)
            if diag:
                p = jnp.where(_causal(p.shape, 1), p, 0.0)
            dv = dv + _dot(p.astype(BF16), do16, "nn")
            ds = p * (_dot(v16, do16, "nt") - delta)
            return dk + _dot(ds.astype(BF16), q, "nn"), dv, dc + jnp.sum(ds, axis=1, keepdims=True)

        zero = jnp.zeros((tk, LANES), F32)
        carry = step(j, (zero, zero, jnp.zeros((tk, 1), F32)), True)
        dk, dv, dc = lax.fori_loop(j + 1, nq, lambda i, c: step(i, c, False), carry)
        dk_ref[...] = dk * FOX_SCALE
        dv_ref[...] = dv.astype(BF16)
        dc_ref[...] = jnp.where(_iota((tk, LANES), 1) == 0, -dc, 0.0)

    wide = (rows, NH * LANES)
    return pl.pallas_call(
        body, name="fox_dkv", grid=(nseq, NH, nq), in_specs=[full, kblk, vblk, kblk, full, full, full],
        out_specs=[kblk, kblk, kblk],
        out_shape=[jax.ShapeDtypeStruct(wide, F32), jax.ShapeDtypeStruct(wide, BF16), jax.ShapeDtypeStruct(wide, F32)],
        compiler_params=_params(3))(qn, kn, proj, cb, do, lse, delta)


def loss_head(out, tgt, rows, width):
    tb = ROW_TILE
    blk = pl.BlockSpec((tb, width), lambda i: (i, 0))
    accb = pl.BlockSpec((8, LANES), lambda i: (0, 0))

    def body(o_ref, t_ref, d32_ref, d16_ref, acc_ref):
        d = o_ref[...] - t_ref[...]
        row_loss = 0.5 * jnp.mean(d * d, axis=1, keepdims=True)
        g = d * (1.0 / width)
        d32_ref[...] = g
        d16_ref[...] = g.astype(BF16)
        part = jnp.where(_iota((tb, LANES), 1) == 0, row_loss, 0.0).reshape(tb // 8, 8, LANES).sum(axis=0)

        @pl.when(pl.program_id(0) == 0)
        def _():
            acc_ref[...] = part

        @pl.when(pl.program_id(0) != 0)
        def _():
            acc_ref[...] += part

    return pl.pallas_call(
        body, name="loss_head", grid=(rows // tb,), in_specs=[blk, blk], out_specs=[blk, blk, accb],
        out_shape=[jax.ShapeDtypeStruct((rows, width), F32), jax.ShapeDtypeStruct((rows, width), BF16),
                   jax.ShapeDtypeStruct((8, LANES), F32)], compiler_params=_params(1))(out, tgt)


def _adamw_update(w, g, m, v):
    m_new = ADAM_B1 * m + (1.0 - ADAM_B1) * g
    v_new = ADAM_B2 * v + (1.0 - ADAM_B2) * (g * g)
    m_hat = m_new / (1.0 - ADAM_B1 ** ADAM_STEP)
    v_hat = v_new / (1.0 - ADAM_B2 ** ADAM_STEP)
    return -ADAM_LR * (m_hat / (jnp.sqrt(v_hat) + ADAM_EPS) + ADAM_WD * w), m_new, v_new


def adamw(name, w, g, m, v):
    rows, cols = w.shape
    tb = min(rows, 128)
    assert rows % tb == 0
    blk = pl.BlockSpec((tb, cols), lambda i: (i, 0))

    def body(w_ref, g_ref, m_ref, v_ref, d_ref, mo_ref, vo_ref):
        d_ref[...], mo_ref[...], vo_ref[...] = _adamw_update(w_ref[...], g_ref[...], m_ref[...], v_ref[...])

    shp = jax.ShapeDtypeStruct(w.shape, F32)
    return pl.pallas_call(body, name=name, grid=(rows // tb,), in_specs=[blk] * 4, out_specs=[blk] * 3,
                          out_shape=[shp] * 3, compiler_params=_params(1))(w, g, m, v)


SPLIT_TILE = 128


def _tiled(shape2d, ax, n_lead, index):
    blk = (SPLIT_TILE, shape2d[1]) if ax == 0 else (shape2d[0], SPLIT_TILE)

    def index_map(*args):
        *lead, t = index(*args)
        return (*lead, t, 0) if ax == 0 else (*lead, 0, t)

    return pl.BlockSpec((None,) * n_lead + blk, index_map)


def adamw_halves(name, w, mine, other, m, v, c, ax):
    steps = w.shape[ax] // 2 // SPLIT_TILE
    assert w.shape[ax] == 2 * steps * SPLIT_TILE

    def body(c_ref, w_ref, mine_ref, other_ref, m_ref, v_ref, g_ref, d_ref, mo_ref, vo_ref):
        g = jnp.where(pl.program_id(0) // steps == c_ref[0], mine_ref[...], other_ref[...])
        g_ref[...] = g
        d_ref[...], mo_ref[...], vo_ref[...] = _adamw_update(w_ref[...], g, m_ref[...], v_ref[...])

    blk = _tiled(w.shape, ax, 0, lambda i, c_ref: (i,))
    hblk = _tiled(mine.shape, ax, 0, lambda i, c_ref: (i % steps,))
    grid_spec = pltpu.PrefetchScalarGridSpec(num_scalar_prefetch=1, grid=(2 * steps,),
                                             in_specs=[blk, hblk, hblk, blk, blk], out_specs=[blk] * 4)
    shp = jax.ShapeDtypeStruct(w.shape, F32)
    return pl.pallas_call(body, name=name, grid_spec=grid_spec, out_shape=[shp] * 4,
                          compiler_params=_params(1))(c, w, mine, other, m, v)


def add_chips(name, slots, parts, chip, axes):
    outs = []
    for idx, (x, own, ax) in enumerate(zip(slots, parts, axes)):
        n, shape2d = x.shape[0], x.shape[1:]
        steps = shape2d[ax] // SPLIT_TILE
        assert shape2d[ax] == steps * SPLIT_TILE

        def body(me_ref, *refs, n=n):
            o_ref = refs[n + 1]
            acc = None
            for t in range(n):
                term = jnp.where(me_ref[0] == t, refs[n][...], refs[t][...]).astype(F32)
                acc = term if acc is None else acc + term
            o_ref[...] = acc

        def filled(t, n=n):
            return lambda i, me_ref: (jnp.where(me_ref[0] == t, (t + 1) % n, t), i)

        grid_spec = pltpu.PrefetchScalarGridSpec(
            num_scalar_prefetch=1, grid=(steps,),
            in_specs=[_tiled(shape2d, ax, 1, filled(t)) for t in range(n)]
            + [_tiled(shape2d, ax, 1, lambda i, me_ref: (me_ref[0], i))],
            out_specs=_tiled(shape2d, ax, 0, lambda i, me_ref: (i,)))
        outs.append(pl.pallas_call(
            body, name=f"{name}_{idx}", grid_spec=grid_spec, out_shape=jax.ShapeDtypeStruct(shape2d, F32),
            compiler_params=_params(1))(chip, *([x] * n), own))
    return outs


def add_pair(name, gs, rs, c, axes):
    outs = []
    for idx, (g, r, ax) in enumerate(zip(gs, rs, axes)):
        nb = r.shape[0]
        steps = r.shape[1 + ax] // SPLIT_TILE
        assert r.shape[1 + ax] == steps * SPLIT_TILE

        def body(c_ref, g_ref, r_ref, o_ref):
            o_ref[...] = (g_ref[...] + r_ref[...]).astype(BF16)

        grid_spec = pltpu.PrefetchScalarGridSpec(
            num_scalar_prefetch=1, grid=(nb, steps),
            in_specs=[_tiled(g.shape[1:], ax, 1, lambda b, i, c_ref: (b, c_ref[0] * steps + i)),
                      _tiled(r.shape[1:], ax, 1, lambda b, i, c_ref: (b, i))],
            out_specs=_tiled(r.shape[1:], ax, 1, lambda b, i, c_ref: (b, i)))
        outs.append(pl.pallas_call(
            body, name=f"{name}_{idx}", grid_spec=grid_spec, out_shape=jax.ShapeDtypeStruct(r.shape, BF16),
            compiler_params=_params(2))(c, g, r))
    return outs


def _place():
    x, y, c = lax.axis_index("x"), lax.axis_index("y"), lax.axis_index("c")
    return x, y, c, [(1 - x, y), (x, 1 - y), (1 - x, 1 - y)]


def _remote(src, dst, send_sem, recv_sem, dev):
    return pltpu.make_async_remote_copy(src_ref=src, dst_ref=dst, send_sem=send_sem, recv_sem=recv_sem,
                                        device_id=dev, device_id_type=MESH)


def _half(ref, lead, ax, which):
    size = ref.shape[len(lead) + ax] // 2
    part = pl.ds(which * size, size)
    return ref.at[(*lead, part, slice(None)) if ax == 0 else (*lead, slice(None), part)]


def gather_weights(shards, axes):
    n = len(shards)

    def body(*refs):
        ins, outs = refs[:n], refs[n:2 * n]
        ici_s, ici_r, d2d_s, d2d_r = refs[2 * n:]
        x, y, c, chips = _place()
        me = 2 * x + y
        sends, passes = [], []
        for w in range(n):
            cp = _remote(ins[w], outs[w].at[me], d2d_s.at[3 * n + w], d2d_r.at[3 * n + w], (x, y, 1 - c))
            cp.start()
            passes.append(cp)
        for w in range(n):
            for j, (ox, oy) in enumerate(chips):
                cp = _remote(_half(ins[w], (), axes[w], c), _half(outs[w], (me,), axes[w], c),
                             ici_s.at[3 * w + j], ici_r.at[3 * w + j], (ox, oy, c))
                cp.start()
                sends.append(cp)
        for w in range(n):
            for j, (ox, oy) in enumerate(chips):
                landed = _half(outs[w], (2 * ox + oy,), axes[w], c)
                _remote(landed, landed, ici_s.at[3 * w + j], ici_r.at[3 * w + j], (ox, oy, c)).wait_recv()
                cp = _remote(landed, landed, d2d_s.at[3 * w + j], d2d_r.at[3 * w + j], (x, y, 1 - c))
                cp.start()
                passes.append(cp)
        for w in range(n):
            for j, (ox, oy) in enumerate(chips):
                other = _half(outs[w], (2 * ox + oy,), axes[w], 1 - c)
                _remote(other, other, d2d_s.at[3 * w + j], d2d_r.at[3 * w + j], (x, y, 1 - c)).wait_recv()
            own = outs[w].at[me]
            _remote(own, own, d2d_s.at[3 * n + w], d2d_r.at[3 * n + w], (x, y, 1 - c)).wait_recv()
        for cp in sends + passes:
            cp.wait_send()

    return pl.pallas_call(
        body, name="gather_weights", in_specs=[ANY] * n, out_specs=[ANY] * n,
        out_shape=[jax.ShapeDtypeStruct((4,) + s.shape, s.dtype) for s in shards],
        scratch_shapes=[pltpu.SemaphoreType.DMA((3 * n,))] * 2 + [pltpu.SemaphoreType.DMA((4 * n,))] * 2,
    )(*shards)


HBM = pl.BlockSpec(memory_space=pltpu.HBM)
SEM = pl.BlockSpec(memory_space=pltpu.SEMAPHORE)
DATAFLOW = pltpu.SideEffectType.DATAFLOW_SIDE_EFFECTING


def _hbm(a):
    return pltpu.with_memory_space_constraint(a, pltpu.HBM)


class SplitExchange:
    def __init__(self, name, srcs, zone_shapes, n_sems, plan):
        self.name, self.n, self.n_sems, self.plan = name, len(srcs), n_sems, plan
        self.srcs = [_hbm(s) for s in srcs]
        self.zones = [_hbm(lax.empty(shape, s.dtype)) for shape, s in zip(zone_shapes, srcs)]

    def start(self, after):
        n, n_after = self.n, len(after)

        def body(*refs):
            ins, lands = refs[:n], refs[n:2 * n]
            send, recv, token = refs[2 * n + n_after], refs[2 * n + n_after + 1], refs[-1]
            for src, dst, si, ri, dev in self.plan(ins, lands)[0]:
                _remote(src, dst, send.at[si], recv.at[ri], dev).start()
            token[...] = jnp.zeros_like(token)

        res = pl.pallas_call(
            body, name=f"{self.name}_start", in_specs=[HBM] * (2 * n) + [ANY] * n_after,
            out_specs=[SEM, SEM] + [HBM] * (2 * n) + [pl.BlockSpec(memory_space=pltpu.VMEM)],
            out_shape=[pltpu.SemaphoreType.DMA((self.n_sems,)), pltpu.SemaphoreType.DMA((self.n_sems,))]
            + [pltpu.HBM(a.shape, a.dtype) for a in self.srcs + self.zones] + [jax.ShapeDtypeStruct((8, LANES), F32)],
            input_output_aliases={i: 2 + i for i in range(2 * n)},
            compiler_params=pltpu.CompilerParams(has_side_effects=DATAFLOW),
        )(*self.srcs, *self.zones, *after)
        self.sems, self.srcs, self.zones = res[:2], list(res[2:2 + n]), list(res[2 + n:2 + 2 * n])
        return res[-1]

    def wait(self, after):
        n = self.n

        def body(*refs):
            ins, lands = refs[:n], refs[n:2 * n]
            send, recv = refs[2 * n], refs[2 * n + 1]
            sends, arrivals = self.plan(ins, lands)
            for src, _, si, _, dev in sends:
                _remote(src, src, send.at[si], recv.at[si], dev).wait_send()
            for landed, ri in arrivals:
                _remote(landed, landed, send.at[ri], recv.at[ri], _place()[:3]).wait_recv()

        res = pl.pallas_call(
            body, name=f"{self.name}_wait", in_specs=[HBM] * (2 * n) + [SEM, SEM, ANY], out_specs=[HBM] * (2 * n),
            out_shape=[pltpu.HBM(a.shape, a.dtype) for a in self.srcs + self.zones],
            input_output_aliases={i: i for i in range(2 * n)},
            compiler_params=pltpu.CompilerParams(has_side_effects=DATAFLOW),
        )(*self.srcs, *self.zones, *self.sems, after)
        self.srcs = list(res[:n])
        return list(res[n:])


def split_gather(shards):
    n = len(shards)

    def plan(ins, lands):
        x, y, c, chips = _place()
        me = 2 * x + y
        sends, arrivals = [], []
        for w in range(n):
            for j, (ox, oy) in enumerate(chips):
                for k in range(2):
                    base = 2 * (3 * w + j)
                    sends.append((_half(ins[w], (), 0, c), _half(lands[w], (me,), 0, c), base + k, base + c, (ox, oy, k)))
                    arrivals.append((_half(lands[w], (2 * ox + oy,), 0, k), base + k))
            sends.append((ins[w], lands[w].at[me], 6 * n + w, 6 * n + w, (x, y, 1 - c)))
            arrivals.append((lands[w].at[me], 6 * n + w))
        return sends, arrivals

    return SplitExchange("gather", shards, [(4,) + s.shape for s in shards], 7 * n, plan)


def split_pair_swap(name, grads, axes):
    def plan(ins, lands):
        x, y, c, _ = _place()
        sends = [(_half(ins[w], (slice(None),), axes[w], 1 - c), lands[w], w, w, (x, y, 1 - c)) for w in range(len(ins))]
        return sends, [(lands[w], w) for w in range(len(ins))]

    halved = [tuple(d // 2 if i == 1 + ax else d for i, d in enumerate(g.shape)) for g, ax in zip(grads, axes)]
    return SplitExchange(name, grads, halved, len(grads), plan)


def split_chip_exchange(name, parts):
    def plan(ins, lands):
        x, y, c, chips = _place()
        sends, arrivals = [], []
        for w in range(len(ins)):
            for j, (ox, oy) in enumerate(chips):
                sends.append((ins[w].at[2 * ox + oy], lands[w].at[2 * x + y], 3 * w + j, 3 * w + j, (ox, oy, c)))
                arrivals.append((lands[w].at[2 * ox + oy], 3 * w + j))
        return sends, arrivals

    return SplitExchange(name, parts, [p.shape for p in parts], 3 * len(parts), plan)


def split_pair_send(halves):
    def plan(ins, lands):
        x, y, c, _ = _place()
        return ([(ins[w], lands[w], w, w, (x, y, 1 - c)) for w in range(len(ins))],
                [(lands[w], w) for w in range(len(ins))])

    return SplitExchange("pair_send", halves, [h.shape for h in halves], len(halves), plan)


def pair_send(halves):
    n = len(halves)

    def body(*refs):
        ins, outs = refs[:n], refs[n:2 * n]
        send, recv = refs[2 * n:]
        x, y, c, _ = _place()
        cps = [_remote(ins[w], outs[w], send.at[w], recv.at[w], (x, y, 1 - c)) for w in range(n)]
        for cp in cps:
            cp.start()
        for cp in cps:
            cp.wait_recv()
        for cp in cps:
            cp.wait_send()

    return pl.pallas_call(
        body, name="pair_send", in_specs=[ANY] * n, out_specs=[ANY] * n,
        out_shape=[jax.ShapeDtypeStruct(h.shape, h.dtype) for h in halves],
        scratch_shapes=[pltpu.SemaphoreType.DMA((n,))] * 2,
    )(*halves)


def all_reduce_small(name, vec, after=()):
    rows = vec.shape[0]

    def body(v_ref, *refs):
        o_ref, buf, send, recv = refs[len(after):]
        x, y, c, _ = _place()
        me = 4 * x + 2 * y + c
        buf[me] = v_ref[...]
        cps = []
        for k in range(1, 8):
            kx, ky, kc = (k >> 2) & 1, (k >> 1) & 1, k & 1
            peer = (x if kx == 0 else 1 - x, y if ky == 0 else 1 - y, c if kc == 0 else 1 - c)
            cp = _remote(v_ref, buf.at[me], send.at[k - 1], recv.at[k - 1], peer)
            cp.start()
            cps.append(cp)
        for k in range(1, 8):
            kx, ky, kc = (k >> 2) & 1, (k >> 1) & 1, k & 1
            px, py, pc = (x if kx == 0 else 1 - x, y if ky == 0 else 1 - y, c if kc == 0 else 1 - c)
            slot = buf.at[4 * px + 2 * py + pc]
            _remote(slot, slot, send.at[k - 1], recv.at[k - 1], (px, py, pc)).wait_recv()
        for cp in cps:
            cp.wait_send()
        acc = buf[0]
        for d in range(1, 8):
            acc = acc + buf[d]
        o_ref[...] = acc

    vm = pl.BlockSpec(memory_space=pltpu.VMEM)
    return pl.pallas_call(
        body, name=name, in_specs=[vm] + [ANY] * len(after), out_specs=vm, out_shape=jax.ShapeDtypeStruct(vec.shape, F32),
        scratch_shapes=[pltpu.VMEM((8, rows, LANES), F32), pltpu.SemaphoreType.DMA((7,)), pltpu.SemaphoreType.DMA((7,))],
    )(vec, *after)


class NoExchange:
    def __init__(self, late):
        self.late = late

    def late_weights(self, after):
        return self.late

    def reduce_start(self, grads):
        return jnp.zeros((8, LANES), F32)

    def reduce_exchange(self, after):
        return jnp.zeros((8, LANES), F32)

    def reduce_finish(self, after):
        return jnp.zeros((8, LANES), F32)

    def input_grad_start(self, dw_main, dw_small):
        return jnp.zeros((8, LANES), F32)

    def input_grad_exchange(self, after):
        return jnp.zeros((8, LANES), F32)


def local_step(x2, tgt2, g1, g2, gdn_ng, qn_g, kn_g, p1, p2, conv_w, wt_main, wt_small, hooks, nseq, seq):
    rows, dm = x2.shape
    wide = NH * LANES
    row = lambda a, off=0, w=None: (a, "row", off, a.shape[1] if w is None else w)
    rowh = lambda a, off=0, w=LANES: (a, "rowh", off, w)
    par = lambda a: (a, "par", 0, a.shape[1])
    parh = lambda a, off=0: (a, "parh", off, LANES)
    o_row = lambda w, dt: (w, "row", w, dt)
    o_rowh = lambda dt, tw=wide, w=LANES: (tw, "rowh", w, dt)

    u, = ew_fwd("rms1", f_rms, [row(x2), par(g1)], [o_row(dm, BF16)], rows)
    proj = matmul("mm_in", u, wt_main, "nt", BF16)
    sp = matmul("mm_in_small", u, wt_small, "nt", F32)
    so, = ew_fwd("small", f_small, [row(sp), par(p1), par(p2)], [o_row(LANES, F32)], rows)
    cs = cumsum_time("cumsum", so, nseq, seq, False)
    gb, bb, cb = ew_fwd("bcast", f_bcast, [row(so), row(cs)], [o_rowh(F32)] * 3, rows, NH)
    ct = transpose_time("c_time_major", cs, nseq, seq)
    conv = {}
    for mode, off in (("q", 0), ("k", NH), ("v", 2 * NH)):
        conv[mode], = ew_fwd(f"conv_{mode}", make_f_conv(mode), [rowh(proj, off), parh(conv_w, off)], [o_rowh(F32)],
                             rows, NH, seq, "hi", CONV_HEADS)
    val, kcum, attn, qdec, kdec, t_inv = gdn_a_fwd(conv["q"], conv["k"], conv["v"], gb, bb, rows)
    o_a, snaps = gdn_b_fwd(val, kcum, attn, qdec, kdec, gb, nseq, seq)
    ya_in, = ew_fwd("gdn_post", f_post, [rowh(o_a), rowh(proj, 3 * NH), par(gdn_ng)], [o_rowh(BF16)], rows, NH)
    fqn, = ew_fwd("fox_qn", f_rms, [rowh(proj, FOX_Q), par(qn_g)], [o_rowh(BF16)], rows, NH)
    fkn, = ew_fwd("fox_kn", f_rms, [rowh(proj, FOX_K), par(kn_g)], [o_rowh(BF16)], rows, NH)
    o_b, o_b16, lse = fox_fwd(fqn, fkn, proj, ct, nseq, seq)
    p_a, p_b, w_o, w_u, w_d = hooks.late_weights(o_a)
    y_a = matmul("mm_pa", ya_in, p_a, "nn", F32, tn=1024)
    y_b = matmul("mm_pb", o_b16, p_b, "nn", F32, tn=1024)
    gates = [row(proj, 7, dm), row(proj, 8, dm)]
    merged, = ew_fwd("merge", f_merge, gates + [row(y_a), row(y_b)], [o_row(dm, BF16)], rows)
    hres = matmul("mm_out", merged, w_o, "nn", F32, add=x2, tn=1024)
    hn, = ew_fwd("rms2", f_rms, [row(hres), par(g2)], [o_row(dm, BF16)], rows)
    up_blocks = w_u.shape[0]
    act, relu2 = matmul("mm_up", hn, w_u, "nn", F32, col_blocks=up_blocks, out_dtypes=[F32, BF16],
                        epilogue=lambda r: [r, jnp.maximum(r, 0.0) * jnp.maximum(r, 0.0)])
    out = matmul("mm_down", relu2, w_d, "nn", F32, add=hres, tn=1024)
    dout, dout16, loss_acc = loss_head(out, tgt2, rows, dm)

    d_act = matmul("mm_d_act", dout16, w_d, "nt", BF16, extras=[act], epilogue=lambda r, a: [2.0 * jnp.maximum(a, 0.0) * r])
    dw_d = matmul("mm_dw_down", relu2, dout16, "tn", F32, tn=1024)
    dw_u = matmul("mm_dw_up", hn, d_act, "tn", F32, col_blocks=up_blocks)
    d_hn = matmul("mm_d_hn", d_act, w_u, "nt", F32, col_blocks=up_blocks)
    dh, dh16, dg2 = ew_bwd("rms2_b", f_rms, [row(hres), par(g2)], [(row(d_hn),)], [row(dout)],
                           lambda g, e: [g[0] + e[0], g[0] + e[0], g[1]],
                           [((rows, dm), "row", dm, F32, None), ((rows, dm), "row", dm, BF16, None), ((1, dm), "par", dm, F32, "all")], rows)
    d_merged = matmul("mm_d_merged", dh16, w_o, "nt", F32, tn=1024)
    dw_o = matmul("mm_dw_out", merged, dh16, "tn", F32, tn=1024)
    seg16 = ((rows, dm), "row", dm, BF16, None)
    d_ga16, d_gb16, d_ya16, d_yb16 = ew_bwd("merge_b", f_merge, gates + [row(y_a), row(y_b)], [(row(d_merged),)], [],
                                            lambda g, e: list(g), [seg16] * 4, rows)
    dp_a = matmul("mm_dp_a", ya_in, d_ya16, "tn", F32, tn=1024)
    d_ya_in = matmul("mm_d_ya_in", d_ya16, p_a, "nt", F32, tn=1024)
    dp_b = matmul("mm_dp_b", o_b16, d_yb16, "tn", F32, tn=1024)
    d_ob = matmul("mm_d_ob", d_yb16, p_b, "nt", F32, tn=1024)
    token = hooks.reduce_start(dict(p_a=dp_a, p_b=dp_b, w_o=dw_o, w_u=dw_u, w_d=dw_d))
    gdn_ng_t = gdn_ng + token[0, 0]
    h32 = ((rows, wide), "rowh", LANES, F32, None)
    h16 = ((rows, wide), "rowh", LANES, BF16, None)
    gain = ((1, LANES), "par", LANES, F32, "all")
    d_oa, d_z16, d_gdn_ng = ew_bwd("gdn_post_b", f_post, [rowh(o_a), rowh(proj, 3 * NH), par(gdn_ng_t)], [(rowh(d_ya_in),)], [],
                                   lambda g, e: list(g), [h32, h16, gain], rows, NH)
    dval, dkc, dat, dqd, dkd, dgb_b = gdn_b_bwd(val, kcum, attn, qdec, kdec, gb, snaps, d_oa, nseq, seq)
    d_cq, d_ck, d_cv, d_gb, d_bb = gdn_a_bwd(conv["q"], conv["k"], conv["v"], gb, bb, t_inv, dval, dkc, dat, dqd, dkd, dgb_b, rows)
    token = hooks.reduce_exchange(d_cq)
    conv_w_t = conv_w + token[0, 0]
    d_pre, d_conv = {}, {}
    tap = ((4, wide), "parh", LANES, F32, "inner")
    for mode, off, ctg in (("q", 0, d_cq), ("k", NH, d_ck), ("v", 2 * NH, d_cv)):
        d_pre[mode], d_conv[mode] = ew_bwd(f"conv_{mode}_b", make_f_conv(mode), [rowh(proj, off), parh(conv_w_t, off)],
                                           [(rowh(ctg),)], [], lambda g, e: list(g), [h16, tap], rows, NH, seq, "hi", CONV_HEADS)
    delta, = ew_fwd("fox_delta", f_delta, [rowh(d_ob), rowh(o_b)], [o_rowh(F32)], rows, NH, after=[token])
    d_fqn, d_cq_b = fox_dq(fqn, fkn, proj, ct, d_ob, lse, delta, nseq, seq)
    d_fkn, d_fv16, d_ck_b = fox_dkv(fqn, fkn, proj, cb, d_ob, lse, delta, nseq, seq)
    token = hooks.reduce_finish(d_fkn)
    qn_g_t, kn_g_t = qn_g + token[0, 0], kn_g + token[0, 0]
    d_fq16, d_qn_g = ew_bwd("fox_qn_b", f_rms, [rowh(proj, FOX_Q), par(qn_g_t)], [(rowh(d_fqn),)], [], lambda g, e: list(g),
                            [h16, gain], rows, NH)
    d_fk16, d_kn_g = ew_bwd("fox_kn_b", f_rms, [rowh(proj, FOX_K), par(kn_g_t)], [(rowh(d_fkn),)], [], lambda g, e: list(g),
                            [h16, gain], rows, NH)
    narrow = ((rows, LANES), "row", LANES, F32, None)
    d_so, d_cs = ew_bwd("bcast_b", f_bcast, [row(so), row(cs)], [(rowh(d_gb),), (rowh(d_bb),), (rowh(d_cq_b), rowh(d_ck_b))], [],
                        lambda g, e: list(g), [narrow, narrow], rows, NH)
    d_logf = cumsum_time("cumsum_b", d_cs, nseq, seq, True)
    vec = ((1, LANES), "par", LANES, F32, "all")
    d_sp16, d_p1, d_p2 = ew_bwd("small_b", f_small, [row(sp), par(p1), par(p2)], [(row(d_so), row(d_logf))], [],
                                lambda g, e: list(g), [((rows, LANES), "row", LANES, BF16, None), vec, vec], rows)
    d_proj16 = jnp.concatenate([d_pre["q"], d_pre["k"], d_pre["v"], d_z16, d_fq16, d_fk16, d_fv16, d_ga16, d_gb16], axis=1)
    dw_main = matmul("mm_dw_main", d_proj16, u, "tn", F32)
    dw_small = matmul("mm_dw_small", d_sp16, u, "tn", F32)
    wt_small_t = wt_small + hooks.input_grad_start(dw_main, dw_small)[0, 0].astype(BF16)
    d_u = matmul("mm_d_u_small", d_sp16, wt_small_t, "nn", F32)
    d_u = matmul("mm_d_u_first", d_proj16, wt_main, "nn", F32, add=d_u, k_part=(0, 2))
    d_u = matmul("mm_d_u_second", d_proj16, wt_main, "nn", F32, add=d_u, k_part=(1, 2), after=[hooks.input_grad_exchange(d_u)])
    dx, dg1 = ew_bwd("rms1_b", f_rms, [row(x2), par(g1)], [(row(d_u),)], [row(dh)], lambda g, e: [g[0] + e[0], g[1]],
                     [((rows, dm), "row", dm, F32, None), ((1, dm), "par", dm, F32, "all")], rows)
    d_conv_w = jnp.concatenate([d_conv["q"], d_conv["k"], d_conv["v"]], axis=1)
    return dict(loss_acc=loss_acc, dx=dx, g1=dg1, g2=dg2, gdn_ng=d_gdn_ng, qn=d_qn_g, kn=d_kn_g, p1=d_p1, p2=d_p2,
                conv=d_conv_w, w_main=dw_main, w_small=dw_small, p_a=dp_a, p_b=dp_b, w_o=dw_o, w_u=dw_u, w_d=dw_d)


_W = NH * LANES
_A0, _A1 = 4 * _W, 4 * _W + 2 * NH
_B0, _B1 = _A1 + 3 * _W, _A1 + 3 * _W + NH
N_IN = _B1 + 2 * _W


def _split_w_in(full_t):
    main = jnp.concatenate([full_t[:_A0], full_t[_A1:_B0], full_t[_B1:]], axis=0)
    small = jnp.concatenate([full_t[_A0:_A1], full_t[_B0:_B1], jnp.zeros((LANES - 3 * NH, full_t.shape[1]), full_t.dtype)], axis=0)
    return main, small


def _join_w_in(main, small):
    return jnp.concatenate([main[:_A0], small[:2 * NH], main[_A0:_A0 + 3 * _W], small[2 * NH:3 * NH], main[_A0 + 3 * _W:]], axis=0)


def _lanes(v, at=0):
    return jnp.pad(v.reshape(1, -1), ((0, 0), (at, LANES - at - v.size)))


def kernel(x, norm_mix_g, w_in, gdn_conv_w, gdn_a_log, gdn_dt_bias, gdn_norm_g, fox_q_norm_g, fox_k_norm_g, fox_f_bias, w_proj_gdn, w_proj_fox, w_out, norm_mlp_g, w_up, w_down, loss_target, m_norm_mix_g, m_w_in, m_gdn_conv_w, m_gdn_a_log, m_gdn_dt_bias, m_gdn_norm_g, m_fox_q_norm_g, m_fox_k_norm_g, m_fox_f_bias, m_w_proj_gdn, m_w_proj_fox, m_w_out, m_norm_mlp_g, m_w_up, m_w_down, v_norm_mix_g, v_w_in, v_gdn_conv_w, v_gdn_a_log, v_gdn_dt_bias, v_gdn_norm_g, v_fox_q_norm_g, v_fox_k_norm_g, v_fox_f_bias, v_w_proj_gdn, v_w_proj_fox, v_w_out, v_norm_mlp_g, v_w_up, v_w_down):
    nseq, seq, dm = x.shape
    rows = nseq * seq
    xi, yi, ci = lax.axis_index("x"), lax.axis_index("y"), lax.axis_index("c")
    chip = 2 * xi + yi
    conv_cols = gdn_conv_w.shape[2]

    tr = lambda a: jnp.swapaxes(a[0], 0, 1)
    big = [tr(w_in), w_proj_gdn[0], w_proj_fox[0], w_out[0], w_up[0], w_down[0]]
    axes = [1, 0, 0, 0, 0, 0]
    big16 = [w.astype(BF16) for w in big]
    conv_slot = jnp.zeros((4, 4, conv_cols), F32).at[:, chip].set(jnp.where(ci == 0, gdn_conv_w[0], 0.0))
    conv_full = all_reduce_small("gather_conv", conv_slot.reshape(-1, LANES)).reshape(4, 4 * conv_cols)
    got_in, = gather_weights(big16[:1], axes[:1])
    wt_main, wt_small = _split_w_in(got_in.reshape(-1, dm))
    core, chip_no = ci.reshape(1).astype(jnp.int32), chip.reshape(1).astype(jnp.int32)
    gather = split_gather(big16[1:])
    token = gather.start([got_in, conv_full])

    class Hooks:
        def late_weights(self, after):
            g_pa, g_pb, g_wo, w_u, g_wd = gather.wait(after)
            return (*(g.reshape(-1, dm) for g in (g_pa, g_pb, g_wo)), w_u, g_wd.reshape(-1, dm))

        def reduce_start(self, grads):
            blocks = [grads["p_a"].reshape(4, -1, dm), grads["p_b"].reshape(4, -1, dm), grads["w_o"].reshape(4, -1, dm),
                      grads["w_u"], grads["w_d"].reshape(4, -1, dm)]
            self.swap = split_pair_swap("pair_swap_late", blocks, axes[1:])
            return self.swap.start([])

        def reduce_exchange(self, after):
            swapped = self.swap.wait(after)
            self.exchange = split_chip_exchange("chip_exchange_late", add_pair("add_pair_late", self.swap.srcs, swapped, core, axes[1:]))
            return self.exchange.start([])

        def reduce_finish(self, after):
            slots = self.exchange.wait(after)
            self.send = split_pair_send(add_chips("add_chips_late", slots, self.exchange.srcs, chip_no, axes[1:]))
            return self.send.start([])

        def input_grad_start(self, dw_main, dw_small):
            self.in_swap = split_pair_swap("pair_swap_in", [_join_w_in(dw_main, dw_small).reshape(4, -1, dm)], axes[:1])
            return self.in_swap.start([])

        def input_grad_exchange(self, after):
            swapped = self.in_swap.wait(after)
            self.in_exchange = split_chip_exchange("chip_exchange_in", add_pair("add_pair_in", self.in_swap.srcs, swapped, core, axes[:1]))
            return self.in_exchange.start([])

    hooks = Hooks()
    p1 = _lanes(gdn_dt_bias[0]) + _lanes(fox_f_bias[0], 2 * NH)
    p2 = _lanes(gdn_a_log[0])

    g = local_step(x.reshape(rows, dm), loss_target.reshape(rows, dm), norm_mix_g + token[0, 0], norm_mlp_g, gdn_norm_g,
                   fox_q_norm_g, fox_k_norm_g, p1, p2, conv_full, wt_main, wt_small, hooks, nseq, seq)

    others = hooks.send.wait(g["dx"])
    big_m = [tr(m_w_in), m_w_proj_gdn[0], m_w_proj_fox[0], m_w_out[0], m_w_up[0], m_w_down[0]]
    big_v = [tr(v_w_in), v_w_proj_gdn[0], v_w_proj_fox[0], v_w_out[0], v_w_up[0], v_w_down[0]]
    names = ["w_in", "w_proj_gdn", "w_proj_fox", "w_out", "w_up", "w_down"]
    big_res, big_grad = {}, {}
    for i in range(1, len(names)):
        big_grad[names[i]], *big_res[names[i]] = adamw_halves(f"adamw_{names[i]}", big[i], hooks.send.srcs[i - 1], others[i - 1],
                                                              big_m[i], big_v[i], core, axes[i])
    slots = hooks.in_exchange.wait(big_res[names[-1]][0])
    mine = add_chips("add_chips_in", slots, hooks.in_exchange.srcs, chip_no, axes[:1])
    res = adamw_halves("adamw_w_in", big[0], mine[0], pair_send(mine)[0], big_m[0], big_v[0], core, axes[0])
    big_grad["w_in"], *big_res["w_in"] = [jnp.swapaxes(r, 0, 1) for r in res]

    small_parts = [g["loss_acc"], g["g1"].reshape(8, LANES), g["g2"].reshape(8, LANES), g["gdn_ng"], g["qn"], g["kn"], g["p1"], g["p2"],
                   g["conv"].reshape(-1, LANES)]
    tiled = [jnp.pad(p, ((0, -p.shape[0] % 8), (0, 0))) for p in small_parts]
    red = all_reduce_small("reduce_small", jnp.concatenate(tiled, axis=0), slots)
    pos, red_parts = 0, []
    for p, t in zip(small_parts, tiled):
        red_parts.append(red[pos:pos + p.shape[0]])
        pos += t.shape[0]
    r_loss, r_g1, r_g2, r_gdn_ng, r_qn, r_kn, r_p1, r_p2, r_conv = red_parts
    loss = jnp.sum(r_loss)
    g_conv = lax.dynamic_slice_in_dim(r_conv.reshape(4, 4, conv_cols), chip, 1, axis=1).reshape(4, conv_cols)
    small_grads = [r_g1.reshape(1, dm), r_p2[:, :NH], r_p1[:, :NH], r_gdn_ng, r_qn, r_kn, r_p1[:, 2 * NH:3 * NH], r_g2.reshape(1, dm)]
    small_w = [norm_mix_g, gdn_a_log, gdn_dt_bias, gdn_norm_g, fox_q_norm_g, fox_k_norm_g, fox_f_bias, norm_mlp_g]
    small_m = [m_norm_mix_g, m_gdn_a_log, m_gdn_dt_bias, m_gdn_norm_g, m_fox_q_norm_g, m_fox_k_norm_g, m_fox_f_bias, m_norm_mlp_g]
    small_v = [v_norm_mix_g, v_gdn_a_log, v_gdn_dt_bias, v_gdn_norm_g, v_fox_q_norm_g, v_fox_k_norm_g, v_fox_f_bias, v_norm_mlp_g]

    def pack(parts):
        flat = jnp.concatenate([jnp.pad(p.reshape(-1), (0, -p.size % LANES)) for p in parts])
        return jnp.pad(flat, (0, -flat.size % (8 * LANES))).reshape(-1, LANES)

    packed = adamw("adamw_small", pack(small_w + [gdn_conv_w[0]]), pack(small_grads + [g_conv]),
                   pack(small_m + [m_gdn_conv_w[0]]), pack(small_v + [v_gdn_conv_w[0]]))

    def unpack(flat2d):
        flat, pos, res = flat2d.reshape(-1), 0, []
        for p in small_w + [gdn_conv_w[0]]:
            res.append(flat[pos:pos + p.size].reshape(p.shape))
            pos += p.size + (-p.size % LANES)
        return res

    s_delta, s_m, s_v = (unpack(a) for a in packed)

    order = ["norm_mix_g", "w_in", "gdn_conv_w", "gdn_a_log", "gdn_dt_bias", "gdn_norm_g", "fox_q_norm_g", "fox_k_norm_g",
             "fox_f_bias", "w_proj_gdn", "w_proj_fox", "w_out", "norm_mlp_g", "w_up", "w_down"]
    small_names = ["norm_mix_g", "gdn_a_log", "gdn_dt_bias", "gdn_norm_g", "fox_q_norm_g", "fox_k_norm_g", "fox_f_bias", "norm_mlp_g",
                   "gdn_conv_w"]
    small_idx = {nm: i for i, nm in enumerate(small_names)}
    shapes = dict(zip(order, (a.shape for a in (norm_mix_g, w_in, gdn_conv_w, gdn_a_log, gdn_dt_bias, gdn_norm_g, fox_q_norm_g,
                                                 fox_k_norm_g, fox_f_bias, w_proj_gdn, w_proj_fox, w_out, norm_mlp_g, w_up, w_down))))
    grads_out, delta_out, m_out, v_out = [], [], [], []
    for nm in order:
        if nm in big_res:
            d, mm, vv = big_res[nm]
            gr = big_grad[nm]
        else:
            i = small_idx[nm]
            gr = (small_grads + [g_conv])[i]
            d, mm, vv = s_delta[i], s_m[i], s_v[i]
        for lst, val in ((grads_out, gr), (delta_out, d), (m_out, mm), (v_out, vv)):
            lst.append(val.reshape(shapes[nm]))
    return (loss, g["dx"].reshape(x.shape), *grads_out, *delta_out, *m_out, *v_out)
```

```python
import functools

import jax
import jax.numpy as jnp
from jax import lax
from jax.experimental import pallas as pl
from jax.experimental.pallas import tpu as pltpu

F32 = jnp.float32
BF16 = jnp.bfloat16
LANES = 128
NH = 8
EPS = 1e-6
GDN_CHUNK = 64
GDN_ROWS = 256
GDN_BASE = 16
ROW_TILE = 512
CONV_HEADS = 2
ATT_TILE = 512
NEG = -1e30
VMEM_LIMIT_BYTES = 48 * 1024 * 1024
HI = lax.Precision.HIGHEST
LO = lax.Precision.DEFAULT
MESH = pl.DeviceIdType.MESH
ANY = pl.BlockSpec(memory_space=pl.ANY)

ADAM_LR, ADAM_B1, ADAM_B2, ADAM_EPS, ADAM_WD, ADAM_STEP = 0.001, 0.9, 0.999, 1e-08, 0.01, 10


def _params(n_grid):
    return pltpu.CompilerParams(dimension_semantics=("arbitrary",) * n_grid,
                                vmem_limit_bytes=VMEM_LIMIT_BYTES)


def _dot(a, b, dims, precision=None):
    dn = {"nn": (((1,), (0,)), ((), ())), "nt": (((1,), (1,)), ((), ())), "tn": (((0,), (0,)), ((), ()))}[dims]
    return lax.dot_general(a, b, dn, precision=precision, preferred_element_type=F32)


def _iota(shape, dim):
    return lax.broadcasted_iota(jnp.int32, shape, dim)


def _split(x, parts):
    out = []
    for _ in range(parts - 1):
        hi = x.astype(BF16)
        out.append(hi)
        x = x - hi.astype(F32)
    return out + [x.astype(BF16)]


def _dot_mask(mask, b, dims):
    m16 = mask.astype(BF16)
    b1, b2, b3 = _split(b, 3)
    return _dot(m16, b1, dims) + (_dot(m16, b2, dims) + _dot(m16, b3, dims))


@jax.custom_vjp
def mm_mask(mask, b):
    return _dot_mask(mask, b, "nn")


mm_mask.defvjp(lambda mask, b: (_dot_mask(mask, b, "nn"), mask),
               lambda mask, g: (jnp.zeros_like(mask), _dot_mask(mask, g, "tn")))


def matmul(name, a, b, dims, out_dtype, add=None, tm=1024, tn=1024, tk=512, col_blocks=None,
           extras=(), epilogue=None, out_dtypes=None, k_part=None, after=()):
    if col_blocks and dims != "tn":
        nb, b_rows, bw = b.shape
        b_shape = (b_rows, nb * bw)
    else:
        b_shape = b.shape
    if dims == "nn":
        (m, k), (_, n) = a.shape, b_shape
    elif dims == "nt":
        (m, k), (n, _) = a.shape, b_shape
    else:
        (k, m), (_, n) = a.shape, b_shape
    if k <= 1024:
        tk = k
    tm, tn, tk = min(tm, m), min(tn, n), min(tk, k)
    assert m % tm == 0 and n % tn == 0 and k % tk == 0, (name, m, n, k)
    k0, nk = (0, k // tk) if k_part is None else (k_part[0] * (k // tk // k_part[1]), k // tk // k_part[1])
    assert k_part is None or (dims == "nn" and not col_blocks and (k // tk) % k_part[1] == 0)
    a_spec = pl.BlockSpec((tk, tm), lambda i, j, kk: (kk, i)) if dims == "tn" else pl.BlockSpec((tm, tk), lambda i, j, kk: (i, kk + k0))
    b_spec = pl.BlockSpec((tn, tk), lambda i, j, kk: (j, kk)) if dims == "nt" else pl.BlockSpec((tk, tn), lambda i, j, kk: (kk + k0, j))
    o_spec = pl.BlockSpec((tm, tn), lambda i, j, kk: (i, j))
    out_shape = (m, n)
    if col_blocks and dims == "nn":
        per = bw // tn
        assert bw % tn == 0
        b_spec = pl.BlockSpec((None, tk, tn), lambda i, j, kk: (j // per, kk, j % per))
    elif col_blocks and dims == "nt":
        per = bw // tk
        assert bw % tk == 0
        b_spec = pl.BlockSpec((None, tn, tk), lambda i, j, kk: (kk // per, j, kk % per))
    elif col_blocks:
        bw = n // col_blocks
        per = bw // tn
        assert bw % tn == 0 and add is None
        o_spec = pl.BlockSpec((None, tm, tn), lambda i, j, kk: (j // per, i, j % per))
        out_shape = (col_blocks, m, bw)
    extras = list(extras) + ([add] if add is not None else [])
    if add is not None:
        assert epilogue is None
        epilogue = lambda r, *e: [r + e[-1]]
    out_dtypes = [out_dtype] if epilogue is None or out_dtypes is None else list(out_dtypes)
    n_ex, n_out = len(extras), len(out_dtypes)

    def body(*refs):
        a_ref, b_ref = refs[0], refs[1]
        ex_refs, o_refs = refs[2:2 + n_ex], refs[2 + n_ex + len(after):2 + n_ex + len(after) + n_out]

        def finish(r):
            res = [r] if epilogue is None else epilogue(r, *[e[...] for e in ex_refs])
            for o_ref, v in zip(o_refs, res):
                o_ref[...] = v.astype(o_ref.dtype)

        if nk == 1:
            finish(_dot(a_ref[...], b_ref[...], dims))
            return
        acc_ref = refs[-1]
        kk = pl.program_id(2)

        @pl.when(kk == 0)
        def _():
            acc_ref[...] = jnp.zeros_like(acc_ref)

        acc_ref[...] += _dot(a_ref[...], b_ref[...], dims)

        @pl.when(kk == nk - 1)
        def _():
            finish(acc_ref[...])

    res = pl.pallas_call(
        body, name=name, grid=(m // tm, n // tn, nk), in_specs=[a_spec, b_spec] + [o_spec] * n_ex + [ANY] * len(after),
        out_specs=[o_spec] * n_out, out_shape=[jax.ShapeDtypeStruct(out_shape, dt) for dt in out_dtypes],
        scratch_shapes=[pltpu.VMEM((tm, tn), F32)] if nk > 1 else [], compiler_params=_params(3),
    )(a, b, *extras, *after)
    return res[0] if n_out == 1 else res


def _ew_spec(kind, off, width, tb, hp, order, shape=None):
    def ih(g0, g1):
        return (g0, g1) if order == "ih" else (g1, g0)

    assert off % hp == 0 or kind in ("row", "par")
    if kind == "row":
        return pl.BlockSpec((tb, width), lambda g0, g1: (ih(g0, g1)[0], off))
    if kind == "rowh":
        return pl.BlockSpec((tb, hp * width), lambda g0, g1: (ih(g0, g1)[0], ih(g0, g1)[1] + off // hp))
    if kind == "par":
        return pl.BlockSpec(shape, lambda g0, g1: (0, 0))
    if kind == "parh":
        return pl.BlockSpec((shape[0], hp * width), lambda g0, g1: (0, ih(g0, g1)[1] + off // hp))
    raise ValueError(kind)


def _ew_grid(rows, tb, nh, hp, order):
    assert nh % hp == 0 and rows % tb == 0
    return (rows // tb, nh // hp) if order == "ih" else (nh // hp, rows // tb)


def _ew_load(ref, kind, width, hh):
    if kind in ("row", "par"):
        return ref[...].astype(F32)
    return ref[:, hh * width:(hh + 1) * width].astype(F32)


def ew_fwd(name, f, ins, outs, rows, nh=1, tb=ROW_TILE, order="ih", hp=None, after=()):
    hp = nh if hp is None else hp
    n_in = len(ins)

    def body(*refs):
        hb = pl.program_id(1) if order == "ih" else pl.program_id(0)
        for hh in range(hp):
            h = hh if hp == nh else hb * hp + hh
            vals = [_ew_load(r, kd, w, hh) for r, (_, kd, _, w) in zip(refs[:n_in], ins)]
            res = f(h, *vals)
            for r, v, (_, kd, w, _) in zip(refs[n_in + len(after):], res, outs):
                if kd == "row":
                    assert hp == 1
                    r[...] = v.astype(r.dtype)
                else:
                    r[:, hh * w:(hh + 1) * w] = v.astype(r.dtype)

    in_specs = [_ew_spec(kd, off, w, tb, hp, order, a.shape) for (a, kd, off, w) in ins]
    out_specs = [_ew_spec(kd, 0, w, tb, hp, order) for (_, kd, w, _) in outs]
    out_shape = [jax.ShapeDtypeStruct((rows, tw), dt) for (tw, _, _, dt) in outs]
    return pl.pallas_call(
        body, name=name, grid=_ew_grid(rows, tb, nh, hp, order), in_specs=in_specs + [ANY] * len(after), out_specs=out_specs,
        out_shape=out_shape, compiler_params=_params(2),
    )(*[a for (a, _, _, _) in ins], *after)


def ew_bwd(name, f, ins, cts, extras, emit, outs, rows, nh=1, tb=ROW_TILE, order="ih", hp=None):
    hp = nh if hp is None else hp
    n_in = len(ins)
    flat_cts = [d for group in cts for d in group]
    n_ct, n_ex = len(flat_cts), len(extras)

    def body(*refs):
        g0, g1 = pl.program_id(0), pl.program_id(1)
        hb = g1 if order == "ih" else g0
        out_refs = refs[n_in + n_ct + n_ex:]
        shared = [None] * len(outs)

        def store(r, v, first, sl=None):
            def put(val, add):
                if sl is None:
                    r[...] = (r[...] + val if add else val).astype(r.dtype)
                else:
                    r[:, sl] = (r[:, sl] + val if add else val).astype(r.dtype)

            if first is None:
                put(v, False)
            else:
                pl.when(first)(lambda: put(v, False))
                pl.when(jnp.logical_not(first))(lambda: put(v, True))

        for hh in range(hp):
            h = hh if hp == nh else hb * hp + hh
            vals = [_ew_load(r, kd, w, hh) for r, (_, kd, _, w) in zip(refs[:n_in], ins)]
            ct_refs = list(zip(refs[n_in:n_in + n_ct], flat_cts))
            ct_vals, pos = [], 0
            for group in cts:
                v = None
                for r, (_, kd, _, w) in ct_refs[pos:pos + len(group)]:
                    t = _ew_load(r, kd, w, hh)
                    v = t if v is None else v + t
                pos += len(group)
                ct_vals.append(v)
            ex_vals = [_ew_load(r, kd, w, hh) for r, (_, kd, _, w) in zip(refs[n_in + n_ct:n_in + n_ct + n_ex], extras)]
            _, vjp = jax.vjp(lambda *a: f(h, *a), *vals)
            res = emit(vjp(tuple(ct_vals)), ex_vals)
            for idx, (r, v, (_, kd, w, _, acc)) in enumerate(zip(out_refs, res, outs)):
                if kd in ("row", "par"):
                    shared[idx] = v if shared[idx] is None else shared[idx] + v
                else:
                    store(r, v, (g1 == 0) if acc == "inner" else None, slice(hh * w, (hh + 1) * w))
        for idx, (r, (_, kd, _, _, acc)) in enumerate(zip(out_refs, outs)):
            if kd in ("row", "par"):
                assert acc == "all" or hp == nh
                store(r, shared[idx], jnp.logical_and(g0 == 0, g1 == 0) if acc == "all" else None)

    operands = list(ins) + flat_cts + list(extras)
    in_specs = [_ew_spec(kd, off, w, tb, hp, order, a.shape) for (a, kd, off, w) in operands]
    out_specs = [_ew_spec(kd, 0, w, tb, hp, order, shp) for (shp, kd, w, _, _) in outs]
    out_shape = [jax.ShapeDtypeStruct(shp, dt) for (shp, _, _, dt, _) in outs]
    return pl.pallas_call(
        body, name=name, grid=_ew_grid(rows, tb, nh, hp, order), in_specs=in_specs, out_specs=out_specs,
        out_shape=out_shape, compiler_params=_params(2),
    )(*[a for (a, _, _, _) in operands])


def f_rms(h, x, g):
    r = lax.rsqrt(jnp.mean(x * x, axis=-1, keepdims=True) + EPS)
    return (x * r * g,)


def _softplus(z):
    return jnp.maximum(z, 0.0) + jnp.log1p(jnp.exp(-jnp.abs(z)))


def f_small(h, sp, p1, p2):
    lane = _iota(sp.shape, 1)
    z = sp + p1
    g = -jnp.exp(p2) * _softplus(z)
    beta = jax.nn.sigmoid(z)
    logf = -_softplus(-z)
    return (jnp.where(lane < NH, g, jnp.where(lane < 2 * NH, beta, jnp.where(lane < 3 * NH, logf, 0.0))),)


def _pick(x, lane_id):
    lane = _iota(x.shape, 1)
    col = jnp.sum(jnp.where(lane == lane_id, x, 0.0), axis=1, keepdims=True)
    return jnp.broadcast_to(col, x.shape)


def f_bcast(h, so, cs):
    return _pick(so, h), _pick(so, h + NH), _pick(cs, h + 2 * NH)


def _shift_down(s):
    def down(x):
        return jnp.where(_iota(x.shape, 0) >= s, pltpu.roll(x, s, 0), 0.0)

    def up(g):
        n = g.shape[0]
        return jnp.where(_iota(g.shape, 0) < n - s, pltpu.roll(g, n - s, 0), 0.0)

    @jax.custom_vjp
    def shift(x):
        return down(x)

    shift.defvjp(lambda x: (down(x), None), lambda _, g: (up(g),))
    return shift


def _silu(x):
    return x * jax.nn.sigmoid(x)


def make_f_conv(mode):
    sh1, sh2, sh3 = _shift_down(1), _shift_down(2), _shift_down(3)

    def f(h, x, w):
        sub = _iota(w.shape, 0)

        def tap(i):
            return jnp.sum(jnp.where(sub == i, w, 0.0), axis=0, keepdims=True)

        y = sh3(x) * tap(0)
        y = y + sh2(x) * tap(1)
        y = y + sh1(x) * tap(2)
        y = y + x * tap(3)
        s = _silu(y)
        if mode == "v":
            return (s,)
        n = s * lax.rsqrt(jnp.sum(s * s, axis=-1, keepdims=True) + EPS)
        if mode == "q":
            n = n * (LANES ** -0.5)
        return (n,)

    return f


def f_post(h, o, z, g):
    r = lax.rsqrt(jnp.mean(o * o, axis=-1, keepdims=True) + EPS)
    return (o * r * g * _silu(z),)


def f_merge(h, ga, gb, ya, yb):
    return (jax.nn.sigmoid(ga) * ya + jax.nn.sigmoid(gb) * yb,)


def f_delta(h, do, o):
    return (jnp.broadcast_to(jnp.sum(do * o, axis=1, keepdims=True), o.shape),)


def cumsum_time(name, x, nseq, seq, reverse):
    nb = seq // LANES

    def body(x_ref, o_ref):
        r, c = _iota((LANES, LANES), 0), _iota((LANES, LANES), 1)
        tri = jnp.where((r <= c) if reverse else (r >= c), 1.0, 0.0).astype(F32)
        carry = jnp.zeros((1, LANES), F32)
        for b in (range(nb - 1, -1, -1) if reverse else range(nb)):
            blk = x_ref[b * LANES:(b + 1) * LANES, :]
            o_ref[b * LANES:(b + 1) * LANES, :] = _dot_mask(tri, blk, "nn") + carry
            carry = carry + jnp.sum(blk, axis=0, keepdims=True)

    spec = pl.BlockSpec((seq, LANES), lambda s: (s, 0))
    return pl.pallas_call(body, name=name, grid=(nseq,), in_specs=[spec], out_specs=spec,
                          out_shape=jax.ShapeDtypeStruct(x.shape, F32), compiler_params=_params(1))(x)


def transpose_time(name, x, nseq, seq):
    def body(x_ref, o_ref):
        o_ref[...] = x_ref[...].T

    return pl.pallas_call(
        body, name=name, grid=(nseq,), in_specs=[pl.BlockSpec((seq, LANES), lambda s: (s, 0))],
        out_specs=pl.BlockSpec((LANES, seq), lambda s: (s, 0)),
        out_shape=jax.ShapeDtypeStruct((nseq * LANES, seq), F32), compiler_params=_params(1))(x)


def _gdn_masks():
    n = GDN_ROWS
    r, c = _iota((n, n), 0), _iota((n, n), 1)
    shift = GDN_CHUNK.bit_length() - 1
    same = lax.shift_right_logical(r, shift) == lax.shift_right_logical(c, shift)
    return r, c, same


def _gdn_decay(gb):
    r, c, same = _gdn_masks()
    seg_tril = jnp.where(jnp.logical_and(same, r >= c), 1.0, 0.0).astype(F32)
    g_cum = mm_mask(seg_tril, gb)
    lane0 = _iota(g_cum.shape, 1) == 0
    g_col = jnp.sum(jnp.where(lane0, g_cum, 0.0), axis=1, keepdims=True)
    g_row = jnp.sum(jnp.where(r == c, jnp.broadcast_to(g_col, (GDN_ROWS, GDN_ROWS)), 0.0), axis=0, keepdims=True)
    return g_cum, g_col - g_row


def gdn_f1(q, k, gb, bb):
    r, c, same = _gdn_masks()
    strict = jnp.logical_and(same, r > c)
    _, diff = _gdn_decay(gb)
    lane0 = _iota(bb.shape, 1) == 0
    beta_col = jnp.sum(jnp.where(lane0, bb, 0.0), axis=1, keepdims=True)
    kk = _dot(k, k, "nt", LO)
    return jnp.where(strict, beta_col * kk * jnp.exp(jnp.where(strict, diff, 0.0)), 0.0)


def gdn_f2(t_corr, q, k, v, gb, bb):
    r, c, same = _gdn_masks()
    incl = jnp.logical_and(same, r >= c)
    g_cum, diff = _gdn_decay(gb)
    decay = jnp.where(incl, jnp.exp(jnp.where(incl, diff, 0.0)), 0.0)
    e_g = jnp.exp(g_cum)
    v_beta, k_beta = v * bb, k * bb * e_g
    value = v_beta + _dot(t_corr, v_beta, "nn", LO)
    k_cum = k_beta + _dot(t_corr, k_beta, "nn", LO)
    attn = _dot(q, k, "nt", LO) * decay
    g_last = mm_mask(jnp.where(same, 1.0, 0.0).astype(F32), gb)
    return value, k_cum, attn, q * e_g, k * jnp.exp(g_last - g_cum)


def tri_inverse(mats):
    n = GDN_ROWS
    r, c = _iota((n, n), 0), _iota((n, n), 1)
    shift = GDN_BASE.bit_length() - 1
    blk = lax.shift_right_logical(r, shift) == lax.shift_right_logical(c, shift)
    each = lambda fn, *lists: [fn(*xs) for xs in zip(*lists)]
    mm = lambda x, y: _dot(x, y, "nn", LO)
    d = each(lambda a: jnp.where(blk, a, 0.0), mats)
    lo = each(lambda a, dd: a - dd, mats, d)
    p = each(lambda dd: -dd, d)
    c_d = p
    for _ in range(shift - 1):
        p = each(mm, p, p)
        c_d = each(lambda cd, pp, prod: cd + pp + prod, c_d, p, each(mm, c_d, p))
    assert GDN_CHUNK // GDN_BASE == 4
    nmat = each(lambda l, prod: l + prod, lo, each(mm, c_d, lo))
    n2 = each(mm, nmat, nmat)
    c_n = each(lambda nn2, nm, prod: (nn2 - nm) - prod, n2, nmat, each(mm, nmat, n2))
    return each(lambda cn, cd, prod: cn + cd + prod, c_n, c_d, each(mm, c_n, c_d))


GDN_AHP = 2


def _gdn_a_specs():
    blk = pl.BlockSpec((GDN_ROWS, GDN_AHP * LANES), lambda i, h: (i, h))
    sq = pl.BlockSpec((GDN_ROWS, GDN_AHP * GDN_ROWS), lambda i, h: (i, h))
    return blk, sq


def _head(ref, hh):
    width = ref.shape[1] // GDN_AHP
    return ref.at[:, hh * width:(hh + 1) * width]


def gdn_a_fwd(q, k, v, gb, bb, rows):
    blk, sq = _gdn_a_specs()

    def body(q_ref, k_ref, v_ref, gb_ref, bb_ref, val_ref, kc_ref, at_ref, qd_ref, kd_ref, t_ref):
        heads = [[_head(r, hh)[...] for r in (q_ref, k_ref, v_ref, gb_ref, bb_ref)] for hh in range(GDN_AHP)]
        t_corr = tri_inverse([gdn_f1(qv, kv, gv, bv) for qv, kv, vv, gv, bv in heads])
        for hh, (qv, kv, vv, gv, bv) in enumerate(heads):
            res = gdn_f2(t_corr[hh], qv, kv, vv, gv, bv)
            for r, x in zip((val_ref, kc_ref, at_ref, qd_ref, kd_ref, t_ref), (*res, t_corr[hh])):
                _head(r, hh)[...] = x.astype(r.dtype)

    wide = lambda dt: jax.ShapeDtypeStruct((rows, NH * LANES), dt)
    square = jax.ShapeDtypeStruct((rows, NH * GDN_ROWS), BF16)
    return pl.pallas_call(
        body, name="gdn_a_fwd", grid=(rows // GDN_ROWS, NH // GDN_AHP), in_specs=[blk] * 5,
        out_specs=[blk, blk, sq, blk, blk, sq], out_shape=[wide(F32), wide(BF16), square, wide(BF16), wide(BF16), square],
        compiler_params=_params(2))(q, k, v, gb, bb)


def gdn_a_bwd(q, k, v, gb, bb, t_inv, dval, dkc, dat, dqd, dkd, dgb_b, rows):
    blk, sq = _gdn_a_specs()

    def body(q_ref, k_ref, v_ref, gb_ref, bb_ref, t_ref, dval_ref, dkc_ref, dat_ref, dqd_ref, dkd_ref, dgbb_ref,
             dq_ref, dk_ref, dv_ref, dgb_ref, dbb_ref):
        for hh in range(GDN_AHP):
            qv, kv, vv, gv, bv = (_head(r, hh)[...] for r in (q_ref, k_ref, v_ref, gb_ref, bb_ref))
            tv = _head(t_ref, hh)[...].astype(F32)
            _, vjp1 = jax.vjp(gdn_f1, qv, kv, gv, bv)
            _, vjp2 = jax.vjp(gdn_f2, tv, qv, kv, vv, gv, bv)
            dt, dq2, dk2, dv2, dgb2, dbb2 = vjp2(tuple(_head(r, hh)[...] for r in (dval_ref, dkc_ref, dat_ref, dqd_ref, dkd_ref)))
            left = dt + _dot(tv, dt, "tn", LO)
            da = -(left + _dot(left, tv, "nt", LO))
            dq1, dk1, dgb1, dbb1 = vjp1(da)
            _head(dq_ref, hh)[...] = dq1 + dq2
            _head(dk_ref, hh)[...] = dk1 + dk2
            _head(dv_ref, hh)[...] = dv2
            _head(dgb_ref, hh)[...] = dgb1 + dgb2 + _head(dgbb_ref, hh)[...]
            _head(dbb_ref, hh)[...] = dbb1 + dbb2

    wide = jax.ShapeDtypeStruct((rows, NH * LANES), F32)
    return pl.pallas_call(
        body, name="gdn_a_bwd", grid=(rows // GDN_ROWS, NH // GDN_AHP),
        in_specs=[blk] * 5 + [sq, blk, blk, sq, blk, blk, blk], out_specs=[blk] * 5, out_shape=[wide] * 5,
        compiler_params=_params(2))(q, k, v, gb, bb, t_inv, dval, dkc, dat, dqd, dkd, dgb_b)


N_CH = GDN_ROWS // GDN_CHUNK


GDN_HP = 8


def gdn_fb(*args):
    per_head = 6 * N_CH
    states = list(args[GDN_HP * per_head:])
    outs = [[None] * N_CH for _ in range(GDN_HP)]
    zero = jnp.zeros((GDN_CHUNK, LANES), F32)
    for c in range(N_CH):
        for hh in range(GDN_HP):
            val, kc, at, qd, kd, gb = (args[hh * per_head + i * N_CH + c] for i in range(6))
            s = states[hh]
            v_new = val - _dot(kc, s, "nn", LO)
            v_pad = jnp.concatenate([zero] * c + [v_new] + [zero] * (N_CH - 1 - c), axis=0)
            outs[hh][c] = _dot(qd, s, "nn", LO) + _dot(at, v_pad, "nn", LO)
            dec = jnp.exp(jnp.sum(gb, axis=0, keepdims=True))
            states[hh] = s * dec + _dot(kd, v_new, "tn", LO)
    return (*[o for head in outs for o in head], *states)


def _gdn_piece(ref, hh, c):
    width = ref.shape[1] // GDN_HP
    return ref.at[c * GDN_CHUNK:(c + 1) * GDN_CHUNK, hh * width:(hh + 1) * width]


def _gdn_pieces(refs, hh):
    return [_gdn_piece(r, hh, c)[...].astype(F32) for r in refs for c in range(N_CH)]


def _gdn_b_specs(nb, rev):
    def blk_row(s, j):
        return s * nb + (nb - 1 - j if rev else j)

    blk = pl.BlockSpec((GDN_ROWS, GDN_HP * LANES), lambda s, hb, j: (blk_row(s, j), hb))
    sq = pl.BlockSpec((GDN_ROWS, GDN_HP * GDN_ROWS), lambda s, hb, j: (blk_row(s, j), hb))
    snap = pl.BlockSpec((GDN_HP * LANES, LANES), lambda s, hb, j: (blk_row(s, j) * (NH // GDN_HP) + hb, 0))
    return blk, sq, snap


def gdn_b_fwd(val, kc, at, qd, kd, gb, nseq, seq):
    nb = seq // GDN_ROWS
    rows = nseq * seq
    blk, sq, snap = _gdn_b_specs(nb, False)

    def body(val_ref, kc_ref, at_ref, qd_ref, kd_ref, gb_ref, o_ref, snap_ref, s_ref):
        @pl.when(pl.program_id(2) == 0)
        def _():
            s_ref[...] = jnp.zeros_like(s_ref)

        states = [s_ref[hh] for hh in range(GDN_HP)]
        for hh in range(GDN_HP):
            snap_ref[hh * LANES:(hh + 1) * LANES, :] = states[hh]
        pieces = [p for hh in range(GDN_HP) for p in _gdn_pieces([val_ref, kc_ref, at_ref, qd_ref, kd_ref, gb_ref], hh)]
        res = gdn_fb(*pieces, *states)
        for hh in range(GDN_HP):
            for c in range(N_CH):
                _gdn_piece(o_ref, hh, c)[...] = res[hh * N_CH + c]
            s_ref[hh] = res[GDN_HP * N_CH + hh]

    return pl.pallas_call(
        body, name="gdn_b_fwd", grid=(nseq, NH // GDN_HP, nb), in_specs=[blk, blk, sq, blk, blk, blk], out_specs=[blk, snap],
        out_shape=[jax.ShapeDtypeStruct((rows, NH * LANES), F32), jax.ShapeDtypeStruct((nseq * nb * NH * LANES, LANES), F32)],
        scratch_shapes=[pltpu.VMEM((GDN_HP, LANES, LANES), F32)], compiler_params=_params(3))(val, kc, at, qd, kd, gb)


def gdn_b_bwd(val, kc, at, qd, kd, gb, snaps, do, nseq, seq):
    nb = seq // GDN_ROWS
    rows = nseq * seq
    blk, sq, snap = _gdn_b_specs(nb, True)

    def body(val_ref, kc_ref, at_ref, qd_ref, kd_ref, gb_ref, snap_ref, do_ref,
             dval_ref, dkc_ref, dat_ref, dqd_ref, dkd_ref, dgb_ref, ds_ref):
        @pl.when(pl.program_id(2) == 0)
        def _():
            ds_ref[...] = jnp.zeros_like(ds_ref)

        pieces = [p for hh in range(GDN_HP) for p in _gdn_pieces([val_ref, kc_ref, at_ref, qd_ref, kd_ref, gb_ref], hh)]
        states = [snap_ref[hh * LANES:(hh + 1) * LANES, :] for hh in range(GDN_HP)]
        _, vjp = jax.vjp(gdn_fb, *pieces, *states)
        cts = [p for hh in range(GDN_HP) for p in _gdn_pieces([do_ref], hh)] + [ds_ref[hh] for hh in range(GDN_HP)]
        grads = vjp(tuple(cts))
        for hh in range(GDN_HP):
            for i, r in enumerate([dval_ref, dkc_ref, dat_ref, dqd_ref, dkd_ref, dgb_ref]):
                for c in range(N_CH):
                    _gdn_piece(r, hh, c)[...] = grads[hh * 6 * N_CH + i * N_CH + c]
            ds_ref[hh] = grads[GDN_HP * 6 * N_CH + hh]

    wide = jax.ShapeDtypeStruct((rows, NH * LANES), F32)
    square = jax.ShapeDtypeStruct((rows, NH * GDN_ROWS), F32)
    return pl.pallas_call(
        body, name="gdn_b_bwd", grid=(nseq, NH // GDN_HP, nb), in_specs=[blk, blk, sq, blk, blk, blk, snap, blk],
        out_specs=[blk, blk, sq, blk, blk, blk], out_shape=[wide, wide, square, wide, wide, wide],
        scratch_shapes=[pltpu.VMEM((GDN_HP, LANES, LANES), F32)], compiler_params=_params(3))(val, kc, at, qd, kd, gb, snaps, do)


FOX_Q, FOX_K, FOX_V = 4 * NH, 5 * NH, 6 * NH
FOX_SCALE = LANES ** -0.5


def _head_row(ct_ref, h, off, width):
    blk = ct_ref[:, pl.ds(off, width)]
    return jnp.sum(jnp.where(_iota(blk.shape, 0) == h, blk, 0.0), axis=0, keepdims=True)


def _col(x):
    return jnp.max(x, axis=1, keepdims=True)


def _row(x):
    return jnp.max(x.T, axis=0, keepdims=True)


def _causal(shape, q_dim):
    return _iota(shape, q_dim) >= _iota(shape, 1 - q_dim)


def fox_fwd(qn, kn, proj, ct, nseq, seq):
    tq = tk = min(ATT_TILE, seq)
    nq = seq // tq
    rows = nseq * seq
    qblk = pl.BlockSpec((tq, LANES), lambda s, h, i: (s * nq + i, h))
    full = pl.BlockSpec((seq, LANES), lambda s, h, i: (s, h))
    vfull = pl.BlockSpec((seq, LANES), lambda s, h, i: (s, h + FOX_V))
    ctb = pl.BlockSpec((NH, seq), lambda s, h, i: (s * (LANES // NH) + 2, 0))

    def body(q_ref, k_ref, v_ref, ct_ref, o_ref, o16_ref, lse_ref):
        h, i = pl.program_id(1), pl.program_id(2)
        q = q_ref[...]

        def step(j, carry, diag):
            m, l, acc = carry
            off = pl.multiple_of(j * tk, tk)
            s = _dot(q, k_ref[pl.ds(off, tk), :], "nt") * FOX_SCALE - _head_row(ct_ref, h, off, tk)
            if diag:
                s = jnp.where(_causal(s.shape, 0), s, NEG)
            m_new = jnp.maximum(m, jnp.max(s, axis=1, keepdims=True))
            p = jnp.exp(s - m_new)
            alpha = jnp.exp(m - m_new)
            l = alpha * l + jnp.sum(p, axis=1, keepdims=True)
            acc = alpha * acc + _dot(p.astype(BF16), v_ref[pl.ds(off, tk), :].astype(BF16), "nn")
            return m_new, l, acc

        init = (jnp.full((tq, 1), NEG, F32), jnp.zeros((tq, 1), F32), jnp.zeros((tq, LANES), F32))
        carry = lax.fori_loop(0, i, lambda j, c: step(j, c, False), init)
        m, l, acc = step(i, carry, True)
        o = acc / l
        o_ref[...] = o
        o16_ref[...] = o.astype(BF16)
        lse_ref[...] = jnp.broadcast_to(m + jnp.log(l), (tq, LANES))

    wide = (rows, NH * LANES)
    return pl.pallas_call(
        body, name="fox_fwd", grid=(nseq, NH, nq), in_specs=[qblk, full, vfull, ctb], out_specs=[qblk] * 3,
        out_shape=[jax.ShapeDtypeStruct(wide, F32), jax.ShapeDtypeStruct(wide, BF16), jax.ShapeDtypeStruct(wide, F32)],
        compiler_params=_params(3))(qn, kn, proj, ct)


def fox_dq(qn, kn, proj, ct, do, lse, delta, nseq, seq):
    tq = tk = min(ATT_TILE, seq)
    nq = seq // tq
    rows = nseq * seq
    qblk = pl.BlockSpec((tq, LANES), lambda s, h, i: (s * nq + i, h))
    full = pl.BlockSpec((seq, LANES), lambda s, h, i: (s, h))
    vfull = pl.BlockSpec((seq, LANES), lambda s, h, i: (s, h + FOX_V))
    ctb = pl.BlockSpec((NH, seq), lambda s, h, i: (s * (LANES // NH) + 2, 0))

    def body(q_ref, k_ref, v_ref, ct_ref, do_ref, lse_ref, dl_ref, dq_ref, dc_ref):
        h, i = pl.program_id(1), pl.program_id(2)
        q = q_ref[...]
        lse, delta = _col(lse_ref[...]), _col(dl_ref[...])
        do16 = do_ref[...].astype(BF16)

        def step(j, carry, diag):
            dq, dc = carry
            off = pl.multiple_of(j * tk, tk)
            k = k_ref[pl.ds(off, tk), :]
            p = jnp.exp(_dot(q, k, "nt") * FOX_SCALE - _head_row(ct_ref, h, off, tk) - lse)
            if diag:
                p = jnp.where(_causal(p.shape, 0), p, 0.0)
            dp = _dot(do16, v_ref[pl.ds(off, tk), :].astype(BF16), "nt")
            ds = p * (dp - delta)
            return dq + _dot(ds.astype(BF16), k, "nn"), dc + jnp.sum(ds, axis=1, keepdims=True)

        init = (jnp.zeros((tq, LANES), F32), jnp.zeros((tq, 1), F32))
        dq, dc = step(i, lax.fori_loop(0, i, lambda j, c: step(j, c, False), init), True)
        dq_ref[...] = dq * FOX_SCALE
        dc_ref[...] = jnp.where(_iota((tq, LANES), 1) == 0, dc, 0.0)

    wide = jax.ShapeDtypeStruct((rows, NH * LANES), F32)
    return pl.pallas_call(
        body, name="fox_dq", grid=(nseq, NH, nq), in_specs=[qblk, full, vfull, ctb, qblk, qblk, qblk],
        out_specs=[qblk, qblk], out_shape=[wide, wide], compiler_params=_params(3))(qn, kn, proj, ct, do, lse, delta)


def fox_dkv(qn, kn, proj, cb, do, lse, delta, nseq, seq):
    tq = tk = min(ATT_TILE, seq)
    nq = seq // tq
    rows = nseq * seq
    kblk = pl.BlockSpec((tk, LANES), lambda s, h, j: (s * nq + j, h))
    vblk = pl.BlockSpec((tk, LANES), lambda s, h, j: (s * nq + j, h + FOX_V))
    full = pl.BlockSpec((seq, LANES), lambda s, h, j: (s, h))

    def body(q_ref, k_ref, v_ref, cb_ref, do_ref, lse_ref, dl_ref, dk_ref, dv_ref, dc_ref):
        j = pl.program_id(2)
        k = k_ref[...]
        v16 = v_ref[...].astype(BF16)
        ck = _col(cb_ref[...])

        def step(i, carry, diag):
            dk, dv, dc = carry
            off = pl.multiple_of(i * tq, tq)
            q = q_ref[pl.ds(off, tq), :]
            do16 = do_ref[pl.ds(off, tq), :].astype(BF16)
            lse, delta = (_row(r[pl.ds(off, tq), :]) for r in (lse_ref, dl_ref))
            p = jnp.exp(_dot(k, q, "nt") * FOX_SCALE - ck - lse)
            if diag:
                p = jnp.where(_causal(p.shape, 1), p, 0.0)
            dv = dv + _dot(p.astype(BF16), do16, "nn")
            ds = p * (_dot(v16, do16, "nt") - delta)
            return dk + _dot(ds.astype(BF16), q, "nn"), dv, dc + jnp.sum(ds, axis=1, keepdims=True)

        zero = jnp.zeros((tk, LANES), F32)
        carry = step(j, (zero, zero, jnp.zeros((tk, 1), F32)), True)
        dk, dv, dc = lax.fori_loop(j + 1, nq, lambda i, c: step(i, c, False), carry)
        dk_ref[...] = dk * FOX_SCALE
        dv_ref[...] = dv.astype(BF16)
        dc_ref[...] = jnp.where(_iota((tk, LANES), 1) == 0, -dc, 0.0)

    wide = (rows, NH * LANES)
    return pl.pallas_call(
        body, name="fox_dkv", grid=(nseq, NH, nq), in_specs=[full, kblk, vblk, kblk, full, full, full],
        out_specs=[kblk, kblk, kblk],
        out_shape=[jax.ShapeDtypeStruct(wide, F32), jax.ShapeDtypeStruct(wide, BF16), jax.ShapeDtypeStruct(wide, F32)],
        compiler_params=_params(3))(qn, kn, proj, cb, do, lse, delta)


def loss_head(out, tgt, rows, width):
    tb = ROW_TILE
    blk = pl.BlockSpec((tb, width), lambda i: (i, 0))
    accb = pl.BlockSpec((8, LANES), lambda i: (0, 0))

    def body(o_ref, t_ref, d32_ref, d16_ref, acc_ref):
        d = o_ref[...] - t_ref[...]
        row_loss = 0.5 * jnp.mean(d * d, axis=1, keepdims=True)
        g = d * (1.0 / width)
        d32_ref[...] = g
        d16_ref[...] = g.astype(BF16)
        part = jnp.where(_iota((tb, LANES), 1) == 0, row_loss, 0.0).reshape(tb // 8, 8, LANES).sum(axis=0)

        @pl.when(pl.program_id(0) == 0)
        def _():
            acc_ref[...] = part

        @pl.when(pl.program_id(0) != 0)
        def _():
            acc_ref[...] += part

    return pl.pallas_call(
        body, name="loss_head", grid=(rows // tb,), in_specs=[blk, blk], out_specs=[blk, blk, accb],
        out_shape=[jax.ShapeDtypeStruct((rows, width), F32), jax.ShapeDtypeStruct((rows, width), BF16),
                   jax.ShapeDtypeStruct((8, LANES), F32)], compiler_params=_params(1))(out, tgt)


def _adamw_update(w, g, m, v):
    m_new = ADAM_B1 * m + (1.0 - ADAM_B1) * g
    v_new = ADAM_B2 * v + (1.0 - ADAM_B2) * (g * g)
    m_hat = m_new / (1.0 - ADAM_B1 ** ADAM_STEP)
    v_hat = v_new / (1.0 - ADAM_B2 ** ADAM_STEP)
    return -ADAM_LR * (m_hat / (jnp.sqrt(v_hat) + ADAM_EPS) + ADAM_WD * w), m_new, v_new


def adamw(name, w, g, m, v):
    rows, cols = w.shape
    tb = min(rows, 128)
    assert rows % tb == 0
    blk = pl.BlockSpec((tb, cols), lambda i: (i, 0))

    def body(w_ref, g_ref, m_ref, v_ref, d_ref, mo_ref, vo_ref):
        d_ref[...], mo_ref[...], vo_ref[...] = _adamw_update(w_ref[...], g_ref[...], m_ref[...], v_ref[...])

    shp = jax.ShapeDtypeStruct(w.shape, F32)
    return pl.pallas_call(body, name=name, grid=(rows // tb,), in_specs=[blk] * 4, out_specs=[blk] * 3,
                          out_shape=[shp] * 3, compiler_params=_params(1))(w, g, m, v)


SPLIT_TILE = 128


def _tiled(shape2d, ax, n_lead, index):
    blk = (SPLIT_TILE, shape2d[1]) if ax == 0 else (shape2d[0], SPLIT_TILE)

    def index_map(*args):
        *lead, t = index(*args)
        return (*lead, t, 0) if ax == 0 else (*lead, 0, t)

    return pl.BlockSpec((None,) * n_lead + blk, index_map)


def adamw_halves(name, w, mine, other, m, v, c, ax):
    steps = w.shape[ax] // 2 // SPLIT_TILE
    assert w.shape[ax] == 2 * steps * SPLIT_TILE

    def body(c_ref, w_ref, mine_ref, other_ref, m_ref, v_ref, g_ref, d_ref, mo_ref, vo_ref):
        g = jnp.where(pl.program_id(0) // steps == c_ref[0], mine_ref[...], other_ref[...])
        g_ref[...] = g
        d_ref[...], mo_ref[...], vo_ref[...] = _adamw_update(w_ref[...], g, m_ref[...], v_ref[...])

    blk = _tiled(w.shape, ax, 0, lambda i, c_ref: (i,))
    hblk = _tiled(mine.shape, ax, 0, lambda i, c_ref: (i % steps,))
    grid_spec = pltpu.PrefetchScalarGridSpec(num_scalar_prefetch=1, grid=(2 * steps,),
                                             in_specs=[blk, hblk, hblk, blk, blk], out_specs=[blk] * 4)
    shp = jax.ShapeDtypeStruct(w.shape, F32)
    return pl.pallas_call(body, name=name, grid_spec=grid_spec, out_shape=[shp] * 4,
                          compiler_params=_params(1))(c, w, mine, other, m, v)


def add_chips(name, slots, parts, chip, axes):
    outs = []
    for idx, (x, own, ax) in enumerate(zip(slots, parts, axes)):
        n, shape2d = x.shape[0], x.shape[1:]
        steps = shape2d[ax] // SPLIT_TILE
        assert shape2d[ax] == steps * SPLIT_TILE

        def body(me_ref, *refs, n=n):
            o_ref = refs[n + 1]
            acc = None
            for t in range(n):
                term = jnp.where(me_ref[0] == t, refs[n][...], refs[t][...]).astype(F32)
                acc = term if acc is None else acc + term
            o_ref[...] = acc

        def filled(t, n=n):
            return lambda i, me_ref: (jnp.where(me_ref[0] == t, (t + 1) % n, t), i)

        grid_spec = pltpu.PrefetchScalarGridSpec(
            num_scalar_prefetch=1, grid=(steps,),
            in_specs=[_tiled(shape2d, ax, 1, filled(t)) for t in range(n)]
            + [_tiled(shape2d, ax, 1, lambda i, me_ref: (me_ref[0], i))],
            out_specs=_tiled(shape2d, ax, 0, lambda i, me_ref: (i,)))
        outs.append(pl.pallas_call(
            body, name=f"{name}_{idx}", grid_spec=grid_spec, out_shape=jax.ShapeDtypeStruct(shape2d, F32),
            compiler_params=_params(1))(chip, *([x] * n), own))
    return outs


def add_pair(name, gs, rs, c, axes):
    outs = []
    for idx, (g, r, ax) in enumerate(zip(gs, rs, axes)):
        nb = r.shape[0]
        steps = r.shape[1 + ax] // SPLIT_TILE
        assert r.shape[1 + ax] == steps * SPLIT_TILE

        def body(c_ref, g_ref, r_ref, o_ref):
            o_ref[...] = (g_ref[...] + r_ref[...]).astype(BF16)

        grid_spec = pltpu.PrefetchScalarGridSpec(
            num_scalar_prefetch=1, grid=(nb, steps),
            in_specs=[_tiled(g.shape[1:], ax, 1, lambda b, i, c_ref: (b, c_ref[0] * steps + i)),
                      _tiled(r.shape[1:], ax, 1, lambda b, i, c_ref: (b, i))],
            out_specs=_tiled(r.shape[1:], ax, 1, lambda b, i, c_ref: (b, i)))
        outs.append(pl.pallas_call(
            body, name=f"{name}_{idx}", grid_spec=grid_spec, out_shape=jax.ShapeDtypeStruct(r.shape, BF16),
            compiler_params=_params(2))(c, g, r))
    return outs


def _place():
    x, y, c = lax.axis_index("x"), lax.axis_index("y"), lax.axis_index("c")
    return x, y, c, [(1 - x, y), (x, 1 - y), (1 - x, 1 - y)]


def _remote(src, dst, send_sem, recv_sem, dev):
    return pltpu.make_async_remote_copy(src_ref=src, dst_ref=dst, send_sem=send_sem, recv_sem=recv_sem,
                                        device_id=dev, device_id_type=MESH)


def _half(ref, lead, ax, which):
    size = ref.shape[len(lead) + ax] // 2
    part = pl.ds(which * size, size)
    return ref.at[(*lead, part, slice(None)) if ax == 0 else (*lead, slice(None), part)]


def gather_weights(shards, axes):
    n = len(shards)

    def body(*refs):
        ins, outs = refs[:n], refs[n:2 * n]
        ici_s, ici_r, d2d_s, d2d_r = refs[2 * n:]
        x, y, c, chips = _place()
        me = 2 * x + y
        sends, passes = [], []
        for w in range(n):
            cp = _remote(ins[w], outs[w].at[me], d2d_s.at[3 * n + w], d2d_r.at[3 * n + w], (x, y, 1 - c))
            cp.start()
            passes.append(cp)
        for w in range(n):
            for j, (ox, oy) in enumerate(chips):
                cp = _remote(_half(ins[w], (), axes[w], c), _half(outs[w], (me,), axes[w], c),
                             ici_s.at[3 * w + j], ici_r.at[3 * w + j], (ox, oy, c))
                cp.start()
                sends.append(cp)
        for w in range(n):
            for j, (ox, oy) in enumerate(chips):
                landed = _half(outs[w], (2 * ox + oy,), axes[w], c)
                _remote(landed, landed, ici_s.at[3 * w + j], ici_r.at[3 * w + j], (ox, oy, c)).wait_recv()
                cp = _remote(landed, landed, d2d_s.at[3 * w + j], d2d_r.at[3 * w + j], (x, y, 1 - c))
                cp.start()
                passes.append(cp)
        for w in range(n):
            for j, (ox, oy) in enumerate(chips):
                other = _half(outs[w], (2 * ox + oy,), axes[w], 1 - c)
                _remote(other, other, d2d_s.at[3 * w + j], d2d_r.at[3 * w + j], (x, y, 1 - c)).wait_recv()
            own = outs[w].at[me]
            _remote(own, own, d2d_s.at[3 * n + w], d2d_r.at[3 * n + w], (x, y, 1 - c)).wait_recv()
        for cp in sends + passes:
            cp.wait_send()

    return pl.pallas_call(
        body, name="gather_weights", in_specs=[ANY] * n, out_specs=[ANY] * n,
        out_shape=[jax.ShapeDtypeStruct((4,) + s.shape, s.dtype) for s in shards],
        scratch_shapes=[pltpu.SemaphoreType.DMA((3 * n,))] * 2 + [pltpu.SemaphoreType.DMA((4 * n,))] * 2,
    )(*shards)


HBM = pl.BlockSpec(memory_space=pltpu.HBM)
SEM = pl.BlockSpec(memory_space=pltpu.SEMAPHORE)
DATAFLOW = pltpu.SideEffectType.DATAFLOW_SIDE_EFFECTING


def _hbm(a):
    return pltpu.with_memory_space_constraint(a, pltpu.HBM)


class SplitExchange:
    def __init__(self, name, srcs, zone_shapes, n_sems, plan):
        self.name, self.n, self.n_sems, self.plan = name, len(srcs), n_sems, plan
        self.srcs = [_hbm(s) for s in srcs]
        self.zones = [_hbm(lax.empty(shape, s.dtype)) for shape, s in zip(zone_shapes, srcs)]

    def start(self, after):
        n, n_after = self.n, len(after)

        def body(*refs):
            ins, lands = refs[:n], refs[n:2 * n]
            send, recv, token = refs[2 * n + n_after], refs[2 * n + n_after + 1], refs[-1]
            for src, dst, si, ri, dev in self.plan(ins, lands)[0]:
                _remote(src, dst, send.at[si], recv.at[ri], dev).start()
            token[...] = jnp.zeros_like(token)

        res = pl.pallas_call(
            body, name=f"{self.name}_start", in_specs=[HBM] * (2 * n) + [ANY] * n_after,
            out_specs=[SEM, SEM] + [HBM] * (2 * n) + [pl.BlockSpec(memory_space=pltpu.VMEM)],
            out_shape=[pltpu.SemaphoreType.DMA((self.n_sems,)), pltpu.SemaphoreType.DMA((self.n_sems,))]
            + [pltpu.HBM(a.shape, a.dtype) for a in self.srcs + self.zones] + [jax.ShapeDtypeStruct((8, LANES), F32)],
            input_output_aliases={i: 2 + i for i in range(2 * n)},
            compiler_params=pltpu.CompilerParams(has_side_effects=DATAFLOW),
        )(*self.srcs, *self.zones, *after)
        self.sems, self.srcs, self.zones = res[:2], list(res[2:2 + n]), list(res[2 + n:2 + 2 * n])
        return res[-1]

    def wait(self, after):
        n = self.n

        def body(*refs):
            ins, lands = refs[:n], refs[n:2 * n]
            send, recv = refs[2 * n], refs[2 * n + 1]
            sends, arrivals = self.plan(ins, lands)
            for src, _, si, _, dev in sends:
                _remote(src, src, send.at[si], recv.at[si], dev).wait_send()
            for landed, ri in arrivals:
                _remote(landed, landed, send.at[ri], recv.at[ri], _place()[:3]).wait_recv()

        res = pl.pallas_call(
            body, name=f"{self.name}_wait", in_specs=[HBM] * (2 * n) + [SEM, SEM, ANY], out_specs=[HBM] * (2 * n),
            out_shape=[pltpu.HBM(a.shape, a.dtype) for a in self.srcs + self.zones],
            input_output_aliases={i: i for i in range(2 * n)},
            compiler_params=pltpu.CompilerParams(has_side_effects=DATAFLOW),
        )(*self.srcs, *self.zones, *self.sems, after)
        self.srcs = list(res[:n])
        return list(res[n:])


def split_gather(shards):
    n = len(shards)

    def plan(ins, lands):
        x, y, c, chips = _place()
        me = 2 * x + y
        sends, arrivals = [], []
        for w in range(n):
            for j, (ox, oy) in enumerate(chips):
                for k in range(2):
                    base = 2 * (3 * w + j)
                    sends.append((_half(ins[w], (), 0, c), _half(lands[w], (me,), 0, c), base + k, base + c, (ox, oy, k)))
                    arrivals.append((_half(lands[w], (2 * ox + oy,), 0, k), base + k))
            sends.append((ins[w], lands[w].at[me], 6 * n + w, 6 * n + w, (x, y, 1 - c)))
            arrivals.append((lands[w].at[me], 6 * n + w))
        return sends, arrivals

    return SplitExchange("gather", shards, [(4,) + s.shape for s in shards], 7 * n, plan)


def split_pair_swap(name, grads, axes):
    def plan(ins, lands):
        x, y, c, _ = _place()
        sends = [(_half(ins[w], (slice(None),), axes[w], 1 - c), lands[w], w, w, (x, y, 1 - c)) for w in range(len(ins))]
        return sends, [(lands[w], w) for w in range(len(ins))]

    halved = [tuple(d // 2 if i == 1 + ax else d for i, d in enumerate(g.shape)) for g, ax in zip(grads, axes)]
    return SplitExchange(name, grads, halved, len(grads), plan)


def split_chip_exchange(name, parts):
    def plan(ins, lands):
        x, y, c, chips = _place()
        sends, arrivals = [], []
        for w in range(len(ins)):
            for j, (ox, oy) in enumerate(chips):
                sends.append((ins[w].at[2 * ox + oy], lands[w].at[2 * x + y], 3 * w + j, 3 * w + j, (ox, oy, c)))
                arrivals.append((lands[w].at[2 * ox + oy], 3 * w + j))
        return sends, arrivals

    return SplitExchange(name, parts, [p.shape for p in parts], 3 * len(parts), plan)


def split_pair_send(halves):
    def plan(ins, lands):
        x, y, c, _ = _place()
        return ([(ins[w], lands[w], w, w, (x, y, 1 - c)) for w in range(len(ins))],
                [(lands[w], w) for w in range(len(ins))])

    return SplitExchange("pair_send", halves, [h.shape for h in halves], len(halves), plan)


def pair_send(halves):
    n = len(halves)

    def body(*refs):
        ins, outs = refs[:n], refs[n:2 * n]
        send, recv = refs[2 * n:]
        x, y, c, _ = _place()
        cps = [_remote(ins[w], outs[w], send.at[w], recv.at[w], (x, y, 1 - c)) for w in range(n)]
        for cp in cps:
            cp.start()
        for cp in cps:
            cp.wait_recv()
        for cp in cps:
            cp.wait_send()

    return pl.pallas_call(
        body, name="pair_send", in_specs=[ANY] * n, out_specs=[ANY] * n,
        out_shape=[jax.ShapeDtypeStruct(h.shape, h.dtype) for h in halves],
        scratch_shapes=[pltpu.SemaphoreType.DMA((n,))] * 2,
    )(*halves)


def all_reduce_small(name, vec, after=()):
    rows = vec.shape[0]

    def body(v_ref, *refs):
        o_ref, buf, send, recv = refs[len(after):]
        x, y, c, _ = _place()
        me = 4 * x + 2 * y + c
        buf[me] = v_ref[...]
        cps = []
        for k in range(1, 8):
            kx, ky, kc = (k >> 2) & 1, (k >> 1) & 1, k & 1
            peer = (x if kx == 0 else 1 - x, y if ky == 0 else 1 - y, c if kc == 0 else 1 - c)
            cp = _remote(v_ref, buf.at[me], send.at[k - 1], recv.at[k - 1], peer)
            cp.start()
            cps.append(cp)
        for k in range(1, 8):
            kx, ky, kc = (k >> 2) & 1, (k >> 1) & 1, k & 1
            px, py, pc = (x if kx == 0 else 1 - x, y if ky == 0 else 1 - y, c if kc == 0 else 1 - c)
            slot = buf.at[4 * px + 2 * py + pc]
            _remote(slot, slot, send.at[k - 1], recv.at[k - 1], (px, py, pc)).wait_recv()
        for cp in cps:
            cp.wait_send()
        acc = buf[0]
        for d in range(1, 8):
            acc = acc + buf[d]
        o_ref[...] = acc

    vm = pl.BlockSpec(memory_space=pltpu.VMEM)
    return pl.pallas_call(
        body, name=name, in_specs=[vm] + [ANY] * len(after), out_specs=vm, out_shape=jax.ShapeDtypeStruct(vec.shape, F32),
        scratch_shapes=[pltpu.VMEM((8, rows, LANES), F32), pltpu.SemaphoreType.DMA((7,)), pltpu.SemaphoreType.DMA((7,))],
    )(vec, *after)


class NoExchange:
    def __init__(self, late):
        self.late = late

    def late_weights(self, after):
        return self.late

    def reduce_start(self, grads):
        return jnp.zeros((8, LANES), F32)

    def reduce_exchange(self, after):
        return jnp.zeros((8, LANES), F32)

    def reduce_finish(self, after):
        return jnp.zeros((8, LANES), F32)

    def input_grad_start(self, dw_main, dw_small):
        return jnp.zeros((8, LANES), F32)

    def input_grad_exchange(self, after):
        return jnp.zeros((8, LANES), F32)


def local_step(x2, tgt2, g1, g2, gdn_ng, qn_g, kn_g, p1, p2, conv_w, wt_main, wt_small, hooks, nseq, seq):
    rows, dm = x2.shape
    wide = NH * LANES
    row = lambda a, off=0, w=None: (a, "row", off, a.shape[1] if w is None else w)
    rowh = lambda a, off=0, w=LANES: (a, "rowh", off, w)
    par = lambda a: (a, "par", 0, a.shape[1])
    parh = lambda a, off=0: (a, "parh", off, LANES)
    o_row = lambda w, dt: (w, "row", w, dt)
    o_rowh = lambda dt, tw=wide, w=LANES: (tw, "rowh", w, dt)

    u, = ew_fwd("rms1", f_rms, [row(x2), par(g1)], [o_row(dm, BF16)], rows)
    proj = matmul("mm_in", u, wt_main, "nt", BF16)
    sp = matmul("mm_in_small", u, wt_small, "nt", F32)
    so, = ew_fwd("small", f_small, [row(sp), par(p1), par(p2)], [o_row(LANES, F32)], rows)
    cs = cumsum_time("cumsum", so, nseq, seq, False)
    gb, bb, cb = ew_fwd("bcast", f_bcast, [row(so), row(cs)], [o_rowh(F32)] * 3, rows, NH)
    ct = transpose_time("c_time_major", cs, nseq, seq)
    conv = {}
    for mode, off in (("q", 0), ("k", NH), ("v", 2 * NH)):
        conv[mode], = ew_fwd(f"conv_{mode}", make_f_conv(mode), [rowh(proj, off), parh(conv_w, off)], [o_rowh(F32)],
                             rows, NH, seq, "hi", CONV_HEADS)
    val, kcum, attn, qdec, kdec, t_inv = gdn_a_fwd(conv["q"], conv["k"], conv["v"], gb, bb, rows)
    o_a, snaps = gdn_b_fwd(val, kcum, attn, qdec, kdec, gb, nseq, seq)
    ya_in, = ew_fwd("gdn_post", f_post, [rowh(o_a), rowh(proj, 3 * NH), par(gdn_ng)], [o_rowh(BF16)], rows, NH)
    fqn, = ew_fwd("fox_qn", f_rms, [rowh(proj, FOX_Q), par(qn_g)], [o_rowh(BF16)], rows, NH)
    fkn, = ew_fwd("fox_kn", f_rms, [rowh(proj, FOX_K), par(kn_g)], [o_rowh(BF16)], rows, NH)
    o_b, o_b16, lse = fox_fwd(fqn, fkn, proj, ct, nseq, seq)
    p_a, p_b, w_o, w_u, w_d = hooks.late_weights(o_a)
    y_a = matmul("mm_pa", ya_in, p_a, "nn", F32, tn=1024)
    y_b = matmul("mm_pb", o_b16, p_b, "nn", F32, tn=1024)
    gates = [row(proj, 7, dm), row(proj, 8, dm)]
    merged, = ew_fwd("merge", f_merge, gates + [row(y_a), row(y_b)], [o_row(dm, BF16)], rows)
    hres = matmul("mm_out", merged, w_o, "nn", F32, add=x2, tn=1024)
    hn, = ew_fwd("rms2", f_rms, [row(hres), par(g2)], [o_row(dm, BF16)], rows)
    up_blocks = w_u.shape[0]
    act, relu2 = matmul("mm_up", hn, w_u, "nn", F32, col_blocks=up_blocks, out_dtypes=[F32, BF16],
                        epilogue=lambda r: [r, jnp.maximum(r, 0.0) * jnp.maximum(r, 0.0)])
    out = matmul("mm_down", relu2, w_d, "nn", F32, add=hres, tn=1024)
    dout, dout16, loss_acc = loss_head(out, tgt2, rows, dm)

    d_act = matmul("mm_d_act", dout16, w_d, "nt", BF16, extras=[act], epilogue=lambda r, a: [2.0 * jnp.maximum(a, 0.0) * r])
    dw_d = matmul("mm_dw_down", relu2, dout16, "tn", F32, tn=1024)
    dw_u = matmul("mm_dw_up", hn, d_act, "tn", F32, col_blocks=up_blocks)
    d_hn = matmul("mm_d_hn", d_act, w_u, "nt", F32, col_blocks=up_blocks)
    dh, dh16, dg2 = ew_bwd("rms2_b", f_rms, [row(hres), par(g2)], [(row(d_hn),)], [row(dout)],
                           lambda g, e: [g[0] + e[0], g[0] + e[0], g[1]],
                           [((rows, dm), "row", dm, F32, None), ((rows, dm), "row", dm, BF16, None), ((1, dm), "par", dm, F32, "all")], rows)
    d_merged = matmul("mm_d_merged", dh16, w_o, "nt", F32, tn=1024)
    dw_o = matmul("mm_dw_out", merged, dh16, "tn", F32, tn=1024)
    seg16 = ((rows, dm), "row", dm, BF16, None)
    d_ga16, d_gb16, d_ya16, d_yb16 = ew_bwd("merge_b", f_merge, gates + [row(y_a), row(y_b)], [(row(d_merged),)], [],
                                            lambda g, e: list(g), [seg16] * 4, rows)
    dp_a = matmul("mm_dp_a", ya_in, d_ya16, "tn", F32, tn=1024)
    d_ya_in = matmul("mm_d_ya_in", d_ya16, p_a, "nt", F32, tn=1024)
    dp_b = matmul("mm_dp_b", o_b16, d_yb16, "tn", F32, tn=1024)
    d_ob = matmul("mm_d_ob", d_yb16, p_b, "nt", F32, tn=1024)
    token = hooks.reduce_start(dict(p_a=dp_a, p_b=dp_b, w_o=dw_o, w_u=dw_u, w_d=dw_d))
    gdn_ng_t = gdn_ng + token[0, 0]
    h32 = ((rows, wide), "rowh", LANES, F32, None)
    h16 = ((rows, wide), "rowh", LANES, BF16, None)
    gain = ((1, LANES), "par", LANES, F32, "all")
    d_oa, d_z16, d_gdn_ng = ew_bwd("gdn_post_b", f_post, [rowh(o_a), rowh(proj, 3 * NH), par(gdn_ng_t)], [(rowh(d_ya_in),)], [],
                                   lambda g, e: list(g), [h32, h16, gain], rows, NH)
    dval, dkc, dat, dqd, dkd, dgb_b = gdn_b_bwd(val, kcum, attn, qdec, kdec, gb, snaps, d_oa, nseq, seq)
    d_cq, d_ck, d_cv, d_gb, d_bb = gdn_a_bwd(conv["q"], conv["k"], conv["v"], gb, bb, t_inv, dval, dkc, dat, dqd, dkd, dgb_b, rows)
    token = hooks.reduce_exchange(d_cq)
    conv_w_t = conv_w + token[0, 0]
    d_pre, d_conv = {}, {}
    tap = ((4, wide), "parh", LANES, F32, "inner")
    for mode, off, ctg in (("q", 0, d_cq), ("k", NH, d_ck), ("v", 2 * NH, d_cv)):
        d_pre[mode], d_conv[mode] = ew_bwd(f"conv_{mode}_b", make_f_conv(mode), [rowh(proj, off), parh(conv_w_t, off)],
                                           [(rowh(ctg),)], [], lambda g, e: list(g), [h16, tap], rows, NH, seq, "hi", CONV_HEADS)
    delta, = ew_fwd("fox_delta", f_delta, [rowh(d_ob), rowh(o_b)], [o_rowh(F32)], rows, NH, after=[token])
    d_fqn, d_cq_b = fox_dq(fqn, fkn, proj, ct, d_ob, lse, delta, nseq, seq)
    d_fkn, d_fv16, d_ck_b = fox_dkv(fqn, fkn, proj, cb, d_ob, lse, delta, nseq, seq)
    token = hooks.reduce_finish(d_fkn)
    qn_g_t, kn_g_t = qn_g + token[0, 0], kn_g + token[0, 0]
    d_fq16, d_qn_g = ew_bwd("fox_qn_b", f_rms, [rowh(proj, FOX_Q), par(qn_g_t)], [(rowh(d_fqn),)], [], lambda g, e: list(g),
                            [h16, gain], rows, NH)
    d_fk16, d_kn_g = ew_bwd("fox_kn_b", f_rms, [rowh(proj, FOX_K), par(kn_g_t)], [(rowh(d_fkn),)], [], lambda g, e: list(g),
                            [h16, gain], rows, NH)
    narrow = ((rows, LANES), "row", LANES, F32, None)
    d_so, d_cs = ew_bwd("bcast_b", f_bcast, [row(so), row(cs)], [(rowh(d_gb),), (rowh(d_bb),), (rowh(d_cq_b), rowh(d_ck_b))], [],
                        lambda g, e: list(g), [narrow, narrow], rows, NH)
    d_logf = cumsum_time("cumsum_b", d_cs, nseq, seq, True)
    vec = ((1, LANES), "par", LANES, F32, "all")
    d_sp16, d_p1, d_p2 = ew_bwd("small_b", f_small, [row(sp), par(p1), par(p2)], [(row(d_so), row(d_logf))], [],
                                lambda g, e: list(g), [((rows, LANES), "row", LANES, BF16, None), vec, vec], rows)
    d_proj16 = jnp.concatenate([d_pre["q"], d_pre["k"], d_pre["v"], d_z16, d_fq16, d_fk16, d_fv16, d_ga16, d_gb16], axis=1)
    dw_main = matmul("mm_dw_main", d_proj16, u, "tn", F32)
    dw_small = matmul("mm_dw_small", d_sp16, u, "tn", F32)
    wt_small_t = wt_small + hooks.input_grad_start(dw_main, dw_small)[0, 0].astype(BF16)
    d_u = matmul("mm_d_u_small", d_sp16, wt_small_t, "nn", F32)
    d_u = matmul("mm_d_u_first", d_proj16, wt_main, "nn", F32, add=d_u, k_part=(0, 2))
    d_u = matmul("mm_d_u_second", d_proj16, wt_main, "nn", F32, add=d_u, k_part=(1, 2), after=[hooks.input_grad_exchange(d_u)])
    dx, dg1 = ew_bwd("rms1_b", f_rms, [row(x2), par(g1)], [(row(d_u),)], [row(dh)], lambda g, e: [g[0] + e[0], g[1]],
                     [((rows, dm), "row", dm, F32, None), ((1, dm), "par", dm, F32, "all")], rows)
    d_conv_w = jnp.concatenate([d_conv["q"], d_conv["k"], d_conv["v"]], axis=1)
    return dict(loss_acc=loss_acc, dx=dx, g1=dg1, g2=dg2, gdn_ng=d_gdn_ng, qn=d_qn_g, kn=d_kn_g, p1=d_p1, p2=d_p2,
                conv=d_conv_w, w_main=dw_main, w_small=dw_small, p_a=dp_a, p_b=dp_b, w_o=dw_o, w_u=dw_u, w_d=dw_d)


_W = NH * LANES
_A0, _A1 = 4 * _W, 4 * _W + 2 * NH
_B0, _B1 = _A1 + 3 * _W, _A1 + 3 * _W + NH
N_IN = _B1 + 2 * _W


def _split_w_in(full_t):
    main = jnp.concatenate([full_t[:_A0], full_t[_A1:_B0], full_t[_B1:]], axis=0)
    small = jnp.concatenate([full_t[_A0:_A1], full_t[_B0:_B1], jnp.zeros((LANES - 3 * NH, full_t.shape[1]), full_t.dtype)], axis=0)
    return main, small


def _join_w_in(main, small):
    return jnp.concatenate([main[:_A0], small[:2 * NH], main[_A0:_A0 + 3 * _W], small[2 * NH:3 * NH], main[_A0 + 3 * _W:]], axis=0)


def _lanes(v, at=0):
    return jnp.pad(v.reshape(1, -1), ((0, 0), (at, LANES - at - v.size)))


def kernel(x, norm_mix_g, w_in, gdn_conv_w, gdn_a_log, gdn_dt_bias, gdn_norm_g, fox_q_norm_g, fox_k_norm_g, fox_f_bias, w_proj_gdn, w_proj_fox, w_out, norm_mlp_g, w_up, w_down, loss_target, m_norm_mix_g, m_w_in, m_gdn_conv_w, m_gdn_a_log, m_gdn_dt_bias, m_gdn_norm_g, m_fox_q_norm_g, m_fox_k_norm_g, m_fox_f_bias, m_w_proj_gdn, m_w_proj_fox, m_w_out, m_norm_mlp_g, m_w_up, m_w_down, v_norm_mix_g, v_w_in, v_gdn_conv_w, v_gdn_a_log, v_gdn_dt_bias, v_gdn_norm_g, v_fox_q_norm_g, v_fox_k_norm_g, v_fox_f_bias, v_w_proj_gdn, v_w_proj_fox, v_w_out, v_norm_mlp_g, v_w_up, v_w_down):
    nseq, seq, dm = x.shape
    rows = nseq * seq
    xi, yi, ci = lax.axis_index("x"), lax.axis_index("y"), lax.axis_index("c")
    chip = 2 * xi + yi
    conv_cols = gdn_conv_w.shape[2]

    tr = lambda a: jnp.swapaxes(a[0], 0, 1)
    big = [tr(w_in), w_proj_gdn[0], w_proj_fox[0], w_out[0], w_up[0], w_down[0]]
    axes = [1, 0, 0, 0, 0, 0]
    big16 = [w.astype(BF16) for w in big]
    conv_slot = jnp.zeros((4, 4, conv_cols), F32).at[:, chip].set(jnp.where(ci == 0, gdn_conv_w[0], 0.0))
    conv_full = all_reduce_small("gather_conv", conv_slot.reshape(-1, LANES)).reshape(4, 4 * conv_cols)
    got_in, = gather_weights(big16[:1], axes[:1])
    wt_main, wt_small = _split_w_in(got_in.reshape(-1, dm))
    core, chip_no = ci.reshape(1).astype(jnp.int32), chip.reshape(1).astype(jnp.int32)
    gather = split_gather(big16[1:])
    token = gather.start([got_in, conv_full])

    class Hooks:
        def late_weights(self, after):
            g_pa, g_pb, g_wo, w_u, g_wd = gather.wait(after)
            return (*(g.reshape(-1, dm) for g in (g_pa, g_pb, g_wo)), w_u, g_wd.reshape(-1, dm))

        def reduce_start(self, grads):
            blocks = [grads["p_a"].reshape(4, -1, dm), grads["p_b"].reshape(4, -1, dm), grads["w_o"].reshape(4, -1, dm),
                      grads["w_u"], grads["w_d"].reshape(4, -1, dm)]
            self.swap = split_pair_swap("pair_swap_late", blocks, axes[1:])
            return self.swap.start([])

        def reduce_exchange(self, after):
            swapped = self.swap.wait(after)
            self.exchange = split_chip_exchange("chip_exchange_late", add_pair("add_pair_late", self.swap.srcs, swapped, core, axes[1:]))
            return self.exchange.start([])

        def reduce_finish(self, after):
            slots = self.exchange.wait(after)
            self.send = split_pair_send(add_chips("add_chips_late", slots, self.exchange.srcs, chip_no, axes[1:]))
            return self.send.start([])

        def input_grad_start(self, dw_main, dw_small):
            self.in_swap = split_pair_swap("pair_swap_in", [_join_w_in(dw_main, dw_small).reshape(4, -1, dm)], axes[:1])
            return self.in_swap.start([])

        def input_grad_exchange(self, after):
            swapped = self.in_swap.wait(after)
            self.in_exchange = split_chip_exchange("chip_exchange_in", add_pair("add_pair_in", self.in_swap.srcs, swapped, core, axes[:1]))
            return self.in_exchange.start([])

    hooks = Hooks()
    p1 = _lanes(gdn_dt_bias[0]) + _lanes(fox_f_bias[0], 2 * NH)
    p2 = _lanes(gdn_a_log[0])

    g = local_step(x.reshape(rows, dm), loss_target.reshape(rows, dm), norm_mix_g + token[0, 0], norm_mlp_g, gdn_norm_g,
                   fox_q_norm_g, fox_k_norm_g, p1, p2, conv_full, wt_main, wt_small, hooks, nseq, seq)

    others = hooks.send.wait(g["dx"])
    big_m = [tr(m_w_in), m_w_proj_gdn[0], m_w_proj_fox[0], m_w_out[0], m_w_up[0], m_w_down[0]]
    big_v = [tr(v_w_in), v_w_proj_gdn[0], v_w_proj_fox[0], v_w_out[0], v_w_up[0], v_w_down[0]]
    names = ["w_in", "w_proj_gdn", "w_proj_fox", "w_out", "w_up", "w_down"]
    big_res, big_grad = {}, {}
    for i in range(1, len(names)):
        big_grad[names[i]], *big_res[names[i]] = adamw_halves(f"adamw_{names[i]}", big[i], hooks.send.srcs[i - 1], others[i - 1],
                                                              big_m[i], big_v[i], core, axes[i])
    slots = hooks.in_exchange.wait(big_res[names[-1]][0])
    mine = add_chips("add_chips_in", slots, hooks.in_exchange.srcs, chip_no, axes[:1])
    res = adamw_halves("adamw_w_in", big[0], mine[0], pair_send(mine)[0], big_m[0], big_v[0], core, axes[0])
    big_grad["w_in"], *big_res["w_in"] = [jnp.swapaxes(r, 0, 1) for r in res]

    small_parts = [g["loss_acc"], g["g1"].reshape(8, LANES), g["g2"].reshape(8, LANES), g["gdn_ng"], g["qn"], g["kn"], g["p1"], g["p2"],
                   g["conv"].reshape(-1, LANES)]
    tiled = [jnp.pad(p, ((0, -p.shape[0] % 8), (0, 0))) for p in small_parts]
    red = all_reduce_small("reduce_small", jnp.concatenate(tiled, axis=0), slots)
    pos, red_parts = 0, []
    for p, t in zip(small_parts, tiled):
        red_parts.append(red[pos:pos + p.shape[0]])
        pos += t.shape[0]
    r_loss, r_g1, r_g2, r_gdn_ng, r_qn, r_kn, r_p1, r_p2, r_conv = red_parts
    loss = jnp.sum(r_loss)
    g_conv = lax.dynamic_slice_in_dim(r_conv.reshape(4, 4, conv_cols), chip, 1, axis=1).reshape(4, conv_cols)
    small_grads = [r_g1.reshape(1, dm), r_p2[:, :NH], r_p1[:, :NH], r_gdn_ng, r_qn, r_kn, r_p1[:, 2 * NH:3 * NH], r_g2.reshape(1, dm)]
    small_w = [norm_mix_g, gdn_a_log, gdn_dt_bias, gdn_norm_g, fox_q_norm_g, fox_k_norm_g, fox_f_bias, norm_mlp_g]
    small_m = [m_norm_mix_g, m_gdn_a_log, m_gdn_dt_bias, m_gdn_norm_g, m_fox_q_norm_g, m_fox_k_norm_g, m_fox_f_bias, m_norm_mlp_g]
    small_v = [v_norm_mix_g, v_gdn_a_log, v_gdn_dt_bias, v_gdn_norm_g, v_fox_q_norm_g, v_fox_k_norm_g, v_fox_f_bias, v_norm_mlp_g]

    def pack(parts):
        flat = jnp.concatenate([jnp.pad(p.reshape(-1), (0, -p.size % LANES)) for p in parts])
        return jnp.pad(flat, (0, -flat.size % (8 * LANES))).reshape(-1, LANES)

    packed = adamw("adamw_small", pack(small_w + [gdn_conv_w[0]]), pack(small_grads + [g_conv]),
                   pack(small_m + [m_gdn_conv_w[0]]), pack(small_v + [v_gdn_conv_w[0]]))

    def unpack(flat2d):
        flat, pos, res = flat2d.reshape(-1), 0, []
        for p in small_w + [gdn_conv_w[0]]:
            res.append(flat[pos:pos + p.size].reshape(p.shape))
            pos += p.size + (-p.size % LANES)
        return res

    s_delta, s_m, s_v = (unpack(a) for a in packed)

    order = ["norm_mix_g", "w_in", "gdn_conv_w", "gdn_a_log", "gdn_dt_bias", "gdn_norm_g", "fox_q_norm_g", "fox_k_norm_g",
             "fox_f_bias", "w_proj_gdn", "w_proj_fox", "w_out", "norm_mlp_g", "w_up", "w_down"]
    small_names = ["norm_mix_g", "gdn_a_log", "gdn_dt_bias", "gdn_norm_g", "fox_q_norm_g", "fox_k_norm_g", "fox_f_bias", "norm_mlp_g",
                   "gdn_conv_w"]
    small_idx = {nm: i for i, nm in enumerate(small_names)}
    shapes = dict(zip(order, (a.shape for a in (norm_mix_g, w_in, gdn_conv_w, gdn_a_log, gdn_dt_bias, gdn_norm_g, fox_q_norm_g,
                                                 fox_k_norm_g, fox_f_bias, w_proj_gdn, w_proj_fox, w_out, norm_mlp_g, w_up, w_down))))
    grads_out, delta_out, m_out, v_out = [], [], [], []
    for nm in order:
        if nm in big_res:
            d, mm, vv = big_res[nm]
            gr = big_grad[nm]
        else:
            i = small_idx[nm]
            gr = (small_grads + [g_conv])[i]
            d, mm, vv = s_delta[i], s_m[i], s_v[i]
        for lst, val in ((grads_out, gr), (delta_out, d), (m_out, mm), (v_out, vv)):
            lst.append(val.reshape(shapes[nm]))
    return (loss, g["dx"].reshape(x.shape), *grads_out, *delta_out, *m_out, *v_out)
```

```python
import functools

import jax
import jax.numpy as jnp
from jax import lax
from jax.experimental import pallas as pl
from jax.experimental.pallas import tpu as pltpu

F32 = jnp.float32
BF16 = jnp.bfloat16
LANES = 128
NH = 8
EPS = 1e-6
GDN_CHUNK = 64
GDN_ROWS = 256
GDN_BASE = 16
ROW_TILE = 512
CONV_HEADS = 2
ATT_TILE = 512
NEG = -1e30
VMEM_LIMIT_BYTES = 48 * 1024 * 1024
HI = lax.Precision.HIGHEST
LO = lax.Precision.DEFAULT
MESH = pl.DeviceIdType.MESH
ANY = pl.BlockSpec(memory_space=pl.ANY)

ADAM_LR, ADAM_B1, ADAM_B2, ADAM_EPS, ADAM_WD, ADAM_STEP = 0.001, 0.9, 0.999, 1e-08, 0.01, 10


def _params(n_grid):
    return pltpu.CompilerParams(dimension_semantics=("arbitrary",) * n_grid,
                                vmem_limit_bytes=VMEM_LIMIT_BYTES)


def _dot(a, b, dims, precision=None):
    dn = {"nn": (((1,), (0,)), ((), ())), "nt": (((1,), (1,)), ((), ())), "tn": (((0,), (0,)), ((), ()))}[dims]
    return lax.dot_general(a, b, dn, precision=precision, preferred_element_type=F32)


def _iota(shape, dim):
    return lax.broadcasted_iota(jnp.int32, shape, dim)


def _split(x, parts):
    out = []
    for _ in range(parts - 1):
        hi = x.astype(BF16)
        out.append(hi)
        x = x - hi.astype(F32)
    return out + [x.astype(BF16)]


def _dot_mask(mask, b, dims):
    m16 = mask.astype(BF16)
    b1, b2, b3 = _split(b, 3)
    return _dot(m16, b1, dims) + (_dot(m16, b2, dims) + _dot(m16, b3, dims))


@jax.custom_vjp
def mm_mask(mask, b):
    return _dot_mask(mask, b, "nn")


mm_mask.defvjp(lambda mask, b: (_dot_mask(mask, b, "nn"), mask),
               lambda mask, g: (jnp.zeros_like(mask), _dot_mask(mask, g, "tn")))


def matmul(name, a, b, dims, out_dtype, add=None, tm=1024, tn=1024, tk=512, col_blocks=None,
           extras=(), epilogue=None, out_dtypes=None, k_part=None, after=()):
    if col_blocks and dims != "tn":
        nb, b_rows, bw = b.shape
        b_shape = (b_rows, nb * bw)
    else:
        b_shape = b.shape
    if dims == "nn":
        (m, k), (_, n) = a.shape, b_shape
    elif dims == "nt":
        (m, k), (n, _) = a.shape, b_shape
    else:
        (k, m), (_, n) = a.shape, b_shape
    if k <= 1024:
        tk = k
    tm, tn, tk = min(tm, m), min(tn, n), min(tk, k)
    assert m % tm == 0 and n % tn == 0 and k % tk == 0, (name, m, n, k)
    k0, nk = (0, k // tk) if k_part is None else (k_part[0] * (k // tk // k_part[1]), k // tk // k_part[1])
    assert k_part is None or (dims == "nn" and not col_blocks and (k // tk) % k_part[1] == 0)
    a_spec = pl.BlockSpec((tk, tm), lambda i, j, kk: (kk, i)) if dims == "tn" else pl.BlockSpec((tm, tk), lambda i, j, kk: (i, kk + k0))
    b_spec = pl.BlockSpec((tn, tk), lambda i, j, kk: (j, kk)) if dims == "nt" else pl.BlockSpec((tk, tn), lambda i, j, kk: (kk + k0, j))
    o_spec = pl.BlockSpec((tm, tn), lambda i, j, kk: (i, j))
    out_shape = (m, n)
    if col_blocks and dims == "nn":
        per = bw // tn
        assert bw % tn == 0
        b_spec = pl.BlockSpec((None, tk, tn), lambda i, j, kk: (j // per, kk, j % per))
    elif col_blocks and dims == "nt":
        per = bw // tk
        assert bw % tk == 0
        b_spec = pl.BlockSpec((None, tn, tk), lambda i, j, kk: (kk // per, j, kk % per))
    elif col_blocks:
        bw = n // col_blocks
        per = bw // tn
        assert bw % tn == 0 and add is None
        o_spec = pl.BlockSpec((None, tm, tn), lambda i, j, kk: (j // per, i, j % per))
        out_shape = (col_blocks, m, bw)
    extras = list(extras) + ([add] if add is not None else [])
    if add is not None:
        assert epilogue is None
        epilogue = lambda r, *e: [r + e[-1]]
    out_dtypes = [out_dtype] if epilogue is None or out_dtypes is None else list(out_dtypes)
    n_ex, n_out = len(extras), len(out_dtypes)

    def body(*refs):
        a_ref, b_ref = refs[0], refs[1]
        ex_refs, o_refs = refs[2:2 + n_ex], refs[2 + n_ex + len(after):2 + n_ex + len(after) + n_out]

        def finish(r):
            res = [r] if epilogue is None else epilogue(r, *[e[...] for e in ex_refs])
            for o_ref, v in zip(o_refs, res):
                o_ref[...] = v.astype(o_ref.dtype)

        if nk == 1:
            finish(_dot(a_ref[...], b_ref[...], dims))
            return
        acc_ref = refs[-1]
        kk = pl.program_id(2)

        @pl.when(kk == 0)
        def _():
            acc_ref[...] = jnp.zeros_like(acc_ref)

        acc_ref[...] += _dot(a_ref[...], b_ref[...], dims)

        @pl.when(kk == nk - 1)
        def _():
            finish(acc_ref[...])

    res = pl.pallas_call(
        body, name=name, grid=(m // tm, n // tn, nk), in_specs=[a_spec, b_spec] + [o_spec] * n_ex + [ANY] * len(after),
        out_specs=[o_spec] * n_out, out_shape=[jax.ShapeDtypeStruct(out_shape, dt) for dt in out_dtypes],
        scratch_shapes=[pltpu.VMEM((tm, tn), F32)] if nk > 1 else [], compiler_params=_params(3),
    )(a, b, *extras, *after)
    return res[0] if n_out == 1 else res


def _ew_spec(kind, off, width, tb, hp, order, shape=None):
    def ih(g0, g1):
        return (g0, g1) if order == "ih" else (g1, g0)

    assert off % hp == 0 or kind in ("row", "par")
    if kind == "row":
        return pl.BlockSpec((tb, width), lambda g0, g1: (ih(g0, g1)[0], off))
    if kind == "rowh":
        return pl.BlockSpec((tb, hp * width), lambda g0, g1: (ih(g0, g1)[0], ih(g0, g1)[1] + off // hp))
    if kind == "par":
        return pl.BlockSpec(shape, lambda g0, g1: (0, 0))
    if kind == "parh":
        return pl.BlockSpec((shape[0], hp * width), lambda g0, g1: (0, ih(g0, g1)[1] + off // hp))
    raise ValueError(kind)


def _ew_grid(rows, tb, nh, hp, order):
    assert nh % hp == 0 and rows % tb == 0
    return (rows // tb, nh // hp) if order == "ih" else (nh // hp, rows // tb)


def _ew_load(ref, kind, width, hh):
    if kind in ("row", "par"):
        return ref[...].astype(F32)
    return ref[:, hh * width:(hh + 1) * width].astype(F32)


def ew_fwd(name, f, ins, outs, rows, nh=1, tb=ROW_TILE, order="ih", hp=None, after=()):
    hp = nh if hp is None else hp
    n_in = len(ins)

    def body(*refs):
        hb = pl.program_id(1) if order == "ih" else pl.program_id(0)
        for hh in range(hp):
            h = hh if hp == nh else hb * hp + hh
            vals = [_ew_load(r, kd, w, hh) for r, (_, kd, _, w) in zip(refs[:n_in], ins)]
            res = f(h, *vals)
            for r, v, (_, kd, w, _) in zip(refs[n_in + len(after):], res, outs):
                if kd == "row":
                    assert hp == 1
                    r[...] = v.astype(r.dtype)
                else:
                    r[:, hh * w:(hh + 1) * w] = v.astype(r.dtype)

    in_specs = [_ew_spec(kd, off, w, tb, hp, order, a.shape) for (a, kd, off, w) in ins]
    out_specs = [_ew_spec(kd, 0, w, tb, hp, order) for (_, kd, w, _) in outs]
    out_shape = [jax.ShapeDtypeStruct((rows, tw), dt) for (tw, _, _, dt) in outs]
    return pl.pallas_call(
        body, name=name, grid=_ew_grid(rows, tb, nh, hp, order), in_specs=in_specs + [ANY] * len(after), out_specs=out_specs,
        out_shape=out_shape, compiler_params=_params(2),
    )(*[a for (a, _, _, _) in ins], *after)


def ew_bwd(name, f, ins, cts, extras, emit, outs, rows, nh=1, tb=ROW_TILE, order="ih", hp=None):
    hp = nh if hp is None else hp
    n_in = len(ins)
    flat_cts = [d for group in cts for d in group]
    n_ct, n_ex = len(flat_cts), len(extras)

    def body(*refs):
        g0, g1 = pl.program_id(0), pl.program_id(1)
        hb = g1 if order == "ih" else g0
        out_refs = refs[n_in + n_ct + n_ex:]
        shared = [None] * len(outs)

        def store(r, v, first, sl=None):
            def put(val, add):
                if sl is None:
                    r[...] = (r[...] + val if add else val).astype(r.dtype)
                else:
                    r[:, sl] = (r[:, sl] + val if add else val).astype(r.dtype)

            if first is None:
                put(v, False)
            else:
                pl.when(first)(lambda: put(v, False))
                pl.when(jnp.logical_not(first))(lambda: put(v, True))

        for hh in range(hp):
            h = hh if hp == nh else hb * hp + hh
            vals = [_ew_load(r, kd, w, hh) for r, (_, kd, _, w) in zip(refs[:n_in], ins)]
            ct_refs = list(zip(refs[n_in:n_in + n_ct], flat_cts))
            ct_vals, pos = [], 0
            for group in cts:
                v = None
                for r, (_, kd, _, w) in ct_refs[pos:pos + len(group)]:
                    t = _ew_load(r, kd, w, hh)
                    v = t if v is None else v + t
                pos += len(group)
                ct_vals.append(v)
            ex_vals = [_ew_load(r, kd, w, hh) for r, (_, kd, _, w) in zip(refs[n_in + n_ct:n_in + n_ct + n_ex], extras)]
            _, vjp = jax.vjp(lambda *a: f(h, *a), *vals)
            res = emit(vjp(tuple(ct_vals)), ex_vals)
            for idx, (r, v, (_, kd, w, _, acc)) in enumerate(zip(out_refs, res, outs)):
                if kd in ("row", "par"):
                    shared[idx] = v if shared[idx] is None else shared[idx] + v
                else:
                    store(r, v, (g1 == 0) if acc == "inner" else None, slice(hh * w, (hh + 1) * w))
        for idx, (r, (_, kd, _, _, acc)) in enumerate(zip(out_refs, outs)):
            if kd in ("row", "par"):
                assert acc == "all" or hp == nh
                store(r, shared[idx], jnp.logical_and(g0 == 0, g1 == 0) if acc == "all" else None)

    operands = list(ins) + flat_cts + list(extras)
    in_specs = [_ew_spec(kd, off, w, tb, hp, order, a.shape) for (a, kd, off, w) in operands]
    out_specs = [_ew_spec(kd, 0, w, tb, hp, order, shp) for (shp, kd, w, _, _) in outs]
    out_shape = [jax.ShapeDtypeStruct(shp, dt) for (shp, _, _, dt, _) in outs]
    return pl.pallas_call(
        body, name=name, grid=_ew_grid(rows, tb, nh, hp, order), in_specs=in_specs, out_specs=out_specs,
        out_shape=out_shape, compiler_params=_params(2),
    )(*[a for (a, _, _, _) in operands])


def f_rms(h, x, g):
    r = lax.rsqrt(jnp.mean(x * x, axis=-1, keepdims=True) + EPS)
    return (x * r * g,)


def _softplus(z):
    return jnp.maximum(z, 0.0) + jnp.log1p(jnp.exp(-jnp.abs(z)))


def f_small(h, sp, p1, p2):
    lane = _iota(sp.shape, 1)
    z = sp + p1
    g = -jnp.exp(p2) * _softplus(z)
    beta = jax.nn.sigmoid(z)
    logf = -_softplus(-z)
    return (jnp.where(lane < NH, g, jnp.where(lane < 2 * NH, beta, jnp.where(lane < 3 * NH, logf, 0.0))),)


def _pick(x, lane_id):
    lane = _iota(x.shape, 1)
    col = jnp.sum(jnp.where(lane == lane_id, x, 0.0), axis=1, keepdims=True)
    return jnp.broadcast_to(col, x.shape)


def f_bcast(h, so, cs):
    return _pick(so, h), _pick(so, h + NH), _pick(cs, h + 2 * NH)


def _shift_down(s):
    def down(x):
        return jnp.where(_iota(x.shape, 0) >= s, pltpu.roll(x, s, 0), 0.0)

    def up(g):
        n = g.shape[0]
        return jnp.where(_iota(g.shape, 0) < n - s, pltpu.roll(g, n - s, 0), 0.0)

    @jax.custom_vjp
    def shift(x):
        return down(x)

    shift.defvjp(lambda x: (down(x), None), lambda _, g: (up(g),))
    return shift


def _silu(x):
    return x * jax.nn.sigmoid(x)


def make_f_conv(mode):
    sh1, sh2, sh3 = _shift_down(1), _shift_down(2), _shift_down(3)

    def f(h, x, w):
        sub = _iota(w.shape, 0)

        def tap(i):
            return jnp.sum(jnp.where(sub == i, w, 0.0), axis=0, keepdims=True)

        y = sh3(x) * tap(0)
        y = y + sh2(x) * tap(1)
        y = y + sh1(x) * tap(2)
        y = y + x * tap(3)
        s = _silu(y)
        if mode == "v":
            return (s,)
        n = s * lax.rsqrt(jnp.sum(s * s, axis=-1, keepdims=True) + EPS)
        if mode == "q":
            n = n * (LANES ** -0.5)
        return (n,)

    return f


def f_post(h, o, z, g):
    r = lax.rsqrt(jnp.mean(o * o, axis=-1, keepdims=True) + EPS)
    return (o * r * g * _silu(z),)


def f_merge(h, ga, gb, ya, yb):
    return (jax.nn.sigmoid(ga) * ya + jax.nn.sigmoid(gb) * yb,)


def f_delta(h, do, o):
    return (jnp.broadcast_to(jnp.sum(do * o, axis=1, keepdims=True), o.shape),)


def cumsum_time(name, x, nseq, seq, reverse):
    nb = seq // LANES

    def body(x_ref, o_ref):
        r, c = _iota((LANES, LANES), 0), _iota((LANES, LANES), 1)
        tri = jnp.where((r <= c) if reverse else (r >= c), 1.0, 0.0).astype(F32)
        carry = jnp.zeros((1, LANES), F32)
        for b in (range(nb - 1, -1, -1) if reverse else range(nb)):
            blk = x_ref[b * LANES:(b + 1) * LANES, :]
            o_ref[b * LANES:(b + 1) * LANES, :] = _dot_mask(tri, blk, "nn") + carry
            carry = carry + jnp.sum(blk, axis=0, keepdims=True)

    spec = pl.BlockSpec((seq, LANES), lambda s: (s, 0))
    return pl.pallas_call(body, name=name, grid=(nseq,), in_specs=[spec], out_specs=spec,
                          out_shape=jax.ShapeDtypeStruct(x.shape, F32), compiler_params=_params(1))(x)


def transpose_time(name, x, nseq, seq):
    def body(x_ref, o_ref):
        o_ref[...] = x_ref[...].T

    return pl.pallas_call(
        body, name=name, grid=(nseq,), in_specs=[pl.BlockSpec((seq, LANES), lambda s: (s, 0))],
        out_specs=pl.BlockSpec((LANES, seq), lambda s: (s, 0)),
        out_shape=jax.ShapeDtypeStruct((nseq * LANES, seq), F32), compiler_params=_params(1))(x)


def _gdn_masks():
    n = GDN_ROWS
    r, c = _iota((n, n), 0), _iota((n, n), 1)
    shift = GDN_CHUNK.bit_length() - 1
    same = lax.shift_right_logical(r, shift) == lax.shift_right_logical(c, shift)
    return r, c, same


def _each(fn, *lists):
    return [fn(*xs) for xs in zip(*lists)]


def _gdn_decay(gbs):
    r, c, same = _gdn_masks()
    seg_tril = jnp.where(jnp.logical_and(same, r >= c), 1.0, 0.0).astype(F32)
    g_cum = _each(lambda gb: mm_mask(seg_tril, gb), gbs)
    lane0 = _iota(gbs[0].shape, 1) == 0
    g_col = _each(lambda g: jnp.sum(jnp.where(lane0, g, 0.0), axis=1, keepdims=True), g_cum)
    g_row = _each(lambda g: jnp.sum(jnp.where(r == c, jnp.broadcast_to(g, (GDN_ROWS, GDN_ROWS)), 0.0), axis=0, keepdims=True), g_col)
    return g_cum, _each(lambda a, b: a - b, g_col, g_row)


def gdn_f1(*args):
    qs, ks, gbs, bbs = (list(args[i::4]) for i in range(4))
    r, c, same = _gdn_masks()
    strict = jnp.logical_and(same, r > c)
    _, diff = _gdn_decay(gbs)
    lane0 = _iota(bbs[0].shape, 1) == 0
    beta_col = _each(lambda bb: jnp.sum(jnp.where(lane0, bb, 0.0), axis=1, keepdims=True), bbs)
    kk = _each(lambda k: _dot(k, k, "nt", LO), ks)
    return tuple(_each(lambda b, x, d: jnp.where(strict, b * x * jnp.exp(jnp.where(strict, d, 0.0)), 0.0), beta_col, kk, diff))


def gdn_f2(*args):
    ts, qs, ks, vs, gbs, bbs = (list(args[i::6]) for i in range(6))
    r, c, same = _gdn_masks()
    incl = jnp.logical_and(same, r >= c)
    g_cum, diff = _gdn_decay(gbs)
    decay = _each(lambda d: jnp.where(incl, jnp.exp(jnp.where(incl, d, 0.0)), 0.0), diff)
    e_g = _each(jnp.exp, g_cum)
    v_beta = _each(lambda v, bb: v * bb, vs, bbs)
    k_beta = _each(lambda k, bb, e: k * bb * e, ks, bbs, e_g)
    value = _each(lambda t, x: x + _dot(t, x, "nn", LO), ts, v_beta)
    k_cum = _each(lambda t, x: x + _dot(t, x, "nn", LO), ts, k_beta)
    attn = _each(lambda q, k, d: _dot(q, k, "nt", LO) * d, qs, ks, decay)
    ones = jnp.where(same, 1.0, 0.0).astype(F32)
    g_last = _each(lambda gb: mm_mask(ones, gb), gbs)
    q_dec = _each(lambda q, e: q * e, qs, e_g)
    k_dec = _each(lambda k, gl, g: k * jnp.exp(gl - g), ks, g_last, g_cum)
    return tuple(x for head in zip(value, k_cum, attn, q_dec, k_dec) for x in head)


def tri_inverse(mats):
    n = GDN_ROWS
    r, c = _iota((n, n), 0), _iota((n, n), 1)
    shift = GDN_BASE.bit_length() - 1
    blk = lax.shift_right_logical(r, shift) == lax.shift_right_logical(c, shift)
    each = lambda fn, *lists: [fn(*xs) for xs in zip(*lists)]
    mm = lambda x, y: _dot(x, y, "nn", LO)
    d = each(lambda a: jnp.where(blk, a, 0.0), mats)
    lo = each(lambda a, dd: a - dd, mats, d)
    p = each(lambda dd: -dd, d)
    c_d = p
    for _ in range(shift - 1):
        p = each(mm, p, p)
        c_d = each(lambda cd, pp, prod: cd + pp + prod, c_d, p, each(mm, c_d, p))
    assert GDN_CHUNK // GDN_BASE == 4
    nmat = each(lambda l, prod: l + prod, lo, each(mm, c_d, lo))
    n2 = each(mm, nmat, nmat)
    c_n = each(lambda nn2, nm, prod: (nn2 - nm) - prod, n2, nmat, each(mm, nmat, n2))
    return each(lambda cn, cd, prod: cn + cd + prod, c_n, c_d, each(mm, c_n, c_d))


GDN_AHP = 4


def _gdn_a_specs():
    blk = pl.BlockSpec((GDN_ROWS, GDN_AHP * LANES), lambda i, h: (i, h))
    sq = pl.BlockSpec((GDN_ROWS, GDN_AHP * GDN_ROWS), lambda i, h: (i, h))
    return blk, sq


def _head(ref, hh):
    width = ref.shape[1] // GDN_AHP
    return ref.at[:, hh * width:(hh + 1) * width]


def gdn_a_fwd(q, k, v, gb, bb, rows):
    blk, sq = _gdn_a_specs()

    def body(q_ref, k_ref, v_ref, gb_ref, bb_ref, val_ref, kc_ref, at_ref, qd_ref, kd_ref, t_ref):
        heads = [[_head(r, hh)[...] for r in (q_ref, k_ref, v_ref, gb_ref, bb_ref)] for hh in range(GDN_AHP)]
        t_corr = tri_inverse(list(gdn_f1(*[x for qv, kv, vv, gv, bv in heads for x in (qv, kv, gv, bv)])))
        res = gdn_f2(*[x for t, head in zip(t_corr, heads) for x in (t, *head)])
        for hh in range(GDN_AHP):
            for r, x in zip((val_ref, kc_ref, at_ref, qd_ref, kd_ref, t_ref), (*res[5 * hh:5 * hh + 5], t_corr[hh])):
                _head(r, hh)[...] = x.astype(r.dtype)

    wide = lambda dt: jax.ShapeDtypeStruct((rows, NH * LANES), dt)
    square = jax.ShapeDtypeStruct((rows, NH * GDN_ROWS), BF16)
    return pl.pallas_call(
        body, name="gdn_a_fwd", grid=(rows // GDN_ROWS, NH // GDN_AHP), in_specs=[blk] * 5,
        out_specs=[blk, blk, sq, blk, blk, sq], out_shape=[wide(F32), wide(BF16), square, wide(BF16), wide(BF16), square],
        compiler_params=_params(2))(q, k, v, gb, bb)


def gdn_a_bwd(q, k, v, gb, bb, t_inv, dval, dkc, dat, dqd, dkd, dgb_b, rows):
    blk, sq = _gdn_a_specs()

    def body(q_ref, k_ref, v_ref, gb_ref, bb_ref, t_ref, dval_ref, dkc_ref, dat_ref, dqd_ref, dkd_ref, dgbb_ref,
             dq_ref, dk_ref, dv_ref, dgb_ref, dbb_ref):
        hs = range(GDN_AHP)
        heads = [[_head(r, hh)[...] for r in (q_ref, k_ref, v_ref, gb_ref, bb_ref)] for hh in hs]
        tvs = [_head(t_ref, hh)[...].astype(F32) for hh in hs]
        _, vjp1 = jax.vjp(gdn_f1, *[x for qv, kv, vv, gv, bv in heads for x in (qv, kv, gv, bv)])
        _, vjp2 = jax.vjp(gdn_f2, *[x for t, head in zip(tvs, heads) for x in (t, *head)])
        g2 = vjp2(tuple(_head(r, hh)[...] for hh in hs for r in (dval_ref, dkc_ref, dat_ref, dqd_ref, dkd_ref)))
        dts = [g2[6 * hh] for hh in hs]
        left = _each(lambda dt, tv: dt + _dot(tv, dt, "tn", LO), dts, tvs)
        g1 = vjp1(tuple(_each(lambda lf, tv: -(lf + _dot(lf, tv, "nt", LO)), left, tvs)))
        for hh in hs:
            dq1, dk1, dgb1, dbb1 = g1[4 * hh:4 * hh + 4]
            _, dq2, dk2, dv2, dgb2, dbb2 = g2[6 * hh:6 * hh + 6]
            _head(dq_ref, hh)[...] = dq1 + dq2
            _head(dk_ref, hh)[...] = dk1 + dk2
            _head(dv_ref, hh)[...] = dv2
            _head(dgb_ref, hh)[...] = dgb1 + dgb2 + _head(dgbb_ref, hh)[...]
            _head(dbb_ref, hh)[...] = dbb1 + dbb2

    wide = jax.ShapeDtypeStruct((rows, NH * LANES), F32)
    return pl.pallas_call(
        body, name="gdn_a_bwd", grid=(rows // GDN_ROWS, NH // GDN_AHP),
        in_specs=[blk] * 5 + [sq, blk, blk, sq, blk, blk, blk], out_specs=[blk] * 5, out_shape=[wide] * 5,
        compiler_params=_params(2))(q, k, v, gb, bb, t_inv, dval, dkc, dat, dqd, dkd, dgb_b)


N_CH = GDN_ROWS // GDN_CHUNK


GDN_HP = 8


def gdn_fb(*args):
    per_head = 6 * N_CH
    states = list(args[GDN_HP * per_head:])
    outs = [[None] * N_CH for _ in range(GDN_HP)]
    zero = jnp.zeros((GDN_CHUNK, LANES), F32)
    for c in range(N_CH):
        for hh in range(GDN_HP):
            val, kc, at, qd, kd, gb = (args[hh * per_head + i * N_CH + c] for i in range(6))
            s = states[hh]
            v_new = val - _dot(kc, s, "nn", LO)
            v_pad = jnp.concatenate([zero] * c + [v_new] + [zero] * (N_CH - 1 - c), axis=0)
            outs[hh][c] = _dot(qd, s, "nn", LO) + _dot(at, v_pad, "nn", LO)
            dec = jnp.exp(jnp.sum(gb, axis=0, keepdims=True))
            states[hh] = s * dec + _dot(kd, v_new, "tn", LO)
    return (*[o for head in outs for o in head], *states)


def _gdn_piece(ref, hh, c):
    width = ref.shape[1] // GDN_HP
    return ref.at[c * GDN_CHUNK:(c + 1) * GDN_CHUNK, hh * width:(hh + 1) * width]


def _gdn_pieces(refs, hh):
    return [_gdn_piece(r, hh, c)[...].astype(F32) for r in refs for c in range(N_CH)]


def _gdn_b_specs(nb, rev):
    def blk_row(s, j):
        return s * nb + (nb - 1 - j if rev else j)

    blk = pl.BlockSpec((GDN_ROWS, GDN_HP * LANES), lambda s, hb, j: (blk_row(s, j), hb))
    sq = pl.BlockSpec((GDN_ROWS, GDN_HP * GDN_ROWS), lambda s, hb, j: (blk_row(s, j), hb))
    snap = pl.BlockSpec((GDN_HP * LANES, LANES), lambda s, hb, j: (blk_row(s, j) * (NH // GDN_HP) + hb, 0))
    return blk, sq, snap


def gdn_b_fwd(val, kc, at, qd, kd, gb, nseq, seq):
    nb = seq // GDN_ROWS
    rows = nseq * seq
    blk, sq, snap = _gdn_b_specs(nb, False)

    def body(val_ref, kc_ref, at_ref, qd_ref, kd_ref, gb_ref, o_ref, snap_ref, s_ref):
        @pl.when(pl.program_id(2) == 0)
        def _():
            s_ref[...] = jnp.zeros_like(s_ref)

        states = [s_ref[hh] for hh in range(GDN_HP)]
        for hh in range(GDN_HP):
            snap_ref[hh * LANES:(hh + 1) * LANES, :] = states[hh]
        pieces = [p for hh in range(GDN_HP) for p in _gdn_pieces([val_ref, kc_ref, at_ref, qd_ref, kd_ref, gb_ref], hh)]
        res = gdn_fb(*pieces, *states)
        for hh in range(GDN_HP):
            for c in range(N_CH):
                _gdn_piece(o_ref, hh, c)[...] = res[hh * N_CH + c]
            s_ref[hh] = res[GDN_HP * N_CH + hh]

    return pl.pallas_call(
        body, name="gdn_b_fwd", grid=(nseq, NH // GDN_HP, nb), in_specs=[blk, blk, sq, blk, blk, blk], out_specs=[blk, snap],
        out_shape=[jax.ShapeDtypeStruct((rows, NH * LANES), F32), jax.ShapeDtypeStruct((nseq * nb * NH * LANES, LANES), F32)],
        scratch_shapes=[pltpu.VMEM((GDN_HP, LANES, LANES), F32)], compiler_params=_params(3))(val, kc, at, qd, kd, gb)


def gdn_b_bwd(val, kc, at, qd, kd, gb, snaps, do, nseq, seq):
    nb = seq // GDN_ROWS
    rows = nseq * seq
    blk, sq, snap = _gdn_b_specs(nb, True)

    def body(val_ref, kc_ref, at_ref, qd_ref, kd_ref, gb_ref, snap_ref, do_ref,
             dval_ref, dkc_ref, dat_ref, dqd_ref, dkd_ref, dgb_ref, ds_ref):
        @pl.when(pl.program_id(2) == 0)
        def _():
            ds_ref[...] = jnp.zeros_like(ds_ref)

        pieces = [p for hh in range(GDN_HP) for p in _gdn_pieces([val_ref, kc_ref, at_ref, qd_ref, kd_ref, gb_ref], hh)]
        states = [snap_ref[hh * LANES:(hh + 1) * LANES, :] for hh in range(GDN_HP)]
        _, vjp = jax.vjp(gdn_fb, *pieces, *states)
        cts = [p for hh in range(GDN_HP) for p in _gdn_pieces([do_ref], hh)] + [ds_ref[hh] for hh in range(GDN_HP)]
        grads = vjp(tuple(cts))
        for hh in range(GDN_HP):
            for i, r in enumerate([dval_ref, dkc_ref, dat_ref, dqd_ref, dkd_ref, dgb_ref]):
                for c in range(N_CH):
                    _gdn_piece(r, hh, c)[...] = grads[hh * 6 * N_CH + i * N_CH + c]
            ds_ref[hh] = grads[GDN_HP * 6 * N_CH + hh]

    wide = jax.ShapeDtypeStruct((rows, NH * LANES), F32)
    square = jax.ShapeDtypeStruct((rows, NH * GDN_ROWS), F32)
    return pl.pallas_call(
        body, name="gdn_b_bwd", grid=(nseq, NH // GDN_HP, nb), in_specs=[blk, blk, sq, blk, blk, blk, snap, blk],
        out_specs=[blk, blk, sq, blk, blk, blk], out_shape=[wide, wide, square, wide, wide, wide],
        scratch_shapes=[pltpu.VMEM((GDN_HP, LANES, LANES), F32)], compiler_params=_params(3))(val, kc, at, qd, kd, gb, snaps, do)


FOX_Q, FOX_K, FOX_V = 4 * NH, 5 * NH, 6 * NH
FOX_SCALE = LANES ** -0.5


def _head_row(ct_ref, h, off, width):
    blk = ct_ref[:, pl.ds(off, width)]
    return jnp.sum(jnp.where(_iota(blk.shape, 0) == h, blk, 0.0), axis=0, keepdims=True)


def _col(x):
    return jnp.max(x, axis=1, keepdims=True)


def _row(x):
    return jnp.max(x.T, axis=0, keepdims=True)


def _causal(shape, q_dim):
    return _iota(shape, q_dim) >= _iota(shape, 1 - q_dim)


def fox_fwd(qn, kn, proj, ct, nseq, seq):
    tq = tk = min(ATT_TILE, seq)
    nq = seq // tq
    rows = nseq * seq
    qblk = pl.BlockSpec((tq, LANES), lambda s, h, i: (s * nq + i, h))
    full = pl.BlockSpec((seq, LANES), lambda s, h, i: (s, h))
    vfull = pl.BlockSpec((seq, LANES), lambda s, h, i: (s, h + FOX_V))
    ctb = pl.BlockSpec((NH, seq), lambda s, h, i: (s * (LANES // NH) + 2, 0))

    def body(q_ref, k_ref, v_ref, ct_ref, o_ref, o16_ref, lse_ref):
        h, i = pl.program_id(1), pl.program_id(2)
        q = q_ref[...]

        def step(j, carry, diag):
            m, l, acc = carry
            off = pl.multiple_of(j * tk, tk)
            s = _dot(q, k_ref[pl.ds(off, tk), :], "nt") * FOX_SCALE - _head_row(ct_ref, h, off, tk)
            if diag:
                s = jnp.where(_causal(s.shape, 0), s, NEG)
            m_new = jnp.maximum(m, jnp.max(s, axis=1, keepdims=True))
            p = jnp.exp(s - m_new)
            alpha = jnp.exp(m - m_new)
            l = alpha * l + jnp.sum(p, axis=1, keepdims=True)
            acc = alpha * acc + _dot(p.astype(BF16), v_ref[pl.ds(off, tk), :].astype(BF16), "nn")
            return m_new, l, acc

        init = (jnp.full((tq, 1), NEG, F32), jnp.zeros((tq, 1), F32), jnp.zeros((tq, LANES), F32))
        carry = lax.fori_loop(0, i, lambda j, c: step(j, c, False), init)
        m, l, acc = step(i, carry, True)
        o = acc / l
        o_ref[...] = o
        o16_ref[...] = o.astype(BF16)
        lse_ref[...] = jnp.broadcast_to(m + jnp.log(l), (tq, LANES))

    wide = (rows, NH * LANES)
    return pl.pallas_call(
        body, name="fox_fwd", grid=(nseq, NH, nq), in_specs=[qblk, full, vfull, ctb], out_specs=[qblk] * 3,
        out_shape=[jax.ShapeDtypeStruct(wide, F32), jax.ShapeDtypeStruct(wide, BF16), jax.ShapeDtypeStruct(wide, F32)],
        compiler_params=_params(3))(qn, kn, proj, ct)


def fox_dq(qn, kn, proj, ct, do, lse, delta, nseq, seq):
    tq = tk = min(ATT_TILE, seq)
    nq = seq // tq
    rows = nseq * seq
    qblk = pl.BlockSpec((tq, LANES), lambda s, h, i: (s * nq + i, h))
    full = pl.BlockSpec((seq, LANES), lambda s, h, i: (s, h))
    vfull = pl.BlockSpec((seq, LANES), lambda s, h, i: (s, h + FOX_V))
    ctb = pl.BlockSpec((NH, seq), lambda s, h, i: (s * (LANES // NH) + 2, 0))

    def body(q_ref, k_ref, v_ref, ct_ref, do_ref, lse_ref, dl_ref, dq_ref, dc_ref):
        h, i = pl.program_id(1), pl.program_id(2)
        q = q_ref[...]
        lse, delta = _col(lse_ref[...]), _col(dl_ref[...])
        do16 = do_ref[...].astype(BF16)

        def step(j, carry, diag):
            dq, dc = carry
            off = pl.multiple_of(j * tk, tk)
            k = k_ref[pl.ds(off, tk), :]
            p = jnp.exp(_dot(q, k, "nt") * FOX_SCALE - _head_row(ct_ref, h, off, tk) - lse)
            if diag:
                p = jnp.where(_causal(p.shape, 0), p, 0.0)
            dp = _dot(do16, v_ref[pl.ds(off, tk), :].astype(BF16), "nt")
            ds = p * (dp - delta)
            return dq + _dot(ds.astype(BF16), k, "nn"), dc + jnp.sum(ds, axis=1, keepdims=True)

        init = (jnp.zeros((tq, LANES), F32), jnp.zeros((tq, 1), F32))
        dq, dc = step(i, lax.fori_loop(0, i, lambda j, c: step(j, c, False), init), True)
        dq_ref[...] = dq * FOX_SCALE
        dc_ref[...] = jnp.where(_iota((tq, LANES), 1) == 0, dc, 0.0)

    wide = jax.ShapeDtypeStruct((rows, NH * LANES), F32)
    return pl.pallas_call(
        body, name="fox_dq", grid=(nseq, NH, nq), in_specs=[qblk, full, vfull, ctb, qblk, qblk, qblk],
        out_specs=[qblk, qblk], out_shape=[wide, wide], compiler_params=_params(3))(qn, kn, proj, ct, do, lse, delta)


def fox_dkv(qn, kn, proj, cb, do, lse, delta, nseq, seq):
    tq = tk = min(ATT_TILE, seq)
    nq = seq // tq
    rows = nseq * seq
    kblk = pl.BlockSpec((tk, LANES), lambda s, h, j: (s * nq + j, h))
    vblk = pl.BlockSpec((tk, LANES), lambda s, h, j: (s * nq + j, h + FOX_V))
    full = pl.BlockSpec((seq, LANES), lambda s, h, j: (s, h))

    def body(q_ref, k_ref, v_ref, cb_ref, do_ref, lse_ref, dl_ref, dk_ref, dv_ref, dc_ref):
        j = pl.program_id(2)
        k = k_ref[...]
        v16 = v_ref[...].astype(BF16)
        ck = _col(cb_ref[...])

        def step(i, carry, diag):
            dk, dv, dc = carry
            off = pl.multiple_of(i * tq, tq)
            q = q_ref[pl.ds(off, tq), :]
            do16 = do_ref[pl.ds(off, tq), :].astype(BF16)
            lse, delta = (_row(r[pl.ds(off, tq), :]) for r in (lse_ref, dl_ref))
            p = jnp.exp(_dot(k, q, "nt") * FOX_SCALE - ck - lse)
            if diag:
                p = jnp.where(_causal(p.shape, 1), p, 0.0)
            dv = dv + _dot(p.astype(BF16), do16, "nn")
            ds = p * (_dot(v16, do16, "nt") - delta)
            return dk + _dot(ds.astype(BF16), q, "nn"), dv, dc + jnp.sum(ds, axis=1, keepdims=True)

        zero = jnp.zeros((tk, LANES), F32)
        carry = step(j, (zero, zero, jnp.zeros((tk, 1), F32)), True)
        dk, dv, dc = lax.fori_loop(j + 1, nq, lambda i, c: step(i, c, False), carry)
        dk_ref[...] = dk * FOX_SCALE
        dv_ref[...] = dv.astype(BF16)
        dc_ref[...] = jnp.where(_iota((tk, LANES), 1) == 0, -dc, 0.0)

    wide = (rows, NH * LANES)
    return pl.pallas_call(
        body, name="fox_dkv", grid=(nseq, NH, nq), in_specs=[full, kblk, vblk, kblk, full, full, full],
        out_specs=[kblk, kblk, kblk],
        out_shape=[jax.ShapeDtypeStruct(wide, F32), jax.ShapeDtypeStruct(wide, BF16), jax.ShapeDtypeStruct(wide, F32)],
        compiler_params=_params(3))(qn, kn, proj, cb, do, lse, delta)


def loss_head(out, tgt, rows, width):
    tb = ROW_TILE
    blk = pl.BlockSpec((tb, width), lambda i: (i, 0))
    accb = pl.BlockSpec((8, LANES), lambda i: (0, 0))

    def body(o_ref, t_ref, d32_ref, d16_ref, acc_ref):
        d = o_ref[...] - t_ref[...]
        row_loss = 0.5 * jnp.mean(d * d, axis=1, keepdims=True)
        g = d * (1.0 / width)
        d32_ref[...] = g
        d16_ref[...] = g.astype(BF16)
        part = jnp.where(_iota((tb, LANES), 1) == 0, row_loss, 0.0).reshape(tb // 8, 8, LANES).sum(axis=0)

        @pl.when(pl.program_id(0) == 0)
        def _():
            acc_ref[...] = part

        @pl.when(pl.program_id(0) != 0)
        def _():
            acc_ref[...] += part

    return pl.pallas_call(
        body, name="loss_head", grid=(rows // tb,), in_specs=[blk, blk], out_specs=[blk, blk, accb],
        out_shape=[jax.ShapeDtypeStruct((rows, width), F32), jax.ShapeDtypeStruct((rows, width), BF16),
                   jax.ShapeDtypeStruct((8, LANES), F32)], compiler_params=_params(1))(out, tgt)


def _adamw_update(w, g, m, v):
    m_new = ADAM_B1 * m + (1.0 - ADAM_B1) * g
    v_new = ADAM_B2 * v + (1.0 - ADAM_B2) * (g * g)
    m_hat = m_new / (1.0 - ADAM_B1 ** ADAM_STEP)
    v_hat = v_new / (1.0 - ADAM_B2 ** ADAM_STEP)
    return -ADAM_LR * (m_hat / (jnp.sqrt(v_hat) + ADAM_EPS) + ADAM_WD * w), m_new, v_new


def adamw(name, w, g, m, v):
    rows, cols = w.shape
    tb = min(rows, 128)
    assert rows % tb == 0
    blk = pl.BlockSpec((tb, cols), lambda i: (i, 0))

    def body(w_ref, g_ref, m_ref, v_ref, d_ref, mo_ref, vo_ref):
        d_ref[...], mo_ref[...], vo_ref[...] = _adamw_update(w_ref[...], g_ref[...], m_ref[...], v_ref[...])

    shp = jax.ShapeDtypeStruct(w.shape, F32)
    return pl.pallas_call(body, name=name, grid=(rows // tb,), in_specs=[blk] * 4, out_specs=[blk] * 3,
                          out_shape=[shp] * 3, compiler_params=_params(1))(w, g, m, v)


SPLIT_TILE = 128


def _tiled(shape2d, ax, n_lead, index):
    blk = (SPLIT_TILE, shape2d[1]) if ax == 0 else (shape2d[0], SPLIT_TILE)

    def index_map(*args):
        *lead, t = index(*args)
        return (*lead, t, 0) if ax == 0 else (*lead, 0, t)

    return pl.BlockSpec((None,) * n_lead + blk, index_map)


def adamw_halves(name, w, mine, other, m, v, c, ax):
    steps = w.shape[ax] // 2 // SPLIT_TILE
    assert w.shape[ax] == 2 * steps * SPLIT_TILE

    def body(c_ref, w_ref, mine_ref, other_ref, m_ref, v_ref, g_ref, d_ref, mo_ref, vo_ref):
        g = jnp.where(pl.program_id(0) // steps == c_ref[0], mine_ref[...], other_ref[...])
        g_ref[...] = g
        d_ref[...], mo_ref[...], vo_ref[...] = _adamw_update(w_ref[...], g, m_ref[...], v_ref[...])

    blk = _tiled(w.shape, ax, 0, lambda i, c_ref: (i,))
    hblk = _tiled(mine.shape, ax, 0, lambda i, c_ref: (i % steps,))
    grid_spec = pltpu.PrefetchScalarGridSpec(num_scalar_prefetch=1, grid=(2 * steps,),
                                             in_specs=[blk, hblk, hblk, blk, blk], out_specs=[blk] * 4)
    shp = jax.ShapeDtypeStruct(w.shape, F32)
    return pl.pallas_call(body, name=name, grid_spec=grid_spec, out_shape=[shp] * 4,
                          compiler_params=_params(1))(c, w, mine, other, m, v)


def add_chips(name, slots, parts, chip, axes):
    outs = []
    for idx, (x, own, ax) in enumerate(zip(slots, parts, axes)):
        n, shape2d = x.shape[0], x.shape[1:]
        steps = shape2d[ax] // SPLIT_TILE
        assert shape2d[ax] == steps * SPLIT_TILE

        def body(me_ref, *refs, n=n):
            o_ref = refs[n + 1]
            acc = None
            for t in range(n):
                term = jnp.where(me_ref[0] == t, refs[n][...], refs[t][...]).astype(F32)
                acc = term if acc is None else acc + term
            o_ref[...] = acc

        def filled(t, n=n):
            return lambda i, me_ref: (jnp.where(me_ref[0] == t, (t + 1) % n, t), i)

        grid_spec = pltpu.PrefetchScalarGridSpec(
            num_scalar_prefetch=1, grid=(steps,),
            in_specs=[_tiled(shape2d, ax, 1, filled(t)) for t in range(n)]
            + [_tiled(shape2d, ax, 1, lambda i, me_ref: (me_ref[0], i))],
            out_specs=_tiled(shape2d, ax, 0, lambda i, me_ref: (i,)))
        outs.append(pl.pallas_call(
            body, name=f"{name}_{idx}", grid_spec=grid_spec, out_shape=jax.ShapeDtypeStruct(shape2d, F32),
            compiler_params=_params(1))(chip, *([x] * n), own))
    return outs


def add_pair(name, gs, rs, c, axes):
    outs = []
    for idx, (g, r, ax) in enumerate(zip(gs, rs, axes)):
        nb = r.shape[0]
        steps = r.shape[1 + ax] // SPLIT_TILE
        assert r.shape[1 + ax] == steps * SPLIT_TILE

        def body(c_ref, g_ref, r_ref, o_ref):
            o_ref[...] = (g_ref[...] + r_ref[...]).astype(BF16)

        grid_spec = pltpu.PrefetchScalarGridSpec(
            num_scalar_prefetch=1, grid=(nb, steps),
            in_specs=[_tiled(g.shape[1:], ax, 1, lambda b, i, c_ref: (b, c_ref[0] * steps + i)),
                      _tiled(r.shape[1:], ax, 1, lambda b, i, c_ref: (b, i))],
            out_specs=_tiled(r.shape[1:], ax, 1, lambda b, i, c_ref: (b, i)))
        outs.append(pl.pallas_call(
            body, name=f"{name}_{idx}", grid_spec=grid_spec, out_shape=jax.ShapeDtypeStruct(r.shape, BF16),
            compiler_params=_params(2))(c, g, r))
    return outs


def _place():
    x, y, c = lax.axis_index("x"), lax.axis_index("y"), lax.axis_index("c")
    return x, y, c, [(1 - x, y), (x, 1 - y), (1 - x, 1 - y)]


def _remote(src, dst, send_sem, recv_sem, dev):
    return pltpu.make_async_remote_copy(src_ref=src, dst_ref=dst, send_sem=send_sem, recv_sem=recv_sem,
                                        device_id=dev, device_id_type=MESH)


def _half(ref, lead, ax, which):
    size = ref.shape[len(lead) + ax] // 2
    part = pl.ds(which * size, size)
    return ref.at[(*lead, part, slice(None)) if ax == 0 else (*lead, slice(None), part)]


def gather_weights(shards, axes):
    n = len(shards)

    def body(*refs):
        ins, outs = refs[:n], refs[n:2 * n]
        ici_s, ici_r, d2d_s, d2d_r = refs[2 * n:]
        x, y, c, chips = _place()
        me = 2 * x + y
        sends, passes = [], []
        for w in range(n):
            cp = _remote(ins[w], outs[w].at[me], d2d_s.at[3 * n + w], d2d_r.at[3 * n + w], (x, y, 1 - c))
            cp.start()
            passes.append(cp)
        for w in range(n):
            for j, (ox, oy) in enumerate(chips):
                cp = _remote(_half(ins[w], (), axes[w], c), _half(outs[w], (me,), axes[w], c),
                             ici_s.at[3 * w + j], ici_r.at[3 * w + j], (ox, oy, c))
                cp.start()
                sends.append(cp)
        for w in range(n):
            for j, (ox, oy) in enumerate(chips):
                landed = _half(outs[w], (2 * ox + oy,), axes[w], c)
                _remote(landed, landed, ici_s.at[3 * w + j], ici_r.at[3 * w + j], (ox, oy, c)).wait_recv()
                cp = _remote(landed, landed, d2d_s.at[3 * w + j], d2d_r.at[3 * w + j], (x, y, 1 - c))
                cp.start()
                passes.append(cp)
        for w in range(n):
            for j, (ox, oy) in enumerate(chips):
                other = _half(outs[w], (2 * ox + oy,), axes[w], 1 - c)
                _remote(other, other, d2d_s.at[3 * w + j], d2d_r.at[3 * w + j], (x, y, 1 - c)).wait_recv()
            own = outs[w].at[me]
            _remote(own, own, d2d_s.at[3 * n + w], d2d_r.at[3 * n + w], (x, y, 1 - c)).wait_recv()
        for cp in sends + passes:
            cp.wait_send()

    return pl.pallas_call(
        body, name="gather_weights", in_specs=[ANY] * n, out_specs=[ANY] * n,
        out_shape=[jax.ShapeDtypeStruct((4,) + s.shape, s.dtype) for s in shards],
        scratch_shapes=[pltpu.SemaphoreType.DMA((3 * n,))] * 2 + [pltpu.SemaphoreType.DMA((4 * n,))] * 2,
    )(*shards)


HBM = pl.BlockSpec(memory_space=pltpu.HBM)
SEM = pl.BlockSpec(memory_space=pltpu.SEMAPHORE)
DATAFLOW = pltpu.SideEffectType.DATAFLOW_SIDE_EFFECTING


def _hbm(a):
    return pltpu.with_memory_space_constraint(a, pltpu.HBM)


class SplitExchange:
    def __init__(self, name, srcs, zone_shapes, n_sems, plan):
        self.name, self.n, self.n_sems, self.plan = name, len(srcs), n_sems, plan
        self.srcs = [_hbm(s) for s in srcs]
        self.zones = [_hbm(lax.empty(shape, s.dtype)) for shape, s in zip(zone_shapes, srcs)]

    def start(self, after):
        n, n_after = self.n, len(after)

        def body(*refs):
            ins, lands = refs[:n], refs[n:2 * n]
            send, recv, token = refs[2 * n + n_after], refs[2 * n + n_after + 1], refs[-1]
            for src, dst, si, ri, dev in self.plan(ins, lands)[0]:
                _remote(src, dst, send.at[si], recv.at[ri], dev).start()
            token[...] = jnp.zeros_like(token)

        res = pl.pallas_call(
            body, name=f"{self.name}_start", in_specs=[HBM] * (2 * n) + [ANY] * n_after,
            out_specs=[SEM, SEM] + [HBM] * (2 * n) + [pl.BlockSpec(memory_space=pltpu.VMEM)],
            out_shape=[pltpu.SemaphoreType.DMA((self.n_sems,)), pltpu.SemaphoreType.DMA((self.n_sems,))]
            + [pltpu.HBM(a.shape, a.dtype) for a in self.srcs + self.zones] + [jax.ShapeDtypeStruct((8, LANES), F32)],
            input_output_aliases={i: 2 + i for i in range(2 * n)},
            compiler_params=pltpu.CompilerParams(has_side_effects=DATAFLOW),
        )(*self.srcs, *self.zones, *after)
        self.sems, self.srcs, self.zones = res[:2], list(res[2:2 + n]), list(res[2 + n:2 + 2 * n])
        return res[-1]

    def wait(self, after):
        n = self.n

        def body(*refs):
            ins, lands = refs[:n], refs[n:2 * n]
            send, recv = refs[2 * n], refs[2 * n + 1]
            sends, arrivals = self.plan(ins, lands)
            for src, _, si, _, dev in sends:
                _remote(src, src, send.at[si], recv.at[si], dev).wait_send()
            for landed, ri in arrivals:
                _remote(landed, landed, send.at[ri], recv.at[ri], _place()[:3]).wait_recv()

        res = pl.pallas_call(
            body, name=f"{self.name}_wait", in_specs=[HBM] * (2 * n) + [SEM, SEM, ANY], out_specs=[HBM] * (2 * n),
            out_shape=[pltpu.HBM(a.shape, a.dtype) for a in self.srcs + self.zones],
            input_output_aliases={i: i for i in range(2 * n)},
            compiler_params=pltpu.CompilerParams(has_side_effects=DATAFLOW),
        )(*self.srcs, *self.zones, *self.sems, after)
        self.srcs = list(res[:n])
        return list(res[n:])


def split_gather(shards):
    n = len(shards)

    def plan(ins, lands):
        x, y, c, chips = _place()
        me = 2 * x + y
        sends, arrivals = [], []
        for w in range(n):
            for j, (ox, oy) in enumerate(chips):
                for k in range(2):
                    base = 2 * (3 * w + j)
                    sends.append((_half(ins[w], (), 0, c), _half(lands[w], (me,), 0, c), base + k, base + c, (ox, oy, k)))
                    arrivals.append((_half(lands[w], (2 * ox + oy,), 0, k), base + k))
            sends.append((ins[w], lands[w].at[me], 6 * n + w, 6 * n + w, (x, y, 1 - c)))
            arrivals.append((lands[w].at[me], 6 * n + w))
        return sends, arrivals

    return SplitExchange("gather", shards, [(4,) + s.shape for s in shards], 7 * n, plan)


def split_pair_swap(name, grads, axes):
    def plan(ins, lands):
        x, y, c, _ = _place()
        sends = [(_half(ins[w], (slice(None),), axes[w], 1 - c), lands[w], w, w, (x, y, 1 - c)) for w in range(len(ins))]
        return sends, [(lands[w], w) for w in range(len(ins))]

    halved = [tuple(d // 2 if i == 1 + ax else d for i, d in enumerate(g.shape)) for g, ax in zip(grads, axes)]
    return SplitExchange(name, grads, halved, len(grads), plan)


def split_chip_exchange(name, parts):
    def plan(ins, lands):
        x, y, c, chips = _place()
        sends, arrivals = [], []
        for w in range(len(ins)):
            for j, (ox, oy) in enumerate(chips):
                sends.append((ins[w].at[2 * ox + oy], lands[w].at[2 * x + y], 3 * w + j, 3 * w + j, (ox, oy, c)))
                arrivals.append((lands[w].at[2 * ox + oy], 3 * w + j))
        return sends, arrivals

    return SplitExchange(name, parts, [p.shape for p in parts], 3 * len(parts), plan)


def split_pair_send(halves):
    def plan(ins, lands):
        x, y, c, _ = _place()
        return ([(ins[w], lands[w], w, w, (x, y, 1 - c)) for w in range(len(ins))],
                [(lands[w], w) for w in range(len(ins))])

    return SplitExchange("pair_send", halves, [h.shape for h in halves], len(halves), plan)


def pair_send(halves):
    n = len(halves)

    def body(*refs):
        ins, outs = refs[:n], refs[n:2 * n]
        send, recv = refs[2 * n:]
        x, y, c, _ = _place()
        cps = [_remote(ins[w], outs[w], send.at[w], recv.at[w], (x, y, 1 - c)) for w in range(n)]
        for cp in cps:
            cp.start()
        for cp in cps:
            cp.wait_recv()
        for cp in cps:
            cp.wait_send()

    return pl.pallas_call(
        body, name="pair_send", in_specs=[ANY] * n, out_specs=[ANY] * n,
        out_shape=[jax.ShapeDtypeStruct(h.shape, h.dtype) for h in halves],
        scratch_shapes=[pltpu.SemaphoreType.DMA((n,))] * 2,
    )(*halves)


def all_reduce_small(name, vec, after=()):
    rows = vec.shape[0]

    def body(v_ref, *refs):
        o_ref, buf, send, recv = refs[len(after):]
        x, y, c, _ = _place()
        me = 4 * x + 2 * y + c
        buf[me] = v_ref[...]
        cps = []
        for k in range(1, 8):
            kx, ky, kc = (k >> 2) & 1, (k >> 1) & 1, k & 1
            peer = (x if kx == 0 else 1 - x, y if ky == 0 else 1 - y, c if kc == 0 else 1 - c)
            cp = _remote(v_ref, buf.at[me], send.at[k - 1], recv.at[k - 1], peer)
            cp.start()
            cps.append(cp)
        for k in range(1, 8):
            kx, ky, kc = (k >> 2) & 1, (k >> 1) & 1, k & 1
            px, py, pc = (x if kx == 0 else 1 - x, y if ky == 0 else 1 - y, c if kc == 0 else 1 - c)
            slot = buf.at[4 * px + 2 * py + pc]
            _remote(slot, slot, send.at[k - 1], recv.at[k - 1], (px, py, pc)).wait_recv()
        for cp in cps:
            cp.wait_send()
        acc = buf[0]
        for d in range(1, 8):
            acc = acc + buf[d]
        o_ref[...] = acc

    vm = pl.BlockSpec(memory_space=pltpu.VMEM)
    return pl.pallas_call(
        body, name=name, in_specs=[vm] + [ANY] * len(after), out_specs=vm, out_shape=jax.ShapeDtypeStruct(vec.shape, F32),
        scratch_shapes=[pltpu.VMEM((8, rows, LANES), F32), pltpu.SemaphoreType.DMA((7,)), pltpu.SemaphoreType.DMA((7,))],
    )(vec, *after)


class NoExchange:
    def __init__(self, late):
        self.late = late

    def late_weights(self, after):
        return self.late

    def reduce_start(self, grads):
        return jnp.zeros((8, LANES), F32)

    def reduce_exchange(self, after):
        return jnp.zeros((8, LANES), F32)

    def reduce_finish(self, after):
        return jnp.zeros((8, LANES), F32)

    def input_grad_start(self, dw_main, dw_small):
        return jnp.zeros((8, LANES), F32)

    def input_grad_exchange(self, after):
        return jnp.zeros((8, LANES), F32)


def local_step(x2, tgt2, g1, g2, gdn_ng, qn_g, kn_g, p1, p2, conv_w, wt_main, wt_small, hooks, nseq, seq):
    rows, dm = x2.shape
    wide = NH * LANES
    row = lambda a, off=0, w=None: (a, "row", off, a.shape[1] if w is None else w)
    rowh = lambda a, off=0, w=LANES: (a, "rowh", off, w)
    par = lambda a: (a, "par", 0, a.shape[1])
    parh = lambda a, off=0: (a, "parh", off, LANES)
    o_row = lambda w, dt: (w, "row", w, dt)
    o_rowh = lambda dt, tw=wide, w=LANES: (tw, "rowh", w, dt)

    u, = ew_fwd("rms1", f_rms, [row(x2), par(g1)], [o_row(dm, BF16)], rows)
    proj = matmul("mm_in", u, wt_main, "nt", BF16)
    sp = matmul("mm_in_small", u, wt_small, "nt", F32)
    so, = ew_fwd("small", f_small, [row(sp), par(p1), par(p2)], [o_row(LANES, F32)], rows)
    cs = cumsum_time("cumsum", so, nseq, seq, False)
    gb, bb, cb = ew_fwd("bcast", f_bcast, [row(so), row(cs)], [o_rowh(F32)] * 3, rows, NH)
    ct = transpose_time("c_time_major", cs, nseq, seq)
    conv = {}
    for mode, off in (("q", 0), ("k", NH), ("v", 2 * NH)):
        conv[mode], = ew_fwd(f"conv_{mode}", make_f_conv(mode), [rowh(proj, off), parh(conv_w, off)], [o_rowh(F32)],
                             rows, NH, seq, "hi", CONV_HEADS)
    val, kcum, attn, qdec, kdec, t_inv = gdn_a_fwd(conv["q"], conv["k"], conv["v"], gb, bb, rows)
    o_a, snaps = gdn_b_fwd(val, kcum, attn, qdec, kdec, gb, nseq, seq)
    ya_in, = ew_fwd("gdn_post", f_post, [rowh(o_a), rowh(proj, 3 * NH), par(gdn_ng)], [o_rowh(BF16)], rows, NH)
    fqn, = ew_fwd("fox_qn", f_rms, [rowh(proj, FOX_Q), par(qn_g)], [o_rowh(BF16)], rows, NH)
    fkn, = ew_fwd("fox_kn", f_rms, [rowh(proj, FOX_K), par(kn_g)], [o_rowh(BF16)], rows, NH)
    o_b, o_b16, lse = fox_fwd(fqn, fkn, proj, ct, nseq, seq)
    p_a, p_b, w_o, w_u, w_d = hooks.late_weights(o_a)
    y_a = matmul("mm_pa", ya_in, p_a, "nn", F32, tn=1024)
    y_b = matmul("mm_pb", o_b16, p_b, "nn", F32, tn=1024)
    gates = [row(proj, 7, dm), row(proj, 8, dm)]
    merged, = ew_fwd("merge", f_merge, gates + [row(y_a), row(y_b)], [o_row(dm, BF16)], rows)
    hres = matmul("mm_out", merged, w_o, "nn", F32, add=x2, tn=1024)
    hn, = ew_fwd("rms2", f_rms, [row(hres), par(g2)], [o_row(dm, BF16)], rows)
    up_blocks = w_u.shape[0]
    act, relu2 = matmul("mm_up", hn, w_u, "nn", F32, col_blocks=up_blocks, out_dtypes=[F32, BF16],
                        epilogue=lambda r: [r, jnp.maximum(r, 0.0) * jnp.maximum(r, 0.0)])
    out = matmul("mm_down", relu2, w_d, "nn", F32, add=hres, tn=1024)
    dout, dout16, loss_acc = loss_head(out, tgt2, rows, dm)

    d_act = matmul("mm_d_act", dout16, w_d, "nt", BF16, extras=[act], epilogue=lambda r, a: [2.0 * jnp.maximum(a, 0.0) * r])
    dw_d = matmul("mm_dw_down", relu2, dout16, "tn", F32, tn=1024)
    dw_u = matmul("mm_dw_up", hn, d_act, "tn", F32, col_blocks=up_blocks)
    d_hn = matmul("mm_d_hn", d_act, w_u, "nt", F32, col_blocks=up_blocks)
    dh, dh16, dg2 = ew_bwd("rms2_b", f_rms, [row(hres), par(g2)], [(row(d_hn),)], [row(dout)],
                           lambda g, e: [g[0] + e[0], g[0] + e[0], g[1]],
                           [((rows, dm), "row", dm, F32, None), ((rows, dm), "row", dm, BF16, None), ((1, dm), "par", dm, F32, "all")], rows)
    d_merged = matmul("mm_d_merged", dh16, w_o, "nt", F32, tn=1024)
    dw_o = matmul("mm_dw_out", merged, dh16, "tn", F32, tn=1024)
    seg16 = ((rows, dm), "row", dm, BF16, None)
    d_ga16, d_gb16, d_ya16, d_yb16 = ew_bwd("merge_b", f_merge, gates + [row(y_a), row(y_b)], [(row(d_merged),)], [],
                                            lambda g, e: list(g), [seg16] * 4, rows)
    dp_a = matmul("mm_dp_a", ya_in, d_ya16, "tn", F32, tn=1024)
    d_ya_in = matmul("mm_d_ya_in", d_ya16, p_a, "nt", F32, tn=1024)
    dp_b = matmul("mm_dp_b", o_b16, d_yb16, "tn", F32, tn=1024)
    d_ob = matmul("mm_d_ob", d_yb16, p_b, "nt", F32, tn=1024)
    token = hooks.reduce_start(dict(p_a=dp_a, p_b=dp_b, w_o=dw_o, w_u=dw_u, w_d=dw_d))
    gdn_ng_t = gdn_ng + token[0, 0]
    h32 = ((rows, wide), "rowh", LANES, F32, None)
    h16 = ((rows, wide), "rowh", LANES, BF16, None)
    gain = ((1, LANES), "par", LANES, F32, "all")
    d_oa, d_z16, d_gdn_ng = ew_bwd("gdn_post_b", f_post, [rowh(o_a), rowh(proj, 3 * NH), par(gdn_ng_t)], [(rowh(d_ya_in),)], [],
                                   lambda g, e: list(g), [h32, h16, gain], rows, NH)
    dval, dkc, dat, dqd, dkd, dgb_b = gdn_b_bwd(val, kcum, attn, qdec, kdec, gb, snaps, d_oa, nseq, seq)
    d_cq, d_ck, d_cv, d_gb, d_bb = gdn_a_bwd(conv["q"], conv["k"], conv["v"], gb, bb, t_inv, dval, dkc, dat, dqd, dkd, dgb_b, rows)
    token = hooks.reduce_exchange(d_cq)
    conv_w_t = conv_w + token[0, 0]
    d_pre, d_conv = {}, {}
    tap = ((4, wide), "parh", LANES, F32, "inner")
    for mode, off, ctg in (("q", 0, d_cq), ("k", NH, d_ck), ("v", 2 * NH, d_cv)):
        d_pre[mode], d_conv[mode] = ew_bwd(f"conv_{mode}_b", make_f_conv(mode), [rowh(proj, off), parh(conv_w_t, off)],
                                           [(rowh(ctg),)], [], lambda g, e: list(g), [h16, tap], rows, NH, seq, "hi", CONV_HEADS)
    delta, = ew_fwd("fox_delta", f_delta, [rowh(d_ob), rowh(o_b)], [o_rowh(F32)], rows, NH, after=[token])
    d_fqn, d_cq_b = fox_dq(fqn, fkn, proj, ct, d_ob, lse, delta, nseq, seq)
    d_fkn, d_fv16, d_ck_b = fox_dkv(fqn, fkn, proj, cb, d_ob, lse, delta, nseq, seq)
    token = hooks.reduce_finish(d_fkn)
    qn_g_t, kn_g_t = qn_g + token[0, 0], kn_g + token[0, 0]
    d_fq16, d_qn_g = ew_bwd("fox_qn_b", f_rms, [rowh(proj, FOX_Q), par(qn_g_t)], [(rowh(d_fqn),)], [], lambda g, e: list(g),
                            [h16, gain], rows, NH)
    d_fk16, d_kn_g = ew_bwd("fox_kn_b", f_rms, [rowh(proj, FOX_K), par(kn_g_t)], [(rowh(d_fkn),)], [], lambda g, e: list(g),
                            [h16, gain], rows, NH)
    narrow = ((rows, LANES), "row", LANES, F32, None)
    d_so, d_cs = ew_bwd("bcast_b", f_bcast, [row(so), row(cs)], [(rowh(d_gb),), (rowh(d_bb),), (rowh(d_cq_b), rowh(d_ck_b))], [],
                        lambda g, e: list(g), [narrow, narrow], rows, NH)
    d_logf = cumsum_time("cumsum_b", d_cs, nseq, seq, True)
    vec = ((1, LANES), "par", LANES, F32, "all")
    d_sp16, d_p1, d_p2 = ew_bwd("small_b", f_small, [row(sp), par(p1), par(p2)], [(row(d_so), row(d_logf))], [],
                                lambda g, e: list(g), [((rows, LANES), "row", LANES, BF16, None), vec, vec], rows)
    d_proj16 = jnp.concatenate([d_pre["q"], d_pre["k"], d_pre["v"], d_z16, d_fq16, d_fk16, d_fv16, d_ga16, d_gb16], axis=1)
    dw_main = matmul("mm_dw_main", d_proj16, u, "tn", F32)
    dw_small = matmul("mm_dw_small", d_sp16, u, "tn", F32)
    wt_small_t = wt_small + hooks.input_grad_start(dw_main, dw_small)[0, 0].astype(BF16)
    d_u = matmul("mm_d_u_small", d_sp16, wt_small_t, "nn", F32)
    d_u = matmul("mm_d_u_first", d_proj16, wt_main, "nn", F32, add=d_u, k_part=(0, 2))
    d_u = matmul("mm_d_u_second", d_proj16, wt_main, "nn", F32, add=d_u, k_part=(1, 2), after=[hooks.input_grad_exchange(d_u)])
    dx, dg1 = ew_bwd("rms1_b", f_rms, [row(x2), par(g1)], [(row(d_u),)], [row(dh)], lambda g, e: [g[0] + e[0], g[1]],
                     [((rows, dm), "row", dm, F32, None), ((1, dm), "par", dm, F32, "all")], rows)
    d_conv_w = jnp.concatenate([d_conv["q"], d_conv["k"], d_conv["v"]], axis=1)
    return dict(loss_acc=loss_acc, dx=dx, g1=dg1, g2=dg2, gdn_ng=d_gdn_ng, qn=d_qn_g, kn=d_kn_g, p1=d_p1, p2=d_p2,
                conv=d_conv_w, w_main=dw_main, w_small=dw_small, p_a=dp_a, p_b=dp_b, w_o=dw_o, w_u=dw_u, w_d=dw_d)


_W = NH * LANES
_A0, _A1 = 4 * _W, 4 * _W + 2 * NH
_B0, _B1 = _A1 + 3 * _W, _A1 + 3 * _W + NH
N_IN = _B1 + 2 * _W


def _split_w_in(full_t):
    main = jnp.concatenate([full_t[:_A0], full_t[_A1:_B0], full_t[_B1:]], axis=0)
    small = jnp.concatenate([full_t[_A0:_A1], full_t[_B0:_B1], jnp.zeros((LANES - 3 * NH, full_t.shape[1]), full_t.dtype)], axis=0)
    return main, small


def _join_w_in(main, small):
    return jnp.concatenate([main[:_A0], small[:2 * NH], main[_A0:_A0 + 3 * _W], small[2 * NH:3 * NH], main[_A0 + 3 * _W:]], axis=0)


def _lanes(v, at=0):
    return jnp.pad(v.reshape(1, -1), ((0, 0), (at, LANES - at - v.size)))


def kernel(x, norm_mix_g, w_in, gdn_conv_w, gdn_a_log, gdn_dt_bias, gdn_norm_g, fox_q_norm_g, fox_k_norm_g, fox_f_bias, w_proj_gdn, w_proj_fox, w_out, norm_mlp_g, w_up, w_down, loss_target, m_norm_mix_g, m_w_in, m_gdn_conv_w, m_gdn_a_log, m_gdn_dt_bias, m_gdn_norm_g, m_fox_q_norm_g, m_fox_k_norm_g, m_fox_f_bias, m_w_proj_gdn, m_w_proj_fox, m_w_out, m_norm_mlp_g, m_w_up, m_w_down, v_norm_mix_g, v_w_in, v_gdn_conv_w, v_gdn_a_log, v_gdn_dt_bias, v_gdn_norm_g, v_fox_q_norm_g, v_fox_k_norm_g, v_fox_f_bias, v_w_proj_gdn, v_w_proj_fox, v_w_out, v_norm_mlp_g, v_w_up, v_w_down):
    nseq, seq, dm = x.shape
    rows = nseq * seq
    xi, yi, ci = lax.axis_index("x"), lax.axis_index("y"), lax.axis_index("c")
    chip = 2 * xi + yi
    conv_cols = gdn_conv_w.shape[2]

    tr = lambda a: jnp.swapaxes(a[0], 0, 1)
    big = [tr(w_in), w_proj_gdn[0], w_proj_fox[0], w_out[0], w_up[0], w_down[0]]
    axes = [1, 0, 0, 0, 0, 0]
    big16 = [w.astype(BF16) for w in big]
    conv_slot = jnp.zeros((4, 4, conv_cols), F32).at[:, chip].set(jnp.where(ci == 0, gdn_conv_w[0], 0.0))
    conv_full = all_reduce_small("gather_conv", conv_slot.reshape(-1, LANES)).reshape(4, 4 * conv_cols)
    got_in, = gather_weights(big16[:1], axes[:1])
    wt_main, wt_small = _split_w_in(got_in.reshape(-1, dm))
    core, chip_no = ci.reshape(1).astype(jnp.int32), chip.reshape(1).astype(jnp.int32)
    gather = split_gather(big16[1:])
    token = gather.start([got_in, conv_full])

    class Hooks:
        def late_weights(self, after):
            g_pa, g_pb, g_wo, w_u, g_wd = gather.wait(after)
            return (*(g.reshape(-1, dm) for g in (g_pa, g_pb, g_wo)), w_u, g_wd.reshape(-1, dm))

        def reduce_start(self, grads):
            blocks = [grads["p_a"].reshape(4, -1, dm), grads["p_b"].reshape(4, -1, dm), grads["w_o"].reshape(4, -1, dm),
                      grads["w_u"], grads["w_d"].reshape(4, -1, dm)]
            self.swap = split_pair_swap("pair_swap_late", blocks, axes[1:])
            return self.swap.start([])

        def reduce_exchange(self, after):
            swapped = self.swap.wait(after)
            self.exchange = split_chip_exchange("chip_exchange_late", add_pair("add_pair_late", self.swap.srcs, swapped, core, axes[1:]))
            return self.exchange.start([])

        def reduce_finish(self, after):
            slots = self.exchange.wait(after)
            self.send = split_pair_send(add_chips("add_chips_late", slots, self.exchange.srcs, chip_no, axes[1:]))
            return self.send.start([])

        def input_grad_start(self, dw_main, dw_small):
            self.in_swap = split_pair_swap("pair_swap_in", [_join_w_in(dw_main, dw_small).reshape(4, -1, dm)], axes[:1])
            return self.in_swap.start([])

        def input_grad_exchange(self, after):
            swapped = self.in_swap.wait(after)
            self.in_exchange = split_chip_exchange("chip_exchange_in", add_pair("add_pair_in", self.in_swap.srcs, swapped, core, axes[:1]))
            return self.in_exchange.start([])

    hooks = Hooks()
    p1 = _lanes(gdn_dt_bias[0]) + _lanes(fox_f_bias[0], 2 * NH)
    p2 = _lanes(gdn_a_log[0])

    g = local_step(x.reshape(rows, dm), loss_target.reshape(rows, dm), norm_mix_g + token[0, 0], norm_mlp_g, gdn_norm_g,
                   fox_q_norm_g, fox_k_norm_g, p1, p2, conv_full, wt_main, wt_small, hooks, nseq, seq)

    others = hooks.send.wait(g["dx"])
    big_m = [tr(m_w_in), m_w_proj_gdn[0], m_w_proj_fox[0], m_w_out[0], m_w_up[0], m_w_down[0]]
    big_v = [tr(v_w_in), v_w_proj_gdn[0], v_w_proj_fox[0], v_w_out[0], v_w_up[0], v_w_down[0]]
    names = ["w_in", "w_proj_gdn", "w_proj_fox", "w_out", "w_up", "w_down"]
    big_res, big_grad = {}, {}
    for i in range(1, len(names)):
        big_grad[names[i]], *big_res[names[i]] = adamw_halves(f"adamw_{names[i]}", big[i], hooks.send.srcs[i - 1], others[i - 1],
                                                              big_m[i], big_v[i], core, axes[i])
    slots = hooks.in_exchange.wait(big_res[names[-1]][0])
    mine = add_chips("add_chips_in", slots, hooks.in_exchange.srcs, chip_no, axes[:1])
    res = adamw_halves("adamw_w_in", big[0], mine[0], pair_send(mine)[0], big_m[0], big_v[0], core, axes[0])
    big_grad["w_in"], *big_res["w_in"] = [jnp.swapaxes(r, 0, 1) for r in res]

    small_parts = [g["loss_acc"], g["g1"].reshape(8, LANES), g["g2"].reshape(8, LANES), g["gdn_ng"], g["qn"], g["kn"], g["p1"], g["p2"],
                   g["conv"].reshape(-1, LANES)]
    tiled = [jnp.pad(p, ((0, -p.shape[0] % 8), (0, 0))) for p in small_parts]
    red = all_reduce_small("reduce_small", jnp.concatenate(tiled, axis=0), slots)
    pos, red_parts = 0, []
    for p, t in zip(small_parts, tiled):
        red_parts.append(red[pos:pos + p.shape[0]])
        pos += t.shape[0]
    r_loss, r_g1, r_g2, r_gdn_ng, r_qn, r_kn, r_p1, r_p2, r_conv = red_parts
    loss = jnp.sum(r_loss)
    g_conv = lax.dynamic_slice_in_dim(r_conv.reshape(4, 4, conv_cols), chip, 1, axis=1).reshape(4, conv_cols)
    small_grads = [r_g1.reshape(1, dm), r_p2[:, :NH], r_p1[:, :NH], r_gdn_ng, r_qn, r_kn, r_p1[:, 2 * NH:3 * NH], r_g2.reshape(1, dm)]
    small_w = [norm_mix_g, gdn_a_log, gdn_dt_bias, gdn_norm_g, fox_q_norm_g, fox_k_norm_g, fox_f_bias, norm_mlp_g]
    small_m = [m_norm_mix_g, m_gdn_a_log, m_gdn_dt_bias, m_gdn_norm_g, m_fox_q_norm_g, m_fox_k_norm_g, m_fox_f_bias, m_norm_mlp_g]
    small_v = [v_norm_mix_g, v_gdn_a_log, v_gdn_dt_bias, v_gdn_norm_g, v_fox_q_norm_g, v_fox_k_norm_g, v_fox_f_bias, v_norm_mlp_g]

    def pack(parts):
        flat = jnp.concatenate([jnp.pad(p.reshape(-1), (0, -p.size % LANES)) for p in parts])
        return jnp.pad(flat, (0, -flat.size % (8 * LANES))).reshape(-1, LANES)

    packed = adamw("adamw_small", pack(small_w + [gdn_conv_w[0]]), pack(small_grads + [g_conv]),
                   pack(small_m + [m_gdn_conv_w[0]]), pack(small_v + [v_gdn_conv_w[0]]))

    def unpack(flat2d):
        flat, pos, res = flat2d.reshape(-1), 0, []
        for p in small_w + [gdn_conv_w[0]]:
            res.append(flat[pos:pos + p.size].reshape(p.shape))
            pos += p.size + (-p.size % LANES)
        return res

    s_delta, s_m, s_v = (unpack(a) for a in packed)

    order = ["norm_mix_g", "w_in", "gdn_conv_w", "gdn_a_log", "gdn_dt_bias", "gdn_norm_g", "fox_q_norm_g", "fox_k_norm_g",
             "fox_f_bias", "w_proj_gdn", "w_proj_fox", "w_out", "norm_mlp_g", "w_up", "w_down"]
    small_names = ["norm_mix_g", "gdn_a_log", "gdn_dt_bias", "gdn_norm_g", "fox_q_norm_g", "fox_k_norm_g", "fox_f_bias", "norm_mlp_g",
                   "gdn_conv_w"]
    small_idx = {nm: i for i, nm in enumerate(small_names)}
    shapes = dict(zip(order, (a.shape for a in (norm_mix_g, w_in, gdn_conv_w, gdn_a_log, gdn_dt_bias, gdn_norm_g, fox_q_norm_g,
                                                 fox_k_norm_g, fox_f_bias, w_proj_gdn, w_proj_fox, w_out, norm_mlp_g, w_up, w_down))))
    grads_out, delta_out, m_out, v_out = [], [], [], []
    for nm in order:
        if nm in big_res:
            d, mm, vv = big_res[nm]
            gr = big_grad[nm]
        else:
            i = small_idx[nm]
            gr = (small_grads + [g_conv])[i]
            d, mm, vv = s_delta[i], s_m[i], s_v[i]
        for lst, val in ((grads_out, gr), (delta_out, d), (m_out, mm), (v_out, vv)):
            lst.append(val.reshape(shapes[nm]))
    return (loss, g["dx"].reshape(x.shape), *grads_out, *delta_out, *m_out, *v_out)
```

```python
import functools

import jax
import jax.numpy as jnp
from jax import lax
from jax.experimental import pallas as pl
from jax.experimental.pallas import tpu as pltpu

F32 = jnp.float32
BF16 = jnp.bfloat16
LANES = 128
NH = 8
EPS = 1e-6
GDN_CHUNK = 64
GDN_ROWS = 256
GDN_BASE = 16
ROW_TILE = 512
CONV_HEADS = 2
ATT_TILE = 512
NEG = -1e30
VMEM_LIMIT_BYTES = 48 * 1024 * 1024
HI = lax.Precision.HIGHEST
LO = lax.Precision.DEFAULT
MESH = pl.DeviceIdType.MESH
ANY = pl.BlockSpec(memory_space=pl.ANY)

ADAM_LR, ADAM_B1, ADAM_B2, ADAM_EPS, ADAM_WD, ADAM_STEP = 0.001, 0.9, 0.999, 1e-08, 0.01, 10


def _params(n_grid):
    return pltpu.CompilerParams(dimension_semantics=("arbitrary",) * n_grid,
                                vmem_limit_bytes=VMEM_LIMIT_BYTES)


def _dot(a, b, dims, precision=None):
    dn = {"nn": (((1,), (0,)), ((), ())), "nt": (((1,), (1,)), ((), ())), "tn": (((0,), (0,)), ((), ()))}[dims]
    return lax.dot_general(a, b, dn, precision=precision, preferred_element_type=F32)


def _iota(shape, dim):
    return lax.broadcasted_iota(jnp.int32, shape, dim)


def _split(x, parts):
    out = []
    for _ in range(parts - 1):
        hi = x.astype(BF16)
        out.append(hi)
        x = x - hi.astype(F32)
    return out + [x.astype(BF16)]


def _dot_mask(mask, b, dims):
    m16 = mask.astype(BF16)
    b1, b2, b3 = _split(b, 3)
    return _dot(m16, b1, dims) + (_dot(m16, b2, dims) + _dot(m16, b3, dims))


@jax.custom_vjp
def mm_mask(mask, b):
    return _dot_mask(mask, b, "nn")


mm_mask.defvjp(lambda mask, b: (_dot_mask(mask, b, "nn"), mask),
               lambda mask, g: (jnp.zeros_like(mask), _dot_mask(mask, g, "tn")))


def matmul(name, a, b, dims, out_dtype, add=None, tm=1024, tn=1024, tk=512, col_blocks=None,
           extras=(), epilogue=None, out_dtypes=None, k_part=None, after=()):
    if col_blocks and dims != "tn":
        nb, b_rows, bw = b.shape
        b_shape = (b_rows, nb * bw)
    else:
        b_shape = b.shape
    if dims == "nn":
        (m, k), (_, n) = a.shape, b_shape
    elif dims == "nt":
        (m, k), (n, _) = a.shape, b_shape
    else:
        (k, m), (_, n) = a.shape, b_shape
    if k <= 1024:
        tk = k
    tm, tn, tk = min(tm, m), min(tn, n), min(tk, k)
    assert m % tm == 0 and n % tn == 0 and k % tk == 0, (name, m, n, k)
    k0, nk = (0, k // tk) if k_part is None else (k_part[0] * (k // tk // k_part[1]), k // tk // k_part[1])
    assert k_part is None or (dims == "nn" and not col_blocks and (k // tk) % k_part[1] == 0)
    a_spec = pl.BlockSpec((tk, tm), lambda i, j, kk: (kk, i)) if dims == "tn" else pl.BlockSpec((tm, tk), lambda i, j, kk: (i, kk + k0))
    b_spec = pl.BlockSpec((tn, tk), lambda i, j, kk: (j, kk)) if dims == "nt" else pl.BlockSpec((tk, tn), lambda i, j, kk: (kk + k0, j))
    o_spec = pl.BlockSpec((tm, tn), lambda i, j, kk: (i, j))
    out_shape = (m, n)
    if col_blocks and dims == "nn":
        per = bw // tn
        assert bw % tn == 0
        b_spec = pl.BlockSpec((None, tk, tn), lambda i, j, kk: (j // per, kk, j % per))
    elif col_blocks and dims == "nt":
        per = bw // tk
        assert bw % tk == 0
        b_spec = pl.BlockSpec((None, tn, tk), lambda i, j, kk: (kk // per, j, kk % per))
    elif col_blocks:
        bw = n // col_blocks
        per = bw // tn
        assert bw % tn == 0 and add is None
        o_spec = pl.BlockSpec((None, tm, tn), lambda i, j, kk: (j // per, i, j % per))
        out_shape = (col_blocks, m, bw)
    extras = list(extras) + ([add] if add is not None else [])
    if add is not None:
        assert epilogue is None
        epilogue = lambda r, *e: [r + e[-1]]
    out_dtypes = [out_dtype] if epilogue is None or out_dtypes is None else list(out_dtypes)
    n_ex, n_out = len(extras), len(out_dtypes)

    def body(*refs):
        a_ref, b_ref = refs[0], refs[1]
        ex_refs, o_refs = refs[2:2 + n_ex], refs[2 + n_ex + len(after):2 + n_ex + len(after) + n_out]

        def finish(r):
            res = [r] if epilogue is None else epilogue(r, *[e[...] for e in ex_refs])
            for o_ref, v in zip(o_refs, res):
                o_ref[...] = v.astype(o_ref.dtype)

        if nk == 1:
            finish(_dot(a_ref[...], b_ref[...], dims))
            return
        acc_ref = refs[-1]
        kk = pl.program_id(2)

        @pl.when(kk == 0)
        def _():
            acc_ref[...] = jnp.zeros_like(acc_ref)

        acc_ref[...] += _dot(a_ref[...], b_ref[...], dims)

        @pl.when(kk == nk - 1)
        def _():
            finish(acc_ref[...])

    res = pl.pallas_call(
        body, name=name, grid=(m // tm, n // tn, nk), in_specs=[a_spec, b_spec] + [o_spec] * n_ex + [ANY] * len(after),
        out_specs=[o_spec] * n_out, out_shape=[jax.ShapeDtypeStruct(out_shape, dt) for dt in out_dtypes],
        scratch_shapes=[pltpu.VMEM((tm, tn), F32)] if nk > 1 else [], compiler_params=_params(3),
    )(a, b, *extras, *after)
    return res[0] if n_out == 1 else res


def _ew_spec(kind, off, width, tb, hp, order, shape=None):
    def ih(g0, g1):
        return (g0, g1) if order == "ih" else (g1, g0)

    assert off % hp == 0 or kind in ("row", "par")
    if kind == "row":
        return pl.BlockSpec((tb, width), lambda g0, g1: (ih(g0, g1)[0], off))
    if kind == "rowh":
        return pl.BlockSpec((tb, hp * width), lambda g0, g1: (ih(g0, g1)[0], ih(g0, g1)[1] + off // hp))
    if kind == "par":
        return pl.BlockSpec(shape, lambda g0, g1: (0, 0))
    if kind == "parh":
        return pl.BlockSpec((shape[0], hp * width), lambda g0, g1: (0, ih(g0, g1)[1] + off // hp))
    raise ValueError(kind)


def _ew_grid(rows, tb, nh, hp, order):
    assert nh % hp == 0 and rows % tb == 0
    return (rows // tb, nh // hp) if order == "ih" else (nh // hp, rows // tb)


def _ew_load(ref, kind, width, hh):
    if kind in ("row", "par"):
        return ref[...].astype(F32)
    return ref[:, hh * width:(hh + 1) * width].astype(F32)


def ew_fwd(name, f, ins, outs, rows, nh=1, tb=ROW_TILE, order="ih", hp=None, after=()):
    hp = nh if hp is None else hp
    n_in = len(ins)

    def body(*refs):
        hb = pl.program_id(1) if order == "ih" else pl.program_id(0)
        for hh in range(hp):
            h = hh if hp == nh else hb * hp + hh
            vals = [_ew_load(r, kd, w, hh) for r, (_, kd, _, w) in zip(refs[:n_in], ins)]
            res = f(h, *vals)
            for r, v, (_, kd, w, _) in zip(refs[n_in + len(after):], res, outs):
                if kd == "row":
                    assert hp == 1
                    r[...] = v.astype(r.dtype)
                else:
                    r[:, hh * w:(hh + 1) * w] = v.astype(r.dtype)

    in_specs = [_ew_spec(kd, off, w, tb, hp, order, a.shape) for (a, kd, off, w) in ins]
    out_specs = [_ew_spec(kd, 0, w, tb, hp, order) for (_, kd, w, _) in outs]
    out_shape = [jax.ShapeDtypeStruct((rows, tw), dt) for (tw, _, _, dt) in outs]
    return pl.pallas_call(
        body, name=name, grid=_ew_grid(rows, tb, nh, hp, order), in_specs=in_specs + [ANY] * len(after), out_specs=out_specs,
        out_shape=out_shape, compiler_params=_params(2),
    )(*[a for (a, _, _, _) in ins], *after)


def ew_bwd(name, f, ins, cts, extras, emit, outs, rows, nh=1, tb=ROW_TILE, order="ih", hp=None):
    hp = nh if hp is None else hp
    n_in = len(ins)
    flat_cts = [d for group in cts for d in group]
    n_ct, n_ex = len(flat_cts), len(extras)

    def body(*refs):
        g0, g1 = pl.program_id(0), pl.program_id(1)
        hb = g1 if order == "ih" else g0
        out_refs = refs[n_in + n_ct + n_ex:]
        shared = [None] * len(outs)

        def store(r, v, first, sl=None):
            def put(val, add):
                if sl is None:
                    r[...] = (r[...] + val if add else val).astype(r.dtype)
                else:
                    r[:, sl] = (r[:, sl] + val if add else val).astype(r.dtype)

            if first is None:
                put(v, False)
            else:
                pl.when(first)(lambda: put(v, False))
                pl.when(jnp.logical_not(first))(lambda: put(v, True))

        for hh in range(hp):
            h = hh if hp == nh else hb * hp + hh
            vals = [_ew_load(r, kd, w, hh) for r, (_, kd, _, w) in zip(refs[:n_in], ins)]
            ct_refs = list(zip(refs[n_in:n_in + n_ct], flat_cts))
            ct_vals, pos = [], 0
            for group in cts:
                v = None
                for r, (_, kd, _, w) in ct_refs[pos:pos + len(group)]:
                    t = _ew_load(r, kd, w, hh)
                    v = t if v is None else v + t
                pos += len(group)
                ct_vals.append(v)
            ex_vals = [_ew_load(r, kd, w, hh) for r, (_, kd, _, w) in zip(refs[n_in + n_ct:n_in + n_ct + n_ex], extras)]
            _, vjp = jax.vjp(lambda *a: f(h, *a), *vals)
            res = emit(vjp(tuple(ct_vals)), ex_vals)
            for idx, (r, v, (_, kd, w, _, acc)) in enumerate(zip(out_refs, res, outs)):
                if kd in ("row", "par"):
                    shared[idx] = v if shared[idx] is None else shared[idx] + v
                else:
                    store(r, v, (g1 == 0) if acc == "inner" else None, slice(hh * w, (hh + 1) * w))
        for idx, (r, (_, kd, _, _, acc)) in enumerate(zip(out_refs, outs)):
            if kd in ("row", "par"):
                assert acc == "all" or hp == nh
                store(r, shared[idx], jnp.logical_and(g0 == 0, g1 == 0) if acc == "all" else None)

    operands = list(ins) + flat_cts + list(extras)
    in_specs = [_ew_spec(kd, off, w, tb, hp, order, a.shape) for (a, kd, off, w) in operands]
    out_specs = [_ew_spec(kd, 0, w, tb, hp, order, shp) for (shp, kd, w, _, _) in outs]
    out_shape = [jax.ShapeDtypeStruct(shp, dt) for (shp, _, _, dt, _) in outs]
    return pl.pallas_call(
        body, name=name, grid=_ew_grid(rows, tb, nh, hp, order), in_specs=in_specs, out_specs=out_specs,
        out_shape=out_shape, compiler_params=_params(2),
    )(*[a for (a, _, _, _) in operands])


def f_rms(h, x, g):
    r = lax.rsqrt(jnp.mean(x * x, axis=-1, keepdims=True) + EPS)
    return (x * r * g,)


def _softplus(z):
    return jnp.maximum(z, 0.0) + jnp.log1p(jnp.exp(-jnp.abs(z)))


def f_small(h, sp, p1, p2):
    lane = _iota(sp.shape, 1)
    z = sp + p1
    g = -jnp.exp(p2) * _softplus(z)
    beta = jax.nn.sigmoid(z)
    logf = -_softplus(-z)
    return (jnp.where(lane < NH, g, jnp.where(lane < 2 * NH, beta, jnp.where(lane < 3 * NH, logf, 0.0))),)


def _pick(x, lane_id):
    lane = _iota(x.shape, 1)
    col = jnp.sum(jnp.where(lane == lane_id, x, 0.0), axis=1, keepdims=True)
    return jnp.broadcast_to(col, x.shape)


def f_bcast(h, so, cs):
    return _pick(so, h), _pick(so, h + NH), _pick(cs, h + 2 * NH)


def _shift_down(s):
    def down(x):
        return jnp.where(_iota(x.shape, 0) >= s, pltpu.roll(x, s, 0), 0.0)

    def up(g):
        n = g.shape[0]
        return jnp.where(_iota(g.shape, 0) < n - s, pltpu.roll(g, n - s, 0), 0.0)

    @jax.custom_vjp
    def shift(x):
        return down(x)

    shift.defvjp(lambda x: (down(x), None), lambda _, g: (up(g),))
    return shift


def _silu(x):
    return x * jax.nn.sigmoid(x)


def make_f_conv(mode):
    sh1, sh2, sh3 = _shift_down(1), _shift_down(2), _shift_down(3)

    def f(h, x, w):
        sub = _iota(w.shape, 0)

        def tap(i):
            return jnp.sum(jnp.where(sub == i, w, 0.0), axis=0, keepdims=True)

        y = sh3(x) * tap(0)
        y = y + sh2(x) * tap(1)
        y = y + sh1(x) * tap(2)
        y = y + x * tap(3)
        s = _silu(y)
        if mode == "v":
            return (s,)
        n = s * lax.rsqrt(jnp.sum(s * s, axis=-1, keepdims=True) + EPS)
        if mode == "q":
            n = n * (LANES ** -0.5)
        return (n,)

    return f


def f_post(h, o, z, g):
    r = lax.rsqrt(jnp.mean(o * o, axis=-1, keepdims=True) + EPS)
    return (o * r * g * _silu(z),)


def f_merge(h, ga, gb, ya, yb):
    return (jax.nn.sigmoid(ga) * ya + jax.nn.sigmoid(gb) * yb,)


def f_delta(h, do, o):
    return (jnp.broadcast_to(jnp.sum(do * o, axis=1, keepdims=True), o.shape),)


def cumsum_time(name, x, nseq, seq, reverse):
    nb = seq // LANES

    def body(x_ref, o_ref):
        r, c = _iota((LANES, LANES), 0), _iota((LANES, LANES), 1)
        tri = jnp.where((r <= c) if reverse else (r >= c), 1.0, 0.0).astype(F32)
        carry = jnp.zeros((1, LANES), F32)
        for b in (range(nb - 1, -1, -1) if reverse else range(nb)):
            blk = x_ref[b * LANES:(b + 1) * LANES, :]
            o_ref[b * LANES:(b + 1) * LANES, :] = _dot_mask(tri, blk, "nn") + carry
            carry = carry + jnp.sum(blk, axis=0, keepdims=True)

    spec = pl.BlockSpec((seq, LANES), lambda s: (s, 0))
    return pl.pallas_call(body, name=name, grid=(nseq,), in_specs=[spec], out_specs=spec,
                          out_shape=jax.ShapeDtypeStruct(x.shape, F32), compiler_params=_params(1))(x)


def transpose_time(name, x, nseq, seq):
    def body(x_ref, o_ref):
        o_ref[...] = x_ref[...].T

    return pl.pallas_call(
        body, name=name, grid=(nseq,), in_specs=[pl.BlockSpec((seq, LANES), lambda s: (s, 0))],
        out_specs=pl.BlockSpec((LANES, seq), lambda s: (s, 0)),
        out_shape=jax.ShapeDtypeStruct((nseq * LANES, seq), F32), compiler_params=_params(1))(x)


def _gdn_masks():
    n = GDN_ROWS
    r, c = _iota((n, n), 0), _iota((n, n), 1)
    shift = GDN_CHUNK.bit_length() - 1
    same = lax.shift_right_logical(r, shift) == lax.shift_right_logical(c, shift)
    return r, c, same


def _each(fn, *lists):
    return [fn(*xs) for xs in zip(*lists)]


def _gdn_decay(gbs):
    r, c, same = _gdn_masks()
    seg_tril = jnp.where(jnp.logical_and(same, r >= c), 1.0, 0.0).astype(F32)
    g_cum = _each(lambda gb: mm_mask(seg_tril, gb), gbs)
    lane0 = _iota(gbs[0].shape, 1) == 0
    g_col = _each(lambda g: jnp.sum(jnp.where(lane0, g, 0.0), axis=1, keepdims=True), g_cum)
    g_row = _each(lambda g: jnp.sum(jnp.where(r == c, jnp.broadcast_to(g, (GDN_ROWS, GDN_ROWS)), 0.0), axis=0, keepdims=True), g_col)
    return g_cum, _each(lambda a, b: a - b, g_col, g_row)


def gdn_f1(*args):
    qs, ks, gbs, bbs = (list(args[i::4]) for i in range(4))
    r, c, same = _gdn_masks()
    strict = jnp.logical_and(same, r > c)
    _, diff = _gdn_decay(gbs)
    lane0 = _iota(bbs[0].shape, 1) == 0
    beta_col = _each(lambda bb: jnp.sum(jnp.where(lane0, bb, 0.0), axis=1, keepdims=True), bbs)
    kk = _each(lambda k: _dot(k, k, "nt", LO), ks)
    return tuple(_each(lambda b, x, d: jnp.where(strict, b * x * jnp.exp(jnp.where(strict, d, 0.0)), 0.0), beta_col, kk, diff))


def gdn_f2(*args):
    ts, qs, ks, vs, gbs, bbs = (list(args[i::6]) for i in range(6))
    r, c, same = _gdn_masks()
    incl = jnp.logical_and(same, r >= c)
    g_cum, diff = _gdn_decay(gbs)
    decay = _each(lambda d: jnp.where(incl, jnp.exp(jnp.where(incl, d, 0.0)), 0.0), diff)
    e_g = _each(jnp.exp, g_cum)
    v_beta = _each(lambda v, bb: v * bb, vs, bbs)
    k_beta = _each(lambda k, bb, e: k * bb * e, ks, bbs, e_g)
    value = _each(lambda t, x: x + _dot(t, x, "nn", LO), ts, v_beta)
    k_cum = _each(lambda t, x: x + _dot(t, x, "nn", LO), ts, k_beta)
    attn = _each(lambda q, k, d: _dot(q, k, "nt", LO) * d, qs, ks, decay)
    ones = jnp.where(same, 1.0, 0.0).astype(F32)
    g_last = _each(lambda gb: mm_mask(ones, gb), gbs)
    q_dec = _each(lambda q, e: q * e, qs, e_g)
    k_dec = _each(lambda k, gl, g: k * jnp.exp(gl - g), ks, g_last, g_cum)
    return tuple(x for head in zip(value, k_cum, attn, q_dec, k_dec) for x in head)


def tri_inverse(mats):
    n = GDN_ROWS
    r, c = _iota((n, n), 0), _iota((n, n), 1)
    shift = GDN_BASE.bit_length() - 1
    blk = lax.shift_right_logical(r, shift) == lax.shift_right_logical(c, shift)
    each = lambda fn, *lists: [fn(*xs) for xs in zip(*lists)]
    mm = lambda x, y: _dot(x, y, "nn", LO)
    d = each(lambda a: jnp.where(blk, a, 0.0), mats)
    lo = each(lambda a, dd: a - dd, mats, d)
    p = each(lambda dd: -dd, d)
    c_d = p
    for _ in range(shift - 1):
        p = each(mm, p, p)
        c_d = each(lambda cd, pp, prod: cd + pp + prod, c_d, p, each(mm, c_d, p))
    assert GDN_CHUNK // GDN_BASE == 4
    nmat = each(lambda l, prod: l + prod, lo, each(mm, c_d, lo))
    n2 = each(mm, nmat, nmat)
    c_n = each(lambda nn2, nm, prod: (nn2 - nm) - prod, n2, nmat, each(mm, nmat, n2))
    return each(lambda cn, cd, prod: cn + cd + prod, c_n, c_d, each(mm, c_n, c_d))


GDN_AHP = 4


def _gdn_a_specs():
    blk = pl.BlockSpec((GDN_ROWS, GDN_AHP * LANES), lambda i, h: (i, h))
    sq = pl.BlockSpec((GDN_ROWS, GDN_AHP * GDN_ROWS), lambda i, h: (i, h))
    return blk, sq


def _head(ref, hh):
    width = ref.shape[1] // GDN_AHP
    return ref.at[:, hh * width:(hh + 1) * width]


def gdn_a_fwd(q, k, v, gb, bb, rows):
    blk, sq = _gdn_a_specs()

    def body(q_ref, k_ref, v_ref, gb_ref, bb_ref, val_ref, kc_ref, at_ref, qd_ref, kd_ref, t_ref):
        heads = [[_head(r, hh)[...] for r in (q_ref, k_ref, v_ref, gb_ref, bb_ref)] for hh in range(GDN_AHP)]
        t_corr = tri_inverse(list(gdn_f1(*[x for qv, kv, vv, gv, bv in heads for x in (qv, kv, gv, bv)])))
        res = gdn_f2(*[x for t, head in zip(t_corr, heads) for x in (t, *head)])
        for hh in range(GDN_AHP):
            for r, x in zip((val_ref, kc_ref, at_ref, qd_ref, kd_ref, t_ref), (*res[5 * hh:5 * hh + 5], t_corr[hh])):
                _head(r, hh)[...] = x.astype(r.dtype)

    wide = lambda dt: jax.ShapeDtypeStruct((rows, NH * LANES), dt)
    square = jax.ShapeDtypeStruct((rows, NH * GDN_ROWS), BF16)
    return pl.pallas_call(
        body, name="gdn_a_fwd", grid=(rows // GDN_ROWS, NH // GDN_AHP), in_specs=[blk] * 5,
        out_specs=[blk, blk, sq, blk, blk, sq], out_shape=[wide(F32), wide(BF16), square, wide(BF16), wide(BF16), square],
        compiler_params=_params(2))(q, k, v, gb, bb)


def gdn_a_bwd(q, k, v, gb, bb, t_inv, dval, dkc, dat, dqd, dkd, dgb_b, rows):
    blk, sq = _gdn_a_specs()

    def body(q_ref, k_ref, v_ref, gb_ref, bb_ref, t_ref, dval_ref, dkc_ref, dat_ref, dqd_ref, dkd_ref, dgbb_ref,
             dq_ref, dk_ref, dv_ref, dgb_ref, dbb_ref):
        hs = range(GDN_AHP)
        heads = [[_head(r, hh)[...] for r in (q_ref, k_ref, v_ref, gb_ref, bb_ref)] for hh in hs]
        tvs = [_head(t_ref, hh)[...].astype(F32) for hh in hs]
        _, vjp1 = jax.vjp(gdn_f1, *[x for qv, kv, vv, gv, bv in heads for x in (qv, kv, gv, bv)])
        _, vjp2 = jax.vjp(gdn_f2, *[x for t, head in zip(tvs, heads) for x in (t, *head)])
        g2 = vjp2(tuple(_head(r, hh)[...] for hh in hs for r in (dval_ref, dkc_ref, dat_ref, dqd_ref, dkd_ref)))
        dts = [g2[6 * hh] for hh in hs]
        left = _each(lambda dt, tv: dt + _dot(tv, dt, "tn", LO), dts, tvs)
        g1 = vjp1(tuple(_each(lambda lf, tv: -(lf + _dot(lf, tv, "nt", LO)), left, tvs)))
        for hh in hs:
            dq1, dk1, dgb1, dbb1 = g1[4 * hh:4 * hh + 4]
            _, dq2, dk2, dv2, dgb2, dbb2 = g2[6 * hh:6 * hh + 6]
            _head(dq_ref, hh)[...] = dq1 + dq2
            _head(dk_ref, hh)[...] = dk1 + dk2
            _head(dv_ref, hh)[...] = dv2
            _head(dgb_ref, hh)[...] = dgb1 + dgb2 + _head(dgbb_ref, hh)[...]
            _head(dbb_ref, hh)[...] = dbb1 + dbb2

    wide = jax.ShapeDtypeStruct((rows, NH * LANES), F32)
    return pl.pallas_call(
        body, name="gdn_a_bwd", grid=(rows // GDN_ROWS, NH // GDN_AHP),
        in_specs=[blk] * 5 + [sq, blk, blk, sq, blk, blk, blk], out_specs=[blk] * 5, out_shape=[wide] * 5,
        compiler_params=_params(2))(q, k, v, gb, bb, t_inv, dval, dkc, dat, dqd, dkd, dgb_b)


N_CH = GDN_ROWS // GDN_CHUNK


GDN_HP = 8


def gdn_fb(*args):
    per_head = 6 * N_CH
    states = list(args[GDN_HP * per_head:])
    outs = [[None] * N_CH for _ in range(GDN_HP)]
    zero = jnp.zeros((GDN_CHUNK, LANES), F32)
    for c in range(N_CH):
        for hh in range(GDN_HP):
            val, kc, at, qd, kd, gb = (args[hh * per_head + i * N_CH + c] for i in range(6))
            s = states[hh]
            v_new = val - _dot(kc, s, "nn", LO)
            v_pad = jnp.concatenate([zero] * c + [v_new] + [zero] * (N_CH - 1 - c), axis=0)
            outs[hh][c] = _dot(qd, s, "nn", LO) + _dot(at, v_pad, "nn", LO)
            dec = jnp.exp(jnp.sum(gb, axis=0, keepdims=True))
            states[hh] = s * dec + _dot(kd, v_new, "tn", LO)
    return (*[o for head in outs for o in head], *states)


def _gdn_piece(ref, hh, c):
    width = ref.shape[1] // GDN_HP
    return ref.at[c * GDN_CHUNK:(c + 1) * GDN_CHUNK, hh * width:(hh + 1) * width]


def _gdn_pieces(refs, hh):
    return [_gdn_piece(r, hh, c)[...].astype(F32) for r in refs for c in range(N_CH)]


def _gdn_b_specs(nb, rev):
    def blk_row(s, j):
        return s * nb + (nb - 1 - j if rev else j)

    blk = pl.BlockSpec((GDN_ROWS, GDN_HP * LANES), lambda s, hb, j: (blk_row(s, j), hb))
    sq = pl.BlockSpec((GDN_ROWS, GDN_HP * GDN_ROWS), lambda s, hb, j: (blk_row(s, j), hb))
    snap = pl.BlockSpec((GDN_HP * LANES, LANES), lambda s, hb, j: (blk_row(s, j) * (NH // GDN_HP) + hb, 0))
    return blk, sq, snap


def gdn_b_fwd(val, kc, at, qd, kd, gb, nseq, seq):
    nb = seq // GDN_ROWS
    rows = nseq * seq
    blk, sq, snap = _gdn_b_specs(nb, False)

    def body(val_ref, kc_ref, at_ref, qd_ref, kd_ref, gb_ref, o_ref, snap_ref, s_ref):
        @pl.when(pl.program_id(2) == 0)
        def _():
            s_ref[...] = jnp.zeros_like(s_ref)

        states = [s_ref[hh] for hh in range(GDN_HP)]
        for hh in range(GDN_HP):
            snap_ref[hh * LANES:(hh + 1) * LANES, :] = states[hh]
        pieces = [p for hh in range(GDN_HP) for p in _gdn_pieces([val_ref, kc_ref, at_ref, qd_ref, kd_ref, gb_ref], hh)]
        res = gdn_fb(*pieces, *states)
        for hh in range(GDN_HP):
            for c in range(N_CH):
                _gdn_piece(o_ref, hh, c)[...] = res[hh * N_CH + c]
            s_ref[hh] = res[GDN_HP * N_CH + hh]

    return pl.pallas_call(
        body, name="gdn_b_fwd", grid=(nseq, NH // GDN_HP, nb), in_specs=[blk, blk, sq, blk, blk, blk], out_specs=[blk, snap],
        out_shape=[jax.ShapeDtypeStruct((rows, NH * LANES), F32), jax.ShapeDtypeStruct((nseq * nb * NH * LANES, LANES), F32)],
        scratch_shapes=[pltpu.VMEM((GDN_HP, LANES, LANES), F32)], compiler_params=_params(3))(val, kc, at, qd, kd, gb)


def gdn_b_bwd(val, kc, at, qd, kd, gb, snaps, do, nseq, seq):
    nb = seq // GDN_ROWS
    rows = nseq * seq
    blk, sq, snap = _gdn_b_specs(nb, True)

    def body(val_ref, kc_ref, at_ref, qd_ref, kd_ref, gb_ref, snap_ref, do_ref,
             dval_ref, dkc_ref, dat_ref, dqd_ref, dkd_ref, dgb_ref, ds_ref):
        @pl.when(pl.program_id(2) == 0)
        def _():
            ds_ref[...] = jnp.zeros_like(ds_ref)

        pieces = [p for hh in range(GDN_HP) for p in _gdn_pieces([val_ref, kc_ref, at_ref, qd_ref, kd_ref, gb_ref], hh)]
        states = [snap_ref[hh * LANES:(hh + 1) * LANES, :] for hh in range(GDN_HP)]
        _, vjp = jax.vjp(gdn_fb, *pieces, *states)
        cts = [p for hh in range(GDN_HP) for p in _gdn_pieces([do_ref], hh)] + [ds_ref[hh] for hh in range(GDN_HP)]
        grads = vjp(tuple(cts))
        for hh in range(GDN_HP):
            for i, r in enumerate([dval_ref, dkc_ref, dat_ref, dqd_ref, dkd_ref, dgb_ref]):
                for c in range(N_CH):
                    _gdn_piece(r, hh, c)[...] = grads[hh * 6 * N_CH + i * N_CH + c]
            ds_ref[hh] = grads[GDN_HP * 6 * N_CH + hh]

    wide = jax.ShapeDtypeStruct((rows, NH * LANES), F32)
    square = jax.ShapeDtypeStruct((rows, NH * GDN_ROWS), F32)
    return pl.pallas_call(
        body, name="gdn_b_bwd", grid=(nseq, NH // GDN_HP, nb), in_specs=[blk, blk, sq, blk, blk, blk, snap, blk],
        out_specs=[blk, blk, sq, blk, blk, blk], out_shape=[wide, wide, square, wide, wide, wide],
        scratch_shapes=[pltpu.VMEM((GDN_HP, LANES, LANES), F32)], compiler_params=_params(3))(val, kc, at, qd, kd, gb, snaps, do)


FOX_Q, FOX_K, FOX_V = 4 * NH, 5 * NH, 6 * NH
FOX_SCALE = LANES ** -0.5


def _head_row(ct_ref, h, off, width):
    blk = ct_ref[:, pl.ds(off, width)]
    return jnp.sum(jnp.where(_iota(blk.shape, 0) == h, blk, 0.0), axis=0, keepdims=True)


def _col(x):
    return jnp.max(x, axis=1, keepdims=True)


def _row(x):
    return jnp.max(x.T, axis=0, keepdims=True)


def _causal(shape, q_dim):
    return _iota(shape, q_dim) >= _iota(shape, 1 - q_dim)


FOX_HP = 2


def _fox_specs(seq, tile, n_tiles):
    tblk = pl.BlockSpec((tile, FOX_HP * LANES), lambda s, h, i: (s * n_tiles + i, h))
    vtblk = pl.BlockSpec((tile, FOX_HP * LANES), lambda s, h, i: (s * n_tiles + i, h + FOX_V // FOX_HP))
    full = pl.BlockSpec((seq, FOX_HP * LANES), lambda s, h, i: (s, h))
    vfull = pl.BlockSpec((seq, FOX_HP * LANES), lambda s, h, i: (s, h + FOX_V // FOX_HP))
    ctb = pl.BlockSpec((NH, seq), lambda s, h, i: (s * (LANES // NH) + 2, 0))
    return tblk, vtblk, full, vfull, ctb


def _lanes_of(hh):
    return slice(hh * LANES, (hh + 1) * LANES)


def fox_fwd(qn, kn, proj, ct, nseq, seq):
    tq = tk = min(ATT_TILE, seq)
    nq = seq // tq
    rows = nseq * seq
    qblk, _, full, vfull, ctb = _fox_specs(seq, tq, nq)
    hs = range(FOX_HP)

    def body(q_ref, k_ref, v_ref, ct_ref, o_ref, o16_ref, lse_ref):
        hb, i = pl.program_id(1), pl.program_id(2)
        q = [q_ref[:, _lanes_of(hh)] for hh in hs]

        def step(j, carry, diag):
            m, l, acc = (list(carry[t::3]) for t in range(3))
            off = pl.multiple_of(j * tk, tk)
            k = [k_ref[pl.ds(off, tk), _lanes_of(hh)] for hh in hs]
            v = [v_ref[pl.ds(off, tk), _lanes_of(hh)].astype(BF16) for hh in hs]
            ck = [_head_row(ct_ref, hb * FOX_HP + hh, off, tk) for hh in hs]
            s = _each(lambda qq, kk, cc: _dot(qq, kk, "nt") * FOX_SCALE - cc, q, k, ck)
            if diag:
                s = _each(lambda x: jnp.where(_causal(x.shape, 0), x, NEG), s)
            m_new = _each(lambda mm, x: jnp.maximum(mm, jnp.max(x, axis=1, keepdims=True)), m, s)
            p = _each(lambda x, mm: jnp.exp(x - mm), s, m_new)
            alpha = _each(lambda mo, mn: jnp.exp(mo - mn), m, m_new)
            l = _each(lambda a, ll, pp: a * ll + jnp.sum(pp, axis=1, keepdims=True), alpha, l, p)
            acc = _each(lambda a, ac, pp, vv: a * ac + _dot(pp.astype(BF16), vv, "nn"), alpha, acc, p, v)
            return tuple(x for head in zip(m_new, l, acc) for x in head)

        init = (jnp.full((tq, 1), NEG, F32), jnp.zeros((tq, 1), F32), jnp.zeros((tq, LANES), F32)) * FOX_HP
        res = step(i, lax.fori_loop(0, i, lambda j, c: step(j, c, False), init), True)
        for hh in hs:
            m, l, acc = res[3 * hh:3 * hh + 3]
            o = acc / l
            o_ref[:, _lanes_of(hh)] = o
            o16_ref[:, _lanes_of(hh)] = o.astype(BF16)
            lse_ref[:, _lanes_of(hh)] = jnp.broadcast_to(m + jnp.log(l), (tq, LANES))

    wide = (rows, NH * LANES)
    return pl.pallas_call(
        body, name="fox_fwd", grid=(nseq, NH // FOX_HP, nq), in_specs=[qblk, full, vfull, ctb], out_specs=[qblk] * 3,
        out_shape=[jax.ShapeDtypeStruct(wide, F32), jax.ShapeDtypeStruct(wide, BF16), jax.ShapeDtypeStruct(wide, F32)],
        compiler_params=_params(3))(qn, kn, proj, ct)


def fox_dq(qn, kn, proj, ct, do, lse, delta, nseq, seq):
    tq = tk = min(ATT_TILE, seq)
    nq = seq // tq
    rows = nseq * seq
    qblk, _, full, vfull, ctb = _fox_specs(seq, tq, nq)
    hs = range(FOX_HP)

    def body(q_ref, k_ref, v_ref, ct_ref, do_ref, lse_ref, dl_ref, dq_ref, dc_ref):
        hb, i = pl.program_id(1), pl.program_id(2)
        q = [q_ref[:, _lanes_of(hh)] for hh in hs]
        lse = [_col(lse_ref[:, _lanes_of(hh)]) for hh in hs]
        delta = [_col(dl_ref[:, _lanes_of(hh)]) for hh in hs]
        do16 = [do_ref[:, _lanes_of(hh)].astype(BF16) for hh in hs]

        def step(j, carry, diag):
            dq, dc = (list(carry[t::2]) for t in range(2))
            off = pl.multiple_of(j * tk, tk)
            k = [k_ref[pl.ds(off, tk), _lanes_of(hh)] for hh in hs]
            v = [v_ref[pl.ds(off, tk), _lanes_of(hh)].astype(BF16) for hh in hs]
            ck = [_head_row(ct_ref, hb * FOX_HP + hh, off, tk) for hh in hs]
            p = _each(lambda qq, kk, cc, ll: jnp.exp(_dot(qq, kk, "nt") * FOX_SCALE - cc - ll), q, k, ck, lse)
            if diag:
                p = _each(lambda x: jnp.where(_causal(x.shape, 0), x, 0.0), p)
            dp = _each(lambda d, vv: _dot(d, vv, "nt"), do16, v)
            ds = _each(lambda pp, d, dl: pp * (d - dl), p, dp, delta)
            dq = _each(lambda a, x, kk: a + _dot(x.astype(BF16), kk, "nn"), dq, ds, k)
            dc = _each(lambda a, x: a + jnp.sum(x, axis=1, keepdims=True), dc, ds)
            return tuple(x for head in zip(dq, dc) for x in head)

        init = (jnp.zeros((tq, LANES), F32), jnp.zeros((tq, 1), F32)) * FOX_HP
        res = step(i, lax.fori_loop(0, i, lambda j, c: step(j, c, False), init), True)
        for hh in hs:
            dq_ref[:, _lanes_of(hh)] = res[2 * hh] * FOX_SCALE
            dc_ref[:, _lanes_of(hh)] = jnp.where(_iota((tq, LANES), 1) == 0, res[2 * hh + 1], 0.0)

    wide = jax.ShapeDtypeStruct((rows, NH * LANES), F32)
    return pl.pallas_call(
        body, name="fox_dq", grid=(nseq, NH // FOX_HP, nq), in_specs=[qblk, full, vfull, ctb, qblk, qblk, qblk],
        out_specs=[qblk, qblk], out_shape=[wide, wide], compiler_params=_params(3))(qn, kn, proj, ct, do, lse, delta)


def fox_dkv(qn, kn, proj, cb, do, lse, delta, nseq, seq):
    tq = tk = min(ATT_TILE, seq)
    nq = seq // tq
    rows = nseq * seq
    kblk, vblk, full, _, _ = _fox_specs(seq, tk, nq)
    hs = range(FOX_HP)

    def body(q_ref, k_ref, v_ref, cb_ref, do_ref, lse_ref, dl_ref, dk_ref, dv_ref, dc_ref):
        j = pl.program_id(2)
        k = [k_ref[:, _lanes_of(hh)] for hh in hs]
        v16 = [v_ref[:, _lanes_of(hh)].astype(BF16) for hh in hs]
        ck = [_col(cb_ref[:, _lanes_of(hh)]) for hh in hs]

        def step(i, carry, diag):
            dk, dv, dc = (list(carry[t::3]) for t in range(3))
            off = pl.multiple_of(i * tq, tq)
            q = [q_ref[pl.ds(off, tq), _lanes_of(hh)] for hh in hs]
            do16 = [do_ref[pl.ds(off, tq), _lanes_of(hh)].astype(BF16) for hh in hs]
            lse = [_row(lse_ref[pl.ds(off, tq), _lanes_of(hh)]) for hh in hs]
            delta = [_row(dl_ref[pl.ds(off, tq), _lanes_of(hh)]) for hh in hs]
            p = _each(lambda kk, qq, cc, ll: jnp.exp(_dot(kk, qq, "nt") * FOX_SCALE - cc - ll), k, q, ck, lse)
            if diag:
                p = _each(lambda x: jnp.where(_causal(x.shape, 1), x, 0.0), p)
            dv = _each(lambda a, pp, d: a + _dot(pp.astype(BF16), d, "nn"), dv, p, do16)
            ds = _each(lambda pp, vv, d, dl: pp * (_dot(vv, d, "nt") - dl), p, v16, do16, delta)
            dk = _each(lambda a, x, qq: a + _dot(x.astype(BF16), qq, "nn"), dk, ds, q)
            dc = _each(lambda a, x: a + jnp.sum(x, axis=1, keepdims=True), dc, ds)
            return tuple(x for head in zip(dk, dv, dc) for x in head)

        zero = jnp.zeros((tk, LANES), F32)
        carry = step(j, (zero, zero, jnp.zeros((tk, 1), F32)) * FOX_HP, True)
        res = lax.fori_loop(j + 1, nq, lambda i, c: step(i, c, False), carry)
        for hh in hs:
            dk, dv, dc = res[3 * hh:3 * hh + 3]
            dk_ref[:, _lanes_of(hh)] = dk * FOX_SCALE
            dv_ref[:, _lanes_of(hh)] = dv.astype(BF16)
            dc_ref[:, _lanes_of(hh)] = jnp.where(_iota((tk, LANES), 1) == 0, -dc, 0.0)

    wide = (rows, NH * LANES)
    return pl.pallas_call(
        body, name="fox_dkv", grid=(nseq, NH // FOX_HP, nq), in_specs=[full, kblk, vblk, kblk, full, full, full],
        out_specs=[kblk, kblk, kblk],
        out_shape=[jax.ShapeDtypeStruct(wide, F32), jax.ShapeDtypeStruct(wide, BF16), jax.ShapeDtypeStruct(wide, F32)],
        compiler_params=_params(3))(qn, kn, proj, cb, do, lse, delta)


def loss_head(out, tgt, rows, width):
    tb = ROW_TILE
    blk = pl.BlockSpec((tb, width), lambda i: (i, 0))
    accb = pl.BlockSpec((8, LANES), lambda i: (0, 0))

    def body(o_ref, t_ref, d32_ref, d16_ref, acc_ref):
        d = o_ref[...] - t_ref[...]
        row_loss = 0.5 * jnp.mean(d * d, axis=1, keepdims=True)
        g = d * (1.0 / width)
        d32_ref[...] = g
        d16_ref[...] = g.astype(BF16)
        part = jnp.where(_iota((tb, LANES), 1) == 0, row_loss, 0.0).reshape(tb // 8, 8, LANES).sum(axis=0)

        @pl.when(pl.program_id(0) == 0)
        def _():
            acc_ref[...] = part

        @pl.when(pl.program_id(0) != 0)
        def _():
            acc_ref[...] += part

    return pl.pallas_call(
        body, name="loss_head", grid=(rows // tb,), in_specs=[blk, blk], out_specs=[blk, blk, accb],
        out_shape=[jax.ShapeDtypeStruct((rows, width), F32), jax.ShapeDtypeStruct((rows, width), BF16),
                   jax.ShapeDtypeStruct((8, LANES), F32)], compiler_params=_params(1))(out, tgt)


def _adamw_update(w, g, m, v):
    m_new = ADAM_B1 * m + (1.0 - ADAM_B1) * g
    v_new = ADAM_B2 * v + (1.0 - ADAM_B2) * (g * g)
    m_hat = m_new / (1.0 - ADAM_B1 ** ADAM_STEP)
    v_hat = v_new / (1.0 - ADAM_B2 ** ADAM_STEP)
    return -ADAM_LR * (m_hat / (jnp.sqrt(v_hat) + ADAM_EPS) + ADAM_WD * w), m_new, v_new


def adamw(name, w, g, m, v):
    rows, cols = w.shape
    tb = min(rows, 128)
    assert rows % tb == 0
    blk = pl.BlockSpec((tb, cols), lambda i: (i, 0))

    def body(w_ref, g_ref, m_ref, v_ref, d_ref, mo_ref, vo_ref):
        d_ref[...], mo_ref[...], vo_ref[...] = _adamw_update(w_ref[...], g_ref[...], m_ref[...], v_ref[...])

    shp = jax.ShapeDtypeStruct(w.shape, F32)
    return pl.pallas_call(body, name=name, grid=(rows // tb,), in_specs=[blk] * 4, out_specs=[blk] * 3,
                          out_shape=[shp] * 3, compiler_params=_params(1))(w, g, m, v)


SPLIT_TILE = 128


def _tiled(shape2d, ax, n_lead, index):
    blk = (SPLIT_TILE, shape2d[1]) if ax == 0 else (shape2d[0], SPLIT_TILE)

    def index_map(*args):
        *lead, t = index(*args)
        return (*lead, t, 0) if ax == 0 else (*lead, 0, t)

    return pl.BlockSpec((None,) * n_lead + blk, index_map)


def adamw_halves(name, w, mine, other, m, v, c, ax):
    steps = w.shape[ax] // 2 // SPLIT_TILE
    assert w.shape[ax] == 2 * steps * SPLIT_TILE

    def body(c_ref, w_ref, mine_ref, other_ref, m_ref, v_ref, g_ref, d_ref, mo_ref, vo_ref):
        g = jnp.where(pl.program_id(0) // steps == c_ref[0], mine_ref[...], other_ref[...])
        g_ref[...] = g
        d_ref[...], mo_ref[...], vo_ref[...] = _adamw_update(w_ref[...], g, m_ref[...], v_ref[...])

    blk = _tiled(w.shape, ax, 0, lambda i, c_ref: (i,))
    hblk = _tiled(mine.shape, ax, 0, lambda i, c_ref: (i % steps,))
    grid_spec = pltpu.PrefetchScalarGridSpec(num_scalar_prefetch=1, grid=(2 * steps,),
                                             in_specs=[blk, hblk, hblk, blk, blk], out_specs=[blk] * 4)
    shp = jax.ShapeDtypeStruct(w.shape, F32)
    return pl.pallas_call(body, name=name, grid_spec=grid_spec, out_shape=[shp] * 4,
                          compiler_params=_params(1))(c, w, mine, other, m, v)


def add_chips(name, slots, parts, chip, axes):
    outs = []
    for idx, (x, own, ax) in enumerate(zip(slots, parts, axes)):
        n, shape2d = x.shape[0], x.shape[1:]
        steps = shape2d[ax] // SPLIT_TILE
        assert shape2d[ax] == steps * SPLIT_TILE

        def body(me_ref, *refs, n=n):
            o_ref = refs[n + 1]
            acc = None
            for t in range(n):
                term = jnp.where(me_ref[0] == t, refs[n][...], refs[t][...]).astype(F32)
                acc = term if acc is None else acc + term
            o_ref[...] = acc

        def filled(t, n=n):
            return lambda i, me_ref: (jnp.where(me_ref[0] == t, (t + 1) % n, t), i)

        grid_spec = pltpu.PrefetchScalarGridSpec(
            num_scalar_prefetch=1, grid=(steps,),
            in_specs=[_tiled(shape2d, ax, 1, filled(t)) for t in range(n)]
            + [_tiled(shape2d, ax, 1, lambda i, me_ref: (me_ref[0], i))],
            out_specs=_tiled(shape2d, ax, 0, lambda i, me_ref: (i,)))
        outs.append(pl.pallas_call(
            body, name=f"{name}_{idx}", grid_spec=grid_spec, out_shape=jax.ShapeDtypeStruct(shape2d, F32),
            compiler_params=_params(1))(chip, *([x] * n), own))
    return outs


def add_pair(name, gs, rs, c, axes):
    outs = []
    for idx, (g, r, ax) in enumerate(zip(gs, rs, axes)):
        nb = r.shape[0]
        steps = r.shape[1 + ax] // SPLIT_TILE
        assert r.shape[1 + ax] == steps * SPLIT_TILE

        def body(c_ref, g_ref, r_ref, o_ref):
            o_ref[...] = (g_ref[...] + r_ref[...]).astype(BF16)

        grid_spec = pltpu.PrefetchScalarGridSpec(
            num_scalar_prefetch=1, grid=(nb, steps),
            in_specs=[_tiled(g.shape[1:], ax, 1, lambda b, i, c_ref: (b, c_ref[0] * steps + i)),
                      _tiled(r.shape[1:], ax, 1, lambda b, i, c_ref: (b, i))],
            out_specs=_tiled(r.shape[1:], ax, 1, lambda b, i, c_ref: (b, i)))
        outs.append(pl.pallas_call(
            body, name=f"{name}_{idx}", grid_spec=grid_spec, out_shape=jax.ShapeDtypeStruct(r.shape, BF16),
            compiler_params=_params(2))(c, g, r))
    return outs


def _place():
    x, y, c = lax.axis_index("x"), lax.axis_index("y"), lax.axis_index("c")
    return x, y, c, [(1 - x, y), (x, 1 - y), (1 - x, 1 - y)]


def _remote(src, dst, send_sem, recv_sem, dev):
    return pltpu.make_async_remote_copy(src_ref=src, dst_ref=dst, send_sem=send_sem, recv_sem=recv_sem,
                                        device_id=dev, device_id_type=MESH)


def _half(ref, lead, ax, which):
    size = ref.shape[len(lead) + ax] // 2
    part = pl.ds(which * size, size)
    return ref.at[(*lead, part, slice(None)) if ax == 0 else (*lead, slice(None), part)]


def gather_weights(shards, axes):
    n = len(shards)

    def body(*refs):
        ins, outs = refs[:n], refs[n:2 * n]
        ici_s, ici_r, d2d_s, d2d_r = refs[2 * n:]
        x, y, c, chips = _place()
        me = 2 * x + y
        sends, passes = [], []
        for w in range(n):
            cp = _remote(ins[w], outs[w].at[me], d2d_s.at[3 * n + w], d2d_r.at[3 * n + w], (x, y, 1 - c))
            cp.start()
            passes.append(cp)
        for w in range(n):
            for j, (ox, oy) in enumerate(chips):
                cp = _remote(_half(ins[w], (), axes[w], c), _half(outs[w], (me,), axes[w], c),
                             ici_s.at[3 * w + j], ici_r.at[3 * w + j], (ox, oy, c))
                cp.start()
                sends.append(cp)
        for w in range(n):
            for j, (ox, oy) in enumerate(chips):
                landed = _half(outs[w], (2 * ox + oy,), axes[w], c)
                _remote(landed, landed, ici_s.at[3 * w + j], ici_r.at[3 * w + j], (ox, oy, c)).wait_recv()
                cp = _remote(landed, landed, d2d_s.at[3 * w + j], d2d_r.at[3 * w + j], (x, y, 1 - c))
                cp.start()
                passes.append(cp)
        for w in range(n):
            for j, (ox, oy) in enumerate(chips):
                other = _half(outs[w], (2 * ox + oy,), axes[w], 1 - c)
                _remote(other, other, d2d_s.at[3 * w + j], d2d_r.at[3 * w + j], (x, y, 1 - c)).wait_recv()
            own = outs[w].at[me]
            _remote(own, own, d2d_s.at[3 * n + w], d2d_r.at[3 * n + w], (x, y, 1 - c)).wait_recv()
        for cp in sends + passes:
            cp.wait_send()

    return pl.pallas_call(
        body, name="gather_weights", in_specs=[ANY] * n, out_specs=[ANY] * n,
        out_shape=[jax.ShapeDtypeStruct((4,) + s.shape, s.dtype) for s in shards],
        scratch_shapes=[pltpu.SemaphoreType.DMA((3 * n,))] * 2 + [pltpu.SemaphoreType.DMA((4 * n,))] * 2,
    )(*shards)


HBM = pl.BlockSpec(memory_space=pltpu.HBM)
SEM = pl.BlockSpec(memory_space=pltpu.SEMAPHORE)
DATAFLOW = pltpu.SideEffectType.DATAFLOW_SIDE_EFFECTING


def _hbm(a):
    return pltpu.with_memory_space_constraint(a, pltpu.HBM)


class SplitExchange:
    def __init__(self, name, srcs, zone_shapes, n_sems, plan):
        self.name, self.n, self.n_sems, self.plan = name, len(srcs), n_sems, plan
        self.srcs = [_hbm(s) for s in srcs]
        self.zones = [_hbm(lax.empty(shape, s.dtype)) for shape, s in zip(zone_shapes, srcs)]

    def start(self, after):
        n, n_after = self.n, len(after)

        def body(*refs):
            ins, lands = refs[:n], refs[n:2 * n]
            send, recv, token = refs[2 * n + n_after], refs[2 * n + n_after + 1], refs[-1]
            for src, dst, si, ri, dev in self.plan(ins, lands)[0]:
                _remote(src, dst, send.at[si], recv.at[ri], dev).start()
            token[...] = jnp.zeros_like(token)

        res = pl.pallas_call(
            body, name=f"{self.name}_start", in_specs=[HBM] * (2 * n) + [ANY] * n_after,
            out_specs=[SEM, SEM] + [HBM] * (2 * n) + [pl.BlockSpec(memory_space=pltpu.VMEM)],
            out_shape=[pltpu.SemaphoreType.DMA((self.n_sems,)), pltpu.SemaphoreType.DMA((self.n_sems,))]
            + [pltpu.HBM(a.shape, a.dtype) for a in self.srcs + self.zones] + [jax.ShapeDtypeStruct((8, LANES), F32)],
            input_output_aliases={i: 2 + i for i in range(2 * n)},
            compiler_params=pltpu.CompilerParams(has_side_effects=DATAFLOW),
        )(*self.srcs, *self.zones, *after)
        self.sems, self.srcs, self.zones = res[:2], list(res[2:2 + n]), list(res[2 + n:2 + 2 * n])
        return res[-1]

    def wait(self, after):
        n = self.n

        def body(*refs):
            ins, lands = refs[:n], refs[n:2 * n]
            send, recv = refs[2 * n], refs[2 * n + 1]
            sends, arrivals = self.plan(ins, lands)
            for src, _, si, _, dev in sends:
                _remote(src, src, send.at[si], recv.at[si], dev).wait_send()
            for landed, ri in arrivals:
                _remote(landed, landed, send.at[ri], recv.at[ri], _place()[:3]).wait_recv()

        res = pl.pallas_call(
            body, name=f"{self.name}_wait", in_specs=[HBM] * (2 * n) + [SEM, SEM, ANY], out_specs=[HBM] * (2 * n),
            out_shape=[pltpu.HBM(a.shape, a.dtype) for a in self.srcs + self.zones],
            input_output_aliases={i: i for i in range(2 * n)},
            compiler_params=pltpu.CompilerParams(has_side_effects=DATAFLOW),
        )(*self.srcs, *self.zones, *self.sems, after)
        self.srcs = list(res[:n])
        return list(res[n:])


def split_gather(shards):
    n = len(shards)

    def plan(ins, lands):
        x, y, c, chips = _place()
        me = 2 * x + y
        sends, arrivals = [], []
        for w in range(n):
            for j, (ox, oy) in enumerate(chips):
                for k in range(2):
                    base = 2 * (3 * w + j)
                    sends.append((_half(ins[w], (), 0, c), _half(lands[w], (me,), 0, c), base + k, base + c, (ox, oy, k)))
                    arrivals.append((_half(lands[w], (2 * ox + oy,), 0, k), base + k))
            sends.append((ins[w], lands[w].at[me], 6 * n + w, 6 * n + w, (x, y, 1 - c)))
            arrivals.append((lands[w].at[me], 6 * n + w))
        return sends, arrivals

    return SplitExchange("gather", shards, [(4,) + s.shape for s in shards], 7 * n, plan)


def split_pair_swap(name, grads, axes):
    def plan(ins, lands):
        x, y, c, _ = _place()
        sends = [(_half(ins[w], (slice(None),), axes[w], 1 - c), lands[w], w, w, (x, y, 1 - c)) for w in range(len(ins))]
        return sends, [(lands[w], w) for w in range(len(ins))]

    halved = [tuple(d // 2 if i == 1 + ax else d for i, d in enumerate(g.shape)) for g, ax in zip(grads, axes)]
    return SplitExchange(name, grads, halved, len(grads), plan)


def split_chip_exchange(name, parts):
    def plan(ins, lands):
        x, y, c, chips = _place()
        sends, arrivals = [], []
        for w in range(len(ins)):
            for j, (ox, oy) in enumerate(chips):
                sends.append((ins[w].at[2 * ox + oy], lands[w].at[2 * x + y], 3 * w + j, 3 * w + j, (ox, oy, c)))
                arrivals.append((lands[w].at[2 * ox + oy], 3 * w + j))
        return sends, arrivals

    return SplitExchange(name, parts, [p.shape for p in parts], 3 * len(parts), plan)


def split_pair_send(halves):
    def plan(ins, lands):
        x, y, c, _ = _place()
        return ([(ins[w], lands[w], w, w, (x, y, 1 - c)) for w in range(len(ins))],
                [(lands[w], w) for w in range(len(ins))])

    return SplitExchange("pair_send", halves, [h.shape for h in halves], len(halves), plan)


def pair_send(halves):
    n = len(halves)

    def body(*refs):
        ins, outs = refs[:n], refs[n:2 * n]
        send, recv = refs[2 * n:]
        x, y, c, _ = _place()
        cps = [_remote(ins[w], outs[w], send.at[w], recv.at[w], (x, y, 1 - c)) for w in range(n)]
        for cp in cps:
            cp.start()
        for cp in cps:
            cp.wait_recv()
        for cp in cps:
            cp.wait_send()

    return pl.pallas_call(
        body, name="pair_send", in_specs=[ANY] * n, out_specs=[ANY] * n,
        out_shape=[jax.ShapeDtypeStruct(h.shape, h.dtype) for h in halves],
        scratch_shapes=[pltpu.SemaphoreType.DMA((n,))] * 2,
    )(*halves)


def all_reduce_small(name, vec, after=()):
    rows = vec.shape[0]

    def body(v_ref, *refs):
        o_ref, buf, send, recv = refs[len(after):]
        x, y, c, _ = _place()
        me = 4 * x + 2 * y + c
        buf[me] = v_ref[...]
        cps = []
        for k in range(1, 8):
            kx, ky, kc = (k >> 2) & 1, (k >> 1) & 1, k & 1
            peer = (x if kx == 0 else 1 - x, y if ky == 0 else 1 - y, c if kc == 0 else 1 - c)
            cp = _remote(v_ref, buf.at[me], send.at[k - 1], recv.at[k - 1], peer)
            cp.start()
            cps.append(cp)
        for k in range(1, 8):
            kx, ky, kc = (k >> 2) & 1, (k >> 1) & 1, k & 1
            px, py, pc = (x if kx == 0 else 1 - x, y if ky == 0 else 1 - y, c if kc == 0 else 1 - c)
            slot = buf.at[4 * px + 2 * py + pc]
            _remote(slot, slot, send.at[k - 1], recv.at[k - 1], (px, py, pc)).wait_recv()
        for cp in cps:
            cp.wait_send()
        acc = buf[0]
        for d in range(1, 8):
            acc = acc + buf[d]
        o_ref[...] = acc

    vm = pl.BlockSpec(memory_space=pltpu.VMEM)
    return pl.pallas_call(
        body, name=name, in_specs=[vm] + [ANY] * len(after), out_specs=vm, out_shape=jax.ShapeDtypeStruct(vec.shape, F32),
        scratch_shapes=[pltpu.VMEM((8, rows, LANES), F32), pltpu.SemaphoreType.DMA((7,)), pltpu.SemaphoreType.DMA((7,))],
    )(vec, *after)


class NoExchange:
    def __init__(self, late):
        self.late = late

    def late_weights(self, after):
        return self.late

    def reduce_start(self, grads):
        return jnp.zeros((8, LANES), F32)

    def reduce_exchange(self, after):
        return jnp.zeros((8, LANES), F32)

    def reduce_finish(self, after):
        return jnp.zeros((8, LANES), F32)

    def input_grad_start(self, dw_main, dw_small):
        return jnp.zeros((8, LANES), F32)

    def input_grad_exchange(self, after):
        return jnp.zeros((8, LANES), F32)


def local_step(x2, tgt2, g1, g2, gdn_ng, qn_g, kn_g, p1, p2, conv_w, wt_main, wt_small, hooks, nseq, seq):
    rows, dm = x2.shape
    wide = NH * LANES
    row = lambda a, off=0, w=None: (a, "row", off, a.shape[1] if w is None else w)
    rowh = lambda a, off=0, w=LANES: (a, "rowh", off, w)
    par = lambda a: (a, "par", 0, a.shape[1])
    parh = lambda a, off=0: (a, "parh", off, LANES)
    o_row = lambda w, dt: (w, "row", w, dt)
    o_rowh = lambda dt, tw=wide, w=LANES: (tw, "rowh", w, dt)

    u, = ew_fwd("rms1", f_rms, [row(x2), par(g1)], [o_row(dm, BF16)], rows)
    proj = matmul("mm_in", u, wt_main, "nt", BF16)
    sp = matmul("mm_in_small", u, wt_small, "nt", F32)
    so, = ew_fwd("small", f_small, [row(sp), par(p1), par(p2)], [o_row(LANES, F32)], rows)
    cs = cumsum_time("cumsum", so, nseq, seq, False)
    gb, bb, cb = ew_fwd("bcast", f_bcast, [row(so), row(cs)], [o_rowh(F32)] * 3, rows, NH)
    ct = transpose_time("c_time_major", cs, nseq, seq)
    conv = {}
    for mode, off in (("q", 0), ("k", NH), ("v", 2 * NH)):
        conv[mode], = ew_fwd(f"conv_{mode}", make_f_conv(mode), [rowh(proj, off), parh(conv_w, off)], [o_rowh(F32)],
                             rows, NH, seq, "hi", CONV_HEADS)
    val, kcum, attn, qdec, kdec, t_inv = gdn_a_fwd(conv["q"], conv["k"], conv["v"], gb, bb, rows)
    o_a, snaps = gdn_b_fwd(val, kcum, attn, qdec, kdec, gb, nseq, seq)
    ya_in, = ew_fwd("gdn_post", f_post, [rowh(o_a), rowh(proj, 3 * NH), par(gdn_ng)], [o_rowh(BF16)], rows, NH)
    fqn, = ew_fwd("fox_qn", f_rms, [rowh(proj, FOX_Q), par(qn_g)], [o_rowh(BF16)], rows, NH)
    fkn, = ew_fwd("fox_kn", f_rms, [rowh(proj, FOX_K), par(kn_g)], [o_rowh(BF16)], rows, NH)
    o_b, o_b16, lse = fox_fwd(fqn, fkn, proj, ct, nseq, seq)
    p_a, p_b, w_o, w_u, w_d = hooks.late_weights(o_a)
    y_a = matmul("mm_pa", ya_in, p_a, "nn", F32, tn=1024)
    y_b = matmul("mm_pb", o_b16, p_b, "nn", F32, tn=1024)
    gates = [row(proj, 7, dm), row(proj, 8, dm)]
    merged, = ew_fwd("merge", f_merge, gates + [row(y_a), row(y_b)], [o_row(dm, BF16)], rows)
    hres = matmul("mm_out", merged, w_o, "nn", F32, add=x2, tn=1024)
    hn, = ew_fwd("rms2", f_rms, [row(hres), par(g2)], [o_row(dm, BF16)], rows)
    up_blocks = w_u.shape[0]
    act, relu2 = matmul("mm_up", hn, w_u, "nn", F32, col_blocks=up_blocks, out_dtypes=[F32, BF16],
                        epilogue=lambda r: [r, jnp.maximum(r, 0.0) * jnp.maximum(r, 0.0)])
    out = matmul("mm_down", relu2, w_d, "nn", F32, add=hres, tn=1024)
    dout, dout16, loss_acc = loss_head(out, tgt2, rows, dm)

    d_act = matmul("mm_d_act", dout16, w_d, "nt", BF16, extras=[act], epilogue=lambda r, a: [2.0 * jnp.maximum(a, 0.0) * r])
    dw_d = matmul("mm_dw_down", relu2, dout16, "tn", F32, tn=1024)
    dw_u = matmul("mm_dw_up", hn, d_act, "tn", F32, col_blocks=up_blocks)
    d_hn = matmul("mm_d_hn", d_act, w_u, "nt", F32, col_blocks=up_blocks)
    dh, dh16, dg2 = ew_bwd("rms2_b", f_rms, [row(hres), par(g2)], [(row(d_hn),)], [row(dout)],
                           lambda g, e: [g[0] + e[0], g[0] + e[0], g[1]],
                           [((rows, dm), "row", dm, F32, None), ((rows, dm), "row", dm, BF16, None), ((1, dm), "par", dm, F32, "all")], rows)
    d_merged = matmul("mm_d_merged", dh16, w_o, "nt", F32, tn=1024)
    dw_o = matmul("mm_dw_out", merged, dh16, "tn", F32, tn=1024)
    seg16 = ((rows, dm), "row", dm, BF16, None)
    d_ga16, d_gb16, d_ya16, d_yb16 = ew_bwd("merge_b", f_merge, gates + [row(y_a), row(y_b)], [(row(d_merged),)], [],
                                            lambda g, e: list(g), [seg16] * 4, rows)
    dp_a = matmul("mm_dp_a", ya_in, d_ya16, "tn", F32, tn=1024)
    d_ya_in = matmul("mm_d_ya_in", d_ya16, p_a, "nt", F32, tn=1024)
    dp_b = matmul("mm_dp_b", o_b16, d_yb16, "tn", F32, tn=1024)
    d_ob = matmul("mm_d_ob", d_yb16, p_b, "nt", F32, tn=1024)
    token = hooks.reduce_start(dict(p_a=dp_a, p_b=dp_b, w_o=dw_o, w_u=dw_u, w_d=dw_d))
    gdn_ng_t = gdn_ng + token[0, 0]
    h32 = ((rows, wide), "rowh", LANES, F32, None)
    h16 = ((rows, wide), "rowh", LANES, BF16, None)
    gain = ((1, LANES), "par", LANES, F32, "all")
    d_oa, d_z16, d_gdn_ng = ew_bwd("gdn_post_b", f_post, [rowh(o_a), rowh(proj, 3 * NH), par(gdn_ng_t)], [(rowh(d_ya_in),)], [],
                                   lambda g, e: list(g), [h32, h16, gain], rows, NH)
    dval, dkc, dat, dqd, dkd, dgb_b = gdn_b_bwd(val, kcum, attn, qdec, kdec, gb, snaps, d_oa, nseq, seq)
    d_cq, d_ck, d_cv, d_gb, d_bb = gdn_a_bwd(conv["q"], conv["k"], conv["v"], gb, bb, t_inv, dval, dkc, dat, dqd, dkd, dgb_b, rows)
    token = hooks.reduce_exchange(d_cq)
    conv_w_t = conv_w + token[0, 0]
    d_pre, d_conv = {}, {}
    tap = ((4, wide), "parh", LANES, F32, "inner")
    for mode, off, ctg in (("q", 0, d_cq), ("k", NH, d_ck), ("v", 2 * NH, d_cv)):
        d_pre[mode], d_conv[mode] = ew_bwd(f"conv_{mode}_b", make_f_conv(mode), [rowh(proj, off), parh(conv_w_t, off)],
                                           [(rowh(ctg),)], [], lambda g, e: list(g), [h16, tap], rows, NH, seq, "hi", CONV_HEADS)
    delta, = ew_fwd("fox_delta", f_delta, [rowh(d_ob), rowh(o_b)], [o_rowh(F32)], rows, NH, after=[token])
    d_fqn, d_cq_b = fox_dq(fqn, fkn, proj, ct, d_ob, lse, delta, nseq, seq)
    d_fkn, d_fv16, d_ck_b = fox_dkv(fqn, fkn, proj, cb, d_ob, lse, delta, nseq, seq)
    token = hooks.reduce_finish(d_fkn)
    qn_g_t, kn_g_t = qn_g + token[0, 0], kn_g + token[0, 0]
    d_fq16, d_qn_g = ew_bwd("fox_qn_b", f_rms, [rowh(proj, FOX_Q), par(qn_g_t)], [(rowh(d_fqn),)], [], lambda g, e: list(g),
                            [h16, gain], rows, NH)
    d_fk16, d_kn_g = ew_bwd("fox_kn_b", f_rms, [rowh(proj, FOX_K), par(kn_g_t)], [(rowh(d_fkn),)], [], lambda g, e: list(g),
                            [h16, gain], rows, NH)
    narrow = ((rows, LANES), "row", LANES, F32, None)
    d_so, d_cs = ew_bwd("bcast_b", f_bcast, [row(so), row(cs)], [(rowh(d_gb),), (rowh(d_bb),), (rowh(d_cq_b), rowh(d_ck_b))], [],
                        lambda g, e: list(g), [narrow, narrow], rows, NH)
    d_logf = cumsum_time("cumsum_b", d_cs, nseq, seq, True)
    vec = ((1, LANES), "par", LANES, F32, "all")
    d_sp16, d_p1, d_p2 = ew_bwd("small_b", f_small, [row(sp), par(p1), par(p2)], [(row(d_so), row(d_logf))], [],
                                lambda g, e: list(g), [((rows, LANES), "row", LANES, BF16, None), vec, vec], rows)
    d_proj16 = jnp.concatenate([d_pre["q"], d_pre["k"], d_pre["v"], d_z16, d_fq16, d_fk16, d_fv16, d_ga16, d_gb16], axis=1)
    dw_main = matmul("mm_dw_main", d_proj16, u, "tn", F32)
    dw_small = matmul("mm_dw_small", d_sp16, u, "tn", F32)
    wt_small_t = wt_small + hooks.input_grad_start(dw_main, dw_small)[0, 0].astype(BF16)
    d_u = matmul("mm_d_u_small", d_sp16, wt_small_t, "nn", F32)
    d_u = matmul("mm_d_u_first", d_proj16, wt_main, "nn", F32, add=d_u, k_part=(0, 2))
    d_u = matmul("mm_d_u_second", d_proj16, wt_main, "nn", F32, add=d_u, k_part=(1, 2), after=[hooks.input_grad_exchange(d_u)])
    dx, dg1 = ew_bwd("rms1_b", f_rms, [row(x2), par(g1)], [(row(d_u),)], [row(dh)], lambda g, e: [g[0] + e[0], g[1]],
                     [((rows, dm), "row", dm, F32, None), ((1, dm), "par", dm, F32, "all")], rows)
    d_conv_w = jnp.concatenate([d_conv["q"], d_conv["k"], d_conv["v"]], axis=1)
    return dict(loss_acc=loss_acc, dx=dx, g1=dg1, g2=dg2, gdn_ng=d_gdn_ng, qn=d_qn_g, kn=d_kn_g, p1=d_p1, p2=d_p2,
                conv=d_conv_w, w_main=dw_main, w_small=dw_small, p_a=dp_a, p_b=dp_b, w_o=dw_o, w_u=dw_u, w_d=dw_d)


_W = NH * LANES
_A0, _A1 = 4 * _W, 4 * _W + 2 * NH
_B0, _B1 = _A1 + 3 * _W, _A1 + 3 * _W + NH
N_IN = _B1 + 2 * _W


def _split_w_in(full_t):
    main = jnp.concatenate([full_t[:_A0], full_t[_A1:_B0], full_t[_B1:]], axis=0)
    small = jnp.concatenate([full_t[_A0:_A1], full_t[_B0:_B1], jnp.zeros((LANES - 3 * NH, full_t.shape[1]), full_t.dtype)], axis=0)
    return main, small


def _join_w_in(main, small):
    return jnp.concatenate([main[:_A0], small[:2 * NH], main[_A0:_A0 + 3 * _W], small[2 * NH:3 * NH], main[_A0 + 3 * _W:]], axis=0)


def _lanes(v, at=0):
    return jnp.pad(v.reshape(1, -1), ((0, 0), (at, LANES - at - v.size)))


def kernel(x, norm_mix_g, w_in, gdn_conv_w, gdn_a_log, gdn_dt_bias, gdn_norm_g, fox_q_norm_g, fox_k_norm_g, fox_f_bias, w_proj_gdn, w_proj_fox, w_out, norm_mlp_g, w_up, w_down, loss_target, m_norm_mix_g, m_w_in, m_gdn_conv_w, m_gdn_a_log, m_gdn_dt_bias, m_gdn_norm_g, m_fox_q_norm_g, m_fox_k_norm_g, m_fox_f_bias, m_w_proj_gdn, m_w_proj_fox, m_w_out, m_norm_mlp_g, m_w_up, m_w_down, v_norm_mix_g, v_w_in, v_gdn_conv_w, v_gdn_a_log, v_gdn_dt_bias, v_gdn_norm_g, v_fox_q_norm_g, v_fox_k_norm_g, v_fox_f_bias, v_w_proj_gdn, v_w_proj_fox, v_w_out, v_norm_mlp_g, v_w_up, v_w_down):
    nseq, seq, dm = x.shape
    rows = nseq * seq
    xi, yi, ci = lax.axis_index("x"), lax.axis_index("y"), lax.axis_index("c")
    chip = 2 * xi + yi
    conv_cols = gdn_conv_w.shape[2]

    tr = lambda a: jnp.swapaxes(a[0], 0, 1)
    big = [tr(w_in), w_proj_gdn[0], w_proj_fox[0], w_out[0], w_up[0], w_down[0]]
    axes = [1, 0, 0, 0, 0, 0]
    big16 = [w.astype(BF16) for w in big]
    conv_slot = jnp.zeros((4, 4, conv_cols), F32).at[:, chip].set(jnp.where(ci == 0, gdn_conv_w[0], 0.0))
    conv_full = all_reduce_small("gather_conv", conv_slot.reshape(-1, LANES)).reshape(4, 4 * conv_cols)
    got_in, = gather_weights(big16[:1], axes[:1])
    wt_main, wt_small = _split_w_in(got_in.reshape(-1, dm))
    core, chip_no = ci.reshape(1).astype(jnp.int32), chip.reshape(1).astype(jnp.int32)
    gather = split_gather(big16[1:])
    token = gather.start([got_in, conv_full])

    class Hooks:
        def late_weights(self, after):
            g_pa, g_pb, g_wo, w_u, g_wd = gather.wait(after)
            return (*(g.reshape(-1, dm) for g in (g_pa, g_pb, g_wo)), w_u, g_wd.reshape(-1, dm))

        def reduce_start(self, grads):
            blocks = [grads["p_a"].reshape(4, -1, dm), grads["p_b"].reshape(4, -1, dm), grads["w_o"].reshape(4, -1, dm),
                      grads["w_u"], grads["w_d"].reshape(4, -1, dm)]
            self.swap = split_pair_swap("pair_swap_late", blocks, axes[1:])
            return self.swap.start([])

        def reduce_exchange(self, after):
            swapped = self.swap.wait(after)
            self.exchange = split_chip_exchange("chip_exchange_late", add_pair("add_pair_late", self.swap.srcs, swapped, core, axes[1:]))
            return self.exchange.start([])

        def reduce_finish(self, after):
            slots = self.exchange.wait(after)
            self.send = split_pair_send(add_chips("add_chips_late", slots, self.exchange.srcs, chip_no, axes[1:]))
            return self.send.start([])

        def input_grad_start(self, dw_main, dw_small):
            self.in_swap = split_pair_swap("pair_swap_in", [_join_w_in(dw_main, dw_small).reshape(4, -1, dm)], axes[:1])
            return self.in_swap.start([])

        def input_grad_exchange(self, after):
            swapped = self.in_swap.wait(after)
            self.in_exchange = split_chip_exchange("chip_exchange_in", add_pair("add_pair_in", self.in_swap.srcs, swapped, core, axes[:1]))
            return self.in_exchange.start([])

    hooks = Hooks()
    p1 = _lanes(gdn_dt_bias[0]) + _lanes(fox_f_bias[0], 2 * NH)
    p2 = _lanes(gdn_a_log[0])

    g = local_step(x.reshape(rows, dm), loss_target.reshape(rows, dm), norm_mix_g + token[0, 0], norm_mlp_g, gdn_norm_g,
                   fox_q_norm_g, fox_k_norm_g, p1, p2, conv_full, wt_main, wt_small, hooks, nseq, seq)

    others = hooks.send.wait(g["dx"])
    big_m = [tr(m_w_in), m_w_proj_gdn[0], m_w_proj_fox[0], m_w_out[0], m_w_up[0], m_w_down[0]]
    big_v = [tr(v_w_in), v_w_proj_gdn[0], v_w_proj_fox[0], v_w_out[0], v_w_up[0], v_w_down[0]]
    names = ["w_in", "w_proj_gdn", "w_proj_fox", "w_out", "w_up", "w_down"]
    big_res, big_grad = {}, {}
    for i in range(1, len(names)):
        big_grad[names[i]], *big_res[names[i]] = adamw_halves(f"adamw_{names[i]}", big[i], hooks.send.srcs[i - 1], others[i - 1],
                                                              big_m[i], big_v[i], core, axes[i])
    slots = hooks.in_exchange.wait(big_res[names[-1]][0])
    mine = add_chips("add_chips_in", slots, hooks.in_exchange.srcs, chip_no, axes[:1])
    res = adamw_halves("adamw_w_in", big[0], mine[0], pair_send(mine)[0], big_m[0], big_v[0], core, axes[0])
    big_grad["w_in"], *big_res["w_in"] = [jnp.swapaxes(r, 0, 1) for r in res]

    small_parts = [g["loss_acc"], g["g1"].reshape(8, LANES), g["g2"].reshape(8, LANES), g["gdn_ng"], g["qn"], g["kn"], g["p1"], g["p2"],
                   g["conv"].reshape(-1, LANES)]
    tiled = [jnp.pad(p, ((0, -p.shape[0] % 8), (0, 0))) for p in small_parts]
    red = all_reduce_small("reduce_small", jnp.concatenate(tiled, axis=0), slots)
    pos, red_parts = 0, []
    for p, t in zip(small_parts, tiled):
        red_parts.append(red[pos:pos + p.shape[0]])
        pos += t.shape[0]
    r_loss, r_g1, r_g2, r_gdn_ng, r_qn, r_kn, r_p1, r_p2, r_conv = red_parts
    loss = jnp.sum(r_loss)
    g_conv = lax.dynamic_slice_in_dim(r_conv.reshape(4, 4, conv_cols), chip, 1, axis=1).reshape(4, conv_cols)
    small_grads = [r_g1.reshape(1, dm), r_p2[:, :NH], r_p1[:, :NH], r_gdn_ng, r_qn, r_kn, r_p1[:, 2 * NH:3 * NH], r_g2.reshape(1, dm)]
    small_w = [norm_mix_g, gdn_a_log, gdn_dt_bias, gdn_norm_g, fox_q_norm_g, fox_k_norm_g, fox_f_bias, norm_mlp_g]
    small_m = [m_norm_mix_g, m_gdn_a_log, m_gdn_dt_bias, m_gdn_norm_g, m_fox_q_norm_g, m_fox_k_norm_g, m_fox_f_bias, m_norm_mlp_g]
    small_v = [v_norm_mix_g, v_gdn_a_log, v_gdn_dt_bias, v_gdn_norm_g, v_fox_q_norm_g, v_fox_k_norm_g, v_fox_f_bias, v_norm_mlp_g]

    def pack(parts):
        flat = jnp.concatenate([jnp.pad(p.reshape(-1), (0, -p.size % LANES)) for p in parts])
        return jnp.pad(flat, (0, -flat.size % (8 * LANES))).reshape(-1, LANES)

    packed = adamw("adamw_small", pack(small_w + [gdn_conv_w[0]]), pack(small_grads + [g_conv]),
                   pack(small_m + [m_gdn_conv_w[0]]), pack(small_v + [v_gdn_conv_w[0]]))

    def unpack(flat2d):
        flat, pos, res = flat2d.reshape(-1), 0, []
        for p in small_w + [gdn_conv_w[0]]:
            res.append(flat[pos:pos + p.size].reshape(p.shape))
            pos += p.size + (-p.size % LANES)
        return res

    s_delta, s_m, s_v = (unpack(a) for a in packed)

    order = ["norm_mix_g", "w_in", "gdn_conv_w", "gdn_a_log", "gdn_dt_bias", "gdn_norm_g", "fox_q_norm_g", "fox_k_norm_g",
             "fox_f_bias", "w_proj_gdn", "w_proj_fox", "w_out", "norm_mlp_g", "w_up", "w_down"]
    small_names = ["norm_mix_g", "gdn_a_log", "gdn_dt_bias", "gdn_norm_g", "fox_q_norm_g", "fox_k_norm_g", "fox_f_bias", "norm_mlp_g",
                   "gdn_conv_w"]
    small_idx = {nm: i for i, nm in enumerate(small_names)}
    shapes = dict(zip(order, (a.shape for a in (norm_mix_g, w_in, gdn_conv_w, gdn_a_log, gdn_dt_bias, gdn_norm_g, fox_q_norm_g,
                                                 fox_k_norm_g, fox_f_bias, w_proj_gdn, w_proj_fox, w_out, norm_mlp_g, w_up, w_down))))
    grads_out, delta_out, m_out, v_out = [], [], [], []
    for nm in order:
        if nm in big_res:
            d, mm, vv = big_res[nm]
            gr = big_grad[nm]
        else:
            i = small_idx[nm]
            gr = (small_grads + [g_conv])[i]
            d, mm, vv = s_delta[i], s_m[i], s_v[i]
        for lst, val in ((grads_out, gr), (delta_out, d), (m_out, mm), (v_out, vv)):
            lst.append(val.reshape(shapes[nm]))
    return (loss, g["dx"].reshape(x.shape), *grads_out, *delta_out, *m_out, *v_out)
```

```python
import functools

import jax
import jax.numpy as jnp
from jax import lax
from jax.experimental import pallas as pl
from jax.experimental.pallas import tpu as pltpu

F32 = jnp.float32
BF16 = jnp.bfloat16
LANES = 128
NH = 8
EPS = 1e-6
GDN_CHUNK = 64
GDN_ROWS = 256
GDN_BASE = 16
ROW_TILE = 512
CONV_HEADS = 2
ATT_TILE = 512
NEG = -1e30
VMEM_LIMIT_BYTES = 48 * 1024 * 1024
HI = lax.Precision.HIGHEST
LO = lax.Precision.DEFAULT
MESH = pl.DeviceIdType.MESH
ANY = pl.BlockSpec(memory_space=pl.ANY)

ADAM_LR, ADAM_B1, ADAM_B2, ADAM_EPS, ADAM_WD, ADAM_STEP = 0.001, 0.9, 0.999, 1e-08, 0.01, 10


def _params(n_grid):
    return pltpu.CompilerParams(dimension_semantics=("arbitrary",) * n_grid,
                                vmem_limit_bytes=VMEM_LIMIT_BYTES)


def _dot(a, b, dims, precision=None):
    dn = {"nn": (((1,), (0,)), ((), ())), "nt": (((1,), (1,)), ((), ())), "tn": (((0,), (0,)), ((), ()))}[dims]
    return lax.dot_general(a, b, dn, precision=precision, preferred_element_type=F32)


def _iota(shape, dim):
    return lax.broadcasted_iota(jnp.int32, shape, dim)


def _split(x, parts):
    out = []
    for _ in range(parts - 1):
        hi = x.astype(BF16)
        out.append(hi)
        x = x - hi.astype(F32)
    return out + [x.astype(BF16)]


def _dot_mask(mask, b, dims):
    m16 = mask.astype(BF16)
    b1, b2, b3 = _split(b, 3)
    return _dot(m16, b1, dims) + (_dot(m16, b2, dims) + _dot(m16, b3, dims))


@jax.custom_vjp
def mm_mask(mask, b):
    return _dot_mask(mask, b, "nn")


mm_mask.defvjp(lambda mask, b: (_dot_mask(mask, b, "nn"), mask),
               lambda mask, g: (jnp.zeros_like(mask), _dot_mask(mask, g, "tn")))


def matmul(name, a, b, dims, out_dtype, add=None, tm=1024, tn=1024, tk=512, col_blocks=None,
           extras=(), epilogue=None, out_dtypes=None, k_part=None, after=()):
    if col_blocks and dims != "tn":
        nb, b_rows, bw = b.shape
        b_shape = (b_rows, nb * bw)
    else:
        b_shape = b.shape
    if dims == "nn":
        (m, k), (_, n) = a.shape, b_shape
    elif dims == "nt":
        (m, k), (n, _) = a.shape, b_shape
    else:
        (k, m), (_, n) = a.shape, b_shape
    k_span = k // (k_part[1] if k_part else 1)
    if col_blocks and dims == "nt":
        k_span = min(k_span, bw)
    tk = k if k <= 1024 else max(t for t in (2048, 1536, 1024, 512, tk) if k_span % t == 0)
    tm, tn, tk = min(tm, m), min(tn, n), min(tk, k)
    assert m % tm == 0 and n % tn == 0 and k % tk == 0, (name, m, n, k)
    k0, nk = (0, k // tk) if k_part is None else (k_part[0] * (k // tk // k_part[1]), k // tk // k_part[1])
    assert k_part is None or (dims == "nn" and not col_blocks and (k // tk) % k_part[1] == 0)
    a_spec = pl.BlockSpec((tk, tm), lambda i, j, kk: (kk, i)) if dims == "tn" else pl.BlockSpec((tm, tk), lambda i, j, kk: (i, kk + k0))
    b_spec = pl.BlockSpec((tn, tk), lambda i, j, kk: (j, kk)) if dims == "nt" else pl.BlockSpec((tk, tn), lambda i, j, kk: (kk + k0, j))
    o_spec = pl.BlockSpec((tm, tn), lambda i, j, kk: (i, j))
    out_shape = (m, n)
    if col_blocks and dims == "nn":
        per = bw // tn
        assert bw % tn == 0
        b_spec = pl.BlockSpec((None, tk, tn), lambda i, j, kk: (j // per, kk, j % per))
    elif col_blocks and dims == "nt":
        per = bw // tk
        assert bw % tk == 0
        b_spec = pl.BlockSpec((None, tn, tk), lambda i, j, kk: (kk // per, j, kk % per))
    elif col_blocks:
        bw = n // col_blocks
        per = bw // tn
        assert bw % tn == 0 and add is None
        o_spec = pl.BlockSpec((None, tm, tn), lambda i, j, kk: (j // per, i, j % per))
        out_shape = (col_blocks, m, bw)
    extras = list(extras) + ([add] if add is not None else [])
    if add is not None:
        assert epilogue is None
        epilogue = lambda r, *e: [r + e[-1]]
    out_dtypes = [out_dtype] if epilogue is None or out_dtypes is None else list(out_dtypes)
    n_ex, n_out = len(extras), len(out_dtypes)

    def body(*refs):
        a_ref, b_ref = refs[0], refs[1]
        ex_refs, o_refs = refs[2:2 + n_ex], refs[2 + n_ex + len(after):2 + n_ex + len(after) + n_out]

        def finish(r):
            res = [r] if epilogue is None else epilogue(r, *[e[...] for e in ex_refs])
            for o_ref, v in zip(o_refs, res):
                o_ref[...] = v.astype(o_ref.dtype)

        if nk == 1:
            finish(_dot(a_ref[...], b_ref[...], dims))
            return
        acc_ref = refs[-1]
        kk = pl.program_id(2)

        @pl.when(kk == 0)
        def _():
            acc_ref[...] = jnp.zeros_like(acc_ref)

        acc_ref[...] += _dot(a_ref[...], b_ref[...], dims)

        @pl.when(kk == nk - 1)
        def _():
            finish(acc_ref[...])

    res = pl.pallas_call(
        body, name=name, grid=(m // tm, n // tn, nk), in_specs=[a_spec, b_spec] + [o_spec] * n_ex + [ANY] * len(after),
        out_specs=[o_spec] * n_out, out_shape=[jax.ShapeDtypeStruct(out_shape, dt) for dt in out_dtypes],
        scratch_shapes=[pltpu.VMEM((tm, tn), F32)] if nk > 1 else [], compiler_params=_params(3),
    )(a, b, *extras, *after)
    return res[0] if n_out == 1 else res


def _ew_spec(kind, off, width, tb, hp, order, shape=None):
    def ih(g0, g1):
        return (g0, g1) if order == "ih" else (g1, g0)

    assert off % hp == 0 or kind in ("row", "par")
    if kind == "row":
        return pl.BlockSpec((tb, width), lambda g0, g1: (ih(g0, g1)[0], off))
    if kind == "rowh":
        return pl.BlockSpec((tb, hp * width), lambda g0, g1: (ih(g0, g1)[0], ih(g0, g1)[1] + off // hp))
    if kind == "par":
        return pl.BlockSpec(shape, lambda g0, g1: (0, 0))
    if kind == "parh":
        return pl.BlockSpec((shape[0], hp * width), lambda g0, g1: (0, ih(g0, g1)[1] + off // hp))
    raise ValueError(kind)


def _ew_grid(rows, tb, nh, hp, order):
    assert nh % hp == 0 and rows % tb == 0
    return (rows // tb, nh // hp) if order == "ih" else (nh // hp, rows // tb)


def _ew_load(ref, kind, width, hh):
    if kind in ("row", "par"):
        return ref[...].astype(F32)
    return ref[:, hh * width:(hh + 1) * width].astype(F32)


def ew_fwd(name, f, ins, outs, rows, nh=1, tb=ROW_TILE, order="ih", hp=None, after=()):
    hp = nh if hp is None else hp
    n_in = len(ins)

    def body(*refs):
        hb = pl.program_id(1) if order == "ih" else pl.program_id(0)
        for hh in range(hp):
            h = hh if hp == nh else hb * hp + hh
            vals = [_ew_load(r, kd, w, hh) for r, (_, kd, _, w) in zip(refs[:n_in], ins)]
            res = f(h, *vals)
            for r, v, (_, kd, w, _) in zip(refs[n_in + len(after):], res, outs):
                if kd == "row":
                    assert hp == 1
                    r[...] = v.astype(r.dtype)
                else:
                    r[:, hh * w:(hh + 1) * w] = v.astype(r.dtype)

    in_specs = [_ew_spec(kd, off, w, tb, hp, order, a.shape) for (a, kd, off, w) in ins]
    out_specs = [_ew_spec(kd, 0, w, tb, hp, order) for (_, kd, w, _) in outs]
    out_shape = [jax.ShapeDtypeStruct((rows, tw), dt) for (tw, _, _, dt) in outs]
    return pl.pallas_call(
        body, name=name, grid=_ew_grid(rows, tb, nh, hp, order), in_specs=in_specs + [ANY] * len(after), out_specs=out_specs,
        out_shape=out_shape, compiler_params=_params(2),
    )(*[a for (a, _, _, _) in ins], *after)


def ew_bwd(name, f, ins, cts, extras, emit, outs, rows, nh=1, tb=ROW_TILE, order="ih", hp=None):
    hp = nh if hp is None else hp
    n_in = len(ins)
    flat_cts = [d for group in cts for d in group]
    n_ct, n_ex = len(flat_cts), len(extras)

    def body(*refs):
        g0, g1 = pl.program_id(0), pl.program_id(1)
        hb = g1 if order == "ih" else g0
        out_refs = refs[n_in + n_ct + n_ex:]
        shared = [None] * len(outs)

        def store(r, v, first, sl=None):
            def put(val, add):
                if sl is None:
                    r[...] = (r[...] + val if add else val).astype(r.dtype)
                else:
                    r[:, sl] = (r[:, sl] + val if add else val).astype(r.dtype)

            if first is None:
                put(v, False)
            else:
                pl.when(first)(lambda: put(v, False))
                pl.when(jnp.logical_not(first))(lambda: put(v, True))

        for hh in range(hp):
            h = hh if hp == nh else hb * hp + hh
            vals = [_ew_load(r, kd, w, hh) for r, (_, kd, _, w) in zip(refs[:n_in], ins)]
            ct_refs = list(zip(refs[n_in:n_in + n_ct], flat_cts))
            ct_vals, pos = [], 0
            for group in cts:
                v = None
                for r, (_, kd, _, w) in ct_refs[pos:pos + len(group)]:
                    t = _ew_load(r, kd, w, hh)
                    v = t if v is None else v + t
                pos += len(group)
                ct_vals.append(v)
            ex_vals = [_ew_load(r, kd, w, hh) for r, (_, kd, _, w) in zip(refs[n_in + n_ct:n_in + n_ct + n_ex], extras)]
            _, vjp = jax.vjp(lambda *a: f(h, *a), *vals)
            res = emit(vjp(tuple(ct_vals)), ex_vals)
            for idx, (r, v, (_, kd, w, _, acc)) in enumerate(zip(out_refs, res, outs)):
                if kd in ("row", "par"):
                    shared[idx] = v if shared[idx] is None else shared[idx] + v
                else:
                    store(r, v, (g1 == 0) if acc == "inner" else None, slice(hh * w, (hh + 1) * w))
        for idx, (r, (_, kd, _, _, acc)) in enumerate(zip(out_refs, outs)):
            if kd in ("row", "par"):
                assert acc == "all" or hp == nh
                store(r, shared[idx], jnp.logical_and(g0 == 0, g1 == 0) if acc == "all" else None)

    operands = list(ins) + flat_cts + list(extras)
    in_specs = [_ew_spec(kd, off, w, tb, hp, order, a.shape) for (a, kd, off, w) in operands]
    out_specs = [_ew_spec(kd, 0, w, tb, hp, order, shp) for (shp, kd, w, _, _) in outs]
    out_shape = [jax.ShapeDtypeStruct(shp, dt) for (shp, _, _, dt, _) in outs]
    return pl.pallas_call(
        body, name=name, grid=_ew_grid(rows, tb, nh, hp, order), in_specs=in_specs, out_specs=out_specs,
        out_shape=out_shape, compiler_params=_params(2),
    )(*[a for (a, _, _, _) in operands])


def f_rms(h, x, g):
    r = lax.rsqrt(jnp.mean(x * x, axis=-1, keepdims=True) + EPS)
    return (x * r * g,)


def _softplus(z):
    return jnp.maximum(z, 0.0) + jnp.log1p(jnp.exp(-jnp.abs(z)))


def f_small(h, sp, p1, p2):
    lane = _iota(sp.shape, 1)
    z = sp + p1
    g = -jnp.exp(p2) * _softplus(z)
    beta = jax.nn.sigmoid(z)
    logf = -_softplus(-z)
    return (jnp.where(lane < NH, g, jnp.where(lane < 2 * NH, beta, jnp.where(lane < 3 * NH, logf, 0.0))),)


def _pick(x, lane_id):
    lane = _iota(x.shape, 1)
    col = jnp.sum(jnp.where(lane == lane_id, x, 0.0), axis=1, keepdims=True)
    return jnp.broadcast_to(col, x.shape)


def f_bcast(h, so, cs):
    return _pick(so, h), _pick(so, h + NH), _pick(cs, h + 2 * NH)


def _shift_down(s):
    def down(x):
        return jnp.where(_iota(x.shape, 0) >= s, pltpu.roll(x, s, 0), 0.0)

    def up(g):
        n = g.shape[0]
        return jnp.where(_iota(g.shape, 0) < n - s, pltpu.roll(g, n - s, 0), 0.0)

    @jax.custom_vjp
    def shift(x):
        return down(x)

    shift.defvjp(lambda x: (down(x), None), lambda _, g: (up(g),))
    return shift


def _silu(x):
    return x * jax.nn.sigmoid(x)


def make_f_conv(mode):
    sh1, sh2, sh3 = _shift_down(1), _shift_down(2), _shift_down(3)

    def f(h, x, w):
        sub = _iota(w.shape, 0)

        def tap(i):
            return jnp.sum(jnp.where(sub == i, w, 0.0), axis=0, keepdims=True)

        y = sh3(x) * tap(0)
        y = y + sh2(x) * tap(1)
        y = y + sh1(x) * tap(2)
        y = y + x * tap(3)
        s = _silu(y)
        if mode == "v":
            return (s,)
        n = s * lax.rsqrt(jnp.sum(s * s, axis=-1, keepdims=True) + EPS)
        if mode == "q":
            n = n * (LANES ** -0.5)
        return (n,)

    return f


def f_post(h, o, z, g):
    r = lax.rsqrt(jnp.mean(o * o, axis=-1, keepdims=True) + EPS)
    return (o * r * g * _silu(z),)


def f_merge(h, ga, gb, ya, yb):
    return (jax.nn.sigmoid(ga) * ya + jax.nn.sigmoid(gb) * yb,)


def f_delta(h, do, o):
    return (jnp.broadcast_to(jnp.sum(do * o, axis=1, keepdims=True), o.shape),)


def cumsum_time(name, x, nseq, seq, reverse):
    nb = seq // LANES

    def body(x_ref, o_ref):
        r, c = _iota((LANES, LANES), 0), _iota((LANES, LANES), 1)
        tri = jnp.where((r <= c) if reverse else (r >= c), 1.0, 0.0).astype(F32)
        carry = jnp.zeros((1, LANES), F32)
        for b in (range(nb - 1, -1, -1) if reverse else range(nb)):
            blk = x_ref[b * LANES:(b + 1) * LANES, :]
            o_ref[b * LANES:(b + 1) * LANES, :] = _dot_mask(tri, blk, "nn") + carry
            carry = carry + jnp.sum(blk, axis=0, keepdims=True)

    spec = pl.BlockSpec((seq, LANES), lambda s: (s, 0))
    return pl.pallas_call(body, name=name, grid=(nseq,), in_specs=[spec], out_specs=spec,
                          out_shape=jax.ShapeDtypeStruct(x.shape, F32), compiler_params=_params(1))(x)


def transpose_time(name, x, nseq, seq):
    def body(x_ref, o_ref):
        o_ref[...] = x_ref[...].T

    return pl.pallas_call(
        body, name=name, grid=(nseq,), in_specs=[pl.BlockSpec((seq, LANES), lambda s: (s, 0))],
        out_specs=pl.BlockSpec((LANES, seq), lambda s: (s, 0)),
        out_shape=jax.ShapeDtypeStruct((nseq * LANES, seq), F32), compiler_params=_params(1))(x)


def _gdn_masks():
    n = GDN_ROWS
    r, c = _iota((n, n), 0), _iota((n, n), 1)
    shift = GDN_CHUNK.bit_length() - 1
    same = lax.shift_right_logical(r, shift) == lax.shift_right_logical(c, shift)
    return r, c, same


def _each(fn, *lists):
    return [fn(*xs) for xs in zip(*lists)]


def _gdn_decay(gbs):
    r, c, same = _gdn_masks()
    seg_tril = jnp.where(jnp.logical_and(same, r >= c), 1.0, 0.0).astype(F32)
    g_cum = _each(lambda gb: mm_mask(seg_tril, gb), gbs)
    lane0 = _iota(gbs[0].shape, 1) == 0
    g_col = _each(lambda g: jnp.sum(jnp.where(lane0, g, 0.0), axis=1, keepdims=True), g_cum)
    g_row = _each(lambda g: jnp.sum(jnp.where(r == c, jnp.broadcast_to(g, (GDN_ROWS, GDN_ROWS)), 0.0), axis=0, keepdims=True), g_col)
    return g_cum, _each(lambda a, b: a - b, g_col, g_row)


def gdn_f1(*args):
    qs, ks, gbs, bbs = (list(args[i::4]) for i in range(4))
    r, c, same = _gdn_masks()
    strict = jnp.logical_and(same, r > c)
    _, diff = _gdn_decay(gbs)
    lane0 = _iota(bbs[0].shape, 1) == 0
    beta_col = _each(lambda bb: jnp.sum(jnp.where(lane0, bb, 0.0), axis=1, keepdims=True), bbs)
    kk = _each(lambda k: _dot(k, k, "nt", LO), ks)
    return tuple(_each(lambda b, x, d: jnp.where(strict, b * x * jnp.exp(jnp.where(strict, d, 0.0)), 0.0), beta_col, kk, diff))


def gdn_f2(*args):
    ts, qs, ks, vs, gbs, bbs = (list(args[i::6]) for i in range(6))
    r, c, same = _gdn_masks()
    incl = jnp.logical_and(same, r >= c)
    g_cum, diff = _gdn_decay(gbs)
    decay = _each(lambda d: jnp.where(incl, jnp.exp(jnp.where(incl, d, 0.0)), 0.0), diff)
    e_g = _each(jnp.exp, g_cum)
    v_beta = _each(lambda v, bb: v * bb, vs, bbs)
    k_beta = _each(lambda k, bb, e: k * bb * e, ks, bbs, e_g)
    value = _each(lambda t, x: x + _dot(t, x, "nn", LO), ts, v_beta)
    k_cum = _each(lambda t, x: x + _dot(t, x, "nn", LO), ts, k_beta)
    attn = _each(lambda q, k, d: _dot(q, k, "nt", LO) * d, qs, ks, decay)
    ones = jnp.where(same, 1.0, 0.0).astype(F32)
    g_last = _each(lambda gb: mm_mask(ones, gb), gbs)
    q_dec = _each(lambda q, e: q * e, qs, e_g)
    k_dec = _each(lambda k, gl, g: k * jnp.exp(gl - g), ks, g_last, g_cum)
    return tuple(x for head in zip(value, k_cum, attn, q_dec, k_dec) for x in head)


def tri_inverse(mats):
    n = GDN_ROWS
    r, c = _iota((n, n), 0), _iota((n, n), 1)
    shift = GDN_BASE.bit_length() - 1
    blk = lax.shift_right_logical(r, shift) == lax.shift_right_logical(c, shift)
    each = lambda fn, *lists: [fn(*xs) for xs in zip(*lists)]
    mm = lambda x, y: _dot(x, y, "nn", LO)
    d = each(lambda a: jnp.where(blk, a, 0.0), mats)
    lo = each(lambda a, dd: a - dd, mats, d)
    p = each(lambda dd: -dd, d)
    c_d = p
    for _ in range(shift - 1):
        p = each(mm, p, p)
        c_d = each(lambda cd, pp, prod: cd + pp + prod, c_d, p, each(mm, c_d, p))
    assert GDN_CHUNK // GDN_BASE == 4
    nmat = each(lambda l, prod: l + prod, lo, each(mm, c_d, lo))
    n2 = each(mm, nmat, nmat)
    c_n = each(lambda nn2, nm, prod: (nn2 - nm) - prod, n2, nmat, each(mm, nmat, n2))
    return each(lambda cn, cd, prod: cn + cd + prod, c_n, c_d, each(mm, c_n, c_d))


GDN_AHP = 4


def _gdn_a_specs():
    blk = pl.BlockSpec((GDN_ROWS, GDN_AHP * LANES), lambda i, h: (i, h))
    sq = pl.BlockSpec((GDN_ROWS, GDN_AHP * GDN_ROWS), lambda i, h: (i, h))
    return blk, sq


def _head(ref, hh):
    width = ref.shape[1] // GDN_AHP
    return ref.at[:, hh * width:(hh + 1) * width]


def gdn_a_fwd(q, k, v, gb, bb, rows):
    blk, sq = _gdn_a_specs()

    def body(q_ref, k_ref, v_ref, gb_ref, bb_ref, val_ref, kc_ref, at_ref, qd_ref, kd_ref, t_ref):
        heads = [[_head(r, hh)[...] for r in (q_ref, k_ref, v_ref, gb_ref, bb_ref)] for hh in range(GDN_AHP)]
        t_corr = tri_inverse(list(gdn_f1(*[x for qv, kv, vv, gv, bv in heads for x in (qv, kv, gv, bv)])))
        res = gdn_f2(*[x for t, head in zip(t_corr, heads) for x in (t, *head)])
        for hh in range(GDN_AHP):
            for r, x in zip((val_ref, kc_ref, at_ref, qd_ref, kd_ref, t_ref), (*res[5 * hh:5 * hh + 5], t_corr[hh])):
                _head(r, hh)[...] = x.astype(r.dtype)

    wide = lambda dt: jax.ShapeDtypeStruct((rows, NH * LANES), dt)
    square = jax.ShapeDtypeStruct((rows, NH * GDN_ROWS), BF16)
    return pl.pallas_call(
        body, name="gdn_a_fwd", grid=(rows // GDN_ROWS, NH // GDN_AHP), in_specs=[blk] * 5,
        out_specs=[blk, blk, sq, blk, blk, sq], out_shape=[wide(F32), wide(BF16), square, wide(BF16), wide(BF16), square],
        compiler_params=_params(2))(q, k, v, gb, bb)


def gdn_a_bwd(q, k, v, gb, bb, t_inv, dval, dkc, dat, dqd, dkd, dgb_b, rows):
    blk, sq = _gdn_a_specs()

    def body(q_ref, k_ref, v_ref, gb_ref, bb_ref, t_ref, dval_ref, dkc_ref, dat_ref, dqd_ref, dkd_ref, dgbb_ref,
             dq_ref, dk_ref, dv_ref, dgb_ref, dbb_ref):
        hs = range(GDN_AHP)
        heads = [[_head(r, hh)[...] for r in (q_ref, k_ref, v_ref, gb_ref, bb_ref)] for hh in hs]
        tvs = [_head(t_ref, hh)[...].astype(F32) for hh in hs]
        _, vjp1 = jax.vjp(gdn_f1, *[x for qv, kv, vv, gv, bv in heads for x in (qv, kv, gv, bv)])
        _, vjp2 = jax.vjp(gdn_f2, *[x for t, head in zip(tvs, heads) for x in (t, *head)])
        g2 = vjp2(tuple(_head(r, hh)[...] for hh in hs for r in (dval_ref, dkc_ref, dat_ref, dqd_ref, dkd_ref)))
        dts = [g2[6 * hh] for hh in hs]
        left = _each(lambda dt, tv: dt + _dot(tv, dt, "tn", LO), dts, tvs)
        g1 = vjp1(tuple(_each(lambda lf, tv: -(lf + _dot(lf, tv, "nt", LO)), left, tvs)))
        for hh in hs:
            dq1, dk1, dgb1, dbb1 = g1[4 * hh:4 * hh + 4]
            _, dq2, dk2, dv2, dgb2, dbb2 = g2[6 * hh:6 * hh + 6]
            _head(dq_ref, hh)[...] = dq1 + dq2
            _head(dk_ref, hh)[...] = dk1 + dk2
            _head(dv_ref, hh)[...] = dv2
            _head(dgb_ref, hh)[...] = dgb1 + dgb2 + _head(dgbb_ref, hh)[...]
            _head(dbb_ref, hh)[...] = dbb1 + dbb2

    wide = jax.ShapeDtypeStruct((rows, NH * LANES), F32)
    return pl.pallas_call(
        body, name="gdn_a_bwd", grid=(rows // GDN_ROWS, NH // GDN_AHP),
        in_specs=[blk] * 5 + [sq, blk, blk, sq, blk, blk, blk], out_specs=[blk] * 5, out_shape=[wide] * 5,
        compiler_params=_params(2))(q, k, v, gb, bb, t_inv, dval, dkc, dat, dqd, dkd, dgb_b)


N_CH = GDN_ROWS // GDN_CHUNK


GDN_HP = 8


def gdn_fb(*args):
    per_head = 6 * N_CH
    states = list(args[GDN_HP * per_head:])
    outs = [[None] * N_CH for _ in range(GDN_HP)]
    zero = jnp.zeros((GDN_CHUNK, LANES), F32)
    for c in range(N_CH):
        for hh in range(GDN_HP):
            val, kc, at, qd, kd, gb = (args[hh * per_head + i * N_CH + c] for i in range(6))
            s = states[hh]
            v_new = val - _dot(kc, s, "nn", LO)
            v_pad = jnp.concatenate([zero] * c + [v_new] + [zero] * (N_CH - 1 - c), axis=0)
            outs[hh][c] = _dot(qd, s, "nn", LO) + _dot(at, v_pad, "nn", LO)
            dec = jnp.exp(jnp.sum(gb, axis=0, keepdims=True))
            states[hh] = s * dec + _dot(kd, v_new, "tn", LO)
    return (*[o for head in outs for o in head], *states)


def _gdn_piece(ref, hh, c):
    width = ref.shape[1] // GDN_HP
    return ref.at[c * GDN_CHUNK:(c + 1) * GDN_CHUNK, hh * width:(hh + 1) * width]


def _gdn_pieces(refs, hh):
    return [_gdn_piece(r, hh, c)[...].astype(F32) for r in refs for c in range(N_CH)]


def _gdn_b_specs(nb, rev):
    def blk_row(s, j):
        return s * nb + (nb - 1 - j if rev else j)

    blk = pl.BlockSpec((GDN_ROWS, GDN_HP * LANES), lambda s, hb, j: (blk_row(s, j), hb))
    sq = pl.BlockSpec((GDN_ROWS, GDN_HP * GDN_ROWS), lambda s, hb, j: (blk_row(s, j), hb))
    snap = pl.BlockSpec((GDN_HP * LANES, LANES), lambda s, hb, j: (blk_row(s, j) * (NH // GDN_HP) + hb, 0))
    return blk, sq, snap


def gdn_b_fwd(val, kc, at, qd, kd, gb, nseq, seq):
    nb = seq // GDN_ROWS
    rows = nseq * seq
    blk, sq, snap = _gdn_b_specs(nb, False)

    def body(val_ref, kc_ref, at_ref, qd_ref, kd_ref, gb_ref, o_ref, snap_ref, s_ref):
        @pl.when(pl.program_id(2) == 0)
        def _():
            s_ref[...] = jnp.zeros_like(s_ref)

        states = [s_ref[hh] for hh in range(GDN_HP)]
        for hh in range(GDN_HP):
            snap_ref[hh * LANES:(hh + 1) * LANES, :] = states[hh]
        pieces = [p for hh in range(GDN_HP) for p in _gdn_pieces([val_ref, kc_ref, at_ref, qd_ref, kd_ref, gb_ref], hh)]
        res = gdn_fb(*pieces, *states)
        for hh in range(GDN_HP):
            for c in range(N_CH):
                _gdn_piece(o_ref, hh, c)[...] = res[hh * N_CH + c]
            s_ref[hh] = res[GDN_HP * N_CH + hh]

    return pl.pallas_call(
        body, name="gdn_b_fwd", grid=(nseq, NH // GDN_HP, nb), in_specs=[blk, blk, sq, blk, blk, blk], out_specs=[blk, snap],
        out_shape=[jax.ShapeDtypeStruct((rows, NH * LANES), F32), jax.ShapeDtypeStruct((nseq * nb * NH * LANES, LANES), F32)],
        scratch_shapes=[pltpu.VMEM((GDN_HP, LANES, LANES), F32)], compiler_params=_params(3))(val, kc, at, qd, kd, gb)


def gdn_b_bwd(val, kc, at, qd, kd, gb, snaps, do, nseq, seq):
    nb = seq // GDN_ROWS
    rows = nseq * seq
    blk, sq, snap = _gdn_b_specs(nb, True)

    def body(val_ref, kc_ref, at_ref, qd_ref, kd_ref, gb_ref, snap_ref, do_ref,
             dval_ref, dkc_ref, dat_ref, dqd_ref, dkd_ref, dgb_ref, ds_ref):
        @pl.when(pl.program_id(2) == 0)
        def _():
            ds_ref[...] = jnp.zeros_like(ds_ref)

        pieces = [p for hh in range(GDN_HP) for p in _gdn_pieces([val_ref, kc_ref, at_ref, qd_ref, kd_ref, gb_ref], hh)]
        states = [snap_ref[hh * LANES:(hh + 1) * LANES, :] for hh in range(GDN_HP)]
        _, vjp = jax.vjp(gdn_fb, *pieces, *states)
        cts = [p for hh in range(GDN_HP) for p in _gdn_pieces([do_ref], hh)] + [ds_ref[hh] for hh in range(GDN_HP)]
        grads = vjp(tuple(cts))
        for hh in range(GDN_HP):
            for i, r in enumerate([dval_ref, dkc_ref, dat_ref, dqd_ref, dkd_ref, dgb_ref]):
                for c in range(N_CH):
                    _gdn_piece(r, hh, c)[...] = grads[hh * 6 * N_CH + i * N_CH + c]
            ds_ref[hh] = grads[GDN_HP * 6 * N_CH + hh]

    wide = jax.ShapeDtypeStruct((rows, NH * LANES), F32)
    square = jax.ShapeDtypeStruct((rows, NH * GDN_ROWS), F32)
    return pl.pallas_call(
        body, name="gdn_b_bwd", grid=(nseq, NH // GDN_HP, nb), in_specs=[blk, blk, sq, blk, blk, blk, snap, blk],
        out_specs=[blk, blk, sq, blk, blk, blk], out_shape=[wide, wide, square, wide, wide, wide],
        scratch_shapes=[pltpu.VMEM((GDN_HP, LANES, LANES), F32)], compiler_params=_params(3))(val, kc, at, qd, kd, gb, snaps, do)


FOX_Q, FOX_K, FOX_V = 4 * NH, 5 * NH, 6 * NH
FOX_SCALE = LANES ** -0.5


def _head_row(ct_ref, h, off, width):
    blk = ct_ref[:, pl.ds(off, width)]
    return jnp.sum(jnp.where(_iota(blk.shape, 0) == h, blk, 0.0), axis=0, keepdims=True)


def _col(x):
    return jnp.max(x, axis=1, keepdims=True)


def _row(x):
    return jnp.max(x.T, axis=0, keepdims=True)


def _causal(shape, q_dim):
    return _iota(shape, q_dim) >= _iota(shape, 1 - q_dim)


FOX_HP = 2


def _fox_specs(seq, tile, n_tiles):
    tblk = pl.BlockSpec((tile, FOX_HP * LANES), lambda s, h, i: (s * n_tiles + i, h))
    vtblk = pl.BlockSpec((tile, FOX_HP * LANES), lambda s, h, i: (s * n_tiles + i, h + FOX_V // FOX_HP))
    full = pl.BlockSpec((seq, FOX_HP * LANES), lambda s, h, i: (s, h))
    vfull = pl.BlockSpec((seq, FOX_HP * LANES), lambda s, h, i: (s, h + FOX_V // FOX_HP))
    ctb = pl.BlockSpec((NH, seq), lambda s, h, i: (s * (LANES // NH) + 2, 0))
    return tblk, vtblk, full, vfull, ctb


def _lanes_of(hh):
    return slice(hh * LANES, (hh + 1) * LANES)


def fox_fwd(qn, kn, proj, ct, nseq, seq):
    tq = tk = min(ATT_TILE, seq)
    nq = seq // tq
    rows = nseq * seq
    qblk, _, full, vfull, ctb = _fox_specs(seq, tq, nq)
    hs = range(FOX_HP)

    def body(q_ref, k_ref, v_ref, ct_ref, o_ref, o16_ref, lse_ref):
        hb, i = pl.program_id(1), pl.program_id(2)
        q = [q_ref[:, _lanes_of(hh)] for hh in hs]

        def step(j, carry, diag):
            m, l, acc = (list(carry[t::3]) for t in range(3))
            off = pl.multiple_of(j * tk, tk)
            k = [k_ref[pl.ds(off, tk), _lanes_of(hh)] for hh in hs]
            v = [v_ref[pl.ds(off, tk), _lanes_of(hh)].astype(BF16) for hh in hs]
            ck = [_head_row(ct_ref, hb * FOX_HP + hh, off, tk) for hh in hs]
            s = _each(lambda qq, kk, cc: _dot(qq, kk, "nt") * FOX_SCALE - cc, q, k, ck)
            if diag:
                s = _each(lambda x: jnp.where(_causal(x.shape, 0), x, NEG), s)
            m_new = _each(lambda mm, x: jnp.maximum(mm, jnp.max(x, axis=1, keepdims=True)), m, s)
            p = _each(lambda x, mm: jnp.exp(x - mm), s, m_new)
            alpha = _each(lambda mo, mn: jnp.exp(mo - mn), m, m_new)
            l = _each(lambda a, ll, pp: a * ll + jnp.sum(pp, axis=1, keepdims=True), alpha, l, p)
            acc = _each(lambda a, ac, pp, vv: a * ac + _dot(pp.astype(BF16), vv, "nn"), alpha, acc, p, v)
            return tuple(x for head in zip(m_new, l, acc) for x in head)

        init = (jnp.full((tq, 1), NEG, F32), jnp.zeros((tq, 1), F32), jnp.zeros((tq, LANES), F32)) * FOX_HP
        res = step(i, lax.fori_loop(0, i, lambda j, c: step(j, c, False), init), True)
        for hh in hs:
            m, l, acc = res[3 * hh:3 * hh + 3]
            o = acc / l
            o_ref[:, _lanes_of(hh)] = o
            o16_ref[:, _lanes_of(hh)] = o.astype(BF16)
            lse_ref[:, _lanes_of(hh)] = jnp.broadcast_to(m + jnp.log(l), (tq, LANES))

    wide = (rows, NH * LANES)
    return pl.pallas_call(
        body, name="fox_fwd", grid=(nseq, NH // FOX_HP, nq), in_specs=[qblk, full, vfull, ctb], out_specs=[qblk] * 3,
        out_shape=[jax.ShapeDtypeStruct(wide, F32), jax.ShapeDtypeStruct(wide, BF16), jax.ShapeDtypeStruct(wide, F32)],
        compiler_params=_params(3))(qn, kn, proj, ct)


def fox_dq(qn, kn, proj, ct, do, lse, delta, nseq, seq):
    tq = tk = min(ATT_TILE, seq)
    nq = seq // tq
    rows = nseq * seq
    qblk, _, full, vfull, ctb = _fox_specs(seq, tq, nq)
    hs = range(FOX_HP)

    def body(q_ref, k_ref, v_ref, ct_ref, do_ref, lse_ref, dl_ref, dq_ref, dc_ref):
        hb, i = pl.program_id(1), pl.program_id(2)
        q = [q_ref[:, _lanes_of(hh)] for hh in hs]
        lse = [_col(lse_ref[:, _lanes_of(hh)]) for hh in hs]
        delta = [_col(dl_ref[:, _lanes_of(hh)]) for hh in hs]
        do16 = [do_ref[:, _lanes_of(hh)].astype(BF16) for hh in hs]

        def step(j, carry, diag):
            dq, dc = (list(carry[t::2]) for t in range(2))
            off = pl.multiple_of(j * tk, tk)
            k = [k_ref[pl.ds(off, tk), _lanes_of(hh)] for hh in hs]
            v = [v_ref[pl.ds(off, tk), _lanes_of(hh)].astype(BF16) for hh in hs]
            ck = [_head_row(ct_ref, hb * FOX_HP + hh, off, tk) for hh in hs]
            p = _each(lambda qq, kk, cc, ll: jnp.exp(_dot(qq, kk, "nt") * FOX_SCALE - cc - ll), q, k, ck, lse)
            if diag:
                p = _each(lambda x: jnp.where(_causal(x.shape, 0), x, 0.0), p)
            dp = _each(lambda d, vv: _dot(d, vv, "nt"), do16, v)
            ds = _each(lambda pp, d, dl: pp * (d - dl), p, dp, delta)
            dq = _each(lambda a, x, kk: a + _dot(x.astype(BF16), kk, "nn"), dq, ds, k)
            dc = _each(lambda a, x: a + jnp.sum(x, axis=1, keepdims=True), dc, ds)
            return tuple(x for head in zip(dq, dc) for x in head)

        init = (jnp.zeros((tq, LANES), F32), jnp.zeros((tq, 1), F32)) * FOX_HP
        res = step(i, lax.fori_loop(0, i, lambda j, c: step(j, c, False), init), True)
        for hh in hs:
            dq_ref[:, _lanes_of(hh)] = res[2 * hh] * FOX_SCALE
            dc_ref[:, _lanes_of(hh)] = jnp.where(_iota((tq, LANES), 1) == 0, res[2 * hh + 1], 0.0)

    wide = jax.ShapeDtypeStruct((rows, NH * LANES), F32)
    return pl.pallas_call(
        body, name="fox_dq", grid=(nseq, NH // FOX_HP, nq), in_specs=[qblk, full, vfull, ctb, qblk, qblk, qblk],
        out_specs=[qblk, qblk], out_shape=[wide, wide], compiler_params=_params(3))(qn, kn, proj, ct, do, lse, delta)


def fox_dkv(qn, kn, proj, cb, do, lse, delta, nseq, seq):
    tq = tk = min(ATT_TILE, seq)
    nq = seq // tq
    rows = nseq * seq
    kblk, vblk, full, _, _ = _fox_specs(seq, tk, nq)
    hs = range(FOX_HP)

    def body(q_ref, k_ref, v_ref, cb_ref, do_ref, lse_ref, dl_ref, dk_ref, dv_ref, dc_ref):
        j = pl.program_id(2)
        k = [k_ref[:, _lanes_of(hh)] for hh in hs]
        v16 = [v_ref[:, _lanes_of(hh)].astype(BF16) for hh in hs]
        ck = [_col(cb_ref[:, _lanes_of(hh)]) for hh in hs]

        def step(i, carry, diag):
            dk, dv, dc = (list(carry[t::3]) for t in range(3))
            off = pl.multiple_of(i * tq, tq)
            q = [q_ref[pl.ds(off, tq), _lanes_of(hh)] for hh in hs]
            do16 = [do_ref[pl.ds(off, tq), _lanes_of(hh)].astype(BF16) for hh in hs]
            lse = [_row(lse_ref[pl.ds(off, tq), _lanes_of(hh)]) for hh in hs]
            delta = [_row(dl_ref[pl.ds(off, tq), _lanes_of(hh)]) for hh in hs]
            p = _each(lambda kk, qq, cc, ll: jnp.exp(_dot(kk, qq, "nt") * FOX_SCALE - cc - ll), k, q, ck, lse)
            if diag:
                p = _each(lambda x: jnp.where(_causal(x.shape, 1), x, 0.0), p)
            dv = _each(lambda a, pp, d: a + _dot(pp.astype(BF16), d, "nn"), dv, p, do16)
            ds = _each(lambda pp, vv, d, dl: pp * (_dot(vv, d, "nt") - dl), p, v16, do16, delta)
            dk = _each(lambda a, x, qq: a + _dot(x.astype(BF16), qq, "nn"), dk, ds, q)
            dc = _each(lambda a, x: a + jnp.sum(x, axis=1, keepdims=True), dc, ds)
            return tuple(x for head in zip(dk, dv, dc) for x in head)

        zero = jnp.zeros((tk, LANES), F32)
        carry = step(j, (zero, zero, jnp.zeros((tk, 1), F32)) * FOX_HP, True)
        res = lax.fori_loop(j + 1, nq, lambda i, c: step(i, c, False), carry)
        for hh in hs:
            dk, dv, dc = res[3 * hh:3 * hh + 3]
            dk_ref[:, _lanes_of(hh)] = dk * FOX_SCALE
            dv_ref[:, _lanes_of(hh)] = dv.astype(BF16)
            dc_ref[:, _lanes_of(hh)] = jnp.where(_iota((tk, LANES), 1) == 0, -dc, 0.0)

    wide = (rows, NH * LANES)
    return pl.pallas_call(
        body, name="fox_dkv", grid=(nseq, NH // FOX_HP, nq), in_specs=[full, kblk, vblk, kblk, full, full, full],
        out_specs=[kblk, kblk, kblk],
        out_shape=[jax.ShapeDtypeStruct(wide, F32), jax.ShapeDtypeStruct(wide, BF16), jax.ShapeDtypeStruct(wide, F32)],
        compiler_params=_params(3))(qn, kn, proj, cb, do, lse, delta)


def loss_head(out, tgt, rows, width):
    tb = ROW_TILE
    blk = pl.BlockSpec((tb, width), lambda i: (i, 0))
    accb = pl.BlockSpec((8, LANES), lambda i: (0, 0))

    def body(o_ref, t_ref, d32_ref, d16_ref, acc_ref):
        d = o_ref[...] - t_ref[...]
        row_loss = 0.5 * jnp.mean(d * d, axis=1, keepdims=True)
        g = d * (1.0 / width)
        d32_ref[...] = g
        d16_ref[...] = g.astype(BF16)
        part = jnp.where(_iota((tb, LANES), 1) == 0, row_loss, 0.0).reshape(tb // 8, 8, LANES).sum(axis=0)

        @pl.when(pl.program_id(0) == 0)
        def _():
            acc_ref[...] = part

        @pl.when(pl.program_id(0) != 0)
        def _():
            acc_ref[...] += part

    return pl.pallas_call(
        body, name="loss_head", grid=(rows // tb,), in_specs=[blk, blk], out_specs=[blk, blk, accb],
        out_shape=[jax.ShapeDtypeStruct((rows, width), F32), jax.ShapeDtypeStruct((rows, width), BF16),
                   jax.ShapeDtypeStruct((8, LANES), F32)], compiler_params=_params(1))(out, tgt)


def _adamw_update(w, g, m, v):
    m_new = ADAM_B1 * m + (1.0 - ADAM_B1) * g
    v_new = ADAM_B2 * v + (1.0 - ADAM_B2) * (g * g)
    m_hat = m_new / (1.0 - ADAM_B1 ** ADAM_STEP)
    v_hat = v_new / (1.0 - ADAM_B2 ** ADAM_STEP)
    return -ADAM_LR * (m_hat / (jnp.sqrt(v_hat) + ADAM_EPS) + ADAM_WD * w), m_new, v_new


def adamw(name, w, g, m, v):
    rows, cols = w.shape
    tb = min(rows, 128)
    assert rows % tb == 0
    blk = pl.BlockSpec((tb, cols), lambda i: (i, 0))

    def body(w_ref, g_ref, m_ref, v_ref, d_ref, mo_ref, vo_ref):
        d_ref[...], mo_ref[...], vo_ref[...] = _adamw_update(w_ref[...], g_ref[...], m_ref[...], v_ref[...])

    shp = jax.ShapeDtypeStruct(w.shape, F32)
    return pl.pallas_call(body, name=name, grid=(rows // tb,), in_specs=[blk] * 4, out_specs=[blk] * 3,
                          out_shape=[shp] * 3, compiler_params=_params(1))(w, g, m, v)


SPLIT_TILE = 128


def _tiled(shape2d, ax, n_lead, index):
    blk = (SPLIT_TILE, shape2d[1]) if ax == 0 else (shape2d[0], SPLIT_TILE)

    def index_map(*args):
        *lead, t = index(*args)
        return (*lead, t, 0) if ax == 0 else (*lead, 0, t)

    return pl.BlockSpec((None,) * n_lead + blk, index_map)


def adamw_halves(name, w, mine, other, m, v, c, ax):
    steps = w.shape[ax] // 2 // SPLIT_TILE
    assert w.shape[ax] == 2 * steps * SPLIT_TILE

    def body(c_ref, w_ref, mine_ref, other_ref, m_ref, v_ref, g_ref, d_ref, mo_ref, vo_ref):
        g = jnp.where(pl.program_id(0) // steps == c_ref[0], mine_ref[...], other_ref[...])
        g_ref[...] = g
        d_ref[...], mo_ref[...], vo_ref[...] = _adamw_update(w_ref[...], g, m_ref[...], v_ref[...])

    blk = _tiled(w.shape, ax, 0, lambda i, c_ref: (i,))
    hblk = _tiled(mine.shape, ax, 0, lambda i, c_ref: (i % steps,))
    grid_spec = pltpu.PrefetchScalarGridSpec(num_scalar_prefetch=1, grid=(2 * steps,),
                                             in_specs=[blk, hblk, hblk, blk, blk], out_specs=[blk] * 4)
    shp = jax.ShapeDtypeStruct(w.shape, F32)
    return pl.pallas_call(body, name=name, grid_spec=grid_spec, out_shape=[shp] * 4,
                          compiler_params=_params(1))(c, w, mine, other, m, v)


def add_chips(name, slots, parts, chip, axes):
    outs = []
    for idx, (x, own, ax) in enumerate(zip(slots, parts, axes)):
        n, shape2d = x.shape[0], x.shape[1:]
        steps = shape2d[ax] // SPLIT_TILE
        assert shape2d[ax] == steps * SPLIT_TILE

        def body(me_ref, *refs, n=n):
            o_ref = refs[n + 1]
            acc = None
            for t in range(n):
                term = jnp.where(me_ref[0] == t, refs[n][...], refs[t][...]).astype(F32)
                acc = term if acc is None else acc + term
            o_ref[...] = acc

        def filled(t, n=n):
            return lambda i, me_ref: (jnp.where(me_ref[0] == t, (t + 1) % n, t), i)

        grid_spec = pltpu.PrefetchScalarGridSpec(
            num_scalar_prefetch=1, grid=(steps,),
            in_specs=[_tiled(shape2d, ax, 1, filled(t)) for t in range(n)]
            + [_tiled(shape2d, ax, 1, lambda i, me_ref: (me_ref[0], i))],
            out_specs=_tiled(shape2d, ax, 0, lambda i, me_ref: (i,)))
        outs.append(pl.pallas_call(
            body, name=f"{name}_{idx}", grid_spec=grid_spec, out_shape=jax.ShapeDtypeStruct(shape2d, F32),
            compiler_params=_params(1))(chip, *([x] * n), own))
    return outs


def add_pair(name, gs, rs, c, axes):
    outs = []
    for idx, (g, r, ax) in enumerate(zip(gs, rs, axes)):
        nb = r.shape[0]
        steps = r.shape[1 + ax] // SPLIT_TILE
        assert r.shape[1 + ax] == steps * SPLIT_TILE

        def body(c_ref, g_ref, r_ref, o_ref):
            o_ref[...] = (g_ref[...] + r_ref[...]).astype(BF16)

        grid_spec = pltpu.PrefetchScalarGridSpec(
            num_scalar_prefetch=1, grid=(nb, steps),
            in_specs=[_tiled(g.shape[1:], ax, 1, lambda b, i, c_ref: (b, c_ref[0] * steps + i)),
                      _tiled(r.shape[1:], ax, 1, lambda b, i, c_ref: (b, i))],
            out_specs=_tiled(r.shape[1:], ax, 1, lambda b, i, c_ref: (b, i)))
        outs.append(pl.pallas_call(
            body, name=f"{name}_{idx}", grid_spec=grid_spec, out_shape=jax.ShapeDtypeStruct(r.shape, BF16),
            compiler_params=_params(2))(c, g, r))
    return outs


def _place():
    x, y, c = lax.axis_index("x"), lax.axis_index("y"), lax.axis_index("c")
    return x, y, c, [(1 - x, y), (x, 1 - y), (1 - x, 1 - y)]


def _remote(src, dst, send_sem, recv_sem, dev):
    return pltpu.make_async_remote_copy(src_ref=src, dst_ref=dst, send_sem=send_sem, recv_sem=recv_sem,
                                        device_id=dev, device_id_type=MESH)


def _half(ref, lead, ax, which):
    size = ref.shape[len(lead) + ax] // 2
    part = pl.ds(which * size, size)
    return ref.at[(*lead, part, slice(None)) if ax == 0 else (*lead, slice(None), part)]


def gather_weights(shards, axes):
    n = len(shards)

    def body(*refs):
        ins, outs = refs[:n], refs[n:2 * n]
        ici_s, ici_r, d2d_s, d2d_r = refs[2 * n:]
        x, y, c, chips = _place()
        me = 2 * x + y
        sends, passes = [], []
        for w in range(n):
            cp = _remote(ins[w], outs[w].at[me], d2d_s.at[3 * n + w], d2d_r.at[3 * n + w], (x, y, 1 - c))
            cp.start()
            passes.append(cp)
        for w in range(n):
            for j, (ox, oy) in enumerate(chips):
                cp = _remote(_half(ins[w], (), axes[w], c), _half(outs[w], (me,), axes[w], c),
                             ici_s.at[3 * w + j], ici_r.at[3 * w + j], (ox, oy, c))
                cp.start()
                sends.append(cp)
        for w in range(n):
            for j, (ox, oy) in enumerate(chips):
                landed = _half(outs[w], (2 * ox + oy,), axes[w], c)
                _remote(landed, landed, ici_s.at[3 * w + j], ici_r.at[3 * w + j], (ox, oy, c)).wait_recv()
                cp = _remote(landed, landed, d2d_s.at[3 * w + j], d2d_r.at[3 * w + j], (x, y, 1 - c))
                cp.start()
                passes.append(cp)
        for w in range(n):
            for j, (ox, oy) in enumerate(chips):
                other = _half(outs[w], (2 * ox + oy,), axes[w], 1 - c)
                _remote(other, other, d2d_s.at[3 * w + j], d2d_r.at[3 * w + j], (x, y, 1 - c)).wait_recv()
            own = outs[w].at[me]
            _remote(own, own, d2d_s.at[3 * n + w], d2d_r.at[3 * n + w], (x, y, 1 - c)).wait_recv()
        for cp in sends + passes:
            cp.wait_send()

    return pl.pallas_call(
        body, name="gather_weights", in_specs=[ANY] * n, out_specs=[ANY] * n,
        out_shape=[jax.ShapeDtypeStruct((4,) + s.shape, s.dtype) for s in shards],
        scratch_shapes=[pltpu.SemaphoreType.DMA((3 * n,))] * 2 + [pltpu.SemaphoreType.DMA((4 * n,))] * 2,
    )(*shards)


HBM = pl.BlockSpec(memory_space=pltpu.HBM)
SEM = pl.BlockSpec(memory_space=pltpu.SEMAPHORE)
DATAFLOW = pltpu.SideEffectType.DATAFLOW_SIDE_EFFECTING


def _hbm(a):
    return pltpu.with_memory_space_constraint(a, pltpu.HBM)


class SplitExchange:
    def __init__(self, name, srcs, zone_shapes, n_sems, plan):
        self.name, self.n, self.n_sems, self.plan = name, len(srcs), n_sems, plan
        self.srcs = [_hbm(s) for s in srcs]
        self.zones = [_hbm(lax.empty(shape, s.dtype)) for shape, s in zip(zone_shapes, srcs)]

    def start(self, after):
        n, n_after = self.n, len(after)

        def body(*refs):
            ins, lands = refs[:n], refs[n:2 * n]
            send, recv, token = refs[2 * n + n_after], refs[2 * n + n_after + 1], refs[-1]
            for src, dst, si, ri, dev in self.plan(ins, lands)[0]:
                _remote(src, dst, send.at[si], recv.at[ri], dev).start()
            token[...] = jnp.zeros_like(token)

        res = pl.pallas_call(
            body, name=f"{self.name}_start", in_specs=[HBM] * (2 * n) + [ANY] * n_after,
            out_specs=[SEM, SEM] + [HBM] * (2 * n) + [pl.BlockSpec(memory_space=pltpu.VMEM)],
            out_shape=[pltpu.SemaphoreType.DMA((self.n_sems,)), pltpu.SemaphoreType.DMA((self.n_sems,))]
            + [pltpu.HBM(a.shape, a.dtype) for a in self.srcs + self.zones] + [jax.ShapeDtypeStruct((8, LANES), F32)],
            input_output_aliases={i: 2 + i for i in range(2 * n)},
            compiler_params=pltpu.CompilerParams(has_side_effects=DATAFLOW),
        )(*self.srcs, *self.zones, *after)
        self.sems, self.srcs, self.zones = res[:2], list(res[2:2 + n]), list(res[2 + n:2 + 2 * n])
        return res[-1]

    def wait(self, after):
        n = self.n

        def body(*refs):
            ins, lands = refs[:n], refs[n:2 * n]
            send, recv = refs[2 * n], refs[2 * n + 1]
            sends, arrivals = self.plan(ins, lands)
            for src, _, si, _, dev in sends:
                _remote(src, src, send.at[si], recv.at[si], dev).wait_send()
            for landed, ri in arrivals:
                _remote(landed, landed, send.at[ri], recv.at[ri], _place()[:3]).wait_recv()

        res = pl.pallas_call(
            body, name=f"{self.name}_wait", in_specs=[HBM] * (2 * n) + [SEM, SEM, ANY], out_specs=[HBM] * (2 * n),
            out_shape=[pltpu.HBM(a.shape, a.dtype) for a in self.srcs + self.zones],
            input_output_aliases={i: i for i in range(2 * n)},
            compiler_params=pltpu.CompilerParams(has_side_effects=DATAFLOW),
        )(*self.srcs, *self.zones, *self.sems, after)
        self.srcs = list(res[:n])
        return list(res[n:])


def split_gather(shards):
    n = len(shards)

    def plan(ins, lands):
        x, y, c, chips = _place()
        me = 2 * x + y
        sends, arrivals = [], []
        for w in range(n):
            for j, (ox, oy) in enumerate(chips):
                for k in range(2):
                    base = 2 * (3 * w + j)
                    sends.append((_half(ins[w], (), 0, c), _half(lands[w], (me,), 0, c), base + k, base + c, (ox, oy, k)))
                    arrivals.append((_half(lands[w], (2 * ox + oy,), 0, k), base + k))
            sends.append((ins[w], lands[w].at[me], 6 * n + w, 6 * n + w, (x, y, 1 - c)))
            arrivals.append((lands[w].at[me], 6 * n + w))
        return sends, arrivals

    return SplitExchange("gather", shards, [(4,) + s.shape for s in shards], 7 * n, plan)


def split_pair_swap(name, grads, axes):
    def plan(ins, lands):
        x, y, c, _ = _place()
        sends = [(_half(ins[w], (slice(None),), axes[w], 1 - c), lands[w], w, w, (x, y, 1 - c)) for w in range(len(ins))]
        return sends, [(lands[w], w) for w in range(len(ins))]

    halved = [tuple(d // 2 if i == 1 + ax else d for i, d in enumerate(g.shape)) for g, ax in zip(grads, axes)]
    return SplitExchange(name, grads, halved, len(grads), plan)


def split_chip_exchange(name, parts):
    def plan(ins, lands):
        x, y, c, chips = _place()
        sends, arrivals = [], []
        for w in range(len(ins)):
            for j, (ox, oy) in enumerate(chips):
                sends.append((ins[w].at[2 * ox + oy], lands[w].at[2 * x + y], 3 * w + j, 3 * w + j, (ox, oy, c)))
                arrivals.append((lands[w].at[2 * ox + oy], 3 * w + j))
        return sends, arrivals

    return SplitExchange(name, parts, [p.shape for p in parts], 3 * len(parts), plan)


def split_pair_send(halves):
    def plan(ins, lands):
        x, y, c, _ = _place()
        return ([(ins[w], lands[w], w, w, (x, y, 1 - c)) for w in range(len(ins))],
                [(lands[w], w) for w in range(len(ins))])

    return SplitExchange("pair_send", halves, [h.shape for h in halves], len(halves), plan)


def pair_send(halves):
    n = len(halves)

    def body(*refs):
        ins, outs = refs[:n], refs[n:2 * n]
        send, recv = refs[2 * n:]
        x, y, c, _ = _place()
        cps = [_remote(ins[w], outs[w], send.at[w], recv.at[w], (x, y, 1 - c)) for w in range(n)]
        for cp in cps:
            cp.start()
        for cp in cps:
            cp.wait_recv()
        for cp in cps:
            cp.wait_send()

    return pl.pallas_call(
        body, name="pair_send", in_specs=[ANY] * n, out_specs=[ANY] * n,
        out_shape=[jax.ShapeDtypeStruct(h.shape, h.dtype) for h in halves],
        scratch_shapes=[pltpu.SemaphoreType.DMA((n,))] * 2,
    )(*halves)


def all_reduce_small(name, vec, after=()):
    rows = vec.shape[0]

    def body(v_ref, *refs):
        o_ref, buf, send, recv = refs[len(after):]
        x, y, c, _ = _place()
        me = 4 * x + 2 * y + c
        buf[me] = v_ref[...]
        cps = []
        for k in range(1, 8):
            kx, ky, kc = (k >> 2) & 1, (k >> 1) & 1, k & 1
            peer = (x if kx == 0 else 1 - x, y if ky == 0 else 1 - y, c if kc == 0 else 1 - c)
            cp = _remote(v_ref, buf.at[me], send.at[k - 1], recv.at[k - 1], peer)
            cp.start()
            cps.append(cp)
        for k in range(1, 8):
            kx, ky, kc = (k >> 2) & 1, (k >> 1) & 1, k & 1
            px, py, pc = (x if kx == 0 else 1 - x, y if ky == 0 else 1 - y, c if kc == 0 else 1 - c)
            slot = buf.at[4 * px + 2 * py + pc]
            _remote(slot, slot, send.at[k - 1], recv.at[k - 1], (px, py, pc)).wait_recv()
        for cp in cps:
            cp.wait_send()
        acc = buf[0]
        for d in range(1, 8):
            acc = acc + buf[d]
        o_ref[...] = acc

    vm = pl.BlockSpec(memory_space=pltpu.VMEM)
    return pl.pallas_call(
        body, name=name, in_specs=[vm] + [ANY] * len(after), out_specs=vm, out_shape=jax.ShapeDtypeStruct(vec.shape, F32),
        scratch_shapes=[pltpu.VMEM((8, rows, LANES), F32), pltpu.SemaphoreType.DMA((7,)), pltpu.SemaphoreType.DMA((7,))],
    )(vec, *after)


class NoExchange:
    def __init__(self, late):
        self.late = late

    def late_weights(self, after):
        return self.late

    def reduce_start(self, grads):
        return jnp.zeros((8, LANES), F32)

    def reduce_exchange(self, after):
        return jnp.zeros((8, LANES), F32)

    def reduce_finish(self, after):
        return jnp.zeros((8, LANES), F32)

    def input_grad_start(self, dw_main, dw_small):
        return jnp.zeros((8, LANES), F32)

    def input_grad_exchange(self, after):
        return jnp.zeros((8, LANES), F32)


def local_step(x2, tgt2, g1, g2, gdn_ng, qn_g, kn_g, p1, p2, conv_w, wt_main, wt_small, hooks, nseq, seq):
    rows, dm = x2.shape
    wide = NH * LANES
    row = lambda a, off=0, w=None: (a, "row", off, a.shape[1] if w is None else w)
    rowh = lambda a, off=0, w=LANES: (a, "rowh", off, w)
    par = lambda a: (a, "par", 0, a.shape[1])
    parh = lambda a, off=0: (a, "parh", off, LANES)
    o_row = lambda w, dt: (w, "row", w, dt)
    o_rowh = lambda dt, tw=wide, w=LANES: (tw, "rowh", w, dt)

    u, = ew_fwd("rms1", f_rms, [row(x2), par(g1)], [o_row(dm, BF16)], rows)
    proj = matmul("mm_in", u, wt_main, "nt", BF16)
    sp = matmul("mm_in_small", u, wt_small, "nt", F32)
    so, = ew_fwd("small", f_small, [row(sp), par(p1), par(p2)], [o_row(LANES, F32)], rows)
    cs = cumsum_time("cumsum", so, nseq, seq, False)
    gb, bb, cb = ew_fwd("bcast", f_bcast, [row(so), row(cs)], [o_rowh(F32)] * 3, rows, NH)
    ct = transpose_time("c_time_major", cs, nseq, seq)
    conv = {}
    for mode, off in (("q", 0), ("k", NH), ("v", 2 * NH)):
        conv[mode], = ew_fwd(f"conv_{mode}", make_f_conv(mode), [rowh(proj, off), parh(conv_w, off)], [o_rowh(F32)],
                             rows, NH, seq, "hi", CONV_HEADS)
    val, kcum, attn, qdec, kdec, t_inv = gdn_a_fwd(conv["q"], conv["k"], conv["v"], gb, bb, rows)
    o_a, snaps = gdn_b_fwd(val, kcum, attn, qdec, kdec, gb, nseq, seq)
    ya_in, = ew_fwd("gdn_post", f_post, [rowh(o_a), rowh(proj, 3 * NH), par(gdn_ng)], [o_rowh(BF16)], rows, NH)
    fqn, = ew_fwd("fox_qn", f_rms, [rowh(proj, FOX_Q), par(qn_g)], [o_rowh(BF16)], rows, NH)
    fkn, = ew_fwd("fox_kn", f_rms, [rowh(proj, FOX_K), par(kn_g)], [o_rowh(BF16)], rows, NH)
    o_b, o_b16, lse = fox_fwd(fqn, fkn, proj, ct, nseq, seq)
    p_a, p_b, w_o, w_u, w_d = hooks.late_weights(o_a)
    y_a = matmul("mm_pa", ya_in, p_a, "nn", F32, tn=1024)
    y_b = matmul("mm_pb", o_b16, p_b, "nn", F32, tn=1024)
    gates = [row(proj, 7, dm), row(proj, 8, dm)]
    merged, = ew_fwd("merge", f_merge, gates + [row(y_a), row(y_b)], [o_row(dm, BF16)], rows)
    hres = matmul("mm_out", merged, w_o, "nn", F32, add=x2, tn=1024)
    hn, = ew_fwd("rms2", f_rms, [row(hres), par(g2)], [o_row(dm, BF16)], rows)
    up_blocks = w_u.shape[0]
    act, relu2 = matmul("mm_up", hn, w_u, "nn", F32, col_blocks=up_blocks, out_dtypes=[F32, BF16],
                        epilogue=lambda r: [r, jnp.maximum(r, 0.0) * jnp.maximum(r, 0.0)])
    out = matmul("mm_down", relu2, w_d, "nn", F32, add=hres, tn=1024)
    dout, dout16, loss_acc = loss_head(out, tgt2, rows, dm)

    d_act = matmul("mm_d_act", dout16, w_d, "nt", BF16, extras=[act], epilogue=lambda r, a: [2.0 * jnp.maximum(a, 0.0) * r])
    dw_d = matmul("mm_dw_down", relu2, dout16, "tn", F32, tn=1024)
    dw_u = matmul("mm_dw_up", hn, d_act, "tn", F32, col_blocks=up_blocks)
    d_hn = matmul("mm_d_hn", d_act, w_u, "nt", F32, col_blocks=up_blocks)
    dh, dh16, dg2 = ew_bwd("rms2_b", f_rms, [row(hres), par(g2)], [(row(d_hn),)], [row(dout)],
                           lambda g, e: [g[0] + e[0], g[0] + e[0], g[1]],
                           [((rows, dm), "row", dm, F32, None), ((rows, dm), "row", dm, BF16, None), ((1, dm), "par", dm, F32, "all")], rows)
    d_merged = matmul("mm_d_merged", dh16, w_o, "nt", F32, tn=1024)
    dw_o = matmul("mm_dw_out", merged, dh16, "tn", F32, tn=1024)
    seg16 = ((rows, dm), "row", dm, BF16, None)
    d_ga16, d_gb16, d_ya16, d_yb16 = ew_bwd("merge_b", f_merge, gates + [row(y_a), row(y_b)], [(row(d_merged),)], [],
                                            lambda g, e: list(g), [seg16] * 4, rows)
    dp_a = matmul("mm_dp_a", ya_in, d_ya16, "tn", F32, tn=1024)
    d_ya_in = matmul("mm_d_ya_in", d_ya16, p_a, "nt", F32, tn=1024)
    dp_b = matmul("mm_dp_b", o_b16, d_yb16, "tn", F32, tn=1024)
    d_ob = matmul("mm_d_ob", d_yb16, p_b, "nt", F32, tn=1024)
    token = hooks.reduce_start(dict(p_a=dp_a, p_b=dp_b, w_o=dw_o, w_u=dw_u, w_d=dw_d))
    gdn_ng_t = gdn_ng + token[0, 0]
    h32 = ((rows, wide), "rowh", LANES, F32, None)
    h16 = ((rows, wide), "rowh", LANES, BF16, None)
    gain = ((1, LANES), "par", LANES, F32, "all")
    d_oa, d_z16, d_gdn_ng = ew_bwd("gdn_post_b", f_post, [rowh(o_a), rowh(proj, 3 * NH), par(gdn_ng_t)], [(rowh(d_ya_in),)], [],
                                   lambda g, e: list(g), [h32, h16, gain], rows, NH)
    dval, dkc, dat, dqd, dkd, dgb_b = gdn_b_bwd(val, kcum, attn, qdec, kdec, gb, snaps, d_oa, nseq, seq)
    d_cq, d_ck, d_cv, d_gb, d_bb = gdn_a_bwd(conv["q"], conv["k"], conv["v"], gb, bb, t_inv, dval, dkc, dat, dqd, dkd, dgb_b, rows)
    token = hooks.reduce_exchange(d_cq)
    conv_w_t = conv_w + token[0, 0]
    d_pre, d_conv = {}, {}
    tap = ((4, wide), "parh", LANES, F32, "inner")
    for mode, off, ctg in (("q", 0, d_cq), ("k", NH, d_ck), ("v", 2 * NH, d_cv)):
        d_pre[mode], d_conv[mode] = ew_bwd(f"conv_{mode}_b", make_f_conv(mode), [rowh(proj, off), parh(conv_w_t, off)],
                                           [(rowh(ctg),)], [], lambda g, e: list(g), [h16, tap], rows, NH, seq, "hi", CONV_HEADS)
    delta, = ew_fwd("fox_delta", f_delta, [rowh(d_ob), rowh(o_b)], [o_rowh(F32)], rows, NH, after=[token])
    d_fqn, d_cq_b = fox_dq(fqn, fkn, proj, ct, d_ob, lse, delta, nseq, seq)
    d_fkn, d_fv16, d_ck_b = fox_dkv(fqn, fkn, proj, cb, d_ob, lse, delta, nseq, seq)
    token = hooks.reduce_finish(d_fkn)
    qn_g_t, kn_g_t = qn_g + token[0, 0], kn_g + token[0, 0]
    d_fq16, d_qn_g = ew_bwd("fox_qn_b", f_rms, [rowh(proj, FOX_Q), par(qn_g_t)], [(rowh(d_fqn),)], [], lambda g, e: list(g),
                            [h16, gain], rows, NH)
    d_fk16, d_kn_g = ew_bwd("fox_kn_b", f_rms, [rowh(proj, FOX_K), par(kn_g_t)], [(rowh(d_fkn),)], [], lambda g, e: list(g),
                            [h16, gain], rows, NH)
    narrow = ((rows, LANES), "row", LANES, F32, None)
    d_so, d_cs = ew_bwd("bcast_b", f_bcast, [row(so), row(cs)], [(rowh(d_gb),), (rowh(d_bb),), (rowh(d_cq_b), rowh(d_ck_b))], [],
                        lambda g, e: list(g), [narrow, narrow], rows, NH)
    d_logf = cumsum_time("cumsum_b", d_cs, nseq, seq, True)
    vec = ((1, LANES), "par", LANES, F32, "all")
    d_sp16, d_p1, d_p2 = ew_bwd("small_b", f_small, [row(sp), par(p1), par(p2)], [(row(d_so), row(d_logf))], [],
                                lambda g, e: list(g), [((rows, LANES), "row", LANES, BF16, None), vec, vec], rows)
    d_proj16 = jnp.concatenate([d_pre["q"], d_pre["k"], d_pre["v"], d_z16, d_fq16, d_fk16, d_fv16, d_ga16, d_gb16], axis=1)
    dw_main = matmul("mm_dw_main", d_proj16, u, "tn", F32)
    dw_small = matmul("mm_dw_small", d_sp16, u, "tn", F32)
    wt_small_t = wt_small + hooks.input_grad_start(dw_main, dw_small)[0, 0].astype(BF16)
    d_u = matmul("mm_d_u_small", d_sp16, wt_small_t, "nn", F32)
    d_u = matmul("mm_d_u_first", d_proj16, wt_main, "nn", F32, add=d_u, k_part=(0, 2))
    d_u = matmul("mm_d_u_second", d_proj16, wt_main, "nn", F32, add=d_u, k_part=(1, 2), after=[hooks.input_grad_exchange(d_u)])
    dx, dg1 = ew_bwd("rms1_b", f_rms, [row(x2), par(g1)], [(row(d_u),)], [row(dh)], lambda g, e: [g[0] + e[0], g[1]],
                     [((rows, dm), "row", dm, F32, None), ((1, dm), "par", dm, F32, "all")], rows)
    d_conv_w = jnp.concatenate([d_conv["q"], d_conv["k"], d_conv["v"]], axis=1)
    return dict(loss_acc=loss_acc, dx=dx, g1=dg1, g2=dg2, gdn_ng=d_gdn_ng, qn=d_qn_g, kn=d_kn_g, p1=d_p1, p2=d_p2,
                conv=d_conv_w, w_main=dw_main, w_small=dw_small, p_a=dp_a, p_b=dp_b, w_o=dw_o, w_u=dw_u, w_d=dw_d)


_W = NH * LANES
_A0, _A1 = 4 * _W, 4 * _W + 2 * NH
_B0, _B1 = _A1 + 3 * _W, _A1 + 3 * _W + NH
N_IN = _B1 + 2 * _W


def _split_w_in(full_t):
    main = jnp.concatenate([full_t[:_A0], full_t[_A1:_B0], full_t[_B1:]], axis=0)
    small = jnp.concatenate([full_t[_A0:_A1], full_t[_B0:_B1], jnp.zeros((LANES - 3 * NH, full_t.shape[1]), full_t.dtype)], axis=0)
    return main, small


def _join_w_in(main, small):
    return jnp.concatenate([main[:_A0], small[:2 * NH], main[_A0:_A0 + 3 * _W], small[2 * NH:3 * NH], main[_A0 + 3 * _W:]], axis=0)


def _lanes(v, at=0):
    return jnp.pad(v.reshape(1, -1), ((0, 0), (at, LANES - at - v.size)))


def kernel(x, norm_mix_g, w_in, gdn_conv_w, gdn_a_log, gdn_dt_bias, gdn_norm_g, fox_q_norm_g, fox_k_norm_g, fox_f_bias, w_proj_gdn, w_proj_fox, w_out, norm_mlp_g, w_up, w_down, loss_target, m_norm_mix_g, m_w_in, m_gdn_conv_w, m_gdn_a_log, m_gdn_dt_bias, m_gdn_norm_g, m_fox_q_norm_g, m_fox_k_norm_g, m_fox_f_bias, m_w_proj_gdn, m_w_proj_fox, m_w_out, m_norm_mlp_g, m_w_up, m_w_down, v_norm_mix_g, v_w_in, v_gdn_conv_w, v_gdn_a_log, v_gdn_dt_bias, v_gdn_norm_g, v_fox_q_norm_g, v_fox_k_norm_g, v_fox_f_bias, v_w_proj_gdn, v_w_proj_fox, v_w_out, v_norm_mlp_g, v_w_up, v_w_down):
    nseq, seq, dm = x.shape
    rows = nseq * seq
    xi, yi, ci = lax.axis_index("x"), lax.axis_index("y"), lax.axis_index("c")
    chip = 2 * xi + yi
    conv_cols = gdn_conv_w.shape[2]

    tr = lambda a: jnp.swapaxes(a[0], 0, 1)
    big = [tr(w_in), w_proj_gdn[0], w_proj_fox[0], w_out[0], w_up[0], w_down[0]]
    axes = [1, 0, 0, 0, 0, 0]
    big16 = [w.astype(BF16) for w in big]
    conv_slot = jnp.zeros((4, 4, conv_cols), F32).at[:, chip].set(jnp.where(ci == 0, gdn_conv_w[0], 0.0))
    conv_full = all_reduce_small("gather_conv", conv_slot.reshape(-1, LANES)).reshape(4, 4 * conv_cols)
    got_in, = gather_weights(big16[:1], axes[:1])
    wt_main, wt_small = _split_w_in(got_in.reshape(-1, dm))
    core, chip_no = ci.reshape(1).astype(jnp.int32), chip.reshape(1).astype(jnp.int32)
    gather = split_gather(big16[1:])
    token = gather.start([got_in, conv_full])

    class Hooks:
        def late_weights(self, after):
            g_pa, g_pb, g_wo, w_u, g_wd = gather.wait(after)
            return (*(g.reshape(-1, dm) for g in (g_pa, g_pb, g_wo)), w_u, g_wd.reshape(-1, dm))

        def reduce_start(self, grads):
            blocks = [grads["p_a"].reshape(4, -1, dm), grads["p_b"].reshape(4, -1, dm), grads["w_o"].reshape(4, -1, dm),
                      grads["w_u"], grads["w_d"].reshape(4, -1, dm)]
            self.swap = split_pair_swap("pair_swap_late", blocks, axes[1:])
            return self.swap.start([])

        def reduce_exchange(self, after):
            swapped = self.swap.wait(after)
            self.exchange = split_chip_exchange("chip_exchange_late", add_pair("add_pair_late", self.swap.srcs, swapped, core, axes[1:]))
            return self.exchange.start([])

        def reduce_finish(self, after):
            slots = self.exchange.wait(after)
            self.send = split_pair_send(add_chips("add_chips_late", slots, self.exchange.srcs, chip_no, axes[1:]))
            return self.send.start([])

        def input_grad_start(self, dw_main, dw_small):
            self.in_swap = split_pair_swap("pair_swap_in", [_join_w_in(dw_main, dw_small).reshape(4, -1, dm)], axes[:1])
            return self.in_swap.start([])

        def input_grad_exchange(self, after):
            swapped = self.in_swap.wait(after)
            self.in_exchange = split_chip_exchange("chip_exchange_in", add_pair("add_pair_in", self.in_swap.srcs, swapped, core, axes[:1]))
            return self.in_exchange.start([])

    hooks = Hooks()
    p1 = _lanes(gdn_dt_bias[0]) + _lanes(fox_f_bias[0], 2 * NH)
    p2 = _lanes(gdn_a_log[0])

    g = local_step(x.reshape(rows, dm), loss_target.reshape(rows, dm), norm_mix_g + token[0, 0], norm_mlp_g, gdn_norm_g,
                   fox_q_norm_g, fox_k_norm_g, p1, p2, conv_full, wt_main, wt_small, hooks, nseq, seq)

    others = hooks.send.wait(g["dx"])
    big_m = [tr(m_w_in), m_w_proj_gdn[0], m_w_proj_fox[0], m_w_out[0], m_w_up[0], m_w_down[0]]
    big_v = [tr(v_w_in), v_w_proj_gdn[0], v_w_proj_fox[0], v_w_out[0], v_w_up[0], v_w_down[0]]
    names = ["w_in", "w_proj_gdn", "w_proj_fox", "w_out", "w_up", "w_down"]
    big_res, big_grad = {}, {}
    for i in range(1, len(names)):
        big_grad[names[i]], *big_res[names[i]] = adamw_halves(f"adamw_{names[i]}", big[i], hooks.send.srcs[i - 1], others[i - 1],
                                                              big_m[i], big_v[i], core, axes[i])
    slots = hooks.in_exchange.wait(big_res[names[-1]][0])
    mine = add_chips("add_chips_in", slots, hooks.in_exchange.srcs, chip_no, axes[:1])
    res = adamw_halves("adamw_w_in", big[0], mine[0], pair_send(mine)[0], big_m[0], big_v[0], core, axes[0])
    big_grad["w_in"], *big_res["w_in"] = [jnp.swapaxes(r, 0, 1) for r in res]

    small_parts = [g["loss_acc"], g["g1"].reshape(8, LANES), g["g2"].reshape(8, LANES), g["gdn_ng"], g["qn"], g["kn"], g["p1"], g["p2"],
                   g["conv"].reshape(-1, LANES)]
    tiled = [jnp.pad(p, ((0, -p.shape[0] % 8), (0, 0))) for p in small_parts]
    red = all_reduce_small("reduce_small", jnp.concatenate(tiled, axis=0), slots)
    pos, red_parts = 0, []
    for p, t in zip(small_parts, tiled):
        red_parts.append(red[pos:pos + p.shape[0]])
        pos += t.shape[0]
    r_loss, r_g1, r_g2, r_gdn_ng, r_qn, r_kn, r_p1, r_p2, r_conv = red_parts
    loss = jnp.sum(r_loss)
    g_conv = lax.dynamic_slice_in_dim(r_conv.reshape(4, 4, conv_cols), chip, 1, axis=1).reshape(4, conv_cols)
    small_grads = [r_g1.reshape(1, dm), r_p2[:, :NH], r_p1[:, :NH], r_gdn_ng, r_qn, r_kn, r_p1[:, 2 * NH:3 * NH], r_g2.reshape(1, dm)]
    small_w = [norm_mix_g, gdn_a_log, gdn_dt_bias, gdn_norm_g, fox_q_norm_g, fox_k_norm_g, fox_f_bias, norm_mlp_g]
    small_m = [m_norm_mix_g, m_gdn_a_log, m_gdn_dt_bias, m_gdn_norm_g, m_fox_q_norm_g, m_fox_k_norm_g, m_fox_f_bias, m_norm_mlp_g]
    small_v = [v_norm_mix_g, v_gdn_a_log, v_gdn_dt_bias, v_gdn_norm_g, v_fox_q_norm_g, v_fox_k_norm_g, v_fox_f_bias, v_norm_mlp_g]

    def pack(parts):
        flat = jnp.concatenate([jnp.pad(p.reshape(-1), (0, -p.size % LANES)) for p in parts])
        return jnp.pad(flat, (0, -flat.size % (8 * LANES))).reshape(-1, LANES)

    packed = adamw("adamw_small", pack(small_w + [gdn_conv_w[0]]), pack(small_grads + [g_conv]),
                   pack(small_m + [m_gdn_conv_w[0]]), pack(small_v + [v_gdn_conv_w[0]]))

    def unpack(flat2d):
        flat, pos, res = flat2d.reshape(-1), 0, []
        for p in small_w + [gdn_conv_w[0]]:
            res.append(flat[pos:pos + p.size].reshape(p.shape))
            pos += p.size + (-p.size % LANES)
        return res

    s_delta, s_m, s_v = (unpack(a) for a in packed)

    order = ["norm_mix_g", "w_in", "gdn_conv_w", "gdn_a_log", "gdn_dt_bias", "gdn_norm_g", "fox_q_norm_g", "fox_k_norm_g",
             "fox_f_bias", "w_proj_gdn", "w_proj_fox", "w_out", "norm_mlp_g", "w_up", "w_down"]
    small_names = ["norm_mix_g", "gdn_a_log", "gdn_dt_bias", "gdn_norm_g", "fox_q_norm_g", "fox_k_norm_g", "fox_f_bias", "norm_mlp_g",
                   "gdn_conv_w"]
    small_idx = {nm: i for i, nm in enumerate(small_names)}
    shapes = dict(zip(order, (a.shape for a in (norm_mix_g, w_in, gdn_conv_w, gdn_a_log, gdn_dt_bias, gdn_norm_g, fox_q_norm_g,
                                                 fox_k_norm_g, fox_f_bias, w_proj_gdn, w_proj_fox, w_out, norm_mlp_g, w_up, w_down))))
    grads_out, delta_out, m_out, v_out = [], [], [], []
    for nm in order:
        if nm in big_res:
            d, mm, vv = big_res[nm]
            gr = big_grad[nm]
        else:
            i = small_idx[nm]
            gr = (small_grads + [g_conv])[i]
            d, mm, vv = s_delta[i], s_m[i], s_v[i]
        for lst, val in ((grads_out, gr), (delta_out, d), (m_out, mm), (v_out, vv)):
            lst.append(val.reshape(shapes[nm]))
    return (loss, g["dx"].reshape(x.shape), *grads_out, *delta_out, *m_out, *v_out)
```

```python
import functools

import jax
import jax.numpy as jnp
from jax import lax
from jax.experimental import pallas as pl
from jax.experimental.pallas import tpu as pltpu

F32 = jnp.float32
BF16 = jnp.bfloat16
LANES = 128
NH = 8
EPS = 1e-6
GDN_CHUNK = 64
GDN_ROWS = 256
GDN_BASE = 16
ROW_TILE = 512
CONV_HEADS = 2
ATT_TILE = 256
NEG = -1e30
VMEM_LIMIT_BYTES = 48 * 1024 * 1024
HI = lax.Precision.HIGHEST
LO = lax.Precision.DEFAULT
MESH = pl.DeviceIdType.MESH
ANY = pl.BlockSpec(memory_space=pl.ANY)

ADAM_LR, ADAM_B1, ADAM_B2, ADAM_EPS, ADAM_WD, ADAM_STEP = 0.001, 0.9, 0.999, 1e-08, 0.01, 10


def _params(n_grid):
    return pltpu.CompilerParams(dimension_semantics=("arbitrary",) * n_grid,
                                vmem_limit_bytes=VMEM_LIMIT_BYTES)


def _dot(a, b, dims, precision=None):
    dn = {"nn": (((1,), (0,)), ((), ())), "nt": (((1,), (1,)), ((), ())), "tn": (((0,), (0,)), ((), ()))}[dims]
    return lax.dot_general(a, b, dn, precision=precision, preferred_element_type=F32)


def _iota(shape, dim):
    return lax.broadcasted_iota(jnp.int32, shape, dim)


def _split(x, parts):
    out = []
    for _ in range(parts - 1):
        hi = x.astype(BF16)
        out.append(hi)
        x = x - hi.astype(F32)
    return out + [x.astype(BF16)]


def _dot_mask(mask, b, dims):
    m16 = mask.astype(BF16)
    b1, b2, b3 = _split(b, 3)
    return _dot(m16, b1, dims) + (_dot(m16, b2, dims) + _dot(m16, b3, dims))


@jax.custom_vjp
def mm_mask(mask, b):
    return _dot_mask(mask, b, "nn")


mm_mask.defvjp(lambda mask, b: (_dot_mask(mask, b, "nn"), mask),
               lambda mask, g: (jnp.zeros_like(mask), _dot_mask(mask, g, "tn")))


def matmul(name, a, b, dims, out_dtype, add=None, tm=1024, tn=1024, tk=512, col_blocks=None,
           extras=(), epilogue=None, out_dtypes=None, k_part=None, after=(), tile_sums=False):
    if col_blocks and dims != "tn":
        nb, b_rows, bw = b.shape
        b_shape = (b_rows, nb * bw)
    else:
        b_shape = b.shape
    if dims == "nn":
        (m, k), (_, n) = a.shape, b_shape
    elif dims == "nt":
        (m, k), (n, _) = a.shape, b_shape
    else:
        (k, m), (_, n) = a.shape, b_shape
    k_span = k // (k_part[1] if k_part else 1)
    if col_blocks and dims == "nt":
        k_span = min(k_span, bw)
    tk = k if k <= 1024 else max(t for t in (2048, 1536, 1024, 512, tk) if k_span % t == 0)
    tm, tn, tk = min(tm, m), min(tn, n), min(tk, k)
    assert m % tm == 0 and n % tn == 0 and k % tk == 0, (name, m, n, k)
    k0, nk = (0, k // tk) if k_part is None else (k_part[0] * (k // tk // k_part[1]), k // tk // k_part[1])
    assert k_part is None or (dims == "nn" and not col_blocks and (k // tk) % k_part[1] == 0)
    a_spec = pl.BlockSpec((tk, tm), lambda i, j, kk: (kk, i)) if dims == "tn" else pl.BlockSpec((tm, tk), lambda i, j, kk: (i, kk + k0))
    b_spec = pl.BlockSpec((tn, tk), lambda i, j, kk: (j, kk)) if dims == "nt" else pl.BlockSpec((tk, tn), lambda i, j, kk: (kk + k0, j))
    o_spec = pl.BlockSpec((tm, tn), lambda i, j, kk: (i, j))
    out_shape = (m, n)
    if col_blocks and dims == "nn":
        per = bw // tn
        assert bw % tn == 0
        b_spec = pl.BlockSpec((None, tk, tn), lambda i, j, kk: (j // per, kk, j % per))
    elif col_blocks and dims == "nt":
        per = bw // tk
        assert bw % tk == 0
        b_spec = pl.BlockSpec((None, tn, tk), lambda i, j, kk: (kk // per, j, kk % per))
    elif col_blocks:
        bw = n // col_blocks
        per = bw // tn
        assert bw % tn == 0 and add is None
        o_spec = pl.BlockSpec((None, tm, tn), lambda i, j, kk: (j // per, i, j % per))
        out_shape = (col_blocks, m, bw)
    extras = list(extras) + ([add] if add is not None else [])
    if add is not None:
        assert epilogue is None
        epilogue = lambda r, *e: [r + e[-1]]
    out_dtypes = [out_dtype] if epilogue is None or out_dtypes is None else list(out_dtypes)
    n_ex, n_out = len(extras), len(out_dtypes)

    def body(*refs):
        a_ref, b_ref = refs[0], refs[1]
        ex_refs, o_refs = refs[2:2 + n_ex], refs[2 + n_ex + len(after):2 + n_ex + len(after) + n_out]

        def finish(r):
            res = [r] if epilogue is None else epilogue(r, *[e[...] for e in ex_refs])
            for o_ref, v in zip(o_refs, res):
                o_ref[...] = v.astype(o_ref.dtype)

        if nk == 1:
            finish(_dot(a_ref[...], b_ref[...], dims))
            return
        acc_ref = refs[-1]
        kk = pl.program_id(2)

        @pl.when(kk == 0)
        def _():
            acc_ref[...] = jnp.zeros_like(acc_ref)

        acc_ref[...] += _dot(a_ref[...], b_ref[...], dims)

        @pl.when(kk == nk - 1)
        def _():
            finish(acc_ref[...])

    out_specs = [o_spec] * n_out
    out_shapes = [jax.ShapeDtypeStruct(out_shape, dt) for dt in out_dtypes]
    if tile_sums:
        out_specs[-1] = pl.BlockSpec((8, LANES), lambda i, j, kk: (i, j))
        out_shapes[-1] = jax.ShapeDtypeStruct((8 * (m // tm), LANES * (n // tn)), out_dtypes[-1])
    res = pl.pallas_call(
        body, name=name, grid=(m // tm, n // tn, nk), in_specs=[a_spec, b_spec] + [o_spec] * n_ex + [ANY] * len(after),
        out_specs=out_specs, out_shape=out_shapes,
        scratch_shapes=[pltpu.VMEM((tm, tn), F32)] if nk > 1 else [], compiler_params=_params(3),
    )(a, b, *extras, *after)
    return res[0] if n_out == 1 else res


def _ew_spec(kind, off, width, tb, hp, order, shape=None):
    def ih(g0, g1):
        return (g0, g1) if order == "ih" else (g1, g0)

    assert off % hp == 0 or kind in ("row", "par")
    if kind == "row":
        return pl.BlockSpec((tb, width), lambda g0, g1: (ih(g0, g1)[0], off))
    if kind == "rowh":
        return pl.BlockSpec((tb, hp * width), lambda g0, g1: (ih(g0, g1)[0], ih(g0, g1)[1] + off // hp))
    if kind == "par":
        return pl.BlockSpec(shape, lambda g0, g1: (0, 0))
    if kind == "parh":
        return pl.BlockSpec((shape[0], hp * width), lambda g0, g1: (0, ih(g0, g1)[1] + off // hp))
    raise ValueError(kind)


def _ew_grid(rows, tb, nh, hp, order):
    assert nh % hp == 0 and rows % tb == 0
    return (rows // tb, nh // hp) if order == "ih" else (nh // hp, rows // tb)


def _ew_load(ref, kind, width, hh):
    if kind in ("row", "par"):
        return ref[...].astype(F32)
    return ref[:, hh * width:(hh + 1) * width].astype(F32)


def ew_fwd(name, f, ins, outs, rows, nh=1, tb=ROW_TILE, order="ih", hp=None, after=()):
    hp = nh if hp is None else hp
    n_in = len(ins)

    def body(*refs):
        hb = pl.program_id(1) if order == "ih" else pl.program_id(0)
        for hh in range(hp):
            h = hh if hp == nh else hb * hp + hh
            vals = [_ew_load(r, kd, w, hh) for r, (_, kd, _, w) in zip(refs[:n_in], ins)]
            res = f(h, *vals)
            for r, v, (_, kd, w, _) in zip(refs[n_in + len(after):], res, outs):
                if kd == "row":
                    assert hp == 1
                    r[...] = v.astype(r.dtype)
                else:
                    r[:, hh * w:(hh + 1) * w] = v.astype(r.dtype)

    in_specs = [_ew_spec(kd, off, w, tb, hp, order, a.shape) for (a, kd, off, w) in ins]
    out_specs = [_ew_spec(kd, 0, w, tb, hp, order) for (_, kd, w, _) in outs]
    out_shape = [jax.ShapeDtypeStruct((rows, tw), dt) for (tw, _, _, dt) in outs]
    return pl.pallas_call(
        body, name=name, grid=_ew_grid(rows, tb, nh, hp, order), in_specs=in_specs + [ANY] * len(after), out_specs=out_specs,
        out_shape=out_shape, compiler_params=_params(2),
    )(*[a for (a, _, _, _) in ins], *after)


def ew_bwd(name, f, ins, cts, extras, emit, outs, rows, nh=1, tb=ROW_TILE, order="ih", hp=None):
    hp = nh if hp is None else hp
    n_in = len(ins)
    flat_cts = [d for group in cts for d in group]
    n_ct, n_ex = len(flat_cts), len(extras)

    def body(*refs):
        g0, g1 = pl.program_id(0), pl.program_id(1)
        hb = g1 if order == "ih" else g0
        out_refs = refs[n_in + n_ct + n_ex:]
        shared = [None] * len(outs)

        def store(r, v, first, sl=None):
            def put(val, add):
                if sl is None:
                    r[...] = (r[...] + val if add else val).astype(r.dtype)
                else:
                    r[:, sl] = (r[:, sl] + val if add else val).astype(r.dtype)

            if first is None:
                put(v, False)
            else:
                pl.when(first)(lambda: put(v, False))
                pl.when(jnp.logical_not(first))(lambda: put(v, True))

        for hh in range(hp):
            h = hh if hp == nh else hb * hp + hh
            vals = [_ew_load(r, kd, w, hh) for r, (_, kd, _, w) in zip(refs[:n_in], ins)]
            ct_refs = list(zip(refs[n_in:n_in + n_ct], flat_cts))
            ct_vals, pos = [], 0
            for group in cts:
                v = None
                for r, (_, kd, _, w) in ct_refs[pos:pos + len(group)]:
                    t = _ew_load(r, kd, w, hh)
                    v = t if v is None else v + t
                pos += len(group)
                ct_vals.append(v)
            ex_vals = [_ew_load(r, kd, w, hh) for r, (_, kd, _, w) in zip(refs[n_in + n_ct:n_in + n_ct + n_ex], extras)]
            _, vjp = jax.vjp(lambda *a: f(h, *a), *vals)
            res = emit(vjp(tuple(ct_vals)), ex_vals)
            for idx, (r, v, (_, kd, w, _, acc)) in enumerate(zip(out_refs, res, outs)):
                if kd in ("row", "par"):
                    shared[idx] = v if shared[idx] is None else shared[idx] + v
                else:
                    store(r, v, (g1 == 0) if acc == "inner" else None, slice(hh * w, (hh + 1) * w))
        for idx, (r, (_, kd, _, _, acc)) in enumerate(zip(out_refs, outs)):
            if kd in ("row", "par"):
                assert acc == "all" or hp == nh
                store(r, shared[idx], jnp.logical_and(g0 == 0, g1 == 0) if acc == "all" else None)

    operands = list(ins) + flat_cts + list(extras)
    in_specs = [_ew_spec(kd, off, w, tb, hp, order, a.shape) for (a, kd, off, w) in operands]
    out_specs = [_ew_spec(kd, 0, w, tb, hp, order, shp) for (shp, kd, w, _, _) in outs]
    out_shape = [jax.ShapeDtypeStruct(shp, dt) for (shp, _, _, dt, _) in outs]
    return pl.pallas_call(
        body, name=name, grid=_ew_grid(rows, tb, nh, hp, order), in_specs=in_specs, out_specs=out_specs,
        out_shape=out_shape, compiler_params=_params(2),
    )(*[a for (a, _, _, _) in operands])


def f_rms(h, x, g):
    r = lax.rsqrt(jnp.mean(x * x, axis=-1, keepdims=True) + EPS)
    return (x * r * g,)


def _softplus(z):
    return jnp.maximum(z, 0.0) + jnp.log1p(jnp.exp(-jnp.abs(z)))


def f_small(h, sp, p1, p2):
    lane = _iota(sp.shape, 1)
    z = sp + p1
    g = -jnp.exp(p2) * _softplus(z)
    beta = jax.nn.sigmoid(z)
    logf = -_softplus(-z)
    return (jnp.where(lane < NH, g, jnp.where(lane < 2 * NH, beta, jnp.where(lane < 3 * NH, logf, 0.0))),)


def _pick(x, lane_id):
    lane = _iota(x.shape, 1)
    col = jnp.sum(jnp.where(lane == lane_id, x, 0.0), axis=1, keepdims=True)
    return jnp.broadcast_to(col, x.shape)


def f_bcast(h, so, cs):
    return _pick(so, h), _pick(so, h + NH), _pick(cs, h + 2 * NH)


def _shift_down(s):
    def down(x):
        return jnp.where(_iota(x.shape, 0) >= s, pltpu.roll(x, s, 0), 0.0)

    def up(g):
        n = g.shape[0]
        return jnp.where(_iota(g.shape, 0) < n - s, pltpu.roll(g, n - s, 0), 0.0)

    @jax.custom_vjp
    def shift(x):
        return down(x)

    shift.defvjp(lambda x: (down(x), None), lambda _, g: (up(g),))
    return shift


def _silu(x):
    return x * jax.nn.sigmoid(x)


def make_f_conv(mode):
    sh1, sh2, sh3 = _shift_down(1), _shift_down(2), _shift_down(3)

    def f(h, x, w):
        sub = _iota(w.shape, 0)

        def tap(i):
            return jnp.sum(jnp.where(sub == i, w, 0.0), axis=0, keepdims=True)

        y = sh3(x) * tap(0)
        y = y + sh2(x) * tap(1)
        y = y + sh1(x) * tap(2)
        y = y + x * tap(3)
        s = _silu(y)
        if mode == "v":
            return (s,)
        n = s * lax.rsqrt(jnp.sum(s * s, axis=-1, keepdims=True) + EPS)
        if mode == "q":
            n = n * (LANES ** -0.5)
        return (n,)

    return f


def f_post(h, o, z, g):
    r = lax.rsqrt(jnp.mean(o * o, axis=-1, keepdims=True) + EPS)
    return (o * r * g * _silu(z),)


def f_merge(h, ga, gb, ya, yb):
    return (jax.nn.sigmoid(ga) * ya + jax.nn.sigmoid(gb) * yb,)


def f_delta(h, do, o):
    return (jnp.broadcast_to(jnp.sum(do * o, axis=1, keepdims=True), o.shape),)


def cumsum_time(name, x, nseq, seq, reverse):
    nb = seq // LANES

    def body(x_ref, o_ref):
        r, c = _iota((LANES, LANES), 0), _iota((LANES, LANES), 1)
        tri = jnp.where((r <= c) if reverse else (r >= c), 1.0, 0.0).astype(F32)
        carry = jnp.zeros((1, LANES), F32)
        for b in (range(nb - 1, -1, -1) if reverse else range(nb)):
            blk = x_ref[b * LANES:(b + 1) * LANES, :]
            o_ref[b * LANES:(b + 1) * LANES, :] = _dot_mask(tri, blk, "nn") + carry
            carry = carry + jnp.sum(blk, axis=0, keepdims=True)

    spec = pl.BlockSpec((seq, LANES), lambda s: (s, 0))
    return pl.pallas_call(body, name=name, grid=(nseq,), in_specs=[spec], out_specs=spec,
                          out_shape=jax.ShapeDtypeStruct(x.shape, F32), compiler_params=_params(1))(x)


def transpose_time(name, x, nseq, seq):
    def body(x_ref, o_ref):
        o_ref[...] = x_ref[...].T

    return pl.pallas_call(
        body, name=name, grid=(nseq,), in_specs=[pl.BlockSpec((seq, LANES), lambda s: (s, 0))],
        out_specs=pl.BlockSpec((LANES, seq), lambda s: (s, 0)),
        out_shape=jax.ShapeDtypeStruct((nseq * LANES, seq), F32), compiler_params=_params(1))(x)


def _gdn_masks():
    n = GDN_ROWS
    r, c = _iota((n, n), 0), _iota((n, n), 1)
    shift = GDN_CHUNK.bit_length() - 1
    same = lax.shift_right_logical(r, shift) == lax.shift_right_logical(c, shift)
    return r, c, same


def _each(fn, *lists):
    return [fn(*xs) for xs in zip(*lists)]


def _gdn_decay(gbs):
    r, c, same = _gdn_masks()
    seg_tril = jnp.where(jnp.logical_and(same, r >= c), 1.0, 0.0).astype(F32)
    g_cum = _each(lambda gb: mm_mask(seg_tril, gb), gbs)
    lane0 = _iota(gbs[0].shape, 1) == 0
    g_col = _each(lambda g: jnp.sum(jnp.where(lane0, g, 0.0), axis=1, keepdims=True), g_cum)
    g_row = _each(lambda g: jnp.sum(jnp.where(r == c, jnp.broadcast_to(g, (GDN_ROWS, GDN_ROWS)), 0.0), axis=0, keepdims=True), g_col)
    return g_cum, _each(lambda a, b: a - b, g_col, g_row)


def gdn_f1(*args):
    qs, ks, gbs, bbs = (list(args[i::4]) for i in range(4))
    r, c, same = _gdn_masks()
    strict = jnp.logical_and(same, r > c)
    _, diff = _gdn_decay(gbs)
    lane0 = _iota(bbs[0].shape, 1) == 0
    beta_col = _each(lambda bb: jnp.sum(jnp.where(lane0, bb, 0.0), axis=1, keepdims=True), bbs)
    kk = _each(lambda k: _dot(k, k, "nt", LO), ks)
    return tuple(_each(lambda b, x, d: jnp.where(strict, b * x * jnp.exp(jnp.where(strict, d, 0.0)), 0.0), beta_col, kk, diff))


def gdn_f2(*args):
    ts, qs, ks, vs, gbs, bbs = (list(args[i::6]) for i in range(6))
    r, c, same = _gdn_masks()
    incl = jnp.logical_and(same, r >= c)
    g_cum, diff = _gdn_decay(gbs)
    decay = _each(lambda d: jnp.where(incl, jnp.exp(jnp.where(incl, d, 0.0)), 0.0), diff)
    e_g = _each(jnp.exp, g_cum)
    v_beta = _each(lambda v, bb: v * bb, vs, bbs)
    k_beta = _each(lambda k, bb, e: k * bb * e, ks, bbs, e_g)
    value = _each(lambda t, x: x + _dot(t, x, "nn", LO), ts, v_beta)
    k_cum = _each(lambda t, x: x + _dot(t, x, "nn", LO), ts, k_beta)
    attn = _each(lambda q, k, d: _dot(q, k, "nt", LO) * d, qs, ks, decay)
    ones = jnp.where(same, 1.0, 0.0).astype(F32)
    g_last = _each(lambda gb: mm_mask(ones, gb), gbs)
    q_dec = _each(lambda q, e: q * e, qs, e_g)
    k_dec = _each(lambda k, gl, g: k * jnp.exp(gl - g), ks, g_last, g_cum)
    return tuple(x for head in zip(value, k_cum, attn, q_dec, k_dec) for x in head)


def tri_inverse(mats):
    n = GDN_ROWS
    r, c = _iota((n, n), 0), _iota((n, n), 1)
    shift = GDN_BASE.bit_length() - 1
    blk = lax.shift_right_logical(r, shift) == lax.shift_right_logical(c, shift)
    each = lambda fn, *lists: [fn(*xs) for xs in zip(*lists)]
    mm = lambda x, y: _dot(x, y, "nn", LO)
    d = each(lambda a: jnp.where(blk, a, 0.0), mats)
    lo = each(lambda a, dd: a - dd, mats, d)
    p = each(lambda dd: -dd, d)
    c_d = p
    for _ in range(shift - 1):
        p = each(mm, p, p)
        c_d = each(lambda cd, pp, prod: cd + pp + prod, c_d, p, each(mm, c_d, p))
    assert GDN_CHUNK // GDN_BASE == 4
    nmat = each(lambda l, prod: l + prod, lo, each(mm, c_d, lo))
    n2 = each(mm, nmat, nmat)
    c_n = each(lambda nn2, nm, prod: (nn2 - nm) - prod, n2, nmat, each(mm, nmat, n2))
    return each(lambda cn, cd, prod: cn + cd + prod, c_n, c_d, each(mm, c_n, c_d))


GDN_AHP = 4


def _gdn_a_specs():
    blk = pl.BlockSpec((GDN_ROWS, GDN_AHP * LANES), lambda i, h: (i, h))
    sq = pl.BlockSpec((GDN_ROWS, GDN_AHP * GDN_ROWS), lambda i, h: (i, h))
    return blk, sq


def _head(ref, hh):
    width = ref.shape[1] // GDN_AHP
    return ref.at[:, hh * width:(hh + 1) * width]


def gdn_a_fwd(q, k, v, gb, bb, rows):
    blk, sq = _gdn_a_specs()

    def body(q_ref, k_ref, v_ref, gb_ref, bb_ref, val_ref, kc_ref, at_ref, qd_ref, kd_ref, t_ref):
        heads = [[_head(r, hh)[...] for r in (q_ref, k_ref, v_ref, gb_ref, bb_ref)] for hh in range(GDN_AHP)]
        t_corr = tri_inverse(list(gdn_f1(*[x for qv, kv, vv, gv, bv in heads for x in (qv, kv, gv, bv)])))
        res = gdn_f2(*[x for t, head in zip(t_corr, heads) for x in (t, *head)])
        for hh in range(GDN_AHP):
            for r, x in zip((val_ref, kc_ref, at_ref, qd_ref, kd_ref, t_ref), (*res[5 * hh:5 * hh + 5], t_corr[hh])):
                _head(r, hh)[...] = x.astype(r.dtype)

    wide = lambda dt: jax.ShapeDtypeStruct((rows, NH * LANES), dt)
    square = jax.ShapeDtypeStruct((rows, NH * GDN_ROWS), BF16)
    return pl.pallas_call(
        body, name="gdn_a_fwd", grid=(rows // GDN_ROWS, NH // GDN_AHP), in_specs=[blk] * 5,
        out_specs=[blk, blk, sq, blk, blk, sq], out_shape=[wide(F32), wide(BF16), square, wide(BF16), wide(BF16), square],
        compiler_params=_params(2))(q, k, v, gb, bb)


def gdn_a_bwd(q, k, v, gb, bb, t_inv, dval, dkc, dat, dqd, dkd, dgb_b, rows):
    blk, sq = _gdn_a_specs()

    def body(q_ref, k_ref, v_ref, gb_ref, bb_ref, t_ref, dval_ref, dkc_ref, dat_ref, dqd_ref, dkd_ref, dgbb_ref,
             dq_ref, dk_ref, dv_ref, dgb_ref, dbb_ref):
        hs = range(GDN_AHP)
        heads = [[_head(r, hh)[...] for r in (q_ref, k_ref, v_ref, gb_ref, bb_ref)] for hh in hs]
        tvs = [_head(t_ref, hh)[...].astype(F32) for hh in hs]
        _, vjp1 = jax.vjp(gdn_f1, *[x for qv, kv, vv, gv, bv in heads for x in (qv, kv, gv, bv)])
        _, vjp2 = jax.vjp(gdn_f2, *[x for t, head in zip(tvs, heads) for x in (t, *head)])
        g2 = vjp2(tuple(_head(r, hh)[...] for hh in hs for r in (dval_ref, dkc_ref, dat_ref, dqd_ref, dkd_ref)))
        dts = [g2[6 * hh] for hh in hs]
        left = _each(lambda dt, tv: dt + _dot(tv, dt, "tn", LO), dts, tvs)
        g1 = vjp1(tuple(_each(lambda lf, tv: -(lf + _dot(lf, tv, "nt", LO)), left, tvs)))
        for hh in hs:
            dq1, dk1, dgb1, dbb1 = g1[4 * hh:4 * hh + 4]
            _, dq2, dk2, dv2, dgb2, dbb2 = g2[6 * hh:6 * hh + 6]
            _head(dq_ref, hh)[...] = dq1 + dq2
            _head(dk_ref, hh)[...] = dk1 + dk2
            _head(dv_ref, hh)[...] = dv2
            _head(dgb_ref, hh)[...] = dgb1 + dgb2 + _head(dgbb_ref, hh)[...]
            _head(dbb_ref, hh)[...] = dbb1 + dbb2

    wide = jax.ShapeDtypeStruct((rows, NH * LANES), F32)
    return pl.pallas_call(
        body, name="gdn_a_bwd", grid=(rows // GDN_ROWS, NH // GDN_AHP),
        in_specs=[blk] * 5 + [sq, blk, blk, sq, blk, blk, blk], out_specs=[blk] * 5, out_shape=[wide] * 5,
        compiler_params=_params(2))(q, k, v, gb, bb, t_inv, dval, dkc, dat, dqd, dkd, dgb_b)


N_CH = GDN_ROWS // GDN_CHUNK


GDN_HP = 8


def gdn_fb(*args):
    per_head = 6 * N_CH
    states = list(args[GDN_HP * per_head:])
    outs = [[None] * N_CH for _ in range(GDN_HP)]
    zero = jnp.zeros((GDN_CHUNK, LANES), F32)
    for c in range(N_CH):
        for hh in range(GDN_HP):
            val, kc, at, qd, kd, gb = (args[hh * per_head + i * N_CH + c] for i in range(6))
            s = states[hh]
            v_new = val - _dot(kc, s, "nn", LO)
            v_pad = jnp.concatenate([zero] * c + [v_new] + [zero] * (N_CH - 1 - c), axis=0)
            outs[hh][c] = _dot(qd, s, "nn", LO) + _dot(at, v_pad, "nn", LO)
            dec = jnp.exp(jnp.sum(gb, axis=0, keepdims=True))
            states[hh] = s * dec + _dot(kd, v_new, "tn", LO)
    return (*[o for head in outs for o in head], *states)


def _gdn_piece(ref, hh, c):
    width = ref.shape[1] // GDN_HP
    return ref.at[c * GDN_CHUNK:(c + 1) * GDN_CHUNK, hh * width:(hh + 1) * width]


def _gdn_pieces(refs, hh):
    return [_gdn_piece(r, hh, c)[...].astype(F32) for r in refs for c in range(N_CH)]


def _gdn_b_specs(nb, rev):
    def blk_row(s, j):
        return s * nb + (nb - 1 - j if rev else j)

    blk = pl.BlockSpec((GDN_ROWS, GDN_HP * LANES), lambda s, hb, j: (blk_row(s, j), hb))
    sq = pl.BlockSpec((GDN_ROWS, GDN_HP * GDN_ROWS), lambda s, hb, j: (blk_row(s, j), hb))
    snap = pl.BlockSpec((GDN_HP * LANES, LANES), lambda s, hb, j: (blk_row(s, j) * (NH // GDN_HP) + hb, 0))
    return blk, sq, snap


def gdn_b_fwd(val, kc, at, qd, kd, gb, nseq, seq):
    nb = seq // GDN_ROWS
    rows = nseq * seq
    blk, sq, snap = _gdn_b_specs(nb, False)

    def body(val_ref, kc_ref, at_ref, qd_ref, kd_ref, gb_ref, o_ref, snap_ref, s_ref):
        @pl.when(pl.program_id(2) == 0)
        def _():
            s_ref[...] = jnp.zeros_like(s_ref)

        states = [s_ref[hh] for hh in range(GDN_HP)]
        for hh in range(GDN_HP):
            snap_ref[hh * LANES:(hh + 1) * LANES, :] = states[hh]
        pieces = [p for hh in range(GDN_HP) for p in _gdn_pieces([val_ref, kc_ref, at_ref, qd_ref, kd_ref, gb_ref], hh)]
        res = gdn_fb(*pieces, *states)
        for hh in range(GDN_HP):
            for c in range(N_CH):
                _gdn_piece(o_ref, hh, c)[...] = res[hh * N_CH + c]
            s_ref[hh] = res[GDN_HP * N_CH + hh]

    return pl.pallas_call(
        body, name="gdn_b_fwd", grid=(nseq, NH // GDN_HP, nb), in_specs=[blk, blk, sq, blk, blk, blk], out_specs=[blk, snap],
        out_shape=[jax.ShapeDtypeStruct((rows, NH * LANES), F32), jax.ShapeDtypeStruct((nseq * nb * NH * LANES, LANES), F32)],
        scratch_shapes=[pltpu.VMEM((GDN_HP, LANES, LANES), F32)], compiler_params=_params(3))(val, kc, at, qd, kd, gb)


def gdn_b_bwd(val, kc, at, qd, kd, gb, snaps, do, nseq, seq):
    nb = seq // GDN_ROWS
    rows = nseq * seq
    blk, sq, snap = _gdn_b_specs(nb, True)

    def body(val_ref, kc_ref, at_ref, qd_ref, kd_ref, gb_ref, snap_ref, do_ref,
             dval_ref, dkc_ref, dat_ref, dqd_ref, dkd_ref, dgb_ref, ds_ref):
        @pl.when(pl.program_id(2) == 0)
        def _():
            ds_ref[...] = jnp.zeros_like(ds_ref)

        pieces = [p for hh in range(GDN_HP) for p in _gdn_pieces([val_ref, kc_ref, at_ref, qd_ref, kd_ref, gb_ref], hh)]
        states = [snap_ref[hh * LANES:(hh + 1) * LANES, :] for hh in range(GDN_HP)]
        _, vjp = jax.vjp(gdn_fb, *pieces, *states)
        cts = [p for hh in range(GDN_HP) for p in _gdn_pieces([do_ref], hh)] + [ds_ref[hh] for hh in range(GDN_HP)]
        grads = vjp(tuple(cts))
        for hh in range(GDN_HP):
            for i, r in enumerate([dval_ref, dkc_ref, dat_ref, dqd_ref, dkd_ref, dgb_ref]):
                for c in range(N_CH):
                    _gdn_piece(r, hh, c)[...] = grads[hh * 6 * N_CH + i * N_CH + c]
            ds_ref[hh] = grads[GDN_HP * 6 * N_CH + hh]

    wide = jax.ShapeDtypeStruct((rows, NH * LANES), F32)
    square = jax.ShapeDtypeStruct((rows, NH * GDN_ROWS), F32)
    return pl.pallas_call(
        body, name="gdn_b_bwd", grid=(nseq, NH // GDN_HP, nb), in_specs=[blk, blk, sq, blk, blk, blk, snap, blk],
        out_specs=[blk, blk, sq, blk, blk, blk], out_shape=[wide, wide, square, wide, wide, wide],
        scratch_shapes=[pltpu.VMEM((GDN_HP, LANES, LANES), F32)], compiler_params=_params(3))(val, kc, at, qd, kd, gb, snaps, do)


FOX_Q, FOX_K, FOX_V = 4 * NH, 5 * NH, 6 * NH
FOX_SCALE = LANES ** -0.5


def _head_row(ct_ref, h, off, width):
    blk = ct_ref[:, pl.ds(off, width)]
    return jnp.sum(jnp.where(_iota(blk.shape, 0) == h, blk, 0.0), axis=0, keepdims=True)


def _col(x):
    return jnp.max(x, axis=1, keepdims=True)


def _row(x):
    return jnp.max(x.T, axis=0, keepdims=True)


def _causal(shape, q_dim):
    return _iota(shape, q_dim) >= _iota(shape, 1 - q_dim)


FOX_HP = 2


def _fox_specs(seq, tile, n_tiles):
    tblk = pl.BlockSpec((tile, FOX_HP * LANES), lambda s, h, i: (s * n_tiles + i, h))
    vtblk = pl.BlockSpec((tile, FOX_HP * LANES), lambda s, h, i: (s * n_tiles + i, h + FOX_V // FOX_HP))
    full = pl.BlockSpec((seq, FOX_HP * LANES), lambda s, h, i: (s, h))
    vfull = pl.BlockSpec((seq, FOX_HP * LANES), lambda s, h, i: (s, h + FOX_V // FOX_HP))
    ctb = pl.BlockSpec((NH, seq), lambda s, h, i: (s * (LANES // NH) + 2, 0))
    return tblk, vtblk, full, vfull, ctb


def _lanes_of(hh):
    return slice(hh * LANES, (hh + 1) * LANES)


def fox_fwd(qn, kn, proj, ct, nseq, seq):
    tq = tk = min(ATT_TILE, seq)
    nq = seq // tq
    rows = nseq * seq
    qblk, _, full, vfull, ctb = _fox_specs(seq, tq, nq)
    hs = range(FOX_HP)

    def body(q_ref, k_ref, v_ref, ct_ref, o_ref, o16_ref, lse_ref):
        hb, i = pl.program_id(1), pl.program_id(2)
        q = [q_ref[:, _lanes_of(hh)] for hh in hs]

        def step(j, carry, diag):
            m, l, acc = (list(carry[t::3]) for t in range(3))
            off = pl.multiple_of(j * tk, tk)
            k = [k_ref[pl.ds(off, tk), _lanes_of(hh)] for hh in hs]
            v = [v_ref[pl.ds(off, tk), _lanes_of(hh)].astype(BF16) for hh in hs]
            ck = [_head_row(ct_ref, hb * FOX_HP + hh, off, tk) for hh in hs]
            s = _each(lambda qq, kk, cc: _dot(qq, kk, "nt") * FOX_SCALE - cc, q, k, ck)
            if diag:
                s = _each(lambda x: jnp.where(_causal(x.shape, 0), x, NEG), s)
            m_new = _each(lambda mm, x: jnp.maximum(mm, jnp.max(x, axis=1, keepdims=True)), m, s)
            p = _each(lambda x, mm: jnp.exp(x - mm), s, m_new)
            alpha = _each(lambda mo, mn: jnp.exp(mo - mn), m, m_new)
            l = _each(lambda a, ll, pp: a * ll + jnp.sum(pp, axis=1, keepdims=True), alpha, l, p)
            acc = _each(lambda a, ac, pp, vv: a * ac + _dot(pp.astype(BF16), vv, "nn"), alpha, acc, p, v)
            return tuple(x for head in zip(m_new, l, acc) for x in head)

        init = (jnp.full((tq, 1), NEG, F32), jnp.zeros((tq, 1), F32), jnp.zeros((tq, LANES), F32)) * FOX_HP
        res = step(i, lax.fori_loop(0, i, lambda j, c: step(j, c, False), init), True)
        for hh in hs:
            m, l, acc = res[3 * hh:3 * hh + 3]
            o = acc / l
            o_ref[:, _lanes_of(hh)] = o
            o16_ref[:, _lanes_of(hh)] = o.astype(BF16)
            lse_ref[:, _lanes_of(hh)] = jnp.broadcast_to(m + jnp.log(l), (tq, LANES))

    wide = (rows, NH * LANES)
    return pl.pallas_call(
        body, name="fox_fwd", grid=(nseq, NH // FOX_HP, nq), in_specs=[qblk, full, vfull, ctb], out_specs=[qblk] * 3,
        out_shape=[jax.ShapeDtypeStruct(wide, F32), jax.ShapeDtypeStruct(wide, BF16), jax.ShapeDtypeStruct(wide, F32)],
        compiler_params=_params(3))(qn, kn, proj, ct)


def fox_dq(qn, kn, proj, ct, do, lse, delta, nseq, seq):
    tq = tk = min(ATT_TILE, seq)
    nq = seq // tq
    rows = nseq * seq
    qblk, _, full, vfull, ctb = _fox_specs(seq, tq, nq)
    hs = range(FOX_HP)

    def body(q_ref, k_ref, v_ref, ct_ref, do_ref, lse_ref, dl_ref, dq_ref, dc_ref):
        hb, i = pl.program_id(1), pl.program_id(2)
        q = [q_ref[:, _lanes_of(hh)] for hh in hs]
        lse = [_col(lse_ref[:, _lanes_of(hh)]) for hh in hs]
        delta = [_col(dl_ref[:, _lanes_of(hh)]) for hh in hs]
        do16 = [do_ref[:, _lanes_of(hh)].astype(BF16) for hh in hs]

        def step(j, carry, diag):
            dq, dc = (list(carry[t::2]) for t in range(2))
            off = pl.multiple_of(j * tk, tk)
            k = [k_ref[pl.ds(off, tk), _lanes_of(hh)] for hh in hs]
            v = [v_ref[pl.ds(off, tk), _lanes_of(hh)].astype(BF16) for hh in hs]
            ck = [_head_row(ct_ref, hb * FOX_HP + hh, off, tk) for hh in hs]
            p = _each(lambda qq, kk, cc, ll: jnp.exp(_dot(qq, kk, "nt") * FOX_SCALE - cc - ll), q, k, ck, lse)
            if diag:
                p = _each(lambda x: jnp.where(_causal(x.shape, 0), x, 0.0), p)
            dp = _each(lambda d, vv: _dot(d, vv, "nt"), do16, v)
            ds = _each(lambda pp, d, dl: pp * (d - dl), p, dp, delta)
            dq = _each(lambda a, x, kk: a + _dot(x.astype(BF16), kk, "nn"), dq, ds, k)
            dc = _each(lambda a, x: a + jnp.sum(x, axis=1, keepdims=True), dc, ds)
            return tuple(x for head in zip(dq, dc) for x in head)

        init = (jnp.zeros((tq, LANES), F32), jnp.zeros((tq, 1), F32)) * FOX_HP
        res = step(i, lax.fori_loop(0, i, lambda j, c: step(j, c, False), init), True)
        for hh in hs:
            dq_ref[:, _lanes_of(hh)] = res[2 * hh] * FOX_SCALE
            dc_ref[:, _lanes_of(hh)] = jnp.where(_iota((tq, LANES), 1) == 0, res[2 * hh + 1], 0.0)

    wide = jax.ShapeDtypeStruct((rows, NH * LANES), F32)
    return pl.pallas_call(
        body, name="fox_dq", grid=(nseq, NH // FOX_HP, nq), in_specs=[qblk, full, vfull, ctb, qblk, qblk, qblk],
        out_specs=[qblk, qblk], out_shape=[wide, wide], compiler_params=_params(3))(qn, kn, proj, ct, do, lse, delta)


def fox_dkv(qn, kn, proj, cb, do, lse, delta, nseq, seq):
    tq = tk = min(ATT_TILE, seq)
    nq = seq // tq
    rows = nseq * seq
    kblk, vblk, full, _, _ = _fox_specs(seq, tk, nq)
    hs = range(FOX_HP)

    def body(q_ref, k_ref, v_ref, cb_ref, do_ref, lse_ref, dl_ref, dk_ref, dv_ref, dc_ref):
        j = pl.program_id(2)
        k = [k_ref[:, _lanes_of(hh)] for hh in hs]
        v16 = [v_ref[:, _lanes_of(hh)].astype(BF16) for hh in hs]
        ck = [_col(cb_ref[:, _lanes_of(hh)]) for hh in hs]

        def step(i, carry, diag):
            dk, dv, dc = (list(carry[t::3]) for t in range(3))
            off = pl.multiple_of(i * tq, tq)
            q = [q_ref[pl.ds(off, tq), _lanes_of(hh)] for hh in hs]
            do16 = [do_ref[pl.ds(off, tq), _lanes_of(hh)].astype(BF16) for hh in hs]
            lse = [_row(lse_ref[pl.ds(off, tq), _lanes_of(hh)]) for hh in hs]
            delta = [_row(dl_ref[pl.ds(off, tq), _lanes_of(hh)]) for hh in hs]
            p = _each(lambda kk, qq, cc, ll: jnp.exp(_dot(kk, qq, "nt") * FOX_SCALE - cc - ll), k, q, ck, lse)
            if diag:
                p = _each(lambda x: jnp.where(_causal(x.shape, 1), x, 0.0), p)
            dv = _each(lambda a, pp, d: a + _dot(pp.astype(BF16), d, "nn"), dv, p, do16)
            ds = _each(lambda pp, vv, d, dl: pp * (_dot(vv, d, "nt") - dl), p, v16, do16, delta)
            dk = _each(lambda a, x, qq: a + _dot(x.astype(BF16), qq, "nn"), dk, ds, q)
            dc = _each(lambda a, x: a + jnp.sum(x, axis=1, keepdims=True), dc, ds)
            return tuple(x for head in zip(dk, dv, dc) for x in head)

        zero = jnp.zeros((tk, LANES), F32)
        carry = step(j, (zero, zero, jnp.zeros((tk, 1), F32)) * FOX_HP, True)
        res = lax.fori_loop(j + 1, nq, lambda i, c: step(i, c, False), carry)
        for hh in hs:
            dk, dv, dc = res[3 * hh:3 * hh + 3]
            dk_ref[:, _lanes_of(hh)] = dk * FOX_SCALE
            dv_ref[:, _lanes_of(hh)] = dv.astype(BF16)
            dc_ref[:, _lanes_of(hh)] = jnp.where(_iota((tk, LANES), 1) == 0, -dc, 0.0)

    wide = (rows, NH * LANES)
    return pl.pallas_call(
        body, name="fox_dkv", grid=(nseq, NH // FOX_HP, nq), in_specs=[full, kblk, vblk, kblk, full, full, full],
        out_specs=[kblk, kblk, kblk],
        out_shape=[jax.ShapeDtypeStruct(wide, F32), jax.ShapeDtypeStruct(wide, BF16), jax.ShapeDtypeStruct(wide, F32)],
        compiler_params=_params(3))(qn, kn, proj, cb, do, lse, delta)


def _adamw_update(w, g, m, v):
    m_new = ADAM_B1 * m + (1.0 - ADAM_B1) * g
    v_new = ADAM_B2 * v + (1.0 - ADAM_B2) * (g * g)
    m_hat = m_new / (1.0 - ADAM_B1 ** ADAM_STEP)
    v_hat = v_new / (1.0 - ADAM_B2 ** ADAM_STEP)
    return -ADAM_LR * (m_hat / (jnp.sqrt(v_hat) + ADAM_EPS) + ADAM_WD * w), m_new, v_new


def adamw(name, w, g, m, v):
    rows, cols = w.shape
    tb = min(rows, 128)
    assert rows % tb == 0
    blk = pl.BlockSpec((tb, cols), lambda i: (i, 0))

    def body(w_ref, g_ref, m_ref, v_ref, d_ref, mo_ref, vo_ref):
        d_ref[...], mo_ref[...], vo_ref[...] = _adamw_update(w_ref[...], g_ref[...], m_ref[...], v_ref[...])

    shp = jax.ShapeDtypeStruct(w.shape, F32)
    return pl.pallas_call(body, name=name, grid=(rows // tb,), in_specs=[blk] * 4, out_specs=[blk] * 3,
                          out_shape=[shp] * 3, compiler_params=_params(1))(w, g, m, v)


SPLIT_TILE = 128


def _tiled(shape2d, ax, n_lead, index):
    blk = (SPLIT_TILE, shape2d[1]) if ax == 0 else (shape2d[0], SPLIT_TILE)

    def index_map(*args):
        *lead, t = index(*args)
        return (*lead, t, 0) if ax == 0 else (*lead, 0, t)

    return pl.BlockSpec((None,) * n_lead + blk, index_map)


def adamw_halves(name, w, mine, other, m, v, c, ax):
    steps = w.shape[ax] // 2 // SPLIT_TILE
    assert w.shape[ax] == 2 * steps * SPLIT_TILE

    def body(c_ref, w_ref, mine_ref, other_ref, m_ref, v_ref, g_ref, d_ref, mo_ref, vo_ref):
        g = jnp.where(pl.program_id(0) // steps == c_ref[0], mine_ref[...], other_ref[...])
        g_ref[...] = g
        d_ref[...], mo_ref[...], vo_ref[...] = _adamw_update(w_ref[...], g, m_ref[...], v_ref[...])

    blk = _tiled(w.shape, ax, 0, lambda i, c_ref: (i,))
    hblk = _tiled(mine.shape, ax, 0, lambda i, c_ref: (i % steps,))
    grid_spec = pltpu.PrefetchScalarGridSpec(num_scalar_prefetch=1, grid=(2 * steps,),
                                             in_specs=[blk, hblk, hblk, blk, blk], out_specs=[blk] * 4)
    shp = jax.ShapeDtypeStruct(w.shape, F32)
    return pl.pallas_call(body, name=name, grid_spec=grid_spec, out_shape=[shp] * 4,
                          compiler_params=_params(1))(c, w, mine, other, m, v)


def add_chips(name, slots, parts, chip, axes):
    outs = []
    for idx, (x, own, ax) in enumerate(zip(slots, parts, axes)):
        n, shape2d = x.shape[0], x.shape[1:]
        steps = shape2d[ax] // SPLIT_TILE
        assert shape2d[ax] == steps * SPLIT_TILE

        def body(me_ref, *refs, n=n):
            o_ref = refs[n + 1]
            acc = None
            for t in range(n):
                term = jnp.where(me_ref[0] == t, refs[n][...], refs[t][...]).astype(F32)
                acc = term if acc is None else acc + term
            o_ref[...] = acc

        def filled(t, n=n):
            return lambda i, me_ref: (jnp.where(me_ref[0] == t, (t + 1) % n, t), i)

        grid_spec = pltpu.PrefetchScalarGridSpec(
            num_scalar_prefetch=1, grid=(steps,),
            in_specs=[_tiled(shape2d, ax, 1, filled(t)) for t in range(n)]
            + [_tiled(shape2d, ax, 1, lambda i, me_ref: (me_ref[0], i))],
            out_specs=_tiled(shape2d, ax, 0, lambda i, me_ref: (i,)))
        outs.append(pl.pallas_call(
            body, name=f"{name}_{idx}", grid_spec=grid_spec, out_shape=jax.ShapeDtypeStruct(shape2d, F32),
            compiler_params=_params(1))(chip, *([x] * n), own))
    return outs


def add_pair(name, gs, rs, c, axes):
    outs = []
    for idx, (g, r, ax) in enumerate(zip(gs, rs, axes)):
        nb = r.shape[0]
        steps = r.shape[1 + ax] // SPLIT_TILE
        assert r.shape[1 + ax] == steps * SPLIT_TILE

        def body(c_ref, g_ref, r_ref, o_ref):
            o_ref[...] = (g_ref[...] + r_ref[...]).astype(BF16)

        grid_spec = pltpu.PrefetchScalarGridSpec(
            num_scalar_prefetch=1, grid=(nb, steps),
            in_specs=[_tiled(g.shape[1:], ax, 1, lambda b, i, c_ref: (b, c_ref[0] * steps + i)),
                      _tiled(r.shape[1:], ax, 1, lambda b, i, c_ref: (b, i))],
            out_specs=_tiled(r.shape[1:], ax, 1, lambda b, i, c_ref: (b, i)))
        outs.append(pl.pallas_call(
            body, name=f"{name}_{idx}", grid_spec=grid_spec, out_shape=jax.ShapeDtypeStruct(r.shape, BF16),
            compiler_params=_params(2))(c, g, r))
    return outs


def _place():
    x, y, c = lax.axis_index("x"), lax.axis_index("y"), lax.axis_index("c")
    return x, y, c, [(1 - x, y), (x, 1 - y), (1 - x, 1 - y)]


def _remote(src, dst, send_sem, recv_sem, dev):
    return pltpu.make_async_remote_copy(src_ref=src, dst_ref=dst, send_sem=send_sem, recv_sem=recv_sem,
                                        device_id=dev, device_id_type=MESH)


def _half(ref, lead, ax, which):
    size = ref.shape[len(lead) + ax] // 2
    part = pl.ds(which * size, size)
    return ref.at[(*lead, part, slice(None)) if ax == 0 else (*lead, slice(None), part)]


def gather_weights(shards, axes):
    n = len(shards)

    def body(*refs):
        ins, outs = refs[:n], refs[n:2 * n]
        ici_s, ici_r, d2d_s, d2d_r = refs[2 * n:]
        x, y, c, chips = _place()
        me = 2 * x + y
        sends, passes = [], []
        for w in range(n):
            cp = _remote(ins[w], outs[w].at[me], d2d_s.at[3 * n + w], d2d_r.at[3 * n + w], (x, y, 1 - c))
            cp.start()
            passes.append(cp)
        for w in range(n):
            for j, (ox, oy) in enumerate(chips):
                cp = _remote(_half(ins[w], (), axes[w], c), _half(outs[w], (me,), axes[w], c),
                             ici_s.at[3 * w + j], ici_r.at[3 * w + j], (ox, oy, c))
                cp.start()
                sends.append(cp)
        for w in range(n):
            for j, (ox, oy) in enumerate(chips):
                landed = _half(outs[w], (2 * ox + oy,), axes[w], c)
                _remote(landed, landed, ici_s.at[3 * w + j], ici_r.at[3 * w + j], (ox, oy, c)).wait_recv()
                cp = _remote(landed, landed, d2d_s.at[3 * w + j], d2d_r.at[3 * w + j], (x, y, 1 - c))
                cp.start()
                passes.append(cp)
        for w in range(n):
            for j, (ox, oy) in enumerate(chips):
                other = _half(outs[w], (2 * ox + oy,), axes[w], 1 - c)
                _remote(other, other, d2d_s.at[3 * w + j], d2d_r.at[3 * w + j], (x, y, 1 - c)).wait_recv()
            own = outs[w].at[me]
            _remote(own, own, d2d_s.at[3 * n + w], d2d_r.at[3 * n + w], (x, y, 1 - c)).wait_recv()
        for cp in sends + passes:
            cp.wait_send()

    return pl.pallas_call(
        body, name="gather_weights", in_specs=[ANY] * n, out_specs=[ANY] * n,
        out_shape=[jax.ShapeDtypeStruct((4,) + s.shape, s.dtype) for s in shards],
        scratch_shapes=[pltpu.SemaphoreType.DMA((3 * n,))] * 2 + [pltpu.SemaphoreType.DMA((4 * n,))] * 2,
    )(*shards)


HBM = pl.BlockSpec(memory_space=pltpu.HBM)
SEM = pl.BlockSpec(memory_space=pltpu.SEMAPHORE)
DATAFLOW = pltpu.SideEffectType.DATAFLOW_SIDE_EFFECTING


def _hbm(a):
    return pltpu.with_memory_space_constraint(a, pltpu.HBM)


class SplitExchange:
    def __init__(self, name, srcs, zone_shapes, n_sems, plan):
        self.name, self.n, self.n_sems, self.plan = name, len(srcs), n_sems, plan
        self.srcs = [_hbm(s) for s in srcs]
        self.zones = [_hbm(lax.empty(shape, s.dtype)) for shape, s in zip(zone_shapes, srcs)]

    def start(self, after):
        n, n_after = self.n, len(after)

        def body(*refs):
            ins, lands = refs[:n], refs[n:2 * n]
            send, recv, token = refs[2 * n + n_after], refs[2 * n + n_after + 1], refs[-1]
            for src, dst, si, ri, dev in self.plan(ins, lands)[0]:
                _remote(src, dst, send.at[si], recv.at[ri], dev).start()
            token[...] = jnp.zeros_like(token)

        res = pl.pallas_call(
            body, name=f"{self.name}_start", in_specs=[HBM] * (2 * n) + [ANY] * n_after,
            out_specs=[SEM, SEM] + [HBM] * (2 * n) + [pl.BlockSpec(memory_space=pltpu.VMEM)],
            out_shape=[pltpu.SemaphoreType.DMA((self.n_sems,)), pltpu.SemaphoreType.DMA((self.n_sems,))]
            + [pltpu.HBM(a.shape, a.dtype) for a in self.srcs + self.zones] + [jax.ShapeDtypeStruct((8, LANES), F32)],
            input_output_aliases={i: 2 + i for i in range(2 * n)},
            compiler_params=pltpu.CompilerParams(has_side_effects=DATAFLOW),
        )(*self.srcs, *self.zones, *after)
        self.sems, self.srcs, self.zones = res[:2], list(res[2:2 + n]), list(res[2 + n:2 + 2 * n])
        return res[-1]

    def wait(self, after):
        n = self.n

        def body(*refs):
            ins, lands = refs[:n], refs[n:2 * n]
            send, recv = refs[2 * n], refs[2 * n + 1]
            sends, arrivals = self.plan(ins, lands)
            for src, _, si, _, dev in sends:
                _remote(src, src, send.at[si], recv.at[si], dev).wait_send()
            for landed, ri in arrivals:
                _remote(landed, landed, send.at[ri], recv.at[ri], _place()[:3]).wait_recv()

        res = pl.pallas_call(
            body, name=f"{self.name}_wait", in_specs=[HBM] * (2 * n) + [SEM, SEM, ANY], out_specs=[HBM] * (2 * n),
            out_shape=[pltpu.HBM(a.shape, a.dtype) for a in self.srcs + self.zones],
            input_output_aliases={i: i for i in range(2 * n)},
            compiler_params=pltpu.CompilerParams(has_side_effects=DATAFLOW),
        )(*self.srcs, *self.zones, *self.sems, after)
        self.srcs = list(res[:n])
        return list(res[n:])


def split_gather(shards):
    n = len(shards)

    def plan(ins, lands):
        x, y, c, chips = _place()
        me = 2 * x + y
        sends, arrivals = [], []
        for w in range(n):
            for j, (ox, oy) in enumerate(chips):
                for k in range(2):
                    base = 2 * (3 * w + j)
                    sends.append((_half(ins[w], (), 0, c), _half(lands[w], (me,), 0, c), base + k, base + c, (ox, oy, k)))
                    arrivals.append((_half(lands[w], (2 * ox + oy,), 0, k), base + k))
            sends.append((ins[w], lands[w].at[me], 6 * n + w, 6 * n + w, (x, y, 1 - c)))
            arrivals.append((lands[w].at[me], 6 * n + w))
        return sends, arrivals

    return SplitExchange("gather", shards, [(4,) + s.shape for s in shards], 7 * n, plan)


def split_pair_swap(name, grads, axes):
    def plan(ins, lands):
        x, y, c, _ = _place()
        sends = [(_half(ins[w], (slice(None),), axes[w], 1 - c), lands[w], w, w, (x, y, 1 - c)) for w in range(len(ins))]
        return sends, [(lands[w], w) for w in range(len(ins))]

    halved = [tuple(d // 2 if i == 1 + ax else d for i, d in enumerate(g.shape)) for g, ax in zip(grads, axes)]
    return SplitExchange(name, grads, halved, len(grads), plan)


def split_chip_exchange(name, parts):
    def plan(ins, lands):
        x, y, c, chips = _place()
        sends, arrivals = [], []
        for w in range(len(ins)):
            for j, (ox, oy) in enumerate(chips):
                sends.append((ins[w].at[2 * ox + oy], lands[w].at[2 * x + y], 3 * w + j, 3 * w + j, (ox, oy, c)))
                arrivals.append((lands[w].at[2 * ox + oy], 3 * w + j))
        return sends, arrivals

    return SplitExchange(name, parts, [p.shape for p in parts], 3 * len(parts), plan)


def split_pair_send(halves):
    def plan(ins, lands):
        x, y, c, _ = _place()
        return ([(ins[w], lands[w], w, w, (x, y, 1 - c)) for w in range(len(ins))],
                [(lands[w], w) for w in range(len(ins))])

    return SplitExchange("pair_send", halves, [h.shape for h in halves], len(halves), plan)


def pair_send(halves):
    n = len(halves)

    def body(*refs):
        ins, outs = refs[:n], refs[n:2 * n]
        send, recv = refs[2 * n:]
        x, y, c, _ = _place()
        cps = [_remote(ins[w], outs[w], send.at[w], recv.at[w], (x, y, 1 - c)) for w in range(n)]
        for cp in cps:
            cp.start()
        for cp in cps:
            cp.wait_recv()
        for cp in cps:
            cp.wait_send()

    return pl.pallas_call(
        body, name="pair_send", in_specs=[ANY] * n, out_specs=[ANY] * n,
        out_shape=[jax.ShapeDtypeStruct(h.shape, h.dtype) for h in halves],
        scratch_shapes=[pltpu.SemaphoreType.DMA((n,))] * 2,
    )(*halves)


def all_reduce_small(name, vec, after=()):
    rows = vec.shape[0]

    def body(v_ref, *refs):
        o_ref, buf, send, recv = refs[len(after):]
        x, y, c, _ = _place()
        me = 4 * x + 2 * y + c
        buf[me] = v_ref[...]
        cps = []
        for k in range(1, 8):
            kx, ky, kc = (k >> 2) & 1, (k >> 1) & 1, k & 1
            peer = (x if kx == 0 else 1 - x, y if ky == 0 else 1 - y, c if kc == 0 else 1 - c)
            cp = _remote(v_ref, buf.at[me], send.at[k - 1], recv.at[k - 1], peer)
            cp.start()
            cps.append(cp)
        for k in range(1, 8):
            kx, ky, kc = (k >> 2) & 1, (k >> 1) & 1, k & 1
            px, py, pc = (x if kx == 0 else 1 - x, y if ky == 0 else 1 - y, c if kc == 0 else 1 - c)
            slot = buf.at[4 * px + 2 * py + pc]
            _remote(slot, slot, send.at[k - 1], recv.at[k - 1], (px, py, pc)).wait_recv()
        for cp in cps:
            cp.wait_send()
        acc = buf[0]
        for d in range(1, 8):
            acc = acc + buf[d]
        o_ref[...] = acc

    vm = pl.BlockSpec(memory_space=pltpu.VMEM)
    return pl.pallas_call(
        body, name=name, in_specs=[vm] + [ANY] * len(after), out_specs=vm, out_shape=jax.ShapeDtypeStruct(vec.shape, F32),
        scratch_shapes=[pltpu.VMEM((8, rows, LANES), F32), pltpu.SemaphoreType.DMA((7,)), pltpu.SemaphoreType.DMA((7,))],
    )(vec, *after)


class NoExchange:
    def __init__(self, late):
        self.late = late

    def late_weights(self, after):
        return self.late

    def reduce_start(self, grads):
        return jnp.zeros((8, LANES), F32)

    def reduce_exchange(self, after):
        return jnp.zeros((8, LANES), F32)

    def reduce_finish(self, after):
        return jnp.zeros((8, LANES), F32)

    def input_grad_start(self, dw_main, dw_small):
        return jnp.zeros((8, LANES), F32)

    def input_grad_exchange(self, after):
        return jnp.zeros((8, LANES), F32)


def local_step(x2, tgt2, g1, g2, gdn_ng, qn_g, kn_g, p1, p2, conv_w, wt_main, wt_small, hooks, nseq, seq):
    rows, dm = x2.shape
    wide = NH * LANES
    row = lambda a, off=0, w=None: (a, "row", off, a.shape[1] if w is None else w)
    rowh = lambda a, off=0, w=LANES: (a, "rowh", off, w)
    par = lambda a: (a, "par", 0, a.shape[1])
    parh = lambda a, off=0: (a, "parh", off, LANES)
    o_row = lambda w, dt: (w, "row", w, dt)
    o_rowh = lambda dt, tw=wide, w=LANES: (tw, "rowh", w, dt)

    u, = ew_fwd("rms1", f_rms, [row(x2), par(g1)], [o_row(dm, BF16)], rows)
    proj = matmul("mm_in", u, wt_main, "nt", BF16)
    sp = matmul("mm_in_small", u, wt_small, "nt", F32)
    so, = ew_fwd("small", f_small, [row(sp), par(p1), par(p2)], [o_row(LANES, F32)], rows)
    cs = cumsum_time("cumsum", so, nseq, seq, False)
    gb, bb, cb = ew_fwd("bcast", f_bcast, [row(so), row(cs)], [o_rowh(F32)] * 3, rows, NH)
    ct = transpose_time("c_time_major", cs, nseq, seq)
    conv = {}
    for mode, off in (("q", 0), ("k", NH), ("v", 2 * NH)):
        conv[mode], = ew_fwd(f"conv_{mode}", make_f_conv(mode), [rowh(proj, off), parh(conv_w, off)], [o_rowh(F32)],
                             rows, NH, seq, "hi", CONV_HEADS)
    val, kcum, attn, qdec, kdec, t_inv = gdn_a_fwd(conv["q"], conv["k"], conv["v"], gb, bb, rows)
    o_a, snaps = gdn_b_fwd(val, kcum, attn, qdec, kdec, gb, nseq, seq)
    ya_in, = ew_fwd("gdn_post", f_post, [rowh(o_a), rowh(proj, 3 * NH), par(gdn_ng)], [o_rowh(BF16)], rows, NH)
    fqn, = ew_fwd("fox_qn", f_rms, [rowh(proj, FOX_Q), par(qn_g)], [o_rowh(BF16)], rows, NH)
    fkn, = ew_fwd("fox_kn", f_rms, [rowh(proj, FOX_K), par(kn_g)], [o_rowh(BF16)], rows, NH)
    o_b, o_b16, lse = fox_fwd(fqn, fkn, proj, ct, nseq, seq)
    p_a, p_b, w_o, w_u, w_d = hooks.late_weights(o_a)
    y_a = matmul("mm_pa", ya_in, p_a, "nn", F32, tn=1024)
    y_b = matmul("mm_pb", o_b16, p_b, "nn", F32, tn=1024)
    gates = [row(proj, 7, dm), row(proj, 8, dm)]
    merged, = ew_fwd("merge", f_merge, gates + [row(y_a), row(y_b)], [o_row(dm, BF16)], rows)
    hres = matmul("mm_out", merged, w_o, "nn", F32, add=x2, tn=1024)
    hn, = ew_fwd("rms2", f_rms, [row(hres), par(g2)], [o_row(dm, BF16)], rows)
    up_blocks = w_u.shape[0]
    act, relu2 = matmul("mm_up", hn, w_u, "nn", F32, col_blocks=up_blocks, out_dtypes=[F32, BF16],
                        epilogue=lambda r: [r, jnp.maximum(r, 0.0) * jnp.maximum(r, 0.0)])
    def loss_tail(r, h_tile, t_tile):
        d = (r + h_tile) - t_tile
        e = (0.5 / dm) * (d * d)
        part = e.reshape(e.shape[0] // 8, 8, e.shape[1]).sum(axis=0)
        part = sum(part[:, t * LANES:(t + 1) * LANES] for t in range(e.shape[1] // LANES))
        g = d * (1.0 / dm)
        return [g, g, part]

    dout, dout16, loss_acc = matmul("mm_down", relu2, w_d, "nn", F32, extras=[hres, tgt2], epilogue=loss_tail,
                                    out_dtypes=[F32, BF16, F32], tile_sums=True, tm=512)

    d_act = matmul("mm_d_act", dout16, w_d, "nt", BF16, extras=[act], epilogue=lambda r, a: [2.0 * jnp.maximum(a, 0.0) * r])
    dw_d = matmul("mm_dw_down", relu2, dout16, "tn", F32, tn=1024)
    dw_u = matmul("mm_dw_up", hn, d_act, "tn", F32, col_blocks=up_blocks)
    d_hn = matmul("mm_d_hn", d_act, w_u, "nt", F32, col_blocks=up_blocks)
    dh, dh16, dg2 = ew_bwd("rms2_b", f_rms, [row(hres), par(g2)], [(row(d_hn),)], [row(dout)],
                           lambda g, e: [g[0] + e[0], g[0] + e[0], g[1]],
                           [((rows, dm), "row", dm, F32, None), ((rows, dm), "row", dm, BF16, None), ((1, dm), "par", dm, F32, "all")], rows)
    d_merged = matmul("mm_d_merged", dh16, w_o, "nt", F32, tn=1024)
    dw_o = matmul("mm_dw_out", merged, dh16, "tn", F32, tn=1024)
    seg16 = ((rows, dm), "row", dm, BF16, None)
    d_ga16, d_gb16, d_ya16, d_yb16 = ew_bwd("merge_b", f_merge, gates + [row(y_a), row(y_b)], [(row(d_merged),)], [],
                                            lambda g, e: list(g), [seg16] * 4, rows)
    dp_a = matmul("mm_dp_a", ya_in, d_ya16, "tn", F32, tn=1024)
    d_ya_in = matmul("mm_d_ya_in", d_ya16, p_a, "nt", F32, tn=1024)
    dp_b = matmul("mm_dp_b", o_b16, d_yb16, "tn", F32, tn=1024)
    d_ob = matmul("mm_d_ob", d_yb16, p_b, "nt", F32, tn=1024)
    token = hooks.reduce_start(dict(p_a=dp_a, p_b=dp_b, w_o=dw_o, w_u=dw_u, w_d=dw_d))
    gdn_ng_t = gdn_ng + token[0, 0]
    h32 = ((rows, wide), "rowh", LANES, F32, None)
    h16 = ((rows, wide), "rowh", LANES, BF16, None)
    gain = ((1, LANES), "par", LANES, F32, "all")
    d_oa, d_z16, d_gdn_ng = ew_bwd("gdn_post_b", f_post, [rowh(o_a), rowh(proj, 3 * NH), par(gdn_ng_t)], [(rowh(d_ya_in),)], [],
                                   lambda g, e: list(g), [h32, h16, gain], rows, NH)
    dval, dkc, dat, dqd, dkd, dgb_b = gdn_b_bwd(val, kcum, attn, qdec, kdec, gb, snaps, d_oa, nseq, seq)
    d_cq, d_ck, d_cv, d_gb, d_bb = gdn_a_bwd(conv["q"], conv["k"], conv["v"], gb, bb, t_inv, dval, dkc, dat, dqd, dkd, dgb_b, rows)
    token = hooks.reduce_exchange(d_cq)
    conv_w_t = conv_w + token[0, 0]
    d_pre, d_conv = {}, {}
    tap = ((4, wide), "parh", LANES, F32, "inner")
    for mode, off, ctg in (("q", 0, d_cq), ("k", NH, d_ck), ("v", 2 * NH, d_cv)):
        d_pre[mode], d_conv[mode] = ew_bwd(f"conv_{mode}_b", make_f_conv(mode), [rowh(proj, off), parh(conv_w_t, off)],
                                           [(rowh(ctg),)], [], lambda g, e: list(g), [h16, tap], rows, NH, seq, "hi", CONV_HEADS)
    delta, = ew_fwd("fox_delta", f_delta, [rowh(d_ob), rowh(o_b)], [o_rowh(F32)], rows, NH, after=[token])
    d_fqn, d_cq_b = fox_dq(fqn, fkn, proj, ct, d_ob, lse, delta, nseq, seq)
    d_fkn, d_fv16, d_ck_b = fox_dkv(fqn, fkn, proj, cb, d_ob, lse, delta, nseq, seq)
    token = hooks.reduce_finish(d_fkn)
    qn_g_t, kn_g_t = qn_g + token[0, 0], kn_g + token[0, 0]
    d_fq16, d_qn_g = ew_bwd("fox_qn_b", f_rms, [rowh(proj, FOX_Q), par(qn_g_t)], [(rowh(d_fqn),)], [], lambda g, e: list(g),
                            [h16, gain], rows, NH)
    d_fk16, d_kn_g = ew_bwd("fox_kn_b", f_rms, [rowh(proj, FOX_K), par(kn_g_t)], [(rowh(d_fkn),)], [], lambda g, e: list(g),
                            [h16, gain], rows, NH)
    narrow = ((rows, LANES), "row", LANES, F32, None)
    d_so, d_cs = ew_bwd("bcast_b", f_bcast, [row(so), row(cs)], [(rowh(d_gb),), (rowh(d_bb),), (rowh(d_cq_b), rowh(d_ck_b))], [],
                        lambda g, e: list(g), [narrow, narrow], rows, NH)
    d_logf = cumsum_time("cumsum_b", d_cs, nseq, seq, True)
    vec = ((1, LANES), "par", LANES, F32, "all")
    d_sp16, d_p1, d_p2 = ew_bwd("small_b", f_small, [row(sp), par(p1), par(p2)], [(row(d_so), row(d_logf))], [],
                                lambda g, e: list(g), [((rows, LANES), "row", LANES, BF16, None), vec, vec], rows)
    d_proj16 = jnp.concatenate([d_pre["q"], d_pre["k"], d_pre["v"], d_z16, d_fq16, d_fk16, d_fv16, d_ga16, d_gb16], axis=1)
    dw_main = matmul("mm_dw_main", d_proj16, u, "tn", F32)
    dw_small = matmul("mm_dw_small", d_sp16, u, "tn", F32)
    wt_small_t = wt_small + hooks.input_grad_start(dw_main, dw_small)[0, 0].astype(BF16)
    d_u = matmul("mm_d_u_small", d_sp16, wt_small_t, "nn", F32)
    d_u = matmul("mm_d_u_first", d_proj16, wt_main, "nn", F32, add=d_u, k_part=(0, 2))
    d_u = matmul("mm_d_u_second", d_proj16, wt_main, "nn", F32, add=d_u, k_part=(1, 2), after=[hooks.input_grad_exchange(d_u)])
    dx, dg1 = ew_bwd("rms1_b", f_rms, [row(x2), par(g1)], [(row(d_u),)], [row(dh)], lambda g, e: [g[0] + e[0], g[1]],
                     [((rows, dm), "row", dm, F32, None), ((1, dm), "par", dm, F32, "all")], rows)
    d_conv_w = jnp.concatenate([d_conv["q"], d_conv["k"], d_conv["v"]], axis=1)
    return dict(loss_acc=loss_acc, dx=dx, g1=dg1, g2=dg2, gdn_ng=d_gdn_ng, qn=d_qn_g, kn=d_kn_g, p1=d_p1, p2=d_p2,
                conv=d_conv_w, w_main=dw_main, w_small=dw_small, p_a=dp_a, p_b=dp_b, w_o=dw_o, w_u=dw_u, w_d=dw_d)


_W = NH * LANES
_A0, _A1 = 4 * _W, 4 * _W + 2 * NH
_B0, _B1 = _A1 + 3 * _W, _A1 + 3 * _W + NH
N_IN = _B1 + 2 * _W


def _split_w_in(full_t):
    main = jnp.concatenate([full_t[:_A0], full_t[_A1:_B0], full_t[_B1:]], axis=0)
    small = jnp.concatenate([full_t[_A0:_A1], full_t[_B0:_B1], jnp.zeros((LANES - 3 * NH, full_t.shape[1]), full_t.dtype)], axis=0)
    return main, small


def _join_w_in(main, small):
    return jnp.concatenate([main[:_A0], small[:2 * NH], main[_A0:_A0 + 3 * _W], small[2 * NH:3 * NH], main[_A0 + 3 * _W:]], axis=0)


def _lanes(v, at=0):
    return jnp.pad(v.reshape(1, -1), ((0, 0), (at, LANES - at - v.size)))


def kernel(x, norm_mix_g, w_in, gdn_conv_w, gdn_a_log, gdn_dt_bias, gdn_norm_g, fox_q_norm_g, fox_k_norm_g, fox_f_bias, w_proj_gdn, w_proj_fox, w_out, norm_mlp_g, w_up, w_down, loss_target, m_norm_mix_g, m_w_in, m_gdn_conv_w, m_gdn_a_log, m_gdn_dt_bias, m_gdn_norm_g, m_fox_q_norm_g, m_fox_k_norm_g, m_fox_f_bias, m_w_proj_gdn, m_w_proj_fox, m_w_out, m_norm_mlp_g, m_w_up, m_w_down, v_norm_mix_g, v_w_in, v_gdn_conv_w, v_gdn_a_log, v_gdn_dt_bias, v_gdn_norm_g, v_fox_q_norm_g, v_fox_k_norm_g, v_fox_f_bias, v_w_proj_gdn, v_w_proj_fox, v_w_out, v_norm_mlp_g, v_w_up, v_w_down):
    nseq, seq, dm = x.shape
    rows = nseq * seq
    xi, yi, ci = lax.axis_index("x"), lax.axis_index("y"), lax.axis_index("c")
    chip = 2 * xi + yi
    conv_cols = gdn_conv_w.shape[2]

    tr = lambda a: jnp.swapaxes(a[0], 0, 1)
    big = [tr(w_in), w_proj_gdn[0], w_proj_fox[0], w_out[0], w_up[0], w_down[0]]
    axes = [1, 0, 0, 0, 0, 0]
    big16 = [w.astype(BF16) for w in big]
    conv_slot = jnp.zeros((4, 4, conv_cols), F32).at[:, chip].set(jnp.where(ci == 0, gdn_conv_w[0], 0.0))
    conv_full = all_reduce_small("gather_conv", conv_slot.reshape(-1, LANES)).reshape(4, 4 * conv_cols)
    got_in, = gather_weights(big16[:1], axes[:1])
    wt_main, wt_small = _split_w_in(got_in.reshape(-1, dm))
    core, chip_no = ci.reshape(1).astype(jnp.int32), chip.reshape(1).astype(jnp.int32)
    gather = split_gather(big16[1:])
    token = gather.start([got_in, conv_full])

    class Hooks:
        def late_weights(self, after):
            g_pa, g_pb, g_wo, w_u, g_wd = gather.wait(after)
            return (*(g.reshape(-1, dm) for g in (g_pa, g_pb, g_wo)), w_u, g_wd.reshape(-1, dm))

        def reduce_start(self, grads):
            blocks = [grads["p_a"].reshape(4, -1, dm), grads["p_b"].reshape(4, -1, dm), grads["w_o"].reshape(4, -1, dm),
                      grads["w_u"], grads["w_d"].reshape(4, -1, dm)]
            self.swap = split_pair_swap("pair_swap_late", blocks, axes[1:])
            return self.swap.start([])

        def reduce_exchange(self, after):
            swapped = self.swap.wait(after)
            self.exchange = split_chip_exchange("chip_exchange_late", add_pair("add_pair_late", self.swap.srcs, swapped, core, axes[1:]))
            return self.exchange.start([])

        def reduce_finish(self, after):
            slots = self.exchange.wait(after)
            self.send = split_pair_send(add_chips("add_chips_late", slots, self.exchange.srcs, chip_no, axes[1:]))
            return self.send.start([])

        def input_grad_start(self, dw_main, dw_small):
            self.in_swap = split_pair_swap("pair_swap_in", [_join_w_in(dw_main, dw_small).reshape(4, -1, dm)], axes[:1])
            return self.in_swap.start([])

        def input_grad_exchange(self, after):
            swapped = self.in_swap.wait(after)
            self.in_exchange = split_chip_exchange("chip_exchange_in", add_pair("add_pair_in", self.in_swap.srcs, swapped, core, axes[:1]))
            return self.in_exchange.start([])

    hooks = Hooks()
    p1 = _lanes(gdn_dt_bias[0]) + _lanes(fox_f_bias[0], 2 * NH)
    p2 = _lanes(gdn_a_log[0])

    g = local_step(x.reshape(rows, dm), loss_target.reshape(rows, dm), norm_mix_g + token[0, 0], norm_mlp_g, gdn_norm_g,
                   fox_q_norm_g, fox_k_norm_g, p1, p2, conv_full, wt_main, wt_small, hooks, nseq, seq)

    others = hooks.send.wait(g["dx"])
    big_m = [tr(m_w_in), m_w_proj_gdn[0], m_w_proj_fox[0], m_w_out[0], m_w_up[0], m_w_down[0]]
    big_v = [tr(v_w_in), v_w_proj_gdn[0], v_w_proj_fox[0], v_w_out[0], v_w_up[0], v_w_down[0]]
    names = ["w_in", "w_proj_gdn", "w_proj_fox", "w_out", "w_up", "w_down"]
    big_res, big_grad = {}, {}
    for i in range(1, len(names)):
        big_grad[names[i]], *big_res[names[i]] = adamw_halves(f"adamw_{names[i]}", big[i], hooks.send.srcs[i - 1], others[i - 1],
                                                              big_m[i], big_v[i], core, axes[i])
    slots = hooks.in_exchange.wait(big_res[names[-1]][0])
    mine = add_chips("add_chips_in", slots, hooks.in_exchange.srcs, chip_no, axes[:1])
    res = adamw_halves("adamw_w_in", big[0], mine[0], pair_send(mine)[0], big_m[0], big_v[0], core, axes[0])
    big_grad["w_in"], *big_res["w_in"] = [jnp.swapaxes(r, 0, 1) for r in res]

    small_parts = [g["loss_acc"], g["g1"].reshape(8, LANES), g["g2"].reshape(8, LANES), g["gdn_ng"], g["qn"], g["kn"], g["p1"], g["p2"],
                   g["conv"].reshape(-1, LANES)]
    tiled = [jnp.pad(p, ((0, -p.shape[0] % 8), (0, 0))) for p in small_parts]
    red = all_reduce_small("reduce_small", jnp.concatenate(tiled, axis=0), slots)
    pos, red_parts = 0, []
    for p, t in zip(small_parts, tiled):
        red_parts.append(red[pos:pos + p.shape[0]])
        pos += t.shape[0]
    r_loss, r_g1, r_g2, r_gdn_ng, r_qn, r_kn, r_p1, r_p2, r_conv = red_parts
    loss = jnp.sum(r_loss)
    g_conv = lax.dynamic_slice_in_dim(r_conv.reshape(4, 4, conv_cols), chip, 1, axis=1).reshape(4, conv_cols)
    small_grads = [r_g1.reshape(1, dm), r_p2[:, :NH], r_p1[:, :NH], r_gdn_ng, r_qn, r_kn, r_p1[:, 2 * NH:3 * NH], r_g2.reshape(1, dm)]
    small_w = [norm_mix_g, gdn_a_log, gdn_dt_bias, gdn_norm_g, fox_q_norm_g, fox_k_norm_g, fox_f_bias, norm_mlp_g]
    small_m = [m_norm_mix_g, m_gdn_a_log, m_gdn_dt_bias, m_gdn_norm_g, m_fox_q_norm_g, m_fox_k_norm_g, m_fox_f_bias, m_norm_mlp_g]
    small_v = [v_norm_mix_g, v_gdn_a_log, v_gdn_dt_bias, v_gdn_norm_g, v_fox_q_norm_g, v_fox_k_norm_g, v_fox_f_bias, v_norm_mlp_g]

    def pack(parts):
        flat = jnp.concatenate([jnp.pad(p.reshape(-1), (0, -p.size % LANES)) for p in parts])
        return jnp.pad(flat, (0, -flat.size % (8 * LANES))).reshape(-1, LANES)

    packed = adamw("adamw_small", pack(small_w + [gdn_conv_w[0]]), pack(small_grads + [g_conv]),
                   pack(small_m + [m_gdn_conv_w[0]]), pack(small_v + [v_gdn_conv_w[0]]))

    def unpack(flat2d):
        flat, pos, res = flat2d.reshape(-1), 0, []
        for p in small_w + [gdn_conv_w[0]]:
            res.append(flat[pos:pos + p.size].reshape(p.shape))
            pos += p.size + (-p.size % LANES)
        return res

    s_delta, s_m, s_v = (unpack(a) for a in packed)

    order = ["norm_mix_g", "w_in", "gdn_conv_w", "gdn_a_log", "gdn_dt_bias", "gdn_norm_g", "fox_q_norm_g", "fox_k_norm_g",
             "fox_f_bias", "w_proj_gdn", "w_proj_fox", "w_out", "norm_mlp_g", "w_up", "w_down"]
    small_names = ["norm_mix_g", "gdn_a_log", "gdn_dt_bias", "gdn_norm_g", "fox_q_norm_g", "fox_k_norm_g", "fox_f_bias", "norm_mlp_g",
                   "gdn_conv_w"]
    small_idx = {nm: i for i, nm in enumerate(small_names)}
    shapes = dict(zip(order, (a.shape for a in (norm_mix_g, w_in, gdn_conv_w, gdn_a_log, gdn_dt_bias, gdn_norm_g, fox_q_norm_g,
                                                 fox_k_norm_g, fox_f_bias, w_proj_gdn, w_proj_fox, w_out, norm_mlp_g, w_up, w_down))))
    grads_out, delta_out, m_out, v_out = [], [], [], []
    for nm in order:
        if nm in big_res:
            d, mm, vv = big_res[nm]
            gr = big_grad[nm]
        else:
            i = small_idx[nm]
            gr = (small_grads + [g_conv])[i]
            d, mm, vv = s_delta[i], s_m[i], s_v[i]
        for lst, val in ((grads_out, gr), (delta_out, d), (m_out, mm), (v_out, vv)):
            lst.append(val.reshape(shapes[nm]))
    return (loss, g["dx"].reshape(x.shape), *grads_out, *delta_out, *m_out, *v_out)
```

```python
import functools

import jax
import jax.numpy as jnp
from jax import lax
from jax.experimental import pallas as pl
from jax.experimental.pallas import tpu as pltpu

F32 = jnp.float32
BF16 = jnp.bfloat16
LANES = 128
NH = 8
EPS = 1e-6
GDN_CHUNK = 64
GDN_ROWS = 256
GDN_BASE = 16
ROW_TILE = 512
CONV_HEADS = 2
ATT_TILE = 512
NEG = -1e30
VMEM_LIMIT_BYTES = 48 * 1024 * 1024
HI = lax.Precision.HIGHEST
LO = lax.Precision.DEFAULT
MESH = pl.DeviceIdType.MESH
ANY = pl.BlockSpec(memory_space=pl.ANY)

ADAM_LR, ADAM_B1, ADAM_B2, ADAM_EPS, ADAM_WD, ADAM_STEP = 0.001, 0.9, 0.999, 1e-08, 0.01, 10


def _params(n_grid):
    return pltpu.CompilerParams(dimension_semantics=("arbitrary",) * n_grid,
                                vmem_limit_bytes=VMEM_LIMIT_BYTES)


def _dot(a, b, dims, precision=None):
    dn = {"nn": (((1,), (0,)), ((), ())), "nt": (((1,), (1,)), ((), ())), "tn": (((0,), (0,)), ((), ()))}[dims]
    return lax.dot_general(a, b, dn, precision=precision, preferred_element_type=F32)


def _iota(shape, dim):
    return lax.broadcasted_iota(jnp.int32, shape, dim)


def _split(x, parts):
    out = []
    for _ in range(parts - 1):
        hi = x.astype(BF16)
        out.append(hi)
        x = x - hi.astype(F32)
    return out + [x.astype(BF16)]


def _dot_mask(mask, b, dims):
    m16 = mask.astype(BF16)
    b1, b2, b3 = _split(b, 3)
    return _dot(m16, b1, dims) + (_dot(m16, b2, dims) + _dot(m16, b3, dims))


@jax.custom_vjp
def mm_mask(mask, b):
    return _dot_mask(mask, b, "nn")


mm_mask.defvjp(lambda mask, b: (_dot_mask(mask, b, "nn"), mask),
               lambda mask, g: (jnp.zeros_like(mask), _dot_mask(mask, g, "tn")))


def matmul(name, a, b, dims, out_dtype, add=None, tm=1024, tn=1024, tk=512, col_blocks=None,
           extras=(), epilogue=None, out_dtypes=None, k_part=None, after=(), tile_sums=False):
    if col_blocks and dims != "tn":
        nb, b_rows, bw = b.shape
        b_shape = (b_rows, nb * bw)
    else:
        b_shape = b.shape
    if dims == "nn":
        (m, k), (_, n) = a.shape, b_shape
    elif dims == "nt":
        (m, k), (n, _) = a.shape, b_shape
    else:
        (k, m), (_, n) = a.shape, b_shape
    k_span = k // (k_part[1] if k_part else 1)
    if col_blocks and dims == "nt":
        k_span = min(k_span, bw)
    tk = k if k <= 1024 else max(t for t in (2048, 1536, 1024, 512, tk) if k_span % t == 0)
    tm, tn, tk = min(tm, m), min(tn, n), min(tk, k)
    assert m % tm == 0 and n % tn == 0 and k % tk == 0, (name, m, n, k)
    k0, nk = (0, k // tk) if k_part is None else (k_part[0] * (k // tk // k_part[1]), k // tk // k_part[1])
    assert k_part is None or (dims == "nn" and not col_blocks and (k // tk) % k_part[1] == 0)
    a_spec = pl.BlockSpec((tk, tm), lambda i, j, kk: (kk, i)) if dims == "tn" else pl.BlockSpec((tm, tk), lambda i, j, kk: (i, kk + k0))
    b_spec = pl.BlockSpec((tn, tk), lambda i, j, kk: (j, kk)) if dims == "nt" else pl.BlockSpec((tk, tn), lambda i, j, kk: (kk + k0, j))
    o_spec = pl.BlockSpec((tm, tn), lambda i, j, kk: (i, j))
    out_shape = (m, n)
    if col_blocks and dims == "nn":
        per = bw // tn
        assert bw % tn == 0
        b_spec = pl.BlockSpec((None, tk, tn), lambda i, j, kk: (j // per, kk, j % per))
    elif col_blocks and dims == "nt":
        per = bw // tk
        assert bw % tk == 0
        b_spec = pl.BlockSpec((None, tn, tk), lambda i, j, kk: (kk // per, j, kk % per))
    elif col_blocks:
        bw = n // col_blocks
        per = bw // tn
        assert bw % tn == 0 and add is None
        o_spec = pl.BlockSpec((None, tm, tn), lambda i, j, kk: (j // per, i, j % per))
        out_shape = (col_blocks, m, bw)
    extras = list(extras) + ([add] if add is not None else [])
    if add is not None:
        assert epilogue is None
        epilogue = lambda r, *e: [r + e[-1]]
    out_dtypes = [out_dtype] if epilogue is None or out_dtypes is None else list(out_dtypes)
    n_ex, n_out = len(extras), len(out_dtypes)

    def body(*refs):
        a_ref, b_ref = refs[0], refs[1]
        ex_refs, o_refs = refs[2:2 + n_ex], refs[2 + n_ex + len(after):2 + n_ex + len(after) + n_out]

        def finish(r):
            res = [r] if epilogue is None else epilogue(r, *[e[...] for e in ex_refs])
            for o_ref, v in zip(o_refs, res):
                o_ref[...] = v.astype(o_ref.dtype)

        if nk == 1:
            finish(_dot(a_ref[...], b_ref[...], dims))
            return
        acc_ref = refs[-1]
        kk = pl.program_id(2)

        @pl.when(kk == 0)
        def _():
            acc_ref[...] = jnp.zeros_like(acc_ref)

        acc_ref[...] += _dot(a_ref[...], b_ref[...], dims)

        @pl.when(kk == nk - 1)
        def _():
            finish(acc_ref[...])

    out_specs = [o_spec] * n_out
    out_shapes = [jax.ShapeDtypeStruct(out_shape, dt) for dt in out_dtypes]
    if tile_sums:
        out_specs[-1] = pl.BlockSpec((8, LANES), lambda i, j, kk: (i, j))
        out_shapes[-1] = jax.ShapeDtypeStruct((8 * (m // tm), LANES * (n // tn)), out_dtypes[-1])
    res = pl.pallas_call(
        body, name=name, grid=(m // tm, n // tn, nk), in_specs=[a_spec, b_spec] + [o_spec] * n_ex + [ANY] * len(after),
        out_specs=out_specs, out_shape=out_shapes,
        scratch_shapes=[pltpu.VMEM((tm, tn), F32)] if nk > 1 else [], compiler_params=_params(3),
    )(a, b, *extras, *after)
    return res[0] if n_out == 1 else res


def _ew_spec(kind, off, width, tb, hp, order, shape=None):
    def ih(g0, g1):
        return (g0, g1) if order == "ih" else (g1, g0)

    assert off % hp == 0 or kind in ("row", "par")
    if kind == "row":
        return pl.BlockSpec((tb, width), lambda g0, g1: (ih(g0, g1)[0], off))
    if kind == "rowh":
        return pl.BlockSpec((tb, hp * width), lambda g0, g1: (ih(g0, g1)[0], ih(g0, g1)[1] + off // hp))
    if kind == "par":
        return pl.BlockSpec(shape, lambda g0, g1: (0, 0))
    if kind == "parh":
        return pl.BlockSpec((shape[0], hp * width), lambda g0, g1: (0, ih(g0, g1)[1] + off // hp))
    raise ValueError(kind)


def _ew_grid(rows, tb, nh, hp, order):
    assert nh % hp == 0 and rows % tb == 0
    return (rows // tb, nh // hp) if order == "ih" else (nh // hp, rows // tb)


def _ew_load(ref, kind, width, hh):
    if kind in ("row", "par"):
        return ref[...].astype(F32)
    return ref[:, hh * width:(hh + 1) * width].astype(F32)


def ew_fwd(name, f, ins, outs, rows, nh=1, tb=ROW_TILE, order="ih", hp=None, after=()):
    hp = nh if hp is None else hp
    n_in = len(ins)

    def body(*refs):
        hb = pl.program_id(1) if order == "ih" else pl.program_id(0)
        for hh in range(hp):
            h = hh if hp == nh else hb * hp + hh
            vals = [_ew_load(r, kd, w, hh) for r, (_, kd, _, w) in zip(refs[:n_in], ins)]
            res = f(h, *vals)
            for r, v, (_, kd, w, _) in zip(refs[n_in + len(after):], res, outs):
                if kd == "row":
                    assert hp == 1
                    r[...] = v.astype(r.dtype)
                else:
                    r[:, hh * w:(hh + 1) * w] = v.astype(r.dtype)

    in_specs = [_ew_spec(kd, off, w, tb, hp, order, a.shape) for (a, kd, off, w) in ins]
    out_specs = [_ew_spec(kd, 0, w, tb, hp, order) for (_, kd, w, _) in outs]
    out_shape = [jax.ShapeDtypeStruct((rows, tw), dt) for (tw, _, _, dt) in outs]
    return pl.pallas_call(
        body, name=name, grid=_ew_grid(rows, tb, nh, hp, order), in_specs=in_specs + [ANY] * len(after), out_specs=out_specs,
        out_shape=out_shape, compiler_params=_params(2),
    )(*[a for (a, _, _, _) in ins], *after)


def ew_bwd(name, f, ins, cts, extras, emit, outs, rows, nh=1, tb=ROW_TILE, order="ih", hp=None):
    hp = nh if hp is None else hp
    n_in = len(ins)
    flat_cts = [d for group in cts for d in group]
    n_ct, n_ex = len(flat_cts), len(extras)

    def body(*refs):
        g0, g1 = pl.program_id(0), pl.program_id(1)
        hb = g1 if order == "ih" else g0
        out_refs = refs[n_in + n_ct + n_ex:]
        shared = [None] * len(outs)

        def store(r, v, first, sl=None):
            def put(val, add):
                if sl is None:
                    r[...] = (r[...] + val if add else val).astype(r.dtype)
                else:
                    r[:, sl] = (r[:, sl] + val if add else val).astype(r.dtype)

            if first is None:
                put(v, False)
            else:
                pl.when(first)(lambda: put(v, False))
                pl.when(jnp.logical_not(first))(lambda: put(v, True))

        for hh in range(hp):
            h = hh if hp == nh else hb * hp + hh
            vals = [_ew_load(r, kd, w, hh) for r, (_, kd, _, w) in zip(refs[:n_in], ins)]
            ct_refs = list(zip(refs[n_in:n_in + n_ct], flat_cts))
            ct_vals, pos = [], 0
            for group in cts:
                v = None
                for r, (_, kd, _, w) in ct_refs[pos:pos + len(group)]:
                    t = _ew_load(r, kd, w, hh)
                    v = t if v is None else v + t
                pos += len(group)
                ct_vals.append(v)
            ex_vals = [_ew_load(r, kd, w, hh) for r, (_, kd, _, w) in zip(refs[n_in + n_ct:n_in + n_ct + n_ex], extras)]
            _, vjp = jax.vjp(lambda *a: f(h, *a), *vals)
            res = emit(vjp(tuple(ct_vals)), ex_vals)
            for idx, (r, v, (_, kd, w, _, acc)) in enumerate(zip(out_refs, res, outs)):
                if kd in ("row", "par"):
                    shared[idx] = v if shared[idx] is None else shared[idx] + v
                else:
                    store(r, v, (g1 == 0) if acc == "inner" else None, slice(hh * w, (hh + 1) * w))
        for idx, (r, (_, kd, _, _, acc)) in enumerate(zip(out_refs, outs)):
            if kd in ("row", "par"):
                assert acc == "all" or hp == nh
                store(r, shared[idx], jnp.logical_and(g0 == 0, g1 == 0) if acc == "all" else None)

    operands = list(ins) + flat_cts + list(extras)
    in_specs = [_ew_spec(kd, off, w, tb, hp, order, a.shape) for (a, kd, off, w) in operands]
    out_specs = [_ew_spec(kd, 0, w, tb, hp, order, shp) for (shp, kd, w, _, _) in outs]
    out_shape = [jax.ShapeDtypeStruct(shp, dt) for (shp, _, _, dt, _) in outs]
    return pl.pallas_call(
        body, name=name, grid=_ew_grid(rows, tb, nh, hp, order), in_specs=in_specs, out_specs=out_specs,
        out_shape=out_shape, compiler_params=_params(2),
    )(*[a for (a, _, _, _) in operands])


def f_rms(h, x, g):
    r = lax.rsqrt(jnp.mean(x * x, axis=-1, keepdims=True) + EPS)
    return (x * r * g,)


def _softplus(z):
    return jnp.maximum(z, 0.0) + jnp.log1p(jnp.exp(-jnp.abs(z)))


def f_small(h, sp, p1, p2):
    lane = _iota(sp.shape, 1)
    z = sp + p1
    g = -jnp.exp(p2) * _softplus(z)
    beta = jax.nn.sigmoid(z)
    logf = -_softplus(-z)
    return (jnp.where(lane < NH, g, jnp.where(lane < 2 * NH, beta, jnp.where(lane < 3 * NH, logf, 0.0))),)


def _pick(x, lane_id):
    lane = _iota(x.shape, 1)
    col = jnp.sum(jnp.where(lane == lane_id, x, 0.0), axis=1, keepdims=True)
    return jnp.broadcast_to(col, x.shape)


def f_bcast(h, so, cs):
    return _pick(so, h), _pick(so, h + NH), _pick(cs, h + 2 * NH)


def _shift_down(s):
    def down(x):
        return jnp.where(_iota(x.shape, 0) >= s, pltpu.roll(x, s, 0), 0.0)

    def up(g):
        n = g.shape[0]
        return jnp.where(_iota(g.shape, 0) < n - s, pltpu.roll(g, n - s, 0), 0.0)

    @jax.custom_vjp
    def shift(x):
        return down(x)

    shift.defvjp(lambda x: (down(x), None), lambda _, g: (up(g),))
    return shift


def _silu(x):
    return x * jax.nn.sigmoid(x)


def make_f_conv(mode):
    sh1, sh2, sh3 = _shift_down(1), _shift_down(2), _shift_down(3)

    def f(h, x, w):
        sub = _iota(w.shape, 0)

        def tap(i):
            return jnp.sum(jnp.where(sub == i, w, 0.0), axis=0, keepdims=True)

        y = sh3(x) * tap(0)
        y = y + sh2(x) * tap(1)
        y = y + sh1(x) * tap(2)
        y = y + x * tap(3)
        s = _silu(y)
        if mode == "v":
            return (s,)
        n = s * lax.rsqrt(jnp.sum(s * s, axis=-1, keepdims=True) + EPS)
        if mode == "q":
            n = n * (LANES ** -0.5)
        return (n,)

    return f


def f_post(h, o, z, g):
    r = lax.rsqrt(jnp.mean(o * o, axis=-1, keepdims=True) + EPS)
    return (o * r * g * _silu(z),)


def f_merge(h, ga, gb, ya, yb):
    return (jax.nn.sigmoid(ga) * ya + jax.nn.sigmoid(gb) * yb,)


def f_delta(h, do, o):
    return (jnp.broadcast_to(jnp.sum(do * o, axis=1, keepdims=True), o.shape),)


def cumsum_time(name, x, nseq, seq, reverse):
    nb = seq // LANES

    def body(x_ref, o_ref):
        r, c = _iota((LANES, LANES), 0), _iota((LANES, LANES), 1)
        tri = jnp.where((r <= c) if reverse else (r >= c), 1.0, 0.0).astype(F32)
        carry = jnp.zeros((1, LANES), F32)
        for b in (range(nb - 1, -1, -1) if reverse else range(nb)):
            blk = x_ref[b * LANES:(b + 1) * LANES, :]
            o_ref[b * LANES:(b + 1) * LANES, :] = _dot_mask(tri, blk, "nn") + carry
            carry = carry + jnp.sum(blk, axis=0, keepdims=True)

    spec = pl.BlockSpec((seq, LANES), lambda s: (s, 0))
    return pl.pallas_call(body, name=name, grid=(nseq,), in_specs=[spec], out_specs=spec,
                          out_shape=jax.ShapeDtypeStruct(x.shape, F32), compiler_params=_params(1))(x)


def transpose_time(name, x, nseq, seq):
    def body(x_ref, o_ref):
        o_ref[...] = x_ref[...].T

    return pl.pallas_call(
        body, name=name, grid=(nseq,), in_specs=[pl.BlockSpec((seq, LANES), lambda s: (s, 0))],
        out_specs=pl.BlockSpec((LANES, seq), lambda s: (s, 0)),
        out_shape=jax.ShapeDtypeStruct((nseq * LANES, seq), F32), compiler_params=_params(1))(x)


def _gdn_masks():
    n = GDN_ROWS
    r, c = _iota((n, n), 0), _iota((n, n), 1)
    shift = GDN_CHUNK.bit_length() - 1
    same = lax.shift_right_logical(r, shift) == lax.shift_right_logical(c, shift)
    return r, c, same


def _each(fn, *lists):
    return [fn(*xs) for xs in zip(*lists)]


def _gdn_decay(gbs):
    r, c, same = _gdn_masks()
    seg_tril = jnp.where(jnp.logical_and(same, r >= c), 1.0, 0.0).astype(F32)
    g_cum = _each(lambda gb: mm_mask(seg_tril, gb), gbs)
    lane0 = _iota(gbs[0].shape, 1) == 0
    g_col = _each(lambda g: jnp.sum(jnp.where(lane0, g, 0.0), axis=1, keepdims=True), g_cum)
    g_row = _each(lambda g: jnp.sum(jnp.where(r == c, jnp.broadcast_to(g, (GDN_ROWS, GDN_ROWS)), 0.0), axis=0, keepdims=True), g_col)
    return g_cum, _each(lambda a, b: a - b, g_col, g_row)


def gdn_f1(*args):
    qs, ks, gbs, bbs = (list(args[i::4]) for i in range(4))
    r, c, same = _gdn_masks()
    strict = jnp.logical_and(same, r > c)
    _, diff = _gdn_decay(gbs)
    lane0 = _iota(bbs[0].shape, 1) == 0
    beta_col = _each(lambda bb: jnp.sum(jnp.where(lane0, bb, 0.0), axis=1, keepdims=True), bbs)
    kk = _each(lambda k: _dot(k, k, "nt", LO), ks)
    return tuple(_each(lambda b, x, d: jnp.where(strict, b * x * jnp.exp(jnp.where(strict, d, 0.0)), 0.0), beta_col, kk, diff))


def gdn_f2(*args):
    ts, qs, ks, vs, gbs, bbs = (list(args[i::6]) for i in range(6))
    r, c, same = _gdn_masks()
    incl = jnp.logical_and(same, r >= c)
    g_cum, diff = _gdn_decay(gbs)
    decay = _each(lambda d: jnp.where(incl, jnp.exp(jnp.where(incl, d, 0.0)), 0.0), diff)
    e_g = _each(jnp.exp, g_cum)
    v_beta = _each(lambda v, bb: v * bb, vs, bbs)
    k_beta = _each(lambda k, bb, e: k * bb * e, ks, bbs, e_g)
    value = _each(lambda t, x: x + _dot(t, x, "nn", LO), ts, v_beta)
    k_cum = _each(lambda t, x: x + _dot(t, x, "nn", LO), ts, k_beta)
    attn = _each(lambda q, k, d: _dot(q, k, "nt", LO) * d, qs, ks, decay)
    ones = jnp.where(same, 1.0, 0.0).astype(F32)
    g_last = _each(lambda gb: mm_mask(ones, gb), gbs)
    q_dec = _each(lambda q, e: q * e, qs, e_g)
    k_dec = _each(lambda k, gl, g: k * jnp.exp(gl - g), ks, g_last, g_cum)
    return tuple(x for head in zip(value, k_cum, attn, q_dec, k_dec) for x in head)


def tri_inverse(mats):
    n = GDN_ROWS
    r, c = _iota((n, n), 0), _iota((n, n), 1)
    shift = GDN_BASE.bit_length() - 1
    blk = lax.shift_right_logical(r, shift) == lax.shift_right_logical(c, shift)
    each = lambda fn, *lists: [fn(*xs) for xs in zip(*lists)]
    mm = lambda x, y: _dot(x, y, "nn", LO)
    d = each(lambda a: jnp.where(blk, a, 0.0), mats)
    lo = each(lambda a, dd: a - dd, mats, d)
    p = each(lambda dd: -dd, d)
    c_d = p
    for _ in range(shift - 1):
        p = each(mm, p, p)
        c_d = each(lambda cd, pp, prod: cd + pp + prod, c_d, p, each(mm, c_d, p))
    assert GDN_CHUNK // GDN_BASE == 4
    nmat = each(lambda l, prod: l + prod, lo, each(mm, c_d, lo))
    n2 = each(mm, nmat, nmat)
    c_n = each(lambda nn2, nm, prod: (nn2 - nm) - prod, n2, nmat, each(mm, nmat, n2))
    return each(lambda cn, cd, prod: cn + cd + prod, c_n, c_d, each(mm, c_n, c_d))


GDN_AHP = 4


def _gdn_a_specs():
    blk = pl.BlockSpec((GDN_ROWS, GDN_AHP * LANES), lambda i, h: (i, h))
    sq = pl.BlockSpec((GDN_ROWS, GDN_AHP * GDN_ROWS), lambda i, h: (i, h))
    return blk, sq


def _head(ref, hh):
    width = ref.shape[1] // GDN_AHP
    return ref.at[:, hh * width:(hh + 1) * width]


def gdn_a_fwd(q, k, v, gb, bb, rows):
    blk, sq = _gdn_a_specs()

    def body(q_ref, k_ref, v_ref, gb_ref, bb_ref, val_ref, kc_ref, at_ref, qd_ref, kd_ref, t_ref):
        heads = [[_head(r, hh)[...] for r in (q_ref, k_ref, v_ref, gb_ref, bb_ref)] for hh in range(GDN_AHP)]
        t_corr = tri_inverse(list(gdn_f1(*[x for qv, kv, vv, gv, bv in heads for x in (qv, kv, gv, bv)])))
        res = gdn_f2(*[x for t, head in zip(t_corr, heads) for x in (t, *head)])
        for hh in range(GDN_AHP):
            for r, x in zip((val_ref, kc_ref, at_ref, qd_ref, kd_ref, t_ref), (*res[5 * hh:5 * hh + 5], t_corr[hh])):
                _head(r, hh)[...] = x.astype(r.dtype)

    wide = lambda dt: jax.ShapeDtypeStruct((rows, NH * LANES), dt)
    square = jax.ShapeDtypeStruct((rows, NH * GDN_ROWS), BF16)
    return pl.pallas_call(
        body, name="gdn_a_fwd", grid=(rows // GDN_ROWS, NH // GDN_AHP), in_specs=[blk] * 5,
        out_specs=[blk, blk, sq, blk, blk, sq], out_shape=[wide(F32), wide(BF16), square, wide(BF16), wide(BF16), square],
        compiler_params=_params(2))(q, k, v, gb, bb)


def gdn_a_bwd(q, k, v, gb, bb, t_inv, dval, dkc, dat, dqd, dkd, dgb_b, rows):
    blk, sq = _gdn_a_specs()

    def body(q_ref, k_ref, v_ref, gb_ref, bb_ref, t_ref, dval_ref, dkc_ref, dat_ref, dqd_ref, dkd_ref, dgbb_ref,
             dq_ref, dk_ref, dv_ref, dgb_ref, dbb_ref):
        hs = range(GDN_AHP)
        heads = [[_head(r, hh)[...] for r in (q_ref, k_ref, v_ref, gb_ref, bb_ref)] for hh in hs]
        tvs = [_head(t_ref, hh)[...].astype(F32) for hh in hs]
        _, vjp1 = jax.vjp(gdn_f1, *[x for qv, kv, vv, gv, bv in heads for x in (qv, kv, gv, bv)])
        _, vjp2 = jax.vjp(gdn_f2, *[x for t, head in zip(tvs, heads) for x in (t, *head)])
        g2 = vjp2(tuple(_head(r, hh)[...] for hh in hs for r in (dval_ref, dkc_ref, dat_ref, dqd_ref, dkd_ref)))
        dts = [g2[6 * hh] for hh in hs]
        left = _each(lambda dt, tv: dt + _dot(tv, dt, "tn", LO), dts, tvs)
        g1 = vjp1(tuple(_each(lambda lf, tv: -(lf + _dot(lf, tv, "nt", LO)), left, tvs)))
        for hh in hs:
            dq1, dk1, dgb1, dbb1 = g1[4 * hh:4 * hh + 4]
            _, dq2, dk2, dv2, dgb2, dbb2 = g2[6 * hh:6 * hh + 6]
            _head(dq_ref, hh)[...] = dq1 + dq2
            _head(dk_ref, hh)[...] = dk1 + dk2
            _head(dv_ref, hh)[...] = dv2
            _head(dgb_ref, hh)[...] = dgb1 + dgb2 + _head(dgbb_ref, hh)[...]
            _head(dbb_ref, hh)[...] = dbb1 + dbb2

    wide = jax.ShapeDtypeStruct((rows, NH * LANES), F32)
    return pl.pallas_call(
        body, name="gdn_a_bwd", grid=(rows // GDN_ROWS, NH // GDN_AHP),
        in_specs=[blk] * 5 + [sq, blk, blk, sq, blk, blk, blk], out_specs=[blk] * 5, out_shape=[wide] * 5,
        compiler_params=_params(2))(q, k, v, gb, bb, t_inv, dval, dkc, dat, dqd, dkd, dgb_b)


N_CH = GDN_ROWS // GDN_CHUNK


GDN_HP = 8


def gdn_chunk(c):
    def f(*args):
        val, kc, at, qd, kd, gb, s = (list(args[i::7]) for i in range(7))
        zero = jnp.zeros((GDN_CHUNK, LANES), F32)
        v_new = _each(lambda v, k, st: v - _dot(k, st, "nn", LO), val, kc, s)
        v_pad = _each(lambda v: jnp.concatenate([zero] * c + [v] + [zero] * (N_CH - 1 - c), axis=0), v_new)
        out = _each(lambda q, st, a, vp: _dot(q, st, "nn", LO) + _dot(a, vp, "nn", LO), qd, s, at, v_pad)
        dec = _each(lambda g: jnp.exp(jnp.sum(g, axis=0, keepdims=True)), gb)
        s_new = _each(lambda st, d, k, v: st * d + _dot(k, v, "tn", LO), s, dec, kd, v_new)
        return tuple(x for head in zip(out, s_new) for x in head)

    return f


def _gdn_piece(ref, hh, c):
    width = ref.shape[1] // GDN_HP
    return ref.at[c * GDN_CHUNK:(c + 1) * GDN_CHUNK, hh * width:(hh + 1) * width]


def _gdn_snap(ref, hh, c):
    row = (hh * N_CH + c) * LANES
    return ref.at[row:row + LANES, :]


def _gdn_b_specs(nb, rev):
    def blk_row(s, j):
        return s * nb + (nb - 1 - j if rev else j)

    blk = pl.BlockSpec((GDN_ROWS, GDN_HP * LANES), lambda s, hb, j: (blk_row(s, j), hb))
    sq = pl.BlockSpec((GDN_ROWS, GDN_HP * GDN_ROWS), lambda s, hb, j: (blk_row(s, j), hb))
    snap = pl.BlockSpec((GDN_HP * N_CH * LANES, LANES), lambda s, hb, j: (blk_row(s, j) * (NH // GDN_HP) + hb, 0))
    return blk, sq, snap


def gdn_b_fwd(val, kc, at, qd, kd, gb, nseq, seq):
    nb = seq // GDN_ROWS
    rows = nseq * seq
    blk, sq, snap = _gdn_b_specs(nb, False)

    def body(val_ref, kc_ref, at_ref, qd_ref, kd_ref, gb_ref, o_ref, snap_ref, s_ref):
        @pl.when(pl.program_id(2) == 0)
        def _():
            s_ref[...] = jnp.zeros_like(s_ref)

        hs = range(GDN_HP)
        states = [s_ref[hh] for hh in hs]
        for c in range(N_CH):
            for hh in hs:
                _gdn_snap(snap_ref, hh, c)[...] = states[hh]
            res = gdn_chunk(c)(*[x for hh in hs for x in (
                *[_gdn_piece(r, hh, c)[...].astype(F32) for r in (val_ref, kc_ref, at_ref, qd_ref, kd_ref, gb_ref)], states[hh])])
            for hh in hs:
                _gdn_piece(o_ref, hh, c)[...] = res[2 * hh]
            states = [res[2 * hh + 1] for hh in hs]
        for hh in hs:
            s_ref[hh] = states[hh]

    return pl.pallas_call(
        body, name="gdn_b_fwd", grid=(nseq, NH // GDN_HP, nb), in_specs=[blk, blk, sq, blk, blk, blk], out_specs=[blk, snap],
        out_shape=[jax.ShapeDtypeStruct((rows, NH * LANES), F32),
                   jax.ShapeDtypeStruct((nseq * nb * NH * N_CH * LANES, LANES), F32)],
        scratch_shapes=[pltpu.VMEM((GDN_HP, LANES, LANES), F32)], compiler_params=_params(3))(val, kc, at, qd, kd, gb)


def gdn_b_bwd(val, kc, at, qd, kd, gb, snaps, do, nseq, seq):
    nb = seq // GDN_ROWS
    rows = nseq * seq
    blk, sq, snap = _gdn_b_specs(nb, True)

    def body(val_ref, kc_ref, at_ref, qd_ref, kd_ref, gb_ref, snap_ref, do_ref,
             dval_ref, dkc_ref, dat_ref, dqd_ref, dkd_ref, dgb_ref, ds_ref):
        @pl.when(pl.program_id(2) == 0)
        def _():
            ds_ref[...] = jnp.zeros_like(ds_ref)

        hs = range(GDN_HP)
        d_states = [ds_ref[hh] for hh in hs]
        for c in reversed(range(N_CH)):
            _, vjp = jax.vjp(gdn_chunk(c), *[x for hh in hs for x in (
                *[_gdn_piece(r, hh, c)[...].astype(F32) for r in (val_ref, kc_ref, at_ref, qd_ref, kd_ref, gb_ref)],
                _gdn_snap(snap_ref, hh, c)[...])])
            grads = vjp(tuple(x for hh in hs for x in (_gdn_piece(do_ref, hh, c)[...], d_states[hh])))
            for hh in hs:
                for i, r in enumerate([dval_ref, dkc_ref, dat_ref, dqd_ref, dkd_ref, dgb_ref]):
                    _gdn_piece(r, hh, c)[...] = grads[7 * hh + i]
            d_states = [grads[7 * hh + 6] for hh in hs]
        for hh in hs:
            ds_ref[hh] = d_states[hh]

    wide = jax.ShapeDtypeStruct((rows, NH * LANES), F32)
    square = jax.ShapeDtypeStruct((rows, NH * GDN_ROWS), F32)
    return pl.pallas_call(
        body, name="gdn_b_bwd", grid=(nseq, NH // GDN_HP, nb), in_specs=[blk, blk, sq, blk, blk, blk, snap, blk],
        out_specs=[blk, blk, sq, blk, blk, blk], out_shape=[wide, wide, square, wide, wide, wide],
        scratch_shapes=[pltpu.VMEM((GDN_HP, LANES, LANES), F32)], compiler_params=_params(3))(val, kc, at, qd, kd, gb, snaps, do)


FOX_Q, FOX_K, FOX_V = 4 * NH, 5 * NH, 6 * NH
FOX_SCALE = LANES ** -0.5


def _head_row(ct_ref, h, off, width):
    blk = ct_ref[:, pl.ds(off, width)]
    return jnp.sum(jnp.where(_iota(blk.shape, 0) == h, blk, 0.0), axis=0, keepdims=True)


def _col(x):
    return jnp.max(x, axis=1, keepdims=True)


def _row(x):
    return jnp.max(x.T, axis=0, keepdims=True)


def _causal(shape, q_dim):
    return _iota(shape, q_dim) >= _iota(shape, 1 - q_dim)


FOX_HP = 2


def _fox_specs(seq, tile, n_tiles):
    tblk = pl.BlockSpec((tile, FOX_HP * LANES), lambda s, h, i: (s * n_tiles + i, h))
    vtblk = pl.BlockSpec((tile, FOX_HP * LANES), lambda s, h, i: (s * n_tiles + i, h + FOX_V // FOX_HP))
    full = pl.BlockSpec((seq, FOX_HP * LANES), lambda s, h, i: (s, h))
    vfull = pl.BlockSpec((seq, FOX_HP * LANES), lambda s, h, i: (s, h + FOX_V // FOX_HP))
    ctb = pl.BlockSpec((NH, seq), lambda s, h, i: (s * (LANES // NH) + 2, 0))
    return tblk, vtblk, full, vfull, ctb


def _lanes_of(hh):
    return slice(hh * LANES, (hh + 1) * LANES)


def fox_fwd(qn, kn, proj, ct, nseq, seq):
    tq = tk = min(ATT_TILE, seq)
    nq = seq // tq
    rows = nseq * seq
    qblk, _, full, vfull, ctb = _fox_specs(seq, tq, nq)
    hs = range(FOX_HP)

    def body(q_ref, k_ref, v_ref, ct_ref, o_ref, o16_ref, lse_ref):
        hb, i = pl.program_id(1), pl.program_id(2)
        q = [q_ref[:, _lanes_of(hh)] for hh in hs]

        def step(j, carry, diag):
            m, l, acc = (list(carry[t::3]) for t in range(3))
            off = pl.multiple_of(j * tk, tk)
            k = [k_ref[pl.ds(off, tk), _lanes_of(hh)] for hh in hs]
            v = [v_ref[pl.ds(off, tk), _lanes_of(hh)].astype(BF16) for hh in hs]
            ck = [_head_row(ct_ref, hb * FOX_HP + hh, off, tk) for hh in hs]
            s = _each(lambda qq, kk, cc: _dot(qq, kk, "nt") * FOX_SCALE - cc, q, k, ck)
            if diag:
                s = _each(lambda x: jnp.where(_causal(x.shape, 0), x, NEG), s)
            m_new = _each(lambda mm, x: jnp.maximum(mm, jnp.max(x, axis=1, keepdims=True)), m, s)
            p = _each(lambda x, mm: jnp.exp(x - mm), s, m_new)
            alpha = _each(lambda mo, mn: jnp.exp(mo - mn), m, m_new)
            l = _each(lambda a, ll, pp: a * ll + jnp.sum(pp, axis=1, keepdims=True), alpha, l, p)
            acc = _each(lambda a, ac, pp, vv: a * ac + _dot(pp.astype(BF16), vv, "nn"), alpha, acc, p, v)
            return tuple(x for head in zip(m_new, l, acc) for x in head)

        init = (jnp.full((tq, 1), NEG, F32), jnp.zeros((tq, 1), F32), jnp.zeros((tq, LANES), F32)) * FOX_HP
        res = step(i, lax.fori_loop(0, i, lambda j, c: step(j, c, False), init), True)
        for hh in hs:
            m, l, acc = res[3 * hh:3 * hh + 3]
            o = acc / l
            o_ref[:, _lanes_of(hh)] = o
            o16_ref[:, _lanes_of(hh)] = o.astype(BF16)
            lse_ref[:, _lanes_of(hh)] = jnp.broadcast_to(m + jnp.log(l), (tq, LANES))

    wide = (rows, NH * LANES)
    return pl.pallas_call(
        body, name="fox_fwd", grid=(nseq, NH // FOX_HP, nq), in_specs=[qblk, full, vfull, ctb], out_specs=[qblk] * 3,
        out_shape=[jax.ShapeDtypeStruct(wide, F32), jax.ShapeDtypeStruct(wide, BF16), jax.ShapeDtypeStruct(wide, F32)],
        compiler_params=_params(3))(qn, kn, proj, ct)


def fox_dq(qn, kn, proj, ct, do, lse, delta, nseq, seq):
    tq = tk = min(ATT_TILE, seq)
    nq = seq // tq
    rows = nseq * seq
    qblk, _, full, vfull, ctb = _fox_specs(seq, tq, nq)
    hs = range(FOX_HP)

    def body(q_ref, k_ref, v_ref, ct_ref, do_ref, lse_ref, dl_ref, dq_ref, dc_ref):
        hb, i = pl.program_id(1), pl.program_id(2)
        q = [q_ref[:, _lanes_of(hh)] for hh in hs]
        lse = [_col(lse_ref[:, _lanes_of(hh)]) for hh in hs]
        delta = [_col(dl_ref[:, _lanes_of(hh)]) for hh in hs]
        do16 = [do_ref[:, _lanes_of(hh)].astype(BF16) for hh in hs]

        def step(j, carry, diag):
            dq, dc = (list(carry[t::2]) for t in range(2))
            off = pl.multiple_of(j * tk, tk)
            k = [k_ref[pl.ds(off, tk), _lanes_of(hh)] for hh in hs]
            v = [v_ref[pl.ds(off, tk), _lanes_of(hh)].astype(BF16) for hh in hs]
            ck = [_head_row(ct_ref, hb * FOX_HP + hh, off, tk) for hh in hs]
            p = _each(lambda qq, kk, cc, ll: jnp.exp(_dot(qq, kk, "nt") * FOX_SCALE - cc - ll), q, k, ck, lse)
            if diag:
                p = _each(lambda x: jnp.where(_causal(x.shape, 0), x, 0.0), p)
            dp = _each(lambda d, vv: _dot(d, vv, "nt"), do16, v)
            ds = _each(lambda pp, d, dl: pp * (d - dl), p, dp, delta)
            dq = _each(lambda a, x, kk: a + _dot(x.astype(BF16), kk, "nn"), dq, ds, k)
            dc = _each(lambda a, x: a + jnp.sum(x, axis=1, keepdims=True), dc, ds)
            return tuple(x for head in zip(dq, dc) for x in head)

        init = (jnp.zeros((tq, LANES), F32), jnp.zeros((tq, 1), F32)) * FOX_HP
        res = step(i, lax.fori_loop(0, i, lambda j, c: step(j, c, False), init), True)
        for hh in hs:
            dq_ref[:, _lanes_of(hh)] = res[2 * hh] * FOX_SCALE
            dc_ref[:, _lanes_of(hh)] = jnp.where(_iota((tq, LANES), 1) == 0, res[2 * hh + 1], 0.0)

    wide = jax.ShapeDtypeStruct((rows, NH * LANES), F32)
    return pl.pallas_call(
        body, name="fox_dq", grid=(nseq, NH // FOX_HP, nq), in_specs=[qblk, full, vfull, ctb, qblk, qblk, qblk],
        out_specs=[qblk, qblk], out_shape=[wide, wide], compiler_params=_params(3))(qn, kn, proj, ct, do, lse, delta)


def fox_dkv(qn, kn, proj, cb, do, lse, delta, nseq, seq):
    tq = tk = min(ATT_TILE, seq)
    nq = seq // tq
    rows = nseq * seq
    kblk, vblk, full, _, _ = _fox_specs(seq, tk, nq)
    hs = range(FOX_HP)

    def body(q_ref, k_ref, v_ref, cb_ref, do_ref, lse_ref, dl_ref, dk_ref, dv_ref, dc_ref):
        j = pl.program_id(2)
        k = [k_ref[:, _lanes_of(hh)] for hh in hs]
        v16 = [v_ref[:, _lanes_of(hh)].astype(BF16) for hh in hs]
        ck = [_col(cb_ref[:, _lanes_of(hh)]) for hh in hs]

        def step(i, carry, diag):
            dk, dv, dc = (list(carry[t::3]) for t in range(3))
            off = pl.multiple_of(i * tq, tq)
            q = [q_ref[pl.ds(off, tq), _lanes_of(hh)] for hh in hs]
            do16 = [do_ref[pl.ds(off, tq), _lanes_of(hh)].astype(BF16) for hh in hs]
            lse = [_row(lse_ref[pl.ds(off, tq), _lanes_of(hh)]) for hh in hs]
            delta = [_row(dl_ref[pl.ds(off, tq), _lanes_of(hh)]) for hh in hs]
            p = _each(lambda kk, qq, cc, ll: jnp.exp(_dot(kk, qq, "nt") * FOX_SCALE - cc - ll), k, q, ck, lse)
            if diag:
                p = _each(lambda x: jnp.where(_causal(x.shape, 1), x, 0.0), p)
            dv = _each(lambda a, pp, d: a + _dot(pp.astype(BF16), d, "nn"), dv, p, do16)
            ds = _each(lambda pp, vv, d, dl: pp * (_dot(vv, d, "nt") - dl), p, v16, do16, delta)
            dk = _each(lambda a, x, qq: a + _dot(x.astype(BF16), qq, "nn"), dk, ds, q)
            dc = _each(lambda a, x: a + jnp.sum(x, axis=1, keepdims=True), dc, ds)
            return tuple(x for head in zip(dk, dv, dc) for x in head)

        zero = jnp.zeros((tk, LANES), F32)
        carry = step(j, (zero, zero, jnp.zeros((tk, 1), F32)) * FOX_HP, True)
        res = lax.fori_loop(j + 1, nq, lambda i, c: step(i, c, False), carry)
        for hh in hs:
            dk, dv, dc = res[3 * hh:3 * hh + 3]
            dk_ref[:, _lanes_of(hh)] = dk * FOX_SCALE
            dv_ref[:, _lanes_of(hh)] = dv.astype(BF16)
            dc_ref[:, _lanes_of(hh)] = jnp.where(_iota((tk, LANES), 1) == 0, -dc, 0.0)

    wide = (rows, NH * LANES)
    return pl.pallas_call(
        body, name="fox_dkv", grid=(nseq, NH // FOX_HP, nq), in_specs=[full, kblk, vblk, kblk, full, full, full],
        out_specs=[kblk, kblk, kblk],
        out_shape=[jax.ShapeDtypeStruct(wide, F32), jax.ShapeDtypeStruct(wide, BF16), jax.ShapeDtypeStruct(wide, F32)],
        compiler_params=_params(3))(qn, kn, proj, cb, do, lse, delta)


def _adamw_update(w, g, m, v):
    m_new = ADAM_B1 * m + (1.0 - ADAM_B1) * g
    v_new = ADAM_B2 * v + (1.0 - ADAM_B2) * (g * g)
    m_hat = m_new / (1.0 - ADAM_B1 ** ADAM_STEP)
    v_hat = v_new / (1.0 - ADAM_B2 ** ADAM_STEP)
    return -ADAM_LR * (m_hat / (jnp.sqrt(v_hat) + ADAM_EPS) + ADAM_WD * w), m_new, v_new


def adamw(name, w, g, m, v):
    rows, cols = w.shape
    tb = min(rows, 128)
    assert rows % tb == 0
    blk = pl.BlockSpec((tb, cols), lambda i: (i, 0))

    def body(w_ref, g_ref, m_ref, v_ref, d_ref, mo_ref, vo_ref):
        d_ref[...], mo_ref[...], vo_ref[...] = _adamw_update(w_ref[...], g_ref[...], m_ref[...], v_ref[...])

    shp = jax.ShapeDtypeStruct(w.shape, F32)
    return pl.pallas_call(body, name=name, grid=(rows // tb,), in_specs=[blk] * 4, out_specs=[blk] * 3,
                          out_shape=[shp] * 3, compiler_params=_params(1))(w, g, m, v)


SPLIT_TILE = 128


def _tiled(shape2d, ax, n_lead, index):
    blk = (SPLIT_TILE, shape2d[1]) if ax == 0 else (shape2d[0], SPLIT_TILE)

    def index_map(*args):
        *lead, t = index(*args)
        return (*lead, t, 0) if ax == 0 else (*lead, 0, t)

    return pl.BlockSpec((None,) * n_lead + blk, index_map)


def adamw_halves(name, w, mine, other, m, v, c, ax):
    steps = w.shape[ax] // 2 // SPLIT_TILE
    assert w.shape[ax] == 2 * steps * SPLIT_TILE

    def body(c_ref, w_ref, mine_ref, other_ref, m_ref, v_ref, g_ref, d_ref, mo_ref, vo_ref):
        g = jnp.where(pl.program_id(0) // steps == c_ref[0], mine_ref[...], other_ref[...])
        g_ref[...] = g
        d_ref[...], mo_ref[...], vo_ref[...] = _adamw_update(w_ref[...], g, m_ref[...], v_ref[...])

    blk = _tiled(w.shape, ax, 0, lambda i, c_ref: (i,))
    hblk = _tiled(mine.shape, ax, 0, lambda i, c_ref: (i % steps,))
    grid_spec = pltpu.PrefetchScalarGridSpec(num_scalar_prefetch=1, grid=(2 * steps,),
                                             in_specs=[blk, hblk, hblk, blk, blk], out_specs=[blk] * 4)
    shp = jax.ShapeDtypeStruct(w.shape, F32)
    return pl.pallas_call(body, name=name, grid_spec=grid_spec, out_shape=[shp] * 4,
                          compiler_params=_params(1))(c, w, mine, other, m, v)


def add_chips(name, slots, parts, chip, axes):
    outs = []
    for idx, (x, own, ax) in enumerate(zip(slots, parts, axes)):
        n, shape2d = x.shape[0], x.shape[1:]
        steps = shape2d[ax] // SPLIT_TILE
        assert shape2d[ax] == steps * SPLIT_TILE

        def body(me_ref, *refs, n=n):
            o_ref = refs[n + 1]
            acc = None
            for t in range(n):
                term = jnp.where(me_ref[0] == t, refs[n][...], refs[t][...]).astype(F32)
                acc = term if acc is None else acc + term
            o_ref[...] = acc

        def filled(t, n=n):
            return lambda i, me_ref: (jnp.where(me_ref[0] == t, (t + 1) % n, t), i)

        grid_spec = pltpu.PrefetchScalarGridSpec(
            num_scalar_prefetch=1, grid=(steps,),
            in_specs=[_tiled(shape2d, ax, 1, filled(t)) for t in range(n)]
            + [_tiled(shape2d, ax, 1, lambda i, me_ref: (me_ref[0], i))],
            out_specs=_tiled(shape2d, ax, 0, lambda i, me_ref: (i,)))
        outs.append(pl.pallas_call(
            body, name=f"{name}_{idx}", grid_spec=grid_spec, out_shape=jax.ShapeDtypeStruct(shape2d, F32),
            compiler_params=_params(1))(chip, *([x] * n), own))
    return outs


def add_pair(name, gs, rs, c, axes):
    outs = []
    for idx, (g, r, ax) in enumerate(zip(gs, rs, axes)):
        nb = r.shape[0]
        steps = r.shape[1 + ax] // SPLIT_TILE
        assert r.shape[1 + ax] == steps * SPLIT_TILE

        def body(c_ref, g_ref, r_ref, o_ref):
            o_ref[...] = (g_ref[...] + r_ref[...]).astype(BF16)

        grid_spec = pltpu.PrefetchScalarGridSpec(
            num_scalar_prefetch=1, grid=(nb, steps),
            in_specs=[_tiled(g.shape[1:], ax, 1, lambda b, i, c_ref: (b, c_ref[0] * steps + i)),
                      _tiled(r.shape[1:], ax, 1, lambda b, i, c_ref: (b, i))],
            out_specs=_tiled(r.shape[1:], ax, 1, lambda b, i, c_ref: (b, i)))
        outs.append(pl.pallas_call(
            body, name=f"{name}_{idx}", grid_spec=grid_spec, out_shape=jax.ShapeDtypeStruct(r.shape, BF16),
            compiler_params=_params(2))(c, g, r))
    return outs


def _place():
    x, y, c = lax.axis_index("x"), lax.axis_index("y"), lax.axis_index("c")
    return x, y, c, [(1 - x, y), (x, 1 - y), (1 - x, 1 - y)]


def _remote(src, dst, send_sem, recv_sem, dev):
    return pltpu.make_async_remote_copy(src_ref=src, dst_ref=dst, send_sem=send_sem, recv_sem=recv_sem,
                                        device_id=dev, device_id_type=MESH)


def _half(ref, lead, ax, which):
    size = ref.shape[len(lead) + ax] // 2
    part = pl.ds(which * size, size)
    return ref.at[(*lead, part, slice(None)) if ax == 0 else (*lead, slice(None), part)]


def gather_weights(shards, axes):
    n = len(shards)

    def body(*refs):
        ins, outs = refs[:n], refs[n:2 * n]
        ici_s, ici_r, d2d_s, d2d_r = refs[2 * n:]
        x, y, c, chips = _place()
        me = 2 * x + y
        sends, passes = [], []
        for w in range(n):
            cp = _remote(ins[w], outs[w].at[me], d2d_s.at[3 * n + w], d2d_r.at[3 * n + w], (x, y, 1 - c))
            cp.start()
            passes.append(cp)
        for w in range(n):
            for j, (ox, oy) in enumerate(chips):
                cp = _remote(_half(ins[w], (), axes[w], c), _half(outs[w], (me,), axes[w], c),
                             ici_s.at[3 * w + j], ici_r.at[3 * w + j], (ox, oy, c))
                cp.start()
                sends.append(cp)
        for w in range(n):
            for j, (ox, oy) in enumerate(chips):
                landed = _half(outs[w], (2 * ox + oy,), axes[w], c)
                _remote(landed, landed, ici_s.at[3 * w + j], ici_r.at[3 * w + j], (ox, oy, c)).wait_recv()
                cp = _remote(landed, landed, d2d_s.at[3 * w + j], d2d_r.at[3 * w + j], (x, y, 1 - c))
                cp.start()
                passes.append(cp)
        for w in range(n):
            for j, (ox, oy) in enumerate(chips):
                other = _half(outs[w], (2 * ox + oy,), axes[w], 1 - c)
                _remote(other, other, d2d_s.at[3 * w + j], d2d_r.at[3 * w + j], (x, y, 1 - c)).wait_recv()
            own = outs[w].at[me]
            _remote(own, own, d2d_s.at[3 * n + w], d2d_r.at[3 * n + w], (x, y, 1 - c)).wait_recv()
        for cp in sends + passes:
            cp.wait_send()

    return pl.pallas_call(
        body, name="gather_weights", in_specs=[ANY] * n, out_specs=[ANY] * n,
        out_shape=[jax.ShapeDtypeStruct((4,) + s.shape, s.dtype) for s in shards],
        scratch_shapes=[pltpu.SemaphoreType.DMA((3 * n,))] * 2 + [pltpu.SemaphoreType.DMA((4 * n,))] * 2,
    )(*shards)


HBM = pl.BlockSpec(memory_space=pltpu.HBM)
SEM = pl.BlockSpec(memory_space=pltpu.SEMAPHORE)
DATAFLOW = pltpu.SideEffectType.DATAFLOW_SIDE_EFFECTING


def _hbm(a):
    return pltpu.with_memory_space_constraint(a, pltpu.HBM)


class SplitExchange:
    def __init__(self, name, srcs, zone_shapes, n_sems, plan):
        self.name, self.n, self.n_sems, self.plan = name, len(srcs), n_sems, plan
        self.srcs = [_hbm(s) for s in srcs]
        self.zones = [_hbm(lax.empty(shape, s.dtype)) for shape, s in zip(zone_shapes, srcs)]

    def start(self, after):
        n, n_after = self.n, len(after)

        def body(*refs):
            ins, lands = refs[:n], refs[n:2 * n]
            send, recv, token = refs[2 * n + n_after], refs[2 * n + n_after + 1], refs[-1]
            for src, dst, si, ri, dev in self.plan(ins, lands)[0]:
                _remote(src, dst, send.at[si], recv.at[ri], dev).start()
            token[...] = jnp.zeros_like(token)

        res = pl.pallas_call(
            body, name=f"{self.name}_start", in_specs=[HBM] * (2 * n) + [ANY] * n_after,
            out_specs=[SEM, SEM] + [HBM] * (2 * n) + [pl.BlockSpec(memory_space=pltpu.VMEM)],
            out_shape=[pltpu.SemaphoreType.DMA((self.n_sems,)), pltpu.SemaphoreType.DMA((self.n_sems,))]
            + [pltpu.HBM(a.shape, a.dtype) for a in self.srcs + self.zones] + [jax.ShapeDtypeStruct((8, LANES), F32)],
            input_output_aliases={i: 2 + i for i in range(2 * n)},
            compiler_params=pltpu.CompilerParams(has_side_effects=DATAFLOW),
        )(*self.srcs, *self.zones, *after)
        self.sems, self.srcs, self.zones = res[:2], list(res[2:2 + n]), list(res[2 + n:2 + 2 * n])
        return res[-1]

    def wait(self, after):
        n = self.n

        def body(*refs):
            ins, lands = refs[:n], refs[n:2 * n]
            send, recv = refs[2 * n], refs[2 * n + 1]
            sends, arrivals = self.plan(ins, lands)
            for src, _, si, _, dev in sends:
                _remote(src, src, send.at[si], recv.at[si], dev).wait_send()
            for landed, ri in arrivals:
                _remote(landed, landed, send.at[ri], recv.at[ri], _place()[:3]).wait_recv()

        res = pl.pallas_call(
            body, name=f"{self.name}_wait", in_specs=[HBM] * (2 * n) + [SEM, SEM, ANY], out_specs=[HBM] * (2 * n),
            out_shape=[pltpu.HBM(a.shape, a.dtype) for a in self.srcs + self.zones],
            input_output_aliases={i: i for i in range(2 * n)},
            compiler_params=pltpu.CompilerParams(has_side_effects=DATAFLOW),
        )(*self.srcs, *self.zones, *self.sems, after)
        self.srcs = list(res[:n])
        return list(res[n:])


def split_gather(shards):
    n = len(shards)

    def plan(ins, lands):
        x, y, c, chips = _place()
        me = 2 * x + y
        sends, arrivals = [], []
        for w in range(n):
            for j, (ox, oy) in enumerate(chips):
                for k in range(2):
                    base = 2 * (3 * w + j)
                    sends.append((_half(ins[w], (), 0, c), _half(lands[w], (me,), 0, c), base + k, base + c, (ox, oy, k)))
                    arrivals.append((_half(lands[w], (2 * ox + oy,), 0, k), base + k))
            sends.append((ins[w], lands[w].at[me], 6 * n + w, 6 * n + w, (x, y, 1 - c)))
            arrivals.append((lands[w].at[me], 6 * n + w))
        return sends, arrivals

    return SplitExchange("gather", shards, [(4,) + s.shape for s in shards], 7 * n, plan)


def split_pair_swap(name, grads, axes):
    def plan(ins, lands):
        x, y, c, _ = _place()
        sends = [(_half(ins[w], (slice(None),), axes[w], 1 - c), lands[w], w, w, (x, y, 1 - c)) for w in range(len(ins))]
        return sends, [(lands[w], w) for w in range(len(ins))]

    halved = [tuple(d // 2 if i == 1 + ax else d for i, d in enumerate(g.shape)) for g, ax in zip(grads, axes)]
    return SplitExchange(name, grads, halved, len(grads), plan)


def split_chip_exchange(name, parts):
    def plan(ins, lands):
        x, y, c, chips = _place()
        sends, arrivals = [], []
        for w in range(len(ins)):
            for j, (ox, oy) in enumerate(chips):
                sends.append((ins[w].at[2 * ox + oy], lands[w].at[2 * x + y], 3 * w + j, 3 * w + j, (ox, oy, c)))
                arrivals.append((lands[w].at[2 * ox + oy], 3 * w + j))
        return sends, arrivals

    return SplitExchange(name, parts, [p.shape for p in parts], 3 * len(parts), plan)


def split_pair_send(halves):
    def plan(ins, lands):
        x, y, c, _ = _place()
        return ([(ins[w], lands[w], w, w, (x, y, 1 - c)) for w in range(len(ins))],
                [(lands[w], w) for w in range(len(ins))])

    return SplitExchange("pair_send", halves, [h.shape for h in halves], len(halves), plan)


def pair_send(halves):
    n = len(halves)

    def body(*refs):
        ins, outs = refs[:n], refs[n:2 * n]
        send, recv = refs[2 * n:]
        x, y, c, _ = _place()
        cps = [_remote(ins[w], outs[w], send.at[w], recv.at[w], (x, y, 1 - c)) for w in range(n)]
        for cp in cps:
            cp.start()
        for cp in cps:
            cp.wait_recv()
        for cp in cps:
            cp.wait_send()

    return pl.pallas_call(
        body, name="pair_send", in_specs=[ANY] * n, out_specs=[ANY] * n,
        out_shape=[jax.ShapeDtypeStruct(h.shape, h.dtype) for h in halves],
        scratch_shapes=[pltpu.SemaphoreType.DMA((n,))] * 2,
    )(*halves)


def all_reduce_small(name, vec, after=()):
    rows = vec.shape[0]

    def body(v_ref, *refs):
        o_ref, buf, send, recv = refs[len(after):]
        x, y, c, _ = _place()
        me = 4 * x + 2 * y + c
        buf[me] = v_ref[...]
        cps = []
        for k in range(1, 8):
            kx, ky, kc = (k >> 2) & 1, (k >> 1) & 1, k & 1
            peer = (x if kx == 0 else 1 - x, y if ky == 0 else 1 - y, c if kc == 0 else 1 - c)
            cp = _remote(v_ref, buf.at[me], send.at[k - 1], recv.at[k - 1], peer)
            cp.start()
            cps.append(cp)
        for k in range(1, 8):
            kx, ky, kc = (k >> 2) & 1, (k >> 1) & 1, k & 1
            px, py, pc = (x if kx == 0 else 1 - x, y if ky == 0 else 1 - y, c if kc == 0 else 1 - c)
            slot = buf.at[4 * px + 2 * py + pc]
            _remote(slot, slot, send.at[k - 1], recv.at[k - 1], (px, py, pc)).wait_recv()
        for cp in cps:
            cp.wait_send()
        acc = buf[0]
        for d in range(1, 8):
            acc = acc + buf[d]
        o_ref[...] = acc

    vm = pl.BlockSpec(memory_space=pltpu.VMEM)
    return pl.pallas_call(
        body, name=name, in_specs=[vm] + [ANY] * len(after), out_specs=vm, out_shape=jax.ShapeDtypeStruct(vec.shape, F32),
        scratch_shapes=[pltpu.VMEM((8, rows, LANES), F32), pltpu.SemaphoreType.DMA((7,)), pltpu.SemaphoreType.DMA((7,))],
    )(vec, *after)


class NoExchange:
    def __init__(self, late):
        self.late = late

    def late_weights(self, after):
        return self.late

    def reduce_start(self, grads):
        return jnp.zeros((8, LANES), F32)

    def reduce_exchange(self, after):
        return jnp.zeros((8, LANES), F32)

    def reduce_finish(self, after):
        return jnp.zeros((8, LANES), F32)

    def input_grad_start(self, dw_main, dw_small):
        return jnp.zeros((8, LANES), F32)

    def input_grad_exchange(self, after):
        return jnp.zeros((8, LANES), F32)


def local_step(x2, tgt2, g1, g2, gdn_ng, qn_g, kn_g, p1, p2, conv_w, wt_main, wt_small, hooks, nseq, seq):
    rows, dm = x2.shape
    wide = NH * LANES
    row = lambda a, off=0, w=None: (a, "row", off, a.shape[1] if w is None else w)
    rowh = lambda a, off=0, w=LANES: (a, "rowh", off, w)
    par = lambda a: (a, "par", 0, a.shape[1])
    parh = lambda a, off=0: (a, "parh", off, LANES)
    o_row = lambda w, dt: (w, "row", w, dt)
    o_rowh = lambda dt, tw=wide, w=LANES: (tw, "rowh", w, dt)

    u, = ew_fwd("rms1", f_rms, [row(x2), par(g1)], [o_row(dm, BF16)], rows)
    proj = matmul("mm_in", u, wt_main, "nt", BF16)
    sp = matmul("mm_in_small", u, wt_small, "nt", F32)
    so, = ew_fwd("small", f_small, [row(sp), par(p1), par(p2)], [o_row(LANES, F32)], rows)
    cs = cumsum_time("cumsum", so, nseq, seq, False)
    gb, bb, cb = ew_fwd("bcast", f_bcast, [row(so), row(cs)], [o_rowh(F32)] * 3, rows, NH)
    ct = transpose_time("c_time_major", cs, nseq, seq)
    conv = {}
    for mode, off in (("q", 0), ("k", NH), ("v", 2 * NH)):
        conv[mode], = ew_fwd(f"conv_{mode}", make_f_conv(mode), [rowh(proj, off), parh(conv_w, off)], [o_rowh(F32)],
                             rows, NH, seq, "hi", CONV_HEADS)
    val, kcum, attn, qdec, kdec, t_inv = gdn_a_fwd(conv["q"], conv["k"], conv["v"], gb, bb, rows)
    o_a, snaps = gdn_b_fwd(val, kcum, attn, qdec, kdec, gb, nseq, seq)
    ya_in, = ew_fwd("gdn_post", f_post, [rowh(o_a), rowh(proj, 3 * NH), par(gdn_ng)], [o_rowh(BF16)], rows, NH)
    fqn, = ew_fwd("fox_qn", f_rms, [rowh(proj, FOX_Q), par(qn_g)], [o_rowh(BF16)], rows, NH)
    fkn, = ew_fwd("fox_kn", f_rms, [rowh(proj, FOX_K), par(kn_g)], [o_rowh(BF16)], rows, NH)
    o_b, o_b16, lse = fox_fwd(fqn, fkn, proj, ct, nseq, seq)
    p_a, p_b, w_o, w_u, w_d = hooks.late_weights(o_a)
    y_a = matmul("mm_pa", ya_in, p_a, "nn", F32, tn=1024)
    y_b = matmul("mm_pb", o_b16, p_b, "nn", F32, tn=1024)
    gates = [row(proj, 7, dm), row(proj, 8, dm)]
    merged, = ew_fwd("merge", f_merge, gates + [row(y_a), row(y_b)], [o_row(dm, BF16)], rows)
    hres = matmul("mm_out", merged, w_o, "nn", F32, add=x2, tn=1024)
    hn, = ew_fwd("rms2", f_rms, [row(hres), par(g2)], [o_row(dm, BF16)], rows)
    up_blocks = w_u.shape[0]
    act, relu2 = matmul("mm_up", hn, w_u, "nn", F32, col_blocks=up_blocks, out_dtypes=[F32, BF16],
                        epilogue=lambda r: [r, jnp.maximum(r, 0.0) * jnp.maximum(r, 0.0)])
    def loss_tail(r, h_tile, t_tile):
        d = (r + h_tile) - t_tile
        e = (0.5 / dm) * (d * d)
        part = e.reshape(e.shape[0] // 8, 8, e.shape[1]).sum(axis=0)
        part = sum(part[:, t * LANES:(t + 1) * LANES] for t in range(e.shape[1] // LANES))
        g = d * (1.0 / dm)
        return [g, g, part]

    dout, dout16, loss_acc = matmul("mm_down", relu2, w_d, "nn", F32, extras=[hres, tgt2], epilogue=loss_tail,
                                    out_dtypes=[F32, BF16, F32], tile_sums=True, tm=512)

    d_act = matmul("mm_d_act", dout16, w_d, "nt", BF16, extras=[act], epilogue=lambda r, a: [2.0 * jnp.maximum(a, 0.0) * r])
    dw_d = matmul("mm_dw_down", relu2, dout16, "tn", F32, tn=1024)
    dw_u = matmul("mm_dw_up", hn, d_act, "tn", F32, col_blocks=up_blocks)
    d_hn = matmul("mm_d_hn", d_act, w_u, "nt", F32, col_blocks=up_blocks)
    dh, dh16, dg2 = ew_bwd("rms2_b", f_rms, [row(hres), par(g2)], [(row(d_hn),)], [row(dout)],
                           lambda g, e: [g[0] + e[0], g[0] + e[0], g[1]],
                           [((rows, dm), "row", dm, F32, None), ((rows, dm), "row", dm, BF16, None), ((1, dm), "par", dm, F32, "all")], rows)
    d_merged = matmul("mm_d_merged", dh16, w_o, "nt", F32, tn=1024)
    dw_o = matmul("mm_dw_out", merged, dh16, "tn", F32, tn=1024)
    seg16 = ((rows, dm), "row", dm, BF16, None)
    d_ga16, d_gb16, d_ya16, d_yb16 = ew_bwd("merge_b", f_merge, gates + [row(y_a), row(y_b)], [(row(d_merged),)], [],
                                            lambda g, e: list(g), [seg16] * 4, rows)
    dp_a = matmul("mm_dp_a", ya_in, d_ya16, "tn", F32, tn=1024)
    d_ya_in = matmul("mm_d_ya_in", d_ya16, p_a, "nt", F32, tn=1024)
    dp_b = matmul("mm_dp_b", o_b16, d_yb16, "tn", F32, tn=1024)
    d_ob = matmul("mm_d_ob", d_yb16, p_b, "nt", F32, tn=1024)
    token = hooks.reduce_start(dict(p_a=dp_a, p_b=dp_b, w_o=dw_o, w_u=dw_u, w_d=dw_d))
    gdn_ng_t = gdn_ng + token[0, 0]
    h32 = ((rows, wide), "rowh", LANES, F32, None)
    h16 = ((rows, wide), "rowh", LANES, BF16, None)
    gain = ((1, LANES), "par", LANES, F32, "all")
    d_oa, d_z16, d_gdn_ng = ew_bwd("gdn_post_b", f_post, [rowh(o_a), rowh(proj, 3 * NH), par(gdn_ng_t)], [(rowh(d_ya_in),)], [],
                                   lambda g, e: list(g), [h32, h16, gain], rows, NH)
    dval, dkc, dat, dqd, dkd, dgb_b = gdn_b_bwd(val, kcum, attn, qdec, kdec, gb, snaps, d_oa, nseq, seq)
    d_cq, d_ck, d_cv, d_gb, d_bb = gdn_a_bwd(conv["q"], conv["k"], conv["v"], gb, bb, t_inv, dval, dkc, dat, dqd, dkd, dgb_b, rows)
    token = hooks.reduce_exchange(d_cq)
    conv_w_t = conv_w + token[0, 0]
    d_pre, d_conv = {}, {}
    tap = ((4, wide), "parh", LANES, F32, "inner")
    for mode, off, ctg in (("q", 0, d_cq), ("k", NH, d_ck), ("v", 2 * NH, d_cv)):
        d_pre[mode], d_conv[mode] = ew_bwd(f"conv_{mode}_b", make_f_conv(mode), [rowh(proj, off), parh(conv_w_t, off)],
                                           [(rowh(ctg),)], [], lambda g, e: list(g), [h16, tap], rows, NH, seq, "hi", CONV_HEADS)
    delta, = ew_fwd("fox_delta", f_delta, [rowh(d_ob), rowh(o_b)], [o_rowh(F32)], rows, NH, after=[token])
    d_fqn, d_cq_b = fox_dq(fqn, fkn, proj, ct, d_ob, lse, delta, nseq, seq)
    d_fkn, d_fv16, d_ck_b = fox_dkv(fqn, fkn, proj, cb, d_ob, lse, delta, nseq, seq)
    token = hooks.reduce_finish(d_fkn)
    qn_g_t, kn_g_t = qn_g + token[0, 0], kn_g + token[0, 0]
    d_fq16, d_qn_g = ew_bwd("fox_qn_b", f_rms, [rowh(proj, FOX_Q), par(qn_g_t)], [(rowh(d_fqn),)], [], lambda g, e: list(g),
                            [h16, gain], rows, NH)
    d_fk16, d_kn_g = ew_bwd("fox_kn_b", f_rms, [rowh(proj, FOX_K), par(kn_g_t)], [(rowh(d_fkn),)], [], lambda g, e: list(g),
                            [h16, gain], rows, NH)
    narrow = ((rows, LANES), "row", LANES, F32, None)
    d_so, d_cs = ew_bwd("bcast_b", f_bcast, [row(so), row(cs)], [(rowh(d_gb),), (rowh(d_bb),), (rowh(d_cq_b), rowh(d_ck_b))], [],
                        lambda g, e: list(g), [narrow, narrow], rows, NH)
    d_logf = cumsum_time("cumsum_b", d_cs, nseq, seq, True)
    vec = ((1, LANES), "par", LANES, F32, "all")
    d_sp16, d_p1, d_p2 = ew_bwd("small_b", f_small, [row(sp), par(p1), par(p2)], [(row(d_so), row(d_logf))], [],
                                lambda g, e: list(g), [((rows, LANES), "row", LANES, BF16, None), vec, vec], rows)
    d_proj16 = jnp.concatenate([d_pre["q"], d_pre["k"], d_pre["v"], d_z16, d_fq16, d_fk16, d_fv16, d_ga16, d_gb16], axis=1)
    dw_main = matmul("mm_dw_main", d_proj16, u, "tn", F32)
    dw_small = matmul("mm_dw_small", d_sp16, u, "tn", F32)
    wt_small_t = wt_small + hooks.input_grad_start(dw_main, dw_small)[0, 0].astype(BF16)
    d_u = matmul("mm_d_u_small", d_sp16, wt_small_t, "nn", F32)
    d_u = matmul("mm_d_u_first", d_proj16, wt_main, "nn", F32, add=d_u, k_part=(0, 2))
    d_u = matmul("mm_d_u_second", d_proj16, wt_main, "nn", F32, add=d_u, k_part=(1, 2), after=[hooks.input_grad_exchange(d_u)])
    dx, dg1 = ew_bwd("rms1_b", f_rms, [row(x2), par(g1)], [(row(d_u),)], [row(dh)], lambda g, e: [g[0] + e[0], g[1]],
                     [((rows, dm), "row", dm, F32, None), ((1, dm), "par", dm, F32, "all")], rows)
    d_conv_w = jnp.concatenate([d_conv["q"], d_conv["k"], d_conv["v"]], axis=1)
    return dict(loss_acc=loss_acc, dx=dx, g1=dg1, g2=dg2, gdn_ng=d_gdn_ng, qn=d_qn_g, kn=d_kn_g, p1=d_p1, p2=d_p2,
                conv=d_conv_w, w_main=dw_main, w_small=dw_small, p_a=dp_a, p_b=dp_b, w_o=dw_o, w_u=dw_u, w_d=dw_d)


_W = NH * LANES
_A0, _A1 = 4 * _W, 4 * _W + 2 * NH
_B0, _B1 = _A1 + 3 * _W, _A1 + 3 * _W + NH
N_IN = _B1 + 2 * _W


def _split_w_in(full_t):
    main = jnp.concatenate([full_t[:_A0], full_t[_A1:_B0], full_t[_B1:]], axis=0)
    small = jnp.concatenate([full_t[_A0:_A1], full_t[_B0:_B1], jnp.zeros((LANES - 3 * NH, full_t.shape[1]), full_t.dtype)], axis=0)
    return main, small


def _join_w_in(main, small):
    return jnp.concatenate([main[:_A0], small[:2 * NH], main[_A0:_A0 + 3 * _W], small[2 * NH:3 * NH], main[_A0 + 3 * _W:]], axis=0)


def _lanes(v, at=0):
    return jnp.pad(v.reshape(1, -1), ((0, 0), (at, LANES - at - v.size)))


def kernel(x, norm_mix_g, w_in, gdn_conv_w, gdn_a_log, gdn_dt_bias, gdn_norm_g, fox_q_norm_g, fox_k_norm_g, fox_f_bias, w_proj_gdn, w_proj_fox, w_out, norm_mlp_g, w_up, w_down, loss_target, m_norm_mix_g, m_w_in, m_gdn_conv_w, m_gdn_a_log, m_gdn_dt_bias, m_gdn_norm_g, m_fox_q_norm_g, m_fox_k_norm_g, m_fox_f_bias, m_w_proj_gdn, m_w_proj_fox, m_w_out, m_norm_mlp_g, m_w_up, m_w_down, v_norm_mix_g, v_w_in, v_gdn_conv_w, v_gdn_a_log, v_gdn_dt_bias, v_gdn_norm_g, v_fox_q_norm_g, v_fox_k_norm_g, v_fox_f_bias, v_w_proj_gdn, v_w_proj_fox, v_w_out, v_norm_mlp_g, v_w_up, v_w_down):
    nseq, seq, dm = x.shape
    rows = nseq * seq
    xi, yi, ci = lax.axis_index("x"), lax.axis_index("y"), lax.axis_index("c")
    chip = 2 * xi + yi
    conv_cols = gdn_conv_w.shape[2]

    tr = lambda a: jnp.swapaxes(a[0], 0, 1)
    big = [tr(w_in), w_proj_gdn[0], w_proj_fox[0], w_out[0], w_up[0], w_down[0]]
    axes = [1, 0, 0, 0, 0, 0]
    big16 = [w.astype(BF16) for w in big]
    conv_slot = jnp.zeros((4, 4, conv_cols), F32).at[:, chip].set(jnp.where(ci == 0, gdn_conv_w[0], 0.0))
    conv_full = all_reduce_small("gather_conv", conv_slot.reshape(-1, LANES)).reshape(4, 4 * conv_cols)
    got_in, = gather_weights(big16[:1], axes[:1])
    wt_main, wt_small = _split_w_in(got_in.reshape(-1, dm))
    core, chip_no = ci.reshape(1).astype(jnp.int32), chip.reshape(1).astype(jnp.int32)
    gather = split_gather(big16[1:])
    token = gather.start([got_in, conv_full])

    class Hooks:
        def late_weights(self, after):
            g_pa, g_pb, g_wo, w_u, g_wd = gather.wait(after)
            return (*(g.reshape(-1, dm) for g in (g_pa, g_pb, g_wo)), w_u, g_wd.reshape(-1, dm))

        def reduce_start(self, grads):
            blocks = [grads["p_a"].reshape(4, -1, dm), grads["p_b"].reshape(4, -1, dm), grads["w_o"].reshape(4, -1, dm),
                      grads["w_u"], grads["w_d"].reshape(4, -1, dm)]
            self.swap = split_pair_swap("pair_swap_late", blocks, axes[1:])
            return self.swap.start([])

        def reduce_exchange(self, after):
            swapped = self.swap.wait(after)
            self.exchange = split_chip_exchange("chip_exchange_late", add_pair("add_pair_late", self.swap.srcs, swapped, core, axes[1:]))
            return self.exchange.start([])

        def reduce_finish(self, after):
            slots = self.exchange.wait(after)
            self.send = split_pair_send(add_chips("add_chips_late", slots, self.exchange.srcs, chip_no, axes[1:]))
            return self.send.start([])

        def input_grad_start(self, dw_main, dw_small):
            self.in_swap = split_pair_swap("pair_swap_in", [_join_w_in(dw_main, dw_small).reshape(4, -1, dm)], axes[:1])
            return self.in_swap.start([])

        def input_grad_exchange(self, after):
            swapped = self.in_swap.wait(after)
            self.in_exchange = split_chip_exchange("chip_exchange_in", add_pair("add_pair_in", self.in_swap.srcs, swapped, core, axes[:1]))
            return self.in_exchange.start([])

    hooks = Hooks()
    p1 = _lanes(gdn_dt_bias[0]) + _lanes(fox_f_bias[0], 2 * NH)
    p2 = _lanes(gdn_a_log[0])

    g = local_step(x.reshape(rows, dm), loss_target.reshape(rows, dm), norm_mix_g + token[0, 0], norm_mlp_g, gdn_norm_g,
                   fox_q_norm_g, fox_k_norm_g, p1, p2, conv_full, wt_main, wt_small, hooks, nseq, seq)

    others = hooks.send.wait(g["dx"])
    big_m = [tr(m_w_in), m_w_proj_gdn[0], m_w_proj_fox[0], m_w_out[0], m_w_up[0], m_w_down[0]]
    big_v = [tr(v_w_in), v_w_proj_gdn[0], v_w_proj_fox[0], v_w_out[0], v_w_up[0], v_w_down[0]]
    names = ["w_in", "w_proj_gdn", "w_proj_fox", "w_out", "w_up", "w_down"]
    big_res, big_grad = {}, {}
    for i in range(1, len(names)):
        big_grad[names[i]], *big_res[names[i]] = adamw_halves(f"adamw_{names[i]}", big[i], hooks.send.srcs[i - 1], others[i - 1],
                                                              big_m[i], big_v[i], core, axes[i])
    slots = hooks.in_exchange.wait(big_res[names[-1]][0])
    mine = add_chips("add_chips_in", slots, hooks.in_exchange.srcs, chip_no, axes[:1])
    res = adamw_halves("adamw_w_in", big[0], mine[0], pair_send(mine)[0], big_m[0], big_v[0], core, axes[0])
    big_grad["w_in"], *big_res["w_in"] = [jnp.swapaxes(r, 0, 1) for r in res]

    small_parts = [g["loss_acc"], g["g1"].reshape(8, LANES), g["g2"].reshape(8, LANES), g["gdn_ng"], g["qn"], g["kn"], g["p1"], g["p2"],
                   g["conv"].reshape(-1, LANES)]
    tiled = [jnp.pad(p, ((0, -p.shape[0] % 8), (0, 0))) for p in small_parts]
    red = all_reduce_small("reduce_small", jnp.concatenate(tiled, axis=0), slots)
    pos, red_parts = 0, []
    for p, t in zip(small_parts, tiled):
        red_parts.append(red[pos:pos + p.shape[0]])
        pos += t.shape[0]
    r_loss, r_g1, r_g2, r_gdn_ng, r_qn, r_kn, r_p1, r_p2, r_conv = red_parts
    loss = jnp.sum(r_loss)
    g_conv = lax.dynamic_slice_in_dim(r_conv.reshape(4, 4, conv_cols), chip, 1, axis=1).reshape(4, conv_cols)
    small_grads = [r_g1.reshape(1, dm), r_p2[:, :NH], r_p1[:, :NH], r_gdn_ng, r_qn, r_kn, r_p1[:, 2 * NH:3 * NH], r_g2.reshape(1, dm)]
    small_w = [norm_mix_g, gdn_a_log, gdn_dt_bias, gdn_norm_g, fox_q_norm_g, fox_k_norm_g, fox_f_bias, norm_mlp_g]
    small_m = [m_norm_mix_g, m_gdn_a_log, m_gdn_dt_bias, m_gdn_norm_g, m_fox_q_norm_g, m_fox_k_norm_g, m_fox_f_bias, m_norm_mlp_g]
    small_v = [v_norm_mix_g, v_gdn_a_log, v_gdn_dt_bias, v_gdn_norm_g, v_fox_q_norm_g, v_fox_k_norm_g, v_fox_f_bias, v_norm_mlp_g]

    def pack(parts):
        flat = jnp.concatenate([jnp.pad(p.reshape(-1), (0, -p.size % LANES)) for p in parts])
        return jnp.pad(flat, (0, -flat.size % (8 * LANES))).reshape(-1, LANES)

    packed = adamw("adamw_small", pack(small_w + [gdn_conv_w[0]]), pack(small_grads + [g_conv]),
                   pack(small_m + [m_gdn_conv_w[0]]), pack(small_v + [v_gdn_conv_w[0]]))

    def unpack(flat2d):
        flat, pos, res = flat2d.reshape(-1), 0, []
        for p in small_w + [gdn_conv_w[0]]:
            res.append(flat[pos:pos + p.size].reshape(p.shape))
            pos += p.size + (-p.size % LANES)
        return res

    s_delta, s_m, s_v = (unpack(a) for a in packed)

    order = ["norm_mix_g", "w_in", "gdn_conv_w", "gdn_a_log", "gdn_dt_bias", "gdn_norm_g", "fox_q_norm_g", "fox_k_norm_g",
             "fox_f_bias", "w_proj_gdn", "w_proj_fox", "w_out", "norm_mlp_g", "w_up", "w_down"]
    small_names = ["norm_mix_g", "gdn_a_log", "gdn_dt_bias", "gdn_norm_g", "fox_q_norm_g", "fox_k_norm_g", "fox_f_bias", "norm_mlp_g",
                   "gdn_conv_w"]
    small_idx = {nm: i for i, nm in enumerate(small_names)}
    shapes = dict(zip(order, (a.shape for a in (norm_mix_g, w_in, gdn_conv_w, gdn_a_log, gdn_dt_bias, gdn_norm_g, fox_q_norm_g,
                                                 fox_k_norm_g, fox_f_bias, w_proj_gdn, w_proj_fox, w_out, norm_mlp_g, w_up, w_down))))
    grads_out, delta_out, m_out, v_out = [], [], [], []
    for nm in order:
        if nm in big_res:
            d, mm, vv = big_res[nm]
            gr = big_grad[nm]
        else:
            i = small_idx[nm]
            gr = (small_grads + [g_conv])[i]
            d, mm, vv = s_delta[i], s_m[i], s_v[i]
        for lst, val in ((grads_out, gr), (delta_out, d), (m_out, mm), (v_out, vv)):
            lst.append(val.reshape(shapes[nm]))
    return (loss, g["dx"].reshape(x.shape), *grads_out, *delta_out, *m_out, *v_out)
```

```python
import functools

import jax
import jax.numpy as jnp
from jax import lax
from jax.experimental import pallas as pl
from jax.experimental.pallas import tpu as pltpu

F32 = jnp.float32
BF16 = jnp.bfloat16
LANES = 128
NH = 8
EPS = 1e-6
GDN_CHUNK = 64
GDN_ROWS = 256
GDN_BASE = 16
ROW_TILE = 512
CONV_HEADS = 2
ATT_TILE = 512
NEG = -1e30
VMEM_LIMIT_BYTES = 58 * 1024 * 1024
HI = lax.Precision.HIGHEST
LO = lax.Precision.DEFAULT
MESH = pl.DeviceIdType.MESH
ANY = pl.BlockSpec(memory_space=pl.ANY)

ADAM_LR, ADAM_B1, ADAM_B2, ADAM_EPS, ADAM_WD, ADAM_STEP = 0.001, 0.9, 0.999, 1e-08, 0.01, 10


def _params(n_grid):
    return pltpu.CompilerParams(dimension_semantics=("arbitrary",) * n_grid,
                                vmem_limit_bytes=VMEM_LIMIT_BYTES)


def _dot(a, b, dims, precision=None):
    dn = {"nn": (((1,), (0,)), ((), ())), "nt": (((1,), (1,)), ((), ())), "tn": (((0,), (0,)), ((), ()))}[dims]
    return lax.dot_general(a, b, dn, precision=precision, preferred_element_type=F32)


def _iota(shape, dim):
    return lax.broadcasted_iota(jnp.int32, shape, dim)


def _split(x, parts):
    out = []
    for _ in range(parts - 1):
        hi = x.astype(BF16)
        out.append(hi)
        x = x - hi.astype(F32)
    return out + [x.astype(BF16)]


def _dot_mask(mask, b, dims):
    m16 = mask.astype(BF16)
    b1, b2, b3 = _split(b, 3)
    return _dot(m16, b1, dims) + (_dot(m16, b2, dims) + _dot(m16, b3, dims))


@jax.custom_vjp
def mm_mask(mask, b):
    return _dot_mask(mask, b, "nn")


mm_mask.defvjp(lambda mask, b: (_dot_mask(mask, b, "nn"), mask),
               lambda mask, g: (jnp.zeros_like(mask), _dot_mask(mask, g, "tn")))


def matmul(name, a, b, dims, out_dtype, add=None, tm=1024, tn=1024, tk=512, col_blocks=None,
           extras=(), epilogue=None, out_dtypes=None, k_part=None, after=(), tile_sums=False):
    if col_blocks and dims != "tn":
        nb, b_rows, bw = b.shape
        b_shape = (b_rows, nb * bw)
    else:
        b_shape = b.shape
    if dims == "nn":
        (m, k), (_, n) = a.shape, b_shape
    elif dims == "nt":
        (m, k), (n, _) = a.shape, b_shape
    else:
        (k, m), (_, n) = a.shape, b_shape
    k_span = k // (k_part[1] if k_part else 1)
    if col_blocks and dims == "nt":
        k_span = min(k_span, bw)
    tk = k if k <= 1024 else max(t for t in (2048, 1536, 1024, 512, tk) if k_span % t == 0)
    tm, tn, tk = min(tm, m), min(tn, n), min(tk, k)
    assert m % tm == 0 and n % tn == 0 and k % tk == 0, (name, m, n, k)
    k0, nk = (0, k // tk) if k_part is None else (k_part[0] * (k // tk // k_part[1]), k // tk // k_part[1])
    assert k_part is None or (dims == "nn" and not col_blocks and (k // tk) % k_part[1] == 0)
    a_spec = pl.BlockSpec((tk, tm), lambda i, j, kk: (kk, i)) if dims == "tn" else pl.BlockSpec((tm, tk), lambda i, j, kk: (i, kk + k0))
    b_spec = pl.BlockSpec((tn, tk), lambda i, j, kk: (j, kk)) if dims == "nt" else pl.BlockSpec((tk, tn), lambda i, j, kk: (kk + k0, j))
    o_spec = pl.BlockSpec((tm, tn), lambda i, j, kk: (i, j))
    out_shape = (m, n)
    if col_blocks and dims == "nn":
        per = bw // tn
        assert bw % tn == 0
        b_spec = pl.BlockSpec((None, tk, tn), lambda i, j, kk: (j // per, kk, j % per))
    elif col_blocks and dims == "nt":
        per = bw // tk
        assert bw % tk == 0
        b_spec = pl.BlockSpec((None, tn, tk), lambda i, j, kk: (kk // per, j, kk % per))
    elif col_blocks:
        bw = n // col_blocks
        per = bw // tn
        assert bw % tn == 0 and add is None
        o_spec = pl.BlockSpec((None, tm, tn), lambda i, j, kk: (j // per, i, j % per))
        out_shape = (col_blocks, m, bw)
    extras = list(extras) + ([add] if add is not None else [])
    if add is not None:
        assert epilogue is None
        epilogue = lambda r, *e: [r + e[-1]]
    out_dtypes = [out_dtype] if epilogue is None or out_dtypes is None else list(out_dtypes)
    n_ex, n_out = len(extras), len(out_dtypes)

    def body(*refs):
        a_ref, b_ref = refs[0], refs[1]
        ex_refs, o_refs = refs[2:2 + n_ex], refs[2 + n_ex + len(after):2 + n_ex + len(after) + n_out]

        def finish(r):
            res = [r] if epilogue is None else epilogue(r, *[e[...] for e in ex_refs])
            for o_ref, v in zip(o_refs, res):
                o_ref[...] = v.astype(o_ref.dtype)

        if nk == 1:
            finish(_dot(a_ref[...], b_ref[...], dims))
            return
        acc_ref = refs[-1]
        kk = pl.program_id(2)

        @pl.when(kk == 0)
        def _():
            acc_ref[...] = jnp.zeros_like(acc_ref)

        acc_ref[...] += _dot(a_ref[...], b_ref[...], dims)

        @pl.when(kk == nk - 1)
        def _():
            finish(acc_ref[...])

    out_specs = [o_spec] * n_out
    out_shapes = [jax.ShapeDtypeStruct(out_shape, dt) for dt in out_dtypes]
    if tile_sums:
        out_specs[-1] = pl.BlockSpec((8, LANES), lambda i, j, kk: (i, j))
        out_shapes[-1] = jax.ShapeDtypeStruct((8 * (m // tm), LANES * (n // tn)), out_dtypes[-1])
    res = pl.pallas_call(
        body, name=name, grid=(m // tm, n // tn, nk), in_specs=[a_spec, b_spec] + [o_spec] * n_ex + [ANY] * len(after),
        out_specs=out_specs, out_shape=out_shapes,
        scratch_shapes=[pltpu.VMEM((tm, tn), F32)] if nk > 1 else [], compiler_params=_params(3),
    )(a, b, *extras, *after)
    return res[0] if n_out == 1 else res


def _ew_spec(kind, off, width, tb, hp, order, shape=None):
    def ih(g0, g1):
        return (g0, g1) if order == "ih" else (g1, g0)

    assert off % hp == 0 or kind in ("row", "par")
    if kind == "row":
        return pl.BlockSpec((tb, width), lambda g0, g1: (ih(g0, g1)[0], off))
    if kind == "rowh":
        return pl.BlockSpec((tb, hp * width), lambda g0, g1: (ih(g0, g1)[0], ih(g0, g1)[1] + off // hp))
    if kind == "par":
        return pl.BlockSpec(shape, lambda g0, g1: (0, 0))
    if kind == "parh":
        return pl.BlockSpec((shape[0], hp * width), lambda g0, g1: (0, ih(g0, g1)[1] + off // hp))
    raise ValueError(kind)


def _ew_grid(rows, tb, nh, hp, order):
    assert nh % hp == 0 and rows % tb == 0
    return (rows // tb, nh // hp) if order == "ih" else (nh // hp, rows // tb)


def _ew_load(ref, kind, width, hh):
    if kind in ("row", "par"):
        return ref[...].astype(F32)
    return ref[:, hh * width:(hh + 1) * width].astype(F32)


def ew_fwd(name, f, ins, outs, rows, nh=1, tb=ROW_TILE, order="ih", hp=None, after=()):
    hp = nh if hp is None else hp
    n_in = len(ins)

    def body(*refs):
        hb = pl.program_id(1) if order == "ih" else pl.program_id(0)
        for hh in range(hp):
            h = hh if hp == nh else hb * hp + hh
            vals = [_ew_load(r, kd, w, hh) for r, (_, kd, _, w) in zip(refs[:n_in], ins)]
            res = f(h, *vals)
            for r, v, (_, kd, w, _) in zip(refs[n_in + len(after):], res, outs):
                if kd == "row":
                    assert hp == 1
                    r[...] = v.astype(r.dtype)
                else:
                    r[:, hh * w:(hh + 1) * w] = v.astype(r.dtype)

    in_specs = [_ew_spec(kd, off, w, tb, hp, order, a.shape) for (a, kd, off, w) in ins]
    out_specs = [_ew_spec(kd, 0, w, tb, hp, order) for (_, kd, w, _) in outs]
    out_shape = [jax.ShapeDtypeStruct((rows, tw), dt) for (tw, _, _, dt) in outs]
    return pl.pallas_call(
        body, name=name, grid=_ew_grid(rows, tb, nh, hp, order), in_specs=in_specs + [ANY] * len(after), out_specs=out_specs,
        out_shape=out_shape, compiler_params=_params(2),
    )(*[a for (a, _, _, _) in ins], *after)


def ew_bwd(name, f, ins, cts, extras, emit, outs, rows, nh=1, tb=ROW_TILE, order="ih", hp=None):
    hp = nh if hp is None else hp
    n_in = len(ins)
    flat_cts = [d for group in cts for d in group]
    n_ct, n_ex = len(flat_cts), len(extras)

    def body(*refs):
        g0, g1 = pl.program_id(0), pl.program_id(1)
        hb = g1 if order == "ih" else g0
        out_refs = refs[n_in + n_ct + n_ex:]
        shared = [None] * len(outs)

        def store(r, v, first, sl=None):
            def put(val, add):
                if sl is None:
                    r[...] = (r[...] + val if add else val).astype(r.dtype)
                else:
                    r[:, sl] = (r[:, sl] + val if add else val).astype(r.dtype)

            if first is None:
                put(v, False)
            else:
                pl.when(first)(lambda: put(v, False))
                pl.when(jnp.logical_not(first))(lambda: put(v, True))

        for hh in range(hp):
            h = hh if hp == nh else hb * hp + hh
            vals = [_ew_load(r, kd, w, hh) for r, (_, kd, _, w) in zip(refs[:n_in], ins)]
            ct_refs = list(zip(refs[n_in:n_in + n_ct], flat_cts))
            ct_vals, pos = [], 0
            for group in cts:
                v = None
                for r, (_, kd, _, w) in ct_refs[pos:pos + len(group)]:
                    t = _ew_load(r, kd, w, hh)
                    v = t if v is None else v + t
                pos += len(group)
                ct_vals.append(v)
            ex_vals = [_ew_load(r, kd, w, hh) for r, (_, kd, _, w) in zip(refs[n_in + n_ct:n_in + n_ct + n_ex], extras)]
            _, vjp = jax.vjp(lambda *a: f(h, *a), *vals)
            res = emit(vjp(tuple(ct_vals)), ex_vals)
            for idx, (r, v, (_, kd, w, _, acc)) in enumerate(zip(out_refs, res, outs)):
                if kd in ("row", "par"):
                    shared[idx] = v if shared[idx] is None else shared[idx] + v
                else:
                    store(r, v, (g1 == 0) if acc == "inner" else None, slice(hh * w, (hh + 1) * w))
        for idx, (r, (_, kd, _, _, acc)) in enumerate(zip(out_refs, outs)):
            if kd in ("row", "par"):
                assert acc == "all" or hp == nh
                store(r, shared[idx], jnp.logical_and(g0 == 0, g1 == 0) if acc == "all" else None)

    operands = list(ins) + flat_cts + list(extras)
    in_specs = [_ew_spec(kd, off, w, tb, hp, order, a.shape) for (a, kd, off, w) in operands]
    out_specs = [_ew_spec(kd, 0, w, tb, hp, order, shp) for (shp, kd, w, _, _) in outs]
    out_shape = [jax.ShapeDtypeStruct(shp, dt) for (shp, _, _, dt, _) in outs]
    return pl.pallas_call(
        body, name=name, grid=_ew_grid(rows, tb, nh, hp, order), in_specs=in_specs, out_specs=out_specs,
        out_shape=out_shape, compiler_params=_params(2),
    )(*[a for (a, _, _, _) in operands])


def f_rms(h, x, g):
    r = lax.rsqrt(jnp.mean(x * x, axis=-1, keepdims=True) + EPS)
    return (x * r * g,)


def _softplus(z):
    return jnp.maximum(z, 0.0) + jnp.log1p(jnp.exp(-jnp.abs(z)))


def f_small(h, sp, p1, p2):
    lane = _iota(sp.shape, 1)
    z = sp + p1
    g = -jnp.exp(p2) * _softplus(z)
    beta = jax.nn.sigmoid(z)
    logf = -_softplus(-z)
    return (jnp.where(lane < NH, g, jnp.where(lane < 2 * NH, beta, jnp.where(lane < 3 * NH, logf, 0.0))),)


def _pick(x, lane_id):
    lane = _iota(x.shape, 1)
    col = jnp.sum(jnp.where(lane == lane_id, x, 0.0), axis=1, keepdims=True)
    return jnp.broadcast_to(col, x.shape)


def f_bcast(h, so, cs):
    return _pick(so, h), _pick(so, h + NH), _pick(cs, h + 2 * NH)


def _shift_down(s):
    def down(x):
        return jnp.where(_iota(x.shape, 0) >= s, pltpu.roll(x, s, 0), 0.0)

    def up(g):
        n = g.shape[0]
        return jnp.where(_iota(g.shape, 0) < n - s, pltpu.roll(g, n - s, 0), 0.0)

    @jax.custom_vjp
    def shift(x):
        return down(x)

    shift.defvjp(lambda x: (down(x), None), lambda _, g: (up(g),))
    return shift


def _silu(x):
    return x * jax.nn.sigmoid(x)


def make_f_conv(mode):
    sh1, sh2, sh3 = _shift_down(1), _shift_down(2), _shift_down(3)

    def f(h, x, w):
        sub = _iota(w.shape, 0)

        def tap(i):
            return jnp.sum(jnp.where(sub == i, w, 0.0), axis=0, keepdims=True)

        y = sh3(x) * tap(0)
        y = y + sh2(x) * tap(1)
        y = y + sh1(x) * tap(2)
        y = y + x * tap(3)
        s = _silu(y)
        if mode == "v":
            return (s,)
        n = s * lax.rsqrt(jnp.sum(s * s, axis=-1, keepdims=True) + EPS)
        if mode == "q":
            n = n * (LANES ** -0.5)
        return (n,)

    return f


def f_post(h, o, z, g):
    r = lax.rsqrt(jnp.mean(o * o, axis=-1, keepdims=True) + EPS)
    return (o * r * g * _silu(z),)


def f_merge(h, ga, gb, ya, yb):
    return (jax.nn.sigmoid(ga) * ya + jax.nn.sigmoid(gb) * yb,)


def f_delta(h, do, o):
    return (jnp.broadcast_to(jnp.sum(do * o, axis=1, keepdims=True), o.shape),)


def cumsum_time(name, x, nseq, seq, reverse):
    nb = seq // LANES

    def body(x_ref, o_ref):
        r, c = _iota((LANES, LANES), 0), _iota((LANES, LANES), 1)
        tri = jnp.where((r <= c) if reverse else (r >= c), 1.0, 0.0).astype(F32)
        carry = jnp.zeros((1, LANES), F32)
        for b in (range(nb - 1, -1, -1) if reverse else range(nb)):
            blk = x_ref[b * LANES:(b + 1) * LANES, :]
            o_ref[b * LANES:(b + 1) * LANES, :] = _dot_mask(tri, blk, "nn") + carry
            carry = carry + jnp.sum(blk, axis=0, keepdims=True)

    spec = pl.BlockSpec((seq, LANES), lambda s: (s, 0))
    return pl.pallas_call(body, name=name, grid=(nseq,), in_specs=[spec], out_specs=spec,
                          out_shape=jax.ShapeDtypeStruct(x.shape, F32), compiler_params=_params(1))(x)


def transpose_time(name, x, nseq, seq):
    def body(x_ref, o_ref):
        o_ref[...] = x_ref[...].T

    return pl.pallas_call(
        body, name=name, grid=(nseq,), in_specs=[pl.BlockSpec((seq, LANES), lambda s: (s, 0))],
        out_specs=pl.BlockSpec((LANES, seq), lambda s: (s, 0)),
        out_shape=jax.ShapeDtypeStruct((nseq * LANES, seq), F32), compiler_params=_params(1))(x)


def _gdn_masks():
    n = GDN_ROWS
    r, c = _iota((n, n), 0), _iota((n, n), 1)
    shift = GDN_CHUNK.bit_length() - 1
    same = lax.shift_right_logical(r, shift) == lax.shift_right_logical(c, shift)
    return r, c, same


def _each(fn, *lists):
    return [fn(*xs) for xs in zip(*lists)]


def _gdn_decay(gbs):
    r, c, same = _gdn_masks()
    seg_tril = jnp.where(jnp.logical_and(same, r >= c), 1.0, 0.0).astype(F32)
    g_cum = _each(lambda gb: mm_mask(seg_tril, gb), gbs)
    lane0 = _iota(gbs[0].shape, 1) == 0
    g_col = _each(lambda g: jnp.sum(jnp.where(lane0, g, 0.0), axis=1, keepdims=True), g_cum)
    g_row = _each(lambda g: jnp.sum(jnp.where(r == c, jnp.broadcast_to(g, (GDN_ROWS, GDN_ROWS)), 0.0), axis=0, keepdims=True), g_col)
    return g_cum, _each(lambda a, b: a - b, g_col, g_row)


def gdn_f1(*args):
    qs, ks, gbs, bbs = (list(args[i::4]) for i in range(4))
    r, c, same = _gdn_masks()
    strict = jnp.logical_and(same, r > c)
    _, diff = _gdn_decay(gbs)
    lane0 = _iota(bbs[0].shape, 1) == 0
    beta_col = _each(lambda bb: jnp.sum(jnp.where(lane0, bb, 0.0), axis=1, keepdims=True), bbs)
    kk = _each(lambda k: _dot(k, k, "nt", LO), ks)
    return tuple(_each(lambda b, x, d: jnp.where(strict, b * x * jnp.exp(jnp.where(strict, d, 0.0)), 0.0), beta_col, kk, diff))


def gdn_f2(*args):
    ts, qs, ks, vs, gbs, bbs = (list(args[i::6]) for i in range(6))
    r, c, same = _gdn_masks()
    incl = jnp.logical_and(same, r >= c)
    g_cum, diff = _gdn_decay(gbs)
    decay = _each(lambda d: jnp.where(incl, jnp.exp(jnp.where(incl, d, 0.0)), 0.0), diff)
    e_g = _each(jnp.exp, g_cum)
    v_beta = _each(lambda v, bb: v * bb, vs, bbs)
    k_beta = _each(lambda k, bb, e: k * bb * e, ks, bbs, e_g)
    value = _each(lambda t, x: x + _dot(t, x, "nn", LO), ts, v_beta)
    k_cum = _each(lambda t, x: x + _dot(t, x, "nn", LO), ts, k_beta)
    attn = _each(lambda q, k, d: _dot(q, k, "nt", LO) * d, qs, ks, decay)
    ones = jnp.where(same, 1.0, 0.0).astype(F32)
    g_last = _each(lambda gb: mm_mask(ones, gb), gbs)
    q_dec = _each(lambda q, e: q * e, qs, e_g)
    k_dec = _each(lambda k, gl, g: k * jnp.exp(gl - g), ks, g_last, g_cum)
    return tuple(x for head in zip(value, k_cum, attn, q_dec, k_dec) for x in head)


def tri_inverse(mats):
    n = GDN_ROWS
    r, c = _iota((n, n), 0), _iota((n, n), 1)
    shift = GDN_BASE.bit_length() - 1
    blk = lax.shift_right_logical(r, shift) == lax.shift_right_logical(c, shift)
    each = lambda fn, *lists: [fn(*xs) for xs in zip(*lists)]
    mm = lambda x, y: _dot(x, y, "nn", LO)
    d = each(lambda a: jnp.where(blk, a, 0.0), mats)
    lo = each(lambda a, dd: a - dd, mats, d)
    p = each(lambda dd: -dd, d)
    c_d = p
    for _ in range(shift - 1):
        p = each(mm, p, p)
        c_d = each(lambda cd, pp, prod: cd + pp + prod, c_d, p, each(mm, c_d, p))
    assert GDN_CHUNK // GDN_BASE == 4
    nmat = each(lambda l, prod: l + prod, lo, each(mm, c_d, lo))
    n2 = each(mm, nmat, nmat)
    c_n = each(lambda nn2, nm, prod: (nn2 - nm) - prod, n2, nmat, each(mm, nmat, n2))
    return each(lambda cn, cd, prod: cn + cd + prod, c_n, c_d, each(mm, c_n, c_d))


GDN_AHP = 4


def _gdn_a_specs():
    blk = pl.BlockSpec((GDN_ROWS, GDN_AHP * LANES), lambda i, h: (i, h))
    sq = pl.BlockSpec((GDN_ROWS, GDN_AHP * GDN_ROWS), lambda i, h: (i, h))
    return blk, sq


def _head(ref, hh):
    width = ref.shape[1] // GDN_AHP
    return ref.at[:, hh * width:(hh + 1) * width]


def gdn_a_fwd(q, k, v, gb, bb, rows):
    blk, sq = _gdn_a_specs()

    def body(q_ref, k_ref, v_ref, gb_ref, bb_ref, val_ref, kc_ref, at_ref, qd_ref, kd_ref, t_ref):
        heads = [[_head(r, hh)[...] for r in (q_ref, k_ref, v_ref, gb_ref, bb_ref)] for hh in range(GDN_AHP)]
        t_corr = tri_inverse(list(gdn_f1(*[x for qv, kv, vv, gv, bv in heads for x in (qv, kv, gv, bv)])))
        res = gdn_f2(*[x for t, head in zip(t_corr, heads) for x in (t, *head)])
        for hh in range(GDN_AHP):
            for r, x in zip((val_ref, kc_ref, at_ref, qd_ref, kd_ref, t_ref), (*res[5 * hh:5 * hh + 5], t_corr[hh])):
                _head(r, hh)[...] = x.astype(r.dtype)

    wide = lambda dt: jax.ShapeDtypeStruct((rows, NH * LANES), dt)
    square = jax.ShapeDtypeStruct((rows, NH * GDN_ROWS), BF16)
    return pl.pallas_call(
        body, name="gdn_a_fwd", grid=(rows // GDN_ROWS, NH // GDN_AHP), in_specs=[blk] * 5,
        out_specs=[blk, blk, sq, blk, blk, sq], out_shape=[wide(F32), wide(BF16), square, wide(BF16), wide(BF16), square],
        compiler_params=_params(2))(q, k, v, gb, bb)


def gdn_a_bwd(q, k, v, gb, bb, t_inv, dval, dkc, dat, dqd, dkd, dgb_b, rows):
    blk, sq = _gdn_a_specs()

    def body(q_ref, k_ref, v_ref, gb_ref, bb_ref, t_ref, dval_ref, dkc_ref, dat_ref, dqd_ref, dkd_ref, dgbb_ref,
             dq_ref, dk_ref, dv_ref, dgb_ref, dbb_ref):
        hs = range(GDN_AHP)
        heads = [[_head(r, hh)[...] for r in (q_ref, k_ref, v_ref, gb_ref, bb_ref)] for hh in hs]
        tvs = [_head(t_ref, hh)[...].astype(F32) for hh in hs]
        _, vjp1 = jax.vjp(gdn_f1, *[x for qv, kv, vv, gv, bv in heads for x in (qv, kv, gv, bv)])
        _, vjp2 = jax.vjp(gdn_f2, *[x for t, head in zip(tvs, heads) for x in (t, *head)])
        g2 = vjp2(tuple(_head(r, hh)[...] for hh in hs for r in (dval_ref, dkc_ref, dat_ref, dqd_ref, dkd_ref)))
        dts = [g2[6 * hh] for hh in hs]
        left = _each(lambda dt, tv: dt + _dot(tv, dt, "tn", LO), dts, tvs)
        g1 = vjp1(tuple(_each(lambda lf, tv: -(lf + _dot(lf, tv, "nt", LO)), left, tvs)))
        for hh in hs:
            dq1, dk1, dgb1, dbb1 = g1[4 * hh:4 * hh + 4]
            _, dq2, dk2, dv2, dgb2, dbb2 = g2[6 * hh:6 * hh + 6]
            _head(dq_ref, hh)[...] = dq1 + dq2
            _head(dk_ref, hh)[...] = dk1 + dk2
            _head(dv_ref, hh)[...] = dv2
            _head(dgb_ref, hh)[...] = dgb1 + dgb2 + _head(dgbb_ref, hh)[...]
            _head(dbb_ref, hh)[...] = dbb1 + dbb2

    wide = jax.ShapeDtypeStruct((rows, NH * LANES), F32)
    return pl.pallas_call(
        body, name="gdn_a_bwd", grid=(rows // GDN_ROWS, NH // GDN_AHP),
        in_specs=[blk] * 5 + [sq, blk, blk, sq, blk, blk, blk], out_specs=[blk] * 5, out_shape=[wide] * 5,
        compiler_params=_params(2))(q, k, v, gb, bb, t_inv, dval, dkc, dat, dqd, dkd, dgb_b)


N_CH = GDN_ROWS // GDN_CHUNK


GDN_HP = 8


def gdn_chunk(c):
    def f(*args):
        val, kc, at, qd, kd, gb, s = (list(args[i::7]) for i in range(7))
        zero = jnp.zeros((GDN_CHUNK, LANES), F32)
        v_new = _each(lambda v, k, st: v - _dot(k, st, "nn", LO), val, kc, s)
        v_pad = _each(lambda v: jnp.concatenate([zero] * c + [v] + [zero] * (N_CH - 1 - c), axis=0), v_new)
        out = _each(lambda q, st, a, vp: _dot(q, st, "nn", LO) + _dot(a, vp, "nn", LO), qd, s, at, v_pad)
        dec = _each(lambda g: jnp.exp(jnp.sum(g, axis=0, keepdims=True)), gb)
        s_new = _each(lambda st, d, k, v: st * d + _dot(k, v, "tn", LO), s, dec, kd, v_new)
        return tuple(x for head in zip(out, s_new) for x in head)

    return f


def _gdn_piece(ref, hh, c):
    width = ref.shape[1] // GDN_HP
    return ref.at[c * GDN_CHUNK:(c + 1) * GDN_CHUNK, hh * width:(hh + 1) * width]


def _gdn_snap(ref, hh, c):
    row = (hh * N_CH + c) * LANES
    return ref.at[row:row + LANES, :]


def _gdn_b_specs(nb, rev):
    def blk_row(s, j):
        return s * nb + (nb - 1 - j if rev else j)

    blk = pl.BlockSpec((GDN_ROWS, GDN_HP * LANES), lambda s, hb, j: (blk_row(s, j), hb))
    sq = pl.BlockSpec((GDN_ROWS, GDN_HP * GDN_ROWS), lambda s, hb, j: (blk_row(s, j), hb))
    snap = pl.BlockSpec((GDN_HP * N_CH * LANES, LANES), lambda s, hb, j: (blk_row(s, j) * (NH // GDN_HP) + hb, 0))
    return blk, sq, snap


def gdn_b_fwd(val, kc, at, qd, kd, gb, nseq, seq):
    nb = seq // GDN_ROWS
    rows = nseq * seq
    blk, sq, snap = _gdn_b_specs(nb, False)

    def body(val_ref, kc_ref, at_ref, qd_ref, kd_ref, gb_ref, o_ref, snap_ref, s_ref):
        @pl.when(pl.program_id(2) == 0)
        def _():
            s_ref[...] = jnp.zeros_like(s_ref)

        hs = range(GDN_HP)
        states = [s_ref[hh] for hh in hs]
        for c in range(N_CH):
            for hh in hs:
                _gdn_snap(snap_ref, hh, c)[...] = states[hh]
            res = gdn_chunk(c)(*[x for hh in hs for x in (
                *[_gdn_piece(r, hh, c)[...].astype(F32) for r in (val_ref, kc_ref, at_ref, qd_ref, kd_ref, gb_ref)], states[hh])])
            for hh in hs:
                _gdn_piece(o_ref, hh, c)[...] = res[2 * hh]
            states = [res[2 * hh + 1] for hh in hs]
        for hh in hs:
            s_ref[hh] = states[hh]

    return pl.pallas_call(
        body, name="gdn_b_fwd", grid=(nseq, NH // GDN_HP, nb), in_specs=[blk, blk, sq, blk, blk, blk], out_specs=[blk, snap],
        out_shape=[jax.ShapeDtypeStruct((rows, NH * LANES), F32),
                   jax.ShapeDtypeStruct((nseq * nb * NH * N_CH * LANES, LANES), F32)],
        scratch_shapes=[pltpu.VMEM((GDN_HP, LANES, LANES), F32)], compiler_params=_params(3))(val, kc, at, qd, kd, gb)


def gdn_b_bwd(val, kc, at, qd, kd, gb, snaps, do, nseq, seq):
    nb = seq // GDN_ROWS
    rows = nseq * seq
    blk, sq, snap = _gdn_b_specs(nb, True)

    def body(val_ref, kc_ref, at_ref, qd_ref, kd_ref, gb_ref, snap_ref, do_ref,
             dval_ref, dkc_ref, dat_ref, dqd_ref, dkd_ref, dgb_ref, ds_ref):
        @pl.when(pl.program_id(2) == 0)
        def _():
            ds_ref[...] = jnp.zeros_like(ds_ref)

        hs = range(GDN_HP)
        d_states = [ds_ref[hh] for hh in hs]
        for c in reversed(range(N_CH)):
            _, vjp = jax.vjp(gdn_chunk(c), *[x for hh in hs for x in (
                *[_gdn_piece(r, hh, c)[...].astype(F32) for r in (val_ref, kc_ref, at_ref, qd_ref, kd_ref, gb_ref)],
                _gdn_snap(snap_ref, hh, c)[...])])
            grads = vjp(tuple(x for hh in hs for x in (_gdn_piece(do_ref, hh, c)[...], d_states[hh])))
            for hh in hs:
                for i, r in enumerate([dval_ref, dkc_ref, dat_ref, dqd_ref, dkd_ref, dgb_ref]):
                    _gdn_piece(r, hh, c)[...] = grads[7 * hh + i]
            d_states = [grads[7 * hh + 6] for hh in hs]
        for hh in hs:
            ds_ref[hh] = d_states[hh]

    wide = jax.ShapeDtypeStruct((rows, NH * LANES), F32)
    square = jax.ShapeDtypeStruct((rows, NH * GDN_ROWS), F32)
    return pl.pallas_call(
        body, name="gdn_b_bwd", grid=(nseq, NH // GDN_HP, nb), in_specs=[blk, blk, sq, blk, blk, blk, snap, blk],
        out_specs=[blk, blk, sq, blk, blk, blk], out_shape=[wide, wide, square, wide, wide, wide],
        scratch_shapes=[pltpu.VMEM((GDN_HP, LANES, LANES), F32)], compiler_params=_params(3))(val, kc, at, qd, kd, gb, snaps, do)


FOX_Q, FOX_K, FOX_V = 4 * NH, 5 * NH, 6 * NH
FOX_SCALE = LANES ** -0.5


def _head_row(ct_ref, h, off, width):
    blk = ct_ref[:, pl.ds(off, width)]
    return jnp.sum(jnp.where(_iota(blk.shape, 0) == h, blk, 0.0), axis=0, keepdims=True)


def _col(x):
    return jnp.max(x, axis=1, keepdims=True)


def _row(x):
    return jnp.max(x.T, axis=0, keepdims=True)


def _causal(shape, q_dim):
    return _iota(shape, q_dim) >= _iota(shape, 1 - q_dim)


FOX_HP = 4


def _fox_specs(seq, tile, n_tiles):
    tblk = pl.BlockSpec((tile, FOX_HP * LANES), lambda s, h, i: (s * n_tiles + i, h))
    vtblk = pl.BlockSpec((tile, FOX_HP * LANES), lambda s, h, i: (s * n_tiles + i, h + FOX_V // FOX_HP))
    full = pl.BlockSpec((seq, FOX_HP * LANES), lambda s, h, i: (s, h))
    vfull = pl.BlockSpec((seq, FOX_HP * LANES), lambda s, h, i: (s, h + FOX_V // FOX_HP))
    ctb = pl.BlockSpec((NH, seq), lambda s, h, i: (s * (LANES // NH) + 2, 0))
    return tblk, vtblk, full, vfull, ctb


def _lanes_of(hh):
    return slice(hh * LANES, (hh + 1) * LANES)


def fox_fwd(qn, kn, proj, ct, nseq, seq):
    tq = tk = min(ATT_TILE, seq)
    nq = seq // tq
    rows = nseq * seq
    qblk, _, full, vfull, ctb = _fox_specs(seq, tq, nq)
    hs = range(FOX_HP)

    def body(q_ref, k_ref, v_ref, ct_ref, o_ref, o16_ref, lse_ref):
        hb, i = pl.program_id(1), pl.program_id(2)
        q = [q_ref[:, _lanes_of(hh)] for hh in hs]

        def step(j, carry, diag):
            m, l, acc = (list(carry[t::3]) for t in range(3))
            off = pl.multiple_of(j * tk, tk)
            k = [k_ref[pl.ds(off, tk), _lanes_of(hh)] for hh in hs]
            v = [v_ref[pl.ds(off, tk), _lanes_of(hh)].astype(BF16) for hh in hs]
            ck = [_head_row(ct_ref, hb * FOX_HP + hh, off, tk) for hh in hs]
            s = _each(lambda qq, kk, cc: _dot(qq, kk, "nt") * FOX_SCALE - cc, q, k, ck)
            if diag:
                s = _each(lambda x: jnp.where(_causal(x.shape, 0), x, NEG), s)
            m_new = _each(lambda mm, x: jnp.maximum(mm, jnp.max(x, axis=1, keepdims=True)), m, s)
            p = _each(lambda x, mm: jnp.exp(x - mm), s, m_new)
            alpha = _each(lambda mo, mn: jnp.exp(mo - mn), m, m_new)
            l = _each(lambda a, ll, pp: a * ll + jnp.sum(pp, axis=1, keepdims=True), alpha, l, p)
            acc = _each(lambda a, ac, pp, vv: a * ac + _dot(pp.astype(BF16), vv, "nn"), alpha, acc, p, v)
            return tuple(x for head in zip(m_new, l, acc) for x in head)

        init = (jnp.full((tq, 1), NEG, F32), jnp.zeros((tq, 1), F32), jnp.zeros((tq, LANES), F32)) * FOX_HP
        res = step(i, lax.fori_loop(0, i, lambda j, c: step(j, c, False), init), True)
        for hh in hs:
            m, l, acc = res[3 * hh:3 * hh + 3]
            o = acc / l
            o_ref[:, _lanes_of(hh)] = o
            o16_ref[:, _lanes_of(hh)] = o.astype(BF16)
            lse_ref[:, _lanes_of(hh)] = jnp.broadcast_to(m + jnp.log(l), (tq, LANES))

    wide = (rows, NH * LANES)
    return pl.pallas_call(
        body, name="fox_fwd", grid=(nseq, NH // FOX_HP, nq), in_specs=[qblk, full, vfull, ctb], out_specs=[qblk] * 3,
        out_shape=[jax.ShapeDtypeStruct(wide, F32), jax.ShapeDtypeStruct(wide, BF16), jax.ShapeDtypeStruct(wide, F32)],
        compiler_params=_params(3))(qn, kn, proj, ct)


def fox_dq(qn, kn, proj, ct, do, lse, delta, nseq, seq):
    tq = tk = min(ATT_TILE, seq)
    nq = seq // tq
    rows = nseq * seq
    qblk, _, full, vfull, ctb = _fox_specs(seq, tq, nq)
    hs = range(FOX_HP)

    def body(q_ref, k_ref, v_ref, ct_ref, do_ref, lse_ref, dl_ref, dq_ref, dc_ref):
        hb, i = pl.program_id(1), pl.program_id(2)
        q = [q_ref[:, _lanes_of(hh)] for hh in hs]
        lse = [_col(lse_ref[:, _lanes_of(hh)]) for hh in hs]
        delta = [_col(dl_ref[:, _lanes_of(hh)]) for hh in hs]
        do16 = [do_ref[:, _lanes_of(hh)].astype(BF16) for hh in hs]

        def step(j, carry, diag):
            dq, dc = (list(carry[t::2]) for t in range(2))
            off = pl.multiple_of(j * tk, tk)
            k = [k_ref[pl.ds(off, tk), _lanes_of(hh)] for hh in hs]
            v = [v_ref[pl.ds(off, tk), _lanes_of(hh)].astype(BF16) for hh in hs]
            ck = [_head_row(ct_ref, hb * FOX_HP + hh, off, tk) for hh in hs]
            p = _each(lambda qq, kk, cc, ll: jnp.exp(_dot(qq, kk, "nt") * FOX_SCALE - cc - ll), q, k, ck, lse)
            if diag:
                p = _each(lambda x: jnp.where(_causal(x.shape, 0), x, 0.0), p)
            dp = _each(lambda d, vv: _dot(d, vv, "nt"), do16, v)
            ds = _each(lambda pp, d, dl: pp * (d - dl), p, dp, delta)
            dq = _each(lambda a, x, kk: a + _dot(x.astype(BF16), kk, "nn"), dq, ds, k)
            dc = _each(lambda a, x: a + jnp.sum(x, axis=1, keepdims=True), dc, ds)
            return tuple(x for head in zip(dq, dc) for x in head)

        init = (jnp.zeros((tq, LANES), F32), jnp.zeros((tq, 1), F32)) * FOX_HP
        res = step(i, lax.fori_loop(0, i, lambda j, c: step(j, c, False), init), True)
        for hh in hs:
            dq_ref[:, _lanes_of(hh)] = res[2 * hh] * FOX_SCALE
            dc_ref[:, _lanes_of(hh)] = jnp.where(_iota((tq, LANES), 1) == 0, res[2 * hh + 1], 0.0)

    wide = jax.ShapeDtypeStruct((rows, NH * LANES), F32)
    return pl.pallas_call(
        body, name="fox_dq", grid=(nseq, NH // FOX_HP, nq), in_specs=[qblk, full, vfull, ctb, qblk, qblk, qblk],
        out_specs=[qblk, qblk], out_shape=[wide, wide], compiler_params=_params(3))(qn, kn, proj, ct, do, lse, delta)


def fox_dkv(qn, kn, proj, cb, do, lse, delta, nseq, seq):
    tq = tk = min(ATT_TILE, seq)
    nq = seq // tq
    rows = nseq * seq
    kblk, vblk, full, _, _ = _fox_specs(seq, tk, nq)
    hs = range(FOX_HP)

    def body(q_ref, k_ref, v_ref, cb_ref, do_ref, lse_ref, dl_ref, dk_ref, dv_ref, dc_ref):
        j = pl.program_id(2)
        k = [k_ref[:, _lanes_of(hh)] for hh in hs]
        v16 = [v_ref[:, _lanes_of(hh)].astype(BF16) for hh in hs]
        ck = [_col(cb_ref[:, _lanes_of(hh)]) for hh in hs]

        def step(i, carry, diag):
            dk, dv, dc = (list(carry[t::3]) for t in range(3))
            off = pl.multiple_of(i * tq, tq)
            q = [q_ref[pl.ds(off, tq), _lanes_of(hh)] for hh in hs]
            do16 = [do_ref[pl.ds(off, tq), _lanes_of(hh)].astype(BF16) for hh in hs]
            lse = [_row(lse_ref[pl.ds(off, tq), _lanes_of(hh)]) for hh in hs]
            delta = [_row(dl_ref[pl.ds(off, tq), _lanes_of(hh)]) for hh in hs]
            p = _each(lambda kk, qq, cc, ll: jnp.exp(_dot(kk, qq, "nt") * FOX_SCALE - cc - ll), k, q, ck, lse)
            if diag:
                p = _each(lambda x: jnp.where(_causal(x.shape, 1), x, 0.0), p)
            dv = _each(lambda a, pp, d: a + _dot(pp.astype(BF16), d, "nn"), dv, p, do16)
            ds = _each(lambda pp, vv, d, dl: pp * (_dot(vv, d, "nt") - dl), p, v16, do16, delta)
            dk = _each(lambda a, x, qq: a + _dot(x.astype(BF16), qq, "nn"), dk, ds, q)
            dc = _each(lambda a, x: a + jnp.sum(x, axis=1, keepdims=True), dc, ds)
            return tuple(x for head in zip(dk, dv, dc) for x in head)

        zero = jnp.zeros((tk, LANES), F32)
        carry = step(j, (zero, zero, jnp.zeros((tk, 1), F32)) * FOX_HP, True)
        res = lax.fori_loop(j + 1, nq, lambda i, c: step(i, c, False), carry)
        for hh in hs:
            dk, dv, dc = res[3 * hh:3 * hh + 3]
            dk_ref[:, _lanes_of(hh)] = dk * FOX_SCALE
            dv_ref[:, _lanes_of(hh)] = dv.astype(BF16)
            dc_ref[:, _lanes_of(hh)] = jnp.where(_iota((tk, LANES), 1) == 0, -dc, 0.0)

    wide = (rows, NH * LANES)
    return pl.pallas_call(
        body, name="fox_dkv", grid=(nseq, NH // FOX_HP, nq), in_specs=[full, kblk, vblk, kblk, full, full, full],
        out_specs=[kblk, kblk, kblk],
        out_shape=[jax.ShapeDtypeStruct(wide, F32), jax.ShapeDtypeStruct(wide, BF16), jax.ShapeDtypeStruct(wide, F32)],
        compiler_params=_params(3))(qn, kn, proj, cb, do, lse, delta)


def _adamw_update(w, g, m, v):
    m_new = ADAM_B1 * m + (1.0 - ADAM_B1) * g
    v_new = ADAM_B2 * v + (1.0 - ADAM_B2) * (g * g)
    m_hat = m_new / (1.0 - ADAM_B1 ** ADAM_STEP)
    v_hat = v_new / (1.0 - ADAM_B2 ** ADAM_STEP)
    return -ADAM_LR * (m_hat / (jnp.sqrt(v_hat) + ADAM_EPS) + ADAM_WD * w), m_new, v_new


def adamw(name, w, g, m, v):
    rows, cols = w.shape
    tb = min(rows, 128)
    assert rows % tb == 0
    blk = pl.BlockSpec((tb, cols), lambda i: (i, 0))

    def body(w_ref, g_ref, m_ref, v_ref, d_ref, mo_ref, vo_ref):
        d_ref[...], mo_ref[...], vo_ref[...] = _adamw_update(w_ref[...], g_ref[...], m_ref[...], v_ref[...])

    shp = jax.ShapeDtypeStruct(w.shape, F32)
    return pl.pallas_call(body, name=name, grid=(rows // tb,), in_specs=[blk] * 4, out_specs=[blk] * 3,
                          out_shape=[shp] * 3, compiler_params=_params(1))(w, g, m, v)


SPLIT_TILE = 128


def _tiled(shape2d, ax, n_lead, index):
    blk = (SPLIT_TILE, shape2d[1]) if ax == 0 else (shape2d[0], SPLIT_TILE)

    def index_map(*args):
        *lead, t = index(*args)
        return (*lead, t, 0) if ax == 0 else (*lead, 0, t)

    return pl.BlockSpec((None,) * n_lead + blk, index_map)


def adamw_halves(name, w, mine, other, m, v, c, ax):
    steps = w.shape[ax] // 2 // SPLIT_TILE
    assert w.shape[ax] == 2 * steps * SPLIT_TILE

    def body(c_ref, w_ref, mine_ref, other_ref, m_ref, v_ref, g_ref, d_ref, mo_ref, vo_ref):
        g = jnp.where(pl.program_id(0) // steps == c_ref[0], mine_ref[...], other_ref[...])
        g_ref[...] = g
        d_ref[...], mo_ref[...], vo_ref[...] = _adamw_update(w_ref[...], g, m_ref[...], v_ref[...])

    blk = _tiled(w.shape, ax, 0, lambda i, c_ref: (i,))
    hblk = _tiled(mine.shape, ax, 0, lambda i, c_ref: (i % steps,))
    grid_spec = pltpu.PrefetchScalarGridSpec(num_scalar_prefetch=1, grid=(2 * steps,),
                                             in_specs=[blk, hblk, hblk, blk, blk], out_specs=[blk] * 4)
    shp = jax.ShapeDtypeStruct(w.shape, F32)
    return pl.pallas_call(body, name=name, grid_spec=grid_spec, out_shape=[shp] * 4,
                          compiler_params=_params(1))(c, w, mine, other, m, v)


def add_chips(name, slots, parts, chip, axes):
    outs = []
    for idx, (x, own, ax) in enumerate(zip(slots, parts, axes)):
        n, shape2d = x.shape[0], x.shape[1:]
        steps = shape2d[ax] // SPLIT_TILE
        assert shape2d[ax] == steps * SPLIT_TILE

        def body(me_ref, *refs, n=n):
            o_ref = refs[n + 1]
            acc = None
            for t in range(n):
                term = jnp.where(me_ref[0] == t, refs[n][...], refs[t][...]).astype(F32)
                acc = term if acc is None else acc + term
            o_ref[...] = acc

        def filled(t, n=n):
            return lambda i, me_ref: (jnp.where(me_ref[0] == t, (t + 1) % n, t), i)

        grid_spec = pltpu.PrefetchScalarGridSpec(
            num_scalar_prefetch=1, grid=(steps,),
            in_specs=[_tiled(shape2d, ax, 1, filled(t)) for t in range(n)]
            + [_tiled(shape2d, ax, 1, lambda i, me_ref: (me_ref[0], i))],
            out_specs=_tiled(shape2d, ax, 0, lambda i, me_ref: (i,)))
        outs.append(pl.pallas_call(
            body, name=f"{name}_{idx}", grid_spec=grid_spec, out_shape=jax.ShapeDtypeStruct(shape2d, F32),
            compiler_params=_params(1))(chip, *([x] * n), own))
    return outs


def add_pair(name, gs, rs, c, axes):
    outs = []
    for idx, (g, r, ax) in enumerate(zip(gs, rs, axes)):
        nb = r.shape[0]
        steps = r.shape[1 + ax] // SPLIT_TILE
        assert r.shape[1 + ax] == steps * SPLIT_TILE

        def body(c_ref, g_ref, r_ref, o_ref):
            o_ref[...] = (g_ref[...] + r_ref[...]).astype(BF16)

        grid_spec = pltpu.PrefetchScalarGridSpec(
            num_scalar_prefetch=1, grid=(nb, steps),
            in_specs=[_tiled(g.shape[1:], ax, 1, lambda b, i, c_ref: (b, c_ref[0] * steps + i)),
                      _tiled(r.shape[1:], ax, 1, lambda b, i, c_ref: (b, i))],
            out_specs=_tiled(r.shape[1:], ax, 1, lambda b, i, c_ref: (b, i)))
        outs.append(pl.pallas_call(
            body, name=f"{name}_{idx}", grid_spec=grid_spec, out_shape=jax.ShapeDtypeStruct(r.shape, BF16),
            compiler_params=_params(2))(c, g, r))
    return outs


def _place():
    x, y, c = lax.axis_index("x"), lax.axis_index("y"), lax.axis_index("c")
    return x, y, c, [(1 - x, y), (x, 1 - y), (1 - x, 1 - y)]


def _remote(src, dst, send_sem, recv_sem, dev):
    return pltpu.make_async_remote_copy(src_ref=src, dst_ref=dst, send_sem=send_sem, recv_sem=recv_sem,
                                        device_id=dev, device_id_type=MESH)


def _half(ref, lead, ax, which):
    size = ref.shape[len(lead) + ax] // 2
    part = pl.ds(which * size, size)
    return ref.at[(*lead, part, slice(None)) if ax == 0 else (*lead, slice(None), part)]


def gather_weights(shards, axes):
    n = len(shards)

    def body(*refs):
        ins, outs = refs[:n], refs[n:2 * n]
        ici_s, ici_r, d2d_s, d2d_r = refs[2 * n:]
        x, y, c, chips = _place()
        me = 2 * x + y
        sends, passes = [], []
        for w in range(n):
            cp = _remote(ins[w], outs[w].at[me], d2d_s.at[3 * n + w], d2d_r.at[3 * n + w], (x, y, 1 - c))
            cp.start()
            passes.append(cp)
        for w in range(n):
            for j, (ox, oy) in enumerate(chips):
                cp = _remote(_half(ins[w], (), axes[w], c), _half(outs[w], (me,), axes[w], c),
                             ici_s.at[3 * w + j], ici_r.at[3 * w + j], (ox, oy, c))
                cp.start()
                sends.append(cp)
        for w in range(n):
            for j, (ox, oy) in enumerate(chips):
                landed = _half(outs[w], (2 * ox + oy,), axes[w], c)
                _remote(landed, landed, ici_s.at[3 * w + j], ici_r.at[3 * w + j], (ox, oy, c)).wait_recv()
                cp = _remote(landed, landed, d2d_s.at[3 * w + j], d2d_r.at[3 * w + j], (x, y, 1 - c))
                cp.start()
                passes.append(cp)
        for w in range(n):
            for j, (ox, oy) in enumerate(chips):
                other = _half(outs[w], (2 * ox + oy,), axes[w], 1 - c)
                _remote(other, other, d2d_s.at[3 * w + j], d2d_r.at[3 * w + j], (x, y, 1 - c)).wait_recv()
            own = outs[w].at[me]
            _remote(own, own, d2d_s.at[3 * n + w], d2d_r.at[3 * n + w], (x, y, 1 - c)).wait_recv()
        for cp in sends + passes:
            cp.wait_send()

    return pl.pallas_call(
        body, name="gather_weights", in_specs=[ANY] * n, out_specs=[ANY] * n,
        out_shape=[jax.ShapeDtypeStruct((4,) + s.shape, s.dtype) for s in shards],
        scratch_shapes=[pltpu.SemaphoreType.DMA((3 * n,))] * 2 + [pltpu.SemaphoreType.DMA((4 * n,))] * 2,
    )(*shards)


HBM = pl.BlockSpec(memory_space=pltpu.HBM)
SEM = pl.BlockSpec(memory_space=pltpu.SEMAPHORE)
DATAFLOW = pltpu.SideEffectType.DATAFLOW_SIDE_EFFECTING


def _hbm(a):
    return pltpu.with_memory_space_constraint(a, pltpu.HBM)


class SplitExchange:
    def __init__(self, name, srcs, zone_shapes, n_sems, plan):
        self.name, self.n, self.n_sems, self.plan = name, len(srcs), n_sems, plan
        self.srcs = [_hbm(s) for s in srcs]
        self.zones = [_hbm(lax.empty(shape, s.dtype)) for shape, s in zip(zone_shapes, srcs)]

    def start(self, after):
        n, n_after = self.n, len(after)

        def body(*refs):
            ins, lands = refs[:n], refs[n:2 * n]
            send, recv, token = refs[2 * n + n_after], refs[2 * n + n_after + 1], refs[-1]
            for src, dst, si, ri, dev in self.plan(ins, lands)[0]:
                _remote(src, dst, send.at[si], recv.at[ri], dev).start()
            token[...] = jnp.zeros_like(token)

        res = pl.pallas_call(
            body, name=f"{self.name}_start", in_specs=[HBM] * (2 * n) + [ANY] * n_after,
            out_specs=[SEM, SEM] + [HBM] * (2 * n) + [pl.BlockSpec(memory_space=pltpu.VMEM)],
            out_shape=[pltpu.SemaphoreType.DMA((self.n_sems,)), pltpu.SemaphoreType.DMA((self.n_sems,))]
            + [pltpu.HBM(a.shape, a.dtype) for a in self.srcs + self.zones] + [jax.ShapeDtypeStruct((8, LANES), F32)],
            input_output_aliases={i: 2 + i for i in range(2 * n)},
            compiler_params=pltpu.CompilerParams(has_side_effects=DATAFLOW),
        )(*self.srcs, *self.zones, *after)
        self.sems, self.srcs, self.zones = res[:2], list(res[2:2 + n]), list(res[2 + n:2 + 2 * n])
        return res[-1]

    def wait(self, after):
        n = self.n

        def body(*refs):
            ins, lands = refs[:n], refs[n:2 * n]
            send, recv = refs[2 * n], refs[2 * n + 1]
            sends, arrivals = self.plan(ins, lands)
            for src, _, si, _, dev in sends:
                _remote(src, src, send.at[si], recv.at[si], dev).wait_send()
            for landed, ri in arrivals:
                _remote(landed, landed, send.at[ri], recv.at[ri], _place()[:3]).wait_recv()

        res = pl.pallas_call(
            body, name=f"{self.name}_wait", in_specs=[HBM] * (2 * n) + [SEM, SEM, ANY], out_specs=[HBM] * (2 * n),
            out_shape=[pltpu.HBM(a.shape, a.dtype) for a in self.srcs + self.zones],
            input_output_aliases={i: i for i in range(2 * n)},
            compiler_params=pltpu.CompilerParams(has_side_effects=DATAFLOW),
        )(*self.srcs, *self.zones, *self.sems, after)
        self.srcs = list(res[:n])
        return list(res[n:])


def split_gather(shards):
    n = len(shards)

    def plan(ins, lands):
        x, y, c, chips = _place()
        me = 2 * x + y
        sends, arrivals = [], []
        for w in range(n):
            for j, (ox, oy) in enumerate(chips):
                for k in range(2):
                    base = 2 * (3 * w + j)
                    sends.append((_half(ins[w], (), 0, c), _half(lands[w], (me,), 0, c), base + k, base + c, (ox, oy, k)))
                    arrivals.append((_half(lands[w], (2 * ox + oy,), 0, k), base + k))
            sends.append((ins[w], lands[w].at[me], 6 * n + w, 6 * n + w, (x, y, 1 - c)))
            arrivals.append((lands[w].at[me], 6 * n + w))
        return sends, arrivals

    return SplitExchange("gather", shards, [(4,) + s.shape for s in shards], 7 * n, plan)


def split_pair_swap(name, grads, axes):
    def plan(ins, lands):
        x, y, c, _ = _place()
        sends = [(_half(ins[w], (slice(None),), axes[w], 1 - c), lands[w], w, w, (x, y, 1 - c)) for w in range(len(ins))]
        return sends, [(lands[w], w) for w in range(len(ins))]

    halved = [tuple(d // 2 if i == 1 + ax else d for i, d in enumerate(g.shape)) for g, ax in zip(grads, axes)]
    return SplitExchange(name, grads, halved, len(grads), plan)


def split_chip_exchange(name, parts):
    def plan(ins, lands):
        x, y, c, chips = _place()
        sends, arrivals = [], []
        for w in range(len(ins)):
            for j, (ox, oy) in enumerate(chips):
                sends.append((ins[w].at[2 * ox + oy], lands[w].at[2 * x + y], 3 * w + j, 3 * w + j, (ox, oy, c)))
                arrivals.append((lands[w].at[2 * ox + oy], 3 * w + j))
        return sends, arrivals

    return SplitExchange(name, parts, [p.shape for p in parts], 3 * len(parts), plan)


def split_pair_send(halves):
    def plan(ins, lands):
        x, y, c, _ = _place()
        return ([(ins[w], lands[w], w, w, (x, y, 1 - c)) for w in range(len(ins))],
                [(lands[w], w) for w in range(len(ins))])

    return SplitExchange("pair_send", halves, [h.shape for h in halves], len(halves), plan)


def pair_send(halves):
    n = len(halves)

    def body(*refs):
        ins, outs = refs[:n], refs[n:2 * n]
        send, recv = refs[2 * n:]
        x, y, c, _ = _place()
        cps = [_remote(ins[w], outs[w], send.at[w], recv.at[w], (x, y, 1 - c)) for w in range(n)]
        for cp in cps:
            cp.start()
        for cp in cps:
            cp.wait_recv()
        for cp in cps:
            cp.wait_send()

    return pl.pallas_call(
        body, name="pair_send", in_specs=[ANY] * n, out_specs=[ANY] * n,
        out_shape=[jax.ShapeDtypeStruct(h.shape, h.dtype) for h in halves],
        scratch_shapes=[pltpu.SemaphoreType.DMA((n,))] * 2,
    )(*halves)


def all_reduce_small(name, vec, after=()):
    rows = vec.shape[0]

    def body(v_ref, *refs):
        o_ref, buf, send, recv = refs[len(after):]
        x, y, c, _ = _place()
        me = 4 * x + 2 * y + c
        buf[me] = v_ref[...]
        cps = []
        for k in range(1, 8):
            kx, ky, kc = (k >> 2) & 1, (k >> 1) & 1, k & 1
            peer = (x if kx == 0 else 1 - x, y if ky == 0 else 1 - y, c if kc == 0 else 1 - c)
            cp = _remote(v_ref, buf.at[me], send.at[k - 1], recv.at[k - 1], peer)
            cp.start()
            cps.append(cp)
        for k in range(1, 8):
            kx, ky, kc = (k >> 2) & 1, (k >> 1) & 1, k & 1
            px, py, pc = (x if kx == 0 else 1 - x, y if ky == 0 else 1 - y, c if kc == 0 else 1 - c)
            slot = buf.at[4 * px + 2 * py + pc]
            _remote(slot, slot, send.at[k - 1], recv.at[k - 1], (px, py, pc)).wait_recv()
        for cp in cps:
            cp.wait_send()
        acc = buf[0]
        for d in range(1, 8):
            acc = acc + buf[d]
        o_ref[...] = acc

    vm = pl.BlockSpec(memory_space=pltpu.VMEM)
    return pl.pallas_call(
        body, name=name, in_specs=[vm] + [ANY] * len(after), out_specs=vm, out_shape=jax.ShapeDtypeStruct(vec.shape, F32),
        scratch_shapes=[pltpu.VMEM((8, rows, LANES), F32), pltpu.SemaphoreType.DMA((7,)), pltpu.SemaphoreType.DMA((7,))],
    )(vec, *after)


class NoExchange:
    def __init__(self, late):
        self.late = late

    def late_weights(self, after):
        return self.late

    def reduce_start(self, grads):
        return jnp.zeros((8, LANES), F32)

    def reduce_exchange(self, after):
        return jnp.zeros((8, LANES), F32)

    def reduce_finish(self, after):
        return jnp.zeros((8, LANES), F32)

    def input_grad_start(self, dw_main, dw_small):
        return jnp.zeros((8, LANES), F32)

    def input_grad_exchange(self, after):
        return jnp.zeros((8, LANES), F32)


def local_step(x2, tgt2, g1, g2, gdn_ng, qn_g, kn_g, p1, p2, conv_w, wt_main, wt_small, hooks, nseq, seq):
    rows, dm = x2.shape
    wide = NH * LANES
    row = lambda a, off=0, w=None: (a, "row", off, a.shape[1] if w is None else w)
    rowh = lambda a, off=0, w=LANES: (a, "rowh", off, w)
    par = lambda a: (a, "par", 0, a.shape[1])
    parh = lambda a, off=0: (a, "parh", off, LANES)
    o_row = lambda w, dt: (w, "row", w, dt)
    o_rowh = lambda dt, tw=wide, w=LANES: (tw, "rowh", w, dt)

    u, = ew_fwd("rms1", f_rms, [row(x2), par(g1)], [o_row(dm, BF16)], rows)
    proj = matmul("mm_in", u, wt_main, "nt", BF16)
    sp = matmul("mm_in_small", u, wt_small, "nt", F32)
    so, = ew_fwd("small", f_small, [row(sp), par(p1), par(p2)], [o_row(LANES, F32)], rows)
    cs = cumsum_time("cumsum", so, nseq, seq, False)
    gb, bb, cb = ew_fwd("bcast", f_bcast, [row(so), row(cs)], [o_rowh(F32)] * 3, rows, NH)
    ct = transpose_time("c_time_major", cs, nseq, seq)
    conv = {}
    for mode, off in (("q", 0), ("k", NH), ("v", 2 * NH)):
        conv[mode], = ew_fwd(f"conv_{mode}", make_f_conv(mode), [rowh(proj, off), parh(conv_w, off)], [o_rowh(F32)],
                             rows, NH, seq, "hi", CONV_HEADS)
    val, kcum, attn, qdec, kdec, t_inv = gdn_a_fwd(conv["q"], conv["k"], conv["v"], gb, bb, rows)
    o_a, snaps = gdn_b_fwd(val, kcum, attn, qdec, kdec, gb, nseq, seq)
    ya_in, = ew_fwd("gdn_post", f_post, [rowh(o_a), rowh(proj, 3 * NH), par(gdn_ng)], [o_rowh(BF16)], rows, NH)
    fqn, = ew_fwd("fox_qn", f_rms, [rowh(proj, FOX_Q), par(qn_g)], [o_rowh(BF16)], rows, NH)
    fkn, = ew_fwd("fox_kn", f_rms, [rowh(proj, FOX_K), par(kn_g)], [o_rowh(BF16)], rows, NH)
    o_b, o_b16, lse = fox_fwd(fqn, fkn, proj, ct, nseq, seq)
    p_a, p_b, w_o, w_u, w_d = hooks.late_weights(o_a)
    y_a = matmul("mm_pa", ya_in, p_a, "nn", F32, tn=1024)
    y_b = matmul("mm_pb", o_b16, p_b, "nn", F32, tn=1024)
    gates = [row(proj, 7, dm), row(proj, 8, dm)]
    merged, = ew_fwd("merge", f_merge, gates + [row(y_a), row(y_b)], [o_row(dm, BF16)], rows)
    hres = matmul("mm_out", merged, w_o, "nn", F32, add=x2, tn=1024)
    hn, = ew_fwd("rms2", f_rms, [row(hres), par(g2)], [o_row(dm, BF16)], rows)
    up_blocks = w_u.shape[0]
    act, relu2 = matmul("mm_up", hn, w_u, "nn", F32, col_blocks=up_blocks, out_dtypes=[F32, BF16],
                        epilogue=lambda r: [r, jnp.maximum(r, 0.0) * jnp.maximum(r, 0.0)])
    def loss_tail(r, h_tile, t_tile):
        d = (r + h_tile) - t_tile
        e = (0.5 / dm) * (d * d)
        part = e.reshape(e.shape[0] // 8, 8, e.shape[1]).sum(axis=0)
        part = sum(part[:, t * LANES:(t + 1) * LANES] for t in range(e.shape[1] // LANES))
        g = d * (1.0 / dm)
        return [g, g, part]

    dout, dout16, loss_acc = matmul("mm_down", relu2, w_d, "nn", F32, extras=[hres, tgt2], epilogue=loss_tail,
                                    out_dtypes=[F32, BF16, F32], tile_sums=True, tm=512)

    d_act = matmul("mm_d_act", dout16, w_d, "nt", BF16, extras=[act], epilogue=lambda r, a: [2.0 * jnp.maximum(a, 0.0) * r])
    dw_d = matmul("mm_dw_down", relu2, dout16, "tn", F32, tn=1024)
    dw_u = matmul("mm_dw_up", hn, d_act, "tn", F32, col_blocks=up_blocks)
    d_hn = matmul("mm_d_hn", d_act, w_u, "nt", F32, col_blocks=up_blocks)
    dh, dh16, dg2 = ew_bwd("rms2_b", f_rms, [row(hres), par(g2)], [(row(d_hn),)], [row(dout)],
                           lambda g, e: [g[0] + e[0], g[0] + e[0], g[1]],
                           [((rows, dm), "row", dm, F32, None), ((rows, dm), "row", dm, BF16, None), ((1, dm), "par", dm, F32, "all")], rows)
    d_merged = matmul("mm_d_merged", dh16, w_o, "nt", F32, tn=1024)
    dw_o = matmul("mm_dw_out", merged, dh16, "tn", F32, tn=1024)
    seg16 = ((rows, dm), "row", dm, BF16, None)
    d_ga16, d_gb16, d_ya16, d_yb16 = ew_bwd("merge_b", f_merge, gates + [row(y_a), row(y_b)], [(row(d_merged),)], [],
                                            lambda g, e: list(g), [seg16] * 4, rows)
    dp_a = matmul("mm_dp_a", ya_in, d_ya16, "tn", F32, tn=1024)
    d_ya_in = matmul("mm_d_ya_in", d_ya16, p_a, "nt", F32, tn=1024)
    dp_b = matmul("mm_dp_b", o_b16, d_yb16, "tn", F32, tn=1024)
    d_ob = matmul("mm_d_ob", d_yb16, p_b, "nt", F32, tn=1024)
    token = hooks.reduce_start(dict(p_a=dp_a, p_b=dp_b, w_o=dw_o, w_u=dw_u, w_d=dw_d))
    gdn_ng_t = gdn_ng + token[0, 0]
    h32 = ((rows, wide), "rowh", LANES, F32, None)
    h16 = ((rows, wide), "rowh", LANES, BF16, None)
    gain = ((1, LANES), "par", LANES, F32, "all")
    d_oa, d_z16, d_gdn_ng = ew_bwd("gdn_post_b", f_post, [rowh(o_a), rowh(proj, 3 * NH), par(gdn_ng_t)], [(rowh(d_ya_in),)], [],
                                   lambda g, e: list(g), [h32, h16, gain], rows, NH)
    dval, dkc, dat, dqd, dkd, dgb_b = gdn_b_bwd(val, kcum, attn, qdec, kdec, gb, snaps, d_oa, nseq, seq)
    d_cq, d_ck, d_cv, d_gb, d_bb = gdn_a_bwd(conv["q"], conv["k"], conv["v"], gb, bb, t_inv, dval, dkc, dat, dqd, dkd, dgb_b, rows)
    token = hooks.reduce_exchange(d_cq)
    conv_w_t = conv_w + token[0, 0]
    d_pre, d_conv = {}, {}
    tap = ((4, wide), "parh", LANES, F32, "inner")
    for mode, off, ctg in (("q", 0, d_cq), ("k", NH, d_ck), ("v", 2 * NH, d_cv)):
        d_pre[mode], d_conv[mode] = ew_bwd(f"conv_{mode}_b", make_f_conv(mode), [rowh(proj, off), parh(conv_w_t, off)],
                                           [(rowh(ctg),)], [], lambda g, e: list(g), [h16, tap], rows, NH, seq, "hi", CONV_HEADS)
    delta, = ew_fwd("fox_delta", f_delta, [rowh(d_ob), rowh(o_b)], [o_rowh(F32)], rows, NH, after=[token])
    d_fqn, d_cq_b = fox_dq(fqn, fkn, proj, ct, d_ob, lse, delta, nseq, seq)
    d_fkn, d_fv16, d_ck_b = fox_dkv(fqn, fkn, proj, cb, d_ob, lse, delta, nseq, seq)
    token = hooks.reduce_finish(d_fkn)
    qn_g_t, kn_g_t = qn_g + token[0, 0], kn_g + token[0, 0]
    d_fq16, d_qn_g = ew_bwd("fox_qn_b", f_rms, [rowh(proj, FOX_Q), par(qn_g_t)], [(rowh(d_fqn),)], [], lambda g, e: list(g),
                            [h16, gain], rows, NH)
    d_fk16, d_kn_g = ew_bwd("fox_kn_b", f_rms, [rowh(proj, FOX_K), par(kn_g_t)], [(rowh(d_fkn),)], [], lambda g, e: list(g),
                            [h16, gain], rows, NH)
    narrow = ((rows, LANES), "row", LANES, F32, None)
    d_so, d_cs = ew_bwd("bcast_b", f_bcast, [row(so), row(cs)], [(rowh(d_gb),), (rowh(d_bb),), (rowh(d_cq_b), rowh(d_ck_b))], [],
                        lambda g, e: list(g), [narrow, narrow], rows, NH)
    d_logf = cumsum_time("cumsum_b", d_cs, nseq, seq, True)
    vec = ((1, LANES), "par", LANES, F32, "all")
    d_sp16, d_p1, d_p2 = ew_bwd("small_b", f_small, [row(sp), par(p1), par(p2)], [(row(d_so), row(d_logf))], [],
                                lambda g, e: list(g), [((rows, LANES), "row", LANES, BF16, None), vec, vec], rows)
    d_proj16 = jnp.concatenate([d_pre["q"], d_pre["k"], d_pre["v"], d_z16, d_fq16, d_fk16, d_fv16, d_ga16, d_gb16], axis=1)
    dw_main = matmul("mm_dw_main", d_proj16, u, "tn", F32)
    dw_small = matmul("mm_dw_small", d_sp16, u, "tn", F32)
    wt_small_t = wt_small + hooks.input_grad_start(dw_main, dw_small)[0, 0].astype(BF16)
    d_u = matmul("mm_d_u_small", d_sp16, wt_small_t, "nn", F32)
    d_u = matmul("mm_d_u_first", d_proj16, wt_main, "nn", F32, add=d_u, k_part=(0, 2))
    d_u = matmul("mm_d_u_second", d_proj16, wt_main, "nn", F32, add=d_u, k_part=(1, 2), after=[hooks.input_grad_exchange(d_u)])
    dx, dg1 = ew_bwd("rms1_b", f_rms, [row(x2), par(g1)], [(row(d_u),)], [row(dh)], lambda g, e: [g[0] + e[0], g[1]],
                     [((rows, dm), "row", dm, F32, None), ((1, dm), "par", dm, F32, "all")], rows)
    d_conv_w = jnp.concatenate([d_conv["q"], d_conv["k"], d_conv["v"]], axis=1)
    return dict(loss_acc=loss_acc, dx=dx, g1=dg1, g2=dg2, gdn_ng=d_gdn_ng, qn=d_qn_g, kn=d_kn_g, p1=d_p1, p2=d_p2,
                conv=d_conv_w, w_main=dw_main, w_small=dw_small, p_a=dp_a, p_b=dp_b, w_o=dw_o, w_u=dw_u, w_d=dw_d)


_W = NH * LANES
_A0, _A1 = 4 * _W, 4 * _W + 2 * NH
_B0, _B1 = _A1 + 3 * _W, _A1 + 3 * _W + NH
N_IN = _B1 + 2 * _W


def _split_w_in(full_t):
    main = jnp.concatenate([full_t[:_A0], full_t[_A1:_B0], full_t[_B1:]], axis=0)
    small = jnp.concatenate([full_t[_A0:_A1], full_t[_B0:_B1], jnp.zeros((LANES - 3 * NH, full_t.shape[1]), full_t.dtype)], axis=0)
    return main, small


def _join_w_in(main, small):
    return jnp.concatenate([main[:_A0], small[:2 * NH], main[_A0:_A0 + 3 * _W], small[2 * NH:3 * NH], main[_A0 + 3 * _W:]], axis=0)


def _lanes(v, at=0):
    return jnp.pad(v.reshape(1, -1), ((0, 0), (at, LANES - at - v.size)))


def kernel(x, norm_mix_g, w_in, gdn_conv_w, gdn_a_log, gdn_dt_bias, gdn_norm_g, fox_q_norm_g, fox_k_norm_g, fox_f_bias, w_proj_gdn, w_proj_fox, w_out, norm_mlp_g, w_up, w_down, loss_target, m_norm_mix_g, m_w_in, m_gdn_conv_w, m_gdn_a_log, m_gdn_dt_bias, m_gdn_norm_g, m_fox_q_norm_g, m_fox_k_norm_g, m_fox_f_bias, m_w_proj_gdn, m_w_proj_fox, m_w_out, m_norm_mlp_g, m_w_up, m_w_down, v_norm_mix_g, v_w_in, v_gdn_conv_w, v_gdn_a_log, v_gdn_dt_bias, v_gdn_norm_g, v_fox_q_norm_g, v_fox_k_norm_g, v_fox_f_bias, v_w_proj_gdn, v_w_proj_fox, v_w_out, v_norm_mlp_g, v_w_up, v_w_down):
    nseq, seq, dm = x.shape
    rows = nseq * seq
    xi, yi, ci = lax.axis_index("x"), lax.axis_index("y"), lax.axis_index("c")
    chip = 2 * xi + yi
    conv_cols = gdn_conv_w.shape[2]

    tr = lambda a: jnp.swapaxes(a[0], 0, 1)
    big = [tr(w_in), w_proj_gdn[0], w_proj_fox[0], w_out[0], w_up[0], w_down[0]]
    axes = [1, 0, 0, 0, 0, 0]
    big16 = [w.astype(BF16) for w in big]
    conv_slot = jnp.zeros((4, 4, conv_cols), F32).at[:, chip].set(jnp.where(ci == 0, gdn_conv_w[0], 0.0))
    conv_full = all_reduce_small("gather_conv", conv_slot.reshape(-1, LANES)).reshape(4, 4 * conv_cols)
    got_in, = gather_weights(big16[:1], axes[:1])
    wt_main, wt_small = _split_w_in(got_in.reshape(-1, dm))
    core, chip_no = ci.reshape(1).astype(jnp.int32), chip.reshape(1).astype(jnp.int32)
    gather = split_gather(big16[1:])
    token = gather.start([got_in, conv_full])

    class Hooks:
        def late_weights(self, after):
            g_pa, g_pb, g_wo, w_u, g_wd = gather.wait(after)
            return (*(g.reshape(-1, dm) for g in (g_pa, g_pb, g_wo)), w_u, g_wd.reshape(-1, dm))

        def reduce_start(self, grads):
            blocks = [grads["p_a"].reshape(4, -1, dm), grads["p_b"].reshape(4, -1, dm), grads["w_o"].reshape(4, -1, dm),
                      grads["w_u"], grads["w_d"].reshape(4, -1, dm)]
            self.swap = split_pair_swap("pair_swap_late", blocks, axes[1:])
            return self.swap.start([])

        def reduce_exchange(self, after):
            swapped = self.swap.wait(after)
            self.exchange = split_chip_exchange("chip_exchange_late", add_pair("add_pair_late", self.swap.srcs, swapped, core, axes[1:]))
            return self.exchange.start([])

        def reduce_finish(self, after):
            slots = self.exchange.wait(after)
            self.send = split_pair_send(add_chips("add_chips_late", slots, self.exchange.srcs, chip_no, axes[1:]))
            return self.send.start([])

        def input_grad_start(self, dw_main, dw_small):
            self.in_swap = split_pair_swap("pair_swap_in", [_join_w_in(dw_main, dw_small).reshape(4, -1, dm)], axes[:1])
            return self.in_swap.start([])

        def input_grad_exchange(self, after):
            swapped = self.in_swap.wait(after)
            self.in_exchange = split_chip_exchange("chip_exchange_in", add_pair("add_pair_in", self.in_swap.srcs, swapped, core, axes[:1]))
            return self.in_exchange.start([])

    hooks = Hooks()
    p1 = _lanes(gdn_dt_bias[0]) + _lanes(fox_f_bias[0], 2 * NH)
    p2 = _lanes(gdn_a_log[0])

    g = local_step(x.reshape(rows, dm), loss_target.reshape(rows, dm), norm_mix_g + token[0, 0], norm_mlp_g, gdn_norm_g,
                   fox_q_norm_g, fox_k_norm_g, p1, p2, conv_full, wt_main, wt_small, hooks, nseq, seq)

    others = hooks.send.wait(g["dx"])
    big_m = [tr(m_w_in), m_w_proj_gdn[0], m_w_proj_fox[0], m_w_out[0], m_w_up[0], m_w_down[0]]
    big_v = [tr(v_w_in), v_w_proj_gdn[0], v_w_proj_fox[0], v_w_out[0], v_w_up[0], v_w_down[0]]
    names = ["w_in", "w_proj_gdn", "w_proj_fox", "w_out", "w_up", "w_down"]
    big_res, big_grad = {}, {}
    for i in range(1, len(names)):
        big_grad[names[i]], *big_res[names[i]] = adamw_halves(f"adamw_{names[i]}", big[i], hooks.send.srcs[i - 1], others[i - 1],
                                                              big_m[i], big_v[i], core, axes[i])
    slots = hooks.in_exchange.wait(big_res[names[-1]][0])
    mine = add_chips("add_chips_in", slots, hooks.in_exchange.srcs, chip_no, axes[:1])
    res = adamw_halves("adamw_w_in", big[0], mine[0], pair_send(mine)[0], big_m[0], big_v[0], core, axes[0])
    big_grad["w_in"], *big_res["w_in"] = [jnp.swapaxes(r, 0, 1) for r in res]

    small_parts = [g["loss_acc"], g["g1"].reshape(8, LANES), g["g2"].reshape(8, LANES), g["gdn_ng"], g["qn"], g["kn"], g["p1"], g["p2"],
                   g["conv"].reshape(-1, LANES)]
    tiled = [jnp.pad(p, ((0, -p.shape[0] % 8), (0, 0))) for p in small_parts]
    red = all_reduce_small("reduce_small", jnp.concatenate(tiled, axis=0), slots)
    pos, red_parts = 0, []
    for p, t in zip(small_parts, tiled):
        red_parts.append(red[pos:pos + p.shape[0]])
        pos += t.shape[0]
    r_loss, r_g1, r_g2, r_gdn_ng, r_qn, r_kn, r_p1, r_p2, r_conv = red_parts
    loss = jnp.sum(r_loss)
    g_conv = lax.dynamic_slice_in_dim(r_conv.reshape(4, 4, conv_cols), chip, 1, axis=1).reshape(4, conv_cols)
    small_grads = [r_g1.reshape(1, dm), r_p2[:, :NH], r_p1[:, :NH], r_gdn_ng, r_qn, r_kn, r_p1[:, 2 * NH:3 * NH], r_g2.reshape(1, dm)]
    small_w = [norm_mix_g, gdn_a_log, gdn_dt_bias, gdn_norm_g, fox_q_norm_g, fox_k_norm_g, fox_f_bias, norm_mlp_g]
    small_m = [m_norm_mix_g, m_gdn_a_log, m_gdn_dt_bias, m_gdn_norm_g, m_fox_q_norm_g, m_fox_k_norm_g, m_fox_f_bias, m_norm_mlp_g]
    small_v = [v_norm_mix_g, v_gdn_a_log, v_gdn_dt_bias, v_gdn_norm_g, v_fox_q_norm_g, v_fox_k_norm_g, v_fox_f_bias, v_norm_mlp_g]

    def pack(parts):
        flat = jnp.concatenate([jnp.pad(p.reshape(-1), (0, -p.size % LANES)) for p in parts])
        return jnp.pad(flat, (0, -flat.size % (8 * LANES))).reshape(-1, LANES)

    packed = adamw("adamw_small", pack(small_w + [gdn_conv_w[0]]), pack(small_grads + [g_conv]),
                   pack(small_m + [m_gdn_conv_w[0]]), pack(small_v + [v_gdn_conv_w[0]]))

    def unpack(flat2d):
        flat, pos, res = flat2d.reshape(-1), 0, []
        for p in small_w + [gdn_conv_w[0]]:
            res.append(flat[pos:pos + p.size].reshape(p.shape))
            pos += p.size + (-p.size % LANES)
        return res

    s_delta, s_m, s_v = (unpack(a) for a in packed)

    order = ["norm_mix_g", "w_in", "gdn_conv_w", "gdn_a_log", "gdn_dt_bias", "gdn_norm_g", "fox_q_norm_g", "fox_k_norm_g",
             "fox_f_bias", "w_proj_gdn", "w_proj_fox", "w_out", "norm_mlp_g", "w_up", "w_down"]
    small_names = ["norm_mix_g", "gdn_a_log", "gdn_dt_bias", "gdn_norm_g", "fox_q_norm_g", "fox_k_norm_g", "fox_f_bias", "norm_mlp_g",
                   "gdn_conv_w"]
    small_idx = {nm: i for i, nm in enumerate(small_names)}
    shapes = dict(zip(order, (a.shape for a in (norm_mix_g, w_in, gdn_conv_w, gdn_a_log, gdn_dt_bias, gdn_norm_g, fox_q_norm_g,
                                                 fox_k_norm_g, fox_f_bias, w_proj_gdn, w_proj_fox, w_out, norm_mlp_g, w_up, w_down))))
    grads_out, delta_out, m_out, v_out = [], [], [], []
    for nm in order:
        if nm in big_res:
            d, mm, vv = big_res[nm]
            gr = big_grad[nm]
        else:
            i = small_idx[nm]
            gr = (small_grads + [g_conv])[i]
            d, mm, vv = s_delta[i], s_m[i], s_v[i]
        for lst, val in ((grads_out, gr), (delta_out, d), (m_out, mm), (v_out, vv)):
            lst.append(val.reshape(shapes[nm]))
    return (loss, g["dx"].reshape(x.shape), *grads_out, *delta_out, *m_out, *v_out)
```

```python
import functools

import jax
import jax.numpy as jnp
from jax import lax
from jax.experimental import pallas as pl
from jax.experimental.pallas import tpu as pltpu

F32 = jnp.float32
BF16 = jnp.bfloat16
LANES = 128
NH = 8
EPS = 1e-6
GDN_CHUNK = 64
GDN_ROWS = 256
GDN_BASE = 16
ROW_TILE = 512
CONV_HEADS = 2
ATT_TILE = 512
NEG = -1e30
VMEM_LIMIT_BYTES = 58 * 1024 * 1024
HI = lax.Precision.HIGHEST
LO = lax.Precision.DEFAULT
MESH = pl.DeviceIdType.MESH
ANY = pl.BlockSpec(memory_space=pl.ANY)

ADAM_LR, ADAM_B1, ADAM_B2, ADAM_EPS, ADAM_WD, ADAM_STEP = 0.001, 0.9, 0.999, 1e-08, 0.01, 10


def _params(n_grid):
    return pltpu.CompilerParams(dimension_semantics=("arbitrary",) * n_grid,
                                vmem_limit_bytes=VMEM_LIMIT_BYTES)


def _dot(a, b, dims, precision=None):
    dn = {"nn": (((1,), (0,)), ((), ())), "nt": (((1,), (1,)), ((), ())), "tn": (((0,), (0,)), ((), ()))}[dims]
    return lax.dot_general(a, b, dn, precision=precision, preferred_element_type=F32)


def _iota(shape, dim):
    return lax.broadcasted_iota(jnp.int32, shape, dim)


def _split(x, parts):
    out = []
    for _ in range(parts - 1):
        hi = x.astype(BF16)
        out.append(hi)
        x = x - hi.astype(F32)
    return out + [x.astype(BF16)]


def _dot_mask(mask, b, dims):
    m16 = mask.astype(BF16)
    b1, b2, b3 = _split(b, 3)
    return _dot(m16, b1, dims) + (_dot(m16, b2, dims) + _dot(m16, b3, dims))


@jax.custom_vjp
def mm_mask(mask, b):
    return _dot_mask(mask, b, "nn")


mm_mask.defvjp(lambda mask, b: (_dot_mask(mask, b, "nn"), mask),
               lambda mask, g: (jnp.zeros_like(mask), _dot_mask(mask, g, "tn")))


def matmul(name, a, b, dims, out_dtype, add=None, tm=1024, tn=1024, tk=512, col_blocks=None,
           extras=(), epilogue=None, out_dtypes=None, k_part=None, after=(), tile_sums=False):
    if col_blocks and dims != "tn":
        nb, b_rows, bw = b.shape
        b_shape = (b_rows, nb * bw)
    else:
        b_shape = b.shape
    if dims == "nn":
        (m, k), (_, n) = a.shape, b_shape
    elif dims == "nt":
        (m, k), (n, _) = a.shape, b_shape
    else:
        (k, m), (_, n) = a.shape, b_shape
    k_span = k // (k_part[1] if k_part else 1)
    if col_blocks and dims == "nt":
        k_span = min(k_span, bw)
    tk = k if k <= 1024 else max(t for t in (2048, 1536, 1024, 512, tk) if k_span % t == 0)
    tm, tn, tk = min(tm, m), min(tn, n), min(tk, k)
    assert m % tm == 0 and n % tn == 0 and k % tk == 0, (name, m, n, k)
    k0, nk = (0, k // tk) if k_part is None else (k_part[0] * (k // tk // k_part[1]), k // tk // k_part[1])
    assert k_part is None or (dims == "nn" and not col_blocks and (k // tk) % k_part[1] == 0)
    a_spec = pl.BlockSpec((tk, tm), lambda i, j, kk: (kk, i)) if dims == "tn" else pl.BlockSpec((tm, tk), lambda i, j, kk: (i, kk + k0))
    b_spec = pl.BlockSpec((tn, tk), lambda i, j, kk: (j, kk)) if dims == "nt" else pl.BlockSpec((tk, tn), lambda i, j, kk: (kk + k0, j))
    o_spec = pl.BlockSpec((tm, tn), lambda i, j, kk: (i, j))
    out_shape = (m, n)
    if col_blocks and dims == "nn":
        per = bw // tn
        assert bw % tn == 0
        b_spec = pl.BlockSpec((None, tk, tn), lambda i, j, kk: (j // per, kk, j % per))
    elif col_blocks and dims == "nt":
        per = bw // tk
        assert bw % tk == 0
        b_spec = pl.BlockSpec((None, tn, tk), lambda i, j, kk: (kk // per, j, kk % per))
    elif col_blocks:
        bw = n // col_blocks
        per = bw // tn
        assert bw % tn == 0 and add is None
        o_spec = pl.BlockSpec((None, tm, tn), lambda i, j, kk: (j // per, i, j % per))
        out_shape = (col_blocks, m, bw)
    extras = list(extras) + ([add] if add is not None else [])
    if add is not None:
        assert epilogue is None
        epilogue = lambda r, *e: [r + e[-1]]
    out_dtypes = [out_dtype] if epilogue is None or out_dtypes is None else list(out_dtypes)
    n_ex, n_out = len(extras), len(out_dtypes)

    def body(*refs):
        a_ref, b_ref = refs[0], refs[1]
        ex_refs, o_refs = refs[2:2 + n_ex], refs[2 + n_ex + len(after):2 + n_ex + len(after) + n_out]

        def finish(r):
            res = [r] if epilogue is None else epilogue(r, *[e[...] for e in ex_refs])
            for o_ref, v in zip(o_refs, res):
                o_ref[...] = v.astype(o_ref.dtype)

        if nk == 1:
            finish(_dot(a_ref[...], b_ref[...], dims))
            return
        acc_ref = refs[-1]
        kk = pl.program_id(2)

        @pl.when(kk == 0)
        def _():
            acc_ref[...] = jnp.zeros_like(acc_ref)

        acc_ref[...] += _dot(a_ref[...], b_ref[...], dims)

        @pl.when(kk == nk - 1)
        def _():
            finish(acc_ref[...])

    out_specs = [o_spec] * n_out
    out_shapes = [jax.ShapeDtypeStruct(out_shape, dt) for dt in out_dtypes]
    if tile_sums:
        out_specs[-1] = pl.BlockSpec((8, LANES), lambda i, j, kk: (i, j))
        out_shapes[-1] = jax.ShapeDtypeStruct((8 * (m // tm), LANES * (n // tn)), out_dtypes[-1])
    res = pl.pallas_call(
        body, name=name, grid=(m // tm, n // tn, nk), in_specs=[a_spec, b_spec] + [o_spec] * n_ex + [ANY] * len(after),
        out_specs=out_specs, out_shape=out_shapes,
        scratch_shapes=[pltpu.VMEM((tm, tn), F32)] if nk > 1 else [], compiler_params=_params(3),
    )(a, b, *extras, *after)
    return res[0] if n_out == 1 else res


def _ew_spec(kind, off, width, tb, hp, order, shape=None):
    def ih(g0, g1):
        return (g0, g1) if order == "ih" else (g1, g0)

    assert off % hp == 0 or kind in ("row", "par")
    if kind == "row":
        return pl.BlockSpec((tb, width), lambda g0, g1: (ih(g0, g1)[0], off))
    if kind == "rowh":
        return pl.BlockSpec((tb, hp * width), lambda g0, g1: (ih(g0, g1)[0], ih(g0, g1)[1] + off // hp))
    if kind == "par":
        return pl.BlockSpec(shape, lambda g0, g1: (0, 0))
    if kind == "parh":
        return pl.BlockSpec((shape[0], hp * width), lambda g0, g1: (0, ih(g0, g1)[1] + off // hp))
    raise ValueError(kind)


def _ew_grid(rows, tb, nh, hp, order):
    assert nh % hp == 0 and rows % tb == 0
    return (rows // tb, nh // hp) if order == "ih" else (nh // hp, rows // tb)


def _ew_load(ref, kind, width, hh):
    if kind in ("row", "par"):
        return ref[...].astype(F32)
    return ref[:, hh * width:(hh + 1) * width].astype(F32)


def ew_fwd(name, f, ins, outs, rows, nh=1, tb=ROW_TILE, order="ih", hp=None, after=()):
    hp = nh if hp is None else hp
    n_in = len(ins)

    def body(*refs):
        hb = pl.program_id(1) if order == "ih" else pl.program_id(0)
        for hh in range(hp):
            h = hh if hp == nh else hb * hp + hh
            vals = [_ew_load(r, kd, w, hh) for r, (_, kd, _, w) in zip(refs[:n_in], ins)]
            res = f(h, *vals)
            for r, v, (_, kd, w, _) in zip(refs[n_in + len(after):], res, outs):
                if kd == "row":
                    assert hp == 1
                    r[...] = v.astype(r.dtype)
                else:
                    r[:, hh * w:(hh + 1) * w] = v.astype(r.dtype)

    in_specs = [_ew_spec(kd, off, w, tb, hp, order, a.shape) for (a, kd, off, w) in ins]
    out_specs = [_ew_spec(kd, 0, w, tb, hp, order) for (_, kd, w, _) in outs]
    out_shape = [jax.ShapeDtypeStruct((rows, tw), dt) for (tw, _, _, dt) in outs]
    return pl.pallas_call(
        body, name=name, grid=_ew_grid(rows, tb, nh, hp, order), in_specs=in_specs + [ANY] * len(after), out_specs=out_specs,
        out_shape=out_shape, compiler_params=_params(2),
    )(*[a for (a, _, _, _) in ins], *after)


def ew_bwd(name, f, ins, cts, extras, emit, outs, rows, nh=1, tb=ROW_TILE, order="ih", hp=None):
    hp = nh if hp is None else hp
    n_in = len(ins)
    flat_cts = [d for group in cts for d in group]
    n_ct, n_ex = len(flat_cts), len(extras)

    def body(*refs):
        g0, g1 = pl.program_id(0), pl.program_id(1)
        hb = g1 if order == "ih" else g0
        out_refs = refs[n_in + n_ct + n_ex:]
        shared = [None] * len(outs)

        def store(r, v, first, sl=None):
            def put(val, add):
                if sl is None:
                    r[...] = (r[...] + val if add else val).astype(r.dtype)
                else:
                    r[:, sl] = (r[:, sl] + val if add else val).astype(r.dtype)

            if first is None:
                put(v, False)
            else:
                pl.when(first)(lambda: put(v, False))
                pl.when(jnp.logical_not(first))(lambda: put(v, True))

        for hh in range(hp):
            h = hh if hp == nh else hb * hp + hh
            vals = [_ew_load(r, kd, w, hh) for r, (_, kd, _, w) in zip(refs[:n_in], ins)]
            ct_refs = list(zip(refs[n_in:n_in + n_ct], flat_cts))
            ct_vals, pos = [], 0
            for group in cts:
                v = None
                for r, (_, kd, _, w) in ct_refs[pos:pos + len(group)]:
                    t = _ew_load(r, kd, w, hh)
                    v = t if v is None else v + t
                pos += len(group)
                ct_vals.append(v)
            ex_vals = [_ew_load(r, kd, w, hh) for r, (_, kd, _, w) in zip(refs[n_in + n_ct:n_in + n_ct + n_ex], extras)]
            _, vjp = jax.vjp(lambda *a: f(h, *a), *vals)
            res = emit(vjp(tuple(ct_vals)), ex_vals)
            for idx, (r, v, (_, kd, w, _, acc)) in enumerate(zip(out_refs, res, outs)):
                if kd in ("row", "par"):
                    shared[idx] = v if shared[idx] is None else shared[idx] + v
                else:
                    store(r, v, (g1 == 0) if acc == "inner" else None, slice(hh * w, (hh + 1) * w))
        for idx, (r, (_, kd, _, _, acc)) in enumerate(zip(out_refs, outs)):
            if kd in ("row", "par"):
                assert acc == "all" or hp == nh
                store(r, shared[idx], jnp.logical_and(g0 == 0, g1 == 0) if acc == "all" else None)

    operands = list(ins) + flat_cts + list(extras)
    in_specs = [_ew_spec(kd, off, w, tb, hp, order, a.shape) for (a, kd, off, w) in operands]
    out_specs = [_ew_spec(kd, 0, w, tb, hp, order, shp) for (shp, kd, w, _, _) in outs]
    out_shape = [jax.ShapeDtypeStruct(shp, dt) for (shp, _, _, dt, _) in outs]
    return pl.pallas_call(
        body, name=name, grid=_ew_grid(rows, tb, nh, hp, order), in_specs=in_specs, out_specs=out_specs,
        out_shape=out_shape, compiler_params=_params(2),
    )(*[a for (a, _, _, _) in operands])


def f_rms(h, x, g):
    r = lax.rsqrt(jnp.mean(x * x, axis=-1, keepdims=True) + EPS)
    return (x * r * g,)


def _softplus(z):
    return jnp.maximum(z, 0.0) + jnp.log1p(jnp.exp(-jnp.abs(z)))


def f_small(h, sp, p1, p2):
    lane = _iota(sp.shape, 1)
    z = sp + p1
    g = -jnp.exp(p2) * _softplus(z)
    beta = jax.nn.sigmoid(z)
    logf = -_softplus(-z)
    return (jnp.where(lane < NH, g, jnp.where(lane < 2 * NH, beta, jnp.where(lane < 3 * NH, logf, 0.0))),)


def _pick(x, lane_id):
    lane = _iota(x.shape, 1)
    col = jnp.sum(jnp.where(lane == lane_id, x, 0.0), axis=1, keepdims=True)
    return jnp.broadcast_to(col, x.shape)


def f_bcast(h, so, cs):
    return _pick(so, h), _pick(so, h + NH), _pick(cs, h + 2 * NH)


def _shift_down(s):
    def down(x):
        r = pltpu.roll(x, s, 0)
        head = jnp.where(_iota((8, x.shape[1]), 0) >= s, r[:8], 0.0)
        return jnp.concatenate([head, r[8:]], axis=0)

    def up(g):
        n = g.shape[0]
        r = pltpu.roll(g, n - s, 0)
        tail = jnp.where(_iota((8, g.shape[1]), 0) < 8 - s, r[n - 8:], 0.0)
        return jnp.concatenate([r[:n - 8], tail], axis=0)

    @jax.custom_vjp
    def shift(x):
        return down(x)

    shift.defvjp(lambda x: (down(x), None), lambda _, g: (up(g),))
    return shift


def _silu(x):
    return x * jax.nn.sigmoid(x)


def make_f_conv(mode):
    sh1, sh2, sh3 = _shift_down(1), _shift_down(2), _shift_down(3)

    def f(h, x, w):
        sub = _iota(w.shape, 0)

        def tap(i):
            return jnp.sum(jnp.where(sub == i, w, 0.0), axis=0, keepdims=True)

        y = sh3(x) * tap(0)
        y = y + sh2(x) * tap(1)
        y = y + sh1(x) * tap(2)
        y = y + x * tap(3)
        s = _silu(y)
        if mode == "v":
            return (s,)
        n = s * lax.rsqrt(jnp.sum(s * s, axis=-1, keepdims=True) + EPS)
        if mode == "q":
            n = n * (LANES ** -0.5)
        return (n,)

    return f


def f_post(h, o, z, g):
    r = lax.rsqrt(jnp.mean(o * o, axis=-1, keepdims=True) + EPS)
    return (o * r * g * _silu(z),)


def f_merge(h, ga, gb, ya, yb):
    return (jax.nn.sigmoid(ga) * ya + jax.nn.sigmoid(gb) * yb,)


def f_delta(h, do, o):
    return (jnp.broadcast_to(jnp.sum(do * o, axis=1, keepdims=True), o.shape),)


def cumsum_time(name, x, nseq, seq, reverse):
    nb = seq // LANES

    def body(x_ref, o_ref):
        r, c = _iota((LANES, LANES), 0), _iota((LANES, LANES), 1)
        tri = jnp.where((r <= c) if reverse else (r >= c), 1.0, 0.0).astype(F32)
        carry = jnp.zeros((1, LANES), F32)
        for b in (range(nb - 1, -1, -1) if reverse else range(nb)):
            blk = x_ref[b * LANES:(b + 1) * LANES, :]
            o_ref[b * LANES:(b + 1) * LANES, :] = _dot_mask(tri, blk, "nn") + carry
            carry = carry + jnp.sum(blk, axis=0, keepdims=True)

    spec = pl.BlockSpec((seq, LANES), lambda s: (s, 0))
    return pl.pallas_call(body, name=name, grid=(nseq,), in_specs=[spec], out_specs=spec,
                          out_shape=jax.ShapeDtypeStruct(x.shape, F32), compiler_params=_params(1))(x)


def transpose_time(name, x, nseq, seq):
    def body(x_ref, o_ref):
        o_ref[...] = x_ref[...].T

    return pl.pallas_call(
        body, name=name, grid=(nseq,), in_specs=[pl.BlockSpec((seq, LANES), lambda s: (s, 0))],
        out_specs=pl.BlockSpec((LANES, seq), lambda s: (s, 0)),
        out_shape=jax.ShapeDtypeStruct((nseq * LANES, seq), F32), compiler_params=_params(1))(x)


def _gdn_masks():
    n = GDN_ROWS
    r, c = _iota((n, n), 0), _iota((n, n), 1)
    shift = GDN_CHUNK.bit_length() - 1
    same = lax.shift_right_logical(r, shift) == lax.shift_right_logical(c, shift)
    return r, c, same


def _each(fn, *lists):
    return [fn(*xs) for xs in zip(*lists)]


def _gdn_decay(gbs):
    r, c, same = _gdn_masks()
    seg_tril = jnp.where(jnp.logical_and(same, r >= c), 1.0, 0.0).astype(F32)
    g_cum = _each(lambda gb: mm_mask(seg_tril, gb), gbs)
    lane0 = _iota(gbs[0].shape, 1) == 0
    g_col = _each(lambda g: jnp.sum(jnp.where(lane0, g, 0.0), axis=1, keepdims=True), g_cum)
    g_row = _each(lambda g: jnp.sum(jnp.where(r == c, jnp.broadcast_to(g, (GDN_ROWS, GDN_ROWS)), 0.0), axis=0, keepdims=True), g_col)
    return g_cum, _each(lambda a, b: a - b, g_col, g_row)


def gdn_f1(*args):
    qs, ks, gbs, bbs = (list(args[i::4]) for i in range(4))
    r, c, same = _gdn_masks()
    strict = jnp.logical_and(same, r > c)
    _, diff = _gdn_decay(gbs)
    lane0 = _iota(bbs[0].shape, 1) == 0
    beta_col = _each(lambda bb: jnp.sum(jnp.where(lane0, bb, 0.0), axis=1, keepdims=True), bbs)
    kk = _each(lambda k: _dot(k, k, "nt", LO), ks)
    return tuple(_each(lambda b, x, d: jnp.where(strict, b * x * jnp.exp(jnp.where(strict, d, 0.0)), 0.0), beta_col, kk, diff))


def gdn_f2(*args):
    ts, qs, ks, vs, gbs, bbs = (list(args[i::6]) for i in range(6))
    r, c, same = _gdn_masks()
    incl = jnp.logical_and(same, r >= c)
    g_cum, diff = _gdn_decay(gbs)
    decay = _each(lambda d: jnp.where(incl, jnp.exp(jnp.where(incl, d, 0.0)), 0.0), diff)
    e_g = _each(jnp.exp, g_cum)
    v_beta = _each(lambda v, bb: v * bb, vs, bbs)
    k_beta = _each(lambda k, bb, e: k * bb * e, ks, bbs, e_g)
    value = _each(lambda t, x: x + _dot(t, x, "nn", LO), ts, v_beta)
    k_cum = _each(lambda t, x: x + _dot(t, x, "nn", LO), ts, k_beta)
    attn = _each(lambda q, k, d: _dot(q, k, "nt", LO) * d, qs, ks, decay)
    ones = jnp.where(same, 1.0, 0.0).astype(F32)
    g_last = _each(lambda gb: mm_mask(ones, gb), gbs)
    q_dec = _each(lambda q, e: q * e, qs, e_g)
    k_dec = _each(lambda k, gl, g: k * jnp.exp(gl - g), ks, g_last, g_cum)
    return tuple(x for head in zip(value, k_cum, attn, q_dec, k_dec) for x in head)


def tri_inverse(mats):
    n = GDN_ROWS
    r, c = _iota((n, n), 0), _iota((n, n), 1)
    shift = GDN_BASE.bit_length() - 1
    blk = lax.shift_right_logical(r, shift) == lax.shift_right_logical(c, shift)
    each = lambda fn, *lists: [fn(*xs) for xs in zip(*lists)]
    mm = lambda x, y: _dot(x, y, "nn", LO)
    d = each(lambda a: jnp.where(blk, a, 0.0), mats)
    lo = each(lambda a, dd: a - dd, mats, d)
    p = each(lambda dd: -dd, d)
    c_d = p
    for _ in range(shift - 1):
        p = each(mm, p, p)
        c_d = each(lambda cd, pp, prod: cd + pp + prod, c_d, p, each(mm, c_d, p))
    assert GDN_CHUNK // GDN_BASE == 4
    nmat = each(lambda l, prod: l + prod, lo, each(mm, c_d, lo))
    n2 = each(mm, nmat, nmat)
    c_n = each(lambda nn2, nm, prod: (nn2 - nm) - prod, n2, nmat, each(mm, nmat, n2))
    return each(lambda cn, cd, prod: cn + cd + prod, c_n, c_d, each(mm, c_n, c_d))


GDN_AHP = 4


def _gdn_a_specs():
    blk = pl.BlockSpec((GDN_ROWS, GDN_AHP * LANES), lambda i, h: (i, h))
    sq = pl.BlockSpec((GDN_ROWS, GDN_AHP * GDN_ROWS), lambda i, h: (i, h))
    return blk, sq


def _head(ref, hh):
    width = ref.shape[1] // GDN_AHP
    return ref.at[:, hh * width:(hh + 1) * width]


def gdn_a_fwd(q, k, v, gb, bb, rows):
    blk, sq = _gdn_a_specs()

    def body(q_ref, k_ref, v_ref, gb_ref, bb_ref, val_ref, kc_ref, at_ref, qd_ref, kd_ref, t_ref):
        heads = [[_head(r, hh)[...] for r in (q_ref, k_ref, v_ref, gb_ref, bb_ref)] for hh in range(GDN_AHP)]
        t_corr = tri_inverse(list(gdn_f1(*[x for qv, kv, vv, gv, bv in heads for x in (qv, kv, gv, bv)])))
        res = gdn_f2(*[x for t, head in zip(t_corr, heads) for x in (t, *head)])
        for hh in range(GDN_AHP):
            for r, x in zip((val_ref, kc_ref, at_ref, qd_ref, kd_ref, t_ref), (*res[5 * hh:5 * hh + 5], t_corr[hh])):
                _head(r, hh)[...] = x.astype(r.dtype)

    wide = lambda dt: jax.ShapeDtypeStruct((rows, NH * LANES), dt)
    square = jax.ShapeDtypeStruct((rows, NH * GDN_ROWS), BF16)
    return pl.pallas_call(
        body, name="gdn_a_fwd", grid=(rows // GDN_ROWS, NH // GDN_AHP), in_specs=[blk] * 5,
        out_specs=[blk, blk, sq, blk, blk, sq], out_shape=[wide(F32), wide(BF16), square, wide(BF16), wide(BF16), square],
        compiler_params=_params(2))(q, k, v, gb, bb)


def gdn_a_bwd(q, k, v, gb, bb, t_inv, dval, dkc, dat, dqd, dkd, dgb_b, rows):
    blk, sq = _gdn_a_specs()

    def body(q_ref, k_ref, v_ref, gb_ref, bb_ref, t_ref, dval_ref, dkc_ref, dat_ref, dqd_ref, dkd_ref, dgbb_ref,
             dq_ref, dk_ref, dv_ref, dgb_ref, dbb_ref):
        hs = range(GDN_AHP)
        heads = [[_head(r, hh)[...] for r in (q_ref, k_ref, v_ref, gb_ref, bb_ref)] for hh in hs]
        tvs = [_head(t_ref, hh)[...].astype(F32) for hh in hs]
        _, vjp1 = jax.vjp(gdn_f1, *[x for qv, kv, vv, gv, bv in heads for x in (qv, kv, gv, bv)])
        _, vjp2 = jax.vjp(gdn_f2, *[x for t, head in zip(tvs, heads) for x in (t, *head)])
        g2 = vjp2(tuple(_head(r, hh)[...] for hh in hs for r in (dval_ref, dkc_ref, dat_ref, dqd_ref, dkd_ref)))
        dts = [g2[6 * hh] for hh in hs]
        left = _each(lambda dt, tv: dt + _dot(tv, dt, "tn", LO), dts, tvs)
        g1 = vjp1(tuple(_each(lambda lf, tv: -(lf + _dot(lf, tv, "nt", LO)), left, tvs)))
        for hh in hs:
            dq1, dk1, dgb1, dbb1 = g1[4 * hh:4 * hh + 4]
            _, dq2, dk2, dv2, dgb2, dbb2 = g2[6 * hh:6 * hh + 6]
            _head(dq_ref, hh)[...] = dq1 + dq2
            _head(dk_ref, hh)[...] = dk1 + dk2
            _head(dv_ref, hh)[...] = dv2
            _head(dgb_ref, hh)[...] = dgb1 + dgb2 + _head(dgbb_ref, hh)[...]
            _head(dbb_ref, hh)[...] = dbb1 + dbb2

    wide = jax.ShapeDtypeStruct((rows, NH * LANES), F32)
    return pl.pallas_call(
        body, name="gdn_a_bwd", grid=(rows // GDN_ROWS, NH // GDN_AHP),
        in_specs=[blk] * 5 + [sq, blk, blk, sq, blk, blk, blk], out_specs=[blk] * 5, out_shape=[wide] * 5,
        compiler_params=_params(2))(q, k, v, gb, bb, t_inv, dval, dkc, dat, dqd, dkd, dgb_b)


N_CH = GDN_ROWS // GDN_CHUNK


GDN_HP = 8


def gdn_chunk(c):
    def f(*args):
        val, kc, at, qd, kd, gb, s = (list(args[i::7]) for i in range(7))
        zero = jnp.zeros((GDN_CHUNK, LANES), F32)
        v_new = _each(lambda v, k, st: v - _dot(k, st, "nn", LO), val, kc, s)
        v_pad = _each(lambda v: jnp.concatenate([zero] * c + [v] + [zero] * (N_CH - 1 - c), axis=0), v_new)
        out = _each(lambda q, st, a, vp: _dot(q, st, "nn", LO) + _dot(a, vp, "nn", LO), qd, s, at, v_pad)
        dec = _each(lambda g: jnp.exp(jnp.sum(g, axis=0, keepdims=True)), gb)
        s_new = _each(lambda st, d, k, v: st * d + _dot(k, v, "tn", LO), s, dec, kd, v_new)
        return tuple(x for head in zip(out, s_new) for x in head)

    return f


def _gdn_piece(ref, hh, c):
    width = ref.shape[1] // GDN_HP
    return ref.at[c * GDN_CHUNK:(c + 1) * GDN_CHUNK, hh * width:(hh + 1) * width]


def _gdn_snap(ref, hh, c):
    row = (hh * N_CH + c) * LANES
    return ref.at[row:row + LANES, :]


def _gdn_b_specs(nb, rev):
    def blk_row(s, j):
        return s * nb + (nb - 1 - j if rev else j)

    blk = pl.BlockSpec((GDN_ROWS, GDN_HP * LANES), lambda s, hb, j: (blk_row(s, j), hb))
    sq = pl.BlockSpec((GDN_ROWS, GDN_HP * GDN_ROWS), lambda s, hb, j: (blk_row(s, j), hb))
    snap = pl.BlockSpec((GDN_HP * N_CH * LANES, LANES), lambda s, hb, j: (blk_row(s, j) * (NH // GDN_HP) + hb, 0))
    return blk, sq, snap


def gdn_b_fwd(val, kc, at, qd, kd, gb, nseq, seq):
    nb = seq // GDN_ROWS
    rows = nseq * seq
    blk, sq, snap = _gdn_b_specs(nb, False)

    def body(val_ref, kc_ref, at_ref, qd_ref, kd_ref, gb_ref, o_ref, snap_ref, s_ref):
        @pl.when(pl.program_id(2) == 0)
        def _():
            s_ref[...] = jnp.zeros_like(s_ref)

        hs = range(GDN_HP)
        states = [s_ref[hh] for hh in hs]
        for c in range(N_CH):
            for hh in hs:
                _gdn_snap(snap_ref, hh, c)[...] = states[hh]
            res = gdn_chunk(c)(*[x for hh in hs for x in (
                *[_gdn_piece(r, hh, c)[...].astype(F32) for r in (val_ref, kc_ref, at_ref, qd_ref, kd_ref, gb_ref)], states[hh])])
            for hh in hs:
                _gdn_piece(o_ref, hh, c)[...] = res[2 * hh]
            states = [res[2 * hh + 1] for hh in hs]
        for hh in hs:
            s_ref[hh] = states[hh]

    return pl.pallas_call(
        body, name="gdn_b_fwd", grid=(nseq, NH // GDN_HP, nb), in_specs=[blk, blk, sq, blk, blk, blk], out_specs=[blk, snap],
        out_shape=[jax.ShapeDtypeStruct((rows, NH * LANES), F32),
                   jax.ShapeDtypeStruct((nseq * nb * NH * N_CH * LANES, LANES), F32)],
        scratch_shapes=[pltpu.VMEM((GDN_HP, LANES, LANES), F32)], compiler_params=_params(3))(val, kc, at, qd, kd, gb)


def gdn_b_bwd(val, kc, at, qd, kd, gb, snaps, do, nseq, seq):
    nb = seq // GDN_ROWS
    rows = nseq * seq
    blk, sq, snap = _gdn_b_specs(nb, True)

    def body(val_ref, kc_ref, at_ref, qd_ref, kd_ref, gb_ref, snap_ref, do_ref,
             dval_ref, dkc_ref, dat_ref, dqd_ref, dkd_ref, dgb_ref, ds_ref):
        @pl.when(pl.program_id(2) == 0)
        def _():
            ds_ref[...] = jnp.zeros_like(ds_ref)

        hs = range(GDN_HP)
        d_states = [ds_ref[hh] for hh in hs]
        for c in reversed(range(N_CH)):
            _, vjp = jax.vjp(gdn_chunk(c), *[x for hh in hs for x in (
                *[_gdn_piece(r, hh, c)[...].astype(F32) for r in (val_ref, kc_ref, at_ref, qd_ref, kd_ref, gb_ref)],
                _gdn_snap(snap_ref, hh, c)[...])])
            grads = vjp(tuple(x for hh in hs for x in (_gdn_piece(do_ref, hh, c)[...], d_states[hh])))
            for hh in hs:
                for i, r in enumerate([dval_ref, dkc_ref, dat_ref, dqd_ref, dkd_ref, dgb_ref]):
                    _gdn_piece(r, hh, c)[...] = grads[7 * hh + i]
            d_states = [grads[7 * hh + 6] for hh in hs]
        for hh in hs:
            ds_ref[hh] = d_states[hh]

    wide = jax.ShapeDtypeStruct((rows, NH * LANES), F32)
    square = jax.ShapeDtypeStruct((rows, NH * GDN_ROWS), F32)
    return pl.pallas_call(
        body, name="gdn_b_bwd", grid=(nseq, NH // GDN_HP, nb), in_specs=[blk, blk, sq, blk, blk, blk, snap, blk],
        out_specs=[blk, blk, sq, blk, blk, blk], out_shape=[wide, wide, square, wide, wide, wide],
        scratch_shapes=[pltpu.VMEM((GDN_HP, LANES, LANES), F32)], compiler_params=_params(3))(val, kc, at, qd, kd, gb, snaps, do)


FOX_Q, FOX_K, FOX_V = 4 * NH, 5 * NH, 6 * NH
FOX_SCALE = LANES ** -0.5


def _head_row(ct_ref, h, off, width):
    blk = ct_ref[:, pl.ds(off, width)]
    return jnp.sum(jnp.where(_iota(blk.shape, 0) == h, blk, 0.0), axis=0, keepdims=True)


def _col(x):
    return jnp.max(x, axis=1, keepdims=True)


def _row(x):
    return jnp.max(x.T, axis=0, keepdims=True)


def _causal(shape, q_dim):
    return _iota(shape, q_dim) >= _iota(shape, 1 - q_dim)


FOX_HP = 4


def _fox_specs(seq, tile, n_tiles):
    tblk = pl.BlockSpec((tile, FOX_HP * LANES), lambda s, h, i: (s * n_tiles + i, h))
    vtblk = pl.BlockSpec((tile, FOX_HP * LANES), lambda s, h, i: (s * n_tiles + i, h + FOX_V // FOX_HP))
    full = pl.BlockSpec((seq, FOX_HP * LANES), lambda s, h, i: (s, h))
    vfull = pl.BlockSpec((seq, FOX_HP * LANES), lambda s, h, i: (s, h + FOX_V // FOX_HP))
    ctb = pl.BlockSpec((NH, seq), lambda s, h, i: (s * (LANES // NH) + 2, 0))
    return tblk, vtblk, full, vfull, ctb


def _lanes_of(hh):
    return slice(hh * LANES, (hh + 1) * LANES)


def fox_fwd(qn, kn, proj, ct, nseq, seq):
    tq = tk = min(ATT_TILE, seq)
    nq = seq // tq
    rows = nseq * seq
    qblk, _, full, vfull, ctb = _fox_specs(seq, tq, nq)
    hs = range(FOX_HP)

    def body(q_ref, k_ref, v_ref, ct_ref, o_ref, o16_ref, lse_ref):
        hb, i = pl.program_id(1), pl.program_id(2)
        q = [q_ref[:, _lanes_of(hh)] for hh in hs]

        def step(j, carry, diag):
            m, l, acc = (list(carry[t::3]) for t in range(3))
            off = pl.multiple_of(j * tk, tk)
            k = [k_ref[pl.ds(off, tk), _lanes_of(hh)] for hh in hs]
            v = [v_ref[pl.ds(off, tk), _lanes_of(hh)].astype(BF16) for hh in hs]
            ck = [_head_row(ct_ref, hb * FOX_HP + hh, off, tk) for hh in hs]
            s = _each(lambda qq, kk, cc: _dot(qq, kk, "nt") * FOX_SCALE - cc, q, k, ck)
            if diag:
                s = _each(lambda x: jnp.where(_causal(x.shape, 0), x, NEG), s)
            m_new = _each(lambda mm, x: jnp.maximum(mm, jnp.max(x, axis=1, keepdims=True)), m, s)
            p = _each(lambda x, mm: jnp.exp(x - mm), s, m_new)
            alpha = _each(lambda mo, mn: jnp.exp(mo - mn), m, m_new)
            l = _each(lambda a, ll, pp: a * ll + jnp.sum(pp, axis=1, keepdims=True), alpha, l, p)
            acc = _each(lambda a, ac, pp, vv: a * ac + _dot(pp.astype(BF16), vv, "nn"), alpha, acc, p, v)
            return tuple(x for head in zip(m_new, l, acc) for x in head)

        init = (jnp.full((tq, 1), NEG, F32), jnp.zeros((tq, 1), F32), jnp.zeros((tq, LANES), F32)) * FOX_HP
        res = step(i, lax.fori_loop(0, i, lambda j, c: step(j, c, False), init), True)
        for hh in hs:
            m, l, acc = res[3 * hh:3 * hh + 3]
            o = acc / l
            o_ref[:, _lanes_of(hh)] = o
            o16_ref[:, _lanes_of(hh)] = o.astype(BF16)
            lse_ref[:, _lanes_of(hh)] = jnp.broadcast_to(m + jnp.log(l), (tq, LANES))

    wide = (rows, NH * LANES)
    return pl.pallas_call(
        body, name="fox_fwd", grid=(nseq, NH // FOX_HP, nq), in_specs=[qblk, full, vfull, ctb], out_specs=[qblk] * 3,
        out_shape=[jax.ShapeDtypeStruct(wide, F32), jax.ShapeDtypeStruct(wide, BF16), jax.ShapeDtypeStruct(wide, F32)],
        compiler_params=_params(3))(qn, kn, proj, ct)


def fox_dq(qn, kn, proj, ct, do, lse, delta, nseq, seq):
    tq = tk = min(ATT_TILE, seq)
    nq = seq // tq
    rows = nseq * seq
    qblk, _, full, vfull, ctb = _fox_specs(seq, tq, nq)
    hs = range(FOX_HP)

    def body(q_ref, k_ref, v_ref, ct_ref, do_ref, lse_ref, dl_ref, dq_ref, dc_ref):
        hb, i = pl.program_id(1), pl.program_id(2)
        q = [q_ref[:, _lanes_of(hh)] for hh in hs]
        lse = [_col(lse_ref[:, _lanes_of(hh)]) for hh in hs]
        delta = [_col(dl_ref[:, _lanes_of(hh)]) for hh in hs]
        do16 = [do_ref[:, _lanes_of(hh)].astype(BF16) for hh in hs]

        def step(j, carry, diag):
            dq, dc = (list(carry[t::2]) for t in range(2))
            off = pl.multiple_of(j * tk, tk)
            k = [k_ref[pl.ds(off, tk), _lanes_of(hh)] for hh in hs]
            v = [v_ref[pl.ds(off, tk), _lanes_of(hh)].astype(BF16) for hh in hs]
            ck = [_head_row(ct_ref, hb * FOX_HP + hh, off, tk) for hh in hs]
            p = _each(lambda qq, kk, cc, ll: jnp.exp(_dot(qq, kk, "nt") * FOX_SCALE - cc - ll), q, k, ck, lse)
            if diag:
                p = _each(lambda x: jnp.where(_causal(x.shape, 0), x, 0.0), p)
            dp = _each(lambda d, vv: _dot(d, vv, "nt"), do16, v)
            ds = _each(lambda pp, d, dl: pp * (d - dl), p, dp, delta)
            dq = _each(lambda a, x, kk: a + _dot(x.astype(BF16), kk, "nn"), dq, ds, k)
            dc = _each(lambda a, x: a + jnp.sum(x, axis=1, keepdims=True), dc, ds)
            return tuple(x for head in zip(dq, dc) for x in head)

        init = (jnp.zeros((tq, LANES), F32), jnp.zeros((tq, 1), F32)) * FOX_HP
        res = step(i, lax.fori_loop(0, i, lambda j, c: step(j, c, False), init), True)
        for hh in hs:
            dq_ref[:, _lanes_of(hh)] = res[2 * hh] * FOX_SCALE
            dc_ref[:, _lanes_of(hh)] = jnp.where(_iota((tq, LANES), 1) == 0, res[2 * hh + 1], 0.0)

    wide = jax.ShapeDtypeStruct((rows, NH * LANES), F32)
    return pl.pallas_call(
        body, name="fox_dq", grid=(nseq, NH // FOX_HP, nq), in_specs=[qblk, full, vfull, ctb, qblk, qblk, qblk],
        out_specs=[qblk, qblk], out_shape=[wide, wide], compiler_params=_params(3))(qn, kn, proj, ct, do, lse, delta)


def fox_dkv(qn, kn, proj, cb, do, lse, delta, nseq, seq):
    tq = tk = min(ATT_TILE, seq)
    nq = seq // tq
    rows = nseq * seq
    kblk, vblk, full, _, _ = _fox_specs(seq, tk, nq)
    hs = range(FOX_HP)

    def body(q_ref, k_ref, v_ref, cb_ref, do_ref, lse_ref, dl_ref, dk_ref, dv_ref, dc_ref):
        j = pl.program_id(2)
        k = [k_ref[:, _lanes_of(hh)] for hh in hs]
        v16 = [v_ref[:, _lanes_of(hh)].astype(BF16) for hh in hs]
        ck = [_col(cb_ref[:, _lanes_of(hh)]) for hh in hs]

        def step(i, carry, diag):
            dk, dv, dc = (list(carry[t::3]) for t in range(3))
            off = pl.multiple_of(i * tq, tq)
            q = [q_ref[pl.ds(off, tq), _lanes_of(hh)] for hh in hs]
            do16 = [do_ref[pl.ds(off, tq), _lanes_of(hh)].astype(BF16) for hh in hs]
            lse = [_row(lse_ref[pl.ds(off, tq), _lanes_of(hh)]) for hh in hs]
            delta = [_row(dl_ref[pl.ds(off, tq), _lanes_of(hh)]) for hh in hs]
            p = _each(lambda kk, qq, cc, ll: jnp.exp(_dot(kk, qq, "nt") * FOX_SCALE - cc - ll), k, q, ck, lse)
            if diag:
                p = _each(lambda x: jnp.where(_causal(x.shape, 1), x, 0.0), p)
            dv = _each(lambda a, pp, d: a + _dot(pp.astype(BF16), d, "nn"), dv, p, do16)
            ds = _each(lambda pp, vv, d, dl: pp * (_dot(vv, d, "nt") - dl), p, v16, do16, delta)
            dk = _each(lambda a, x, qq: a + _dot(x.astype(BF16), qq, "nn"), dk, ds, q)
            dc = _each(lambda a, x: a + jnp.sum(x, axis=1, keepdims=True), dc, ds)
            return tuple(x for head in zip(dk, dv, dc) for x in head)

        zero = jnp.zeros((tk, LANES), F32)
        carry = step(j, (zero, zero, jnp.zeros((tk, 1), F32)) * FOX_HP, True)
        res = lax.fori_loop(j + 1, nq, lambda i, c: step(i, c, False), carry)
        for hh in hs:
            dk, dv, dc = res[3 * hh:3 * hh + 3]
            dk_ref[:, _lanes_of(hh)] = dk * FOX_SCALE
            dv_ref[:, _lanes_of(hh)] = dv.astype(BF16)
            dc_ref[:, _lanes_of(hh)] = jnp.where(_iota((tk, LANES), 1) == 0, -dc, 0.0)

    wide = (rows, NH * LANES)
    return pl.pallas_call(
        body, name="fox_dkv", grid=(nseq, NH // FOX_HP, nq), in_specs=[full, kblk, vblk, kblk, full, full, full],
        out_specs=[kblk, kblk, kblk],
        out_shape=[jax.ShapeDtypeStruct(wide, F32), jax.ShapeDtypeStruct(wide, BF16), jax.ShapeDtypeStruct(wide, F32)],
        compiler_params=_params(3))(qn, kn, proj, cb, do, lse, delta)


def _adamw_update(w, g, m, v):
    m_new = ADAM_B1 * m + (1.0 - ADAM_B1) * g
    v_new = ADAM_B2 * v + (1.0 - ADAM_B2) * (g * g)
    m_hat = m_new / (1.0 - ADAM_B1 ** ADAM_STEP)
    v_hat = v_new / (1.0 - ADAM_B2 ** ADAM_STEP)
    return -ADAM_LR * (m_hat / (jnp.sqrt(v_hat) + ADAM_EPS) + ADAM_WD * w), m_new, v_new


def adamw(name, w, g, m, v):
    rows, cols = w.shape
    tb = min(rows, 128)
    assert rows % tb == 0
    blk = pl.BlockSpec((tb, cols), lambda i: (i, 0))

    def body(w_ref, g_ref, m_ref, v_ref, d_ref, mo_ref, vo_ref):
        d_ref[...], mo_ref[...], vo_ref[...] = _adamw_update(w_ref[...], g_ref[...], m_ref[...], v_ref[...])

    shp = jax.ShapeDtypeStruct(w.shape, F32)
    return pl.pallas_call(body, name=name, grid=(rows // tb,), in_specs=[blk] * 4, out_specs=[blk] * 3,
                          out_shape=[shp] * 3, compiler_params=_params(1))(w, g, m, v)


SPLIT_TILE = 128


def _tiled(shape2d, ax, n_lead, index):
    blk = (SPLIT_TILE, shape2d[1]) if ax == 0 else (shape2d[0], SPLIT_TILE)

    def index_map(*args):
        *lead, t = index(*args)
        return (*lead, t, 0) if ax == 0 else (*lead, 0, t)

    return pl.BlockSpec((None,) * n_lead + blk, index_map)


def adamw_halves(name, w, mine, other, m, v, c, ax):
    steps = w.shape[ax] // 2 // SPLIT_TILE
    assert w.shape[ax] == 2 * steps * SPLIT_TILE

    def body(c_ref, w_ref, mine_ref, other_ref, m_ref, v_ref, g_ref, d_ref, mo_ref, vo_ref):
        g = jnp.where(pl.program_id(0) // steps == c_ref[0], mine_ref[...], other_ref[...])
        g_ref[...] = g
        d_ref[...], mo_ref[...], vo_ref[...] = _adamw_update(w_ref[...], g, m_ref[...], v_ref[...])

    blk = _tiled(w.shape, ax, 0, lambda i, c_ref: (i,))
    hblk = _tiled(mine.shape, ax, 0, lambda i, c_ref: (i % steps,))
    grid_spec = pltpu.PrefetchScalarGridSpec(num_scalar_prefetch=1, grid=(2 * steps,),
                                             in_specs=[blk, hblk, hblk, blk, blk], out_specs=[blk] * 4)
    shp = jax.ShapeDtypeStruct(w.shape, F32)
    return pl.pallas_call(body, name=name, grid_spec=grid_spec, out_shape=[shp] * 4,
                          compiler_params=_params(1))(c, w, mine, other, m, v)


def add_chips(name, slots, parts, chip, axes):
    outs = []
    for idx, (x, own, ax) in enumerate(zip(slots, parts, axes)):
        n, shape2d = x.shape[0], x.shape[1:]
        steps = shape2d[ax] // SPLIT_TILE
        assert shape2d[ax] == steps * SPLIT_TILE

        def body(me_ref, *refs, n=n):
            o_ref = refs[n + 1]
            acc = None
            for t in range(n):
                term = jnp.where(me_ref[0] == t, refs[n][...], refs[t][...]).astype(F32)
                acc = term if acc is None else acc + term
            o_ref[...] = acc

        def filled(t, n=n):
            return lambda i, me_ref: (jnp.where(me_ref[0] == t, (t + 1) % n, t), i)

        grid_spec = pltpu.PrefetchScalarGridSpec(
            num_scalar_prefetch=1, grid=(steps,),
            in_specs=[_tiled(shape2d, ax, 1, filled(t)) for t in range(n)]
            + [_tiled(shape2d, ax, 1, lambda i, me_ref: (me_ref[0], i))],
            out_specs=_tiled(shape2d, ax, 0, lambda i, me_ref: (i,)))
        outs.append(pl.pallas_call(
            body, name=f"{name}_{idx}", grid_spec=grid_spec, out_shape=jax.ShapeDtypeStruct(shape2d, F32),
            compiler_params=_params(1))(chip, *([x] * n), own))
    return outs


def add_pair(name, gs, rs, c, axes):
    outs = []
    for idx, (g, r, ax) in enumerate(zip(gs, rs, axes)):
        nb = r.shape[0]
        steps = r.shape[1 + ax] // SPLIT_TILE
        assert r.shape[1 + ax] == steps * SPLIT_TILE

        def body(c_ref, g_ref, r_ref, o_ref):
            o_ref[...] = (g_ref[...] + r_ref[...]).astype(BF16)

        grid_spec = pltpu.PrefetchScalarGridSpec(
            num_scalar_prefetch=1, grid=(nb, steps),
            in_specs=[_tiled(g.shape[1:], ax, 1, lambda b, i, c_ref: (b, c_ref[0] * steps + i)),
                      _tiled(r.shape[1:], ax, 1, lambda b, i, c_ref: (b, i))],
            out_specs=_tiled(r.shape[1:], ax, 1, lambda b, i, c_ref: (b, i)))
        outs.append(pl.pallas_call(
            body, name=f"{name}_{idx}", grid_spec=grid_spec, out_shape=jax.ShapeDtypeStruct(r.shape, BF16),
            compiler_params=_params(2))(c, g, r))
    return outs


def _place():
    x, y, c = lax.axis_index("x"), lax.axis_index("y"), lax.axis_index("c")
    return x, y, c, [(1 - x, y), (x, 1 - y), (1 - x, 1 - y)]


def _remote(src, dst, send_sem, recv_sem, dev):
    return pltpu.make_async_remote_copy(src_ref=src, dst_ref=dst, send_sem=send_sem, recv_sem=recv_sem,
                                        device_id=dev, device_id_type=MESH)


def _half(ref, lead, ax, which):
    size = ref.shape[len(lead) + ax] // 2
    part = pl.ds(which * size, size)
    return ref.at[(*lead, part, slice(None)) if ax == 0 else (*lead, slice(None), part)]


def gather_weights(shards, axes):
    n = len(shards)

    def body(*refs):
        ins, outs = refs[:n], refs[n:2 * n]
        ici_s, ici_r, d2d_s, d2d_r = refs[2 * n:]
        x, y, c, chips = _place()
        me = 2 * x + y
        sends, passes = [], []
        for w in range(n):
            cp = _remote(ins[w], outs[w].at[me], d2d_s.at[3 * n + w], d2d_r.at[3 * n + w], (x, y, 1 - c))
            cp.start()
            passes.append(cp)
        for w in range(n):
            for j, (ox, oy) in enumerate(chips):
                cp = _remote(_half(ins[w], (), axes[w], c), _half(outs[w], (me,), axes[w], c),
                             ici_s.at[3 * w + j], ici_r.at[3 * w + j], (ox, oy, c))
                cp.start()
                sends.append(cp)
        for w in range(n):
            for j, (ox, oy) in enumerate(chips):
                landed = _half(outs[w], (2 * ox + oy,), axes[w], c)
                _remote(landed, landed, ici_s.at[3 * w + j], ici_r.at[3 * w + j], (ox, oy, c)).wait_recv()
                cp = _remote(landed, landed, d2d_s.at[3 * w + j], d2d_r.at[3 * w + j], (x, y, 1 - c))
                cp.start()
                passes.append(cp)
        for w in range(n):
            for j, (ox, oy) in enumerate(chips):
                other = _half(outs[w], (2 * ox + oy,), axes[w], 1 - c)
                _remote(other, other, d2d_s.at[3 * w + j], d2d_r.at[3 * w + j], (x, y, 1 - c)).wait_recv()
            own = outs[w].at[me]
            _remote(own, own, d2d_s.at[3 * n + w], d2d_r.at[3 * n + w], (x, y, 1 - c)).wait_recv()
        for cp in sends + passes:
            cp.wait_send()

    return pl.pallas_call(
        body, name="gather_weights", in_specs=[ANY] * n, out_specs=[ANY] * n,
        out_shape=[jax.ShapeDtypeStruct((4,) + s.shape, s.dtype) for s in shards],
        scratch_shapes=[pltpu.SemaphoreType.DMA((3 * n,))] * 2 + [pltpu.SemaphoreType.DMA((4 * n,))] * 2,
    )(*shards)


HBM = pl.BlockSpec(memory_space=pltpu.HBM)
SEM = pl.BlockSpec(memory_space=pltpu.SEMAPHORE)
DATAFLOW = pltpu.SideEffectType.DATAFLOW_SIDE_EFFECTING


def _hbm(a):
    return pltpu.with_memory_space_constraint(a, pltpu.HBM)


class SplitExchange:
    def __init__(self, name, srcs, zone_shapes, n_sems, plan):
        self.name, self.n, self.n_sems, self.plan = name, len(srcs), n_sems, plan
        self.srcs = [_hbm(s) for s in srcs]
        self.zones = [_hbm(lax.empty(shape, s.dtype)) for shape, s in zip(zone_shapes, srcs)]

    def start(self, after):
        n, n_after = self.n, len(after)

        def body(*refs):
            ins, lands = refs[:n], refs[n:2 * n]
            send, recv, token = refs[2 * n + n_after], refs[2 * n + n_after + 1], refs[-1]
            for src, dst, si, ri, dev in self.plan(ins, lands)[0]:
                _remote(src, dst, send.at[si], recv.at[ri], dev).start()
            token[...] = jnp.zeros_like(token)

        res = pl.pallas_call(
            body, name=f"{self.name}_start", in_specs=[HBM] * (2 * n) + [ANY] * n_after,
            out_specs=[SEM, SEM] + [HBM] * (2 * n) + [pl.BlockSpec(memory_space=pltpu.VMEM)],
            out_shape=[pltpu.SemaphoreType.DMA((self.n_sems,)), pltpu.SemaphoreType.DMA((self.n_sems,))]
            + [pltpu.HBM(a.shape, a.dtype) for a in self.srcs + self.zones] + [jax.ShapeDtypeStruct((8, LANES), F32)],
            input_output_aliases={i: 2 + i for i in range(2 * n)},
            compiler_params=pltpu.CompilerParams(has_side_effects=DATAFLOW),
        )(*self.srcs, *self.zones, *after)
        self.sems, self.srcs, self.zones = res[:2], list(res[2:2 + n]), list(res[2 + n:2 + 2 * n])
        return res[-1]

    def wait(self, after):
        n = self.n

        def body(*refs):
            ins, lands = refs[:n], refs[n:2 * n]
            send, recv = refs[2 * n], refs[2 * n + 1]
            sends, arrivals = self.plan(ins, lands)
            for src, _, si, _, dev in sends:
                _remote(src, src, send.at[si], recv.at[si], dev).wait_send()
            for landed, ri in arrivals:
                _remote(landed, landed, send.at[ri], recv.at[ri], _place()[:3]).wait_recv()

        res = pl.pallas_call(
            body, name=f"{self.name}_wait", in_specs=[HBM] * (2 * n) + [SEM, SEM, ANY], out_specs=[HBM] * (2 * n),
            out_shape=[pltpu.HBM(a.shape, a.dtype) for a in self.srcs + self.zones],
            input_output_aliases={i: i for i in range(2 * n)},
            compiler_params=pltpu.CompilerParams(has_side_effects=DATAFLOW),
        )(*self.srcs, *self.zones, *self.sems, after)
        self.srcs = list(res[:n])
        return list(res[n:])


def split_gather(shards):
    n = len(shards)

    def plan(ins, lands):
        x, y, c, chips = _place()
        me = 2 * x + y
        sends, arrivals = [], []
        for w in range(n):
            for j, (ox, oy) in enumerate(chips):
                for k in range(2):
                    base = 2 * (3 * w + j)
                    sends.append((_half(ins[w], (), 0, c), _half(lands[w], (me,), 0, c), base + k, base + c, (ox, oy, k)))
                    arrivals.append((_half(lands[w], (2 * ox + oy,), 0, k), base + k))
            sends.append((ins[w], lands[w].at[me], 6 * n + w, 6 * n + w, (x, y, 1 - c)))
            arrivals.append((lands[w].at[me], 6 * n + w))
        return sends, arrivals

    return SplitExchange("gather", shards, [(4,) + s.shape for s in shards], 7 * n, plan)


def split_pair_swap(name, grads, axes):
    def plan(ins, lands):
        x, y, c, _ = _place()
        sends = [(_half(ins[w], (slice(None),), axes[w], 1 - c), lands[w], w, w, (x, y, 1 - c)) for w in range(len(ins))]
        return sends, [(lands[w], w) for w in range(len(ins))]

    halved = [tuple(d // 2 if i == 1 + ax else d for i, d in enumerate(g.shape)) for g, ax in zip(grads, axes)]
    return SplitExchange(name, grads, halved, len(grads), plan)


def split_chip_exchange(name, parts):
    def plan(ins, lands):
        x, y, c, chips = _place()
        sends, arrivals = [], []
        for w in range(len(ins)):
            for j, (ox, oy) in enumerate(chips):
                sends.append((ins[w].at[2 * ox + oy], lands[w].at[2 * x + y], 3 * w + j, 3 * w + j, (ox, oy, c)))
                arrivals.append((lands[w].at[2 * ox + oy], 3 * w + j))
        return sends, arrivals

    return SplitExchange(name, parts, [p.shape for p in parts], 3 * len(parts), plan)


def split_pair_send(halves):
    def plan(ins, lands):
        x, y, c, _ = _place()
        return ([(ins[w], lands[w], w, w, (x, y, 1 - c)) for w in range(len(ins))],
                [(lands[w], w) for w in range(len(ins))])

    return SplitExchange("pair_send", halves, [h.shape for h in halves], len(halves), plan)


def pair_send(halves):
    n = len(halves)

    def body(*refs):
        ins, outs = refs[:n], refs[n:2 * n]
        send, recv = refs[2 * n:]
        x, y, c, _ = _place()
        cps = [_remote(ins[w], outs[w], send.at[w], recv.at[w], (x, y, 1 - c)) for w in range(n)]
        for cp in cps:
            cp.start()
        for cp in cps:
            cp.wait_recv()
        for cp in cps:
            cp.wait_send()

    return pl.pallas_call(
        body, name="pair_send", in_specs=[ANY] * n, out_specs=[ANY] * n,
        out_shape=[jax.ShapeDtypeStruct(h.shape, h.dtype) for h in halves],
        scratch_shapes=[pltpu.SemaphoreType.DMA((n,))] * 2,
    )(*halves)


def all_reduce_small(name, vec, after=()):
    rows = vec.shape[0]

    def body(v_ref, *refs):
        o_ref, buf, send, recv = refs[len(after):]
        x, y, c, _ = _place()
        me = 4 * x + 2 * y + c
        buf[me] = v_ref[...]
        cps = []
        for k in range(1, 8):
            kx, ky, kc = (k >> 2) & 1, (k >> 1) & 1, k & 1
            peer = (x if kx == 0 else 1 - x, y if ky == 0 else 1 - y, c if kc == 0 else 1 - c)
            cp = _remote(v_ref, buf.at[me], send.at[k - 1], recv.at[k - 1], peer)
            cp.start()
            cps.append(cp)
        for k in range(1, 8):
            kx, ky, kc = (k >> 2) & 1, (k >> 1) & 1, k & 1
            px, py, pc = (x if kx == 0 else 1 - x, y if ky == 0 else 1 - y, c if kc == 0 else 1 - c)
            slot = buf.at[4 * px + 2 * py + pc]
            _remote(slot, slot, send.at[k - 1], recv.at[k - 1], (px, py, pc)).wait_recv()
        for cp in cps:
            cp.wait_send()
        acc = buf[0]
        for d in range(1, 8):
            acc = acc + buf[d]
        o_ref[...] = acc

    vm = pl.BlockSpec(memory_space=pltpu.VMEM)
    return pl.pallas_call(
        body, name=name, in_specs=[vm] + [ANY] * len(after), out_specs=vm, out_shape=jax.ShapeDtypeStruct(vec.shape, F32),
        scratch_shapes=[pltpu.VMEM((8, rows, LANES), F32), pltpu.SemaphoreType.DMA((7,)), pltpu.SemaphoreType.DMA((7,))],
    )(vec, *after)


class NoExchange:
    def __init__(self, late):
        self.late = late

    def late_weights(self, after):
        return self.late

    def reduce_start(self, grads):
        return jnp.zeros((8, LANES), F32)

    def reduce_exchange(self, after):
        return jnp.zeros((8, LANES), F32)

    def reduce_finish(self, after):
        return jnp.zeros((8, LANES), F32)

    def input_grad_start(self, dw_main, dw_small):
        return jnp.zeros((8, LANES), F32)

    def input_grad_exchange(self, after):
        return jnp.zeros((8, LANES), F32)


def local_step(x2, tgt2, g1, g2, gdn_ng, qn_g, kn_g, p1, p2, conv_w, wt_main, wt_small, hooks, nseq, seq):
    rows, dm = x2.shape
    wide = NH * LANES
    row = lambda a, off=0, w=None: (a, "row", off, a.shape[1] if w is None else w)
    rowh = lambda a, off=0, w=LANES: (a, "rowh", off, w)
    par = lambda a: (a, "par", 0, a.shape[1])
    parh = lambda a, off=0: (a, "parh", off, LANES)
    o_row = lambda w, dt: (w, "row", w, dt)
    o_rowh = lambda dt, tw=wide, w=LANES: (tw, "rowh", w, dt)

    u, = ew_fwd("rms1", f_rms, [row(x2), par(g1)], [o_row(dm, BF16)], rows)
    proj = matmul("mm_in", u, wt_main, "nt", BF16)
    sp = matmul("mm_in_small", u, wt_small, "nt", F32)
    so, = ew_fwd("small", f_small, [row(sp), par(p1), par(p2)], [o_row(LANES, F32)], rows)
    cs = cumsum_time("cumsum", so, nseq, seq, False)
    gb, bb, cb = ew_fwd("bcast", f_bcast, [row(so), row(cs)], [o_rowh(F32)] * 3, rows, NH)
    ct = transpose_time("c_time_major", cs, nseq, seq)
    conv = {}
    for mode, off in (("q", 0), ("k", NH), ("v", 2 * NH)):
        conv[mode], = ew_fwd(f"conv_{mode}", make_f_conv(mode), [rowh(proj, off), parh(conv_w, off)], [o_rowh(F32)],
                             rows, NH, seq, "hi", CONV_HEADS)
    val, kcum, attn, qdec, kdec, t_inv = gdn_a_fwd(conv["q"], conv["k"], conv["v"], gb, bb, rows)
    o_a, snaps = gdn_b_fwd(val, kcum, attn, qdec, kdec, gb, nseq, seq)
    ya_in, = ew_fwd("gdn_post", f_post, [rowh(o_a), rowh(proj, 3 * NH), par(gdn_ng)], [o_rowh(BF16)], rows, NH)
    fqn, = ew_fwd("fox_qn", f_rms, [rowh(proj, FOX_Q), par(qn_g)], [o_rowh(BF16)], rows, NH)
    fkn, = ew_fwd("fox_kn", f_rms, [rowh(proj, FOX_K), par(kn_g)], [o_rowh(BF16)], rows, NH)
    o_b, o_b16, lse = fox_fwd(fqn, fkn, proj, ct, nseq, seq)
    p_a, p_b, w_o, w_u, w_d = hooks.late_weights(o_a)
    y_a = matmul("mm_pa", ya_in, p_a, "nn", F32, tn=1024)
    y_b = matmul("mm_pb", o_b16, p_b, "nn", F32, tn=1024)
    gates = [row(proj, 7, dm), row(proj, 8, dm)]
    merged, = ew_fwd("merge", f_merge, gates + [row(y_a), row(y_b)], [o_row(dm, BF16)], rows)
    hres = matmul("mm_out", merged, w_o, "nn", F32, add=x2, tn=1024)
    hn, = ew_fwd("rms2", f_rms, [row(hres), par(g2)], [o_row(dm, BF16)], rows)
    up_blocks = w_u.shape[0]
    act, relu2 = matmul("mm_up", hn, w_u, "nn", F32, col_blocks=up_blocks, out_dtypes=[F32, BF16],
                        epilogue=lambda r: [r, jnp.maximum(r, 0.0) * jnp.maximum(r, 0.0)])
    def loss_tail(r, h_tile, t_tile):
        d = (r + h_tile) - t_tile
        e = (0.5 / dm) * (d * d)
        part = e.reshape(e.shape[0] // 8, 8, e.shape[1]).sum(axis=0)
        part = sum(part[:, t * LANES:(t + 1) * LANES] for t in range(e.shape[1] // LANES))
        g = d * (1.0 / dm)
        return [g, g, part]

    dout, dout16, loss_acc = matmul("mm_down", relu2, w_d, "nn", F32, extras=[hres, tgt2], epilogue=loss_tail,
                                    out_dtypes=[F32, BF16, F32], tile_sums=True)

    d_act = matmul("mm_d_act", dout16, w_d, "nt", BF16, extras=[act], epilogue=lambda r, a: [2.0 * jnp.maximum(a, 0.0) * r])
    dw_d = matmul("mm_dw_down", relu2, dout16, "tn", F32, tn=1024)
    dw_u = matmul("mm_dw_up", hn, d_act, "tn", F32, col_blocks=up_blocks)
    d_hn = matmul("mm_d_hn", d_act, w_u, "nt", F32, col_blocks=up_blocks)
    dh, dh16, dg2 = ew_bwd("rms2_b", f_rms, [row(hres), par(g2)], [(row(d_hn),)], [row(dout)],
                           lambda g, e: [g[0] + e[0], g[0] + e[0], g[1]],
                           [((rows, dm), "row", dm, F32, None), ((rows, dm), "row", dm, BF16, None), ((1, dm), "par", dm, F32, "all")], rows)
    d_merged = matmul("mm_d_merged", dh16, w_o, "nt", F32, tn=1024)
    dw_o = matmul("mm_dw_out", merged, dh16, "tn", F32, tn=1024)
    seg16 = ((rows, dm), "row", dm, BF16, None)
    d_ga16, d_gb16, d_ya16, d_yb16 = ew_bwd("merge_b", f_merge, gates + [row(y_a), row(y_b)], [(row(d_merged),)], [],
                                            lambda g, e: list(g), [seg16] * 4, rows)
    dp_a = matmul("mm_dp_a", ya_in, d_ya16, "tn", F32, tn=1024)
    d_ya_in = matmul("mm_d_ya_in", d_ya16, p_a, "nt", F32, tn=1024)
    dp_b = matmul("mm_dp_b", o_b16, d_yb16, "tn", F32, tn=1024)
    d_ob = matmul("mm_d_ob", d_yb16, p_b, "nt", F32, tn=1024)
    token = hooks.reduce_start(dict(p_a=dp_a, p_b=dp_b, w_o=dw_o, w_u=dw_u, w_d=dw_d))
    gdn_ng_t = gdn_ng + token[0, 0]
    h32 = ((rows, wide), "rowh", LANES, F32, None)
    h16 = ((rows, wide), "rowh", LANES, BF16, None)
    gain = ((1, LANES), "par", LANES, F32, "all")
    d_oa, d_z16, d_gdn_ng = ew_bwd("gdn_post_b", f_post, [rowh(o_a), rowh(proj, 3 * NH), par(gdn_ng_t)], [(rowh(d_ya_in),)], [],
                                   lambda g, e: list(g), [h32, h16, gain], rows, NH)
    dval, dkc, dat, dqd, dkd, dgb_b = gdn_b_bwd(val, kcum, attn, qdec, kdec, gb, snaps, d_oa, nseq, seq)
    d_cq, d_ck, d_cv, d_gb, d_bb = gdn_a_bwd(conv["q"], conv["k"], conv["v"], gb, bb, t_inv, dval, dkc, dat, dqd, dkd, dgb_b, rows)
    token = hooks.reduce_exchange(d_cq)
    conv_w_t = conv_w + token[0, 0]
    d_pre, d_conv = {}, {}
    tap = ((4, wide), "parh", LANES, F32, "inner")
    for mode, off, ctg in (("q", 0, d_cq), ("k", NH, d_ck), ("v", 2 * NH, d_cv)):
        d_pre[mode], d_conv[mode] = ew_bwd(f"conv_{mode}_b", make_f_conv(mode), [rowh(proj, off), parh(conv_w_t, off)],
                                           [(rowh(ctg),)], [], lambda g, e: list(g), [h16, tap], rows, NH, seq, "hi", CONV_HEADS)
    delta, = ew_fwd("fox_delta", f_delta, [rowh(d_ob), rowh(o_b)], [o_rowh(F32)], rows, NH, after=[token])
    d_fqn, d_cq_b = fox_dq(fqn, fkn, proj, ct, d_ob, lse, delta, nseq, seq)
    d_fkn, d_fv16, d_ck_b = fox_dkv(fqn, fkn, proj, cb, d_ob, lse, delta, nseq, seq)
    token = hooks.reduce_finish(d_fkn)
    qn_g_t, kn_g_t = qn_g + token[0, 0], kn_g + token[0, 0]
    d_fq16, d_qn_g = ew_bwd("fox_qn_b", f_rms, [rowh(proj, FOX_Q), par(qn_g_t)], [(rowh(d_fqn),)], [], lambda g, e: list(g),
                            [h16, gain], rows, NH)
    d_fk16, d_kn_g = ew_bwd("fox_kn_b", f_rms, [rowh(proj, FOX_K), par(kn_g_t)], [(rowh(d_fkn),)], [], lambda g, e: list(g),
                            [h16, gain], rows, NH)
    narrow = ((rows, LANES), "row", LANES, F32, None)
    d_so, d_cs = ew_bwd("bcast_b", f_bcast, [row(so), row(cs)], [(rowh(d_gb),), (rowh(d_bb),), (rowh(d_cq_b), rowh(d_ck_b))], [],
                        lambda g, e: list(g), [narrow, narrow], rows, NH)
    d_logf = cumsum_time("cumsum_b", d_cs, nseq, seq, True)
    vec = ((1, LANES), "par", LANES, F32, "all")
    d_sp16, d_p1, d_p2 = ew_bwd("small_b", f_small, [row(sp), par(p1), par(p2)], [(row(d_so), row(d_logf))], [],
                                lambda g, e: list(g), [((rows, LANES), "row", LANES, BF16, None), vec, vec], rows)
    d_proj16 = jnp.concatenate([d_pre["q"], d_pre["k"], d_pre["v"], d_z16, d_fq16, d_fk16, d_fv16, d_ga16, d_gb16], axis=1)
    dw_main = matmul("mm_dw_main", d_proj16, u, "tn", F32)
    dw_small = matmul("mm_dw_small", d_sp16, u, "tn", F32)
    wt_small_t = wt_small + hooks.input_grad_start(dw_main, dw_small)[0, 0].astype(BF16)
    d_u = matmul("mm_d_u_small", d_sp16, wt_small_t, "nn", F32)
    d_u = matmul("mm_d_u_first", d_proj16, wt_main, "nn", F32, add=d_u, k_part=(0, 2))
    d_u = matmul("mm_d_u_second", d_proj16, wt_main, "nn", F32, add=d_u, k_part=(1, 2), after=[hooks.input_grad_exchange(d_u)])
    dx, dg1 = ew_bwd("rms1_b", f_rms, [row(x2), par(g1)], [(row(d_u),)], [row(dh)], lambda g, e: [g[0] + e[0], g[1]],
                     [((rows, dm), "row", dm, F32, None), ((1, dm), "par", dm, F32, "all")], rows)
    d_conv_w = jnp.concatenate([d_conv["q"], d_conv["k"], d_conv["v"]], axis=1)
    return dict(loss_acc=loss_acc, dx=dx, g1=dg1, g2=dg2, gdn_ng=d_gdn_ng, qn=d_qn_g, kn=d_kn_g, p1=d_p1, p2=d_p2,
                conv=d_conv_w, w_main=dw_main, w_small=dw_small, p_a=dp_a, p_b=dp_b, w_o=dw_o, w_u=dw_u, w_d=dw_d)


_W = NH * LANES
_A0, _A1 = 4 * _W, 4 * _W + 2 * NH
_B0, _B1 = _A1 + 3 * _W, _A1 + 3 * _W + NH
N_IN = _B1 + 2 * _W


def _split_w_in(full_t):
    main = jnp.concatenate([full_t[:_A0], full_t[_A1:_B0], full_t[_B1:]], axis=0)
    small = jnp.concatenate([full_t[_A0:_A1], full_t[_B0:_B1], jnp.zeros((LANES - 3 * NH, full_t.shape[1]), full_t.dtype)], axis=0)
    return main, small


def _join_w_in(main, small):
    return jnp.concatenate([main[:_A0], small[:2 * NH], main[_A0:_A0 + 3 * _W], small[2 * NH:3 * NH], main[_A0 + 3 * _W:]], axis=0)


def _lanes(v, at=0):
    return jnp.pad(v.reshape(1, -1), ((0, 0), (at, LANES - at - v.size)))


def kernel(x, norm_mix_g, w_in, gdn_conv_w, gdn_a_log, gdn_dt_bias, gdn_norm_g, fox_q_norm_g, fox_k_norm_g, fox_f_bias, w_proj_gdn, w_proj_fox, w_out, norm_mlp_g, w_up, w_down, loss_target, m_norm_mix_g, m_w_in, m_gdn_conv_w, m_gdn_a_log, m_gdn_dt_bias, m_gdn_norm_g, m_fox_q_norm_g, m_fox_k_norm_g, m_fox_f_bias, m_w_proj_gdn, m_w_proj_fox, m_w_out, m_norm_mlp_g, m_w_up, m_w_down, v_norm_mix_g, v_w_in, v_gdn_conv_w, v_gdn_a_log, v_gdn_dt_bias, v_gdn_norm_g, v_fox_q_norm_g, v_fox_k_norm_g, v_fox_f_bias, v_w_proj_gdn, v_w_proj_fox, v_w_out, v_norm_mlp_g, v_w_up, v_w_down):
    nseq, seq, dm = x.shape
    rows = nseq * seq
    xi, yi, ci = lax.axis_index("x"), lax.axis_index("y"), lax.axis_index("c")
    chip = 2 * xi + yi
    conv_cols = gdn_conv_w.shape[2]

    tr = lambda a: jnp.swapaxes(a[0], 0, 1)
    big = [tr(w_in), w_proj_gdn[0], w_proj_fox[0], w_out[0], w_up[0], w_down[0]]
    axes = [1, 0, 0, 0, 0, 0]
    big16 = [w.astype(BF16) for w in big]
    conv_slot = jnp.zeros((4, 4, conv_cols), F32).at[:, chip].set(jnp.where(ci == 0, gdn_conv_w[0], 0.0))
    conv_full = all_reduce_small("gather_conv", conv_slot.reshape(-1, LANES)).reshape(4, 4 * conv_cols)
    got_in, = gather_weights(big16[:1], axes[:1])
    wt_main, wt_small = _split_w_in(got_in.reshape(-1, dm))
    core, chip_no = ci.reshape(1).astype(jnp.int32), chip.reshape(1).astype(jnp.int32)
    gather = split_gather(big16[1:])
    token = gather.start([got_in, conv_full])

    class Hooks:
        def late_weights(self, after):
            g_pa, g_pb, g_wo, w_u, g_wd = gather.wait(after)
            return (*(g.reshape(-1, dm) for g in (g_pa, g_pb, g_wo)), w_u, g_wd.reshape(-1, dm))

        def reduce_start(self, grads):
            blocks = [grads["p_a"].reshape(4, -1, dm), grads["p_b"].reshape(4, -1, dm), grads["w_o"].reshape(4, -1, dm),
                      grads["w_u"], grads["w_d"].reshape(4, -1, dm)]
            self.swap = split_pair_swap("pair_swap_late", blocks, axes[1:])
            return self.swap.start([])

        def reduce_exchange(self, after):
            swapped = self.swap.wait(after)
            self.exchange = split_chip_exchange("chip_exchange_late", add_pair("add_pair_late", self.swap.srcs, swapped, core, axes[1:]))
            return self.exchange.start([])

        def reduce_finish(self, after):
            slots = self.exchange.wait(after)
            self.send = split_pair_send(add_chips("add_chips_late", slots, self.exchange.srcs, chip_no, axes[1:]))
            return self.send.start([])

        def input_grad_start(self, dw_main, dw_small):
            self.in_swap = split_pair_swap("pair_swap_in", [_join_w_in(dw_main, dw_small).reshape(4, -1, dm)], axes[:1])
            return self.in_swap.start([])

        def input_grad_exchange(self, after):
            swapped = self.in_swap.wait(after)
            self.in_exchange = split_chip_exchange("chip_exchange_in", add_pair("add_pair_in", self.in_swap.srcs, swapped, core, axes[:1]))
            return self.in_exchange.start([])

    hooks = Hooks()
    p1 = _lanes(gdn_dt_bias[0]) + _lanes(fox_f_bias[0], 2 * NH)
    p2 = _lanes(gdn_a_log[0])

    g = local_step(x.reshape(rows, dm), loss_target.reshape(rows, dm), norm_mix_g + token[0, 0], norm_mlp_g, gdn_norm_g,
                   fox_q_norm_g, fox_k_norm_g, p1, p2, conv_full, wt_main, wt_small, hooks, nseq, seq)

    others = hooks.send.wait(g["dx"])
    big_m = [tr(m_w_in), m_w_proj_gdn[0], m_w_proj_fox[0], m_w_out[0], m_w_up[0], m_w_down[0]]
    big_v = [tr(v_w_in), v_w_proj_gdn[0], v_w_proj_fox[0], v_w_out[0], v_w_up[0], v_w_down[0]]
    names = ["w_in", "w_proj_gdn", "w_proj_fox", "w_out", "w_up", "w_down"]
    big_res, big_grad = {}, {}
    for i in range(1, len(names)):
        big_grad[names[i]], *big_res[names[i]] = adamw_halves(f"adamw_{names[i]}", big[i], hooks.send.srcs[i - 1], others[i - 1],
                                                              big_m[i], big_v[i], core, axes[i])
    slots = hooks.in_exchange.wait(big_res[names[-1]][0])
    mine = add_chips("add_chips_in", slots, hooks.in_exchange.srcs, chip_no, axes[:1])
    res = adamw_halves("adamw_w_in", big[0], mine[0], pair_send(mine)[0], big_m[0], big_v[0], core, axes[0])
    big_grad["w_in"], *big_res["w_in"] = [jnp.swapaxes(r, 0, 1) for r in res]

    small_parts = [g["loss_acc"], g["g1"].reshape(8, LANES), g["g2"].reshape(8, LANES), g["gdn_ng"], g["qn"], g["kn"], g["p1"], g["p2"],
                   g["conv"].reshape(-1, LANES)]
    tiled = [jnp.pad(p, ((0, -p.shape[0] % 8), (0, 0))) for p in small_parts]
    red = all_reduce_small("reduce_small", jnp.concatenate(tiled, axis=0), slots)
    pos, red_parts = 0, []
    for p, t in zip(small_parts, tiled):
        red_parts.append(red[pos:pos + p.shape[0]])
        pos += t.shape[0]
    r_loss, r_g1, r_g2, r_gdn_ng, r_qn, r_kn, r_p1, r_p2, r_conv = red_parts
    loss = jnp.sum(r_loss)
    g_conv = lax.dynamic_slice_in_dim(r_conv.reshape(4, 4, conv_cols), chip, 1, axis=1).reshape(4, conv_cols)
    small_grads = [r_g1.reshape(1, dm), r_p2[:, :NH], r_p1[:, :NH], r_gdn_ng, r_qn, r_kn, r_p1[:, 2 * NH:3 * NH], r_g2.reshape(1, dm)]
    small_w = [norm_mix_g, gdn_a_log, gdn_dt_bias, gdn_norm_g, fox_q_norm_g, fox_k_norm_g, fox_f_bias, norm_mlp_g]
    small_m = [m_norm_mix_g, m_gdn_a_log, m_gdn_dt_bias, m_gdn_norm_g, m_fox_q_norm_g, m_fox_k_norm_g, m_fox_f_bias, m_norm_mlp_g]
    small_v = [v_norm_mix_g, v_gdn_a_log, v_gdn_dt_bias, v_gdn_norm_g, v_fox_q_norm_g, v_fox_k_norm_g, v_fox_f_bias, v_norm_mlp_g]

    def pack(parts):
        flat = jnp.concatenate([jnp.pad(p.reshape(-1), (0, -p.size % LANES)) for p in parts])
        return jnp.pad(flat, (0, -flat.size % (8 * LANES))).reshape(-1, LANES)

    packed = adamw("adamw_small", pack(small_w + [gdn_conv_w[0]]), pack(small_grads + [g_conv]),
                   pack(small_m + [m_gdn_conv_w[0]]), pack(small_v + [v_gdn_conv_w[0]]))

    def unpack(flat2d):
        flat, pos, res = flat2d.reshape(-1), 0, []
        for p in small_w + [gdn_conv_w[0]]:
            res.append(flat[pos:pos + p.size].reshape(p.shape))
            pos += p.size + (-p.size % LANES)
        return res

    s_delta, s_m, s_v = (unpack(a) for a in packed)

    order = ["norm_mix_g", "w_in", "gdn_conv_w", "gdn_a_log", "gdn_dt_bias", "gdn_norm_g", "fox_q_norm_g", "fox_k_norm_g",
             "fox_f_bias", "w_proj_gdn", "w_proj_fox", "w_out", "norm_mlp_g", "w_up", "w_down"]
    small_names = ["norm_mix_g", "gdn_a_log", "gdn_dt_bias", "gdn_norm_g", "fox_q_norm_g", "fox_k_norm_g", "fox_f_bias", "norm_mlp_g",
                   "gdn_conv_w"]
    small_idx = {nm: i for i, nm in enumerate(small_names)}
    shapes = dict(zip(order, (a.shape for a in (norm_mix_g, w_in, gdn_conv_w, gdn_a_log, gdn_dt_bias, gdn_norm_g, fox_q_norm_g,
                                                 fox_k_norm_g, fox_f_bias, w_proj_gdn, w_proj_fox, w_out, norm_mlp_g, w_up, w_down))))
    grads_out, delta_out, m_out, v_out = [], [], [], []
    for nm in order:
        if nm in big_res:
            d, mm, vv = big_res[nm]
            gr = big_grad[nm]
        else:
            i = small_idx[nm]
            gr = (small_grads + [g_conv])[i]
            d, mm, vv = s_delta[i], s_m[i], s_v[i]
        for lst, val in ((grads_out, gr), (delta_out, d), (m_out, mm), (v_out, vv)):
            lst.append(val.reshape(shapes[nm]))
    return (loss, g["dx"].reshape(x.shape), *grads_out, *delta_out, *m_out, *v_out)
```

```python
import functools

import jax
import jax.numpy as jnp
from jax import lax
from jax.experimental import pallas as pl
from jax.experimental.pallas import tpu as pltpu

F32 = jnp.float32
BF16 = jnp.bfloat16
LANES = 128
NH = 8
EPS = 1e-6
GDN_CHUNK = 64
GDN_ROWS = 256
GDN_BASE = 16
ROW_TILE = 512
CONV_HEADS = 2
ATT_TILE = 512
NEG = -1e30
VMEM_LIMIT_BYTES = 58 * 1024 * 1024
HI = lax.Precision.HIGHEST
LO = lax.Precision.DEFAULT
MESH = pl.DeviceIdType.MESH
ANY = pl.BlockSpec(memory_space=pl.ANY)

ADAM_LR, ADAM_B1, ADAM_B2, ADAM_EPS, ADAM_WD, ADAM_STEP = 0.001, 0.9, 0.999, 1e-08, 0.01, 10


def _params(n_grid):
    return pltpu.CompilerParams(dimension_semantics=("arbitrary",) * n_grid,
                                vmem_limit_bytes=VMEM_LIMIT_BYTES)


def _dot(a, b, dims, precision=None):
    dn = {"nn": (((1,), (0,)), ((), ())), "nt": (((1,), (1,)), ((), ())), "tn": (((0,), (0,)), ((), ()))}[dims]
    return lax.dot_general(a, b, dn, precision=precision, preferred_element_type=F32)


def _iota(shape, dim):
    return lax.broadcasted_iota(jnp.int32, shape, dim)


def _split(x, parts):
    out = []
    for _ in range(parts - 1):
        hi = x.astype(BF16)
        out.append(hi)
        x = x - hi.astype(F32)
    return out + [x.astype(BF16)]


def _dot_mask(mask, b, dims):
    m16 = mask.astype(BF16)
    b1, b2, b3 = _split(b, 3)
    return _dot(m16, b1, dims) + (_dot(m16, b2, dims) + _dot(m16, b3, dims))


@jax.custom_vjp
def mm_mask(mask, b):
    return _dot_mask(mask, b, "nn")


mm_mask.defvjp(lambda mask, b: (_dot_mask(mask, b, "nn"), mask),
               lambda mask, g: (jnp.zeros_like(mask), _dot_mask(mask, g, "tn")))


def matmul(name, a, b, dims, out_dtype, add=None, tm=1024, tn=1024, tk=512, col_blocks=None,
           extras=(), epilogue=None, out_dtypes=None, k_part=None, after=(), tile_sums=False):
    if col_blocks and dims != "tn":
        nb, b_rows, bw = b.shape
        b_shape = (b_rows, nb * bw)
    else:
        b_shape = b.shape
    if dims == "nn":
        (m, k), (_, n) = a.shape, b_shape
    elif dims == "nt":
        (m, k), (n, _) = a.shape, b_shape
    else:
        (k, m), (_, n) = a.shape, b_shape
    k_span = k // (k_part[1] if k_part else 1)
    if col_blocks and dims == "nt":
        k_span = min(k_span, bw)
    tk = k if k <= 1024 else max(t for t in (2048, 1536, 1024, 512, tk) if k_span % t == 0)
    tm, tn, tk = min(tm, m), min(tn, n), min(tk, k)
    assert m % tm == 0 and n % tn == 0 and k % tk == 0, (name, m, n, k)
    k0, nk = (0, k // tk) if k_part is None else (k_part[0] * (k // tk // k_part[1]), k // tk // k_part[1])
    assert k_part is None or (dims == "nn" and not col_blocks and (k // tk) % k_part[1] == 0)
    a_spec = pl.BlockSpec((tk, tm), lambda i, j, kk: (kk, i)) if dims == "tn" else pl.BlockSpec((tm, tk), lambda i, j, kk: (i, kk + k0))
    b_spec = pl.BlockSpec((tn, tk), lambda i, j, kk: (j, kk)) if dims == "nt" else pl.BlockSpec((tk, tn), lambda i, j, kk: (kk + k0, j))
    o_spec = pl.BlockSpec((tm, tn), lambda i, j, kk: (i, j))
    out_shape = (m, n)
    if col_blocks and dims == "nn":
        per = bw // tn
        assert bw % tn == 0
        b_spec = pl.BlockSpec((None, tk, tn), lambda i, j, kk: (j // per, kk, j % per))
    elif col_blocks and dims == "nt":
        per = bw // tk
        assert bw % tk == 0
        b_spec = pl.BlockSpec((None, tn, tk), lambda i, j, kk: (kk // per, j, kk % per))
    elif col_blocks:
        bw = n // col_blocks
        per = bw // tn
        assert bw % tn == 0 and add is None
        o_spec = pl.BlockSpec((None, tm, tn), lambda i, j, kk: (j // per, i, j % per))
        out_shape = (col_blocks, m, bw)
    extras = list(extras) + ([add] if add is not None else [])
    if add is not None:
        assert epilogue is None
        epilogue = lambda r, *e: [r + e[-1]]
    out_dtypes = [out_dtype] if epilogue is None or out_dtypes is None else list(out_dtypes)
    n_ex, n_out = len(extras), len(out_dtypes)

    def body(*refs):
        a_ref, b_ref = refs[0], refs[1]
        ex_refs, o_refs = refs[2:2 + n_ex], refs[2 + n_ex + len(after):2 + n_ex + len(after) + n_out]

        def finish(r):
            res = [r] if epilogue is None else epilogue(r, *[e[...] for e in ex_refs])
            for o_ref, v in zip(o_refs, res):
                o_ref[...] = v.astype(o_ref.dtype)

        if nk == 1:
            finish(_dot(a_ref[...], b_ref[...], dims))
            return
        acc_ref = refs[-1]
        kk = pl.program_id(2)

        @pl.when(kk == 0)
        def _():
            acc_ref[...] = jnp.zeros_like(acc_ref)

        acc_ref[...] += _dot(a_ref[...], b_ref[...], dims)

        @pl.when(kk == nk - 1)
        def _():
            finish(acc_ref[...])

    out_specs = [o_spec] * n_out
    out_shapes = [jax.ShapeDtypeStruct(out_shape, dt) for dt in out_dtypes]
    if tile_sums:
        out_specs[-1] = pl.BlockSpec((8, LANES), lambda i, j, kk: (i, j))
        out_shapes[-1] = jax.ShapeDtypeStruct((8 * (m // tm), LANES * (n // tn)), out_dtypes[-1])
    res = pl.pallas_call(
        body, name=name, grid=(m // tm, n // tn, nk), in_specs=[a_spec, b_spec] + [o_spec] * n_ex + [ANY] * len(after),
        out_specs=out_specs, out_shape=out_shapes,
        scratch_shapes=[pltpu.VMEM((tm, tn), F32)] if nk > 1 else [], compiler_params=_params(3),
    )(a, b, *extras, *after)
    return res[0] if n_out == 1 else res


def _ew_spec(kind, off, width, tb, hp, order, shape=None):
    def ih(g0, g1):
        return (g0, g1) if order == "ih" else (g1, g0)

    assert off % hp == 0 or kind in ("row", "par")
    if kind == "row":
        return pl.BlockSpec((tb, width), lambda g0, g1: (ih(g0, g1)[0], off))
    if kind == "rowh":
        return pl.BlockSpec((tb, hp * width), lambda g0, g1: (ih(g0, g1)[0], ih(g0, g1)[1] + off // hp))
    if kind == "par":
        return pl.BlockSpec(shape, lambda g0, g1: (0, 0))
    if kind == "parh":
        return pl.BlockSpec((shape[0], hp * width), lambda g0, g1: (0, ih(g0, g1)[1] + off // hp))
    raise ValueError(kind)


def _ew_grid(rows, tb, nh, hp, order):
    assert nh % hp == 0 and rows % tb == 0
    return (rows // tb, nh // hp) if order == "ih" else (nh // hp, rows // tb)


def _ew_load(ref, kind, width, hh):
    if kind in ("row", "par"):
        return ref[...].astype(F32)
    return ref[:, hh * width:(hh + 1) * width].astype(F32)


def ew_fwd(name, f, ins, outs, rows, nh=1, tb=ROW_TILE, order="ih", hp=None, after=()):
    hp = nh if hp is None else hp
    n_in = len(ins)

    def body(*refs):
        hb = pl.program_id(1) if order == "ih" else pl.program_id(0)
        for hh in range(hp):
            h = hh if hp == nh else hb * hp + hh
            vals = [_ew_load(r, kd, w, hh) for r, (_, kd, _, w) in zip(refs[:n_in], ins)]
            res = f(h, *vals)
            for r, v, (_, kd, w, _) in zip(refs[n_in + len(after):], res, outs):
                if kd == "row":
                    assert hp == 1
                    r[...] = v.astype(r.dtype)
                else:
                    r[:, hh * w:(hh + 1) * w] = v.astype(r.dtype)

    in_specs = [_ew_spec(kd, off, w, tb, hp, order, a.shape) for (a, kd, off, w) in ins]
    out_specs = [_ew_spec(kd, 0, w, tb, hp, order) for (_, kd, w, _) in outs]
    out_shape = [jax.ShapeDtypeStruct((rows, tw), dt) for (tw, _, _, dt) in outs]
    return pl.pallas_call(
        body, name=name, grid=_ew_grid(rows, tb, nh, hp, order), in_specs=in_specs + [ANY] * len(after), out_specs=out_specs,
        out_shape=out_shape, compiler_params=_params(2),
    )(*[a for (a, _, _, _) in ins], *after)


def ew_bwd(name, f, ins, cts, extras, emit, outs, rows, nh=1, tb=ROW_TILE, order="ih", hp=None):
    hp = nh if hp is None else hp
    n_in = len(ins)
    flat_cts = [d for group in cts for d in group]
    n_ct, n_ex = len(flat_cts), len(extras)

    def body(*refs):
        g0, g1 = pl.program_id(0), pl.program_id(1)
        hb = g1 if order == "ih" else g0
        out_refs = refs[n_in + n_ct + n_ex:]
        shared = [None] * len(outs)

        def store(r, v, first, sl=None):
            def put(val, add):
                if sl is None:
                    r[...] = (r[...] + val if add else val).astype(r.dtype)
                else:
                    r[:, sl] = (r[:, sl] + val if add else val).astype(r.dtype)

            if first is None:
                put(v, False)
            else:
                pl.when(first)(lambda: put(v, False))
                pl.when(jnp.logical_not(first))(lambda: put(v, True))

        for hh in range(hp):
            h = hh if hp == nh else hb * hp + hh
            vals = [_ew_load(r, kd, w, hh) for r, (_, kd, _, w) in zip(refs[:n_in], ins)]
            ct_refs = list(zip(refs[n_in:n_in + n_ct], flat_cts))
            ct_vals, pos = [], 0
            for group in cts:
                v = None
                for r, (_, kd, _, w) in ct_refs[pos:pos + len(group)]:
                    t = _ew_load(r, kd, w, hh)
                    v = t if v is None else v + t
                pos += len(group)
                ct_vals.append(v)
            ex_vals = [_ew_load(r, kd, w, hh) for r, (_, kd, _, w) in zip(refs[n_in + n_ct:n_in + n_ct + n_ex], extras)]
            _, vjp = jax.vjp(lambda *a: f(h, *a), *vals)
            res = emit(vjp(tuple(ct_vals)), ex_vals)
            for idx, (r, v, (_, kd, w, _, acc)) in enumerate(zip(out_refs, res, outs)):
                if kd in ("row", "par"):
                    shared[idx] = v if shared[idx] is None else shared[idx] + v
                else:
                    store(r, v, (g1 == 0) if acc == "inner" else None, slice(hh * w, (hh + 1) * w))
        for idx, (r, (_, kd, _, _, acc)) in enumerate(zip(out_refs, outs)):
            if kd in ("row", "par"):
                assert acc == "all" or hp == nh
                store(r, shared[idx], jnp.logical_and(g0 == 0, g1 == 0) if acc == "all" else None)

    operands = list(ins) + flat_cts + list(extras)
    in_specs = [_ew_spec(kd, off, w, tb, hp, order, a.shape) for (a, kd, off, w) in operands]
    out_specs = [_ew_spec(kd, 0, w, tb, hp, order, shp) for (shp, kd, w, _, _) in outs]
    out_shape = [jax.ShapeDtypeStruct(shp, dt) for (shp, _, _, dt, _) in outs]
    return pl.pallas_call(
        body, name=name, grid=_ew_grid(rows, tb, nh, hp, order), in_specs=in_specs, out_specs=out_specs,
        out_shape=out_shape, compiler_params=_params(2),
    )(*[a for (a, _, _, _) in operands])


def f_rms(h, x, g):
    r = lax.rsqrt(jnp.mean(x * x, axis=-1, keepdims=True) + EPS)
    return (x * r * g,)


def _softplus(z):
    return jnp.maximum(z, 0.0) + jnp.log1p(jnp.exp(-jnp.abs(z)))


def f_small(h, sp, p1, p2):
    lane = _iota(sp.shape, 1)
    z = sp + p1
    g = -jnp.exp(p2) * _softplus(z)
    beta = jax.nn.sigmoid(z)
    logf = -_softplus(-z)
    return (jnp.where(lane < NH, g, jnp.where(lane < 2 * NH, beta, jnp.where(lane < 3 * NH, logf, 0.0))),)


def _pick(x, lane_id):
    lane = _iota(x.shape, 1)
    col = jnp.sum(jnp.where(lane == lane_id, x, 0.0), axis=1, keepdims=True)
    return jnp.broadcast_to(col, x.shape)


def f_bcast(h, so, cs):
    return _pick(so, h), _pick(so, h + NH), _pick(cs, h + 2 * NH)


def _shift_down(s):
    def down(x):
        r = pltpu.roll(x, s, 0)
        head = jnp.where(_iota((8, x.shape[1]), 0) >= s, r[:8], 0.0)
        return jnp.concatenate([head, r[8:]], axis=0)

    def up(g):
        n = g.shape[0]
        r = pltpu.roll(g, n - s, 0)
        tail = jnp.where(_iota((8, g.shape[1]), 0) < 8 - s, r[n - 8:], 0.0)
        return jnp.concatenate([r[:n - 8], tail], axis=0)

    @jax.custom_vjp
    def shift(x):
        return down(x)

    shift.defvjp(lambda x: (down(x), None), lambda _, g: (up(g),))
    return shift


def _silu(x):
    return x * jax.nn.sigmoid(x)


def make_f_conv(mode):
    sh1, sh2, sh3 = _shift_down(1), _shift_down(2), _shift_down(3)

    def f(h, x, w):
        sub = _iota(w.shape, 0)

        def tap(i):
            return jnp.sum(jnp.where(sub == i, w, 0.0), axis=0, keepdims=True)

        y = sh3(x) * tap(0)
        y = y + sh2(x) * tap(1)
        y = y + sh1(x) * tap(2)
        y = y + x * tap(3)
        s = _silu(y)
        if mode == "v":
            return (s,)
        n = s * lax.rsqrt(jnp.sum(s * s, axis=-1, keepdims=True) + EPS)
        if mode == "q":
            n = n * (LANES ** -0.5)
        return (n,)

    return f


def f_post(h, o, z, g):
    r = lax.rsqrt(jnp.mean(o * o, axis=-1, keepdims=True) + EPS)
    return (o * r * g * _silu(z),)


def f_merge(h, ga, gb, ya, yb):
    return (jax.nn.sigmoid(ga) * ya + jax.nn.sigmoid(gb) * yb,)


def f_delta(h, do, o):
    return (jnp.broadcast_to(jnp.sum(do * o, axis=1, keepdims=True), o.shape),)


def cumsum_time(name, x, nseq, seq, reverse):
    nb = seq // LANES

    def body(x_ref, o_ref):
        r, c = _iota((LANES, LANES), 0), _iota((LANES, LANES), 1)
        tri = jnp.where((r <= c) if reverse else (r >= c), 1.0, 0.0).astype(F32)
        carry = jnp.zeros((1, LANES), F32)
        for b in (range(nb - 1, -1, -1) if reverse else range(nb)):
            blk = x_ref[b * LANES:(b + 1) * LANES, :]
            o_ref[b * LANES:(b + 1) * LANES, :] = _dot_mask(tri, blk, "nn") + carry
            carry = carry + jnp.sum(blk, axis=0, keepdims=True)

    spec = pl.BlockSpec((seq, LANES), lambda s: (s, 0))
    return pl.pallas_call(body, name=name, grid=(nseq,), in_specs=[spec], out_specs=spec,
                          out_shape=jax.ShapeDtypeStruct(x.shape, F32), compiler_params=_params(1))(x)


def transpose_time(name, x, nseq, seq):
    def body(x_ref, o_ref):
        o_ref[...] = x_ref[...].T

    return pl.pallas_call(
        body, name=name, grid=(nseq,), in_specs=[pl.BlockSpec((seq, LANES), lambda s: (s, 0))],
        out_specs=pl.BlockSpec((LANES, seq), lambda s: (s, 0)),
        out_shape=jax.ShapeDtypeStruct((nseq * LANES, seq), F32), compiler_params=_params(1))(x)


def _gdn_masks():
    n = GDN_ROWS
    r, c = _iota((n, n), 0), _iota((n, n), 1)
    shift = GDN_CHUNK.bit_length() - 1
    same = lax.shift_right_logical(r, shift) == lax.shift_right_logical(c, shift)
    return r, c, same


def _each(fn, *lists):
    return [fn(*xs) for xs in zip(*lists)]


def _gdn_decay(gbs):
    r, c, same = _gdn_masks()
    seg_tril = jnp.where(jnp.logical_and(same, r >= c), 1.0, 0.0).astype(F32)
    g_cum = _each(lambda gb: mm_mask(seg_tril, gb), gbs)
    lane0 = _iota(gbs[0].shape, 1) == 0
    g_col = _each(lambda g: jnp.sum(jnp.where(lane0, g, 0.0), axis=1, keepdims=True), g_cum)
    g_row = _each(lambda g: jnp.sum(jnp.where(r == c, jnp.broadcast_to(g, (GDN_ROWS, GDN_ROWS)), 0.0), axis=0, keepdims=True), g_col)
    return g_cum, _each(lambda a, b: a - b, g_col, g_row)


def gdn_f1(*args):
    qs, ks, gbs, bbs = (list(args[i::4]) for i in range(4))
    r, c, same = _gdn_masks()
    strict = jnp.logical_and(same, r > c)
    _, diff = _gdn_decay(gbs)
    lane0 = _iota(bbs[0].shape, 1) == 0
    beta_col = _each(lambda bb: jnp.sum(jnp.where(lane0, bb, 0.0), axis=1, keepdims=True), bbs)
    kk = _each(lambda k: _dot(k, k, "nt", LO), ks)
    return tuple(_each(lambda b, x, d: jnp.where(strict, b * x * jnp.exp(jnp.where(strict, d, 0.0)), 0.0), beta_col, kk, diff))


def gdn_f2(*args):
    ts, qs, ks, vs, gbs, bbs = (list(args[i::6]) for i in range(6))
    r, c, same = _gdn_masks()
    incl = jnp.logical_and(same, r >= c)
    g_cum, diff = _gdn_decay(gbs)
    decay = _each(lambda d: jnp.where(incl, jnp.exp(jnp.where(incl, d, 0.0)), 0.0), diff)
    e_g = _each(jnp.exp, g_cum)
    v_beta = _each(lambda v, bb: v * bb, vs, bbs)
    k_beta = _each(lambda k, bb, e: k * bb * e, ks, bbs, e_g)
    value = _each(lambda t, x: x + _dot(t, x, "nn", LO), ts, v_beta)
    k_cum = _each(lambda t, x: x + _dot(t, x, "nn", LO), ts, k_beta)
    attn = _each(lambda q, k, d: _dot(q, k, "nt", LO) * d, qs, ks, decay)
    ones = jnp.where(same, 1.0, 0.0).astype(F32)
    g_last = _each(lambda gb: mm_mask(ones, gb), gbs)
    q_dec = _each(lambda q, e: q * e, qs, e_g)
    k_dec = _each(lambda k, gl, g: k * jnp.exp(gl - g), ks, g_last, g_cum)
    return tuple(x for head in zip(value, k_cum, attn, q_dec, k_dec) for x in head)


def tri_inverse(mats):
    n = GDN_ROWS
    r, c = _iota((n, n), 0), _iota((n, n), 1)
    shift = GDN_BASE.bit_length() - 1
    blk = lax.shift_right_logical(r, shift) == lax.shift_right_logical(c, shift)
    each = lambda fn, *lists: [fn(*xs) for xs in zip(*lists)]
    mm = lambda x, y: _dot(x, y, "nn", LO)
    d = each(lambda a: jnp.where(blk, a, 0.0), mats)
    lo = each(lambda a, dd: a - dd, mats, d)
    p = each(lambda dd: -dd, d)
    c_d = p
    for _ in range(shift - 1):
        p = each(mm, p, p)
        c_d = each(lambda cd, pp, prod: cd + pp + prod, c_d, p, each(mm, c_d, p))
    assert GDN_CHUNK // GDN_BASE == 4
    nmat = each(lambda l, prod: l + prod, lo, each(mm, c_d, lo))
    n2 = each(mm, nmat, nmat)
    c_n = each(lambda nn2, nm, prod: (nn2 - nm) - prod, n2, nmat, each(mm, nmat, n2))
    return each(lambda cn, cd, prod: cn + cd + prod, c_n, c_d, each(mm, c_n, c_d))


GDN_AHP = 4


def _gdn_a_specs():
    blk = pl.BlockSpec((GDN_ROWS, GDN_AHP * LANES), lambda i, h: (i, h))
    sq = pl.BlockSpec((GDN_ROWS, GDN_AHP * GDN_ROWS), lambda i, h: (i, h))
    return blk, sq


def _head(ref, hh):
    width = ref.shape[1] // GDN_AHP
    return ref.at[:, hh * width:(hh + 1) * width]


def gdn_a_fwd(q, k, v, gb, bb, rows):
    blk, sq = _gdn_a_specs()

    def body(q_ref, k_ref, v_ref, gb_ref, bb_ref, val_ref, kc_ref, at_ref, qd_ref, kd_ref, t_ref):
        heads = [[_head(r, hh)[...] for r in (q_ref, k_ref, v_ref, gb_ref, bb_ref)] for hh in range(GDN_AHP)]
        t_corr = tri_inverse(list(gdn_f1(*[x for qv, kv, vv, gv, bv in heads for x in (qv, kv, gv, bv)])))
        res = gdn_f2(*[x for t, head in zip(t_corr, heads) for x in (t, *head)])
        for hh in range(GDN_AHP):
            for r, x in zip((val_ref, kc_ref, at_ref, qd_ref, kd_ref, t_ref), (*res[5 * hh:5 * hh + 5], t_corr[hh])):
                _head(r, hh)[...] = x.astype(r.dtype)

    wide = lambda dt: jax.ShapeDtypeStruct((rows, NH * LANES), dt)
    square = jax.ShapeDtypeStruct((rows, NH * GDN_ROWS), BF16)
    return pl.pallas_call(
        body, name="gdn_a_fwd", grid=(rows // GDN_ROWS, NH // GDN_AHP), in_specs=[blk] * 5,
        out_specs=[blk, blk, sq, blk, blk, sq], out_shape=[wide(F32), wide(BF16), square, wide(BF16), wide(BF16), square],
        compiler_params=_params(2))(q, k, v, gb, bb)


def gdn_a_bwd(q, k, v, gb, bb, t_inv, dval, dkc, dat, dqd, dkd, dgb_b, rows):
    blk, sq = _gdn_a_specs()

    def body(q_ref, k_ref, v_ref, gb_ref, bb_ref, t_ref, dval_ref, dkc_ref, dat_ref, dqd_ref, dkd_ref, dgbb_ref,
             dq_ref, dk_ref, dv_ref, dgb_ref, dbb_ref):
        hs = range(GDN_AHP)
        heads = [[_head(r, hh)[...] for r in (q_ref, k_ref, v_ref, gb_ref, bb_ref)] for hh in hs]
        tvs = [_head(t_ref, hh)[...].astype(F32) for hh in hs]
        _, vjp1 = jax.vjp(gdn_f1, *[x for qv, kv, vv, gv, bv in heads for x in (qv, kv, gv, bv)])
        _, vjp2 = jax.vjp(gdn_f2, *[x for t, head in zip(tvs, heads) for x in (t, *head)])
        g2 = vjp2(tuple(_head(r, hh)[...] for hh in hs for r in (dval_ref, dkc_ref, dat_ref, dqd_ref, dkd_ref)))
        dts = [g2[6 * hh] for hh in hs]
        left = _each(lambda dt, tv: dt + _dot(tv, dt, "tn", LO), dts, tvs)
        g1 = vjp1(tuple(_each(lambda lf, tv: -(lf + _dot(lf, tv, "nt", LO)), left, tvs)))
        for hh in hs:
            dq1, dk1, dgb1, dbb1 = g1[4 * hh:4 * hh + 4]
            _, dq2, dk2, dv2, dgb2, dbb2 = g2[6 * hh:6 * hh + 6]
            _head(dq_ref, hh)[...] = dq1 + dq2
            _head(dk_ref, hh)[...] = dk1 + dk2
            _head(dv_ref, hh)[...] = dv2
            _head(dgb_ref, hh)[...] = dgb1 + dgb2 + _head(dgbb_ref, hh)[...]
            _head(dbb_ref, hh)[...] = dbb1 + dbb2

    wide = jax.ShapeDtypeStruct((rows, NH * LANES), F32)
    return pl.pallas_call(
        body, name="gdn_a_bwd", grid=(rows // GDN_ROWS, NH // GDN_AHP),
        in_specs=[blk] * 5 + [sq, blk, blk, sq, blk, blk, blk], out_specs=[blk] * 5, out_shape=[wide] * 5,
        compiler_params=_params(2))(q, k, v, gb, bb, t_inv, dval, dkc, dat, dqd, dkd, dgb_b)


N_CH = GDN_ROWS // GDN_CHUNK


GDN_HP = 8


def gdn_chunk(c):
    def f(*args):
        val, kc, at, qd, kd, gb, s = (list(args[i::7]) for i in range(7))
        zero = jnp.zeros((GDN_CHUNK, LANES), F32)
        v_new = _each(lambda v, k, st: v - _dot(k, st, "nn", LO), val, kc, s)
        v_pad = _each(lambda v: jnp.concatenate([zero] * c + [v] + [zero] * (N_CH - 1 - c), axis=0), v_new)
        out = _each(lambda q, st, a, vp: _dot(q, st, "nn", LO) + _dot(a, vp, "nn", LO), qd, s, at, v_pad)
        dec = _each(lambda g: jnp.exp(jnp.sum(g, axis=0, keepdims=True)), gb)
        s_new = _each(lambda st, d, k, v: st * d + _dot(k, v, "tn", LO), s, dec, kd, v_new)
        return tuple(x for head in zip(out, s_new) for x in head)

    return f


def _gdn_piece(ref, hh, c):
    width = ref.shape[1] // GDN_HP
    return ref.at[c * GDN_CHUNK:(c + 1) * GDN_CHUNK, hh * width:(hh + 1) * width]


def _gdn_snap(ref, hh, c):
    row = (hh * N_CH + c) * LANES
    return ref.at[row:row + LANES, :]


def _gdn_b_specs(nb, rev):
    def blk_row(s, j):
        return s * nb + (nb - 1 - j if rev else j)

    blk = pl.BlockSpec((GDN_ROWS, GDN_HP * LANES), lambda s, hb, j: (blk_row(s, j), hb))
    sq = pl.BlockSpec((GDN_ROWS, GDN_HP * GDN_ROWS), lambda s, hb, j: (blk_row(s, j), hb))
    snap = pl.BlockSpec((GDN_HP * N_CH * LANES, LANES), lambda s, hb, j: (blk_row(s, j) * (NH // GDN_HP) + hb, 0))
    return blk, sq, snap


def gdn_b_fwd(val, kc, at, qd, kd, gb, nseq, seq):
    nb = seq // GDN_ROWS
    rows = nseq * seq
    blk, sq, snap = _gdn_b_specs(nb, False)

    def body(val_ref, kc_ref, at_ref, qd_ref, kd_ref, gb_ref, o_ref, snap_ref, s_ref):
        @pl.when(pl.program_id(2) == 0)
        def _():
            s_ref[...] = jnp.zeros_like(s_ref)

        hs = range(GDN_HP)
        states = [s_ref[hh] for hh in hs]
        for c in range(N_CH):
            for hh in hs:
                _gdn_snap(snap_ref, hh, c)[...] = states[hh]
            res = gdn_chunk(c)(*[x for hh in hs for x in (
                *[_gdn_piece(r, hh, c)[...].astype(F32) for r in (val_ref, kc_ref, at_ref, qd_ref, kd_ref, gb_ref)], states[hh])])
            for hh in hs:
                _gdn_piece(o_ref, hh, c)[...] = res[2 * hh]
            states = [res[2 * hh + 1] for hh in hs]
        for hh in hs:
            s_ref[hh] = states[hh]

    return pl.pallas_call(
        body, name="gdn_b_fwd", grid=(nseq, NH // GDN_HP, nb), in_specs=[blk, blk, sq, blk, blk, blk], out_specs=[blk, snap],
        out_shape=[jax.ShapeDtypeStruct((rows, NH * LANES), F32),
                   jax.ShapeDtypeStruct((nseq * nb * NH * N_CH * LANES, LANES), F32)],
        scratch_shapes=[pltpu.VMEM((GDN_HP, LANES, LANES), F32)], compiler_params=_params(3))(val, kc, at, qd, kd, gb)


def gdn_b_bwd(val, kc, at, qd, kd, gb, snaps, do, nseq, seq):
    nb = seq // GDN_ROWS
    rows = nseq * seq
    blk, sq, snap = _gdn_b_specs(nb, True)

    def body(val_ref, kc_ref, at_ref, qd_ref, kd_ref, gb_ref, snap_ref, do_ref,
             dval_ref, dkc_ref, dat_ref, dqd_ref, dkd_ref, dgb_ref, ds_ref):
        @pl.when(pl.program_id(2) == 0)
        def _():
            ds_ref[...] = jnp.zeros_like(ds_ref)

        hs = range(GDN_HP)
        d_states = [ds_ref[hh] for hh in hs]
        for c in reversed(range(N_CH)):
            _, vjp = jax.vjp(gdn_chunk(c), *[x for hh in hs for x in (
                *[_gdn_piece(r, hh, c)[...].astype(F32) for r in (val_ref, kc_ref, at_ref, qd_ref, kd_ref, gb_ref)],
                _gdn_snap(snap_ref, hh, c)[...])])
            grads = vjp(tuple(x for hh in hs for x in (_gdn_piece(do_ref, hh, c)[...], d_states[hh])))
            for hh in hs:
                for i, r in enumerate([dval_ref, dkc_ref, dat_ref, dqd_ref, dkd_ref, dgb_ref]):
                    _gdn_piece(r, hh, c)[...] = grads[7 * hh + i]
            d_states = [grads[7 * hh + 6] for hh in hs]
        for hh in hs:
            ds_ref[hh] = d_states[hh]

    wide = jax.ShapeDtypeStruct((rows, NH * LANES), F32)
    square = jax.ShapeDtypeStruct((rows, NH * GDN_ROWS), F32)
    return pl.pallas_call(
        body, name="gdn_b_bwd", grid=(nseq, NH // GDN_HP, nb), in_specs=[blk, blk, sq, blk, blk, blk, snap, blk],
        out_specs=[blk, blk, sq, blk, blk, blk], out_shape=[wide, wide, square, wide, wide, wide],
        scratch_shapes=[pltpu.VMEM((GDN_HP, LANES, LANES), F32)], compiler_params=_params(3))(val, kc, at, qd, kd, gb, snaps, do)


FOX_Q, FOX_K, FOX_V = 4 * NH, 5 * NH, 6 * NH
FOX_SCALE = LANES ** -0.5


def _head_row(ct_ref, h, off, width):
    blk = ct_ref[:, pl.ds(off, width)]
    return jnp.sum(jnp.where(_iota(blk.shape, 0) == h, blk, 0.0), axis=0, keepdims=True)


def _col(x):
    return jnp.max(x, axis=1, keepdims=True)


def _row(x):
    return jnp.max(x.T, axis=0, keepdims=True)


def _causal(shape, q_dim):
    return _iota(shape, q_dim) >= _iota(shape, 1 - q_dim)


FOX_HP = 4


def _fox_specs(seq, tile, n_tiles):
    tblk = pl.BlockSpec((tile, FOX_HP * LANES), lambda s, h, i: (s * n_tiles + i, h))
    vtblk = pl.BlockSpec((tile, FOX_HP * LANES), lambda s, h, i: (s * n_tiles + i, h + FOX_V // FOX_HP))
    full = pl.BlockSpec((seq, FOX_HP * LANES), lambda s, h, i: (s, h))
    vfull = pl.BlockSpec((seq, FOX_HP * LANES), lambda s, h, i: (s, h + FOX_V // FOX_HP))
    ctb = pl.BlockSpec((NH, seq), lambda s, h, i: (s * (LANES // NH) + 2, 0))
    return tblk, vtblk, full, vfull, ctb


def _lanes_of(hh):
    return slice(hh * LANES, (hh + 1) * LANES)


def fox_fwd(qn, kn, proj, ct, nseq, seq):
    tq = tk = min(ATT_TILE, seq)
    nq = seq // tq
    rows = nseq * seq
    qblk, _, full, vfull, ctb = _fox_specs(seq, tq, nq)
    hs = range(FOX_HP)

    def body(q_ref, k_ref, v_ref, ct_ref, o_ref, o16_ref, lse_ref):
        hb, i = pl.program_id(1), pl.program_id(2)
        q = [q_ref[:, _lanes_of(hh)] for hh in hs]

        def step(j, carry, diag):
            m, l, acc = (list(carry[t::3]) for t in range(3))
            off = pl.multiple_of(j * tk, tk)
            k = [k_ref[pl.ds(off, tk), _lanes_of(hh)] for hh in hs]
            v = [v_ref[pl.ds(off, tk), _lanes_of(hh)].astype(BF16) for hh in hs]
            ck = [_head_row(ct_ref, hb * FOX_HP + hh, off, tk) for hh in hs]
            s = _each(lambda qq, kk, cc: _dot(qq, kk, "nt") * FOX_SCALE - cc, q, k, ck)
            if diag:
                s = _each(lambda x: jnp.where(_causal(x.shape, 0), x, NEG), s)
            m_new = _each(lambda mm, x: jnp.maximum(mm, jnp.max(x, axis=1, keepdims=True)), m, s)
            p = _each(lambda x, mm: jnp.exp(x - mm), s, m_new)
            alpha = _each(lambda mo, mn: jnp.exp(mo - mn), m, m_new)
            l = _each(lambda a, ll, pp: a * ll + jnp.sum(pp, axis=1, keepdims=True), alpha, l, p)
            acc = _each(lambda a, ac, pp, vv: a * ac + _dot(pp.astype(BF16), vv, "nn"), alpha, acc, p, v)
            return tuple(x for head in zip(m_new, l, acc) for x in head)

        init = (jnp.full((tq, 1), NEG, F32), jnp.zeros((tq, 1), F32), jnp.zeros((tq, LANES), F32)) * FOX_HP
        res = step(i, lax.fori_loop(0, i, lambda j, c: step(j, c, False), init), True)
        for hh in hs:
            m, l, acc = res[3 * hh:3 * hh + 3]
            o = acc / l
            o_ref[:, _lanes_of(hh)] = o
            o16_ref[:, _lanes_of(hh)] = o.astype(BF16)
            lse_ref[:, _lanes_of(hh)] = jnp.broadcast_to(m + jnp.log(l), (tq, LANES))

    wide = (rows, NH * LANES)
    return pl.pallas_call(
        body, name="fox_fwd", grid=(nseq, NH // FOX_HP, nq), in_specs=[qblk, full, vfull, ctb], out_specs=[qblk] * 3,
        out_shape=[jax.ShapeDtypeStruct(wide, F32), jax.ShapeDtypeStruct(wide, BF16), jax.ShapeDtypeStruct(wide, F32)],
        compiler_params=_params(3))(qn, kn, proj, ct)


def fox_dq(qn, kn, proj, ct, do, lse, delta, nseq, seq):
    tq = tk = min(ATT_TILE, seq)
    nq = seq // tq
    rows = nseq * seq
    qblk, _, full, vfull, ctb = _fox_specs(seq, tq, nq)
    hs = range(FOX_HP)

    def body(q_ref, k_ref, v_ref, ct_ref, do_ref, lse_ref, dl_ref, dq_ref, dc_ref):
        hb, i = pl.program_id(1), pl.program_id(2)
        q = [q_ref[:, _lanes_of(hh)] for hh in hs]
        lse = [_col(lse_ref[:, _lanes_of(hh)]) for hh in hs]
        delta = [_col(dl_ref[:, _lanes_of(hh)]) for hh in hs]
        do16 = [do_ref[:, _lanes_of(hh)].astype(BF16) for hh in hs]

        def step(j, carry, diag):
            dq, dc = (list(carry[t::2]) for t in range(2))
            off = pl.multiple_of(j * tk, tk)
            k = [k_ref[pl.ds(off, tk), _lanes_of(hh)] for hh in hs]
            v = [v_ref[pl.ds(off, tk), _lanes_of(hh)].astype(BF16) for hh in hs]
            ck = [_head_row(ct_ref, hb * FOX_HP + hh, off, tk) for hh in hs]
            p = _each(lambda qq, kk, cc, ll: jnp.exp(_dot(qq, kk, "nt") * FOX_SCALE - cc - ll), q, k, ck, lse)
            if diag:
                p = _each(lambda x: jnp.where(_causal(x.shape, 0), x, 0.0), p)
            dp = _each(lambda d, vv: _dot(d, vv, "nt"), do16, v)
            ds = _each(lambda pp, d, dl: pp * (d - dl), p, dp, delta)
            dq = _each(lambda a, x, kk: a + _dot(x.astype(BF16), kk, "nn"), dq, ds, k)
            dc = _each(lambda a, x: a + jnp.sum(x, axis=1, keepdims=True), dc, ds)
            return tuple(x for head in zip(dq, dc) for x in head)

        init = (jnp.zeros((tq, LANES), F32), jnp.zeros((tq, 1), F32)) * FOX_HP
        res = step(i, lax.fori_loop(0, i, lambda j, c: step(j, c, False), init), True)
        for hh in hs:
            dq_ref[:, _lanes_of(hh)] = res[2 * hh] * FOX_SCALE
            dc_ref[:, _lanes_of(hh)] = jnp.where(_iota((tq, LANES), 1) == 0, res[2 * hh + 1], 0.0)

    wide = jax.ShapeDtypeStruct((rows, NH * LANES), F32)
    return pl.pallas_call(
        body, name="fox_dq", grid=(nseq, NH // FOX_HP, nq), in_specs=[qblk, full, vfull, ctb, qblk, qblk, qblk],
        out_specs=[qblk, qblk], out_shape=[wide, wide], compiler_params=_params(3))(qn, kn, proj, ct, do, lse, delta)


def fox_dkv(qn, kn, proj, cb, do, lse, delta, nseq, seq):
    tq = tk = min(ATT_TILE, seq)
    nq = seq // tq
    rows = nseq * seq
    kblk, vblk, full, _, _ = _fox_specs(seq, tk, nq)
    hs = range(FOX_HP)

    def body(q_ref, k_ref, v_ref, cb_ref, do_ref, lse_ref, dl_ref, dk_ref, dv_ref, dc_ref):
        j = pl.program_id(2)
        k = [k_ref[:, _lanes_of(hh)] for hh in hs]
        v16 = [v_ref[:, _lanes_of(hh)].astype(BF16) for hh in hs]
        ck = [_col(cb_ref[:, _lanes_of(hh)]) for hh in hs]

        def step(i, carry, diag):
            dk, dv, dc = (list(carry[t::3]) for t in range(3))
            off = pl.multiple_of(i * tq, tq)
            q = [q_ref[pl.ds(off, tq), _lanes_of(hh)] for hh in hs]
            do16 = [do_ref[pl.ds(off, tq), _lanes_of(hh)].astype(BF16) for hh in hs]
            lse = [_row(lse_ref[pl.ds(off, tq), _lanes_of(hh)]) for hh in hs]
            delta = [_row(dl_ref[pl.ds(off, tq), _lanes_of(hh)]) for hh in hs]
            p = _each(lambda kk, qq, cc, ll: jnp.exp(_dot(kk, qq, "nt") * FOX_SCALE - cc - ll), k, q, ck, lse)
            if diag:
                p = _each(lambda x: jnp.where(_causal(x.shape, 1), x, 0.0), p)
            dv = _each(lambda a, pp, d: a + _dot(pp.astype(BF16), d, "nn"), dv, p, do16)
            ds = _each(lambda pp, vv, d, dl: pp * (_dot(vv, d, "nt") - dl), p, v16, do16, delta)
            dk = _each(lambda a, x, qq: a + _dot(x.astype(BF16), qq, "nn"), dk, ds, q)
            dc = _each(lambda a, x: a + jnp.sum(x, axis=1, keepdims=True), dc, ds)
            return tuple(x for head in zip(dk, dv, dc) for x in head)

        zero = jnp.zeros((tk, LANES), F32)
        carry = step(j, (zero, zero, jnp.zeros((tk, 1), F32)) * FOX_HP, True)
        res = lax.fori_loop(j + 1, nq, lambda i, c: step(i, c, False), carry)
        for hh in hs:
            dk, dv, dc = res[3 * hh:3 * hh + 3]
            dk_ref[:, _lanes_of(hh)] = dk * FOX_SCALE
            dv_ref[:, _lanes_of(hh)] = dv.astype(BF16)
            dc_ref[:, _lanes_of(hh)] = jnp.where(_iota((tk, LANES), 1) == 0, -dc, 0.0)

    wide = (rows, NH * LANES)
    return pl.pallas_call(
        body, name="fox_dkv", grid=(nseq, NH // FOX_HP, nq), in_specs=[full, kblk, vblk, kblk, full, full, full],
        out_specs=[kblk, kblk, kblk],
        out_shape=[jax.ShapeDtypeStruct(wide, F32), jax.ShapeDtypeStruct(wide, BF16), jax.ShapeDtypeStruct(wide, F32)],
        compiler_params=_params(3))(qn, kn, proj, cb, do, lse, delta)


def _adamw_update(w, g, m, v):
    m_new = ADAM_B1 * m + (1.0 - ADAM_B1) * g
    v_new = ADAM_B2 * v + (1.0 - ADAM_B2) * (g * g)
    m_hat = m_new / (1.0 - ADAM_B1 ** ADAM_STEP)
    v_hat = v_new / (1.0 - ADAM_B2 ** ADAM_STEP)
    return -ADAM_LR * (m_hat / (jnp.sqrt(v_hat) + ADAM_EPS) + ADAM_WD * w), m_new, v_new


def adamw(name, w, g, m, v):
    rows, cols = w.shape
    tb = min(rows, 128)
    assert rows % tb == 0
    blk = pl.BlockSpec((tb, cols), lambda i: (i, 0))

    def body(w_ref, g_ref, m_ref, v_ref, d_ref, mo_ref, vo_ref):
        d_ref[...], mo_ref[...], vo_ref[...] = _adamw_update(w_ref[...], g_ref[...], m_ref[...], v_ref[...])

    shp = jax.ShapeDtypeStruct(w.shape, F32)
    return pl.pallas_call(body, name=name, grid=(rows // tb,), in_specs=[blk] * 4, out_specs=[blk] * 3,
                          out_shape=[shp] * 3, compiler_params=_params(1))(w, g, m, v)


SPLIT_TILE = 128


def _tiled(shape2d, ax, n_lead, index):
    blk = (SPLIT_TILE, shape2d[1]) if ax == 0 else (shape2d[0], SPLIT_TILE)

    def index_map(*args):
        *lead, t = index(*args)
        return (*lead, t, 0) if ax == 0 else (*lead, 0, t)

    return pl.BlockSpec((None,) * n_lead + blk, index_map)


def adamw_halves(name, w, mine, other, m, v, c, ax):
    steps = w.shape[ax] // 2 // SPLIT_TILE
    assert w.shape[ax] == 2 * steps * SPLIT_TILE

    def body(c_ref, w_ref, mine_ref, other_ref, m_ref, v_ref, g_ref, d_ref, mo_ref, vo_ref):
        g = jnp.where(pl.program_id(0) // steps == c_ref[0], mine_ref[...], other_ref[...])
        g_ref[...] = g
        d_ref[...], mo_ref[...], vo_ref[...] = _adamw_update(w_ref[...], g, m_ref[...], v_ref[...])

    blk = _tiled(w.shape, ax, 0, lambda i, c_ref: (i,))
    hblk = _tiled(mine.shape, ax, 0, lambda i, c_ref: (i % steps,))
    grid_spec = pltpu.PrefetchScalarGridSpec(num_scalar_prefetch=1, grid=(2 * steps,),
                                             in_specs=[blk, hblk, hblk, blk, blk], out_specs=[blk] * 4)
    shp = jax.ShapeDtypeStruct(w.shape, F32)
    return pl.pallas_call(body, name=name, grid_spec=grid_spec, out_shape=[shp] * 4,
                          compiler_params=_params(1))(c, w, mine, other, m, v)


def add_chips(name, slots, parts, chip, axes):
    outs = []
    for idx, (x, own, ax) in enumerate(zip(slots, parts, axes)):
        n, shape2d = x.shape[0], x.shape[1:]
        steps = shape2d[ax] // SPLIT_TILE
        assert shape2d[ax] == steps * SPLIT_TILE

        def body(me_ref, *refs, n=n):
            o_ref = refs[n + 1]
            acc = None
            for t in range(n):
                term = jnp.where(me_ref[0] == t, refs[n][...], refs[t][...]).astype(F32)
                acc = term if acc is None else acc + term
            o_ref[...] = acc

        def filled(t, n=n):
            return lambda i, me_ref: (jnp.where(me_ref[0] == t, (t + 1) % n, t), i)

        grid_spec = pltpu.PrefetchScalarGridSpec(
            num_scalar_prefetch=1, grid=(steps,),
            in_specs=[_tiled(shape2d, ax, 1, filled(t)) for t in range(n)]
            + [_tiled(shape2d, ax, 1, lambda i, me_ref: (me_ref[0], i))],
            out_specs=_tiled(shape2d, ax, 0, lambda i, me_ref: (i,)))
        outs.append(pl.pallas_call(
            body, name=f"{name}_{idx}", grid_spec=grid_spec, out_shape=jax.ShapeDtypeStruct(shape2d, F32),
            compiler_params=_params(1))(chip, *([x] * n), own))
    return outs


def add_pair(name, gs, rs, c, axes):
    outs = []
    for idx, (g, r, ax) in enumerate(zip(gs, rs, axes)):
        nb = r.shape[0]
        steps = r.shape[1 + ax] // SPLIT_TILE
        assert r.shape[1 + ax] == steps * SPLIT_TILE

        def body(c_ref, g_ref, r_ref, o_ref):
            o_ref[...] = (g_ref[...] + r_ref[...]).astype(BF16)

        grid_spec = pltpu.PrefetchScalarGridSpec(
            num_scalar_prefetch=1, grid=(nb, steps),
            in_specs=[_tiled(g.shape[1:], ax, 1, lambda b, i, c_ref: (b, c_ref[0] * steps + i)),
                      _tiled(r.shape[1:], ax, 1, lambda b, i, c_ref: (b, i))],
            out_specs=_tiled(r.shape[1:], ax, 1, lambda b, i, c_ref: (b, i)))
        outs.append(pl.pallas_call(
            body, name=f"{name}_{idx}", grid_spec=grid_spec, out_shape=jax.ShapeDtypeStruct(r.shape, BF16),
            compiler_params=_params(2))(c, g, r))
    return outs


def _place():
    x, y, c = lax.axis_index("x"), lax.axis_index("y"), lax.axis_index("c")
    return x, y, c, [(1 - x, y), (x, 1 - y), (1 - x, 1 - y)]


def _remote(src, dst, send_sem, recv_sem, dev):
    return pltpu.make_async_remote_copy(src_ref=src, dst_ref=dst, send_sem=send_sem, recv_sem=recv_sem,
                                        device_id=dev, device_id_type=MESH)


def _half(ref, lead, ax, which):
    size = ref.shape[len(lead) + ax] // 2
    part = pl.ds(which * size, size)
    return ref.at[(*lead, part, slice(None)) if ax == 0 else (*lead, slice(None), part)]


def gather_ring(shard):
    rows, cols = shard.shape
    half = cols // 2
    top = rows // 2 // 16 * 16
    assert shard.dtype == BF16 and half % LANES == 0

    def body(in_ref, out_ref, ici_s, ici_r, d2d_s, d2d_r):
        x, y, c, _ = _place()
        me, xn, yn, dg = 2 * x + y, 2 * (1 - x) + y, 2 * x + (1 - y), 2 * (1 - x) + (1 - y)
        to_x, to_y, sib = (1 - x, y, c), (x, 1 - y, c), (x, y, 1 - c)
        mine, other = pl.ds(c * half, half), pl.ds((1 - c) * half, half)
        upper, lower = pl.ds(0, top), pl.ds(top, rows - top)
        started = []

        def send(src, dst, sems, k, dev):
            cp = _remote(src, dst, sems[0].at[k], sems[1].at[k], dev)
            cp.start()
            started.append(cp)

        def arrive(dst, sems, k):
            _remote(dst, dst, sems[0].at[k], sems[1].at[k], sib).wait_recv()

        ici, d2d = (ici_s, ici_r), (d2d_s, d2d_r)
        send(in_ref, out_ref.at[me], d2d, 0, sib)
        send(in_ref.at[:, mine], out_ref.at[me, :, mine], ici, 0, to_x)
        send(in_ref.at[:, mine], out_ref.at[me, :, mine], ici, 1, to_y)
        arrive(out_ref.at[xn, :, mine], ici, 0)
        send(out_ref.at[xn, upper, mine], out_ref.at[xn, upper, mine], ici, 2, to_y)
        send(out_ref.at[xn, :, mine], out_ref.at[xn, :, mine], d2d, 1, sib)
        arrive(out_ref.at[yn, :, mine], ici, 1)
        send(out_ref.at[yn, lower, mine], out_ref.at[yn, lower, mine], ici, 3, to_x)
        send(out_ref.at[yn, :, mine], out_ref.at[yn, :, mine], d2d, 2, sib)
        arrive(out_ref.at[dg, upper, mine], ici, 2)
        send(out_ref.at[dg, upper, mine], out_ref.at[dg, upper, mine], d2d, 3, sib)
        arrive(out_ref.at[dg, lower, mine], ici, 3)
        send(out_ref.at[dg, lower, mine], out_ref.at[dg, lower, mine], d2d, 4, sib)
        arrive(out_ref.at[me], d2d, 0)
        arrive(out_ref.at[xn, :, other], d2d, 1)
        arrive(out_ref.at[yn, :, other], d2d, 2)
        arrive(out_ref.at[dg, upper, other], d2d, 3)
        arrive(out_ref.at[dg, lower, other], d2d, 4)
        for cp in started:
            cp.wait_send()

    return pl.pallas_call(
        body, name="gather_ring", in_specs=[ANY], out_specs=ANY, out_shape=jax.ShapeDtypeStruct((4,) + shard.shape, shard.dtype),
        scratch_shapes=[pltpu.SemaphoreType.DMA((4,))] * 2 + [pltpu.SemaphoreType.DMA((5,))] * 2,
    )(shard)


HBM = pl.BlockSpec(memory_space=pltpu.HBM)
SEM = pl.BlockSpec(memory_space=pltpu.SEMAPHORE)
DATAFLOW = pltpu.SideEffectType.DATAFLOW_SIDE_EFFECTING


def _hbm(a):
    return pltpu.with_memory_space_constraint(a, pltpu.HBM)


class SplitExchange:
    def __init__(self, name, srcs, zone_shapes, n_sems, plan):
        self.name, self.n, self.n_sems, self.plan = name, len(srcs), n_sems, plan
        self.srcs = [_hbm(s) for s in srcs]
        self.zones = [_hbm(lax.empty(shape, s.dtype)) for shape, s in zip(zone_shapes, srcs)]

    def start(self, after):
        n, n_after = self.n, len(after)

        def body(*refs):
            ins, lands = refs[:n], refs[n:2 * n]
            send, recv, token = refs[2 * n + n_after], refs[2 * n + n_after + 1], refs[-1]
            for src, dst, si, ri, dev in self.plan(ins, lands)[0]:
                _remote(src, dst, send.at[si], recv.at[ri], dev).start()
            token[...] = jnp.zeros_like(token)

        res = pl.pallas_call(
            body, name=f"{self.name}_start", in_specs=[HBM] * (2 * n) + [ANY] * n_after,
            out_specs=[SEM, SEM] + [HBM] * (2 * n) + [pl.BlockSpec(memory_space=pltpu.VMEM)],
            out_shape=[pltpu.SemaphoreType.DMA((self.n_sems,)), pltpu.SemaphoreType.DMA((self.n_sems,))]
            + [pltpu.HBM(a.shape, a.dtype) for a in self.srcs + self.zones] + [jax.ShapeDtypeStruct((8, LANES), F32)],
            input_output_aliases={i: 2 + i for i in range(2 * n)},
            compiler_params=pltpu.CompilerParams(has_side_effects=DATAFLOW),
        )(*self.srcs, *self.zones, *after)
        self.sems, self.srcs, self.zones = res[:2], list(res[2:2 + n]), list(res[2 + n:2 + 2 * n])
        return res[-1]

    def wait(self, after):
        n = self.n

        def body(*refs):
            ins, lands = refs[:n], refs[n:2 * n]
            send, recv = refs[2 * n], refs[2 * n + 1]
            sends, arrivals = self.plan(ins, lands)
            for src, _, si, _, dev in sends:
                _remote(src, src, send.at[si], recv.at[si], dev).wait_send()
            for landed, ri in arrivals:
                _remote(landed, landed, send.at[ri], recv.at[ri], _place()[:3]).wait_recv()

        res = pl.pallas_call(
            body, name=f"{self.name}_wait", in_specs=[HBM] * (2 * n) + [SEM, SEM, ANY], out_specs=[HBM] * (2 * n),
            out_shape=[pltpu.HBM(a.shape, a.dtype) for a in self.srcs + self.zones],
            input_output_aliases={i: i for i in range(2 * n)},
            compiler_params=pltpu.CompilerParams(has_side_effects=DATAFLOW),
        )(*self.srcs, *self.zones, *self.sems, after)
        self.srcs = list(res[:n])
        return list(res[n:])


def split_gather(shards):
    n = len(shards)

    def plan(ins, lands):
        x, y, c, chips = _place()
        me = 2 * x + y
        sends, arrivals = [], []
        for w in range(n):
            for j, (ox, oy) in enumerate(chips):
                for k in range(2):
                    base = 2 * (3 * w + j)
                    sends.append((_half(ins[w], (), 0, c), _half(lands[w], (me,), 0, c), base + k, base + c, (ox, oy, k)))
                    arrivals.append((_half(lands[w], (2 * ox + oy,), 0, k), base + k))
            sends.append((ins[w], lands[w].at[me], 6 * n + w, 6 * n + w, (x, y, 1 - c)))
            arrivals.append((lands[w].at[me], 6 * n + w))
        return sends, arrivals

    return SplitExchange("gather", shards, [(4,) + s.shape for s in shards], 7 * n, plan)


def split_pair_swap(name, grads, axes):
    def plan(ins, lands):
        x, y, c, _ = _place()
        sends = [(_half(ins[w], (slice(None),), axes[w], 1 - c), lands[w], w, w, (x, y, 1 - c)) for w in range(len(ins))]
        return sends, [(lands[w], w) for w in range(len(ins))]

    halved = [tuple(d // 2 if i == 1 + ax else d for i, d in enumerate(g.shape)) for g, ax in zip(grads, axes)]
    return SplitExchange(name, grads, halved, len(grads), plan)


def split_chip_exchange(name, parts):
    def plan(ins, lands):
        x, y, c, chips = _place()
        sends, arrivals = [], []
        for w in range(len(ins)):
            for j, (ox, oy) in enumerate(chips):
                sends.append((ins[w].at[2 * ox + oy], lands[w].at[2 * x + y], 3 * w + j, 3 * w + j, (ox, oy, c)))
                arrivals.append((lands[w].at[2 * ox + oy], 3 * w + j))
        return sends, arrivals

    return SplitExchange(name, parts, [p.shape for p in parts], 3 * len(parts), plan)


def split_pair_send(halves):
    def plan(ins, lands):
        x, y, c, _ = _place()
        return ([(ins[w], lands[w], w, w, (x, y, 1 - c)) for w in range(len(ins))],
                [(lands[w], w) for w in range(len(ins))])

    return SplitExchange("pair_send", halves, [h.shape for h in halves], len(halves), plan)


def pair_send(halves):
    n = len(halves)

    def body(*refs):
        ins, outs = refs[:n], refs[n:2 * n]
        send, recv = refs[2 * n:]
        x, y, c, _ = _place()
        cps = [_remote(ins[w], outs[w], send.at[w], recv.at[w], (x, y, 1 - c)) for w in range(n)]
        for cp in cps:
            cp.start()
        for cp in cps:
            cp.wait_recv()
        for cp in cps:
            cp.wait_send()

    return pl.pallas_call(
        body, name="pair_send", in_specs=[ANY] * n, out_specs=[ANY] * n,
        out_shape=[jax.ShapeDtypeStruct(h.shape, h.dtype) for h in halves],
        scratch_shapes=[pltpu.SemaphoreType.DMA((n,))] * 2,
    )(*halves)


def all_reduce_small(name, vec, after=()):
    rows = vec.shape[0]

    def body(v_ref, *refs):
        o_ref, buf, send, recv = refs[len(after):]
        x, y, c, _ = _place()
        me = 4 * x + 2 * y + c
        buf[me] = v_ref[...]
        cps = []
        for k in range(1, 8):
            kx, ky, kc = (k >> 2) & 1, (k >> 1) & 1, k & 1
            peer = (x if kx == 0 else 1 - x, y if ky == 0 else 1 - y, c if kc == 0 else 1 - c)
            cp = _remote(v_ref, buf.at[me], send.at[k - 1], recv.at[k - 1], peer)
            cp.start()
            cps.append(cp)
        for k in range(1, 8):
            kx, ky, kc = (k >> 2) & 1, (k >> 1) & 1, k & 1
            px, py, pc = (x if kx == 0 else 1 - x, y if ky == 0 else 1 - y, c if kc == 0 else 1 - c)
            slot = buf.at[4 * px + 2 * py + pc]
            _remote(slot, slot, send.at[k - 1], recv.at[k - 1], (px, py, pc)).wait_recv()
        for cp in cps:
            cp.wait_send()
        acc = buf[0]
        for d in range(1, 8):
            acc = acc + buf[d]
        o_ref[...] = acc

    vm = pl.BlockSpec(memory_space=pltpu.VMEM)
    return pl.pallas_call(
        body, name=name, in_specs=[vm] + [ANY] * len(after), out_specs=vm, out_shape=jax.ShapeDtypeStruct(vec.shape, F32),
        scratch_shapes=[pltpu.VMEM((8, rows, LANES), F32), pltpu.SemaphoreType.DMA((7,)), pltpu.SemaphoreType.DMA((7,))],
    )(vec, *after)


class NoExchange:
    def __init__(self, late):
        self.late = late

    def late_weights(self, after):
        return self.late

    def reduce_start(self, grads):
        return jnp.zeros((8, LANES), F32)

    def reduce_exchange(self, after):
        return jnp.zeros((8, LANES), F32)

    def reduce_finish(self, after):
        return jnp.zeros((8, LANES), F32)

    def input_grad_start(self, dw_main, dw_small):
        return jnp.zeros((8, LANES), F32)

    def input_grad_exchange(self, after):
        return jnp.zeros((8, LANES), F32)


def local_step(x2, tgt2, g1, g2, gdn_ng, qn_g, kn_g, p1, p2, conv_w, wt_main, wt_small, hooks, nseq, seq):
    rows, dm = x2.shape
    wide = NH * LANES
    row = lambda a, off=0, w=None: (a, "row", off, a.shape[1] if w is None else w)
    rowh = lambda a, off=0, w=LANES: (a, "rowh", off, w)
    par = lambda a: (a, "par", 0, a.shape[1])
    parh = lambda a, off=0: (a, "parh", off, LANES)
    o_row = lambda w, dt: (w, "row", w, dt)
    o_rowh = lambda dt, tw=wide, w=LANES: (tw, "rowh", w, dt)

    u, = ew_fwd("rms1", f_rms, [row(x2), par(g1)], [o_row(dm, BF16)], rows)
    proj = matmul("mm_in", u, wt_main, "nt", BF16)
    sp = matmul("mm_in_small", u, wt_small, "nt", F32)
    so, = ew_fwd("small", f_small, [row(sp), par(p1), par(p2)], [o_row(LANES, F32)], rows)
    cs = cumsum_time("cumsum", so, nseq, seq, False)
    gb, bb, cb = ew_fwd("bcast", f_bcast, [row(so), row(cs)], [o_rowh(F32)] * 3, rows, NH)
    ct = transpose_time("c_time_major", cs, nseq, seq)
    conv = {}
    for mode, off in (("q", 0), ("k", NH), ("v", 2 * NH)):
        conv[mode], = ew_fwd(f"conv_{mode}", make_f_conv(mode), [rowh(proj, off), parh(conv_w, off)], [o_rowh(F32)],
                             rows, NH, seq, "hi", CONV_HEADS)
    val, kcum, attn, qdec, kdec, t_inv = gdn_a_fwd(conv["q"], conv["k"], conv["v"], gb, bb, rows)
    o_a, snaps = gdn_b_fwd(val, kcum, attn, qdec, kdec, gb, nseq, seq)
    ya_in, = ew_fwd("gdn_post", f_post, [rowh(o_a), rowh(proj, 3 * NH), par(gdn_ng)], [o_rowh(BF16)], rows, NH)
    fqn, = ew_fwd("fox_qn", f_rms, [rowh(proj, FOX_Q), par(qn_g)], [o_rowh(BF16)], rows, NH)
    fkn, = ew_fwd("fox_kn", f_rms, [rowh(proj, FOX_K), par(kn_g)], [o_rowh(BF16)], rows, NH)
    o_b, o_b16, lse = fox_fwd(fqn, fkn, proj, ct, nseq, seq)
    p_a, p_b, w_o, w_u, w_d = hooks.late_weights(o_a)
    y_a = matmul("mm_pa", ya_in, p_a, "nn", F32, tn=1024)
    y_b = matmul("mm_pb", o_b16, p_b, "nn", F32, tn=1024)
    gates = [row(proj, 7, dm), row(proj, 8, dm)]
    merged, = ew_fwd("merge", f_merge, gates + [row(y_a), row(y_b)], [o_row(dm, BF16)], rows)
    hres = matmul("mm_out", merged, w_o, "nn", F32, add=x2, tn=1024)
    hn, = ew_fwd("rms2", f_rms, [row(hres), par(g2)], [o_row(dm, BF16)], rows)
    up_blocks = w_u.shape[0]
    act, relu2 = matmul("mm_up", hn, w_u, "nn", F32, col_blocks=up_blocks, out_dtypes=[F32, BF16],
                        epilogue=lambda r: [r, jnp.maximum(r, 0.0) * jnp.maximum(r, 0.0)])
    def loss_tail(r, h_tile, t_tile):
        d = (r + h_tile) - t_tile
        e = (0.5 / dm) * (d * d)
        part = e.reshape(e.shape[0] // 8, 8, e.shape[1]).sum(axis=0)
        part = sum(part[:, t * LANES:(t + 1) * LANES] for t in range(e.shape[1] // LANES))
        g = d * (1.0 / dm)
        return [g, g, part]

    dout, dout16, loss_acc = matmul("mm_down", relu2, w_d, "nn", F32, extras=[hres, tgt2], epilogue=loss_tail,
                                    out_dtypes=[F32, BF16, F32], tile_sums=True)

    d_act = matmul("mm_d_act", dout16, w_d, "nt", BF16, extras=[act], epilogue=lambda r, a: [2.0 * jnp.maximum(a, 0.0) * r])
    dw_d = matmul("mm_dw_down", relu2, dout16, "tn", F32, tn=1024)
    dw_u = matmul("mm_dw_up", hn, d_act, "tn", F32, col_blocks=up_blocks)
    d_hn = matmul("mm_d_hn", d_act, w_u, "nt", F32, col_blocks=up_blocks)
    dh, dh16, dg2 = ew_bwd("rms2_b", f_rms, [row(hres), par(g2)], [(row(d_hn),)], [row(dout)],
                           lambda g, e: [g[0] + e[0], g[0] + e[0], g[1]],
                           [((rows, dm), "row", dm, F32, None), ((rows, dm), "row", dm, BF16, None), ((1, dm), "par", dm, F32, "all")], rows)
    d_merged = matmul("mm_d_merged", dh16, w_o, "nt", F32, tn=1024)
    dw_o = matmul("mm_dw_out", merged, dh16, "tn", F32, tn=1024)
    seg16 = ((rows, dm), "row", dm, BF16, None)
    d_ga16, d_gb16, d_ya16, d_yb16 = ew_bwd("merge_b", f_merge, gates + [row(y_a), row(y_b)], [(row(d_merged),)], [],
                                            lambda g, e: list(g), [seg16] * 4, rows)
    dp_a = matmul("mm_dp_a", ya_in, d_ya16, "tn", F32, tn=1024)
    d_ya_in = matmul("mm_d_ya_in", d_ya16, p_a, "nt", F32, tn=1024)
    dp_b = matmul("mm_dp_b", o_b16, d_yb16, "tn", F32, tn=1024)
    d_ob = matmul("mm_d_ob", d_yb16, p_b, "nt", F32, tn=1024)
    token = hooks.reduce_start(dict(p_a=dp_a, p_b=dp_b, w_o=dw_o, w_u=dw_u, w_d=dw_d))
    gdn_ng_t = gdn_ng + token[0, 0]
    h32 = ((rows, wide), "rowh", LANES, F32, None)
    h16 = ((rows, wide), "rowh", LANES, BF16, None)
    gain = ((1, LANES), "par", LANES, F32, "all")
    d_oa, d_z16, d_gdn_ng = ew_bwd("gdn_post_b", f_post, [rowh(o_a), rowh(proj, 3 * NH), par(gdn_ng_t)], [(rowh(d_ya_in),)], [],
                                   lambda g, e: list(g), [h32, h16, gain], rows, NH)
    dval, dkc, dat, dqd, dkd, dgb_b = gdn_b_bwd(val, kcum, attn, qdec, kdec, gb, snaps, d_oa, nseq, seq)
    d_cq, d_ck, d_cv, d_gb, d_bb = gdn_a_bwd(conv["q"], conv["k"], conv["v"], gb, bb, t_inv, dval, dkc, dat, dqd, dkd, dgb_b, rows)
    token = hooks.reduce_exchange(d_cq)
    conv_w_t = conv_w + token[0, 0]
    d_pre, d_conv = {}, {}
    tap = ((4, wide), "parh", LANES, F32, "inner")
    for mode, off, ctg in (("q", 0, d_cq), ("k", NH, d_ck), ("v", 2 * NH, d_cv)):
        d_pre[mode], d_conv[mode] = ew_bwd(f"conv_{mode}_b", make_f_conv(mode), [rowh(proj, off), parh(conv_w_t, off)],
                                           [(rowh(ctg),)], [], lambda g, e: list(g), [h16, tap], rows, NH, seq, "hi", CONV_HEADS)
    delta, = ew_fwd("fox_delta", f_delta, [rowh(d_ob), rowh(o_b)], [o_rowh(F32)], rows, NH, after=[token])
    d_fqn, d_cq_b = fox_dq(fqn, fkn, proj, ct, d_ob, lse, delta, nseq, seq)
    d_fkn, d_fv16, d_ck_b = fox_dkv(fqn, fkn, proj, cb, d_ob, lse, delta, nseq, seq)
    token = hooks.reduce_finish(d_fkn)
    qn_g_t, kn_g_t = qn_g + token[0, 0], kn_g + token[0, 0]
    d_fq16, d_qn_g = ew_bwd("fox_qn_b", f_rms, [rowh(proj, FOX_Q), par(qn_g_t)], [(rowh(d_fqn),)], [], lambda g, e: list(g),
                            [h16, gain], rows, NH)
    d_fk16, d_kn_g = ew_bwd("fox_kn_b", f_rms, [rowh(proj, FOX_K), par(kn_g_t)], [(rowh(d_fkn),)], [], lambda g, e: list(g),
                            [h16, gain], rows, NH)
    narrow = ((rows, LANES), "row", LANES, F32, None)
    d_so, d_cs = ew_bwd("bcast_b", f_bcast, [row(so), row(cs)], [(rowh(d_gb),), (rowh(d_bb),), (rowh(d_cq_b), rowh(d_ck_b))], [],
                        lambda g, e: list(g), [narrow, narrow], rows, NH)
    d_logf = cumsum_time("cumsum_b", d_cs, nseq, seq, True)
    vec = ((1, LANES), "par", LANES, F32, "all")
    d_sp16, d_p1, d_p2 = ew_bwd("small_b", f_small, [row(sp), par(p1), par(p2)], [(row(d_so), row(d_logf))], [],
                                lambda g, e: list(g), [((rows, LANES), "row", LANES, BF16, None), vec, vec], rows)
    d_proj16 = jnp.concatenate([d_pre["q"], d_pre["k"], d_pre["v"], d_z16, d_fq16, d_fk16, d_fv16, d_ga16, d_gb16], axis=1)
    dw_main = matmul("mm_dw_main", d_proj16, u, "tn", F32)
    dw_small = matmul("mm_dw_small", d_sp16, u, "tn", F32)
    wt_small_t = wt_small + hooks.input_grad_start(dw_main, dw_small)[0, 0].astype(BF16)
    d_u = matmul("mm_d_u_small", d_sp16, wt_small_t, "nn", F32)
    d_u = matmul("mm_d_u_first", d_proj16, wt_main, "nn", F32, add=d_u, k_part=(0, 2))
    d_u = matmul("mm_d_u_second", d_proj16, wt_main, "nn", F32, add=d_u, k_part=(1, 2), after=[hooks.input_grad_exchange(d_u)])
    dx, dg1 = ew_bwd("rms1_b", f_rms, [row(x2), par(g1)], [(row(d_u),)], [row(dh)], lambda g, e: [g[0] + e[0], g[1]],
                     [((rows, dm), "row", dm, F32, None), ((1, dm), "par", dm, F32, "all")], rows)
    d_conv_w = jnp.concatenate([d_conv["q"], d_conv["k"], d_conv["v"]], axis=1)
    return dict(loss_acc=loss_acc, dx=dx, g1=dg1, g2=dg2, gdn_ng=d_gdn_ng, qn=d_qn_g, kn=d_kn_g, p1=d_p1, p2=d_p2,
                conv=d_conv_w, w_main=dw_main, w_small=dw_small, p_a=dp_a, p_b=dp_b, w_o=dw_o, w_u=dw_u, w_d=dw_d)


_W = NH * LANES
_A0, _A1 = 4 * _W, 4 * _W + 2 * NH
_B0, _B1 = _A1 + 3 * _W, _A1 + 3 * _W + NH
N_IN = _B1 + 2 * _W


def _split_w_in(full_t):
    main = jnp.concatenate([full_t[:_A0], full_t[_A1:_B0], full_t[_B1:]], axis=0)
    small = jnp.concatenate([full_t[_A0:_A1], full_t[_B0:_B1], jnp.zeros((LANES - 3 * NH, full_t.shape[1]), full_t.dtype)], axis=0)
    return main, small


def _join_w_in(main, small):
    return jnp.concatenate([main[:_A0], small[:2 * NH], main[_A0:_A0 + 3 * _W], small[2 * NH:3 * NH], main[_A0 + 3 * _W:]], axis=0)


def _lanes(v, at=0):
    return jnp.pad(v.reshape(1, -1), ((0, 0), (at, LANES - at - v.size)))


def kernel(x, norm_mix_g, w_in, gdn_conv_w, gdn_a_log, gdn_dt_bias, gdn_norm_g, fox_q_norm_g, fox_k_norm_g, fox_f_bias, w_proj_gdn, w_proj_fox, w_out, norm_mlp_g, w_up, w_down, loss_target, m_norm_mix_g, m_w_in, m_gdn_conv_w, m_gdn_a_log, m_gdn_dt_bias, m_gdn_norm_g, m_fox_q_norm_g, m_fox_k_norm_g, m_fox_f_bias, m_w_proj_gdn, m_w_proj_fox, m_w_out, m_norm_mlp_g, m_w_up, m_w_down, v_norm_mix_g, v_w_in, v_gdn_conv_w, v_gdn_a_log, v_gdn_dt_bias, v_gdn_norm_g, v_fox_q_norm_g, v_fox_k_norm_g, v_fox_f_bias, v_w_proj_gdn, v_w_proj_fox, v_w_out, v_norm_mlp_g, v_w_up, v_w_down):
    nseq, seq, dm = x.shape
    rows = nseq * seq
    xi, yi, ci = lax.axis_index("x"), lax.axis_index("y"), lax.axis_index("c")
    chip = 2 * xi + yi
    conv_cols = gdn_conv_w.shape[2]

    tr = lambda a: jnp.swapaxes(a[0], 0, 1)
    big = [tr(w_in), w_proj_gdn[0], w_proj_fox[0], w_out[0], w_up[0], w_down[0]]
    axes = [1, 0, 0, 0, 0, 0]
    big16 = [w.astype(BF16) for w in big]
    conv_slot = jnp.zeros((4, 4, conv_cols), F32).at[:, chip].set(jnp.where(ci == 0, gdn_conv_w[0], 0.0))
    conv_full = all_reduce_small("gather_conv", conv_slot.reshape(-1, LANES)).reshape(4, 4 * conv_cols)
    got_in = gather_ring(big16[0])
    wt_main, wt_small = _split_w_in(got_in.reshape(-1, dm))
    core, chip_no = ci.reshape(1).astype(jnp.int32), chip.reshape(1).astype(jnp.int32)
    gather = split_gather(big16[1:])
    token = gather.start([got_in, conv_full])

    class Hooks:
        def late_weights(self, after):
            g_pa, g_pb, g_wo, w_u, g_wd = gather.wait(after)
            return (*(g.reshape(-1, dm) for g in (g_pa, g_pb, g_wo)), w_u, g_wd.reshape(-1, dm))

        def reduce_start(self, grads):
            blocks = [grads["p_a"].reshape(4, -1, dm), grads["p_b"].reshape(4, -1, dm), grads["w_o"].reshape(4, -1, dm),
                      grads["w_u"], grads["w_d"].reshape(4, -1, dm)]
            self.swap = split_pair_swap("pair_swap_late", blocks, axes[1:])
            return self.swap.start([])

        def reduce_exchange(self, after):
            swapped = self.swap.wait(after)
            self.exchange = split_chip_exchange("chip_exchange_late", add_pair("add_pair_late", self.swap.srcs, swapped, core, axes[1:]))
            return self.exchange.start([])

        def reduce_finish(self, after):
            slots = self.exchange.wait(after)
            self.send = split_pair_send(add_chips("add_chips_late", slots, self.exchange.srcs, chip_no, axes[1:]))
            return self.send.start([])

        def input_grad_start(self, dw_main, dw_small):
            self.in_swap = split_pair_swap("pair_swap_in", [_join_w_in(dw_main, dw_small).reshape(4, -1, dm)], axes[:1])
            return self.in_swap.start([])

        def input_grad_exchange(self, after):
            swapped = self.in_swap.wait(after)
            self.in_exchange = split_chip_exchange("chip_exchange_in", add_pair("add_pair_in", self.in_swap.srcs, swapped, core, axes[:1]))
            return self.in_exchange.start([])

    hooks = Hooks()
    p1 = _lanes(gdn_dt_bias[0]) + _lanes(fox_f_bias[0], 2 * NH)
    p2 = _lanes(gdn_a_log[0])

    g = local_step(x.reshape(rows, dm), loss_target.reshape(rows, dm), norm_mix_g + token[0, 0], norm_mlp_g, gdn_norm_g,
                   fox_q_norm_g, fox_k_norm_g, p1, p2, conv_full, wt_main, wt_small, hooks, nseq, seq)

    others = hooks.send.wait(g["dx"])
    big_m = [tr(m_w_in), m_w_proj_gdn[0], m_w_proj_fox[0], m_w_out[0], m_w_up[0], m_w_down[0]]
    big_v = [tr(v_w_in), v_w_proj_gdn[0], v_w_proj_fox[0], v_w_out[0], v_w_up[0], v_w_down[0]]
    names = ["w_in", "w_proj_gdn", "w_proj_fox", "w_out", "w_up", "w_down"]
    big_res, big_grad = {}, {}
    for i in range(1, len(names)):
        big_grad[names[i]], *big_res[names[i]] = adamw_halves(f"adamw_{names[i]}", big[i], hooks.send.srcs[i - 1], others[i - 1],
                                                              big_m[i], big_v[i], core, axes[i])
    slots = hooks.in_exchange.wait(big_res[names[-1]][0])
    mine = add_chips("add_chips_in", slots, hooks.in_exchange.srcs, chip_no, axes[:1])
    res = adamw_halves("adamw_w_in", big[0], mine[0], pair_send(mine)[0], big_m[0], big_v[0], core, axes[0])
    big_grad["w_in"], *big_res["w_in"] = [jnp.swapaxes(r, 0, 1) for r in res]

    small_parts = [g["loss_acc"], g["g1"].reshape(8, LANES), g["g2"].reshape(8, LANES), g["gdn_ng"], g["qn"], g["kn"], g["p1"], g["p2"],
                   g["conv"].reshape(-1, LANES)]
    tiled = [jnp.pad(p, ((0, -p.shape[0] % 8), (0, 0))) for p in small_parts]
    red = all_reduce_small("reduce_small", jnp.concatenate(tiled, axis=0), slots)
    pos, red_parts = 0, []
    for p, t in zip(small_parts, tiled):
        red_parts.append(red[pos:pos + p.shape[0]])
        pos += t.shape[0]
    r_loss, r_g1, r_g2, r_gdn_ng, r_qn, r_kn, r_p1, r_p2, r_conv = red_parts
    loss = jnp.sum(r_loss)
    g_conv = lax.dynamic_slice_in_dim(r_conv.reshape(4, 4, conv_cols), chip, 1, axis=1).reshape(4, conv_cols)
    small_grads = [r_g1.reshape(1, dm), r_p2[:, :NH], r_p1[:, :NH], r_gdn_ng, r_qn, r_kn, r_p1[:, 2 * NH:3 * NH], r_g2.reshape(1, dm)]
    small_w = [norm_mix_g, gdn_a_log, gdn_dt_bias, gdn_norm_g, fox_q_norm_g, fox_k_norm_g, fox_f_bias, norm_mlp_g]
    small_m = [m_norm_mix_g, m_gdn_a_log, m_gdn_dt_bias, m_gdn_norm_g, m_fox_q_norm_g, m_fox_k_norm_g, m_fox_f_bias, m_norm_mlp_g]
    small_v = [v_norm_mix_g, v_gdn_a_log, v_gdn_dt_bias, v_gdn_norm_g, v_fox_q_norm_g, v_fox_k_norm_g, v_fox_f_bias, v_norm_mlp_g]

    def pack(parts):
        flat = jnp.concatenate([jnp.pad(p.reshape(-1), (0, -p.size % LANES)) for p in parts])
        return jnp.pad(flat, (0, -flat.size % (8 * LANES))).reshape(-1, LANES)

    packed = adamw("adamw_small", pack(small_w + [gdn_conv_w[0]]), pack(small_grads + [g_conv]),
                   pack(small_m + [m_gdn_conv_w[0]]), pack(small_v + [v_gdn_conv_w[0]]))

    def unpack(flat2d):
        flat, pos, res = flat2d.reshape(-1), 0, []
        for p in small_w + [gdn_conv_w[0]]:
            res.append(flat[pos:pos + p.size].reshape(p.shape))
            pos += p.size + (-p.size % LANES)
        return res

    s_delta, s_m, s_v = (unpack(a) for a in packed)

    order = ["norm_mix_g", "w_in", "gdn_conv_w", "gdn_a_log", "gdn_dt_bias", "gdn_norm_g", "fox_q_norm_g", "fox_k_norm_g",
             "fox_f_bias", "w_proj_gdn", "w_proj_fox", "w_out", "norm_mlp_g", "w_up", "w_down"]
    small_names = ["norm_mix_g", "gdn_a_log", "gdn_dt_bias", "gdn_norm_g", "fox_q_norm_g", "fox_k_norm_g", "fox_f_bias", "norm_mlp_g",
                   "gdn_conv_w"]
    small_idx = {nm: i for i, nm in enumerate(small_names)}
    shapes = dict(zip(order, (a.shape for a in (norm_mix_g, w_in, gdn_conv_w, gdn_a_log, gdn_dt_bias, gdn_norm_g, fox_q_norm_g,
                                                 fox_k_norm_g, fox_f_bias, w_proj_gdn, w_proj_fox, w_out, norm_mlp_g, w_up, w_down))))
    grads_out, delta_out, m_out, v_out = [], [], [], []
    for nm in order:
        if nm in big_res:
            d, mm, vv = big_res[nm]
            gr = big_grad[nm]
        else:
            i = small_idx[nm]
            gr = (small_grads + [g_conv])[i]
            d, mm, vv = s_delta[i], s_m[i], s_v[i]
        for lst, val in ((grads_out, gr), (delta_out, d), (m_out, mm), (v_out, vv)):
            lst.append(val.reshape(shapes[nm]))
    return (loss, g["dx"].reshape(x.shape), *grads_out, *delta_out, *m_out, *v_out)
```

```python
import functools

import jax
import jax.numpy as jnp
from jax import lax
from jax.experimental import pallas as pl
from jax.experimental.pallas import tpu as pltpu

F32 = jnp.float32
BF16 = jnp.bfloat16
LANES = 128
NH = 8
EPS = 1e-6
GDN_CHUNK = 64
GDN_ROWS = 256
GDN_BASE = 16
ROW_TILE = 512
CONV_HEADS = 2
ATT_TILE = 512
NEG = -1e30
VMEM_LIMIT_BYTES = 58 * 1024 * 1024
LO = lax.Precision.DEFAULT
MESH = pl.DeviceIdType.MESH
ANY = pl.BlockSpec(memory_space=pl.ANY)

ADAM_LR, ADAM_B1, ADAM_B2, ADAM_EPS, ADAM_WD, ADAM_STEP = 0.001, 0.9, 0.999, 1e-08, 0.01, 10


def _params(n_grid):
    return pltpu.CompilerParams(dimension_semantics=("arbitrary",) * n_grid,
                                vmem_limit_bytes=VMEM_LIMIT_BYTES)


def _dot(a, b, dims, precision=None):
    dn = {"nn": (((1,), (0,)), ((), ())), "nt": (((1,), (1,)), ((), ())), "tn": (((0,), (0,)), ((), ()))}[dims]
    return lax.dot_general(a, b, dn, precision=precision, preferred_element_type=F32)


def _iota(shape, dim):
    return lax.broadcasted_iota(jnp.int32, shape, dim)


def _split(x, parts):
    out = []
    for _ in range(parts - 1):
        hi = x.astype(BF16)
        out.append(hi)
        x = x - hi.astype(F32)
    return out + [x.astype(BF16)]


def _dot_mask(mask, b, dims, terms=3):
    m16 = mask.astype(BF16)
    acc = None
    for part in reversed(_split(b, terms)):
        prod = _dot(m16, part, dims)
        acc = prod if acc is None else acc + prod
    return acc


@jax.custom_vjp
def mm_mask(mask, b):
    return _dot_mask(mask, b, "nn", 2)


mm_mask.defvjp(lambda mask, b: (_dot_mask(mask, b, "nn", 2), mask),
               lambda mask, g: (jnp.zeros_like(mask), _dot_mask(mask, g, "tn", 2)))


def matmul(name, a, b, dims, out_dtype, add=None, tm=1024, tn=1024, tk=512, col_blocks=None,
           extras=(), epilogue=None, out_dtypes=None, k_part=None, after=(), tile_sums=False):
    if col_blocks and dims != "tn":
        nb, b_rows, bw = b.shape
        b_shape = (b_rows, nb * bw)
    else:
        b_shape = b.shape
    if dims == "nn":
        (m, k), (_, n) = a.shape, b_shape
    elif dims == "nt":
        (m, k), (n, _) = a.shape, b_shape
    else:
        (k, m), (_, n) = a.shape, b_shape
    k_span = k // (k_part[1] if k_part else 1)
    if col_blocks and dims == "nt":
        k_span = min(k_span, bw)
    tk = k if k <= 1024 else max(t for t in (2048, 1536, 1024, 512, tk) if k_span % t == 0)
    tm, tn, tk = min(tm, m), min(tn, n), min(tk, k)
    assert m % tm == 0 and n % tn == 0 and k % tk == 0, (name, m, n, k)
    k0, nk = (0, k // tk) if k_part is None else (k_part[0] * (k // tk // k_part[1]), k // tk // k_part[1])
    assert k_part is None or (dims == "nn" and not col_blocks and (k // tk) % k_part[1] == 0)
    a_spec = pl.BlockSpec((tk, tm), lambda i, j, kk: (kk, i)) if dims == "tn" else pl.BlockSpec((tm, tk), lambda i, j, kk: (i, kk + k0))
    b_spec = pl.BlockSpec((tn, tk), lambda i, j, kk: (j, kk)) if dims == "nt" else pl.BlockSpec((tk, tn), lambda i, j, kk: (kk + k0, j))
    o_spec = pl.BlockSpec((tm, tn), lambda i, j, kk: (i, j))
    out_shape = (m, n)
    if col_blocks and dims == "nn":
        per = bw // tn
        assert bw % tn == 0
        b_spec = pl.BlockSpec((None, tk, tn), lambda i, j, kk: (j // per, kk, j % per))
    elif col_blocks and dims == "nt":
        per = bw // tk
        assert bw % tk == 0
        b_spec = pl.BlockSpec((None, tn, tk), lambda i, j, kk: (kk // per, j, kk % per))
    elif col_blocks:
        bw = n // col_blocks
        per = bw // tn
        assert bw % tn == 0 and add is None
        o_spec = pl.BlockSpec((None, tm, tn), lambda i, j, kk: (j // per, i, j % per))
        out_shape = (col_blocks, m, bw)
    extras = list(extras) + ([add] if add is not None else [])
    if add is not None:
        assert epilogue is None
        epilogue = lambda r, *e: [r + e[-1]]
    out_dtypes = [out_dtype] if epilogue is None or out_dtypes is None else list(out_dtypes)
    n_ex, n_out = len(extras), len(out_dtypes)

    def body(*refs):
        a_ref, b_ref = refs[0], refs[1]
        ex_refs, o_refs = refs[2:2 + n_ex], refs[2 + n_ex + len(after):2 + n_ex + len(after) + n_out]

        def finish(r):
            res = [r] if epilogue is None else epilogue(r, *[e[...] for e in ex_refs])
            for o_ref, v in zip(o_refs, res):
                o_ref[...] = v.astype(o_ref.dtype)

        if nk == 1:
            finish(_dot(a_ref[...], b_ref[...], dims))
            return
        acc_ref = refs[-1]
        kk = pl.program_id(2)

        @pl.when(kk == 0)
        def _():
            acc_ref[...] = jnp.zeros_like(acc_ref)

        acc_ref[...] += _dot(a_ref[...], b_ref[...], dims)

        @pl.when(kk == nk - 1)
        def _():
            finish(acc_ref[...])

    out_specs = [o_spec] * n_out
    out_shapes = [jax.ShapeDtypeStruct(out_shape, dt) for dt in out_dtypes]
    if tile_sums:
        out_specs[-1] = pl.BlockSpec((8, LANES), lambda i, j, kk: (i, j))
        out_shapes[-1] = jax.ShapeDtypeStruct((8 * (m // tm), LANES * (n // tn)), out_dtypes[-1])
    res = pl.pallas_call(
        body, name=name, grid=(m // tm, n // tn, nk), in_specs=[a_spec, b_spec] + [o_spec] * n_ex + [ANY] * len(after),
        out_specs=out_specs, out_shape=out_shapes,
        scratch_shapes=[pltpu.VMEM((tm, tn), F32)] if nk > 1 else [], compiler_params=_params(3),
    )(a, b, *extras, *after)
    return res[0] if n_out == 1 else res


def _ew_spec(kind, off, width, tb, hp, order, shape=None):
    def ih(g0, g1):
        return (g0, g1) if order == "ih" else (g1, g0)

    assert off % hp == 0 or kind in ("row", "par")
    if kind == "row":
        return pl.BlockSpec((tb, width), lambda g0, g1: (ih(g0, g1)[0], off))
    if kind == "rowh":
        return pl.BlockSpec((tb, hp * width), lambda g0, g1: (ih(g0, g1)[0], ih(g0, g1)[1] + off // hp))
    if kind == "par":
        return pl.BlockSpec(shape, lambda g0, g1: (0, 0))
    if kind == "parh":
        return pl.BlockSpec((shape[0], hp * width), lambda g0, g1: (0, ih(g0, g1)[1] + off // hp))
    raise ValueError(kind)


def _ew_grid(rows, tb, nh, hp, order):
    assert nh % hp == 0 and rows % tb == 0
    return (rows // tb, nh // hp) if order == "ih" else (nh // hp, rows // tb)


def _ew_load(ref, kind, width, hh):
    if kind in ("row", "par"):
        return ref[...].astype(F32)
    return ref[:, hh * width:(hh + 1) * width].astype(F32)


def ew_fwd(name, f, ins, outs, rows, nh=1, tb=ROW_TILE, order="ih", hp=None, after=()):
    hp = nh if hp is None else hp
    n_in = len(ins)

    def body(*refs):
        hb = pl.program_id(1) if order == "ih" else pl.program_id(0)
        for hh in range(hp):
            h = hh if hp == nh else hb * hp + hh
            vals = [_ew_load(r, kd, w, hh) for r, (_, kd, _, w) in zip(refs[:n_in], ins)]
            res = f(h, *vals)
            for r, v, (_, kd, w, _) in zip(refs[n_in + len(after):], res, outs):
                if kd == "row":
                    assert hp == 1
                    r[...] = v.astype(r.dtype)
                else:
                    r[:, hh * w:(hh + 1) * w] = v.astype(r.dtype)

    in_specs = [_ew_spec(kd, off, w, tb, hp, order, a.shape) for (a, kd, off, w) in ins]
    out_specs = [_ew_spec(kd, 0, w, tb, hp, order) for (_, kd, w, _) in outs]
    out_shape = [jax.ShapeDtypeStruct((rows, tw), dt) for (tw, _, _, dt) in outs]
    return pl.pallas_call(
        body, name=name, grid=_ew_grid(rows, tb, nh, hp, order), in_specs=in_specs + [ANY] * len(after), out_specs=out_specs,
        out_shape=out_shape, compiler_params=_params(2),
    )(*[a for (a, _, _, _) in ins], *after)


def ew_bwd(name, f, ins, cts, extras, emit, outs, rows, nh=1, tb=ROW_TILE, order="ih", hp=None):
    hp = nh if hp is None else hp
    n_in = len(ins)
    flat_cts = [d for group in cts for d in group]
    n_ct, n_ex = len(flat_cts), len(extras)

    def body(*refs):
        g0, g1 = pl.program_id(0), pl.program_id(1)
        hb = g1 if order == "ih" else g0
        out_refs = refs[n_in + n_ct + n_ex:]
        shared = [None] * len(outs)

        def store(r, v, first, sl=None):
            def put(val, add):
                if sl is None:
                    r[...] = (r[...] + val if add else val).astype(r.dtype)
                else:
                    r[:, sl] = (r[:, sl] + val if add else val).astype(r.dtype)

            if first is None:
                put(v, False)
            else:
                pl.when(first)(lambda: put(v, False))
                pl.when(jnp.logical_not(first))(lambda: put(v, True))

        for hh in range(hp):
            h = hh if hp == nh else hb * hp + hh
            vals = [_ew_load(r, kd, w, hh) for r, (_, kd, _, w) in zip(refs[:n_in], ins)]
            ct_refs = list(zip(refs[n_in:n_in + n_ct], flat_cts))
            ct_vals, pos = [], 0
            for group in cts:
                v = None
                for r, (_, kd, _, w) in ct_refs[pos:pos + len(group)]:
                    t = _ew_load(r, kd, w, hh)
                    v = t if v is None else v + t
                pos += len(group)
                ct_vals.append(v)
            ex_vals = [_ew_load(r, kd, w, hh) for r, (_, kd, _, w) in zip(refs[n_in + n_ct:n_in + n_ct + n_ex], extras)]
            _, vjp = jax.vjp(lambda *a: f(h, *a), *vals)
            res = emit(vjp(tuple(ct_vals)), ex_vals)
            for idx, (r, v, (_, kd, w, _, acc)) in enumerate(zip(out_refs, res, outs)):
                if kd in ("row", "par"):
                    shared[idx] = v if shared[idx] is None else shared[idx] + v
                else:
                    store(r, v, (g1 == 0) if acc == "inner" else None, slice(hh * w, (hh + 1) * w))
        for idx, (r, (_, kd, _, _, acc)) in enumerate(zip(out_refs, outs)):
            if kd in ("row", "par"):
                assert acc == "all" or hp == nh
                store(r, shared[idx], jnp.logical_and(g0 == 0, g1 == 0) if acc == "all" else None)

    operands = list(ins) + flat_cts + list(extras)
    in_specs = [_ew_spec(kd, off, w, tb, hp, order, a.shape) for (a, kd, off, w) in operands]
    out_specs = [_ew_spec(kd, 0, w, tb, hp, order, shp) for (shp, kd, w, _, _) in outs]
    out_shape = [jax.ShapeDtypeStruct(shp, dt) for (shp, _, _, dt, _) in outs]
    return pl.pallas_call(
        body, name=name, grid=_ew_grid(rows, tb, nh, hp, order), in_specs=in_specs, out_specs=out_specs,
        out_shape=out_shape, compiler_params=_params(2),
    )(*[a for (a, _, _, _) in operands])


def f_rms(h, x, g):
    r = lax.rsqrt(jnp.mean(x * x, axis=-1, keepdims=True) + EPS)
    return (x * r * g,)


def _softplus(z):
    return jnp.maximum(z, 0.0) + jnp.log1p(jnp.exp(-jnp.abs(z)))


def f_small(h, sp, p1, p2):
    lane = _iota(sp.shape, 1)
    z = sp + p1
    g = -jnp.exp(p2) * _softplus(z)
    beta = jax.nn.sigmoid(z)
    logf = -_softplus(-z)
    return (jnp.where(lane < NH, g, jnp.where(lane < 2 * NH, beta, jnp.where(lane < 3 * NH, logf, 0.0))),)


def _pick(x, lane_id):
    lane = _iota(x.shape, 1)
    col = jnp.sum(jnp.where(lane == lane_id, x, 0.0), axis=1, keepdims=True)
    return jnp.broadcast_to(col, x.shape)


def f_bcast(h, so, cs):
    return _pick(so, h), _pick(so, h + NH), _pick(cs, h + 2 * NH)


def _shift_down(s):
    def down(x):
        r = pltpu.roll(x, s, 0)
        head = jnp.where(_iota((8, x.shape[1]), 0) >= s, r[:8], 0.0)
        return jnp.concatenate([head, r[8:]], axis=0)

    def up(g):
        n = g.shape[0]
        r = pltpu.roll(g, n - s, 0)
        tail = jnp.where(_iota((8, g.shape[1]), 0) < 8 - s, r[n - 8:], 0.0)
        return jnp.concatenate([r[:n - 8], tail], axis=0)

    @jax.custom_vjp
    def shift(x):
        return down(x)

    shift.defvjp(lambda x: (down(x), None), lambda _, g: (up(g),))
    return shift


def _silu(x):
    return x * jax.nn.sigmoid(x)


def make_f_conv(mode):
    sh1, sh2, sh3 = _shift_down(1), _shift_down(2), _shift_down(3)

    def f(h, x, w):
        sub = _iota(w.shape, 0)

        def tap(i):
            return jnp.sum(jnp.where(sub == i, w, 0.0), axis=0, keepdims=True)

        y = sh3(x) * tap(0)
        y = y + sh2(x) * tap(1)
        y = y + sh1(x) * tap(2)
        y = y + x * tap(3)
        s = _silu(y)
        if mode == "v":
            return (s,)
        n = s * lax.rsqrt(jnp.sum(s * s, axis=-1, keepdims=True) + EPS)
        if mode == "q":
            n = n * (LANES ** -0.5)
        return (n,)

    return f


def f_post(h, o, z, g):
    r = lax.rsqrt(jnp.mean(o * o, axis=-1, keepdims=True) + EPS)
    return (o * r * g * _silu(z),)


def f_merge(h, ga, gb, ya, yb):
    return (jax.nn.sigmoid(ga) * ya + jax.nn.sigmoid(gb) * yb,)


def f_delta(h, do, o):
    return (jnp.broadcast_to(jnp.sum(do * o, axis=1, keepdims=True), o.shape),)


def cumsum_time(name, x, nseq, seq, reverse):
    nb = seq // LANES

    def body(x_ref, o_ref):
        r, c = _iota((LANES, LANES), 0), _iota((LANES, LANES), 1)
        tri = jnp.where((r <= c) if reverse else (r >= c), 1.0, 0.0).astype(F32)
        carry = jnp.zeros((1, LANES), F32)
        for b in (range(nb - 1, -1, -1) if reverse else range(nb)):
            blk = x_ref[b * LANES:(b + 1) * LANES, :]
            o_ref[b * LANES:(b + 1) * LANES, :] = _dot_mask(tri, blk, "nn") + carry
            carry = carry + jnp.sum(blk, axis=0, keepdims=True)

    spec = pl.BlockSpec((seq, LANES), lambda s: (s, 0))
    return pl.pallas_call(body, name=name, grid=(nseq,), in_specs=[spec], out_specs=spec,
                          out_shape=jax.ShapeDtypeStruct(x.shape, F32), compiler_params=_params(1))(x)


def transpose_time(name, x, nseq, seq):
    def body(x_ref, o_ref):
        o_ref[...] = x_ref[...].T

    return pl.pallas_call(
        body, name=name, grid=(nseq,), in_specs=[pl.BlockSpec((seq, LANES), lambda s: (s, 0))],
        out_specs=pl.BlockSpec((LANES, seq), lambda s: (s, 0)),
        out_shape=jax.ShapeDtypeStruct((nseq * LANES, seq), F32), compiler_params=_params(1))(x)


def _gdn_masks():
    n = GDN_ROWS
    r, c = _iota((n, n), 0), _iota((n, n), 1)
    shift = GDN_CHUNK.bit_length() - 1
    same = lax.shift_right_logical(r, shift) == lax.shift_right_logical(c, shift)
    return r, c, same


def _each(fn, *lists):
    return [fn(*xs) for xs in zip(*lists)]


def _gdn_decay(gbs):
    r, c, same = _gdn_masks()
    seg_tril = jnp.where(jnp.logical_and(same, r >= c), 1.0, 0.0).astype(F32)
    g_cum = _each(lambda gb: mm_mask(seg_tril, gb), gbs)
    lane0 = _iota(gbs[0].shape, 1) == 0
    g_col = _each(lambda g: jnp.sum(jnp.where(lane0, g, 0.0), axis=1, keepdims=True), g_cum)
    g_row = _each(lambda g: jnp.sum(jnp.where(r == c, jnp.broadcast_to(g, (GDN_ROWS, GDN_ROWS)), 0.0), axis=0, keepdims=True), g_col)
    return g_cum, _each(lambda a, b: a - b, g_col, g_row)


def gdn_f1(*args):
    qs, ks, gbs, bbs = (list(args[i::4]) for i in range(4))
    r, c, same = _gdn_masks()
    strict = jnp.logical_and(same, r > c)
    _, diff = _gdn_decay(gbs)
    lane0 = _iota(bbs[0].shape, 1) == 0
    beta_col = _each(lambda bb: jnp.sum(jnp.where(lane0, bb, 0.0), axis=1, keepdims=True), bbs)
    kk = _each(lambda k: _dot(k, k, "nt", LO), ks)
    return tuple(_each(lambda b, x, d: jnp.where(strict, b * x * jnp.exp(jnp.where(strict, d, 0.0)), 0.0), beta_col, kk, diff))


def gdn_f2(*args):
    ts, qs, ks, vs, gbs, bbs = (list(args[i::6]) for i in range(6))
    r, c, same = _gdn_masks()
    incl = jnp.logical_and(same, r >= c)
    g_cum, diff = _gdn_decay(gbs)
    decay = _each(lambda d: jnp.where(incl, jnp.exp(jnp.where(incl, d, 0.0)), 0.0), diff)
    e_g = _each(jnp.exp, g_cum)
    v_beta = _each(lambda v, bb: v * bb, vs, bbs)
    k_beta = _each(lambda k, bb, e: k * bb * e, ks, bbs, e_g)
    value = _each(lambda t, x: x + _dot(t, x, "nn", LO), ts, v_beta)
    k_cum = _each(lambda t, x: x + _dot(t, x, "nn", LO), ts, k_beta)
    attn = _each(lambda q, k, d: _dot(q, k, "nt", LO) * d, qs, ks, decay)
    ones = jnp.where(same, 1.0, 0.0).astype(F32)
    g_last = _each(lambda gb: mm_mask(ones, gb), gbs)
    q_dec = _each(lambda q, e: q * e, qs, e_g)
    k_dec = _each(lambda k, gl, g: k * jnp.exp(gl - g), ks, g_last, g_cum)
    return tuple(x for head in zip(value, k_cum, attn, q_dec, k_dec) for x in head)


def tri_inverse(mats):
    n = GDN_ROWS
    r, c = _iota((n, n), 0), _iota((n, n), 1)
    shift = GDN_BASE.bit_length() - 1
    blk = lax.shift_right_logical(r, shift) == lax.shift_right_logical(c, shift)
    each = lambda fn, *lists: [fn(*xs) for xs in zip(*lists)]
    mm = lambda x, y: _dot(x, y, "nn", LO)
    d = each(lambda a: jnp.where(blk, a, 0.0), mats)
    lo = each(lambda a, dd: a - dd, mats, d)
    p = each(lambda dd: -dd, d)
    c_d = p
    for _ in range(shift - 1):
        p = each(mm, p, p)
        c_d = each(lambda cd, pp, prod: cd + pp + prod, c_d, p, each(mm, c_d, p))
    assert GDN_CHUNK // GDN_BASE == 4
    nmat = each(lambda l, prod: l + prod, lo, each(mm, c_d, lo))
    n2 = each(mm, nmat, nmat)
    c_n = each(lambda nn2, nm, prod: (nn2 - nm) - prod, n2, nmat, each(mm, nmat, n2))
    return each(lambda cn, cd, prod: cn + cd + prod, c_n, c_d, each(mm, c_n, c_d))


GDN_AHP = 4


def _gdn_a_specs():
    blk = pl.BlockSpec((GDN_ROWS, GDN_AHP * LANES), lambda i, h: (i, h))
    sq = pl.BlockSpec((GDN_ROWS, GDN_AHP * GDN_ROWS), lambda i, h: (i, h))
    return blk, sq


def _head(ref, hh):
    width = ref.shape[1] // GDN_AHP
    return ref.at[:, hh * width:(hh + 1) * width]


def gdn_a_fwd(q, k, v, gb, bb, rows):
    blk, sq = _gdn_a_specs()

    def body(q_ref, k_ref, v_ref, gb_ref, bb_ref, val_ref, kc_ref, at_ref, qd_ref, kd_ref, t_ref):
        heads = [[_head(r, hh)[...] for r in (q_ref, k_ref, v_ref, gb_ref, bb_ref)] for hh in range(GDN_AHP)]
        t_corr = tri_inverse(list(gdn_f1(*[x for qv, kv, vv, gv, bv in heads for x in (qv, kv, gv, bv)])))
        res = gdn_f2(*[x for t, head in zip(t_corr, heads) for x in (t, *head)])
        for hh in range(GDN_AHP):
            for r, x in zip((val_ref, kc_ref, at_ref, qd_ref, kd_ref, t_ref), (*res[5 * hh:5 * hh + 5], t_corr[hh])):
                _head(r, hh)[...] = x.astype(r.dtype)

    wide = lambda dt: jax.ShapeDtypeStruct((rows, NH * LANES), dt)
    square = jax.ShapeDtypeStruct((rows, NH * GDN_ROWS), BF16)
    return pl.pallas_call(
        body, name="gdn_a_fwd", grid=(rows // GDN_ROWS, NH // GDN_AHP), in_specs=[blk] * 5,
        out_specs=[blk, blk, sq, blk, blk, sq], out_shape=[wide(F32), wide(BF16), square, wide(BF16), wide(BF16), square],
        compiler_params=_params(2))(q, k, v, gb, bb)


def gdn_a_bwd(q, k, v, gb, bb, t_inv, dval, dkc, dat, dqd, dkd, dgb_b, rows):
    blk, sq = _gdn_a_specs()

    def body(q_ref, k_ref, v_ref, gb_ref, bb_ref, t_ref, dval_ref, dkc_ref, dat_ref, dqd_ref, dkd_ref, dgbb_ref,
             dq_ref, dk_ref, dv_ref, dgb_ref, dbb_ref):
        hs = range(GDN_AHP)
        heads = [[_head(r, hh)[...] for r in (q_ref, k_ref, v_ref, gb_ref, bb_ref)] for hh in hs]
        tvs = [_head(t_ref, hh)[...].astype(F32) for hh in hs]
        _, vjp1 = jax.vjp(gdn_f1, *[x for qv, kv, vv, gv, bv in heads for x in (qv, kv, gv, bv)])
        _, vjp2 = jax.vjp(gdn_f2, *[x for t, head in zip(tvs, heads) for x in (t, *head)])
        g2 = vjp2(tuple(_head(r, hh)[...] for hh in hs for r in (dval_ref, dkc_ref, dat_ref, dqd_ref, dkd_ref)))
        dts = [g2[6 * hh] for hh in hs]
        left = _each(lambda dt, tv: dt + _dot(tv, dt, "tn", LO), dts, tvs)
        g1 = vjp1(tuple(_each(lambda lf, tv: -(lf + _dot(lf, tv, "nt", LO)), left, tvs)))
        for hh in hs:
            dq1, dk1, dgb1, dbb1 = g1[4 * hh:4 * hh + 4]
            _, dq2, dk2, dv2, dgb2, dbb2 = g2[6 * hh:6 * hh + 6]
            _head(dq_ref, hh)[...] = dq1 + dq2
            _head(dk_ref, hh)[...] = dk1 + dk2
            _head(dv_ref, hh)[...] = dv2
            _head(dgb_ref, hh)[...] = dgb1 + dgb2 + _head(dgbb_ref, hh)[...]
            _head(dbb_ref, hh)[...] = dbb1 + dbb2

    wide = jax.ShapeDtypeStruct((rows, NH * LANES), F32)
    return pl.pallas_call(
        body, name="gdn_a_bwd", grid=(rows // GDN_ROWS, NH // GDN_AHP),
        in_specs=[blk] * 5 + [sq, blk, blk, sq, blk, blk, blk], out_specs=[blk] * 5, out_shape=[wide] * 5,
        compiler_params=_params(2))(q, k, v, gb, bb, t_inv, dval, dkc, dat, dqd, dkd, dgb_b)


N_CH = GDN_ROWS // GDN_CHUNK


GDN_HP = 8


def gdn_chunk(c):
    def f(*args):
        val, kc, at, qd, kd, gb, s = (list(args[i::7]) for i in range(7))
        zero = jnp.zeros((GDN_CHUNK, LANES), F32)
        v_new = _each(lambda v, k, st: v - _dot(k, st, "nn", LO), val, kc, s)
        v_pad = _each(lambda v: jnp.concatenate([zero] * c + [v] + [zero] * (N_CH - 1 - c), axis=0), v_new)
        out = _each(lambda q, st, a, vp: _dot(q, st, "nn", LO) + _dot(a, vp, "nn", LO), qd, s, at, v_pad)
        dec = _each(lambda g: jnp.exp(jnp.sum(g, axis=0, keepdims=True)), gb)
        s_new = _each(lambda st, d, k, v: st * d + _dot(k, v, "tn", LO), s, dec, kd, v_new)
        return tuple(x for head in zip(out, s_new) for x in head)

    return f


def _gdn_piece(ref, hh, c):
    width = ref.shape[1] // GDN_HP
    return ref.at[c * GDN_CHUNK:(c + 1) * GDN_CHUNK, hh * width:(hh + 1) * width]


def _gdn_snap(ref, hh, c):
    row = (hh * N_CH + c) * LANES
    return ref.at[row:row + LANES, :]


def _gdn_b_specs(nb, rev):
    def blk_row(s, j):
        return s * nb + (nb - 1 - j if rev else j)

    blk = pl.BlockSpec((GDN_ROWS, GDN_HP * LANES), lambda s, hb, j: (blk_row(s, j), hb))
    sq = pl.BlockSpec((GDN_ROWS, GDN_HP * GDN_ROWS), lambda s, hb, j: (blk_row(s, j), hb))
    snap = pl.BlockSpec((GDN_HP * N_CH * LANES, LANES), lambda s, hb, j: (blk_row(s, j) * (NH // GDN_HP) + hb, 0))
    return blk, sq, snap


def gdn_b_fwd(val, kc, at, qd, kd, gb, nseq, seq):
    nb = seq // GDN_ROWS
    rows = nseq * seq
    blk, sq, snap = _gdn_b_specs(nb, False)

    def body(val_ref, kc_ref, at_ref, qd_ref, kd_ref, gb_ref, o_ref, snap_ref, s_ref):
        @pl.when(pl.program_id(2) == 0)
        def _():
            s_ref[...] = jnp.zeros_like(s_ref)

        hs = range(GDN_HP)
        states = [s_ref[hh] for hh in hs]
        for c in range(N_CH):
            for hh in hs:
                _gdn_snap(snap_ref, hh, c)[...] = states[hh]
            res = gdn_chunk(c)(*[x for hh in hs for x in (
                *[_gdn_piece(r, hh, c)[...].astype(F32) for r in (val_ref, kc_ref, at_ref, qd_ref, kd_ref, gb_ref)], states[hh])])
            for hh in hs:
                _gdn_piece(o_ref, hh, c)[...] = res[2 * hh]
            states = [res[2 * hh + 1] for hh in hs]
        for hh in hs:
            s_ref[hh] = states[hh]

    return pl.pallas_call(
        body, name="gdn_b_fwd", grid=(nseq, NH // GDN_HP, nb), in_specs=[blk, blk, sq, blk, blk, blk], out_specs=[blk, snap],
        out_shape=[jax.ShapeDtypeStruct((rows, NH * LANES), F32),
                   jax.ShapeDtypeStruct((nseq * nb * NH * N_CH * LANES, LANES), F32)],
        scratch_shapes=[pltpu.VMEM((GDN_HP, LANES, LANES), F32)], compiler_params=_params(3))(val, kc, at, qd, kd, gb)


def gdn_b_bwd(val, kc, at, qd, kd, gb, snaps, do, nseq, seq):
    nb = seq // GDN_ROWS
    rows = nseq * seq
    blk, sq, snap = _gdn_b_specs(nb, True)

    def body(val_ref, kc_ref, at_ref, qd_ref, kd_ref, gb_ref, snap_ref, do_ref,
             dval_ref, dkc_ref, dat_ref, dqd_ref, dkd_ref, dgb_ref, ds_ref):
        @pl.when(pl.program_id(2) == 0)
        def _():
            ds_ref[...] = jnp.zeros_like(ds_ref)

        hs = range(GDN_HP)
        d_states = [ds_ref[hh] for hh in hs]
        for c in reversed(range(N_CH)):
            _, vjp = jax.vjp(gdn_chunk(c), *[x for hh in hs for x in (
                *[_gdn_piece(r, hh, c)[...].astype(F32) for r in (val_ref, kc_ref, at_ref, qd_ref, kd_ref, gb_ref)],
                _gdn_snap(snap_ref, hh, c)[...])])
            grads = vjp(tuple(x for hh in hs for x in (_gdn_piece(do_ref, hh, c)[...], d_states[hh])))
            for hh in hs:
                for i, r in enumerate([dval_ref, dkc_ref, dat_ref, dqd_ref, dkd_ref, dgb_ref]):
                    _gdn_piece(r, hh, c)[...] = grads[7 * hh + i]
            d_states = [grads[7 * hh + 6] for hh in hs]
        for hh in hs:
            ds_ref[hh] = d_states[hh]

    wide = jax.ShapeDtypeStruct((rows, NH * LANES), F32)
    square = jax.ShapeDtypeStruct((rows, NH * GDN_ROWS), F32)
    return pl.pallas_call(
        body, name="gdn_b_bwd", grid=(nseq, NH // GDN_HP, nb), in_specs=[blk, blk, sq, blk, blk, blk, snap, blk],
        out_specs=[blk, blk, sq, blk, blk, blk], out_shape=[wide, wide, square, wide, wide, wide],
        scratch_shapes=[pltpu.VMEM((GDN_HP, LANES, LANES), F32)], compiler_params=_params(3))(val, kc, at, qd, kd, gb, snaps, do)


FOX_Q, FOX_K, FOX_V = 4 * NH, 5 * NH, 6 * NH
FOX_SCALE = LANES ** -0.5


def _head_row(ct_ref, h, off, width):
    blk = ct_ref[:, pl.ds(off, width)]
    return jnp.sum(jnp.where(_iota(blk.shape, 0) == h, blk, 0.0), axis=0, keepdims=True)


def _col(x):
    return jnp.max(x, axis=1, keepdims=True)


def _row(x):
    return jnp.max(x.T, axis=0, keepdims=True)


def _causal(shape, q_dim):
    return _iota(shape, q_dim) >= _iota(shape, 1 - q_dim)


FOX_HP = 4


def _fox_specs(seq, tile, n_tiles):
    tblk = pl.BlockSpec((tile, FOX_HP * LANES), lambda s, h, i: (s * n_tiles + i, h))
    vtblk = pl.BlockSpec((tile, FOX_HP * LANES), lambda s, h, i: (s * n_tiles + i, h + FOX_V // FOX_HP))
    full = pl.BlockSpec((seq, FOX_HP * LANES), lambda s, h, i: (s, h))
    vfull = pl.BlockSpec((seq, FOX_HP * LANES), lambda s, h, i: (s, h + FOX_V // FOX_HP))
    ctb = pl.BlockSpec((NH, seq), lambda s, h, i: (s * (LANES // NH) + 2, 0))
    return tblk, vtblk, full, vfull, ctb


def _lanes_of(hh):
    return slice(hh * LANES, (hh + 1) * LANES)


def fox_fwd(qn, kn, proj, ct, nseq, seq):
    tq = tk = min(ATT_TILE, seq)
    nq = seq // tq
    rows = nseq * seq
    qblk, _, full, vfull, ctb = _fox_specs(seq, tq, nq)
    hs = range(FOX_HP)

    def body(q_ref, k_ref, v_ref, ct_ref, o_ref, o16_ref, lse_ref):
        hb, i = pl.program_id(1), pl.program_id(2)
        q = [q_ref[:, _lanes_of(hh)] for hh in hs]

        def step(j, carry, diag):
            m, l, acc = (list(carry[t::3]) for t in range(3))
            off = pl.multiple_of(j * tk, tk)
            k = [k_ref[pl.ds(off, tk), _lanes_of(hh)] for hh in hs]
            v = [v_ref[pl.ds(off, tk), _lanes_of(hh)].astype(BF16) for hh in hs]
            ck = [_head_row(ct_ref, hb * FOX_HP + hh, off, tk) for hh in hs]
            s = _each(lambda qq, kk, cc: _dot(qq, kk, "nt") * FOX_SCALE - cc, q, k, ck)
            if diag:
                s = _each(lambda x: jnp.where(_causal(x.shape, 0), x, NEG), s)
            m_new = _each(lambda mm, x: jnp.maximum(mm, jnp.max(x, axis=1, keepdims=True)), m, s)
            p = _each(lambda x, mm: jnp.exp(x - mm), s, m_new)
            alpha = _each(lambda mo, mn: jnp.exp(mo - mn), m, m_new)
            l = _each(lambda a, ll, pp: a * ll + jnp.sum(pp, axis=1, keepdims=True), alpha, l, p)
            acc = _each(lambda a, ac, pp, vv: a * ac + _dot(pp.astype(BF16), vv, "nn"), alpha, acc, p, v)
            return tuple(x for head in zip(m_new, l, acc) for x in head)

        init = (jnp.full((tq, 1), NEG, F32), jnp.zeros((tq, 1), F32), jnp.zeros((tq, LANES), F32)) * FOX_HP
        res = step(i, lax.fori_loop(0, i, lambda j, c: step(j, c, False), init), True)
        for hh in hs:
            m, l, acc = res[3 * hh:3 * hh + 3]
            o = acc / l
            o_ref[:, _lanes_of(hh)] = o
            o16_ref[:, _lanes_of(hh)] = o.astype(BF16)
            lse_ref[:, _lanes_of(hh)] = jnp.broadcast_to(m + jnp.log(l), (tq, LANES))

    wide = (rows, NH * LANES)
    return pl.pallas_call(
        body, name="fox_fwd", grid=(nseq, NH // FOX_HP, nq), in_specs=[qblk, full, vfull, ctb], out_specs=[qblk] * 3,
        out_shape=[jax.ShapeDtypeStruct(wide, F32), jax.ShapeDtypeStruct(wide, BF16), jax.ShapeDtypeStruct(wide, F32)],
        compiler_params=_params(3))(qn, kn, proj, ct)


def fox_dq(qn, kn, proj, ct, do, lse, delta, nseq, seq):
    tq = tk = min(ATT_TILE, seq)
    nq = seq // tq
    rows = nseq * seq
    qblk, _, full, vfull, ctb = _fox_specs(seq, tq, nq)
    hs = range(FOX_HP)

    def body(q_ref, k_ref, v_ref, ct_ref, do_ref, lse_ref, dl_ref, dq_ref, dc_ref):
        hb, i = pl.program_id(1), pl.program_id(2)
        q = [q_ref[:, _lanes_of(hh)] for hh in hs]
        lse = [_col(lse_ref[:, _lanes_of(hh)]) for hh in hs]
        delta = [_col(dl_ref[:, _lanes_of(hh)]) for hh in hs]
        do16 = [do_ref[:, _lanes_of(hh)].astype(BF16) for hh in hs]

        def step(j, carry, diag):
            dq, dc = (list(carry[t::2]) for t in range(2))
            off = pl.multiple_of(j * tk, tk)
            k = [k_ref[pl.ds(off, tk), _lanes_of(hh)] for hh in hs]
            v = [v_ref[pl.ds(off, tk), _lanes_of(hh)].astype(BF16) for hh in hs]
            ck = [_head_row(ct_ref, hb * FOX_HP + hh, off, tk) for hh in hs]
            p = _each(lambda qq, kk, cc, ll: jnp.exp(_dot(qq, kk, "nt") * FOX_SCALE - cc - ll), q, k, ck, lse)
            if diag:
                p = _each(lambda x: jnp.where(_causal(x.shape, 0), x, 0.0), p)
            dp = _each(lambda d, vv: _dot(d, vv, "nt"), do16, v)
            ds = _each(lambda pp, d, dl: pp * (d - dl), p, dp, delta)
            dq = _each(lambda a, x, kk: a + _dot(x.astype(BF16), kk, "nn"), dq, ds, k)
            dc = _each(lambda a, x: a + jnp.sum(x, axis=1, keepdims=True), dc, ds)
            return tuple(x for head in zip(dq, dc) for x in head)

        init = (jnp.zeros((tq, LANES), F32), jnp.zeros((tq, 1), F32)) * FOX_HP
        res = step(i, lax.fori_loop(0, i, lambda j, c: step(j, c, False), init), True)
        for hh in hs:
            dq_ref[:, _lanes_of(hh)] = res[2 * hh] * FOX_SCALE
            dc_ref[:, _lanes_of(hh)] = jnp.where(_iota((tq, LANES), 1) == 0, res[2 * hh + 1], 0.0)

    wide = jax.ShapeDtypeStruct((rows, NH * LANES), F32)
    return pl.pallas_call(
        body, name="fox_dq", grid=(nseq, NH // FOX_HP, nq), in_specs=[qblk, full, vfull, ctb, qblk, qblk, qblk],
        out_specs=[qblk, qblk], out_shape=[wide, wide], compiler_params=_params(3))(qn, kn, proj, ct, do, lse, delta)


def fox_dkv(qn, kn, proj, cb, do, lse, delta, nseq, seq):
    tq = tk = min(ATT_TILE, seq)
    nq = seq // tq
    rows = nseq * seq
    kblk, vblk, full, _, _ = _fox_specs(seq, tk, nq)
    hs = range(FOX_HP)

    def body(q_ref, k_ref, v_ref, cb_ref, do_ref, lse_ref, dl_ref, dk_ref, dv_ref, dc_ref):
        j = pl.program_id(2)
        k = [k_ref[:, _lanes_of(hh)] for hh in hs]
        v16 = [v_ref[:, _lanes_of(hh)].astype(BF16) for hh in hs]
        ck = [_col(cb_ref[:, _lanes_of(hh)]) for hh in hs]

        def step(i, carry, diag):
            dk, dv, dc = (list(carry[t::3]) for t in range(3))
            off = pl.multiple_of(i * tq, tq)
            q = [q_ref[pl.ds(off, tq), _lanes_of(hh)] for hh in hs]
            do16 = [do_ref[pl.ds(off, tq), _lanes_of(hh)].astype(BF16) for hh in hs]
            lse = [_row(lse_ref[pl.ds(off, tq), _lanes_of(hh)]) for hh in hs]
            delta = [_row(dl_ref[pl.ds(off, tq), _lanes_of(hh)]) for hh in hs]
            p = _each(lambda kk, qq, cc, ll: jnp.exp(_dot(kk, qq, "nt") * FOX_SCALE - cc - ll), k, q, ck, lse)
            if diag:
                p = _each(lambda x: jnp.where(_causal(x.shape, 1), x, 0.0), p)
            dv = _each(lambda a, pp, d: a + _dot(pp.astype(BF16), d, "nn"), dv, p, do16)
            ds = _each(lambda pp, vv, d, dl: pp * (_dot(vv, d, "nt") - dl), p, v16, do16, delta)
            dk = _each(lambda a, x, qq: a + _dot(x.astype(BF16), qq, "nn"), dk, ds, q)
            dc = _each(lambda a, x: a + jnp.sum(x, axis=1, keepdims=True), dc, ds)
            return tuple(x for head in zip(dk, dv, dc) for x in head)

        zero = jnp.zeros((tk, LANES), F32)
        carry = step(j, (zero, zero, jnp.zeros((tk, 1), F32)) * FOX_HP, True)
        res = lax.fori_loop(j + 1, nq, lambda i, c: step(i, c, False), carry)
        for hh in hs:
            dk, dv, dc = res[3 * hh:3 * hh + 3]
            dk_ref[:, _lanes_of(hh)] = dk * FOX_SCALE
            dv_ref[:, _lanes_of(hh)] = dv.astype(BF16)
            dc_ref[:, _lanes_of(hh)] = jnp.where(_iota((tk, LANES), 1) == 0, -dc, 0.0)

    wide = (rows, NH * LANES)
    return pl.pallas_call(
        body, name="fox_dkv", grid=(nseq, NH // FOX_HP, nq), in_specs=[full, kblk, vblk, kblk, full, full, full],
        out_specs=[kblk, kblk, kblk],
        out_shape=[jax.ShapeDtypeStruct(wide, F32), jax.ShapeDtypeStruct(wide, BF16), jax.ShapeDtypeStruct(wide, F32)],
        compiler_params=_params(3))(qn, kn, proj, cb, do, lse, delta)


def _adamw_update(w, g, m, v):
    m_new = ADAM_B1 * m + (1.0 - ADAM_B1) * g
    v_new = ADAM_B2 * v + (1.0 - ADAM_B2) * (g * g)
    m_hat = m_new / (1.0 - ADAM_B1 ** ADAM_STEP)
    v_hat = v_new / (1.0 - ADAM_B2 ** ADAM_STEP)
    return -ADAM_LR * (m_hat / (jnp.sqrt(v_hat) + ADAM_EPS) + ADAM_WD * w), m_new, v_new


def adamw(name, w, g, m, v):
    rows, cols = w.shape
    tb = min(rows, 128)
    assert rows % tb == 0
    blk = pl.BlockSpec((tb, cols), lambda i: (i, 0))

    def body(w_ref, g_ref, m_ref, v_ref, d_ref, mo_ref, vo_ref):
        d_ref[...], mo_ref[...], vo_ref[...] = _adamw_update(w_ref[...], g_ref[...], m_ref[...], v_ref[...])

    shp = jax.ShapeDtypeStruct(w.shape, F32)
    return pl.pallas_call(body, name=name, grid=(rows // tb,), in_specs=[blk] * 4, out_specs=[blk] * 3,
                          out_shape=[shp] * 3, compiler_params=_params(1))(w, g, m, v)


SPLIT_TILE = 128


def _tiled(shape2d, ax, n_lead, index):
    blk = (SPLIT_TILE, shape2d[1]) if ax == 0 else (shape2d[0], SPLIT_TILE)

    def index_map(*args):
        *lead, t = index(*args)
        return (*lead, t, 0) if ax == 0 else (*lead, 0, t)

    return pl.BlockSpec((None,) * n_lead + blk, index_map)


def adamw_halves(name, w, mine, other, m, v, c, ax):
    steps = w.shape[ax] // 2 // SPLIT_TILE
    assert w.shape[ax] == 2 * steps * SPLIT_TILE

    def body(c_ref, w_ref, mine_ref, other_ref, m_ref, v_ref, g_ref, d_ref, mo_ref, vo_ref):
        g = jnp.where(pl.program_id(0) // steps == c_ref[0], mine_ref[...], other_ref[...])
        g_ref[...] = g
        d_ref[...], mo_ref[...], vo_ref[...] = _adamw_update(w_ref[...], g, m_ref[...], v_ref[...])

    blk = _tiled(w.shape, ax, 0, lambda i, c_ref: (i,))
    hblk = _tiled(mine.shape, ax, 0, lambda i, c_ref: (i % steps,))
    grid_spec = pltpu.PrefetchScalarGridSpec(num_scalar_prefetch=1, grid=(2 * steps,),
                                             in_specs=[blk, hblk, hblk, blk, blk], out_specs=[blk] * 4)
    shp = jax.ShapeDtypeStruct(w.shape, F32)
    return pl.pallas_call(body, name=name, grid_spec=grid_spec, out_shape=[shp] * 4,
                          compiler_params=_params(1))(c, w, mine, other, m, v)


def add_chips(name, slots, parts, chip, axes):
    outs = []
    for idx, (x, own, ax) in enumerate(zip(slots, parts, axes)):
        n, shape2d = x.shape[0], x.shape[1:]
        steps = shape2d[ax] // SPLIT_TILE
        assert shape2d[ax] == steps * SPLIT_TILE

        def body(me_ref, *refs, n=n):
            o_ref = refs[n + 1]
            acc = None
            for t in range(n):
                term = jnp.where(me_ref[0] == t, refs[n][...], refs[t][...]).astype(F32)
                acc = term if acc is None else acc + term
            o_ref[...] = acc

        def filled(t, n=n):
            return lambda i, me_ref: (jnp.where(me_ref[0] == t, (t + 1) % n, t), i)

        grid_spec = pltpu.PrefetchScalarGridSpec(
            num_scalar_prefetch=1, grid=(steps,),
            in_specs=[_tiled(shape2d, ax, 1, filled(t)) for t in range(n)]
            + [_tiled(shape2d, ax, 1, lambda i, me_ref: (me_ref[0], i))],
            out_specs=_tiled(shape2d, ax, 0, lambda i, me_ref: (i,)))
        outs.append(pl.pallas_call(
            body, name=f"{name}_{idx}", grid_spec=grid_spec, out_shape=jax.ShapeDtypeStruct(shape2d, F32),
            compiler_params=_params(1))(chip, *([x] * n), own))
    return outs


def add_pair(name, gs, rs, c, axes):
    outs = []
    for idx, (g, r, ax) in enumerate(zip(gs, rs, axes)):
        nb = r.shape[0]
        steps = r.shape[1 + ax] // SPLIT_TILE
        assert r.shape[1 + ax] == steps * SPLIT_TILE

        def body(c_ref, g_ref, r_ref, o_ref):
            o_ref[...] = (g_ref[...] + r_ref[...]).astype(BF16)

        grid_spec = pltpu.PrefetchScalarGridSpec(
            num_scalar_prefetch=1, grid=(nb, steps),
            in_specs=[_tiled(g.shape[1:], ax, 1, lambda b, i, c_ref: (b, c_ref[0] * steps + i)),
                      _tiled(r.shape[1:], ax, 1, lambda b, i, c_ref: (b, i))],
            out_specs=_tiled(r.shape[1:], ax, 1, lambda b, i, c_ref: (b, i)))
        outs.append(pl.pallas_call(
            body, name=f"{name}_{idx}", grid_spec=grid_spec, out_shape=jax.ShapeDtypeStruct(r.shape, BF16),
            compiler_params=_params(2))(c, g, r))
    return outs


def _place():
    x, y, c = lax.axis_index("x"), lax.axis_index("y"), lax.axis_index("c")
    return x, y, c, [(1 - x, y), (x, 1 - y), (1 - x, 1 - y)]


def _remote(src, dst, send_sem, recv_sem, dev):
    return pltpu.make_async_remote_copy(src_ref=src, dst_ref=dst, send_sem=send_sem, recv_sem=recv_sem,
                                        device_id=dev, device_id_type=MESH)


def _half(ref, lead, ax, which):
    size = ref.shape[len(lead) + ax] // 2
    part = pl.ds(which * size, size)
    return ref.at[(*lead, part, slice(None)) if ax == 0 else (*lead, slice(None), part)]


def gather_ring(shard):
    rows, cols = shard.shape
    half = cols // 2
    top = rows // 2 // 16 * 16
    assert shard.dtype == BF16 and half % LANES == 0

    def body(in_ref, out_ref, ici_s, ici_r, d2d_s, d2d_r):
        x, y, c, _ = _place()
        me, xn, yn, dg = 2 * x + y, 2 * (1 - x) + y, 2 * x + (1 - y), 2 * (1 - x) + (1 - y)
        to_x, to_y, sib = (1 - x, y, c), (x, 1 - y, c), (x, y, 1 - c)
        mine, other = pl.ds(c * half, half), pl.ds((1 - c) * half, half)
        upper, lower = pl.ds(0, top), pl.ds(top, rows - top)
        started = []

        def send(src, dst, sems, k, dev):
            cp = _remote(src, dst, sems[0].at[k], sems[1].at[k], dev)
            cp.start()
            started.append(cp)

        def arrive(dst, sems, k):
            _remote(dst, dst, sems[0].at[k], sems[1].at[k], sib).wait_recv()

        ici, d2d = (ici_s, ici_r), (d2d_s, d2d_r)
        send(in_ref, out_ref.at[me], d2d, 0, sib)
        send(in_ref.at[:, mine], out_ref.at[me, :, mine], ici, 0, to_x)
        send(in_ref.at[:, mine], out_ref.at[me, :, mine], ici, 1, to_y)
        arrive(out_ref.at[xn, :, mine], ici, 0)
        send(out_ref.at[xn, upper, mine], out_ref.at[xn, upper, mine], ici, 2, to_y)
        send(out_ref.at[xn, :, mine], out_ref.at[xn, :, mine], d2d, 1, sib)
        arrive(out_ref.at[yn, :, mine], ici, 1)
        send(out_ref.at[yn, lower, mine], out_ref.at[yn, lower, mine], ici, 3, to_x)
        send(out_ref.at[yn, :, mine], out_ref.at[yn, :, mine], d2d, 2, sib)
        arrive(out_ref.at[dg, upper, mine], ici, 2)
        send(out_ref.at[dg, upper, mine], out_ref.at[dg, upper, mine], d2d, 3, sib)
        arrive(out_ref.at[dg, lower, mine], ici, 3)
        send(out_ref.at[dg, lower, mine], out_ref.at[dg, lower, mine], d2d, 4, sib)
        arrive(out_ref.at[me], d2d, 0)
        arrive(out_ref.at[xn, :, other], d2d, 1)
        arrive(out_ref.at[yn, :, other], d2d, 2)
        arrive(out_ref.at[dg, upper, other], d2d, 3)
        arrive(out_ref.at[dg, lower, other], d2d, 4)
        for cp in started:
            cp.wait_send()

    return pl.pallas_call(
        body, name="gather_ring", in_specs=[ANY], out_specs=ANY, out_shape=jax.ShapeDtypeStruct((4,) + shard.shape, shard.dtype),
        scratch_shapes=[pltpu.SemaphoreType.DMA((4,))] * 2 + [pltpu.SemaphoreType.DMA((5,))] * 2,
    )(shard)


HBM = pl.BlockSpec(memory_space=pltpu.HBM)
SEM = pl.BlockSpec(memory_space=pltpu.SEMAPHORE)
DATAFLOW = pltpu.SideEffectType.DATAFLOW_SIDE_EFFECTING


def _hbm(a):
    return pltpu.with_memory_space_constraint(a, pltpu.HBM)


class SplitExchange:
    def __init__(self, name, srcs, zone_shapes, n_sems, plan):
        self.name, self.n, self.n_sems, self.plan = name, len(srcs), n_sems, plan
        self.srcs = [_hbm(s) for s in srcs]
        self.zones = [_hbm(lax.empty(shape, s.dtype)) for shape, s in zip(zone_shapes, srcs)]

    def start(self, after):
        n, n_after = self.n, len(after)

        def body(*refs):
            ins, lands = refs[:n], refs[n:2 * n]
            send, recv, token = refs[2 * n + n_after], refs[2 * n + n_after + 1], refs[-1]
            for src, dst, si, ri, dev in self.plan(ins, lands)[0]:
                _remote(src, dst, send.at[si], recv.at[ri], dev).start()
            token[...] = jnp.zeros_like(token)

        res = pl.pallas_call(
            body, name=f"{self.name}_start", in_specs=[HBM] * (2 * n) + [ANY] * n_after,
            out_specs=[SEM, SEM] + [HBM] * (2 * n) + [pl.BlockSpec(memory_space=pltpu.VMEM)],
            out_shape=[pltpu.SemaphoreType.DMA((self.n_sems,)), pltpu.SemaphoreType.DMA((self.n_sems,))]
            + [pltpu.HBM(a.shape, a.dtype) for a in self.srcs + self.zones] + [jax.ShapeDtypeStruct((8, LANES), F32)],
            input_output_aliases={i: 2 + i for i in range(2 * n)},
            compiler_params=pltpu.CompilerParams(has_side_effects=DATAFLOW),
        )(*self.srcs, *self.zones, *after)
        self.sems, self.srcs, self.zones = res[:2], list(res[2:2 + n]), list(res[2 + n:2 + 2 * n])
        return res[-1]

    def wait(self, after):
        n = self.n

        def body(*refs):
            ins, lands = refs[:n], refs[n:2 * n]
            send, recv = refs[2 * n], refs[2 * n + 1]
            sends, arrivals = self.plan(ins, lands)
            for src, _, si, _, dev in sends:
                _remote(src, src, send.at[si], recv.at[si], dev).wait_send()
            for landed, ri in arrivals:
                _remote(landed, landed, send.at[ri], recv.at[ri], _place()[:3]).wait_recv()

        res = pl.pallas_call(
            body, name=f"{self.name}_wait", in_specs=[HBM] * (2 * n) + [SEM, SEM, ANY], out_specs=[HBM] * (2 * n),
            out_shape=[pltpu.HBM(a.shape, a.dtype) for a in self.srcs + self.zones],
            input_output_aliases={i: i for i in range(2 * n)},
            compiler_params=pltpu.CompilerParams(has_side_effects=DATAFLOW),
        )(*self.srcs, *self.zones, *self.sems, after)
        self.srcs = list(res[:n])
        return list(res[n:])


def split_gather(shards):
    n = len(shards)

    def plan(ins, lands):
        x, y, c, chips = _place()
        me = 2 * x + y
        sends, arrivals = [], []
        for w in range(n):
            for j, (ox, oy) in enumerate(chips):
                for k in range(2):
                    base = 2 * (3 * w + j)
                    sends.append((_half(ins[w], (), 0, c), _half(lands[w], (me,), 0, c), base + k, base + c, (ox, oy, k)))
                    arrivals.append((_half(lands[w], (2 * ox + oy,), 0, k), base + k))
            sends.append((ins[w], lands[w].at[me], 6 * n + w, 6 * n + w, (x, y, 1 - c)))
            arrivals.append((lands[w].at[me], 6 * n + w))
        return sends, arrivals

    return SplitExchange("gather", shards, [(4,) + s.shape for s in shards], 7 * n, plan)


def split_pair_swap(name, grads, axes):
    def plan(ins, lands):
        x, y, c, _ = _place()
        sends = [(_half(ins[w], (slice(None),), axes[w], 1 - c), lands[w], w, w, (x, y, 1 - c)) for w in range(len(ins))]
        return sends, [(lands[w], w) for w in range(len(ins))]

    halved = [tuple(d // 2 if i == 1 + ax else d for i, d in enumerate(g.shape)) for g, ax in zip(grads, axes)]
    return SplitExchange(name, grads, halved, len(grads), plan)


def split_chip_exchange(name, parts):
    def plan(ins, lands):
        x, y, c, chips = _place()
        sends, arrivals = [], []
        for w in range(len(ins)):
            for j, (ox, oy) in enumerate(chips):
                sends.append((ins[w].at[2 * ox + oy], lands[w].at[2 * x + y], 3 * w + j, 3 * w + j, (ox, oy, c)))
                arrivals.append((lands[w].at[2 * ox + oy], 3 * w + j))
        return sends, arrivals

    return SplitExchange(name, parts, [p.shape for p in parts], 3 * len(parts), plan)


def split_pair_send(halves):
    def plan(ins, lands):
        x, y, c, _ = _place()
        return ([(ins[w], lands[w], w, w, (x, y, 1 - c)) for w in range(len(ins))],
                [(lands[w], w) for w in range(len(ins))])

    return SplitExchange("pair_send", halves, [h.shape for h in halves], len(halves), plan)


def pair_send(halves):
    n = len(halves)

    def body(*refs):
        ins, outs = refs[:n], refs[n:2 * n]
        send, recv = refs[2 * n:]
        x, y, c, _ = _place()
        cps = [_remote(ins[w], outs[w], send.at[w], recv.at[w], (x, y, 1 - c)) for w in range(n)]
        for cp in cps:
            cp.start()
        for cp in cps:
            cp.wait_recv()
        for cp in cps:
            cp.wait_send()

    return pl.pallas_call(
        body, name="pair_send", in_specs=[ANY] * n, out_specs=[ANY] * n,
        out_shape=[jax.ShapeDtypeStruct(h.shape, h.dtype) for h in halves],
        scratch_shapes=[pltpu.SemaphoreType.DMA((n,))] * 2,
    )(*halves)


def all_reduce_small(name, vec, after=()):
    rows = vec.shape[0]

    def body(v_ref, *refs):
        o_ref, buf, send, recv = refs[len(after):]
        x, y, c, _ = _place()
        me = 4 * x + 2 * y + c
        buf[me] = v_ref[...]
        cps = []
        for k in range(1, 8):
            kx, ky, kc = (k >> 2) & 1, (k >> 1) & 1, k & 1
            peer = (x if kx == 0 else 1 - x, y if ky == 0 else 1 - y, c if kc == 0 else 1 - c)
            cp = _remote(v_ref, buf.at[me], send.at[k - 1], recv.at[k - 1], peer)
            cp.start()
            cps.append(cp)
        for k in range(1, 8):
            kx, ky, kc = (k >> 2) & 1, (k >> 1) & 1, k & 1
            px, py, pc = (x if kx == 0 else 1 - x, y if ky == 0 else 1 - y, c if kc == 0 else 1 - c)
            slot = buf.at[4 * px + 2 * py + pc]
            _remote(slot, slot, send.at[k - 1], recv.at[k - 1], (px, py, pc)).wait_recv()
        for cp in cps:
            cp.wait_send()
        acc = buf[0]
        for d in range(1, 8):
            acc = acc + buf[d]
        o_ref[...] = acc

    vm = pl.BlockSpec(memory_space=pltpu.VMEM)
    return pl.pallas_call(
        body, name=name, in_specs=[vm] + [ANY] * len(after), out_specs=vm, out_shape=jax.ShapeDtypeStruct(vec.shape, F32),
        scratch_shapes=[pltpu.VMEM((8, rows, LANES), F32), pltpu.SemaphoreType.DMA((7,)), pltpu.SemaphoreType.DMA((7,))],
    )(vec, *after)


class NoExchange:
    def __init__(self, late):
        self.late = late

    def late_weights(self, after):
        return self.late

    def reduce_start(self, grads):
        return jnp.zeros((8, LANES), F32)

    def reduce_exchange(self, after):
        return jnp.zeros((8, LANES), F32)

    def reduce_finish(self, after):
        return jnp.zeros((8, LANES), F32)

    def input_grad_start(self, dw_main, dw_small):
        return jnp.zeros((8, LANES), F32)

    def input_grad_exchange(self, after):
        return jnp.zeros((8, LANES), F32)


def local_step(x2, tgt2, g1, g2, gdn_ng, qn_g, kn_g, p1, p2, conv_w, wt_main, wt_small, hooks, nseq, seq):
    rows, dm = x2.shape
    wide = NH * LANES
    row = lambda a, off=0, w=None: (a, "row", off, a.shape[1] if w is None else w)
    rowh = lambda a, off=0, w=LANES: (a, "rowh", off, w)
    par = lambda a: (a, "par", 0, a.shape[1])
    parh = lambda a, off=0: (a, "parh", off, LANES)
    o_row = lambda w, dt: (w, "row", w, dt)
    o_rowh = lambda dt, tw=wide, w=LANES: (tw, "rowh", w, dt)

    u, = ew_fwd("rms1", f_rms, [row(x2), par(g1)], [o_row(dm, BF16)], rows)
    proj = matmul("mm_in", u, wt_main, "nt", BF16)
    sp = matmul("mm_in_small", u, wt_small, "nt", F32)
    so, = ew_fwd("small", f_small, [row(sp), par(p1), par(p2)], [o_row(LANES, F32)], rows)
    cs = cumsum_time("cumsum", so, nseq, seq, False)
    gb, bb, cb = ew_fwd("bcast", f_bcast, [row(so), row(cs)], [o_rowh(F32)] * 3, rows, NH)
    ct = transpose_time("c_time_major", cs, nseq, seq)
    conv = {}
    for mode, off in (("q", 0), ("k", NH), ("v", 2 * NH)):
        conv[mode], = ew_fwd(f"conv_{mode}", make_f_conv(mode), [rowh(proj, off), parh(conv_w, off)], [o_rowh(F32)],
                             rows, NH, seq, "hi", CONV_HEADS)
    val, kcum, attn, qdec, kdec, t_inv = gdn_a_fwd(conv["q"], conv["k"], conv["v"], gb, bb, rows)
    o_a, snaps = gdn_b_fwd(val, kcum, attn, qdec, kdec, gb, nseq, seq)
    ya_in, = ew_fwd("gdn_post", f_post, [rowh(o_a), rowh(proj, 3 * NH), par(gdn_ng)], [o_rowh(BF16)], rows, NH)
    fqn, = ew_fwd("fox_qn", f_rms, [rowh(proj, FOX_Q), par(qn_g)], [o_rowh(BF16)], rows, NH)
    fkn, = ew_fwd("fox_kn", f_rms, [rowh(proj, FOX_K), par(kn_g)], [o_rowh(BF16)], rows, NH)
    o_b, o_b16, lse = fox_fwd(fqn, fkn, proj, ct, nseq, seq)
    p_a, p_b, w_o, w_u, w_d = hooks.late_weights(o_a)
    y_a = matmul("mm_pa", ya_in, p_a, "nn", F32, tn=1024)
    y_b = matmul("mm_pb", o_b16, p_b, "nn", F32, tn=1024)
    gates = [row(proj, 7, dm), row(proj, 8, dm)]
    merged, = ew_fwd("merge", f_merge, gates + [row(y_a), row(y_b)], [o_row(dm, BF16)], rows)
    hres = matmul("mm_out", merged, w_o, "nn", F32, add=x2, tn=1024)
    hn, = ew_fwd("rms2", f_rms, [row(hres), par(g2)], [o_row(dm, BF16)], rows)
    up_blocks = w_u.shape[0]
    act, relu2 = matmul("mm_up", hn, w_u, "nn", F32, col_blocks=up_blocks, out_dtypes=[F32, BF16],
                        epilogue=lambda r: [r, jnp.maximum(r, 0.0) * jnp.maximum(r, 0.0)])
    def loss_tail(r, h_tile, t_tile):
        d = (r + h_tile) - t_tile
        e = (0.5 / dm) * (d * d)
        part = e.reshape(e.shape[0] // 8, 8, e.shape[1]).sum(axis=0)
        part = sum(part[:, t * LANES:(t + 1) * LANES] for t in range(e.shape[1] // LANES))
        g = d * (1.0 / dm)
        return [g, g, part]

    dout, dout16, loss_acc = matmul("mm_down", relu2, w_d, "nn", F32, extras=[hres, tgt2], epilogue=loss_tail,
                                    out_dtypes=[F32, BF16, F32], tile_sums=True)

    d_act = matmul("mm_d_act", dout16, w_d, "nt", BF16, extras=[act], epilogue=lambda r, a: [2.0 * jnp.maximum(a, 0.0) * r])
    dw_d = matmul("mm_dw_down", relu2, dout16, "tn", F32, tn=1024)
    dw_u = matmul("mm_dw_up", hn, d_act, "tn", F32, col_blocks=up_blocks)
    d_hn = matmul("mm_d_hn", d_act, w_u, "nt", F32, col_blocks=up_blocks)
    dh, dh16, dg2 = ew_bwd("rms2_b", f_rms, [row(hres), par(g2)], [(row(d_hn),)], [row(dout)],
                           lambda g, e: [g[0] + e[0], g[0] + e[0], g[1]],
                           [((rows, dm), "row", dm, F32, None), ((rows, dm), "row", dm, BF16, None), ((1, dm), "par", dm, F32, "all")], rows)
    d_merged = matmul("mm_d_merged", dh16, w_o, "nt", F32, tn=1024)
    dw_o = matmul("mm_dw_out", merged, dh16, "tn", F32, tn=1024)
    seg16 = ((rows, dm), "row", dm, BF16, None)
    d_ga16, d_gb16, d_ya16, d_yb16 = ew_bwd("merge_b", f_merge, gates + [row(y_a), row(y_b)], [(row(d_merged),)], [],
                                            lambda g, e: list(g), [seg16] * 4, rows)
    dp_a = matmul("mm_dp_a", ya_in, d_ya16, "tn", F32, tn=1024)
    d_ya_in = matmul("mm_d_ya_in", d_ya16, p_a, "nt", F32, tn=1024)
    dp_b = matmul("mm_dp_b", o_b16, d_yb16, "tn", F32, tn=1024)
    d_ob = matmul("mm_d_ob", d_yb16, p_b, "nt", F32, tn=1024)
    token = hooks.reduce_start(dict(p_a=dp_a, p_b=dp_b, w_o=dw_o, w_u=dw_u, w_d=dw_d))
    gdn_ng_t = gdn_ng + token[0, 0]
    h32 = ((rows, wide), "rowh", LANES, F32, None)
    h16 = ((rows, wide), "rowh", LANES, BF16, None)
    gain = ((1, LANES), "par", LANES, F32, "all")
    d_oa, d_z16, d_gdn_ng = ew_bwd("gdn_post_b", f_post, [rowh(o_a), rowh(proj, 3 * NH), par(gdn_ng_t)], [(rowh(d_ya_in),)], [],
                                   lambda g, e: list(g), [h32, h16, gain], rows, NH)
    dval, dkc, dat, dqd, dkd, dgb_b = gdn_b_bwd(val, kcum, attn, qdec, kdec, gb, snaps, d_oa, nseq, seq)
    d_cq, d_ck, d_cv, d_gb, d_bb = gdn_a_bwd(conv["q"], conv["k"], conv["v"], gb, bb, t_inv, dval, dkc, dat, dqd, dkd, dgb_b, rows)
    token = hooks.reduce_exchange(d_cq)
    conv_w_t = conv_w + token[0, 0]
    d_pre, d_conv = {}, {}
    tap = ((4, wide), "parh", LANES, F32, "inner")
    for mode, off, ctg in (("q", 0, d_cq), ("k", NH, d_ck), ("v", 2 * NH, d_cv)):
        d_pre[mode], d_conv[mode] = ew_bwd(f"conv_{mode}_b", make_f_conv(mode), [rowh(proj, off), parh(conv_w_t, off)],
                                           [(rowh(ctg),)], [], lambda g, e: list(g), [h16, tap], rows, NH, seq, "hi", CONV_HEADS)
    delta, = ew_fwd("fox_delta", f_delta, [rowh(d_ob), rowh(o_b)], [o_rowh(F32)], rows, NH, after=[token])
    d_fqn, d_cq_b = fox_dq(fqn, fkn, proj, ct, d_ob, lse, delta, nseq, seq)
    d_fkn, d_fv16, d_ck_b = fox_dkv(fqn, fkn, proj, cb, d_ob, lse, delta, nseq, seq)
    token = hooks.reduce_finish(d_fkn)
    qn_g_t, kn_g_t = qn_g + token[0, 0], kn_g + token[0, 0]
    d_fq16, d_qn_g = ew_bwd("fox_qn_b", f_rms, [rowh(proj, FOX_Q), par(qn_g_t)], [(rowh(d_fqn),)], [], lambda g, e: list(g),
                            [h16, gain], rows, NH)
    d_fk16, d_kn_g = ew_bwd("fox_kn_b", f_rms, [rowh(proj, FOX_K), par(kn_g_t)], [(rowh(d_fkn),)], [], lambda g, e: list(g),
                            [h16, gain], rows, NH)
    narrow = ((rows, LANES), "row", LANES, F32, None)
    d_so, d_cs = ew_bwd("bcast_b", f_bcast, [row(so), row(cs)], [(rowh(d_gb),), (rowh(d_bb),), (rowh(d_cq_b), rowh(d_ck_b))], [],
                        lambda g, e: list(g), [narrow, narrow], rows, NH)
    d_logf = cumsum_time("cumsum_b", d_cs, nseq, seq, True)
    vec = ((1, LANES), "par", LANES, F32, "all")
    d_sp16, d_p1, d_p2 = ew_bwd("small_b", f_small, [row(sp), par(p1), par(p2)], [(row(d_so), row(d_logf))], [],
                                lambda g, e: list(g), [((rows, LANES), "row", LANES, BF16, None), vec, vec], rows)
    d_proj16 = jnp.concatenate([d_pre["q"], d_pre["k"], d_pre["v"], d_z16, d_fq16, d_fk16, d_fv16, d_ga16, d_gb16], axis=1)
    dw_main = matmul("mm_dw_main", d_proj16, u, "tn", F32)
    dw_small = matmul("mm_dw_small", d_sp16, u, "tn", F32)
    wt_small_t = wt_small + hooks.input_grad_start(dw_main, dw_small)[0, 0].astype(BF16)
    d_u = matmul("mm_d_u_small", d_sp16, wt_small_t, "nn", F32)
    d_u = matmul("mm_d_u_first", d_proj16, wt_main, "nn", F32, add=d_u, k_part=(0, 2))
    d_u = matmul("mm_d_u_second", d_proj16, wt_main, "nn", F32, add=d_u, k_part=(1, 2), after=[hooks.input_grad_exchange(d_u)])
    dx, dg1 = ew_bwd("rms1_b", f_rms, [row(x2), par(g1)], [(row(d_u),)], [row(dh)], lambda g, e: [g[0] + e[0], g[1]],
                     [((rows, dm), "row", dm, F32, None), ((1, dm), "par", dm, F32, "all")], rows)
    d_conv_w = jnp.concatenate([d_conv["q"], d_conv["k"], d_conv["v"]], axis=1)
    return dict(loss_acc=loss_acc, dx=dx, g1=dg1, g2=dg2, gdn_ng=d_gdn_ng, qn=d_qn_g, kn=d_kn_g, p1=d_p1, p2=d_p2,
                conv=d_conv_w, w_main=dw_main, w_small=dw_small, p_a=dp_a, p_b=dp_b, w_o=dw_o, w_u=dw_u, w_d=dw_d)


_W = NH * LANES
_A0, _A1 = 4 * _W, 4 * _W + 2 * NH
_B0, _B1 = _A1 + 3 * _W, _A1 + 3 * _W + NH


def _split_w_in(full_t):
    main = jnp.concatenate([full_t[:_A0], full_t[_A1:_B0], full_t[_B1:]], axis=0)
    small = jnp.concatenate([full_t[_A0:_A1], full_t[_B0:_B1], jnp.zeros((LANES - 3 * NH, full_t.shape[1]), full_t.dtype)], axis=0)
    return main, small


def _join_w_in(main, small):
    return jnp.concatenate([main[:_A0], small[:2 * NH], main[_A0:_A0 + 3 * _W], small[2 * NH:3 * NH], main[_A0 + 3 * _W:]], axis=0)


def _lanes(v, at=0):
    return jnp.pad(v.reshape(1, -1), ((0, 0), (at, LANES - at - v.size)))


def kernel(x, norm_mix_g, w_in, gdn_conv_w, gdn_a_log, gdn_dt_bias, gdn_norm_g, fox_q_norm_g, fox_k_norm_g, fox_f_bias, w_proj_gdn, w_proj_fox, w_out, norm_mlp_g, w_up, w_down, loss_target, m_norm_mix_g, m_w_in, m_gdn_conv_w, m_gdn_a_log, m_gdn_dt_bias, m_gdn_norm_g, m_fox_q_norm_g, m_fox_k_norm_g, m_fox_f_bias, m_w_proj_gdn, m_w_proj_fox, m_w_out, m_norm_mlp_g, m_w_up, m_w_down, v_norm_mix_g, v_w_in, v_gdn_conv_w, v_gdn_a_log, v_gdn_dt_bias, v_gdn_norm_g, v_fox_q_norm_g, v_fox_k_norm_g, v_fox_f_bias, v_w_proj_gdn, v_w_proj_fox, v_w_out, v_norm_mlp_g, v_w_up, v_w_down):
    nseq, seq, dm = x.shape
    rows = nseq * seq
    xi, yi, ci = lax.axis_index("x"), lax.axis_index("y"), lax.axis_index("c")
    chip = 2 * xi + yi
    conv_cols = gdn_conv_w.shape[2]

    tr = lambda a: jnp.swapaxes(a[0], 0, 1)
    big = [tr(w_in), w_proj_gdn[0], w_proj_fox[0], w_out[0], w_up[0], w_down[0]]
    axes = [1, 0, 0, 0, 0, 0]
    big16 = [w.astype(BF16) for w in big]
    conv_slot = jnp.zeros((4, 4, conv_cols), F32).at[:, chip].set(jnp.where(ci == 0, gdn_conv_w[0], 0.0))
    conv_full = all_reduce_small("gather_conv", conv_slot.reshape(-1, LANES)).reshape(4, 4 * conv_cols)
    got_in = gather_ring(big16[0])
    wt_main, wt_small = _split_w_in(got_in.reshape(-1, dm))
    core, chip_no = ci.reshape(1).astype(jnp.int32), chip.reshape(1).astype(jnp.int32)
    gather = split_gather(big16[1:])
    token = gather.start([got_in, conv_full])

    class Hooks:
        def late_weights(self, after):
            g_pa, g_pb, g_wo, w_u, g_wd = gather.wait(after)
            return (*(g.reshape(-1, dm) for g in (g_pa, g_pb, g_wo)), w_u, g_wd.reshape(-1, dm))

        def reduce_start(self, grads):
            blocks = [grads["p_a"].reshape(4, -1, dm), grads["p_b"].reshape(4, -1, dm), grads["w_o"].reshape(4, -1, dm),
                      grads["w_u"], grads["w_d"].reshape(4, -1, dm)]
            self.swap = split_pair_swap("pair_swap_late", blocks, axes[1:])
            return self.swap.start([])

        def reduce_exchange(self, after):
            swapped = self.swap.wait(after)
            self.exchange = split_chip_exchange("chip_exchange_late", add_pair("add_pair_late", self.swap.srcs, swapped, core, axes[1:]))
            return self.exchange.start([])

        def reduce_finish(self, after):
            slots = self.exchange.wait(after)
            self.send = split_pair_send(add_chips("add_chips_late", slots, self.exchange.srcs, chip_no, axes[1:]))
            return self.send.start([])

        def input_grad_start(self, dw_main, dw_small):
            self.in_swap = split_pair_swap("pair_swap_in", [_join_w_in(dw_main, dw_small).reshape(4, -1, dm)], axes[:1])
            return self.in_swap.start([])

        def input_grad_exchange(self, after):
            swapped = self.in_swap.wait(after)
            self.in_exchange = split_chip_exchange("chip_exchange_in", add_pair("add_pair_in", self.in_swap.srcs, swapped, core, axes[:1]))
            return self.in_exchange.start([])

    hooks = Hooks()
    p1 = _lanes(gdn_dt_bias[0]) + _lanes(fox_f_bias[0], 2 * NH)
    p2 = _lanes(gdn_a_log[0])

    g = local_step(x.reshape(rows, dm), loss_target.reshape(rows, dm), norm_mix_g + token[0, 0], norm_mlp_g, gdn_norm_g,
                   fox_q_norm_g, fox_k_norm_g, p1, p2, conv_full, wt_main, wt_small, hooks, nseq, seq)

    others = hooks.send.wait(g["dx"])
    big_m = [tr(m_w_in), m_w_proj_gdn[0], m_w_proj_fox[0], m_w_out[0], m_w_up[0], m_w_down[0]]
    big_v = [tr(v_w_in), v_w_proj_gdn[0], v_w_proj_fox[0], v_w_out[0], v_w_up[0], v_w_down[0]]
    names = ["w_in", "w_proj_gdn", "w_proj_fox", "w_out", "w_up", "w_down"]
    big_res, big_grad = {}, {}
    for i in range(1, len(names)):
        big_grad[names[i]], *big_res[names[i]] = adamw_halves(f"adamw_{names[i]}", big[i], hooks.send.srcs[i - 1], others[i - 1],
                                                              big_m[i], big_v[i], core, axes[i])
    slots = hooks.in_exchange.wait(big_res[names[-1]][0])
    mine = add_chips("add_chips_in", slots, hooks.in_exchange.srcs, chip_no, axes[:1])
    res = adamw_halves("adamw_w_in", big[0], mine[0], pair_send(mine)[0], big_m[0], big_v[0], core, axes[0])
    big_grad["w_in"], *big_res["w_in"] = [jnp.swapaxes(r, 0, 1) for r in res]

    small_parts = [g["loss_acc"], g["g1"].reshape(8, LANES), g["g2"].reshape(8, LANES), g["gdn_ng"], g["qn"], g["kn"], g["p1"], g["p2"],
                   g["conv"].reshape(-1, LANES)]
    tiled = [jnp.pad(p, ((0, -p.shape[0] % 8), (0, 0))) for p in small_parts]
    red = all_reduce_small("reduce_small", jnp.concatenate(tiled, axis=0), slots)
    pos, red_parts = 0, []
    for p, t in zip(small_parts, tiled):
        red_parts.append(red[pos:pos + p.shape[0]])
        pos += t.shape[0]
    r_loss, r_g1, r_g2, r_gdn_ng, r_qn, r_kn, r_p1, r_p2, r_conv = red_parts
    loss = jnp.sum(r_loss)
    g_conv = lax.dynamic_slice_in_dim(r_conv.reshape(4, 4, conv_cols), chip, 1, axis=1).reshape(4, conv_cols)
    small_grads = [r_g1.reshape(1, dm), r_p2[:, :NH], r_p1[:, :NH], r_gdn_ng, r_qn, r_kn, r_p1[:, 2 * NH:3 * NH], r_g2.reshape(1, dm)]
    small_w = [norm_mix_g, gdn_a_log, gdn_dt_bias, gdn_norm_g, fox_q_norm_g, fox_k_norm_g, fox_f_bias, norm_mlp_g]
    small_m = [m_norm_mix_g, m_gdn_a_log, m_gdn_dt_bias, m_gdn_norm_g, m_fox_q_norm_g, m_fox_k_norm_g, m_fox_f_bias, m_norm_mlp_g]
    small_v = [v_norm_mix_g, v_gdn_a_log, v_gdn_dt_bias, v_gdn_norm_g, v_fox_q_norm_g, v_fox_k_norm_g, v_fox_f_bias, v_norm_mlp_g]

    def pack(parts):
        flat = jnp.concatenate([jnp.pad(p.reshape(-1), (0, -p.size % LANES)) for p in parts])
        return jnp.pad(flat, (0, -flat.size % (8 * LANES))).reshape(-1, LANES)

    packed = adamw("adamw_small", pack(small_w + [gdn_conv_w[0]]), pack(small_grads + [g_conv]),
                   pack(small_m + [m_gdn_conv_w[0]]), pack(small_v + [v_gdn_conv_w[0]]))

    def unpack(flat2d):
        flat, pos, res = flat2d.reshape(-1), 0, []
        for p in small_w + [gdn_conv_w[0]]:
            res.append(flat[pos:pos + p.size].reshape(p.shape))
            pos += p.size + (-p.size % LANES)
        return res

    s_delta, s_m, s_v = (unpack(a) for a in packed)

    order = ["norm_mix_g", "w_in", "gdn_conv_w", "gdn_a_log", "gdn_dt_bias", "gdn_norm_g", "fox_q_norm_g", "fox_k_norm_g",
             "fox_f_bias", "w_proj_gdn", "w_proj_fox", "w_out", "norm_mlp_g", "w_up", "w_down"]
    small_names = ["norm_mix_g", "gdn_a_log", "gdn_dt_bias", "gdn_norm_g", "fox_q_norm_g", "fox_k_norm_g", "fox_f_bias", "norm_mlp_g",
                   "gdn_conv_w"]
    small_idx = {nm: i for i, nm in enumerate(small_names)}
    shapes = dict(zip(order, (a.shape for a in (norm_mix_g, w_in, gdn_conv_w, gdn_a_log, gdn_dt_bias, gdn_norm_g, fox_q_norm_g,
                                                 fox_k_norm_g, fox_f_bias, w_proj_gdn, w_proj_fox, w_out, norm_mlp_g, w_up, w_down))))
    grads_out, delta_out, m_out, v_out = [], [], [], []
    for nm in order:
        if nm in big_res:
            d, mm, vv = big_res[nm]
            gr = big_grad[nm]
        else:
            i = small_idx[nm]
            gr = (small_grads + [g_conv])[i]
            d, mm, vv = s_delta[i], s_m[i], s_v[i]
        for lst, val in ((grads_out, gr), (delta_out, d), (m_out, mm), (v_out, vv)):
            lst.append(val.reshape(shapes[nm]))
    return (loss, g["dx"].reshape(x.shape), *grads_out, *delta_out, *m_out, *v_out)
```

```python
import functools

import jax
import jax.numpy as jnp
from jax import lax
from jax.experimental import pallas as pl
from jax.experimental.pallas import tpu as pltpu

F32 = jnp.float32
BF16 = jnp.bfloat16
LANES = 128
NH = 8
EPS = 1e-6
GDN_CHUNK = 64
GDN_ROWS = 256
GDN_BASE = 16
ROW_TILE = 512
CONV_HEADS = 2
ATT_TILE = 512
NEG = -1e30
VMEM_LIMIT_BYTES = 58 * 1024 * 1024
LO = lax.Precision.DEFAULT
MESH = pl.DeviceIdType.MESH
ANY = pl.BlockSpec(memory_space=pl.ANY)

ADAM_LR, ADAM_B1, ADAM_B2, ADAM_EPS, ADAM_WD, ADAM_STEP = 0.001, 0.9, 0.999, 1e-08, 0.01, 10


def _params(n_grid):
    return pltpu.CompilerParams(dimension_semantics=("arbitrary",) * n_grid,
                                vmem_limit_bytes=VMEM_LIMIT_BYTES)


def _dot(a, b, dims, precision=None):
    dn = {"nn": (((1,), (0,)), ((), ())), "nt": (((1,), (1,)), ((), ())), "tn": (((0,), (0,)), ((), ()))}[dims]
    return lax.dot_general(a, b, dn, precision=precision, preferred_element_type=F32)


def _iota(shape, dim):
    return lax.broadcasted_iota(jnp.int32, shape, dim)


def _split(x, parts):
    out = []
    for _ in range(parts - 1):
        hi = x.astype(BF16)
        out.append(hi)
        x = x - hi.astype(F32)
    return out + [x.astype(BF16)]


def _dot_mask(mask, b, dims, terms=3):
    m16 = mask.astype(BF16)
    acc = None
    for part in reversed(_split(b, terms)):
        prod = _dot(m16, part, dims)
        acc = prod if acc is None else acc + prod
    return acc


@jax.custom_vjp
def mm_mask(mask, b):
    return _dot_mask(mask, b, "nn", 2)


mm_mask.defvjp(lambda mask, b: (_dot_mask(mask, b, "nn", 2), mask),
               lambda mask, g: (jnp.zeros_like(mask), _dot_mask(mask, g, "tn", 2)))


def matmul(name, a, b, dims, out_dtype, add=None, tm=1024, tn=1024, tk=512, col_blocks=None,
           extras=(), epilogue=None, out_dtypes=None, k_part=None, after=(), tile_sums=False):
    if col_blocks and dims != "tn":
        nb, b_rows, bw = b.shape
        b_shape = (b_rows, nb * bw)
    else:
        b_shape = b.shape
    if dims == "nn":
        (m, k), (_, n) = a.shape, b_shape
    elif dims == "nt":
        (m, k), (n, _) = a.shape, b_shape
    else:
        (k, m), (_, n) = a.shape, b_shape
    k_span = k // (k_part[1] if k_part else 1)
    if col_blocks and dims == "nt":
        k_span = min(k_span, bw)
    tk = k if k <= 1024 else max(t for t in (2048, 1536, 1024, 512, tk) if k_span % t == 0)
    tm, tn, tk = min(tm, m), min(tn, n), min(tk, k)
    assert m % tm == 0 and n % tn == 0 and k % tk == 0, (name, m, n, k)
    k0, nk = (0, k // tk) if k_part is None else (k_part[0] * (k // tk // k_part[1]), k // tk // k_part[1])
    assert k_part is None or (dims == "nn" and not col_blocks and (k // tk) % k_part[1] == 0)
    a_spec = pl.BlockSpec((tk, tm), lambda i, j, kk: (kk, i)) if dims == "tn" else pl.BlockSpec((tm, tk), lambda i, j, kk: (i, kk + k0))
    b_spec = pl.BlockSpec((tn, tk), lambda i, j, kk: (j, kk)) if dims == "nt" else pl.BlockSpec((tk, tn), lambda i, j, kk: (kk + k0, j))
    o_spec = pl.BlockSpec((tm, tn), lambda i, j, kk: (i, j))
    out_shape = (m, n)
    if col_blocks and dims == "nn":
        per = bw // tn
        assert bw % tn == 0
        b_spec = pl.BlockSpec((None, tk, tn), lambda i, j, kk: (j // per, kk, j % per))
    elif col_blocks and dims == "nt":
        per = bw // tk
        assert bw % tk == 0
        b_spec = pl.BlockSpec((None, tn, tk), lambda i, j, kk: (kk // per, j, kk % per))
    elif col_blocks:
        bw = n // col_blocks
        per = bw // tn
        assert bw % tn == 0 and add is None
        o_spec = pl.BlockSpec((None, tm, tn), lambda i, j, kk: (j // per, i, j % per))
        out_shape = (col_blocks, m, bw)
    extras = list(extras) + ([add] if add is not None else [])
    if add is not None:
        assert epilogue is None
        epilogue = lambda r, *e: [r + e[-1]]
    out_dtypes = [out_dtype] if epilogue is None or out_dtypes is None else list(out_dtypes)
    n_ex, n_out = len(extras), len(out_dtypes)

    def body(*refs):
        a_ref, b_ref = refs[0], refs[1]
        ex_refs, o_refs = refs[2:2 + n_ex], refs[2 + n_ex + len(after):2 + n_ex + len(after) + n_out]

        def finish(r):
            res = [r] if epilogue is None else epilogue(r, *[e[...] for e in ex_refs])
            for o_ref, v in zip(o_refs, res):
                o_ref[...] = v.astype(o_ref.dtype)

        if nk == 1:
            finish(_dot(a_ref[...], b_ref[...], dims))
            return
        acc_ref = refs[-1]
        kk = pl.program_id(2)

        @pl.when(kk == 0)
        def _():
            acc_ref[...] = jnp.zeros_like(acc_ref)

        acc_ref[...] += _dot(a_ref[...], b_ref[...], dims)

        @pl.when(kk == nk - 1)
        def _():
            finish(acc_ref[...])

    out_specs = [o_spec] * n_out
    out_shapes = [jax.ShapeDtypeStruct(out_shape, dt) for dt in out_dtypes]
    if tile_sums:
        out_specs[-1] = pl.BlockSpec((8, LANES), lambda i, j, kk: (i, j))
        out_shapes[-1] = jax.ShapeDtypeStruct((8 * (m // tm), LANES * (n // tn)), out_dtypes[-1])
    res = pl.pallas_call(
        body, name=name, grid=(m // tm, n // tn, nk), in_specs=[a_spec, b_spec] + [o_spec] * n_ex + [ANY] * len(after),
        out_specs=out_specs, out_shape=out_shapes,
        scratch_shapes=[pltpu.VMEM((tm, tn), F32)] if nk > 1 else [], compiler_params=_params(3),
    )(a, b, *extras, *after)
    return res[0] if n_out == 1 else res


def _ew_spec(kind, off, width, tb, hp, order, shape=None):
    def ih(g0, g1):
        return (g0, g1) if order == "ih" else (g1, g0)

    assert off % hp == 0 or kind in ("row", "par")
    if kind == "row":
        return pl.BlockSpec((tb, width), lambda g0, g1: (ih(g0, g1)[0], off))
    if kind == "rowh":
        return pl.BlockSpec((tb, hp * width), lambda g0, g1: (ih(g0, g1)[0], ih(g0, g1)[1] + off // hp))
    if kind == "par":
        return pl.BlockSpec(shape, lambda g0, g1: (0, 0))
    if kind == "parh":
        return pl.BlockSpec((shape[0], hp * width), lambda g0, g1: (0, ih(g0, g1)[1] + off // hp))
    raise ValueError(kind)


def _ew_grid(rows, tb, nh, hp, order):
    assert nh % hp == 0 and rows % tb == 0
    return (rows // tb, nh // hp) if order == "ih" else (nh // hp, rows // tb)


def _ew_load(ref, kind, width, hh):
    if kind in ("row", "par"):
        return ref[...].astype(F32)
    return ref[:, hh * width:(hh + 1) * width].astype(F32)


def ew_fwd(name, f, ins, outs, rows, nh=1, tb=ROW_TILE, order="ih", hp=None, after=()):
    hp = nh if hp is None else hp
    n_in = len(ins)

    def body(*refs):
        hb = pl.program_id(1) if order == "ih" else pl.program_id(0)
        for hh in range(hp):
            h = hh if hp == nh else hb * hp + hh
            vals = [_ew_load(r, kd, w, hh) for r, (_, kd, _, w) in zip(refs[:n_in], ins)]
            res = f(h, *vals)
            for r, v, (_, kd, w, _) in zip(refs[n_in + len(after):], res, outs):
                if kd == "row":
                    assert hp == 1
                    r[...] = v.astype(r.dtype)
                else:
                    r[:, hh * w:(hh + 1) * w] = v.astype(r.dtype)

    in_specs = [_ew_spec(kd, off, w, tb, hp, order, a.shape) for (a, kd, off, w) in ins]
    out_specs = [_ew_spec(kd, 0, w, tb, hp, order) for (_, kd, w, _) in outs]
    out_shape = [jax.ShapeDtypeStruct((rows, tw), dt) for (tw, _, _, dt) in outs]
    return pl.pallas_call(
        body, name=name, grid=_ew_grid(rows, tb, nh, hp, order), in_specs=in_specs + [ANY] * len(after), out_specs=out_specs,
        out_shape=out_shape, compiler_params=_params(2),
    )(*[a for (a, _, _, _) in ins], *after)


def ew_bwd(name, f, ins, cts, extras, emit, outs, rows, nh=1, tb=ROW_TILE, order="ih", hp=None):
    hp = nh if hp is None else hp
    n_in = len(ins)
    flat_cts = [d for group in cts for d in group]
    n_ct, n_ex = len(flat_cts), len(extras)

    def body(*refs):
        g0, g1 = pl.program_id(0), pl.program_id(1)
        hb = g1 if order == "ih" else g0
        out_refs = refs[n_in + n_ct + n_ex:]
        shared = [None] * len(outs)

        def store(r, v, first, sl=None):
            def put(val, add):
                if sl is None:
                    r[...] = (r[...] + val if add else val).astype(r.dtype)
                else:
                    r[:, sl] = (r[:, sl] + val if add else val).astype(r.dtype)

            if first is None:
                put(v, False)
            else:
                pl.when(first)(lambda: put(v, False))
                pl.when(jnp.logical_not(first))(lambda: put(v, True))

        for hh in range(hp):
            h = hh if hp == nh else hb * hp + hh
            vals = [_ew_load(r, kd, w, hh) for r, (_, kd, _, w) in zip(refs[:n_in], ins)]
            ct_refs = list(zip(refs[n_in:n_in + n_ct], flat_cts))
            ct_vals, pos = [], 0
            for group in cts:
                v = None
                for r, (_, kd, _, w) in ct_refs[pos:pos + len(group)]:
                    t = _ew_load(r, kd, w, hh)
                    v = t if v is None else v + t
                pos += len(group)
                ct_vals.append(v)
            ex_vals = [_ew_load(r, kd, w, hh) for r, (_, kd, _, w) in zip(refs[n_in + n_ct:n_in + n_ct + n_ex], extras)]
            _, vjp = jax.vjp(lambda *a: f(h, *a), *vals)
            res = emit(vjp(tuple(ct_vals)), ex_vals)
            for idx, (r, v, (_, kd, w, _, acc)) in enumerate(zip(out_refs, res, outs)):
                if kd in ("row", "par"):
                    shared[idx] = v if shared[idx] is None else shared[idx] + v
                else:
                    store(r, v, (g1 == 0) if acc == "inner" else None, slice(hh * w, (hh + 1) * w))
        for idx, (r, (_, kd, _, _, acc)) in enumerate(zip(out_refs, outs)):
            if kd in ("row", "par"):
                assert acc == "all" or hp == nh
                store(r, shared[idx], jnp.logical_and(g0 == 0, g1 == 0) if acc == "all" else None)

    operands = list(ins) + flat_cts + list(extras)
    in_specs = [_ew_spec(kd, off, w, tb, hp, order, a.shape) for (a, kd, off, w) in operands]
    out_specs = [_ew_spec(kd, 0, w, tb, hp, order, shp) for (shp, kd, w, _, _) in outs]
    out_shape = [jax.ShapeDtypeStruct(shp, dt) for (shp, _, _, dt, _) in outs]
    return pl.pallas_call(
        body, name=name, grid=_ew_grid(rows, tb, nh, hp, order), in_specs=in_specs, out_specs=out_specs,
        out_shape=out_shape, compiler_params=_params(2),
    )(*[a for (a, _, _, _) in operands])


def f_rms(h, x, g):
    r = lax.rsqrt(jnp.mean(x * x, axis=-1, keepdims=True) + EPS)
    return (x * r * g,)


def _softplus(z):
    return jnp.maximum(z, 0.0) + jnp.log1p(jnp.exp(-jnp.abs(z)))


def f_small(h, sp, p1, p2):
    lane = _iota(sp.shape, 1)
    z = sp + p1
    g = -jnp.exp(p2) * _softplus(z)
    beta = jax.nn.sigmoid(z)
    logf = -_softplus(-z)
    return (jnp.where(lane < NH, g, jnp.where(lane < 2 * NH, beta, jnp.where(lane < 3 * NH, logf, 0.0))),)


def _pick(x, lane_id):
    lane = _iota(x.shape, 1)
    col = jnp.sum(jnp.where(lane == lane_id, x, 0.0), axis=1, keepdims=True)
    return jnp.broadcast_to(col, x.shape)


def f_bcast(h, so, cs):
    return _pick(so, h), _pick(so, h + NH), _pick(cs, h + 2 * NH)


def _shift_down(s):
    def down(x):
        r = pltpu.roll(x, s, 0)
        head = jnp.where(_iota((8, x.shape[1]), 0) >= s, r[:8], 0.0)
        return jnp.concatenate([head, r[8:]], axis=0)

    def up(g):
        n = g.shape[0]
        r = pltpu.roll(g, n - s, 0)
        tail = jnp.where(_iota((8, g.shape[1]), 0) < 8 - s, r[n - 8:], 0.0)
        return jnp.concatenate([r[:n - 8], tail], axis=0)

    @jax.custom_vjp
    def shift(x):
        return down(x)

    shift.defvjp(lambda x: (down(x), None), lambda _, g: (up(g),))
    return shift


def _silu(x):
    return x * jax.nn.sigmoid(x)


def make_f_conv(mode):
    sh1, sh2, sh3 = _shift_down(1), _shift_down(2), _shift_down(3)

    def f(h, x, w):
        sub = _iota(w.shape, 0)

        def tap(i):
            return jnp.sum(jnp.where(sub == i, w, 0.0), axis=0, keepdims=True)

        y = sh3(x) * tap(0)
        y = y + sh2(x) * tap(1)
        y = y + sh1(x) * tap(2)
        y = y + x * tap(3)
        s = _silu(y)
        if mode == "v":
            return (s,)
        n = s * lax.rsqrt(jnp.sum(s * s, axis=-1, keepdims=True) + EPS)
        if mode == "q":
            n = n * (LANES ** -0.5)
        return (n,)

    return f


def f_post(h, o, z, g):
    r = lax.rsqrt(jnp.mean(o * o, axis=-1, keepdims=True) + EPS)
    return (o * r * g * _silu(z),)


def f_merge(h, ga, gb, ya, yb):
    return (jax.nn.sigmoid(ga) * ya + jax.nn.sigmoid(gb) * yb,)


def f_delta(h, do, o):
    return (jnp.broadcast_to(jnp.sum(do * o, axis=1, keepdims=True), o.shape),)


def cumsum_time(name, x, nseq, seq, reverse):
    nb = seq // LANES

    def body(x_ref, o_ref):
        r, c = _iota((LANES, LANES), 0), _iota((LANES, LANES), 1)
        tri = jnp.where((r <= c) if reverse else (r >= c), 1.0, 0.0).astype(F32)
        carry = jnp.zeros((1, LANES), F32)
        for b in (range(nb - 1, -1, -1) if reverse else range(nb)):
            blk = x_ref[b * LANES:(b + 1) * LANES, :]
            o_ref[b * LANES:(b + 1) * LANES, :] = _dot_mask(tri, blk, "nn") + carry
            carry = carry + jnp.sum(blk, axis=0, keepdims=True)

    spec = pl.BlockSpec((seq, LANES), lambda s: (s, 0))
    return pl.pallas_call(body, name=name, grid=(nseq,), in_specs=[spec], out_specs=spec,
                          out_shape=jax.ShapeDtypeStruct(x.shape, F32), compiler_params=_params(1))(x)


def transpose_time(name, x, nseq, seq):
    def body(x_ref, o_ref):
        o_ref[...] = x_ref[...].T

    return pl.pallas_call(
        body, name=name, grid=(nseq,), in_specs=[pl.BlockSpec((seq, LANES), lambda s: (s, 0))],
        out_specs=pl.BlockSpec((LANES, seq), lambda s: (s, 0)),
        out_shape=jax.ShapeDtypeStruct((nseq * LANES, seq), F32), compiler_params=_params(1))(x)


def _gdn_masks():
    n = GDN_ROWS
    r, c = _iota((n, n), 0), _iota((n, n), 1)
    shift = GDN_CHUNK.bit_length() - 1
    same = lax.shift_right_logical(r, shift) == lax.shift_right_logical(c, shift)
    return r, c, same


def _each(fn, *lists):
    return [fn(*xs) for xs in zip(*lists)]


def _gdn_decay(gbs):
    r, c, same = _gdn_masks()
    seg_tril = jnp.where(jnp.logical_and(same, r >= c), 1.0, 0.0).astype(F32)
    g_cum = _each(lambda gb: mm_mask(seg_tril, gb), gbs)
    lane0 = _iota(gbs[0].shape, 1) == 0
    g_col = _each(lambda g: jnp.sum(jnp.where(lane0, g, 0.0), axis=1, keepdims=True), g_cum)
    g_row = _each(lambda g: jnp.sum(jnp.where(r == c, jnp.broadcast_to(g, (GDN_ROWS, GDN_ROWS)), 0.0), axis=0, keepdims=True), g_col)
    return g_cum, _each(lambda a, b: a - b, g_col, g_row)


def gdn_a_mats(ks, bbs, diff):
    r, c, same = _gdn_masks()
    strict = jnp.logical_and(same, r > c)
    lane0 = _iota(bbs[0].shape, 1) == 0
    beta_col = _each(lambda bb: jnp.sum(jnp.where(lane0, bb, 0.0), axis=1, keepdims=True), bbs)
    kk = _each(lambda k: _dot(k, k, "nt", LO), ks)
    return _each(lambda b, x, d: jnp.where(strict, b * x * jnp.exp(jnp.where(strict, d, 0.0)), 0.0), beta_col, kk, diff)


@jax.custom_vjp
def saved_inverse(a, t_corr):
    return t_corr


def _saved_inverse_bwd(t, dt):
    left = dt + _dot(t, dt, "tn", LO)
    return -(left + _dot(left, t, "nt", LO)), jnp.zeros_like(t)


saved_inverse.defvjp(lambda a, t_corr: (t_corr, t_corr), _saved_inverse_bwd)


def gdn_block(*args):
    ts, qs, ks, vs, gbs, bbs = (list(args[i::6]) for i in range(6))
    g_cum, diff = _gdn_decay(gbs)
    ts = _each(saved_inverse, gdn_a_mats(ks, bbs, diff), ts)
    return gdn_outputs(ts, qs, ks, vs, gbs, bbs, g_cum, diff)


def gdn_outputs(ts, qs, ks, vs, gbs, bbs, g_cum, diff):
    r, c, same = _gdn_masks()
    incl = jnp.logical_and(same, r >= c)
    decay = _each(lambda d: jnp.where(incl, jnp.exp(jnp.where(incl, d, 0.0)), 0.0), diff)
    e_g = _each(jnp.exp, g_cum)
    v_beta = _each(lambda v, bb: v * bb, vs, bbs)
    k_beta = _each(lambda k, bb, e: k * bb * e, ks, bbs, e_g)
    value = _each(lambda t, x: x + _dot(t, x, "nn", LO), ts, v_beta)
    k_cum = _each(lambda t, x: x + _dot(t, x, "nn", LO), ts, k_beta)
    attn = _each(lambda q, k, d: _dot(q, k, "nt", LO) * d, qs, ks, decay)
    ones = jnp.where(same, 1.0, 0.0).astype(F32)
    g_last = _each(lambda gb: mm_mask(ones, gb), gbs)
    q_dec = _each(lambda q, e: q * e, qs, e_g)
    k_dec = _each(lambda k, gl, g: k * jnp.exp(gl - g), ks, g_last, g_cum)
    return tuple(x for head in zip(value, k_cum, attn, q_dec, k_dec) for x in head)


def tri_inverse(mats):
    n = GDN_ROWS
    r, c = _iota((n, n), 0), _iota((n, n), 1)
    shift = GDN_BASE.bit_length() - 1
    blk = lax.shift_right_logical(r, shift) == lax.shift_right_logical(c, shift)
    each = lambda fn, *lists: [fn(*xs) for xs in zip(*lists)]
    mm = lambda x, y: _dot(x, y, "nn", LO)
    d = each(lambda a: jnp.where(blk, a, 0.0), mats)
    lo = each(lambda a, dd: a - dd, mats, d)
    p = each(lambda dd: -dd, d)
    c_d = p
    for _ in range(shift - 1):
        p = each(mm, p, p)
        c_d = each(lambda cd, pp, prod: cd + pp + prod, c_d, p, each(mm, c_d, p))
    assert GDN_CHUNK // GDN_BASE == 4
    nmat = each(lambda l, prod: l + prod, lo, each(mm, c_d, lo))
    n2 = each(mm, nmat, nmat)
    c_n = each(lambda nn2, nm, prod: (nn2 - nm) - prod, n2, nmat, each(mm, nmat, n2))
    return each(lambda cn, cd, prod: cn + cd + prod, c_n, c_d, each(mm, c_n, c_d))


GDN_AHP = 4


def _gdn_a_specs():
    blk = pl.BlockSpec((GDN_ROWS, GDN_AHP * LANES), lambda i, h: (i, h))
    sq = pl.BlockSpec((GDN_ROWS, GDN_AHP * GDN_ROWS), lambda i, h: (i, h))
    return blk, sq


def _head(ref, hh):
    width = ref.shape[1] // GDN_AHP
    return ref.at[:, hh * width:(hh + 1) * width]


def gdn_a_fwd(q, k, v, gb, bb, rows):
    blk, sq = _gdn_a_specs()

    def body(q_ref, k_ref, v_ref, gb_ref, bb_ref, val_ref, kc_ref, at_ref, qd_ref, kd_ref, t_ref):
        heads = [[_head(r, hh)[...] for r in (q_ref, k_ref, v_ref, gb_ref, bb_ref)] for hh in range(GDN_AHP)]
        qs, ks, vs, gbs, bbs = (list(col) for col in zip(*heads))
        g_cum, diff = _gdn_decay(gbs)
        t_corr = tri_inverse(gdn_a_mats(ks, bbs, diff))
        res = gdn_outputs(t_corr, qs, ks, vs, gbs, bbs, g_cum, diff)
        for hh in range(GDN_AHP):
            for r, x in zip((val_ref, kc_ref, at_ref, qd_ref, kd_ref, t_ref), (*res[5 * hh:5 * hh + 5], t_corr[hh])):
                _head(r, hh)[...] = x.astype(r.dtype)

    wide = lambda dt: jax.ShapeDtypeStruct((rows, NH * LANES), dt)
    square = jax.ShapeDtypeStruct((rows, NH * GDN_ROWS), BF16)
    return pl.pallas_call(
        body, name="gdn_a_fwd", grid=(rows // GDN_ROWS, NH // GDN_AHP), in_specs=[blk] * 5,
        out_specs=[blk, blk, sq, blk, blk, sq], out_shape=[wide(F32), wide(BF16), square, wide(BF16), wide(BF16), square],
        compiler_params=_params(2))(q, k, v, gb, bb)


def gdn_a_bwd(q, k, v, gb, bb, t_inv, dval, dkc, dat, dqd, dkd, dgb_b, rows):
    blk, sq = _gdn_a_specs()

    def body(q_ref, k_ref, v_ref, gb_ref, bb_ref, t_ref, dval_ref, dkc_ref, dat_ref, dqd_ref, dkd_ref, dgbb_ref,
             dq_ref, dk_ref, dv_ref, dgb_ref, dbb_ref):
        hs = range(GDN_AHP)
        heads = [[_head(r, hh)[...] for r in (q_ref, k_ref, v_ref, gb_ref, bb_ref)] for hh in hs]
        tvs = [_head(t_ref, hh)[...].astype(F32) for hh in hs]
        _, vjp = jax.vjp(gdn_block, *[x for t, head in zip(tvs, heads) for x in (t, *head)])
        grads = vjp(tuple(_head(r, hh)[...] for hh in hs for r in (dval_ref, dkc_ref, dat_ref, dqd_ref, dkd_ref)))
        for hh in hs:
            _, dq, dk, dv, dgb, dbb = grads[6 * hh:6 * hh + 6]
            _head(dq_ref, hh)[...] = dq
            _head(dk_ref, hh)[...] = dk
            _head(dv_ref, hh)[...] = dv
            _head(dgb_ref, hh)[...] = dgb + _head(dgbb_ref, hh)[...]
            _head(dbb_ref, hh)[...] = dbb

    wide = jax.ShapeDtypeStruct((rows, NH * LANES), F32)
    return pl.pallas_call(
        body, name="gdn_a_bwd", grid=(rows // GDN_ROWS, NH // GDN_AHP),
        in_specs=[blk] * 5 + [sq, blk, blk, sq, blk, blk, blk], out_specs=[blk] * 5, out_shape=[wide] * 5,
        compiler_params=_params(2))(q, k, v, gb, bb, t_inv, dval, dkc, dat, dqd, dkd, dgb_b)


N_CH = GDN_ROWS // GDN_CHUNK


GDN_HP = 8


def gdn_chunk(c):
    def f(*args):
        val, kc, at, qd, kd, gb, s = (list(args[i::7]) for i in range(7))
        zero = jnp.zeros((GDN_CHUNK, LANES), F32)
        v_new = _each(lambda v, k, st: v - _dot(k, st, "nn", LO), val, kc, s)
        v_pad = _each(lambda v: jnp.concatenate([zero] * c + [v] + [zero] * (N_CH - 1 - c), axis=0), v_new)
        out = _each(lambda q, st, a, vp: _dot(q, st, "nn", LO) + _dot(a, vp, "nn", LO), qd, s, at, v_pad)
        dec = _each(lambda g: jnp.exp(jnp.sum(g, axis=0, keepdims=True)), gb)
        s_new = _each(lambda st, d, k, v: st * d + _dot(k, v, "tn", LO), s, dec, kd, v_new)
        return tuple(x for head in zip(out, s_new) for x in head)

    return f


def _gdn_piece(ref, hh, c):
    width = ref.shape[1] // GDN_HP
    return ref.at[c * GDN_CHUNK:(c + 1) * GDN_CHUNK, hh * width:(hh + 1) * width]


def _gdn_snap(ref, hh, c):
    row = (hh * N_CH + c) * LANES
    return ref.at[row:row + LANES, :]


def _gdn_b_specs(nb, rev):
    def blk_row(s, j):
        return s * nb + (nb - 1 - j if rev else j)

    blk = pl.BlockSpec((GDN_ROWS, GDN_HP * LANES), lambda s, hb, j: (blk_row(s, j), hb))
    sq = pl.BlockSpec((GDN_ROWS, GDN_HP * GDN_ROWS), lambda s, hb, j: (blk_row(s, j), hb))
    snap = pl.BlockSpec((GDN_HP * N_CH * LANES, LANES), lambda s, hb, j: (blk_row(s, j) * (NH // GDN_HP) + hb, 0))
    return blk, sq, snap


def gdn_b_fwd(val, kc, at, qd, kd, gb, nseq, seq):
    nb = seq // GDN_ROWS
    rows = nseq * seq
    blk, sq, snap = _gdn_b_specs(nb, False)

    def body(val_ref, kc_ref, at_ref, qd_ref, kd_ref, gb_ref, o_ref, snap_ref, s_ref):
        @pl.when(pl.program_id(2) == 0)
        def _():
            s_ref[...] = jnp.zeros_like(s_ref)

        hs = range(GDN_HP)
        states = [s_ref[hh] for hh in hs]
        for c in range(N_CH):
            for hh in hs:
                _gdn_snap(snap_ref, hh, c)[...] = states[hh]
            res = gdn_chunk(c)(*[x for hh in hs for x in (
                *[_gdn_piece(r, hh, c)[...].astype(F32) for r in (val_ref, kc_ref, at_ref, qd_ref, kd_ref, gb_ref)], states[hh])])
            for hh in hs:
                _gdn_piece(o_ref, hh, c)[...] = res[2 * hh]
            states = [res[2 * hh + 1] for hh in hs]
        for hh in hs:
            s_ref[hh] = states[hh]

    return pl.pallas_call(
        body, name="gdn_b_fwd", grid=(nseq, NH // GDN_HP, nb), in_specs=[blk, blk, sq, blk, blk, blk], out_specs=[blk, snap],
        out_shape=[jax.ShapeDtypeStruct((rows, NH * LANES), F32),
                   jax.ShapeDtypeStruct((nseq * nb * NH * N_CH * LANES, LANES), F32)],
        scratch_shapes=[pltpu.VMEM((GDN_HP, LANES, LANES), F32)], compiler_params=_params(3))(val, kc, at, qd, kd, gb)


def gdn_b_bwd(val, kc, at, qd, kd, gb, snaps, do, nseq, seq):
    nb = seq // GDN_ROWS
    rows = nseq * seq
    blk, sq, snap = _gdn_b_specs(nb, True)

    def body(val_ref, kc_ref, at_ref, qd_ref, kd_ref, gb_ref, snap_ref, do_ref,
             dval_ref, dkc_ref, dat_ref, dqd_ref, dkd_ref, dgb_ref, ds_ref):
        @pl.when(pl.program_id(2) == 0)
        def _():
            ds_ref[...] = jnp.zeros_like(ds_ref)

        hs = range(GDN_HP)
        d_states = [ds_ref[hh] for hh in hs]
        for c in reversed(range(N_CH)):
            _, vjp = jax.vjp(gdn_chunk(c), *[x for hh in hs for x in (
                *[_gdn_piece(r, hh, c)[...].astype(F32) for r in (val_ref, kc_ref, at_ref, qd_ref, kd_ref, gb_ref)],
                _gdn_snap(snap_ref, hh, c)[...])])
            grads = vjp(tuple(x for hh in hs for x in (_gdn_piece(do_ref, hh, c)[...], d_states[hh])))
            for hh in hs:
                for i, r in enumerate([dval_ref, dkc_ref, dat_ref, dqd_ref, dkd_ref, dgb_ref]):
                    _gdn_piece(r, hh, c)[...] = grads[7 * hh + i]
            d_states = [grads[7 * hh + 6] for hh in hs]
        for hh in hs:
            ds_ref[hh] = d_states[hh]

    wide = jax.ShapeDtypeStruct((rows, NH * LANES), F32)
    square = jax.ShapeDtypeStruct((rows, NH * GDN_ROWS), F32)
    return pl.pallas_call(
        body, name="gdn_b_bwd", grid=(nseq, NH // GDN_HP, nb), in_specs=[blk, blk, sq, blk, blk, blk, snap, blk],
        out_specs=[blk, blk, sq, blk, blk, blk], out_shape=[wide, wide, square, wide, wide, wide],
        scratch_shapes=[pltpu.VMEM((GDN_HP, LANES, LANES), F32)], compiler_params=_params(3))(val, kc, at, qd, kd, gb, snaps, do)


FOX_Q, FOX_K, FOX_V = 4 * NH, 5 * NH, 6 * NH
FOX_SCALE = LANES ** -0.5


def _head_row(ct_ref, h, off, width):
    blk = ct_ref[:, pl.ds(off, width)]
    return jnp.sum(jnp.where(_iota(blk.shape, 0) == h, blk, 0.0), axis=0, keepdims=True)


def _col(x):
    return jnp.max(x, axis=1, keepdims=True)


def _row(x):
    return jnp.max(x.T, axis=0, keepdims=True)


def _causal(shape, q_dim):
    return _iota(shape, q_dim) >= _iota(shape, 1 - q_dim)


FOX_HP = 4


def _fox_specs(seq, tile, n_tiles):
    tblk = pl.BlockSpec((tile, FOX_HP * LANES), lambda s, h, i: (s * n_tiles + i, h))
    vtblk = pl.BlockSpec((tile, FOX_HP * LANES), lambda s, h, i: (s * n_tiles + i, h + FOX_V // FOX_HP))
    full = pl.BlockSpec((seq, FOX_HP * LANES), lambda s, h, i: (s, h))
    vfull = pl.BlockSpec((seq, FOX_HP * LANES), lambda s, h, i: (s, h + FOX_V // FOX_HP))
    ctb = pl.BlockSpec((NH, seq), lambda s, h, i: (s * (LANES // NH) + 2, 0))
    return tblk, vtblk, full, vfull, ctb


def _lanes_of(hh):
    return slice(hh * LANES, (hh + 1) * LANES)


def fox_fwd(qn, kn, proj, ct, nseq, seq):
    tq = tk = min(ATT_TILE, seq)
    nq = seq // tq
    rows = nseq * seq
    qblk, _, full, vfull, ctb = _fox_specs(seq, tq, nq)
    hs = range(FOX_HP)

    def body(q_ref, k_ref, v_ref, ct_ref, o_ref, o16_ref, lse_ref):
        hb, i = pl.program_id(1), pl.program_id(2)
        q = [q_ref[:, _lanes_of(hh)] for hh in hs]

        def step(j, carry, diag):
            m, l, acc = (list(carry[t::3]) for t in range(3))
            off = pl.multiple_of(j * tk, tk)
            k = [k_ref[pl.ds(off, tk), _lanes_of(hh)] for hh in hs]
            v = [v_ref[pl.ds(off, tk), _lanes_of(hh)].astype(BF16) for hh in hs]
            ck = [_head_row(ct_ref, hb * FOX_HP + hh, off, tk) for hh in hs]
            s = _each(lambda qq, kk, cc: _dot(qq, kk, "nt") * FOX_SCALE - cc, q, k, ck)
            if diag:
                s = _each(lambda x: jnp.where(_causal(x.shape, 0), x, NEG), s)
            m_new = _each(lambda mm, x: jnp.maximum(mm, jnp.max(x, axis=1, keepdims=True)), m, s)
            p = _each(lambda x, mm: jnp.exp(x - mm), s, m_new)
            alpha = _each(lambda mo, mn: jnp.exp(mo - mn), m, m_new)
            l = _each(lambda a, ll, pp: a * ll + jnp.sum(pp, axis=1, keepdims=True), alpha, l, p)
            acc = _each(lambda a, ac, pp, vv: a * ac + _dot(pp.astype(BF16), vv, "nn"), alpha, acc, p, v)
            return tuple(x for head in zip(m_new, l, acc) for x in head)

        init = (jnp.full((tq, 1), NEG, F32), jnp.zeros((tq, 1), F32), jnp.zeros((tq, LANES), F32)) * FOX_HP
        res = step(i, lax.fori_loop(0, i, lambda j, c: step(j, c, False), init), True)
        for hh in hs:
            m, l, acc = res[3 * hh:3 * hh + 3]
            o = acc / l
            o_ref[:, _lanes_of(hh)] = o
            o16_ref[:, _lanes_of(hh)] = o.astype(BF16)
            lse_ref[:, _lanes_of(hh)] = jnp.broadcast_to(m + jnp.log(l), (tq, LANES))

    wide = (rows, NH * LANES)
    return pl.pallas_call(
        body, name="fox_fwd", grid=(nseq, NH // FOX_HP, nq), in_specs=[qblk, full, vfull, ctb], out_specs=[qblk] * 3,
        out_shape=[jax.ShapeDtypeStruct(wide, F32), jax.ShapeDtypeStruct(wide, BF16), jax.ShapeDtypeStruct(wide, F32)],
        compiler_params=_params(3))(qn, kn, proj, ct)


def fox_dq(qn, kn, proj, ct, do, lse, delta, nseq, seq):
    tq = tk = min(ATT_TILE, seq)
    nq = seq // tq
    rows = nseq * seq
    qblk, _, full, vfull, ctb = _fox_specs(seq, tq, nq)
    hs = range(FOX_HP)

    def body(q_ref, k_ref, v_ref, ct_ref, do_ref, lse_ref, dl_ref, dq_ref, dc_ref):
        hb, i = pl.program_id(1), pl.program_id(2)
        q = [q_ref[:, _lanes_of(hh)] for hh in hs]
        lse = [_col(lse_ref[:, _lanes_of(hh)]) for hh in hs]
        delta = [_col(dl_ref[:, _lanes_of(hh)]) for hh in hs]
        do16 = [do_ref[:, _lanes_of(hh)].astype(BF16) for hh in hs]

        def step(j, carry, diag):
            dq, dc = (list(carry[t::2]) for t in range(2))
            off = pl.multiple_of(j * tk, tk)
            k = [k_ref[pl.ds(off, tk), _lanes_of(hh)] for hh in hs]
            v = [v_ref[pl.ds(off, tk), _lanes_of(hh)].astype(BF16) for hh in hs]
            ck = [_head_row(ct_ref, hb * FOX_HP + hh, off, tk) for hh in hs]
            p = _each(lambda qq, kk, cc, ll: jnp.exp(_dot(qq, kk, "nt") * FOX_SCALE - cc - ll), q, k, ck, lse)
            if diag:
                p = _each(lambda x: jnp.where(_causal(x.shape, 0), x, 0.0), p)
            dp = _each(lambda d, vv: _dot(d, vv, "nt"), do16, v)
            ds = _each(lambda pp, d, dl: pp * (d - dl), p, dp, delta)
            dq = _each(lambda a, x, kk: a + _dot(x.astype(BF16), kk, "nn"), dq, ds, k)
            dc = _each(lambda a, x: a + jnp.sum(x, axis=1, keepdims=True), dc, ds)
            return tuple(x for head in zip(dq, dc) for x in head)

        init = (jnp.zeros((tq, LANES), F32), jnp.zeros((tq, 1), F32)) * FOX_HP
        res = step(i, lax.fori_loop(0, i, lambda j, c: step(j, c, False), init), True)
        for hh in hs:
            dq_ref[:, _lanes_of(hh)] = res[2 * hh] * FOX_SCALE
            dc_ref[:, _lanes_of(hh)] = jnp.where(_iota((tq, LANES), 1) == 0, res[2 * hh + 1], 0.0)

    wide = jax.ShapeDtypeStruct((rows, NH * LANES), F32)
    return pl.pallas_call(
        body, name="fox_dq", grid=(nseq, NH // FOX_HP, nq), in_specs=[qblk, full, vfull, ctb, qblk, qblk, qblk],
        out_specs=[qblk, qblk], out_shape=[wide, wide], compiler_params=_params(3))(qn, kn, proj, ct, do, lse, delta)


def fox_dkv(qn, kn, proj, cb, do, lse, delta, nseq, seq):
    tq = tk = min(ATT_TILE, seq)
    nq = seq // tq
    rows = nseq * seq
    kblk, vblk, full, _, _ = _fox_specs(seq, tk, nq)
    hs = range(FOX_HP)

    def body(q_ref, k_ref, v_ref, cb_ref, do_ref, lse_ref, dl_ref, dk_ref, dv_ref, dc_ref):
        j = pl.program_id(2)
        k = [k_ref[:, _lanes_of(hh)] for hh in hs]
        v16 = [v_ref[:, _lanes_of(hh)].astype(BF16) for hh in hs]
        ck = [_col(cb_ref[:, _lanes_of(hh)]) for hh in hs]

        def step(i, carry, diag):
            dk, dv, dc = (list(carry[t::3]) for t in range(3))
            off = pl.multiple_of(i * tq, tq)
            q = [q_ref[pl.ds(off, tq), _lanes_of(hh)] for hh in hs]
            do16 = [do_ref[pl.ds(off, tq), _lanes_of(hh)].astype(BF16) for hh in hs]
            lse = [_row(lse_ref[pl.ds(off, tq), _lanes_of(hh)]) for hh in hs]
            delta = [_row(dl_ref[pl.ds(off, tq), _lanes_of(hh)]) for hh in hs]
            p = _each(lambda kk, qq, cc, ll: jnp.exp(_dot(kk, qq, "nt") * FOX_SCALE - cc - ll), k, q, ck, lse)
            if diag:
                p = _each(lambda x: jnp.where(_causal(x.shape, 1), x, 0.0), p)
            dv = _each(lambda a, pp, d: a + _dot(pp.astype(BF16), d, "nn"), dv, p, do16)
            ds = _each(lambda pp, vv, d, dl: pp * (_dot(vv, d, "nt") - dl), p, v16, do16, delta)
            dk = _each(lambda a, x, qq: a + _dot(x.astype(BF16), qq, "nn"), dk, ds, q)
            dc = _each(lambda a, x: a + jnp.sum(x, axis=1, keepdims=True), dc, ds)
            return tuple(x for head in zip(dk, dv, dc) for x in head)

        zero = jnp.zeros((tk, LANES), F32)
        carry = step(j, (zero, zero, jnp.zeros((tk, 1), F32)) * FOX_HP, True)
        res = lax.fori_loop(j + 1, nq, lambda i, c: step(i, c, False), carry)
        for hh in hs:
            dk, dv, dc = res[3 * hh:3 * hh + 3]
            dk_ref[:, _lanes_of(hh)] = dk * FOX_SCALE
            dv_ref[:, _lanes_of(hh)] = dv.astype(BF16)
            dc_ref[:, _lanes_of(hh)] = jnp.where(_iota((tk, LANES), 1) == 0, -dc, 0.0)

    wide = (rows, NH * LANES)
    return pl.pallas_call(
        body, name="fox_dkv", grid=(nseq, NH // FOX_HP, nq), in_specs=[full, kblk, vblk, kblk, full, full, full],
        out_specs=[kblk, kblk, kblk],
        out_shape=[jax.ShapeDtypeStruct(wide, F32), jax.ShapeDtypeStruct(wide, BF16), jax.ShapeDtypeStruct(wide, F32)],
        compiler_params=_params(3))(qn, kn, proj, cb, do, lse, delta)


def _adamw_update(w, g, m, v):
    m_new = ADAM_B1 * m + (1.0 - ADAM_B1) * g
    v_new = ADAM_B2 * v + (1.0 - ADAM_B2) * (g * g)
    m_hat = m_new / (1.0 - ADAM_B1 ** ADAM_STEP)
    v_hat = v_new / (1.0 - ADAM_B2 ** ADAM_STEP)
    return -ADAM_LR * (m_hat / (jnp.sqrt(v_hat) + ADAM_EPS) + ADAM_WD * w), m_new, v_new


def adamw(name, w, g, m, v):
    rows, cols = w.shape
    tb = min(rows, 128)
    assert rows % tb == 0
    blk = pl.BlockSpec((tb, cols), lambda i: (i, 0))

    def body(w_ref, g_ref, m_ref, v_ref, d_ref, mo_ref, vo_ref):
        d_ref[...], mo_ref[...], vo_ref[...] = _adamw_update(w_ref[...], g_ref[...], m_ref[...], v_ref[...])

    shp = jax.ShapeDtypeStruct(w.shape, F32)
    return pl.pallas_call(body, name=name, grid=(rows // tb,), in_specs=[blk] * 4, out_specs=[blk] * 3,
                          out_shape=[shp] * 3, compiler_params=_params(1))(w, g, m, v)


SPLIT_TILE = 128


def _tiled(shape2d, ax, n_lead, index):
    blk = (SPLIT_TILE, shape2d[1]) if ax == 0 else (shape2d[0], SPLIT_TILE)

    def index_map(*args):
        *lead, t = index(*args)
        return (*lead, t, 0) if ax == 0 else (*lead, 0, t)

    return pl.BlockSpec((None,) * n_lead + blk, index_map)


def adamw_halves(name, w, mine, other, m, v, c, ax):
    steps = w.shape[ax] // 2 // SPLIT_TILE
    assert w.shape[ax] == 2 * steps * SPLIT_TILE

    def body(c_ref, w_ref, mine_ref, other_ref, m_ref, v_ref, g_ref, d_ref, mo_ref, vo_ref):
        g = jnp.where(pl.program_id(0) // steps == c_ref[0], mine_ref[...], other_ref[...])
        g_ref[...] = g
        d_ref[...], mo_ref[...], vo_ref[...] = _adamw_update(w_ref[...], g, m_ref[...], v_ref[...])

    blk = _tiled(w.shape, ax, 0, lambda i, c_ref: (i,))
    hblk = _tiled(mine.shape, ax, 0, lambda i, c_ref: (i % steps,))
    grid_spec = pltpu.PrefetchScalarGridSpec(num_scalar_prefetch=1, grid=(2 * steps,),
                                             in_specs=[blk, hblk, hblk, blk, blk], out_specs=[blk] * 4)
    shp = jax.ShapeDtypeStruct(w.shape, F32)
    return pl.pallas_call(body, name=name, grid_spec=grid_spec, out_shape=[shp] * 4,
                          compiler_params=_params(1))(c, w, mine, other, m, v)


def add_chips(name, slots, parts, chip, axes):
    outs = []
    for idx, (x, own, ax) in enumerate(zip(slots, parts, axes)):
        n, shape2d = x.shape[0], x.shape[1:]
        steps = shape2d[ax] // SPLIT_TILE
        assert shape2d[ax] == steps * SPLIT_TILE

        def body(me_ref, *refs, n=n):
            o_ref = refs[n + 1]
            acc = None
            for t in range(n):
                term = jnp.where(me_ref[0] == t, refs[n][...], refs[t][...]).astype(F32)
                acc = term if acc is None else acc + term
            o_ref[...] = acc

        def filled(t, n=n):
            return lambda i, me_ref: (jnp.where(me_ref[0] == t, (t + 1) % n, t), i)

        grid_spec = pltpu.PrefetchScalarGridSpec(
            num_scalar_prefetch=1, grid=(steps,),
            in_specs=[_tiled(shape2d, ax, 1, filled(t)) for t in range(n)]
            + [_tiled(shape2d, ax, 1, lambda i, me_ref: (me_ref[0], i))],
            out_specs=_tiled(shape2d, ax, 0, lambda i, me_ref: (i,)))
        outs.append(pl.pallas_call(
            body, name=f"{name}_{idx}", grid_spec=grid_spec, out_shape=jax.ShapeDtypeStruct(shape2d, F32),
            compiler_params=_params(1))(chip, *([x] * n), own))
    return outs


def add_pair(name, gs, rs, c, axes):
    outs = []
    for idx, (g, r, ax) in enumerate(zip(gs, rs, axes)):
        nb = r.shape[0]
        steps = r.shape[1 + ax] // SPLIT_TILE
        assert r.shape[1 + ax] == steps * SPLIT_TILE

        def body(c_ref, g_ref, r_ref, o_ref):
            o_ref[...] = (g_ref[...] + r_ref[...]).astype(BF16)

        grid_spec = pltpu.PrefetchScalarGridSpec(
            num_scalar_prefetch=1, grid=(nb, steps),
            in_specs=[_tiled(g.shape[1:], ax, 1, lambda b, i, c_ref: (b, c_ref[0] * steps + i)),
                      _tiled(r.shape[1:], ax, 1, lambda b, i, c_ref: (b, i))],
            out_specs=_tiled(r.shape[1:], ax, 1, lambda b, i, c_ref: (b, i)))
        outs.append(pl.pallas_call(
            body, name=f"{name}_{idx}", grid_spec=grid_spec, out_shape=jax.ShapeDtypeStruct(r.shape, BF16),
            compiler_params=_params(2))(c, g, r))
    return outs


def _place():
    x, y, c = lax.axis_index("x"), lax.axis_index("y"), lax.axis_index("c")
    return x, y, c, [(1 - x, y), (x, 1 - y), (1 - x, 1 - y)]


def _remote(src, dst, send_sem, recv_sem, dev):
    return pltpu.make_async_remote_copy(src_ref=src, dst_ref=dst, send_sem=send_sem, recv_sem=recv_sem,
                                        device_id=dev, device_id_type=MESH)


def _half(ref, lead, ax, which):
    size = ref.shape[len(lead) + ax] // 2
    part = pl.ds(which * size, size)
    return ref.at[(*lead, part, slice(None)) if ax == 0 else (*lead, slice(None), part)]


def gather_ring(shard):
    rows, cols = shard.shape
    half = cols // 2
    top = rows // 2 // 16 * 16
    assert shard.dtype == BF16 and half % LANES == 0

    def body(in_ref, out_ref, ici_s, ici_r, d2d_s, d2d_r):
        x, y, c, _ = _place()
        me, xn, yn, dg = 2 * x + y, 2 * (1 - x) + y, 2 * x + (1 - y), 2 * (1 - x) + (1 - y)
        to_x, to_y, sib = (1 - x, y, c), (x, 1 - y, c), (x, y, 1 - c)
        mine, other = pl.ds(c * half, half), pl.ds((1 - c) * half, half)
        upper, lower = pl.ds(0, top), pl.ds(top, rows - top)
        started = []

        def send(src, dst, sems, k, dev):
            cp = _remote(src, dst, sems[0].at[k], sems[1].at[k], dev)
            cp.start()
            started.append(cp)

        def arrive(dst, sems, k):
            _remote(dst, dst, sems[0].at[k], sems[1].at[k], sib).wait_recv()

        ici, d2d = (ici_s, ici_r), (d2d_s, d2d_r)
        send(in_ref, out_ref.at[me], d2d, 0, sib)
        send(in_ref.at[:, mine], out_ref.at[me, :, mine], ici, 0, to_x)
        send(in_ref.at[:, mine], out_ref.at[me, :, mine], ici, 1, to_y)
        arrive(out_ref.at[xn, :, mine], ici, 0)
        send(out_ref.at[xn, upper, mine], out_ref.at[xn, upper, mine], ici, 2, to_y)
        send(out_ref.at[xn, :, mine], out_ref.at[xn, :, mine], d2d, 1, sib)
        arrive(out_ref.at[yn, :, mine], ici, 1)
        send(out_ref.at[yn, lower, mine], out_ref.at[yn, lower, mine], ici, 3, to_x)
        send(out_ref.at[yn, :, mine], out_ref.at[yn, :, mine], d2d, 2, sib)
        arrive(out_ref.at[dg, upper, mine], ici, 2)
        send(out_ref.at[dg, upper, mine], out_ref.at[dg, upper, mine], d2d, 3, sib)
        arrive(out_ref.at[dg, lower, mine], ici, 3)
        send(out_ref.at[dg, lower, mine], out_ref.at[dg, lower, mine], d2d, 4, sib)
        arrive(out_ref.at[me], d2d, 0)
        arrive(out_ref.at[xn, :, other], d2d, 1)
        arrive(out_ref.at[yn, :, other], d2d, 2)
        arrive(out_ref.at[dg, upper, other], d2d, 3)
        arrive(out_ref.at[dg, lower, other], d2d, 4)
        for cp in started:
            cp.wait_send()

    return pl.pallas_call(
        body, name="gather_ring", in_specs=[ANY], out_specs=ANY, out_shape=jax.ShapeDtypeStruct((4,) + shard.shape, shard.dtype),
        scratch_shapes=[pltpu.SemaphoreType.DMA((4,))] * 2 + [pltpu.SemaphoreType.DMA((5,))] * 2,
    )(shard)


HBM = pl.BlockSpec(memory_space=pltpu.HBM)
SEM = pl.BlockSpec(memory_space=pltpu.SEMAPHORE)
DATAFLOW = pltpu.SideEffectType.DATAFLOW_SIDE_EFFECTING


def _hbm(a):
    return pltpu.with_memory_space_constraint(a, pltpu.HBM)


class SplitExchange:
    def __init__(self, name, srcs, zone_shapes, n_sems, plan):
        self.name, self.n, self.n_sems, self.plan = name, len(srcs), n_sems, plan
        self.srcs = [_hbm(s) for s in srcs]
        self.zones = [_hbm(lax.empty(shape, s.dtype)) for shape, s in zip(zone_shapes, srcs)]

    def start(self, after):
        n, n_after = self.n, len(after)

        def body(*refs):
            ins, lands = refs[:n], refs[n:2 * n]
            send, recv, token = refs[2 * n + n_after], refs[2 * n + n_after + 1], refs[-1]
            for src, dst, si, ri, dev in self.plan(ins, lands)[0]:
                _remote(src, dst, send.at[si], recv.at[ri], dev).start()
            token[...] = jnp.zeros_like(token)

        res = pl.pallas_call(
            body, name=f"{self.name}_start", in_specs=[HBM] * (2 * n) + [ANY] * n_after,
            out_specs=[SEM, SEM] + [HBM] * (2 * n) + [pl.BlockSpec(memory_space=pltpu.VMEM)],
            out_shape=[pltpu.SemaphoreType.DMA((self.n_sems,)), pltpu.SemaphoreType.DMA((self.n_sems,))]
            + [pltpu.HBM(a.shape, a.dtype) for a in self.srcs + self.zones] + [jax.ShapeDtypeStruct((8, LANES), F32)],
            input_output_aliases={i: 2 + i for i in range(2 * n)},
            compiler_params=pltpu.CompilerParams(has_side_effects=DATAFLOW),
        )(*self.srcs, *self.zones, *after)
        self.sems, self.srcs, self.zones = res[:2], list(res[2:2 + n]), list(res[2 + n:2 + 2 * n])
        return res[-1]

    def wait(self, after):
        n = self.n

        def body(*refs):
            ins, lands = refs[:n], refs[n:2 * n]
            send, recv = refs[2 * n], refs[2 * n + 1]
            sends, arrivals = self.plan(ins, lands)
            for src, _, si, _, dev in sends:
                _remote(src, src, send.at[si], recv.at[si], dev).wait_send()
            for landed, ri in arrivals:
                _remote(landed, landed, send.at[ri], recv.at[ri], _place()[:3]).wait_recv()

        res = pl.pallas_call(
            body, name=f"{self.name}_wait", in_specs=[HBM] * (2 * n) + [SEM, SEM, ANY], out_specs=[HBM] * (2 * n),
            out_shape=[pltpu.HBM(a.shape, a.dtype) for a in self.srcs + self.zones],
            input_output_aliases={i: i for i in range(2 * n)},
            compiler_params=pltpu.CompilerParams(has_side_effects=DATAFLOW),
        )(*self.srcs, *self.zones, *self.sems, after)
        self.srcs = list(res[:n])
        return list(res[n:])


def split_gather(shards):
    n = len(shards)

    def plan(ins, lands):
        x, y, c, chips = _place()
        me = 2 * x + y
        sends, arrivals = [], []
        for w in range(n):
            for j, (ox, oy) in enumerate(chips):
                for k in range(2):
                    base = 2 * (3 * w + j)
                    sends.append((_half(ins[w], (), 0, c), _half(lands[w], (me,), 0, c), base + k, base + c, (ox, oy, k)))
                    arrivals.append((_half(lands[w], (2 * ox + oy,), 0, k), base + k))
            sends.append((ins[w], lands[w].at[me], 6 * n + w, 6 * n + w, (x, y, 1 - c)))
            arrivals.append((lands[w].at[me], 6 * n + w))
        return sends, arrivals

    return SplitExchange("gather", shards, [(4,) + s.shape for s in shards], 7 * n, plan)


def split_pair_swap(name, grads, axes):
    def plan(ins, lands):
        x, y, c, _ = _place()
        sends = [(_half(ins[w], (slice(None),), axes[w], 1 - c), lands[w], w, w, (x, y, 1 - c)) for w in range(len(ins))]
        return sends, [(lands[w], w) for w in range(len(ins))]

    halved = [tuple(d // 2 if i == 1 + ax else d for i, d in enumerate(g.shape)) for g, ax in zip(grads, axes)]
    return SplitExchange(name, grads, halved, len(grads), plan)


def split_chip_exchange(name, parts):
    def plan(ins, lands):
        x, y, c, chips = _place()
        sends, arrivals = [], []
        for w in range(len(ins)):
            for j, (ox, oy) in enumerate(chips):
                sends.append((ins[w].at[2 * ox + oy], lands[w].at[2 * x + y], 3 * w + j, 3 * w + j, (ox, oy, c)))
                arrivals.append((lands[w].at[2 * ox + oy], 3 * w + j))
        return sends, arrivals

    return SplitExchange(name, parts, [p.shape for p in parts], 3 * len(parts), plan)


def split_pair_send(halves):
    def plan(ins, lands):
        x, y, c, _ = _place()
        return ([(ins[w], lands[w], w, w, (x, y, 1 - c)) for w in range(len(ins))],
                [(lands[w], w) for w in range(len(ins))])

    return SplitExchange("pair_send", halves, [h.shape for h in halves], len(halves), plan)


def pair_send(halves):
    n = len(halves)

    def body(*refs):
        ins, outs = refs[:n], refs[n:2 * n]
        send, recv = refs[2 * n:]
        x, y, c, _ = _place()
        cps = [_remote(ins[w], outs[w], send.at[w], recv.at[w], (x, y, 1 - c)) for w in range(n)]
        for cp in cps:
            cp.start()
        for cp in cps:
            cp.wait_recv()
        for cp in cps:
            cp.wait_send()

    return pl.pallas_call(
        body, name="pair_send", in_specs=[ANY] * n, out_specs=[ANY] * n,
        out_shape=[jax.ShapeDtypeStruct(h.shape, h.dtype) for h in halves],
        scratch_shapes=[pltpu.SemaphoreType.DMA((n,))] * 2,
    )(*halves)


def all_reduce_small(name, vec, after=()):
    rows = vec.shape[0]

    def body(v_ref, *refs):
        o_ref, buf, send, recv = refs[len(after):]
        x, y, c, _ = _place()
        me = 4 * x + 2 * y + c
        buf[me] = v_ref[...]
        cps = []
        for k in range(1, 8):
            kx, ky, kc = (k >> 2) & 1, (k >> 1) & 1, k & 1
            peer = (x if kx == 0 else 1 - x, y if ky == 0 else 1 - y, c if kc == 0 else 1 - c)
            cp = _remote(v_ref, buf.at[me], send.at[k - 1], recv.at[k - 1], peer)
            cp.start()
            cps.append(cp)
        for k in range(1, 8):
            kx, ky, kc = (k >> 2) & 1, (k >> 1) & 1, k & 1
            px, py, pc = (x if kx == 0 else 1 - x, y if ky == 0 else 1 - y, c if kc == 0 else 1 - c)
            slot = buf.at[4 * px + 2 * py + pc]
            _remote(slot, slot, send.at[k - 1], recv.at[k - 1], (px, py, pc)).wait_recv()
        for cp in cps:
            cp.wait_send()
        acc = buf[0]
        for d in range(1, 8):
            acc = acc + buf[d]
        o_ref[...] = acc

    vm = pl.BlockSpec(memory_space=pltpu.VMEM)
    return pl.pallas_call(
        body, name=name, in_specs=[vm] + [ANY] * len(after), out_specs=vm, out_shape=jax.ShapeDtypeStruct(vec.shape, F32),
        scratch_shapes=[pltpu.VMEM((8, rows, LANES), F32), pltpu.SemaphoreType.DMA((7,)), pltpu.SemaphoreType.DMA((7,))],
    )(vec, *after)


class NoExchange:
    def __init__(self, late):
        self.late = late

    def late_weights(self, after):
        return self.late

    def reduce_start(self, grads):
        return jnp.zeros((8, LANES), F32)

    def reduce_exchange(self, after):
        return jnp.zeros((8, LANES), F32)

    def reduce_finish(self, after):
        return jnp.zeros((8, LANES), F32)

    def input_grad_start(self, dw_main, dw_small):
        return jnp.zeros((8, LANES), F32)

    def input_grad_exchange(self, after):
        return jnp.zeros((8, LANES), F32)


def local_step(x2, tgt2, g1, g2, gdn_ng, qn_g, kn_g, p1, p2, conv_w, wt_main, wt_small, hooks, nseq, seq):
    rows, dm = x2.shape
    wide = NH * LANES
    row = lambda a, off=0, w=None: (a, "row", off, a.shape[1] if w is None else w)
    rowh = lambda a, off=0, w=LANES: (a, "rowh", off, w)
    par = lambda a: (a, "par", 0, a.shape[1])
    parh = lambda a, off=0: (a, "parh", off, LANES)
    o_row = lambda w, dt: (w, "row", w, dt)
    o_rowh = lambda dt, tw=wide, w=LANES: (tw, "rowh", w, dt)

    u, = ew_fwd("rms1", f_rms, [row(x2), par(g1)], [o_row(dm, BF16)], rows)
    proj = matmul("mm_in", u, wt_main, "nt", BF16)
    sp = matmul("mm_in_small", u, wt_small, "nt", F32)
    so, = ew_fwd("small", f_small, [row(sp), par(p1), par(p2)], [o_row(LANES, F32)], rows)
    cs = cumsum_time("cumsum", so, nseq, seq, False)
    gb, bb, cb = ew_fwd("bcast", f_bcast, [row(so), row(cs)], [o_rowh(F32)] * 3, rows, NH)
    ct = transpose_time("c_time_major", cs, nseq, seq)
    conv = {}
    for mode, off in (("q", 0), ("k", NH), ("v", 2 * NH)):
        conv[mode], = ew_fwd(f"conv_{mode}", make_f_conv(mode), [rowh(proj, off), parh(conv_w, off)], [o_rowh(F32)],
                             rows, NH, seq, "hi", CONV_HEADS)
    val, kcum, attn, qdec, kdec, t_inv = gdn_a_fwd(conv["q"], conv["k"], conv["v"], gb, bb, rows)
    o_a, snaps = gdn_b_fwd(val, kcum, attn, qdec, kdec, gb, nseq, seq)
    ya_in, = ew_fwd("gdn_post", f_post, [rowh(o_a), rowh(proj, 3 * NH), par(gdn_ng)], [o_rowh(BF16)], rows, NH)
    fqn, = ew_fwd("fox_qn", f_rms, [rowh(proj, FOX_Q), par(qn_g)], [o_rowh(BF16)], rows, NH)
    fkn, = ew_fwd("fox_kn", f_rms, [rowh(proj, FOX_K), par(kn_g)], [o_rowh(BF16)], rows, NH)
    o_b, o_b16, lse = fox_fwd(fqn, fkn, proj, ct, nseq, seq)
    p_a, p_b, w_o, w_u, w_d = hooks.late_weights(o_a)
    y_a = matmul("mm_pa", ya_in, p_a, "nn", F32, tn=1024)
    y_b = matmul("mm_pb", o_b16, p_b, "nn", F32, tn=1024)
    gates = [row(proj, 7, dm), row(proj, 8, dm)]
    merged, = ew_fwd("merge", f_merge, gates + [row(y_a), row(y_b)], [o_row(dm, BF16)], rows)
    hres = matmul("mm_out", merged, w_o, "nn", F32, add=x2, tn=1024)
    hn, = ew_fwd("rms2", f_rms, [row(hres), par(g2)], [o_row(dm, BF16)], rows)
    up_blocks = w_u.shape[0]
    act, relu2 = matmul("mm_up", hn, w_u, "nn", F32, col_blocks=up_blocks, out_dtypes=[F32, BF16],
                        epilogue=lambda r: [r, jnp.maximum(r, 0.0) * jnp.maximum(r, 0.0)])
    def loss_tail(r, h_tile, t_tile):
        d = (r + h_tile) - t_tile
        e = (0.5 / dm) * (d * d)
        part = e.reshape(e.shape[0] // 8, 8, e.shape[1]).sum(axis=0)
        part = sum(part[:, t * LANES:(t + 1) * LANES] for t in range(e.shape[1] // LANES))
        g = d * (1.0 / dm)
        return [g, g, part]

    dout, dout16, loss_acc = matmul("mm_down", relu2, w_d, "nn", F32, extras=[hres, tgt2], epilogue=loss_tail,
                                    out_dtypes=[F32, BF16, F32], tile_sums=True)

    d_act = matmul("mm_d_act", dout16, w_d, "nt", BF16, extras=[act], epilogue=lambda r, a: [2.0 * jnp.maximum(a, 0.0) * r])
    dw_d = matmul("mm_dw_down", relu2, dout16, "tn", F32, tn=1024)
    dw_u = matmul("mm_dw_up", hn, d_act, "tn", F32, col_blocks=up_blocks)
    d_hn = matmul("mm_d_hn", d_act, w_u, "nt", F32, col_blocks=up_blocks)
    dh, dh16, dg2 = ew_bwd("rms2_b", f_rms, [row(hres), par(g2)], [(row(d_hn),)], [row(dout)],
                           lambda g, e: [g[0] + e[0], g[0] + e[0], g[1]],
                           [((rows, dm), "row", dm, F32, None), ((rows, dm), "row", dm, BF16, None), ((1, dm), "par", dm, F32, "all")], rows)
    d_merged = matmul("mm_d_merged", dh16, w_o, "nt", F32, tn=1024)
    dw_o = matmul("mm_dw_out", merged, dh16, "tn", F32, tn=1024)
    seg16 = ((rows, dm), "row", dm, BF16, None)
    d_ga16, d_gb16, d_ya16, d_yb16 = ew_bwd("merge_b", f_merge, gates + [row(y_a), row(y_b)], [(row(d_merged),)], [],
                                            lambda g, e: list(g), [seg16] * 4, rows)
    dp_a = matmul("mm_dp_a", ya_in, d_ya16, "tn", F32, tn=1024)
    d_ya_in = matmul("mm_d_ya_in", d_ya16, p_a, "nt", F32, tn=1024)
    dp_b = matmul("mm_dp_b", o_b16, d_yb16, "tn", F32, tn=1024)
    d_ob = matmul("mm_d_ob", d_yb16, p_b, "nt", F32, tn=1024)
    token = hooks.reduce_start(dict(p_a=dp_a, p_b=dp_b, w_o=dw_o, w_u=dw_u, w_d=dw_d))
    gdn_ng_t = gdn_ng + token[0, 0]
    h32 = ((rows, wide), "rowh", LANES, F32, None)
    h16 = ((rows, wide), "rowh", LANES, BF16, None)
    gain = ((1, LANES), "par", LANES, F32, "all")
    d_oa, d_z16, d_gdn_ng = ew_bwd("gdn_post_b", f_post, [rowh(o_a), rowh(proj, 3 * NH), par(gdn_ng_t)], [(rowh(d_ya_in),)], [],
                                   lambda g, e: list(g), [h32, h16, gain], rows, NH)
    dval, dkc, dat, dqd, dkd, dgb_b = gdn_b_bwd(val, kcum, attn, qdec, kdec, gb, snaps, d_oa, nseq, seq)
    d_cq, d_ck, d_cv, d_gb, d_bb = gdn_a_bwd(conv["q"], conv["k"], conv["v"], gb, bb, t_inv, dval, dkc, dat, dqd, dkd, dgb_b, rows)
    token = hooks.reduce_exchange(d_cq)
    conv_w_t = conv_w + token[0, 0]
    d_pre, d_conv = {}, {}
    tap = ((4, wide), "parh", LANES, F32, "inner")
    for mode, off, ctg in (("q", 0, d_cq), ("k", NH, d_ck), ("v", 2 * NH, d_cv)):
        d_pre[mode], d_conv[mode] = ew_bwd(f"conv_{mode}_b", make_f_conv(mode), [rowh(proj, off), parh(conv_w_t, off)],
                                           [(rowh(ctg),)], [], lambda g, e: list(g), [h16, tap], rows, NH, seq, "hi", CONV_HEADS)
    delta, = ew_fwd("fox_delta", f_delta, [rowh(d_ob), rowh(o_b)], [o_rowh(F32)], rows, NH, after=[token])
    d_fqn, d_cq_b = fox_dq(fqn, fkn, proj, ct, d_ob, lse, delta, nseq, seq)
    d_fkn, d_fv16, d_ck_b = fox_dkv(fqn, fkn, proj, cb, d_ob, lse, delta, nseq, seq)
    token = hooks.reduce_finish(d_fkn)
    qn_g_t, kn_g_t = qn_g + token[0, 0], kn_g + token[0, 0]
    d_fq16, d_qn_g = ew_bwd("fox_qn_b", f_rms, [rowh(proj, FOX_Q), par(qn_g_t)], [(rowh(d_fqn),)], [], lambda g, e: list(g),
                            [h16, gain], rows, NH)
    d_fk16, d_kn_g = ew_bwd("fox_kn_b", f_rms, [rowh(proj, FOX_K), par(kn_g_t)], [(rowh(d_fkn),)], [], lambda g, e: list(g),
                            [h16, gain], rows, NH)
    narrow = ((rows, LANES), "row", LANES, F32, None)
    d_so, d_cs = ew_bwd("bcast_b", f_bcast, [row(so), row(cs)], [(rowh(d_gb),), (rowh(d_bb),), (rowh(d_cq_b), rowh(d_ck_b))], [],
                        lambda g, e: list(g), [narrow, narrow], rows, NH)
    d_logf = cumsum_time("cumsum_b", d_cs, nseq, seq, True)
    vec = ((1, LANES), "par", LANES, F32, "all")
    d_sp16, d_p1, d_p2 = ew_bwd("small_b", f_small, [row(sp), par(p1), par(p2)], [(row(d_so), row(d_logf))], [],
                                lambda g, e: list(g), [((rows, LANES), "row", LANES, BF16, None), vec, vec], rows)
    d_proj16 = jnp.concatenate([d_pre["q"], d_pre["k"], d_pre["v"], d_z16, d_fq16, d_fk16, d_fv16, d_ga16, d_gb16], axis=1)
    dw_main = matmul("mm_dw_main", d_proj16, u, "tn", F32)
    dw_small = matmul("mm_dw_small", d_sp16, u, "tn", F32)
    wt_small_t = wt_small + hooks.input_grad_start(dw_main, dw_small)[0, 0].astype(BF16)
    d_u = matmul("mm_d_u_small", d_sp16, wt_small_t, "nn", F32)
    d_u = matmul("mm_d_u_first", d_proj16, wt_main, "nn", F32, add=d_u, k_part=(0, 2))
    d_u = matmul("mm_d_u_second", d_proj16, wt_main, "nn", F32, add=d_u, k_part=(1, 2), after=[hooks.input_grad_exchange(d_u)])
    dx, dg1 = ew_bwd("rms1_b", f_rms, [row(x2), par(g1)], [(row(d_u),)], [row(dh)], lambda g, e: [g[0] + e[0], g[1]],
                     [((rows, dm), "row", dm, F32, None), ((1, dm), "par", dm, F32, "all")], rows)
    d_conv_w = jnp.concatenate([d_conv["q"], d_conv["k"], d_conv["v"]], axis=1)
    return dict(loss_acc=loss_acc, dx=dx, g1=dg1, g2=dg2, gdn_ng=d_gdn_ng, qn=d_qn_g, kn=d_kn_g, p1=d_p1, p2=d_p2,
                conv=d_conv_w, w_main=dw_main, w_small=dw_small, p_a=dp_a, p_b=dp_b, w_o=dw_o, w_u=dw_u, w_d=dw_d)


_W = NH * LANES
_A0, _A1 = 4 * _W, 4 * _W + 2 * NH
_B0, _B1 = _A1 + 3 * _W, _A1 + 3 * _W + NH


def _split_w_in(full_t):
    main = jnp.concatenate([full_t[:_A0], full_t[_A1:_B0], full_t[_B1:]], axis=0)
    small = jnp.concatenate([full_t[_A0:_A1], full_t[_B0:_B1], jnp.zeros((LANES - 3 * NH, full_t.shape[1]), full_t.dtype)], axis=0)
    return main, small


def _join_w_in(main, small):
    return jnp.concatenate([main[:_A0], small[:2 * NH], main[_A0:_A0 + 3 * _W], small[2 * NH:3 * NH], main[_A0 + 3 * _W:]], axis=0)


def _lanes(v, at=0):
    return jnp.pad(v.reshape(1, -1), ((0, 0), (at, LANES - at - v.size)))


def kernel(x, norm_mix_g, w_in, gdn_conv_w, gdn_a_log, gdn_dt_bias, gdn_norm_g, fox_q_norm_g, fox_k_norm_g, fox_f_bias, w_proj_gdn, w_proj_fox, w_out, norm_mlp_g, w_up, w_down, loss_target, m_norm_mix_g, m_w_in, m_gdn_conv_w, m_gdn_a_log, m_gdn_dt_bias, m_gdn_norm_g, m_fox_q_norm_g, m_fox_k_norm_g, m_fox_f_bias, m_w_proj_gdn, m_w_proj_fox, m_w_out, m_norm_mlp_g, m_w_up, m_w_down, v_norm_mix_g, v_w_in, v_gdn_conv_w, v_gdn_a_log, v_gdn_dt_bias, v_gdn_norm_g, v_fox_q_norm_g, v_fox_k_norm_g, v_fox_f_bias, v_w_proj_gdn, v_w_proj_fox, v_w_out, v_norm_mlp_g, v_w_up, v_w_down):
    nseq, seq, dm = x.shape
    rows = nseq * seq
    xi, yi, ci = lax.axis_index("x"), lax.axis_index("y"), lax.axis_index("c")
    chip = 2 * xi + yi
    conv_cols = gdn_conv_w.shape[2]

    tr = lambda a: jnp.swapaxes(a[0], 0, 1)
    big = [tr(w_in), w_proj_gdn[0], w_proj_fox[0], w_out[0], w_up[0], w_down[0]]
    axes = [1, 0, 0, 0, 0, 0]
    big16 = [w.astype(BF16) for w in big]
    conv_slot = jnp.zeros((4, 4, conv_cols), F32).at[:, chip].set(jnp.where(ci == 0, gdn_conv_w[0], 0.0))
    conv_full = all_reduce_small("gather_conv", conv_slot.reshape(-1, LANES)).reshape(4, 4 * conv_cols)
    got_in = gather_ring(big16[0])
    wt_main, wt_small = _split_w_in(got_in.reshape(-1, dm))
    core, chip_no = ci.reshape(1).astype(jnp.int32), chip.reshape(1).astype(jnp.int32)
    gather = split_gather(big16[1:])
    token = gather.start([got_in, conv_full])

    class Hooks:
        def late_weights(self, after):
            g_pa, g_pb, g_wo, w_u, g_wd = gather.wait(after)
            return (*(g.reshape(-1, dm) for g in (g_pa, g_pb, g_wo)), w_u, g_wd.reshape(-1, dm))

        def reduce_start(self, grads):
            blocks = [grads["p_a"].reshape(4, -1, dm), grads["p_b"].reshape(4, -1, dm), grads["w_o"].reshape(4, -1, dm),
                      grads["w_u"], grads["w_d"].reshape(4, -1, dm)]
            self.swap = split_pair_swap("pair_swap_late", blocks, axes[1:])
            return self.swap.start([])

        def reduce_exchange(self, after):
            swapped = self.swap.wait(after)
            self.exchange = split_chip_exchange("chip_exchange_late", add_pair("add_pair_late", self.swap.srcs, swapped, core, axes[1:]))
            return self.exchange.start([])

        def reduce_finish(self, after):
            slots = self.exchange.wait(after)
            self.send = split_pair_send(add_chips("add_chips_late", slots, self.exchange.srcs, chip_no, axes[1:]))
            return self.send.start([])

        def input_grad_start(self, dw_main, dw_small):
            self.in_swap = split_pair_swap("pair_swap_in", [_join_w_in(dw_main, dw_small).reshape(4, -1, dm)], axes[:1])
            return self.in_swap.start([])

        def input_grad_exchange(self, after):
            swapped = self.in_swap.wait(after)
            self.in_exchange = split_chip_exchange("chip_exchange_in", add_pair("add_pair_in", self.in_swap.srcs, swapped, core, axes[:1]))
            return self.in_exchange.start([])

    hooks = Hooks()
    p1 = _lanes(gdn_dt_bias[0]) + _lanes(fox_f_bias[0], 2 * NH)
    p2 = _lanes(gdn_a_log[0])

    g = local_step(x.reshape(rows, dm), loss_target.reshape(rows, dm), norm_mix_g + token[0, 0], norm_mlp_g, gdn_norm_g,
                   fox_q_norm_g, fox_k_norm_g, p1, p2, conv_full, wt_main, wt_small, hooks, nseq, seq)

    others = hooks.send.wait(g["dx"])
    big_m = [tr(m_w_in), m_w_proj_gdn[0], m_w_proj_fox[0], m_w_out[0], m_w_up[0], m_w_down[0]]
    big_v = [tr(v_w_in), v_w_proj_gdn[0], v_w_proj_fox[0], v_w_out[0], v_w_up[0], v_w_down[0]]
    names = ["w_in", "w_proj_gdn", "w_proj_fox", "w_out", "w_up", "w_down"]
    big_res, big_grad = {}, {}
    for i in range(1, len(names)):
        big_grad[names[i]], *big_res[names[i]] = adamw_halves(f"adamw_{names[i]}", big[i], hooks.send.srcs[i - 1], others[i - 1],
                                                              big_m[i], big_v[i], core, axes[i])
    slots = hooks.in_exchange.wait(big_res[names[-1]][0])
    mine = add_chips("add_chips_in", slots, hooks.in_exchange.srcs, chip_no, axes[:1])
    res = adamw_halves("adamw_w_in", big[0], mine[0], pair_send(mine)[0], big_m[0], big_v[0], core, axes[0])
    big_grad["w_in"], *big_res["w_in"] = [jnp.swapaxes(r, 0, 1) for r in res]

    small_parts = [g["loss_acc"], g["g1"].reshape(8, LANES), g["g2"].reshape(8, LANES), g["gdn_ng"], g["qn"], g["kn"], g["p1"], g["p2"],
                   g["conv"].reshape(-1, LANES)]
    tiled = [jnp.pad(p, ((0, -p.shape[0] % 8), (0, 0))) for p in small_parts]
    red = all_reduce_small("reduce_small", jnp.concatenate(tiled, axis=0), slots)
    pos, red_parts = 0, []
    for p, t in zip(small_parts, tiled):
        red_parts.append(red[pos:pos + p.shape[0]])
        pos += t.shape[0]
    r_loss, r_g1, r_g2, r_gdn_ng, r_qn, r_kn, r_p1, r_p2, r_conv = red_parts
    loss = jnp.sum(r_loss)
    g_conv = lax.dynamic_slice_in_dim(r_conv.reshape(4, 4, conv_cols), chip, 1, axis=1).reshape(4, conv_cols)
    small_grads = [r_g1.reshape(1, dm), r_p2[:, :NH], r_p1[:, :NH], r_gdn_ng, r_qn, r_kn, r_p1[:, 2 * NH:3 * NH], r_g2.reshape(1, dm)]
    small_w = [norm_mix_g, gdn_a_log, gdn_dt_bias, gdn_norm_g, fox_q_norm_g, fox_k_norm_g, fox_f_bias, norm_mlp_g]
    small_m = [m_norm_mix_g, m_gdn_a_log, m_gdn_dt_bias, m_gdn_norm_g, m_fox_q_norm_g, m_fox_k_norm_g, m_fox_f_bias, m_norm_mlp_g]
    small_v = [v_norm_mix_g, v_gdn_a_log, v_gdn_dt_bias, v_gdn_norm_g, v_fox_q_norm_g, v_fox_k_norm_g, v_fox_f_bias, v_norm_mlp_g]

    def pack(parts):
        flat = jnp.concatenate([jnp.pad(p.reshape(-1), (0, -p.size % LANES)) for p in parts])
        return jnp.pad(flat, (0, -flat.size % (8 * LANES))).reshape(-1, LANES)

    packed = adamw("adamw_small", pack(small_w + [gdn_conv_w[0]]), pack(small_grads + [g_conv]),
                   pack(small_m + [m_gdn_conv_w[0]]), pack(small_v + [v_gdn_conv_w[0]]))

    def unpack(flat2d):
        flat, pos, res = flat2d.reshape(-1), 0, []
        for p in small_w + [gdn_conv_w[0]]:
            res.append(flat[pos:pos + p.size].reshape(p.shape))
            pos += p.size + (-p.size % LANES)
        return res

    s_delta, s_m, s_v = (unpack(a) for a in packed)

    order = ["norm_mix_g", "w_in", "gdn_conv_w", "gdn_a_log", "gdn_dt_bias", "gdn_norm_g", "fox_q_norm_g", "fox_k_norm_g",
             "fox_f_bias", "w_proj_gdn", "w_proj_fox", "w_out", "norm_mlp_g", "w_up", "w_down"]
    small_names = ["norm_mix_g", "gdn_a_log", "gdn_dt_bias", "gdn_norm_g", "fox_q_norm_g", "fox_k_norm_g", "fox_f_bias", "norm_mlp_g",
                   "gdn_conv_w"]
    small_idx = {nm: i for i, nm in enumerate(small_names)}
    shapes = dict(zip(order, (a.shape for a in (norm_mix_g, w_in, gdn_conv_w, gdn_a_log, gdn_dt_bias, gdn_norm_g, fox_q_norm_g,
                                                 fox_k_norm_g, fox_f_bias, w_proj_gdn, w_proj_fox, w_out, norm_mlp_g, w_up, w_down))))
    grads_out, delta_out, m_out, v_out = [], [], [], []
    for nm in order:
        if nm in big_res:
            d, mm, vv = big_res[nm]
            gr = big_grad[nm]
        else:
            i = small_idx[nm]
            gr = (small_grads + [g_conv])[i]
            d, mm, vv = s_delta[i], s_m[i], s_v[i]
        for lst, val in ((grads_out, gr), (delta_out, d), (m_out, mm), (v_out, vv)):
            lst.append(val.reshape(shapes[nm]))
    return (loss, g["dx"].reshape(x.shape), *grads_out, *delta_out, *m_out, *v_out)
```

```python
import functools

import jax
import jax.numpy as jnp
from jax import lax
from jax.experimental import pallas as pl
from jax.experimental.pallas import tpu as pltpu

F32 = jnp.float32
BF16 = jnp.bfloat16
LANES = 128
NH = 8
EPS = 1e-6
GDN_CHUNK = 64
GDN_ROWS = 256
GDN_BASE = 16
ROW_TILE = 512
CONV_HEADS = 2
ATT_TILE = 512
NEG = -1e30
VMEM_LIMIT_BYTES = 58 * 1024 * 1024
LO = lax.Precision.DEFAULT
MESH = pl.DeviceIdType.MESH
ANY = pl.BlockSpec(memory_space=pl.ANY)

ADAM_LR, ADAM_B1, ADAM_B2, ADAM_EPS, ADAM_WD, ADAM_STEP = 0.001, 0.9, 0.999, 1e-08, 0.01, 10


def _params(n_grid):
    return pltpu.CompilerParams(dimension_semantics=("arbitrary",) * n_grid,
                                vmem_limit_bytes=VMEM_LIMIT_BYTES)


def _dot(a, b, dims, precision=None):
    dn = {"nn": (((1,), (0,)), ((), ())), "nt": (((1,), (1,)), ((), ())), "tn": (((0,), (0,)), ((), ()))}[dims]
    return lax.dot_general(a, b, dn, precision=precision, preferred_element_type=F32)


def _iota(shape, dim):
    return lax.broadcasted_iota(jnp.int32, shape, dim)


def _split(x, parts):
    out = []
    for _ in range(parts - 1):
        hi = x.astype(BF16)
        out.append(hi)
        x = x - hi.astype(F32)
    return out + [x.astype(BF16)]


def _dot_mask(mask, b, dims, terms=3):
    m16 = mask.astype(BF16)
    acc = None
    for part in reversed(_split(b, terms)):
        prod = _dot(m16, part, dims)
        acc = prod if acc is None else acc + prod
    return acc


@jax.custom_vjp
def mm_mask(mask, b):
    return _dot_mask(mask, b, "nn", 2)


mm_mask.defvjp(lambda mask, b: (_dot_mask(mask, b, "nn", 2), mask),
               lambda mask, g: (jnp.zeros_like(mask), _dot_mask(mask, g, "tn", 2)))


def matmul(name, a, b, dims, out_dtype, add=None, tm=1024, tn=1024, tk=512, col_blocks=None,
           extras=(), epilogue=None, out_dtypes=None, k_part=None, after=(), tile_sums=False):
    if col_blocks and dims != "tn":
        nb, b_rows, bw = b.shape
        b_shape = (b_rows, nb * bw)
    else:
        b_shape = b.shape
    if dims == "nn":
        (m, k), (_, n) = a.shape, b_shape
    elif dims == "nt":
        (m, k), (n, _) = a.shape, b_shape
    else:
        (k, m), (_, n) = a.shape, b_shape
    k_span = k // (k_part[1] if k_part else 1)
    if col_blocks and dims == "nt":
        k_span = min(k_span, bw)
    tk = k if k <= 1024 else max(t for t in (2048, 1536, 1024, 512, tk) if k_span % t == 0)
    tm, tn, tk = min(tm, m), min(tn, n), min(tk, k)
    assert m % tm == 0 and n % tn == 0 and k % tk == 0, (name, m, n, k)
    k0, nk = (0, k // tk) if k_part is None else (k_part[0] * (k // tk // k_part[1]), k // tk // k_part[1])
    assert k_part is None or (dims == "nn" and not col_blocks and (k // tk) % k_part[1] == 0)
    a_spec = pl.BlockSpec((tk, tm), lambda i, j, kk: (kk, i)) if dims == "tn" else pl.BlockSpec((tm, tk), lambda i, j, kk: (i, kk + k0))
    b_spec = pl.BlockSpec((tn, tk), lambda i, j, kk: (j, kk)) if dims == "nt" else pl.BlockSpec((tk, tn), lambda i, j, kk: (kk + k0, j))
    o_spec = pl.BlockSpec((tm, tn), lambda i, j, kk: (i, j))
    out_shape = (m, n)
    if col_blocks and dims == "nn":
        per = bw // tn
        assert bw % tn == 0
        b_spec = pl.BlockSpec((None, tk, tn), lambda i, j, kk: (j // per, kk, j % per))
    elif col_blocks and dims == "nt":
        per = bw // tk
        assert bw % tk == 0
        b_spec = pl.BlockSpec((None, tn, tk), lambda i, j, kk: (kk // per, j, kk % per))
    elif col_blocks:
        bw = n // col_blocks
        per = bw // tn
        assert bw % tn == 0 and add is None
        o_spec = pl.BlockSpec((None, tm, tn), lambda i, j, kk: (j // per, i, j % per))
        out_shape = (col_blocks, m, bw)
    extras = list(extras) + ([add] if add is not None else [])
    if add is not None:
        assert epilogue is None
        epilogue = lambda r, *e: [r + e[-1]]
    out_dtypes = [out_dtype] if epilogue is None or out_dtypes is None else list(out_dtypes)
    n_ex, n_out = len(extras), len(out_dtypes)

    def body(*refs):
        a_ref, b_ref = refs[0], refs[1]
        ex_refs, o_refs = refs[2:2 + n_ex], refs[2 + n_ex + len(after):2 + n_ex + len(after) + n_out]

        def finish(r):
            res = [r] if epilogue is None else epilogue(r, *[e[...] for e in ex_refs])
            for o_ref, v in zip(o_refs, res):
                o_ref[...] = v.astype(o_ref.dtype)

        if nk == 1:
            finish(_dot(a_ref[...], b_ref[...], dims))
            return
        acc_ref = refs[-1]
        kk = pl.program_id(2)

        @pl.when(kk == 0)
        def _():
            acc_ref[...] = jnp.zeros_like(acc_ref)

        acc_ref[...] += _dot(a_ref[...], b_ref[...], dims)

        @pl.when(kk == nk - 1)
        def _():
            finish(acc_ref[...])

    out_specs = [o_spec] * n_out
    out_shapes = [jax.ShapeDtypeStruct(out_shape, dt) for dt in out_dtypes]
    if tile_sums:
        out_specs[-1] = pl.BlockSpec((8, LANES), lambda i, j, kk: (i, j))
        out_shapes[-1] = jax.ShapeDtypeStruct((8 * (m // tm), LANES * (n // tn)), out_dtypes[-1])
    res = pl.pallas_call(
        body, name=name, grid=(m // tm, n // tn, nk), in_specs=[a_spec, b_spec] + [o_spec] * n_ex + [ANY] * len(after),
        out_specs=out_specs, out_shape=out_shapes,
        scratch_shapes=[pltpu.VMEM((tm, tn), F32)] if nk > 1 else [], compiler_params=_params(3),
    )(a, b, *extras, *after)
    return res[0] if n_out == 1 else res


def _ew_spec(kind, off, width, tb, hp, order, shape=None):
    def ih(g0, g1):
        return (g0, g1) if order == "ih" else (g1, g0)

    assert off % hp == 0 or kind in ("row", "par")
    if kind == "row":
        return pl.BlockSpec((tb, width), lambda g0, g1: (ih(g0, g1)[0], off))
    if kind == "rowh":
        return pl.BlockSpec((tb, hp * width), lambda g0, g1: (ih(g0, g1)[0], ih(g0, g1)[1] + off // hp))
    if kind == "par":
        return pl.BlockSpec(shape, lambda g0, g1: (0, 0))
    if kind == "parh":
        return pl.BlockSpec((shape[0], hp * width), lambda g0, g1: (0, ih(g0, g1)[1] + off // hp))
    raise ValueError(kind)


def _ew_grid(rows, tb, nh, hp, order):
    assert nh % hp == 0 and rows % tb == 0
    return (rows // tb, nh // hp) if order == "ih" else (nh // hp, rows // tb)


def _ew_load(ref, kind, width, hh):
    if kind in ("row", "par"):
        return ref[...].astype(F32)
    return ref[:, hh * width:(hh + 1) * width].astype(F32)


def ew_fwd(name, f, ins, outs, rows, nh=1, tb=ROW_TILE, order="ih", hp=None, after=()):
    hp = nh if hp is None else hp
    n_in = len(ins)

    def body(*refs):
        hb = pl.program_id(1) if order == "ih" else pl.program_id(0)
        for hh in range(hp):
            h = hh if hp == nh else hb * hp + hh
            vals = [_ew_load(r, kd, w, hh) for r, (_, kd, _, w) in zip(refs[:n_in], ins)]
            res = f(h, *vals)
            for r, v, (_, kd, w, _) in zip(refs[n_in + len(after):], res, outs):
                if kd == "row":
                    assert hp == 1
                    r[...] = v.astype(r.dtype)
                else:
                    r[:, hh * w:(hh + 1) * w] = v.astype(r.dtype)

    in_specs = [_ew_spec(kd, off, w, tb, hp, order, a.shape) for (a, kd, off, w) in ins]
    out_specs = [_ew_spec(kd, 0, w, tb, hp, order) for (_, kd, w, _) in outs]
    out_shape = [jax.ShapeDtypeStruct((rows, tw), dt) for (tw, _, _, dt) in outs]
    return pl.pallas_call(
        body, name=name, grid=_ew_grid(rows, tb, nh, hp, order), in_specs=in_specs + [ANY] * len(after), out_specs=out_specs,
        out_shape=out_shape, compiler_params=_params(2),
    )(*[a for (a, _, _, _) in ins], *after)


def ew_bwd(name, f, ins, cts, extras, emit, outs, rows, nh=1, tb=ROW_TILE, order="ih", hp=None):
    hp = nh if hp is None else hp
    n_in = len(ins)
    flat_cts = [d for group in cts for d in group]
    n_ct, n_ex = len(flat_cts), len(extras)

    def body(*refs):
        g0, g1 = pl.program_id(0), pl.program_id(1)
        hb = g1 if order == "ih" else g0
        out_refs = refs[n_in + n_ct + n_ex:]
        shared = [None] * len(outs)

        def store(r, v, first, sl=None):
            def put(val, add):
                if sl is None:
                    r[...] = (r[...] + val if add else val).astype(r.dtype)
                else:
                    r[:, sl] = (r[:, sl] + val if add else val).astype(r.dtype)

            if first is None:
                put(v, False)
            else:
                pl.when(first)(lambda: put(v, False))
                pl.when(jnp.logical_not(first))(lambda: put(v, True))

        for hh in range(hp):
            h = hh if hp == nh else hb * hp + hh
            vals = [_ew_load(r, kd, w, hh) for r, (_, kd, _, w) in zip(refs[:n_in], ins)]
            ct_refs = list(zip(refs[n_in:n_in + n_ct], flat_cts))
            ct_vals, pos = [], 0
            for group in cts:
                v = None
                for r, (_, kd, _, w) in ct_refs[pos:pos + len(group)]:
                    t = _ew_load(r, kd, w, hh)
                    v = t if v is None else v + t
                pos += len(group)
                ct_vals.append(v)
            ex_vals = [_ew_load(r, kd, w, hh) for r, (_, kd, _, w) in zip(refs[n_in + n_ct:n_in + n_ct + n_ex], extras)]
            _, vjp = jax.vjp(lambda *a: f(h, *a), *vals)
            res = emit(vjp(tuple(ct_vals)), ex_vals)
            for idx, (r, v, (_, kd, w, _, acc)) in enumerate(zip(out_refs, res, outs)):
                if kd in ("row", "par"):
                    shared[idx] = v if shared[idx] is None else shared[idx] + v
                else:
                    store(r, v, (g1 == 0) if acc == "inner" else None, slice(hh * w, (hh + 1) * w))
        for idx, (r, (_, kd, _, _, acc)) in enumerate(zip(out_refs, outs)):
            if kd in ("row", "par"):
                assert acc == "all" or hp == nh
                store(r, shared[idx], jnp.logical_and(g0 == 0, g1 == 0) if acc == "all" else None)

    operands = list(ins) + flat_cts + list(extras)
    in_specs = [_ew_spec(kd, off, w, tb, hp, order, a.shape) for (a, kd, off, w) in operands]
    out_specs = [_ew_spec(kd, 0, w, tb, hp, order, shp) for (shp, kd, w, _, _) in outs]
    out_shape = [jax.ShapeDtypeStruct(shp, dt) for (shp, _, _, dt, _) in outs]
    return pl.pallas_call(
        body, name=name, grid=_ew_grid(rows, tb, nh, hp, order), in_specs=in_specs, out_specs=out_specs,
        out_shape=out_shape, compiler_params=_params(2),
    )(*[a for (a, _, _, _) in operands])


def f_rms(h, x, g):
    r = lax.rsqrt(jnp.mean(x * x, axis=-1, keepdims=True) + EPS)
    return (x * r * g,)


def _softplus(z):
    return jnp.maximum(z, 0.0) + jnp.log1p(jnp.exp(-jnp.abs(z)))


def f_small(h, sp, p1, p2):
    lane = _iota(sp.shape, 1)
    z = sp + p1
    g = -jnp.exp(p2) * _softplus(z)
    beta = jax.nn.sigmoid(z)
    logf = -_softplus(-z)
    return (jnp.where(lane < NH, g, jnp.where(lane < 2 * NH, beta, jnp.where(lane < 3 * NH, logf, 0.0))),)


def _pick(x, lane_id):
    lane = _iota(x.shape, 1)
    col = jnp.sum(jnp.where(lane == lane_id, x, 0.0), axis=1, keepdims=True)
    return jnp.broadcast_to(col, x.shape)


def f_bcast(h, so, cs):
    return _pick(so, h), _pick(so, h + NH), _pick(cs, h + 2 * NH)


def _shift_down(s):
    def down(x):
        r = pltpu.roll(x, s, 0)
        head = jnp.where(_iota((8, x.shape[1]), 0) >= s, r[:8], 0.0)
        return jnp.concatenate([head, r[8:]], axis=0)

    def up(g):
        n = g.shape[0]
        r = pltpu.roll(g, n - s, 0)
        tail = jnp.where(_iota((8, g.shape[1]), 0) < 8 - s, r[n - 8:], 0.0)
        return jnp.concatenate([r[:n - 8], tail], axis=0)

    @jax.custom_vjp
    def shift(x):
        return down(x)

    shift.defvjp(lambda x: (down(x), None), lambda _, g: (up(g),))
    return shift


def _silu(x):
    return x * jax.nn.sigmoid(x)


def make_f_conv(mode):
    sh1, sh2, sh3 = _shift_down(1), _shift_down(2), _shift_down(3)

    def f(h, x, w):
        sub = _iota(w.shape, 0)

        def tap(i):
            return jnp.sum(jnp.where(sub == i, w, 0.0), axis=0, keepdims=True)

        y = sh3(x) * tap(0)
        y = y + sh2(x) * tap(1)
        y = y + sh1(x) * tap(2)
        y = y + x * tap(3)
        s = _silu(y)
        if mode == "v":
            return (s,)
        n = s * lax.rsqrt(jnp.sum(s * s, axis=-1, keepdims=True) + EPS)
        if mode == "q":
            n = n * (LANES ** -0.5)
        return (n,)

    return f


def f_post(h, o, z, g):
    r = lax.rsqrt(jnp.mean(o * o, axis=-1, keepdims=True) + EPS)
    return (o * r * g * _silu(z),)


def f_merge(h, ga, gb, ya, yb):
    return (jax.nn.sigmoid(ga) * ya + jax.nn.sigmoid(gb) * yb,)


def cumsum_time(name, x, nseq, seq, reverse):
    nb = seq // LANES

    def body(x_ref, o_ref):
        r, c = _iota((LANES, LANES), 0), _iota((LANES, LANES), 1)
        tri = jnp.where((r <= c) if reverse else (r >= c), 1.0, 0.0).astype(F32)
        carry = jnp.zeros((1, LANES), F32)
        for b in (range(nb - 1, -1, -1) if reverse else range(nb)):
            blk = x_ref[b * LANES:(b + 1) * LANES, :]
            o_ref[b * LANES:(b + 1) * LANES, :] = _dot_mask(tri, blk, "nn") + carry
            carry = carry + jnp.sum(blk, axis=0, keepdims=True)

    spec = pl.BlockSpec((seq, LANES), lambda s: (s, 0))
    return pl.pallas_call(body, name=name, grid=(nseq,), in_specs=[spec], out_specs=spec,
                          out_shape=jax.ShapeDtypeStruct(x.shape, F32), compiler_params=_params(1))(x)


def transpose_time(name, x, nseq, seq):
    def body(x_ref, o_ref):
        o_ref[...] = x_ref[...].T

    return pl.pallas_call(
        body, name=name, grid=(nseq,), in_specs=[pl.BlockSpec((seq, LANES), lambda s: (s, 0))],
        out_specs=pl.BlockSpec((LANES, seq), lambda s: (s, 0)),
        out_shape=jax.ShapeDtypeStruct((nseq * LANES, seq), F32), compiler_params=_params(1))(x)


def _gdn_masks():
    n = GDN_ROWS
    r, c = _iota((n, n), 0), _iota((n, n), 1)
    shift = GDN_CHUNK.bit_length() - 1
    same = lax.shift_right_logical(r, shift) == lax.shift_right_logical(c, shift)
    return r, c, same


def _each(fn, *lists):
    return [fn(*xs) for xs in zip(*lists)]


def _gdn_decay(gbs):
    r, c, same = _gdn_masks()
    seg_tril = jnp.where(jnp.logical_and(same, r >= c), 1.0, 0.0).astype(F32)
    g_cum = _each(lambda gb: mm_mask(seg_tril, gb), gbs)
    lane0 = _iota(gbs[0].shape, 1) == 0
    g_col = _each(lambda g: jnp.sum(jnp.where(lane0, g, 0.0), axis=1, keepdims=True), g_cum)
    g_row = _each(lambda g: jnp.sum(jnp.where(r == c, jnp.broadcast_to(g, (GDN_ROWS, GDN_ROWS)), 0.0), axis=0, keepdims=True), g_col)
    return g_cum, _each(lambda a, b: a - b, g_col, g_row)


def gdn_a_mats(ks, bbs, diff):
    r, c, same = _gdn_masks()
    strict = jnp.logical_and(same, r > c)
    lane0 = _iota(bbs[0].shape, 1) == 0
    beta_col = _each(lambda bb: jnp.sum(jnp.where(lane0, bb, 0.0), axis=1, keepdims=True), bbs)
    kk = _each(lambda k: _dot(k, k, "nt", LO), ks)
    return _each(lambda b, x, d: jnp.where(strict, b * x * jnp.exp(jnp.where(strict, d, 0.0)), 0.0), beta_col, kk, diff)


@jax.custom_vjp
def saved_inverse(a, t_corr):
    return t_corr


def _saved_inverse_bwd(t, dt):
    left = dt + _dot(t, dt, "tn", LO)
    return -(left + _dot(left, t, "nt", LO)), jnp.zeros_like(t)


saved_inverse.defvjp(lambda a, t_corr: (t_corr, t_corr), _saved_inverse_bwd)


def gdn_block(*args):
    ts, qs, ks, vs, gbs, bbs = (list(args[i::6]) for i in range(6))
    g_cum, diff = _gdn_decay(gbs)
    ts = _each(saved_inverse, gdn_a_mats(ks, bbs, diff), ts)
    return gdn_outputs(ts, qs, ks, vs, gbs, bbs, g_cum, diff)


def gdn_outputs(ts, qs, ks, vs, gbs, bbs, g_cum, diff):
    r, c, same = _gdn_masks()
    incl = jnp.logical_and(same, r >= c)
    decay = _each(lambda d: jnp.where(incl, jnp.exp(jnp.where(incl, d, 0.0)), 0.0), diff)
    e_g = _each(jnp.exp, g_cum)
    v_beta = _each(lambda v, bb: v * bb, vs, bbs)
    k_beta = _each(lambda k, bb, e: k * bb * e, ks, bbs, e_g)
    value = _each(lambda t, x: x + _dot(t, x, "nn", LO), ts, v_beta)
    k_cum = _each(lambda t, x: x + _dot(t, x, "nn", LO), ts, k_beta)
    attn = _each(lambda q, k, d: _dot(q, k, "nt", LO) * d, qs, ks, decay)
    ones = jnp.where(same, 1.0, 0.0).astype(F32)
    g_last = _each(lambda gb: mm_mask(ones, gb), gbs)
    q_dec = _each(lambda q, e: q * e, qs, e_g)
    k_dec = _each(lambda k, gl, g: k * jnp.exp(gl - g), ks, g_last, g_cum)
    return tuple(x for head in zip(value, k_cum, attn, q_dec, k_dec) for x in head)


def tri_inverse(mats):
    n = GDN_ROWS
    r, c = _iota((n, n), 0), _iota((n, n), 1)
    shift = GDN_BASE.bit_length() - 1
    blk = lax.shift_right_logical(r, shift) == lax.shift_right_logical(c, shift)
    each = lambda fn, *lists: [fn(*xs) for xs in zip(*lists)]
    mm = lambda x, y: _dot(x, y, "nn", LO)
    d = each(lambda a: jnp.where(blk, a, 0.0), mats)
    lo = each(lambda a, dd: a - dd, mats, d)
    p = each(lambda dd: -dd, d)
    c_d = p
    for _ in range(shift - 1):
        p = each(mm, p, p)
        c_d = each(lambda cd, pp, prod: cd + pp + prod, c_d, p, each(mm, c_d, p))
    assert GDN_CHUNK // GDN_BASE == 4
    nmat = each(lambda l, prod: l + prod, lo, each(mm, c_d, lo))
    n2 = each(mm, nmat, nmat)
    c_n = each(lambda nn2, nm, prod: (nn2 - nm) - prod, n2, nmat, each(mm, nmat, n2))
    return each(lambda cn, cd, prod: cn + cd + prod, c_n, c_d, each(mm, c_n, c_d))


GDN_AHP = 4


def _gdn_a_specs():
    blk = pl.BlockSpec((GDN_ROWS, GDN_AHP * LANES), lambda i, h: (i, h))
    sq = pl.BlockSpec((GDN_ROWS, GDN_AHP * GDN_ROWS), lambda i, h: (i, h))
    return blk, sq


def _head(ref, hh):
    width = ref.shape[1] // GDN_AHP
    return ref.at[:, hh * width:(hh + 1) * width]


def gdn_a_fwd(q, k, v, gb, bb, rows):
    blk, sq = _gdn_a_specs()

    def body(q_ref, k_ref, v_ref, gb_ref, bb_ref, val_ref, kc_ref, at_ref, qd_ref, kd_ref, t_ref):
        heads = [[_head(r, hh)[...] for r in (q_ref, k_ref, v_ref, gb_ref, bb_ref)] for hh in range(GDN_AHP)]
        qs, ks, vs, gbs, bbs = (list(col) for col in zip(*heads))
        g_cum, diff = _gdn_decay(gbs)
        t_corr = tri_inverse(gdn_a_mats(ks, bbs, diff))
        res = gdn_outputs(t_corr, qs, ks, vs, gbs, bbs, g_cum, diff)
        for hh in range(GDN_AHP):
            for r, x in zip((val_ref, kc_ref, at_ref, qd_ref, kd_ref, t_ref), (*res[5 * hh:5 * hh + 5], t_corr[hh])):
                _head(r, hh)[...] = x.astype(r.dtype)

    wide = lambda dt: jax.ShapeDtypeStruct((rows, NH * LANES), dt)
    square = jax.ShapeDtypeStruct((rows, NH * GDN_ROWS), BF16)
    return pl.pallas_call(
        body, name="gdn_a_fwd", grid=(rows // GDN_ROWS, NH // GDN_AHP), in_specs=[blk] * 5,
        out_specs=[blk, blk, sq, blk, blk, sq], out_shape=[wide(F32), wide(BF16), square, wide(BF16), wide(BF16), square],
        compiler_params=_params(2))(q, k, v, gb, bb)


def gdn_a_bwd(q, k, v, gb, bb, t_inv, dval, dkc, dat, dqd, dkd, dgb_b, rows):
    blk, sq = _gdn_a_specs()

    def body(q_ref, k_ref, v_ref, gb_ref, bb_ref, t_ref, dval_ref, dkc_ref, dat_ref, dqd_ref, dkd_ref, dgbb_ref,
             dq_ref, dk_ref, dv_ref, dgb_ref, dbb_ref):
        hs = range(GDN_AHP)
        heads = [[_head(r, hh)[...] for r in (q_ref, k_ref, v_ref, gb_ref, bb_ref)] for hh in hs]
        tvs = [_head(t_ref, hh)[...].astype(F32) for hh in hs]
        _, vjp = jax.vjp(gdn_block, *[x for t, head in zip(tvs, heads) for x in (t, *head)])
        grads = vjp(tuple(_head(r, hh)[...] for hh in hs for r in (dval_ref, dkc_ref, dat_ref, dqd_ref, dkd_ref)))
        for hh in hs:
            _, dq, dk, dv, dgb, dbb = grads[6 * hh:6 * hh + 6]
            _head(dq_ref, hh)[...] = dq
            _head(dk_ref, hh)[...] = dk
            _head(dv_ref, hh)[...] = dv
            _head(dgb_ref, hh)[...] = dgb + _head(dgbb_ref, hh)[...]
            _head(dbb_ref, hh)[...] = dbb

    wide = jax.ShapeDtypeStruct((rows, NH * LANES), F32)
    return pl.pallas_call(
        body, name="gdn_a_bwd", grid=(rows // GDN_ROWS, NH // GDN_AHP),
        in_specs=[blk] * 5 + [sq, blk, blk, sq, blk, blk, blk], out_specs=[blk] * 5, out_shape=[wide] * 5,
        compiler_params=_params(2))(q, k, v, gb, bb, t_inv, dval, dkc, dat, dqd, dkd, dgb_b)


N_CH = GDN_ROWS // GDN_CHUNK


GDN_HP = 8


def gdn_chunk(c):
    def f(*args):
        val, kc, at, qd, kd, gb, s = (list(args[i::7]) for i in range(7))
        zero = jnp.zeros((GDN_CHUNK, LANES), F32)
        v_new = _each(lambda v, k, st: v - _dot(k, st, "nn", LO), val, kc, s)
        v_pad = _each(lambda v: jnp.concatenate([zero] * c + [v] + [zero] * (N_CH - 1 - c), axis=0), v_new)
        out = _each(lambda q, st, a, vp: _dot(q, st, "nn", LO) + _dot(a, vp, "nn", LO), qd, s, at, v_pad)
        dec = _each(lambda g: jnp.exp(jnp.sum(g, axis=0, keepdims=True)), gb)
        s_new = _each(lambda st, d, k, v: st * d + _dot(k, v, "tn", LO), s, dec, kd, v_new)
        return tuple(x for head in zip(out, s_new) for x in head)

    return f


def _gdn_piece(ref, hh, c):
    width = ref.shape[1] // GDN_HP
    return ref.at[c * GDN_CHUNK:(c + 1) * GDN_CHUNK, hh * width:(hh + 1) * width]


def _gdn_snap(ref, hh, c):
    row = (hh * N_CH + c) * LANES
    return ref.at[row:row + LANES, :]


def _gdn_b_specs(nb, rev):
    def blk_row(s, j):
        return s * nb + (nb - 1 - j if rev else j)

    blk = pl.BlockSpec((GDN_ROWS, GDN_HP * LANES), lambda s, hb, j: (blk_row(s, j), hb))
    sq = pl.BlockSpec((GDN_ROWS, GDN_HP * GDN_ROWS), lambda s, hb, j: (blk_row(s, j), hb))
    snap = pl.BlockSpec((GDN_HP * N_CH * LANES, LANES), lambda s, hb, j: (blk_row(s, j) * (NH // GDN_HP) + hb, 0))
    return blk, sq, snap


def gdn_b_fwd(val, kc, at, qd, kd, gb, nseq, seq):
    nb = seq // GDN_ROWS
    rows = nseq * seq
    blk, sq, snap = _gdn_b_specs(nb, False)

    def body(val_ref, kc_ref, at_ref, qd_ref, kd_ref, gb_ref, o_ref, snap_ref, s_ref):
        @pl.when(pl.program_id(2) == 0)
        def _():
            s_ref[...] = jnp.zeros_like(s_ref)

        hs = range(GDN_HP)
        states = [s_ref[hh] for hh in hs]
        for c in range(N_CH):
            for hh in hs:
                _gdn_snap(snap_ref, hh, c)[...] = states[hh]
            res = gdn_chunk(c)(*[x for hh in hs for x in (
                *[_gdn_piece(r, hh, c)[...].astype(F32) for r in (val_ref, kc_ref, at_ref, qd_ref, kd_ref, gb_ref)], states[hh])])
            for hh in hs:
                _gdn_piece(o_ref, hh, c)[...] = res[2 * hh]
            states = [res[2 * hh + 1] for hh in hs]
        for hh in hs:
            s_ref[hh] = states[hh]

    return pl.pallas_call(
        body, name="gdn_b_fwd", grid=(nseq, NH // GDN_HP, nb), in_specs=[blk, blk, sq, blk, blk, blk], out_specs=[blk, snap],
        out_shape=[jax.ShapeDtypeStruct((rows, NH * LANES), F32),
                   jax.ShapeDtypeStruct((nseq * nb * NH * N_CH * LANES, LANES), F32)],
        scratch_shapes=[pltpu.VMEM((GDN_HP, LANES, LANES), F32)], compiler_params=_params(3))(val, kc, at, qd, kd, gb)


def gdn_b_bwd(val, kc, at, qd, kd, gb, snaps, do, nseq, seq):
    nb = seq // GDN_ROWS
    rows = nseq * seq
    blk, sq, snap = _gdn_b_specs(nb, True)

    def body(val_ref, kc_ref, at_ref, qd_ref, kd_ref, gb_ref, snap_ref, do_ref,
             dval_ref, dkc_ref, dat_ref, dqd_ref, dkd_ref, dgb_ref, ds_ref):
        @pl.when(pl.program_id(2) == 0)
        def _():
            ds_ref[...] = jnp.zeros_like(ds_ref)

        hs = range(GDN_HP)
        d_states = [ds_ref[hh] for hh in hs]
        for c in reversed(range(N_CH)):
            _, vjp = jax.vjp(gdn_chunk(c), *[x for hh in hs for x in (
                *[_gdn_piece(r, hh, c)[...].astype(F32) for r in (val_ref, kc_ref, at_ref, qd_ref, kd_ref, gb_ref)],
                _gdn_snap(snap_ref, hh, c)[...])])
            grads = vjp(tuple(x for hh in hs for x in (_gdn_piece(do_ref, hh, c)[...], d_states[hh])))
            for hh in hs:
                for i, r in enumerate([dval_ref, dkc_ref, dat_ref, dqd_ref, dkd_ref, dgb_ref]):
                    _gdn_piece(r, hh, c)[...] = grads[7 * hh + i]
            d_states = [grads[7 * hh + 6] for hh in hs]
        for hh in hs:
            ds_ref[hh] = d_states[hh]

    wide = jax.ShapeDtypeStruct((rows, NH * LANES), F32)
    square = jax.ShapeDtypeStruct((rows, NH * GDN_ROWS), F32)
    return pl.pallas_call(
        body, name="gdn_b_bwd", grid=(nseq, NH // GDN_HP, nb), in_specs=[blk, blk, sq, blk, blk, blk, snap, blk],
        out_specs=[blk, blk, sq, blk, blk, blk], out_shape=[wide, wide, square, wide, wide, wide],
        scratch_shapes=[pltpu.VMEM((GDN_HP, LANES, LANES), F32)], compiler_params=_params(3))(val, kc, at, qd, kd, gb, snaps, do)


FOX_Q, FOX_K, FOX_V = 4 * NH, 5 * NH, 6 * NH
FOX_SCALE = LANES ** -0.5


def _head_row(ct_ref, h, off, width):
    blk = ct_ref[:, pl.ds(off, width)]
    return jnp.sum(jnp.where(_iota(blk.shape, 0) == h, blk, 0.0), axis=0, keepdims=True)


def _col(x):
    return jnp.max(x, axis=1, keepdims=True)


def _row(x):
    return jnp.max(x.T, axis=0, keepdims=True)


def _causal(shape, q_dim):
    return _iota(shape, q_dim) >= _iota(shape, 1 - q_dim)


FOX_HP = 4


def _fox_specs(seq, tile, n_tiles):
    tblk = pl.BlockSpec((tile, FOX_HP * LANES), lambda s, h, i: (s * n_tiles + i, h))
    vtblk = pl.BlockSpec((tile, FOX_HP * LANES), lambda s, h, i: (s * n_tiles + i, h + FOX_V // FOX_HP))
    full = pl.BlockSpec((seq, FOX_HP * LANES), lambda s, h, i: (s, h))
    vfull = pl.BlockSpec((seq, FOX_HP * LANES), lambda s, h, i: (s, h + FOX_V // FOX_HP))
    ctb = pl.BlockSpec((NH, seq), lambda s, h, i: (s * (LANES // NH) + 2, 0))
    return tblk, vtblk, full, vfull, ctb


def _lanes_of(hh):
    return slice(hh * LANES, (hh + 1) * LANES)


def fox_fwd(qn, kn, proj, ct, nseq, seq):
    tq = tk = min(ATT_TILE, seq)
    nq = seq // tq
    rows = nseq * seq
    qblk, _, full, vfull, ctb = _fox_specs(seq, tq, nq)
    hs = range(FOX_HP)

    def body(q_ref, k_ref, v_ref, ct_ref, o_ref, o16_ref, lse_ref):
        hb, i = pl.program_id(1), pl.program_id(2)
        q = [q_ref[:, _lanes_of(hh)] for hh in hs]

        def step(j, carry, diag):
            m, l, acc = (list(carry[t::3]) for t in range(3))
            off = pl.multiple_of(j * tk, tk)
            k = [k_ref[pl.ds(off, tk), _lanes_of(hh)] for hh in hs]
            v = [v_ref[pl.ds(off, tk), _lanes_of(hh)].astype(BF16) for hh in hs]
            ck = [_head_row(ct_ref, hb * FOX_HP + hh, off, tk) for hh in hs]
            s = _each(lambda qq, kk, cc: _dot(qq, kk, "nt") * FOX_SCALE - cc, q, k, ck)
            if diag:
                s = _each(lambda x: jnp.where(_causal(x.shape, 0), x, NEG), s)
            m_new = _each(lambda mm, x: jnp.maximum(mm, jnp.max(x, axis=1, keepdims=True)), m, s)
            p = _each(lambda x, mm: jnp.exp(x - mm), s, m_new)
            alpha = _each(lambda mo, mn: jnp.exp(mo - mn), m, m_new)
            l = _each(lambda a, ll, pp: a * ll + jnp.sum(pp, axis=1, keepdims=True), alpha, l, p)
            acc = _each(lambda a, ac, pp, vv: a * ac + _dot(pp.astype(BF16), vv, "nn"), alpha, acc, p, v)
            return tuple(x for head in zip(m_new, l, acc) for x in head)

        init = (jnp.full((tq, 1), NEG, F32), jnp.zeros((tq, 1), F32), jnp.zeros((tq, LANES), F32)) * FOX_HP
        res = step(i, lax.fori_loop(0, i, lambda j, c: step(j, c, False), init), True)
        for hh in hs:
            m, l, acc = res[3 * hh:3 * hh + 3]
            o = acc / l
            o_ref[:, _lanes_of(hh)] = o
            o16_ref[:, _lanes_of(hh)] = o.astype(BF16)
            lse_ref[:, _lanes_of(hh)] = jnp.broadcast_to(m + jnp.log(l), (tq, LANES))

    wide = (rows, NH * LANES)
    return pl.pallas_call(
        body, name="fox_fwd", grid=(nseq, NH // FOX_HP, nq), in_specs=[qblk, full, vfull, ctb], out_specs=[qblk] * 3,
        out_shape=[jax.ShapeDtypeStruct(wide, F32), jax.ShapeDtypeStruct(wide, BF16), jax.ShapeDtypeStruct(wide, F32)],
        compiler_params=_params(3))(qn, kn, proj, ct)


def fox_dq(qn, kn, proj, ct, do, lse, o, after, nseq, seq):
    tq = tk = min(ATT_TILE, seq)
    nq = seq // tq
    rows = nseq * seq
    qblk, _, full, vfull, ctb = _fox_specs(seq, tq, nq)
    hs = range(FOX_HP)

    def body(q_ref, k_ref, v_ref, ct_ref, do_ref, lse_ref, o_ref, *rest):
        dq_ref, dc_ref = rest[len(after):]
        hb, i = pl.program_id(1), pl.program_id(2)
        q = [q_ref[:, _lanes_of(hh)] for hh in hs]
        lse = [_col(lse_ref[:, _lanes_of(hh)]) for hh in hs]
        delta = [jnp.sum(do_ref[:, _lanes_of(hh)] * o_ref[:, _lanes_of(hh)], axis=1, keepdims=True) for hh in hs]
        do16 = [do_ref[:, _lanes_of(hh)].astype(BF16) for hh in hs]

        def step(j, carry, diag):
            dq, dc = (list(carry[t::2]) for t in range(2))
            off = pl.multiple_of(j * tk, tk)
            k = [k_ref[pl.ds(off, tk), _lanes_of(hh)] for hh in hs]
            v = [v_ref[pl.ds(off, tk), _lanes_of(hh)].astype(BF16) for hh in hs]
            ck = [_head_row(ct_ref, hb * FOX_HP + hh, off, tk) for hh in hs]
            p = _each(lambda qq, kk, cc, ll: jnp.exp(_dot(qq, kk, "nt") * FOX_SCALE - cc - ll), q, k, ck, lse)
            if diag:
                p = _each(lambda x: jnp.where(_causal(x.shape, 0), x, 0.0), p)
            dp = _each(lambda d, vv: _dot(d, vv, "nt"), do16, v)
            ds = _each(lambda pp, d, dl: pp * (d - dl), p, dp, delta)
            dq = _each(lambda a, x, kk: a + _dot(x.astype(BF16), kk, "nn"), dq, ds, k)
            dc = _each(lambda a, x: a + jnp.sum(x, axis=1, keepdims=True), dc, ds)
            return tuple(x for head in zip(dq, dc) for x in head)

        init = (jnp.zeros((tq, LANES), F32), jnp.zeros((tq, 1), F32)) * FOX_HP
        res = step(i, lax.fori_loop(0, i, lambda j, c: step(j, c, False), init), True)
        for hh in hs:
            dq_ref[:, _lanes_of(hh)] = res[2 * hh] * FOX_SCALE
            dc_ref[:, _lanes_of(hh)] = jnp.where(_iota((tq, LANES), 1) == 0, res[2 * hh + 1], 0.0)

    wide = jax.ShapeDtypeStruct((rows, NH * LANES), F32)
    return pl.pallas_call(
        body, name="fox_dq", grid=(nseq, NH // FOX_HP, nq), in_specs=[qblk, full, vfull, ctb, qblk, qblk, qblk] + [ANY] * len(after),
        out_specs=[qblk, qblk], out_shape=[wide, wide], compiler_params=_params(3))(qn, kn, proj, ct, do, lse, o, *after)


def fox_dkv(qn, kn, proj, cb, do, lse, o, after, nseq, seq):
    tq = tk = min(ATT_TILE, seq)
    nq = seq // tq
    rows = nseq * seq
    kblk, vblk, full, _, _ = _fox_specs(seq, tk, nq)
    hs = range(FOX_HP)

    def body(q_ref, k_ref, v_ref, cb_ref, do_ref, lse_ref, o_ref, *rest):
        dk_ref, dv_ref, dc_ref = rest[len(after):]
        j = pl.program_id(2)
        k = [k_ref[:, _lanes_of(hh)] for hh in hs]
        v16 = [v_ref[:, _lanes_of(hh)].astype(BF16) for hh in hs]
        ck = [_col(cb_ref[:, _lanes_of(hh)]) for hh in hs]

        def step(i, carry, diag):
            dk, dv, dc = (list(carry[t::3]) for t in range(3))
            off = pl.multiple_of(i * tq, tq)
            q = [q_ref[pl.ds(off, tq), _lanes_of(hh)] for hh in hs]
            do32 = [do_ref[pl.ds(off, tq), _lanes_of(hh)] for hh in hs]
            do16 = [d.astype(BF16) for d in do32]
            lse = [_row(lse_ref[pl.ds(off, tq), _lanes_of(hh)]) for hh in hs]
            delta = [_row(jnp.broadcast_to(jnp.sum(d * o_ref[pl.ds(off, tq), _lanes_of(hh)], axis=1, keepdims=True), (tq, LANES)))
                     for hh, d in zip(hs, do32)]
            p = _each(lambda kk, qq, cc, ll: jnp.exp(_dot(kk, qq, "nt") * FOX_SCALE - cc - ll), k, q, ck, lse)
            if diag:
                p = _each(lambda x: jnp.where(_causal(x.shape, 1), x, 0.0), p)
            dv = _each(lambda a, pp, d: a + _dot(pp.astype(BF16), d, "nn"), dv, p, do16)
            ds = _each(lambda pp, vv, d, dl: pp * (_dot(vv, d, "nt") - dl), p, v16, do16, delta)
            dk = _each(lambda a, x, qq: a + _dot(x.astype(BF16), qq, "nn"), dk, ds, q)
            dc = _each(lambda a, x: a + jnp.sum(x, axis=1, keepdims=True), dc, ds)
            return tuple(x for head in zip(dk, dv, dc) for x in head)

        zero = jnp.zeros((tk, LANES), F32)
        carry = step(j, (zero, zero, jnp.zeros((tk, 1), F32)) * FOX_HP, True)
        res = lax.fori_loop(j + 1, nq, lambda i, c: step(i, c, False), carry)
        for hh in hs:
            dk, dv, dc = res[3 * hh:3 * hh + 3]
            dk_ref[:, _lanes_of(hh)] = dk * FOX_SCALE
            dv_ref[:, _lanes_of(hh)] = dv.astype(BF16)
            dc_ref[:, _lanes_of(hh)] = jnp.where(_iota((tk, LANES), 1) == 0, -dc, 0.0)

    wide = (rows, NH * LANES)
    return pl.pallas_call(
        body, name="fox_dkv", grid=(nseq, NH // FOX_HP, nq), in_specs=[full, kblk, vblk, kblk, full, full, full] + [ANY] * len(after),
        out_specs=[kblk, kblk, kblk],
        out_shape=[jax.ShapeDtypeStruct(wide, F32), jax.ShapeDtypeStruct(wide, BF16), jax.ShapeDtypeStruct(wide, F32)],
        compiler_params=_params(3))(qn, kn, proj, cb, do, lse, o, *after)


def _adamw_update(w, g, m, v):
    m_new = ADAM_B1 * m + (1.0 - ADAM_B1) * g
    v_new = ADAM_B2 * v + (1.0 - ADAM_B2) * (g * g)
    m_hat = m_new / (1.0 - ADAM_B1 ** ADAM_STEP)
    v_hat = v_new / (1.0 - ADAM_B2 ** ADAM_STEP)
    return -ADAM_LR * (m_hat / (jnp.sqrt(v_hat) + ADAM_EPS) + ADAM_WD * w), m_new, v_new


def adamw(name, w, g, m, v):
    rows, cols = w.shape
    tb = min(rows, 128)
    assert rows % tb == 0
    blk = pl.BlockSpec((tb, cols), lambda i: (i, 0))

    def body(w_ref, g_ref, m_ref, v_ref, d_ref, mo_ref, vo_ref):
        d_ref[...], mo_ref[...], vo_ref[...] = _adamw_update(w_ref[...], g_ref[...], m_ref[...], v_ref[...])

    shp = jax.ShapeDtypeStruct(w.shape, F32)
    return pl.pallas_call(body, name=name, grid=(rows // tb,), in_specs=[blk] * 4, out_specs=[blk] * 3,
                          out_shape=[shp] * 3, compiler_params=_params(1))(w, g, m, v)


SPLIT_TILE = 128


def _tiled(shape2d, ax, n_lead, index):
    blk = (SPLIT_TILE, shape2d[1]) if ax == 0 else (shape2d[0], SPLIT_TILE)

    def index_map(*args):
        *lead, t = index(*args)
        return (*lead, t, 0) if ax == 0 else (*lead, 0, t)

    return pl.BlockSpec((None,) * n_lead + blk, index_map)


def adamw_halves(name, w, mine, other, m, v, c, ax):
    steps = w.shape[ax] // 2 // SPLIT_TILE
    assert w.shape[ax] == 2 * steps * SPLIT_TILE

    def body(c_ref, w_ref, mine_ref, other_ref, m_ref, v_ref, g_ref, d_ref, mo_ref, vo_ref):
        g = jnp.where(pl.program_id(0) // steps == c_ref[0], mine_ref[...], other_ref[...])
        g_ref[...] = g
        d_ref[...], mo_ref[...], vo_ref[...] = _adamw_update(w_ref[...], g, m_ref[...], v_ref[...])

    blk = _tiled(w.shape, ax, 0, lambda i, c_ref: (i,))
    hblk = _tiled(mine.shape, ax, 0, lambda i, c_ref: (i % steps,))
    grid_spec = pltpu.PrefetchScalarGridSpec(num_scalar_prefetch=1, grid=(2 * steps,),
                                             in_specs=[blk, hblk, hblk, blk, blk], out_specs=[blk] * 4)
    shp = jax.ShapeDtypeStruct(w.shape, F32)
    return pl.pallas_call(body, name=name, grid_spec=grid_spec, out_shape=[shp] * 4,
                          compiler_params=_params(1))(c, w, mine, other, m, v)


def add_chips(name, slots, parts, chip, axes):
    outs = []
    for idx, (x, own, ax) in enumerate(zip(slots, parts, axes)):
        n, shape2d = x.shape[0], x.shape[1:]
        steps = shape2d[ax] // SPLIT_TILE
        assert shape2d[ax] == steps * SPLIT_TILE

        def body(me_ref, *refs, n=n):
            o_ref = refs[n + 1]
            acc = None
            for t in range(n):
                term = jnp.where(me_ref[0] == t, refs[n][...], refs[t][...]).astype(F32)
                acc = term if acc is None else acc + term
            o_ref[...] = acc

        def filled(t, n=n):
            return lambda i, me_ref: (jnp.where(me_ref[0] == t, (t + 1) % n, t), i)

        grid_spec = pltpu.PrefetchScalarGridSpec(
            num_scalar_prefetch=1, grid=(steps,),
            in_specs=[_tiled(shape2d, ax, 1, filled(t)) for t in range(n)]
            + [_tiled(shape2d, ax, 1, lambda i, me_ref: (me_ref[0], i))],
            out_specs=_tiled(shape2d, ax, 0, lambda i, me_ref: (i,)))
        outs.append(pl.pallas_call(
            body, name=f"{name}_{idx}", grid_spec=grid_spec, out_shape=jax.ShapeDtypeStruct(shape2d, F32),
            compiler_params=_params(1))(chip, *([x] * n), own))
    return outs


def add_pair(name, gs, rs, c, axes):
    outs = []
    for idx, (g, r, ax) in enumerate(zip(gs, rs, axes)):
        nb = r.shape[0]
        steps = r.shape[1 + ax] // SPLIT_TILE
        assert r.shape[1 + ax] == steps * SPLIT_TILE

        def body(c_ref, g_ref, r_ref, o_ref):
            o_ref[...] = (g_ref[...] + r_ref[...]).astype(BF16)

        grid_spec = pltpu.PrefetchScalarGridSpec(
            num_scalar_prefetch=1, grid=(nb, steps),
            in_specs=[_tiled(g.shape[1:], ax, 1, lambda b, i, c_ref: (b, c_ref[0] * steps + i)),
                      _tiled(r.shape[1:], ax, 1, lambda b, i, c_ref: (b, i))],
            out_specs=_tiled(r.shape[1:], ax, 1, lambda b, i, c_ref: (b, i)))
        outs.append(pl.pallas_call(
            body, name=f"{name}_{idx}", grid_spec=grid_spec, out_shape=jax.ShapeDtypeStruct(r.shape, BF16),
            compiler_params=_params(2))(c, g, r))
    return outs


def _place():
    x, y, c = lax.axis_index("x"), lax.axis_index("y"), lax.axis_index("c")
    return x, y, c, [(1 - x, y), (x, 1 - y), (1 - x, 1 - y)]


def _remote(src, dst, send_sem, recv_sem, dev):
    return pltpu.make_async_remote_copy(src_ref=src, dst_ref=dst, send_sem=send_sem, recv_sem=recv_sem,
                                        device_id=dev, device_id_type=MESH)


def _half(ref, lead, ax, which):
    size = ref.shape[len(lead) + ax] // 2
    part = pl.ds(which * size, size)
    return ref.at[(*lead, part, slice(None)) if ax == 0 else (*lead, slice(None), part)]


def gather_ring(shard):
    rows, cols = shard.shape
    half = cols // 2
    top = rows // 2 // 16 * 16
    assert shard.dtype == BF16 and half % LANES == 0

    def body(in_ref, out_ref, ici_s, ici_r, d2d_s, d2d_r):
        x, y, c, _ = _place()
        me, xn, yn, dg = 2 * x + y, 2 * (1 - x) + y, 2 * x + (1 - y), 2 * (1 - x) + (1 - y)
        to_x, to_y, sib = (1 - x, y, c), (x, 1 - y, c), (x, y, 1 - c)
        mine, other = pl.ds(c * half, half), pl.ds((1 - c) * half, half)
        upper, lower = pl.ds(0, top), pl.ds(top, rows - top)
        started = []

        def send(src, dst, sems, k, dev):
            cp = _remote(src, dst, sems[0].at[k], sems[1].at[k], dev)
            cp.start()
            started.append(cp)

        def arrive(dst, sems, k):
            _remote(dst, dst, sems[0].at[k], sems[1].at[k], sib).wait_recv()

        ici, d2d = (ici_s, ici_r), (d2d_s, d2d_r)
        send(in_ref, out_ref.at[me], d2d, 0, sib)
        send(in_ref.at[:, mine], out_ref.at[me, :, mine], ici, 0, to_x)
        send(in_ref.at[:, mine], out_ref.at[me, :, mine], ici, 1, to_y)
        arrive(out_ref.at[xn, :, mine], ici, 0)
        send(out_ref.at[xn, upper, mine], out_ref.at[xn, upper, mine], ici, 2, to_y)
        send(out_ref.at[xn, :, mine], out_ref.at[xn, :, mine], d2d, 1, sib)
        arrive(out_ref.at[yn, :, mine], ici, 1)
        send(out_ref.at[yn, lower, mine], out_ref.at[yn, lower, mine], ici, 3, to_x)
        send(out_ref.at[yn, :, mine], out_ref.at[yn, :, mine], d2d, 2, sib)
        arrive(out_ref.at[dg, upper, mine], ici, 2)
        send(out_ref.at[dg, upper, mine], out_ref.at[dg, upper, mine], d2d, 3, sib)
        arrive(out_ref.at[dg, lower, mine], ici, 3)
        send(out_ref.at[dg, lower, mine], out_ref.at[dg, lower, mine], d2d, 4, sib)
        arrive(out_ref.at[me], d2d, 0)
        arrive(out_ref.at[xn, :, other], d2d, 1)
        arrive(out_ref.at[yn, :, other], d2d, 2)
        arrive(out_ref.at[dg, upper, other], d2d, 3)
        arrive(out_ref.at[dg, lower, other], d2d, 4)
        for cp in started:
            cp.wait_send()

    return pl.pallas_call(
        body, name="gather_ring", in_specs=[ANY], out_specs=ANY, out_shape=jax.ShapeDtypeStruct((4,) + shard.shape, shard.dtype),
        scratch_shapes=[pltpu.SemaphoreType.DMA((4,))] * 2 + [pltpu.SemaphoreType.DMA((5,))] * 2,
    )(shard)


HBM = pl.BlockSpec(memory_space=pltpu.HBM)
SEM = pl.BlockSpec(memory_space=pltpu.SEMAPHORE)
DATAFLOW = pltpu.SideEffectType.DATAFLOW_SIDE_EFFECTING


def _hbm(a):
    return pltpu.with_memory_space_constraint(a, pltpu.HBM)


class SplitExchange:
    def __init__(self, name, srcs, zone_shapes, n_sems, plan):
        self.name, self.n, self.n_sems, self.plan = name, len(srcs), n_sems, plan
        self.srcs = [_hbm(s) for s in srcs]
        self.zones = [_hbm(lax.empty(shape, s.dtype)) for shape, s in zip(zone_shapes, srcs)]

    def start(self, after):
        n, n_after = self.n, len(after)

        def body(*refs):
            ins, lands = refs[:n], refs[n:2 * n]
            send, recv, token = refs[2 * n + n_after], refs[2 * n + n_after + 1], refs[-1]
            for src, dst, si, ri, dev in self.plan(ins, lands)[0]:
                _remote(src, dst, send.at[si], recv.at[ri], dev).start()
            token[...] = jnp.zeros_like(token)

        res = pl.pallas_call(
            body, name=f"{self.name}_start", in_specs=[HBM] * (2 * n) + [ANY] * n_after,
            out_specs=[SEM, SEM] + [HBM] * (2 * n) + [pl.BlockSpec(memory_space=pltpu.VMEM)],
            out_shape=[pltpu.SemaphoreType.DMA((self.n_sems,)), pltpu.SemaphoreType.DMA((self.n_sems,))]
            + [pltpu.HBM(a.shape, a.dtype) for a in self.srcs + self.zones] + [jax.ShapeDtypeStruct((8, LANES), F32)],
            input_output_aliases={i: 2 + i for i in range(2 * n)},
            compiler_params=pltpu.CompilerParams(has_side_effects=DATAFLOW),
        )(*self.srcs, *self.zones, *after)
        self.sems, self.srcs, self.zones = res[:2], list(res[2:2 + n]), list(res[2 + n:2 + 2 * n])
        return res[-1]

    def wait(self, after):
        n = self.n

        def body(*refs):
            ins, lands = refs[:n], refs[n:2 * n]
            send, recv = refs[2 * n], refs[2 * n + 1]
            sends, arrivals = self.plan(ins, lands)
            for src, _, si, _, dev in sends:
                _remote(src, src, send.at[si], recv.at[si], dev).wait_send()
            for landed, ri in arrivals:
                _remote(landed, landed, send.at[ri], recv.at[ri], _place()[:3]).wait_recv()

        res = pl.pallas_call(
            body, name=f"{self.name}_wait", in_specs=[HBM] * (2 * n) + [SEM, SEM, ANY], out_specs=[HBM] * (2 * n),
            out_shape=[pltpu.HBM(a.shape, a.dtype) for a in self.srcs + self.zones],
            input_output_aliases={i: i for i in range(2 * n)},
            compiler_params=pltpu.CompilerParams(has_side_effects=DATAFLOW),
        )(*self.srcs, *self.zones, *self.sems, after)
        self.srcs = list(res[:n])
        return list(res[n:])


def split_gather(shards):
    n = len(shards)

    def plan(ins, lands):
        x, y, c, chips = _place()
        me = 2 * x + y
        sends, arrivals = [], []
        for w in range(n):
            for j, (ox, oy) in enumerate(chips):
                for k in range(2):
                    base = 2 * (3 * w + j)
                    sends.append((_half(ins[w], (), 0, c), _half(lands[w], (me,), 0, c), base + k, base + c, (ox, oy, k)))
                    arrivals.append((_half(lands[w], (2 * ox + oy,), 0, k), base + k))
            sends.append((ins[w], lands[w].at[me], 6 * n + w, 6 * n + w, (x, y, 1 - c)))
            arrivals.append((lands[w].at[me], 6 * n + w))
        return sends, arrivals

    return SplitExchange("gather", shards, [(4,) + s.shape for s in shards], 7 * n, plan)


def split_pair_swap(name, grads, axes):
    def plan(ins, lands):
        x, y, c, _ = _place()
        sends = [(_half(ins[w], (slice(None),), axes[w], 1 - c), lands[w], w, w, (x, y, 1 - c)) for w in range(len(ins))]
        return sends, [(lands[w], w) for w in range(len(ins))]

    halved = [tuple(d // 2 if i == 1 + ax else d for i, d in enumerate(g.shape)) for g, ax in zip(grads, axes)]
    return SplitExchange(name, grads, halved, len(grads), plan)


def split_chip_exchange(name, parts):
    def plan(ins, lands):
        x, y, c, chips = _place()
        sends, arrivals = [], []
        for w in range(len(ins)):
            for j, (ox, oy) in enumerate(chips):
                sends.append((ins[w].at[2 * ox + oy], lands[w].at[2 * x + y], 3 * w + j, 3 * w + j, (ox, oy, c)))
                arrivals.append((lands[w].at[2 * ox + oy], 3 * w + j))
        return sends, arrivals

    return SplitExchange(name, parts, [p.shape for p in parts], 3 * len(parts), plan)


def split_pair_send(halves):
    def plan(ins, lands):
        x, y, c, _ = _place()
        return ([(ins[w], lands[w], w, w, (x, y, 1 - c)) for w in range(len(ins))],
                [(lands[w], w) for w in range(len(ins))])

    return SplitExchange("pair_send", halves, [h.shape for h in halves], len(halves), plan)


def pair_send(halves):
    n = len(halves)

    def body(*refs):
        ins, outs = refs[:n], refs[n:2 * n]
        send, recv = refs[2 * n:]
        x, y, c, _ = _place()
        cps = [_remote(ins[w], outs[w], send.at[w], recv.at[w], (x, y, 1 - c)) for w in range(n)]
        for cp in cps:
            cp.start()
        for cp in cps:
            cp.wait_recv()
        for cp in cps:
            cp.wait_send()

    return pl.pallas_call(
        body, name="pair_send", in_specs=[ANY] * n, out_specs=[ANY] * n,
        out_shape=[jax.ShapeDtypeStruct(h.shape, h.dtype) for h in halves],
        scratch_shapes=[pltpu.SemaphoreType.DMA((n,))] * 2,
    )(*halves)


def all_reduce_small(name, vec, after=()):
    rows = vec.shape[0]

    def body(v_ref, *refs):
        o_ref, buf, send, recv = refs[len(after):]
        x, y, c, _ = _place()
        me = 4 * x + 2 * y + c
        buf[me] = v_ref[...]
        cps = []
        for k in range(1, 8):
            kx, ky, kc = (k >> 2) & 1, (k >> 1) & 1, k & 1
            peer = (x if kx == 0 else 1 - x, y if ky == 0 else 1 - y, c if kc == 0 else 1 - c)
            cp = _remote(v_ref, buf.at[me], send.at[k - 1], recv.at[k - 1], peer)
            cp.start()
            cps.append(cp)
        for k in range(1, 8):
            kx, ky, kc = (k >> 2) & 1, (k >> 1) & 1, k & 1
            px, py, pc = (x if kx == 0 else 1 - x, y if ky == 0 else 1 - y, c if kc == 0 else 1 - c)
            slot = buf.at[4 * px + 2 * py + pc]
            _remote(slot, slot, send.at[k - 1], recv.at[k - 1], (px, py, pc)).wait_recv()
        for cp in cps:
            cp.wait_send()
        acc = buf[0]
        for d in range(1, 8):
            acc = acc + buf[d]
        o_ref[...] = acc

    vm = pl.BlockSpec(memory_space=pltpu.VMEM)
    return pl.pallas_call(
        body, name=name, in_specs=[vm] + [ANY] * len(after), out_specs=vm, out_shape=jax.ShapeDtypeStruct(vec.shape, F32),
        scratch_shapes=[pltpu.VMEM((8, rows, LANES), F32), pltpu.SemaphoreType.DMA((7,)), pltpu.SemaphoreType.DMA((7,))],
    )(vec, *after)


class NoExchange:
    def __init__(self, late):
        self.late = late

    def late_weights(self, after):
        return self.late

    def reduce_start(self, grads):
        return jnp.zeros((8, LANES), F32)

    def reduce_exchange(self, after):
        return jnp.zeros((8, LANES), F32)

    def reduce_finish(self, after):
        return jnp.zeros((8, LANES), F32)

    def input_grad_start(self, dw_main, dw_small):
        return jnp.zeros((8, LANES), F32)

    def input_grad_exchange(self, after):
        return jnp.zeros((8, LANES), F32)


def local_step(x2, tgt2, g1, g2, gdn_ng, qn_g, kn_g, p1, p2, conv_w, wt_main, wt_small, hooks, nseq, seq):
    rows, dm = x2.shape
    wide = NH * LANES
    row = lambda a, off=0, w=None: (a, "row", off, a.shape[1] if w is None else w)
    rowh = lambda a, off=0, w=LANES: (a, "rowh", off, w)
    par = lambda a: (a, "par", 0, a.shape[1])
    parh = lambda a, off=0: (a, "parh", off, LANES)
    o_row = lambda w, dt: (w, "row", w, dt)
    o_rowh = lambda dt, tw=wide, w=LANES: (tw, "rowh", w, dt)

    u, = ew_fwd("rms1", f_rms, [row(x2), par(g1)], [o_row(dm, BF16)], rows)
    proj = matmul("mm_in", u, wt_main, "nt", BF16)
    sp = matmul("mm_in_small", u, wt_small, "nt", F32)
    so, = ew_fwd("small", f_small, [row(sp), par(p1), par(p2)], [o_row(LANES, F32)], rows)
    cs = cumsum_time("cumsum", so, nseq, seq, False)
    gb, bb, cb = ew_fwd("bcast", f_bcast, [row(so), row(cs)], [o_rowh(F32)] * 3, rows, NH)
    ct = transpose_time("c_time_major", cs, nseq, seq)
    conv = {}
    for mode, off in (("q", 0), ("k", NH), ("v", 2 * NH)):
        conv[mode], = ew_fwd(f"conv_{mode}", make_f_conv(mode), [rowh(proj, off), parh(conv_w, off)], [o_rowh(F32)],
                             rows, NH, seq, "hi", CONV_HEADS)
    val, kcum, attn, qdec, kdec, t_inv = gdn_a_fwd(conv["q"], conv["k"], conv["v"], gb, bb, rows)
    o_a, snaps = gdn_b_fwd(val, kcum, attn, qdec, kdec, gb, nseq, seq)
    ya_in, = ew_fwd("gdn_post", f_post, [rowh(o_a), rowh(proj, 3 * NH), par(gdn_ng)], [o_rowh(BF16)], rows, NH)
    fqn, = ew_fwd("fox_qn", f_rms, [rowh(proj, FOX_Q), par(qn_g)], [o_rowh(BF16)], rows, NH)
    fkn, = ew_fwd("fox_kn", f_rms, [rowh(proj, FOX_K), par(kn_g)], [o_rowh(BF16)], rows, NH)
    o_b, o_b16, lse = fox_fwd(fqn, fkn, proj, ct, nseq, seq)
    p_a, p_b, w_o, w_u, w_d = hooks.late_weights(o_a)
    y_a = matmul("mm_pa", ya_in, p_a, "nn", F32, tn=1024)
    y_b = matmul("mm_pb", o_b16, p_b, "nn", F32, tn=1024)
    gates = [row(proj, 7, dm), row(proj, 8, dm)]
    merged, = ew_fwd("merge", f_merge, gates + [row(y_a), row(y_b)], [o_row(dm, BF16)], rows)
    hres = matmul("mm_out", merged, w_o, "nn", F32, add=x2, tn=1024)
    hn, = ew_fwd("rms2", f_rms, [row(hres), par(g2)], [o_row(dm, BF16)], rows)
    up_blocks = w_u.shape[0]
    act, relu2 = matmul("mm_up", hn, w_u, "nn", F32, col_blocks=up_blocks, out_dtypes=[F32, BF16],
                        epilogue=lambda r: [r, jnp.maximum(r, 0.0) * jnp.maximum(r, 0.0)])
    def loss_tail(r, h_tile, t_tile):
        d = (r + h_tile) - t_tile
        e = (0.5 / dm) * (d * d)
        part = e.reshape(e.shape[0] // 8, 8, e.shape[1]).sum(axis=0)
        part = sum(part[:, t * LANES:(t + 1) * LANES] for t in range(e.shape[1] // LANES))
        g = d * (1.0 / dm)
        return [g, g, part]

    dout, dout16, loss_acc = matmul("mm_down", relu2, w_d, "nn", F32, extras=[hres, tgt2], epilogue=loss_tail,
                                    out_dtypes=[F32, BF16, F32], tile_sums=True)

    d_act = matmul("mm_d_act", dout16, w_d, "nt", BF16, extras=[act], epilogue=lambda r, a: [2.0 * jnp.maximum(a, 0.0) * r])
    dw_d = matmul("mm_dw_down", relu2, dout16, "tn", F32, tn=1024)
    dw_u = matmul("mm_dw_up", hn, d_act, "tn", F32, col_blocks=up_blocks)
    d_hn = matmul("mm_d_hn", d_act, w_u, "nt", F32, col_blocks=up_blocks)
    dh, dh16, dg2 = ew_bwd("rms2_b", f_rms, [row(hres), par(g2)], [(row(d_hn),)], [row(dout)],
                           lambda g, e: [g[0] + e[0], g[0] + e[0], g[1]],
                           [((rows, dm), "row", dm, F32, None), ((rows, dm), "row", dm, BF16, None), ((1, dm), "par", dm, F32, "all")], rows)
    d_merged = matmul("mm_d_merged", dh16, w_o, "nt", F32, tn=1024)
    dw_o = matmul("mm_dw_out", merged, dh16, "tn", F32, tn=1024)
    seg16 = ((rows, dm), "row", dm, BF16, None)
    d_ga16, d_gb16, d_ya16, d_yb16 = ew_bwd("merge_b", f_merge, gates + [row(y_a), row(y_b)], [(row(d_merged),)], [],
                                            lambda g, e: list(g), [seg16] * 4, rows)
    dp_a = matmul("mm_dp_a", ya_in, d_ya16, "tn", F32, tn=1024)
    d_ya_in = matmul("mm_d_ya_in", d_ya16, p_a, "nt", F32, tn=1024)
    dp_b = matmul("mm_dp_b", o_b16, d_yb16, "tn", F32, tn=1024)
    d_ob = matmul("mm_d_ob", d_yb16, p_b, "nt", F32, tn=1024)
    token = hooks.reduce_start(dict(p_a=dp_a, p_b=dp_b, w_o=dw_o, w_u=dw_u, w_d=dw_d))
    gdn_ng_t = gdn_ng + token[0, 0]
    h32 = ((rows, wide), "rowh", LANES, F32, None)
    h16 = ((rows, wide), "rowh", LANES, BF16, None)
    gain = ((1, LANES), "par", LANES, F32, "all")
    d_oa, d_z16, d_gdn_ng = ew_bwd("gdn_post_b", f_post, [rowh(o_a), rowh(proj, 3 * NH), par(gdn_ng_t)], [(rowh(d_ya_in),)], [],
                                   lambda g, e: list(g), [h32, h16, gain], rows, NH)
    dval, dkc, dat, dqd, dkd, dgb_b = gdn_b_bwd(val, kcum, attn, qdec, kdec, gb, snaps, d_oa, nseq, seq)
    d_cq, d_ck, d_cv, d_gb, d_bb = gdn_a_bwd(conv["q"], conv["k"], conv["v"], gb, bb, t_inv, dval, dkc, dat, dqd, dkd, dgb_b, rows)
    token = hooks.reduce_exchange(d_cq)
    conv_w_t = conv_w + token[0, 0]
    d_pre, d_conv = {}, {}
    tap = ((4, wide), "parh", LANES, F32, "inner")
    for mode, off, ctg in (("q", 0, d_cq), ("k", NH, d_ck), ("v", 2 * NH, d_cv)):
        d_pre[mode], d_conv[mode] = ew_bwd(f"conv_{mode}_b", make_f_conv(mode), [rowh(proj, off), parh(conv_w_t, off)],
                                           [(rowh(ctg),)], [], lambda g, e: list(g), [h16, tap], rows, NH, seq, "hi", CONV_HEADS)
    d_fqn, d_cq_b = fox_dq(fqn, fkn, proj, ct, d_ob, lse, o_b, [token], nseq, seq)
    d_fkn, d_fv16, d_ck_b = fox_dkv(fqn, fkn, proj, cb, d_ob, lse, o_b, [token], nseq, seq)
    token = hooks.reduce_finish(d_fkn)
    qn_g_t, kn_g_t = qn_g + token[0, 0], kn_g + token[0, 0]
    d_fq16, d_qn_g = ew_bwd("fox_qn_b", f_rms, [rowh(proj, FOX_Q), par(qn_g_t)], [(rowh(d_fqn),)], [], lambda g, e: list(g),
                            [h16, gain], rows, NH)
    d_fk16, d_kn_g = ew_bwd("fox_kn_b", f_rms, [rowh(proj, FOX_K), par(kn_g_t)], [(rowh(d_fkn),)], [], lambda g, e: list(g),
                            [h16, gain], rows, NH)
    narrow = ((rows, LANES), "row", LANES, F32, None)
    d_so, d_cs = ew_bwd("bcast_b", f_bcast, [row(so), row(cs)], [(rowh(d_gb),), (rowh(d_bb),), (rowh(d_cq_b), rowh(d_ck_b))], [],
                        lambda g, e: list(g), [narrow, narrow], rows, NH)
    d_logf = cumsum_time("cumsum_b", d_cs, nseq, seq, True)
    vec = ((1, LANES), "par", LANES, F32, "all")
    d_sp16, d_p1, d_p2 = ew_bwd("small_b", f_small, [row(sp), par(p1), par(p2)], [(row(d_so), row(d_logf))], [],
                                lambda g, e: list(g), [((rows, LANES), "row", LANES, BF16, None), vec, vec], rows)
    d_proj16 = jnp.concatenate([d_pre["q"], d_pre["k"], d_pre["v"], d_z16, d_fq16, d_fk16, d_fv16, d_ga16, d_gb16], axis=1)
    dw_main = matmul("mm_dw_main", d_proj16, u, "tn", F32)
    dw_small = matmul("mm_dw_small", d_sp16, u, "tn", F32)
    wt_small_t = wt_small + hooks.input_grad_start(dw_main, dw_small)[0, 0].astype(BF16)
    d_u = matmul("mm_d_u_small", d_sp16, wt_small_t, "nn", F32)
    d_u = matmul("mm_d_u_first", d_proj16, wt_main, "nn", F32, add=d_u, k_part=(0, 2))
    d_u = matmul("mm_d_u_second", d_proj16, wt_main, "nn", F32, add=d_u, k_part=(1, 2), after=[hooks.input_grad_exchange(d_u)])
    dx, dg1 = ew_bwd("rms1_b", f_rms, [row(x2), par(g1)], [(row(d_u),)], [row(dh)], lambda g, e: [g[0] + e[0], g[1]],
                     [((rows, dm), "row", dm, F32, None), ((1, dm), "par", dm, F32, "all")], rows)
    d_conv_w = jnp.concatenate([d_conv["q"], d_conv["k"], d_conv["v"]], axis=1)
    return dict(loss_acc=loss_acc, dx=dx, g1=dg1, g2=dg2, gdn_ng=d_gdn_ng, qn=d_qn_g, kn=d_kn_g, p1=d_p1, p2=d_p2,
                conv=d_conv_w, w_main=dw_main, w_small=dw_small, p_a=dp_a, p_b=dp_b, w_o=dw_o, w_u=dw_u, w_d=dw_d)


_W = NH * LANES
_A0, _A1 = 4 * _W, 4 * _W + 2 * NH
_B0, _B1 = _A1 + 3 * _W, _A1 + 3 * _W + NH


def _split_w_in(full_t):
    main = jnp.concatenate([full_t[:_A0], full_t[_A1:_B0], full_t[_B1:]], axis=0)
    small = jnp.concatenate([full_t[_A0:_A1], full_t[_B0:_B1], jnp.zeros((LANES - 3 * NH, full_t.shape[1]), full_t.dtype)], axis=0)
    return main, small


def _join_w_in(main, small):
    return jnp.concatenate([main[:_A0], small[:2 * NH], main[_A0:_A0 + 3 * _W], small[2 * NH:3 * NH], main[_A0 + 3 * _W:]], axis=0)


def _lanes(v, at=0):
    return jnp.pad(v.reshape(1, -1), ((0, 0), (at, LANES - at - v.size)))


def kernel(x, norm_mix_g, w_in, gdn_conv_w, gdn_a_log, gdn_dt_bias, gdn_norm_g, fox_q_norm_g, fox_k_norm_g, fox_f_bias, w_proj_gdn, w_proj_fox, w_out, norm_mlp_g, w_up, w_down, loss_target, m_norm_mix_g, m_w_in, m_gdn_conv_w, m_gdn_a_log, m_gdn_dt_bias, m_gdn_norm_g, m_fox_q_norm_g, m_fox_k_norm_g, m_fox_f_bias, m_w_proj_gdn, m_w_proj_fox, m_w_out, m_norm_mlp_g, m_w_up, m_w_down, v_norm_mix_g, v_w_in, v_gdn_conv_w, v_gdn_a_log, v_gdn_dt_bias, v_gdn_norm_g, v_fox_q_norm_g, v_fox_k_norm_g, v_fox_f_bias, v_w_proj_gdn, v_w_proj_fox, v_w_out, v_norm_mlp_g, v_w_up, v_w_down):
    nseq, seq, dm = x.shape
    rows = nseq * seq
    xi, yi, ci = lax.axis_index("x"), lax.axis_index("y"), lax.axis_index("c")
    chip = 2 * xi + yi
    conv_cols = gdn_conv_w.shape[2]

    tr = lambda a: jnp.swapaxes(a[0], 0, 1)
    big = [tr(w_in), w_proj_gdn[0], w_proj_fox[0], w_out[0], w_up[0], w_down[0]]
    axes = [1, 0, 0, 0, 0, 0]
    big16 = [w.astype(BF16) for w in big]
    conv_slot = jnp.zeros((4, 4, conv_cols), F32).at[:, chip].set(jnp.where(ci == 0, gdn_conv_w[0], 0.0))
    conv_full = all_reduce_small("gather_conv", conv_slot.reshape(-1, LANES)).reshape(4, 4 * conv_cols)
    got_in = gather_ring(big16[0])
    wt_main, wt_small = _split_w_in(got_in.reshape(-1, dm))
    core, chip_no = ci.reshape(1).astype(jnp.int32), chip.reshape(1).astype(jnp.int32)
    gather = split_gather(big16[1:])
    token = gather.start([got_in, conv_full])

    class Hooks:
        def late_weights(self, after):
            g_pa, g_pb, g_wo, w_u, g_wd = gather.wait(after)
            return (*(g.reshape(-1, dm) for g in (g_pa, g_pb, g_wo)), w_u, g_wd.reshape(-1, dm))

        def reduce_start(self, grads):
            blocks = [grads["p_a"].reshape(4, -1, dm), grads["p_b"].reshape(4, -1, dm), grads["w_o"].reshape(4, -1, dm),
                      grads["w_u"], grads["w_d"].reshape(4, -1, dm)]
            self.swap = split_pair_swap("pair_swap_late", blocks, axes[1:])
            return self.swap.start([])

        def reduce_exchange(self, after):
            swapped = self.swap.wait(after)
            self.exchange = split_chip_exchange("chip_exchange_late", add_pair("add_pair_late", self.swap.srcs, swapped, core, axes[1:]))
            return self.exchange.start([])

        def reduce_finish(self, after):
            slots = self.exchange.wait(after)
            self.send = split_pair_send(add_chips("add_chips_late", slots, self.exchange.srcs, chip_no, axes[1:]))
            return self.send.start([])

        def input_grad_start(self, dw_main, dw_small):
            self.in_swap = split_pair_swap("pair_swap_in", [_join_w_in(dw_main, dw_small).reshape(4, -1, dm)], axes[:1])
            return self.in_swap.start([])

        def input_grad_exchange(self, after):
            swapped = self.in_swap.wait(after)
            self.in_exchange = split_chip_exchange("chip_exchange_in", add_pair("add_pair_in", self.in_swap.srcs, swapped, core, axes[:1]))
            return self.in_exchange.start([])

    hooks = Hooks()
    p1 = _lanes(gdn_dt_bias[0]) + _lanes(fox_f_bias[0], 2 * NH)
    p2 = _lanes(gdn_a_log[0])

    g = local_step(x.reshape(rows, dm), loss_target.reshape(rows, dm), norm_mix_g + token[0, 0], norm_mlp_g, gdn_norm_g,
                   fox_q_norm_g, fox_k_norm_g, p1, p2, conv_full, wt_main, wt_small, hooks, nseq, seq)

    others = hooks.send.wait(g["dx"])
    big_m = [tr(m_w_in), m_w_proj_gdn[0], m_w_proj_fox[0], m_w_out[0], m_w_up[0], m_w_down[0]]
    big_v = [tr(v_w_in), v_w_proj_gdn[0], v_w_proj_fox[0], v_w_out[0], v_w_up[0], v_w_down[0]]
    names = ["w_in", "w_proj_gdn", "w_proj_fox", "w_out", "w_up", "w_down"]
    big_res, big_grad = {}, {}
    for i in range(1, len(names)):
        big_grad[names[i]], *big_res[names[i]] = adamw_halves(f"adamw_{names[i]}", big[i], hooks.send.srcs[i - 1], others[i - 1],
                                                              big_m[i], big_v[i], core, axes[i])
    slots = hooks.in_exchange.wait(big_res[names[-1]][0])
    mine = add_chips("add_chips_in", slots, hooks.in_exchange.srcs, chip_no, axes[:1])
    res = adamw_halves("adamw_w_in", big[0], mine[0], pair_send(mine)[0], big_m[0], big_v[0], core, axes[0])
    big_grad["w_in"], *big_res["w_in"] = [jnp.swapaxes(r, 0, 1) for r in res]

    small_parts = [g["loss_acc"], g["g1"].reshape(8, LANES), g["g2"].reshape(8, LANES), g["gdn_ng"], g["qn"], g["kn"], g["p1"], g["p2"],
                   g["conv"].reshape(-1, LANES)]
    tiled = [jnp.pad(p, ((0, -p.shape[0] % 8), (0, 0))) for p in small_parts]
    red = all_reduce_small("reduce_small", jnp.concatenate(tiled, axis=0), slots)
    pos, red_parts = 0, []
    for p, t in zip(small_parts, tiled):
        red_parts.append(red[pos:pos + p.shape[0]])
        pos += t.shape[0]
    r_loss, r_g1, r_g2, r_gdn_ng, r_qn, r_kn, r_p1, r_p2, r_conv = red_parts
    loss = jnp.sum(r_loss)
    g_conv = lax.dynamic_slice_in_dim(r_conv.reshape(4, 4, conv_cols), chip, 1, axis=1).reshape(4, conv_cols)
    small_grads = [r_g1.reshape(1, dm), r_p2[:, :NH], r_p1[:, :NH], r_gdn_ng, r_qn, r_kn, r_p1[:, 2 * NH:3 * NH], r_g2.reshape(1, dm)]
    small_w = [norm_mix_g, gdn_a_log, gdn_dt_bias, gdn_norm_g, fox_q_norm_g, fox_k_norm_g, fox_f_bias, norm_mlp_g]
    small_m = [m_norm_mix_g, m_gdn_a_log, m_gdn_dt_bias, m_gdn_norm_g, m_fox_q_norm_g, m_fox_k_norm_g, m_fox_f_bias, m_norm_mlp_g]
    small_v = [v_norm_mix_g, v_gdn_a_log, v_gdn_dt_bias, v_gdn_norm_g, v_fox_q_norm_g, v_fox_k_norm_g, v_fox_f_bias, v_norm_mlp_g]

    def pack(parts):
        flat = jnp.concatenate([jnp.pad(p.reshape(-1), (0, -p.size % LANES)) for p in parts])
        return jnp.pad(flat, (0, -flat.size % (8 * LANES))).reshape(-1, LANES)

    packed = adamw("adamw_small", pack(small_w + [gdn_conv_w[0]]), pack(small_grads + [g_conv]),
                   pack(small_m + [m_gdn_conv_w[0]]), pack(small_v + [v_gdn_conv_w[0]]))

    def unpack(flat2d):
        flat, pos, res = flat2d.reshape(-1), 0, []
        for p in small_w + [gdn_conv_w[0]]:
            res.append(flat[pos:pos + p.size].reshape(p.shape))
            pos += p.size + (-p.size % LANES)
        return res

    s_delta, s_m, s_v = (unpack(a) for a in packed)

    order = ["norm_mix_g", "w_in", "gdn_conv_w", "gdn_a_log", "gdn_dt_bias", "gdn_norm_g", "fox_q_norm_g", "fox_k_norm_g",
             "fox_f_bias", "w_proj_gdn", "w_proj_fox", "w_out", "norm_mlp_g", "w_up", "w_down"]
    small_names = ["norm_mix_g", "gdn_a_log", "gdn_dt_bias", "gdn_norm_g", "fox_q_norm_g", "fox_k_norm_g", "fox_f_bias", "norm_mlp_g",
                   "gdn_conv_w"]
    small_idx = {nm: i for i, nm in enumerate(small_names)}
    shapes = dict(zip(order, (a.shape for a in (norm_mix_g, w_in, gdn_conv_w, gdn_a_log, gdn_dt_bias, gdn_norm_g, fox_q_norm_g,
                                                 fox_k_norm_g, fox_f_bias, w_proj_gdn, w_proj_fox, w_out, norm_mlp_g, w_up, w_down))))
    grads_out, delta_out, m_out, v_out = [], [], [], []
    for nm in order:
        if nm in big_res:
            d, mm, vv = big_res[nm]
            gr = big_grad[nm]
        else:
            i = small_idx[nm]
            gr = (small_grads + [g_conv])[i]
            d, mm, vv = s_delta[i], s_m[i], s_v[i]
        for lst, val in ((grads_out, gr), (delta_out, d), (m_out, mm), (v_out, vv)):
            lst.append(val.reshape(shapes[nm]))
    return (loss, g["dx"].reshape(x.shape), *grads_out, *delta_out, *m_out, *v_out)
```

```python
import functools

import jax
import jax.numpy as jnp
from jax import lax
from jax.experimental import pallas as pl
from jax.experimental.pallas import tpu as pltpu

F32 = jnp.float32
BF16 = jnp.bfloat16
LANES = 128
NH = 8
EPS = 1e-6
GDN_CHUNK = 64
GDN_ROWS = 256
GDN_BASE = 16
ROW_TILE = 512
CONV_HEADS = 2
ATT_TILE = 512
NEG = -1e30
VMEM_LIMIT_BYTES = 58 * 1024 * 1024
LO = lax.Precision.DEFAULT
MESH = pl.DeviceIdType.MESH
ANY = pl.BlockSpec(memory_space=pl.ANY)

ADAM_LR, ADAM_B1, ADAM_B2, ADAM_EPS, ADAM_WD, ADAM_STEP = 0.001, 0.9, 0.999, 1e-08, 0.01, 10


def _params(n_grid):
    return pltpu.CompilerParams(dimension_semantics=("arbitrary",) * n_grid,
                                vmem_limit_bytes=VMEM_LIMIT_BYTES)


def _dot(a, b, dims, precision=None):
    dn = {"nn": (((1,), (0,)), ((), ())), "nt": (((1,), (1,)), ((), ())), "tn": (((0,), (0,)), ((), ()))}[dims]
    return lax.dot_general(a, b, dn, precision=precision, preferred_element_type=F32)


def _iota(shape, dim):
    return lax.broadcasted_iota(jnp.int32, shape, dim)


def _split(x, parts):
    out = []
    for _ in range(parts - 1):
        hi = x.astype(BF16)
        out.append(hi)
        x = x - hi.astype(F32)
    return out + [x.astype(BF16)]


def _dot_mask(mask, b, dims, terms=3):
    m16 = mask.astype(BF16)
    acc = None
    for part in reversed(_split(b, terms)):
        prod = _dot(m16, part, dims)
        acc = prod if acc is None else acc + prod
    return acc


@jax.custom_vjp
def mm_mask(mask, b):
    return _dot_mask(mask, b, "nn", 2)


mm_mask.defvjp(lambda mask, b: (_dot_mask(mask, b, "nn", 2), mask),
               lambda mask, g: (jnp.zeros_like(mask), _dot_mask(mask, g, "tn", 2)))


def matmul(name, a, b, dims, out_dtype, add=None, tm=1024, tn=1024, tk=512, col_blocks=None,
           extras=(), epilogue=None, out_dtypes=None, k_part=None, after=(), tile_sums=False):
    if col_blocks and dims != "tn":
        nb, b_rows, bw = b.shape
        b_shape = (b_rows, nb * bw)
    else:
        b_shape = b.shape
    if dims == "nn":
        (m, k), (_, n) = a.shape, b_shape
    elif dims == "nt":
        (m, k), (n, _) = a.shape, b_shape
    else:
        (k, m), (_, n) = a.shape, b_shape
    k_span = k // (k_part[1] if k_part else 1)
    if col_blocks and dims == "nt":
        k_span = min(k_span, bw)
    tk = k if k <= 1024 else max(t for t in (2048, 1536, 1024, 512, tk) if k_span % t == 0)
    tm, tn, tk = min(tm, m), min(tn, n), min(tk, k)
    assert m % tm == 0 and n % tn == 0 and k % tk == 0, (name, m, n, k)
    k0, nk = (0, k // tk) if k_part is None else (k_part[0] * (k // tk // k_part[1]), k // tk // k_part[1])
    assert k_part is None or (dims == "nn" and not col_blocks and (k // tk) % k_part[1] == 0)
    a_spec = pl.BlockSpec((tk, tm), lambda i, j, kk: (kk, i)) if dims == "tn" else pl.BlockSpec((tm, tk), lambda i, j, kk: (i, kk + k0))
    b_spec = pl.BlockSpec((tn, tk), lambda i, j, kk: (j, kk)) if dims == "nt" else pl.BlockSpec((tk, tn), lambda i, j, kk: (kk + k0, j))
    o_spec = pl.BlockSpec((tm, tn), lambda i, j, kk: (i, j))
    out_shape = (m, n)
    if col_blocks and dims == "nn":
        per = bw // tn
        assert bw % tn == 0
        b_spec = pl.BlockSpec((None, tk, tn), lambda i, j, kk: (j // per, kk, j % per))
    elif col_blocks and dims == "nt":
        per = bw // tk
        assert bw % tk == 0
        b_spec = pl.BlockSpec((None, tn, tk), lambda i, j, kk: (kk // per, j, kk % per))
    elif col_blocks:
        bw = n // col_blocks
        per = bw // tn
        assert bw % tn == 0 and add is None
        o_spec = pl.BlockSpec((None, tm, tn), lambda i, j, kk: (j // per, i, j % per))
        out_shape = (col_blocks, m, bw)
    extras = list(extras) + ([add] if add is not None else [])
    if add is not None:
        assert epilogue is None
        epilogue = lambda r, *e: [r + e[-1]]
    out_dtypes = [out_dtype] if epilogue is None or out_dtypes is None else list(out_dtypes)
    n_ex, n_out = len(extras), len(out_dtypes)

    def body(*refs):
        a_ref, b_ref = refs[0], refs[1]
        ex_refs, o_refs = refs[2:2 + n_ex], refs[2 + n_ex + len(after):2 + n_ex + len(after) + n_out]

        def finish(r):
            res = [r] if epilogue is None else epilogue(r, *[e[...] for e in ex_refs])
            for o_ref, v in zip(o_refs, res):
                o_ref[...] = v.astype(o_ref.dtype)

        if nk == 1:
            finish(_dot(a_ref[...], b_ref[...], dims))
            return
        acc_ref = refs[-1]
        kk = pl.program_id(2)

        @pl.when(kk == 0)
        def _():
            acc_ref[...] = jnp.zeros_like(acc_ref)

        acc_ref[...] += _dot(a_ref[...], b_ref[...], dims)

        @pl.when(kk == nk - 1)
        def _():
            finish(acc_ref[...])

    out_specs = [o_spec] * n_out
    out_shapes = [jax.ShapeDtypeStruct(out_shape, dt) for dt in out_dtypes]
    if tile_sums:
        out_specs[-1] = pl.BlockSpec((8, LANES), lambda i, j, kk: (i, j))
        out_shapes[-1] = jax.ShapeDtypeStruct((8 * (m // tm), LANES * (n // tn)), out_dtypes[-1])
    res = pl.pallas_call(
        body, name=name, grid=(m // tm, n // tn, nk), in_specs=[a_spec, b_spec] + [o_spec] * n_ex + [ANY] * len(after),
        out_specs=out_specs, out_shape=out_shapes,
        scratch_shapes=[pltpu.VMEM((tm, tn), F32)] if nk > 1 else [], compiler_params=_params(3),
    )(a, b, *extras, *after)
    return res[0] if n_out == 1 else res


def _ew_spec(kind, off, width, tb, hp, order, shape=None):
    def ih(g0, g1):
        return (g0, g1) if order == "ih" else (g1, g0)

    assert off % hp == 0 or kind in ("row", "par")
    if kind == "row":
        return pl.BlockSpec((tb, width), lambda g0, g1: (ih(g0, g1)[0], off))
    if kind == "rowh":
        return pl.BlockSpec((tb, hp * width), lambda g0, g1: (ih(g0, g1)[0], ih(g0, g1)[1] + off // hp))
    if kind == "par":
        return pl.BlockSpec(shape, lambda g0, g1: (0, 0))
    if kind == "parh":
        return pl.BlockSpec((shape[0], hp * width), lambda g0, g1: (0, ih(g0, g1)[1] + off // hp))
    raise ValueError(kind)


def _ew_grid(rows, tb, nh, hp, order):
    assert nh % hp == 0 and rows % tb == 0
    return (rows // tb, nh // hp) if order == "ih" else (nh // hp, rows // tb)


def _ew_load(ref, kind, width, hh):
    if kind in ("row", "par"):
        return ref[...].astype(F32)
    return ref[:, hh * width:(hh + 1) * width].astype(F32)


def ew_fwd(name, f, ins, outs, rows, nh=1, tb=ROW_TILE, order="ih", hp=None, after=()):
    hp = nh if hp is None else hp
    n_in = len(ins)

    def body(*refs):
        hb = pl.program_id(1) if order == "ih" else pl.program_id(0)
        for hh in range(hp):
            h = hh if hp == nh else hb * hp + hh
            vals = [_ew_load(r, kd, w, hh) for r, (_, kd, _, w) in zip(refs[:n_in], ins)]
            res = f(h, *vals)
            for r, v, (_, kd, w, _) in zip(refs[n_in + len(after):], res, outs):
                if kd == "row":
                    assert hp == 1
                    r[...] = v.astype(r.dtype)
                else:
                    r[:, hh * w:(hh + 1) * w] = v.astype(r.dtype)

    in_specs = [_ew_spec(kd, off, w, tb, hp, order, a.shape) for (a, kd, off, w) in ins]
    out_specs = [_ew_spec(kd, 0, w, tb, hp, order) for (_, kd, w, _) in outs]
    out_shape = [jax.ShapeDtypeStruct((rows, tw), dt) for (tw, _, _, dt) in outs]
    return pl.pallas_call(
        body, name=name, grid=_ew_grid(rows, tb, nh, hp, order), in_specs=in_specs + [ANY] * len(after), out_specs=out_specs,
        out_shape=out_shape, compiler_params=_params(2),
    )(*[a for (a, _, _, _) in ins], *after)


def ew_bwd(name, f, ins, cts, extras, emit, outs, rows, nh=1, tb=ROW_TILE, order="ih", hp=None):
    hp = nh if hp is None else hp
    n_in = len(ins)
    flat_cts = [d for group in cts for d in group]
    n_ct, n_ex = len(flat_cts), len(extras)

    def body(*refs):
        g0, g1 = pl.program_id(0), pl.program_id(1)
        hb = g1 if order == "ih" else g0
        out_refs = refs[n_in + n_ct + n_ex:]
        shared = [None] * len(outs)

        def store(r, v, first, sl=None):
            def put(val, add):
                if sl is None:
                    r[...] = (r[...] + val if add else val).astype(r.dtype)
                else:
                    r[:, sl] = (r[:, sl] + val if add else val).astype(r.dtype)

            if first is None:
                put(v, False)
            else:
                pl.when(first)(lambda: put(v, False))
                pl.when(jnp.logical_not(first))(lambda: put(v, True))

        for hh in range(hp):
            h = hh if hp == nh else hb * hp + hh
            vals = [_ew_load(r, kd, w, hh) for r, (_, kd, _, w) in zip(refs[:n_in], ins)]
            ct_refs = list(zip(refs[n_in:n_in + n_ct], flat_cts))
            ct_vals, pos = [], 0
            for group in cts:
                v = None
                for r, (_, kd, _, w) in ct_refs[pos:pos + len(group)]:
                    t = _ew_load(r, kd, w, hh)
                    v = t if v is None else v + t
                pos += len(group)
                ct_vals.append(v)
            ex_vals = [_ew_load(r, kd, w, hh) for r, (_, kd, _, w) in zip(refs[n_in + n_ct:n_in + n_ct + n_ex], extras)]
            _, vjp = jax.vjp(lambda *a: f(h, *a), *vals)
            res = emit(vjp(tuple(ct_vals)), ex_vals)
            for idx, (r, v, (_, kd, w, _, acc)) in enumerate(zip(out_refs, res, outs)):
                if kd in ("row", "par"):
                    shared[idx] = v if shared[idx] is None else shared[idx] + v
                else:
                    store(r, v, (g1 == 0) if acc == "inner" else None, slice(hh * w, (hh + 1) * w))
        for idx, (r, (_, kd, _, _, acc)) in enumerate(zip(out_refs, outs)):
            if kd in ("row", "par"):
                assert acc == "all" or hp == nh
                store(r, shared[idx], jnp.logical_and(g0 == 0, g1 == 0) if acc == "all" else None)

    operands = list(ins) + flat_cts + list(extras)
    in_specs = [_ew_spec(kd, off, w, tb, hp, order, a.shape) for (a, kd, off, w) in operands]
    out_specs = [_ew_spec(kd, 0, w, tb, hp, order, shp) for (shp, kd, w, _, _) in outs]
    out_shape = [jax.ShapeDtypeStruct(shp, dt) for (shp, _, _, dt, _) in outs]
    return pl.pallas_call(
        body, name=name, grid=_ew_grid(rows, tb, nh, hp, order), in_specs=in_specs, out_specs=out_specs,
        out_shape=out_shape, compiler_params=_params(2),
    )(*[a for (a, _, _, _) in operands])


def f_rms(h, x, g):
    r = lax.rsqrt(jnp.mean(x * x, axis=-1, keepdims=True) + EPS)
    return (x * r * g,)


def _softplus(z):
    return jnp.maximum(z, 0.0) + jnp.log1p(jnp.exp(-jnp.abs(z)))


def f_small(h, sp, p1, p2):
    lane = _iota(sp.shape, 1)
    z = sp + p1
    g = -jnp.exp(p2) * _softplus(z)
    beta = jax.nn.sigmoid(z)
    logf = -_softplus(-z)
    return (jnp.where(lane < NH, g, jnp.where(lane < 2 * NH, beta, jnp.where(lane < 3 * NH, logf, 0.0))),)


def _pick(x, lane_id):
    lane = _iota(x.shape, 1)
    col = jnp.sum(jnp.where(lane == lane_id, x, 0.0), axis=1, keepdims=True)
    return jnp.broadcast_to(col, x.shape)


def f_bcast(h, so, cs):
    return _pick(so, h), _pick(so, h + NH), _pick(cs, h + 2 * NH)


def _shift_down(s):
    def down(x):
        r = pltpu.roll(x, s, 0)
        head = jnp.where(_iota((8, x.shape[1]), 0) >= s, r[:8], 0.0)
        return jnp.concatenate([head, r[8:]], axis=0)

    def up(g):
        n = g.shape[0]
        r = pltpu.roll(g, n - s, 0)
        tail = jnp.where(_iota((8, g.shape[1]), 0) < 8 - s, r[n - 8:], 0.0)
        return jnp.concatenate([r[:n - 8], tail], axis=0)

    @jax.custom_vjp
    def shift(x):
        return down(x)

    shift.defvjp(lambda x: (down(x), None), lambda _, g: (up(g),))
    return shift


def _silu(x):
    return x * jax.nn.sigmoid(x)


def make_f_conv(mode):
    sh1, sh2, sh3 = _shift_down(1), _shift_down(2), _shift_down(3)

    def f(h, x, w):
        sub = _iota(w.shape, 0)

        def tap(i):
            return jnp.sum(jnp.where(sub == i, w, 0.0), axis=0, keepdims=True)

        y = sh3(x) * tap(0)
        y = y + sh2(x) * tap(1)
        y = y + sh1(x) * tap(2)
        y = y + x * tap(3)
        s = _silu(y)
        if mode == "v":
            return (s,)
        n = s * lax.rsqrt(jnp.sum(s * s, axis=-1, keepdims=True) + EPS)
        if mode == "q":
            n = n * (LANES ** -0.5)
        return (n,)

    return f


def f_post(h, o, z, g):
    r = lax.rsqrt(jnp.mean(o * o, axis=-1, keepdims=True) + EPS)
    return (o * r * g * _silu(z),)


def f_merge(h, ga, gb, ya, yb):
    return (jax.nn.sigmoid(ga) * ya + jax.nn.sigmoid(gb) * yb,)


def cumsum_time(name, x, nseq, seq, reverse):
    nb = seq // LANES

    def body(x_ref, o_ref):
        r, c = _iota((LANES, LANES), 0), _iota((LANES, LANES), 1)
        tri = jnp.where((r <= c) if reverse else (r >= c), 1.0, 0.0).astype(F32)
        carry = jnp.zeros((1, LANES), F32)
        for b in (range(nb - 1, -1, -1) if reverse else range(nb)):
            blk = x_ref[b * LANES:(b + 1) * LANES, :]
            o_ref[b * LANES:(b + 1) * LANES, :] = _dot_mask(tri, blk, "nn") + carry
            carry = carry + jnp.sum(blk, axis=0, keepdims=True)

    spec = pl.BlockSpec((seq, LANES), lambda s: (s, 0))
    return pl.pallas_call(body, name=name, grid=(nseq,), in_specs=[spec], out_specs=spec,
                          out_shape=jax.ShapeDtypeStruct(x.shape, F32), compiler_params=_params(1))(x)


def transpose_time(name, x, nseq, seq):
    def body(x_ref, o_ref):
        o_ref[...] = x_ref[...].T

    return pl.pallas_call(
        body, name=name, grid=(nseq,), in_specs=[pl.BlockSpec((seq, LANES), lambda s: (s, 0))],
        out_specs=pl.BlockSpec((LANES, seq), lambda s: (s, 0)),
        out_shape=jax.ShapeDtypeStruct((nseq * LANES, seq), F32), compiler_params=_params(1))(x)


def _gdn_masks():
    n = GDN_ROWS
    r, c = _iota((n, n), 0), _iota((n, n), 1)
    shift = GDN_CHUNK.bit_length() - 1
    same = lax.shift_right_logical(r, shift) == lax.shift_right_logical(c, shift)
    return r, c, same


def _each(fn, *lists):
    return [fn(*xs) for xs in zip(*lists)]


def _gdn_decay(gbs):
    r, c, same = _gdn_masks()
    seg_tril = jnp.where(jnp.logical_and(same, r >= c), 1.0, 0.0).astype(F32)
    g_cum = _each(lambda gb: mm_mask(seg_tril, gb), gbs)
    lane0 = _iota(gbs[0].shape, 1) == 0
    g_col = _each(lambda g: jnp.sum(jnp.where(lane0, g, 0.0), axis=1, keepdims=True), g_cum)
    g_row = _each(lambda g: jnp.sum(jnp.where(r == c, jnp.broadcast_to(g, (GDN_ROWS, GDN_ROWS)), 0.0), axis=0, keepdims=True), g_col)
    return g_cum, _each(lambda a, b: a - b, g_col, g_row)


def gdn_a_mats(ks, bbs, diff):
    r, c, same = _gdn_masks()
    strict = jnp.logical_and(same, r > c)
    lane0 = _iota(bbs[0].shape, 1) == 0
    beta_col = _each(lambda bb: jnp.sum(jnp.where(lane0, bb, 0.0), axis=1, keepdims=True), bbs)
    kk = _each(lambda k: _dot(k, k, "nt", LO), ks)
    return _each(lambda b, x, d: jnp.where(strict, b * x * jnp.exp(jnp.where(strict, d, 0.0)), 0.0), beta_col, kk, diff)


@jax.custom_vjp
def saved_inverse(a, t_corr):
    return t_corr


def _saved_inverse_bwd(t, dt):
    left = dt + _dot(t, dt, "tn", LO)
    return -(left + _dot(left, t, "nt", LO)), jnp.zeros_like(t)


saved_inverse.defvjp(lambda a, t_corr: (t_corr, t_corr), _saved_inverse_bwd)


def gdn_block(*args):
    ts, qs, ks, vs, gbs, bbs = (list(args[i::6]) for i in range(6))
    g_cum, diff = _gdn_decay(gbs)
    ts = _each(saved_inverse, gdn_a_mats(ks, bbs, diff), ts)
    return gdn_outputs(ts, qs, ks, vs, gbs, bbs, g_cum, diff)


def gdn_outputs(ts, qs, ks, vs, gbs, bbs, g_cum, diff):
    r, c, same = _gdn_masks()
    incl = jnp.logical_and(same, r >= c)
    decay = _each(lambda d: jnp.where(incl, jnp.exp(jnp.where(incl, d, 0.0)), 0.0), diff)
    e_g = _each(jnp.exp, g_cum)
    v_beta = _each(lambda v, bb: v * bb, vs, bbs)
    k_beta = _each(lambda k, bb, e: k * bb * e, ks, bbs, e_g)
    value = _each(lambda t, x: x + _dot(t, x, "nn", LO), ts, v_beta)
    k_cum = _each(lambda t, x: x + _dot(t, x, "nn", LO), ts, k_beta)
    attn = _each(lambda q, k, d: _dot(q, k, "nt", LO) * d, qs, ks, decay)
    ones = jnp.where(same, 1.0, 0.0).astype(F32)
    g_last = _each(lambda gb: mm_mask(ones, gb), gbs)
    q_dec = _each(lambda q, e: q * e, qs, e_g)
    k_dec = _each(lambda k, gl, g: k * jnp.exp(gl - g), ks, g_last, g_cum)
    return tuple(x for head in zip(value, k_cum, attn, q_dec, k_dec) for x in head)


def tri_inverse(mats):
    n = GDN_ROWS
    r, c = _iota((n, n), 0), _iota((n, n), 1)
    shift = GDN_BASE.bit_length() - 1
    blk = lax.shift_right_logical(r, shift) == lax.shift_right_logical(c, shift)
    each = lambda fn, *lists: [fn(*xs) for xs in zip(*lists)]
    mm = lambda x, y: _dot(x, y, "nn", LO)
    d = each(lambda a: jnp.where(blk, a, 0.0), mats)
    lo = each(lambda a, dd: a - dd, mats, d)
    p = each(lambda dd: -dd, d)
    c_d = p
    for _ in range(shift - 1):
        p = each(mm, p, p)
        c_d = each(lambda cd, pp, prod: cd + pp + prod, c_d, p, each(mm, c_d, p))
    assert GDN_CHUNK // GDN_BASE == 4
    nmat = each(lambda l, prod: l + prod, lo, each(mm, c_d, lo))
    n2 = each(mm, nmat, nmat)
    c_n = each(lambda nn2, nm, prod: (nn2 - nm) - prod, n2, nmat, each(mm, nmat, n2))
    return each(lambda cn, cd, prod: cn + cd + prod, c_n, c_d, each(mm, c_n, c_d))


GDN_AHP = 4


def _gdn_a_specs():
    blk = pl.BlockSpec((GDN_ROWS, GDN_AHP * LANES), lambda i, h: (i, h))
    sq = pl.BlockSpec((GDN_ROWS, GDN_AHP * GDN_ROWS), lambda i, h: (i, h))
    return blk, sq


def _head(ref, hh):
    width = ref.shape[1] // GDN_AHP
    return ref.at[:, hh * width:(hh + 1) * width]


def gdn_a_fwd(q, k, v, gb, bb, rows):
    blk, sq = _gdn_a_specs()

    def body(q_ref, k_ref, v_ref, gb_ref, bb_ref, val_ref, kc_ref, at_ref, qd_ref, kd_ref, t_ref):
        heads = [[_head(r, hh)[...] for r in (q_ref, k_ref, v_ref, gb_ref, bb_ref)] for hh in range(GDN_AHP)]
        qs, ks, vs, gbs, bbs = (list(col) for col in zip(*heads))
        g_cum, diff = _gdn_decay(gbs)
        t_corr = tri_inverse(gdn_a_mats(ks, bbs, diff))
        res = gdn_outputs(t_corr, qs, ks, vs, gbs, bbs, g_cum, diff)
        for hh in range(GDN_AHP):
            for r, x in zip((val_ref, kc_ref, at_ref, qd_ref, kd_ref, t_ref), (*res[5 * hh:5 * hh + 5], t_corr[hh])):
                _head(r, hh)[...] = x.astype(r.dtype)

    wide = lambda dt: jax.ShapeDtypeStruct((rows, NH * LANES), dt)
    square = jax.ShapeDtypeStruct((rows, NH * GDN_ROWS), BF16)
    return pl.pallas_call(
        body, name="gdn_a_fwd", grid=(rows // GDN_ROWS, NH // GDN_AHP), in_specs=[blk] * 5,
        out_specs=[blk, blk, sq, blk, blk, sq], out_shape=[wide(F32), wide(BF16), square, wide(BF16), wide(BF16), square],
        compiler_params=_params(2))(q, k, v, gb, bb)


def gdn_a_bwd(q, k, v, gb, bb, t_inv, dval, dkc, dat, dqd, dkd, dgb_b, rows):
    blk, sq = _gdn_a_specs()

    def body(q_ref, k_ref, v_ref, gb_ref, bb_ref, t_ref, dval_ref, dkc_ref, dat_ref, dqd_ref, dkd_ref, dgbb_ref,
             dq_ref, dk_ref, dv_ref, dgb_ref, dbb_ref):
        hs = range(GDN_AHP)
        heads = [[_head(r, hh)[...] for r in (q_ref, k_ref, v_ref, gb_ref, bb_ref)] for hh in hs]
        tvs = [_head(t_ref, hh)[...].astype(F32) for hh in hs]
        _, vjp = jax.vjp(gdn_block, *[x for t, head in zip(tvs, heads) for x in (t, *head)])
        grads = vjp(tuple(_head(r, hh)[...] for hh in hs for r in (dval_ref, dkc_ref, dat_ref, dqd_ref, dkd_ref)))
        for hh in hs:
            _, dq, dk, dv, dgb, dbb = grads[6 * hh:6 * hh + 6]
            _head(dq_ref, hh)[...] = dq
            _head(dk_ref, hh)[...] = dk
            _head(dv_ref, hh)[...] = dv
            _head(dgb_ref, hh)[...] = dgb + _head(dgbb_ref, hh)[...]
            _head(dbb_ref, hh)[...] = dbb

    wide = jax.ShapeDtypeStruct((rows, NH * LANES), F32)
    return pl.pallas_call(
        body, name="gdn_a_bwd", grid=(rows // GDN_ROWS, NH // GDN_AHP),
        in_specs=[blk] * 5 + [sq, blk, blk, sq, blk, blk, blk], out_specs=[blk] * 5, out_shape=[wide] * 5,
        compiler_params=_params(2))(q, k, v, gb, bb, t_inv, dval, dkc, dat, dqd, dkd, dgb_b)


N_CH = GDN_ROWS // GDN_CHUNK


GDN_HP = 8


def gdn_chunk(c):
    def f(*args):
        val, kc, at, qd, kd, gb, s = (list(args[i::7]) for i in range(7))
        zero = jnp.zeros((GDN_CHUNK, LANES), F32)
        v_new = _each(lambda v, k, st: v - _dot(k, st, "nn", LO), val, kc, s)
        v_pad = _each(lambda v: jnp.concatenate([zero] * c + [v] + [zero] * (N_CH - 1 - c), axis=0), v_new)
        out = _each(lambda q, st, a, vp: _dot(q, st, "nn", LO) + _dot(a, vp, "nn", LO), qd, s, at, v_pad)
        dec = _each(lambda g: jnp.exp(jnp.sum(g, axis=0, keepdims=True)), gb)
        s_new = _each(lambda st, d, k, v: st * d + _dot(k, v, "tn", LO), s, dec, kd, v_new)
        return tuple(x for head in zip(out, s_new) for x in head)

    return f


def _gdn_piece(ref, hh, c):
    width = ref.shape[1] // GDN_HP
    return ref.at[c * GDN_CHUNK:(c + 1) * GDN_CHUNK, hh * width:(hh + 1) * width]


def _gdn_snap(ref, hh, c):
    row = (hh * N_CH + c) * LANES
    return ref.at[row:row + LANES, :]


def _gdn_b_specs(nb, rev):
    def blk_row(s, j):
        return s * nb + (nb - 1 - j if rev else j)

    blk = pl.BlockSpec((GDN_ROWS, GDN_HP * LANES), lambda s, hb, j: (blk_row(s, j), hb))
    sq = pl.BlockSpec((GDN_ROWS, GDN_HP * GDN_ROWS), lambda s, hb, j: (blk_row(s, j), hb))
    snap = pl.BlockSpec((GDN_HP * N_CH * LANES, LANES), lambda s, hb, j: (blk_row(s, j) * (NH // GDN_HP) + hb, 0))
    return blk, sq, snap


def gdn_b_fwd(val, kc, at, qd, kd, gb, nseq, seq):
    nb = seq // GDN_ROWS
    rows = nseq * seq
    blk, sq, snap = _gdn_b_specs(nb, False)

    def body(val_ref, kc_ref, at_ref, qd_ref, kd_ref, gb_ref, o_ref, snap_ref, s_ref):
        @pl.when(pl.program_id(2) == 0)
        def _():
            s_ref[...] = jnp.zeros_like(s_ref)

        hs = range(GDN_HP)
        states = [s_ref[hh] for hh in hs]
        for c in range(N_CH):
            for hh in hs:
                _gdn_snap(snap_ref, hh, c)[...] = states[hh]
            res = gdn_chunk(c)(*[x for hh in hs for x in (
                *[_gdn_piece(r, hh, c)[...].astype(F32) for r in (val_ref, kc_ref, at_ref, qd_ref, kd_ref, gb_ref)], states[hh])])
            for hh in hs:
                _gdn_piece(o_ref, hh, c)[...] = res[2 * hh]
            states = [res[2 * hh + 1] for hh in hs]
        for hh in hs:
            s_ref[hh] = states[hh]

    return pl.pallas_call(
        body, name="gdn_b_fwd", grid=(nseq, NH // GDN_HP, nb), in_specs=[blk, blk, sq, blk, blk, blk], out_specs=[blk, snap],
        out_shape=[jax.ShapeDtypeStruct((rows, NH * LANES), F32),
                   jax.ShapeDtypeStruct((nseq * nb * NH * N_CH * LANES, LANES), F32)],
        scratch_shapes=[pltpu.VMEM((GDN_HP, LANES, LANES), F32)], compiler_params=_params(3))(val, kc, at, qd, kd, gb)


def gdn_b_bwd(val, kc, at, qd, kd, gb, snaps, do, nseq, seq):
    nb = seq // GDN_ROWS
    rows = nseq * seq
    blk, sq, snap = _gdn_b_specs(nb, True)

    def body(val_ref, kc_ref, at_ref, qd_ref, kd_ref, gb_ref, snap_ref, do_ref,
             dval_ref, dkc_ref, dat_ref, dqd_ref, dkd_ref, dgb_ref, ds_ref):
        @pl.when(pl.program_id(2) == 0)
        def _():
            ds_ref[...] = jnp.zeros_like(ds_ref)

        hs = range(GDN_HP)
        d_states = [ds_ref[hh] for hh in hs]
        for c in reversed(range(N_CH)):
            _, vjp = jax.vjp(gdn_chunk(c), *[x for hh in hs for x in (
                *[_gdn_piece(r, hh, c)[...].astype(F32) for r in (val_ref, kc_ref, at_ref, qd_ref, kd_ref, gb_ref)],
                _gdn_snap(snap_ref, hh, c)[...])])
            grads = vjp(tuple(x for hh in hs for x in (_gdn_piece(do_ref, hh, c)[...], d_states[hh])))
            for hh in hs:
                for i, r in enumerate([dval_ref, dkc_ref, dat_ref, dqd_ref, dkd_ref, dgb_ref]):
                    _gdn_piece(r, hh, c)[...] = grads[7 * hh + i]
            d_states = [grads[7 * hh + 6] for hh in hs]
        for hh in hs:
            ds_ref[hh] = d_states[hh]

    wide = jax.ShapeDtypeStruct((rows, NH * LANES), F32)
    square = jax.ShapeDtypeStruct((rows, NH * GDN_ROWS), F32)
    return pl.pallas_call(
        body, name="gdn_b_bwd", grid=(nseq, NH // GDN_HP, nb), in_specs=[blk, blk, sq, blk, blk, blk, snap, blk],
        out_specs=[blk, blk, sq, blk, blk, blk], out_shape=[wide, wide, square, wide, wide, wide],
        scratch_shapes=[pltpu.VMEM((GDN_HP, LANES, LANES), F32)], compiler_params=_params(3))(val, kc, at, qd, kd, gb, snaps, do)


FOX_Q, FOX_K, FOX_V = 4 * NH, 5 * NH, 6 * NH
FOX_SCALE = LANES ** -0.5


def _head_row(ct_ref, h, off, width):
    blk = ct_ref[:, pl.ds(off, width)]
    return jnp.sum(jnp.where(_iota(blk.shape, 0) == h, blk, 0.0), axis=0, keepdims=True)


def _col(x):
    return jnp.max(x, axis=1, keepdims=True)


def _row(x):
    return jnp.max(x.T, axis=0, keepdims=True)


def _causal(shape, q_dim):
    return _iota(shape, q_dim) >= _iota(shape, 1 - q_dim)


FOX_HP = 4


def _fox_specs(seq, tile, n_tiles):
    tblk = pl.BlockSpec((tile, FOX_HP * LANES), lambda s, h, i: (s * n_tiles + i, h))
    vtblk = pl.BlockSpec((tile, FOX_HP * LANES), lambda s, h, i: (s * n_tiles + i, h + FOX_V // FOX_HP))
    full = pl.BlockSpec((seq, FOX_HP * LANES), lambda s, h, i: (s, h))
    vfull = pl.BlockSpec((seq, FOX_HP * LANES), lambda s, h, i: (s, h + FOX_V // FOX_HP))
    ctb = pl.BlockSpec((NH, seq), lambda s, h, i: (s * (LANES // NH) + 2, 0))
    return tblk, vtblk, full, vfull, ctb


def _lanes_of(hh):
    return slice(hh * LANES, (hh + 1) * LANES)


def fox_fwd(qn, kn, proj, ct, nseq, seq):
    tq = tk = min(ATT_TILE, seq)
    nq = seq // tq
    rows = nseq * seq
    qblk, _, full, vfull, ctb = _fox_specs(seq, tq, nq)
    hs = range(FOX_HP)

    def body(q_ref, k_ref, v_ref, ct_ref, o_ref, o16_ref, lse_ref):
        hb, i = pl.program_id(1), pl.program_id(2)
        q = [q_ref[:, _lanes_of(hh)] for hh in hs]

        def step(j, carry, diag):
            m, l, acc = (list(carry[t::3]) for t in range(3))
            off = pl.multiple_of(j * tk, tk)
            k = [k_ref[pl.ds(off, tk), _lanes_of(hh)] for hh in hs]
            v = [v_ref[pl.ds(off, tk), _lanes_of(hh)].astype(BF16) for hh in hs]
            ck = [_head_row(ct_ref, hb * FOX_HP + hh, off, tk) for hh in hs]
            s = _each(lambda qq, kk, cc: _dot(qq, kk, "nt") * FOX_SCALE - cc, q, k, ck)
            if diag:
                s = _each(lambda x: jnp.where(_causal(x.shape, 0), x, NEG), s)
            m_new = _each(lambda mm, x: jnp.maximum(mm, jnp.max(x, axis=1, keepdims=True)), m, s)
            p = _each(lambda x, mm: jnp.exp(x - mm), s, m_new)
            alpha = _each(lambda mo, mn: jnp.exp(mo - mn), m, m_new)
            l = _each(lambda a, ll, pp: a * ll + jnp.sum(pp, axis=1, keepdims=True), alpha, l, p)
            acc = _each(lambda a, ac, pp, vv: a * ac + _dot(pp.astype(BF16), vv, "nn"), alpha, acc, p, v)
            return tuple(x for head in zip(m_new, l, acc) for x in head)

        init = (jnp.full((tq, 1), NEG, F32), jnp.zeros((tq, 1), F32), jnp.zeros((tq, LANES), F32)) * FOX_HP
        res = step(i, lax.fori_loop(0, i, lambda j, c: step(j, c, False), init), True)
        for hh in hs:
            m, l, acc = res[3 * hh:3 * hh + 3]
            o = acc / l
            o_ref[:, _lanes_of(hh)] = o
            o16_ref[:, _lanes_of(hh)] = o.astype(BF16)
            lse_ref[:, _lanes_of(hh)] = jnp.broadcast_to(m + jnp.log(l), (tq, LANES))

    wide = (rows, NH * LANES)
    return pl.pallas_call(
        body, name="fox_fwd", grid=(nseq, NH // FOX_HP, nq), in_specs=[qblk, full, vfull, ctb], out_specs=[qblk] * 3,
        out_shape=[jax.ShapeDtypeStruct(wide, F32), jax.ShapeDtypeStruct(wide, BF16), jax.ShapeDtypeStruct(wide, F32)],
        compiler_params=_params(3))(qn, kn, proj, ct)


def fox_dq(qn, kn, proj, ct, do, lse, o, after, nseq, seq):
    tq = tk = min(ATT_TILE, seq)
    nq = seq // tq
    rows = nseq * seq
    qblk, _, full, vfull, ctb = _fox_specs(seq, tq, nq)
    hs = range(FOX_HP)

    def body(q_ref, k_ref, v_ref, ct_ref, do_ref, lse_ref, o_ref, *rest):
        dq_ref, dc_ref = rest[len(after):]
        hb, i = pl.program_id(1), pl.program_id(2)
        q = [q_ref[:, _lanes_of(hh)] for hh in hs]
        lse = [_col(lse_ref[:, _lanes_of(hh)]) for hh in hs]
        delta = [jnp.sum(do_ref[:, _lanes_of(hh)] * o_ref[:, _lanes_of(hh)], axis=1, keepdims=True) for hh in hs]
        do16 = [do_ref[:, _lanes_of(hh)].astype(BF16) for hh in hs]

        def step(j, carry, diag):
            dq, dc = (list(carry[t::2]) for t in range(2))
            off = pl.multiple_of(j * tk, tk)
            k = [k_ref[pl.ds(off, tk), _lanes_of(hh)] for hh in hs]
            v = [v_ref[pl.ds(off, tk), _lanes_of(hh)].astype(BF16) for hh in hs]
            ck = [_head_row(ct_ref, hb * FOX_HP + hh, off, tk) for hh in hs]
            p = _each(lambda qq, kk, cc, ll: jnp.exp(_dot(qq, kk, "nt") * FOX_SCALE - cc - ll), q, k, ck, lse)
            if diag:
                p = _each(lambda x: jnp.where(_causal(x.shape, 0), x, 0.0), p)
            dp = _each(lambda d, vv: _dot(d, vv, "nt"), do16, v)
            ds = _each(lambda pp, d, dl: pp * (d - dl), p, dp, delta)
            dq = _each(lambda a, x, kk: a + _dot(x.astype(BF16), kk, "nn"), dq, ds, k)
            dc = _each(lambda a, x: a + jnp.sum(x, axis=1, keepdims=True), dc, ds)
            return tuple(x for head in zip(dq, dc) for x in head)

        init = (jnp.zeros((tq, LANES), F32), jnp.zeros((tq, 1), F32)) * FOX_HP
        res = step(i, lax.fori_loop(0, i, lambda j, c: step(j, c, False), init), True)
        for hh in hs:
            dq_ref[:, _lanes_of(hh)] = res[2 * hh] * FOX_SCALE
            dc_ref[:, _lanes_of(hh)] = jnp.where(_iota((tq, LANES), 1) == 0, res[2 * hh + 1], 0.0)

    wide = jax.ShapeDtypeStruct((rows, NH * LANES), F32)
    return pl.pallas_call(
        body, name="fox_dq", grid=(nseq, NH // FOX_HP, nq), in_specs=[qblk, full, vfull, ctb, qblk, qblk, qblk] + [ANY] * len(after),
        out_specs=[qblk, qblk], out_shape=[wide, wide], compiler_params=_params(3))(qn, kn, proj, ct, do, lse, o, *after)


def fox_dkv(qn, kn, proj, cb, do, lse, o, after, nseq, seq):
    tq = tk = min(ATT_TILE, seq)
    nq = seq // tq
    rows = nseq * seq
    kblk, vblk, full, _, _ = _fox_specs(seq, tk, nq)
    hs = range(FOX_HP)

    def body(q_ref, k_ref, v_ref, cb_ref, do_ref, lse_ref, o_ref, *rest):
        dk_ref, dv_ref, dc_ref = rest[len(after):]
        j = pl.program_id(2)
        k = [k_ref[:, _lanes_of(hh)] for hh in hs]
        v16 = [v_ref[:, _lanes_of(hh)].astype(BF16) for hh in hs]
        ck = [_col(cb_ref[:, _lanes_of(hh)]) for hh in hs]

        def step(i, carry, diag):
            dk, dv, dc = (list(carry[t::3]) for t in range(3))
            off = pl.multiple_of(i * tq, tq)
            q = [q_ref[pl.ds(off, tq), _lanes_of(hh)] for hh in hs]
            do32 = [do_ref[pl.ds(off, tq), _lanes_of(hh)] for hh in hs]
            do16 = [d.astype(BF16) for d in do32]
            lse = [_row(lse_ref[pl.ds(off, tq), _lanes_of(hh)]) for hh in hs]
            delta = [_row(jnp.broadcast_to(jnp.sum(d * o_ref[pl.ds(off, tq), _lanes_of(hh)], axis=1, keepdims=True), (tq, LANES)))
                     for hh, d in zip(hs, do32)]
            p = _each(lambda kk, qq, cc, ll: jnp.exp(_dot(kk, qq, "nt") * FOX_SCALE - cc - ll), k, q, ck, lse)
            if diag:
                p = _each(lambda x: jnp.where(_causal(x.shape, 1), x, 0.0), p)
            dv = _each(lambda a, pp, d: a + _dot(pp.astype(BF16), d, "nn"), dv, p, do16)
            ds = _each(lambda pp, vv, d, dl: pp * (_dot(vv, d, "nt") - dl), p, v16, do16, delta)
            dk = _each(lambda a, x, qq: a + _dot(x.astype(BF16), qq, "nn"), dk, ds, q)
            dc = _each(lambda a, x: a + jnp.sum(x, axis=1, keepdims=True), dc, ds)
            return tuple(x for head in zip(dk, dv, dc) for x in head)

        zero = jnp.zeros((tk, LANES), F32)
        carry = step(j, (zero, zero, jnp.zeros((tk, 1), F32)) * FOX_HP, True)
        res = lax.fori_loop(j + 1, nq, lambda i, c: step(i, c, False), carry)
        for hh in hs:
            dk, dv, dc = res[3 * hh:3 * hh + 3]
            dk_ref[:, _lanes_of(hh)] = dk * FOX_SCALE
            dv_ref[:, _lanes_of(hh)] = dv.astype(BF16)
            dc_ref[:, _lanes_of(hh)] = jnp.where(_iota((tk, LANES), 1) == 0, -dc, 0.0)

    wide = (rows, NH * LANES)
    return pl.pallas_call(
        body, name="fox_dkv", grid=(nseq, NH // FOX_HP, nq), in_specs=[full, kblk, vblk, kblk, full, full, full] + [ANY] * len(after),
        out_specs=[kblk, kblk, kblk],
        out_shape=[jax.ShapeDtypeStruct(wide, F32), jax.ShapeDtypeStruct(wide, BF16), jax.ShapeDtypeStruct(wide, F32)],
        compiler_params=_params(3))(qn, kn, proj, cb, do, lse, o, *after)


def _adamw_update(w, g, m, v):
    m_new = ADAM_B1 * m + (1.0 - ADAM_B1) * g
    v_new = ADAM_B2 * v + (1.0 - ADAM_B2) * (g * g)
    m_hat = m_new / (1.0 - ADAM_B1 ** ADAM_STEP)
    v_hat = v_new / (1.0 - ADAM_B2 ** ADAM_STEP)
    return -ADAM_LR * (m_hat / (jnp.sqrt(v_hat) + ADAM_EPS) + ADAM_WD * w), m_new, v_new


def adamw(name, w, g, m, v):
    rows, cols = w.shape
    tb = min(rows, 128)
    assert rows % tb == 0
    blk = pl.BlockSpec((tb, cols), lambda i: (i, 0))

    def body(w_ref, g_ref, m_ref, v_ref, d_ref, mo_ref, vo_ref):
        d_ref[...], mo_ref[...], vo_ref[...] = _adamw_update(w_ref[...], g_ref[...], m_ref[...], v_ref[...])

    shp = jax.ShapeDtypeStruct(w.shape, F32)
    return pl.pallas_call(body, name=name, grid=(rows // tb,), in_specs=[blk] * 4, out_specs=[blk] * 3,
                          out_shape=[shp] * 3, compiler_params=_params(1))(w, g, m, v)


SPLIT_TILE = 128


def _tiled(shape2d, ax, n_lead, index):
    blk = (SPLIT_TILE, shape2d[1]) if ax == 0 else (shape2d[0], SPLIT_TILE)

    def index_map(*args):
        *lead, t = index(*args)
        return (*lead, t, 0) if ax == 0 else (*lead, 0, t)

    return pl.BlockSpec((None,) * n_lead + blk, index_map)


def adamw_halves(name, w, mine, other, m, v, c, ax):
    steps = w.shape[ax] // 2 // SPLIT_TILE
    assert w.shape[ax] == 2 * steps * SPLIT_TILE

    def body(c_ref, w_ref, mine_ref, other_ref, m_ref, v_ref, g_ref, d_ref, mo_ref, vo_ref):
        g = jnp.where(pl.program_id(0) // steps == c_ref[0], mine_ref[...], other_ref[...])
        g_ref[...] = g
        d_ref[...], mo_ref[...], vo_ref[...] = _adamw_update(w_ref[...], g, m_ref[...], v_ref[...])

    blk = _tiled(w.shape, ax, 0, lambda i, c_ref: (i,))
    hblk = _tiled(mine.shape, ax, 0, lambda i, c_ref: (i % steps,))
    grid_spec = pltpu.PrefetchScalarGridSpec(num_scalar_prefetch=1, grid=(2 * steps,),
                                             in_specs=[blk, hblk, hblk, blk, blk], out_specs=[blk] * 4)
    shp = jax.ShapeDtypeStruct(w.shape, F32)
    return pl.pallas_call(body, name=name, grid_spec=grid_spec, out_shape=[shp] * 4,
                          compiler_params=_params(1))(c, w, mine, other, m, v)


def add_chips(name, slots, parts, chip, axes):
    outs = []
    for idx, (x, own, ax) in enumerate(zip(slots, parts, axes)):
        n, shape2d = x.shape[0], x.shape[1:]
        steps = shape2d[ax] // SPLIT_TILE
        assert shape2d[ax] == steps * SPLIT_TILE

        def body(me_ref, *refs, n=n):
            o_ref = refs[n + 1]
            acc = None
            for t in range(n):
                term = jnp.where(me_ref[0] == t, refs[n][...], refs[t][...]).astype(F32)
                acc = term if acc is None else acc + term
            o_ref[...] = acc

        def filled(t, n=n):
            return lambda i, me_ref: (jnp.where(me_ref[0] == t, (t + 1) % n, t), i)

        grid_spec = pltpu.PrefetchScalarGridSpec(
            num_scalar_prefetch=1, grid=(steps,),
            in_specs=[_tiled(shape2d, ax, 1, filled(t)) for t in range(n)]
            + [_tiled(shape2d, ax, 1, lambda i, me_ref: (me_ref[0], i))],
            out_specs=_tiled(shape2d, ax, 0, lambda i, me_ref: (i,)))
        outs.append(pl.pallas_call(
            body, name=f"{name}_{idx}", grid_spec=grid_spec, out_shape=jax.ShapeDtypeStruct(shape2d, F32),
            compiler_params=_params(1))(chip, *([x] * n), own))
    return outs


def add_pair(name, gs, rs, c, axes):
    outs = []
    for idx, (g, r, ax) in enumerate(zip(gs, rs, axes)):
        nb = r.shape[0]
        steps = r.shape[1 + ax] // SPLIT_TILE
        assert r.shape[1 + ax] == steps * SPLIT_TILE

        def body(c_ref, g_ref, r_ref, o_ref):
            o_ref[...] = (g_ref[...] + r_ref[...]).astype(BF16)

        grid_spec = pltpu.PrefetchScalarGridSpec(
            num_scalar_prefetch=1, grid=(nb, steps),
            in_specs=[_tiled(g.shape[1:], ax, 1, lambda b, i, c_ref: (b, c_ref[0] * steps + i)),
                      _tiled(r.shape[1:], ax, 1, lambda b, i, c_ref: (b, i))],
            out_specs=_tiled(r.shape[1:], ax, 1, lambda b, i, c_ref: (b, i)))
        outs.append(pl.pallas_call(
            body, name=f"{name}_{idx}", grid_spec=grid_spec, out_shape=jax.ShapeDtypeStruct(r.shape, BF16),
            compiler_params=_params(2))(c, g, r))
    return outs


def _place():
    x, y, c = lax.axis_index("x"), lax.axis_index("y"), lax.axis_index("c")
    return x, y, c, [(1 - x, y), (x, 1 - y), (1 - x, 1 - y)]


def _remote(src, dst, send_sem, recv_sem, dev):
    return pltpu.make_async_remote_copy(src_ref=src, dst_ref=dst, send_sem=send_sem, recv_sem=recv_sem,
                                        device_id=dev, device_id_type=MESH)


def _half(ref, lead, ax, which):
    size = ref.shape[len(lead) + ax] // 2
    part = pl.ds(which * size, size)
    return ref.at[(*lead, part, slice(None)) if ax == 0 else (*lead, slice(None), part)]


def gather_ring(shard):
    rows, cols = shard.shape
    half = cols // 2
    top = rows // 2 // 16 * 16
    assert shard.dtype == BF16 and half % LANES == 0

    def body(in_ref, out_ref, ici_s, ici_r, d2d_s, d2d_r):
        x, y, c, _ = _place()
        me, xn, yn, dg = 2 * x + y, 2 * (1 - x) + y, 2 * x + (1 - y), 2 * (1 - x) + (1 - y)
        to_x, to_y, sib = (1 - x, y, c), (x, 1 - y, c), (x, y, 1 - c)
        mine, other = pl.ds(c * half, half), pl.ds((1 - c) * half, half)
        upper, lower = pl.ds(0, top), pl.ds(top, rows - top)
        started = []

        def send(src, dst, sems, k, dev):
            cp = _remote(src, dst, sems[0].at[k], sems[1].at[k], dev)
            cp.start()
            started.append(cp)

        def arrive(dst, sems, k):
            _remote(dst, dst, sems[0].at[k], sems[1].at[k], sib).wait_recv()

        ici, d2d = (ici_s, ici_r), (d2d_s, d2d_r)
        send(in_ref, out_ref.at[me], d2d, 0, sib)
        send(in_ref.at[upper, mine], out_ref.at[me, upper, mine], ici, 0, to_x)
        send(in_ref.at[lower, mine], out_ref.at[me, lower, mine], ici, 1, to_x)
        send(in_ref.at[lower, mine], out_ref.at[me, lower, mine], ici, 2, to_y)
        send(in_ref.at[upper, mine], out_ref.at[me, upper, mine], ici, 3, to_y)
        arrive(out_ref.at[xn, upper, mine], ici, 0)
        send(out_ref.at[xn, upper, mine], out_ref.at[xn, upper, mine], ici, 4, to_y)
        arrive(out_ref.at[yn, lower, mine], ici, 2)
        send(out_ref.at[yn, lower, mine], out_ref.at[yn, lower, mine], ici, 5, to_x)
        arrive(out_ref.at[xn, lower, mine], ici, 1)
        send(out_ref.at[xn, :, mine], out_ref.at[xn, :, mine], d2d, 1, sib)
        arrive(out_ref.at[yn, upper, mine], ici, 3)
        send(out_ref.at[yn, :, mine], out_ref.at[yn, :, mine], d2d, 2, sib)
        arrive(out_ref.at[dg, upper, mine], ici, 4)
        send(out_ref.at[dg, upper, mine], out_ref.at[dg, upper, mine], d2d, 3, sib)
        arrive(out_ref.at[dg, lower, mine], ici, 5)
        send(out_ref.at[dg, lower, mine], out_ref.at[dg, lower, mine], d2d, 4, sib)
        arrive(out_ref.at[me], d2d, 0)
        arrive(out_ref.at[xn, :, other], d2d, 1)
        arrive(out_ref.at[yn, :, other], d2d, 2)
        arrive(out_ref.at[dg, upper, other], d2d, 3)
        arrive(out_ref.at[dg, lower, other], d2d, 4)
        for cp in started:
            cp.wait_send()

    return pl.pallas_call(
        body, name="gather_ring", in_specs=[ANY], out_specs=ANY, out_shape=jax.ShapeDtypeStruct((4,) + shard.shape, shard.dtype),
        scratch_shapes=[pltpu.SemaphoreType.DMA((6,))] * 2 + [pltpu.SemaphoreType.DMA((5,))] * 2,
    )(shard)


HBM = pl.BlockSpec(memory_space=pltpu.HBM)
SEM = pl.BlockSpec(memory_space=pltpu.SEMAPHORE)
DATAFLOW = pltpu.SideEffectType.DATAFLOW_SIDE_EFFECTING


def _hbm(a):
    return pltpu.with_memory_space_constraint(a, pltpu.HBM)


class SplitExchange:
    def __init__(self, name, srcs, zone_shapes, n_sems, plan):
        self.name, self.n, self.n_sems, self.plan = name, len(srcs), n_sems, plan
        self.srcs = [_hbm(s) for s in srcs]
        self.zones = [_hbm(lax.empty(shape, s.dtype)) for shape, s in zip(zone_shapes, srcs)]

    def start(self, after):
        n, n_after = self.n, len(after)

        def body(*refs):
            ins, lands = refs[:n], refs[n:2 * n]
            send, recv, token = refs[2 * n + n_after], refs[2 * n + n_after + 1], refs[-1]
            for src, dst, si, ri, dev in self.plan(ins, lands)[0]:
                _remote(src, dst, send.at[si], recv.at[ri], dev).start()
            token[...] = jnp.zeros_like(token)

        res = pl.pallas_call(
            body, name=f"{self.name}_start", in_specs=[HBM] * (2 * n) + [ANY] * n_after,
            out_specs=[SEM, SEM] + [HBM] * (2 * n) + [pl.BlockSpec(memory_space=pltpu.VMEM)],
            out_shape=[pltpu.SemaphoreType.DMA((self.n_sems,)), pltpu.SemaphoreType.DMA((self.n_sems,))]
            + [pltpu.HBM(a.shape, a.dtype) for a in self.srcs + self.zones] + [jax.ShapeDtypeStruct((8, LANES), F32)],
            input_output_aliases={i: 2 + i for i in range(2 * n)},
            compiler_params=pltpu.CompilerParams(has_side_effects=DATAFLOW),
        )(*self.srcs, *self.zones, *after)
        self.sems, self.srcs, self.zones = res[:2], list(res[2:2 + n]), list(res[2 + n:2 + 2 * n])
        return res[-1]

    def wait(self, after):
        n = self.n

        def body(*refs):
            ins, lands = refs[:n], refs[n:2 * n]
            send, recv = refs[2 * n], refs[2 * n + 1]
            sends, arrivals = self.plan(ins, lands)
            for src, _, si, _, dev in sends:
                _remote(src, src, send.at[si], recv.at[si], dev).wait_send()
            for landed, ri in arrivals:
                _remote(landed, landed, send.at[ri], recv.at[ri], _place()[:3]).wait_recv()

        res = pl.pallas_call(
            body, name=f"{self.name}_wait", in_specs=[HBM] * (2 * n) + [SEM, SEM, ANY], out_specs=[HBM] * (2 * n),
            out_shape=[pltpu.HBM(a.shape, a.dtype) for a in self.srcs + self.zones],
            input_output_aliases={i: i for i in range(2 * n)},
            compiler_params=pltpu.CompilerParams(has_side_effects=DATAFLOW),
        )(*self.srcs, *self.zones, *self.sems, after)
        self.srcs = list(res[:n])
        return list(res[n:])


def split_gather(shards):
    n = len(shards)

    def plan(ins, lands):
        x, y, c, chips = _place()
        me = 2 * x + y
        sends, arrivals = [], []
        for w in range(n):
            for j, (ox, oy) in enumerate(chips):
                for k in range(2):
                    base = 2 * (3 * w + j)
                    sends.append((_half(ins[w], (), 0, c), _half(lands[w], (me,), 0, c), base + k, base + c, (ox, oy, k)))
                    arrivals.append((_half(lands[w], (2 * ox + oy,), 0, k), base + k))
            sends.append((ins[w], lands[w].at[me], 6 * n + w, 6 * n + w, (x, y, 1 - c)))
            arrivals.append((lands[w].at[me], 6 * n + w))
        return sends, arrivals

    return SplitExchange("gather", shards, [(4,) + s.shape for s in shards], 7 * n, plan)


def split_pair_swap(name, grads, axes):
    def plan(ins, lands):
        x, y, c, _ = _place()
        sends = [(_half(ins[w], (slice(None),), axes[w], 1 - c), lands[w], w, w, (x, y, 1 - c)) for w in range(len(ins))]
        return sends, [(lands[w], w) for w in range(len(ins))]

    halved = [tuple(d // 2 if i == 1 + ax else d for i, d in enumerate(g.shape)) for g, ax in zip(grads, axes)]
    return SplitExchange(name, grads, halved, len(grads), plan)


def split_chip_exchange(name, parts):
    def plan(ins, lands):
        x, y, c, chips = _place()
        sends, arrivals = [], []
        for w in range(len(ins)):
            for j, (ox, oy) in enumerate(chips):
                sends.append((ins[w].at[2 * ox + oy], lands[w].at[2 * x + y], 3 * w + j, 3 * w + j, (ox, oy, c)))
                arrivals.append((lands[w].at[2 * ox + oy], 3 * w + j))
        return sends, arrivals

    return SplitExchange(name, parts, [p.shape for p in parts], 3 * len(parts), plan)


def split_pair_send(halves):
    def plan(ins, lands):
        x, y, c, _ = _place()
        return ([(ins[w], lands[w], w, w, (x, y, 1 - c)) for w in range(len(ins))],
                [(lands[w], w) for w in range(len(ins))])

    return SplitExchange("pair_send", halves, [h.shape for h in halves], len(halves), plan)


def pair_send(halves):
    n = len(halves)

    def body(*refs):
        ins, outs = refs[:n], refs[n:2 * n]
        send, recv = refs[2 * n:]
        x, y, c, _ = _place()
        cps = [_remote(ins[w], outs[w], send.at[w], recv.at[w], (x, y, 1 - c)) for w in range(n)]
        for cp in cps:
            cp.start()
        for cp in cps:
            cp.wait_recv()
        for cp in cps:
            cp.wait_send()

    return pl.pallas_call(
        body, name="pair_send", in_specs=[ANY] * n, out_specs=[ANY] * n,
        out_shape=[jax.ShapeDtypeStruct(h.shape, h.dtype) for h in halves],
        scratch_shapes=[pltpu.SemaphoreType.DMA((n,))] * 2,
    )(*halves)


def all_reduce_small(name, vec, after=()):
    rows = vec.shape[0]

    def body(v_ref, *refs):
        o_ref, buf, send, recv = refs[len(after):]
        x, y, c, _ = _place()
        me = 4 * x + 2 * y + c
        buf[me] = v_ref[...]
        cps = []
        for k in range(1, 8):
            kx, ky, kc = (k >> 2) & 1, (k >> 1) & 1, k & 1
            peer = (x if kx == 0 else 1 - x, y if ky == 0 else 1 - y, c if kc == 0 else 1 - c)
            cp = _remote(v_ref, buf.at[me], send.at[k - 1], recv.at[k - 1], peer)
            cp.start()
            cps.append(cp)
        for k in range(1, 8):
            kx, ky, kc = (k >> 2) & 1, (k >> 1) & 1, k & 1
            px, py, pc = (x if kx == 0 else 1 - x, y if ky == 0 else 1 - y, c if kc == 0 else 1 - c)
            slot = buf.at[4 * px + 2 * py + pc]
            _remote(slot, slot, send.at[k - 1], recv.at[k - 1], (px, py, pc)).wait_recv()
        for cp in cps:
            cp.wait_send()
        acc = buf[0]
        for d in range(1, 8):
            acc = acc + buf[d]
        o_ref[...] = acc

    vm = pl.BlockSpec(memory_space=pltpu.VMEM)
    return pl.pallas_call(
        body, name=name, in_specs=[vm] + [ANY] * len(after), out_specs=vm, out_shape=jax.ShapeDtypeStruct(vec.shape, F32),
        scratch_shapes=[pltpu.VMEM((8, rows, LANES), F32), pltpu.SemaphoreType.DMA((7,)), pltpu.SemaphoreType.DMA((7,))],
    )(vec, *after)


class NoExchange:
    def __init__(self, late):
        self.late = late

    def late_weights(self, after):
        return self.late

    def reduce_start(self, grads):
        return jnp.zeros((8, LANES), F32)

    def reduce_exchange(self, after):
        return jnp.zeros((8, LANES), F32)

    def reduce_finish(self, after):
        return jnp.zeros((8, LANES), F32)

    def input_grad_start(self, dw_main, dw_small):
        return jnp.zeros((8, LANES), F32)

    def input_grad_exchange(self, after):
        return jnp.zeros((8, LANES), F32)


def local_step(x2, tgt2, g1, g2, gdn_ng, qn_g, kn_g, p1, p2, conv_w, wt_main, wt_small, hooks, nseq, seq):
    rows, dm = x2.shape
    wide = NH * LANES
    row = lambda a, off=0, w=None: (a, "row", off, a.shape[1] if w is None else w)
    rowh = lambda a, off=0, w=LANES: (a, "rowh", off, w)
    par = lambda a: (a, "par", 0, a.shape[1])
    parh = lambda a, off=0: (a, "parh", off, LANES)
    o_row = lambda w, dt: (w, "row", w, dt)
    o_rowh = lambda dt, tw=wide, w=LANES: (tw, "rowh", w, dt)

    u, = ew_fwd("rms1", f_rms, [row(x2), par(g1)], [o_row(dm, BF16)], rows)
    proj = matmul("mm_in", u, wt_main, "nt", BF16)
    sp = matmul("mm_in_small", u, wt_small, "nt", F32)
    so, = ew_fwd("small", f_small, [row(sp), par(p1), par(p2)], [o_row(LANES, F32)], rows)
    cs = cumsum_time("cumsum", so, nseq, seq, False)
    gb, bb, cb = ew_fwd("bcast", f_bcast, [row(so), row(cs)], [o_rowh(F32)] * 3, rows, NH)
    ct = transpose_time("c_time_major", cs, nseq, seq)
    conv = {}
    for mode, off in (("q", 0), ("k", NH), ("v", 2 * NH)):
        conv[mode], = ew_fwd(f"conv_{mode}", make_f_conv(mode), [rowh(proj, off), parh(conv_w, off)], [o_rowh(F32)],
                             rows, NH, seq, "hi", CONV_HEADS)
    val, kcum, attn, qdec, kdec, t_inv = gdn_a_fwd(conv["q"], conv["k"], conv["v"], gb, bb, rows)
    o_a, snaps = gdn_b_fwd(val, kcum, attn, qdec, kdec, gb, nseq, seq)
    ya_in, = ew_fwd("gdn_post", f_post, [rowh(o_a), rowh(proj, 3 * NH), par(gdn_ng)], [o_rowh(BF16)], rows, NH)
    fqn, = ew_fwd("fox_qn", f_rms, [rowh(proj, FOX_Q), par(qn_g)], [o_rowh(BF16)], rows, NH)
    fkn, = ew_fwd("fox_kn", f_rms, [rowh(proj, FOX_K), par(kn_g)], [o_rowh(BF16)], rows, NH)
    o_b, o_b16, lse = fox_fwd(fqn, fkn, proj, ct, nseq, seq)
    p_a, p_b, w_o, w_u, w_d = hooks.late_weights(o_a)
    y_a = matmul("mm_pa", ya_in, p_a, "nn", F32, tn=1024)
    y_b = matmul("mm_pb", o_b16, p_b, "nn", F32, tn=1024)
    gates = [row(proj, 7, dm), row(proj, 8, dm)]
    merged, = ew_fwd("merge", f_merge, gates + [row(y_a), row(y_b)], [o_row(dm, BF16)], rows)
    hres = matmul("mm_out", merged, w_o, "nn", F32, add=x2, tn=1024)
    hn, = ew_fwd("rms2", f_rms, [row(hres), par(g2)], [o_row(dm, BF16)], rows)
    up_blocks = w_u.shape[0]
    act, relu2 = matmul("mm_up", hn, w_u, "nn", F32, col_blocks=up_blocks, out_dtypes=[F32, BF16],
                        epilogue=lambda r: [r, jnp.maximum(r, 0.0) * jnp.maximum(r, 0.0)])
    def loss_tail(r, h_tile, t_tile):
        d = (r + h_tile) - t_tile
        e = (0.5 / dm) * (d * d)
        part = e.reshape(e.shape[0] // 8, 8, e.shape[1]).sum(axis=0)
        part = sum(part[:, t * LANES:(t + 1) * LANES] for t in range(e.shape[1] // LANES))
        g = d * (1.0 / dm)
        return [g, g, part]

    dout, dout16, loss_acc = matmul("mm_down", relu2, w_d, "nn", F32, extras=[hres, tgt2], epilogue=loss_tail,
                                    out_dtypes=[F32, BF16, F32], tile_sums=True)

    d_act = matmul("mm_d_act", dout16, w_d, "nt", BF16, extras=[act], epilogue=lambda r, a: [2.0 * jnp.maximum(a, 0.0) * r])
    dw_d = matmul("mm_dw_down", relu2, dout16, "tn", F32, tn=1024)
    dw_u = matmul("mm_dw_up", hn, d_act, "tn", F32, col_blocks=up_blocks)
    d_hn = matmul("mm_d_hn", d_act, w_u, "nt", F32, col_blocks=up_blocks)
    dh, dh16, dg2 = ew_bwd("rms2_b", f_rms, [row(hres), par(g2)], [(row(d_hn),)], [row(dout)],
                           lambda g, e: [g[0] + e[0], g[0] + e[0], g[1]],
                           [((rows, dm), "row", dm, F32, None), ((rows, dm), "row", dm, BF16, None), ((1, dm), "par", dm, F32, "all")], rows)
    d_merged = matmul("mm_d_merged", dh16, w_o, "nt", F32, tn=1024)
    dw_o = matmul("mm_dw_out", merged, dh16, "tn", F32, tn=1024)
    seg16 = ((rows, dm), "row", dm, BF16, None)
    d_ga16, d_gb16, d_ya16, d_yb16 = ew_bwd("merge_b", f_merge, gates + [row(y_a), row(y_b)], [(row(d_merged),)], [],
                                            lambda g, e: list(g), [seg16] * 4, rows)
    dp_a = matmul("mm_dp_a", ya_in, d_ya16, "tn", F32, tn=1024)
    d_ya_in = matmul("mm_d_ya_in", d_ya16, p_a, "nt", F32, tn=1024)
    dp_b = matmul("mm_dp_b", o_b16, d_yb16, "tn", F32, tn=1024)
    d_ob = matmul("mm_d_ob", d_yb16, p_b, "nt", F32, tn=1024)
    token = hooks.reduce_start(dict(p_a=dp_a, p_b=dp_b, w_o=dw_o, w_u=dw_u, w_d=dw_d))
    gdn_ng_t = gdn_ng + token[0, 0]
    h32 = ((rows, wide), "rowh", LANES, F32, None)
    h16 = ((rows, wide), "rowh", LANES, BF16, None)
    gain = ((1, LANES), "par", LANES, F32, "all")
    d_oa, d_z16, d_gdn_ng = ew_bwd("gdn_post_b", f_post, [rowh(o_a), rowh(proj, 3 * NH), par(gdn_ng_t)], [(rowh(d_ya_in),)], [],
                                   lambda g, e: list(g), [h32, h16, gain], rows, NH)
    dval, dkc, dat, dqd, dkd, dgb_b = gdn_b_bwd(val, kcum, attn, qdec, kdec, gb, snaps, d_oa, nseq, seq)
    d_cq, d_ck, d_cv, d_gb, d_bb = gdn_a_bwd(conv["q"], conv["k"], conv["v"], gb, bb, t_inv, dval, dkc, dat, dqd, dkd, dgb_b, rows)
    token = hooks.reduce_exchange(d_cq)
    conv_w_t = conv_w + token[0, 0]
    d_pre, d_conv = {}, {}
    tap = ((4, wide), "parh", LANES, F32, "inner")
    for mode, off, ctg in (("q", 0, d_cq), ("k", NH, d_ck), ("v", 2 * NH, d_cv)):
        d_pre[mode], d_conv[mode] = ew_bwd(f"conv_{mode}_b", make_f_conv(mode), [rowh(proj, off), parh(conv_w_t, off)],
                                           [(rowh(ctg),)], [], lambda g, e: list(g), [h16, tap], rows, NH, seq, "hi", CONV_HEADS)
    d_fqn, d_cq_b = fox_dq(fqn, fkn, proj, ct, d_ob, lse, o_b, [token], nseq, seq)
    d_fkn, d_fv16, d_ck_b = fox_dkv(fqn, fkn, proj, cb, d_ob, lse, o_b, [token], nseq, seq)
    token = hooks.reduce_finish(d_fkn)
    qn_g_t, kn_g_t = qn_g + token[0, 0], kn_g + token[0, 0]
    d_fq16, d_qn_g = ew_bwd("fox_qn_b", f_rms, [rowh(proj, FOX_Q), par(qn_g_t)], [(rowh(d_fqn),)], [], lambda g, e: list(g),
                            [h16, gain], rows, NH)
    d_fk16, d_kn_g = ew_bwd("fox_kn_b", f_rms, [rowh(proj, FOX_K), par(kn_g_t)], [(rowh(d_fkn),)], [], lambda g, e: list(g),
                            [h16, gain], rows, NH)
    narrow = ((rows, LANES), "row", LANES, F32, None)
    d_so, d_cs = ew_bwd("bcast_b", f_bcast, [row(so), row(cs)], [(rowh(d_gb),), (rowh(d_bb),), (rowh(d_cq_b), rowh(d_ck_b))], [],
                        lambda g, e: list(g), [narrow, narrow], rows, NH)
    d_logf = cumsum_time("cumsum_b", d_cs, nseq, seq, True)
    vec = ((1, LANES), "par", LANES, F32, "all")
    d_sp16, d_p1, d_p2 = ew_bwd("small_b", f_small, [row(sp), par(p1), par(p2)], [(row(d_so), row(d_logf))], [],
                                lambda g, e: list(g), [((rows, LANES), "row", LANES, BF16, None), vec, vec], rows)
    d_proj16 = jnp.concatenate([d_pre["q"], d_pre["k"], d_pre["v"], d_z16, d_fq16, d_fk16, d_fv16, d_ga16, d_gb16], axis=1)
    dw_main = matmul("mm_dw_main", d_proj16, u, "tn", F32)
    dw_small = matmul("mm_dw_small", d_sp16, u, "tn", F32)
    wt_small_t = wt_small + hooks.input_grad_start(dw_main, dw_small)[0, 0].astype(BF16)
    d_u = matmul("mm_d_u_small", d_sp16, wt_small_t, "nn", F32)
    d_u = matmul("mm_d_u_first", d_proj16, wt_main, "nn", F32, add=d_u, k_part=(0, 2))
    d_u = matmul("mm_d_u_second", d_proj16, wt_main, "nn", F32, add=d_u, k_part=(1, 2), after=[hooks.input_grad_exchange(d_u)])
    dx, dg1 = ew_bwd("rms1_b", f_rms, [row(x2), par(g1)], [(row(d_u),)], [row(dh)], lambda g, e: [g[0] + e[0], g[1]],
                     [((rows, dm), "row", dm, F32, None), ((1, dm), "par", dm, F32, "all")], rows)
    d_conv_w = jnp.concatenate([d_conv["q"], d_conv["k"], d_conv["v"]], axis=1)
    return dict(loss_acc=loss_acc, dx=dx, g1=dg1, g2=dg2, gdn_ng=d_gdn_ng, qn=d_qn_g, kn=d_kn_g, p1=d_p1, p2=d_p2,
                conv=d_conv_w, w_main=dw_main, w_small=dw_small, p_a=dp_a, p_b=dp_b, w_o=dw_o, w_u=dw_u, w_d=dw_d)


_W = NH * LANES
_A0, _A1 = 4 * _W, 4 * _W + 2 * NH
_B0, _B1 = _A1 + 3 * _W, _A1 + 3 * _W + NH


def _split_w_in(full_t):
    main = jnp.concatenate([full_t[:_A0], full_t[_A1:_B0], full_t[_B1:]], axis=0)
    small = jnp.concatenate([full_t[_A0:_A1], full_t[_B0:_B1], jnp.zeros((LANES - 3 * NH, full_t.shape[1]), full_t.dtype)], axis=0)
    return main, small


def _join_w_in(main, small):
    return jnp.concatenate([main[:_A0], small[:2 * NH], main[_A0:_A0 + 3 * _W], small[2 * NH:3 * NH], main[_A0 + 3 * _W:]], axis=0)


def _lanes(v, at=0):
    return jnp.pad(v.reshape(1, -1), ((0, 0), (at, LANES - at - v.size)))


def kernel(x, norm_mix_g, w_in, gdn_conv_w, gdn_a_log, gdn_dt_bias, gdn_norm_g, fox_q_norm_g, fox_k_norm_g, fox_f_bias, w_proj_gdn, w_proj_fox, w_out, norm_mlp_g, w_up, w_down, loss_target, m_norm_mix_g, m_w_in, m_gdn_conv_w, m_gdn_a_log, m_gdn_dt_bias, m_gdn_norm_g, m_fox_q_norm_g, m_fox_k_norm_g, m_fox_f_bias, m_w_proj_gdn, m_w_proj_fox, m_w_out, m_norm_mlp_g, m_w_up, m_w_down, v_norm_mix_g, v_w_in, v_gdn_conv_w, v_gdn_a_log, v_gdn_dt_bias, v_gdn_norm_g, v_fox_q_norm_g, v_fox_k_norm_g, v_fox_f_bias, v_w_proj_gdn, v_w_proj_fox, v_w_out, v_norm_mlp_g, v_w_up, v_w_down):
    nseq, seq, dm = x.shape
    rows = nseq * seq
    xi, yi, ci = lax.axis_index("x"), lax.axis_index("y"), lax.axis_index("c")
    chip = 2 * xi + yi
    conv_cols = gdn_conv_w.shape[2]

    tr = lambda a: jnp.swapaxes(a[0], 0, 1)
    big = [tr(w_in), w_proj_gdn[0], w_proj_fox[0], w_out[0], w_up[0], w_down[0]]
    axes = [1, 0, 0, 0, 0, 0]
    big16 = [w.astype(BF16) for w in big]
    conv_slot = jnp.zeros((4, 4, conv_cols), F32).at[:, chip].set(jnp.where(ci == 0, gdn_conv_w[0], 0.0))
    conv_full = all_reduce_small("gather_conv", conv_slot.reshape(-1, LANES)).reshape(4, 4 * conv_cols)
    got_in = gather_ring(big16[0])
    wt_main, wt_small = _split_w_in(got_in.reshape(-1, dm))
    core, chip_no = ci.reshape(1).astype(jnp.int32), chip.reshape(1).astype(jnp.int32)
    gather = split_gather(big16[1:])
    token = gather.start([got_in, conv_full])

    class Hooks:
        def late_weights(self, after):
            g_pa, g_pb, g_wo, w_u, g_wd = gather.wait(after)
            return (*(g.reshape(-1, dm) for g in (g_pa, g_pb, g_wo)), w_u, g_wd.reshape(-1, dm))

        def reduce_start(self, grads):
            blocks = [grads["p_a"].reshape(4, -1, dm), grads["p_b"].reshape(4, -1, dm), grads["w_o"].reshape(4, -1, dm),
                      grads["w_u"], grads["w_d"].reshape(4, -1, dm)]
            self.swap = split_pair_swap("pair_swap_late", blocks, axes[1:])
            return self.swap.start([])

        def reduce_exchange(self, after):
            swapped = self.swap.wait(after)
            self.exchange = split_chip_exchange("chip_exchange_late", add_pair("add_pair_late", self.swap.srcs, swapped, core, axes[1:]))
            return self.exchange.start([])

        def reduce_finish(self, after):
            slots = self.exchange.wait(after)
            self.send = split_pair_send(add_chips("add_chips_late", slots, self.exchange.srcs, chip_no, axes[1:]))
            return self.send.start([])

        def input_grad_start(self, dw_main, dw_small):
            self.in_swap = split_pair_swap("pair_swap_in", [_join_w_in(dw_main, dw_small).reshape(4, -1, dm)], axes[:1])
            return self.in_swap.start([])

        def input_grad_exchange(self, after):
            swapped = self.in_swap.wait(after)
            self.in_exchange = split_chip_exchange("chip_exchange_in", add_pair("add_pair_in", self.in_swap.srcs, swapped, core, axes[:1]))
            return self.in_exchange.start([])

    hooks = Hooks()
    p1 = _lanes(gdn_dt_bias[0]) + _lanes(fox_f_bias[0], 2 * NH)
    p2 = _lanes(gdn_a_log[0])

    g = local_step(x.reshape(rows, dm), loss_target.reshape(rows, dm), norm_mix_g + token[0, 0], norm_mlp_g, gdn_norm_g,
                   fox_q_norm_g, fox_k_norm_g, p1, p2, conv_full, wt_main, wt_small, hooks, nseq, seq)

    others = hooks.send.wait(g["dx"])
    big_m = [tr(m_w_in), m_w_proj_gdn[0], m_w_proj_fox[0], m_w_out[0], m_w_up[0], m_w_down[0]]
    big_v = [tr(v_w_in), v_w_proj_gdn[0], v_w_proj_fox[0], v_w_out[0], v_w_up[0], v_w_down[0]]
    names = ["w_in", "w_proj_gdn", "w_proj_fox", "w_out", "w_up", "w_down"]
    big_res, big_grad = {}, {}
    for i in range(1, len(names)):
        big_grad[names[i]], *big_res[names[i]] = adamw_halves(f"adamw_{names[i]}", big[i], hooks.send.srcs[i - 1], others[i - 1],
                                                              big_m[i], big_v[i], core, axes[i])
    slots = hooks.in_exchange.wait(big_res[names[-1]][0])
    mine = add_chips("add_chips_in", slots, hooks.in_exchange.srcs, chip_no, axes[:1])
    res = adamw_halves("adamw_w_in", big[0], mine[0], pair_send(mine)[0], big_m[0], big_v[0], core, axes[0])
    big_grad["w_in"], *big_res["w_in"] = [jnp.swapaxes(r, 0, 1) for r in res]

    small_parts = [g["loss_acc"], g["g1"].reshape(8, LANES), g["g2"].reshape(8, LANES), g["gdn_ng"], g["qn"], g["kn"], g["p1"], g["p2"],
                   g["conv"].reshape(-1, LANES)]
    tiled = [jnp.pad(p, ((0, -p.shape[0] % 8), (0, 0))) for p in small_parts]
    red = all_reduce_small("reduce_small", jnp.concatenate(tiled, axis=0), slots)
    pos, red_parts = 0, []
    for p, t in zip(small_parts, tiled):
        red_parts.append(red[pos:pos + p.shape[0]])
        pos += t.shape[0]
    r_loss, r_g1, r_g2, r_gdn_ng, r_qn, r_kn, r_p1, r_p2, r_conv = red_parts
    loss = jnp.sum(r_loss)
    g_conv = lax.dynamic_slice_in_dim(r_conv.reshape(4, 4, conv_cols), chip, 1, axis=1).reshape(4, conv_cols)
    small_grads = [r_g1.reshape(1, dm), r_p2[:, :NH], r_p1[:, :NH], r_gdn_ng, r_qn, r_kn, r_p1[:, 2 * NH:3 * NH], r_g2.reshape(1, dm)]
    small_w = [norm_mix_g, gdn_a_log, gdn_dt_bias, gdn_norm_g, fox_q_norm_g, fox_k_norm_g, fox_f_bias, norm_mlp_g]
    small_m = [m_norm_mix_g, m_gdn_a_log, m_gdn_dt_bias, m_gdn_norm_g, m_fox_q_norm_g, m_fox_k_norm_g, m_fox_f_bias, m_norm_mlp_g]
    small_v = [v_norm_mix_g, v_gdn_a_log, v_gdn_dt_bias, v_gdn_norm_g, v_fox_q_norm_g, v_fox_k_norm_g, v_fox_f_bias, v_norm_mlp_g]

    def pack(parts):
        flat = jnp.concatenate([jnp.pad(p.reshape(-1), (0, -p.size % LANES)) for p in parts])
        return jnp.pad(flat, (0, -flat.size % (8 * LANES))).reshape(-1, LANES)

    packed = adamw("adamw_small", pack(small_w + [gdn_conv_w[0]]), pack(small_grads + [g_conv]),
                   pack(small_m + [m_gdn_conv_w[0]]), pack(small_v + [v_gdn_conv_w[0]]))

    def unpack(flat2d):
        flat, pos, res = flat2d.reshape(-1), 0, []
        for p in small_w + [gdn_conv_w[0]]:
            res.append(flat[pos:pos + p.size].reshape(p.shape))
            pos += p.size + (-p.size % LANES)
        return res

    s_delta, s_m, s_v = (unpack(a) for a in packed)

    order = ["norm_mix_g", "w_in", "gdn_conv_w", "gdn_a_log", "gdn_dt_bias", "gdn_norm_g", "fox_q_norm_g", "fox_k_norm_g",
             "fox_f_bias", "w_proj_gdn", "w_proj_fox", "w_out", "norm_mlp_g", "w_up", "w_down"]
    small_names = ["norm_mix_g", "gdn_a_log", "gdn_dt_bias", "gdn_norm_g", "fox_q_norm_g", "fox_k_norm_g", "fox_f_bias", "norm_mlp_g",
                   "gdn_conv_w"]
    small_idx = {nm: i for i, nm in enumerate(small_names)}
    shapes = dict(zip(order, (a.shape for a in (norm_mix_g, w_in, gdn_conv_w, gdn_a_log, gdn_dt_bias, gdn_norm_g, fox_q_norm_g,
                                                 fox_k_norm_g, fox_f_bias, w_proj_gdn, w_proj_fox, w_out, norm_mlp_g, w_up, w_down))))
    grads_out, delta_out, m_out, v_out = [], [], [], []
    for nm in order:
        if nm in big_res:
            d, mm, vv = big_res[nm]
            gr = big_grad[nm]
        else:
            i = small_idx[nm]
            gr = (small_grads + [g_conv])[i]
            d, mm, vv = s_delta[i], s_m[i], s_v[i]
        for lst, val in ((grads_out, gr), (delta_out, d), (m_out, mm), (v_out, vv)):
            lst.append(val.reshape(shapes[nm]))
    return (loss, g["dx"].reshape(x.shape), *grads_out, *delta_out, *m_out, *v_out)
```

```python
import functools

import jax
import jax.numpy as jnp
from jax import lax
from jax.experimental import pallas as pl
from jax.experimental.pallas import tpu as pltpu

F32 = jnp.float32
BF16 = jnp.bfloat16
LANES = 128
NH = 8
EPS = 1e-6
GDN_CHUNK = 64
GDN_ROWS = 256
GDN_BASE = 16
ROW_TILE = 512
CONV_HEADS = 2
ATT_TILE = 512
NEG = -1e30
VMEM_LIMIT_BYTES = 58 * 1024 * 1024
LO = lax.Precision.DEFAULT
MESH = pl.DeviceIdType.MESH
ANY = pl.BlockSpec(memory_space=pl.ANY)

ADAM_LR, ADAM_B1, ADAM_B2, ADAM_EPS, ADAM_WD, ADAM_STEP = 0.001, 0.9, 0.999, 1e-08, 0.01, 10


def _params(n_grid):
    return pltpu.CompilerParams(dimension_semantics=("arbitrary",) * n_grid,
                                vmem_limit_bytes=VMEM_LIMIT_BYTES)


def _dot(a, b, dims, precision=None):
    dn = {"nn": (((1,), (0,)), ((), ())), "nt": (((1,), (1,)), ((), ())), "tn": (((0,), (0,)), ((), ()))}[dims]
    return lax.dot_general(a, b, dn, precision=precision, preferred_element_type=F32)


def _iota(shape, dim):
    return lax.broadcasted_iota(jnp.int32, shape, dim)


def _split(x, parts):
    out = []
    for _ in range(parts - 1):
        hi = x.astype(BF16)
        out.append(hi)
        x = x - hi.astype(F32)
    return out + [x.astype(BF16)]


def _dot_mask(mask, b, dims, terms=3):
    m16 = mask.astype(BF16)
    acc = None
    for part in reversed(_split(b, terms)):
        prod = _dot(m16, part, dims)
        acc = prod if acc is None else acc + prod
    return acc


@jax.custom_vjp
def mm_mask(mask, b):
    return _dot_mask(mask, b, "nn", 2)


mm_mask.defvjp(lambda mask, b: (_dot_mask(mask, b, "nn", 2), mask),
               lambda mask, g: (jnp.zeros_like(mask), _dot_mask(mask, g, "tn", 2)))


def matmul(name, a, b, dims, out_dtype, add=None, tm=1024, tn=1024, tk=512, col_blocks=None,
           extras=(), epilogue=None, out_dtypes=None, k_part=None, after=(), tile_sums=False):
    if col_blocks and dims != "tn":
        nb, b_rows, bw = b.shape
        b_shape = (b_rows, nb * bw)
    else:
        b_shape = b.shape
    if dims == "nn":
        (m, k), (_, n) = a.shape, b_shape
    elif dims == "nt":
        (m, k), (n, _) = a.shape, b_shape
    else:
        (k, m), (_, n) = a.shape, b_shape
    k_span = k // (k_part[1] if k_part else 1)
    if col_blocks and dims == "nt":
        k_span = min(k_span, bw)
    tk = k if k <= 1024 else max(t for t in (2048, 1536, 1024, 512, tk) if k_span % t == 0)
    tm, tn, tk = min(tm, m), min(tn, n), min(tk, k)
    assert m % tm == 0 and n % tn == 0 and k % tk == 0, (name, m, n, k)
    k0, nk = (0, k // tk) if k_part is None else (k_part[0] * (k // tk // k_part[1]), k // tk // k_part[1])
    assert k_part is None or (dims == "nn" and not col_blocks and (k // tk) % k_part[1] == 0)
    a_spec = pl.BlockSpec((tk, tm), lambda i, j, kk: (kk, i)) if dims == "tn" else pl.BlockSpec((tm, tk), lambda i, j, kk: (i, kk + k0))
    b_spec = pl.BlockSpec((tn, tk), lambda i, j, kk: (j, kk)) if dims == "nt" else pl.BlockSpec((tk, tn), lambda i, j, kk: (kk + k0, j))
    o_spec = pl.BlockSpec((tm, tn), lambda i, j, kk: (i, j))
    out_shape = (m, n)
    if col_blocks and dims == "nn":
        per = bw // tn
        assert bw % tn == 0
        b_spec = pl.BlockSpec((None, tk, tn), lambda i, j, kk: (j // per, kk, j % per))
    elif col_blocks and dims == "nt":
        per = bw // tk
        assert bw % tk == 0
        b_spec = pl.BlockSpec((None, tn, tk), lambda i, j, kk: (kk // per, j, kk % per))
    elif col_blocks:
        bw = n // col_blocks
        per = bw // tn
        assert bw % tn == 0 and add is None
        o_spec = pl.BlockSpec((None, tm, tn), lambda i, j, kk: (j // per, i, j % per))
        out_shape = (col_blocks, m, bw)
    extras = list(extras) + ([add] if add is not None else [])
    if add is not None:
        assert epilogue is None
        epilogue = lambda r, *e: [r + e[-1]]
    out_dtypes = [out_dtype] if epilogue is None or out_dtypes is None else list(out_dtypes)
    n_ex, n_out = len(extras), len(out_dtypes)

    def body(*refs):
        a_ref, b_ref = refs[0], refs[1]
        ex_refs, o_refs = refs[2:2 + n_ex], refs[2 + n_ex + len(after):2 + n_ex + len(after) + n_out]

        def finish(r):
            res = [r] if epilogue is None else epilogue(r, *[e[...] for e in ex_refs])
            for o_ref, v in zip(o_refs, res):
                o_ref[...] = v.astype(o_ref.dtype)

        if nk == 1:
            finish(_dot(a_ref[...], b_ref[...], dims))
            return
        acc_ref = refs[-1]
        kk = pl.program_id(2)

        @pl.when(kk == 0)
        def _():
            acc_ref[...] = jnp.zeros_like(acc_ref)

        acc_ref[...] += _dot(a_ref[...], b_ref[...], dims)

        @pl.when(kk == nk - 1)
        def _():
            finish(acc_ref[...])

    out_specs = [o_spec] * n_out
    out_shapes = [jax.ShapeDtypeStruct(out_shape, dt) for dt in out_dtypes]
    if tile_sums:
        out_specs[-1] = pl.BlockSpec((8, LANES), lambda i, j, kk: (i, j))
        out_shapes[-1] = jax.ShapeDtypeStruct((8 * (m // tm), LANES * (n // tn)), out_dtypes[-1])
    res = pl.pallas_call(
        body, name=name, grid=(m // tm, n // tn, nk), in_specs=[a_spec, b_spec] + [o_spec] * n_ex + [ANY] * len(after),
        out_specs=out_specs, out_shape=out_shapes,
        scratch_shapes=[pltpu.VMEM((tm, tn), F32)] if nk > 1 else [], compiler_params=_params(3),
    )(a, b, *extras, *after)
    return res[0] if n_out == 1 else res


def _ew_spec(kind, off, width, tb, hp, order, shape=None):
    def ih(g0, g1):
        return (g0, g1) if order == "ih" else (g1, g0)

    assert off % hp == 0 or kind in ("row", "par")
    if kind == "row":
        return pl.BlockSpec((tb, width), lambda g0, g1: (ih(g0, g1)[0], off))
    if kind == "rowh":
        return pl.BlockSpec((tb, hp * width), lambda g0, g1: (ih(g0, g1)[0], ih(g0, g1)[1] + off // hp))
    if kind == "par":
        return pl.BlockSpec(shape, lambda g0, g1: (0, 0))
    if kind == "parh":
        return pl.BlockSpec((shape[0], hp * width), lambda g0, g1: (0, ih(g0, g1)[1] + off // hp))
    raise ValueError(kind)


def _ew_grid(rows, tb, nh, hp, order):
    assert nh % hp == 0 and rows % tb == 0
    return (rows // tb, nh // hp) if order == "ih" else (nh // hp, rows // tb)


def _ew_load(ref, kind, width, hh):
    if kind in ("row", "par"):
        return ref[...].astype(F32)
    return ref[:, hh * width:(hh + 1) * width].astype(F32)


def ew_fwd(name, f, ins, outs, rows, nh=1, tb=ROW_TILE, order="ih", hp=None, after=()):
    hp = nh if hp is None else hp
    n_in = len(ins)

    def body(*refs):
        hb = pl.program_id(1) if order == "ih" else pl.program_id(0)
        for hh in range(hp):
            h = hh if hp == nh else hb * hp + hh
            vals = [_ew_load(r, kd, w, hh) for r, (_, kd, _, w) in zip(refs[:n_in], ins)]
            res = f(h, *vals)
            for r, v, (_, kd, w, _) in zip(refs[n_in + len(after):], res, outs):
                if kd == "row":
                    assert hp == 1
                    r[...] = v.astype(r.dtype)
                else:
                    r[:, hh * w:(hh + 1) * w] = v.astype(r.dtype)

    in_specs = [_ew_spec(kd, off, w, tb, hp, order, a.shape) for (a, kd, off, w) in ins]
    out_specs = [_ew_spec(kd, 0, w, tb, hp, order) for (_, kd, w, _) in outs]
    out_shape = [jax.ShapeDtypeStruct((rows, tw), dt) for (tw, _, _, dt) in outs]
    return pl.pallas_call(
        body, name=name, grid=_ew_grid(rows, tb, nh, hp, order), in_specs=in_specs + [ANY] * len(after), out_specs=out_specs,
        out_shape=out_shape, compiler_params=_params(2),
    )(*[a for (a, _, _, _) in ins], *after)


def ew_bwd(name, f, ins, cts, extras, emit, outs, rows, nh=1, tb=ROW_TILE, order="ih", hp=None):
    hp = nh if hp is None else hp
    n_in = len(ins)
    flat_cts = [d for group in cts for d in group]
    n_ct, n_ex = len(flat_cts), len(extras)

    def body(*refs):
        g0, g1 = pl.program_id(0), pl.program_id(1)
        hb = g1 if order == "ih" else g0
        out_refs = refs[n_in + n_ct + n_ex:]
        shared = [None] * len(outs)

        def store(r, v, first, sl=None):
            def put(val, add):
                if sl is None:
                    r[...] = (r[...] + val if add else val).astype(r.dtype)
                else:
                    r[:, sl] = (r[:, sl] + val if add else val).astype(r.dtype)

            if first is None:
                put(v, False)
            else:
                pl.when(first)(lambda: put(v, False))
                pl.when(jnp.logical_not(first))(lambda: put(v, True))

        for hh in range(hp):
            h = hh if hp == nh else hb * hp + hh
            vals = [_ew_load(r, kd, w, hh) for r, (_, kd, _, w) in zip(refs[:n_in], ins)]
            ct_refs = list(zip(refs[n_in:n_in + n_ct], flat_cts))
            ct_vals, pos = [], 0
            for group in cts:
                v = None
                for r, (_, kd, _, w) in ct_refs[pos:pos + len(group)]:
                    t = _ew_load(r, kd, w, hh)
                    v = t if v is None else v + t
                pos += len(group)
                ct_vals.append(v)
            ex_vals = [_ew_load(r, kd, w, hh) for r, (_, kd, _, w) in zip(refs[n_in + n_ct:n_in + n_ct + n_ex], extras)]
            _, vjp = jax.vjp(lambda *a: f(h, *a), *vals)
            res = emit(vjp(tuple(ct_vals)), ex_vals)
            for idx, (r, v, (_, kd, w, _, acc)) in enumerate(zip(out_refs, res, outs)):
                if kd in ("row", "par"):
                    shared[idx] = v if shared[idx] is None else shared[idx] + v
                else:
                    store(r, v, (g1 == 0) if acc == "inner" else None, slice(hh * w, (hh + 1) * w))
        for idx, (r, (_, kd, _, _, acc)) in enumerate(zip(out_refs, outs)):
            if kd in ("row", "par"):
                assert acc == "all" or hp == nh
                store(r, shared[idx], jnp.logical_and(g0 == 0, g1 == 0) if acc == "all" else None)

    operands = list(ins) + flat_cts + list(extras)
    in_specs = [_ew_spec(kd, off, w, tb, hp, order, a.shape) for (a, kd, off, w) in operands]
    out_specs = [_ew_spec(kd, 0, w, tb, hp, order, shp) for (shp, kd, w, _, _) in outs]
    out_shape = [jax.ShapeDtypeStruct(shp, dt) for (shp, _, _, dt, _) in outs]
    return pl.pallas_call(
        body, name=name, grid=_ew_grid(rows, tb, nh, hp, order), in_specs=in_specs, out_specs=out_specs,
        out_shape=out_shape, compiler_params=_params(2),
    )(*[a for (a, _, _, _) in operands])


def f_rms(h, x, g):
    r = lax.rsqrt(jnp.mean(x * x, axis=-1, keepdims=True) + EPS)
    return (x * r * g,)


def _softplus(z):
    return jnp.maximum(z, 0.0) + jnp.log1p(jnp.exp(-jnp.abs(z)))


def f_small(h, sp, p1, p2):
    lane = _iota(sp.shape, 1)
    z = sp + p1
    g = -jnp.exp(p2) * _softplus(z)
    beta = jax.nn.sigmoid(z)
    logf = -_softplus(-z)
    return (jnp.where(lane < NH, g, jnp.where(lane < 2 * NH, beta, jnp.where(lane < 3 * NH, logf, 0.0))),)


def _pick(x, lane_id):
    lane = _iota(x.shape, 1)
    col = jnp.sum(jnp.where(lane == lane_id, x, 0.0), axis=1, keepdims=True)
    return jnp.broadcast_to(col, x.shape)


def f_bcast(h, so, cs):
    return _pick(so, h), _pick(so, h + NH), _pick(cs, h + 2 * NH)


def _shift_down(s):
    def down(x):
        r = pltpu.roll(x, s, 0)
        head = jnp.where(_iota((8, x.shape[1]), 0) >= s, r[:8], 0.0)
        return jnp.concatenate([head, r[8:]], axis=0)

    def up(g):
        n = g.shape[0]
        r = pltpu.roll(g, n - s, 0)
        tail = jnp.where(_iota((8, g.shape[1]), 0) < 8 - s, r[n - 8:], 0.0)
        return jnp.concatenate([r[:n - 8], tail], axis=0)

    @jax.custom_vjp
    def shift(x):
        return down(x)

    shift.defvjp(lambda x: (down(x), None), lambda _, g: (up(g),))
    return shift


def _silu(x):
    return x * jax.nn.sigmoid(x)


def make_f_conv(mode):
    sh1, sh2, sh3 = _shift_down(1), _shift_down(2), _shift_down(3)

    def f(h, x, w):
        sub = _iota(w.shape, 0)

        def tap(i):
            return jnp.sum(jnp.where(sub == i, w, 0.0), axis=0, keepdims=True)

        y = sh3(x) * tap(0)
        y = y + sh2(x) * tap(1)
        y = y + sh1(x) * tap(2)
        y = y + x * tap(3)
        s = _silu(y)
        if mode == "v":
            return (s,)
        n = s * lax.rsqrt(jnp.sum(s * s, axis=-1, keepdims=True) + EPS)
        if mode == "q":
            n = n * (LANES ** -0.5)
        return (n,)

    return f


def f_post(h, o, z, g):
    r = lax.rsqrt(jnp.mean(o * o, axis=-1, keepdims=True) + EPS)
    return (o * r * g * _silu(z),)


def f_merge(h, ga, gb, ya, yb):
    return (jax.nn.sigmoid(ga) * ya + jax.nn.sigmoid(gb) * yb,)


def cumsum_time(name, x, nseq, seq, reverse):
    nb = seq // LANES

    def body(x_ref, o_ref):
        r, c = _iota((LANES, LANES), 0), _iota((LANES, LANES), 1)
        tri = jnp.where((r <= c) if reverse else (r >= c), 1.0, 0.0).astype(F32)
        carry = jnp.zeros((1, LANES), F32)
        for b in (range(nb - 1, -1, -1) if reverse else range(nb)):
            blk = x_ref[b * LANES:(b + 1) * LANES, :]
            o_ref[b * LANES:(b + 1) * LANES, :] = _dot_mask(tri, blk, "nn") + carry
            carry = carry + jnp.sum(blk, axis=0, keepdims=True)

    spec = pl.BlockSpec((seq, LANES), lambda s: (s, 0))
    return pl.pallas_call(body, name=name, grid=(nseq,), in_specs=[spec], out_specs=spec,
                          out_shape=jax.ShapeDtypeStruct(x.shape, F32), compiler_params=_params(1))(x)


def transpose_time(name, x, nseq, seq):
    def body(x_ref, o_ref):
        o_ref[...] = x_ref[...].T

    return pl.pallas_call(
        body, name=name, grid=(nseq,), in_specs=[pl.BlockSpec((seq, LANES), lambda s: (s, 0))],
        out_specs=pl.BlockSpec((LANES, seq), lambda s: (s, 0)),
        out_shape=jax.ShapeDtypeStruct((nseq * LANES, seq), F32), compiler_params=_params(1))(x)


def _gdn_masks():
    n = GDN_ROWS
    r, c = _iota((n, n), 0), _iota((n, n), 1)
    shift = GDN_CHUNK.bit_length() - 1
    same = lax.shift_right_logical(r, shift) == lax.shift_right_logical(c, shift)
    return r, c, same


def _each(fn, *lists):
    return [fn(*xs) for xs in zip(*lists)]


def _gdn_decay(gbs):
    r, c, same = _gdn_masks()
    seg_tril = jnp.where(jnp.logical_and(same, r >= c), 1.0, 0.0).astype(F32)
    g_cum = _each(lambda gb: mm_mask(seg_tril, gb), gbs)
    lane0 = _iota(gbs[0].shape, 1) == 0
    g_col = _each(lambda g: jnp.sum(jnp.where(lane0, g, 0.0), axis=1, keepdims=True), g_cum)
    g_row = _each(lambda g: jnp.sum(jnp.where(r == c, jnp.broadcast_to(g, (GDN_ROWS, GDN_ROWS)), 0.0), axis=0, keepdims=True), g_col)
    return g_cum, _each(lambda a, b: a - b, g_col, g_row)


def gdn_a_mats(ks, bbs, diff):
    r, c, same = _gdn_masks()
    strict = jnp.logical_and(same, r > c)
    lane0 = _iota(bbs[0].shape, 1) == 0
    beta_col = _each(lambda bb: jnp.sum(jnp.where(lane0, bb, 0.0), axis=1, keepdims=True), bbs)
    kk = _each(lambda k: _dot(k, k, "nt", LO), ks)
    return _each(lambda b, x, d: jnp.where(strict, b * x * jnp.exp(jnp.where(strict, d, 0.0)), 0.0), beta_col, kk, diff)


@jax.custom_vjp
def saved_inverse(a, t_corr):
    return t_corr


def _saved_inverse_bwd(t, dt):
    left = dt + _dot(t, dt, "tn", LO)
    return -(left + _dot(left, t, "nt", LO)), jnp.zeros_like(t)


saved_inverse.defvjp(lambda a, t_corr: (t_corr, t_corr), _saved_inverse_bwd)


def gdn_block(*args):
    ts, qs, ks, vs, gbs, bbs = (list(args[i::6]) for i in range(6))
    g_cum, diff = _gdn_decay(gbs)
    ts = _each(saved_inverse, gdn_a_mats(ks, bbs, diff), ts)
    return gdn_outputs(ts, qs, ks, vs, gbs, bbs, g_cum, diff)


def gdn_outputs(ts, qs, ks, vs, gbs, bbs, g_cum, diff):
    r, c, same = _gdn_masks()
    incl = jnp.logical_and(same, r >= c)
    decay = _each(lambda d: jnp.where(incl, jnp.exp(jnp.where(incl, d, 0.0)), 0.0), diff)
    e_g = _each(jnp.exp, g_cum)
    v_beta = _each(lambda v, bb: v * bb, vs, bbs)
    k_beta = _each(lambda k, bb, e: k * bb * e, ks, bbs, e_g)
    value = _each(lambda t, x: x + _dot(t, x, "nn", LO), ts, v_beta)
    k_cum = _each(lambda t, x: x + _dot(t, x, "nn", LO), ts, k_beta)
    attn = _each(lambda q, k, d: _dot(q, k, "nt", LO) * d, qs, ks, decay)
    ones = jnp.where(same, 1.0, 0.0).astype(F32)
    g_last = _each(lambda gb: mm_mask(ones, gb), gbs)
    q_dec = _each(lambda q, e: q * e, qs, e_g)
    k_dec = _each(lambda k, gl, g: k * jnp.exp(gl - g), ks, g_last, g_cum)
    return tuple(x for head in zip(value, k_cum, attn, q_dec, k_dec) for x in head)


def tri_inverse(mats):
    n = GDN_ROWS
    r, c = _iota((n, n), 0), _iota((n, n), 1)
    shift = GDN_BASE.bit_length() - 1
    blk = lax.shift_right_logical(r, shift) == lax.shift_right_logical(c, shift)
    each = lambda fn, *lists: [fn(*xs) for xs in zip(*lists)]
    mm = lambda x, y: _dot(x, y, "nn", LO)
    d = each(lambda a: jnp.where(blk, a, 0.0), mats)
    lo = each(lambda a, dd: a - dd, mats, d)
    p = each(lambda dd: -dd, d)
    c_d = p
    for _ in range(shift - 1):
        p = each(mm, p, p)
        c_d = each(lambda cd, pp, prod: cd + pp + prod, c_d, p, each(mm, c_d, p))
    assert GDN_CHUNK // GDN_BASE == 4
    nmat = each(lambda l, prod: l + prod, lo, each(mm, c_d, lo))
    n2 = each(mm, nmat, nmat)
    c_n = each(lambda nn2, nm, prod: (nn2 - nm) - prod, n2, nmat, each(mm, nmat, n2))
    return each(lambda cn, cd, prod: cn + cd + prod, c_n, c_d, each(mm, c_n, c_d))


GDN_AHP = 8


def _gdn_a_specs():
    blk = pl.BlockSpec((GDN_ROWS, GDN_AHP * LANES), lambda i, h: (i, h))
    sq = pl.BlockSpec((GDN_ROWS, GDN_AHP * GDN_ROWS), lambda i, h: (i, h))
    return blk, sq


def _head(ref, hh):
    width = ref.shape[1] // GDN_AHP
    return ref.at[:, hh * width:(hh + 1) * width]


def gdn_a_fwd(q, k, v, gb, bb, rows):
    blk, sq = _gdn_a_specs()

    def body(q_ref, k_ref, v_ref, gb_ref, bb_ref, val_ref, kc_ref, at_ref, qd_ref, kd_ref, t_ref):
        heads = [[_head(r, hh)[...] for r in (q_ref, k_ref, v_ref, gb_ref, bb_ref)] for hh in range(GDN_AHP)]
        qs, ks, vs, gbs, bbs = (list(col) for col in zip(*heads))
        g_cum, diff = _gdn_decay(gbs)
        t_corr = tri_inverse(gdn_a_mats(ks, bbs, diff))
        res = gdn_outputs(t_corr, qs, ks, vs, gbs, bbs, g_cum, diff)
        for hh in range(GDN_AHP):
            for r, x in zip((val_ref, kc_ref, at_ref, qd_ref, kd_ref, t_ref), (*res[5 * hh:5 * hh + 5], t_corr[hh])):
                _head(r, hh)[...] = x.astype(r.dtype)

    wide = lambda dt: jax.ShapeDtypeStruct((rows, NH * LANES), dt)
    square = jax.ShapeDtypeStruct((rows, NH * GDN_ROWS), BF16)
    return pl.pallas_call(
        body, name="gdn_a_fwd", grid=(rows // GDN_ROWS, NH // GDN_AHP), in_specs=[blk] * 5,
        out_specs=[blk, blk, sq, blk, blk, sq], out_shape=[wide(F32), wide(BF16), square, wide(BF16), wide(BF16), square],
        compiler_params=_params(2))(q, k, v, gb, bb)


def gdn_a_bwd(q, k, v, gb, bb, t_inv, dval, dkc, dat, dqd, dkd, dgb_b, rows):
    blk, sq = _gdn_a_specs()

    def body(q_ref, k_ref, v_ref, gb_ref, bb_ref, t_ref, dval_ref, dkc_ref, dat_ref, dqd_ref, dkd_ref, dgbb_ref,
             dq_ref, dk_ref, dv_ref, dgb_ref, dbb_ref):
        hs = range(GDN_AHP)
        heads = [[_head(r, hh)[...] for r in (q_ref, k_ref, v_ref, gb_ref, bb_ref)] for hh in hs]
        tvs = [_head(t_ref, hh)[...].astype(F32) for hh in hs]
        _, vjp = jax.vjp(gdn_block, *[x for t, head in zip(tvs, heads) for x in (t, *head)])
        grads = vjp(tuple(_head(r, hh)[...] for hh in hs for r in (dval_ref, dkc_ref, dat_ref, dqd_ref, dkd_ref)))
        for hh in hs:
            _, dq, dk, dv, dgb, dbb = grads[6 * hh:6 * hh + 6]
            _head(dq_ref, hh)[...] = dq
            _head(dk_ref, hh)[...] = dk
            _head(dv_ref, hh)[...] = dv
            _head(dgb_ref, hh)[...] = dgb + _head(dgbb_ref, hh)[...]
            _head(dbb_ref, hh)[...] = dbb

    wide = jax.ShapeDtypeStruct((rows, NH * LANES), F32)
    return pl.pallas_call(
        body, name="gdn_a_bwd", grid=(rows // GDN_ROWS, NH // GDN_AHP),
        in_specs=[blk] * 5 + [sq, blk, blk, sq, blk, blk, blk], out_specs=[blk] * 5, out_shape=[wide] * 5,
        compiler_params=_params(2))(q, k, v, gb, bb, t_inv, dval, dkc, dat, dqd, dkd, dgb_b)


N_CH = GDN_ROWS // GDN_CHUNK


GDN_HP = 8


def gdn_chunk(c):
    def f(*args):
        val, kc, at, qd, kd, gb, s = (list(args[i::7]) for i in range(7))
        zero = jnp.zeros((GDN_CHUNK, LANES), F32)
        v_new = _each(lambda v, k, st: v - _dot(k, st, "nn", LO), val, kc, s)
        v_pad = _each(lambda v: jnp.concatenate([zero] * c + [v] + [zero] * (N_CH - 1 - c), axis=0), v_new)
        out = _each(lambda q, st, a, vp: _dot(q, st, "nn", LO) + _dot(a, vp, "nn", LO), qd, s, at, v_pad)
        dec = _each(lambda g: jnp.exp(jnp.sum(g, axis=0, keepdims=True)), gb)
        s_new = _each(lambda st, d, k, v: st * d + _dot(k, v, "tn", LO), s, dec, kd, v_new)
        return tuple(x for head in zip(out, s_new) for x in head)

    return f


def _gdn_piece(ref, hh, c):
    width = ref.shape[1] // GDN_HP
    return ref.at[c * GDN_CHUNK:(c + 1) * GDN_CHUNK, hh * width:(hh + 1) * width]


def _gdn_snap(ref, hh, c):
    row = (hh * N_CH + c) * LANES
    return ref.at[row:row + LANES, :]


def _gdn_b_specs(nb, rev):
    def blk_row(s, j):
        return s * nb + (nb - 1 - j if rev else j)

    blk = pl.BlockSpec((GDN_ROWS, GDN_HP * LANES), lambda s, hb, j: (blk_row(s, j), hb))
    sq = pl.BlockSpec((GDN_ROWS, GDN_HP * GDN_ROWS), lambda s, hb, j: (blk_row(s, j), hb))
    snap = pl.BlockSpec((GDN_HP * N_CH * LANES, LANES), lambda s, hb, j: (blk_row(s, j) * (NH // GDN_HP) + hb, 0))
    return blk, sq, snap


def gdn_b_fwd(val, kc, at, qd, kd, gb, nseq, seq):
    nb = seq // GDN_ROWS
    rows = nseq * seq
    blk, sq, snap = _gdn_b_specs(nb, False)

    def body(val_ref, kc_ref, at_ref, qd_ref, kd_ref, gb_ref, o_ref, snap_ref, s_ref):
        @pl.when(pl.program_id(2) == 0)
        def _():
            s_ref[...] = jnp.zeros_like(s_ref)

        hs = range(GDN_HP)
        states = [s_ref[hh] for hh in hs]
        for c in range(N_CH):
            for hh in hs:
                _gdn_snap(snap_ref, hh, c)[...] = states[hh]
            res = gdn_chunk(c)(*[x for hh in hs for x in (
                *[_gdn_piece(r, hh, c)[...].astype(F32) for r in (val_ref, kc_ref, at_ref, qd_ref, kd_ref, gb_ref)], states[hh])])
            for hh in hs:
                _gdn_piece(o_ref, hh, c)[...] = res[2 * hh]
            states = [res[2 * hh + 1] for hh in hs]
        for hh in hs:
            s_ref[hh] = states[hh]

    return pl.pallas_call(
        body, name="gdn_b_fwd", grid=(nseq, NH // GDN_HP, nb), in_specs=[blk, blk, sq, blk, blk, blk], out_specs=[blk, snap],
        out_shape=[jax.ShapeDtypeStruct((rows, NH * LANES), F32),
                   jax.ShapeDtypeStruct((nseq * nb * NH * N_CH * LANES, LANES), F32)],
        scratch_shapes=[pltpu.VMEM((GDN_HP, LANES, LANES), F32)], compiler_params=_params(3))(val, kc, at, qd, kd, gb)


def gdn_b_bwd(val, kc, at, qd, kd, gb, snaps, do, nseq, seq):
    nb = seq // GDN_ROWS
    rows = nseq * seq
    blk, sq, snap = _gdn_b_specs(nb, True)

    def body(val_ref, kc_ref, at_ref, qd_ref, kd_ref, gb_ref, snap_ref, do_ref,
             dval_ref, dkc_ref, dat_ref, dqd_ref, dkd_ref, dgb_ref, ds_ref):
        @pl.when(pl.program_id(2) == 0)
        def _():
            ds_ref[...] = jnp.zeros_like(ds_ref)

        hs = range(GDN_HP)
        d_states = [ds_ref[hh] for hh in hs]
        for c in reversed(range(N_CH)):
            _, vjp = jax.vjp(gdn_chunk(c), *[x for hh in hs for x in (
                *[_gdn_piece(r, hh, c)[...].astype(F32) for r in (val_ref, kc_ref, at_ref, qd_ref, kd_ref, gb_ref)],
                _gdn_snap(snap_ref, hh, c)[...])])
            grads = vjp(tuple(x for hh in hs for x in (_gdn_piece(do_ref, hh, c)[...], d_states[hh])))
            for hh in hs:
                for i, r in enumerate([dval_ref, dkc_ref, dat_ref, dqd_ref, dkd_ref, dgb_ref]):
                    _gdn_piece(r, hh, c)[...] = grads[7 * hh + i]
            d_states = [grads[7 * hh + 6] for hh in hs]
        for hh in hs:
            ds_ref[hh] = d_states[hh]

    wide = jax.ShapeDtypeStruct((rows, NH * LANES), F32)
    square = jax.ShapeDtypeStruct((rows, NH * GDN_ROWS), F32)
    return pl.pallas_call(
        body, name="gdn_b_bwd", grid=(nseq, NH // GDN_HP, nb), in_specs=[blk, blk, sq, blk, blk, blk, snap, blk],
        out_specs=[blk, blk, sq, blk, blk, blk], out_shape=[wide, wide, square, wide, wide, wide],
        scratch_shapes=[pltpu.VMEM((GDN_HP, LANES, LANES), F32)], compiler_params=_params(3))(val, kc, at, qd, kd, gb, snaps, do)


FOX_Q, FOX_K, FOX_V = 4 * NH, 5 * NH, 6 * NH
FOX_SCALE = LANES ** -0.5


def _head_row(ct_ref, h, off, width):
    blk = ct_ref[:, pl.ds(off, width)]
    return jnp.sum(jnp.where(_iota(blk.shape, 0) == h, blk, 0.0), axis=0, keepdims=True)


def _col(x):
    return jnp.max(x, axis=1, keepdims=True)


def _row(x):
    return jnp.max(x.T, axis=0, keepdims=True)


def _causal(shape, q_dim):
    return _iota(shape, q_dim) >= _iota(shape, 1 - q_dim)


FOX_HP = 4


def _fox_specs(seq, tile, n_tiles):
    tblk = pl.BlockSpec((tile, FOX_HP * LANES), lambda s, h, i: (s * n_tiles + i, h))
    vtblk = pl.BlockSpec((tile, FOX_HP * LANES), lambda s, h, i: (s * n_tiles + i, h + FOX_V // FOX_HP))
    full = pl.BlockSpec((seq, FOX_HP * LANES), lambda s, h, i: (s, h))
    vfull = pl.BlockSpec((seq, FOX_HP * LANES), lambda s, h, i: (s, h + FOX_V // FOX_HP))
    ctb = pl.BlockSpec((NH, seq), lambda s, h, i: (s * (LANES // NH) + 2, 0))
    return tblk, vtblk, full, vfull, ctb


def _lanes_of(hh):
    return slice(hh * LANES, (hh + 1) * LANES)


def fox_fwd(qn, kn, proj, ct, nseq, seq):
    tq = tk = min(ATT_TILE, seq)
    nq = seq // tq
    rows = nseq * seq
    qblk, _, full, vfull, ctb = _fox_specs(seq, tq, nq)
    hs = range(FOX_HP)

    def body(q_ref, k_ref, v_ref, ct_ref, o_ref, o16_ref, lse_ref):
        hb, i = pl.program_id(1), pl.program_id(2)
        q = [q_ref[:, _lanes_of(hh)] for hh in hs]

        def step(j, carry, diag):
            m, l, acc = (list(carry[t::3]) for t in range(3))
            off = pl.multiple_of(j * tk, tk)
            k = [k_ref[pl.ds(off, tk), _lanes_of(hh)] for hh in hs]
            v = [v_ref[pl.ds(off, tk), _lanes_of(hh)].astype(BF16) for hh in hs]
            ck = [_head_row(ct_ref, hb * FOX_HP + hh, off, tk) for hh in hs]
            s = _each(lambda qq, kk, cc: _dot(qq, kk, "nt") * FOX_SCALE - cc, q, k, ck)
            if diag:
                s = _each(lambda x: jnp.where(_causal(x.shape, 0), x, NEG), s)
            m_new = _each(lambda mm, x: jnp.maximum(mm, jnp.max(x, axis=1, keepdims=True)), m, s)
            p = _each(lambda x, mm: jnp.exp(x - mm), s, m_new)
            alpha = _each(lambda mo, mn: jnp.exp(mo - mn), m, m_new)
            l = _each(lambda a, ll, pp: a * ll + jnp.sum(pp, axis=1, keepdims=True), alpha, l, p)
            acc = _each(lambda a, ac, pp, vv: a * ac + _dot(pp.astype(BF16), vv, "nn"), alpha, acc, p, v)
            return tuple(x for head in zip(m_new, l, acc) for x in head)

        init = (jnp.full((tq, 1), NEG, F32), jnp.zeros((tq, 1), F32), jnp.zeros((tq, LANES), F32)) * FOX_HP
        res = step(i, lax.fori_loop(0, i, lambda j, c: step(j, c, False), init), True)
        for hh in hs:
            m, l, acc = res[3 * hh:3 * hh + 3]
            o = acc / l
            o_ref[:, _lanes_of(hh)] = o
            o16_ref[:, _lanes_of(hh)] = o.astype(BF16)
            lse_ref[:, _lanes_of(hh)] = jnp.broadcast_to(m + jnp.log(l), (tq, LANES))

    wide = (rows, NH * LANES)
    return pl.pallas_call(
        body, name="fox_fwd", grid=(nseq, NH // FOX_HP, nq), in_specs=[qblk, full, vfull, ctb], out_specs=[qblk] * 3,
        out_shape=[jax.ShapeDtypeStruct(wide, F32), jax.ShapeDtypeStruct(wide, BF16), jax.ShapeDtypeStruct(wide, F32)],
        compiler_params=_params(3))(qn, kn, proj, ct)


def fox_dq(qn, kn, proj, ct, do, lse, o, after, nseq, seq):
    tq = tk = min(ATT_TILE, seq)
    nq = seq // tq
    rows = nseq * seq
    qblk, _, full, vfull, ctb = _fox_specs(seq, tq, nq)
    hs = range(FOX_HP)

    def body(q_ref, k_ref, v_ref, ct_ref, do_ref, lse_ref, o_ref, *rest):
        dq_ref, dc_ref = rest[len(after):]
        hb, i = pl.program_id(1), pl.program_id(2)
        q = [q_ref[:, _lanes_of(hh)] for hh in hs]
        lse = [_col(lse_ref[:, _lanes_of(hh)]) for hh in hs]
        delta = [jnp.sum(do_ref[:, _lanes_of(hh)] * o_ref[:, _lanes_of(hh)], axis=1, keepdims=True) for hh in hs]
        do16 = [do_ref[:, _lanes_of(hh)].astype(BF16) for hh in hs]

        def step(j, carry, diag):
            dq, dc = (list(carry[t::2]) for t in range(2))
            off = pl.multiple_of(j * tk, tk)
            k = [k_ref[pl.ds(off, tk), _lanes_of(hh)] for hh in hs]
            v = [v_ref[pl.ds(off, tk), _lanes_of(hh)].astype(BF16) for hh in hs]
            ck = [_head_row(ct_ref, hb * FOX_HP + hh, off, tk) for hh in hs]
            p = _each(lambda qq, kk, cc, ll: jnp.exp(_dot(qq, kk, "nt") * FOX_SCALE - cc - ll), q, k, ck, lse)
            if diag:
                p = _each(lambda x: jnp.where(_causal(x.shape, 0), x, 0.0), p)
            dp = _each(lambda d, vv: _dot(d, vv, "nt"), do16, v)
            ds = _each(lambda pp, d, dl: pp * (d - dl), p, dp, delta)
            dq = _each(lambda a, x, kk: a + _dot(x.astype(BF16), kk, "nn"), dq, ds, k)
            dc = _each(lambda a, x: a + jnp.sum(x, axis=1, keepdims=True), dc, ds)
            return tuple(x for head in zip(dq, dc) for x in head)

        init = (jnp.zeros((tq, LANES), F32), jnp.zeros((tq, 1), F32)) * FOX_HP
        res = step(i, lax.fori_loop(0, i, lambda j, c: step(j, c, False), init), True)
        for hh in hs:
            dq_ref[:, _lanes_of(hh)] = res[2 * hh] * FOX_SCALE
            dc_ref[:, _lanes_of(hh)] = jnp.where(_iota((tq, LANES), 1) == 0, res[2 * hh + 1], 0.0)

    wide = jax.ShapeDtypeStruct((rows, NH * LANES), F32)
    return pl.pallas_call(
        body, name="fox_dq", grid=(nseq, NH // FOX_HP, nq), in_specs=[qblk, full, vfull, ctb, qblk, qblk, qblk] + [ANY] * len(after),
        out_specs=[qblk, qblk], out_shape=[wide, wide], compiler_params=_params(3))(qn, kn, proj, ct, do, lse, o, *after)


def fox_dkv(qn, kn, proj, cb, do, lse, o, after, nseq, seq):
    tq = tk = min(ATT_TILE, seq)
    nq = seq // tq
    rows = nseq * seq
    kblk, vblk, full, _, _ = _fox_specs(seq, tk, nq)
    hs = range(FOX_HP)

    def body(q_ref, k_ref, v_ref, cb_ref, do_ref, lse_ref, o_ref, *rest):
        dk_ref, dv_ref, dc_ref = rest[len(after):]
        j = pl.program_id(2)
        k = [k_ref[:, _lanes_of(hh)] for hh in hs]
        v16 = [v_ref[:, _lanes_of(hh)].astype(BF16) for hh in hs]
        ck = [_col(cb_ref[:, _lanes_of(hh)]) for hh in hs]

        def step(i, carry, diag):
            dk, dv, dc = (list(carry[t::3]) for t in range(3))
            off = pl.multiple_of(i * tq, tq)
            q = [q_ref[pl.ds(off, tq), _lanes_of(hh)] for hh in hs]
            do32 = [do_ref[pl.ds(off, tq), _lanes_of(hh)] for hh in hs]
            do16 = [d.astype(BF16) for d in do32]
            lse = [_row(lse_ref[pl.ds(off, tq), _lanes_of(hh)]) for hh in hs]
            delta = [_row(jnp.broadcast_to(jnp.sum(d * o_ref[pl.ds(off, tq), _lanes_of(hh)], axis=1, keepdims=True), (tq, LANES)))
                     for hh, d in zip(hs, do32)]
            p = _each(lambda kk, qq, cc, ll: jnp.exp(_dot(kk, qq, "nt") * FOX_SCALE - cc - ll), k, q, ck, lse)
            if diag:
                p = _each(lambda x: jnp.where(_causal(x.shape, 1), x, 0.0), p)
            dv = _each(lambda a, pp, d: a + _dot(pp.astype(BF16), d, "nn"), dv, p, do16)
            ds = _each(lambda pp, vv, d, dl: pp * (_dot(vv, d, "nt") - dl), p, v16, do16, delta)
            dk = _each(lambda a, x, qq: a + _dot(x.astype(BF16), qq, "nn"), dk, ds, q)
            dc = _each(lambda a, x: a + jnp.sum(x, axis=1, keepdims=True), dc, ds)
            return tuple(x for head in zip(dk, dv, dc) for x in head)

        zero = jnp.zeros((tk, LANES), F32)
        carry = step(j, (zero, zero, jnp.zeros((tk, 1), F32)) * FOX_HP, True)
        res = lax.fori_loop(j + 1, nq, lambda i, c: step(i, c, False), carry)
        for hh in hs:
            dk, dv, dc = res[3 * hh:3 * hh + 3]
            dk_ref[:, _lanes_of(hh)] = dk * FOX_SCALE
            dv_ref[:, _lanes_of(hh)] = dv.astype(BF16)
            dc_ref[:, _lanes_of(hh)] = jnp.where(_iota((tk, LANES), 1) == 0, -dc, 0.0)

    wide = (rows, NH * LANES)
    return pl.pallas_call(
        body, name="fox_dkv", grid=(nseq, NH // FOX_HP, nq), in_specs=[full, kblk, vblk, kblk, full, full, full] + [ANY] * len(after),
        out_specs=[kblk, kblk, kblk],
        out_shape=[jax.ShapeDtypeStruct(wide, F32), jax.ShapeDtypeStruct(wide, BF16), jax.ShapeDtypeStruct(wide, F32)],
        compiler_params=_params(3))(qn, kn, proj, cb, do, lse, o, *after)


def _adamw_update(w, g, m, v):
    m_new = ADAM_B1 * m + (1.0 - ADAM_B1) * g
    v_new = ADAM_B2 * v + (1.0 - ADAM_B2) * (g * g)
    m_hat = m_new / (1.0 - ADAM_B1 ** ADAM_STEP)
    v_hat = v_new / (1.0 - ADAM_B2 ** ADAM_STEP)
    return -ADAM_LR * (m_hat / (jnp.sqrt(v_hat) + ADAM_EPS) + ADAM_WD * w), m_new, v_new


def adamw(name, w, g, m, v):
    rows, cols = w.shape
    tb = min(rows, 128)
    assert rows % tb == 0
    blk = pl.BlockSpec((tb, cols), lambda i: (i, 0))

    def body(w_ref, g_ref, m_ref, v_ref, d_ref, mo_ref, vo_ref):
        d_ref[...], mo_ref[...], vo_ref[...] = _adamw_update(w_ref[...], g_ref[...], m_ref[...], v_ref[...])

    shp = jax.ShapeDtypeStruct(w.shape, F32)
    return pl.pallas_call(body, name=name, grid=(rows // tb,), in_specs=[blk] * 4, out_specs=[blk] * 3,
                          out_shape=[shp] * 3, compiler_params=_params(1))(w, g, m, v)


SPLIT_TILE = 128


def _tiled(shape2d, ax, n_lead, index):
    blk = (SPLIT_TILE, shape2d[1]) if ax == 0 else (shape2d[0], SPLIT_TILE)

    def index_map(*args):
        *lead, t = index(*args)
        return (*lead, t, 0) if ax == 0 else (*lead, 0, t)

    return pl.BlockSpec((None,) * n_lead + blk, index_map)


def adamw_halves(name, w, mine, other, m, v, c, ax):
    steps = w.shape[ax] // 2 // SPLIT_TILE
    assert w.shape[ax] == 2 * steps * SPLIT_TILE

    def body(c_ref, w_ref, mine_ref, other_ref, m_ref, v_ref, g_ref, d_ref, mo_ref, vo_ref):
        g = jnp.where(pl.program_id(0) // steps == c_ref[0], mine_ref[...], other_ref[...])
        g_ref[...] = g
        d_ref[...], mo_ref[...], vo_ref[...] = _adamw_update(w_ref[...], g, m_ref[...], v_ref[...])

    blk = _tiled(w.shape, ax, 0, lambda i, c_ref: (i,))
    hblk = _tiled(mine.shape, ax, 0, lambda i, c_ref: (i % steps,))
    grid_spec = pltpu.PrefetchScalarGridSpec(num_scalar_prefetch=1, grid=(2 * steps,),
                                             in_specs=[blk, hblk, hblk, blk, blk], out_specs=[blk] * 4)
    shp = jax.ShapeDtypeStruct(w.shape, F32)
    return pl.pallas_call(body, name=name, grid_spec=grid_spec, out_shape=[shp] * 4,
                          compiler_params=_params(1))(c, w, mine, other, m, v)


def add_chips(name, slots, parts, chip, axes):
    outs = []
    for idx, (x, own, ax) in enumerate(zip(slots, parts, axes)):
        n, shape2d = x.shape[0], x.shape[1:]
        steps = shape2d[ax] // SPLIT_TILE
        assert shape2d[ax] == steps * SPLIT_TILE

        def body(me_ref, *refs, n=n):
            o_ref = refs[n + 1]
            acc = None
            for t in range(n):
                term = jnp.where(me_ref[0] == t, refs[n][...], refs[t][...]).astype(F32)
                acc = term if acc is None else acc + term
            o_ref[...] = acc

        def filled(t, n=n):
            return lambda i, me_ref: (jnp.where(me_ref[0] == t, (t + 1) % n, t), i)

        grid_spec = pltpu.PrefetchScalarGridSpec(
            num_scalar_prefetch=1, grid=(steps,),
            in_specs=[_tiled(shape2d, ax, 1, filled(t)) for t in range(n)]
            + [_tiled(shape2d, ax, 1, lambda i, me_ref: (me_ref[0], i))],
            out_specs=_tiled(shape2d, ax, 0, lambda i, me_ref: (i,)))
        outs.append(pl.pallas_call(
            body, name=f"{name}_{idx}", grid_spec=grid_spec, out_shape=jax.ShapeDtypeStruct(shape2d, F32),
            compiler_params=_params(1))(chip, *([x] * n), own))
    return outs


def add_pair(name, gs, rs, c, axes):
    outs = []
    for idx, (g, r, ax) in enumerate(zip(gs, rs, axes)):
        nb = r.shape[0]
        steps = r.shape[1 + ax] // SPLIT_TILE
        assert r.shape[1 + ax] == steps * SPLIT_TILE

        def body(c_ref, g_ref, r_ref, o_ref):
            o_ref[...] = (g_ref[...] + r_ref[...]).astype(BF16)

        grid_spec = pltpu.PrefetchScalarGridSpec(
            num_scalar_prefetch=1, grid=(nb, steps),
            in_specs=[_tiled(g.shape[1:], ax, 1, lambda b, i, c_ref: (b, c_ref[0] * steps + i)),
                      _tiled(r.shape[1:], ax, 1, lambda b, i, c_ref: (b, i))],
            out_specs=_tiled(r.shape[1:], ax, 1, lambda b, i, c_ref: (b, i)))
        outs.append(pl.pallas_call(
            body, name=f"{name}_{idx}", grid_spec=grid_spec, out_shape=jax.ShapeDtypeStruct(r.shape, BF16),
            compiler_params=_params(2))(c, g, r))
    return outs


def _place():
    x, y, c = lax.axis_index("x"), lax.axis_index("y"), lax.axis_index("c")
    return x, y, c, [(1 - x, y), (x, 1 - y), (1 - x, 1 - y)]


def _remote(src, dst, send_sem, recv_sem, dev):
    return pltpu.make_async_remote_copy(src_ref=src, dst_ref=dst, send_sem=send_sem, recv_sem=recv_sem,
                                        device_id=dev, device_id_type=MESH)


def _half(ref, lead, ax, which):
    size = ref.shape[len(lead) + ax] // 2
    part = pl.ds(which * size, size)
    return ref.at[(*lead, part, slice(None)) if ax == 0 else (*lead, slice(None), part)]


def gather_ring(shard):
    rows, cols = shard.shape
    half = cols // 2
    top = rows // 2 // 16 * 16
    assert shard.dtype == BF16 and half % LANES == 0

    def body(in_ref, out_ref, ici_s, ici_r, d2d_s, d2d_r):
        x, y, c, _ = _place()
        me, xn, yn, dg = 2 * x + y, 2 * (1 - x) + y, 2 * x + (1 - y), 2 * (1 - x) + (1 - y)
        to_x, to_y, sib = (1 - x, y, c), (x, 1 - y, c), (x, y, 1 - c)
        mine, other = pl.ds(c * half, half), pl.ds((1 - c) * half, half)
        upper, lower = pl.ds(0, top), pl.ds(top, rows - top)
        started = []

        def send(src, dst, sems, k, dev):
            cp = _remote(src, dst, sems[0].at[k], sems[1].at[k], dev)
            cp.start()
            started.append(cp)

        def arrive(dst, sems, k):
            _remote(dst, dst, sems[0].at[k], sems[1].at[k], sib).wait_recv()

        ici, d2d = (ici_s, ici_r), (d2d_s, d2d_r)
        send(in_ref, out_ref.at[me], d2d, 0, sib)
        send(in_ref.at[:, mine], out_ref.at[me, :, mine], ici, 0, to_x)
        send(in_ref.at[:, mine], out_ref.at[me, :, mine], ici, 1, to_y)
        arrive(out_ref.at[xn, :, mine], ici, 0)
        send(out_ref.at[xn, upper, mine], out_ref.at[xn, upper, mine], ici, 2, to_y)
        send(out_ref.at[xn, :, mine], out_ref.at[xn, :, mine], d2d, 1, sib)
        arrive(out_ref.at[yn, :, mine], ici, 1)
        send(out_ref.at[yn, lower, mine], out_ref.at[yn, lower, mine], ici, 3, to_x)
        send(out_ref.at[yn, :, mine], out_ref.at[yn, :, mine], d2d, 2, sib)
        arrive(out_ref.at[dg, upper, mine], ici, 2)
        send(out_ref.at[dg, upper, mine], out_ref.at[dg, upper, mine], d2d, 3, sib)
        arrive(out_ref.at[dg, lower, mine], ici, 3)
        send(out_ref.at[dg, lower, mine], out_ref.at[dg, lower, mine], d2d, 4, sib)
        arrive(out_ref.at[me], d2d, 0)
        arrive(out_ref.at[xn, :, other], d2d, 1)
        arrive(out_ref.at[yn, :, other], d2d, 2)
        arrive(out_ref.at[dg, upper, other], d2d, 3)
        arrive(out_ref.at[dg, lower, other], d2d, 4)
        for cp in started:
            cp.wait_send()

    return pl.pallas_call(
        body, name="gather_ring", in_specs=[ANY], out_specs=ANY, out_shape=jax.ShapeDtypeStruct((4,) + shard.shape, shard.dtype),
        scratch_shapes=[pltpu.SemaphoreType.DMA((4,))] * 2 + [pltpu.SemaphoreType.DMA((5,))] * 2,
    )(shard)


HBM = pl.BlockSpec(memory_space=pltpu.HBM)
SEM = pl.BlockSpec(memory_space=pltpu.SEMAPHORE)
DATAFLOW = pltpu.SideEffectType.DATAFLOW_SIDE_EFFECTING


def _hbm(a):
    return pltpu.with_memory_space_constraint(a, pltpu.HBM)


class SplitExchange:
    def __init__(self, name, srcs, zone_shapes, n_sems, plan):
        self.name, self.n, self.n_sems, self.plan = name, len(srcs), n_sems, plan
        self.srcs = [_hbm(s) for s in srcs]
        self.zones = [_hbm(lax.empty(shape, s.dtype)) for shape, s in zip(zone_shapes, srcs)]

    def start(self, after):
        n, n_after = self.n, len(after)

        def body(*refs):
            ins, lands = refs[:n], refs[n:2 * n]
            send, recv, token = refs[2 * n + n_after], refs[2 * n + n_after + 1], refs[-1]
            for src, dst, si, ri, dev in self.plan(ins, lands)[0]:
                _remote(src, dst, send.at[si], recv.at[ri], dev).start()
            token[...] = jnp.zeros_like(token)

        res = pl.pallas_call(
            body, name=f"{self.name}_start", in_specs=[HBM] * (2 * n) + [ANY] * n_after,
            out_specs=[SEM, SEM] + [HBM] * (2 * n) + [pl.BlockSpec(memory_space=pltpu.VMEM)],
            out_shape=[pltpu.SemaphoreType.DMA((self.n_sems,)), pltpu.SemaphoreType.DMA((self.n_sems,))]
            + [pltpu.HBM(a.shape, a.dtype) for a in self.srcs + self.zones] + [jax.ShapeDtypeStruct((8, LANES), F32)],
            input_output_aliases={i: 2 + i for i in range(2 * n)},
            compiler_params=pltpu.CompilerParams(has_side_effects=DATAFLOW),
        )(*self.srcs, *self.zones, *after)
        self.sems, self.srcs, self.zones = res[:2], list(res[2:2 + n]), list(res[2 + n:2 + 2 * n])
        return res[-1]

    def wait(self, after):
        n = self.n

        def body(*refs):
            ins, lands = refs[:n], refs[n:2 * n]
            send, recv = refs[2 * n], refs[2 * n + 1]
            sends, arrivals = self.plan(ins, lands)
            for src, _, si, _, dev in sends:
                _remote(src, src, send.at[si], recv.at[si], dev).wait_send()
            for landed, ri in arrivals:
                _remote(landed, landed, send.at[ri], recv.at[ri], _place()[:3]).wait_recv()

        res = pl.pallas_call(
            body, name=f"{self.name}_wait", in_specs=[HBM] * (2 * n) + [SEM, SEM, ANY], out_specs=[HBM] * (2 * n),
            out_shape=[pltpu.HBM(a.shape, a.dtype) for a in self.srcs + self.zones],
            input_output_aliases={i: i for i in range(2 * n)},
            compiler_params=pltpu.CompilerParams(has_side_effects=DATAFLOW),
        )(*self.srcs, *self.zones, *self.sems, after)
        self.srcs = list(res[:n])
        return list(res[n:])


def split_gather(shards):
    n = len(shards)

    def plan(ins, lands):
        x, y, c, chips = _place()
        me = 2 * x + y
        sends, arrivals = [], []
        for w in range(n):
            for j, (ox, oy) in enumerate(chips):
                for k in range(2):
                    base = 2 * (3 * w + j)
                    sends.append((_half(ins[w], (), 0, c), _half(lands[w], (me,), 0, c), base + k, base + c, (ox, oy, k)))
                    arrivals.append((_half(lands[w], (2 * ox + oy,), 0, k), base + k))
            sends.append((ins[w], lands[w].at[me], 6 * n + w, 6 * n + w, (x, y, 1 - c)))
            arrivals.append((lands[w].at[me], 6 * n + w))
        return sends, arrivals

    return SplitExchange("gather", shards, [(4,) + s.shape for s in shards], 7 * n, plan)


def split_pair_swap(name, grads, axes):
    def plan(ins, lands):
        x, y, c, _ = _place()
        sends = [(_half(ins[w], (slice(None),), axes[w], 1 - c), lands[w], w, w, (x, y, 1 - c)) for w in range(len(ins))]
        return sends, [(lands[w], w) for w in range(len(ins))]

    halved = [tuple(d // 2 if i == 1 + ax else d for i, d in enumerate(g.shape)) for g, ax in zip(grads, axes)]
    return SplitExchange(name, grads, halved, len(grads), plan)


def split_chip_exchange(name, parts):
    def plan(ins, lands):
        x, y, c, chips = _place()
        sends, arrivals = [], []
        for w in range(len(ins)):
            for j, (ox, oy) in enumerate(chips):
                sends.append((ins[w].at[2 * ox + oy], lands[w].at[2 * x + y], 3 * w + j, 3 * w + j, (ox, oy, c)))
                arrivals.append((lands[w].at[2 * ox + oy], 3 * w + j))
        return sends, arrivals

    return SplitExchange(name, parts, [p.shape for p in parts], 3 * len(parts), plan)


def split_pair_send(halves):
    def plan(ins, lands):
        x, y, c, _ = _place()
        return ([(ins[w], lands[w], w, w, (x, y, 1 - c)) for w in range(len(ins))],
                [(lands[w], w) for w in range(len(ins))])

    return SplitExchange("pair_send", halves, [h.shape for h in halves], len(halves), plan)


def pair_send(halves):
    n = len(halves)

    def body(*refs):
        ins, outs = refs[:n], refs[n:2 * n]
        send, recv = refs[2 * n:]
        x, y, c, _ = _place()
        cps = [_remote(ins[w], outs[w], send.at[w], recv.at[w], (x, y, 1 - c)) for w in range(n)]
        for cp in cps:
            cp.start()
        for cp in cps:
            cp.wait_recv()
        for cp in cps:
            cp.wait_send()

    return pl.pallas_call(
        body, name="pair_send", in_specs=[ANY] * n, out_specs=[ANY] * n,
        out_shape=[jax.ShapeDtypeStruct(h.shape, h.dtype) for h in halves],
        scratch_shapes=[pltpu.SemaphoreType.DMA((n,))] * 2,
    )(*halves)


def all_reduce_small(name, vec, after=()):
    rows = vec.shape[0]

    def body(v_ref, *refs):
        o_ref, buf, send, recv = refs[len(after):]
        x, y, c, _ = _place()
        me = 4 * x + 2 * y + c
        buf[me] = v_ref[...]
        cps = []
        for k in range(1, 8):
            kx, ky, kc = (k >> 2) & 1, (k >> 1) & 1, k & 1
            peer = (x if kx == 0 else 1 - x, y if ky == 0 else 1 - y, c if kc == 0 else 1 - c)
            cp = _remote(v_ref, buf.at[me], send.at[k - 1], recv.at[k - 1], peer)
            cp.start()
            cps.append(cp)
        for k in range(1, 8):
            kx, ky, kc = (k >> 2) & 1, (k >> 1) & 1, k & 1
            px, py, pc = (x if kx == 0 else 1 - x, y if ky == 0 else 1 - y, c if kc == 0 else 1 - c)
            slot = buf.at[4 * px + 2 * py + pc]
            _remote(slot, slot, send.at[k - 1], recv.at[k - 1], (px, py, pc)).wait_recv()
        for cp in cps:
            cp.wait_send()
        acc = buf[0]
        for d in range(1, 8):
            acc = acc + buf[d]
        o_ref[...] = acc

    vm = pl.BlockSpec(memory_space=pltpu.VMEM)
    return pl.pallas_call(
        body, name=name, in_specs=[vm] + [ANY] * len(after), out_specs=vm, out_shape=jax.ShapeDtypeStruct(vec.shape, F32),
        scratch_shapes=[pltpu.VMEM((8, rows, LANES), F32), pltpu.SemaphoreType.DMA((7,)), pltpu.SemaphoreType.DMA((7,))],
    )(vec, *after)


class NoExchange:
    def __init__(self, late):
        self.late = late

    def late_weights(self, after):
        return self.late

    def reduce_start(self, grads):
        return jnp.zeros((8, LANES), F32)

    def reduce_exchange(self, after):
        return jnp.zeros((8, LANES), F32)

    def reduce_finish(self, after):
        return jnp.zeros((8, LANES), F32)

    def input_grad_start(self, dw_main, dw_small):
        return jnp.zeros((8, LANES), F32)

    def input_grad_exchange(self, after):
        return jnp.zeros((8, LANES), F32)


def local_step(x2, tgt2, g1, g2, gdn_ng, qn_g, kn_g, p1, p2, conv_w, wt_main, wt_small, hooks, nseq, seq):
    rows, dm = x2.shape
    wide = NH * LANES
    row = lambda a, off=0, w=None: (a, "row", off, a.shape[1] if w is None else w)
    rowh = lambda a, off=0, w=LANES: (a, "rowh", off, w)
    par = lambda a: (a, "par", 0, a.shape[1])
    parh = lambda a, off=0: (a, "parh", off, LANES)
    o_row = lambda w, dt: (w, "row", w, dt)
    o_rowh = lambda dt, tw=wide, w=LANES: (tw, "rowh", w, dt)

    u, = ew_fwd("rms1", f_rms, [row(x2), par(g1)], [o_row(dm, BF16)], rows)
    proj = matmul("mm_in", u, wt_main, "nt", BF16)
    sp = matmul("mm_in_small", u, wt_small, "nt", F32)
    so, = ew_fwd("small", f_small, [row(sp), par(p1), par(p2)], [o_row(LANES, F32)], rows)
    cs = cumsum_time("cumsum", so, nseq, seq, False)
    gb, bb, cb = ew_fwd("bcast", f_bcast, [row(so), row(cs)], [o_rowh(F32)] * 3, rows, NH)
    ct = transpose_time("c_time_major", cs, nseq, seq)
    conv = {}
    for mode, off in (("q", 0), ("k", NH), ("v", 2 * NH)):
        conv[mode], = ew_fwd(f"conv_{mode}", make_f_conv(mode), [rowh(proj, off), parh(conv_w, off)], [o_rowh(F32)],
                             rows, NH, seq, "hi", CONV_HEADS)
    val, kcum, attn, qdec, kdec, t_inv = gdn_a_fwd(conv["q"], conv["k"], conv["v"], gb, bb, rows)
    o_a, snaps = gdn_b_fwd(val, kcum, attn, qdec, kdec, gb, nseq, seq)
    ya_in, = ew_fwd("gdn_post", f_post, [rowh(o_a), rowh(proj, 3 * NH), par(gdn_ng)], [o_rowh(BF16)], rows, NH)
    fqn, = ew_fwd("fox_qn", f_rms, [rowh(proj, FOX_Q), par(qn_g)], [o_rowh(BF16)], rows, NH)
    fkn, = ew_fwd("fox_kn", f_rms, [rowh(proj, FOX_K), par(kn_g)], [o_rowh(BF16)], rows, NH)
    o_b, o_b16, lse = fox_fwd(fqn, fkn, proj, ct, nseq, seq)
    p_a, p_b, w_o, w_u, w_d = hooks.late_weights(o_a)
    y_a = matmul("mm_pa", ya_in, p_a, "nn", F32, tn=1024)
    y_b = matmul("mm_pb", o_b16, p_b, "nn", F32, tn=1024)
    gates = [row(proj, 7, dm), row(proj, 8, dm)]
    merged, = ew_fwd("merge", f_merge, gates + [row(y_a), row(y_b)], [o_row(dm, BF16)], rows)
    hres = matmul("mm_out", merged, w_o, "nn", F32, add=x2, tn=1024)
    hn, = ew_fwd("rms2", f_rms, [row(hres), par(g2)], [o_row(dm, BF16)], rows)
    up_blocks = w_u.shape[0]
    act, relu2 = matmul("mm_up", hn, w_u, "nn", F32, col_blocks=up_blocks, out_dtypes=[F32, BF16],
                        epilogue=lambda r: [r, jnp.maximum(r, 0.0) * jnp.maximum(r, 0.0)])
    def loss_tail(r, h_tile, t_tile):
        d = (r + h_tile) - t_tile
        e = (0.5 / dm) * (d * d)
        part = e.reshape(e.shape[0] // 8, 8, e.shape[1]).sum(axis=0)
        part = sum(part[:, t * LANES:(t + 1) * LANES] for t in range(e.shape[1] // LANES))
        g = d * (1.0 / dm)
        return [g, g, part]

    dout, dout16, loss_acc = matmul("mm_down", relu2, w_d, "nn", F32, extras=[hres, tgt2], epilogue=loss_tail,
                                    out_dtypes=[F32, BF16, F32], tile_sums=True)

    d_act = matmul("mm_d_act", dout16, w_d, "nt", BF16, extras=[act], epilogue=lambda r, a: [2.0 * jnp.maximum(a, 0.0) * r])
    dw_d = matmul("mm_dw_down", relu2, dout16, "tn", F32, tn=1024)
    dw_u = matmul("mm_dw_up", hn, d_act, "tn", F32, col_blocks=up_blocks)
    d_hn = matmul("mm_d_hn", d_act, w_u, "nt", F32, col_blocks=up_blocks)
    dh, dh16, dg2 = ew_bwd("rms2_b", f_rms, [row(hres), par(g2)], [(row(d_hn),)], [row(dout)],
                           lambda g, e: [g[0] + e[0], g[0] + e[0], g[1]],
                           [((rows, dm), "row", dm, F32, None), ((rows, dm), "row", dm, BF16, None), ((1, dm), "par", dm, F32, "all")], rows)
    d_merged = matmul("mm_d_merged", dh16, w_o, "nt", F32, tn=1024)
    dw_o = matmul("mm_dw_out", merged, dh16, "tn", F32, tn=1024)
    seg16 = ((rows, dm), "row", dm, BF16, None)
    d_ga16, d_gb16, d_ya16, d_yb16 = ew_bwd("merge_b", f_merge, gates + [row(y_a), row(y_b)], [(row(d_merged),)], [],
                                            lambda g, e: list(g), [seg16] * 4, rows)
    dp_a = matmul("mm_dp_a", ya_in, d_ya16, "tn", F32, tn=1024)
    d_ya_in = matmul("mm_d_ya_in", d_ya16, p_a, "nt", F32, tn=1024)
    dp_b = matmul("mm_dp_b", o_b16, d_yb16, "tn", F32, tn=1024)
    d_ob = matmul("mm_d_ob", d_yb16, p_b, "nt", F32, tn=1024)
    token = hooks.reduce_start(dict(p_a=dp_a, p_b=dp_b, w_o=dw_o, w_u=dw_u, w_d=dw_d))
    gdn_ng_t = gdn_ng + token[0, 0]
    h32 = ((rows, wide), "rowh", LANES, F32, None)
    h16 = ((rows, wide), "rowh", LANES, BF16, None)
    gain = ((1, LANES), "par", LANES, F32, "all")
    d_oa, d_z16, d_gdn_ng = ew_bwd("gdn_post_b", f_post, [rowh(o_a), rowh(proj, 3 * NH), par(gdn_ng_t)], [(rowh(d_ya_in),)], [],
                                   lambda g, e: list(g), [h32, h16, gain], rows, NH)
    dval, dkc, dat, dqd, dkd, dgb_b = gdn_b_bwd(val, kcum, attn, qdec, kdec, gb, snaps, d_oa, nseq, seq)
    d_cq, d_ck, d_cv, d_gb, d_bb = gdn_a_bwd(conv["q"], conv["k"], conv["v"], gb, bb, t_inv, dval, dkc, dat, dqd, dkd, dgb_b, rows)
    token = hooks.reduce_exchange(d_cq)
    conv_w_t = conv_w + token[0, 0]
    d_pre, d_conv = {}, {}
    tap = ((4, wide), "parh", LANES, F32, "inner")
    for mode, off, ctg in (("q", 0, d_cq), ("k", NH, d_ck), ("v", 2 * NH, d_cv)):
        d_pre[mode], d_conv[mode] = ew_bwd(f"conv_{mode}_b", make_f_conv(mode), [rowh(proj, off), parh(conv_w_t, off)],
                                           [(rowh(ctg),)], [], lambda g, e: list(g), [h16, tap], rows, NH, seq, "hi", CONV_HEADS)
    d_fqn, d_cq_b = fox_dq(fqn, fkn, proj, ct, d_ob, lse, o_b, [token], nseq, seq)
    d_fkn, d_fv16, d_ck_b = fox_dkv(fqn, fkn, proj, cb, d_ob, lse, o_b, [token], nseq, seq)
    token = hooks.reduce_finish(d_fkn)
    qn_g_t, kn_g_t = qn_g + token[0, 0], kn_g + token[0, 0]
    d_fq16, d_qn_g = ew_bwd("fox_qn_b", f_rms, [rowh(proj, FOX_Q), par(qn_g_t)], [(rowh(d_fqn),)], [], lambda g, e: list(g),
                            [h16, gain], rows, NH)
    d_fk16, d_kn_g = ew_bwd("fox_kn_b", f_rms, [rowh(proj, FOX_K), par(kn_g_t)], [(rowh(d_fkn),)], [], lambda g, e: list(g),
                            [h16, gain], rows, NH)
    narrow = ((rows, LANES), "row", LANES, F32, None)
    d_so, d_cs = ew_bwd("bcast_b", f_bcast, [row(so), row(cs)], [(rowh(d_gb),), (rowh(d_bb),), (rowh(d_cq_b), rowh(d_ck_b))], [],
                        lambda g, e: list(g), [narrow, narrow], rows, NH)
    d_logf = cumsum_time("cumsum_b", d_cs, nseq, seq, True)
    vec = ((1, LANES), "par", LANES, F32, "all")
    d_sp16, d_p1, d_p2 = ew_bwd("small_b", f_small, [row(sp), par(p1), par(p2)], [(row(d_so), row(d_logf))], [],
                                lambda g, e: list(g), [((rows, LANES), "row", LANES, BF16, None), vec, vec], rows)
    d_proj16 = jnp.concatenate([d_pre["q"], d_pre["k"], d_pre["v"], d_z16, d_fq16, d_fk16, d_fv16, d_ga16, d_gb16], axis=1)
    dw_main = matmul("mm_dw_main", d_proj16, u, "tn", F32)
    dw_small = matmul("mm_dw_small", d_sp16, u, "tn", F32)
    wt_small_t = wt_small + hooks.input_grad_start(dw_main, dw_small)[0, 0].astype(BF16)
    d_u = matmul("mm_d_u_small", d_sp16, wt_small_t, "nn", F32)
    d_u = matmul("mm_d_u_first", d_proj16, wt_main, "nn", F32, add=d_u, k_part=(0, 2))
    d_u = matmul("mm_d_u_second", d_proj16, wt_main, "nn", F32, add=d_u, k_part=(1, 2), after=[hooks.input_grad_exchange(d_u)])
    dx, dg1 = ew_bwd("rms1_b", f_rms, [row(x2), par(g1)], [(row(d_u),)], [row(dh)], lambda g, e: [g[0] + e[0], g[1]],
                     [((rows, dm), "row", dm, F32, None), ((1, dm), "par", dm, F32, "all")], rows)
    d_conv_w = jnp.concatenate([d_conv["q"], d_conv["k"], d_conv["v"]], axis=1)
    return dict(loss_acc=loss_acc, dx=dx, g1=dg1, g2=dg2, gdn_ng=d_gdn_ng, qn=d_qn_g, kn=d_kn_g, p1=d_p1, p2=d_p2,
                conv=d_conv_w, w_main=dw_main, w_small=dw_small, p_a=dp_a, p_b=dp_b, w_o=dw_o, w_u=dw_u, w_d=dw_d)


_W = NH * LANES
_A0, _A1 = 4 * _W, 4 * _W + 2 * NH
_B0, _B1 = _A1 + 3 * _W, _A1 + 3 * _W + NH


def _split_w_in(full_t):
    main = jnp.concatenate([full_t[:_A0], full_t[_A1:_B0], full_t[_B1:]], axis=0)
    small = jnp.concatenate([full_t[_A0:_A1], full_t[_B0:_B1], jnp.zeros((LANES - 3 * NH, full_t.shape[1]), full_t.dtype)], axis=0)
    return main, small


def _join_w_in(main, small):
    return jnp.concatenate([main[:_A0], small[:2 * NH], main[_A0:_A0 + 3 * _W], small[2 * NH:3 * NH], main[_A0 + 3 * _W:]], axis=0)


def _lanes(v, at=0):
    return jnp.pad(v.reshape(1, -1), ((0, 0), (at, LANES - at - v.size)))


def kernel(x, norm_mix_g, w_in, gdn_conv_w, gdn_a_log, gdn_dt_bias, gdn_norm_g, fox_q_norm_g, fox_k_norm_g, fox_f_bias, w_proj_gdn, w_proj_fox, w_out, norm_mlp_g, w_up, w_down, loss_target, m_norm_mix_g, m_w_in, m_gdn_conv_w, m_gdn_a_log, m_gdn_dt_bias, m_gdn_norm_g, m_fox_q_norm_g, m_fox_k_norm_g, m_fox_f_bias, m_w_proj_gdn, m_w_proj_fox, m_w_out, m_norm_mlp_g, m_w_up, m_w_down, v_norm_mix_g, v_w_in, v_gdn_conv_w, v_gdn_a_log, v_gdn_dt_bias, v_gdn_norm_g, v_fox_q_norm_g, v_fox_k_norm_g, v_fox_f_bias, v_w_proj_gdn, v_w_proj_fox, v_w_out, v_norm_mlp_g, v_w_up, v_w_down):
    nseq, seq, dm = x.shape
    rows = nseq * seq
    xi, yi, ci = lax.axis_index("x"), lax.axis_index("y"), lax.axis_index("c")
    chip = 2 * xi + yi
    conv_cols = gdn_conv_w.shape[2]

    tr = lambda a: jnp.swapaxes(a[0], 0, 1)
    big = [tr(w_in), w_proj_gdn[0], w_proj_fox[0], w_out[0], w_up[0], w_down[0]]
    axes = [1, 0, 0, 0, 0, 0]
    big16 = [w.astype(BF16) for w in big]
    conv_slot = jnp.zeros((4, 4, conv_cols), F32).at[:, chip].set(jnp.where(ci == 0, gdn_conv_w[0], 0.0))
    conv_full = all_reduce_small("gather_conv", conv_slot.reshape(-1, LANES)).reshape(4, 4 * conv_cols)
    got_in = gather_ring(big16[0])
    wt_main, wt_small = _split_w_in(got_in.reshape(-1, dm))
    core, chip_no = ci.reshape(1).astype(jnp.int32), chip.reshape(1).astype(jnp.int32)
    gather = split_gather(big16[1:])
    token = gather.start([got_in, conv_full])

    class Hooks:
        def late_weights(self, after):
            g_pa, g_pb, g_wo, w_u, g_wd = gather.wait(after)
            return (*(g.reshape(-1, dm) for g in (g_pa, g_pb, g_wo)), w_u, g_wd.reshape(-1, dm))

        def reduce_start(self, grads):
            blocks = [grads["p_a"].reshape(4, -1, dm), grads["p_b"].reshape(4, -1, dm), grads["w_o"].reshape(4, -1, dm),
                      grads["w_u"], grads["w_d"].reshape(4, -1, dm)]
            self.swap = split_pair_swap("pair_swap_late", blocks, axes[1:])
            return self.swap.start([])

        def reduce_exchange(self, after):
            swapped = self.swap.wait(after)
            self.exchange = split_chip_exchange("chip_exchange_late", add_pair("add_pair_late", self.swap.srcs, swapped, core, axes[1:]))
            return self.exchange.start([])

        def reduce_finish(self, after):
            slots = self.exchange.wait(after)
            self.send = split_pair_send(add_chips("add_chips_late", slots, self.exchange.srcs, chip_no, axes[1:]))
            return self.send.start([])

        def input_grad_start(self, dw_main, dw_small):
            self.in_swap = split_pair_swap("pair_swap_in", [_join_w_in(dw_main, dw_small).reshape(4, -1, dm)], axes[:1])
            return self.in_swap.start([])

        def input_grad_exchange(self, after):
            swapped = self.in_swap.wait(after)
            self.in_exchange = split_chip_exchange("chip_exchange_in", add_pair("add_pair_in", self.in_swap.srcs, swapped, core, axes[:1]))
            return self.in_exchange.start([])

    hooks = Hooks()
    p1 = _lanes(gdn_dt_bias[0]) + _lanes(fox_f_bias[0], 2 * NH)
    p2 = _lanes(gdn_a_log[0])

    g = local_step(x.reshape(rows, dm), loss_target.reshape(rows, dm), norm_mix_g + token[0, 0], norm_mlp_g, gdn_norm_g,
                   fox_q_norm_g, fox_k_norm_g, p1, p2, conv_full, wt_main, wt_small, hooks, nseq, seq)

    others = hooks.send.wait(g["dx"])
    big_m = [tr(m_w_in), m_w_proj_gdn[0], m_w_proj_fox[0], m_w_out[0], m_w_up[0], m_w_down[0]]
    big_v = [tr(v_w_in), v_w_proj_gdn[0], v_w_proj_fox[0], v_w_out[0], v_w_up[0], v_w_down[0]]
    names = ["w_in", "w_proj_gdn", "w_proj_fox", "w_out", "w_up", "w_down"]
    big_res, big_grad = {}, {}
    for i in range(1, len(names)):
        big_grad[names[i]], *big_res[names[i]] = adamw_halves(f"adamw_{names[i]}", big[i], hooks.send.srcs[i - 1], others[i - 1],
                                                              big_m[i], big_v[i], core, axes[i])
    slots = hooks.in_exchange.wait(big_res[names[-1]][0])
    mine = add_chips("add_chips_in", slots, hooks.in_exchange.srcs, chip_no, axes[:1])
    res = adamw_halves("adamw_w_in", big[0], mine[0], pair_send(mine)[0], big_m[0], big_v[0], core, axes[0])
    big_grad["w_in"], *big_res["w_in"] = [jnp.swapaxes(r, 0, 1) for r in res]

    small_parts = [g["loss_acc"], g["g1"].reshape(8, LANES), g["g2"].reshape(8, LANES), g["gdn_ng"], g["qn"], g["kn"], g["p1"], g["p2"],
                   g["conv"].reshape(-1, LANES)]
    tiled = [jnp.pad(p, ((0, -p.shape[0] % 8), (0, 0))) for p in small_parts]
    red = all_reduce_small("reduce_small", jnp.concatenate(tiled, axis=0), slots)
    pos, red_parts = 0, []
    for p, t in zip(small_parts, tiled):
        red_parts.append(red[pos:pos + p.shape[0]])
        pos += t.shape[0]
    r_loss, r_g1, r_g2, r_gdn_ng, r_qn, r_kn, r_p1, r_p2, r_conv = red_parts
    loss = jnp.sum(r_loss)
    g_conv = lax.dynamic_slice_in_dim(r_conv.reshape(4, 4, conv_cols), chip, 1, axis=1).reshape(4, conv_cols)
    small_grads = [r_g1.reshape(1, dm), r_p2[:, :NH], r_p1[:, :NH], r_gdn_ng, r_qn, r_kn, r_p1[:, 2 * NH:3 * NH], r_g2.reshape(1, dm)]
    small_w = [norm_mix_g, gdn_a_log, gdn_dt_bias, gdn_norm_g, fox_q_norm_g, fox_k_norm_g, fox_f_bias, norm_mlp_g]
    small_m = [m_norm_mix_g, m_gdn_a_log, m_gdn_dt_bias, m_gdn_norm_g, m_fox_q_norm_g, m_fox_k_norm_g, m_fox_f_bias, m_norm_mlp_g]
    small_v = [v_norm_mix_g, v_gdn_a_log, v_gdn_dt_bias, v_gdn_norm_g, v_fox_q_norm_g, v_fox_k_norm_g, v_fox_f_bias, v_norm_mlp_g]

    def pack(parts):
        flat = jnp.concatenate([jnp.pad(p.reshape(-1), (0, -p.size % LANES)) for p in parts])
        return jnp.pad(flat, (0, -flat.size % (8 * LANES))).reshape(-1, LANES)

    packed = adamw("adamw_small", pack(small_w + [gdn_conv_w[0]]), pack(small_grads + [g_conv]),
                   pack(small_m + [m_gdn_conv_w[0]]), pack(small_v + [v_gdn_conv_w[0]]))

    def unpack(flat2d):
        flat, pos, res = flat2d.reshape(-1), 0, []
        for p in small_w + [gdn_conv_w[0]]:
            res.append(flat[pos:pos + p.size].reshape(p.shape))
            pos += p.size + (-p.size % LANES)
        return res

    s_delta, s_m, s_v = (unpack(a) for a in packed)

    order = ["norm_mix_g", "w_in", "gdn_conv_w", "gdn_a_log", "gdn_dt_bias", "gdn_norm_g", "fox_q_norm_g", "fox_k_norm_g",
             "fox_f_bias", "w_proj_gdn", "w_proj_fox", "w_out", "norm_mlp_g", "w_up", "w_down"]
    small_names = ["norm_mix_g", "gdn_a_log", "gdn_dt_bias", "gdn_norm_g", "fox_q_norm_g", "fox_k_norm_g", "fox_f_bias", "norm_mlp_g",
                   "gdn_conv_w"]
    small_idx = {nm: i for i, nm in enumerate(small_names)}
    shapes = dict(zip(order, (a.shape for a in (norm_mix_g, w_in, gdn_conv_w, gdn_a_log, gdn_dt_bias, gdn_norm_g, fox_q_norm_g,
                                                 fox_k_norm_g, fox_f_bias, w_proj_gdn, w_proj_fox, w_out, norm_mlp_g, w_up, w_down))))
    grads_out, delta_out, m_out, v_out = [], [], [], []
    for nm in order:
        if nm in big_res:
            d, mm, vv = big_res[nm]
            gr = big_grad[nm]
        else:
            i = small_idx[nm]
            gr = (small_grads + [g_conv])[i]
            d, mm, vv = s_delta[i], s_m[i], s_v[i]
        for lst, val in ((grads_out, gr), (delta_out, d), (m_out, mm), (v_out, vv)):
            lst.append(val.reshape(shapes[nm]))
    return (loss, g["dx"].reshape(x.shape), *grads_out, *delta_out, *m_out, *v_out)
```

```python
import functools

import jax
import jax.numpy as jnp
from jax import lax
from jax.experimental import pallas as pl
from jax.experimental.pallas import tpu as pltpu

F32 = jnp.float32
BF16 = jnp.bfloat16
LANES = 128
NH = 8
EPS = 1e-6
GDN_CHUNK = 64
GDN_ROWS = 256
GDN_BASE = 16
ROW_TILE = 512
CONV_HEADS = 2
ATT_TILE = 512
NEG = -1e30
VMEM_LIMIT_BYTES = 58 * 1024 * 1024
LO = lax.Precision.DEFAULT
MESH = pl.DeviceIdType.MESH
ANY = pl.BlockSpec(memory_space=pl.ANY)

ADAM_LR, ADAM_B1, ADAM_B2, ADAM_EPS, ADAM_WD, ADAM_STEP = 0.001, 0.9, 0.999, 1e-08, 0.01, 10


def _params(n_grid):
    return pltpu.CompilerParams(dimension_semantics=("arbitrary",) * n_grid,
                                vmem_limit_bytes=VMEM_LIMIT_BYTES)


def _dot(a, b, dims, precision=None):
    dn = {"nn": (((1,), (0,)), ((), ())), "nt": (((1,), (1,)), ((), ())), "tn": (((0,), (0,)), ((), ()))}[dims]
    return lax.dot_general(a, b, dn, precision=precision, preferred_element_type=F32)


def _iota(shape, dim):
    return lax.broadcasted_iota(jnp.int32, shape, dim)


def _split(x, parts):
    out = []
    for _ in range(parts - 1):
        hi = x.astype(BF16)
        out.append(hi)
        x = x - hi.astype(F32)
    return out + [x.astype(BF16)]


def _dot_mask(mask, b, dims, terms=3):
    m16 = mask.astype(BF16)
    acc = None
    for part in reversed(_split(b, terms)):
        prod = _dot(m16, part, dims)
        acc = prod if acc is None else acc + prod
    return acc


@jax.custom_vjp
def mm_mask(mask, b):
    return _dot_mask(mask, b, "nn", 2)


mm_mask.defvjp(lambda mask, b: (_dot_mask(mask, b, "nn", 2), mask),
               lambda mask, g: (jnp.zeros_like(mask), _dot_mask(mask, g, "tn", 2)))


def matmul(name, a, b, dims, out_dtype, add=None, tm=1024, tn=1024, tk=512, col_blocks=None,
           extras=(), epilogue=None, out_dtypes=None, k_part=None, after=(), tile_sums=False):
    if col_blocks and dims != "tn":
        nb, b_rows, bw = b.shape
        b_shape = (b_rows, nb * bw)
    else:
        b_shape = b.shape
    if dims == "nn":
        (m, k), (_, n) = a.shape, b_shape
    elif dims == "nt":
        (m, k), (n, _) = a.shape, b_shape
    else:
        (k, m), (_, n) = a.shape, b_shape
    k_span = k // (k_part[1] if k_part else 1)
    if col_blocks and dims == "nt":
        k_span = min(k_span, bw)
    tk = k if k <= 1024 else max(t for t in (2048, 1536, 1024, 512, tk) if k_span % t == 0)
    tm, tn, tk = min(tm, m), min(tn, n), min(tk, k)
    assert m % tm == 0 and n % tn == 0 and k % tk == 0, (name, m, n, k)
    k0, nk = (0, k // tk) if k_part is None else (k_part[0] * (k // tk // k_part[1]), k // tk // k_part[1])
    assert k_part is None or (dims == "nn" and not col_blocks and (k // tk) % k_part[1] == 0)
    a_spec = pl.BlockSpec((tk, tm), lambda i, j, kk: (kk, i)) if dims == "tn" else pl.BlockSpec((tm, tk), lambda i, j, kk: (i, kk + k0))
    b_spec = pl.BlockSpec((tn, tk), lambda i, j, kk: (j, kk)) if dims == "nt" else pl.BlockSpec((tk, tn), lambda i, j, kk: (kk + k0, j))
    o_spec = pl.BlockSpec((tm, tn), lambda i, j, kk: (i, j))
    out_shape = (m, n)
    if col_blocks and dims == "nn":
        per = bw // tn
        assert bw % tn == 0
        b_spec = pl.BlockSpec((None, tk, tn), lambda i, j, kk: (j // per, kk, j % per))
    elif col_blocks and dims == "nt":
        per = bw // tk
        assert bw % tk == 0
        b_spec = pl.BlockSpec((None, tn, tk), lambda i, j, kk: (kk // per, j, kk % per))
    elif col_blocks:
        bw = n // col_blocks
        per = bw // tn
        assert bw % tn == 0 and add is None
        o_spec = pl.BlockSpec((None, tm, tn), lambda i, j, kk: (j // per, i, j % per))
        out_shape = (col_blocks, m, bw)
    extras = list(extras) + ([add] if add is not None else [])
    if add is not None:
        assert epilogue is None
        epilogue = lambda r, *e: [r + e[-1]]
    out_dtypes = [out_dtype] if epilogue is None or out_dtypes is None else list(out_dtypes)
    n_ex, n_out = len(extras), len(out_dtypes)

    def body(*refs):
        a_ref, b_ref = refs[0], refs[1]
        ex_refs, o_refs = refs[2:2 + n_ex], refs[2 + n_ex + len(after):2 + n_ex + len(after) + n_out]

        def finish(r):
            res = [r] if epilogue is None else epilogue(r, *[e[...] for e in ex_refs])
            for o_ref, v in zip(o_refs, res):
                o_ref[...] = v.astype(o_ref.dtype)

        if nk == 1:
            finish(_dot(a_ref[...], b_ref[...], dims))
            return
        acc_ref = refs[-1]
        kk = pl.program_id(2)

        @pl.when(kk == 0)
        def _():
            acc_ref[...] = jnp.zeros_like(acc_ref)

        acc_ref[...] += _dot(a_ref[...], b_ref[...], dims)

        @pl.when(kk == nk - 1)
        def _():
            finish(acc_ref[...])

    out_specs = [o_spec] * n_out
    out_shapes = [jax.ShapeDtypeStruct(out_shape, dt) for dt in out_dtypes]
    if tile_sums:
        out_specs[-1] = pl.BlockSpec((8, LANES), lambda i, j, kk: (i, j))
        out_shapes[-1] = jax.ShapeDtypeStruct((8 * (m // tm), LANES * (n // tn)), out_dtypes[-1])
    res = pl.pallas_call(
        body, name=name, grid=(m // tm, n // tn, nk), in_specs=[a_spec, b_spec] + [o_spec] * n_ex + [ANY] * len(after),
        out_specs=out_specs, out_shape=out_shapes,
        scratch_shapes=[pltpu.VMEM((tm, tn), F32)] if nk > 1 else [], compiler_params=_params(3),
    )(a, b, *extras, *after)
    return res[0] if n_out == 1 else res


def _ew_spec(kind, off, width, tb, hp, order, shape=None):
    def ih(g0, g1):
        return (g0, g1) if order == "ih" else (g1, g0)

    assert off % hp == 0 or kind in ("row", "par")
    if kind == "row":
        return pl.BlockSpec((tb, width), lambda g0, g1: (ih(g0, g1)[0], off))
    if kind == "rowh":
        return pl.BlockSpec((tb, hp * width), lambda g0, g1: (ih(g0, g1)[0], ih(g0, g1)[1] + off // hp))
    if kind == "par":
        return pl.BlockSpec(shape, lambda g0, g1: (0, 0))
    if kind == "parh":
        return pl.BlockSpec((shape[0], hp * width), lambda g0, g1: (0, ih(g0, g1)[1] + off // hp))
    raise ValueError(kind)


def _ew_grid(rows, tb, nh, hp, order):
    assert nh % hp == 0 and rows % tb == 0
    return (rows // tb, nh // hp) if order == "ih" else (nh // hp, rows // tb)


def _ew_load(ref, kind, width, hh):
    if kind in ("row", "par"):
        return ref[...].astype(F32)
    return ref[:, hh * width:(hh + 1) * width].astype(F32)


def ew_fwd(name, f, ins, outs, rows, nh=1, tb=ROW_TILE, order="ih", hp=None, after=()):
    hp = nh if hp is None else hp
    n_in = len(ins)

    def body(*refs):
        hb = pl.program_id(1) if order == "ih" else pl.program_id(0)
        for hh in range(hp):
            h = hh if hp == nh else hb * hp + hh
            vals = [_ew_load(r, kd, w, hh) for r, (_, kd, _, w) in zip(refs[:n_in], ins)]
            res = f(h, *vals)
            for r, v, (_, kd, w, _) in zip(refs[n_in + len(after):], res, outs):
                if kd == "row":
                    assert hp == 1
                    r[...] = v.astype(r.dtype)
                else:
                    r[:, hh * w:(hh + 1) * w] = v.astype(r.dtype)

    in_specs = [_ew_spec(kd, off, w, tb, hp, order, a.shape) for (a, kd, off, w) in ins]
    out_specs = [_ew_spec(kd, 0, w, tb, hp, order) for (_, kd, w, _) in outs]
    out_shape = [jax.ShapeDtypeStruct((rows, tw), dt) for (tw, _, _, dt) in outs]
    return pl.pallas_call(
        body, name=name, grid=_ew_grid(rows, tb, nh, hp, order), in_specs=in_specs + [ANY] * len(after), out_specs=out_specs,
        out_shape=out_shape, compiler_params=_params(2),
    )(*[a for (a, _, _, _) in ins], *after)


def ew_bwd(name, f, ins, cts, extras, emit, outs, rows, nh=1, tb=ROW_TILE, order="ih", hp=None):
    hp = nh if hp is None else hp
    n_in = len(ins)
    flat_cts = [d for group in cts for d in group]
    n_ct, n_ex = len(flat_cts), len(extras)

    def body(*refs):
        g0, g1 = pl.program_id(0), pl.program_id(1)
        hb = g1 if order == "ih" else g0
        out_refs = refs[n_in + n_ct + n_ex:]
        shared = [None] * len(outs)

        def store(r, v, first, sl=None):
            def put(val, add):
                if sl is None:
                    r[...] = (r[...] + val if add else val).astype(r.dtype)
                else:
                    r[:, sl] = (r[:, sl] + val if add else val).astype(r.dtype)

            if first is None:
                put(v, False)
            else:
                pl.when(first)(lambda: put(v, False))
                pl.when(jnp.logical_not(first))(lambda: put(v, True))

        for hh in range(hp):
            h = hh if hp == nh else hb * hp + hh
            vals = [_ew_load(r, kd, w, hh) for r, (_, kd, _, w) in zip(refs[:n_in], ins)]
            ct_refs = list(zip(refs[n_in:n_in + n_ct], flat_cts))
            ct_vals, pos = [], 0
            for group in cts:
                v = None
                for r, (_, kd, _, w) in ct_refs[pos:pos + len(group)]:
                    t = _ew_load(r, kd, w, hh)
                    v = t if v is None else v + t
                pos += len(group)
                ct_vals.append(v)
            ex_vals = [_ew_load(r, kd, w, hh) for r, (_, kd, _, w) in zip(refs[n_in + n_ct:n_in + n_ct + n_ex], extras)]
            _, vjp = jax.vjp(lambda *a: f(h, *a), *vals)
            res = emit(vjp(tuple(ct_vals)), ex_vals)
            for idx, (r, v, (_, kd, w, _, acc)) in enumerate(zip(out_refs, res, outs)):
                if kd in ("row", "par"):
                    shared[idx] = v if shared[idx] is None else shared[idx] + v
                else:
                    store(r, v, (g1 == 0) if acc == "inner" else None, slice(hh * w, (hh + 1) * w))
        for idx, (r, (_, kd, _, _, acc)) in enumerate(zip(out_refs, outs)):
            if kd in ("row", "par"):
                assert acc == "all" or hp == nh
                store(r, shared[idx], jnp.logical_and(g0 == 0, g1 == 0) if acc == "all" else None)

    operands = list(ins) + flat_cts + list(extras)
    in_specs = [_ew_spec(kd, off, w, tb, hp, order, a.shape) for (a, kd, off, w) in operands]
    out_specs = [_ew_spec(kd, 0, w, tb, hp, order, shp) for (shp, kd, w, _, _) in outs]
    out_shape = [jax.ShapeDtypeStruct(shp, dt) for (shp, _, _, dt, _) in outs]
    return pl.pallas_call(
        body, name=name, grid=_ew_grid(rows, tb, nh, hp, order), in_specs=in_specs, out_specs=out_specs,
        out_shape=out_shape, compiler_params=_params(2),
    )(*[a for (a, _, _, _) in operands])


def f_rms(h, x, g):
    r = lax.rsqrt(jnp.mean(x * x, axis=-1, keepdims=True) + EPS)
    return (x * r * g,)


def _softplus(z):
    return jnp.maximum(z, 0.0) + jnp.log1p(jnp.exp(-jnp.abs(z)))


def f_small(h, sp, p1, p2):
    lane = _iota(sp.shape, 1)
    z = sp + p1
    g = -jnp.exp(p2) * _softplus(z)
    beta = jax.nn.sigmoid(z)
    logf = -_softplus(-z)
    return (jnp.where(lane < NH, g, jnp.where(lane < 2 * NH, beta, jnp.where(lane < 3 * NH, logf, 0.0))),)


def _pick(x, lane_id):
    lane = _iota(x.shape, 1)
    col = jnp.sum(jnp.where(lane == lane_id, x, 0.0), axis=1, keepdims=True)
    return jnp.broadcast_to(col, x.shape)


def f_bcast(h, so, cs):
    return _pick(so, h), _pick(so, h + NH), _pick(cs, h + 2 * NH)


def _shift_down(s):
    def down(x):
        r = pltpu.roll(x, s, 0)
        head = jnp.where(_iota((8, x.shape[1]), 0) >= s, r[:8], 0.0)
        return jnp.concatenate([head, r[8:]], axis=0)

    def up(g):
        n = g.shape[0]
        r = pltpu.roll(g, n - s, 0)
        tail = jnp.where(_iota((8, g.shape[1]), 0) < 8 - s, r[n - 8:], 0.0)
        return jnp.concatenate([r[:n - 8], tail], axis=0)

    @jax.custom_vjp
    def shift(x):
        return down(x)

    shift.defvjp(lambda x: (down(x), None), lambda _, g: (up(g),))
    return shift


def _silu(x):
    return x * jax.nn.sigmoid(x)


def make_f_conv(mode):
    sh1, sh2, sh3 = _shift_down(1), _shift_down(2), _shift_down(3)

    def f(h, x, w):
        sub = _iota(w.shape, 0)

        def tap(i):
            return jnp.sum(jnp.where(sub == i, w, 0.0), axis=0, keepdims=True)

        y = sh3(x) * tap(0)
        y = y + sh2(x) * tap(1)
        y = y + sh1(x) * tap(2)
        y = y + x * tap(3)
        s = _silu(y)
        if mode == "v":
            return (s,)
        n = s * lax.rsqrt(jnp.sum(s * s, axis=-1, keepdims=True) + EPS)
        if mode == "q":
            n = n * (LANES ** -0.5)
        return (n,)

    return f


def f_post(h, o, z, g):
    r = lax.rsqrt(jnp.mean(o * o, axis=-1, keepdims=True) + EPS)
    return (o * r * g * _silu(z),)


def f_merge(h, ga, gb, ya, yb):
    return (jax.nn.sigmoid(ga) * ya + jax.nn.sigmoid(gb) * yb,)


def cumsum_time(name, x, nseq, seq, reverse):
    nb = seq // LANES

    def body(x_ref, o_ref):
        r, c = _iota((LANES, LANES), 0), _iota((LANES, LANES), 1)
        tri = jnp.where((r <= c) if reverse else (r >= c), 1.0, 0.0).astype(F32)
        carry = jnp.zeros((1, LANES), F32)
        for b in (range(nb - 1, -1, -1) if reverse else range(nb)):
            blk = x_ref[b * LANES:(b + 1) * LANES, :]
            o_ref[b * LANES:(b + 1) * LANES, :] = _dot_mask(tri, blk, "nn") + carry
            carry = carry + jnp.sum(blk, axis=0, keepdims=True)

    spec = pl.BlockSpec((seq, LANES), lambda s: (s, 0))
    return pl.pallas_call(body, name=name, grid=(nseq,), in_specs=[spec], out_specs=spec,
                          out_shape=jax.ShapeDtypeStruct(x.shape, F32), compiler_params=_params(1))(x)


def transpose_time(name, x, nseq, seq):
    def body(x_ref, o_ref):
        o_ref[...] = x_ref[...].T

    return pl.pallas_call(
        body, name=name, grid=(nseq,), in_specs=[pl.BlockSpec((seq, LANES), lambda s: (s, 0))],
        out_specs=pl.BlockSpec((LANES, seq), lambda s: (s, 0)),
        out_shape=jax.ShapeDtypeStruct((nseq * LANES, seq), F32), compiler_params=_params(1))(x)


def _gdn_masks():
    n = GDN_ROWS
    r, c = _iota((n, n), 0), _iota((n, n), 1)
    shift = GDN_CHUNK.bit_length() - 1
    same = lax.shift_right_logical(r, shift) == lax.shift_right_logical(c, shift)
    return r, c, same


def _each(fn, *lists):
    return [fn(*xs) for xs in zip(*lists)]


def _gdn_decay(gbs):
    r, c, same = _gdn_masks()
    seg_tril = jnp.where(jnp.logical_and(same, r >= c), 1.0, 0.0).astype(F32)
    g_cum = _each(lambda gb: mm_mask(seg_tril, gb), gbs)
    lane0 = _iota(gbs[0].shape, 1) == 0
    g_col = _each(lambda g: jnp.sum(jnp.where(lane0, g, 0.0), axis=1, keepdims=True), g_cum)
    g_row = _each(lambda g: jnp.sum(jnp.where(r == c, jnp.broadcast_to(g, (GDN_ROWS, GDN_ROWS)), 0.0), axis=0, keepdims=True), g_col)
    return g_cum, _each(lambda a, b: a - b, g_col, g_row)


def gdn_a_mats(ks, bbs, diff):
    r, c, same = _gdn_masks()
    strict = jnp.logical_and(same, r > c)
    lane0 = _iota(bbs[0].shape, 1) == 0
    beta_col = _each(lambda bb: jnp.sum(jnp.where(lane0, bb, 0.0), axis=1, keepdims=True), bbs)
    kk = _each(lambda k: _dot(k, k, "nt", LO), ks)
    return _each(lambda b, x, d: jnp.where(strict, b * x * jnp.exp(jnp.where(strict, d, 0.0)), 0.0), beta_col, kk, diff)


@jax.custom_vjp
def saved_inverse(a, t_corr):
    return t_corr


def _saved_inverse_bwd(t, dt):
    left = dt + _dot(t, dt, "tn", LO)
    return -(left + _dot(left, t, "nt", LO)), jnp.zeros_like(t)


saved_inverse.defvjp(lambda a, t_corr: (t_corr, t_corr), _saved_inverse_bwd)


def gdn_block(*args):
    ts, qs, ks, vs, gbs, bbs = (list(args[i::6]) for i in range(6))
    g_cum, diff = _gdn_decay(gbs)
    ts = _each(saved_inverse, gdn_a_mats(ks, bbs, diff), ts)
    return gdn_outputs(ts, qs, ks, vs, gbs, bbs, g_cum, diff)


def gdn_outputs(ts, qs, ks, vs, gbs, bbs, g_cum, diff):
    r, c, same = _gdn_masks()
    incl = jnp.logical_and(same, r >= c)
    decay = _each(lambda d: jnp.where(incl, jnp.exp(jnp.where(incl, d, 0.0)), 0.0), diff)
    e_g = _each(jnp.exp, g_cum)
    v_beta = _each(lambda v, bb: v * bb, vs, bbs)
    k_beta = _each(lambda k, bb, e: k * bb * e, ks, bbs, e_g)
    value = _each(lambda t, x: x + _dot(t, x, "nn", LO), ts, v_beta)
    k_cum = _each(lambda t, x: x + _dot(t, x, "nn", LO), ts, k_beta)
    attn = _each(lambda q, k, d: _dot(q, k, "nt", LO) * d, qs, ks, decay)
    ones = jnp.where(same, 1.0, 0.0).astype(F32)
    g_last = _each(lambda gb: mm_mask(ones, gb), gbs)
    q_dec = _each(lambda q, e: q * e, qs, e_g)
    k_dec = _each(lambda k, gl, g: k * jnp.exp(gl - g), ks, g_last, g_cum)
    return tuple(x for head in zip(value, k_cum, attn, q_dec, k_dec) for x in head)


def tri_inverse(mats):
    n = GDN_ROWS
    r, c = _iota((n, n), 0), _iota((n, n), 1)
    shift = GDN_BASE.bit_length() - 1
    blk = lax.shift_right_logical(r, shift) == lax.shift_right_logical(c, shift)
    each = lambda fn, *lists: [fn(*xs) for xs in zip(*lists)]
    mm = lambda x, y: _dot(x, y, "nn", LO)
    d = each(lambda a: jnp.where(blk, a, 0.0), mats)
    lo = each(lambda a, dd: a - dd, mats, d)
    p = each(lambda dd: -dd, d)
    c_d = p
    for _ in range(shift - 1):
        p = each(mm, p, p)
        c_d = each(lambda cd, pp, prod: cd + pp + prod, c_d, p, each(mm, c_d, p))
    assert GDN_CHUNK // GDN_BASE == 4
    nmat = each(lambda l, prod: l + prod, lo, each(mm, c_d, lo))
    n2 = each(mm, nmat, nmat)
    c_n = each(lambda nn2, nm, prod: (nn2 - nm) - prod, n2, nmat, each(mm, nmat, n2))
    return each(lambda cn, cd, prod: cn + cd + prod, c_n, c_d, each(mm, c_n, c_d))


GDN_AHP = 8


def _gdn_a_specs():
    blk = pl.BlockSpec((GDN_ROWS, GDN_AHP * LANES), lambda i, h: (i, h))
    sq = pl.BlockSpec((GDN_ROWS, GDN_AHP * GDN_ROWS), lambda i, h: (i, h))
    return blk, sq


def _head(ref, hh):
    width = ref.shape[1] // GDN_AHP
    return ref.at[:, hh * width:(hh + 1) * width]


def gdn_a_fwd(q, k, v, gb, bb, rows):
    blk, sq = _gdn_a_specs()

    def body(q_ref, k_ref, v_ref, gb_ref, bb_ref, val_ref, kc_ref, at_ref, qd_ref, kd_ref, t_ref):
        heads = [[_head(r, hh)[...] for r in (q_ref, k_ref, v_ref, gb_ref, bb_ref)] for hh in range(GDN_AHP)]
        qs, ks, vs, gbs, bbs = (list(col) for col in zip(*heads))
        g_cum, diff = _gdn_decay(gbs)
        t_corr = tri_inverse(gdn_a_mats(ks, bbs, diff))
        res = gdn_outputs(t_corr, qs, ks, vs, gbs, bbs, g_cum, diff)
        for hh in range(GDN_AHP):
            for r, x in zip((val_ref, kc_ref, at_ref, qd_ref, kd_ref, t_ref), (*res[5 * hh:5 * hh + 5], t_corr[hh])):
                _head(r, hh)[...] = x.astype(r.dtype)

    wide = lambda dt: jax.ShapeDtypeStruct((rows, NH * LANES), dt)
    square = jax.ShapeDtypeStruct((rows, NH * GDN_ROWS), BF16)
    return pl.pallas_call(
        body, name="gdn_a_fwd", grid=(rows // GDN_ROWS, NH // GDN_AHP), in_specs=[blk] * 5,
        out_specs=[blk, blk, sq, blk, blk, sq], out_shape=[wide(F32), wide(BF16), square, wide(BF16), wide(BF16), square],
        compiler_params=_params(2))(q, k, v, gb, bb)


def gdn_a_bwd(q, k, v, gb, bb, t_inv, dval, dkc, dat, dqd, dkd, dgb_b, rows):
    blk, sq = _gdn_a_specs()

    def body(q_ref, k_ref, v_ref, gb_ref, bb_ref, t_ref, dval_ref, dkc_ref, dat_ref, dqd_ref, dkd_ref, dgbb_ref,
             dq_ref, dk_ref, dv_ref, dgb_ref, dbb_ref):
        hs = range(GDN_AHP)
        heads = [[_head(r, hh)[...] for r in (q_ref, k_ref, v_ref, gb_ref, bb_ref)] for hh in hs]
        tvs = [_head(t_ref, hh)[...].astype(F32) for hh in hs]
        _, vjp = jax.vjp(gdn_block, *[x for t, head in zip(tvs, heads) for x in (t, *head)])
        grads = vjp(tuple(_head(r, hh)[...] for hh in hs for r in (dval_ref, dkc_ref, dat_ref, dqd_ref, dkd_ref)))
        for hh in hs:
            _, dq, dk, dv, dgb, dbb = grads[6 * hh:6 * hh + 6]
            _head(dq_ref, hh)[...] = dq
            _head(dk_ref, hh)[...] = dk
            _head(dv_ref, hh)[...] = dv
            _head(dgb_ref, hh)[...] = dgb + _head(dgbb_ref, hh)[...]
            _head(dbb_ref, hh)[...] = dbb

    wide = jax.ShapeDtypeStruct((rows, NH * LANES), F32)
    return pl.pallas_call(
        body, name="gdn_a_bwd", grid=(rows // GDN_ROWS, NH // GDN_AHP),
        in_specs=[blk] * 5 + [sq, blk, blk, sq, blk, blk, blk], out_specs=[blk] * 5, out_shape=[wide] * 5,
        compiler_params=_params(2))(q, k, v, gb, bb, t_inv, dval, dkc, dat, dqd, dkd, dgb_b)


N_CH = GDN_ROWS // GDN_CHUNK


GDN_HP = 8


def gdn_chunk(c):
    def f(*args):
        val, kc, at, qd, kd, gb, s = (list(args[i::7]) for i in range(7))
        zero = jnp.zeros((GDN_CHUNK, LANES), F32)
        v_new = _each(lambda v, k, st: v - _dot(k, st, "nn", LO), val, kc, s)
        v_pad = _each(lambda v: jnp.concatenate([zero] * c + [v] + [zero] * (N_CH - 1 - c), axis=0), v_new)
        out = _each(lambda q, st, a, vp: _dot(q, st, "nn", LO) + _dot(a, vp, "nn", LO), qd, s, at, v_pad)
        dec = _each(lambda g: jnp.exp(jnp.sum(g, axis=0, keepdims=True)), gb)
        s_new = _each(lambda st, d, k, v: st * d + _dot(k, v, "tn", LO), s, dec, kd, v_new)
        return tuple(x for head in zip(out, s_new) for x in head)

    return f


def _gdn_piece(ref, hh, c):
    width = ref.shape[1] // GDN_HP
    return ref.at[c * GDN_CHUNK:(c + 1) * GDN_CHUNK, hh * width:(hh + 1) * width]


def _gdn_snap(ref, hh, c):
    row = (hh * N_CH + c) * LANES
    return ref.at[row:row + LANES, :]


def _gdn_b_specs(nb, rev):
    def blk_row(s, j):
        return s * nb + (nb - 1 - j if rev else j)

    blk = pl.BlockSpec((GDN_ROWS, GDN_HP * LANES), lambda s, hb, j: (blk_row(s, j), hb))
    sq = pl.BlockSpec((GDN_ROWS, GDN_HP * GDN_ROWS), lambda s, hb, j: (blk_row(s, j), hb))
    snap = pl.BlockSpec((GDN_HP * N_CH * LANES, LANES), lambda s, hb, j: (blk_row(s, j) * (NH // GDN_HP) + hb, 0))
    return blk, sq, snap


def gdn_b_fwd(val, kc, at, qd, kd, gb, nseq, seq):
    nb = seq // GDN_ROWS
    rows = nseq * seq
    blk, sq, snap = _gdn_b_specs(nb, False)

    def body(val_ref, kc_ref, at_ref, qd_ref, kd_ref, gb_ref, o_ref, snap_ref, s_ref):
        @pl.when(pl.program_id(2) == 0)
        def _():
            s_ref[...] = jnp.zeros_like(s_ref)

        hs = range(GDN_HP)
        states = [s_ref[hh] for hh in hs]
        for c in range(N_CH):
            for hh in hs:
                _gdn_snap(snap_ref, hh, c)[...] = states[hh]
            res = gdn_chunk(c)(*[x for hh in hs for x in (
                *[_gdn_piece(r, hh, c)[...].astype(F32) for r in (val_ref, kc_ref, at_ref, qd_ref, kd_ref, gb_ref)], states[hh])])
            for hh in hs:
                _gdn_piece(o_ref, hh, c)[...] = res[2 * hh]
            states = [res[2 * hh + 1] for hh in hs]
        for hh in hs:
            s_ref[hh] = states[hh]

    return pl.pallas_call(
        body, name="gdn_b_fwd", grid=(nseq, NH // GDN_HP, nb), in_specs=[blk, blk, sq, blk, blk, blk], out_specs=[blk, snap],
        out_shape=[jax.ShapeDtypeStruct((rows, NH * LANES), F32),
                   jax.ShapeDtypeStruct((nseq * nb * NH * N_CH * LANES, LANES), F32)],
        scratch_shapes=[pltpu.VMEM((GDN_HP, LANES, LANES), F32)], compiler_params=_params(3))(val, kc, at, qd, kd, gb)


def gdn_b_bwd(val, kc, at, qd, kd, gb, snaps, do, nseq, seq):
    nb = seq // GDN_ROWS
    rows = nseq * seq
    blk, sq, snap = _gdn_b_specs(nb, True)

    def body(val_ref, kc_ref, at_ref, qd_ref, kd_ref, gb_ref, snap_ref, do_ref,
             dval_ref, dkc_ref, dat_ref, dqd_ref, dkd_ref, dgb_ref, ds_ref):
        @pl.when(pl.program_id(2) == 0)
        def _():
            ds_ref[...] = jnp.zeros_like(ds_ref)

        hs = range(GDN_HP)
        d_states = [ds_ref[hh] for hh in hs]
        for c in reversed(range(N_CH)):
            _, vjp = jax.vjp(gdn_chunk(c), *[x for hh in hs for x in (
                *[_gdn_piece(r, hh, c)[...].astype(F32) for r in (val_ref, kc_ref, at_ref, qd_ref, kd_ref, gb_ref)],
                _gdn_snap(snap_ref, hh, c)[...])])
            grads = vjp(tuple(x for hh in hs for x in (_gdn_piece(do_ref, hh, c)[...], d_states[hh])))
            for hh in hs:
                for i, r in enumerate([dval_ref, dkc_ref, dat_ref, dqd_ref, dkd_ref, dgb_ref]):
                    _gdn_piece(r, hh, c)[...] = grads[7 * hh + i]
            d_states = [grads[7 * hh + 6] for hh in hs]
        for hh in hs:
            ds_ref[hh] = d_states[hh]

    wide = jax.ShapeDtypeStruct((rows, NH * LANES), F32)
    square = jax.ShapeDtypeStruct((rows, NH * GDN_ROWS), F32)
    return pl.pallas_call(
        body, name="gdn_b_bwd", grid=(nseq, NH // GDN_HP, nb), in_specs=[blk, blk, sq, blk, blk, blk, snap, blk],
        out_specs=[blk, blk, sq, blk, blk, blk], out_shape=[wide, wide, square, wide, wide, wide],
        scratch_shapes=[pltpu.VMEM((GDN_HP, LANES, LANES), F32)], compiler_params=_params(3))(val, kc, at, qd, kd, gb, snaps, do)


FOX_Q, FOX_K, FOX_V = 4 * NH, 5 * NH, 6 * NH
FOX_SCALE = LANES ** -0.5


def _head_row(ct_ref, h, off, width):
    blk = ct_ref[:, pl.ds(off, width)]
    return jnp.sum(jnp.where(_iota(blk.shape, 0) == h, blk, 0.0), axis=0, keepdims=True)


def _col(x):
    return jnp.max(x, axis=1, keepdims=True)


def _row(x):
    return jnp.max(x.T, axis=0, keepdims=True)


def _causal(shape, q_dim):
    return _iota(shape, q_dim) >= _iota(shape, 1 - q_dim)


FOX_HP = 4


def _fox_specs(seq, tile, n_tiles):
    tblk = pl.BlockSpec((tile, FOX_HP * LANES), lambda s, h, i: (s * n_tiles + i, h))
    vtblk = pl.BlockSpec((tile, FOX_HP * LANES), lambda s, h, i: (s * n_tiles + i, h + FOX_V // FOX_HP))
    full = pl.BlockSpec((seq, FOX_HP * LANES), lambda s, h, i: (s, h))
    vfull = pl.BlockSpec((seq, FOX_HP * LANES), lambda s, h, i: (s, h + FOX_V // FOX_HP))
    ctb = pl.BlockSpec((NH, seq), lambda s, h, i: (s * (LANES // NH) + 2, 0))
    return tblk, vtblk, full, vfull, ctb


def _lanes_of(hh):
    return slice(hh * LANES, (hh + 1) * LANES)


def fox_fwd(qn, kn, proj, ct, nseq, seq):
    tq = tk = min(ATT_TILE, seq)
    nq = seq // tq
    rows = nseq * seq
    qblk, _, full, vfull, ctb = _fox_specs(seq, tq, nq)
    hs = range(FOX_HP)

    def body(q_ref, k_ref, v_ref, ct_ref, o_ref, o16_ref, lse_ref):
        hb, i = pl.program_id(1), pl.program_id(2)
        q = [q_ref[:, _lanes_of(hh)] for hh in hs]

        def step(j, carry, diag):
            m, l, acc = (list(carry[t::3]) for t in range(3))
            off = pl.multiple_of(j * tk, tk)
            k = [k_ref[pl.ds(off, tk), _lanes_of(hh)] for hh in hs]
            v = [v_ref[pl.ds(off, tk), _lanes_of(hh)].astype(BF16) for hh in hs]
            ck = [_head_row(ct_ref, hb * FOX_HP + hh, off, tk) for hh in hs]
            s = _each(lambda qq, kk, cc: _dot(qq, kk, "nt") * FOX_SCALE - cc, q, k, ck)
            if diag:
                s = _each(lambda x: jnp.where(_causal(x.shape, 0), x, NEG), s)
            m_new = _each(lambda mm, x: jnp.maximum(mm, jnp.max(x, axis=1, keepdims=True)), m, s)
            p = _each(lambda x, mm: jnp.exp(x - mm), s, m_new)
            alpha = _each(lambda mo, mn: jnp.exp(mo - mn), m, m_new)
            l = _each(lambda a, ll, pp: a * ll + jnp.sum(pp, axis=1, keepdims=True), alpha, l, p)
            acc = _each(lambda a, ac, pp, vv: a * ac + _dot(pp.astype(BF16), vv, "nn"), alpha, acc, p, v)
            return tuple(x for head in zip(m_new, l, acc) for x in head)

        init = (jnp.full((tq, 1), NEG, F32), jnp.zeros((tq, 1), F32), jnp.zeros((tq, LANES), F32)) * FOX_HP
        res = step(i, lax.fori_loop(0, i, lambda j, c: step(j, c, False), init), True)
        for hh in hs:
            m, l, acc = res[3 * hh:3 * hh + 3]
            o = acc / l
            o_ref[:, _lanes_of(hh)] = o
            o16_ref[:, _lanes_of(hh)] = o.astype(BF16)
            lse_ref[:, _lanes_of(hh)] = jnp.broadcast_to(m + jnp.log(l), (tq, LANES))

    wide = (rows, NH * LANES)
    return pl.pallas_call(
        body, name="fox_fwd", grid=(nseq, NH // FOX_HP, nq), in_specs=[qblk, full, vfull, ctb], out_specs=[qblk] * 3,
        out_shape=[jax.ShapeDtypeStruct(wide, F32), jax.ShapeDtypeStruct(wide, BF16), jax.ShapeDtypeStruct(wide, F32)],
        compiler_params=_params(3))(qn, kn, proj, ct)


def fox_dq(qn, kn, proj, ct, do, lse, o, after, nseq, seq):
    tq = tk = min(ATT_TILE, seq)
    nq = seq // tq
    rows = nseq * seq
    qblk, _, full, vfull, ctb = _fox_specs(seq, tq, nq)
    hs = range(FOX_HP)

    def body(q_ref, k_ref, v_ref, ct_ref, do_ref, lse_ref, o_ref, *rest):
        dq_ref, dc_ref = rest[len(after):]
        hb, i = pl.program_id(1), pl.program_id(2)
        q = [q_ref[:, _lanes_of(hh)] for hh in hs]
        lse = [_col(lse_ref[:, _lanes_of(hh)]) for hh in hs]
        delta = [jnp.sum(do_ref[:, _lanes_of(hh)] * o_ref[:, _lanes_of(hh)], axis=1, keepdims=True) for hh in hs]
        do16 = [do_ref[:, _lanes_of(hh)].astype(BF16) for hh in hs]

        def step(j, carry, diag):
            dq, dc = (list(carry[t::2]) for t in range(2))
            off = pl.multiple_of(j * tk, tk)
            k = [k_ref[pl.ds(off, tk), _lanes_of(hh)] for hh in hs]
            v = [v_ref[pl.ds(off, tk), _lanes_of(hh)].astype(BF16) for hh in hs]
            ck = [_head_row(ct_ref, hb * FOX_HP + hh, off, tk) for hh in hs]
            p = _each(lambda qq, kk, cc, ll: jnp.exp(_dot(qq, kk, "nt") * FOX_SCALE - cc - ll), q, k, ck, lse)
            if diag:
                p = _each(lambda x: jnp.where(_causal(x.shape, 0), x, 0.0), p)
            dp = _each(lambda d, vv: _dot(d, vv, "nt"), do16, v)
            ds = _each(lambda pp, d, dl: pp * (d - dl), p, dp, delta)
            dq = _each(lambda a, x, kk: a + _dot(x.astype(BF16), kk, "nn"), dq, ds, k)
            dc = _each(lambda a, x: a + jnp.sum(x, axis=1, keepdims=True), dc, ds)
            return tuple(x for head in zip(dq, dc) for x in head)

        init = (jnp.zeros((tq, LANES), F32), jnp.zeros((tq, 1), F32)) * FOX_HP
        res = step(i, lax.fori_loop(0, i, lambda j, c: step(j, c, False), init), True)
        for hh in hs:
            dq_ref[:, _lanes_of(hh)] = res[2 * hh] * FOX_SCALE
            dc_ref[:, _lanes_of(hh)] = jnp.where(_iota((tq, LANES), 1) == 0, res[2 * hh + 1], 0.0)

    wide = jax.ShapeDtypeStruct((rows, NH * LANES), F32)
    return pl.pallas_call(
        body, name="fox_dq", grid=(nseq, NH // FOX_HP, nq), in_specs=[qblk, full, vfull, ctb, qblk, qblk, qblk] + [ANY] * len(after),
        out_specs=[qblk, qblk], out_shape=[wide, wide], compiler_params=_params(3))(qn, kn, proj, ct, do, lse, o, *after)


def fox_dkv(qn, kn, proj, cb, do, lse, o, after, nseq, seq):
    tq = tk = min(ATT_TILE, seq)
    nq = seq // tq
    rows = nseq * seq
    kblk, vblk, full, _, _ = _fox_specs(seq, tk, nq)
    hs = range(FOX_HP)

    def body(q_ref, k_ref, v_ref, cb_ref, do_ref, lse_ref, o_ref, *rest):
        dk_ref, dv_ref, dc_ref = rest[len(after):]
        j = pl.program_id(2)
        k = [k_ref[:, _lanes_of(hh)] for hh in hs]
        v16 = [v_ref[:, _lanes_of(hh)].astype(BF16) for hh in hs]
        ck = [_col(cb_ref[:, _lanes_of(hh)]) for hh in hs]

        def step(i, carry, diag):
            dk, dv, dc = (list(carry[t::3]) for t in range(3))
            off = pl.multiple_of(i * tq, tq)
            q = [q_ref[pl.ds(off, tq), _lanes_of(hh)] for hh in hs]
            do32 = [do_ref[pl.ds(off, tq), _lanes_of(hh)] for hh in hs]
            do16 = [d.astype(BF16) for d in do32]
            lse = [_row(lse_ref[pl.ds(off, tq), _lanes_of(hh)]) for hh in hs]
            delta = [_row(jnp.broadcast_to(jnp.sum(d * o_ref[pl.ds(off, tq), _lanes_of(hh)], axis=1, keepdims=True), (tq, LANES)))
                     for hh, d in zip(hs, do32)]
            p = _each(lambda kk, qq, cc, ll: jnp.exp(_dot(kk, qq, "nt") * FOX_SCALE - cc - ll), k, q, ck, lse)
            if diag:
                p = _each(lambda x: jnp.where(_causal(x.shape, 1), x, 0.0), p)
            dv = _each(lambda a, pp, d: a + _dot(pp.astype(BF16), d, "nn"), dv, p, do16)
            ds = _each(lambda pp, vv, d, dl: pp * (_dot(vv, d, "nt") - dl), p, v16, do16, delta)
            dk = _each(lambda a, x, qq: a + _dot(x.astype(BF16), qq, "nn"), dk, ds, q)
            dc = _each(lambda a, x: a + jnp.sum(x, axis=1, keepdims=True), dc, ds)
            return tuple(x for head in zip(dk, dv, dc) for x in head)

        zero = jnp.zeros((tk, LANES), F32)
        carry = step(j, (zero, zero, jnp.zeros((tk, 1), F32)) * FOX_HP, True)
        res = lax.fori_loop(j + 1, nq, lambda i, c: step(i, c, False), carry)
        for hh in hs:
            dk, dv, dc = res[3 * hh:3 * hh + 3]
            dk_ref[:, _lanes_of(hh)] = dk * FOX_SCALE
            dv_ref[:, _lanes_of(hh)] = dv.astype(BF16)
            dc_ref[:, _lanes_of(hh)] = jnp.where(_iota((tk, LANES), 1) == 0, -dc, 0.0)

    wide = (rows, NH * LANES)
    return pl.pallas_call(
        body, name="fox_dkv", grid=(nseq, NH // FOX_HP, nq), in_specs=[full, kblk, vblk, kblk, full, full, full] + [ANY] * len(after),
        out_specs=[kblk, kblk, kblk],
        out_shape=[jax.ShapeDtypeStruct(wide, F32), jax.ShapeDtypeStruct(wide, BF16), jax.ShapeDtypeStruct(wide, F32)],
        compiler_params=_params(3))(qn, kn, proj, cb, do, lse, o, *after)


def _adamw_update(w, g, m, v):
    m_new = ADAM_B1 * m + (1.0 - ADAM_B1) * g
    v_new = ADAM_B2 * v + (1.0 - ADAM_B2) * (g * g)
    m_hat = m_new / (1.0 - ADAM_B1 ** ADAM_STEP)
    v_hat = v_new / (1.0 - ADAM_B2 ** ADAM_STEP)
    return -ADAM_LR * (m_hat / (jnp.sqrt(v_hat) + ADAM_EPS) + ADAM_WD * w), m_new, v_new


def adamw(name, w, g, m, v):
    rows, cols = w.shape
    tb = min(rows, 128)
    assert rows % tb == 0
    blk = pl.BlockSpec((tb, cols), lambda i: (i, 0))

    def body(w_ref, g_ref, m_ref, v_ref, d_ref, mo_ref, vo_ref):
        d_ref[...], mo_ref[...], vo_ref[...] = _adamw_update(w_ref[...], g_ref[...], m_ref[...], v_ref[...])

    shp = jax.ShapeDtypeStruct(w.shape, F32)
    return pl.pallas_call(body, name=name, grid=(rows // tb,), in_specs=[blk] * 4, out_specs=[blk] * 3,
                          out_shape=[shp] * 3, compiler_params=_params(1))(w, g, m, v)


SPLIT_TILE = 128


def _tiled(shape2d, ax, n_lead, index):
    blk = (SPLIT_TILE, shape2d[1]) if ax == 0 else (shape2d[0], SPLIT_TILE)

    def index_map(*args):
        *lead, t = index(*args)
        return (*lead, t, 0) if ax == 0 else (*lead, 0, t)

    return pl.BlockSpec((None,) * n_lead + blk, index_map)


def adamw_halves(name, w, mine, other, m, v, c, ax):
    steps = w.shape[ax] // 2 // SPLIT_TILE
    assert w.shape[ax] == 2 * steps * SPLIT_TILE

    def body(c_ref, w_ref, mine_ref, other_ref, m_ref, v_ref, g_ref, d_ref, mo_ref, vo_ref):
        g = jnp.where(pl.program_id(0) // steps == c_ref[0], mine_ref[...], other_ref[...])
        g_ref[...] = g
        d_ref[...], mo_ref[...], vo_ref[...] = _adamw_update(w_ref[...], g, m_ref[...], v_ref[...])

    blk = _tiled(w.shape, ax, 0, lambda i, c_ref: (i,))
    hblk = _tiled(mine.shape, ax, 0, lambda i, c_ref: (i % steps,))
    grid_spec = pltpu.PrefetchScalarGridSpec(num_scalar_prefetch=1, grid=(2 * steps,),
                                             in_specs=[blk, hblk, hblk, blk, blk], out_specs=[blk] * 4)
    shp = jax.ShapeDtypeStruct(w.shape, F32)
    return pl.pallas_call(body, name=name, grid_spec=grid_spec, out_shape=[shp] * 4,
                          compiler_params=_params(1))(c, w, mine, other, m, v)


def add_chips(name, slots, parts, chip, axes):
    outs = []
    for idx, (x, own, ax) in enumerate(zip(slots, parts, axes)):
        n, shape2d = x.shape[0], x.shape[1:]
        steps = shape2d[ax] // SPLIT_TILE
        assert shape2d[ax] == steps * SPLIT_TILE

        def body(me_ref, *refs, n=n):
            o_ref = refs[n + 1]
            acc = None
            for t in range(n):
                term = jnp.where(me_ref[0] == t, refs[n][...], refs[t][...]).astype(F32)
                acc = term if acc is None else acc + term
            o_ref[...] = acc

        def filled(t, n=n):
            return lambda i, me_ref: (jnp.where(me_ref[0] == t, (t + 1) % n, t), i)

        grid_spec = pltpu.PrefetchScalarGridSpec(
            num_scalar_prefetch=1, grid=(steps,),
            in_specs=[_tiled(shape2d, ax, 1, filled(t)) for t in range(n)]
            + [_tiled(shape2d, ax, 1, lambda i, me_ref: (me_ref[0], i))],
            out_specs=_tiled(shape2d, ax, 0, lambda i, me_ref: (i,)))
        outs.append(pl.pallas_call(
            body, name=f"{name}_{idx}", grid_spec=grid_spec, out_shape=jax.ShapeDtypeStruct(shape2d, F32),
            compiler_params=_params(1))(chip, *([x] * n), own))
    return outs


def add_pair(name, gs, rs, c, axes):
    outs = []
    for idx, (g, r, ax) in enumerate(zip(gs, rs, axes)):
        nb = r.shape[0]
        steps = r.shape[1 + ax] // SPLIT_TILE
        assert r.shape[1 + ax] == steps * SPLIT_TILE

        def body(c_ref, g_ref, r_ref, o_ref):
            o_ref[...] = (g_ref[...] + r_ref[...]).astype(BF16)

        grid_spec = pltpu.PrefetchScalarGridSpec(
            num_scalar_prefetch=1, grid=(nb, steps),
            in_specs=[_tiled(g.shape[1:], ax, 1, lambda b, i, c_ref: (b, c_ref[0] * steps + i)),
                      _tiled(r.shape[1:], ax, 1, lambda b, i, c_ref: (b, i))],
            out_specs=_tiled(r.shape[1:], ax, 1, lambda b, i, c_ref: (b, i)))
        outs.append(pl.pallas_call(
            body, name=f"{name}_{idx}", grid_spec=grid_spec, out_shape=jax.ShapeDtypeStruct(r.shape, BF16),
            compiler_params=_params(2))(c, g, r))
    return outs


def _place():
    x, y, c = lax.axis_index("x"), lax.axis_index("y"), lax.axis_index("c")
    return x, y, c, [(1 - x, y), (x, 1 - y), (1 - x, 1 - y)]


def _remote(src, dst, send_sem, recv_sem, dev):
    return pltpu.make_async_remote_copy(src_ref=src, dst_ref=dst, send_sem=send_sem, recv_sem=recv_sem,
                                        device_id=dev, device_id_type=MESH)


def _half(ref, lead, ax, which):
    size = ref.shape[len(lead) + ax] // 2
    part = pl.ds(which * size, size)
    return ref.at[(*lead, part, slice(None)) if ax == 0 else (*lead, slice(None), part)]


def gather_ring(shard):
    rows, cols = shard.shape
    half = cols // 2
    top = rows // 2 // 16 * 16
    assert shard.dtype == BF16 and half % LANES == 0

    def body(in_ref, out_ref, ici_s, ici_r, d2d_s, d2d_r):
        x, y, c, _ = _place()
        me, xn, yn, dg = 2 * x + y, 2 * (1 - x) + y, 2 * x + (1 - y), 2 * (1 - x) + (1 - y)
        to_x, to_y, sib = (1 - x, y, c), (x, 1 - y, c), (x, y, 1 - c)
        mine, other = pl.ds(c * half, half), pl.ds((1 - c) * half, half)
        upper, lower = pl.ds(0, top), pl.ds(top, rows - top)
        started = []

        def send(src, dst, sems, k, dev):
            cp = _remote(src, dst, sems[0].at[k], sems[1].at[k], dev)
            cp.start()
            started.append(cp)

        def arrive(dst, sems, k):
            _remote(dst, dst, sems[0].at[k], sems[1].at[k], sib).wait_recv()

        ici, d2d = (ici_s, ici_r), (d2d_s, d2d_r)
        send(in_ref, out_ref.at[me], d2d, 0, sib)
        send(in_ref.at[:, mine], out_ref.at[me, :, mine], ici, 0, to_x)
        send(in_ref.at[:, mine], out_ref.at[me, :, mine], ici, 1, to_y)
        arrive(out_ref.at[xn, :, mine], ici, 0)
        send(out_ref.at[xn, upper, mine], out_ref.at[xn, upper, mine], ici, 2, to_y)
        send(out_ref.at[xn, :, mine], out_ref.at[xn, :, mine], d2d, 1, sib)
        arrive(out_ref.at[yn, :, mine], ici, 1)
        send(out_ref.at[yn, lower, mine], out_ref.at[yn, lower, mine], ici, 3, to_x)
        send(out_ref.at[yn, :, mine], out_ref.at[yn, :, mine], d2d, 2, sib)
        arrive(out_ref.at[dg, upper, mine], ici, 2)
        send(out_ref.at[dg, upper, mine], out_ref.at[dg, upper, mine], d2d, 3, sib)
        arrive(out_ref.at[dg, lower, mine], ici, 3)
        send(out_ref.at[dg, lower, mine], out_ref.at[dg, lower, mine], d2d, 4, sib)
        arrive(out_ref.at[me], d2d, 0)
        arrive(out_ref.at[xn, :, other], d2d, 1)
        arrive(out_ref.at[yn, :, other], d2d, 2)
        arrive(out_ref.at[dg, upper, other], d2d, 3)
        arrive(out_ref.at[dg, lower, other], d2d, 4)
        for cp in started:
            cp.wait_send()

    return pl.pallas_call(
        body, name="gather_ring", in_specs=[ANY], out_specs=ANY, out_shape=jax.ShapeDtypeStruct((4,) + shard.shape, shard.dtype),
        scratch_shapes=[pltpu.SemaphoreType.DMA((4,))] * 2 + [pltpu.SemaphoreType.DMA((5,))] * 2,
    )(shard)


HBM = pl.BlockSpec(memory_space=pltpu.HBM)
SEM = pl.BlockSpec(memory_space=pltpu.SEMAPHORE)
DATAFLOW = pltpu.SideEffectType.DATAFLOW_SIDE_EFFECTING


def _hbm(a):
    return pltpu.with_memory_space_constraint(a, pltpu.HBM)


class SplitExchange:
    def __init__(self, name, srcs, zone_shapes, n_sems, plan):
        self.name, self.n, self.n_sems, self.plan = name, len(srcs), n_sems, plan
        self.srcs = [_hbm(s) for s in srcs]
        self.zones = [_hbm(lax.empty(shape, s.dtype)) for shape, s in zip(zone_shapes, srcs)]

    def start(self, after):
        n, n_after = self.n, len(after)

        def body(*refs):
            ins, lands = refs[:n], refs[n:2 * n]
            send, recv, token = refs[2 * n + n_after], refs[2 * n + n_after + 1], refs[-1]
            for src, dst, si, ri, dev in self.plan(ins, lands)[0]:
                _remote(src, dst, send.at[si], recv.at[ri], dev).start()
            token[...] = jnp.zeros_like(token)

        res = pl.pallas_call(
            body, name=f"{self.name}_start", in_specs=[HBM] * (2 * n) + [ANY] * n_after,
            out_specs=[SEM, SEM] + [HBM] * (2 * n) + [pl.BlockSpec(memory_space=pltpu.VMEM)],
            out_shape=[pltpu.SemaphoreType.DMA((self.n_sems,)), pltpu.SemaphoreType.DMA((self.n_sems,))]
            + [pltpu.HBM(a.shape, a.dtype) for a in self.srcs + self.zones] + [jax.ShapeDtypeStruct((8, LANES), F32)],
            input_output_aliases={i: 2 + i for i in range(2 * n)},
            compiler_params=pltpu.CompilerParams(has_side_effects=DATAFLOW),
        )(*self.srcs, *self.zones, *after)
        self.sems, self.srcs, self.zones = res[:2], list(res[2:2 + n]), list(res[2 + n:2 + 2 * n])
        return res[-1]

    def wait(self, after):
        n = self.n

        def body(*refs):
            ins, lands = refs[:n], refs[n:2 * n]
            send, recv = refs[2 * n], refs[2 * n + 1]
            sends, arrivals = self.plan(ins, lands)
            for src, _, si, _, dev in sends:
                _remote(src, src, send.at[si], recv.at[si], dev).wait_send()
            for landed, ri in arrivals:
                _remote(landed, landed, send.at[ri], recv.at[ri], _place()[:3]).wait_recv()

        res = pl.pallas_call(
            body, name=f"{self.name}_wait", in_specs=[HBM] * (2 * n) + [SEM, SEM, ANY], out_specs=[HBM] * (2 * n),
            out_shape=[pltpu.HBM(a.shape, a.dtype) for a in self.srcs + self.zones],
            input_output_aliases={i: i for i in range(2 * n)},
            compiler_params=pltpu.CompilerParams(has_side_effects=DATAFLOW),
        )(*self.srcs, *self.zones, *self.sems, after)
        self.srcs = list(res[:n])
        return list(res[n:])


def split_gather(shards):
    n = len(shards)

    def plan(ins, lands):
        x, y, c, chips = _place()
        me = 2 * x + y
        sends, arrivals = [], []
        for w in range(n):
            for j, (ox, oy) in enumerate(chips):
                for k in range(2):
                    base = 2 * (3 * w + j)
                    sends.append((_half(ins[w], (), 0, c), _half(lands[w], (me,), 0, c), base + k, base + c, (ox, oy, k)))
                    arrivals.append((_half(lands[w], (2 * ox + oy,), 0, k), base + k))
            sends.append((ins[w], lands[w].at[me], 6 * n + w, 6 * n + w, (x, y, 1 - c)))
            arrivals.append((lands[w].at[me], 6 * n + w))
        return sends, arrivals

    return SplitExchange("gather", shards, [(4,) + s.shape for s in shards], 7 * n, plan)


def split_pair_swap(name, grads, axes):
    def plan(ins, lands):
        x, y, c, _ = _place()
        sends = [(_half(ins[w], (slice(None),), axes[w], 1 - c), lands[w], w, w, (x, y, 1 - c)) for w in range(len(ins))]
        return sends, [(lands[w], w) for w in range(len(ins))]

    halved = [tuple(d // 2 if i == 1 + ax else d for i, d in enumerate(g.shape)) for g, ax in zip(grads, axes)]
    return SplitExchange(name, grads, halved, len(grads), plan)


def split_chip_exchange(name, parts):
    def plan(ins, lands):
        x, y, c, chips = _place()
        sends, arrivals = [], []
        for w in range(len(ins)):
            for j, (ox, oy) in enumerate(chips):
                sends.append((ins[w].at[2 * ox + oy], lands[w].at[2 * x + y], 3 * w + j, 3 * w + j, (ox, oy, c)))
                arrivals.append((lands[w].at[2 * ox + oy], 3 * w + j))
        return sends, arrivals

    return SplitExchange(name, parts, [p.shape for p in parts], 3 * len(parts), plan)


def split_pair_send(halves):
    def plan(ins, lands):
        x, y, c, _ = _place()
        return ([(ins[w], lands[w], w, w, (x, y, 1 - c)) for w in range(len(ins))],
                [(lands[w], w) for w in range(len(ins))])

    return SplitExchange("pair_send", halves, [h.shape for h in halves], len(halves), plan)


def pair_send(halves):
    n = len(halves)

    def body(*refs):
        ins, outs = refs[:n], refs[n:2 * n]
        send, recv = refs[2 * n:]
        x, y, c, _ = _place()
        cps = [_remote(ins[w], outs[w], send.at[w], recv.at[w], (x, y, 1 - c)) for w in range(n)]
        for cp in cps:
            cp.start()
        for cp in cps:
            cp.wait_recv()
        for cp in cps:
            cp.wait_send()

    return pl.pallas_call(
        body, name="pair_send", in_specs=[ANY] * n, out_specs=[ANY] * n,
        out_shape=[jax.ShapeDtypeStruct(h.shape, h.dtype) for h in halves],
        scratch_shapes=[pltpu.SemaphoreType.DMA((n,))] * 2,
    )(*halves)


def all_reduce_small(name, vec, after=()):
    rows = vec.shape[0]

    def body(v_ref, *refs):
        o_ref, buf, send, recv = refs[len(after):]
        x, y, c, _ = _place()
        me = 4 * x + 2 * y + c
        buf[me] = v_ref[...]
        cps = []
        for k in range(1, 8):
            kx, ky, kc = (k >> 2) & 1, (k >> 1) & 1, k & 1
            peer = (x if kx == 0 else 1 - x, y if ky == 0 else 1 - y, c if kc == 0 else 1 - c)
            cp = _remote(v_ref, buf.at[me], send.at[k - 1], recv.at[k - 1], peer)
            cp.start()
            cps.append(cp)
        for k in range(1, 8):
            kx, ky, kc = (k >> 2) & 1, (k >> 1) & 1, k & 1
            px, py, pc = (x if kx == 0 else 1 - x, y if ky == 0 else 1 - y, c if kc == 0 else 1 - c)
            slot = buf.at[4 * px + 2 * py + pc]
            _remote(slot, slot, send.at[k - 1], recv.at[k - 1], (px, py, pc)).wait_recv()
        for cp in cps:
            cp.wait_send()
        acc = buf[0]
        for d in range(1, 8):
            acc = acc + buf[d]
        o_ref[...] = acc

    vm = pl.BlockSpec(memory_space=pltpu.VMEM)
    return pl.pallas_call(
        body, name=name, in_specs=[vm] + [ANY] * len(after), out_specs=vm, out_shape=jax.ShapeDtypeStruct(vec.shape, F32),
        scratch_shapes=[pltpu.VMEM((8, rows, LANES), F32), pltpu.SemaphoreType.DMA((7,)), pltpu.SemaphoreType.DMA((7,))],
    )(vec, *after)


class NoExchange:
    def __init__(self, late):
        self.late = late

    def late_weights(self, after):
        return self.late

    def reduce_start(self, grads):
        return jnp.zeros((8, LANES), F32)

    def reduce_exchange(self, after):
        return jnp.zeros((8, LANES), F32)

    def reduce_finish(self, after):
        return jnp.zeros((8, LANES), F32)

    def input_grad_start(self, dw_main, dw_small):
        return jnp.zeros((8, LANES), F32)

    def input_grad_exchange(self, after):
        return jnp.zeros((8, LANES), F32)


def local_step(x2, tgt2, g1, g2, gdn_ng, qn_g, kn_g, p1, p2, conv_w, wt_main, wt_small, hooks, nseq, seq):
    rows, dm = x2.shape
    wide = NH * LANES
    row = lambda a, off=0, w=None: (a, "row", off, a.shape[1] if w is None else w)
    rowh = lambda a, off=0, w=LANES: (a, "rowh", off, w)
    par = lambda a: (a, "par", 0, a.shape[1])
    parh = lambda a, off=0: (a, "parh", off, LANES)
    o_row = lambda w, dt: (w, "row", w, dt)
    o_rowh = lambda dt, tw=wide, w=LANES: (tw, "rowh", w, dt)

    u, = ew_fwd("rms1", f_rms, [row(x2), par(g1)], [o_row(dm, BF16)], rows)
    proj = matmul("mm_in", u, wt_main, "nt", BF16, tm=2048)
    sp = matmul("mm_in_small", u, wt_small, "nt", F32)
    so, = ew_fwd("small", f_small, [row(sp), par(p1), par(p2)], [o_row(LANES, F32)], rows)
    cs = cumsum_time("cumsum", so, nseq, seq, False)
    gb, bb, cb = ew_fwd("bcast", f_bcast, [row(so), row(cs)], [o_rowh(F32)] * 3, rows, NH)
    ct = transpose_time("c_time_major", cs, nseq, seq)
    conv = {}
    for mode, off in (("q", 0), ("k", NH), ("v", 2 * NH)):
        conv[mode], = ew_fwd(f"conv_{mode}", make_f_conv(mode), [rowh(proj, off), parh(conv_w, off)], [o_rowh(F32)],
                             rows, NH, seq, "hi", CONV_HEADS)
    val, kcum, attn, qdec, kdec, t_inv = gdn_a_fwd(conv["q"], conv["k"], conv["v"], gb, bb, rows)
    o_a, snaps = gdn_b_fwd(val, kcum, attn, qdec, kdec, gb, nseq, seq)
    ya_in, = ew_fwd("gdn_post", f_post, [rowh(o_a), rowh(proj, 3 * NH), par(gdn_ng)], [o_rowh(BF16)], rows, NH)
    fqn, = ew_fwd("fox_qn", f_rms, [rowh(proj, FOX_Q), par(qn_g)], [o_rowh(BF16)], rows, NH)
    fkn, = ew_fwd("fox_kn", f_rms, [rowh(proj, FOX_K), par(kn_g)], [o_rowh(BF16)], rows, NH)
    o_b, o_b16, lse = fox_fwd(fqn, fkn, proj, ct, nseq, seq)
    p_a, p_b, w_o, w_u, w_d = hooks.late_weights(o_a)
    y_a = matmul("mm_pa", ya_in, p_a, "nn", F32, tn=1024)
    y_b = matmul("mm_pb", o_b16, p_b, "nn", F32, tn=1024)
    gates = [row(proj, 7, dm), row(proj, 8, dm)]
    merged, = ew_fwd("merge", f_merge, gates + [row(y_a), row(y_b)], [o_row(dm, BF16)], rows)
    hres = matmul("mm_out", merged, w_o, "nn", F32, add=x2, tn=1024)
    hn, = ew_fwd("rms2", f_rms, [row(hres), par(g2)], [o_row(dm, BF16)], rows)
    up_blocks = w_u.shape[0]
    act, relu2 = matmul("mm_up", hn, w_u, "nn", F32, col_blocks=up_blocks, out_dtypes=[F32, BF16],
                        epilogue=lambda r: [r, jnp.maximum(r, 0.0) * jnp.maximum(r, 0.0)])
    def loss_tail(r, h_tile, t_tile):
        d = (r + h_tile) - t_tile
        e = (0.5 / dm) * (d * d)
        part = e.reshape(e.shape[0] // 8, 8, e.shape[1]).sum(axis=0)
        part = sum(part[:, t * LANES:(t + 1) * LANES] for t in range(e.shape[1] // LANES))
        g = d * (1.0 / dm)
        return [g, g, part]

    dout, dout16, loss_acc = matmul("mm_down", relu2, w_d, "nn", F32, extras=[hres, tgt2], epilogue=loss_tail,
                                    out_dtypes=[F32, BF16, F32], tile_sums=True)

    d_act = matmul("mm_d_act", dout16, w_d, "nt", BF16, extras=[act], epilogue=lambda r, a: [2.0 * jnp.maximum(a, 0.0) * r])
    dw_d = matmul("mm_dw_down", relu2, dout16, "tn", F32, tn=1024)
    dw_u = matmul("mm_dw_up", hn, d_act, "tn", F32, col_blocks=up_blocks)
    d_hn = matmul("mm_d_hn", d_act, w_u, "nt", F32, col_blocks=up_blocks)
    dh, dh16, dg2 = ew_bwd("rms2_b", f_rms, [row(hres), par(g2)], [(row(d_hn),)], [row(dout)],
                           lambda g, e: [g[0] + e[0], g[0] + e[0], g[1]],
                           [((rows, dm), "row", dm, F32, None), ((rows, dm), "row", dm, BF16, None), ((1, dm), "par", dm, F32, "all")], rows)
    d_merged = matmul("mm_d_merged", dh16, w_o, "nt", F32, tn=1024)
    dw_o = matmul("mm_dw_out", merged, dh16, "tn", F32, tn=1024)
    seg16 = ((rows, dm), "row", dm, BF16, None)
    d_ga16, d_gb16, d_ya16, d_yb16 = ew_bwd("merge_b", f_merge, gates + [row(y_a), row(y_b)], [(row(d_merged),)], [],
                                            lambda g, e: list(g), [seg16] * 4, rows)
    dp_a = matmul("mm_dp_a", ya_in, d_ya16, "tn", F32, tn=1024)
    d_ya_in = matmul("mm_d_ya_in", d_ya16, p_a, "nt", F32, tn=1024)
    dp_b = matmul("mm_dp_b", o_b16, d_yb16, "tn", F32, tn=1024)
    d_ob = matmul("mm_d_ob", d_yb16, p_b, "nt", F32, tn=1024)
    token = hooks.reduce_start(dict(p_a=dp_a, p_b=dp_b, w_o=dw_o, w_u=dw_u, w_d=dw_d))
    gdn_ng_t = gdn_ng + token[0, 0]
    h32 = ((rows, wide), "rowh", LANES, F32, None)
    h16 = ((rows, wide), "rowh", LANES, BF16, None)
    gain = ((1, LANES), "par", LANES, F32, "all")
    d_oa, d_z16, d_gdn_ng = ew_bwd("gdn_post_b", f_post, [rowh(o_a), rowh(proj, 3 * NH), par(gdn_ng_t)], [(rowh(d_ya_in),)], [],
                                   lambda g, e: list(g), [h32, h16, gain], rows, NH)
    dval, dkc, dat, dqd, dkd, dgb_b = gdn_b_bwd(val, kcum, attn, qdec, kdec, gb, snaps, d_oa, nseq, seq)
    d_cq, d_ck, d_cv, d_gb, d_bb = gdn_a_bwd(conv["q"], conv["k"], conv["v"], gb, bb, t_inv, dval, dkc, dat, dqd, dkd, dgb_b, rows)
    token = hooks.reduce_exchange(d_cq)
    conv_w_t = conv_w + token[0, 0]
    d_pre, d_conv = {}, {}
    tap = ((4, wide), "parh", LANES, F32, "inner")
    for mode, off, ctg in (("q", 0, d_cq), ("k", NH, d_ck), ("v", 2 * NH, d_cv)):
        d_pre[mode], d_conv[mode] = ew_bwd(f"conv_{mode}_b", make_f_conv(mode), [rowh(proj, off), parh(conv_w_t, off)],
                                           [(rowh(ctg),)], [], lambda g, e: list(g), [h16, tap], rows, NH, seq, "hi", CONV_HEADS)
    d_fqn, d_cq_b = fox_dq(fqn, fkn, proj, ct, d_ob, lse, o_b, [token], nseq, seq)
    d_fkn, d_fv16, d_ck_b = fox_dkv(fqn, fkn, proj, cb, d_ob, lse, o_b, [token], nseq, seq)
    token = hooks.reduce_finish(d_fkn)
    qn_g_t, kn_g_t = qn_g + token[0, 0], kn_g + token[0, 0]
    d_fq16, d_qn_g = ew_bwd("fox_qn_b", f_rms, [rowh(proj, FOX_Q), par(qn_g_t)], [(rowh(d_fqn),)], [], lambda g, e: list(g),
                            [h16, gain], rows, NH)
    d_fk16, d_kn_g = ew_bwd("fox_kn_b", f_rms, [rowh(proj, FOX_K), par(kn_g_t)], [(rowh(d_fkn),)], [], lambda g, e: list(g),
                            [h16, gain], rows, NH)
    narrow = ((rows, LANES), "row", LANES, F32, None)
    d_so, d_cs = ew_bwd("bcast_b", f_bcast, [row(so), row(cs)], [(rowh(d_gb),), (rowh(d_bb),), (rowh(d_cq_b), rowh(d_ck_b))], [],
                        lambda g, e: list(g), [narrow, narrow], rows, NH)
    d_logf = cumsum_time("cumsum_b", d_cs, nseq, seq, True)
    vec = ((1, LANES), "par", LANES, F32, "all")
    d_sp16, d_p1, d_p2 = ew_bwd("small_b", f_small, [row(sp), par(p1), par(p2)], [(row(d_so), row(d_logf))], [],
                                lambda g, e: list(g), [((rows, LANES), "row", LANES, BF16, None), vec, vec], rows)
    d_proj16 = jnp.concatenate([d_pre["q"], d_pre["k"], d_pre["v"], d_z16, d_fq16, d_fk16, d_fv16, d_ga16, d_gb16], axis=1)
    dw_main = matmul("mm_dw_main", d_proj16, u, "tn", F32)
    dw_small = matmul("mm_dw_small", d_sp16, u, "tn", F32)
    wt_small_t = wt_small + hooks.input_grad_start(dw_main, dw_small)[0, 0].astype(BF16)
    d_u = matmul("mm_d_u_small", d_sp16, wt_small_t, "nn", F32)
    d_u = matmul("mm_d_u_first", d_proj16, wt_main, "nn", F32, add=d_u, k_part=(0, 2))
    d_u = matmul("mm_d_u_second", d_proj16, wt_main, "nn", F32, add=d_u, k_part=(1, 2), after=[hooks.input_grad_exchange(d_u)])
    dx, dg1 = ew_bwd("rms1_b", f_rms, [row(x2), par(g1)], [(row(d_u),)], [row(dh)], lambda g, e: [g[0] + e[0], g[1]],
                     [((rows, dm), "row", dm, F32, None), ((1, dm), "par", dm, F32, "all")], rows)
    d_conv_w = jnp.concatenate([d_conv["q"], d_conv["k"], d_conv["v"]], axis=1)
    return dict(loss_acc=loss_acc, dx=dx, g1=dg1, g2=dg2, gdn_ng=d_gdn_ng, qn=d_qn_g, kn=d_kn_g, p1=d_p1, p2=d_p2,
                conv=d_conv_w, w_main=dw_main, w_small=dw_small, p_a=dp_a, p_b=dp_b, w_o=dw_o, w_u=dw_u, w_d=dw_d)


_W = NH * LANES
_A0, _A1 = 4 * _W, 4 * _W + 2 * NH
_B0, _B1 = _A1 + 3 * _W, _A1 + 3 * _W + NH


def _split_w_in(full_t):
    main = jnp.concatenate([full_t[:_A0], full_t[_A1:_B0], full_t[_B1:]], axis=0)
    small = jnp.concatenate([full_t[_A0:_A1], full_t[_B0:_B1], jnp.zeros((LANES - 3 * NH, full_t.shape[1]), full_t.dtype)], axis=0)
    return main, small


def _join_w_in(main, small):
    return jnp.concatenate([main[:_A0], small[:2 * NH], main[_A0:_A0 + 3 * _W], small[2 * NH:3 * NH], main[_A0 + 3 * _W:]], axis=0)


def _lanes(v, at=0):
    return jnp.pad(v.reshape(1, -1), ((0, 0), (at, LANES - at - v.size)))


def kernel(x, norm_mix_g, w_in, gdn_conv_w, gdn_a_log, gdn_dt_bias, gdn_norm_g, fox_q_norm_g, fox_k_norm_g, fox_f_bias, w_proj_gdn, w_proj_fox, w_out, norm_mlp_g, w_up, w_down, loss_target, m_norm_mix_g, m_w_in, m_gdn_conv_w, m_gdn_a_log, m_gdn_dt_bias, m_gdn_norm_g, m_fox_q_norm_g, m_fox_k_norm_g, m_fox_f_bias, m_w_proj_gdn, m_w_proj_fox, m_w_out, m_norm_mlp_g, m_w_up, m_w_down, v_norm_mix_g, v_w_in, v_gdn_conv_w, v_gdn_a_log, v_gdn_dt_bias, v_gdn_norm_g, v_fox_q_norm_g, v_fox_k_norm_g, v_fox_f_bias, v_w_proj_gdn, v_w_proj_fox, v_w_out, v_norm_mlp_g, v_w_up, v_w_down):
    nseq, seq, dm = x.shape
    rows = nseq * seq
    xi, yi, ci = lax.axis_index("x"), lax.axis_index("y"), lax.axis_index("c")
    chip = 2 * xi + yi
    conv_cols = gdn_conv_w.shape[2]

    tr = lambda a: jnp.swapaxes(a[0], 0, 1)
    big = [tr(w_in), w_proj_gdn[0], w_proj_fox[0], w_out[0], w_up[0], w_down[0]]
    axes = [1, 0, 0, 0, 0, 0]
    big16 = [w.astype(BF16) for w in big]
    conv_slot = jnp.zeros((4, 4, conv_cols), F32).at[:, chip].set(jnp.where(ci == 0, gdn_conv_w[0], 0.0))
    conv_full = all_reduce_small("gather_conv", conv_slot.reshape(-1, LANES)).reshape(4, 4 * conv_cols)
    got_in = gather_ring(big16[0])
    wt_main, wt_small = _split_w_in(got_in.reshape(-1, dm))
    core, chip_no = ci.reshape(1).astype(jnp.int32), chip.reshape(1).astype(jnp.int32)
    gather = split_gather(big16[1:])
    token = gather.start([got_in, conv_full])

    class Hooks:
        def late_weights(self, after):
            g_pa, g_pb, g_wo, w_u, g_wd = gather.wait(after)
            return (*(g.reshape(-1, dm) for g in (g_pa, g_pb, g_wo)), w_u, g_wd.reshape(-1, dm))

        def reduce_start(self, grads):
            blocks = [grads["p_a"].reshape(4, -1, dm), grads["p_b"].reshape(4, -1, dm), grads["w_o"].reshape(4, -1, dm),
                      grads["w_u"], grads["w_d"].reshape(4, -1, dm)]
            self.swap = split_pair_swap("pair_swap_late", blocks, axes[1:])
            return self.swap.start([])

        def reduce_exchange(self, after):
            swapped = self.swap.wait(after)
            self.exchange = split_chip_exchange("chip_exchange_late", add_pair("add_pair_late", self.swap.srcs, swapped, core, axes[1:]))
            return self.exchange.start([])

        def reduce_finish(self, after):
            slots = self.exchange.wait(after)
            self.send = split_pair_send(add_chips("add_chips_late", slots, self.exchange.srcs, chip_no, axes[1:]))
            return self.send.start([])

        def input_grad_start(self, dw_main, dw_small):
            self.in_swap = split_pair_swap("pair_swap_in", [_join_w_in(dw_main, dw_small).reshape(4, -1, dm)], axes[:1])
            return self.in_swap.start([])

        def input_grad_exchange(self, after):
            swapped = self.in_swap.wait(after)
            self.in_exchange = split_chip_exchange("chip_exchange_in", add_pair("add_pair_in", self.in_swap.srcs, swapped, core, axes[:1]))
            return self.in_exchange.start([])

    hooks = Hooks()
    p1 = _lanes(gdn_dt_bias[0]) + _lanes(fox_f_bias[0], 2 * NH)
    p2 = _lanes(gdn_a_log[0])

    g = local_step(x.reshape(rows, dm), loss_target.reshape(rows, dm), norm_mix_g + token[0, 0], norm_mlp_g, gdn_norm_g,
                   fox_q_norm_g, fox_k_norm_g, p1, p2, conv_full, wt_main, wt_small, hooks, nseq, seq)

    others = hooks.send.wait(g["dx"])
    big_m = [tr(m_w_in), m_w_proj_gdn[0], m_w_proj_fox[0], m_w_out[0], m_w_up[0], m_w_down[0]]
    big_v = [tr(v_w_in), v_w_proj_gdn[0], v_w_proj_fox[0], v_w_out[0], v_w_up[0], v_w_down[0]]
    names = ["w_in", "w_proj_gdn", "w_proj_fox", "w_out", "w_up", "w_down"]
    big_res, big_grad = {}, {}
    for i in range(1, len(names)):
        big_grad[names[i]], *big_res[names[i]] = adamw_halves(f"adamw_{names[i]}", big[i], hooks.send.srcs[i - 1], others[i - 1],
                                                              big_m[i], big_v[i], core, axes[i])
    slots = hooks.in_exchange.wait(big_res[names[-1]][0])
    mine = add_chips("add_chips_in", slots, hooks.in_exchange.srcs, chip_no, axes[:1])
    res = adamw_halves("adamw_w_in", big[0], mine[0], pair_send(mine)[0], big_m[0], big_v[0], core, axes[0])
    big_grad["w_in"], *big_res["w_in"] = [jnp.swapaxes(r, 0, 1) for r in res]

    small_parts = [g["loss_acc"], g["g1"].reshape(8, LANES), g["g2"].reshape(8, LANES), g["gdn_ng"], g["qn"], g["kn"], g["p1"], g["p2"],
                   g["conv"].reshape(-1, LANES)]
    tiled = [jnp.pad(p, ((0, -p.shape[0] % 8), (0, 0))) for p in small_parts]
    red = all_reduce_small("reduce_small", jnp.concatenate(tiled, axis=0), slots)
    pos, red_parts = 0, []
    for p, t in zip(small_parts, tiled):
        red_parts.append(red[pos:pos + p.shape[0]])
        pos += t.shape[0]
    r_loss, r_g1, r_g2, r_gdn_ng, r_qn, r_kn, r_p1, r_p2, r_conv = red_parts
    loss = jnp.sum(r_loss)
    g_conv = lax.dynamic_slice_in_dim(r_conv.reshape(4, 4, conv_cols), chip, 1, axis=1).reshape(4, conv_cols)
    small_grads = [r_g1.reshape(1, dm), r_p2[:, :NH], r_p1[:, :NH], r_gdn_ng, r_qn, r_kn, r_p1[:, 2 * NH:3 * NH], r_g2.reshape(1, dm)]
    small_w = [norm_mix_g, gdn_a_log, gdn_dt_bias, gdn_norm_g, fox_q_norm_g, fox_k_norm_g, fox_f_bias, norm_mlp_g]
    small_m = [m_norm_mix_g, m_gdn_a_log, m_gdn_dt_bias, m_gdn_norm_g, m_fox_q_norm_g, m_fox_k_norm_g, m_fox_f_bias, m_norm_mlp_g]
    small_v = [v_norm_mix_g, v_gdn_a_log, v_gdn_dt_bias, v_gdn_norm_g, v_fox_q_norm_g, v_fox_k_norm_g, v_fox_f_bias, v_norm_mlp_g]

    def pack(parts):
        flat = jnp.concatenate([jnp.pad(p.reshape(-1), (0, -p.size % LANES)) for p in parts])
        return jnp.pad(flat, (0, -flat.size % (8 * LANES))).reshape(-1, LANES)

    packed = adamw("adamw_small", pack(small_w + [gdn_conv_w[0]]), pack(small_grads + [g_conv]),
                   pack(small_m + [m_gdn_conv_w[0]]), pack(small_v + [v_gdn_conv_w[0]]))

    def unpack(flat2d):
        flat, pos, res = flat2d.reshape(-1), 0, []
        for p in small_w + [gdn_conv_w[0]]:
            res.append(flat[pos:pos + p.size].reshape(p.shape))
            pos += p.size + (-p.size % LANES)
        return res

    s_delta, s_m, s_v = (unpack(a) for a in packed)

    order = ["norm_mix_g", "w_in", "gdn_conv_w", "gdn_a_log", "gdn_dt_bias", "gdn_norm_g", "fox_q_norm_g", "fox_k_norm_g",
             "fox_f_bias", "w_proj_gdn", "w_proj_fox", "w_out", "norm_mlp_g", "w_up", "w_down"]
    small_names = ["norm_mix_g", "gdn_a_log", "gdn_dt_bias", "gdn_norm_g", "fox_q_norm_g", "fox_k_norm_g", "fox_f_bias", "norm_mlp_g",
                   "gdn_conv_w"]
    small_idx = {nm: i for i, nm in enumerate(small_names)}
    shapes = dict(zip(order, (a.shape for a in (norm_mix_g, w_in, gdn_conv_w, gdn_a_log, gdn_dt_bias, gdn_norm_g, fox_q_norm_g,
                                                 fox_k_norm_g, fox_f_bias, w_proj_gdn, w_proj_fox, w_out, norm_mlp_g, w_up, w_down))))
    grads_out, delta_out, m_out, v_out = [], [], [], []
    for nm in order:
        if nm in big_res:
            d, mm, vv = big_res[nm]
            gr = big_grad[nm]
        else:
            i = small_idx[nm]
            gr = (small_grads + [g_conv])[i]
            d, mm, vv = s_delta[i], s_m[i], s_v[i]
        for lst, val in ((grads_out, gr), (delta_out, d), (m_out, mm), (v_out, vv)):
            lst.append(val.reshape(shapes[nm]))
    return (loss, g["dx"].reshape(x.shape), *grads_out, *delta_out, *m_out, *v_out)
```

```python
import functools

import jax
import jax.numpy as jnp
from jax import lax
from jax.experimental import pallas as pl
from jax.experimental.pallas import tpu as pltpu

F32 = jnp.float32
BF16 = jnp.bfloat16
LANES = 128
NH = 8
EPS = 1e-6
GDN_CHUNK = 64
GDN_ROWS = 256
GDN_BASE = 16
ROW_TILE = 512
CONV_HEADS = 2
ATT_TILE = 512
NEG = -1e30
VMEM_LIMIT_BYTES = 58 * 1024 * 1024
LO = lax.Precision.DEFAULT
MESH = pl.DeviceIdType.MESH
ANY = pl.BlockSpec(memory_space=pl.ANY)

ADAM_LR, ADAM_B1, ADAM_B2, ADAM_EPS, ADAM_WD, ADAM_STEP = 0.001, 0.9, 0.999, 1e-08, 0.01, 10


def _params(n_grid):
    return pltpu.CompilerParams(dimension_semantics=("arbitrary",) * n_grid,
                                vmem_limit_bytes=VMEM_LIMIT_BYTES)


def _dot(a, b, dims, precision=None):
    dn = {"nn": (((1,), (0,)), ((), ())), "nt": (((1,), (1,)), ((), ())), "tn": (((0,), (0,)), ((), ()))}[dims]
    return lax.dot_general(a, b, dn, precision=precision, preferred_element_type=F32)


def _iota(shape, dim):
    return lax.broadcasted_iota(jnp.int32, shape, dim)


def _split(x, parts):
    out = []
    for _ in range(parts - 1):
        hi = x.astype(BF16)
        out.append(hi)
        x = x - hi.astype(F32)
    return out + [x.astype(BF16)]


def _dot_mask(mask, b, dims, terms=3):
    m16 = mask.astype(BF16)
    acc = None
    for part in reversed(_split(b, terms)):
        prod = _dot(m16, part, dims)
        acc = prod if acc is None else acc + prod
    return acc


@jax.custom_vjp
def mm_mask(mask, b):
    return _dot_mask(mask, b, "nn", 2)


mm_mask.defvjp(lambda mask, b: (_dot_mask(mask, b, "nn", 2), mask),
               lambda mask, g: (jnp.zeros_like(mask), _dot_mask(mask, g, "tn", 2)))


def matmul(name, a, b, dims, out_dtype, add=None, tm=1024, tn=1024, tk=512, col_blocks=None,
           extras=(), epilogue=None, out_dtypes=None, k_part=None, after=(), tile_sums=False):
    if col_blocks and dims != "tn":
        nb, b_rows, bw = b.shape
        b_shape = (b_rows, nb * bw)
    else:
        b_shape = b.shape
    if dims == "nn":
        (m, k), (_, n) = a.shape, b_shape
    elif dims == "nt":
        (m, k), (n, _) = a.shape, b_shape
    else:
        (k, m), (_, n) = a.shape, b_shape
    k_span = k // (k_part[1] if k_part else 1)
    if col_blocks and dims == "nt":
        k_span = min(k_span, bw)
    tk = k if k <= 1024 else max(t for t in (2048, 1536, 1024, 512, tk) if k_span % t == 0)
    tm, tn, tk = min(tm, m), min(tn, n), min(tk, k)
    assert m % tm == 0 and n % tn == 0 and k % tk == 0, (name, m, n, k)
    k0, nk = (0, k // tk) if k_part is None else (k_part[0] * (k // tk // k_part[1]), k // tk // k_part[1])
    assert k_part is None or (dims == "nn" and not col_blocks and (k // tk) % k_part[1] == 0)
    a_spec = pl.BlockSpec((tk, tm), lambda i, j, kk: (kk, i)) if dims == "tn" else pl.BlockSpec((tm, tk), lambda i, j, kk: (i, kk + k0))
    b_spec = pl.BlockSpec((tn, tk), lambda i, j, kk: (j, kk)) if dims == "nt" else pl.BlockSpec((tk, tn), lambda i, j, kk: (kk + k0, j))
    o_spec = pl.BlockSpec((tm, tn), lambda i, j, kk: (i, j))
    out_shape = (m, n)
    if col_blocks and dims == "nn":
        per = bw // tn
        assert bw % tn == 0
        b_spec = pl.BlockSpec((None, tk, tn), lambda i, j, kk: (j // per, kk, j % per))
    elif col_blocks and dims == "nt":
        per = bw // tk
        assert bw % tk == 0
        b_spec = pl.BlockSpec((None, tn, tk), lambda i, j, kk: (kk // per, j, kk % per))
    elif col_blocks:
        bw = n // col_blocks
        per = bw // tn
        assert bw % tn == 0 and add is None
        o_spec = pl.BlockSpec((None, tm, tn), lambda i, j, kk: (j // per, i, j % per))
        out_shape = (col_blocks, m, bw)
    extras = list(extras) + ([add] if add is not None else [])
    if add is not None:
        assert epilogue is None
        epilogue = lambda r, *e: [r + e[-1]]
    out_dtypes = [out_dtype] if epilogue is None or out_dtypes is None else list(out_dtypes)
    n_ex, n_out = len(extras), len(out_dtypes)

    def body(*refs):
        a_ref, b_ref = refs[0], refs[1]
        ex_refs, o_refs = refs[2:2 + n_ex], refs[2 + n_ex + len(after):2 + n_ex + len(after) + n_out]

        def finish(r):
            res = [r] if epilogue is None else epilogue(r, *[e[...] for e in ex_refs])
            for o_ref, v in zip(o_refs, res):
                o_ref[...] = v.astype(o_ref.dtype)

        if nk == 1:
            finish(_dot(a_ref[...], b_ref[...], dims))
            return
        acc_ref = refs[-1]
        kk = pl.program_id(2)

        @pl.when(kk == 0)
        def _():
            acc_ref[...] = jnp.zeros_like(acc_ref)

        acc_ref[...] += _dot(a_ref[...], b_ref[...], dims)

        @pl.when(kk == nk - 1)
        def _():
            finish(acc_ref[...])

    out_specs = [o_spec] * n_out
    out_shapes = [jax.ShapeDtypeStruct(out_shape, dt) for dt in out_dtypes]
    if tile_sums:
        out_specs[-1] = pl.BlockSpec((8, LANES), lambda i, j, kk: (i, j))
        out_shapes[-1] = jax.ShapeDtypeStruct((8 * (m // tm), LANES * (n // tn)), out_dtypes[-1])
    res = pl.pallas_call(
        body, name=name, grid=(m // tm, n // tn, nk), in_specs=[a_spec, b_spec] + [o_spec] * n_ex + [ANY] * len(after),
        out_specs=out_specs, out_shape=out_shapes,
        scratch_shapes=[pltpu.VMEM((tm, tn), F32)] if nk > 1 else [], compiler_params=_params(3),
    )(a, b, *extras, *after)
    return res[0] if n_out == 1 else res


def _ew_spec(kind, off, width, tb, hp, order, shape=None):
    def ih(g0, g1):
        return (g0, g1) if order == "ih" else (g1, g0)

    assert off % hp == 0 or kind in ("row", "par")
    if kind == "row":
        return pl.BlockSpec((tb, width), lambda g0, g1: (ih(g0, g1)[0], off))
    if kind == "rowh":
        return pl.BlockSpec((tb, hp * width), lambda g0, g1: (ih(g0, g1)[0], ih(g0, g1)[1] + off // hp))
    if kind == "par":
        return pl.BlockSpec(shape, lambda g0, g1: (0, 0))
    if kind == "parh":
        return pl.BlockSpec((shape[0], hp * width), lambda g0, g1: (0, ih(g0, g1)[1] + off // hp))
    raise ValueError(kind)


def _ew_grid(rows, tb, nh, hp, order):
    assert nh % hp == 0 and rows % tb == 0
    return (rows // tb, nh // hp) if order == "ih" else (nh // hp, rows // tb)


def _ew_load(ref, kind, width, hh):
    if kind in ("row", "par"):
        return ref[...].astype(F32)
    return ref[:, hh * width:(hh + 1) * width].astype(F32)


def ew_fwd(name, f, ins, outs, rows, nh=1, tb=ROW_TILE, order="ih", hp=None, after=()):
    hp = nh if hp is None else hp
    n_in = len(ins)

    def body(*refs):
        hb = pl.program_id(1) if order == "ih" else pl.program_id(0)
        for hh in range(hp):
            h = hh if hp == nh else hb * hp + hh
            vals = [_ew_load(r, kd, w, hh) for r, (_, kd, _, w) in zip(refs[:n_in], ins)]
            res = f(h, *vals)
            for r, v, (_, kd, w, _) in zip(refs[n_in + len(after):], res, outs):
                if kd == "row":
                    assert hp == 1
                    r[...] = v.astype(r.dtype)
                else:
                    r[:, hh * w:(hh + 1) * w] = v.astype(r.dtype)

    in_specs = [_ew_spec(kd, off, w, tb, hp, order, a.shape) for (a, kd, off, w) in ins]
    out_specs = [_ew_spec(kd, 0, w, tb, hp, order) for (_, kd, w, _) in outs]
    out_shape = [jax.ShapeDtypeStruct((rows, tw), dt) for (tw, _, _, dt) in outs]
    return pl.pallas_call(
        body, name=name, grid=_ew_grid(rows, tb, nh, hp, order), in_specs=in_specs + [ANY] * len(after), out_specs=out_specs,
        out_shape=out_shape, compiler_params=_params(2),
    )(*[a for (a, _, _, _) in ins], *after)


def ew_bwd(name, f, ins, cts, extras, emit, outs, rows, nh=1, tb=ROW_TILE, order="ih", hp=None):
    hp = nh if hp is None else hp
    n_in = len(ins)
    flat_cts = [d for group in cts for d in group]
    n_ct, n_ex = len(flat_cts), len(extras)

    def body(*refs):
        g0, g1 = pl.program_id(0), pl.program_id(1)
        hb = g1 if order == "ih" else g0
        out_refs = refs[n_in + n_ct + n_ex:]
        shared = [None] * len(outs)

        def store(r, v, first, sl=None):
            def put(val, add):
                if sl is None:
                    r[...] = (r[...] + val if add else val).astype(r.dtype)
                else:
                    r[:, sl] = (r[:, sl] + val if add else val).astype(r.dtype)

            if first is None:
                put(v, False)
            else:
                pl.when(first)(lambda: put(v, False))
                pl.when(jnp.logical_not(first))(lambda: put(v, True))

        for hh in range(hp):
            h = hh if hp == nh else hb * hp + hh
            vals = [_ew_load(r, kd, w, hh) for r, (_, kd, _, w) in zip(refs[:n_in], ins)]
            ct_refs = list(zip(refs[n_in:n_in + n_ct], flat_cts))
            ct_vals, pos = [], 0
            for group in cts:
                v = None
                for r, (_, kd, _, w) in ct_refs[pos:pos + len(group)]:
                    t = _ew_load(r, kd, w, hh)
                    v = t if v is None else v + t
                pos += len(group)
                ct_vals.append(v)
            ex_vals = [_ew_load(r, kd, w, hh) for r, (_, kd, _, w) in zip(refs[n_in + n_ct:n_in + n_ct + n_ex], extras)]
            _, vjp = jax.vjp(lambda *a: f(h, *a), *vals)
            res = emit(vjp(tuple(ct_vals)), ex_vals)
            for idx, (r, v, (_, kd, w, _, acc)) in enumerate(zip(out_refs, res, outs)):
                if kd in ("row", "par"):
                    shared[idx] = v if shared[idx] is None else shared[idx] + v
                else:
                    store(r, v, (g1 == 0) if acc == "inner" else None, slice(hh * w, (hh + 1) * w))
        for idx, (r, (_, kd, _, _, acc)) in enumerate(zip(out_refs, outs)):
            if kd in ("row", "par"):
                assert acc == "all" or hp == nh
                store(r, shared[idx], jnp.logical_and(g0 == 0, g1 == 0) if acc == "all" else None)

    operands = list(ins) + flat_cts + list(extras)
    in_specs = [_ew_spec(kd, off, w, tb, hp, order, a.shape) for (a, kd, off, w) in operands]
    out_specs = [_ew_spec(kd, 0, w, tb, hp, order, shp) for (shp, kd, w, _, _) in outs]
    out_shape = [jax.ShapeDtypeStruct(shp, dt) for (shp, _, _, dt, _) in outs]
    return pl.pallas_call(
        body, name=name, grid=_ew_grid(rows, tb, nh, hp, order), in_specs=in_specs, out_specs=out_specs,
        out_shape=out_shape, compiler_params=_params(2),
    )(*[a for (a, _, _, _) in operands])


def f_rms(h, x, g):
    r = lax.rsqrt(jnp.mean(x * x, axis=-1, keepdims=True) + EPS)
    return (x * r * g,)


def _softplus(z):
    return jnp.maximum(z, 0.0) + jnp.log1p(jnp.exp(-jnp.abs(z)))


def f_small(h, sp, p1, p2):
    lane = _iota(sp.shape, 1)
    z = sp + p1
    g = -jnp.exp(p2) * _softplus(z)
    beta = jax.nn.sigmoid(z)
    logf = -_softplus(-z)
    return (jnp.where(lane < NH, g, jnp.where(lane < 2 * NH, beta, jnp.where(lane < 3 * NH, logf, 0.0))),)


def _pick(x, lane_id):
    lane = _iota(x.shape, 1)
    col = jnp.sum(jnp.where(lane == lane_id, x, 0.0), axis=1, keepdims=True)
    return jnp.broadcast_to(col, x.shape)


def f_bcast(h, so, cs):
    return _pick(so, h), _pick(so, h + NH), _pick(cs, h + 2 * NH)


def _shift_down(s):
    def down(x):
        r = pltpu.roll(x, s, 0)
        head = jnp.where(_iota((8, x.shape[1]), 0) >= s, r[:8], 0.0)
        return jnp.concatenate([head, r[8:]], axis=0)

    def up(g):
        n = g.shape[0]
        r = pltpu.roll(g, n - s, 0)
        tail = jnp.where(_iota((8, g.shape[1]), 0) < 8 - s, r[n - 8:], 0.0)
        return jnp.concatenate([r[:n - 8], tail], axis=0)

    @jax.custom_vjp
    def shift(x):
        return down(x)

    shift.defvjp(lambda x: (down(x), None), lambda _, g: (up(g),))
    return shift


def _silu(x):
    return x * jax.nn.sigmoid(x)


def make_f_conv(mode):
    sh1, sh2, sh3 = _shift_down(1), _shift_down(2), _shift_down(3)

    def f(h, x, w):
        sub = _iota(w.shape, 0)

        def tap(i):
            return jnp.sum(jnp.where(sub == i, w, 0.0), axis=0, keepdims=True)

        y = sh3(x) * tap(0)
        y = y + sh2(x) * tap(1)
        y = y + sh1(x) * tap(2)
        y = y + x * tap(3)
        s = _silu(y)
        if mode == "v":
            return (s,)
        n = s * lax.rsqrt(jnp.sum(s * s, axis=-1, keepdims=True) + EPS)
        if mode == "q":
            n = n * (LANES ** -0.5)
        return (n,)

    return f


def f_post(h, o, z, g):
    r = lax.rsqrt(jnp.mean(o * o, axis=-1, keepdims=True) + EPS)
    return (o * r * g * _silu(z),)


def f_merge(h, ga, gb, ya, yb):
    return (jax.nn.sigmoid(ga) * ya + jax.nn.sigmoid(gb) * yb,)


def cumsum_time(name, x, nseq, seq, reverse):
    nb = seq // LANES

    def body(x_ref, o_ref):
        r, c = _iota((LANES, LANES), 0), _iota((LANES, LANES), 1)
        tri = jnp.where((r <= c) if reverse else (r >= c), 1.0, 0.0).astype(F32)
        carry = jnp.zeros((1, LANES), F32)
        for b in (range(nb - 1, -1, -1) if reverse else range(nb)):
            blk = x_ref[b * LANES:(b + 1) * LANES, :]
            o_ref[b * LANES:(b + 1) * LANES, :] = _dot_mask(tri, blk, "nn") + carry
            carry = carry + jnp.sum(blk, axis=0, keepdims=True)

    spec = pl.BlockSpec((seq, LANES), lambda s: (s, 0))
    return pl.pallas_call(body, name=name, grid=(nseq,), in_specs=[spec], out_specs=spec,
                          out_shape=jax.ShapeDtypeStruct(x.shape, F32), compiler_params=_params(1))(x)


def transpose_time(name, x, nseq, seq):
    def body(x_ref, o_ref):
        o_ref[...] = x_ref[...].T

    return pl.pallas_call(
        body, name=name, grid=(nseq,), in_specs=[pl.BlockSpec((seq, LANES), lambda s: (s, 0))],
        out_specs=pl.BlockSpec((LANES, seq), lambda s: (s, 0)),
        out_shape=jax.ShapeDtypeStruct((nseq * LANES, seq), F32), compiler_params=_params(1))(x)


def _gdn_masks():
    n = GDN_ROWS
    r, c = _iota((n, n), 0), _iota((n, n), 1)
    shift = GDN_CHUNK.bit_length() - 1
    same = lax.shift_right_logical(r, shift) == lax.shift_right_logical(c, shift)
    return r, c, same


def _each(fn, *lists):
    return [fn(*xs) for xs in zip(*lists)]


def _gdn_decay(gbs):
    r, c, same = _gdn_masks()
    seg_tril = jnp.where(jnp.logical_and(same, r >= c), 1.0, 0.0).astype(F32)
    g_cum = _each(lambda gb: mm_mask(seg_tril, gb), gbs)
    lane0 = _iota(gbs[0].shape, 1) == 0
    g_col = _each(lambda g: jnp.sum(jnp.where(lane0, g, 0.0), axis=1, keepdims=True), g_cum)
    g_row = _each(lambda g: jnp.sum(jnp.where(r == c, jnp.broadcast_to(g, (GDN_ROWS, GDN_ROWS)), 0.0), axis=0, keepdims=True), g_col)
    return g_cum, _each(lambda a, b: a - b, g_col, g_row)


def gdn_a_mats(ks, bbs, diff):
    r, c, same = _gdn_masks()
    strict = jnp.logical_and(same, r > c)
    lane0 = _iota(bbs[0].shape, 1) == 0
    beta_col = _each(lambda bb: jnp.sum(jnp.where(lane0, bb, 0.0), axis=1, keepdims=True), bbs)
    kk = _each(lambda k: _dot(k, k, "nt", LO), ks)
    return _each(lambda b, x, d: jnp.where(strict, b * x * jnp.exp(jnp.where(strict, d, 0.0)), 0.0), beta_col, kk, diff)


@jax.custom_vjp
def saved_inverse(a, t_corr):
    return t_corr


def _saved_inverse_bwd(t, dt):
    left = dt + _dot(t, dt, "tn", LO)
    return -(left + _dot(left, t, "nt", LO)), jnp.zeros_like(t)


saved_inverse.defvjp(lambda a, t_corr: (t_corr, t_corr), _saved_inverse_bwd)


def gdn_block(*args):
    ts, qs, ks, vs, gbs, bbs = (list(args[i::6]) for i in range(6))
    g_cum, diff = _gdn_decay(gbs)
    ts = _each(saved_inverse, gdn_a_mats(ks, bbs, diff), ts)
    return gdn_outputs(ts, qs, ks, vs, gbs, bbs, g_cum, diff)


def gdn_outputs(ts, qs, ks, vs, gbs, bbs, g_cum, diff):
    r, c, same = _gdn_masks()
    incl = jnp.logical_and(same, r >= c)
    decay = _each(lambda d: jnp.where(incl, jnp.exp(jnp.where(incl, d, 0.0)), 0.0), diff)
    e_g = _each(jnp.exp, g_cum)
    v_beta = _each(lambda v, bb: v * bb, vs, bbs)
    k_beta = _each(lambda k, bb, e: k * bb * e, ks, bbs, e_g)
    value = _each(lambda t, x: x + _dot(t, x, "nn", LO), ts, v_beta)
    k_cum = _each(lambda t, x: x + _dot(t, x, "nn", LO), ts, k_beta)
    attn = _each(lambda q, k, d: _dot(q, k, "nt", LO) * d, qs, ks, decay)
    ones = jnp.where(same, 1.0, 0.0).astype(F32)
    g_last = _each(lambda gb: mm_mask(ones, gb), gbs)
    q_dec = _each(lambda q, e: q * e, qs, e_g)
    k_dec = _each(lambda k, gl, g: k * jnp.exp(gl - g), ks, g_last, g_cum)
    return tuple(x for head in zip(value, k_cum, attn, q_dec, k_dec) for x in head)


def tri_inverse(mats):
    n = GDN_ROWS
    r, c = _iota((n, n), 0), _iota((n, n), 1)
    shift = GDN_BASE.bit_length() - 1
    blk = lax.shift_right_logical(r, shift) == lax.shift_right_logical(c, shift)
    each = lambda fn, *lists: [fn(*xs) for xs in zip(*lists)]
    mm = lambda x, y: _dot(x, y, "nn", LO)
    d = each(lambda a: jnp.where(blk, a, 0.0), mats)
    lo = each(lambda a, dd: a - dd, mats, d)
    p = each(lambda dd: -dd, d)
    c_d = p
    for _ in range(shift - 1):
        p = each(mm, p, p)
        c_d = each(lambda cd, pp, prod: cd + pp + prod, c_d, p, each(mm, c_d, p))
    assert GDN_CHUNK // GDN_BASE == 4
    nmat = each(lambda l, prod: l + prod, lo, each(mm, c_d, lo))
    n2 = each(mm, nmat, nmat)
    c_n = each(lambda nn2, nm, prod: (nn2 - nm) - prod, n2, nmat, each(mm, nmat, n2))
    return each(lambda cn, cd, prod: cn + cd + prod, c_n, c_d, each(mm, c_n, c_d))


GDN_AHP = 8


def _gdn_a_specs():
    blk = pl.BlockSpec((GDN_ROWS, GDN_AHP * LANES), lambda i, h: (i, h))
    sq = pl.BlockSpec((GDN_ROWS, GDN_AHP * GDN_ROWS), lambda i, h: (i, h))
    return blk, sq


def _head(ref, hh):
    width = ref.shape[1] // GDN_AHP
    return ref.at[:, hh * width:(hh + 1) * width]


def gdn_a_fwd(q, k, v, gb, bb, rows):
    blk, sq = _gdn_a_specs()

    def body(q_ref, k_ref, v_ref, gb_ref, bb_ref, val_ref, kc_ref, at_ref, qd_ref, kd_ref, t_ref):
        heads = [[_head(r, hh)[...] for r in (q_ref, k_ref, v_ref, gb_ref, bb_ref)] for hh in range(GDN_AHP)]
        qs, ks, vs, gbs, bbs = (list(col) for col in zip(*heads))
        g_cum, diff = _gdn_decay(gbs)
        t_corr = tri_inverse(gdn_a_mats(ks, bbs, diff))
        res = gdn_outputs(t_corr, qs, ks, vs, gbs, bbs, g_cum, diff)
        for hh in range(GDN_AHP):
            for r, x in zip((val_ref, kc_ref, at_ref, qd_ref, kd_ref, t_ref), (*res[5 * hh:5 * hh + 5], t_corr[hh])):
                _head(r, hh)[...] = x.astype(r.dtype)

    wide = lambda dt: jax.ShapeDtypeStruct((rows, NH * LANES), dt)
    square = jax.ShapeDtypeStruct((rows, NH * GDN_ROWS), BF16)
    return pl.pallas_call(
        body, name="gdn_a_fwd", grid=(rows // GDN_ROWS, NH // GDN_AHP), in_specs=[blk] * 5,
        out_specs=[blk, blk, sq, blk, blk, sq], out_shape=[wide(F32), wide(BF16), square, wide(BF16), wide(BF16), square],
        compiler_params=_params(2))(q, k, v, gb, bb)


def gdn_a_bwd(q, k, v, gb, bb, t_inv, dval, dkc, dat, dqd, dkd, dgb_b, rows):
    blk, sq = _gdn_a_specs()

    def body(q_ref, k_ref, v_ref, gb_ref, bb_ref, t_ref, dval_ref, dkc_ref, dat_ref, dqd_ref, dkd_ref, dgbb_ref,
             dq_ref, dk_ref, dv_ref, dgb_ref, dbb_ref):
        hs = range(GDN_AHP)
        heads = [[_head(r, hh)[...] for r in (q_ref, k_ref, v_ref, gb_ref, bb_ref)] for hh in hs]
        tvs = [_head(t_ref, hh)[...].astype(F32) for hh in hs]
        _, vjp = jax.vjp(gdn_block, *[x for t, head in zip(tvs, heads) for x in (t, *head)])
        grads = vjp(tuple(_head(r, hh)[...] for hh in hs for r in (dval_ref, dkc_ref, dat_ref, dqd_ref, dkd_ref)))
        for hh in hs:
            _, dq, dk, dv, dgb, dbb = grads[6 * hh:6 * hh + 6]
            _head(dq_ref, hh)[...] = dq
            _head(dk_ref, hh)[...] = dk
            _head(dv_ref, hh)[...] = dv
            _head(dgb_ref, hh)[...] = dgb + _head(dgbb_ref, hh)[...]
            _head(dbb_ref, hh)[...] = dbb

    wide = jax.ShapeDtypeStruct((rows, NH * LANES), F32)
    return pl.pallas_call(
        body, name="gdn_a_bwd", grid=(rows // GDN_ROWS, NH // GDN_AHP),
        in_specs=[blk] * 5 + [sq, blk, blk, sq, blk, blk, blk], out_specs=[blk] * 5, out_shape=[wide] * 5,
        compiler_params=_params(2))(q, k, v, gb, bb, t_inv, dval, dkc, dat, dqd, dkd, dgb_b)


N_CH = GDN_ROWS // GDN_CHUNK


GDN_HP = 8


def gdn_chunk(c):
    def f(*args):
        val, kc, at, qd, kd, gb, s = (list(args[i::7]) for i in range(7))
        zero = jnp.zeros((GDN_CHUNK, LANES), F32)
        v_new = _each(lambda v, k, st: v - _dot(k, st, "nn", LO), val, kc, s)
        v_pad = _each(lambda v: jnp.concatenate([zero] * c + [v] + [zero] * (N_CH - 1 - c), axis=0), v_new)
        out = _each(lambda q, st, a, vp: _dot(q, st, "nn", LO) + _dot(a, vp, "nn", LO), qd, s, at, v_pad)
        dec = _each(lambda g: jnp.exp(jnp.sum(g, axis=0, keepdims=True)), gb)
        s_new = _each(lambda st, d, k, v: st * d + _dot(k, v, "tn", LO), s, dec, kd, v_new)
        return tuple(x for head in zip(out, s_new) for x in head)

    return f


def _gdn_piece(ref, hh, c):
    width = ref.shape[1] // GDN_HP
    return ref.at[c * GDN_CHUNK:(c + 1) * GDN_CHUNK, hh * width:(hh + 1) * width]


def _gdn_snap(ref, hh, c):
    row = (hh * N_CH + c) * LANES
    return ref.at[row:row + LANES, :]


def _gdn_b_specs(nb, rev):
    def blk_row(s, j):
        return s * nb + (nb - 1 - j if rev else j)

    blk = pl.BlockSpec((GDN_ROWS, GDN_HP * LANES), lambda s, hb, j: (blk_row(s, j), hb))
    sq = pl.BlockSpec((GDN_ROWS, GDN_HP * GDN_ROWS), lambda s, hb, j: (blk_row(s, j), hb))
    snap = pl.BlockSpec((GDN_HP * N_CH * LANES, LANES), lambda s, hb, j: (blk_row(s, j) * (NH // GDN_HP) + hb, 0))
    return blk, sq, snap


def gdn_b_fwd(val, kc, at, qd, kd, gb, nseq, seq):
    nb = seq // GDN_ROWS
    rows = nseq * seq
    blk, sq, snap = _gdn_b_specs(nb, False)

    def body(val_ref, kc_ref, at_ref, qd_ref, kd_ref, gb_ref, o_ref, snap_ref, s_ref):
        @pl.when(pl.program_id(2) == 0)
        def _():
            s_ref[...] = jnp.zeros_like(s_ref)

        hs = range(GDN_HP)
        states = [s_ref[hh] for hh in hs]
        for c in range(N_CH):
            for hh in hs:
                _gdn_snap(snap_ref, hh, c)[...] = states[hh]
            res = gdn_chunk(c)(*[x for hh in hs for x in (
                *[_gdn_piece(r, hh, c)[...].astype(F32) for r in (val_ref, kc_ref, at_ref, qd_ref, kd_ref, gb_ref)], states[hh])])
            for hh in hs:
                _gdn_piece(o_ref, hh, c)[...] = res[2 * hh]
            states = [res[2 * hh + 1] for hh in hs]
        for hh in hs:
            s_ref[hh] = states[hh]

    return pl.pallas_call(
        body, name="gdn_b_fwd", grid=(nseq, NH // GDN_HP, nb), in_specs=[blk, blk, sq, blk, blk, blk], out_specs=[blk, snap],
        out_shape=[jax.ShapeDtypeStruct((rows, NH * LANES), F32),
                   jax.ShapeDtypeStruct((nseq * nb * NH * N_CH * LANES, LANES), F32)],
        scratch_shapes=[pltpu.VMEM((GDN_HP, LANES, LANES), F32)], compiler_params=_params(3))(val, kc, at, qd, kd, gb)


def gdn_b_bwd(val, kc, at, qd, kd, gb, snaps, do, nseq, seq):
    nb = seq // GDN_ROWS
    rows = nseq * seq
    blk, sq, snap = _gdn_b_specs(nb, True)

    def body(val_ref, kc_ref, at_ref, qd_ref, kd_ref, gb_ref, snap_ref, do_ref,
             dval_ref, dkc_ref, dat_ref, dqd_ref, dkd_ref, dgb_ref, ds_ref):
        @pl.when(pl.program_id(2) == 0)
        def _():
            ds_ref[...] = jnp.zeros_like(ds_ref)

        hs = range(GDN_HP)
        d_states = [ds_ref[hh] for hh in hs]
        for c in reversed(range(N_CH)):
            _, vjp = jax.vjp(gdn_chunk(c), *[x for hh in hs for x in (
                *[_gdn_piece(r, hh, c)[...].astype(F32) for r in (val_ref, kc_ref, at_ref, qd_ref, kd_ref, gb_ref)],
                _gdn_snap(snap_ref, hh, c)[...])])
            grads = vjp(tuple(x for hh in hs for x in (_gdn_piece(do_ref, hh, c)[...], d_states[hh])))
            for hh in hs:
                for i, r in enumerate([dval_ref, dkc_ref, dat_ref, dqd_ref, dkd_ref, dgb_ref]):
                    _gdn_piece(r, hh, c)[...] = grads[7 * hh + i]
            d_states = [grads[7 * hh + 6] for hh in hs]
        for hh in hs:
            ds_ref[hh] = d_states[hh]

    wide = jax.ShapeDtypeStruct((rows, NH * LANES), F32)
    square = jax.ShapeDtypeStruct((rows, NH * GDN_ROWS), F32)
    return pl.pallas_call(
        body, name="gdn_b_bwd", grid=(nseq, NH // GDN_HP, nb), in_specs=[blk, blk, sq, blk, blk, blk, snap, blk],
        out_specs=[blk, blk, sq, blk, blk, blk], out_shape=[wide, wide, square, wide, wide, wide],
        scratch_shapes=[pltpu.VMEM((GDN_HP, LANES, LANES), F32)], compiler_params=_params(3))(val, kc, at, qd, kd, gb, snaps, do)


FOX_Q, FOX_K, FOX_V = 4 * NH, 5 * NH, 6 * NH
FOX_SCALE = LANES ** -0.5


def _head_row(ct_ref, h, off, width):
    blk = ct_ref[:, pl.ds(off, width)]
    return jnp.sum(jnp.where(_iota(blk.shape, 0) == h, blk, 0.0), axis=0, keepdims=True)


def _col(x):
    return jnp.max(x, axis=1, keepdims=True)


def _row(x):
    return jnp.max(x.T, axis=0, keepdims=True)


def _causal(shape, q_dim):
    return _iota(shape, q_dim) >= _iota(shape, 1 - q_dim)


FOX_HP = 4


def _fox_specs(seq, tile, n_tiles):
    tblk = pl.BlockSpec((tile, FOX_HP * LANES), lambda s, h, i: (s * n_tiles + i, h))
    vtblk = pl.BlockSpec((tile, FOX_HP * LANES), lambda s, h, i: (s * n_tiles + i, h + FOX_V // FOX_HP))
    full = pl.BlockSpec((seq, FOX_HP * LANES), lambda s, h, i: (s, h))
    vfull = pl.BlockSpec((seq, FOX_HP * LANES), lambda s, h, i: (s, h + FOX_V // FOX_HP))
    ctb = pl.BlockSpec((NH, seq), lambda s, h, i: (s * (LANES // NH) + 2, 0))
    return tblk, vtblk, full, vfull, ctb


def _lanes_of(hh):
    return slice(hh * LANES, (hh + 1) * LANES)


def fox_fwd(qn, kn, proj, ct, nseq, seq):
    tq = tk = min(ATT_TILE, seq)
    nq = seq // tq
    rows = nseq * seq
    qblk, _, full, vfull, ctb = _fox_specs(seq, tq, nq)
    hs = range(FOX_HP)

    def body(q_ref, k_ref, v_ref, ct_ref, o_ref, o16_ref, lse_ref):
        hb, i = pl.program_id(1), pl.program_id(2)
        q = [q_ref[:, _lanes_of(hh)] for hh in hs]

        def step(j, carry, diag):
            m, l, acc = (list(carry[t::3]) for t in range(3))
            off = pl.multiple_of(j * tk, tk)
            k = [k_ref[pl.ds(off, tk), _lanes_of(hh)] for hh in hs]
            v = [v_ref[pl.ds(off, tk), _lanes_of(hh)].astype(BF16) for hh in hs]
            ck = [_head_row(ct_ref, hb * FOX_HP + hh, off, tk) for hh in hs]
            s = _each(lambda qq, kk, cc: _dot(qq, kk, "nt") * FOX_SCALE - cc, q, k, ck)
            if diag:
                s = _each(lambda x: jnp.where(_causal(x.shape, 0), x, NEG), s)
            m_new = _each(lambda mm, x: jnp.maximum(mm, jnp.max(x, axis=1, keepdims=True)), m, s)
            p = _each(lambda x, mm: jnp.exp(x - mm), s, m_new)
            alpha = _each(lambda mo, mn: jnp.exp(mo - mn), m, m_new)
            l = _each(lambda a, ll, pp: a * ll + jnp.sum(pp, axis=1, keepdims=True), alpha, l, p)
            acc = _each(lambda a, ac, pp, vv: a * ac + _dot(pp.astype(BF16), vv, "nn"), alpha, acc, p, v)
            return tuple(x for head in zip(m_new, l, acc) for x in head)

        init = (jnp.full((tq, 1), NEG, F32), jnp.zeros((tq, 1), F32), jnp.zeros((tq, LANES), F32)) * FOX_HP
        res = step(i, lax.fori_loop(0, i, lambda j, c: step(j, c, False), init), True)
        for hh in hs:
            m, l, acc = res[3 * hh:3 * hh + 3]
            o = acc / l
            o_ref[:, _lanes_of(hh)] = o
            o16_ref[:, _lanes_of(hh)] = o.astype(BF16)
            lse_ref[:, _lanes_of(hh)] = jnp.broadcast_to(m + jnp.log(l), (tq, LANES))

    wide = (rows, NH * LANES)
    return pl.pallas_call(
        body, name="fox_fwd", grid=(nseq, NH // FOX_HP, nq), in_specs=[qblk, full, vfull, ctb], out_specs=[qblk] * 3,
        out_shape=[jax.ShapeDtypeStruct(wide, F32), jax.ShapeDtypeStruct(wide, BF16), jax.ShapeDtypeStruct(wide, F32)],
        compiler_params=_params(3))(qn, kn, proj, ct)


def fox_dq(qn, kn, proj, ct, do, lse, o, after, nseq, seq):
    tq = tk = min(ATT_TILE, seq)
    nq = seq // tq
    rows = nseq * seq
    qblk, _, full, vfull, ctb = _fox_specs(seq, tq, nq)
    hs = range(FOX_HP)

    def body(q_ref, k_ref, v_ref, ct_ref, do_ref, lse_ref, o_ref, *rest):
        dq_ref, dc_ref = rest[len(after):]
        hb, i = pl.program_id(1), pl.program_id(2)
        q = [q_ref[:, _lanes_of(hh)] for hh in hs]
        lse = [_col(lse_ref[:, _lanes_of(hh)]) for hh in hs]
        delta = [jnp.sum(do_ref[:, _lanes_of(hh)] * o_ref[:, _lanes_of(hh)], axis=1, keepdims=True) for hh in hs]
        do16 = [do_ref[:, _lanes_of(hh)].astype(BF16) for hh in hs]

        def step(j, carry, diag):
            dq, dc = (list(carry[t::2]) for t in range(2))
            off = pl.multiple_of(j * tk, tk)
            k = [k_ref[pl.ds(off, tk), _lanes_of(hh)] for hh in hs]
            v = [v_ref[pl.ds(off, tk), _lanes_of(hh)].astype(BF16) for hh in hs]
            ck = [_head_row(ct_ref, hb * FOX_HP + hh, off, tk) for hh in hs]
            p = _each(lambda qq, kk, cc, ll: jnp.exp(_dot(qq, kk, "nt") * FOX_SCALE - cc - ll), q, k, ck, lse)
            if diag:
                p = _each(lambda x: jnp.where(_causal(x.shape, 0), x, 0.0), p)
            dp = _each(lambda d, vv: _dot(d, vv, "nt"), do16, v)
            ds = _each(lambda pp, d, dl: pp * (d - dl), p, dp, delta)
            dq = _each(lambda a, x, kk: a + _dot(x.astype(BF16), kk, "nn"), dq, ds, k)
            dc = _each(lambda a, x: a + jnp.sum(x, axis=1, keepdims=True), dc, ds)
            return tuple(x for head in zip(dq, dc) for x in head)

        init = (jnp.zeros((tq, LANES), F32), jnp.zeros((tq, 1), F32)) * FOX_HP
        res = step(i, lax.fori_loop(0, i, lambda j, c: step(j, c, False), init), True)
        for hh in hs:
            dq_ref[:, _lanes_of(hh)] = res[2 * hh] * FOX_SCALE
            dc_ref[:, _lanes_of(hh)] = jnp.where(_iota((tq, LANES), 1) == 0, res[2 * hh + 1], 0.0)

    wide = jax.ShapeDtypeStruct((rows, NH * LANES), F32)
    return pl.pallas_call(
        body, name="fox_dq", grid=(nseq, NH // FOX_HP, nq), in_specs=[qblk, full, vfull, ctb, qblk, qblk, qblk] + [ANY] * len(after),
        out_specs=[qblk, qblk], out_shape=[wide, wide], compiler_params=_params(3))(qn, kn, proj, ct, do, lse, o, *after)


def fox_dkv(qn, kn, proj, cb, do, lse, o, after, nseq, seq):
    tq = tk = min(ATT_TILE, seq)
    nq = seq // tq
    rows = nseq * seq
    kblk, vblk, full, _, _ = _fox_specs(seq, tk, nq)
    hs = range(FOX_HP)

    def body(q_ref, k_ref, v_ref, cb_ref, do_ref, lse_ref, o_ref, *rest):
        dk_ref, dv_ref, dc_ref = rest[len(after):]
        j = pl.program_id(2)
        k = [k_ref[:, _lanes_of(hh)] for hh in hs]
        v16 = [v_ref[:, _lanes_of(hh)].astype(BF16) for hh in hs]
        ck = [_col(cb_ref[:, _lanes_of(hh)]) for hh in hs]

        def step(i, carry, diag):
            dk, dv, dc = (list(carry[t::3]) for t in range(3))
            off = pl.multiple_of(i * tq, tq)
            q = [q_ref[pl.ds(off, tq), _lanes_of(hh)] for hh in hs]
            do32 = [do_ref[pl.ds(off, tq), _lanes_of(hh)] for hh in hs]
            do16 = [d.astype(BF16) for d in do32]
            lse = [_row(lse_ref[pl.ds(off, tq), _lanes_of(hh)]) for hh in hs]
            delta = [_row(jnp.broadcast_to(jnp.sum(d * o_ref[pl.ds(off, tq), _lanes_of(hh)], axis=1, keepdims=True), (tq, LANES)))
                     for hh, d in zip(hs, do32)]
            p = _each(lambda kk, qq, cc, ll: jnp.exp(_dot(kk, qq, "nt") * FOX_SCALE - cc - ll), k, q, ck, lse)
            if diag:
                p = _each(lambda x: jnp.where(_causal(x.shape, 1), x, 0.0), p)
            dv = _each(lambda a, pp, d: a + _dot(pp.astype(BF16), d, "nn"), dv, p, do16)
            ds = _each(lambda pp, vv, d, dl: pp * (_dot(vv, d, "nt") - dl), p, v16, do16, delta)
            dk = _each(lambda a, x, qq: a + _dot(x.astype(BF16), qq, "nn"), dk, ds, q)
            dc = _each(lambda a, x: a + jnp.sum(x, axis=1, keepdims=True), dc, ds)
            return tuple(x for head in zip(dk, dv, dc) for x in head)

        zero = jnp.zeros((tk, LANES), F32)
        carry = step(j, (zero, zero, jnp.zeros((tk, 1), F32)) * FOX_HP, True)
        res = lax.fori_loop(j + 1, nq, lambda i, c: step(i, c, False), carry)
        for hh in hs:
            dk, dv, dc = res[3 * hh:3 * hh + 3]
            dk_ref[:, _lanes_of(hh)] = dk * FOX_SCALE
            dv_ref[:, _lanes_of(hh)] = dv.astype(BF16)
            dc_ref[:, _lanes_of(hh)] = jnp.where(_iota((tk, LANES), 1) == 0, -dc, 0.0)

    wide = (rows, NH * LANES)
    return pl.pallas_call(
        body, name="fox_dkv", grid=(nseq, NH // FOX_HP, nq), in_specs=[full, kblk, vblk, kblk, full, full, full] + [ANY] * len(after),
        out_specs=[kblk, kblk, kblk],
        out_shape=[jax.ShapeDtypeStruct(wide, F32), jax.ShapeDtypeStruct(wide, BF16), jax.ShapeDtypeStruct(wide, F32)],
        compiler_params=_params(3))(qn, kn, proj, cb, do, lse, o, *after)


def _adamw_update(w, g, m, v):
    m_new = ADAM_B1 * m + (1.0 - ADAM_B1) * g
    v_new = ADAM_B2 * v + (1.0 - ADAM_B2) * (g * g)
    m_hat = m_new / (1.0 - ADAM_B1 ** ADAM_STEP)
    v_hat = v_new / (1.0 - ADAM_B2 ** ADAM_STEP)
    return -ADAM_LR * (m_hat / (jnp.sqrt(v_hat) + ADAM_EPS) + ADAM_WD * w), m_new, v_new


def adamw(name, w, g, m, v):
    rows, cols = w.shape
    tb = min(rows, 128)
    assert rows % tb == 0
    blk = pl.BlockSpec((tb, cols), lambda i: (i, 0))

    def body(w_ref, g_ref, m_ref, v_ref, d_ref, mo_ref, vo_ref):
        d_ref[...], mo_ref[...], vo_ref[...] = _adamw_update(w_ref[...], g_ref[...], m_ref[...], v_ref[...])

    shp = jax.ShapeDtypeStruct(w.shape, F32)
    return pl.pallas_call(body, name=name, grid=(rows // tb,), in_specs=[blk] * 4, out_specs=[blk] * 3,
                          out_shape=[shp] * 3, compiler_params=_params(1))(w, g, m, v)


SPLIT_TILE = 128


def _tiled(shape2d, ax, n_lead, index):
    blk = (SPLIT_TILE, shape2d[1]) if ax == 0 else (shape2d[0], SPLIT_TILE)

    def index_map(*args):
        *lead, t = index(*args)
        return (*lead, t, 0) if ax == 0 else (*lead, 0, t)

    return pl.BlockSpec((None,) * n_lead + blk, index_map)


def adamw_halves(name, w, mine, other, m, v, c, ax):
    steps = w.shape[ax] // 2 // SPLIT_TILE
    assert w.shape[ax] == 2 * steps * SPLIT_TILE

    def body(c_ref, w_ref, mine_ref, other_ref, m_ref, v_ref, g_ref, d_ref, mo_ref, vo_ref):
        g = jnp.where(pl.program_id(0) // steps == c_ref[0], mine_ref[...], other_ref[...])
        g_ref[...] = g
        d_ref[...], mo_ref[...], vo_ref[...] = _adamw_update(w_ref[...], g, m_ref[...], v_ref[...])

    blk = _tiled(w.shape, ax, 0, lambda i, c_ref: (i,))
    hblk = _tiled(mine.shape, ax, 0, lambda i, c_ref: (i % steps,))
    grid_spec = pltpu.PrefetchScalarGridSpec(num_scalar_prefetch=1, grid=(2 * steps,),
                                             in_specs=[blk, hblk, hblk, blk, blk], out_specs=[blk] * 4)
    shp = jax.ShapeDtypeStruct(w.shape, F32)
    return pl.pallas_call(body, name=name, grid_spec=grid_spec, out_shape=[shp] * 4,
                          compiler_params=_params(1))(c, w, mine, other, m, v)


def add_chips(name, slots, parts, chip, axes):
    outs = []
    for idx, (x, own, ax) in enumerate(zip(slots, parts, axes)):
        n, shape2d = x.shape[0], x.shape[1:]
        steps = shape2d[ax] // SPLIT_TILE
        assert shape2d[ax] == steps * SPLIT_TILE

        def body(me_ref, *refs, n=n):
            o_ref = refs[n + 1]
            acc = None
            for t in range(n):
                term = jnp.where(me_ref[0] == t, refs[n][...], refs[t][...]).astype(F32)
                acc = term if acc is None else acc + term
            o_ref[...] = acc

        def filled(t, n=n):
            return lambda i, me_ref: (jnp.where(me_ref[0] == t, (t + 1) % n, t), i)

        grid_spec = pltpu.PrefetchScalarGridSpec(
            num_scalar_prefetch=1, grid=(steps,),
            in_specs=[_tiled(shape2d, ax, 1, filled(t)) for t in range(n)]
            + [_tiled(shape2d, ax, 1, lambda i, me_ref: (me_ref[0], i))],
            out_specs=_tiled(shape2d, ax, 0, lambda i, me_ref: (i,)))
        outs.append(pl.pallas_call(
            body, name=f"{name}_{idx}", grid_spec=grid_spec, out_shape=jax.ShapeDtypeStruct(shape2d, F32),
            compiler_params=_params(1))(chip, *([x] * n), own))
    return outs


def add_pair(name, gs, rs, c, axes):
    outs = []
    for idx, (g, r, ax) in enumerate(zip(gs, rs, axes)):
        nb = r.shape[0]
        steps = r.shape[1 + ax] // SPLIT_TILE
        assert r.shape[1 + ax] == steps * SPLIT_TILE

        def body(c_ref, g_ref, r_ref, o_ref):
            o_ref[...] = (g_ref[...] + r_ref[...]).astype(BF16)

        grid_spec = pltpu.PrefetchScalarGridSpec(
            num_scalar_prefetch=1, grid=(nb, steps),
            in_specs=[_tiled(g.shape[1:], ax, 1, lambda b, i, c_ref: (b, c_ref[0] * steps + i)),
                      _tiled(r.shape[1:], ax, 1, lambda b, i, c_ref: (b, i))],
            out_specs=_tiled(r.shape[1:], ax, 1, lambda b, i, c_ref: (b, i)))
        outs.append(pl.pallas_call(
            body, name=f"{name}_{idx}", grid_spec=grid_spec, out_shape=jax.ShapeDtypeStruct(r.shape, BF16),
            compiler_params=_params(2))(c, g, r))
    return outs


def _place():
    x, y, c = lax.axis_index("x"), lax.axis_index("y"), lax.axis_index("c")
    return x, y, c, [(1 - x, y), (x, 1 - y), (1 - x, 1 - y)]


def _remote(src, dst, send_sem, recv_sem, dev):
    return pltpu.make_async_remote_copy(src_ref=src, dst_ref=dst, send_sem=send_sem, recv_sem=recv_sem,
                                        device_id=dev, device_id_type=MESH)


def _half(ref, lead, ax, which):
    size = ref.shape[len(lead) + ax] // 2
    part = pl.ds(which * size, size)
    return ref.at[(*lead, part, slice(None)) if ax == 0 else (*lead, slice(None), part)]


def gather_ring(shard):
    rows, cols = shard.shape
    half = cols // 2
    top = rows // 2 // 16 * 16
    assert shard.dtype == BF16 and half % LANES == 0

    def body(in_ref, out_ref, ici_s, ici_r, d2d_s, d2d_r):
        x, y, c, _ = _place()
        me, xn, yn, dg = 2 * x + y, 2 * (1 - x) + y, 2 * x + (1 - y), 2 * (1 - x) + (1 - y)
        to_x, to_y, sib = (1 - x, y, c), (x, 1 - y, c), (x, y, 1 - c)
        mine, other = pl.ds(c * half, half), pl.ds((1 - c) * half, half)
        upper, lower = pl.ds(0, top), pl.ds(top, rows - top)
        started = []

        def send(src, dst, sems, k, dev):
            cp = _remote(src, dst, sems[0].at[k], sems[1].at[k], dev)
            cp.start()
            started.append(cp)

        def arrive(dst, sems, k):
            _remote(dst, dst, sems[0].at[k], sems[1].at[k], sib).wait_recv()

        ici, d2d = (ici_s, ici_r), (d2d_s, d2d_r)
        send(in_ref, out_ref.at[me], d2d, 0, sib)
        send(in_ref.at[:, mine], out_ref.at[me, :, mine], ici, 0, to_x)
        send(in_ref.at[:, mine], out_ref.at[me, :, mine], ici, 1, to_y)
        arrive(out_ref.at[xn, :, mine], ici, 0)
        send(out_ref.at[xn, upper, mine], out_ref.at[xn, upper, mine], ici, 2, to_y)
        send(out_ref.at[xn, :, mine], out_ref.at[xn, :, mine], d2d, 1, sib)
        arrive(out_ref.at[yn, :, mine], ici, 1)
        send(out_ref.at[yn, lower, mine], out_ref.at[yn, lower, mine], ici, 3, to_x)
        send(out_ref.at[yn, :, mine], out_ref.at[yn, :, mine], d2d, 2, sib)
        arrive(out_ref.at[dg, upper, mine], ici, 2)
        send(out_ref.at[dg, upper, mine], out_ref.at[dg, upper, mine], d2d, 3, sib)
        arrive(out_ref.at[dg, lower, mine], ici, 3)
        send(out_ref.at[dg, lower, mine], out_ref.at[dg, lower, mine], d2d, 4, sib)
        arrive(out_ref.at[me], d2d, 0)
        arrive(out_ref.at[xn, :, other], d2d, 1)
        arrive(out_ref.at[yn, :, other], d2d, 2)
        arrive(out_ref.at[dg, upper, other], d2d, 3)
        arrive(out_ref.at[dg, lower, other], d2d, 4)
        for cp in started:
            cp.wait_send()

    return pl.pallas_call(
        body, name="gather_ring", in_specs=[ANY], out_specs=ANY, out_shape=jax.ShapeDtypeStruct((4,) + shard.shape, shard.dtype),
        scratch_shapes=[pltpu.SemaphoreType.DMA((4,))] * 2 + [pltpu.SemaphoreType.DMA((5,))] * 2,
    )(shard)


HBM = pl.BlockSpec(memory_space=pltpu.HBM)
SEM = pl.BlockSpec(memory_space=pltpu.SEMAPHORE)
DATAFLOW = pltpu.SideEffectType.DATAFLOW_SIDE_EFFECTING


def _hbm(a):
    return pltpu.with_memory_space_constraint(a, pltpu.HBM)


class SplitExchange:
    def __init__(self, name, srcs, zone_shapes, n_sems, plan):
        self.name, self.n, self.n_sems, self.plan = name, len(srcs), n_sems, plan
        self.srcs = [_hbm(s) for s in srcs]
        self.zones = [_hbm(lax.empty(shape, s.dtype)) for shape, s in zip(zone_shapes, srcs)]

    def start(self, after):
        n, n_after = self.n, len(after)

        def body(*refs):
            ins, lands = refs[:n], refs[n:2 * n]
            send, recv, token = refs[2 * n + n_after], refs[2 * n + n_after + 1], refs[-1]
            for src, dst, si, ri, dev in self.plan(ins, lands)[0]:
                _remote(src, dst, send.at[si], recv.at[ri], dev).start()
            token[...] = jnp.zeros_like(token)

        res = pl.pallas_call(
            body, name=f"{self.name}_start", in_specs=[HBM] * (2 * n) + [ANY] * n_after,
            out_specs=[SEM, SEM] + [HBM] * (2 * n) + [pl.BlockSpec(memory_space=pltpu.VMEM)],
            out_shape=[pltpu.SemaphoreType.DMA((self.n_sems,)), pltpu.SemaphoreType.DMA((self.n_sems,))]
            + [pltpu.HBM(a.shape, a.dtype) for a in self.srcs + self.zones] + [jax.ShapeDtypeStruct((8, LANES), F32)],
            input_output_aliases={i: 2 + i for i in range(2 * n)},
            compiler_params=pltpu.CompilerParams(has_side_effects=DATAFLOW),
        )(*self.srcs, *self.zones, *after)
        self.sems, self.srcs, self.zones = res[:2], list(res[2:2 + n]), list(res[2 + n:2 + 2 * n])
        return res[-1]

    def wait(self, after):
        n = self.n

        def body(*refs):
            ins, lands = refs[:n], refs[n:2 * n]
            send, recv = refs[2 * n], refs[2 * n + 1]
            sends, arrivals = self.plan(ins, lands)
            for src, _, si, _, dev in sends:
                _remote(src, src, send.at[si], recv.at[si], dev).wait_send()
            for landed, ri in arrivals:
                _remote(landed, landed, send.at[ri], recv.at[ri], _place()[:3]).wait_recv()

        res = pl.pallas_call(
            body, name=f"{self.name}_wait", in_specs=[HBM] * (2 * n) + [SEM, SEM, ANY], out_specs=[HBM] * (2 * n),
            out_shape=[pltpu.HBM(a.shape, a.dtype) for a in self.srcs + self.zones],
            input_output_aliases={i: i for i in range(2 * n)},
            compiler_params=pltpu.CompilerParams(has_side_effects=DATAFLOW),
        )(*self.srcs, *self.zones, *self.sems, after)
        self.srcs = list(res[:n])
        return list(res[n:])


def split_gather(shards):
    n = len(shards)

    def plan(ins, lands):
        x, y, c, chips = _place()
        me = 2 * x + y
        sends, arrivals = [], []
        for w in range(n):
            for j, (ox, oy) in enumerate(chips):
                for k in range(2):
                    base = 2 * (3 * w + j)
                    sends.append((_half(ins[w], (), 0, c), _half(lands[w], (me,), 0, c), base + k, base + c, (ox, oy, k)))
                    arrivals.append((_half(lands[w], (2 * ox + oy,), 0, k), base + k))
            sends.append((ins[w], lands[w].at[me], 6 * n + w, 6 * n + w, (x, y, 1 - c)))
            arrivals.append((lands[w].at[me], 6 * n + w))
        return sends, arrivals

    return SplitExchange("gather", shards, [(4,) + s.shape for s in shards], 7 * n, plan)


def split_pair_swap(name, grads, axes):
    def plan(ins, lands):
        x, y, c, _ = _place()
        sends = [(_half(ins[w], (slice(None),), axes[w], 1 - c), lands[w], w, w, (x, y, 1 - c)) for w in range(len(ins))]
        return sends, [(lands[w], w) for w in range(len(ins))]

    halved = [tuple(d // 2 if i == 1 + ax else d for i, d in enumerate(g.shape)) for g, ax in zip(grads, axes)]
    return SplitExchange(name, grads, halved, len(grads), plan)


def split_chip_exchange(name, parts):
    def plan(ins, lands):
        x, y, c, chips = _place()
        sends, arrivals = [], []
        for w in range(len(ins)):
            for j, (ox, oy) in enumerate(chips):
                sends.append((ins[w].at[2 * ox + oy], lands[w].at[2 * x + y], 3 * w + j, 3 * w + j, (ox, oy, c)))
                arrivals.append((lands[w].at[2 * ox + oy], 3 * w + j))
        return sends, arrivals

    return SplitExchange(name, parts, [p.shape for p in parts], 3 * len(parts), plan)


def split_pair_send(halves):
    def plan(ins, lands):
        x, y, c, _ = _place()
        return ([(ins[w], lands[w], w, w, (x, y, 1 - c)) for w in range(len(ins))],
                [(lands[w], w) for w in range(len(ins))])

    return SplitExchange("pair_send", halves, [h.shape for h in halves], len(halves), plan)


def pair_send(halves):
    n = len(halves)

    def body(*refs):
        ins, outs = refs[:n], refs[n:2 * n]
        send, recv = refs[2 * n:]
        x, y, c, _ = _place()
        cps = [_remote(ins[w], outs[w], send.at[w], recv.at[w], (x, y, 1 - c)) for w in range(n)]
        for cp in cps:
            cp.start()
        for cp in cps:
            cp.wait_recv()
        for cp in cps:
            cp.wait_send()

    return pl.pallas_call(
        body, name="pair_send", in_specs=[ANY] * n, out_specs=[ANY] * n,
        out_shape=[jax.ShapeDtypeStruct(h.shape, h.dtype) for h in halves],
        scratch_shapes=[pltpu.SemaphoreType.DMA((n,))] * 2,
    )(*halves)


def all_reduce_small(name, vec, after=()):
    rows = vec.shape[0]

    def body(v_ref, *refs):
        o_ref, buf, send, recv = refs[len(after):]
        x, y, c, _ = _place()
        me = 4 * x + 2 * y + c
        buf[me] = v_ref[...]
        cps = []
        for k in range(1, 8):
            kx, ky, kc = (k >> 2) & 1, (k >> 1) & 1, k & 1
            peer = (x if kx == 0 else 1 - x, y if ky == 0 else 1 - y, c if kc == 0 else 1 - c)
            cp = _remote(v_ref, buf.at[me], send.at[k - 1], recv.at[k - 1], peer)
            cp.start()
            cps.append(cp)
        for k in range(1, 8):
            kx, ky, kc = (k >> 2) & 1, (k >> 1) & 1, k & 1
            px, py, pc = (x if kx == 0 else 1 - x, y if ky == 0 else 1 - y, c if kc == 0 else 1 - c)
            slot = buf.at[4 * px + 2 * py + pc]
            _remote(slot, slot, send.at[k - 1], recv.at[k - 1], (px, py, pc)).wait_recv()
        for cp in cps:
            cp.wait_send()
        acc = buf[0]
        for d in range(1, 8):
            acc = acc + buf[d]
        o_ref[...] = acc

    vm = pl.BlockSpec(memory_space=pltpu.VMEM)
    return pl.pallas_call(
        body, name=name, in_specs=[vm] + [ANY] * len(after), out_specs=vm, out_shape=jax.ShapeDtypeStruct(vec.shape, F32),
        scratch_shapes=[pltpu.VMEM((8, rows, LANES), F32), pltpu.SemaphoreType.DMA((7,)), pltpu.SemaphoreType.DMA((7,))],
    )(vec, *after)


class NoExchange:
    def __init__(self, late):
        self.late = late

    def late_weights(self, after):
        return self.late

    def reduce_start(self, grads):
        return jnp.zeros((8, LANES), F32)

    def reduce_exchange(self, after):
        return jnp.zeros((8, LANES), F32)

    def reduce_finish(self, after):
        return jnp.zeros((8, LANES), F32)

    def input_grad_start(self, dw_main, dw_small):
        return jnp.zeros((8, LANES), F32)

    def input_grad_exchange(self, after):
        return jnp.zeros((8, LANES), F32)


def local_step(x2, tgt2, g1, g2, gdn_ng, qn_g, kn_g, p1, p2, conv_w, wt_main, wt_small, hooks, nseq, seq):
    rows, dm = x2.shape
    wide = NH * LANES
    row = lambda a, off=0, w=None: (a, "row", off, a.shape[1] if w is None else w)
    rowh = lambda a, off=0, w=LANES: (a, "rowh", off, w)
    par = lambda a: (a, "par", 0, a.shape[1])
    parh = lambda a, off=0: (a, "parh", off, LANES)
    o_row = lambda w, dt: (w, "row", w, dt)
    o_rowh = lambda dt, tw=wide, w=LANES: (tw, "rowh", w, dt)

    u, = ew_fwd("rms1", f_rms, [row(x2), par(g1)], [o_row(dm, BF16)], rows)
    proj = matmul("mm_in", u, wt_main, "nt", BF16, tm=2048)
    sp = matmul("mm_in_small", u, wt_small, "nt", F32)
    so, = ew_fwd("small", f_small, [row(sp), par(p1), par(p2)], [o_row(LANES, F32)], rows)
    cs = cumsum_time("cumsum", so, nseq, seq, False)
    gb, bb, cb = ew_fwd("bcast", f_bcast, [row(so), row(cs)], [o_rowh(F32)] * 3, rows, NH)
    ct = transpose_time("c_time_major", cs, nseq, seq)
    conv = {}
    for mode, off in (("q", 0), ("k", NH), ("v", 2 * NH)):
        conv[mode], = ew_fwd(f"conv_{mode}", make_f_conv(mode), [rowh(proj, off), parh(conv_w, off)], [o_rowh(F32)],
                             rows, NH, seq, "hi", CONV_HEADS)
    val, kcum, attn, qdec, kdec, t_inv = gdn_a_fwd(conv["q"], conv["k"], conv["v"], gb, bb, rows)
    o_a, snaps = gdn_b_fwd(val, kcum, attn, qdec, kdec, gb, nseq, seq)
    ya_in, = ew_fwd("gdn_post", f_post, [rowh(o_a), rowh(proj, 3 * NH), par(gdn_ng)], [o_rowh(BF16)], rows, NH)
    fqn, = ew_fwd("fox_qn", f_rms, [rowh(proj, FOX_Q), par(qn_g)], [o_rowh(BF16)], rows, NH)
    fkn, = ew_fwd("fox_kn", f_rms, [rowh(proj, FOX_K), par(kn_g)], [o_rowh(BF16)], rows, NH)
    o_b, o_b16, lse = fox_fwd(fqn, fkn, proj, ct, nseq, seq)
    p_a, p_b, w_o, w_u, w_d = hooks.late_weights(o_a)
    y_a = matmul("mm_pa", ya_in, p_a, "nn", F32, tn=1024)
    y_b = matmul("mm_pb", o_b16, p_b, "nn", F32, tn=1024)
    gates = [row(proj, 7, dm), row(proj, 8, dm)]
    merged, = ew_fwd("merge", f_merge, gates + [row(y_a), row(y_b)], [o_row(dm, BF16)], rows)
    hres = matmul("mm_out", merged, w_o, "nn", F32, add=x2, tn=1024)
    hn, = ew_fwd("rms2", f_rms, [row(hres), par(g2)], [o_row(dm, BF16)], rows)
    up_blocks = w_u.shape[0]
    act, relu2 = matmul("mm_up", hn, w_u, "nn", F32, col_blocks=up_blocks, out_dtypes=[F32, BF16],
                        epilogue=lambda r: [r, jnp.maximum(r, 0.0) * jnp.maximum(r, 0.0)], tm=2048)
    def loss_tail(r, h_tile, t_tile):
        d = (r + h_tile) - t_tile
        e = (0.5 / dm) * (d * d)
        part = e.reshape(e.shape[0] // 8, 8, e.shape[1]).sum(axis=0)
        part = sum(part[:, t * LANES:(t + 1) * LANES] for t in range(e.shape[1] // LANES))
        g = d * (1.0 / dm)
        return [g, g, part]

    dout, dout16, loss_acc = matmul("mm_down", relu2, w_d, "nn", F32, extras=[hres, tgt2], epilogue=loss_tail,
                                    out_dtypes=[F32, BF16, F32], tile_sums=True)

    d_act = matmul("mm_d_act", dout16, w_d, "nt", BF16, extras=[act], epilogue=lambda r, a: [2.0 * jnp.maximum(a, 0.0) * r],
                   tm=2048)
    dw_d = matmul("mm_dw_down", relu2, dout16, "tn", F32, tn=1024)
    dw_u = matmul("mm_dw_up", hn, d_act, "tn", F32, col_blocks=up_blocks)
    d_hn = matmul("mm_d_hn", d_act, w_u, "nt", F32, col_blocks=up_blocks)
    dh, dh16, dg2 = ew_bwd("rms2_b", f_rms, [row(hres), par(g2)], [(row(d_hn),)], [row(dout)],
                           lambda g, e: [g[0] + e[0], g[0] + e[0], g[1]],
                           [((rows, dm), "row", dm, F32, None), ((rows, dm), "row", dm, BF16, None), ((1, dm), "par", dm, F32, "all")], rows)
    d_merged = matmul("mm_d_merged", dh16, w_o, "nt", F32, tn=1024)
    dw_o = matmul("mm_dw_out", merged, dh16, "tn", F32, tn=1024)
    seg16 = ((rows, dm), "row", dm, BF16, None)
    d_ga16, d_gb16, d_ya16, d_yb16 = ew_bwd("merge_b", f_merge, gates + [row(y_a), row(y_b)], [(row(d_merged),)], [],
                                            lambda g, e: list(g), [seg16] * 4, rows)
    dp_a = matmul("mm_dp_a", ya_in, d_ya16, "tn", F32, tn=1024)
    d_ya_in = matmul("mm_d_ya_in", d_ya16, p_a, "nt", F32, tn=1024)
    dp_b = matmul("mm_dp_b", o_b16, d_yb16, "tn", F32, tn=1024)
    d_ob = matmul("mm_d_ob", d_yb16, p_b, "nt", F32, tn=1024)
    token = hooks.reduce_start(dict(p_a=dp_a, p_b=dp_b, w_o=dw_o, w_u=dw_u, w_d=dw_d))
    gdn_ng_t = gdn_ng + token[0, 0]
    h32 = ((rows, wide), "rowh", LANES, F32, None)
    h16 = ((rows, wide), "rowh", LANES, BF16, None)
    gain = ((1, LANES), "par", LANES, F32, "all")
    d_oa, d_z16, d_gdn_ng = ew_bwd("gdn_post_b", f_post, [rowh(o_a), rowh(proj, 3 * NH), par(gdn_ng_t)], [(rowh(d_ya_in),)], [],
                                   lambda g, e: list(g), [h32, h16, gain], rows, NH)
    dval, dkc, dat, dqd, dkd, dgb_b = gdn_b_bwd(val, kcum, attn, qdec, kdec, gb, snaps, d_oa, nseq, seq)
    d_cq, d_ck, d_cv, d_gb, d_bb = gdn_a_bwd(conv["q"], conv["k"], conv["v"], gb, bb, t_inv, dval, dkc, dat, dqd, dkd, dgb_b, rows)
    token = hooks.reduce_exchange(d_cq)
    conv_w_t = conv_w + token[0, 0]
    d_pre, d_conv = {}, {}
    tap = ((4, wide), "parh", LANES, F32, "inner")
    for mode, off, ctg in (("q", 0, d_cq), ("k", NH, d_ck), ("v", 2 * NH, d_cv)):
        d_pre[mode], d_conv[mode] = ew_bwd(f"conv_{mode}_b", make_f_conv(mode), [rowh(proj, off), parh(conv_w_t, off)],
                                           [(rowh(ctg),)], [], lambda g, e: list(g), [h16, tap], rows, NH, seq, "hi", CONV_HEADS)
    d_fqn, d_cq_b = fox_dq(fqn, fkn, proj, ct, d_ob, lse, o_b, [token], nseq, seq)
    d_fkn, d_fv16, d_ck_b = fox_dkv(fqn, fkn, proj, cb, d_ob, lse, o_b, [token], nseq, seq)
    token = hooks.reduce_finish(d_fkn)
    qn_g_t, kn_g_t = qn_g + token[0, 0], kn_g + token[0, 0]
    d_fq16, d_qn_g = ew_bwd("fox_qn_b", f_rms, [rowh(proj, FOX_Q), par(qn_g_t)], [(rowh(d_fqn),)], [], lambda g, e: list(g),
                            [h16, gain], rows, NH)
    d_fk16, d_kn_g = ew_bwd("fox_kn_b", f_rms, [rowh(proj, FOX_K), par(kn_g_t)], [(rowh(d_fkn),)], [], lambda g, e: list(g),
                            [h16, gain], rows, NH)
    narrow = ((rows, LANES), "row", LANES, F32, None)
    d_so, d_cs = ew_bwd("bcast_b", f_bcast, [row(so), row(cs)], [(rowh(d_gb),), (rowh(d_bb),), (rowh(d_cq_b), rowh(d_ck_b))], [],
                        lambda g, e: list(g), [narrow, narrow], rows, NH)
    d_logf = cumsum_time("cumsum_b", d_cs, nseq, seq, True)
    vec = ((1, LANES), "par", LANES, F32, "all")
    d_sp16, d_p1, d_p2 = ew_bwd("small_b", f_small, [row(sp), par(p1), par(p2)], [(row(d_so), row(d_logf))], [],
                                lambda g, e: list(g), [((rows, LANES), "row", LANES, BF16, None), vec, vec], rows)
    d_proj16 = jnp.concatenate([d_pre["q"], d_pre["k"], d_pre["v"], d_z16, d_fq16, d_fk16, d_fv16, d_ga16, d_gb16], axis=1)
    dw_main = matmul("mm_dw_main", d_proj16, u, "tn", F32)
    dw_small = matmul("mm_dw_small", d_sp16, u, "tn", F32)
    wt_small_t = wt_small + hooks.input_grad_start(dw_main, dw_small)[0, 0].astype(BF16)
    d_u = matmul("mm_d_u_small", d_sp16, wt_small_t, "nn", F32)
    d_u = matmul("mm_d_u_first", d_proj16, wt_main, "nn", F32, add=d_u, k_part=(0, 2))
    d_u = matmul("mm_d_u_second", d_proj16, wt_main, "nn", F32, add=d_u, k_part=(1, 2), after=[hooks.input_grad_exchange(d_u)])
    dx, dg1 = ew_bwd("rms1_b", f_rms, [row(x2), par(g1)], [(row(d_u),)], [row(dh)], lambda g, e: [g[0] + e[0], g[1]],
                     [((rows, dm), "row", dm, F32, None), ((1, dm), "par", dm, F32, "all")], rows)
    d_conv_w = jnp.concatenate([d_conv["q"], d_conv["k"], d_conv["v"]], axis=1)
    return dict(loss_acc=loss_acc, dx=dx, g1=dg1, g2=dg2, gdn_ng=d_gdn_ng, qn=d_qn_g, kn=d_kn_g, p1=d_p1, p2=d_p2,
                conv=d_conv_w, w_main=dw_main, w_small=dw_small, p_a=dp_a, p_b=dp_b, w_o=dw_o, w_u=dw_u, w_d=dw_d)


_W = NH * LANES
_A0, _A1 = 4 * _W, 4 * _W + 2 * NH
_B0, _B1 = _A1 + 3 * _W, _A1 + 3 * _W + NH


def _split_w_in(full_t):
    main = jnp.concatenate([full_t[:_A0], full_t[_A1:_B0], full_t[_B1:]], axis=0)
    small = jnp.concatenate([full_t[_A0:_A1], full_t[_B0:_B1], jnp.zeros((LANES - 3 * NH, full_t.shape[1]), full_t.dtype)], axis=0)
    return main, small


def _join_w_in(main, small):
    return jnp.concatenate([main[:_A0], small[:2 * NH], main[_A0:_A0 + 3 * _W], small[2 * NH:3 * NH], main[_A0 + 3 * _W:]], axis=0)


def _lanes(v, at=0):
    return jnp.pad(v.reshape(1, -1), ((0, 0), (at, LANES - at - v.size)))


def kernel(x, norm_mix_g, w_in, gdn_conv_w, gdn_a_log, gdn_dt_bias, gdn_norm_g, fox_q_norm_g, fox_k_norm_g, fox_f_bias, w_proj_gdn, w_proj_fox, w_out, norm_mlp_g, w_up, w_down, loss_target, m_norm_mix_g, m_w_in, m_gdn_conv_w, m_gdn_a_log, m_gdn_dt_bias, m_gdn_norm_g, m_fox_q_norm_g, m_fox_k_norm_g, m_fox_f_bias, m_w_proj_gdn, m_w_proj_fox, m_w_out, m_norm_mlp_g, m_w_up, m_w_down, v_norm_mix_g, v_w_in, v_gdn_conv_w, v_gdn_a_log, v_gdn_dt_bias, v_gdn_norm_g, v_fox_q_norm_g, v_fox_k_norm_g, v_fox_f_bias, v_w_proj_gdn, v_w_proj_fox, v_w_out, v_norm_mlp_g, v_w_up, v_w_down):
    nseq, seq, dm = x.shape
    rows = nseq * seq
    xi, yi, ci = lax.axis_index("x"), lax.axis_index("y"), lax.axis_index("c")
    chip = 2 * xi + yi
    conv_cols = gdn_conv_w.shape[2]

    tr = lambda a: jnp.swapaxes(a[0], 0, 1)
    big = [tr(w_in), w_proj_gdn[0], w_proj_fox[0], w_out[0], w_up[0], w_down[0]]
    axes = [1, 0, 0, 0, 0, 0]
    big16 = [w.astype(BF16) for w in big]
    conv_slot = jnp.zeros((4, 4, conv_cols), F32).at[:, chip].set(jnp.where(ci == 0, gdn_conv_w[0], 0.0))
    conv_full = all_reduce_small("gather_conv", conv_slot.reshape(-1, LANES)).reshape(4, 4 * conv_cols)
    got_in = gather_ring(big16[0])
    wt_main, wt_small = _split_w_in(got_in.reshape(-1, dm))
    core, chip_no = ci.reshape(1).astype(jnp.int32), chip.reshape(1).astype(jnp.int32)
    gather = split_gather(big16[1:])
    token = gather.start([got_in, conv_full])

    class Hooks:
        def late_weights(self, after):
            g_pa, g_pb, g_wo, w_u, g_wd = gather.wait(after)
            return (*(g.reshape(-1, dm) for g in (g_pa, g_pb, g_wo)), w_u, g_wd.reshape(-1, dm))

        def reduce_start(self, grads):
            blocks = [grads["p_a"].reshape(4, -1, dm), grads["p_b"].reshape(4, -1, dm), grads["w_o"].reshape(4, -1, dm),
                      grads["w_u"], grads["w_d"].reshape(4, -1, dm)]
            self.swap = split_pair_swap("pair_swap_late", blocks, axes[1:])
            return self.swap.start([])

        def reduce_exchange(self, after):
            swapped = self.swap.wait(after)
            self.exchange = split_chip_exchange("chip_exchange_late", add_pair("add_pair_late", self.swap.srcs, swapped, core, axes[1:]))
            return self.exchange.start([])

        def reduce_finish(self, after):
            slots = self.exchange.wait(after)
            self.send = split_pair_send(add_chips("add_chips_late", slots, self.exchange.srcs, chip_no, axes[1:]))
            return self.send.start([])

        def input_grad_start(self, dw_main, dw_small):
            self.in_swap = split_pair_swap("pair_swap_in", [_join_w_in(dw_main, dw_small).reshape(4, -1, dm)], axes[:1])
            return self.in_swap.start([])

        def input_grad_exchange(self, after):
            swapped = self.in_swap.wait(after)
            self.in_exchange = split_chip_exchange("chip_exchange_in", add_pair("add_pair_in", self.in_swap.srcs, swapped, core, axes[:1]))
            return self.in_exchange.start([])

    hooks = Hooks()
    p1 = _lanes(gdn_dt_bias[0]) + _lanes(fox_f_bias[0], 2 * NH)
    p2 = _lanes(gdn_a_log[0])

    g = local_step(x.reshape(rows, dm), loss_target.reshape(rows, dm), norm_mix_g + token[0, 0], norm_mlp_g, gdn_norm_g,
                   fox_q_norm_g, fox_k_norm_g, p1, p2, conv_full, wt_main, wt_small, hooks, nseq, seq)

    others = hooks.send.wait(g["dx"])
    big_m = [tr(m_w_in), m_w_proj_gdn[0], m_w_proj_fox[0], m_w_out[0], m_w_up[0], m_w_down[0]]
    big_v = [tr(v_w_in), v_w_proj_gdn[0], v_w_proj_fox[0], v_w_out[0], v_w_up[0], v_w_down[0]]
    names = ["w_in", "w_proj_gdn", "w_proj_fox", "w_out", "w_up", "w_down"]
    big_res, big_grad = {}, {}
    for i in range(1, len(names)):
        big_grad[names[i]], *big_res[names[i]] = adamw_halves(f"adamw_{names[i]}", big[i], hooks.send.srcs[i - 1], others[i - 1],
                                                              big_m[i], big_v[i], core, axes[i])
    slots = hooks.in_exchange.wait(big_res[names[-1]][0])
    mine = add_chips("add_chips_in", slots, hooks.in_exchange.srcs, chip_no, axes[:1])
    res = adamw_halves("adamw_w_in", big[0], mine[0], pair_send(mine)[0], big_m[0], big_v[0], core, axes[0])
    big_grad["w_in"], *big_res["w_in"] = [jnp.swapaxes(r, 0, 1) for r in res]

    small_parts = [g["loss_acc"], g["g1"].reshape(8, LANES), g["g2"].reshape(8, LANES), g["gdn_ng"], g["qn"], g["kn"], g["p1"], g["p2"],
                   g["conv"].reshape(-1, LANES)]
    tiled = [jnp.pad(p, ((0, -p.shape[0] % 8), (0, 0))) for p in small_parts]
    red = all_reduce_small("reduce_small", jnp.concatenate(tiled, axis=0), slots)
    pos, red_parts = 0, []
    for p, t in zip(small_parts, tiled):
        red_parts.append(red[pos:pos + p.shape[0]])
        pos += t.shape[0]
    r_loss, r_g1, r_g2, r_gdn_ng, r_qn, r_kn, r_p1, r_p2, r_conv = red_parts
    loss = jnp.sum(r_loss)
    g_conv = lax.dynamic_slice_in_dim(r_conv.reshape(4, 4, conv_cols), chip, 1, axis=1).reshape(4, conv_cols)
    small_grads = [r_g1.reshape(1, dm), r_p2[:, :NH], r_p1[:, :NH], r_gdn_ng, r_qn, r_kn, r_p1[:, 2 * NH:3 * NH], r_g2.reshape(1, dm)]
    small_w = [norm_mix_g, gdn_a_log, gdn_dt_bias, gdn_norm_g, fox_q_norm_g, fox_k_norm_g, fox_f_bias, norm_mlp_g]
    small_m = [m_norm_mix_g, m_gdn_a_log, m_gdn_dt_bias, m_gdn_norm_g, m_fox_q_norm_g, m_fox_k_norm_g, m_fox_f_bias, m_norm_mlp_g]
    small_v = [v_norm_mix_g, v_gdn_a_log, v_gdn_dt_bias, v_gdn_norm_g, v_fox_q_norm_g, v_fox_k_norm_g, v_fox_f_bias, v_norm_mlp_g]

    def pack(parts):
        flat = jnp.concatenate([jnp.pad(p.reshape(-1), (0, -p.size % LANES)) for p in parts])
        return jnp.pad(flat, (0, -flat.size % (8 * LANES))).reshape(-1, LANES)

    packed = adamw("adamw_small", pack(small_w + [gdn_conv_w[0]]), pack(small_grads + [g_conv]),
                   pack(small_m + [m_gdn_conv_w[0]]), pack(small_v + [v_gdn_conv_w[0]]))

    def unpack(flat2d):
        flat, pos, res = flat2d.reshape(-1), 0, []
        for p in small_w + [gdn_conv_w[0]]:
            res.append(flat[pos:pos + p.size].reshape(p.shape))
            pos += p.size + (-p.size % LANES)
        return res

    s_delta, s_m, s_v = (unpack(a) for a in packed)

    order = ["norm_mix_g", "w_in", "gdn_conv_w", "gdn_a_log", "gdn_dt_bias", "gdn_norm_g", "fox_q_norm_g", "fox_k_norm_g",
             "fox_f_bias", "w_proj_gdn", "w_proj_fox", "w_out", "norm_mlp_g", "w_up", "w_down"]
    small_names = ["norm_mix_g", "gdn_a_log", "gdn_dt_bias", "gdn_norm_g", "fox_q_norm_g", "fox_k_norm_g", "fox_f_bias", "norm_mlp_g",
                   "gdn_conv_w"]
    small_idx = {nm: i for i, nm in enumerate(small_names)}
    shapes = dict(zip(order, (a.shape for a in (norm_mix_g, w_in, gdn_conv_w, gdn_a_log, gdn_dt_bias, gdn_norm_g, fox_q_norm_g,
                                                 fox_k_norm_g, fox_f_bias, w_proj_gdn, w_proj_fox, w_out, norm_mlp_g, w_up, w_down))))
    grads_out, delta_out, m_out, v_out = [], [], [], []
    for nm in order:
        if nm in big_res:
            d, mm, vv = big_res[nm]
            gr = big_grad[nm]
        else:
            i = small_idx[nm]
            gr = (small_grads + [g_conv])[i]
            d, mm, vv = s_delta[i], s_m[i], s_v[i]
        for lst, val in ((grads_out, gr), (delta_out, d), (m_out, mm), (v_out, vv)):
            lst.append(val.reshape(shapes[nm]))
    return (loss, g["dx"].reshape(x.shape), *grads_out, *delta_out, *m_out, *v_out)
```

```python
import functools

import jax
import jax.numpy as jnp
from jax import lax
from jax.experimental import pallas as pl
from jax.experimental.pallas import tpu as pltpu

F32 = jnp.float32
BF16 = jnp.bfloat16
LANES = 128
NH = 8
EPS = 1e-6
GDN_CHUNK = 64
GDN_ROWS = 256
GDN_BASE = 16
ROW_TILE = 512
CONV_HEADS = 2
ATT_TILE = 512
NEG = -1e30
VMEM_LIMIT_BYTES = 58 * 1024 * 1024
LO = lax.Precision.DEFAULT
MESH = pl.DeviceIdType.MESH
ANY = pl.BlockSpec(memory_space=pl.ANY)

ADAM_LR, ADAM_B1, ADAM_B2, ADAM_EPS, ADAM_WD, ADAM_STEP = 0.001, 0.9, 0.999, 1e-08, 0.01, 10


def _params(n_grid):
    return pltpu.CompilerParams(dimension_semantics=("arbitrary",) * n_grid,
                                vmem_limit_bytes=VMEM_LIMIT_BYTES)


def _dot(a, b, dims, precision=None):
    dn = {"nn": (((1,), (0,)), ((), ())), "nt": (((1,), (1,)), ((), ())), "tn": (((0,), (0,)), ((), ()))}[dims]
    return lax.dot_general(a, b, dn, precision=precision, preferred_element_type=F32)


def _iota(shape, dim):
    return lax.broadcasted_iota(jnp.int32, shape, dim)


def _split(x, parts):
    out = []
    for _ in range(parts - 1):
        hi = x.astype(BF16)
        out.append(hi)
        x = x - hi.astype(F32)
    return out + [x.astype(BF16)]


def _dot_mask(mask, b, dims, terms=3):
    m16 = mask.astype(BF16)
    acc = None
    for part in reversed(_split(b, terms)):
        prod = _dot(m16, part, dims)
        acc = prod if acc is None else acc + prod
    return acc


@jax.custom_vjp
def mm_mask(mask, b):
    return _dot_mask(mask, b, "nn", 2)


mm_mask.defvjp(lambda mask, b: (_dot_mask(mask, b, "nn", 2), mask),
               lambda mask, g: (jnp.zeros_like(mask), _dot_mask(mask, g, "tn", 2)))


def matmul(name, a, b, dims, out_dtype, add=None, tm=1024, tn=1024, tk=512, col_blocks=None,
           extras=(), epilogue=None, out_dtypes=None, k_part=None, after=(), tile_sums=False):
    if col_blocks and dims != "tn":
        nb, b_rows, bw = b.shape
        b_shape = (b_rows, nb * bw)
    else:
        b_shape = b.shape
    if dims == "nn":
        (m, k), (_, n) = a.shape, b_shape
    elif dims == "nt":
        (m, k), (n, _) = a.shape, b_shape
    else:
        (k, m), (_, n) = a.shape, b_shape
    k_span = k // (k_part[1] if k_part else 1)
    if col_blocks and dims == "nt":
        k_span = min(k_span, bw)
    tk = k if k <= 1024 else max(t for t in (2048, 1536, 1024, 512, tk) if k_span % t == 0)
    tm, tn, tk = min(tm, m), min(tn, n), min(tk, k)
    assert m % tm == 0 and n % tn == 0 and k % tk == 0, (name, m, n, k)
    k0, nk = (0, k // tk) if k_part is None else (k_part[0] * (k // tk // k_part[1]), k // tk // k_part[1])
    assert k_part is None or (dims == "nn" and not col_blocks and (k // tk) % k_part[1] == 0)
    a_spec = pl.BlockSpec((tk, tm), lambda i, j, kk: (kk, i)) if dims == "tn" else pl.BlockSpec((tm, tk), lambda i, j, kk: (i, kk + k0))
    b_spec = pl.BlockSpec((tn, tk), lambda i, j, kk: (j, kk)) if dims == "nt" else pl.BlockSpec((tk, tn), lambda i, j, kk: (kk + k0, j))
    o_spec = pl.BlockSpec((tm, tn), lambda i, j, kk: (i, j))
    out_shape = (m, n)
    if col_blocks and dims == "nn":
        per = bw // tn
        assert bw % tn == 0
        b_spec = pl.BlockSpec((None, tk, tn), lambda i, j, kk: (j // per, kk, j % per))
    elif col_blocks and dims == "nt":
        per = bw // tk
        assert bw % tk == 0
        b_spec = pl.BlockSpec((None, tn, tk), lambda i, j, kk: (kk // per, j, kk % per))
    elif col_blocks:
        bw = n // col_blocks
        per = bw // tn
        assert bw % tn == 0 and add is None
        o_spec = pl.BlockSpec((None, tm, tn), lambda i, j, kk: (j // per, i, j % per))
        out_shape = (col_blocks, m, bw)
    extras = list(extras) + ([add] if add is not None else [])
    if add is not None:
        assert epilogue is None
        epilogue = lambda r, *e: [r + e[-1]]
    out_dtypes = [out_dtype] if epilogue is None or out_dtypes is None else list(out_dtypes)
    n_ex, n_out = len(extras), len(out_dtypes)

    def body(*refs):
        a_ref, b_ref = refs[0], refs[1]
        ex_refs, o_refs = refs[2:2 + n_ex], refs[2 + n_ex + len(after):2 + n_ex + len(after) + n_out]

        def finish(r):
            res = [r] if epilogue is None else epilogue(r, *[e[...] for e in ex_refs])
            for o_ref, v in zip(o_refs, res):
                o_ref[...] = v.astype(o_ref.dtype)

        if nk == 1:
            finish(_dot(a_ref[...], b_ref[...], dims))
            return
        acc_ref = refs[-1]
        kk = pl.program_id(2)

        @pl.when(kk == 0)
        def _():
            acc_ref[...] = jnp.zeros_like(acc_ref)

        acc_ref[...] += _dot(a_ref[...], b_ref[...], dims)

        @pl.when(kk == nk - 1)
        def _():
            finish(acc_ref[...])

    out_specs = [o_spec] * n_out
    out_shapes = [jax.ShapeDtypeStruct(out_shape, dt) for dt in out_dtypes]
    if tile_sums:
        out_specs[-1] = pl.BlockSpec((8, LANES), lambda i, j, kk: (i, j))
        out_shapes[-1] = jax.ShapeDtypeStruct((8 * (m // tm), LANES * (n // tn)), out_dtypes[-1])
    res = pl.pallas_call(
        body, name=name, grid=(m // tm, n // tn, nk), in_specs=[a_spec, b_spec] + [o_spec] * n_ex + [ANY] * len(after),
        out_specs=out_specs, out_shape=out_shapes,
        scratch_shapes=[pltpu.VMEM((tm, tn), F32)] if nk > 1 else [], compiler_params=_params(3),
    )(a, b, *extras, *after)
    return res[0] if n_out == 1 else res


def _ew_spec(kind, off, width, tb, hp, order, shape=None):
    def ih(g0, g1):
        return (g0, g1) if order == "ih" else (g1, g0)

    assert off % hp == 0 or kind in ("row", "par")
    if kind == "row":
        return pl.BlockSpec((tb, width), lambda g0, g1: (ih(g0, g1)[0], off))
    if kind == "rowh":
        return pl.BlockSpec((tb, hp * width), lambda g0, g1: (ih(g0, g1)[0], ih(g0, g1)[1] + off // hp))
    if kind == "par":
        return pl.BlockSpec(shape, lambda g0, g1: (0, 0))
    if kind == "parh":
        return pl.BlockSpec((shape[0], hp * width), lambda g0, g1: (0, ih(g0, g1)[1] + off // hp))
    raise ValueError(kind)


def _ew_grid(rows, tb, nh, hp, order):
    assert nh % hp == 0 and rows % tb == 0
    return (rows // tb, nh // hp) if order == "ih" else (nh // hp, rows // tb)


def _ew_load(ref, kind, width, hh):
    if kind in ("row", "par"):
        return ref[...].astype(F32)
    return ref[:, hh * width:(hh + 1) * width].astype(F32)


def ew_fwd(name, f, ins, outs, rows, nh=1, tb=ROW_TILE, order="ih", hp=None, after=()):
    hp = nh if hp is None else hp
    n_in = len(ins)

    def body(*refs):
        hb = pl.program_id(1) if order == "ih" else pl.program_id(0)
        for hh in range(hp):
            h = hh if hp == nh else hb * hp + hh
            vals = [_ew_load(r, kd, w, hh) for r, (_, kd, _, w) in zip(refs[:n_in], ins)]
            res = f(h, *vals)
            for r, v, (_, kd, w, _) in zip(refs[n_in + len(after):], res, outs):
                if kd == "row":
                    assert hp == 1
                    r[...] = v.astype(r.dtype)
                else:
                    r[:, hh * w:(hh + 1) * w] = v.astype(r.dtype)

    in_specs = [_ew_spec(kd, off, w, tb, hp, order, a.shape) for (a, kd, off, w) in ins]
    out_specs = [_ew_spec(kd, 0, w, tb, hp, order) for (_, kd, w, _) in outs]
    out_shape = [jax.ShapeDtypeStruct((rows, tw), dt) for (tw, _, _, dt) in outs]
    return pl.pallas_call(
        body, name=name, grid=_ew_grid(rows, tb, nh, hp, order), in_specs=in_specs + [ANY] * len(after), out_specs=out_specs,
        out_shape=out_shape, compiler_params=_params(2),
    )(*[a for (a, _, _, _) in ins], *after)


def ew_bwd(name, f, ins, cts, extras, emit, outs, rows, nh=1, tb=ROW_TILE, order="ih", hp=None):
    hp = nh if hp is None else hp
    n_in = len(ins)
    flat_cts = [d for group in cts for d in group]
    n_ct, n_ex = len(flat_cts), len(extras)

    def body(*refs):
        g0, g1 = pl.program_id(0), pl.program_id(1)
        hb = g1 if order == "ih" else g0
        out_refs = refs[n_in + n_ct + n_ex:]
        shared = [None] * len(outs)

        def store(r, v, first, sl=None):
            def put(val, add):
                if sl is None:
                    r[...] = (r[...] + val if add else val).astype(r.dtype)
                else:
                    r[:, sl] = (r[:, sl] + val if add else val).astype(r.dtype)

            if first is None:
                put(v, False)
            else:
                pl.when(first)(lambda: put(v, False))
                pl.when(jnp.logical_not(first))(lambda: put(v, True))

        for hh in range(hp):
            h = hh if hp == nh else hb * hp + hh
            vals = [_ew_load(r, kd, w, hh) for r, (_, kd, _, w) in zip(refs[:n_in], ins)]
            ct_refs = list(zip(refs[n_in:n_in + n_ct], flat_cts))
            ct_vals, pos = [], 0
            for group in cts:
                v = None
                for r, (_, kd, _, w) in ct_refs[pos:pos + len(group)]:
                    t = _ew_load(r, kd, w, hh)
                    v = t if v is None else v + t
                pos += len(group)
                ct_vals.append(v)
            ex_vals = [_ew_load(r, kd, w, hh) for r, (_, kd, _, w) in zip(refs[n_in + n_ct:n_in + n_ct + n_ex], extras)]
            _, vjp = jax.vjp(lambda *a: f(h, *a), *vals)
            res = emit(vjp(tuple(ct_vals)), ex_vals)
            for idx, (r, v, (_, kd, w, _, acc)) in enumerate(zip(out_refs, res, outs)):
                if kd in ("row", "par"):
                    shared[idx] = v if shared[idx] is None else shared[idx] + v
                else:
                    store(r, v, (g1 == 0) if acc == "inner" else None, slice(hh * w, (hh + 1) * w))
        for idx, (r, (_, kd, _, _, acc)) in enumerate(zip(out_refs, outs)):
            if kd in ("row", "par"):
                assert acc == "all" or hp == nh
                store(r, shared[idx], jnp.logical_and(g0 == 0, g1 == 0) if acc == "all" else None)

    operands = list(ins) + flat_cts + list(extras)
    in_specs = [_ew_spec(kd, off, w, tb, hp, order, a.shape) for (a, kd, off, w) in operands]
    out_specs = [_ew_spec(kd, 0, w, tb, hp, order, shp) for (shp, kd, w, _, _) in outs]
    out_shape = [jax.ShapeDtypeStruct(shp, dt) for (shp, _, _, dt, _) in outs]
    return pl.pallas_call(
        body, name=name, grid=_ew_grid(rows, tb, nh, hp, order), in_specs=in_specs, out_specs=out_specs,
        out_shape=out_shape, compiler_params=_params(2),
    )(*[a for (a, _, _, _) in operands])


def f_rms(h, x, g):
    r = lax.rsqrt(jnp.mean(x * x, axis=-1, keepdims=True) + EPS)
    return (x * r * g,)


def _softplus(z):
    return jnp.maximum(z, 0.0) + jnp.log1p(jnp.exp(-jnp.abs(z)))


def f_small(h, sp, p1, p2):
    lane = _iota(sp.shape, 1)
    z = sp + p1
    g = -jnp.exp(p2) * _softplus(z)
    beta = jax.nn.sigmoid(z)
    logf = -_softplus(-z)
    return (jnp.where(lane < NH, g, jnp.where(lane < 2 * NH, beta, jnp.where(lane < 3 * NH, logf, 0.0))),)


def _pick(x, lane_id):
    lane = _iota(x.shape, 1)
    col = jnp.sum(jnp.where(lane == lane_id, x, 0.0), axis=1, keepdims=True)
    return jnp.broadcast_to(col, x.shape)


def f_bcast(h, so, cs):
    return _pick(so, h), _pick(so, h + NH), _pick(cs, h + 2 * NH)


def _shift_down(s):
    def down(x):
        r = pltpu.roll(x, s, 0)
        head = jnp.where(_iota((8, x.shape[1]), 0) >= s, r[:8], 0.0)
        return jnp.concatenate([head, r[8:]], axis=0)

    def up(g):
        n = g.shape[0]
        r = pltpu.roll(g, n - s, 0)
        tail = jnp.where(_iota((8, g.shape[1]), 0) < 8 - s, r[n - 8:], 0.0)
        return jnp.concatenate([r[:n - 8], tail], axis=0)

    @jax.custom_vjp
    def shift(x):
        return down(x)

    shift.defvjp(lambda x: (down(x), None), lambda _, g: (up(g),))
    return shift


def _silu(x):
    return x * jax.nn.sigmoid(x)


def make_f_conv(mode):
    sh1, sh2, sh3 = _shift_down(1), _shift_down(2), _shift_down(3)

    def f(h, x, w):
        sub = _iota(w.shape, 0)

        def tap(i):
            return jnp.sum(jnp.where(sub == i, w, 0.0), axis=0, keepdims=True)

        y = sh3(x) * tap(0)
        y = y + sh2(x) * tap(1)
        y = y + sh1(x) * tap(2)
        y = y + x * tap(3)
        s = _silu(y)
        if mode == "v":
            return (s,)
        n = s * lax.rsqrt(jnp.sum(s * s, axis=-1, keepdims=True) + EPS)
        if mode == "q":
            n = n * (LANES ** -0.5)
        return (n,)

    return f


def f_post(h, o, z, g):
    r = lax.rsqrt(jnp.mean(o * o, axis=-1, keepdims=True) + EPS)
    return (o * r * g * _silu(z),)


def f_merge(h, ga, gb, ya, yb):
    return (jax.nn.sigmoid(ga) * ya + jax.nn.sigmoid(gb) * yb,)


def cumsum_time(name, x, nseq, seq, reverse):
    nb = seq // LANES

    def body(x_ref, o_ref):
        r, c = _iota((LANES, LANES), 0), _iota((LANES, LANES), 1)
        tri = jnp.where((r <= c) if reverse else (r >= c), 1.0, 0.0).astype(F32)
        carry = jnp.zeros((1, LANES), F32)
        for b in (range(nb - 1, -1, -1) if reverse else range(nb)):
            blk = x_ref[b * LANES:(b + 1) * LANES, :]
            o_ref[b * LANES:(b + 1) * LANES, :] = _dot_mask(tri, blk, "nn") + carry
            carry = carry + jnp.sum(blk, axis=0, keepdims=True)

    spec = pl.BlockSpec((seq, LANES), lambda s: (s, 0))
    return pl.pallas_call(body, name=name, grid=(nseq,), in_specs=[spec], out_specs=spec,
                          out_shape=jax.ShapeDtypeStruct(x.shape, F32), compiler_params=_params(1))(x)


def transpose_time(name, x, nseq, seq):
    def body(x_ref, o_ref):
        o_ref[...] = x_ref[...].T

    return pl.pallas_call(
        body, name=name, grid=(nseq,), in_specs=[pl.BlockSpec((seq, LANES), lambda s: (s, 0))],
        out_specs=pl.BlockSpec((LANES, seq), lambda s: (s, 0)),
        out_shape=jax.ShapeDtypeStruct((nseq * LANES, seq), F32), compiler_params=_params(1))(x)


def _gdn_masks():
    n = GDN_ROWS
    r, c = _iota((n, n), 0), _iota((n, n), 1)
    shift = GDN_CHUNK.bit_length() - 1
    same = lax.shift_right_logical(r, shift) == lax.shift_right_logical(c, shift)
    return r, c, same


def _each(fn, *lists):
    return [fn(*xs) for xs in zip(*lists)]


def _gdn_decay(gbs):
    r, c, same = _gdn_masks()
    seg_tril = jnp.where(jnp.logical_and(same, r >= c), 1.0, 0.0).astype(F32)
    g_cum = _each(lambda gb: mm_mask(seg_tril, gb), gbs)
    lane0 = _iota(gbs[0].shape, 1) == 0
    g_col = _each(lambda g: jnp.sum(jnp.where(lane0, g, 0.0), axis=1, keepdims=True), g_cum)
    g_row = _each(lambda g: jnp.sum(jnp.where(r == c, jnp.broadcast_to(g, (GDN_ROWS, GDN_ROWS)), 0.0), axis=0, keepdims=True), g_col)
    return g_cum, _each(lambda a, b: a - b, g_col, g_row)


def gdn_a_mats(ks, bbs, diff):
    r, c, same = _gdn_masks()
    strict = jnp.logical_and(same, r > c)
    lane0 = _iota(bbs[0].shape, 1) == 0
    beta_col = _each(lambda bb: jnp.sum(jnp.where(lane0, bb, 0.0), axis=1, keepdims=True), bbs)
    kk = _each(lambda k: _dot(k, k, "nt", LO), ks)
    return _each(lambda b, x, d: jnp.where(strict, b * x * jnp.exp(jnp.where(strict, d, 0.0)), 0.0), beta_col, kk, diff)


@jax.custom_vjp
def saved_inverse(a, t_corr):
    return t_corr


def _saved_inverse_bwd(t, dt):
    left = dt + _dot(t, dt, "tn", LO)
    return -(left + _dot(left, t, "nt", LO)), jnp.zeros_like(t)


saved_inverse.defvjp(lambda a, t_corr: (t_corr, t_corr), _saved_inverse_bwd)


def gdn_block(*args):
    ts, qs, ks, vs, gbs, bbs = (list(args[i::6]) for i in range(6))
    g_cum, diff = _gdn_decay(gbs)
    ts = _each(saved_inverse, gdn_a_mats(ks, bbs, diff), ts)
    return gdn_outputs(ts, qs, ks, vs, gbs, bbs, g_cum, diff)


def gdn_outputs(ts, qs, ks, vs, gbs, bbs, g_cum, diff):
    r, c, same = _gdn_masks()
    incl = jnp.logical_and(same, r >= c)
    decay = _each(lambda d: jnp.where(incl, jnp.exp(jnp.where(incl, d, 0.0)), 0.0), diff)
    e_g = _each(jnp.exp, g_cum)
    v_beta = _each(lambda v, bb: v * bb, vs, bbs)
    k_beta = _each(lambda k, bb, e: k * bb * e, ks, bbs, e_g)
    value = _each(lambda t, x: x + _dot(t, x, "nn", LO), ts, v_beta)
    k_cum = _each(lambda t, x: x + _dot(t, x, "nn", LO), ts, k_beta)
    attn = _each(lambda q, k, d: _dot(q, k, "nt", LO) * d, qs, ks, decay)
    ones = jnp.where(same, 1.0, 0.0).astype(F32)
    g_last = _each(lambda gb: mm_mask(ones, gb), gbs)
    q_dec = _each(lambda q, e: q * e, qs, e_g)
    k_dec = _each(lambda k, gl, g: k * jnp.exp(gl - g), ks, g_last, g_cum)
    return tuple(x for head in zip(value, k_cum, attn, q_dec, k_dec) for x in head)


def tri_inverse(mats):
    n = GDN_ROWS
    r, c = _iota((n, n), 0), _iota((n, n), 1)
    shift = GDN_BASE.bit_length() - 1
    blk = lax.shift_right_logical(r, shift) == lax.shift_right_logical(c, shift)
    each = lambda fn, *lists: [fn(*xs) for xs in zip(*lists)]
    mm = lambda x, y: _dot(x, y, "nn", LO)
    d = each(lambda a: jnp.where(blk, a, 0.0), mats)
    lo = each(lambda a, dd: a - dd, mats, d)
    p = each(lambda dd: -dd, d)
    c_d = p
    for _ in range(shift - 1):
        p = each(mm, p, p)
        c_d = each(lambda cd, pp, prod: cd + pp + prod, c_d, p, each(mm, c_d, p))
    assert GDN_CHUNK // GDN_BASE == 4
    nmat = each(lambda l, prod: l + prod, lo, each(mm, c_d, lo))
    n2 = each(mm, nmat, nmat)
    c_n = each(lambda nn2, nm, prod: (nn2 - nm) - prod, n2, nmat, each(mm, nmat, n2))
    return each(lambda cn, cd, prod: cn + cd + prod, c_n, c_d, each(mm, c_n, c_d))


GDN_AHP = 8


def _gdn_a_specs():
    blk = pl.BlockSpec((GDN_ROWS, GDN_AHP * LANES), lambda i, h: (i, h))
    sq = pl.BlockSpec((GDN_ROWS, GDN_AHP * GDN_ROWS), lambda i, h: (i, h))
    return blk, sq


def _head(ref, hh):
    width = ref.shape[1] // GDN_AHP
    return ref.at[:, hh * width:(hh + 1) * width]


def gdn_a_fwd(q, k, v, gb, bb, rows):
    blk, sq = _gdn_a_specs()

    def body(q_ref, k_ref, v_ref, gb_ref, bb_ref, val_ref, kc_ref, at_ref, qd_ref, kd_ref, t_ref):
        heads = [[_head(r, hh)[...] for r in (q_ref, k_ref, v_ref, gb_ref, bb_ref)] for hh in range(GDN_AHP)]
        qs, ks, vs, gbs, bbs = (list(col) for col in zip(*heads))
        g_cum, diff = _gdn_decay(gbs)
        t_corr = tri_inverse(gdn_a_mats(ks, bbs, diff))
        res = gdn_outputs(t_corr, qs, ks, vs, gbs, bbs, g_cum, diff)
        for hh in range(GDN_AHP):
            for r, x in zip((val_ref, kc_ref, at_ref, qd_ref, kd_ref, t_ref), (*res[5 * hh:5 * hh + 5], t_corr[hh])):
                _head(r, hh)[...] = x.astype(r.dtype)

    wide = lambda dt: jax.ShapeDtypeStruct((rows, NH * LANES), dt)
    square = jax.ShapeDtypeStruct((rows, NH * GDN_ROWS), BF16)
    return pl.pallas_call(
        body, name="gdn_a_fwd", grid=(rows // GDN_ROWS, NH // GDN_AHP), in_specs=[blk] * 5,
        out_specs=[blk, blk, sq, blk, blk, sq], out_shape=[wide(F32), wide(BF16), square, wide(BF16), wide(BF16), square],
        compiler_params=_params(2))(q, k, v, gb, bb)


def gdn_a_bwd(q, k, v, gb, bb, t_inv, dval, dkc, dat, dqd, dkd, dgb_b, rows):
    blk, sq = _gdn_a_specs()

    def body(q_ref, k_ref, v_ref, gb_ref, bb_ref, t_ref, dval_ref, dkc_ref, dat_ref, dqd_ref, dkd_ref, dgbb_ref,
             dq_ref, dk_ref, dv_ref, dgb_ref, dbb_ref):
        hs = range(GDN_AHP)
        heads = [[_head(r, hh)[...] for r in (q_ref, k_ref, v_ref, gb_ref, bb_ref)] for hh in hs]
        tvs = [_head(t_ref, hh)[...].astype(F32) for hh in hs]
        _, vjp = jax.vjp(gdn_block, *[x for t, head in zip(tvs, heads) for x in (t, *head)])
        grads = vjp(tuple(_head(r, hh)[...] for hh in hs for r in (dval_ref, dkc_ref, dat_ref, dqd_ref, dkd_ref)))
        for hh in hs:
            _, dq, dk, dv, dgb, dbb = grads[6 * hh:6 * hh + 6]
            _head(dq_ref, hh)[...] = dq
            _head(dk_ref, hh)[...] = dk
            _head(dv_ref, hh)[...] = dv
            _head(dgb_ref, hh)[...] = dgb + _head(dgbb_ref, hh)[...]
            _head(dbb_ref, hh)[...] = dbb

    wide = jax.ShapeDtypeStruct((rows, NH * LANES), F32)
    return pl.pallas_call(
        body, name="gdn_a_bwd", grid=(rows // GDN_ROWS, NH // GDN_AHP),
        in_specs=[blk] * 5 + [sq, blk, blk, sq, blk, blk, blk], out_specs=[blk] * 5, out_shape=[wide] * 5,
        compiler_params=_params(2))(q, k, v, gb, bb, t_inv, dval, dkc, dat, dqd, dkd, dgb_b)


N_CH = GDN_ROWS // GDN_CHUNK


GDN_HP = 8


def gdn_chunk(c):
    def f(*args):
        val, kc, at, qd, kd, gb, s = (list(args[i::7]) for i in range(7))
        zero = jnp.zeros((GDN_CHUNK, LANES), F32)
        v_new = _each(lambda v, k, st: v - _dot(k, st, "nn", LO), val, kc, s)
        v_pad = _each(lambda v: jnp.concatenate([zero] * c + [v] + [zero] * (N_CH - 1 - c), axis=0), v_new)
        out = _each(lambda q, st, a, vp: _dot(q, st, "nn", LO) + _dot(a, vp, "nn", LO), qd, s, at, v_pad)
        dec = _each(lambda g: jnp.exp(jnp.sum(g, axis=0, keepdims=True)), gb)
        s_new = _each(lambda st, d, k, v: st * d + _dot(k, v, "tn", LO), s, dec, kd, v_new)
        return tuple(x for head in zip(out, s_new) for x in head)

    return f


def _gdn_piece(ref, hh, c):
    width = ref.shape[1] // GDN_HP
    return ref.at[c * GDN_CHUNK:(c + 1) * GDN_CHUNK, hh * width:(hh + 1) * width]


def _gdn_snap(ref, hh, c):
    row = (hh * N_CH + c) * LANES
    return ref.at[row:row + LANES, :]


def _gdn_b_specs(nb, rev):
    def blk_row(s, j):
        return s * nb + (nb - 1 - j if rev else j)

    blk = pl.BlockSpec((GDN_ROWS, GDN_HP * LANES), lambda s, hb, j: (blk_row(s, j), hb))
    sq = pl.BlockSpec((GDN_ROWS, GDN_HP * GDN_ROWS), lambda s, hb, j: (blk_row(s, j), hb))
    snap = pl.BlockSpec((GDN_HP * N_CH * LANES, LANES), lambda s, hb, j: (blk_row(s, j) * (NH // GDN_HP) + hb, 0))
    return blk, sq, snap


def gdn_b_fwd(val, kc, at, qd, kd, gb, nseq, seq):
    nb = seq // GDN_ROWS
    rows = nseq * seq
    blk, sq, snap = _gdn_b_specs(nb, False)

    def body(val_ref, kc_ref, at_ref, qd_ref, kd_ref, gb_ref, o_ref, snap_ref, s_ref):
        @pl.when(pl.program_id(2) == 0)
        def _():
            s_ref[...] = jnp.zeros_like(s_ref)

        hs = range(GDN_HP)
        states = [s_ref[hh] for hh in hs]
        for c in range(N_CH):
            for hh in hs:
                _gdn_snap(snap_ref, hh, c)[...] = states[hh]
            res = gdn_chunk(c)(*[x for hh in hs for x in (
                *[_gdn_piece(r, hh, c)[...].astype(F32) for r in (val_ref, kc_ref, at_ref, qd_ref, kd_ref, gb_ref)], states[hh])])
            for hh in hs:
                _gdn_piece(o_ref, hh, c)[...] = res[2 * hh]
            states = [res[2 * hh + 1] for hh in hs]
        for hh in hs:
            s_ref[hh] = states[hh]

    return pl.pallas_call(
        body, name="gdn_b_fwd", grid=(nseq, NH // GDN_HP, nb), in_specs=[blk, blk, sq, blk, blk, blk], out_specs=[blk, snap],
        out_shape=[jax.ShapeDtypeStruct((rows, NH * LANES), F32),
                   jax.ShapeDtypeStruct((nseq * nb * NH * N_CH * LANES, LANES), F32)],
        scratch_shapes=[pltpu.VMEM((GDN_HP, LANES, LANES), F32)], compiler_params=_params(3))(val, kc, at, qd, kd, gb)


def gdn_b_bwd(val, kc, at, qd, kd, gb, snaps, do, nseq, seq):
    nb = seq // GDN_ROWS
    rows = nseq * seq
    blk, sq, snap = _gdn_b_specs(nb, True)

    def body(val_ref, kc_ref, at_ref, qd_ref, kd_ref, gb_ref, snap_ref, do_ref,
             dval_ref, dkc_ref, dat_ref, dqd_ref, dkd_ref, dgb_ref, ds_ref):
        @pl.when(pl.program_id(2) == 0)
        def _():
            ds_ref[...] = jnp.zeros_like(ds_ref)

        hs = range(GDN_HP)
        d_states = [ds_ref[hh] for hh in hs]
        for c in reversed(range(N_CH)):
            _, vjp = jax.vjp(gdn_chunk(c), *[x for hh in hs for x in (
                *[_gdn_piece(r, hh, c)[...].astype(F32) for r in (val_ref, kc_ref, at_ref, qd_ref, kd_ref, gb_ref)],
                _gdn_snap(snap_ref, hh, c)[...])])
            grads = vjp(tuple(x for hh in hs for x in (_gdn_piece(do_ref, hh, c)[...], d_states[hh])))
            for hh in hs:
                for i, r in enumerate([dval_ref, dkc_ref, dat_ref, dqd_ref, dkd_ref, dgb_ref]):
                    _gdn_piece(r, hh, c)[...] = grads[7 * hh + i]
            d_states = [grads[7 * hh + 6] for hh in hs]
        for hh in hs:
            ds_ref[hh] = d_states[hh]

    wide = jax.ShapeDtypeStruct((rows, NH * LANES), F32)
    square = jax.ShapeDtypeStruct((rows, NH * GDN_ROWS), F32)
    return pl.pallas_call(
        body, name="gdn_b_bwd", grid=(nseq, NH // GDN_HP, nb), in_specs=[blk, blk, sq, blk, blk, blk, snap, blk],
        out_specs=[blk, blk, sq, blk, blk, blk], out_shape=[wide, wide, square, wide, wide, wide],
        scratch_shapes=[pltpu.VMEM((GDN_HP, LANES, LANES), F32)], compiler_params=_params(3))(val, kc, at, qd, kd, gb, snaps, do)


FOX_Q, FOX_K, FOX_V = 4 * NH, 5 * NH, 6 * NH
FOX_SCALE = LANES ** -0.5


def _head_row(ct_ref, h, off, width):
    blk = ct_ref[:, pl.ds(off, width)]
    return jnp.sum(jnp.where(_iota(blk.shape, 0) == h, blk, 0.0), axis=0, keepdims=True)


def _col(x):
    return jnp.max(x, axis=1, keepdims=True)


def _row(x):
    return jnp.max(x.T, axis=0, keepdims=True)


def _causal(shape, q_dim):
    return _iota(shape, q_dim) >= _iota(shape, 1 - q_dim)


FOX_HP = 4


def _fox_specs(seq, tile, n_tiles):
    tblk = pl.BlockSpec((tile, FOX_HP * LANES), lambda s, h, i: (s * n_tiles + i, h))
    vtblk = pl.BlockSpec((tile, FOX_HP * LANES), lambda s, h, i: (s * n_tiles + i, h + FOX_V // FOX_HP))
    full = pl.BlockSpec((seq, FOX_HP * LANES), lambda s, h, i: (s, h))
    vfull = pl.BlockSpec((seq, FOX_HP * LANES), lambda s, h, i: (s, h + FOX_V // FOX_HP))
    ctb = pl.BlockSpec((NH, seq), lambda s, h, i: (s * (LANES // NH) + 2, 0))
    return tblk, vtblk, full, vfull, ctb


def _lanes_of(hh):
    return slice(hh * LANES, (hh + 1) * LANES)


def fox_fwd(qn, kn, proj, ct, nseq, seq):
    tq = tk = min(ATT_TILE, seq)
    nq = seq // tq
    rows = nseq * seq
    qblk, _, full, vfull, ctb = _fox_specs(seq, tq, nq)
    hs = range(FOX_HP)

    def body(q_ref, k_ref, v_ref, ct_ref, o_ref, o16_ref, lse_ref):
        hb, i = pl.program_id(1), pl.program_id(2)
        q = [q_ref[:, _lanes_of(hh)] for hh in hs]

        def step(j, carry, diag):
            m, l, acc = (list(carry[t::3]) for t in range(3))
            off = pl.multiple_of(j * tk, tk)
            k = [k_ref[pl.ds(off, tk), _lanes_of(hh)] for hh in hs]
            v = [v_ref[pl.ds(off, tk), _lanes_of(hh)].astype(BF16) for hh in hs]
            ck = [_head_row(ct_ref, hb * FOX_HP + hh, off, tk) for hh in hs]
            s = _each(lambda qq, kk, cc: _dot(qq, kk, "nt") * FOX_SCALE - cc, q, k, ck)
            if diag:
                s = _each(lambda x: jnp.where(_causal(x.shape, 0), x, NEG), s)
            m_new = _each(lambda mm, x: jnp.maximum(mm, jnp.max(x, axis=1, keepdims=True)), m, s)
            p = _each(lambda x, mm: jnp.exp(x - mm), s, m_new)
            alpha = _each(lambda mo, mn: jnp.exp(mo - mn), m, m_new)
            l = _each(lambda a, ll, pp: a * ll + jnp.sum(pp, axis=1, keepdims=True), alpha, l, p)
            acc = _each(lambda a, ac, pp, vv: a * ac + _dot(pp.astype(BF16), vv, "nn"), alpha, acc, p, v)
            return tuple(x for head in zip(m_new, l, acc) for x in head)

        init = (jnp.full((tq, 1), NEG, F32), jnp.zeros((tq, 1), F32), jnp.zeros((tq, LANES), F32)) * FOX_HP
        res = step(i, lax.fori_loop(0, i, lambda j, c: step(j, c, False), init), True)
        for hh in hs:
            m, l, acc = res[3 * hh:3 * hh + 3]
            o = acc / l
            o_ref[:, _lanes_of(hh)] = o
            o16_ref[:, _lanes_of(hh)] = o.astype(BF16)
            lse_ref[:, _lanes_of(hh)] = jnp.broadcast_to(m + jnp.log(l), (tq, LANES))

    wide = (rows, NH * LANES)
    return pl.pallas_call(
        body, name="fox_fwd", grid=(nseq, NH // FOX_HP, nq), in_specs=[qblk, full, vfull, ctb], out_specs=[qblk] * 3,
        out_shape=[jax.ShapeDtypeStruct(wide, F32), jax.ShapeDtypeStruct(wide, BF16), jax.ShapeDtypeStruct(wide, F32)],
        compiler_params=_params(3))(qn, kn, proj, ct)


def fox_dq(qn, kn, proj, ct, do, lse, o, after, nseq, seq):
    tq = tk = min(ATT_TILE, seq)
    nq = seq // tq
    rows = nseq * seq
    qblk, _, full, vfull, ctb = _fox_specs(seq, tq, nq)
    hs = range(FOX_HP)

    def body(q_ref, k_ref, v_ref, ct_ref, do_ref, lse_ref, o_ref, *rest):
        dq_ref, dc_ref = rest[len(after):]
        hb, i = pl.program_id(1), pl.program_id(2)
        q = [q_ref[:, _lanes_of(hh)] for hh in hs]
        lse = [_col(lse_ref[:, _lanes_of(hh)]) for hh in hs]
        delta = [jnp.sum(do_ref[:, _lanes_of(hh)] * o_ref[:, _lanes_of(hh)], axis=1, keepdims=True) for hh in hs]
        do16 = [do_ref[:, _lanes_of(hh)].astype(BF16) for hh in hs]

        def step(j, carry, diag):
            dq, dc = (list(carry[t::2]) for t in range(2))
            off = pl.multiple_of(j * tk, tk)
            k = [k_ref[pl.ds(off, tk), _lanes_of(hh)] for hh in hs]
            v = [v_ref[pl.ds(off, tk), _lanes_of(hh)].astype(BF16) for hh in hs]
            ck = [_head_row(ct_ref, hb * FOX_HP + hh, off, tk) for hh in hs]
            p = _each(lambda qq, kk, cc, ll: jnp.exp(_dot(qq, kk, "nt") * FOX_SCALE - cc - ll), q, k, ck, lse)
            if diag:
                p = _each(lambda x: jnp.where(_causal(x.shape, 0), x, 0.0), p)
            dp = _each(lambda d, vv: _dot(d, vv, "nt"), do16, v)
            ds = _each(lambda pp, d, dl: pp * (d - dl), p, dp, delta)
            dq = _each(lambda a, x, kk: a + _dot(x.astype(BF16), kk, "nn"), dq, ds, k)
            dc = _each(lambda a, x: a + jnp.sum(x, axis=1, keepdims=True), dc, ds)
            return tuple(x for head in zip(dq, dc) for x in head)

        init = (jnp.zeros((tq, LANES), F32), jnp.zeros((tq, 1), F32)) * FOX_HP
        res = step(i, lax.fori_loop(0, i, lambda j, c: step(j, c, False), init), True)
        for hh in hs:
            dq_ref[:, _lanes_of(hh)] = res[2 * hh] * FOX_SCALE
            dc_ref[:, _lanes_of(hh)] = jnp.where(_iota((tq, LANES), 1) == 0, res[2 * hh + 1], 0.0)

    wide = jax.ShapeDtypeStruct((rows, NH * LANES), F32)
    return pl.pallas_call(
        body, name="fox_dq", grid=(nseq, NH // FOX_HP, nq), in_specs=[qblk, full, vfull, ctb, qblk, qblk, qblk] + [ANY] * len(after),
        out_specs=[qblk, qblk], out_shape=[wide, wide], compiler_params=_params(3))(qn, kn, proj, ct, do, lse, o, *after)


def fox_dkv(qn, kn, proj, cb, do, lse, o, after, nseq, seq):
    tq = tk = min(ATT_TILE, seq)
    nq = seq // tq
    rows = nseq * seq
    kblk, vblk, full, _, _ = _fox_specs(seq, tk, nq)
    hs = range(FOX_HP)

    def body(q_ref, k_ref, v_ref, cb_ref, do_ref, lse_ref, o_ref, *rest):
        dk_ref, dv_ref, dc_ref = rest[len(after):]
        j = pl.program_id(2)
        k = [k_ref[:, _lanes_of(hh)] for hh in hs]
        v16 = [v_ref[:, _lanes_of(hh)].astype(BF16) for hh in hs]
        ck = [_col(cb_ref[:, _lanes_of(hh)]) for hh in hs]

        def step(i, carry, diag):
            dk, dv, dc = (list(carry[t::3]) for t in range(3))
            off = pl.multiple_of(i * tq, tq)
            q = [q_ref[pl.ds(off, tq), _lanes_of(hh)] for hh in hs]
            do32 = [do_ref[pl.ds(off, tq), _lanes_of(hh)] for hh in hs]
            do16 = [d.astype(BF16) for d in do32]
            lse = [_row(lse_ref[pl.ds(off, tq), _lanes_of(hh)]) for hh in hs]
            delta = [_row(jnp.broadcast_to(jnp.sum(d * o_ref[pl.ds(off, tq), _lanes_of(hh)], axis=1, keepdims=True), (tq, LANES)))
                     for hh, d in zip(hs, do32)]
            p = _each(lambda kk, qq, cc, ll: jnp.exp(_dot(kk, qq, "nt") * FOX_SCALE - cc - ll), k, q, ck, lse)
            if diag:
                p = _each(lambda x: jnp.where(_causal(x.shape, 1), x, 0.0), p)
            dv = _each(lambda a, pp, d: a + _dot(pp.astype(BF16), d, "nn"), dv, p, do16)
            ds = _each(lambda pp, vv, d, dl: pp * (_dot(vv, d, "nt") - dl), p, v16, do16, delta)
            dk = _each(lambda a, x, qq: a + _dot(x.astype(BF16), qq, "nn"), dk, ds, q)
            dc = _each(lambda a, x: a + jnp.sum(x, axis=1, keepdims=True), dc, ds)
            return tuple(x for head in zip(dk, dv, dc) for x in head)

        zero = jnp.zeros((tk, LANES), F32)
        carry = step(j, (zero, zero, jnp.zeros((tk, 1), F32)) * FOX_HP, True)
        res = lax.fori_loop(j + 1, nq, lambda i, c: step(i, c, False), carry)
        for hh in hs:
            dk, dv, dc = res[3 * hh:3 * hh + 3]
            dk_ref[:, _lanes_of(hh)] = dk * FOX_SCALE
            dv_ref[:, _lanes_of(hh)] = dv.astype(BF16)
            dc_ref[:, _lanes_of(hh)] = jnp.where(_iota((tk, LANES), 1) == 0, -dc, 0.0)

    wide = (rows, NH * LANES)
    return pl.pallas_call(
        body, name="fox_dkv", grid=(nseq, NH // FOX_HP, nq), in_specs=[full, kblk, vblk, kblk, full, full, full] + [ANY] * len(after),
        out_specs=[kblk, kblk, kblk],
        out_shape=[jax.ShapeDtypeStruct(wide, F32), jax.ShapeDtypeStruct(wide, BF16), jax.ShapeDtypeStruct(wide, F32)],
        compiler_params=_params(3))(qn, kn, proj, cb, do, lse, o, *after)


def _adamw_update(w, g, m, v):
    m_new = ADAM_B1 * m + (1.0 - ADAM_B1) * g
    v_new = ADAM_B2 * v + (1.0 - ADAM_B2) * (g * g)
    m_hat = m_new / (1.0 - ADAM_B1 ** ADAM_STEP)
    v_hat = v_new / (1.0 - ADAM_B2 ** ADAM_STEP)
    return -ADAM_LR * (m_hat / (jnp.sqrt(v_hat) + ADAM_EPS) + ADAM_WD * w), m_new, v_new


def adamw(name, w, g, m, v):
    rows, cols = w.shape
    tb = min(rows, 128)
    assert rows % tb == 0
    blk = pl.BlockSpec((tb, cols), lambda i: (i, 0))

    def body(w_ref, g_ref, m_ref, v_ref, d_ref, mo_ref, vo_ref):
        d_ref[...], mo_ref[...], vo_ref[...] = _adamw_update(w_ref[...], g_ref[...], m_ref[...], v_ref[...])

    shp = jax.ShapeDtypeStruct(w.shape, F32)
    return pl.pallas_call(body, name=name, grid=(rows // tb,), in_specs=[blk] * 4, out_specs=[blk] * 3,
                          out_shape=[shp] * 3, compiler_params=_params(1))(w, g, m, v)


SPLIT_TILE = 128


def _tiled(shape2d, ax, n_lead, index):
    blk = (SPLIT_TILE, shape2d[1]) if ax == 0 else (shape2d[0], SPLIT_TILE)

    def index_map(*args):
        *lead, t = index(*args)
        return (*lead, t, 0) if ax == 0 else (*lead, 0, t)

    return pl.BlockSpec((None,) * n_lead + blk, index_map)


def adamw_halves(name, w, mine, other, m, v, c, ax):
    steps = w.shape[ax] // 2 // SPLIT_TILE
    assert w.shape[ax] == 2 * steps * SPLIT_TILE

    def body(c_ref, w_ref, mine_ref, other_ref, m_ref, v_ref, g_ref, d_ref, mo_ref, vo_ref):
        g = jnp.where(pl.program_id(0) // steps == c_ref[0], mine_ref[...], other_ref[...])
        g_ref[...] = g
        d_ref[...], mo_ref[...], vo_ref[...] = _adamw_update(w_ref[...], g, m_ref[...], v_ref[...])

    blk = _tiled(w.shape, ax, 0, lambda i, c_ref: (i,))
    hblk = _tiled(mine.shape, ax, 0, lambda i, c_ref: (i % steps,))
    grid_spec = pltpu.PrefetchScalarGridSpec(num_scalar_prefetch=1, grid=(2 * steps,),
                                             in_specs=[blk, hblk, hblk, blk, blk], out_specs=[blk] * 4)
    shp = jax.ShapeDtypeStruct(w.shape, F32)
    return pl.pallas_call(body, name=name, grid_spec=grid_spec, out_shape=[shp] * 4,
                          compiler_params=_params(1))(c, w, mine, other, m, v)


def add_chips(name, slots, parts, chip, axes):
    outs = []
    for idx, (x, own, ax) in enumerate(zip(slots, parts, axes)):
        n, shape2d = x.shape[0], x.shape[1:]
        steps = shape2d[ax] // SPLIT_TILE
        assert shape2d[ax] == steps * SPLIT_TILE

        def body(me_ref, *refs, n=n):
            o_ref = refs[n + 1]
            acc = None
            for t in range(n):
                term = jnp.where(me_ref[0] == t, refs[n][...], refs[t][...]).astype(F32)
                acc = term if acc is None else acc + term
            o_ref[...] = acc

        def filled(t, n=n):
            return lambda i, me_ref: (jnp.where(me_ref[0] == t, (t + 1) % n, t), i)

        grid_spec = pltpu.PrefetchScalarGridSpec(
            num_scalar_prefetch=1, grid=(steps,),
            in_specs=[_tiled(shape2d, ax, 1, filled(t)) for t in range(n)]
            + [_tiled(shape2d, ax, 1, lambda i, me_ref: (me_ref[0], i))],
            out_specs=_tiled(shape2d, ax, 0, lambda i, me_ref: (i,)))
        outs.append(pl.pallas_call(
            body, name=f"{name}_{idx}", grid_spec=grid_spec, out_shape=jax.ShapeDtypeStruct(shape2d, F32),
            compiler_params=_params(1))(chip, *([x] * n), own))
    return outs


def add_pair(name, gs, rs, c, axes):
    outs = []
    for idx, (g, r, ax) in enumerate(zip(gs, rs, axes)):
        nb = r.shape[0]
        steps = r.shape[1 + ax] // SPLIT_TILE
        assert r.shape[1 + ax] == steps * SPLIT_TILE

        def body(c_ref, g_ref, r_ref, o_ref):
            o_ref[...] = (g_ref[...] + r_ref[...]).astype(BF16)

        grid_spec = pltpu.PrefetchScalarGridSpec(
            num_scalar_prefetch=1, grid=(nb, steps),
            in_specs=[_tiled(g.shape[1:], ax, 1, lambda b, i, c_ref: (b, c_ref[0] * steps + i)),
                      _tiled(r.shape[1:], ax, 1, lambda b, i, c_ref: (b, i))],
            out_specs=_tiled(r.shape[1:], ax, 1, lambda b, i, c_ref: (b, i)))
        outs.append(pl.pallas_call(
            body, name=f"{name}_{idx}", grid_spec=grid_spec, out_shape=jax.ShapeDtypeStruct(r.shape, BF16),
            compiler_params=_params(2))(c, g, r))
    return outs


def _place():
    x, y, c = lax.axis_index("x"), lax.axis_index("y"), lax.axis_index("c")
    return x, y, c, [(1 - x, y), (x, 1 - y), (1 - x, 1 - y)]


def _remote(src, dst, send_sem, recv_sem, dev):
    return pltpu.make_async_remote_copy(src_ref=src, dst_ref=dst, send_sem=send_sem, recv_sem=recv_sem,
                                        device_id=dev, device_id_type=MESH)


def _half(ref, lead, ax, which):
    size = ref.shape[len(lead) + ax] // 2
    part = pl.ds(which * size, size)
    return ref.at[(*lead, part, slice(None)) if ax == 0 else (*lead, slice(None), part)]


def gather_ring(shard):
    rows, cols = shard.shape
    half = cols // 2
    top = rows // 2 // 16 * 16
    assert shard.dtype == BF16 and half % LANES == 0

    def body(in_ref, out_ref, ici_s, ici_r, d2d_s, d2d_r):
        x, y, c, _ = _place()
        me, xn, yn, dg = 2 * x + y, 2 * (1 - x) + y, 2 * x + (1 - y), 2 * (1 - x) + (1 - y)
        to_x, to_y, sib = (1 - x, y, c), (x, 1 - y, c), (x, y, 1 - c)
        mine, other = pl.ds(c * half, half), pl.ds((1 - c) * half, half)
        upper, lower = pl.ds(0, top), pl.ds(top, rows - top)
        started = []

        def send(src, dst, sems, k, dev):
            cp = _remote(src, dst, sems[0].at[k], sems[1].at[k], dev)
            cp.start()
            started.append(cp)

        def arrive(dst, sems, k):
            _remote(dst, dst, sems[0].at[k], sems[1].at[k], sib).wait_recv()

        ici, d2d = (ici_s, ici_r), (d2d_s, d2d_r)
        send(in_ref, out_ref.at[me], d2d, 0, sib)
        send(in_ref.at[:, mine], out_ref.at[me, :, mine], ici, 0, to_x)
        send(in_ref.at[:, mine], out_ref.at[me, :, mine], ici, 1, to_y)
        arrive(out_ref.at[xn, :, mine], ici, 0)
        send(out_ref.at[xn, upper, mine], out_ref.at[xn, upper, mine], ici, 2, to_y)
        send(out_ref.at[xn, :, mine], out_ref.at[xn, :, mine], d2d, 1, sib)
        arrive(out_ref.at[yn, :, mine], ici, 1)
        send(out_ref.at[yn, lower, mine], out_ref.at[yn, lower, mine], ici, 3, to_x)
        send(out_ref.at[yn, :, mine], out_ref.at[yn, :, mine], d2d, 2, sib)
        arrive(out_ref.at[dg, upper, mine], ici, 2)
        send(out_ref.at[dg, upper, mine], out_ref.at[dg, upper, mine], d2d, 3, sib)
        arrive(out_ref.at[dg, lower, mine], ici, 3)
        send(out_ref.at[dg, lower, mine], out_ref.at[dg, lower, mine], d2d, 4, sib)
        arrive(out_ref.at[me], d2d, 0)
        arrive(out_ref.at[xn, :, other], d2d, 1)
        arrive(out_ref.at[yn, :, other], d2d, 2)
        arrive(out_ref.at[dg, upper, other], d2d, 3)
        arrive(out_ref.at[dg, lower, other], d2d, 4)
        for cp in started:
            cp.wait_send()

    return pl.pallas_call(
        body, name="gather_ring", in_specs=[ANY], out_specs=ANY, out_shape=jax.ShapeDtypeStruct((4,) + shard.shape, shard.dtype),
        scratch_shapes=[pltpu.SemaphoreType.DMA((4,))] * 2 + [pltpu.SemaphoreType.DMA((5,))] * 2,
    )(shard)


HBM = pl.BlockSpec(memory_space=pltpu.HBM)
SEM = pl.BlockSpec(memory_space=pltpu.SEMAPHORE)
DATAFLOW = pltpu.SideEffectType.DATAFLOW_SIDE_EFFECTING


def _hbm(a):
    return pltpu.with_memory_space_constraint(a, pltpu.HBM)


class SplitExchange:
    def __init__(self, name, srcs, zone_shapes, n_sems, plan):
        self.name, self.n, self.n_sems, self.plan = name, len(srcs), n_sems, plan
        self.srcs = [_hbm(s) for s in srcs]
        self.zones = [_hbm(lax.empty(shape, s.dtype)) for shape, s in zip(zone_shapes, srcs)]

    def start(self, after):
        n, n_after = self.n, len(after)

        def body(*refs):
            ins, lands = refs[:n], refs[n:2 * n]
            send, recv, token = refs[2 * n + n_after], refs[2 * n + n_after + 1], refs[-1]
            for src, dst, si, ri, dev in self.plan(ins, lands)[0]:
                _remote(src, dst, send.at[si], recv.at[ri], dev).start()
            token[...] = jnp.zeros_like(token)

        res = pl.pallas_call(
            body, name=f"{self.name}_start", in_specs=[HBM] * (2 * n) + [ANY] * n_after,
            out_specs=[SEM, SEM] + [HBM] * (2 * n) + [pl.BlockSpec(memory_space=pltpu.VMEM)],
            out_shape=[pltpu.SemaphoreType.DMA((self.n_sems,)), pltpu.SemaphoreType.DMA((self.n_sems,))]
            + [pltpu.HBM(a.shape, a.dtype) for a in self.srcs + self.zones] + [jax.ShapeDtypeStruct((8, LANES), F32)],
            input_output_aliases={i: 2 + i for i in range(2 * n)},
            compiler_params=pltpu.CompilerParams(has_side_effects=DATAFLOW),
        )(*self.srcs, *self.zones, *after)
        self.sems, self.srcs, self.zones = res[:2], list(res[2:2 + n]), list(res[2 + n:2 + 2 * n])
        return res[-1]

    def wait(self, after):
        n = self.n

        def body(*refs):
            ins, lands = refs[:n], refs[n:2 * n]
            send, recv = refs[2 * n], refs[2 * n + 1]
            sends, arrivals = self.plan(ins, lands)
            for src, _, si, _, dev in sends:
                _remote(src, src, send.at[si], recv.at[si], dev).wait_send()
            for landed, ri in arrivals:
                _remote(landed, landed, send.at[ri], recv.at[ri], _place()[:3]).wait_recv()

        res = pl.pallas_call(
            body, name=f"{self.name}_wait", in_specs=[HBM] * (2 * n) + [SEM, SEM, ANY], out_specs=[HBM] * (2 * n),
            out_shape=[pltpu.HBM(a.shape, a.dtype) for a in self.srcs + self.zones],
            input_output_aliases={i: i for i in range(2 * n)},
            compiler_params=pltpu.CompilerParams(has_side_effects=DATAFLOW),
        )(*self.srcs, *self.zones, *self.sems, after)
        self.srcs = list(res[:n])
        return list(res[n:])


def split_gather(shards):
    n = len(shards)

    def plan(ins, lands):
        x, y, c, chips = _place()
        me = 2 * x + y
        sends, arrivals = [], []
        for w in range(n):
            for j, (ox, oy) in enumerate(chips):
                for k in range(2):
                    base = 2 * (3 * w + j)
                    sends.append((_half(ins[w], (), 0, c), _half(lands[w], (me,), 0, c), base + k, base + c, (ox, oy, k)))
                    arrivals.append((_half(lands[w], (2 * ox + oy,), 0, k), base + k))
            sends.append((ins[w], lands[w].at[me], 6 * n + w, 6 * n + w, (x, y, 1 - c)))
            arrivals.append((lands[w].at[me], 6 * n + w))
        return sends, arrivals

    return SplitExchange("gather", shards, [(4,) + s.shape for s in shards], 7 * n, plan)


def split_pair_swap(name, grads, axes):
    def plan(ins, lands):
        x, y, c, _ = _place()
        sends = [(_half(ins[w], (slice(None),), axes[w], 1 - c), lands[w], w, w, (x, y, 1 - c)) for w in range(len(ins))]
        return sends, [(lands[w], w) for w in range(len(ins))]

    halved = [tuple(d // 2 if i == 1 + ax else d for i, d in enumerate(g.shape)) for g, ax in zip(grads, axes)]
    return SplitExchange(name, grads, halved, len(grads), plan)


def split_chip_exchange(name, parts):
    def plan(ins, lands):
        x, y, c, chips = _place()
        sends, arrivals = [], []
        for w in range(len(ins)):
            for j, (ox, oy) in enumerate(chips):
                sends.append((ins[w].at[2 * ox + oy], lands[w].at[2 * x + y], 3 * w + j, 3 * w + j, (ox, oy, c)))
                arrivals.append((lands[w].at[2 * ox + oy], 3 * w + j))
        return sends, arrivals

    return SplitExchange(name, parts, [p.shape for p in parts], 3 * len(parts), plan)


def split_pair_send(halves):
    def plan(ins, lands):
        x, y, c, _ = _place()
        return ([(ins[w], lands[w], w, w, (x, y, 1 - c)) for w in range(len(ins))],
                [(lands[w], w) for w in range(len(ins))])

    return SplitExchange("pair_send", halves, [h.shape for h in halves], len(halves), plan)


def pair_send(halves):
    n = len(halves)

    def body(*refs):
        ins, outs = refs[:n], refs[n:2 * n]
        send, recv = refs[2 * n:]
        x, y, c, _ = _place()
        cps = [_remote(ins[w], outs[w], send.at[w], recv.at[w], (x, y, 1 - c)) for w in range(n)]
        for cp in cps:
            cp.start()
        for cp in cps:
            cp.wait_recv()
        for cp in cps:
            cp.wait_send()

    return pl.pallas_call(
        body, name="pair_send", in_specs=[ANY] * n, out_specs=[ANY] * n,
        out_shape=[jax.ShapeDtypeStruct(h.shape, h.dtype) for h in halves],
        scratch_shapes=[pltpu.SemaphoreType.DMA((n,))] * 2,
    )(*halves)


def all_reduce_small(name, vec, after=()):
    rows = vec.shape[0]

    def body(v_ref, *refs):
        o_ref, buf, send, recv = refs[len(after):]
        x, y, c, _ = _place()
        me = 4 * x + 2 * y + c
        buf[me] = v_ref[...]
        cps = []
        for k in range(1, 8):
            kx, ky, kc = (k >> 2) & 1, (k >> 1) & 1, k & 1
            peer = (x if kx == 0 else 1 - x, y if ky == 0 else 1 - y, c if kc == 0 else 1 - c)
            cp = _remote(v_ref, buf.at[me], send.at[k - 1], recv.at[k - 1], peer)
            cp.start()
            cps.append(cp)
        for k in range(1, 8):
            kx, ky, kc = (k >> 2) & 1, (k >> 1) & 1, k & 1
            px, py, pc = (x if kx == 0 else 1 - x, y if ky == 0 else 1 - y, c if kc == 0 else 1 - c)
            slot = buf.at[4 * px + 2 * py + pc]
            _remote(slot, slot, send.at[k - 1], recv.at[k - 1], (px, py, pc)).wait_recv()
        for cp in cps:
            cp.wait_send()
        acc = buf[0]
        for d in range(1, 8):
            acc = acc + buf[d]
        o_ref[...] = acc

    vm = pl.BlockSpec(memory_space=pltpu.VMEM)
    return pl.pallas_call(
        body, name=name, in_specs=[vm] + [ANY] * len(after), out_specs=vm, out_shape=jax.ShapeDtypeStruct(vec.shape, F32),
        scratch_shapes=[pltpu.VMEM((8, rows, LANES), F32), pltpu.SemaphoreType.DMA((7,)), pltpu.SemaphoreType.DMA((7,))],
    )(vec, *after)


class NoExchange:
    def __init__(self, late):
        self.late = late

    def late_weights(self, after):
        return self.late

    def reduce_start(self, grads):
        return jnp.zeros((8, LANES), F32)

    def reduce_exchange(self, after):
        return jnp.zeros((8, LANES), F32)

    def reduce_finish(self, after):
        return jnp.zeros((8, LANES), F32)

    def input_grad_start(self, dw_main, dw_small):
        return jnp.zeros((8, LANES), F32)

    def input_grad_exchange(self, after):
        return jnp.zeros((8, LANES), F32)


def local_step(x2, tgt2, g1, g2, gdn_ng, qn_g, kn_g, p1, p2, conv_w, wt_main, wt_small, hooks, nseq, seq):
    rows, dm = x2.shape
    wide = NH * LANES
    row = lambda a, off=0, w=None: (a, "row", off, a.shape[1] if w is None else w)
    rowh = lambda a, off=0, w=LANES: (a, "rowh", off, w)
    par = lambda a: (a, "par", 0, a.shape[1])
    parh = lambda a, off=0: (a, "parh", off, LANES)
    o_row = lambda w, dt: (w, "row", w, dt)
    o_rowh = lambda dt, tw=wide, w=LANES: (tw, "rowh", w, dt)

    u, = ew_fwd("rms1", f_rms, [row(x2), par(g1)], [o_row(dm, BF16)], rows)
    proj = matmul("mm_in", u, wt_main, "nt", BF16, tm=2048)
    sp = matmul("mm_in_small", u, wt_small, "nt", F32)
    so, = ew_fwd("small", f_small, [row(sp), par(p1), par(p2)], [o_row(LANES, F32)], rows)
    cs = cumsum_time("cumsum", so, nseq, seq, False)
    gb, bb, cb = ew_fwd("bcast", f_bcast, [row(so), row(cs)], [o_rowh(F32)] * 3, rows, NH)
    ct = transpose_time("c_time_major", cs, nseq, seq)
    conv = {}
    for mode, off in (("q", 0), ("k", NH), ("v", 2 * NH)):
        conv[mode], = ew_fwd(f"conv_{mode}", make_f_conv(mode), [rowh(proj, off), parh(conv_w, off)], [o_rowh(F32)],
                             rows, NH, seq, "hi", CONV_HEADS)
    val, kcum, attn, qdec, kdec, t_inv = gdn_a_fwd(conv["q"], conv["k"], conv["v"], gb, bb, rows)
    o_a, snaps = gdn_b_fwd(val, kcum, attn, qdec, kdec, gb, nseq, seq)
    ya_in, = ew_fwd("gdn_post", f_post, [rowh(o_a), rowh(proj, 3 * NH), par(gdn_ng)], [o_rowh(BF16)], rows, NH)
    fqn, = ew_fwd("fox_qn", f_rms, [rowh(proj, FOX_Q), par(qn_g)], [o_rowh(BF16)], rows, NH)
    fkn, = ew_fwd("fox_kn", f_rms, [rowh(proj, FOX_K), par(kn_g)], [o_rowh(BF16)], rows, NH)
    o_b, o_b16, lse = fox_fwd(fqn, fkn, proj, ct, nseq, seq)
    p_a, p_b, w_o, w_u, w_d = hooks.late_weights(o_a)
    y_a = matmul("mm_pa", ya_in, p_a, "nn", F32, tn=1024)
    y_b = matmul("mm_pb", o_b16, p_b, "nn", F32, tn=1024)
    gates = [row(proj, 7, dm), row(proj, 8, dm)]
    merged, = ew_fwd("merge", f_merge, gates + [row(y_a), row(y_b)], [o_row(dm, BF16)], rows)
    hres = matmul("mm_out", merged, w_o, "nn", F32, add=x2, tn=1024)
    hn, = ew_fwd("rms2", f_rms, [row(hres), par(g2)], [o_row(dm, BF16)], rows)
    up_blocks = w_u.shape[0]
    act, relu2 = matmul("mm_up", hn, w_u, "nn", F32, col_blocks=up_blocks, out_dtypes=[F32, BF16],
                        epilogue=lambda r: [r, jnp.maximum(r, 0.0) * jnp.maximum(r, 0.0)], tm=2048)
    def loss_tail(r, h_tile, t_tile):
        d = (r + h_tile) - t_tile
        e = (0.5 / dm) * (d * d)
        part = e.reshape(e.shape[0] // 8, 8, e.shape[1]).sum(axis=0)
        part = sum(part[:, t * LANES:(t + 1) * LANES] for t in range(e.shape[1] // LANES))
        g = d * (1.0 / dm)
        return [g, g, part]

    dout, dout16, loss_acc = matmul("mm_down", relu2, w_d, "nn", F32, extras=[hres, tgt2], epilogue=loss_tail,
                                    out_dtypes=[F32, BF16, F32], tile_sums=True)

    d_act = matmul("mm_d_act", dout16, w_d, "nt", BF16, extras=[act], epilogue=lambda r, a: [2.0 * jnp.maximum(a, 0.0) * r],
                   tm=2048)
    dw_d = matmul("mm_dw_down", relu2, dout16, "tn", F32, tn=1024)
    dw_u = matmul("mm_dw_up", hn, d_act, "tn", F32, col_blocks=up_blocks)
    d_hn = matmul("mm_d_hn", d_act, w_u, "nt", F32, col_blocks=up_blocks, tm=2048)
    dh, dh16, dg2 = ew_bwd("rms2_b", f_rms, [row(hres), par(g2)], [(row(d_hn),)], [row(dout)],
                           lambda g, e: [g[0] + e[0], g[0] + e[0], g[1]],
                           [((rows, dm), "row", dm, F32, None), ((rows, dm), "row", dm, BF16, None), ((1, dm), "par", dm, F32, "all")], rows)
    d_merged = matmul("mm_d_merged", dh16, w_o, "nt", F32, tn=1024)
    dw_o = matmul("mm_dw_out", merged, dh16, "tn", F32, tn=1024)
    seg16 = ((rows, dm), "row", dm, BF16, None)
    d_ga16, d_gb16, d_ya16, d_yb16 = ew_bwd("merge_b", f_merge, gates + [row(y_a), row(y_b)], [(row(d_merged),)], [],
                                            lambda g, e: list(g), [seg16] * 4, rows)
    dp_a = matmul("mm_dp_a", ya_in, d_ya16, "tn", F32, tn=1024)
    d_ya_in = matmul("mm_d_ya_in", d_ya16, p_a, "nt", F32, tn=1024)
    dp_b = matmul("mm_dp_b", o_b16, d_yb16, "tn", F32, tn=1024)
    d_ob = matmul("mm_d_ob", d_yb16, p_b, "nt", F32, tn=1024)
    token = hooks.reduce_start(dict(p_a=dp_a, p_b=dp_b, w_o=dw_o, w_u=dw_u, w_d=dw_d))
    gdn_ng_t = gdn_ng + token[0, 0]
    h32 = ((rows, wide), "rowh", LANES, F32, None)
    h16 = ((rows, wide), "rowh", LANES, BF16, None)
    gain = ((1, LANES), "par", LANES, F32, "all")
    d_oa, d_z16, d_gdn_ng = ew_bwd("gdn_post_b", f_post, [rowh(o_a), rowh(proj, 3 * NH), par(gdn_ng_t)], [(rowh(d_ya_in),)], [],
                                   lambda g, e: list(g), [h32, h16, gain], rows, NH)
    dval, dkc, dat, dqd, dkd, dgb_b = gdn_b_bwd(val, kcum, attn, qdec, kdec, gb, snaps, d_oa, nseq, seq)
    d_cq, d_ck, d_cv, d_gb, d_bb = gdn_a_bwd(conv["q"], conv["k"], conv["v"], gb, bb, t_inv, dval, dkc, dat, dqd, dkd, dgb_b, rows)
    token = hooks.reduce_exchange(d_cq)
    conv_w_t = conv_w + token[0, 0]
    d_pre, d_conv = {}, {}
    tap = ((4, wide), "parh", LANES, F32, "inner")
    for mode, off, ctg in (("q", 0, d_cq), ("k", NH, d_ck), ("v", 2 * NH, d_cv)):
        d_pre[mode], d_conv[mode] = ew_bwd(f"conv_{mode}_b", make_f_conv(mode), [rowh(proj, off), parh(conv_w_t, off)],
                                           [(rowh(ctg),)], [], lambda g, e: list(g), [h16, tap], rows, NH, seq, "hi", CONV_HEADS)
    d_fqn, d_cq_b = fox_dq(fqn, fkn, proj, ct, d_ob, lse, o_b, [token], nseq, seq)
    d_fkn, d_fv16, d_ck_b = fox_dkv(fqn, fkn, proj, cb, d_ob, lse, o_b, [token], nseq, seq)
    token = hooks.reduce_finish(d_fkn)
    qn_g_t, kn_g_t = qn_g + token[0, 0], kn_g + token[0, 0]
    d_fq16, d_qn_g = ew_bwd("fox_qn_b", f_rms, [rowh(proj, FOX_Q), par(qn_g_t)], [(rowh(d_fqn),)], [], lambda g, e: list(g),
                            [h16, gain], rows, NH)
    d_fk16, d_kn_g = ew_bwd("fox_kn_b", f_rms, [rowh(proj, FOX_K), par(kn_g_t)], [(rowh(d_fkn),)], [], lambda g, e: list(g),
                            [h16, gain], rows, NH)
    narrow = ((rows, LANES), "row", LANES, F32, None)
    d_so, d_cs = ew_bwd("bcast_b", f_bcast, [row(so), row(cs)], [(rowh(d_gb),), (rowh(d_bb),), (rowh(d_cq_b), rowh(d_ck_b))], [],
                        lambda g, e: list(g), [narrow, narrow], rows, NH)
    d_logf = cumsum_time("cumsum_b", d_cs, nseq, seq, True)
    vec = ((1, LANES), "par", LANES, F32, "all")
    d_sp16, d_p1, d_p2 = ew_bwd("small_b", f_small, [row(sp), par(p1), par(p2)], [(row(d_so), row(d_logf))], [],
                                lambda g, e: list(g), [((rows, LANES), "row", LANES, BF16, None), vec, vec], rows)
    d_proj16 = jnp.concatenate([d_pre["q"], d_pre["k"], d_pre["v"], d_z16, d_fq16, d_fk16, d_fv16, d_ga16, d_gb16], axis=1)
    dw_main = matmul("mm_dw_main", d_proj16, u, "tn", F32)
    dw_small = matmul("mm_dw_small", d_sp16, u, "tn", F32)
    wt_small_t = wt_small + hooks.input_grad_start(dw_main, dw_small)[0, 0].astype(BF16)
    d_u = matmul("mm_d_u_small", d_sp16, wt_small_t, "nn", F32)
    d_u = matmul("mm_d_u_first", d_proj16, wt_main, "nn", F32, add=d_u, k_part=(0, 2))
    d_u = matmul("mm_d_u_second", d_proj16, wt_main, "nn", F32, add=d_u, k_part=(1, 2), after=[hooks.input_grad_exchange(d_u)])
    dx, dg1 = ew_bwd("rms1_b", f_rms, [row(x2), par(g1)], [(row(d_u),)], [row(dh)], lambda g, e: [g[0] + e[0], g[1]],
                     [((rows, dm), "row", dm, F32, None), ((1, dm), "par", dm, F32, "all")], rows)
    d_conv_w = jnp.concatenate([d_conv["q"], d_conv["k"], d_conv["v"]], axis=1)
    return dict(loss_acc=loss_acc, dx=dx, g1=dg1, g2=dg2, gdn_ng=d_gdn_ng, qn=d_qn_g, kn=d_kn_g, p1=d_p1, p2=d_p2,
                conv=d_conv_w, w_main=dw_main, w_small=dw_small, p_a=dp_a, p_b=dp_b, w_o=dw_o, w_u=dw_u, w_d=dw_d)


_W = NH * LANES
_A0, _A1 = 4 * _W, 4 * _W + 2 * NH
_B0, _B1 = _A1 + 3 * _W, _A1 + 3 * _W + NH


def _split_w_in(full_t):
    main = jnp.concatenate([full_t[:_A0], full_t[_A1:_B0], full_t[_B1:]], axis=0)
    small = jnp.concatenate([full_t[_A0:_A1], full_t[_B0:_B1], jnp.zeros((LANES - 3 * NH, full_t.shape[1]), full_t.dtype)], axis=0)
    return main, small


def _join_w_in(main, small):
    return jnp.concatenate([main[:_A0], small[:2 * NH], main[_A0:_A0 + 3 * _W], small[2 * NH:3 * NH], main[_A0 + 3 * _W:]], axis=0)


def _lanes(v, at=0):
    return jnp.pad(v.reshape(1, -1), ((0, 0), (at, LANES - at - v.size)))


def kernel(x, norm_mix_g, w_in, gdn_conv_w, gdn_a_log, gdn_dt_bias, gdn_norm_g, fox_q_norm_g, fox_k_norm_g, fox_f_bias, w_proj_gdn, w_proj_fox, w_out, norm_mlp_g, w_up, w_down, loss_target, m_norm_mix_g, m_w_in, m_gdn_conv_w, m_gdn_a_log, m_gdn_dt_bias, m_gdn_norm_g, m_fox_q_norm_g, m_fox_k_norm_g, m_fox_f_bias, m_w_proj_gdn, m_w_proj_fox, m_w_out, m_norm_mlp_g, m_w_up, m_w_down, v_norm_mix_g, v_w_in, v_gdn_conv_w, v_gdn_a_log, v_gdn_dt_bias, v_gdn_norm_g, v_fox_q_norm_g, v_fox_k_norm_g, v_fox_f_bias, v_w_proj_gdn, v_w_proj_fox, v_w_out, v_norm_mlp_g, v_w_up, v_w_down):
    nseq, seq, dm = x.shape
    rows = nseq * seq
    xi, yi, ci = lax.axis_index("x"), lax.axis_index("y"), lax.axis_index("c")
    chip = 2 * xi + yi
    conv_cols = gdn_conv_w.shape[2]

    tr = lambda a: jnp.swapaxes(a[0], 0, 1)
    big = [tr(w_in), w_proj_gdn[0], w_proj_fox[0], w_out[0], w_up[0], w_down[0]]
    axes = [1, 0, 0, 0, 0, 0]
    big16 = [w.astype(BF16) for w in big]
    conv_slot = jnp.zeros((4, 4, conv_cols), F32).at[:, chip].set(jnp.where(ci == 0, gdn_conv_w[0], 0.0))
    conv_full = all_reduce_small("gather_conv", conv_slot.reshape(-1, LANES)).reshape(4, 4 * conv_cols)
    got_in = gather_ring(big16[0])
    wt_main, wt_small = _split_w_in(got_in.reshape(-1, dm))
    core, chip_no = ci.reshape(1).astype(jnp.int32), chip.reshape(1).astype(jnp.int32)
    gather = split_gather(big16[1:])
    token = gather.start([got_in, conv_full])

    class Hooks:
        def late_weights(self, after):
            g_pa, g_pb, g_wo, w_u, g_wd = gather.wait(after)
            return (*(g.reshape(-1, dm) for g in (g_pa, g_pb, g_wo)), w_u, g_wd.reshape(-1, dm))

        def reduce_start(self, grads):
            blocks = [grads["p_a"].reshape(4, -1, dm), grads["p_b"].reshape(4, -1, dm), grads["w_o"].reshape(4, -1, dm),
                      grads["w_u"], grads["w_d"].reshape(4, -1, dm)]
            self.swap = split_pair_swap("pair_swap_late", blocks, axes[1:])
            return self.swap.start([])

        def reduce_exchange(self, after):
            swapped = self.swap.wait(after)
            self.exchange = split_chip_exchange("chip_exchange_late", add_pair("add_pair_late", self.swap.srcs, swapped, core, axes[1:]))
            return self.exchange.start([])

        def reduce_finish(self, after):
            slots = self.exchange.wait(after)
            self.send = split_pair_send(add_chips("add_chips_late", slots, self.exchange.srcs, chip_no, axes[1:]))
            return self.send.start([])

        def input_grad_start(self, dw_main, dw_small):
            self.in_swap = split_pair_swap("pair_swap_in", [_join_w_in(dw_main, dw_small).reshape(4, -1, dm)], axes[:1])
            return self.in_swap.start([])

        def input_grad_exchange(self, after):
            swapped = self.in_swap.wait(after)
            self.in_exchange = split_chip_exchange("chip_exchange_in", add_pair("add_pair_in", self.in_swap.srcs, swapped, core, axes[:1]))
            return self.in_exchange.start([])

    hooks = Hooks()
    p1 = _lanes(gdn_dt_bias[0]) + _lanes(fox_f_bias[0], 2 * NH)
    p2 = _lanes(gdn_a_log[0])

    g = local_step(x.reshape(rows, dm), loss_target.reshape(rows, dm), norm_mix_g + token[0, 0], norm_mlp_g, gdn_norm_g,
                   fox_q_norm_g, fox_k_norm_g, p1, p2, conv_full, wt_main, wt_small, hooks, nseq, seq)

    others = hooks.send.wait(g["dx"])
    big_m = [tr(m_w_in), m_w_proj_gdn[0], m_w_proj_fox[0], m_w_out[0], m_w_up[0], m_w_down[0]]
    big_v = [tr(v_w_in), v_w_proj_gdn[0], v_w_proj_fox[0], v_w_out[0], v_w_up[0], v_w_down[0]]
    names = ["w_in", "w_proj_gdn", "w_proj_fox", "w_out", "w_up", "w_down"]
    big_res, big_grad = {}, {}
    for i in range(1, len(names)):
        big_grad[names[i]], *big_res[names[i]] = adamw_halves(f"adamw_{names[i]}", big[i], hooks.send.srcs[i - 1], others[i - 1],
                                                              big_m[i], big_v[i], core, axes[i])
    slots = hooks.in_exchange.wait(big_res[names[-1]][0])
    mine = add_chips("add_chips_in", slots, hooks.in_exchange.srcs, chip_no, axes[:1])
    res = adamw_halves("adamw_w_in", big[0], mine[0], pair_send(mine)[0], big_m[0], big_v[0], core, axes[0])
    big_grad["w_in"], *big_res["w_in"] = [jnp.swapaxes(r, 0, 1) for r in res]

    small_parts = [g["loss_acc"], g["g1"].reshape(8, LANES), g["g2"].reshape(8, LANES), g["gdn_ng"], g["qn"], g["kn"], g["p1"], g["p2"],
                   g["conv"].reshape(-1, LANES)]
    tiled = [jnp.pad(p, ((0, -p.shape[0] % 8), (0, 0))) for p in small_parts]
    red = all_reduce_small("reduce_small", jnp.concatenate(tiled, axis=0), slots)
    pos, red_parts = 0, []
    for p, t in zip(small_parts, tiled):
        red_parts.append(red[pos:pos + p.shape[0]])
        pos += t.shape[0]
    r_loss, r_g1, r_g2, r_gdn_ng, r_qn, r_kn, r_p1, r_p2, r_conv = red_parts
    loss = jnp.sum(r_loss)
    g_conv = lax.dynamic_slice_in_dim(r_conv.reshape(4, 4, conv_cols), chip, 1, axis=1).reshape(4, conv_cols)
    small_grads = [r_g1.reshape(1, dm), r_p2[:, :NH], r_p1[:, :NH], r_gdn_ng, r_qn, r_kn, r_p1[:, 2 * NH:3 * NH], r_g2.reshape(1, dm)]
    small_w = [norm_mix_g, gdn_a_log, gdn_dt_bias, gdn_norm_g, fox_q_norm_g, fox_k_norm_g, fox_f_bias, norm_mlp_g]
    small_m = [m_norm_mix_g, m_gdn_a_log, m_gdn_dt_bias, m_gdn_norm_g, m_fox_q_norm_g, m_fox_k_norm_g, m_fox_f_bias, m_norm_mlp_g]
    small_v = [v_norm_mix_g, v_gdn_a_log, v_gdn_dt_bias, v_gdn_norm_g, v_fox_q_norm_g, v_fox_k_norm_g, v_fox_f_bias, v_norm_mlp_g]

    def pack(parts):
        flat = jnp.concatenate([jnp.pad(p.reshape(-1), (0, -p.size % LANES)) for p in parts])
        return jnp.pad(flat, (0, -flat.size % (8 * LANES))).reshape(-1, LANES)

    packed = adamw("adamw_small", pack(small_w + [gdn_conv_w[0]]), pack(small_grads + [g_conv]),
                   pack(small_m + [m_gdn_conv_w[0]]), pack(small_v + [v_gdn_conv_w[0]]))

    def unpack(flat2d):
        flat, pos, res = flat2d.reshape(-1), 0, []
        for p in small_w + [gdn_conv_w[0]]:
            res.append(flat[pos:pos + p.size].reshape(p.shape))
            pos += p.size + (-p.size % LANES)
        return res

    s_delta, s_m, s_v = (unpack(a) for a in packed)

    order = ["norm_mix_g", "w_in", "gdn_conv_w", "gdn_a_log", "gdn_dt_bias", "gdn_norm_g", "fox_q_norm_g", "fox_k_norm_g",
             "fox_f_bias", "w_proj_gdn", "w_proj_fox", "w_out", "norm_mlp_g", "w_up", "w_down"]
    small_names = ["norm_mix_g", "gdn_a_log", "gdn_dt_bias", "gdn_norm_g", "fox_q_norm_g", "fox_k_norm_g", "fox_f_bias", "norm_mlp_g",
                   "gdn_conv_w"]
    small_idx = {nm: i for i, nm in enumerate(small_names)}
    shapes = dict(zip(order, (a.shape for a in (norm_mix_g, w_in, gdn_conv_w, gdn_a_log, gdn_dt_bias, gdn_norm_g, fox_q_norm_g,
                                                 fox_k_norm_g, fox_f_bias, w_proj_gdn, w_proj_fox, w_out, norm_mlp_g, w_up, w_down))))
    grads_out, delta_out, m_out, v_out = [], [], [], []
    for nm in order:
        if nm in big_res:
            d, mm, vv = big_res[nm]
            gr = big_grad[nm]
        else:
            i = small_idx[nm]
            gr = (small_grads + [g_conv])[i]
            d, mm, vv = s_delta[i], s_m[i], s_v[i]
        for lst, val in ((grads_out, gr), (delta_out, d), (m_out, mm), (v_out, vv)):
            lst.append(val.reshape(shapes[nm]))
    return (loss, g["dx"].reshape(x.shape), *grads_out, *delta_out, *m_out, *v_out)
```

```python
import functools

import jax
import jax.numpy as jnp
from jax import lax
from jax.experimental import pallas as pl
from jax.experimental.pallas import tpu as pltpu

F32 = jnp.float32
BF16 = jnp.bfloat16
LANES = 128
NH = 8
EPS = 1e-6
GDN_CHUNK = 64
GDN_ROWS = 256
GDN_BASE = 16
ROW_TILE = 512
CONV_HEADS = 2
ATT_TILE = 512
NEG = -1e30
VMEM_LIMIT_BYTES = 58 * 1024 * 1024
LO = lax.Precision.DEFAULT
MESH = pl.DeviceIdType.MESH
ANY = pl.BlockSpec(memory_space=pl.ANY)

ADAM_LR, ADAM_B1, ADAM_B2, ADAM_EPS, ADAM_WD, ADAM_STEP = 0.001, 0.9, 0.999, 1e-08, 0.01, 10


def _params(n_grid):
    return pltpu.CompilerParams(dimension_semantics=("arbitrary",) * n_grid,
                                vmem_limit_bytes=VMEM_LIMIT_BYTES)


def _dot(a, b, dims, precision=None):
    dn = {"nn": (((1,), (0,)), ((), ())), "nt": (((1,), (1,)), ((), ())), "tn": (((0,), (0,)), ((), ()))}[dims]
    return lax.dot_general(a, b, dn, precision=precision, preferred_element_type=F32)


def _iota(shape, dim):
    return lax.broadcasted_iota(jnp.int32, shape, dim)


def _split(x, parts):
    out = []
    for _ in range(parts - 1):
        hi = x.astype(BF16)
        out.append(hi)
        x = x - hi.astype(F32)
    return out + [x.astype(BF16)]


def _dot_mask(mask, b, dims, terms=3):
    m16 = mask.astype(BF16)
    acc = None
    for part in reversed(_split(b, terms)):
        prod = _dot(m16, part, dims)
        acc = prod if acc is None else acc + prod
    return acc


@jax.custom_vjp
def mm_mask(mask, b):
    return _dot_mask(mask, b, "nn", 2)


mm_mask.defvjp(lambda mask, b: (_dot_mask(mask, b, "nn", 2), mask),
               lambda mask, g: (jnp.zeros_like(mask), _dot_mask(mask, g, "tn", 2)))


def matmul(name, a, b, dims, out_dtype, add=None, tm=1024, tn=1024, tk=512, col_blocks=None,
           extras=(), epilogue=None, out_dtypes=None, k_part=None, after=(), tile_sums=False):
    if col_blocks and dims != "tn":
        nb, b_rows, bw = b.shape
        b_shape = (b_rows, nb * bw)
    else:
        b_shape = b.shape
    if dims == "nn":
        (m, k), (_, n) = a.shape, b_shape
    elif dims == "nt":
        (m, k), (n, _) = a.shape, b_shape
    else:
        (k, m), (_, n) = a.shape, b_shape
    k_span = k // (k_part[1] if k_part else 1)
    if col_blocks and dims == "nt":
        k_span = min(k_span, bw)
    tk = k if k <= 1024 else max(t for t in (2048, 1536, 1024, 512, tk) if k_span % t == 0)
    tm, tn, tk = min(tm, m), min(tn, n), min(tk, k)
    assert m % tm == 0 and n % tn == 0 and k % tk == 0, (name, m, n, k)
    k0, nk = (0, k // tk) if k_part is None else (k_part[0] * (k // tk // k_part[1]), k // tk // k_part[1])
    assert k_part is None or (dims == "nn" and not col_blocks and (k // tk) % k_part[1] == 0)
    a_spec = pl.BlockSpec((tk, tm), lambda i, j, kk: (kk, i)) if dims == "tn" else pl.BlockSpec((tm, tk), lambda i, j, kk: (i, kk + k0))
    b_spec = pl.BlockSpec((tn, tk), lambda i, j, kk: (j, kk)) if dims == "nt" else pl.BlockSpec((tk, tn), lambda i, j, kk: (kk + k0, j))
    o_spec = pl.BlockSpec((tm, tn), lambda i, j, kk: (i, j))
    out_shape = (m, n)
    if col_blocks and dims == "nn":
        per = bw // tn
        assert bw % tn == 0
        b_spec = pl.BlockSpec((None, tk, tn), lambda i, j, kk: (j // per, kk, j % per))
    elif col_blocks and dims == "nt":
        per = bw // tk
        assert bw % tk == 0
        b_spec = pl.BlockSpec((None, tn, tk), lambda i, j, kk: (kk // per, j, kk % per))
    elif col_blocks:
        bw = n // col_blocks
        per = bw // tn
        assert bw % tn == 0 and add is None
        o_spec = pl.BlockSpec((None, tm, tn), lambda i, j, kk: (j // per, i, j % per))
        out_shape = (col_blocks, m, bw)
    extras = list(extras) + ([add] if add is not None else [])
    if add is not None:
        assert epilogue is None
        epilogue = lambda r, *e: [r + e[-1]]
    out_dtypes = [out_dtype] if epilogue is None or out_dtypes is None else list(out_dtypes)
    n_ex, n_out = len(extras), len(out_dtypes)

    def body(*refs):
        a_ref, b_ref = refs[0], refs[1]
        ex_refs, o_refs = refs[2:2 + n_ex], refs[2 + n_ex + len(after):2 + n_ex + len(after) + n_out]

        def finish(r):
            res = [r] if epilogue is None else epilogue(r, *[e[...] for e in ex_refs])
            for o_ref, v in zip(o_refs, res):
                o_ref[...] = v.astype(o_ref.dtype)

        if nk == 1:
            finish(_dot(a_ref[...], b_ref[...], dims))
            return
        acc_ref = refs[-1]
        kk = pl.program_id(2)

        @pl.when(kk == 0)
        def _():
            acc_ref[...] = jnp.zeros_like(acc_ref)

        acc_ref[...] += _dot(a_ref[...], b_ref[...], dims)

        @pl.when(kk == nk - 1)
        def _():
            finish(acc_ref[...])

    out_specs = [o_spec] * n_out
    out_shapes = [jax.ShapeDtypeStruct(out_shape, dt) for dt in out_dtypes]
    if tile_sums:
        out_specs[-1] = pl.BlockSpec((8, LANES), lambda i, j, kk: (i, j))
        out_shapes[-1] = jax.ShapeDtypeStruct((8 * (m // tm), LANES * (n // tn)), out_dtypes[-1])
    res = pl.pallas_call(
        body, name=name, grid=(m // tm, n // tn, nk), in_specs=[a_spec, b_spec] + [o_spec] * n_ex + [ANY] * len(after),
        out_specs=out_specs, out_shape=out_shapes,
        scratch_shapes=[pltpu.VMEM((tm, tn), F32)] if nk > 1 else [], compiler_params=_params(3),
    )(a, b, *extras, *after)
    return res[0] if n_out == 1 else res


def _ew_spec(kind, off, width, tb, hp, order, shape=None):
    def ih(g0, g1):
        return (g0, g1) if order == "ih" else (g1, g0)

    assert off % hp == 0 or kind in ("row", "par")
    if kind == "row":
        return pl.BlockSpec((tb, width), lambda g0, g1: (ih(g0, g1)[0], off))
    if kind == "rowh":
        return pl.BlockSpec((tb, hp * width), lambda g0, g1: (ih(g0, g1)[0], ih(g0, g1)[1] + off // hp))
    if kind == "par":
        return pl.BlockSpec(shape, lambda g0, g1: (0, 0))
    if kind == "parh":
        return pl.BlockSpec((shape[0], hp * width), lambda g0, g1: (0, ih(g0, g1)[1] + off // hp))
    raise ValueError(kind)


def _ew_grid(rows, tb, nh, hp, order):
    assert nh % hp == 0 and rows % tb == 0
    return (rows // tb, nh // hp) if order == "ih" else (nh // hp, rows // tb)


def _ew_load(ref, kind, width, hh):
    if kind in ("row", "par"):
        return ref[...].astype(F32)
    return ref[:, hh * width:(hh + 1) * width].astype(F32)


def ew_fwd(name, f, ins, outs, rows, nh=1, tb=ROW_TILE, order="ih", hp=None, after=()):
    hp = nh if hp is None else hp
    n_in = len(ins)

    def body(*refs):
        hb = pl.program_id(1) if order == "ih" else pl.program_id(0)
        for hh in range(hp):
            h = hh if hp == nh else hb * hp + hh
            vals = [_ew_load(r, kd, w, hh) for r, (_, kd, _, w) in zip(refs[:n_in], ins)]
            res = f(h, *vals)
            for r, v, (_, kd, w, _) in zip(refs[n_in + len(after):], res, outs):
                if kd == "row":
                    assert hp == 1
                    r[...] = v.astype(r.dtype)
                else:
                    r[:, hh * w:(hh + 1) * w] = v.astype(r.dtype)

    in_specs = [_ew_spec(kd, off, w, tb, hp, order, a.shape) for (a, kd, off, w) in ins]
    out_specs = [_ew_spec(kd, 0, w, tb, hp, order) for (_, kd, w, _) in outs]
    out_shape = [jax.ShapeDtypeStruct((rows, tw), dt) for (tw, _, _, dt) in outs]
    return pl.pallas_call(
        body, name=name, grid=_ew_grid(rows, tb, nh, hp, order), in_specs=in_specs + [ANY] * len(after), out_specs=out_specs,
        out_shape=out_shape, compiler_params=_params(2),
    )(*[a for (a, _, _, _) in ins], *after)


def ew_bwd(name, f, ins, cts, extras, emit, outs, rows, nh=1, tb=ROW_TILE, order="ih", hp=None):
    hp = nh if hp is None else hp
    n_in = len(ins)
    flat_cts = [d for group in cts for d in group]
    n_ct, n_ex = len(flat_cts), len(extras)

    def body(*refs):
        g0, g1 = pl.program_id(0), pl.program_id(1)
        hb = g1 if order == "ih" else g0
        out_refs = refs[n_in + n_ct + n_ex:]
        shared = [None] * len(outs)

        def store(r, v, first, sl=None):
            def put(val, add):
                if sl is None:
                    r[...] = (r[...] + val if add else val).astype(r.dtype)
                else:
                    r[:, sl] = (r[:, sl] + val if add else val).astype(r.dtype)

            if first is None:
                put(v, False)
            else:
                pl.when(first)(lambda: put(v, False))
                pl.when(jnp.logical_not(first))(lambda: put(v, True))

        for hh in range(hp):
            h = hh if hp == nh else hb * hp + hh
            vals = [_ew_load(r, kd, w, hh) for r, (_, kd, _, w) in zip(refs[:n_in], ins)]
            ct_refs = list(zip(refs[n_in:n_in + n_ct], flat_cts))
            ct_vals, pos = [], 0
            for group in cts:
                v = None
                for r, (_, kd, _, w) in ct_refs[pos:pos + len(group)]:
                    t = _ew_load(r, kd, w, hh)
                    v = t if v is None else v + t
                pos += len(group)
                ct_vals.append(v)
            ex_vals = [_ew_load(r, kd, w, hh) for r, (_, kd, _, w) in zip(refs[n_in + n_ct:n_in + n_ct + n_ex], extras)]
            _, vjp = jax.vjp(lambda *a: f(h, *a), *vals)
            res = emit(vjp(tuple(ct_vals)), ex_vals)
            for idx, (r, v, (_, kd, w, _, acc)) in enumerate(zip(out_refs, res, outs)):
                if kd in ("row", "par"):
                    shared[idx] = v if shared[idx] is None else shared[idx] + v
                else:
                    store(r, v, (g1 == 0) if acc == "inner" else None, slice(hh * w, (hh + 1) * w))
        for idx, (r, (_, kd, _, _, acc)) in enumerate(zip(out_refs, outs)):
            if kd in ("row", "par"):
                assert acc == "all" or hp == nh
                store(r, shared[idx], jnp.logical_and(g0 == 0, g1 == 0) if acc == "all" else None)

    operands = list(ins) + flat_cts + list(extras)
    in_specs = [_ew_spec(kd, off, w, tb, hp, order, a.shape) for (a, kd, off, w) in operands]
    out_specs = [_ew_spec(kd, 0, w, tb, hp, order, shp) for (shp, kd, w, _, _) in outs]
    out_shape = [jax.ShapeDtypeStruct(shp, dt) for (shp, _, _, dt, _) in outs]
    return pl.pallas_call(
        body, name=name, grid=_ew_grid(rows, tb, nh, hp, order), in_specs=in_specs, out_specs=out_specs,
        out_shape=out_shape, compiler_params=_params(2),
    )(*[a for (a, _, _, _) in operands])


def f_rms(h, x, g):
    r = lax.rsqrt(jnp.mean(x * x, axis=-1, keepdims=True) + EPS)
    return (x * r * g,)


def _softplus(z):
    return jnp.maximum(z, 0.0) + jnp.log1p(jnp.exp(-jnp.abs(z)))


def f_small(h, sp, p1, p2):
    lane = _iota(sp.shape, 1)
    z = sp + p1
    g = -jnp.exp(p2) * _softplus(z)
    beta = jax.nn.sigmoid(z)
    logf = -_softplus(-z)
    return (jnp.where(lane < NH, g, jnp.where(lane < 2 * NH, beta, jnp.where(lane < 3 * NH, logf, 0.0))),)


def _pick(x, lane_id):
    lane = _iota(x.shape, 1)
    col = jnp.sum(jnp.where(lane == lane_id, x, 0.0), axis=1, keepdims=True)
    return jnp.broadcast_to(col, x.shape)


def f_bcast(h, so, cs):
    return _pick(so, h), _pick(so, h + NH), _pick(cs, h + 2 * NH)


def _shift_down(s):
    def down(x):
        r = pltpu.roll(x, s, 0)
        head = jnp.where(_iota((8, x.shape[1]), 0) >= s, r[:8], 0.0)
        return jnp.concatenate([head, r[8:]], axis=0)

    def up(g):
        n = g.shape[0]
        r = pltpu.roll(g, n - s, 0)
        tail = jnp.where(_iota((8, g.shape[1]), 0) < 8 - s, r[n - 8:], 0.0)
        return jnp.concatenate([r[:n - 8], tail], axis=0)

    @jax.custom_vjp
    def shift(x):
        return down(x)

    shift.defvjp(lambda x: (down(x), None), lambda _, g: (up(g),))
    return shift


def _silu(x):
    return x * jax.nn.sigmoid(x)


def make_f_conv(mode):
    sh1, sh2, sh3 = _shift_down(1), _shift_down(2), _shift_down(3)

    def f(h, x, w):
        sub = _iota(w.shape, 0)

        def tap(i):
            return jnp.sum(jnp.where(sub == i, w, 0.0), axis=0, keepdims=True)

        y = sh3(x) * tap(0)
        y = y + sh2(x) * tap(1)
        y = y + sh1(x) * tap(2)
        y = y + x * tap(3)
        s = _silu(y)
        if mode == "v":
            return (s,)
        n = s * lax.rsqrt(jnp.sum(s * s, axis=-1, keepdims=True) + EPS)
        if mode == "q":
            n = n * (LANES ** -0.5)
        return (n,)

    return f


def f_post(h, o, z, g):
    r = lax.rsqrt(jnp.mean(o * o, axis=-1, keepdims=True) + EPS)
    return (o * r * g * _silu(z),)


def f_merge(h, ga, gb, ya, yb):
    return (jax.nn.sigmoid(ga) * ya + jax.nn.sigmoid(gb) * yb,)


def cumsum_time(name, x, nseq, seq, reverse):
    nb = seq // LANES

    def body(x_ref, o_ref):
        r, c = _iota((LANES, LANES), 0), _iota((LANES, LANES), 1)
        tri = jnp.where((r <= c) if reverse else (r >= c), 1.0, 0.0).astype(F32)
        carry = jnp.zeros((1, LANES), F32)
        for b in (range(nb - 1, -1, -1) if reverse else range(nb)):
            blk = x_ref[b * LANES:(b + 1) * LANES, :]
            o_ref[b * LANES:(b + 1) * LANES, :] = _dot_mask(tri, blk, "nn") + carry
            carry = carry + jnp.sum(blk, axis=0, keepdims=True)

    spec = pl.BlockSpec((seq, LANES), lambda s: (s, 0))
    return pl.pallas_call(body, name=name, grid=(nseq,), in_specs=[spec], out_specs=spec,
                          out_shape=jax.ShapeDtypeStruct(x.shape, F32), compiler_params=_params(1))(x)


def transpose_time(name, x, nseq, seq):
    def body(x_ref, o_ref):
        o_ref[...] = x_ref[...].T

    return pl.pallas_call(
        body, name=name, grid=(nseq,), in_specs=[pl.BlockSpec((seq, LANES), lambda s: (s, 0))],
        out_specs=pl.BlockSpec((LANES, seq), lambda s: (s, 0)),
        out_shape=jax.ShapeDtypeStruct((nseq * LANES, seq), F32), compiler_params=_params(1))(x)


def _gdn_masks():
    n = GDN_ROWS
    r, c = _iota((n, n), 0), _iota((n, n), 1)
    shift = GDN_CHUNK.bit_length() - 1
    same = lax.shift_right_logical(r, shift) == lax.shift_right_logical(c, shift)
    return r, c, same


def _each(fn, *lists):
    return [fn(*xs) for xs in zip(*lists)]


def _gdn_decay(gbs):
    r, c, same = _gdn_masks()
    seg_tril = jnp.where(jnp.logical_and(same, r >= c), 1.0, 0.0).astype(F32)
    g_cum = _each(lambda gb: mm_mask(seg_tril, gb), gbs)
    lane0 = _iota(gbs[0].shape, 1) == 0
    g_col = _each(lambda g: jnp.sum(jnp.where(lane0, g, 0.0), axis=1, keepdims=True), g_cum)
    g_row = _each(lambda g: jnp.sum(jnp.where(r == c, jnp.broadcast_to(g, (GDN_ROWS, GDN_ROWS)), 0.0), axis=0, keepdims=True), g_col)
    return g_cum, _each(lambda a, b: a - b, g_col, g_row)


def gdn_a_mats(ks, bbs, diff):
    r, c, same = _gdn_masks()
    strict = jnp.logical_and(same, r > c)
    lane0 = _iota(bbs[0].shape, 1) == 0
    beta_col = _each(lambda bb: jnp.sum(jnp.where(lane0, bb, 0.0), axis=1, keepdims=True), bbs)
    kk = _each(lambda k: _dot(k, k, "nt", LO), ks)
    return _each(lambda b, x, d: jnp.where(strict, b * x * jnp.exp(jnp.where(strict, d, 0.0)), 0.0), beta_col, kk, diff)


@jax.custom_vjp
def saved_inverse(a, t_corr):
    return t_corr


def _saved_inverse_bwd(t, dt):
    left = dt + _dot(t, dt, "tn", LO)
    return -(left + _dot(left, t, "nt", LO)), jnp.zeros_like(t)


saved_inverse.defvjp(lambda a, t_corr: (t_corr, t_corr), _saved_inverse_bwd)


def gdn_block(*args):
    ts, qs, ks, vs, gbs, bbs = (list(args[i::6]) for i in range(6))
    g_cum, diff = _gdn_decay(gbs)
    ts = _each(saved_inverse, gdn_a_mats(ks, bbs, diff), ts)
    return gdn_outputs(ts, qs, ks, vs, gbs, bbs, g_cum, diff)


def gdn_outputs(ts, qs, ks, vs, gbs, bbs, g_cum, diff):
    r, c, same = _gdn_masks()
    incl = jnp.logical_and(same, r >= c)
    decay = _each(lambda d: jnp.where(incl, jnp.exp(jnp.where(incl, d, 0.0)), 0.0), diff)
    e_g = _each(jnp.exp, g_cum)
    v_beta = _each(lambda v, bb: v * bb, vs, bbs)
    k_beta = _each(lambda k, bb, e: k * bb * e, ks, bbs, e_g)
    value = _each(lambda t, x: x + _dot(t, x, "nn", LO), ts, v_beta)
    k_cum = _each(lambda t, x: x + _dot(t, x, "nn", LO), ts, k_beta)
    attn = _each(lambda q, k, d: _dot(q, k, "nt", LO) * d, qs, ks, decay)
    ones = jnp.where(same, 1.0, 0.0).astype(F32)
    g_last = _each(lambda gb: mm_mask(ones, gb), gbs)
    q_dec = _each(lambda q, e: q * e, qs, e_g)
    k_dec = _each(lambda k, gl, g: k * jnp.exp(gl - g), ks, g_last, g_cum)
    return tuple(x for head in zip(value, k_cum, attn, q_dec, k_dec) for x in head)


def tri_inverse(mats):
    n = GDN_ROWS
    r, c = _iota((n, n), 0), _iota((n, n), 1)
    shift = GDN_BASE.bit_length() - 1
    blk = lax.shift_right_logical(r, shift) == lax.shift_right_logical(c, shift)
    each = lambda fn, *lists: [fn(*xs) for xs in zip(*lists)]
    mm = lambda x, y: _dot(x, y, "nn", LO)
    d = each(lambda a: jnp.where(blk, a, 0.0), mats)
    lo = each(lambda a, dd: a - dd, mats, d)
    p = each(lambda dd: -dd, d)
    c_d = p
    for _ in range(shift - 1):
        p = each(mm, p, p)
        c_d = each(lambda cd, pp, prod: cd + pp + prod, c_d, p, each(mm, c_d, p))
    assert GDN_CHUNK // GDN_BASE == 4
    nmat = each(lambda l, prod: l + prod, lo, each(mm, c_d, lo))
    n2 = each(mm, nmat, nmat)
    c_n = each(lambda nn2, nm, prod: (nn2 - nm) - prod, n2, nmat, each(mm, nmat, n2))
    return each(lambda cn, cd, prod: cn + cd + prod, c_n, c_d, each(mm, c_n, c_d))


GDN_AHP = 8


def _gdn_a_specs():
    blk = pl.BlockSpec((GDN_ROWS, GDN_AHP * LANES), lambda i, h: (i, h))
    sq = pl.BlockSpec((GDN_ROWS, GDN_AHP * GDN_ROWS), lambda i, h: (i, h))
    return blk, sq


def _head(ref, hh):
    width = ref.shape[1] // GDN_AHP
    return ref.at[:, hh * width:(hh + 1) * width]


def gdn_a_fwd(q, k, v, gb, bb, rows):
    blk, sq = _gdn_a_specs()

    def body(q_ref, k_ref, v_ref, gb_ref, bb_ref, val_ref, kc_ref, at_ref, qd_ref, kd_ref, t_ref):
        heads = [[_head(r, hh)[...] for r in (q_ref, k_ref, v_ref, gb_ref, bb_ref)] for hh in range(GDN_AHP)]
        qs, ks, vs, gbs, bbs = (list(col) for col in zip(*heads))
        g_cum, diff = _gdn_decay(gbs)
        t_corr = tri_inverse(gdn_a_mats(ks, bbs, diff))
        res = gdn_outputs(t_corr, qs, ks, vs, gbs, bbs, g_cum, diff)
        for hh in range(GDN_AHP):
            for r, x in zip((val_ref, kc_ref, at_ref, qd_ref, kd_ref, t_ref), (*res[5 * hh:5 * hh + 5], t_corr[hh])):
                _head(r, hh)[...] = x.astype(r.dtype)

    wide = lambda dt: jax.ShapeDtypeStruct((rows, NH * LANES), dt)
    square = jax.ShapeDtypeStruct((rows, NH * GDN_ROWS), BF16)
    return pl.pallas_call(
        body, name="gdn_a_fwd", grid=(rows // GDN_ROWS, NH // GDN_AHP), in_specs=[blk] * 5,
        out_specs=[blk, blk, sq, blk, blk, sq], out_shape=[wide(F32), wide(BF16), square, wide(BF16), wide(BF16), square],
        compiler_params=_params(2))(q, k, v, gb, bb)


def gdn_a_bwd(q, k, v, gb, bb, t_inv, dval, dkc, dat, dqd, dkd, dgb_b, rows):
    blk, sq = _gdn_a_specs()

    def body(q_ref, k_ref, v_ref, gb_ref, bb_ref, t_ref, dval_ref, dkc_ref, dat_ref, dqd_ref, dkd_ref, dgbb_ref,
             dq_ref, dk_ref, dv_ref, dgb_ref, dbb_ref):
        hs = range(GDN_AHP)
        heads = [[_head(r, hh)[...] for r in (q_ref, k_ref, v_ref, gb_ref, bb_ref)] for hh in hs]
        tvs = [_head(t_ref, hh)[...].astype(F32) for hh in hs]
        _, vjp = jax.vjp(gdn_block, *[x for t, head in zip(tvs, heads) for x in (t, *head)])
        grads = vjp(tuple(_head(r, hh)[...] for hh in hs for r in (dval_ref, dkc_ref, dat_ref, dqd_ref, dkd_ref)))
        for hh in hs:
            _, dq, dk, dv, dgb, dbb = grads[6 * hh:6 * hh + 6]
            _head(dq_ref, hh)[...] = dq
            _head(dk_ref, hh)[...] = dk
            _head(dv_ref, hh)[...] = dv
            _head(dgb_ref, hh)[...] = dgb + _head(dgbb_ref, hh)[...]
            _head(dbb_ref, hh)[...] = dbb

    wide = jax.ShapeDtypeStruct((rows, NH * LANES), F32)
    return pl.pallas_call(
        body, name="gdn_a_bwd", grid=(rows // GDN_ROWS, NH // GDN_AHP),
        in_specs=[blk] * 5 + [sq, blk, blk, sq, blk, blk, blk], out_specs=[blk] * 5, out_shape=[wide] * 5,
        compiler_params=_params(2))(q, k, v, gb, bb, t_inv, dval, dkc, dat, dqd, dkd, dgb_b)


N_CH = GDN_ROWS // GDN_CHUNK


GDN_HP = 8


def gdn_chunk(c):
    def f(*args):
        val, kc, at, qd, kd, gb, s = (list(args[i::7]) for i in range(7))
        zero = jnp.zeros((GDN_CHUNK, LANES), F32)
        v_new = _each(lambda v, k, st: v - _dot(k, st, "nn", LO), val, kc, s)
        v_pad = _each(lambda v: jnp.concatenate([zero] * c + [v] + [zero] * (N_CH - 1 - c), axis=0), v_new)
        out = _each(lambda q, st, a, vp: _dot(q, st, "nn", LO) + _dot(a, vp, "nn", LO), qd, s, at, v_pad)
        dec = _each(lambda g: jnp.exp(jnp.sum(g, axis=0, keepdims=True)), gb)
        s_new = _each(lambda st, d, k, v: st * d + _dot(k, v, "tn", LO), s, dec, kd, v_new)
        return tuple(x for head in zip(out, s_new) for x in head)

    return f


def _gdn_piece(ref, hh, c):
    width = ref.shape[1] // GDN_HP
    return ref.at[c * GDN_CHUNK:(c + 1) * GDN_CHUNK, hh * width:(hh + 1) * width]


def _gdn_snap(ref, hh, c):
    row = (hh * N_CH + c) * LANES
    return ref.at[row:row + LANES, :]


def _gdn_b_specs(nb, rev):
    def blk_row(s, j):
        return s * nb + (nb - 1 - j if rev else j)

    blk = pl.BlockSpec((GDN_ROWS, GDN_HP * LANES), lambda s, hb, j: (blk_row(s, j), hb))
    sq = pl.BlockSpec((GDN_ROWS, GDN_HP * GDN_ROWS), lambda s, hb, j: (blk_row(s, j), hb))
    snap = pl.BlockSpec((GDN_HP * N_CH * LANES, LANES), lambda s, hb, j: (blk_row(s, j) * (NH // GDN_HP) + hb, 0))
    return blk, sq, snap


def gdn_b_fwd(val, kc, at, qd, kd, gb, nseq, seq):
    nb = seq // GDN_ROWS
    rows = nseq * seq
    blk, sq, snap = _gdn_b_specs(nb, False)

    def body(val_ref, kc_ref, at_ref, qd_ref, kd_ref, gb_ref, o_ref, snap_ref, s_ref):
        @pl.when(pl.program_id(2) == 0)
        def _():
            s_ref[...] = jnp.zeros_like(s_ref)

        hs = range(GDN_HP)
        states = [s_ref[hh] for hh in hs]
        for c in range(N_CH):
            for hh in hs:
                _gdn_snap(snap_ref, hh, c)[...] = states[hh]
            res = gdn_chunk(c)(*[x for hh in hs for x in (
                *[_gdn_piece(r, hh, c)[...].astype(F32) for r in (val_ref, kc_ref, at_ref, qd_ref, kd_ref, gb_ref)], states[hh])])
            for hh in hs:
                _gdn_piece(o_ref, hh, c)[...] = res[2 * hh]
            states = [res[2 * hh + 1] for hh in hs]
        for hh in hs:
            s_ref[hh] = states[hh]

    return pl.pallas_call(
        body, name="gdn_b_fwd", grid=(nseq, NH // GDN_HP, nb), in_specs=[blk, blk, sq, blk, blk, blk], out_specs=[blk, snap],
        out_shape=[jax.ShapeDtypeStruct((rows, NH * LANES), F32),
                   jax.ShapeDtypeStruct((nseq * nb * NH * N_CH * LANES, LANES), F32)],
        scratch_shapes=[pltpu.VMEM((GDN_HP, LANES, LANES), F32)], compiler_params=_params(3))(val, kc, at, qd, kd, gb)


def gdn_b_bwd(val, kc, at, qd, kd, gb, snaps, do, nseq, seq):
    nb = seq // GDN_ROWS
    rows = nseq * seq
    blk, sq, snap = _gdn_b_specs(nb, True)

    def body(val_ref, kc_ref, at_ref, qd_ref, kd_ref, gb_ref, snap_ref, do_ref,
             dval_ref, dkc_ref, dat_ref, dqd_ref, dkd_ref, dgb_ref, ds_ref):
        @pl.when(pl.program_id(2) == 0)
        def _():
            ds_ref[...] = jnp.zeros_like(ds_ref)

        hs = range(GDN_HP)
        d_states = [ds_ref[hh] for hh in hs]
        for c in reversed(range(N_CH)):
            _, vjp = jax.vjp(gdn_chunk(c), *[x for hh in hs for x in (
                *[_gdn_piece(r, hh, c)[...].astype(F32) for r in (val_ref, kc_ref, at_ref, qd_ref, kd_ref, gb_ref)],
                _gdn_snap(snap_ref, hh, c)[...])])
            grads = vjp(tuple(x for hh in hs for x in (_gdn_piece(do_ref, hh, c)[...], d_states[hh])))
            for hh in hs:
                for i, r in enumerate([dval_ref, dkc_ref, dat_ref, dqd_ref, dkd_ref, dgb_ref]):
                    _gdn_piece(r, hh, c)[...] = grads[7 * hh + i]
            d_states = [grads[7 * hh + 6] for hh in hs]
        for hh in hs:
            ds_ref[hh] = d_states[hh]

    wide = jax.ShapeDtypeStruct((rows, NH * LANES), F32)
    square = jax.ShapeDtypeStruct((rows, NH * GDN_ROWS), F32)
    return pl.pallas_call(
        body, name="gdn_b_bwd", grid=(nseq, NH // GDN_HP, nb), in_specs=[blk, blk, sq, blk, blk, blk, snap, blk],
        out_specs=[blk, blk, sq, blk, blk, blk], out_shape=[wide, wide, square, wide, wide, wide],
        scratch_shapes=[pltpu.VMEM((GDN_HP, LANES, LANES), F32)], compiler_params=_params(3))(val, kc, at, qd, kd, gb, snaps, do)


FOX_Q, FOX_K, FOX_V = 4 * NH, 5 * NH, 6 * NH
FOX_SCALE = LANES ** -0.5


def _head_row(ct_ref, h, off, width):
    blk = ct_ref[:, pl.ds(off, width)]
    return jnp.sum(jnp.where(_iota(blk.shape, 0) == h, blk, 0.0), axis=0, keepdims=True)


def _col(x):
    return jnp.max(x, axis=1, keepdims=True)


def _row(x):
    return jnp.max(x.T, axis=0, keepdims=True)


def _causal(shape, q_dim):
    return _iota(shape, q_dim) >= _iota(shape, 1 - q_dim)


FOX_HP = 4


def _fox_specs(seq, tile, n_tiles):
    tblk = pl.BlockSpec((tile, FOX_HP * LANES), lambda s, h, i: (s * n_tiles + i, h))
    vtblk = pl.BlockSpec((tile, FOX_HP * LANES), lambda s, h, i: (s * n_tiles + i, h + FOX_V // FOX_HP))
    full = pl.BlockSpec((seq, FOX_HP * LANES), lambda s, h, i: (s, h))
    vfull = pl.BlockSpec((seq, FOX_HP * LANES), lambda s, h, i: (s, h + FOX_V // FOX_HP))
    ctb = pl.BlockSpec((NH, seq), lambda s, h, i: (s * (LANES // NH) + 2, 0))
    return tblk, vtblk, full, vfull, ctb


def _lanes_of(hh):
    return slice(hh * LANES, (hh + 1) * LANES)


def fox_fwd(qn, kn, proj, ct, nseq, seq):
    tq = tk = min(ATT_TILE, seq)
    nq = seq // tq
    rows = nseq * seq
    qblk, _, full, vfull, ctb = _fox_specs(seq, tq, nq)
    hs = range(FOX_HP)

    def body(q_ref, k_ref, v_ref, ct_ref, o_ref, o16_ref, lse_ref):
        hb, i = pl.program_id(1), pl.program_id(2)
        q = [q_ref[:, _lanes_of(hh)] for hh in hs]

        def step(j, carry, diag):
            m, l, acc = (list(carry[t::3]) for t in range(3))
            off = pl.multiple_of(j * tk, tk)
            k = [k_ref[pl.ds(off, tk), _lanes_of(hh)] for hh in hs]
            v = [v_ref[pl.ds(off, tk), _lanes_of(hh)].astype(BF16) for hh in hs]
            ck = [_head_row(ct_ref, hb * FOX_HP + hh, off, tk) for hh in hs]
            s = _each(lambda qq, kk, cc: _dot(qq, kk, "nt") * FOX_SCALE - cc, q, k, ck)
            if diag:
                s = _each(lambda x: jnp.where(_causal(x.shape, 0), x, NEG), s)
            m_new = _each(lambda mm, x: jnp.maximum(mm, jnp.max(x, axis=1, keepdims=True)), m, s)
            p = _each(lambda x, mm: jnp.exp(x - mm), s, m_new)
            alpha = _each(lambda mo, mn: jnp.exp(mo - mn), m, m_new)
            l = _each(lambda a, ll, pp: a * ll + jnp.sum(pp, axis=1, keepdims=True), alpha, l, p)
            acc = _each(lambda a, ac, pp, vv: a * ac + _dot(pp.astype(BF16), vv, "nn"), alpha, acc, p, v)
            return tuple(x for head in zip(m_new, l, acc) for x in head)

        init = (jnp.full((tq, 1), NEG, F32), jnp.zeros((tq, 1), F32), jnp.zeros((tq, LANES), F32)) * FOX_HP
        res = step(i, lax.fori_loop(0, i, lambda j, c: step(j, c, False), init), True)
        for hh in hs:
            m, l, acc = res[3 * hh:3 * hh + 3]
            o = acc / l
            o_ref[:, _lanes_of(hh)] = o
            o16_ref[:, _lanes_of(hh)] = o.astype(BF16)
            lse_ref[:, _lanes_of(hh)] = jnp.broadcast_to(m + jnp.log(l), (tq, LANES))

    wide = (rows, NH * LANES)
    return pl.pallas_call(
        body, name="fox_fwd", grid=(nseq, NH // FOX_HP, nq), in_specs=[qblk, full, vfull, ctb], out_specs=[qblk] * 3,
        out_shape=[jax.ShapeDtypeStruct(wide, F32), jax.ShapeDtypeStruct(wide, BF16), jax.ShapeDtypeStruct(wide, F32)],
        compiler_params=_params(3))(qn, kn, proj, ct)


def fox_dq(qn, kn, proj, ct, do, lse, o, after, nseq, seq):
    tq = tk = min(ATT_TILE, seq)
    nq = seq // tq
    rows = nseq * seq
    qblk, _, full, vfull, ctb = _fox_specs(seq, tq, nq)
    hs = range(FOX_HP)

    def body(q_ref, k_ref, v_ref, ct_ref, do_ref, lse_ref, o_ref, *rest):
        dq_ref, dc_ref = rest[len(after):]
        hb, i = pl.program_id(1), pl.program_id(2)
        q = [q_ref[:, _lanes_of(hh)] for hh in hs]
        lse = [_col(lse_ref[:, _lanes_of(hh)]) for hh in hs]
        delta = [jnp.sum(do_ref[:, _lanes_of(hh)] * o_ref[:, _lanes_of(hh)], axis=1, keepdims=True) for hh in hs]
        do16 = [do_ref[:, _lanes_of(hh)].astype(BF16) for hh in hs]

        def step(j, carry, diag):
            dq, dc = (list(carry[t::2]) for t in range(2))
            off = pl.multiple_of(j * tk, tk)
            k = [k_ref[pl.ds(off, tk), _lanes_of(hh)] for hh in hs]
            v = [v_ref[pl.ds(off, tk), _lanes_of(hh)].astype(BF16) for hh in hs]
            ck = [_head_row(ct_ref, hb * FOX_HP + hh, off, tk) for hh in hs]
            p = _each(lambda qq, kk, cc, ll: jnp.exp(_dot(qq, kk, "nt") * FOX_SCALE - cc - ll), q, k, ck, lse)
            if diag:
                p = _each(lambda x: jnp.where(_causal(x.shape, 0), x, 0.0), p)
            dp = _each(lambda d, vv: _dot(d, vv, "nt"), do16, v)
            ds = _each(lambda pp, d, dl: pp * (d - dl), p, dp, delta)
            dq = _each(lambda a, x, kk: a + _dot(x.astype(BF16), kk, "nn"), dq, ds, k)
            dc = _each(lambda a, x: a + jnp.sum(x, axis=1, keepdims=True), dc, ds)
            return tuple(x for head in zip(dq, dc) for x in head)

        init = (jnp.zeros((tq, LANES), F32), jnp.zeros((tq, 1), F32)) * FOX_HP
        res = step(i, lax.fori_loop(0, i, lambda j, c: step(j, c, False), init), True)
        for hh in hs:
            dq_ref[:, _lanes_of(hh)] = res[2 * hh] * FOX_SCALE
            dc_ref[:, _lanes_of(hh)] = jnp.where(_iota((tq, LANES), 1) == 0, res[2 * hh + 1], 0.0)

    wide = jax.ShapeDtypeStruct((rows, NH * LANES), F32)
    return pl.pallas_call(
        body, name="fox_dq", grid=(nseq, NH // FOX_HP, nq), in_specs=[qblk, full, vfull, ctb, qblk, qblk, qblk] + [ANY] * len(after),
        out_specs=[qblk, qblk], out_shape=[wide, wide], compiler_params=_params(3))(qn, kn, proj, ct, do, lse, o, *after)


def fox_dkv(qn, kn, proj, cb, do, lse, o, after, nseq, seq):
    tq = tk = min(ATT_TILE, seq)
    nq = seq // tq
    rows = nseq * seq
    kblk, vblk, full, _, _ = _fox_specs(seq, tk, nq)
    hs = range(FOX_HP)

    def body(q_ref, k_ref, v_ref, cb_ref, do_ref, lse_ref, o_ref, *rest):
        dk_ref, dv_ref, dc_ref = rest[len(after):]
        j = pl.program_id(2)
        k = [k_ref[:, _lanes_of(hh)] for hh in hs]
        v16 = [v_ref[:, _lanes_of(hh)].astype(BF16) for hh in hs]
        ck = [_col(cb_ref[:, _lanes_of(hh)]) for hh in hs]

        def step(i, carry, diag):
            dk, dv, dc = (list(carry[t::3]) for t in range(3))
            off = pl.multiple_of(i * tq, tq)
            q = [q_ref[pl.ds(off, tq), _lanes_of(hh)] for hh in hs]
            do32 = [do_ref[pl.ds(off, tq), _lanes_of(hh)] for hh in hs]
            do16 = [d.astype(BF16) for d in do32]
            lse = [_row(lse_ref[pl.ds(off, tq), _lanes_of(hh)]) for hh in hs]
            delta = [_row(jnp.broadcast_to(jnp.sum(d * o_ref[pl.ds(off, tq), _lanes_of(hh)], axis=1, keepdims=True), (tq, LANES)))
                     for hh, d in zip(hs, do32)]
            p = _each(lambda kk, qq, cc, ll: jnp.exp(_dot(kk, qq, "nt") * FOX_SCALE - cc - ll), k, q, ck, lse)
            if diag:
                p = _each(lambda x: jnp.where(_causal(x.shape, 1), x, 0.0), p)
            dv = _each(lambda a, pp, d: a + _dot(pp.astype(BF16), d, "nn"), dv, p, do16)
            ds = _each(lambda pp, vv, d, dl: pp * (_dot(vv, d, "nt") - dl), p, v16, do16, delta)
            dk = _each(lambda a, x, qq: a + _dot(x.astype(BF16), qq, "nn"), dk, ds, q)
            dc = _each(lambda a, x: a + jnp.sum(x, axis=1, keepdims=True), dc, ds)
            return tuple(x for head in zip(dk, dv, dc) for x in head)

        zero = jnp.zeros((tk, LANES), F32)
        carry = step(j, (zero, zero, jnp.zeros((tk, 1), F32)) * FOX_HP, True)
        res = lax.fori_loop(j + 1, nq, lambda i, c: step(i, c, False), carry)
        for hh in hs:
            dk, dv, dc = res[3 * hh:3 * hh + 3]
            dk_ref[:, _lanes_of(hh)] = dk * FOX_SCALE
            dv_ref[:, _lanes_of(hh)] = dv.astype(BF16)
            dc_ref[:, _lanes_of(hh)] = jnp.where(_iota((tk, LANES), 1) == 0, -dc, 0.0)

    wide = (rows, NH * LANES)
    return pl.pallas_call(
        body, name="fox_dkv", grid=(nseq, NH // FOX_HP, nq), in_specs=[full, kblk, vblk, kblk, full, full, full] + [ANY] * len(after),
        out_specs=[kblk, kblk, kblk],
        out_shape=[jax.ShapeDtypeStruct(wide, F32), jax.ShapeDtypeStruct(wide, BF16), jax.ShapeDtypeStruct(wide, F32)],
        compiler_params=_params(3))(qn, kn, proj, cb, do, lse, o, *after)


def _adamw_update(w, g, m, v):
    m_new = ADAM_B1 * m + (1.0 - ADAM_B1) * g
    v_new = ADAM_B2 * v + (1.0 - ADAM_B2) * (g * g)
    m_hat = m_new / (1.0 - ADAM_B1 ** ADAM_STEP)
    v_hat = v_new / (1.0 - ADAM_B2 ** ADAM_STEP)
    return -ADAM_LR * (m_hat / (jnp.sqrt(v_hat) + ADAM_EPS) + ADAM_WD * w), m_new, v_new


def adamw(name, w, g, m, v):
    rows, cols = w.shape
    tb = min(rows, 128)
    assert rows % tb == 0
    blk = pl.BlockSpec((tb, cols), lambda i: (i, 0))

    def body(w_ref, g_ref, m_ref, v_ref, d_ref, mo_ref, vo_ref):
        d_ref[...], mo_ref[...], vo_ref[...] = _adamw_update(w_ref[...], g_ref[...], m_ref[...], v_ref[...])

    shp = jax.ShapeDtypeStruct(w.shape, F32)
    return pl.pallas_call(body, name=name, grid=(rows // tb,), in_specs=[blk] * 4, out_specs=[blk] * 3,
                          out_shape=[shp] * 3, compiler_params=_params(1))(w, g, m, v)


SPLIT_TILE = 128


def _tiled(shape2d, ax, n_lead, index):
    blk = (SPLIT_TILE, shape2d[1]) if ax == 0 else (shape2d[0], SPLIT_TILE)

    def index_map(*args):
        *lead, t = index(*args)
        return (*lead, t, 0) if ax == 0 else (*lead, 0, t)

    return pl.BlockSpec((None,) * n_lead + blk, index_map)


def adamw_halves(name, w, mine, other, m, v, c, ax):
    steps = w.shape[ax] // 2 // SPLIT_TILE
    assert w.shape[ax] == 2 * steps * SPLIT_TILE

    def body(c_ref, w_ref, mine_ref, other_ref, m_ref, v_ref, g_ref, d_ref, mo_ref, vo_ref):
        g = jnp.where(pl.program_id(0) // steps == c_ref[0], mine_ref[...], other_ref[...])
        g_ref[...] = g
        d_ref[...], mo_ref[...], vo_ref[...] = _adamw_update(w_ref[...], g, m_ref[...], v_ref[...])

    blk = _tiled(w.shape, ax, 0, lambda i, c_ref: (i,))
    hblk = _tiled(mine.shape, ax, 0, lambda i, c_ref: (i % steps,))
    grid_spec = pltpu.PrefetchScalarGridSpec(num_scalar_prefetch=1, grid=(2 * steps,),
                                             in_specs=[blk, hblk, hblk, blk, blk], out_specs=[blk] * 4)
    shp = jax.ShapeDtypeStruct(w.shape, F32)
    return pl.pallas_call(body, name=name, grid_spec=grid_spec, out_shape=[shp] * 4,
                          compiler_params=_params(1))(c, w, mine, other, m, v)


def add_chips(name, slots, parts, chip, axes):
    outs = []
    for idx, (x, own, ax) in enumerate(zip(slots, parts, axes)):
        n, shape2d = x.shape[0], x.shape[1:]
        steps = shape2d[ax] // SPLIT_TILE
        assert shape2d[ax] == steps * SPLIT_TILE

        def body(me_ref, *refs, n=n):
            o_ref = refs[n + 1]
            acc = None
            for t in range(n):
                term = jnp.where(me_ref[0] == t, refs[n][...], refs[t][...]).astype(F32)
                acc = term if acc is None else acc + term
            o_ref[...] = acc

        def filled(t, n=n):
            return lambda i, me_ref: (jnp.where(me_ref[0] == t, (t + 1) % n, t), i)

        grid_spec = pltpu.PrefetchScalarGridSpec(
            num_scalar_prefetch=1, grid=(steps,),
            in_specs=[_tiled(shape2d, ax, 1, filled(t)) for t in range(n)]
            + [_tiled(shape2d, ax, 1, lambda i, me_ref: (me_ref[0], i))],
            out_specs=_tiled(shape2d, ax, 0, lambda i, me_ref: (i,)))
        outs.append(pl.pallas_call(
            body, name=f"{name}_{idx}", grid_spec=grid_spec, out_shape=jax.ShapeDtypeStruct(shape2d, F32),
            compiler_params=_params(1))(chip, *([x] * n), own))
    return outs


def add_pair(name, gs, rs, c, axes):
    outs = []
    for idx, (g, r, ax) in enumerate(zip(gs, rs, axes)):
        nb = r.shape[0]
        steps = r.shape[1 + ax] // SPLIT_TILE
        assert r.shape[1 + ax] == steps * SPLIT_TILE

        def body(c_ref, g_ref, r_ref, o_ref):
            o_ref[...] = (g_ref[...] + r_ref[...]).astype(BF16)

        grid_spec = pltpu.PrefetchScalarGridSpec(
            num_scalar_prefetch=1, grid=(nb, steps),
            in_specs=[_tiled(g.shape[1:], ax, 1, lambda b, i, c_ref: (b, c_ref[0] * steps + i)),
                      _tiled(r.shape[1:], ax, 1, lambda b, i, c_ref: (b, i))],
            out_specs=_tiled(r.shape[1:], ax, 1, lambda b, i, c_ref: (b, i)))
        outs.append(pl.pallas_call(
            body, name=f"{name}_{idx}", grid_spec=grid_spec, out_shape=jax.ShapeDtypeStruct(r.shape, BF16),
            compiler_params=_params(2))(c, g, r))
    return outs


def _place():
    x, y, c = lax.axis_index("x"), lax.axis_index("y"), lax.axis_index("c")
    return x, y, c, [(1 - x, y), (x, 1 - y), (1 - x, 1 - y)]


def _remote(src, dst, send_sem, recv_sem, dev):
    return pltpu.make_async_remote_copy(src_ref=src, dst_ref=dst, send_sem=send_sem, recv_sem=recv_sem,
                                        device_id=dev, device_id_type=MESH)


def _half(ref, lead, ax, which):
    size = ref.shape[len(lead) + ax] // 2
    part = pl.ds(which * size, size)
    return ref.at[(*lead, part, slice(None)) if ax == 0 else (*lead, slice(None), part)]


def gather_ring(shard):
    rows, cols = shard.shape
    half = cols // 2
    top = rows // 2 // 16 * 16
    assert shard.dtype == BF16 and half % LANES == 0

    def body(in_ref, out_ref, ici_s, ici_r, d2d_s, d2d_r):
        x, y, c, _ = _place()
        me, xn, yn, dg = 2 * x + y, 2 * (1 - x) + y, 2 * x + (1 - y), 2 * (1 - x) + (1 - y)
        to_x, to_y, sib = (1 - x, y, c), (x, 1 - y, c), (x, y, 1 - c)
        mine, other = pl.ds(c * half, half), pl.ds((1 - c) * half, half)
        upper, lower = pl.ds(0, top), pl.ds(top, rows - top)
        started = []

        def send(src, dst, sems, k, dev):
            cp = _remote(src, dst, sems[0].at[k], sems[1].at[k], dev)
            cp.start()
            started.append(cp)

        def arrive(dst, sems, k):
            _remote(dst, dst, sems[0].at[k], sems[1].at[k], sib).wait_recv()

        ici, d2d = (ici_s, ici_r), (d2d_s, d2d_r)
        send(in_ref, out_ref.at[me], d2d, 0, sib)
        send(in_ref.at[:, mine], out_ref.at[me, :, mine], ici, 0, to_x)
        send(in_ref.at[:, mine], out_ref.at[me, :, mine], ici, 1, to_y)
        arrive(out_ref.at[xn, :, mine], ici, 0)
        send(out_ref.at[xn, upper, mine], out_ref.at[xn, upper, mine], ici, 2, to_y)
        send(out_ref.at[xn, :, mine], out_ref.at[xn, :, mine], d2d, 1, sib)
        arrive(out_ref.at[yn, :, mine], ici, 1)
        send(out_ref.at[yn, lower, mine], out_ref.at[yn, lower, mine], ici, 3, to_x)
        send(out_ref.at[yn, :, mine], out_ref.at[yn, :, mine], d2d, 2, sib)
        arrive(out_ref.at[dg, upper, mine], ici, 2)
        send(out_ref.at[dg, upper, mine], out_ref.at[dg, upper, mine], d2d, 3, sib)
        arrive(out_ref.at[dg, lower, mine], ici, 3)
        send(out_ref.at[dg, lower, mine], out_ref.at[dg, lower, mine], d2d, 4, sib)
        arrive(out_ref.at[me], d2d, 0)
        arrive(out_ref.at[xn, :, other], d2d, 1)
        arrive(out_ref.at[yn, :, other], d2d, 2)
        arrive(out_ref.at[dg, upper, other], d2d, 3)
        arrive(out_ref.at[dg, lower, other], d2d, 4)
        for cp in started:
            cp.wait_send()

    return pl.pallas_call(
        body, name="gather_ring", in_specs=[ANY], out_specs=ANY, out_shape=jax.ShapeDtypeStruct((4,) + shard.shape, shard.dtype),
        scratch_shapes=[pltpu.SemaphoreType.DMA((4,))] * 2 + [pltpu.SemaphoreType.DMA((5,))] * 2,
    )(shard)


HBM = pl.BlockSpec(memory_space=pltpu.HBM)
SEM = pl.BlockSpec(memory_space=pltpu.SEMAPHORE)
DATAFLOW = pltpu.SideEffectType.DATAFLOW_SIDE_EFFECTING


def _hbm(a):
    return pltpu.with_memory_space_constraint(a, pltpu.HBM)


class SplitExchange:
    def __init__(self, name, srcs, zone_shapes, n_sems, plan):
        self.name, self.n, self.n_sems, self.plan = name, len(srcs), n_sems, plan
        self.srcs = [_hbm(s) for s in srcs]
        self.zones = [_hbm(lax.empty(shape, s.dtype)) for shape, s in zip(zone_shapes, srcs)]

    def start(self, after):
        n, n_after = self.n, len(after)

        def body(*refs):
            ins, lands = refs[:n], refs[n:2 * n]
            send, recv, token = refs[2 * n + n_after], refs[2 * n + n_after + 1], refs[-1]
            for src, dst, si, ri, dev in self.plan(ins, lands)[0]:
                _remote(src, dst, send.at[si], recv.at[ri], dev).start()
            token[...] = jnp.zeros_like(token)

        res = pl.pallas_call(
            body, name=f"{self.name}_start", in_specs=[HBM] * (2 * n) + [ANY] * n_after,
            out_specs=[SEM, SEM] + [HBM] * (2 * n) + [pl.BlockSpec(memory_space=pltpu.VMEM)],
            out_shape=[pltpu.SemaphoreType.DMA((self.n_sems,)), pltpu.SemaphoreType.DMA((self.n_sems,))]
            + [pltpu.HBM(a.shape, a.dtype) for a in self.srcs + self.zones] + [jax.ShapeDtypeStruct((8, LANES), F32)],
            input_output_aliases={i: 2 + i for i in range(2 * n)},
            compiler_params=pltpu.CompilerParams(has_side_effects=DATAFLOW),
        )(*self.srcs, *self.zones, *after)
        self.sems, self.srcs, self.zones = res[:2], list(res[2:2 + n]), list(res[2 + n:2 + 2 * n])
        return res[-1]

    def wait(self, after):
        n = self.n

        def body(*refs):
            ins, lands = refs[:n], refs[n:2 * n]
            send, recv = refs[2 * n], refs[2 * n + 1]
            sends, arrivals = self.plan(ins, lands)
            for src, _, si, _, dev in sends:
                _remote(src, src, send.at[si], recv.at[si], dev).wait_send()
            for landed, ri in arrivals:
                _remote(landed, landed, send.at[ri], recv.at[ri], _place()[:3]).wait_recv()

        res = pl.pallas_call(
            body, name=f"{self.name}_wait", in_specs=[HBM] * (2 * n) + [SEM, SEM, ANY], out_specs=[HBM] * (2 * n),
            out_shape=[pltpu.HBM(a.shape, a.dtype) for a in self.srcs + self.zones],
            input_output_aliases={i: i for i in range(2 * n)},
            compiler_params=pltpu.CompilerParams(has_side_effects=DATAFLOW),
        )(*self.srcs, *self.zones, *self.sems, after)
        self.srcs = list(res[:n])
        return list(res[n:])


def split_gather(shards):
    n = len(shards)

    def plan(ins, lands):
        x, y, c, chips = _place()
        me = 2 * x + y
        sends, arrivals = [], []
        for w in range(n):
            for j, (ox, oy) in enumerate(chips):
                for k in range(2):
                    base = 2 * (3 * w + j)
                    sends.append((_half(ins[w], (), 0, c), _half(lands[w], (me,), 0, c), base + k, base + c, (ox, oy, k)))
                    arrivals.append((_half(lands[w], (2 * ox + oy,), 0, k), base + k))
            sends.append((ins[w], lands[w].at[me], 6 * n + w, 6 * n + w, (x, y, 1 - c)))
            arrivals.append((lands[w].at[me], 6 * n + w))
        return sends, arrivals

    return SplitExchange("gather", shards, [(4,) + s.shape for s in shards], 7 * n, plan)


def split_pair_swap(name, grads, axes):
    def plan(ins, lands):
        x, y, c, _ = _place()
        sends = [(_half(ins[w], (slice(None),), axes[w], 1 - c), lands[w], w, w, (x, y, 1 - c)) for w in range(len(ins))]
        return sends, [(lands[w], w) for w in range(len(ins))]

    halved = [tuple(d // 2 if i == 1 + ax else d for i, d in enumerate(g.shape)) for g, ax in zip(grads, axes)]
    return SplitExchange(name, grads, halved, len(grads), plan)


def split_chip_exchange(name, parts):
    def plan(ins, lands):
        x, y, c, chips = _place()
        sends, arrivals = [], []
        for w in range(len(ins)):
            for j, (ox, oy) in enumerate(chips):
                sends.append((ins[w].at[2 * ox + oy], lands[w].at[2 * x + y], 3 * w + j, 3 * w + j, (ox, oy, c)))
                arrivals.append((lands[w].at[2 * ox + oy], 3 * w + j))
        return sends, arrivals

    return SplitExchange(name, parts, [p.shape for p in parts], 3 * len(parts), plan)


def split_pair_send(halves):
    def plan(ins, lands):
        x, y, c, _ = _place()
        return ([(ins[w], lands[w], w, w, (x, y, 1 - c)) for w in range(len(ins))],
                [(lands[w], w) for w in range(len(ins))])

    return SplitExchange("pair_send", halves, [h.shape for h in halves], len(halves), plan)


def pair_send(halves):
    n = len(halves)

    def body(*refs):
        ins, outs = refs[:n], refs[n:2 * n]
        send, recv = refs[2 * n:]
        x, y, c, _ = _place()
        cps = [_remote(ins[w], outs[w], send.at[w], recv.at[w], (x, y, 1 - c)) for w in range(n)]
        for cp in cps:
            cp.start()
        for cp in cps:
            cp.wait_recv()
        for cp in cps:
            cp.wait_send()

    return pl.pallas_call(
        body, name="pair_send", in_specs=[ANY] * n, out_specs=[ANY] * n,
        out_shape=[jax.ShapeDtypeStruct(h.shape, h.dtype) for h in halves],
        scratch_shapes=[pltpu.SemaphoreType.DMA((n,))] * 2,
    )(*halves)


def all_reduce_small(name, vec, after=()):
    rows = vec.shape[0]

    def body(v_ref, *refs):
        o_ref, buf, send, recv = refs[len(after):]
        x, y, c, _ = _place()
        me = 4 * x + 2 * y + c
        buf[me] = v_ref[...]
        cps = []
        for k in range(1, 8):
            kx, ky, kc = (k >> 2) & 1, (k >> 1) & 1, k & 1
            peer = (x if kx == 0 else 1 - x, y if ky == 0 else 1 - y, c if kc == 0 else 1 - c)
            cp = _remote(v_ref, buf.at[me], send.at[k - 1], recv.at[k - 1], peer)
            cp.start()
            cps.append(cp)
        for k in range(1, 8):
            kx, ky, kc = (k >> 2) & 1, (k >> 1) & 1, k & 1
            px, py, pc = (x if kx == 0 else 1 - x, y if ky == 0 else 1 - y, c if kc == 0 else 1 - c)
            slot = buf.at[4 * px + 2 * py + pc]
            _remote(slot, slot, send.at[k - 1], recv.at[k - 1], (px, py, pc)).wait_recv()
        for cp in cps:
            cp.wait_send()
        acc = buf[0]
        for d in range(1, 8):
            acc = acc + buf[d]
        o_ref[...] = acc

    vm = pl.BlockSpec(memory_space=pltpu.VMEM)
    return pl.pallas_call(
        body, name=name, in_specs=[vm] + [ANY] * len(after), out_specs=vm, out_shape=jax.ShapeDtypeStruct(vec.shape, F32),
        scratch_shapes=[pltpu.VMEM((8, rows, LANES), F32), pltpu.SemaphoreType.DMA((7,)), pltpu.SemaphoreType.DMA((7,))],
    )(vec, *after)


class NoExchange:
    def __init__(self, late):
        self.late = late

    def late_weights(self, after):
        return self.late

    def reduce_start(self, grads):
        return jnp.zeros((8, LANES), F32)

    def reduce_exchange(self, after):
        return jnp.zeros((8, LANES), F32)

    def reduce_finish(self, after):
        return jnp.zeros((8, LANES), F32)

    def input_grad_start(self, dw_main, dw_small):
        return jnp.zeros((8, LANES), F32)

    def input_grad_exchange(self, after):
        return jnp.zeros((8, LANES), F32)


def local_step(x2, tgt2, g1, g2, gdn_ng, qn_g, kn_g, p1, p2, conv_w, wt_main, wt_small, hooks, nseq, seq):
    rows, dm = x2.shape
    wide = NH * LANES
    row = lambda a, off=0, w=None: (a, "row", off, a.shape[1] if w is None else w)
    rowh = lambda a, off=0, w=LANES: (a, "rowh", off, w)
    par = lambda a: (a, "par", 0, a.shape[1])
    parh = lambda a, off=0: (a, "parh", off, LANES)
    o_row = lambda w, dt: (w, "row", w, dt)
    o_rowh = lambda dt, tw=wide, w=LANES: (tw, "rowh", w, dt)

    u, = ew_fwd("rms1", f_rms, [row(x2), par(g1)], [o_row(dm, BF16)], rows)
    proj = matmul("mm_in", u, wt_main, "nt", BF16, tm=2048)
    sp = matmul("mm_in_small", u, wt_small, "nt", F32)
    so, = ew_fwd("small", f_small, [row(sp), par(p1), par(p2)], [o_row(LANES, F32)], rows)
    cs = cumsum_time("cumsum", so, nseq, seq, False)
    gb, bb, cb = ew_fwd("bcast", f_bcast, [row(so), row(cs)], [o_rowh(F32)] * 3, rows, NH)
    ct = transpose_time("c_time_major", cs, nseq, seq)
    conv = {}
    for mode, off in (("q", 0), ("k", NH), ("v", 2 * NH)):
        conv[mode], = ew_fwd(f"conv_{mode}", make_f_conv(mode), [rowh(proj, off), parh(conv_w, off)], [o_rowh(F32)],
                             rows, NH, seq, "hi", CONV_HEADS)
    val, kcum, attn, qdec, kdec, t_inv = gdn_a_fwd(conv["q"], conv["k"], conv["v"], gb, bb, rows)
    o_a, snaps = gdn_b_fwd(val, kcum, attn, qdec, kdec, gb, nseq, seq)
    ya_in, = ew_fwd("gdn_post", f_post, [rowh(o_a), rowh(proj, 3 * NH), par(gdn_ng)], [o_rowh(BF16)], rows, NH)
    fqn, = ew_fwd("fox_qn", f_rms, [rowh(proj, FOX_Q), par(qn_g)], [o_rowh(BF16)], rows, NH)
    fkn, = ew_fwd("fox_kn", f_rms, [rowh(proj, FOX_K), par(kn_g)], [o_rowh(BF16)], rows, NH)
    o_b, o_b16, lse = fox_fwd(fqn, fkn, proj, ct, nseq, seq)
    p_a, p_b, w_o, w_u, w_d = hooks.late_weights(o_a)
    y_a = matmul("mm_pa", ya_in, p_a, "nn", F32, tn=1024)
    y_b = matmul("mm_pb", o_b16, p_b, "nn", F32, tn=1024)
    gates = [row(proj, 7, dm), row(proj, 8, dm)]
    merged, = ew_fwd("merge", f_merge, gates + [row(y_a), row(y_b)], [o_row(dm, BF16)], rows)
    hres = matmul("mm_out", merged, w_o, "nn", F32, add=x2, tn=1024)
    hn, = ew_fwd("rms2", f_rms, [row(hres), par(g2)], [o_row(dm, BF16)], rows)
    up_blocks = w_u.shape[0]
    act, relu2 = matmul("mm_up", hn, w_u, "nn", F32, col_blocks=up_blocks, out_dtypes=[F32, BF16],
                        epilogue=lambda r: [r, jnp.maximum(r, 0.0) * jnp.maximum(r, 0.0)], tm=2048)
    def loss_tail(r, h_tile, t_tile):
        d = (r + h_tile) - t_tile
        e = (0.5 / dm) * (d * d)
        part = e.reshape(e.shape[0] // 8, 8, e.shape[1]).sum(axis=0)
        part = sum(part[:, t * LANES:(t + 1) * LANES] for t in range(e.shape[1] // LANES))
        g = d * (1.0 / dm)
        return [g, g, part]

    dout, dout16, loss_acc = matmul("mm_down", relu2, w_d, "nn", F32, extras=[hres, tgt2], epilogue=loss_tail,
                                    out_dtypes=[F32, BF16, F32], tile_sums=True)

    d_act = matmul("mm_d_act", dout16, w_d, "nt", BF16, extras=[act], epilogue=lambda r, a: [2.0 * jnp.maximum(a, 0.0) * r],
                   tm=2048)
    dw_d = matmul("mm_dw_down", relu2, dout16, "tn", F32, tn=1024)
    dw_u = matmul("mm_dw_up", hn, d_act, "tn", F32, col_blocks=up_blocks)
    d_hn = matmul("mm_d_hn", d_act, w_u, "nt", F32, col_blocks=up_blocks, tm=2048)
    dh, dh16, dg2 = ew_bwd("rms2_b", f_rms, [row(hres), par(g2)], [(row(d_hn),)], [row(dout)],
                           lambda g, e: [g[0] + e[0], g[0] + e[0], g[1]],
                           [((rows, dm), "row", dm, F32, None), ((rows, dm), "row", dm, BF16, None), ((1, dm), "par", dm, F32, "all")], rows)
    d_merged = matmul("mm_d_merged", dh16, w_o, "nt", F32, tn=1024)
    dw_o = matmul("mm_dw_out", merged, dh16, "tn", F32, tn=1024)
    seg16 = ((rows, dm), "row", dm, BF16, None)
    d_ga16, d_gb16, d_ya16, d_yb16 = ew_bwd("merge_b", f_merge, gates + [row(y_a), row(y_b)], [(row(d_merged),)], [],
                                            lambda g, e: list(g), [seg16] * 4, rows)
    dp_a = matmul("mm_dp_a", ya_in, d_ya16, "tn", F32, tn=1024)
    d_ya_in = matmul("mm_d_ya_in", d_ya16, p_a, "nt", F32, tn=1024)
    dp_b = matmul("mm_dp_b", o_b16, d_yb16, "tn", F32, tn=1024)
    d_ob = matmul("mm_d_ob", d_yb16, p_b, "nt", F32, tn=1024)
    token = hooks.reduce_start(dict(p_a=dp_a, p_b=dp_b, w_o=dw_o, w_u=dw_u, w_d=dw_d))
    gdn_ng_t = gdn_ng + token[0, 0]
    h32 = ((rows, wide), "rowh", LANES, F32, None)
    h16 = ((rows, wide), "rowh", LANES, BF16, None)
    gain = ((1, LANES), "par", LANES, F32, "all")
    d_oa, d_z16, d_gdn_ng = ew_bwd("gdn_post_b", f_post, [rowh(o_a), rowh(proj, 3 * NH), par(gdn_ng_t)], [(rowh(d_ya_in),)], [],
                                   lambda g, e: list(g), [h32, h16, gain], rows, NH)
    dval, dkc, dat, dqd, dkd, dgb_b = gdn_b_bwd(val, kcum, attn, qdec, kdec, gb, snaps, d_oa, nseq, seq)
    d_cq, d_ck, d_cv, d_gb, d_bb = gdn_a_bwd(conv["q"], conv["k"], conv["v"], gb, bb, t_inv, dval, dkc, dat, dqd, dkd, dgb_b, rows)
    token = hooks.reduce_exchange(d_cq)
    conv_w_t = conv_w + token[0, 0]
    d_pre, d_conv = {}, {}
    tap = ((4, wide), "parh", LANES, F32, "inner")
    for mode, off, ctg in (("q", 0, d_cq), ("k", NH, d_ck), ("v", 2 * NH, d_cv)):
        d_pre[mode], d_conv[mode] = ew_bwd(f"conv_{mode}_b", make_f_conv(mode), [rowh(proj, off), parh(conv_w_t, off)],
                                           [(rowh(ctg),)], [], lambda g, e: list(g), [h16, tap], rows, NH, seq, "hi", CONV_HEADS)
    d_fqn, d_cq_b = fox_dq(fqn, fkn, proj, ct, d_ob, lse, o_b, [token], nseq, seq)
    d_fkn, d_fv16, d_ck_b = fox_dkv(fqn, fkn, proj, cb, d_ob, lse, o_b, [token], nseq, seq)
    token = hooks.reduce_finish(d_fkn)
    qn_g_t, kn_g_t = qn_g + token[0, 0], kn_g + token[0, 0]
    d_fq16, d_qn_g = ew_bwd("fox_qn_b", f_rms, [rowh(proj, FOX_Q), par(qn_g_t)], [(rowh(d_fqn),)], [], lambda g, e: list(g),
                            [h16, gain], rows, NH)
    d_fk16, d_kn_g = ew_bwd("fox_kn_b", f_rms, [rowh(proj, FOX_K), par(kn_g_t)], [(rowh(d_fkn),)], [], lambda g, e: list(g),
                            [h16, gain], rows, NH)
    narrow = ((rows, LANES), "row", LANES, F32, None)
    d_so, d_cs = ew_bwd("bcast_b", f_bcast, [row(so), row(cs)], [(rowh(d_gb),), (rowh(d_bb),), (rowh(d_cq_b), rowh(d_ck_b))], [],
                        lambda g, e: list(g), [narrow, narrow], rows, NH)
    d_logf = cumsum_time("cumsum_b", d_cs, nseq, seq, True)
    vec = ((1, LANES), "par", LANES, F32, "all")
    d_sp16, d_p1, d_p2 = ew_bwd("small_b", f_small, [row(sp), par(p1), par(p2)], [(row(d_so), row(d_logf))], [],
                                lambda g, e: list(g), [((rows, LANES), "row", LANES, BF16, None), vec, vec], rows)
    d_proj16 = jnp.concatenate([d_pre["q"], d_pre["k"], d_pre["v"], d_z16, d_fq16, d_fk16, d_fv16, d_ga16, d_gb16], axis=1)
    dw_main = matmul("mm_dw_main", d_proj16, u, "tn", F32, tm=1536)
    dw_small = matmul("mm_dw_small", d_sp16, u, "tn", F32)
    wt_small_t = wt_small + hooks.input_grad_start(dw_main, dw_small)[0, 0].astype(BF16)
    d_u = matmul("mm_d_u_small", d_sp16, wt_small_t, "nn", F32)
    d_u = matmul("mm_d_u_first", d_proj16, wt_main, "nn", F32, add=d_u, k_part=(0, 2))
    d_u = matmul("mm_d_u_second", d_proj16, wt_main, "nn", F32, add=d_u, k_part=(1, 2), after=[hooks.input_grad_exchange(d_u)])
    dx, dg1 = ew_bwd("rms1_b", f_rms, [row(x2), par(g1)], [(row(d_u),)], [row(dh)], lambda g, e: [g[0] + e[0], g[1]],
                     [((rows, dm), "row", dm, F32, None), ((1, dm), "par", dm, F32, "all")], rows)
    d_conv_w = jnp.concatenate([d_conv["q"], d_conv["k"], d_conv["v"]], axis=1)
    return dict(loss_acc=loss_acc, dx=dx, g1=dg1, g2=dg2, gdn_ng=d_gdn_ng, qn=d_qn_g, kn=d_kn_g, p1=d_p1, p2=d_p2,
                conv=d_conv_w, w_main=dw_main, w_small=dw_small, p_a=dp_a, p_b=dp_b, w_o=dw_o, w_u=dw_u, w_d=dw_d)


_W = NH * LANES
_A0, _A1 = 4 * _W, 4 * _W + 2 * NH
_B0, _B1 = _A1 + 3 * _W, _A1 + 3 * _W + NH


def _split_w_in(full_t):
    main = jnp.concatenate([full_t[:_A0], full_t[_A1:_B0], full_t[_B1:]], axis=0)
    small = jnp.concatenate([full_t[_A0:_A1], full_t[_B0:_B1], jnp.zeros((LANES - 3 * NH, full_t.shape[1]), full_t.dtype)], axis=0)
    return main, small


def _join_w_in(main, small):
    return jnp.concatenate([main[:_A0], small[:2 * NH], main[_A0:_A0 + 3 * _W], small[2 * NH:3 * NH], main[_A0 + 3 * _W:]], axis=0)


def _lanes(v, at=0):
    return jnp.pad(v.reshape(1, -1), ((0, 0), (at, LANES - at - v.size)))


def kernel(x, norm_mix_g, w_in, gdn_conv_w, gdn_a_log, gdn_dt_bias, gdn_norm_g, fox_q_norm_g, fox_k_norm_g, fox_f_bias, w_proj_gdn, w_proj_fox, w_out, norm_mlp_g, w_up, w_down, loss_target, m_norm_mix_g, m_w_in, m_gdn_conv_w, m_gdn_a_log, m_gdn_dt_bias, m_gdn_norm_g, m_fox_q_norm_g, m_fox_k_norm_g, m_fox_f_bias, m_w_proj_gdn, m_w_proj_fox, m_w_out, m_norm_mlp_g, m_w_up, m_w_down, v_norm_mix_g, v_w_in, v_gdn_conv_w, v_gdn_a_log, v_gdn_dt_bias, v_gdn_norm_g, v_fox_q_norm_g, v_fox_k_norm_g, v_fox_f_bias, v_w_proj_gdn, v_w_proj_fox, v_w_out, v_norm_mlp_g, v_w_up, v_w_down):
    nseq, seq, dm = x.shape
    rows = nseq * seq
    xi, yi, ci = lax.axis_index("x"), lax.axis_index("y"), lax.axis_index("c")
    chip = 2 * xi + yi
    conv_cols = gdn_conv_w.shape[2]

    tr = lambda a: jnp.swapaxes(a[0], 0, 1)
    big = [tr(w_in), w_proj_gdn[0], w_proj_fox[0], w_out[0], w_up[0], w_down[0]]
    axes = [1, 0, 0, 0, 0, 0]
    big16 = [w.astype(BF16) for w in big]
    conv_slot = jnp.zeros((4, 4, conv_cols), F32).at[:, chip].set(jnp.where(ci == 0, gdn_conv_w[0], 0.0))
    conv_full = all_reduce_small("gather_conv", conv_slot.reshape(-1, LANES)).reshape(4, 4 * conv_cols)
    got_in = gather_ring(big16[0])
    wt_main, wt_small = _split_w_in(got_in.reshape(-1, dm))
    core, chip_no = ci.reshape(1).astype(jnp.int32), chip.reshape(1).astype(jnp.int32)
    gather = split_gather(big16[1:])
    token = gather.start([got_in, conv_full])

    class Hooks:
        def late_weights(self, after):
            g_pa, g_pb, g_wo, w_u, g_wd = gather.wait(after)
            return (*(g.reshape(-1, dm) for g in (g_pa, g_pb, g_wo)), w_u, g_wd.reshape(-1, dm))

        def reduce_start(self, grads):
            blocks = [grads["p_a"].reshape(4, -1, dm), grads["p_b"].reshape(4, -1, dm), grads["w_o"].reshape(4, -1, dm),
                      grads["w_u"], grads["w_d"].reshape(4, -1, dm)]
            self.swap = split_pair_swap("pair_swap_late", blocks, axes[1:])
            return self.swap.start([])

        def reduce_exchange(self, after):
            swapped = self.swap.wait(after)
            self.exchange = split_chip_exchange("chip_exchange_late", add_pair("add_pair_late", self.swap.srcs, swapped, core, axes[1:]))
            return self.exchange.start([])

        def reduce_finish(self, after):
            slots = self.exchange.wait(after)
            self.send = split_pair_send(add_chips("add_chips_late", slots, self.exchange.srcs, chip_no, axes[1:]))
            return self.send.start([])

        def input_grad_start(self, dw_main, dw_small):
            self.in_swap = split_pair_swap("pair_swap_in", [_join_w_in(dw_main, dw_small).reshape(4, -1, dm)], axes[:1])
            return self.in_swap.start([])

        def input_grad_exchange(self, after):
            swapped = self.in_swap.wait(after)
            self.in_exchange = split_chip_exchange("chip_exchange_in", add_pair("add_pair_in", self.in_swap.srcs, swapped, core, axes[:1]))
            return self.in_exchange.start([])

    hooks = Hooks()
    p1 = _lanes(gdn_dt_bias[0]) + _lanes(fox_f_bias[0], 2 * NH)
    p2 = _lanes(gdn_a_log[0])

    g = local_step(x.reshape(rows, dm), loss_target.reshape(rows, dm), norm_mix_g + token[0, 0], norm_mlp_g, gdn_norm_g,
                   fox_q_norm_g, fox_k_norm_g, p1, p2, conv_full, wt_main, wt_small, hooks, nseq, seq)

    others = hooks.send.wait(g["dx"])
    big_m = [tr(m_w_in), m_w_proj_gdn[0], m_w_proj_fox[0], m_w_out[0], m_w_up[0], m_w_down[0]]
    big_v = [tr(v_w_in), v_w_proj_gdn[0], v_w_proj_fox[0], v_w_out[0], v_w_up[0], v_w_down[0]]
    names = ["w_in", "w_proj_gdn", "w_proj_fox", "w_out", "w_up", "w_down"]
    big_res, big_grad = {}, {}
    for i in range(1, len(names)):
        big_grad[names[i]], *big_res[names[i]] = adamw_halves(f"adamw_{names[i]}", big[i], hooks.send.srcs[i - 1], others[i - 1],
                                                              big_m[i], big_v[i], core, axes[i])
    slots = hooks.in_exchange.wait(big_res[names[-1]][0])
    mine = add_chips("add_chips_in", slots, hooks.in_exchange.srcs, chip_no, axes[:1])
    res = adamw_halves("adamw_w_in", big[0], mine[0], pair_send(mine)[0], big_m[0], big_v[0], core, axes[0])
    big_grad["w_in"], *big_res["w_in"] = [jnp.swapaxes(r, 0, 1) for r in res]

    small_parts = [g["loss_acc"], g["g1"].reshape(8, LANES), g["g2"].reshape(8, LANES), g["gdn_ng"], g["qn"], g["kn"], g["p1"], g["p2"],
                   g["conv"].reshape(-1, LANES)]
    tiled = [jnp.pad(p, ((0, -p.shape[0] % 8), (0, 0))) for p in small_parts]
    red = all_reduce_small("reduce_small", jnp.concatenate(tiled, axis=0), slots)
    pos, red_parts = 0, []
    for p, t in zip(small_parts, tiled):
        red_parts.append(red[pos:pos + p.shape[0]])
        pos += t.shape[0]
    r_loss, r_g1, r_g2, r_gdn_ng, r_qn, r_kn, r_p1, r_p2, r_conv = red_parts
    loss = jnp.sum(r_loss)
    g_conv = lax.dynamic_slice_in_dim(r_conv.reshape(4, 4, conv_cols), chip, 1, axis=1).reshape(4, conv_cols)
    small_grads = [r_g1.reshape(1, dm), r_p2[:, :NH], r_p1[:, :NH], r_gdn_ng, r_qn, r_kn, r_p1[:, 2 * NH:3 * NH], r_g2.reshape(1, dm)]
    small_w = [norm_mix_g, gdn_a_log, gdn_dt_bias, gdn_norm_g, fox_q_norm_g, fox_k_norm_g, fox_f_bias, norm_mlp_g]
    small_m = [m_norm_mix_g, m_gdn_a_log, m_gdn_dt_bias, m_gdn_norm_g, m_fox_q_norm_g, m_fox_k_norm_g, m_fox_f_bias, m_norm_mlp_g]
    small_v = [v_norm_mix_g, v_gdn_a_log, v_gdn_dt_bias, v_gdn_norm_g, v_fox_q_norm_g, v_fox_k_norm_g, v_fox_f_bias, v_norm_mlp_g]

    def pack(parts):
        flat = jnp.concatenate([jnp.pad(p.reshape(-1), (0, -p.size % LANES)) for p in parts])
        return jnp.pad(flat, (0, -flat.size % (8 * LANES))).reshape(-1, LANES)

    packed = adamw("adamw_small", pack(small_w + [gdn_conv_w[0]]), pack(small_grads + [g_conv]),
                   pack(small_m + [m_gdn_conv_w[0]]), pack(small_v + [v_gdn_conv_w[0]]))

    def unpack(flat2d):
        flat, pos, res = flat2d.reshape(-1), 0, []
        for p in small_w + [gdn_conv_w[0]]:
            res.append(flat[pos:pos + p.size].reshape(p.shape))
            pos += p.size + (-p.size % LANES)
        return res

    s_delta, s_m, s_v = (unpack(a) for a in packed)

    order = ["norm_mix_g", "w_in", "gdn_conv_w", "gdn_a_log", "gdn_dt_bias", "gdn_norm_g", "fox_q_norm_g", "fox_k_norm_g",
             "fox_f_bias", "w_proj_gdn", "w_proj_fox", "w_out", "norm_mlp_g", "w_up", "w_down"]
    small_names = ["norm_mix_g", "gdn_a_log", "gdn_dt_bias", "gdn_norm_g", "fox_q_norm_g", "fox_k_norm_g", "fox_f_bias", "norm_mlp_g",
                   "gdn_conv_w"]
    small_idx = {nm: i for i, nm in enumerate(small_names)}
    shapes = dict(zip(order, (a.shape for a in (norm_mix_g, w_in, gdn_conv_w, gdn_a_log, gdn_dt_bias, gdn_norm_g, fox_q_norm_g,
                                                 fox_k_norm_g, fox_f_bias, w_proj_gdn, w_proj_fox, w_out, norm_mlp_g, w_up, w_down))))
    grads_out, delta_out, m_out, v_out = [], [], [], []
    for nm in order:
        if nm in big_res:
            d, mm, vv = big_res[nm]
            gr = big_grad[nm]
        else:
            i = small_idx[nm]
            gr = (small_grads + [g_conv])[i]
            d, mm, vv = s_delta[i], s_m[i], s_v[i]
        for lst, val in ((grads_out, gr), (delta_out, d), (m_out, mm), (v_out, vv)):
            lst.append(val.reshape(shapes[nm]))
    return (loss, g["dx"].reshape(x.shape), *grads_out, *delta_out, *m_out, *v_out)
```
